```python
import math
import jax, jax.numpy as jnp
from jax import lax
import numpy as np

D_MODEL = 1024
BATCH = 8
SEQ = 2048
DEPTH = 4

N_META = 16
MIX_W = D_MODEL
ATT_W = MIX_W // 2
REC_W = MIX_W - ATT_W
ATT_HEAD_DIM = 64
N_ATT_HEADS = ATT_W // ATT_HEAD_DIM
N_REC_BLOCKS = 8
REC_BLOCK = REC_W // N_REC_BLOCKS
CONV_WIDTH = 4
RG_C = 8.0
D_FF = 4 * D_MODEL
Q_BLOCK = 128
NORM_EPS = 1e-6
D_IN = 3 * ATT_W + N_ATT_HEADS + 2 * REC_W

kernel_name = "hymba_fox_rglru_hybrid"


def rmsnorm(x, g):
    xf = x.astype(jnp.float32)
    y = xf * lax.rsqrt(jnp.mean(xf * xf, axis=-1, keepdims=True) + NORM_EPS)
    return (y * g.astype(jnp.float32)).astype(x.dtype)


def forgetting_attention(q, k, v, log_f):
    T = q.shape[1]
    scale = 1.0 / math.sqrt(q.shape[-1])
    c = jnp.cumsum(log_f, axis=1)
    c_bh = jnp.transpose(c, (0, 2, 1))
    starts = [0] + list(range(N_META, T, Q_BLOCK))
    ends = starts[1:] + [T]
    pos = jnp.arange(T)
    outs = []
    for qs, qe in zip(starts, ends):
        qb = q[:, qs:qe]
        kb = k[:, :qe]
        vb = v[:, :qe]
        s = jnp.einsum('bqhd,bkhd->bhqk', qb, kb).astype(jnp.float32) * scale
        bias = c_bh[:, :, qs:qe, None] - c_bh[:, :, None, :qe]
        mask = pos[qs:qe, None] >= pos[None, :qe]
        s = jnp.where(mask[None, None], s + bias, -jnp.inf)
        p = jax.nn.softmax(s, axis=-1).astype(vb.dtype)
        outs.append(jnp.einsum('bhqk,bkhd->bqhd', p, vb))
    return jnp.concatenate(outs, axis=1)


def causal_depthwise_conv(x, w, b):
    C = x.shape[-1]
    y = lax.conv_general_dilated(
        x, w[:, None, :].astype(x.dtype), window_strides=(1,),
        padding=[(CONV_WIDTH - 1, 0)], dimension_numbers=('NWC', 'WIO', 'NWC'),
        feature_group_count=C)
    return y + b.astype(x.dtype)


def block_diag_linear(x, w, b):
    B, T, C = x.shape
    xb = x.reshape(B, T, N_REC_BLOCKS, REC_BLOCK)
    y = jnp.einsum('btnd,nde->btne', xb, w.astype(x.dtype)).reshape(B, T, C)
    return y + b.astype(x.dtype)


def rg_lru(x, w_ga, b_ga, w_gx, b_gx, lru_L):
    r = jax.nn.sigmoid(block_diag_linear(x, w_ga, b_ga).astype(jnp.float32))
    i = jax.nn.sigmoid(block_diag_linear(x, w_gx, b_gx).astype(jnp.float32))
    log_a = RG_C * r * jax.nn.log_sigmoid(lru_L.astype(jnp.float32))
    a = jnp.exp(log_a)
    mult = jnp.sqrt(-jnp.expm1(2.0 * log_a))
    u = mult * i * x.astype(jnp.float32)

    def combine(e1, e2):
        a1, b1 = e1
        a2, b2 = e2
        return a1 * a2, a2 * b1 + b2

    _, h = lax.associative_scan(combine, (a, u), axis=1)
    return h.astype(x.dtype)


def _fwd_setup_inputs(seed: int = 0) -> dict:
    key = jax.random.key(seed)
    ks = jax.random.split(key, 24)
    f32 = jnp.float32
    L = DEPTH
    nrm = lambda k, shape, s: (jax.random.normal(k, shape, f32) * s)
    x = jax.random.normal(ks[0], (BATCH, SEQ, D_MODEL), f32)
    meta = nrm(ks[1], (N_META, D_MODEL), 1.0)
    attn_norm_g = 1.0 + nrm(ks[2], (L, D_MODEL), 0.02)
    w_in = nrm(ks[3], (L, D_MODEL, D_IN), D_MODEL ** -0.5)
    b_f = jax.random.uniform(ks[4], (L, N_ATT_HEADS), f32, 1.0, 5.0)
    conv_w = nrm(ks[5], (L, CONV_WIDTH, REC_W), CONV_WIDTH ** -0.5)
    conv_b = nrm(ks[6], (L, REC_W), 0.02)
    w_gate_a = nrm(ks[7], (L, N_REC_BLOCKS, REC_BLOCK, REC_BLOCK), REC_BLOCK ** -0.5)
    b_gate_a = nrm(ks[8], (L, REC_W), 0.02)
    w_gate_x = nrm(ks[9], (L, N_REC_BLOCKS, REC_BLOCK, REC_BLOCK), REC_BLOCK ** -0.5)
    b_gate_x = nrm(ks[10], (L, REC_W), 0.02)
    u = jax.random.uniform(ks[11], (L, REC_W), f32, 0.9, 0.999)
    a_base = u ** (1.0 / RG_C)
    lru_L = jnp.log(a_base) - jnp.log1p(-a_base)
    attn_out_g = 1.0 + nrm(ks[12], (L, ATT_W), 0.02)
    rec_out_g = 1.0 + nrm(ks[13], (L, REC_W), 0.02)
    w_out = nrm(ks[14], (L, MIX_W, D_MODEL), (MIX_W * 2 * DEPTH) ** -0.5)
    mlp_norm_g = 1.0 + nrm(ks[15], (L, D_MODEL), 0.02)
    w_up = nrm(ks[16], (L, D_MODEL, D_FF), D_MODEL ** -0.5)
    w_down = nrm(ks[17], (L, D_FF, D_MODEL), (D_FF * 2 * DEPTH) ** -0.5)
    final_g = 1.0 + nrm(ks[18], (D_MODEL,), 0.02)
    return {"x": x, "meta": meta, "attn_norm_g": attn_norm_g, "w_in": w_in,
            "b_f": b_f, "conv_w": conv_w, "conv_b": conv_b,
            "w_gate_a": w_gate_a, "b_gate_a": b_gate_a,
            "w_gate_x": w_gate_x, "b_gate_x": b_gate_x, "lru_L": lru_L,
            "attn_out_g": attn_out_g, "rec_out_g": rec_out_g, "w_out": w_out,
            "mlp_norm_g": mlp_norm_g, "w_up": w_up, "w_down": w_down,
            "final_g": final_g}


def _fwd_reference(x, meta, attn_norm_g, w_in, b_f, conv_w, conv_b, w_gate_a, b_gate_a,
              w_gate_x, b_gate_x, lru_L, attn_out_g, rec_out_g, w_out,
              mlp_norm_g, w_up, w_down, final_g):
    B, S, D = x.shape
    meta_b = jnp.broadcast_to(meta.astype(x.dtype)[None], (B, N_META, D))
    h = jnp.concatenate([meta_b, x], axis=1)
    T = h.shape[1]
    splits = np.cumsum([ATT_W, ATT_W, ATT_W, N_ATT_HEADS, REC_W]).tolist()
    for l in range(DEPTH):
        z = rmsnorm(h, attn_norm_g[l])
        proj = z @ w_in[l]
        q, k, v, f_logit, xr, yr = jnp.split(proj, splits, axis=-1)
        log_f = jax.nn.log_sigmoid(f_logit.astype(jnp.float32) + b_f[l].astype(jnp.float32))
        qh = q.reshape(B, T, N_ATT_HEADS, ATT_HEAD_DIM)
        kh = k.reshape(B, T, N_ATT_HEADS, ATT_HEAD_DIM)
        vh = v.reshape(B, T, N_ATT_HEADS, ATT_HEAD_DIM)
        attn = forgetting_attention(qh, kh, vh, log_f).reshape(B, T, ATT_W)
        xc = causal_depthwise_conv(xr, conv_w[l], conv_b[l])
        hr = rg_lru(xc, w_gate_a[l], b_gate_a[l], w_gate_x[l], b_gate_x[l], lru_L[l])
        rec = hr * jax.nn.gelu(yr)
        mix = jnp.concatenate([rmsnorm(attn, attn_out_g[l]), rmsnorm(rec, rec_out_g[l])], axis=-1)
        h = h + mix @ w_out[l]
        z = rmsnorm(h, mlp_norm_g[l])
        u = jax.nn.relu(z @ w_up[l])
        h = h + (u * u) @ w_down[l]
    h = rmsnorm(h, final_g)
    return h[:, N_META:]


import jax as _jax
import jax.numpy as _jnp

TWIN_FORMAT = 'train_step'
FWD_PARAMS = ['x', 'meta', 'attn_norm_g', 'w_in', 'b_f', 'conv_w', 'conv_b', 'w_gate_a', 'b_gate_a', 'w_gate_x', 'b_gate_x', 'lru_L', 'attn_out_g', 'rec_out_g', 'w_out', 'mlp_norm_g', 'w_up', 'w_down', 'final_g']
TWIN_WEIGHTS = ['meta', 'attn_norm_g', 'w_in', 'b_f', 'conv_w', 'conv_b', 'w_gate_a', 'b_gate_a', 'w_gate_x', 'b_gate_x', 'lru_L', 'attn_out_g', 'rec_out_g', 'w_out', 'mlp_norm_g', 'w_up', 'w_down', 'final_g']
TWIN_DIFF_INPUT = 'x'
TWIN_INPUTS = ['x', 'meta', 'attn_norm_g', 'w_in', 'b_f', 'conv_w', 'conv_b', 'w_gate_a', 'b_gate_a', 'w_gate_x', 'b_gate_x', 'lru_L', 'attn_out_g', 'rec_out_g', 'w_out', 'mlp_norm_g', 'w_up', 'w_down', 'final_g', 'loss_target', 'm_meta', 'm_attn_norm_g', 'm_w_in', 'm_b_f', 'm_conv_w', 'm_conv_b', 'm_w_gate_a', 'm_b_gate_a', 'm_w_gate_x', 'm_b_gate_x', 'm_lru_L', 'm_attn_out_g', 'm_rec_out_g', 'm_w_out', 'm_mlp_norm_g', 'm_w_up', 'm_w_down', 'm_final_g', 'v_meta', 'v_attn_norm_g', 'v_w_in', 'v_b_f', 'v_conv_w', 'v_conv_b', 'v_w_gate_a', 'v_b_gate_a', 'v_w_gate_x', 'v_b_gate_x', 'v_lru_L', 'v_attn_out_g', 'v_rec_out_g', 'v_w_out', 'v_mlp_norm_g', 'v_w_up', 'v_w_down', 'v_final_g']
TWIN_OUTPUTS = ['loss', 'grad_x', 'grad_meta', 'grad_attn_norm_g', 'grad_w_in', 'grad_b_f', 'grad_conv_w', 'grad_conv_b', 'grad_w_gate_a', 'grad_b_gate_a', 'grad_w_gate_x', 'grad_b_gate_x', 'grad_lru_L', 'grad_attn_out_g', 'grad_rec_out_g', 'grad_w_out', 'grad_mlp_norm_g', 'grad_w_up', 'grad_w_down', 'grad_final_g', 'delta_meta', 'delta_attn_norm_g', 'delta_w_in', 'delta_b_f', 'delta_conv_w', 'delta_conv_b', 'delta_w_gate_a', 'delta_b_gate_a', 'delta_w_gate_x', 'delta_b_gate_x', 'delta_lru_L', 'delta_attn_out_g', 'delta_rec_out_g', 'delta_w_out', 'delta_mlp_norm_g', 'delta_w_up', 'delta_w_down', 'delta_final_g', 'new_m_meta', 'new_m_attn_norm_g', 'new_m_w_in', 'new_m_b_f', 'new_m_conv_w', 'new_m_conv_b', 'new_m_w_gate_a', 'new_m_b_gate_a', 'new_m_w_gate_x', 'new_m_b_gate_x', 'new_m_lru_L', 'new_m_attn_out_g', 'new_m_rec_out_g', 'new_m_w_out', 'new_m_mlp_norm_g', 'new_m_w_up', 'new_m_w_down', 'new_m_final_g', 'new_v_meta', 'new_v_attn_norm_g', 'new_v_w_in', 'new_v_b_f', 'new_v_conv_w', 'new_v_conv_b', 'new_v_w_gate_a', 'new_v_b_gate_a', 'new_v_w_gate_x', 'new_v_b_gate_x', 'new_v_lru_L', 'new_v_attn_out_g', 'new_v_rec_out_g', 'new_v_w_out', 'new_v_mlp_norm_g', 'new_v_w_up', 'new_v_w_down', 'new_v_final_g']
TWIN_LEAF_KINDS = {'loss': 'loss', 'grad_x': 'grad_x', 'grad_meta': 'grad_w', 'grad_attn_norm_g': 'grad_w', 'grad_w_in': 'grad_w', 'grad_b_f': 'grad_w', 'grad_conv_w': 'grad_w', 'grad_conv_b': 'grad_w', 'grad_w_gate_a': 'grad_w', 'grad_b_gate_a': 'grad_w', 'grad_w_gate_x': 'grad_w', 'grad_b_gate_x': 'grad_w', 'grad_lru_L': 'grad_w', 'grad_attn_out_g': 'grad_w', 'grad_rec_out_g': 'grad_w', 'grad_w_out': 'grad_w', 'grad_mlp_norm_g': 'grad_w', 'grad_w_up': 'grad_w', 'grad_w_down': 'grad_w', 'grad_final_g': 'grad_w', 'delta_meta': 'delta_w', 'delta_attn_norm_g': 'delta_w', 'delta_w_in': 'delta_w', 'delta_b_f': 'delta_w', 'delta_conv_w': 'delta_w', 'delta_conv_b': 'delta_w', 'delta_w_gate_a': 'delta_w', 'delta_b_gate_a': 'delta_w', 'delta_w_gate_x': 'delta_w', 'delta_b_gate_x': 'delta_w', 'delta_lru_L': 'delta_w', 'delta_attn_out_g': 'delta_w', 'delta_rec_out_g': 'delta_w', 'delta_w_out': 'delta_w', 'delta_mlp_norm_g': 'delta_w', 'delta_w_up': 'delta_w', 'delta_w_down': 'delta_w', 'delta_final_g': 'delta_w', 'new_m_meta': 'new_m', 'new_m_attn_norm_g': 'new_m', 'new_m_w_in': 'new_m', 'new_m_b_f': 'new_m', 'new_m_conv_w': 'new_m', 'new_m_conv_b': 'new_m', 'new_m_w_gate_a': 'new_m', 'new_m_b_gate_a': 'new_m', 'new_m_w_gate_x': 'new_m', 'new_m_b_gate_x': 'new_m', 'new_m_lru_L': 'new_m', 'new_m_attn_out_g': 'new_m', 'new_m_rec_out_g': 'new_m', 'new_m_w_out': 'new_m', 'new_m_mlp_norm_g': 'new_m', 'new_m_w_up': 'new_m', 'new_m_w_down': 'new_m', 'new_m_final_g': 'new_m', 'new_v_meta': 'new_v', 'new_v_attn_norm_g': 'new_v', 'new_v_w_in': 'new_v', 'new_v_b_f': 'new_v', 'new_v_conv_w': 'new_v', 'new_v_conv_b': 'new_v', 'new_v_w_gate_a': 'new_v', 'new_v_b_gate_a': 'new_v', 'new_v_w_gate_x': 'new_v', 'new_v_b_gate_x': 'new_v', 'new_v_lru_L': 'new_v', 'new_v_attn_out_g': 'new_v', 'new_v_rec_out_g': 'new_v', 'new_v_w_out': 'new_v', 'new_v_mlp_norm_g': 'new_v', 'new_v_w_up': 'new_v', 'new_v_w_down': 'new_v', 'new_v_final_g': 'new_v'}


def _forward(args):
    return _fwd_reference(*[args[k] for k in FWD_PARAMS])


def _output_shape():
    out = _jax.eval_shape(lambda: _forward(_fwd_setup_inputs(0)))
    return out.shape, out.dtype

N_MICROBATCH = 1
ADAM_LR = 0.001
ADAM_B1 = 0.9
ADAM_B2 = 0.999
ADAM_EPS = 1e-08
ADAM_WD = 0.01
ADAM_STEP = 10
PER_EXAMPLE_BATCH_AXIS = {'x': 0, 'loss_target': 0}
SHARED_INPUTS = []
_WEIGHT_DTYPES = {'meta': _jnp.float32, 'attn_norm_g': _jnp.float32, 'w_in': _jnp.float32, 'b_f': _jnp.float32, 'conv_w': _jnp.float32, 'conv_b': _jnp.float32, 'w_gate_a': _jnp.float32, 'b_gate_a': _jnp.float32, 'w_gate_x': _jnp.float32, 'b_gate_x': _jnp.float32, 'lru_L': _jnp.float32, 'attn_out_g': _jnp.float32, 'rec_out_g': _jnp.float32, 'w_out': _jnp.float32, 'mlp_norm_g': _jnp.float32, 'w_up': _jnp.float32, 'w_down': _jnp.float32, 'final_g': _jnp.float32}
MOMENT_SCALE = {'meta': 4.979077e-03, 'attn_norm_g': 5.970126e-02, 'w_in': 3.648541e-02, 'b_f': 1.478234e-01, 'conv_w': 4.519801e-02, 'conv_b': 3.443253e-01, 'w_gate_a': 1.358518e-02, 'b_gate_a': 1.027041e-02, 'w_gate_x': 2.472739e-02, 'b_gate_x': 1.797018e-02, 'lru_L': 2.122631e-02, 'attn_out_g': 4.315280e-02, 'rec_out_g': 5.147395e-02, 'w_out': 1.257493e-01, 'mlp_norm_g': 5.147715e-02, 'w_up': 2.543263e-02, 'w_down': 1.521914e-01, 'final_g': 1.615832e+01}


def _to_microbatches(a, axis):
    t = _jnp.moveaxis(a, axis, 0)
    t = t.reshape((N_MICROBATCH, t.shape[0] // N_MICROBATCH) + t.shape[1:])
    return _jnp.moveaxis(t, 1, axis + 1)


def setup_inputs(seed: int = 0) -> dict:
    inp = _fwd_setup_inputs(seed)
    key = _jax.random.fold_in(_jax.random.key(seed), 7919)
    shape, _ = _output_shape()
    out = dict(inp)
    out["loss_target"] = _jax.random.normal(_jax.random.fold_in(key, 0), shape, _jnp.float32)
    for i, name in enumerate(TWIN_WEIGHTS):
        w = inp[name].astype(_jnp.float32)
        if MOMENT_SCALE is None:
            s = _jnp.sqrt(_jnp.mean(_jnp.square(w)) + 1e-30)
        else:
            s = MOMENT_SCALE[name]
        km, kv = _jax.random.split(_jax.random.fold_in(key, i + 1))
        out[name] = w
        out["m_" + name] = s * _jax.random.normal(km, w.shape, _jnp.float32)
        out["v_" + name] = (s * s) * _jax.random.uniform(kv, w.shape, _jnp.float32, 0.5, 1.5)
    if N_MICROBATCH > 1:
        for name, axis in PER_EXAMPLE_BATCH_AXIS.items():
            out[name] = _to_microbatches(out[name], axis)
    return {'x': out['x'], 'meta': out['meta'], 'attn_norm_g': out['attn_norm_g'], 'w_in': out['w_in'], 'b_f': out['b_f'], 'conv_w': out['conv_w'], 'conv_b': out['conv_b'], 'w_gate_a': out['w_gate_a'], 'b_gate_a': out['b_gate_a'], 'w_gate_x': out['w_gate_x'], 'b_gate_x': out['b_gate_x'], 'lru_L': out['lru_L'], 'attn_out_g': out['attn_out_g'], 'rec_out_g': out['rec_out_g'], 'w_out': out['w_out'], 'mlp_norm_g': out['mlp_norm_g'], 'w_up': out['w_up'], 'w_down': out['w_down'], 'final_g': out['final_g'], 'loss_target': out['loss_target'], 'm_meta': out['m_meta'], 'm_attn_norm_g': out['m_attn_norm_g'], 'm_w_in': out['m_w_in'], 'm_b_f': out['m_b_f'], 'm_conv_w': out['m_conv_w'], 'm_conv_b': out['m_conv_b'], 'm_w_gate_a': out['m_w_gate_a'], 'm_b_gate_a': out['m_b_gate_a'], 'm_w_gate_x': out['m_w_gate_x'], 'm_b_gate_x': out['m_b_gate_x'], 'm_lru_L': out['m_lru_L'], 'm_attn_out_g': out['m_attn_out_g'], 'm_rec_out_g': out['m_rec_out_g'], 'm_w_out': out['m_w_out'], 'm_mlp_norm_g': out['m_mlp_norm_g'], 'm_w_up': out['m_w_up'], 'm_w_down': out['m_w_down'], 'm_final_g': out['m_final_g'], 'v_meta': out['v_meta'], 'v_attn_norm_g': out['v_attn_norm_g'], 'v_w_in': out['v_w_in'], 'v_b_f': out['v_b_f'], 'v_conv_w': out['v_conv_w'], 'v_conv_b': out['v_conv_b'], 'v_w_gate_a': out['v_w_gate_a'], 'v_b_gate_a': out['v_b_gate_a'], 'v_w_gate_x': out['v_w_gate_x'], 'v_b_gate_x': out['v_b_gate_x'], 'v_lru_L': out['v_lru_L'], 'v_attn_out_g': out['v_attn_out_g'], 'v_rec_out_g': out['v_rec_out_g'], 'v_w_out': out['v_w_out'], 'v_mlp_norm_g': out['v_mlp_norm_g'], 'v_w_up': out['v_w_up'], 'v_w_down': out['v_w_down'], 'v_final_g': out['v_final_g']}


def _loss(weights, diff, rest, loss_target):
    with _jax.named_scope("forward"):
        args = {**rest, TWIN_DIFF_INPUT: diff, **{k: w.astype(_WEIGHT_DTYPES[k]) for k, w in weights.items()}}
        y = _forward(args)
    with _jax.named_scope("loss_head"):
        err = _jnp.square(y.astype(_jnp.float32) - loss_target)
        return 0.5 * _jnp.sum(_jnp.mean(err, axis=-1)) if err.ndim else 0.5 * err


def _adamw(w, g, m, v):
    m = ADAM_B1 * m + (1.0 - ADAM_B1) * g
    v = ADAM_B2 * v + (1.0 - ADAM_B2) * _jnp.square(g)
    m_hat = m / (1.0 - ADAM_B1 ** ADAM_STEP)
    v_hat = v / (1.0 - ADAM_B2 ** ADAM_STEP)
    delta = -ADAM_LR * (m_hat / (_jnp.sqrt(v_hat) + ADAM_EPS) + ADAM_WD * w)
    return delta, m, v


def reference(x, meta, attn_norm_g, w_in, b_f, conv_w, conv_b, w_gate_a, b_gate_a, w_gate_x, b_gate_x, lru_L, attn_out_g, rec_out_g, w_out, mlp_norm_g, w_up, w_down, final_g, loss_target, m_meta, m_attn_norm_g, m_w_in, m_b_f, m_conv_w, m_conv_b, m_w_gate_a, m_b_gate_a, m_w_gate_x, m_b_gate_x, m_lru_L, m_attn_out_g, m_rec_out_g, m_w_out, m_mlp_norm_g, m_w_up, m_w_down, m_final_g, v_meta, v_attn_norm_g, v_w_in, v_b_f, v_conv_w, v_conv_b, v_w_gate_a, v_b_gate_a, v_w_gate_x, v_b_gate_x, v_lru_L, v_attn_out_g, v_rec_out_g, v_w_out, v_mlp_norm_g, v_w_up, v_w_down, v_final_g):
    given = dict(x=x, meta=meta, attn_norm_g=attn_norm_g, w_in=w_in, b_f=b_f, conv_w=conv_w, conv_b=conv_b, w_gate_a=w_gate_a, b_gate_a=b_gate_a, w_gate_x=w_gate_x, b_gate_x=b_gate_x, lru_L=lru_L, attn_out_g=attn_out_g, rec_out_g=rec_out_g, w_out=w_out, mlp_norm_g=mlp_norm_g, w_up=w_up, w_down=w_down, final_g=final_g, loss_target=loss_target, m_meta=m_meta, m_attn_norm_g=m_attn_norm_g, m_w_in=m_w_in, m_b_f=m_b_f, m_conv_w=m_conv_w, m_conv_b=m_conv_b, m_w_gate_a=m_w_gate_a, m_b_gate_a=m_b_gate_a, m_w_gate_x=m_w_gate_x, m_b_gate_x=m_b_gate_x, m_lru_L=m_lru_L, m_attn_out_g=m_attn_out_g, m_rec_out_g=m_rec_out_g, m_w_out=m_w_out, m_mlp_norm_g=m_mlp_norm_g, m_w_up=m_w_up, m_w_down=m_w_down, m_final_g=m_final_g, v_meta=v_meta, v_attn_norm_g=v_attn_norm_g, v_w_in=v_w_in, v_b_f=v_b_f, v_conv_w=v_conv_w, v_conv_b=v_conv_b, v_w_gate_a=v_w_gate_a, v_b_gate_a=v_b_gate_a, v_w_gate_x=v_w_gate_x, v_b_gate_x=v_b_gate_x, v_lru_L=v_lru_L, v_attn_out_g=v_attn_out_g, v_rec_out_g=v_rec_out_g, v_w_out=v_w_out, v_mlp_norm_g=v_mlp_norm_g, v_w_up=v_w_up, v_w_down=v_w_down, v_final_g=v_final_g)
    weights = {n: given[n] for n in TWIN_WEIGHTS}
    shared = {n: given[n] for n in SHARED_INPUTS}
    per_example = {n: given[n] for n in ['x']}
    grad_fn = _jax.value_and_grad(_loss, argnums=(0, 1))

    def one_microbatch(ex, loss_target):
        ex = dict(ex)
        diff = ex.pop(TWIN_DIFF_INPUT)
        return grad_fn(weights, diff, {**shared, **ex}, loss_target)

    if N_MICROBATCH == 1:
        loss, (grad_w, grad_x) = one_microbatch(per_example, given["loss_target"])
    else:
        def body(carry, xs):
            loss_sum, grad_sum = carry
            l_k, (gw_k, gx_k) = one_microbatch(xs[0], xs[1])
            with _jax.named_scope("update"):
                return (loss_sum + l_k, _jax.tree.map(_jnp.add, grad_sum, gw_k)), gx_k

        init = (_jnp.zeros((), _jnp.float32), _jax.tree.map(_jnp.zeros_like, weights))
        (loss, grad_w), grad_x = _jax.lax.scan(body, init, (per_example, given["loss_target"]))
    with _jax.named_scope("update"):
        delta_w, new_m, new_v = {}, {}, {}
        for n in TWIN_WEIGHTS:
            delta_w[n], new_m[n], new_v[n] = _adamw(weights[n], grad_w[n], given["m_" + n], given["v_" + n])
    return (loss, grad_x, *[grad_w[n] for n in TWIN_WEIGHTS], *[delta_w[n] for n in TWIN_WEIGHTS],
            *[new_m[n] for n in TWIN_WEIGHTS], *[new_v[n] for n in TWIN_WEIGHTS])
```

```python
import functools
import math

import jax
import jax.numpy as jnp
from jax import lax
from jax.experimental import pallas as pl
from jax.experimental.pallas import tpu as pltpu

F32 = jnp.float32
BF16 = jnp.bfloat16

N_META = 16
HEAD_DIM = 64
N_REC_BLOCKS = 8
CONV_WIDTH = 4
RG_C = 8.0
NORM_EPS = 1e-6
ADAM_LR = 0.001
ADAM_B1 = 0.9
ADAM_B2 = 0.999
ADAM_EPS = 1e-08
ADAM_WD = 0.01
ADAM_STEP = 10

LANES = 128
SUBLANES = 8
SEQ_TILE = 128
VMEM_CAP = 60 * 2**20
VMEM_SLACK = 6 * 2**20
NEG_BIG = -1e30
N_CHIPS = 4
N_DEV = 8
MESH = pl.DeviceIdType.MESH


def _nbytes(shape, dtype):
    return math.prod(shape) * jnp.dtype(dtype).itemsize


def _call(body, args, *, name, out_shape, grid=(), in_specs=None, out_specs=None,
          scratch_shapes=(), semantics=None, vmem_bytes=None, **kw):
    cp = {}
    if semantics is not None:
        cp["dimension_semantics"] = semantics
    if vmem_bytes is not None:
        cp["vmem_limit_bytes"] = int(min(VMEM_CAP, vmem_bytes + VMEM_SLACK))
    fn = pl.pallas_call(
        body, name=name, out_shape=out_shape, grid=grid,
        in_specs=in_specs, out_specs=out_specs, scratch_shapes=scratch_shapes,
        compiler_params=pltpu.CompilerParams(**cp), **kw)
    return fn(*args)


def _divisor_tile(n, unit, target):
    best = None
    for t in range(unit, min(n, target) + 1, unit):
        if n % t == 0:
            best = t
    return n if best is None else best


def _sigmoid(x):
    return 1.0 / (1.0 + jnp.exp(-x))


def _log1p_unit(e):
    series = e * (1.0 - e * (0.5 - e * (1.0 / 3.0)))
    return jnp.where(e < 1e-2, series, jnp.log(1.0 + e))


def _log_sigmoid(x):
    return jnp.minimum(x, 0.0) - _log1p_unit(jnp.exp(-jnp.abs(x)))


def _expm1_nonpos(x):
    small = x * (1.0 + x * (1.0 / 2 + x * (1.0 / 6 + x * (1.0 / 24 + x * (1.0 / 120 + x * (1.0 / 720))))))
    return jnp.where(x > -0.25, small, jnp.exp(x) - 1.0)


_GELU_K = math.sqrt(2.0 / math.pi)
_GELU_C = 0.044715


def _gelu_and_grad(y):
    th = jnp.tanh(_GELU_K * (y + _GELU_C * y * y * y))
    g = 0.5 * y * (1.0 + th)
    dg = 0.5 * (1.0 + th) + 0.5 * y * (1.0 - th * th) * _GELU_K * (1.0 + 3.0 * _GELU_C * y * y)
    return g, dg


def _rstd(x):
    return lax.rsqrt(jnp.mean(x * x, axis=-1, keepdims=True) + NORM_EPS)


def _rms_bwd(dz, x, g):
    rs = _rstd(x)
    xh = x * rs
    dgp = jnp.sum(dz * xh, axis=0, keepdims=True)
    dxh = dz * g
    dx = rs * (dxh - xh * jnp.mean(dxh * xh, axis=-1, keepdims=True))
    return dx, dgp


def _dot(a, b):
    return jnp.dot(a, b, preferred_element_type=F32)


def _dot_nt(a, b):
    return lax.dot_general(a, b, (((1,), (1,)), ((), ())), preferred_element_type=F32)


def _dot_tn(a, b):
    return lax.dot_general(a, b, (((0,), (0,)), ((), ())), preferred_element_type=F32)


def _full(shape):
    nd = len(shape)
    return pl.BlockSpec(shape, lambda *_: (0,) * nd)


def _rms_fwd(h, g):
    tp, d = h.shape
    tm = _divisor_tile(tp, 16, 544)

    def body(h_ref, g_ref, z_ref):
        x = h_ref[...]
        z_ref[...] = (x * _rstd(x) * g_ref[...]).astype(BF16)

    return _call(body, (h, g), name="rms_fwd", grid=(tp // tm,),
                 in_specs=[pl.BlockSpec((tm, d), lambda i: (i, 0)), _full((1, d))],
                 out_specs=pl.BlockSpec((tm, d), lambda i: (i, 0)),
                 out_shape=jax.ShapeDtypeStruct((tp, d), BF16), semantics=("parallel",))


def _proj(z, w_big, att_w):
    tp, d = z.shape
    nb = w_big.shape[1]
    tn = _divisor_tile(nb, LANES, 512)
    assert (3 * att_w) % tn == 0
    n_qkv = 3 * att_w // tn
    scale = 1.0 / math.sqrt(HEAD_DIM)

    def body(z_ref, w_ref, p_ref, qkv_ref):
        j = pl.program_id(0)
        acc = _dot(z_ref[...], w_ref[...])
        p_ref[...] = acc

        @pl.when(j < n_qkv)
        def _():
            col = j * tn + lax.broadcasted_iota(jnp.int32, (1, tn), 1)
            qkv_ref[...] = (acc * jnp.where(col < att_w, scale, 1.0)).astype(BF16)

    vm = 2 * (_nbytes((tp, d), BF16) + _nbytes((d, tn), BF16) + _nbytes((tp, tn), F32) * 2)
    return _call(body, (z, w_big), name="proj", grid=(nb // tn,),
                 in_specs=[_full((tp, d)), pl.BlockSpec((d, tn), lambda j: (0, j))],
                 out_specs=[pl.BlockSpec((tp, tn), lambda j: (0, j)),
                            pl.BlockSpec((tp, tn), lambda j: (0, jnp.minimum(j, n_qkv - 1)))],
                 out_shape=[jax.ShapeDtypeStruct((tp, nb), F32),
                            jax.ShapeDtypeStruct((tp, 3 * att_w), BF16)],
                 semantics=("arbitrary",), vmem_bytes=vm)


def _tile_cumsum(x, row, reverse=False):
    for s in (1, 2, 4):
        if reverse:
            x = x + jnp.where(row < SUBLANES - s, pltpu.roll(x, SUBLANES - s, 0), 0.0)
        else:
            x = x + jnp.where(row >= s, pltpu.roll(x, s, 0), 0.0)
    return x


def _fgate_fwd(proj, b_f_pad):
    tp, nb = proj.shape
    fblk = nb // LANES - 1

    def body(f_ref, b_ref, c_ref):
        b = b_ref[...]
        row = lax.broadcasted_iota(jnp.int32, (SUBLANES, LANES), 0)

        def step(i, carry):
            r0 = pl.multiple_of(i * SUBLANES, SUBLANES)
            lf = _log_sigmoid(f_ref[pl.ds(r0, SUBLANES), :] + b)
            x = _tile_cumsum(lf, row) + carry
            c_ref[pl.ds(r0, SUBLANES), :] = x
            return x[SUBLANES - 1:SUBLANES, :]

        lax.fori_loop(0, tp // SUBLANES, step, jnp.zeros((1, LANES), F32))

    return _call(body, (proj, b_f_pad), name="fgate_fwd", grid=(1,),
                 in_specs=[pl.BlockSpec((tp, LANES), lambda i: (0, fblk)), _full((1, LANES))],
                 out_specs=_full((tp, LANES)),
                 out_shape=jax.ShapeDtypeStruct((tp, LANES), F32), semantics=("arbitrary",))


def _fgate_bwd(proj, b_f_pad, dc):
    tp, nb = proj.shape
    fblk = nb // LANES - 1

    def body(f_ref, b_ref, dc_ref, df_ref, db_ref):
        b = b_ref[...]
        row = lax.broadcasted_iota(jnp.int32, (SUBLANES, LANES), 0)
        nt = tp // SUBLANES

        def step(i, carry):
            suffix, acc = carry
            r0 = pl.multiple_of((nt - 1 - i) * SUBLANES, SUBLANES)
            dlf = _tile_cumsum(dc_ref[pl.ds(r0, SUBLANES), :], row, reverse=True) + suffix
            df = dlf * _sigmoid(-(f_ref[pl.ds(r0, SUBLANES), :] + b))
            df_ref[pl.ds(r0, SUBLANES), :] = df
            return dlf[0:1, :], acc + df

        _, acc = lax.fori_loop(0, nt, step, (jnp.zeros((1, LANES), F32), jnp.zeros((SUBLANES, LANES), F32)))
        db_ref[...] = jnp.broadcast_to(jnp.sum(acc, axis=0, keepdims=True), (SUBLANES, LANES))

    return _call(body, (proj, b_f_pad, dc), name="fgate_bwd", grid=(1,),
                 in_specs=[pl.BlockSpec((tp, LANES), lambda i: (0, fblk)), _full((1, LANES)), _full((tp, LANES))],
                 out_specs=[_full((tp, LANES)), _full((SUBLANES, LANES))],
                 out_shape=[jax.ShapeDtypeStruct((tp, LANES), F32),
                            jax.ShapeDtypeStruct((SUBLANES, LANES), F32)], semantics=("arbitrary",))


ATT_BQ = 128


def _attn_scores(q, k, cq, ck, mask):
    return jnp.where(mask, _dot_nt(q, k) + (cq - ck), NEG_BIG)


def _attn_fwd(qkv, c_b, c_t, nh):
    tp = qkv.shape[0]
    att_w = nh * HEAD_DIM
    npair = nh // 2
    bq = ATT_BQ

    def body(q_ref, k_ref, v_ref, cb_ref, ct_ref, o_ref, lse_ref):
        p = pl.program_id(0)
        i = pl.program_id(1)
        rows = i * bq + lax.broadcasted_iota(jnp.int32, (bq, tp), 0)
        cols = lax.broadcasted_iota(jnp.int32, (bq, tp), 1)
        mask = cols <= rows
        outs, lses = [], []
        for hh in range(2):
            lo = HEAD_DIM * hh
            s = _attn_scores(q_ref[:, lo:lo + HEAD_DIM], k_ref[:, lo:lo + HEAD_DIM],
                             cb_ref[:, lo:lo + 1], ct_ref[pl.ds(2 * p + hh, 1), :], mask)
            m = jnp.max(s, axis=1, keepdims=True)
            e = jnp.exp(s - m)
            l = jnp.sum(e, axis=1, keepdims=True)
            o = _dot(e.astype(BF16), v_ref[:, lo:lo + HEAD_DIM]) / l
            outs.append(o)
            lses.append(jnp.broadcast_to(m + jnp.log(l), (bq, HEAD_DIM)))
        o_ref[...] = jnp.concatenate(outs, axis=1)
        lse_ref[...] = jnp.concatenate(lses, axis=1)

    blk = pl.BlockSpec((bq, LANES), lambda p, i: (i, p))
    vm = 4 * _nbytes((tp, LANES), BF16) + 8 * _nbytes((bq, tp), F32)
    return _call(body, (qkv, qkv, qkv, c_b, c_t), name="attn_fwd", grid=(npair, tp // bq),
                 in_specs=[blk,
                           pl.BlockSpec((tp, LANES), lambda p, i: (0, npair + p)),
                           pl.BlockSpec((tp, LANES), lambda p, i: (0, 2 * npair + p)),
                           blk, _full((nh, tp))],
                 out_specs=[blk, blk],
                 out_shape=[jax.ShapeDtypeStruct((tp, att_w), F32)] * 2,
                 semantics=("parallel", "parallel"), vmem_bytes=vm)


def _attn_bwd(qkv, c_b, c_t, lse_b, do, nh):
    tp = qkv.shape[0]
    att_w = nh * HEAD_DIM
    npair = nh // 2
    bq = ATT_BQ
    scale = 1.0 / math.sqrt(HEAD_DIM)

    def body(q_ref, k_ref, v_ref, cb_ref, ct_ref, lse_ref, do_ref, dq_ref, dk_ref, dv_ref, dct_ref):
        p = pl.program_id(0)
        i = pl.program_id(1)

        @pl.when(i == 0)
        def _():
            dk_ref[...] = jnp.zeros_like(dk_ref)
            dv_ref[...] = jnp.zeros_like(dv_ref)

        @pl.when(jnp.logical_and(i == 0, p == 0))
        def _():
            dct_ref[...] = jnp.zeros_like(dct_ref)

        rows = i * bq + lax.broadcasted_iota(jnp.int32, (bq, tp), 0)
        cols = lax.broadcasted_iota(jnp.int32, (bq, tp), 1)
        mask = cols <= rows
        dqs, dks, dvs = [], [], []
        for hh in range(2):
            lo = HEAD_DIM * hh
            q = q_ref[:, lo:lo + HEAD_DIM]
            k = k_ref[:, lo:lo + HEAD_DIM]
            v = v_ref[:, lo:lo + HEAD_DIM]
            s = _attn_scores(q, k, cb_ref[:, lo:lo + 1], ct_ref[pl.ds(2 * p + hh, 1), :], mask)
            pr = jnp.exp(s - lse_ref[:, lo:lo + 1])
            doutb = do_ref[:, lo:lo + HEAD_DIM].astype(BF16)
            dp = _dot_nt(doutb, v)
            ds = pr * (dp - jnp.sum(pr * dp, axis=1, keepdims=True))
            dsb = ds.astype(BF16)
            dqs.append(_dot(dsb, k) * scale)
            dks.append(_dot_tn(dsb, q))
            dvs.append(_dot_tn(pr.astype(BF16), doutb))
            hrow = pl.ds(2 * p + hh, 1)
            dct_ref[hrow, :] = dct_ref[hrow, :] - jnp.sum(ds, axis=0, keepdims=True)
        dq_ref[...] = jnp.concatenate(dqs, axis=1)
        dk_ref[...] += jnp.concatenate(dks, axis=1)
        dv_ref[...] += jnp.concatenate(dvs, axis=1)

    blk = pl.BlockSpec((bq, LANES), lambda p, i: (i, p))
    col = pl.BlockSpec((tp, LANES), lambda p, i: (0, p))
    vm = 4 * _nbytes((tp, LANES), BF16) + 4 * _nbytes((tp, LANES), F32) + 12 * _nbytes((bq, tp), F32)
    return _call(body, (qkv, qkv, qkv, c_b, c_t, lse_b, do), name="attn_bwd", grid=(npair, tp // bq),
                 in_specs=[blk,
                           pl.BlockSpec((tp, LANES), lambda p, i: (0, npair + p)),
                           pl.BlockSpec((tp, LANES), lambda p, i: (0, 2 * npair + p)),
                           blk, _full((nh, tp)), blk, blk],
                 out_specs=[blk, col, col, _full((nh, tp))],
                 out_shape=[jax.ShapeDtypeStruct((tp, att_w), F32)] * 3 + [jax.ShapeDtypeStruct((nh, tp), F32)],
                 semantics=("arbitrary", "arbitrary"), vmem_bytes=vm)


REC_ROWS = 128
HALO = SUBLANES


def _conv_taps(cat):
    taps = []
    for k in range(CONV_WIDTH):
        sh = CONV_WIDTH - 1 - k
        taps.append((pltpu.roll(cat, sh, 0) if sh else cat)[HALO:])
    return taps


def _rec_gates(xc, wa_ref, ba_ref, wx_ref, bx_ref, l_ref):
    xcb = xc.astype(BF16)
    r = _sigmoid(_dot(xcb, wa_ref[...]) + ba_ref[...])
    ig = _sigmoid(_dot(xcb, wx_ref[...]) + bx_ref[...])
    ls = _log_sigmoid(l_ref[...])
    log_a = RG_C * r * ls
    return xcb, r, ig, ls, log_a


def _rec_fwd(proj, xr_blk, yr_blk, rec_w, conv_w, conv_b, wa, ba, wx, bx, lru):
    tp = proj.shape[0]
    w = rec_w
    r_rows = REC_ROWS
    nc = tp // r_rows
    cpb = w // LANES

    def body(xr_ref, yr_ref, cw_ref, cb_ref, wa_ref, ba_ref, wx_ref, bx_ref, l_ref,
             hr_ref, rec_ref, prev_s, carry_s, a_s, u_s):
        i = pl.program_id(0)

        @pl.when(i == 0)
        def _():
            prev_s[...] = jnp.zeros_like(prev_s)
            carry_s[...] = jnp.zeros_like(carry_s)

        x = xr_ref[...]
        taps = _conv_taps(jnp.concatenate([prev_s[...], x], axis=0))
        prev_s[...] = x[r_rows - HALO:]
        xc = cb_ref[...]
        for k in range(CONV_WIDTH):
            xc = xc + cw_ref[k:k + 1, :] * taps[k]
        _, r, ig, ls, log_a = _rec_gates(xc, wa_ref, ba_ref, wx_ref, bx_ref, l_ref)
        a_s[...] = jnp.exp(log_a)
        u_s[...] = jnp.sqrt(-_expm1_nonpos(2.0 * log_a)) * ig * xc

        def tile(j, h):
            r0 = pl.multiple_of(j * SUBLANES, SUBLANES)
            at = a_s[pl.ds(r0, SUBLANES), :]
            ut = u_s[pl.ds(r0, SUBLANES), :]
            out = []
            for rr in range(SUBLANES):
                h = at[rr:rr + 1] * h + ut[rr:rr + 1]
                out.append(h)
            hr_ref[pl.ds(r0, SUBLANES), :] = jnp.concatenate(out, axis=0)
            return h

        carry_s[0:1, :] = lax.fori_loop(0, r_rows // SUBLANES, tile, carry_s[0:1, :])
        g, _ = _gelu_and_grad(yr_ref[...])
        rec_ref[...] = hr_ref[...] * g

    blk = pl.BlockSpec((r_rows, w), lambda i: (i, 0))
    vm = 16 * _nbytes((r_rows, w), F32) + 4 * _nbytes((w, w), BF16)
    return _call(body, (proj, proj, conv_w, conv_b, wa, ba, wx, bx, lru), name="rec_fwd", grid=(nc,),
                 in_specs=[pl.BlockSpec((r_rows, w), lambda i: (i, xr_blk)),
                           pl.BlockSpec((r_rows, w), lambda i: (i, yr_blk)),
                           _full((CONV_WIDTH, w)), _full((1, w)), _full((w, w)), _full((1, w)),
                           _full((w, w)), _full((1, w)), _full((1, w))],
                 out_specs=[blk, blk],
                 out_shape=[jax.ShapeDtypeStruct((tp, w), F32)] * 2,
                 scratch_shapes=[pltpu.VMEM((HALO, w), F32), pltpu.VMEM((SUBLANES, w), F32),
                                 pltpu.VMEM((r_rows, w), F32), pltpu.VMEM((r_rows, w), F32)],
                 semantics=("arbitrary",), vmem_bytes=vm)


def _rec_bwd(proj, xr_blk, yr_blk, rec_w, hr, drec, conv_w, conv_b, wa, ba, wx, bx, lru):
    tp = proj.shape[0]
    w = rec_w
    r_rows = REC_ROWS
    nc = tp // r_rows
    hpc = r_rows // HALO

    def body(xr_ref, xh_ref, yr_ref, hr_ref, hh_ref, drec_ref, cw_ref, cb_ref, wa_ref, ba_ref, wx_ref, bx_ref,
             l_ref, dxr_ref, dyr_ref, dwa_ref, dwx_ref, small_ref, lam_s, a_s, dhr_s, carry_s, next_s):
        i = pl.program_id(0)
        first = (nc - 1 - i) == 0

        @pl.when(i == 0)
        def _():
            carry_s[...] = jnp.zeros_like(carry_s)
            next_s[...] = jnp.zeros_like(next_s)
            dwa_ref[...] = jnp.zeros_like(dwa_ref)
            dwx_ref[...] = jnp.zeros_like(dwx_ref)
            small_ref[...] = jnp.zeros_like(small_ref)

        x = xr_ref[...]
        xprev = jnp.where(first, 0.0, xh_ref[...])
        taps = _conv_taps(jnp.concatenate([xprev, x], axis=0))
        xc = cb_ref[...]
        for k in range(CONV_WIDTH):
            xc = xc + cw_ref[k:k + 1, :] * taps[k]
        xcb, r, ig, ls, log_a = _rec_gates(xc, wa_ref, ba_ref, wx_ref, bx_ref, l_ref)
        a = jnp.exp(log_a)
        a2 = jnp.exp(2.0 * log_a)
        mult = jnp.sqrt(-_expm1_nonpos(2.0 * log_a))
        g, dg = _gelu_and_grad(yr_ref[...])
        hr_v = hr_ref[...]
        drec_v = drec_ref[...]
        dhr_s[...] = drec_v * g
        dyr_ref[...] = drec_v * hr_v * dg
        a_s[...] = a

        def tile(jj, carry):
            r0 = pl.multiple_of((r_rows // SUBLANES - 1 - jj) * SUBLANES, SUBLANES)
            at = a_s[pl.ds(r0, SUBLANES), :]
            dt = dhr_s[pl.ds(r0, SUBLANES), :]
            out = [None] * SUBLANES
            for rr in range(SUBLANES - 1, -1, -1):
                lam = dt[rr:rr + 1] + carry
                out[rr] = lam
                carry = at[rr:rr + 1] * lam
            lam_s[pl.ds(r0, SUBLANES), :] = jnp.concatenate(out, axis=0)
            return carry

        carry_s[0:1, :] = lax.fori_loop(0, r_rows // SUBLANES, tile, carry_s[0:1, :])
        lam = lam_s[...]
        hprev = jnp.where(first, 0.0, hh_ref[...])
        hr_prev = pltpu.roll(jnp.concatenate([hprev, hr_v], axis=0), 1, 0)[HALO:]
        da = lam * hr_prev
        dxc = lam * mult * ig
        di = lam * mult * xc
        dmult = lam * ig * xc
        dlog_a = da * a - dmult * a2 / mult
        dr = dlog_a * (RG_C * ls)
        dls = jnp.sum(dlog_a * (RG_C * r), axis=0, keepdims=True)
        dga = dr * r * (1.0 - r)
        dgx = di * ig * (1.0 - ig)
        dgab = dga.astype(BF16)
        dgxb = dgx.astype(BF16)
        dxc = dxc + _dot_nt(dgab, wa_ref[...]) + _dot_nt(dgxb, wx_ref[...])
        dwa_ref[...] += _dot_tn(xcb, dgab)
        dwx_ref[...] += _dot_tn(xcb, dgxb)
        cat = jnp.concatenate([dxc, next_s[...]], axis=0)
        next_s[...] = dxc[0:HALO]
        dxr = cw_ref[CONV_WIDTH - 1:CONV_WIDTH, :] * dxc
        for k in range(CONV_WIDTH - 1):
            sh = CONV_WIDTH - 1 - k
            dxr = dxr + cw_ref[k:k + 1, :] * pltpu.roll(cat, r_rows + HALO - sh, 0)[:r_rows]
        dxr_ref[...] = dxr
        rows = [jnp.sum(dxc * taps[k], axis=0, keepdims=True) for k in range(CONV_WIDTH)]
        rows += [jnp.sum(dxc, axis=0, keepdims=True), jnp.sum(dga, axis=0, keepdims=True),
                 jnp.sum(dgx, axis=0, keepdims=True), dls * _sigmoid(-l_ref[...])]
        small_ref[...] += jnp.concatenate(rows, axis=0)

    def rev(i):
        return nc - 1 - i

    def halo(i):
        return jnp.maximum(rev(i) * hpc - 1, 0)

    blk = pl.BlockSpec((r_rows, w), lambda i: (rev(i), 0))
    vm = 40 * _nbytes((r_rows, w), F32) + 6 * _nbytes((w, w), F32)
    return _call(body, (proj, proj, proj, hr, hr, drec, conv_w, conv_b, wa, ba, wx, bx, lru),
                 name="rec_bwd", grid=(nc,),
                 in_specs=[pl.BlockSpec((r_rows, w), lambda i: (rev(i), xr_blk)),
                           pl.BlockSpec((HALO, w), lambda i: (halo(i), xr_blk)),
                           pl.BlockSpec((r_rows, w), lambda i: (rev(i), yr_blk)),
                           blk,
                           pl.BlockSpec((HALO, w), lambda i: (halo(i), 0)),
                           blk,
                           _full((CONV_WIDTH, w)), _full((1, w)), _full((w, w)), _full((1, w)),
                           _full((w, w)), _full((1, w)), _full((1, w))],
                 out_specs=[blk, blk, _full((w, w)), _full((w, w)), _full((SUBLANES, w))],
                 out_shape=[jax.ShapeDtypeStruct((tp, w), F32)] * 2
                 + [jax.ShapeDtypeStruct((w, w), F32)] * 2 + [jax.ShapeDtypeStruct((SUBLANES, w), F32)],
                 scratch_shapes=[pltpu.VMEM((r_rows, w), F32)] * 3
                 + [pltpu.VMEM((SUBLANES, w), F32), pltpu.VMEM((HALO, w), F32)],
                 semantics=("arbitrary",), vmem_bytes=vm)


ROW_TARGET = 544


def _mixer_out(attn, rec, g_a, g_r, w_out, h, g_next):
    tp, d = h.shape
    aw, rw = attn.shape[1], rec.shape[1]
    tm = _divisor_tile(tp, 16, ROW_TARGET)

    def body(a_ref, r_ref, ga_ref, gr_ref, w_ref, h_ref, gn_ref, h1_ref, z_ref, mix_ref):
        a = a_ref[...]
        r = r_ref[...]
        mix = jnp.concatenate([a * _rstd(a) * ga_ref[...], r * _rstd(r) * gr_ref[...]], axis=1).astype(BF16)
        mix_ref[...] = mix
        h1 = h_ref[...] + _dot(mix, w_ref[...])
        h1_ref[...] = h1
        z_ref[...] = (h1 * _rstd(h1) * gn_ref[...]).astype(BF16)

    row = lambda wd: pl.BlockSpec((tm, wd), lambda i: (i, 0))
    vm = 2 * _nbytes((d, d), BF16) + 12 * _nbytes((tm, d), F32)
    return _call(body, (attn, rec, g_a, g_r, w_out, h, g_next), name="mixer_out", grid=(tp // tm,),
                 in_specs=[row(aw), row(rw), _full((1, aw)), _full((1, rw)), _full((d, d)), row(d), _full((1, d))],
                 out_specs=[row(d), row(d), row(d)],
                 out_shape=[jax.ShapeDtypeStruct((tp, d), F32), jax.ShapeDtypeStruct((tp, d), BF16),
                            jax.ShapeDtypeStruct((tp, d), BF16)],
                 semantics=("parallel",), vmem_bytes=vm)


def _mixer_bwd(dh_b, w_out, attn, rec, g_a, g_r):
    tp, d = dh_b.shape
    aw, rw = attn.shape[1], rec.shape[1]
    tm = _divisor_tile(tp, 16, ROW_TARGET)

    def body(dh_ref, w_ref, a_ref, r_ref, ga_ref, gr_ref, da_ref, dr_ref, dg_ref):
        @pl.when(pl.program_id(0) == 0)
        def _():
            dg_ref[...] = jnp.zeros_like(dg_ref)

        dmix = _dot_nt(dh_ref[...], w_ref[...])
        da, dga = _rms_bwd(dmix[:, :aw], a_ref[...], ga_ref[...])
        dr, dgr = _rms_bwd(dmix[:, aw:], r_ref[...], gr_ref[...])
        da_ref[...] = da
        dr_ref[...] = dr
        dg_ref[...] += jnp.broadcast_to(jnp.concatenate([dga, dgr], axis=1), (SUBLANES, d))

    row = lambda wd: pl.BlockSpec((tm, wd), lambda i: (i, 0))
    vm = 2 * _nbytes((d, d), BF16) + 12 * _nbytes((tm, d), F32)
    return _call(body, (dh_b, w_out, attn, rec, g_a, g_r), name="mixer_bwd", grid=(tp // tm,),
                 in_specs=[row(d), _full((d, d)), row(aw), row(rw), _full((1, aw)), _full((1, rw))],
                 out_specs=[row(aw), row(rw), _full((SUBLANES, d))],
                 out_shape=[jax.ShapeDtypeStruct((tp, aw), F32), jax.ShapeDtypeStruct((tp, rw), F32),
                            jax.ShapeDtypeStruct((SUBLANES, d), F32)],
                 semantics=("arbitrary",), vmem_bytes=vm)


def _mlp_up(z, w_up):
    tp, d = z.shape
    ff = w_up.shape[1]
    tn = _divisor_tile(ff, LANES, 512)

    def body(z_ref, w_ref, act_ref, up_ref):
        up = _dot(z_ref[...], w_ref[...])
        r = jnp.maximum(up, 0.0)
        act_ref[...] = (r * r).astype(BF16)
        up_ref[...] = up.astype(BF16)

    col = pl.BlockSpec((tp, tn), lambda j: (0, j))
    vm = 2 * _nbytes((tp, d), BF16) + 2 * _nbytes((d, tn), BF16) + 8 * _nbytes((tp, tn), F32)
    return _call(body, (z, w_up), name="mlp_up", grid=(ff // tn,),
                 in_specs=[_full((tp, d)), pl.BlockSpec((d, tn), lambda j: (0, j))],
                 out_specs=[col, col],
                 out_shape=[jax.ShapeDtypeStruct((tp, ff), BF16)] * 2,
                 semantics=("parallel",), vmem_bytes=vm)


def _mlp_down(act, w_down, h, g_next):
    tp, d = h.shape
    ff = act.shape[1]
    tm = _divisor_tile(tp, 16, ROW_TARGET)

    def body(a_ref, w_ref, h_ref, gn_ref, h2_ref, z_ref):
        h2 = h_ref[...] + _dot(a_ref[...], w_ref[...])
        h2_ref[...] = h2
        z_ref[...] = (h2 * _rstd(h2) * gn_ref[...]).astype(BF16)

    row = lambda wd: pl.BlockSpec((tm, wd), lambda i: (i, 0))
    vm = 2 * _nbytes((ff, d), BF16) + 2 * _nbytes((tm, ff), BF16) + 10 * _nbytes((tm, d), F32)
    return _call(body, (act, w_down, h, g_next), name="mlp_down", grid=(tp // tm,),
                 in_specs=[row(ff), _full((ff, d)), row(d), _full((1, d))],
                 out_specs=[row(d), row(d)],
                 out_shape=[jax.ShapeDtypeStruct((tp, d), F32), jax.ShapeDtypeStruct((tp, d), BF16)],
                 semantics=("parallel",), vmem_bytes=vm)


def _loss_bwd(h, g, target, n_real):
    tp, d = h.shape
    tm = _divisor_tile(tp, 16, ROW_TARGET)

    def body(h_ref, g_ref, t_ref, dh_ref, dhb_ref, dg_ref, loss_ref):
        i = pl.program_id(0)

        @pl.when(i == 0)
        def _():
            dg_ref[...] = jnp.zeros_like(dg_ref)
            loss_ref[...] = jnp.zeros_like(loss_ref)

        x = h_ref[...]
        gv = g_ref[...]
        rowi = i * tm + lax.broadcasted_iota(jnp.int32, (tm, 1), 0)
        real = jnp.logical_and(rowi >= N_META, rowi < N_META + n_real)
        err = jnp.where(real, x * _rstd(x) * gv - t_ref[...], 0.0)
        loss_ref[...] += 0.5 * jnp.sum(jnp.mean(err * err, axis=-1, keepdims=True))
        dx, dgp = _rms_bwd(err * (1.0 / d), x, gv)
        dh_ref[...] = dx
        dhb_ref[...] = dx.astype(BF16)
        dg_ref[...] += jnp.broadcast_to(dgp, (SUBLANES, d))

    row = pl.BlockSpec((tm, d), lambda i: (i, 0))
    return _call(body, (h, g, target), name="loss_bwd", grid=(tp // tm,),
                 in_specs=[row, _full((1, d)), row],
                 out_specs=[row, row, _full((SUBLANES, d)), _full((SUBLANES, LANES))],
                 out_shape=[jax.ShapeDtypeStruct((tp, d), F32), jax.ShapeDtypeStruct((tp, d), BF16),
                            jax.ShapeDtypeStruct((SUBLANES, d), F32), jax.ShapeDtypeStruct((SUBLANES, LANES), F32)],
                 semantics=("arbitrary",), vmem_bytes=16 * _nbytes((tm, d), F32))


def _mlp_dup(dh_b, w_down, up):
    tp, d = dh_b.shape
    ff = w_down.shape[0]
    tn = _divisor_tile(ff, LANES, 512)

    def body(dh_ref, w_ref, up_ref, dup_ref):
        dact = _dot_nt(dh_ref[...], w_ref[...])
        dup_ref[...] = (dact * (2.0 * jnp.maximum(up_ref[...].astype(F32), 0.0))).astype(BF16)

    col = pl.BlockSpec((tp, tn), lambda j: (0, j))
    vm = 2 * _nbytes((tp, d), BF16) + 2 * _nbytes((tn, d), BF16) + 8 * _nbytes((tp, tn), F32)
    return _call(body, (dh_b, w_down, up), name="mlp_dup", grid=(ff // tn,),
                 in_specs=[_full((tp, d)), pl.BlockSpec((tn, d), lambda j: (j, 0)), col],
                 out_specs=col, out_shape=jax.ShapeDtypeStruct((tp, ff), BF16),
                 semantics=("parallel",), vmem_bytes=vm)


def _grad_w(a, b, out_dtype=BF16):
    tp, k = a.shape
    n = b.shape[1]
    tk = _divisor_tile(k, LANES, 1024)
    tn = _divisor_tile(n, LANES, 512)

    def body(a_ref, b_ref, o_ref):
        o_ref[...] = _dot_tn(a_ref[...], b_ref[...]).astype(out_dtype)

    vm = 2 * _nbytes((tp, tk), BF16) + 2 * _nbytes((tp, tn), BF16) + 6 * _nbytes((tk, tn), F32) \
        + 2 * _nbytes((tp, tk), F32)
    return _call(body, (a, b), name="grad_w", grid=(k // tk, n // tn),
                 in_specs=[pl.BlockSpec((tp, tk), lambda i, j: (0, i)), pl.BlockSpec((tp, tn), lambda i, j: (0, j))],
                 out_specs=pl.BlockSpec((tk, tn), lambda i, j: (i, j)),
                 out_shape=jax.ShapeDtypeStruct((k, n), out_dtype),
                 semantics=("parallel", "parallel"), vmem_bytes=vm)


def _dx_norm_bwd(dy_b, w, h, g, dres):
    tp, kk = dy_b.shape
    d = w.shape[0]
    tm = _divisor_tile(tp, 16, ROW_TARGET)

    def body(dy_ref, w_ref, h_ref, g_ref, dres_ref, dh_ref, dhb_ref, dg_ref):
        @pl.when(pl.program_id(0) == 0)
        def _():
            dg_ref[...] = jnp.zeros_like(dg_ref)

        dz = _dot_nt(dy_ref[...], w_ref[...])
        dx, dgp = _rms_bwd(dz, h_ref[...], g_ref[...])
        dh = dres_ref[...] + dx
        dh_ref[...] = dh
        dhb_ref[...] = dh.astype(BF16)
        dg_ref[...] += jnp.broadcast_to(dgp, (SUBLANES, d))

    row = lambda wd: pl.BlockSpec((tm, wd), lambda i: (i, 0))
    vm = 2 * _nbytes((d, kk), BF16) + 2 * _nbytes((tm, kk), BF16) + 14 * _nbytes((tm, d), F32)
    return _call(body, (dy_b, w, h, g, dres), name="dx_norm_bwd", grid=(tp // tm,),
                 in_specs=[row(kk), _full((d, kk)), row(d), _full((1, d)), row(d)],
                 out_specs=[row(d), row(d), _full((SUBLANES, d))],
                 out_shape=[jax.ShapeDtypeStruct((tp, d), F32), jax.ShapeDtypeStruct((tp, d), BF16),
                            jax.ShapeDtypeStruct((SUBLANES, d), F32)],
                 semantics=("arbitrary",), vmem_bytes=vm)


def _block_diag(wg):
    nb, b, _ = wg.shape
    eye = jnp.eye(nb, dtype=wg.dtype)
    return (eye[:, None, :, None] * wg[:, :, None, :]).reshape(nb * b, nb * b)


def _diag_blocks(dense, nb):
    b = dense.shape[0] // nb
    d4 = dense.reshape(nb, b, nb, b)
    return jnp.stack([d4[i, :, i, :] for i in range(nb)])


def _local_step(x, target, meta, p):
    s_len, d = x.shape
    depth = p["w_qkvxyf"].shape[0]
    att_w = d // 2
    rec_w = d - att_w
    nh = att_w // HEAD_DIM
    t_len = N_META + s_len
    tp = -(-t_len // SEQ_TILE) * SEQ_TILE
    pad = tp - t_len
    h = jnp.concatenate([meta, x, jnp.zeros((pad, d), F32)], axis=0)
    tgt = jnp.concatenate([jnp.zeros((N_META, d), F32), target, jnp.zeros((pad, d), F32)], axis=0)
    xr_blk = 3 * att_w // rec_w
    yr_blk = xr_blk + 1
    row1 = lambda v: v.reshape(1, -1)

    saved = []
    z = _rms_fwd(h, row1(p["attn_norm_g"][0]))
    for l in range(depth):
        wa_d = _block_diag(p["w_gate_a"][l]).astype(BF16)
        wx_d = _block_diag(p["w_gate_x"][l]).astype(BF16)
        b_f_pad = jnp.zeros((1, LANES), F32).at[0, :nh].set(p["b_f"][l])
        proj, qkv = _proj(z, p["w_qkvxyf"][l], att_w)
        c = _fgate_fwd(proj, b_f_pad)
        c_t = c[:, :nh].T
        c_b = jnp.repeat(c[:, :nh], HEAD_DIM, axis=1)
        attn, lse_b = _attn_fwd(qkv, c_b, c_t, nh)
        hr, rec = _rec_fwd(proj, xr_blk, yr_blk, rec_w, p["conv_w"][l], row1(p["conv_b"][l]), wa_d,
                           row1(p["b_gate_a"][l]), wx_d, row1(p["b_gate_x"][l]), row1(p["lru_L"][l]))
        h1, z2, mix = _mixer_out(attn, rec, row1(p["attn_out_g"][l]), row1(p["rec_out_g"][l]),
                                 p["w_out"][l], h, row1(p["mlp_norm_g"][l]))
        act, up = _mlp_up(z2, p["w_up"][l])
        g_next = p["attn_norm_g"][l + 1] if l + 1 < depth else p["final_g"]
        h2, z_next = _mlp_down(act, p["w_down"][l], h1, row1(g_next))
        saved.append(dict(h0=h, z1=z, proj=proj, qkv=qkv, c_b=c_b, c_t=c_t, attn=attn, lse_b=lse_b, hr=hr,
                          rec=rec, h1=h1, z2=z2, mix=mix, act=act, up=up, wa_d=wa_d, wx_d=wx_d, b_f_pad=b_f_pad))
        h, z = h2, z_next

    dh, dh_b, dg_final, loss = _loss_bwd(h, row1(p["final_g"]), tgt, s_len)
    grads = {k: [None] * depth for k in (
        "attn_norm_g", "w_qkvxyf", "b_f", "conv_w", "conv_b", "w_gate_a", "b_gate_a", "w_gate_x", "b_gate_x",
        "lru_L", "attn_out_g", "rec_out_g", "w_out", "mlp_norm_g", "w_up", "w_down")}
    for l in reversed(range(depth)):
        sv = saved[l]
        dup = _mlp_dup(dh_b, p["w_down"][l], sv["up"])
        grads["w_down"][l] = _grad_w(sv["act"], dh_b)
        grads["w_up"][l] = _grad_w(sv["z2"], dup)
        dh, dh_b, dg2 = _dx_norm_bwd(dup, p["w_up"][l], sv["h1"], row1(p["mlp_norm_g"][l]), dh)
        grads["mlp_norm_g"][l] = dg2[0]
        grads["w_out"][l] = _grad_w(sv["mix"], dh_b)
        dattn, drec, dg_mix = _mixer_bwd(dh_b, p["w_out"][l], sv["attn"], sv["rec"],
                                         row1(p["attn_out_g"][l]), row1(p["rec_out_g"][l]))
        grads["attn_out_g"][l] = dg_mix[0, :att_w]
        grads["rec_out_g"][l] = dg_mix[0, att_w:]
        dxr, dyr, dwa, dwx, small = _rec_bwd(
            sv["proj"], xr_blk, yr_blk, rec_w, sv["hr"], drec, p["conv_w"][l], row1(p["conv_b"][l]), sv["wa_d"],
            row1(p["b_gate_a"][l]), sv["wx_d"], row1(p["b_gate_x"][l]), row1(p["lru_L"][l]))
        grads["conv_w"][l] = small[:CONV_WIDTH]
        grads["conv_b"][l] = small[4]
        grads["b_gate_a"][l] = small[5]
        grads["b_gate_x"][l] = small[6]
        grads["lru_L"][l] = small[7]
        grads["w_gate_a"][l] = _diag_blocks(dwa, N_REC_BLOCKS)
        grads["w_gate_x"][l] = _diag_blocks(dwx, N_REC_BLOCKS)
        dq, dk, dv, dct = _attn_bwd(sv["qkv"], sv["c_b"], sv["c_t"], sv["lse_b"], dattn, nh)
        dc = jnp.zeros((tp, LANES), F32).at[:, :nh].set(dct.T)
        df, db_f = _fgate_bwd(sv["proj"], sv["b_f_pad"], dc)
        grads["b_f"][l] = db_f[0, :nh]
        dproj = jnp.concatenate([dq.astype(BF16), dk.astype(BF16), dv.astype(BF16), dxr.astype(BF16),
                                 dyr.astype(BF16), df.astype(BF16)], axis=1)
        grads["w_qkvxyf"][l] = _grad_w(sv["z1"], dproj)
        dh, dh_b, dg1 = _dx_norm_bwd(dproj, p["w_qkvxyf"][l], sv["h0"], row1(p["attn_norm_g"][l]), dh)
        grads["attn_norm_g"][l] = dg1[0]
    grads = {k: jnp.stack(v) for k, v in grads.items()}
    grads["final_g"] = dg_final[0]
    return loss, dh[N_META:t_len], dh[:N_META], grads


def _pack_w_in(w_in, att_w, rec_w, nh):
    qkv = w_in[..., :3 * att_w]
    f = w_in[..., 3 * att_w:3 * att_w + nh]
    xy = w_in[..., 3 * att_w + nh:]
    fpad = jnp.zeros(f.shape[:-1] + (LANES - nh,), w_in.dtype)
    return jnp.concatenate([qkv, xy, f, fpad], axis=-1)


def _unpack_w_in(w_big, att_w, rec_w, nh):
    qkv = w_big[..., :3 * att_w]
    xy = w_big[..., 3 * att_w:3 * att_w + 2 * rec_w]
    f = w_big[..., 3 * att_w + 2 * rec_w:3 * att_w + 2 * rec_w + nh]
    return jnp.concatenate([qkv, f, xy], axis=-1)


ANY = pl.BlockSpec(memory_space=pl.ANY)


def _coords():
    return lax.axis_index("x"), lax.axis_index("y"), lax.axis_index("c")


def _other_chips(x, y):
    return [(1 - x, y), (x, 1 - y), (1 - x, 1 - y)]


def _remote(src, dst, send_sems, recv_sems, k, to):
    return pltpu.make_async_remote_copy(src_ref=src, dst_ref=dst, send_sem=send_sems.at[k],
                                        recv_sem=recv_sems.at[k], device_id=to, device_id_type=MESH)


def _all_gather_chips(shards):
    n = len(shards)
    per = 6

    def body(*refs):
        ins, outs = refs[:n], refs[n:2 * n]
        send_sems, recv_sems, local_sems = refs[2 * n:]
        x, y, c = _coords()
        me = 2 * x + y
        sibling = (x, y, 1 - c)
        chips = _other_chips(x, y)
        local = [pltpu.make_async_copy(ins[t], outs[t].at[me], local_sems.at[t]) for t in range(n)]
        for cp in local:
            cp.start()
        sends = []
        for t in range(n):
            for j, (px, py) in enumerate(chips):
                cp = _remote(ins[t].at[c], outs[t].at[me, c], send_sems, recv_sems, per * t + j, (px, py, c))
                cp.start()
                sends.append(cp)
        for t in range(n):
            for j, (px, py) in enumerate(chips):
                landed = outs[t].at[2 * px + py, c]
                _remote(landed, landed, send_sems, recv_sems, per * t + j, (px, py, c)).wait_recv()
                cp = _remote(landed, landed, send_sems, recv_sems, per * t + 3 + j, sibling)
                cp.start()
                sends.append(cp)
        for t in range(n):
            for j, (px, py) in enumerate(chips):
                passed = outs[t].at[2 * px + py, 1 - c]
                _remote(passed, passed, send_sems, recv_sems, per * t + 3 + j, sibling).wait_recv()
        for cp in sends:
            cp.wait_send()
        for cp in local:
            cp.wait()

    return _call(body, tuple(shards), name="all_gather_chips",
                 in_specs=[ANY] * n, out_specs=[ANY] * n,
                 out_shape=[jax.ShapeDtypeStruct((N_CHIPS,) + s.shape, s.dtype) for s in shards],
                 scratch_shapes=[pltpu.SemaphoreType.DMA((per * n,)), pltpu.SemaphoreType.DMA((per * n,)),
                                 pltpu.SemaphoreType.DMA((n,))])


def _scatter_to_owner_chips(parts):
    n = len(parts)
    per = 3

    def body(*refs):
        ins, outs = refs[:n], refs[n:2 * n]
        send_sems, recv_sems, local_sems = refs[2 * n:]
        x, y, c = _coords()
        me = 2 * x + y
        chips = _other_chips(x, y)
        local = [pltpu.make_async_copy(ins[t].at[me], outs[t].at[me], local_sems.at[t]) for t in range(n)]
        for cp in local:
            cp.start()
        sends = []
        for t in range(n):
            for j, (px, py) in enumerate(chips):
                cp = _remote(ins[t].at[2 * px + py], outs[t].at[me], send_sems, recv_sems, per * t + j, (px, py, c))
                cp.start()
                sends.append(cp)
        for t in range(n):
            for j, (px, py) in enumerate(chips):
                landed = outs[t].at[2 * px + py]
                _remote(landed, landed, send_sems, recv_sems, per * t + j, (px, py, c)).wait_recv()
        for cp in sends:
            cp.wait_send()
        for cp in local:
            cp.wait()

    return _call(body, tuple(parts), name="scatter_to_owner_chips",
                 in_specs=[ANY] * n, out_specs=[ANY] * n,
                 out_shape=[jax.ShapeDtypeStruct(s.shape, s.dtype) for s in parts],
                 scratch_shapes=[pltpu.SemaphoreType.DMA((per * n,)), pltpu.SemaphoreType.DMA((per * n,)),
                                 pltpu.SemaphoreType.DMA((n,))])


def _swap_with_sibling(arrs):
    n = len(arrs)

    def body(*refs):
        ins, outs = refs[:n], refs[n:2 * n]
        send_sems, recv_sems = refs[2 * n:]
        x, y, c = _coords()
        cps = [_remote(ins[t], outs[t], send_sems, recv_sems, t, (x, y, 1 - c)) for t in range(n)]
        for cp in cps:
            cp.start()
        for cp in cps:
            cp.wait_recv()
        for cp in cps:
            cp.wait_send()

    return _call(body, tuple(arrs), name="swap_with_sibling",
                 in_specs=[ANY] * n, out_specs=[ANY] * n,
                 out_shape=[jax.ShapeDtypeStruct(s.shape, s.dtype) for s in arrs],
                 scratch_shapes=[pltpu.SemaphoreType.DMA((n,)), pltpu.SemaphoreType.DMA((n,))])


def _all_gather_devices(buf):
    per = N_DEV - 1

    def body(in_ref, out_ref, send_sems, recv_sems, local_sem):
        x, y, c = _coords()
        me = 4 * x + 2 * y + c
        local = pltpu.make_async_copy(in_ref, out_ref.at[me], local_sem)
        local.start()
        peers = []
        for k in range(1, N_DEV):
            fx, fy, fc = (k >> 2) & 1, (k >> 1) & 1, k & 1
            peers.append((x ^ fx, y ^ fy, c ^ fc))
        sends = [_remote(in_ref, out_ref.at[me], send_sems, recv_sems, k, peer) for k, peer in enumerate(peers)]
        for cp in sends:
            cp.start()
        for k, (px, py, pc) in enumerate(peers):
            landed = out_ref.at[4 * px + 2 * py + pc]
            _remote(landed, landed, send_sems, recv_sems, k, (px, py, pc)).wait_recv()
        for cp in sends:
            cp.wait_send()
        local.wait()

    return _call(body, (buf,), name="all_gather_devices", in_specs=[ANY], out_specs=ANY,
                 out_shape=jax.ShapeDtypeStruct((N_DEV,) + buf.shape, buf.dtype),
                 scratch_shapes=[pltpu.SemaphoreType.DMA((per,)), pltpu.SemaphoreType.DMA((per,)),
                                 pltpu.SemaphoreType.DMA(())])


ELEM_ROWS = 256


def _sum_slabs(r):
    n, rows, cols = r.shape
    br = _divisor_tile(rows, 16, ELEM_ROWS)

    def body(r_ref, o_ref):
        acc = r_ref[0].astype(F32)
        for j in range(1, n):
            acc = acc + r_ref[j].astype(F32)
        o_ref[...] = acc

    return _call(body, (r,), name="sum_slabs", grid=(rows // br,),
                 in_specs=[pl.BlockSpec((n, br, cols), lambda i: (0, i, 0))],
                 out_specs=pl.BlockSpec((br, cols), lambda i: (i, 0)),
                 out_shape=jax.ShapeDtypeStruct((rows, cols), F32), semantics=("parallel",))


def _adamw(w, g_a, g_b, m, v):
    rows, cols = w.shape
    br = _divisor_tile(rows, 8, ELEM_ROWS)
    c1 = 1.0 - ADAM_B1 ** ADAM_STEP
    c2 = 1.0 - ADAM_B2 ** ADAM_STEP
    two = g_b is not None

    def body(*refs):
        w_ref, ga_ref = refs[0], refs[1]
        m_ref, v_ref, g_ref, d_ref, nm_ref, nv_ref = refs[3 if two else 2:]
        g = (ga_ref[...] + refs[2][...]) if two else ga_ref[...]
        nm = ADAM_B1 * m_ref[...] + (1.0 - ADAM_B1) * g
        nv = ADAM_B2 * v_ref[...] + (1.0 - ADAM_B2) * (g * g)
        g_ref[...] = g
        nm_ref[...] = nm
        nv_ref[...] = nv
        d_ref[...] = -ADAM_LR * ((nm / c1) / (jnp.sqrt(nv / c2) + ADAM_EPS) + ADAM_WD * w_ref[...])

    blk = pl.BlockSpec((br, cols), lambda i: (i, 0))
    args = (w, g_a, g_b, m, v) if two else (w, g_a, m, v)
    return _call(body, args, name="adamw", grid=(rows // br,),
                 in_specs=[blk] * len(args), out_specs=[blk] * 4,
                 out_shape=[jax.ShapeDtypeStruct((rows, cols), F32)] * 4, semantics=("parallel",))


BIG = ("w_in", "w_out", "w_up", "w_down")
WEIGHTS = ("meta", "attn_norm_g", "w_in", "b_f", "conv_w", "conv_b", "w_gate_a", "b_gate_a", "w_gate_x",
           "b_gate_x", "lru_L", "attn_out_g", "rec_out_g", "w_out", "mlp_norm_g", "w_up", "w_down", "final_g")
SMALL = tuple(k for k in WEIGHTS if k not in BIG)
COL_SHARDED_SMALL = ("meta", "conv_w")
PACK_UNIT = 16 * LANES


def _pack(arrs):
    flat = jnp.concatenate([a.reshape(-1) for a in arrs])
    total = -(-flat.shape[0] // PACK_UNIT) * PACK_UNIT
    return jnp.pad(flat, (0, total - flat.shape[0])).reshape(-1, LANES)


def _unpack(buf, shapes):
    flat = buf.reshape(-1)
    out, off = [], 0
    for s in shapes:
        size = math.prod(s)
        out.append(flat[off:off + size].reshape(s))
        off += size
    return out


def _halves(a):
    return a.reshape((2, a.shape[0] // 2) + a.shape[1:])


def _cols_from_chips(g):
    return jnp.moveaxis(g, 0, -2).reshape(g.shape[1:-1] + (N_CHIPS * g.shape[-1],))


def _rows_from_chips(g):
    return jnp.moveaxis(g, 0, 1).reshape((g.shape[1], N_CHIPS * g.shape[2], g.shape[3]))


def _cols_to_chips(w):
    l, k, n = w.shape
    return jnp.moveaxis(w.reshape(l, k, N_CHIPS, n // N_CHIPS), 2, 0)


def _rows_to_chips(w):
    l, k, n = w.shape
    return jnp.moveaxis(w.reshape(l, N_CHIPS, k // N_CHIPS, n), 1, 0)


def kernel(x, meta, attn_norm_g, w_in, b_f, conv_w, conv_b, w_gate_a, b_gate_a, w_gate_x, b_gate_x, lru_L, attn_out_g, rec_out_g, w_out, mlp_norm_g, w_up, w_down, final_g, loss_target, m_meta, m_attn_norm_g, m_w_in, m_b_f, m_conv_w, m_conv_b, m_w_gate_a, m_b_gate_a, m_w_gate_x, m_b_gate_x, m_lru_L, m_attn_out_g, m_rec_out_g, m_w_out, m_mlp_norm_g, m_w_up, m_w_down, m_final_g, v_meta, v_attn_norm_g, v_w_in, v_b_f, v_conv_w, v_conv_b, v_w_gate_a, v_b_gate_a, v_w_gate_x, v_b_gate_x, v_lru_L, v_attn_out_g, v_rec_out_g, v_w_out, v_mlp_norm_g, v_w_up, v_w_down, v_final_g):
    w = dict(meta=meta, attn_norm_g=attn_norm_g, w_in=w_in, b_f=b_f, conv_w=conv_w, conv_b=conv_b,
             w_gate_a=w_gate_a, b_gate_a=b_gate_a, w_gate_x=w_gate_x, b_gate_x=b_gate_x, lru_L=lru_L,
             attn_out_g=attn_out_g, rec_out_g=rec_out_g, w_out=w_out, mlp_norm_g=mlp_norm_g, w_up=w_up,
             w_down=w_down, final_g=final_g)
    m = dict(meta=m_meta, attn_norm_g=m_attn_norm_g, w_in=m_w_in, b_f=m_b_f, conv_w=m_conv_w, conv_b=m_conv_b,
             w_gate_a=m_w_gate_a, b_gate_a=m_b_gate_a, w_gate_x=m_w_gate_x, b_gate_x=m_b_gate_x, lru_L=m_lru_L,
             attn_out_g=m_attn_out_g, rec_out_g=m_rec_out_g, w_out=m_w_out, mlp_norm_g=m_mlp_norm_g,
             w_up=m_w_up, w_down=m_w_down, final_g=m_final_g)
    v = dict(meta=v_meta, attn_norm_g=v_attn_norm_g, w_in=v_w_in, b_f=v_b_f, conv_w=v_conv_w, conv_b=v_conv_b,
             w_gate_a=v_w_gate_a, b_gate_a=v_b_gate_a, w_gate_x=v_w_gate_x, b_gate_x=v_b_gate_x, lru_L=v_lru_L,
             attn_out_g=v_attn_out_g, rec_out_g=v_rec_out_g, w_out=v_w_out, mlp_norm_g=v_mlp_norm_g,
             w_up=v_w_up, w_down=v_w_down, final_g=v_final_g)
    d = x.shape[2]
    att_w = d // 2
    rec_w = d - att_w
    nh = att_w // HEAD_DIM
    chip = 2 * lax.axis_index("x") + lax.axis_index("y")

    gathered = _all_gather_chips([_halves(w[k].astype(BF16)) for k in BIG]
                                 + [_halves(w["conv_w"]), _halves(w["meta"])])
    g_in, g_out, g_up, g_down, g_conv, g_meta = [g.reshape((N_CHIPS, g.shape[1] * g.shape[2]) + g.shape[3:])
                                                 for g in gathered]
    p = dict(w)
    p["w_qkvxyf"] = _pack_w_in(_cols_from_chips(g_in), att_w, rec_w, nh)
    p["w_out"] = _rows_from_chips(g_out)
    p["w_up"] = _cols_from_chips(g_up)
    p["w_down"] = _rows_from_chips(g_down)
    p["conv_w"] = _cols_from_chips(g_conv)
    meta_full = jnp.moveaxis(g_meta, 0, 1).reshape(N_META, d)

    loss_part, dx, dmeta, grads = _local_step(x[0], loss_target[0], meta_full, p)
    grads["meta"] = dmeta

    parts = [_cols_to_chips(_unpack_w_in(grads["w_qkvxyf"], att_w, rec_w, nh)), _rows_to_chips(grads["w_out"]),
             _cols_to_chips(grads["w_up"]), _rows_to_chips(grads["w_down"])]
    landed = _scatter_to_owner_chips(parts)
    sums = [_sum_slabs(r.reshape(N_CHIPS, -1, r.shape[-1])) for r in landed]
    theirs = _swap_with_sibling(sums)
    out_g, out_d, out_m, out_v = {}, {}, {}, {}
    for k, mine, other in zip(BIG, sums, theirs):
        rows = lambda a: a.reshape(-1, a.shape[-1])
        res = _adamw(rows(w[k]), mine, other, rows(m[k]), rows(v[k]))
        out_g[k], out_d[k], out_m[k], out_v[k] = [r.reshape(w[k].shape) for r in res]

    full_shapes = [grads[k].shape for k in SMALL] + [(1,)]
    packed = _pack([grads[k].astype(F32) for k in SMALL] + [loss_part[0, :1]])
    total = _sum_slabs(_all_gather_devices(packed))
    small_g = dict(zip(SMALL + ("loss",), _unpack(total, full_shapes)))
    for k in COL_SHARDED_SMALL:
        n = w[k].shape[-1]
        small_g[k] = lax.dynamic_slice_in_dim(small_g[k], chip * n, n, axis=small_g[k].ndim - 1)
    local_shapes = [w[k].shape for k in SMALL]
    res = _adamw(_pack([w[k] for k in SMALL]), _pack([small_g[k] for k in SMALL]), None,
                 _pack([m[k] for k in SMALL]), _pack([v[k] for k in SMALL]))
    for dst, buf in zip((out_g, out_d, out_m, out_v), res):
        dst.update(zip(SMALL, _unpack(buf, local_shapes)))

    return (small_g["loss"].reshape(()), dx[None],
            *[out_g[k] for k in WEIGHTS], *[out_d[k] for k in WEIGHTS],
            *[out_m[k] for k in WEIGHTS], *[out_v[k] for k in WEIGHTS])
```

```python
import functools
import math

import jax
import jax.numpy as jnp
from jax import lax
from jax.experimental import pallas as pl
from jax.experimental.pallas import tpu as pltpu

F32 = jnp.float32
BF16 = jnp.bfloat16

N_META = 16
HEAD_DIM = 64
N_REC_BLOCKS = 8
CONV_WIDTH = 4
RG_C = 8.0
NORM_EPS = 1e-6
ADAM_LR = 0.001
ADAM_B1 = 0.9
ADAM_B2 = 0.999
ADAM_EPS = 1e-08
ADAM_WD = 0.01
ADAM_STEP = 10

LANES = 128
SUBLANES = 8
SEQ_TILE = 128
VMEM_CAP = 60 * 2**20
VMEM_SLACK = 6 * 2**20
NEG_BIG = -1e30
N_CHIPS = 4
N_DEV = 8
MESH = pl.DeviceIdType.MESH


def _nbytes(shape, dtype):
    return math.prod(shape) * jnp.dtype(dtype).itemsize


def _call(body, args, *, name, out_shape, grid=(), in_specs=None, out_specs=None,
          scratch_shapes=(), semantics=None, vmem_bytes=None, **kw):
    cp = {}
    if semantics is not None:
        cp["dimension_semantics"] = semantics
    if vmem_bytes is not None:
        cp["vmem_limit_bytes"] = int(min(VMEM_CAP, vmem_bytes + VMEM_SLACK))
    fn = pl.pallas_call(
        body, name=name, out_shape=out_shape, grid=grid,
        in_specs=in_specs, out_specs=out_specs, scratch_shapes=scratch_shapes,
        compiler_params=pltpu.CompilerParams(**cp), **kw)
    return fn(*args)


def _divisor_tile(n, unit, target):
    best = None
    for t in range(unit, min(n, target) + 1, unit):
        if n % t == 0:
            best = t
    return n if best is None else best


def _sigmoid(x):
    return 1.0 / (1.0 + jnp.exp(-x))


def _log1p_unit(e):
    series = e * (1.0 - e * (0.5 - e * (1.0 / 3.0)))
    return jnp.where(e < 1e-2, series, jnp.log(1.0 + e))


def _log_sigmoid(x):
    return jnp.minimum(x, 0.0) - _log1p_unit(jnp.exp(-jnp.abs(x)))


def _expm1_nonpos(x):
    small = x * (1.0 + x * (1.0 / 2 + x * (1.0 / 6 + x * (1.0 / 24 + x * (1.0 / 120 + x * (1.0 / 720))))))
    return jnp.where(x > -0.25, small, jnp.exp(x) - 1.0)


_GELU_K = math.sqrt(2.0 / math.pi)
_GELU_C = 0.044715


def _gelu_and_grad(y):
    th = jnp.tanh(_GELU_K * (y + _GELU_C * y * y * y))
    g = 0.5 * y * (1.0 + th)
    dg = 0.5 * (1.0 + th) + 0.5 * y * (1.0 - th * th) * _GELU_K * (1.0 + 3.0 * _GELU_C * y * y)
    return g, dg


def _rstd(x):
    return lax.rsqrt(jnp.mean(x * x, axis=-1, keepdims=True) + NORM_EPS)


def _rms_bwd(dz, x, g):
    rs = _rstd(x)
    xh = x * rs
    dgp = jnp.sum(dz * xh, axis=0, keepdims=True)
    dxh = dz * g
    dx = rs * (dxh - xh * jnp.mean(dxh * xh, axis=-1, keepdims=True))
    return dx, dgp


def _dot(a, b):
    return jnp.dot(a, b, preferred_element_type=F32)


def _dot_nt(a, b):
    return lax.dot_general(a, b, (((1,), (1,)), ((), ())), preferred_element_type=F32)


def _dot_tn(a, b):
    return lax.dot_general(a, b, (((0,), (0,)), ((), ())), preferred_element_type=F32)


def _full(shape):
    nd = len(shape)
    return pl.BlockSpec(shape, lambda *_: (0,) * nd)


def _rms_fwd(h, g):
    tp, d = h.shape
    tm = _divisor_tile(tp, 16, 544)

    def body(h_ref, g_ref, z_ref):
        x = h_ref[...]
        z_ref[...] = (x * _rstd(x) * g_ref[...]).astype(BF16)

    return _call(body, (h, g), name="rms_fwd", grid=(tp // tm,),
                 in_specs=[pl.BlockSpec((tm, d), lambda i: (i, 0)), _full((1, d))],
                 out_specs=pl.BlockSpec((tm, d), lambda i: (i, 0)),
                 out_shape=jax.ShapeDtypeStruct((tp, d), BF16), semantics=("parallel",))


def _proj(z, w_big, att_w):
    tp, d = z.shape
    nb = w_big.shape[1]
    tn = _divisor_tile(nb, LANES, 512)
    assert (3 * att_w) % tn == 0
    n_qkv = 3 * att_w // tn
    scale = 1.0 / math.sqrt(HEAD_DIM)

    def body(z_ref, w_ref, p_ref, qkv_ref):
        j = pl.program_id(0)
        acc = _dot(z_ref[...], w_ref[...])
        p_ref[...] = acc

        @pl.when(j < n_qkv)
        def _():
            col = j * tn + lax.broadcasted_iota(jnp.int32, (1, tn), 1)
            qkv_ref[...] = (acc * jnp.where(col < att_w, scale, 1.0)).astype(BF16)

    vm = 2 * (_nbytes((tp, d), BF16) + _nbytes((d, tn), BF16) + _nbytes((tp, tn), F32) * 2)
    return _call(body, (z, w_big), name="proj", grid=(nb // tn,),
                 in_specs=[_full((tp, d)), pl.BlockSpec((d, tn), lambda j: (0, j))],
                 out_specs=[pl.BlockSpec((tp, tn), lambda j: (0, j)),
                            pl.BlockSpec((tp, tn), lambda j: (0, jnp.minimum(j, n_qkv - 1)))],
                 out_shape=[jax.ShapeDtypeStruct((tp, nb), F32),
                            jax.ShapeDtypeStruct((tp, 3 * att_w), BF16)],
                 semantics=("arbitrary",), vmem_bytes=vm)


def _tile_cumsum(x, row, reverse=False):
    for s in (1, 2, 4):
        if reverse:
            x = x + jnp.where(row < SUBLANES - s, pltpu.roll(x, SUBLANES - s, 0), 0.0)
        else:
            x = x + jnp.where(row >= s, pltpu.roll(x, s, 0), 0.0)
    return x


def _fgate_fwd(proj, b_f_pad, nh):
    tp, nb = proj.shape
    fblk = nb // LANES - 1

    def body(f_ref, b_ref, c_ref, ct_ref):
        b = b_ref[...]
        row = lax.broadcasted_iota(jnp.int32, (SUBLANES, LANES), 0)

        def step(i, carry):
            r0 = pl.multiple_of(i * SUBLANES, SUBLANES)
            lf = _log_sigmoid(f_ref[pl.ds(r0, SUBLANES), :] + b)
            x = _tile_cumsum(lf, row) + carry
            c_ref[pl.ds(r0, SUBLANES), :] = x
            return x[SUBLANES - 1:SUBLANES, :]

        lax.fori_loop(0, tp // SUBLANES, step, jnp.zeros((1, LANES), F32))
        ct_ref[...] = c_ref[...].T[:nh, :]

    return _call(body, (proj, b_f_pad), name="fgate_fwd", grid=(1,),
                 in_specs=[pl.BlockSpec((tp, LANES), lambda i: (0, fblk)), _full((1, LANES))],
                 out_specs=[_full((tp, LANES)), _full((nh, tp))],
                 out_shape=[jax.ShapeDtypeStruct((tp, LANES), F32), jax.ShapeDtypeStruct((nh, tp), F32)],
                 semantics=("arbitrary",))


def _fgate_bwd(proj, b_f_pad, dct):
    tp, nb = proj.shape
    nh = dct.shape[0]
    fblk = nb // LANES - 1

    def body(f_ref, b_ref, dct_ref, df_ref, db_ref, dc_s):
        b = b_ref[...]
        row = lax.broadcasted_iota(jnp.int32, (SUBLANES, LANES), 0)
        nt = tp // SUBLANES
        dc_s[...] = jnp.concatenate([dct_ref[...], jnp.zeros((LANES - nh, tp), F32)], axis=0).T

        def step(i, carry):
            suffix, acc = carry
            r0 = pl.multiple_of((nt - 1 - i) * SUBLANES, SUBLANES)
            dlf = _tile_cumsum(dc_s[pl.ds(r0, SUBLANES), :], row, reverse=True) + suffix
            df = dlf * _sigmoid(-(f_ref[pl.ds(r0, SUBLANES), :] + b))
            dc_s[pl.ds(r0, SUBLANES), :] = df
            return dlf[0:1, :], acc + df

        _, acc = lax.fori_loop(0, nt, step, (jnp.zeros((1, LANES), F32), jnp.zeros((SUBLANES, LANES), F32)))
        df_ref[...] = dc_s[...].astype(BF16)
        db_ref[...] = jnp.broadcast_to(jnp.sum(acc, axis=0, keepdims=True), (SUBLANES, LANES))

    return _call(body, (proj, b_f_pad, dct), name="fgate_bwd", grid=(1,),
                 in_specs=[pl.BlockSpec((tp, LANES), lambda i: (0, fblk)), _full((1, LANES)), _full((nh, tp))],
                 out_specs=[_full((tp, LANES)), _full((SUBLANES, LANES))],
                 out_shape=[jax.ShapeDtypeStruct((tp, LANES), BF16),
                            jax.ShapeDtypeStruct((SUBLANES, LANES), F32)],
                 scratch_shapes=[pltpu.VMEM((tp, LANES), F32)], semantics=("arbitrary",))


ATT_BQ = 128


def _attn_scores(q, k, cq, ck, mask):
    return jnp.where(mask, _dot_nt(q, k) + (cq - ck), NEG_BIG)


ATT_BUCKET = 3


def _key_buckets(nq):
    return [(lo, min(lo + ATT_BUCKET, nq), min(lo + ATT_BUCKET, nq) * ATT_BQ) for lo in range(0, nq, ATT_BUCKET)]


def _for_bucket(i, nq, fn):
    for lo, hi, klen in _key_buckets(nq):
        pl.when(jnp.logical_and(i >= lo, i < hi))(functools.partial(fn, klen))


def _head_column(c_blk, h):
    lane = lax.broadcasted_iota(jnp.int32, c_blk.shape, 1)
    return jnp.sum(jnp.where(lane == h, c_blk, 0.0), axis=1, keepdims=True)


def _causal_mask(i, klen):
    rows = i * ATT_BQ + lax.broadcasted_iota(jnp.int32, (ATT_BQ, klen), 0)
    return lax.broadcasted_iota(jnp.int32, (ATT_BQ, klen), 1) <= rows


def _attn_fwd(qkv, c, c_t, nh):
    tp = qkv.shape[0]
    att_w = nh * HEAD_DIM
    npair = nh // 2
    bq = ATT_BQ
    nq = tp // bq

    def body(q_ref, k_ref, v_ref, c_ref, ct_ref, o_ref, lse_ref):
        p = pl.program_id(0)
        i = pl.program_id(1)

        def compute(klen):
            mask = _causal_mask(i, klen)
            outs, lses = [], []
            for hh in range(2):
                lo = HEAD_DIM * hh
                h = 2 * p + hh
                s = _attn_scores(q_ref[:, lo:lo + HEAD_DIM], k_ref[:klen, lo:lo + HEAD_DIM],
                                 _head_column(c_ref[...], h), ct_ref[pl.ds(h, 1), :klen], mask)
                m = jnp.max(s, axis=1, keepdims=True)
                e = jnp.exp(s - m)
                l = jnp.sum(e, axis=1, keepdims=True)
                outs.append(_dot(e.astype(BF16), v_ref[:klen, lo:lo + HEAD_DIM]) / l)
                lses.append(jnp.broadcast_to(m + jnp.log(l), (bq, HEAD_DIM)))
            o_ref[...] = jnp.concatenate(outs, axis=1)
            lse_ref[...] = jnp.concatenate(lses, axis=1)

        _for_bucket(i, nq, compute)

    blk = pl.BlockSpec((bq, LANES), lambda p, i: (i, p))
    vm = 4 * _nbytes((tp, LANES), BF16) + 8 * _nbytes((bq, tp), F32)
    return _call(body, (qkv, qkv, qkv, c, c_t), name="attn_fwd", grid=(npair, nq),
                 in_specs=[blk,
                           pl.BlockSpec((tp, LANES), lambda p, i: (0, npair + p)),
                           pl.BlockSpec((tp, LANES), lambda p, i: (0, 2 * npair + p)),
                           pl.BlockSpec((bq, LANES), lambda p, i: (i, 0)), _full((nh, tp))],
                 out_specs=[blk, blk],
                 out_shape=[jax.ShapeDtypeStruct((tp, att_w), F32)] * 2,
                 semantics=("parallel", "parallel"), vmem_bytes=vm)


def _attn_bwd(qkv, c, c_t, lse_b, do, nh):
    tp = qkv.shape[0]
    att_w = nh * HEAD_DIM
    npair = nh // 2
    bq = ATT_BQ
    nq = tp // bq
    scale = 1.0 / math.sqrt(HEAD_DIM)

    def body(q_ref, k_ref, v_ref, c_ref, ct_ref, lse_ref, do_ref, dq_ref, dk_ref, dv_ref, dct_ref, dk_s, dv_s):
        p = pl.program_id(0)
        i = pl.program_id(1)

        @pl.when(i == 0)
        def _():
            dk_s[...] = jnp.zeros_like(dk_s)
            dv_s[...] = jnp.zeros_like(dv_s)

        @pl.when(jnp.logical_and(i == 0, p == 0))
        def _():
            dct_ref[...] = jnp.zeros_like(dct_ref)

        def compute(klen):
            mask = _causal_mask(i, klen)
            dqs, dks, dvs = [], [], []
            for hh in range(2):
                lo = HEAD_DIM * hh
                h = 2 * p + hh
                q = q_ref[:, lo:lo + HEAD_DIM]
                k = k_ref[:klen, lo:lo + HEAD_DIM]
                v = v_ref[:klen, lo:lo + HEAD_DIM]
                s = _attn_scores(q, k, _head_column(c_ref[...], h), ct_ref[pl.ds(h, 1), :klen], mask)
                pr = jnp.exp(s - lse_ref[:, lo:lo + 1])
                doutb = do_ref[:, lo:lo + HEAD_DIM].astype(BF16)
                dp = _dot_nt(doutb, v)
                ds = pr * (dp - jnp.sum(pr * dp, axis=1, keepdims=True))
                dsb = ds.astype(BF16)
                dqs.append(_dot(dsb, k) * scale)
                dks.append(_dot_tn(dsb, q))
                dvs.append(_dot_tn(pr.astype(BF16), doutb))
                dct_ref[pl.ds(h, 1), :klen] = dct_ref[pl.ds(h, 1), :klen] - jnp.sum(ds, axis=0, keepdims=True)
            dq_ref[...] = jnp.concatenate(dqs, axis=1).astype(BF16)
            dk_s[:klen, :] += jnp.concatenate(dks, axis=1)
            dv_s[:klen, :] += jnp.concatenate(dvs, axis=1)

        _for_bucket(i, nq, compute)

        @pl.when(i == nq - 1)
        def _():
            dk_ref[...] = dk_s[...].astype(BF16)
            dv_ref[...] = dv_s[...].astype(BF16)

    blk = pl.BlockSpec((bq, LANES), lambda p, i: (i, p))
    col = pl.BlockSpec((tp, LANES), lambda p, i: (0, p))
    vm = 6 * _nbytes((tp, LANES), BF16) + 2 * _nbytes((tp, LANES), F32) + 12 * _nbytes((bq, tp), F32)
    return _call(body, (qkv, qkv, qkv, c, c_t, lse_b, do), name="attn_bwd", grid=(npair, nq),
                 in_specs=[blk,
                           pl.BlockSpec((tp, LANES), lambda p, i: (0, npair + p)),
                           pl.BlockSpec((tp, LANES), lambda p, i: (0, 2 * npair + p)),
                           pl.BlockSpec((bq, LANES), lambda p, i: (i, 0)), _full((nh, tp)), blk, blk],
                 out_specs=[blk, col, col, _full((nh, tp))],
                 out_shape=[jax.ShapeDtypeStruct((tp, att_w), BF16)] * 3 + [jax.ShapeDtypeStruct((nh, tp), F32)],
                 scratch_shapes=[pltpu.VMEM((tp, LANES), F32)] * 2,
                 semantics=("arbitrary", "arbitrary"), vmem_bytes=vm)


REC_ROWS = 128
HALO = SUBLANES


def _conv_taps(cat):
    taps = []
    for k in range(CONV_WIDTH):
        sh = CONV_WIDTH - 1 - k
        taps.append((pltpu.roll(cat, sh, 0) if sh else cat)[HALO:])
    return taps


def _rec_gates(xc, wa_ref, ba_ref, wx_ref, bx_ref, l_ref):
    xcb = xc.astype(BF16)
    r = _sigmoid(_dot(xcb, wa_ref[...]) + ba_ref[...])
    ig = _sigmoid(_dot(xcb, wx_ref[...]) + bx_ref[...])
    ls = _log_sigmoid(l_ref[...])
    log_a = RG_C * r * ls
    return xcb, r, ig, ls, log_a


def _rec_fwd(proj, xr_blk, yr_blk, rec_w, conv_w, conv_b, wa, ba, wx, bx, lru):
    tp = proj.shape[0]
    w = rec_w
    r_rows = REC_ROWS
    nc = tp // r_rows
    cpb = w // LANES

    def body(xr_ref, yr_ref, cw_ref, cb_ref, wa_ref, ba_ref, wx_ref, bx_ref, l_ref,
             hr_ref, rec_ref, prev_s, carry_s, a_s, u_s):
        i = pl.program_id(0)

        @pl.when(i == 0)
        def _():
            prev_s[...] = jnp.zeros_like(prev_s)
            carry_s[...] = jnp.zeros_like(carry_s)

        x = xr_ref[...]
        taps = _conv_taps(jnp.concatenate([prev_s[...], x], axis=0))
        prev_s[...] = x[r_rows - HALO:]
        xc = cb_ref[...]
        for k in range(CONV_WIDTH):
            xc = xc + cw_ref[k:k + 1, :] * taps[k]
        _, r, ig, ls, log_a = _rec_gates(xc, wa_ref, ba_ref, wx_ref, bx_ref, l_ref)
        a_s[...] = jnp.exp(log_a)
        u_s[...] = jnp.sqrt(-_expm1_nonpos(2.0 * log_a)) * ig * xc

        def tile(j, h):
            r0 = pl.multiple_of(j * SUBLANES, SUBLANES)
            at = a_s[pl.ds(r0, SUBLANES), :]
            ut = u_s[pl.ds(r0, SUBLANES), :]
            out = []
            for rr in range(SUBLANES):
                h = at[rr:rr + 1] * h + ut[rr:rr + 1]
                out.append(h)
            hr_ref[pl.ds(r0, SUBLANES), :] = jnp.concatenate(out, axis=0)
            return h

        carry_s[0:1, :] = lax.fori_loop(0, r_rows // SUBLANES, tile, carry_s[0:1, :])
        g, _ = _gelu_and_grad(yr_ref[...])
        rec_ref[...] = hr_ref[...] * g

    blk = pl.BlockSpec((r_rows, w), lambda i: (i, 0))
    vm = 16 * _nbytes((r_rows, w), F32) + 4 * _nbytes((w, w), BF16)
    return _call(body, (proj, proj, conv_w, conv_b, wa, ba, wx, bx, lru), name="rec_fwd", grid=(nc,),
                 in_specs=[pl.BlockSpec((r_rows, w), lambda i: (i, xr_blk)),
                           pl.BlockSpec((r_rows, w), lambda i: (i, yr_blk)),
                           _full((CONV_WIDTH, w)), _full((1, w)), _full((w, w)), _full((1, w)),
                           _full((w, w)), _full((1, w)), _full((1, w))],
                 out_specs=[blk, blk],
                 out_shape=[jax.ShapeDtypeStruct((tp, w), F32)] * 2,
                 scratch_shapes=[pltpu.VMEM((HALO, w), F32), pltpu.VMEM((SUBLANES, w), F32),
                                 pltpu.VMEM((r_rows, w), F32), pltpu.VMEM((r_rows, w), F32)],
                 semantics=("arbitrary",), vmem_bytes=vm)


def _rec_bwd(proj, xr_blk, yr_blk, rec_w, hr, drec, conv_w, conv_b, wa, ba, wx, bx, lru):
    tp = proj.shape[0]
    w = rec_w
    r_rows = REC_ROWS
    nc = tp // r_rows
    hpc = r_rows // HALO

    def body(xr_ref, xh_ref, yr_ref, hr_ref, hh_ref, drec_ref, cw_ref, cb_ref, wa_ref, ba_ref, wx_ref, bx_ref,
             l_ref, dxr_ref, dyr_ref, dwa_ref, dwx_ref, small_ref, lam_s, a_s, dhr_s, carry_s, next_s):
        i = pl.program_id(0)
        first = (nc - 1 - i) == 0

        @pl.when(i == 0)
        def _():
            carry_s[...] = jnp.zeros_like(carry_s)
            next_s[...] = jnp.zeros_like(next_s)
            dwa_ref[...] = jnp.zeros_like(dwa_ref)
            dwx_ref[...] = jnp.zeros_like(dwx_ref)
            small_ref[...] = jnp.zeros_like(small_ref)

        x = xr_ref[...]
        xprev = jnp.where(first, 0.0, xh_ref[...])
        taps = _conv_taps(jnp.concatenate([xprev, x], axis=0))
        xc = cb_ref[...]
        for k in range(CONV_WIDTH):
            xc = xc + cw_ref[k:k + 1, :] * taps[k]
        xcb, r, ig, ls, log_a = _rec_gates(xc, wa_ref, ba_ref, wx_ref, bx_ref, l_ref)
        a = jnp.exp(log_a)
        a2 = jnp.exp(2.0 * log_a)
        mult = jnp.sqrt(-_expm1_nonpos(2.0 * log_a))
        g, dg = _gelu_and_grad(yr_ref[...])
        hr_v = hr_ref[...]
        drec_v = drec_ref[...]
        dhr_s[...] = drec_v * g
        dyr_ref[...] = (drec_v * hr_v * dg).astype(BF16)
        a_s[...] = a

        def tile(jj, carry):
            r0 = pl.multiple_of((r_rows // SUBLANES - 1 - jj) * SUBLANES, SUBLANES)
            at = a_s[pl.ds(r0, SUBLANES), :]
            dt = dhr_s[pl.ds(r0, SUBLANES), :]
            out = [None] * SUBLANES
            for rr in range(SUBLANES - 1, -1, -1):
                lam = dt[rr:rr + 1] + carry
                out[rr] = lam
                carry = at[rr:rr + 1] * lam
            lam_s[pl.ds(r0, SUBLANES), :] = jnp.concatenate(out, axis=0)
            return carry

        carry_s[0:1, :] = lax.fori_loop(0, r_rows // SUBLANES, tile, carry_s[0:1, :])
        lam = lam_s[...]
        hprev = jnp.where(first, 0.0, hh_ref[...])
        hr_prev = pltpu.roll(jnp.concatenate([hprev, hr_v], axis=0), 1, 0)[HALO:]
        da = lam * hr_prev
        dxc = lam * mult * ig
        di = lam * mult * xc
        dmult = lam * ig * xc
        dlog_a = da * a - dmult * a2 / mult
        dr = dlog_a * (RG_C * ls)
        dls = jnp.sum(dlog_a * (RG_C * r), axis=0, keepdims=True)
        dga = dr * r * (1.0 - r)
        dgx = di * ig * (1.0 - ig)
        dgab = dga.astype(BF16)
        dgxb = dgx.astype(BF16)
        dxc = dxc + _dot_nt(dgab, wa_ref[...]) + _dot_nt(dgxb, wx_ref[...])
        dwa_ref[...] += _dot_tn(xcb, dgab)
        dwx_ref[...] += _dot_tn(xcb, dgxb)
        cat = jnp.concatenate([dxc, next_s[...]], axis=0)
        next_s[...] = dxc[0:HALO]
        dxr = cw_ref[CONV_WIDTH - 1:CONV_WIDTH, :] * dxc
        for k in range(CONV_WIDTH - 1):
            sh = CONV_WIDTH - 1 - k
            dxr = dxr + cw_ref[k:k + 1, :] * pltpu.roll(cat, r_rows + HALO - sh, 0)[:r_rows]
        dxr_ref[...] = dxr.astype(BF16)
        rows = [jnp.sum(dxc * taps[k], axis=0, keepdims=True) for k in range(CONV_WIDTH)]
        rows += [jnp.sum(dxc, axis=0, keepdims=True), jnp.sum(dga, axis=0, keepdims=True),
                 jnp.sum(dgx, axis=0, keepdims=True), dls * _sigmoid(-l_ref[...])]
        small_ref[...] += jnp.concatenate(rows, axis=0)

    def rev(i):
        return nc - 1 - i

    def halo(i):
        return jnp.maximum(rev(i) * hpc - 1, 0)

    blk = pl.BlockSpec((r_rows, w), lambda i: (rev(i), 0))
    vm = 40 * _nbytes((r_rows, w), F32) + 6 * _nbytes((w, w), F32)
    return _call(body, (proj, proj, proj, hr, hr, drec, conv_w, conv_b, wa, ba, wx, bx, lru),
                 name="rec_bwd", grid=(nc,),
                 in_specs=[pl.BlockSpec((r_rows, w), lambda i: (rev(i), xr_blk)),
                           pl.BlockSpec((HALO, w), lambda i: (halo(i), xr_blk)),
                           pl.BlockSpec((r_rows, w), lambda i: (rev(i), yr_blk)),
                           blk,
                           pl.BlockSpec((HALO, w), lambda i: (halo(i), 0)),
                           blk,
                           _full((CONV_WIDTH, w)), _full((1, w)), _full((w, w)), _full((1, w)),
                           _full((w, w)), _full((1, w)), _full((1, w))],
                 out_specs=[blk, blk, _full((w, w)), _full((w, w)), _full((SUBLANES, w))],
                 out_shape=[jax.ShapeDtypeStruct((tp, w), BF16)] * 2
                 + [jax.ShapeDtypeStruct((w, w), F32)] * 2 + [jax.ShapeDtypeStruct((SUBLANES, w), F32)],
                 scratch_shapes=[pltpu.VMEM((r_rows, w), F32)] * 3
                 + [pltpu.VMEM((SUBLANES, w), F32), pltpu.VMEM((HALO, w), F32)],
                 semantics=("arbitrary",), vmem_bytes=vm)


ROW_TARGET = 544


def _layer_of(w, l):
    return pl.BlockSpec((N_CHIPS, None) + w.shape[2:], lambda *_: (0, l, 0, 0))


def _mixer_out(attn, rec, g_a, g_r, w_out, l, h, g_next):
    tp, d = h.shape
    aw, rw = attn.shape[1], rec.shape[1]
    kc = d // N_CHIPS
    tm = _divisor_tile(tp, 16, ROW_TARGET)

    def body(a_ref, r_ref, ga_ref, gr_ref, w_ref, h_ref, gn_ref, h1_ref, z_ref, mix_ref):
        a = a_ref[...]
        r = r_ref[...]
        mix = jnp.concatenate([a * _rstd(a) * ga_ref[...], r * _rstd(r) * gr_ref[...]], axis=1).astype(BF16)
        mix_ref[...] = mix
        h1 = h_ref[...]
        for j in range(N_CHIPS):
            h1 = h1 + _dot(mix[:, j * kc:(j + 1) * kc], w_ref[j])
        h1_ref[...] = h1
        z_ref[...] = (h1 * _rstd(h1) * gn_ref[...]).astype(BF16)

    row = lambda wd: pl.BlockSpec((tm, wd), lambda i: (i, 0))
    vm = 2 * _nbytes((d, d), BF16) + 12 * _nbytes((tm, d), F32)
    return _call(body, (attn, rec, g_a, g_r, w_out, h, g_next), name="mixer_out", grid=(tp // tm,),
                 in_specs=[row(aw), row(rw), _full((1, aw)), _full((1, rw)), _layer_of(w_out, l), row(d),
                           _full((1, d))],
                 out_specs=[row(d), row(d), row(d)],
                 out_shape=[jax.ShapeDtypeStruct((tp, d), F32), jax.ShapeDtypeStruct((tp, d), BF16),
                            jax.ShapeDtypeStruct((tp, d), BF16)],
                 semantics=("parallel",), vmem_bytes=vm)


def _mixer_bwd(dh_b, w_out, l, attn, rec, g_a, g_r):
    tp, d = dh_b.shape
    aw, rw = attn.shape[1], rec.shape[1]
    tm = _divisor_tile(tp, 16, ROW_TARGET)

    def body(dh_ref, w_ref, a_ref, r_ref, ga_ref, gr_ref, da_ref, dr_ref, dg_ref):
        @pl.when(pl.program_id(0) == 0)
        def _():
            dg_ref[...] = jnp.zeros_like(dg_ref)

        dh = dh_ref[...]
        dmix = jnp.concatenate([_dot_nt(dh, w_ref[j]) for j in range(N_CHIPS)], axis=1)
        da, dga = _rms_bwd(dmix[:, :aw], a_ref[...], ga_ref[...])
        dr, dgr = _rms_bwd(dmix[:, aw:], r_ref[...], gr_ref[...])
        da_ref[...] = da
        dr_ref[...] = dr
        dg_ref[...] += jnp.broadcast_to(jnp.concatenate([dga, dgr], axis=1), (SUBLANES, d))

    row = lambda wd: pl.BlockSpec((tm, wd), lambda i: (i, 0))
    vm = 2 * _nbytes((d, d), BF16) + 12 * _nbytes((tm, d), F32)
    return _call(body, (dh_b, w_out, attn, rec, g_a, g_r), name="mixer_bwd", grid=(tp // tm,),
                 in_specs=[row(d), _layer_of(w_out, l), row(aw), row(rw), _full((1, aw)), _full((1, rw))],
                 out_specs=[row(aw), row(rw), _full((SUBLANES, d))],
                 out_shape=[jax.ShapeDtypeStruct((tp, aw), F32), jax.ShapeDtypeStruct((tp, rw), F32),
                            jax.ShapeDtypeStruct((SUBLANES, d), F32)],
                 semantics=("arbitrary",), vmem_bytes=vm)


def _mlp_up(z, w_up, l):
    tp, d = z.shape
    fc = w_up.shape[3]
    ff = N_CHIPS * fc
    tn = _divisor_tile(fc, LANES, 512)
    per = fc // tn

    def body(z_ref, w_ref, act_ref, up_ref):
        up = _dot(z_ref[...], w_ref[...])
        r = jnp.maximum(up, 0.0)
        act_ref[...] = (r * r).astype(BF16)
        up_ref[...] = up.astype(BF16)

    col = pl.BlockSpec((tp, tn), lambda j: (0, j))
    vm = 2 * _nbytes((tp, d), BF16) + 2 * _nbytes((d, tn), BF16) + 8 * _nbytes((tp, tn), F32)
    return _call(body, (z, w_up), name="mlp_up", grid=(ff // tn,),
                 in_specs=[_full((tp, d)), pl.BlockSpec((None, None, d, tn), lambda j: (j // per, l, 0, j % per))],
                 out_specs=[col, col],
                 out_shape=[jax.ShapeDtypeStruct((tp, ff), BF16)] * 2,
                 semantics=("parallel",), vmem_bytes=vm)


def _mlp_down(act, w_down, l, h, g_next):
    tp, d = h.shape
    ff = act.shape[1]
    fc = ff // N_CHIPS
    tm = _divisor_tile(tp, 16, ROW_TARGET)

    def body(a_ref, w_ref, h_ref, gn_ref, h2_ref, z_ref):
        h2 = h_ref[...]
        for j in range(N_CHIPS):
            h2 = h2 + _dot(a_ref[:, j * fc:(j + 1) * fc], w_ref[j])
        h2_ref[...] = h2
        z_ref[...] = (h2 * _rstd(h2) * gn_ref[...]).astype(BF16)

    row = lambda wd: pl.BlockSpec((tm, wd), lambda i: (i, 0))
    vm = 2 * _nbytes((ff, d), BF16) + 2 * _nbytes((tm, ff), BF16) + 10 * _nbytes((tm, d), F32)
    return _call(body, (act, w_down, h, g_next), name="mlp_down", grid=(tp // tm,),
                 in_specs=[row(ff), _layer_of(w_down, l), row(d), _full((1, d))],
                 out_specs=[row(d), row(d)],
                 out_shape=[jax.ShapeDtypeStruct((tp, d), F32), jax.ShapeDtypeStruct((tp, d), BF16)],
                 semantics=("parallel",), vmem_bytes=vm)


def _loss_bwd(h, g, target, n_real):
    tp, d = h.shape
    tm = _divisor_tile(tp, 16, ROW_TARGET)

    def body(h_ref, g_ref, t_ref, dh_ref, dhb_ref, dg_ref, loss_ref):
        i = pl.program_id(0)

        @pl.when(i == 0)
        def _():
            dg_ref[...] = jnp.zeros_like(dg_ref)
            loss_ref[...] = jnp.zeros_like(loss_ref)

        x = h_ref[...]
        gv = g_ref[...]
        rowi = i * tm + lax.broadcasted_iota(jnp.int32, (tm, 1), 0)
        real = jnp.logical_and(rowi >= N_META, rowi < N_META + n_real)
        err = jnp.where(real, x * _rstd(x) * gv - t_ref[...], 0.0)
        loss_ref[...] += 0.5 * jnp.sum(jnp.mean(err * err, axis=-1, keepdims=True))
        dx, dgp = _rms_bwd(err * (1.0 / d), x, gv)
        dh_ref[...] = dx
        dhb_ref[...] = dx.astype(BF16)
        dg_ref[...] += jnp.broadcast_to(dgp, (SUBLANES, d))

    row = pl.BlockSpec((tm, d), lambda i: (i, 0))
    return _call(body, (h, g, target), name="loss_bwd", grid=(tp // tm,),
                 in_specs=[row, _full((1, d)), row],
                 out_specs=[row, row, _full((SUBLANES, d)), _full((SUBLANES, LANES))],
                 out_shape=[jax.ShapeDtypeStruct((tp, d), F32), jax.ShapeDtypeStruct((tp, d), BF16),
                            jax.ShapeDtypeStruct((SUBLANES, d), F32), jax.ShapeDtypeStruct((SUBLANES, LANES), F32)],
                 semantics=("arbitrary",), vmem_bytes=16 * _nbytes((tm, d), F32))


def _mlp_dup(dh_b, w_down, l, up):
    tp, d = dh_b.shape
    fc = w_down.shape[2]
    ff = N_CHIPS * fc
    tn = _divisor_tile(fc, LANES, 512)
    per = fc // tn

    def body(dh_ref, w_ref, up_ref, dup_ref):
        dact = _dot_nt(dh_ref[...], w_ref[...])
        dup_ref[...] = (dact * (2.0 * jnp.maximum(up_ref[...].astype(F32), 0.0))).astype(BF16)

    col = pl.BlockSpec((tp, tn), lambda j: (0, j))
    vm = 2 * _nbytes((tp, d), BF16) + 2 * _nbytes((tn, d), BF16) + 8 * _nbytes((tp, tn), F32)
    return _call(body, (dh_b, w_down, up), name="mlp_dup", grid=(ff // tn,),
                 in_specs=[_full((tp, d)), pl.BlockSpec((None, None, tn, d), lambda j: (j // per, l, j % per, 0)),
                           col],
                 out_specs=col, out_shape=jax.ShapeDtypeStruct((tp, ff), BF16),
                 semantics=("parallel",), vmem_bytes=vm)


def _grad_w(a, b, cols_to_chips=False):
    tp, k = a.shape
    n = b.shape[1]
    tk = _divisor_tile(k, LANES, 1024)
    nc = n // N_CHIPS if cols_to_chips else n
    tn = _divisor_tile(nc, LANES, 512)
    per = nc // tn

    def body(a_ref, b_ref, o_ref):
        o_ref[...] = _dot_tn(a_ref[...], b_ref[...]).astype(BF16)

    if cols_to_chips:
        out_spec = pl.BlockSpec((None, tk, tn), lambda i, j: (j // per, i, j % per))
        out_shape = jax.ShapeDtypeStruct((N_CHIPS, k, nc), BF16)
    else:
        out_spec = pl.BlockSpec((tk, tn), lambda i, j: (i, j))
        out_shape = jax.ShapeDtypeStruct((k, n), BF16)
    vm = 2 * _nbytes((tp, tk), BF16) + 2 * _nbytes((tp, tn), BF16) + 6 * _nbytes((tk, tn), F32) \
        + 2 * _nbytes((tp, tk), F32)
    return _call(body, (a, b), name="grad_w", grid=(k // tk, n // tn),
                 in_specs=[pl.BlockSpec((tp, tk), lambda i, j: (0, i)), pl.BlockSpec((tp, tn), lambda i, j: (0, j))],
                 out_specs=out_spec, out_shape=out_shape,
                 semantics=("parallel", "parallel"), vmem_bytes=vm)


def _dx_norm_bwd(pieces, w, w_spec, w_piece, h, g, dres):
    tp, d = h.shape
    tm = _divisor_tile(tp, 16, ROW_TARGET)
    n = len(pieces)

    def body(*refs):
        dy_refs = refs[:n]
        w_ref, h_ref, g_ref, dres_ref, dh_ref, dhb_ref, dg_ref = refs[n:]

        @pl.when(pl.program_id(0) == 0)
        def _():
            dg_ref[...] = jnp.zeros_like(dg_ref)

        dz = _dot_nt(dy_refs[0][...], w_piece(w_ref, 0))
        for i in range(1, n):
            dz = dz + _dot_nt(dy_refs[i][...], w_piece(w_ref, i))
        dx, dgp = _rms_bwd(dz, h_ref[...], g_ref[...])
        dh = dres_ref[...] + dx
        dh_ref[...] = dh
        dhb_ref[...] = dh.astype(BF16)
        dg_ref[...] += jnp.broadcast_to(dgp, (SUBLANES, d))

    row = lambda wd: pl.BlockSpec((tm, wd), lambda i: (i, 0))
    kk = sum(wd for _, _, wd in pieces)
    vm = 2 * _nbytes((d, kk), BF16) + 2 * _nbytes((tm, kk), BF16) + 14 * _nbytes((tm, d), F32)
    piece_specs = [pl.BlockSpec((tm, wd), functools.partial(lambda i, cb: (i, cb), cb=cb)) for _, cb, wd in pieces]
    return _call(body, tuple(a for a, _, _ in pieces) + (w, h, g, dres), name="dx_norm_bwd", grid=(tp // tm,),
                 in_specs=piece_specs + [w_spec, row(d), _full((1, d)), row(d)],
                 out_specs=[row(d), row(d), _full((SUBLANES, d))],
                 out_shape=[jax.ShapeDtypeStruct((tp, d), F32), jax.ShapeDtypeStruct((tp, d), BF16),
                            jax.ShapeDtypeStruct((SUBLANES, d), F32)],
                 semantics=("arbitrary",), vmem_bytes=vm)


def _block_diag(wg):
    nb, b, _ = wg.shape
    eye = jnp.eye(nb, dtype=wg.dtype)
    return (eye[:, None, :, None] * wg[:, :, None, :]).reshape(nb * b, nb * b)


def _diag_blocks(dense, nb):
    b = dense.shape[0] // nb
    d4 = dense.reshape(nb, b, nb, b)
    return jnp.stack([d4[i, :, i, :] for i in range(nb)])


def _row(v):
    return v.reshape(1, -1)


def _forward_layer(l, h, z, p, w_in_big, g_next):
    d = h.shape[1]
    att_w = d // 2
    rec_w = d - att_w
    nh = att_w // HEAD_DIM
    xr_blk = 3 * att_w // rec_w
    wa_d = _block_diag(p["w_gate_a"][l]).astype(BF16)
    wx_d = _block_diag(p["w_gate_x"][l]).astype(BF16)
    b_f_pad = jnp.zeros((1, LANES), F32).at[0, :nh].set(p["b_f"][l])
    proj, qkv = _proj(z, w_in_big, att_w)
    c, c_t = _fgate_fwd(proj, b_f_pad, nh)
    attn, lse_b = _attn_fwd(qkv, c, c_t, nh)
    hr, rec = _rec_fwd(proj, xr_blk, xr_blk + 1, rec_w, p["conv_w"][l], _row(p["conv_b"][l]), wa_d,
                       _row(p["b_gate_a"][l]), wx_d, _row(p["b_gate_x"][l]), _row(p["lru_L"][l]))
    h1, z2, mix = _mixer_out(attn, rec, _row(p["attn_out_g"][l]), _row(p["rec_out_g"][l]),
                             p["w_out"], l, h, _row(p["mlp_norm_g"][l]))
    act, up = _mlp_up(z2, p["w_up"], l)
    h2, z_next = _mlp_down(act, p["w_down"], l, h1, _row(g_next))
    saved = dict(h0=h, z1=z, proj=proj, qkv=qkv, c=c, c_t=c_t, attn=attn, lse_b=lse_b, hr=hr, rec=rec, h1=h1,
                 z2=z2, mix=mix, act=act, up=up, wa_d=wa_d, wx_d=wx_d, b_f_pad=b_f_pad, w_in_big=w_in_big)
    return h2, z_next, saved


def _backward_mlp(l, dh, dh_b, sv, p):
    fc = p["w_up"].shape[3]
    dup = _mlp_dup(dh_b, p["w_down"], l, sv["up"])
    g_down = _grad_w(sv["act"], dh_b)
    g_up = _grad_w(sv["z2"], dup, cols_to_chips=True)
    dh, dh_b, dg2 = _dx_norm_bwd([(dup, j, fc) for j in range(N_CHIPS)], p["w_up"], _layer_of(p["w_up"], l),
                                 lambda w_ref, j: w_ref[j], sv["h1"], _row(p["mlp_norm_g"][l]), dh)
    big = dict(w_down=g_down.reshape((N_CHIPS, -1) + g_down.shape[1:]), w_up=g_up)
    return dh, dh_b, big, dict(mlp_norm_g=dg2[0])


def _backward_mixer(l, dh, dh_b, sv, p):
    d = dh.shape[1]
    att_w = d // 2
    rec_w = d - att_w
    nh = att_w // HEAD_DIM
    xr_blk = 3 * att_w // rec_w
    small = {}
    g_out = _grad_w(sv["mix"], dh_b)
    dattn, drec, dg_mix = _mixer_bwd(dh_b, p["w_out"], l, sv["attn"], sv["rec"],
                                     _row(p["attn_out_g"][l]), _row(p["rec_out_g"][l]))
    small["attn_out_g"] = dg_mix[0, :att_w]
    small["rec_out_g"] = dg_mix[0, att_w:]
    dxr, dyr, dwa, dwx, sm = _rec_bwd(
        sv["proj"], xr_blk, xr_blk + 1, rec_w, sv["hr"], drec, p["conv_w"][l], _row(p["conv_b"][l]), sv["wa_d"],
        _row(p["b_gate_a"][l]), sv["wx_d"], _row(p["b_gate_x"][l]), _row(p["lru_L"][l]))
    small.update(conv_w=sm[:CONV_WIDTH], conv_b=sm[4], b_gate_a=sm[5], b_gate_x=sm[6], lru_L=sm[7],
                 w_gate_a=_diag_blocks(dwa, N_REC_BLOCKS), w_gate_x=_diag_blocks(dwx, N_REC_BLOCKS))
    dq, dk, dv, dct = _attn_bwd(sv["qkv"], sv["c"], sv["c_t"], sv["lse_b"], dattn, nh)
    df, db_f = _fgate_bwd(sv["proj"], sv["b_f_pad"], dct)
    small["b_f"] = db_f[0, :nh]
    pieces = [dq, dk, dv, dxr, dyr, df]
    offs = [0, att_w, 2 * att_w, 3 * att_w, 3 * att_w + rec_w, 3 * att_w + 2 * rec_w]
    gq, gk, gv, gxr, gyr, gf = [_grad_w(sv["z1"], pc) for pc in pieces]
    g_in = jnp.concatenate([gq, gk, gv, gf[:, :nh], gxr, gyr], axis=1)
    w_big = sv["w_in_big"]
    widths = [pc.shape[1] for pc in pieces]
    dh, dh_b, dg1 = _dx_norm_bwd(
        [(pc, 0, wd) for pc, wd in zip(pieces, widths)], w_big, _full(w_big.shape),
        lambda w_ref, i: w_ref[:, offs[i]:offs[i] + widths[i]], sv["h0"], _row(p["attn_norm_g"][l]), dh)
    small["attn_norm_g"] = dg1[0]
    big = dict(w_in=jnp.moveaxis(g_in.reshape(d, N_CHIPS, -1), 1, 0),
               w_out=g_out.reshape((N_CHIPS, -1) + g_out.shape[1:]))
    return dh, dh_b, big, small


def _pack_w_in(w_in, att_w, rec_w, nh):
    qkv = w_in[..., :3 * att_w]
    f = w_in[..., 3 * att_w:3 * att_w + nh]
    xy = w_in[..., 3 * att_w + nh:]
    fpad = jnp.zeros(f.shape[:-1] + (LANES - nh,), w_in.dtype)
    return jnp.concatenate([qkv, xy, f, fpad], axis=-1)


ANY = pl.BlockSpec(memory_space=pl.ANY)


def _coords():
    return lax.axis_index("x"), lax.axis_index("y"), lax.axis_index("c")


def _other_chips(x, y):
    return [(1 - x, y), (x, 1 - y), (1 - x, 1 - y)]


def _remote(src, dst, send_sems, recv_sems, k, to):
    return pltpu.make_async_remote_copy(src_ref=src, dst_ref=dst, send_sem=send_sems.at[k],
                                        recv_sem=recv_sems.at[k], device_id=to, device_id_type=MESH)


def _all_gather_chips(shards):
    n = len(shards)
    per = 6

    def body(*refs):
        ins, outs = refs[:n], refs[n:2 * n]
        send_sems, recv_sems, local_sems = refs[2 * n:]
        x, y, c = _coords()
        me = 2 * x + y
        sibling = (x, y, 1 - c)
        chips = _other_chips(x, y)
        local = [pltpu.make_async_copy(ins[t], outs[t].at[me], local_sems.at[t]) for t in range(n)]
        for cp in local:
            cp.start()
        sends = []
        for t in range(n):
            for j, (px, py) in enumerate(chips):
                cp = _remote(ins[t].at[c], outs[t].at[me, c], send_sems, recv_sems, per * t + j, (px, py, c))
                cp.start()
                sends.append(cp)
        for t in range(n):
            for j, (px, py) in enumerate(chips):
                landed = outs[t].at[2 * px + py, c]
                _remote(landed, landed, send_sems, recv_sems, per * t + j, (px, py, c)).wait_recv()
                cp = _remote(landed, landed, send_sems, recv_sems, per * t + 3 + j, sibling)
                cp.start()
                sends.append(cp)
        for t in range(n):
            for j, (px, py) in enumerate(chips):
                passed = outs[t].at[2 * px + py, 1 - c]
                _remote(passed, passed, send_sems, recv_sems, per * t + 3 + j, sibling).wait_recv()
        for cp in sends:
            cp.wait_send()
        for cp in local:
            cp.wait()

    return _call(body, tuple(shards), name="all_gather_chips",
                 in_specs=[ANY] * n, out_specs=[ANY] * n,
                 out_shape=[jax.ShapeDtypeStruct((N_CHIPS,) + s.shape, s.dtype) for s in shards],
                 scratch_shapes=[pltpu.SemaphoreType.DMA((per * n,)), pltpu.SemaphoreType.DMA((per * n,)),
                                 pltpu.SemaphoreType.DMA((n,))])


def _scatter_to_owner_chips(parts):
    n = len(parts)
    per = 3

    def body(*refs):
        ins, outs = refs[:n], refs[n:2 * n]
        send_sems, recv_sems, local_sems = refs[2 * n:]
        x, y, c = _coords()
        me = 2 * x + y
        chips = _other_chips(x, y)
        local = [pltpu.make_async_copy(ins[t].at[me], outs[t].at[me], local_sems.at[t]) for t in range(n)]
        for cp in local:
            cp.start()
        sends = []
        for t in range(n):
            for j, (px, py) in enumerate(chips):
                cp = _remote(ins[t].at[2 * px + py], outs[t].at[me], send_sems, recv_sems, per * t + j, (px, py, c))
                cp.start()
                sends.append(cp)
        for t in range(n):
            for j, (px, py) in enumerate(chips):
                landed = outs[t].at[2 * px + py]
                _remote(landed, landed, send_sems, recv_sems, per * t + j, (px, py, c)).wait_recv()
        for cp in sends:
            cp.wait_send()
        for cp in local:
            cp.wait()

    return _call(body, tuple(parts), name="scatter_to_owner_chips",
                 in_specs=[ANY] * n, out_specs=[ANY] * n,
                 out_shape=[jax.ShapeDtypeStruct(s.shape, s.dtype) for s in parts],
                 scratch_shapes=[pltpu.SemaphoreType.DMA((per * n,)), pltpu.SemaphoreType.DMA((per * n,)),
                                 pltpu.SemaphoreType.DMA((n,))])


def _swap_with_sibling(arrs):
    n = len(arrs)

    def body(*refs):
        ins, outs = refs[:n], refs[n:2 * n]
        send_sems, recv_sems = refs[2 * n:]
        x, y, c = _coords()
        cps = [_remote(ins[t], outs[t], send_sems, recv_sems, t, (x, y, 1 - c)) for t in range(n)]
        for cp in cps:
            cp.start()
        for cp in cps:
            cp.wait_recv()
        for cp in cps:
            cp.wait_send()

    return _call(body, tuple(arrs), name="swap_with_sibling",
                 in_specs=[ANY] * n, out_specs=[ANY] * n,
                 out_shape=[jax.ShapeDtypeStruct(s.shape, s.dtype) for s in arrs],
                 scratch_shapes=[pltpu.SemaphoreType.DMA((n,)), pltpu.SemaphoreType.DMA((n,))])


def _all_gather_devices(buf):
    per = N_DEV - 1

    def body(in_ref, out_ref, send_sems, recv_sems, local_sem):
        x, y, c = _coords()
        me = 4 * x + 2 * y + c
        local = pltpu.make_async_copy(in_ref, out_ref.at[me], local_sem)
        local.start()
        peers = []
        for k in range(1, N_DEV):
            fx, fy, fc = (k >> 2) & 1, (k >> 1) & 1, k & 1
            peers.append((x ^ fx, y ^ fy, c ^ fc))
        sends = [_remote(in_ref, out_ref.at[me], send_sems, recv_sems, k, peer) for k, peer in enumerate(peers)]
        for cp in sends:
            cp.start()
        for k, (px, py, pc) in enumerate(peers):
            landed = out_ref.at[4 * px + 2 * py + pc]
            _remote(landed, landed, send_sems, recv_sems, k, (px, py, pc)).wait_recv()
        for cp in sends:
            cp.wait_send()
        local.wait()

    return _call(body, (buf,), name="all_gather_devices", in_specs=[ANY], out_specs=ANY,
                 out_shape=jax.ShapeDtypeStruct((N_DEV,) + buf.shape, buf.dtype),
                 scratch_shapes=[pltpu.SemaphoreType.DMA((per,)), pltpu.SemaphoreType.DMA((per,)),
                                 pltpu.SemaphoreType.DMA(())])


ELEM_ROWS = 256


def _sum_slabs(r):
    n, rows, cols = r.shape
    br = _divisor_tile(rows, 16, ELEM_ROWS)

    def body(r_ref, o_ref):
        acc = r_ref[0].astype(F32)
        for j in range(1, n):
            acc = acc + r_ref[j].astype(F32)
        o_ref[...] = acc

    return _call(body, (r,), name="sum_slabs", grid=(rows // br,),
                 in_specs=[pl.BlockSpec((n, br, cols), lambda i: (0, i, 0))],
                 out_specs=pl.BlockSpec((br, cols), lambda i: (i, 0)),
                 out_shape=jax.ShapeDtypeStruct((rows, cols), F32), semantics=("parallel",))


def _adamw_math(w, g, m, v):
    c1 = 1.0 - ADAM_B1 ** ADAM_STEP
    c2 = 1.0 - ADAM_B2 ** ADAM_STEP
    nm = ADAM_B1 * m + (1.0 - ADAM_B1) * g
    nv = ADAM_B2 * v + (1.0 - ADAM_B2) * (g * g)
    delta = -ADAM_LR * ((nm / c1) / (jnp.sqrt(nv / c2) + ADAM_EPS) + ADAM_WD * w)
    return delta, nm, nv


def _adamw(w, g, m, v):
    rows, cols = w.shape
    br = _divisor_tile(rows, 8, ELEM_ROWS)

    def body(w_ref, g_ref, m_ref, v_ref, d_ref, nm_ref, nv_ref):
        d_ref[...], nm_ref[...], nv_ref[...] = _adamw_math(w_ref[...], g_ref[...], m_ref[...], v_ref[...])

    blk = pl.BlockSpec((br, cols), lambda i: (i, 0))
    return _call(body, (w, g, m, v), name="adamw", grid=(rows // br,),
                 in_specs=[blk] * 4, out_specs=[blk] * 3,
                 out_shape=[jax.ShapeDtypeStruct((rows, cols), F32)] * 3, semantics=("parallel",))


def _adamw_layer(w, m, v, l, g_mine, g_theirs, prev):
    _, rows, cols = w.shape
    br = _divisor_tile(rows, 8, ELEM_ROWS)

    def body(w_ref, m_ref, v_ref, ga_ref, gb_ref, *rest):
        g_ref, d_ref, nm_ref, nv_ref = rest[4:]
        g = ga_ref[...] + gb_ref[...]
        g_ref[...] = g
        d_ref[...], nm_ref[...], nv_ref[...] = _adamw_math(w_ref[...], g, m_ref[...], v_ref[...])

    slot = pl.BlockSpec((None, br, cols), lambda i: (l, i, 0))
    blk = pl.BlockSpec((br, cols), lambda i: (i, 0))
    return _call(body, (w, m, v, g_mine, g_theirs) + tuple(prev), name="adamw_layer", grid=(rows // br,),
                 in_specs=[slot] * 3 + [blk] * 2 + [ANY] * 4, out_specs=[slot] * 4,
                 out_shape=[jax.ShapeDtypeStruct(w.shape, F32)] * 4,
                 input_output_aliases={5: 0, 6: 1, 7: 2, 8: 3}, semantics=("parallel",))


BIG = ("w_in", "w_out", "w_up", "w_down")
WEIGHTS = ("meta", "attn_norm_g", "w_in", "b_f", "conv_w", "conv_b", "w_gate_a", "b_gate_a", "w_gate_x",
           "b_gate_x", "lru_L", "attn_out_g", "rec_out_g", "w_out", "mlp_norm_g", "w_up", "w_down", "final_g")
SMALL = tuple(k for k in WEIGHTS if k not in BIG)
COL_SHARDED_SMALL = ("meta", "conv_w")
PACK_UNIT = 16 * LANES


def _pack(arrs):
    flat = jnp.concatenate([a.reshape(-1) for a in arrs])
    total = -(-flat.shape[0] // PACK_UNIT) * PACK_UNIT
    return jnp.pad(flat, (0, total - flat.shape[0])).reshape(-1, LANES)


def _unpack(buf, shapes):
    flat = buf.reshape(-1)
    out, off = [], 0
    for s in shapes:
        size = math.prod(s)
        out.append(flat[off:off + size].reshape(s))
        off += size
    return out


def _halves(a):
    return a.reshape((2, a.shape[0] // 2) + a.shape[1:])


def _cols_from_chips(g):
    return jnp.moveaxis(g, 0, -2).reshape(g.shape[1:-1] + (N_CHIPS * g.shape[-1],))


def kernel(x, meta, attn_norm_g, w_in, b_f, conv_w, conv_b, w_gate_a, b_gate_a, w_gate_x, b_gate_x, lru_L, attn_out_g, rec_out_g, w_out, mlp_norm_g, w_up, w_down, final_g, loss_target, m_meta, m_attn_norm_g, m_w_in, m_b_f, m_conv_w, m_conv_b, m_w_gate_a, m_b_gate_a, m_w_gate_x, m_b_gate_x, m_lru_L, m_attn_out_g, m_rec_out_g, m_w_out, m_mlp_norm_g, m_w_up, m_w_down, m_final_g, v_meta, v_attn_norm_g, v_w_in, v_b_f, v_conv_w, v_conv_b, v_w_gate_a, v_b_gate_a, v_w_gate_x, v_b_gate_x, v_lru_L, v_attn_out_g, v_rec_out_g, v_w_out, v_mlp_norm_g, v_w_up, v_w_down, v_final_g):
    w = dict(meta=meta, attn_norm_g=attn_norm_g, w_in=w_in, b_f=b_f, conv_w=conv_w, conv_b=conv_b,
             w_gate_a=w_gate_a, b_gate_a=b_gate_a, w_gate_x=w_gate_x, b_gate_x=b_gate_x, lru_L=lru_L,
             attn_out_g=attn_out_g, rec_out_g=rec_out_g, w_out=w_out, mlp_norm_g=mlp_norm_g, w_up=w_up,
             w_down=w_down, final_g=final_g)
    m = dict(meta=m_meta, attn_norm_g=m_attn_norm_g, w_in=m_w_in, b_f=m_b_f, conv_w=m_conv_w, conv_b=m_conv_b,
             w_gate_a=m_w_gate_a, b_gate_a=m_b_gate_a, w_gate_x=m_w_gate_x, b_gate_x=m_b_gate_x, lru_L=m_lru_L,
             attn_out_g=m_attn_out_g, rec_out_g=m_rec_out_g, w_out=m_w_out, mlp_norm_g=m_mlp_norm_g,
             w_up=m_w_up, w_down=m_w_down, final_g=m_final_g)
    v = dict(meta=v_meta, attn_norm_g=v_attn_norm_g, w_in=v_w_in, b_f=v_b_f, conv_w=v_conv_w, conv_b=v_conv_b,
             w_gate_a=v_w_gate_a, b_gate_a=v_b_gate_a, w_gate_x=v_w_gate_x, b_gate_x=v_b_gate_x, lru_L=v_lru_L,
             attn_out_g=v_attn_out_g, rec_out_g=v_rec_out_g, w_out=v_w_out, mlp_norm_g=v_mlp_norm_g,
             w_up=v_w_up, w_down=v_w_down, final_g=v_final_g)
    s_len, d = x.shape[1], x.shape[2]
    depth = w_in.shape[0]
    att_w = d // 2
    rec_w = d - att_w
    nh = att_w // HEAD_DIM
    chip = 2 * lax.axis_index("x") + lax.axis_index("y")

    gathered = _all_gather_chips([_halves(w[k].astype(BF16)) for k in BIG]
                                 + [_halves(w["conv_w"]), _halves(w["meta"])])
    g_in, g_out, g_up, g_down, g_conv, g_meta = [g.reshape((N_CHIPS, g.shape[1] * g.shape[2]) + g.shape[3:])
                                                 for g in gathered]
    p = dict(w)
    p.update(w_out=g_out, w_up=g_up, w_down=g_down, conv_w=_cols_from_chips(g_conv))
    w_in_big = [_pack_w_in(jnp.moveaxis(g_in[:, l], 0, 1).reshape(d, -1), att_w, rec_w, nh) for l in range(depth)]
    meta_full = jnp.moveaxis(g_meta, 0, 1).reshape(N_META, d)

    t_len = N_META + s_len
    pad = -t_len % SEQ_TILE
    h = jnp.concatenate([meta_full, x[0], jnp.zeros((pad, d), F32)], axis=0)
    tgt = jnp.concatenate([jnp.zeros((N_META, d), F32), loss_target[0], jnp.zeros((pad, d), F32)], axis=0)
    z = _rms_fwd(h, _row(p["attn_norm_g"][0]))
    saved = []
    for l in range(depth):
        g_next = p["attn_norm_g"][l + 1] if l + 1 < depth else p["final_g"]
        h, z, sv = _forward_layer(l, h, z, p, w_in_big[l], g_next)
        saved.append(sv)
    dh, dh_b, dg_final, loss_part = _loss_bwd(h, _row(p["final_g"]), tgt, s_len)

    small = {k: [None] * depth for k in SMALL if k not in ("meta", "final_g")}
    landed = [None] * depth
    for l in reversed(range(depth)):
        dh, dh_b, big_mlp, sm_mlp = _backward_mlp(l, dh, dh_b, saved[l], p)
        dh, dh_b, big_mix, sm_mix = _backward_mixer(l, dh, dh_b, saved[l], p)
        big = {**big_mlp, **big_mix}
        landed[l] = dict(zip(BIG, _scatter_to_owner_chips([big[k] for k in BIG])))
        for k, val in {**sm_mlp, **sm_mix}.items():
            small[k][l] = val
    grads = {k: jnp.stack(val) for k, val in small.items()}
    grads["final_g"] = dg_final[0]
    grads["meta"] = dh[:N_META]
    dx = dh[N_META:t_len]

    outs = {k: [lax.empty(w[k].shape, F32) for _ in range(4)] for k in BIG}
    for l in reversed(range(depth)):
        sums = [_sum_slabs(landed[l][k]) for k in BIG]
        theirs = _swap_with_sibling(sums)
        for k, mine, other in zip(BIG, sums, theirs):
            outs[k] = _adamw_layer(w[k], m[k], v[k], l, mine, other, outs[k])
    out_g, out_d, out_m, out_v = [{k: outs[k][i] for k in BIG} for i in range(4)]

    full_shapes = [grads[k].shape for k in SMALL] + [(1,)]
    packed = _pack([grads[k].astype(F32) for k in SMALL] + [loss_part[0, :1]])
    total = _sum_slabs(_all_gather_devices(packed))
    small_g = dict(zip(SMALL + ("loss",), _unpack(total, full_shapes)))
    for k in COL_SHARDED_SMALL:
        n = w[k].shape[-1]
        small_g[k] = lax.dynamic_slice_in_dim(small_g[k], chip * n, n, axis=small_g[k].ndim - 1)
    local_shapes = [w[k].shape for k in SMALL]
    res = _adamw(_pack([w[k] for k in SMALL]), _pack([small_g[k] for k in SMALL]),
                 _pack([m[k] for k in SMALL]), _pack([v[k] for k in SMALL]))
    out_g.update({k: small_g[k] for k in SMALL})
    for dst, buf in zip((out_d, out_m, out_v), res):
        dst.update(zip(SMALL, _unpack(buf, local_shapes)))

    return (small_g["loss"].reshape(()), dx[None],
            *[out_g[k] for k in WEIGHTS], *[out_d[k] for k in WEIGHTS],
            *[out_m[k] for k in WEIGHTS], *[out_v[k] for k in WEIGHTS])
```

```python
import functools
import math

import jax
import jax.numpy as jnp
from jax import lax
from jax.experimental import pallas as pl
from jax.experimental.pallas import tpu as pltpu

F32 = jnp.float32
BF16 = jnp.bfloat16

N_META = 16
HEAD_DIM = 64
N_REC_BLOCKS = 8
CONV_WIDTH = 4
RG_C = 8.0
NORM_EPS = 1e-6
ADAM_LR = 0.001
ADAM_B1 = 0.9
ADAM_B2 = 0.999
ADAM_EPS = 1e-08
ADAM_WD = 0.01
ADAM_STEP = 10

LANES = 128
SUBLANES = 8
SEQ_TILE = 128
VMEM_CAP = 60 * 2**20
VMEM_SLACK = 6 * 2**20
NEG_BIG = -1e30
N_CHIPS = 4
N_DEV = 8
MESH = pl.DeviceIdType.MESH


def _nbytes(shape, dtype):
    return math.prod(shape) * jnp.dtype(dtype).itemsize


def _call(body, args, *, name, out_shape, grid=(), in_specs=None, out_specs=None, scratch_shapes=(),
          grid_spec=None, semantics=None, vmem_bytes=None, side_effects=None, **kw):
    cp = {}
    if semantics is not None:
        cp["dimension_semantics"] = semantics
    if vmem_bytes is not None:
        cp["vmem_limit_bytes"] = int(min(VMEM_CAP, vmem_bytes + VMEM_SLACK))
    if side_effects is not None:
        cp["has_side_effects"] = side_effects
    if grid_spec is not None:
        kw["grid_spec"] = grid_spec
    else:
        kw.update(grid=grid, in_specs=in_specs, out_specs=out_specs, scratch_shapes=scratch_shapes)
    fn = pl.pallas_call(
        body, name=name, out_shape=out_shape,
        compiler_params=pltpu.CompilerParams(**cp), **kw)
    return fn(*args)


def _divisor_tile(n, unit, target):
    best = None
    for t in range(unit, min(n, target) + 1, unit):
        if n % t == 0:
            best = t
    return n if best is None else best


def _sigmoid(x):
    return 1.0 / (1.0 + jnp.exp(-x))


def _log1p_unit(e):
    series = e * (1.0 - e * (0.5 - e * (1.0 / 3.0)))
    return jnp.where(e < 1e-2, series, jnp.log(1.0 + e))


def _log_sigmoid(x):
    return jnp.minimum(x, 0.0) - _log1p_unit(jnp.exp(-jnp.abs(x)))


def _expm1_nonpos(x):
    small = x * (1.0 + x * (1.0 / 2 + x * (1.0 / 6 + x * (1.0 / 24 + x * (1.0 / 120 + x * (1.0 / 720))))))
    return jnp.where(x > -0.25, small, jnp.exp(x) - 1.0)


_GELU_K = math.sqrt(2.0 / math.pi)
_GELU_C = 0.044715


def _gelu_and_grad(y):
    th = jnp.tanh(_GELU_K * (y + _GELU_C * y * y * y))
    g = 0.5 * y * (1.0 + th)
    dg = 0.5 * (1.0 + th) + 0.5 * y * (1.0 - th * th) * _GELU_K * (1.0 + 3.0 * _GELU_C * y * y)
    return g, dg


def _rstd(x):
    return lax.rsqrt(jnp.mean(x * x, axis=-1, keepdims=True) + NORM_EPS)


def _rms_bwd(dz, x, g):
    rs = _rstd(x)
    xh = x * rs
    dgp = jnp.sum(dz * xh, axis=0, keepdims=True)
    dxh = dz * g
    dx = rs * (dxh - xh * jnp.mean(dxh * xh, axis=-1, keepdims=True))
    return dx, dgp


def _dot(a, b):
    return jnp.dot(a, b, preferred_element_type=F32)


def _dot_nt(a, b):
    return lax.dot_general(a, b, (((1,), (1,)), ((), ())), preferred_element_type=F32)


def _dot_tn(a, b):
    return lax.dot_general(a, b, (((0,), (0,)), ((), ())), preferred_element_type=F32)


def _full(shape):
    nd = len(shape)
    return pl.BlockSpec(shape, lambda *_: (0,) * nd)


def _rms_fwd(h, g):
    tp, d = h.shape
    tm = _divisor_tile(tp, 16, 544)

    def body(h_ref, g_ref, z_ref):
        x = h_ref[...]
        z_ref[...] = (x * _rstd(x) * g_ref[...]).astype(BF16)

    return _call(body, (h, g), name="rms_fwd", grid=(tp // tm,),
                 in_specs=[pl.BlockSpec((tm, d), lambda i: (i, 0)), _full((1, d))],
                 out_specs=pl.BlockSpec((tm, d), lambda i: (i, 0)),
                 out_shape=jax.ShapeDtypeStruct((tp, d), BF16), semantics=("parallel",))


def _proj(z, w_big, att_w):
    tp, d = z.shape
    nb = w_big.shape[1]
    tn = _divisor_tile(nb, LANES, 512)
    assert (3 * att_w) % tn == 0
    n_qkv = 3 * att_w // tn
    scale = 1.0 / math.sqrt(HEAD_DIM)

    def body(z_ref, w_ref, p_ref, qkv_ref):
        j = pl.program_id(0)
        acc = _dot(z_ref[...], w_ref[...])
        p_ref[...] = acc

        @pl.when(j < n_qkv)
        def _():
            col = j * tn + lax.broadcasted_iota(jnp.int32, (1, tn), 1)
            qkv_ref[...] = (acc * jnp.where(col < att_w, scale, 1.0)).astype(BF16)

    vm = 2 * (_nbytes((tp, d), BF16) + _nbytes((d, tn), BF16) + _nbytes((tp, tn), F32) * 2)
    return _call(body, (z, w_big), name="proj", grid=(nb // tn,),
                 in_specs=[_full((tp, d)), pl.BlockSpec((d, tn), lambda j: (0, j))],
                 out_specs=[pl.BlockSpec((tp, tn), lambda j: (0, j)),
                            pl.BlockSpec((tp, tn), lambda j: (0, jnp.minimum(j, n_qkv - 1)))],
                 out_shape=[jax.ShapeDtypeStruct((tp, nb), F32),
                            jax.ShapeDtypeStruct((tp, 3 * att_w), BF16)],
                 semantics=("arbitrary",), vmem_bytes=vm)


def _tile_cumsum(x, row, reverse=False):
    for s in (1, 2, 4):
        if reverse:
            x = x + jnp.where(row < SUBLANES - s, pltpu.roll(x, SUBLANES - s, 0), 0.0)
        else:
            x = x + jnp.where(row >= s, pltpu.roll(x, s, 0), 0.0)
    return x


def _fgate_fwd(proj, b_f_pad, nh):
    tp, nb = proj.shape
    fblk = nb // LANES - 1

    def body(f_ref, b_ref, c_ref, ct_ref):
        b = b_ref[...]
        row = lax.broadcasted_iota(jnp.int32, (SUBLANES, LANES), 0)

        def step(i, carry):
            r0 = pl.multiple_of(i * SUBLANES, SUBLANES)
            lf = _log_sigmoid(f_ref[pl.ds(r0, SUBLANES), :] + b)
            x = _tile_cumsum(lf, row) + carry
            c_ref[pl.ds(r0, SUBLANES), :] = x
            return x[SUBLANES - 1:SUBLANES, :]

        lax.fori_loop(0, tp // SUBLANES, step, jnp.zeros((1, LANES), F32))
        ct_ref[...] = c_ref[...].T[:nh, :]

    return _call(body, (proj, b_f_pad), name="fgate_fwd", grid=(1,),
                 in_specs=[pl.BlockSpec((tp, LANES), lambda i: (0, fblk)), _full((1, LANES))],
                 out_specs=[_full((tp, LANES)), _full((nh, tp))],
                 out_shape=[jax.ShapeDtypeStruct((tp, LANES), F32), jax.ShapeDtypeStruct((nh, tp), F32)],
                 semantics=("arbitrary",))


def _fgate_bwd(proj, b_f_pad, dct):
    tp, nb = proj.shape
    nh = dct.shape[0]
    fblk = nb // LANES - 1

    def body(f_ref, b_ref, dct_ref, df_ref, db_ref, dc_s):
        b = b_ref[...]
        row = lax.broadcasted_iota(jnp.int32, (SUBLANES, LANES), 0)
        nt = tp // SUBLANES
        dc_s[...] = jnp.concatenate([dct_ref[...], jnp.zeros((LANES - nh, tp), F32)], axis=0).T

        def step(i, carry):
            suffix, acc = carry
            r0 = pl.multiple_of((nt - 1 - i) * SUBLANES, SUBLANES)
            dlf = _tile_cumsum(dc_s[pl.ds(r0, SUBLANES), :], row, reverse=True) + suffix
            df = dlf * _sigmoid(-(f_ref[pl.ds(r0, SUBLANES), :] + b))
            dc_s[pl.ds(r0, SUBLANES), :] = df
            return dlf[0:1, :], acc + df

        _, acc = lax.fori_loop(0, nt, step, (jnp.zeros((1, LANES), F32), jnp.zeros((SUBLANES, LANES), F32)))
        df_ref[...] = dc_s[...].astype(BF16)
        db_ref[...] = jnp.broadcast_to(jnp.sum(acc, axis=0, keepdims=True), (SUBLANES, LANES))

    return _call(body, (proj, b_f_pad, dct), name="fgate_bwd", grid=(1,),
                 in_specs=[pl.BlockSpec((tp, LANES), lambda i: (0, fblk)), _full((1, LANES)), _full((nh, tp))],
                 out_specs=[_full((tp, LANES)), _full((SUBLANES, LANES))],
                 out_shape=[jax.ShapeDtypeStruct((tp, LANES), BF16),
                            jax.ShapeDtypeStruct((SUBLANES, LANES), F32)],
                 scratch_shapes=[pltpu.VMEM((tp, LANES), F32)], semantics=("arbitrary",))


ATT_BQ = 128


def _attn_scores(q, k, cq, ck, mask):
    return jnp.where(mask, _dot_nt(q, k) + (cq - ck), NEG_BIG)


ATT_BUCKET = 3


def _key_buckets(nq):
    return [(lo, min(lo + ATT_BUCKET, nq), min(lo + ATT_BUCKET, nq) * ATT_BQ) for lo in range(0, nq, ATT_BUCKET)]


def _for_bucket(i, nq, fn):
    for lo, hi, klen in _key_buckets(nq):
        pl.when(jnp.logical_and(i >= lo, i < hi))(functools.partial(fn, klen))


def _head_column(c_blk, h):
    lane = lax.broadcasted_iota(jnp.int32, c_blk.shape, 1)
    return jnp.sum(jnp.where(lane == h, c_blk, 0.0), axis=1, keepdims=True)


def _causal_mask(i, klen):
    rows = i * ATT_BQ + lax.broadcasted_iota(jnp.int32, (ATT_BQ, klen), 0)
    return lax.broadcasted_iota(jnp.int32, (ATT_BQ, klen), 1) <= rows


def _attn_fwd(qkv, c, c_t, nh):
    tp = qkv.shape[0]
    att_w = nh * HEAD_DIM
    npair = nh // 2
    bq = ATT_BQ
    nq = tp // bq

    def body(q_ref, k_ref, v_ref, c_ref, ct_ref, o_ref, lse_ref):
        p = pl.program_id(0)
        i = pl.program_id(1)

        def compute(klen):
            mask = _causal_mask(i, klen)
            outs, lses = [], []
            for hh in range(2):
                lo = HEAD_DIM * hh
                h = 2 * p + hh
                s = _attn_scores(q_ref[:, lo:lo + HEAD_DIM], k_ref[:klen, lo:lo + HEAD_DIM],
                                 _head_column(c_ref[...], h), ct_ref[pl.ds(h, 1), :klen], mask)
                m = jnp.max(s, axis=1, keepdims=True)
                e = jnp.exp(s - m)
                l = jnp.sum(e, axis=1, keepdims=True)
                outs.append(_dot(e.astype(BF16), v_ref[:klen, lo:lo + HEAD_DIM]) / l)
                lses.append(jnp.broadcast_to(m + jnp.log(l), (bq, HEAD_DIM)))
            o_ref[...] = jnp.concatenate(outs, axis=1)
            lse_ref[...] = jnp.concatenate(lses, axis=1)

        _for_bucket(i, nq, compute)

    blk = pl.BlockSpec((bq, LANES), lambda p, i: (i, p))
    vm = 4 * _nbytes((tp, LANES), BF16) + 8 * _nbytes((bq, tp), F32)
    return _call(body, (qkv, qkv, qkv, c, c_t), name="attn_fwd", grid=(npair, nq),
                 in_specs=[blk,
                           pl.BlockSpec((tp, LANES), lambda p, i: (0, npair + p)),
                           pl.BlockSpec((tp, LANES), lambda p, i: (0, 2 * npair + p)),
                           pl.BlockSpec((bq, LANES), lambda p, i: (i, 0)), _full((nh, tp))],
                 out_specs=[blk, blk],
                 out_shape=[jax.ShapeDtypeStruct((tp, att_w), F32)] * 2,
                 semantics=("parallel", "parallel"), vmem_bytes=vm)


def _attn_bwd(qkv, c, c_t, lse_b, do, nh):
    tp = qkv.shape[0]
    att_w = nh * HEAD_DIM
    npair = nh // 2
    bq = ATT_BQ
    nq = tp // bq
    scale = 1.0 / math.sqrt(HEAD_DIM)

    def body(q_ref, k_ref, v_ref, c_ref, ct_ref, lse_ref, do_ref, dq_ref, dk_ref, dv_ref, dct_ref, dk_s, dv_s):
        p = pl.program_id(0)
        i = pl.program_id(1)

        @pl.when(i == 0)
        def _():
            dk_s[...] = jnp.zeros_like(dk_s)
            dv_s[...] = jnp.zeros_like(dv_s)

        @pl.when(jnp.logical_and(i == 0, p == 0))
        def _():
            dct_ref[...] = jnp.zeros_like(dct_ref)

        def compute(klen):
            mask = _causal_mask(i, klen)
            dqs, dks, dvs = [], [], []
            for hh in range(2):
                lo = HEAD_DIM * hh
                h = 2 * p + hh
                q = q_ref[:, lo:lo + HEAD_DIM]
                k = k_ref[:klen, lo:lo + HEAD_DIM]
                v = v_ref[:klen, lo:lo + HEAD_DIM]
                s = _attn_scores(q, k, _head_column(c_ref[...], h), ct_ref[pl.ds(h, 1), :klen], mask)
                pr = jnp.exp(s - lse_ref[:, lo:lo + 1])
                doutb = do_ref[:, lo:lo + HEAD_DIM].astype(BF16)
                dp = _dot_nt(doutb, v)
                ds = pr * (dp - jnp.sum(pr * dp, axis=1, keepdims=True))
                dsb = ds.astype(BF16)
                dqs.append(_dot(dsb, k) * scale)
                dks.append(_dot_tn(dsb, q))
                dvs.append(_dot_tn(pr.astype(BF16), doutb))
                dct_ref[pl.ds(h, 1), :klen] = dct_ref[pl.ds(h, 1), :klen] - jnp.sum(ds, axis=0, keepdims=True)
            dq_ref[...] = jnp.concatenate(dqs, axis=1).astype(BF16)
            dk_s[:klen, :] += jnp.concatenate(dks, axis=1)
            dv_s[:klen, :] += jnp.concatenate(dvs, axis=1)

        _for_bucket(i, nq, compute)

        @pl.when(i == nq - 1)
        def _():
            dk_ref[...] = dk_s[...].astype(BF16)
            dv_ref[...] = dv_s[...].astype(BF16)

    blk = pl.BlockSpec((bq, LANES), lambda p, i: (i, p))
    col = pl.BlockSpec((tp, LANES), lambda p, i: (0, p))
    vm = 6 * _nbytes((tp, LANES), BF16) + 2 * _nbytes((tp, LANES), F32) + 12 * _nbytes((bq, tp), F32)
    return _call(body, (qkv, qkv, qkv, c, c_t, lse_b, do), name="attn_bwd", grid=(npair, nq),
                 in_specs=[blk,
                           pl.BlockSpec((tp, LANES), lambda p, i: (0, npair + p)),
                           pl.BlockSpec((tp, LANES), lambda p, i: (0, 2 * npair + p)),
                           pl.BlockSpec((bq, LANES), lambda p, i: (i, 0)), _full((nh, tp)), blk, blk],
                 out_specs=[blk, col, col, _full((nh, tp))],
                 out_shape=[jax.ShapeDtypeStruct((tp, att_w), BF16)] * 3 + [jax.ShapeDtypeStruct((nh, tp), F32)],
                 scratch_shapes=[pltpu.VMEM((tp, LANES), F32)] * 2,
                 semantics=("arbitrary", "arbitrary"), vmem_bytes=vm)


REC_ROWS = 128
HALO = SUBLANES


def _conv_taps(cat):
    taps = []
    for k in range(CONV_WIDTH):
        sh = CONV_WIDTH - 1 - k
        taps.append((pltpu.roll(cat, sh, 0) if sh else cat)[HALO:])
    return taps


def _rec_gates(xc, wa_ref, ba_ref, wx_ref, bx_ref, l_ref):
    xcb = xc.astype(BF16)
    r = _sigmoid(_dot(xcb, wa_ref[...]) + ba_ref[...])
    ig = _sigmoid(_dot(xcb, wx_ref[...]) + bx_ref[...])
    ls = _log_sigmoid(l_ref[...])
    log_a = RG_C * r * ls
    return xcb, r, ig, ls, log_a


def _rec_fwd(proj, xr_blk, yr_blk, rec_w, conv_w, conv_b, wa, ba, wx, bx, lru):
    tp = proj.shape[0]
    w = rec_w
    r_rows = REC_ROWS
    nc = tp // r_rows
    cpb = w // LANES

    def body(xr_ref, yr_ref, cw_ref, cb_ref, wa_ref, ba_ref, wx_ref, bx_ref, l_ref,
             hr_ref, rec_ref, prev_s, carry_s, a_s, u_s):
        i = pl.program_id(0)

        @pl.when(i == 0)
        def _():
            prev_s[...] = jnp.zeros_like(prev_s)
            carry_s[...] = jnp.zeros_like(carry_s)

        x = xr_ref[...]
        taps = _conv_taps(jnp.concatenate([prev_s[...], x], axis=0))
        prev_s[...] = x[r_rows - HALO:]
        xc = cb_ref[...]
        for k in range(CONV_WIDTH):
            xc = xc + cw_ref[k:k + 1, :] * taps[k]
        _, r, ig, ls, log_a = _rec_gates(xc, wa_ref, ba_ref, wx_ref, bx_ref, l_ref)
        a_s[...] = jnp.exp(log_a)
        u_s[...] = jnp.sqrt(-_expm1_nonpos(2.0 * log_a)) * ig * xc

        def tile(j, h):
            r0 = pl.multiple_of(j * SUBLANES, SUBLANES)
            at = a_s[pl.ds(r0, SUBLANES), :]
            ut = u_s[pl.ds(r0, SUBLANES), :]
            out = []
            for rr in range(SUBLANES):
                h = at[rr:rr + 1] * h + ut[rr:rr + 1]
                out.append(h)
            hr_ref[pl.ds(r0, SUBLANES), :] = jnp.concatenate(out, axis=0)
            return h

        carry_s[0:1, :] = lax.fori_loop(0, r_rows // SUBLANES, tile, carry_s[0:1, :])
        g, _ = _gelu_and_grad(yr_ref[...])
        rec_ref[...] = hr_ref[...] * g

    blk = pl.BlockSpec((r_rows, w), lambda i: (i, 0))
    vm = 16 * _nbytes((r_rows, w), F32) + 4 * _nbytes((w, w), BF16)
    return _call(body, (proj, proj, conv_w, conv_b, wa, ba, wx, bx, lru), name="rec_fwd", grid=(nc,),
                 in_specs=[pl.BlockSpec((r_rows, w), lambda i: (i, xr_blk)),
                           pl.BlockSpec((r_rows, w), lambda i: (i, yr_blk)),
                           _full((CONV_WIDTH, w)), _full((1, w)), _full((w, w)), _full((1, w)),
                           _full((w, w)), _full((1, w)), _full((1, w))],
                 out_specs=[blk, blk],
                 out_shape=[jax.ShapeDtypeStruct((tp, w), F32)] * 2,
                 scratch_shapes=[pltpu.VMEM((HALO, w), F32), pltpu.VMEM((SUBLANES, w), F32),
                                 pltpu.VMEM((r_rows, w), F32), pltpu.VMEM((r_rows, w), F32)],
                 semantics=("arbitrary",), vmem_bytes=vm)


def _rec_bwd(proj, xr_blk, yr_blk, rec_w, hr, drec, conv_w, conv_b, wa, ba, wx, bx, lru):
    tp = proj.shape[0]
    w = rec_w
    r_rows = REC_ROWS
    nc = tp // r_rows
    hpc = r_rows // HALO

    def body(xr_ref, xh_ref, yr_ref, hr_ref, hh_ref, drec_ref, cw_ref, cb_ref, wa_ref, ba_ref, wx_ref, bx_ref,
             l_ref, dxr_ref, dyr_ref, dwa_ref, dwx_ref, small_ref, lam_s, a_s, dhr_s, carry_s, next_s):
        i = pl.program_id(0)
        first = (nc - 1 - i) == 0

        @pl.when(i == 0)
        def _():
            carry_s[...] = jnp.zeros_like(carry_s)
            next_s[...] = jnp.zeros_like(next_s)
            dwa_ref[...] = jnp.zeros_like(dwa_ref)
            dwx_ref[...] = jnp.zeros_like(dwx_ref)
            small_ref[...] = jnp.zeros_like(small_ref)

        x = xr_ref[...]
        xprev = jnp.where(first, 0.0, xh_ref[...])
        taps = _conv_taps(jnp.concatenate([xprev, x], axis=0))
        xc = cb_ref[...]
        for k in range(CONV_WIDTH):
            xc = xc + cw_ref[k:k + 1, :] * taps[k]
        xcb, r, ig, ls, log_a = _rec_gates(xc, wa_ref, ba_ref, wx_ref, bx_ref, l_ref)
        a = jnp.exp(log_a)
        a2 = jnp.exp(2.0 * log_a)
        mult = jnp.sqrt(-_expm1_nonpos(2.0 * log_a))
        g, dg = _gelu_and_grad(yr_ref[...])
        hr_v = hr_ref[...]
        drec_v = drec_ref[...]
        dhr_s[...] = drec_v * g
        dyr_ref[...] = (drec_v * hr_v * dg).astype(BF16)
        a_s[...] = a

        def tile(jj, carry):
            r0 = pl.multiple_of((r_rows // SUBLANES - 1 - jj) * SUBLANES, SUBLANES)
            at = a_s[pl.ds(r0, SUBLANES), :]
            dt = dhr_s[pl.ds(r0, SUBLANES), :]
            out = [None] * SUBLANES
            for rr in range(SUBLANES - 1, -1, -1):
                lam = dt[rr:rr + 1] + carry
                out[rr] = lam
                carry = at[rr:rr + 1] * lam
            lam_s[pl.ds(r0, SUBLANES), :] = jnp.concatenate(out, axis=0)
            return carry

        carry_s[0:1, :] = lax.fori_loop(0, r_rows // SUBLANES, tile, carry_s[0:1, :])
        lam = lam_s[...]
        hprev = jnp.where(first, 0.0, hh_ref[...])
        hr_prev = pltpu.roll(jnp.concatenate([hprev, hr_v], axis=0), 1, 0)[HALO:]
        da = lam * hr_prev
        dxc = lam * mult * ig
        di = lam * mult * xc
        dmult = lam * ig * xc
        dlog_a = da * a - dmult * a2 / mult
        dr = dlog_a * (RG_C * ls)
        dls = jnp.sum(dlog_a * (RG_C * r), axis=0, keepdims=True)
        dga = dr * r * (1.0 - r)
        dgx = di * ig * (1.0 - ig)
        dgab = dga.astype(BF16)
        dgxb = dgx.astype(BF16)
        dxc = dxc + _dot_nt(dgab, wa_ref[...]) + _dot_nt(dgxb, wx_ref[...])
        dwa_ref[...] += _dot_tn(xcb, dgab)
        dwx_ref[...] += _dot_tn(xcb, dgxb)
        cat = jnp.concatenate([dxc, next_s[...]], axis=0)
        next_s[...] = dxc[0:HALO]
        dxr = cw_ref[CONV_WIDTH - 1:CONV_WIDTH, :] * dxc
        for k in range(CONV_WIDTH - 1):
            sh = CONV_WIDTH - 1 - k
            dxr = dxr + cw_ref[k:k + 1, :] * pltpu.roll(cat, r_rows + HALO - sh, 0)[:r_rows]
        dxr_ref[...] = dxr.astype(BF16)
        rows = [jnp.sum(dxc * taps[k], axis=0, keepdims=True) for k in range(CONV_WIDTH)]
        rows += [jnp.sum(dxc, axis=0, keepdims=True), jnp.sum(dga, axis=0, keepdims=True),
                 jnp.sum(dgx, axis=0, keepdims=True), dls * _sigmoid(-l_ref[...])]
        small_ref[...] += jnp.concatenate(rows, axis=0)

    def rev(i):
        return nc - 1 - i

    def halo(i):
        return jnp.maximum(rev(i) * hpc - 1, 0)

    blk = pl.BlockSpec((r_rows, w), lambda i: (rev(i), 0))
    vm = 40 * _nbytes((r_rows, w), F32) + 6 * _nbytes((w, w), F32)
    return _call(body, (proj, proj, proj, hr, hr, drec, conv_w, conv_b, wa, ba, wx, bx, lru),
                 name="rec_bwd", grid=(nc,),
                 in_specs=[pl.BlockSpec((r_rows, w), lambda i: (rev(i), xr_blk)),
                           pl.BlockSpec((HALO, w), lambda i: (halo(i), xr_blk)),
                           pl.BlockSpec((r_rows, w), lambda i: (rev(i), yr_blk)),
                           blk,
                           pl.BlockSpec((HALO, w), lambda i: (halo(i), 0)),
                           blk,
                           _full((CONV_WIDTH, w)), _full((1, w)), _full((w, w)), _full((1, w)),
                           _full((w, w)), _full((1, w)), _full((1, w))],
                 out_specs=[blk, blk, _full((w, w)), _full((w, w)), _full((SUBLANES, w))],
                 out_shape=[jax.ShapeDtypeStruct((tp, w), BF16)] * 2
                 + [jax.ShapeDtypeStruct((w, w), F32)] * 2 + [jax.ShapeDtypeStruct((SUBLANES, w), F32)],
                 scratch_shapes=[pltpu.VMEM((r_rows, w), F32)] * 3
                 + [pltpu.VMEM((SUBLANES, w), F32), pltpu.VMEM((HALO, w), F32)],
                 semantics=("arbitrary",), vmem_bytes=vm)


ROW_TARGET = 544


def _mixer_out(attn, rec, g_a, g_r, w_out, h, g_next):
    tp, d = h.shape
    aw, rw = attn.shape[1], rec.shape[1]
    kc = d // N_CHIPS
    tm = _divisor_tile(tp, 16, ROW_TARGET)

    def body(a_ref, r_ref, ga_ref, gr_ref, w_ref, h_ref, gn_ref, h1_ref, z_ref, mix_ref):
        a = a_ref[...]
        r = r_ref[...]
        mix = jnp.concatenate([a * _rstd(a) * ga_ref[...], r * _rstd(r) * gr_ref[...]], axis=1).astype(BF16)
        mix_ref[...] = mix
        h1 = h_ref[...]
        for j in range(N_CHIPS):
            h1 = h1 + _dot(mix[:, j * kc:(j + 1) * kc], w_ref[j])
        h1_ref[...] = h1
        z_ref[...] = (h1 * _rstd(h1) * gn_ref[...]).astype(BF16)

    row = lambda wd: pl.BlockSpec((tm, wd), lambda i: (i, 0))
    vm = 2 * _nbytes((d, d), BF16) + 12 * _nbytes((tm, d), F32)
    return _call(body, (attn, rec, g_a, g_r, w_out, h, g_next), name="mixer_out", grid=(tp // tm,),
                 in_specs=[row(aw), row(rw), _full((1, aw)), _full((1, rw)), _full(w_out.shape), row(d),
                           _full((1, d))],
                 out_specs=[row(d), row(d), row(d)],
                 out_shape=[jax.ShapeDtypeStruct((tp, d), F32), jax.ShapeDtypeStruct((tp, d), BF16),
                            jax.ShapeDtypeStruct((tp, d), BF16)],
                 semantics=("parallel",), vmem_bytes=vm)


def _mixer_bwd(dh_b, w_out, attn, rec, g_a, g_r):
    tp, d = dh_b.shape
    aw, rw = attn.shape[1], rec.shape[1]
    tm = _divisor_tile(tp, 16, ROW_TARGET)

    def body(dh_ref, w_ref, a_ref, r_ref, ga_ref, gr_ref, da_ref, dr_ref, dg_ref):
        @pl.when(pl.program_id(0) == 0)
        def _():
            dg_ref[...] = jnp.zeros_like(dg_ref)

        dh = dh_ref[...]
        dmix = jnp.concatenate([_dot_nt(dh, w_ref[j]) for j in range(N_CHIPS)], axis=1)
        da, dga = _rms_bwd(dmix[:, :aw], a_ref[...], ga_ref[...])
        dr, dgr = _rms_bwd(dmix[:, aw:], r_ref[...], gr_ref[...])
        da_ref[...] = da
        dr_ref[...] = dr
        dg_ref[...] += jnp.broadcast_to(jnp.concatenate([dga, dgr], axis=1), (SUBLANES, d))

    row = lambda wd: pl.BlockSpec((tm, wd), lambda i: (i, 0))
    vm = 2 * _nbytes((d, d), BF16) + 12 * _nbytes((tm, d), F32)
    return _call(body, (dh_b, w_out, attn, rec, g_a, g_r), name="mixer_bwd", grid=(tp // tm,),
                 in_specs=[row(d), _full(w_out.shape), row(aw), row(rw), _full((1, aw)), _full((1, rw))],
                 out_specs=[row(aw), row(rw), _full((SUBLANES, d))],
                 out_shape=[jax.ShapeDtypeStruct((tp, aw), F32), jax.ShapeDtypeStruct((tp, rw), F32),
                            jax.ShapeDtypeStruct((SUBLANES, d), F32)],
                 semantics=("arbitrary",), vmem_bytes=vm)


def _mlp_up(z, w_up):
    tp, d = z.shape
    fc = w_up.shape[2]
    ff = N_CHIPS * fc
    tn = _divisor_tile(fc, LANES, 512)
    per = fc // tn

    def body(z_ref, w_ref, act_ref, up_ref):
        up = _dot(z_ref[...], w_ref[...])
        r = jnp.maximum(up, 0.0)
        act_ref[...] = (r * r).astype(BF16)
        up_ref[...] = up.astype(BF16)

    col = pl.BlockSpec((tp, tn), lambda j: (0, j))
    vm = 2 * _nbytes((tp, d), BF16) + 2 * _nbytes((d, tn), BF16) + 8 * _nbytes((tp, tn), F32)
    return _call(body, (z, w_up), name="mlp_up", grid=(ff // tn,),
                 in_specs=[_full((tp, d)), pl.BlockSpec((None, d, tn), lambda j: (j // per, 0, j % per))],
                 out_specs=[col, col],
                 out_shape=[jax.ShapeDtypeStruct((tp, ff), BF16)] * 2,
                 semantics=("parallel",), vmem_bytes=vm)


def _mlp_down(act, w_down, h, g_next):
    tp, d = h.shape
    ff = act.shape[1]
    fc = ff // N_CHIPS
    tm = _divisor_tile(tp, 16, ROW_TARGET)

    def body(a_ref, w_ref, h_ref, gn_ref, h2_ref, z_ref):
        h2 = h_ref[...]
        for j in range(N_CHIPS):
            h2 = h2 + _dot(a_ref[:, j * fc:(j + 1) * fc], w_ref[j])
        h2_ref[...] = h2
        z_ref[...] = (h2 * _rstd(h2) * gn_ref[...]).astype(BF16)

    row = lambda wd: pl.BlockSpec((tm, wd), lambda i: (i, 0))
    vm = 2 * _nbytes((ff, d), BF16) + 2 * _nbytes((tm, ff), BF16) + 10 * _nbytes((tm, d), F32)
    return _call(body, (act, w_down, h, g_next), name="mlp_down", grid=(tp // tm,),
                 in_specs=[row(ff), _full(w_down.shape), row(d), _full((1, d))],
                 out_specs=[row(d), row(d)],
                 out_shape=[jax.ShapeDtypeStruct((tp, d), F32), jax.ShapeDtypeStruct((tp, d), BF16)],
                 semantics=("parallel",), vmem_bytes=vm)


def _loss_bwd(h, g, target, n_real):
    tp, d = h.shape
    tm = _divisor_tile(tp, 16, ROW_TARGET)

    def body(h_ref, g_ref, t_ref, dh_ref, dhb_ref, dg_ref, loss_ref):
        i = pl.program_id(0)

        @pl.when(i == 0)
        def _():
            dg_ref[...] = jnp.zeros_like(dg_ref)
            loss_ref[...] = jnp.zeros_like(loss_ref)

        x = h_ref[...]
        gv = g_ref[...]
        rowi = i * tm + lax.broadcasted_iota(jnp.int32, (tm, 1), 0)
        real = jnp.logical_and(rowi >= N_META, rowi < N_META + n_real)
        err = jnp.where(real, x * _rstd(x) * gv - t_ref[...], 0.0)
        loss_ref[...] += 0.5 * jnp.sum(jnp.mean(err * err, axis=-1, keepdims=True))
        dx, dgp = _rms_bwd(err * (1.0 / d), x, gv)
        dh_ref[...] = dx
        dhb_ref[...] = dx.astype(BF16)
        dg_ref[...] += jnp.broadcast_to(dgp, (SUBLANES, d))

    row = pl.BlockSpec((tm, d), lambda i: (i, 0))
    return _call(body, (h, g, target), name="loss_bwd", grid=(tp // tm,),
                 in_specs=[row, _full((1, d)), row],
                 out_specs=[row, row, _full((SUBLANES, d)), _full((SUBLANES, LANES))],
                 out_shape=[jax.ShapeDtypeStruct((tp, d), F32), jax.ShapeDtypeStruct((tp, d), BF16),
                            jax.ShapeDtypeStruct((SUBLANES, d), F32), jax.ShapeDtypeStruct((SUBLANES, LANES), F32)],
                 semantics=("arbitrary",), vmem_bytes=16 * _nbytes((tm, d), F32))


def _mlp_dup(dh_b, w_down, up):
    tp, d = dh_b.shape
    fc = w_down.shape[1]
    ff = N_CHIPS * fc
    tn = _divisor_tile(fc, LANES, 512)
    per = fc // tn

    def body(dh_ref, w_ref, up_ref, dup_ref):
        dact = _dot_nt(dh_ref[...], w_ref[...])
        dup_ref[...] = (dact * (2.0 * jnp.maximum(up_ref[...].astype(F32), 0.0))).astype(BF16)

    col = pl.BlockSpec((tp, tn), lambda j: (0, j))
    vm = 2 * _nbytes((tp, d), BF16) + 2 * _nbytes((tn, d), BF16) + 8 * _nbytes((tp, tn), F32)
    return _call(body, (dh_b, w_down, up), name="mlp_dup", grid=(ff // tn,),
                 in_specs=[_full((tp, d)), pl.BlockSpec((None, tn, d), lambda j: (j // per, j % per, 0)),
                           col],
                 out_specs=col, out_shape=jax.ShapeDtypeStruct((tp, ff), BF16),
                 semantics=("parallel",), vmem_bytes=vm)


def _grad_w(a, b, cols_to_chips=False):
    tp, k = a.shape
    n = b.shape[1]
    tk = _divisor_tile(k, LANES, 1024)
    nc = n // N_CHIPS if cols_to_chips else n
    tn = _divisor_tile(nc, LANES, 512)
    per = nc // tn

    def body(a_ref, b_ref, o_ref):
        o_ref[...] = _dot_tn(a_ref[...], b_ref[...]).astype(BF16)

    if cols_to_chips:
        out_spec = pl.BlockSpec((None, tk, tn), lambda i, j: (j // per, i, j % per))
        out_shape = jax.ShapeDtypeStruct((N_CHIPS, k, nc), BF16)
    else:
        out_spec = pl.BlockSpec((tk, tn), lambda i, j: (i, j))
        out_shape = jax.ShapeDtypeStruct((k, n), BF16)
    vm = 2 * _nbytes((tp, tk), BF16) + 2 * _nbytes((tp, tn), BF16) + 6 * _nbytes((tk, tn), F32) \
        + 2 * _nbytes((tp, tk), F32)
    return _call(body, (a, b), name="grad_w", grid=(k // tk, n // tn),
                 in_specs=[pl.BlockSpec((tp, tk), lambda i, j: (0, i)), pl.BlockSpec((tp, tn), lambda i, j: (0, j))],
                 out_specs=out_spec, out_shape=out_shape,
                 semantics=("parallel", "parallel"), vmem_bytes=vm)


def _dx_norm_bwd(pieces, w, w_spec, w_piece, h, g, dres):
    tp, d = h.shape
    tm = _divisor_tile(tp, 16, ROW_TARGET)
    n = len(pieces)

    def body(*refs):
        dy_refs = refs[:n]
        w_ref, h_ref, g_ref, dres_ref, dh_ref, dhb_ref, dg_ref = refs[n:]

        @pl.when(pl.program_id(0) == 0)
        def _():
            dg_ref[...] = jnp.zeros_like(dg_ref)

        dz = _dot_nt(dy_refs[0][...], w_piece(w_ref, 0))
        for i in range(1, n):
            dz = dz + _dot_nt(dy_refs[i][...], w_piece(w_ref, i))
        dx, dgp = _rms_bwd(dz, h_ref[...], g_ref[...])
        dh = dres_ref[...] + dx
        dh_ref[...] = dh
        dhb_ref[...] = dh.astype(BF16)
        dg_ref[...] += jnp.broadcast_to(dgp, (SUBLANES, d))

    row = lambda wd: pl.BlockSpec((tm, wd), lambda i: (i, 0))
    kk = sum(wd for _, _, wd in pieces)
    vm = 2 * _nbytes((d, kk), BF16) + 2 * _nbytes((tm, kk), BF16) + 14 * _nbytes((tm, d), F32)
    piece_specs = [pl.BlockSpec((tm, wd), functools.partial(lambda i, cb: (i, cb), cb=cb)) for _, cb, wd in pieces]
    return _call(body, tuple(a for a, _, _ in pieces) + (w, h, g, dres), name="dx_norm_bwd", grid=(tp // tm,),
                 in_specs=piece_specs + [w_spec, row(d), _full((1, d)), row(d)],
                 out_specs=[row(d), row(d), _full((SUBLANES, d))],
                 out_shape=[jax.ShapeDtypeStruct((tp, d), F32), jax.ShapeDtypeStruct((tp, d), BF16),
                            jax.ShapeDtypeStruct((SUBLANES, d), F32)],
                 semantics=("arbitrary",), vmem_bytes=vm)


def _block_diag(wg):
    nb, b, _ = wg.shape
    eye = jnp.eye(nb, dtype=wg.dtype)
    return (eye[:, None, :, None] * wg[:, :, None, :]).reshape(nb * b, nb * b)


def _diag_blocks(dense, nb):
    b = dense.shape[0] // nb
    d4 = dense.reshape(nb, b, nb, b)
    return jnp.stack([d4[i, :, i, :] for i in range(nb)])


def _row(v):
    return v.reshape(1, -1)


def _forward_layer(l, h, z, p, big, g_next):
    d = h.shape[1]
    att_w = d // 2
    rec_w = d - att_w
    nh = att_w // HEAD_DIM
    xr_blk = 3 * att_w // rec_w
    wa_d = _block_diag(p["w_gate_a"][l]).astype(BF16)
    wx_d = _block_diag(p["w_gate_x"][l]).astype(BF16)
    b_f_pad = jnp.zeros((1, LANES), F32).at[0, :nh].set(p["b_f"][l])
    proj, qkv = _proj(z, big["w_in_big"], att_w)
    c, c_t = _fgate_fwd(proj, b_f_pad, nh)
    attn, lse_b = _attn_fwd(qkv, c, c_t, nh)
    hr, rec = _rec_fwd(proj, xr_blk, xr_blk + 1, rec_w, p["conv_w"][l], _row(p["conv_b"][l]), wa_d,
                       _row(p["b_gate_a"][l]), wx_d, _row(p["b_gate_x"][l]), _row(p["lru_L"][l]))
    h1, z2, mix = _mixer_out(attn, rec, _row(p["attn_out_g"][l]), _row(p["rec_out_g"][l]),
                             big["w_out"], h, _row(p["mlp_norm_g"][l]))
    act, up = _mlp_up(z2, big["w_up"])
    h2, z_next = _mlp_down(act, big["w_down"], h1, _row(g_next))
    saved = dict(h0=h, z1=z, proj=proj, qkv=qkv, c=c, c_t=c_t, attn=attn, lse_b=lse_b, hr=hr, rec=rec, h1=h1,
                 z2=z2, mix=mix, act=act, up=up, wa_d=wa_d, wx_d=wx_d, b_f_pad=b_f_pad, big=big)
    return h2, z_next, saved


def _backward_mlp(l, dh, dh_b, sv, p):
    w_up, w_down = sv["big"]["w_up"], sv["big"]["w_down"]
    fc = w_up.shape[2]
    dup = _mlp_dup(dh_b, w_down, sv["up"])
    g_down = _grad_w(sv["act"], dh_b)
    g_up = _grad_w(sv["z2"], dup, cols_to_chips=True)
    dh, dh_b, dg2 = _dx_norm_bwd([(dup, j, fc) for j in range(N_CHIPS)], w_up, _full(w_up.shape),
                                 lambda w_ref, j: w_ref[j], sv["h1"], _row(p["mlp_norm_g"][l]), dh)
    big = dict(w_down=g_down.reshape((N_CHIPS, -1) + g_down.shape[1:]), w_up=g_up)
    return dh, dh_b, big, dict(mlp_norm_g=dg2[0])


def _backward_mixer(l, dh, dh_b, sv, p):
    d = dh.shape[1]
    att_w = d // 2
    rec_w = d - att_w
    nh = att_w // HEAD_DIM
    xr_blk = 3 * att_w // rec_w
    small = {}
    g_out = _grad_w(sv["mix"], dh_b)
    dattn, drec, dg_mix = _mixer_bwd(dh_b, sv["big"]["w_out"], sv["attn"], sv["rec"],
                                     _row(p["attn_out_g"][l]), _row(p["rec_out_g"][l]))
    small["attn_out_g"] = dg_mix[0, :att_w]
    small["rec_out_g"] = dg_mix[0, att_w:]
    dxr, dyr, dwa, dwx, sm = _rec_bwd(
        sv["proj"], xr_blk, xr_blk + 1, rec_w, sv["hr"], drec, p["conv_w"][l], _row(p["conv_b"][l]), sv["wa_d"],
        _row(p["b_gate_a"][l]), sv["wx_d"], _row(p["b_gate_x"][l]), _row(p["lru_L"][l]))
    small.update(conv_w=sm[:CONV_WIDTH], conv_b=sm[4], b_gate_a=sm[5], b_gate_x=sm[6], lru_L=sm[7],
                 w_gate_a=_diag_blocks(dwa, N_REC_BLOCKS), w_gate_x=_diag_blocks(dwx, N_REC_BLOCKS))
    dq, dk, dv, dct = _attn_bwd(sv["qkv"], sv["c"], sv["c_t"], sv["lse_b"], dattn, nh)
    df, db_f = _fgate_bwd(sv["proj"], sv["b_f_pad"], dct)
    small["b_f"] = db_f[0, :nh]
    pieces = [dq, dk, dv, dxr, dyr, df]
    offs = [0, att_w, 2 * att_w, 3 * att_w, 3 * att_w + rec_w, 3 * att_w + 2 * rec_w]
    gq, gk, gv, gxr, gyr, gf = [_grad_w(sv["z1"], pc) for pc in pieces]
    g_in = jnp.concatenate([gq, gk, gv, gf[:, :nh], gxr, gyr], axis=1)
    w_big = sv["big"]["w_in_big"]
    widths = [pc.shape[1] for pc in pieces]
    dh, dh_b, dg1 = _dx_norm_bwd(
        [(pc, 0, wd) for pc, wd in zip(pieces, widths)], w_big, _full(w_big.shape),
        lambda w_ref, i: w_ref[:, offs[i]:offs[i] + widths[i]], sv["h0"], _row(p["attn_norm_g"][l]), dh)
    small["attn_norm_g"] = dg1[0]
    big = dict(w_in=jnp.moveaxis(g_in.reshape(d, N_CHIPS, -1), 1, 0),
               w_out=g_out.reshape((N_CHIPS, -1) + g_out.shape[1:]))
    return dh, dh_b, big, small


def _pack_w_in(w_in, att_w, rec_w, nh):
    qkv = w_in[..., :3 * att_w]
    f = w_in[..., 3 * att_w:3 * att_w + nh]
    xy = w_in[..., 3 * att_w + nh:]
    fpad = jnp.zeros(f.shape[:-1] + (LANES - nh,), w_in.dtype)
    return jnp.concatenate([qkv, xy, f, fpad], axis=-1)


ANY = pl.BlockSpec(memory_space=pl.ANY)


def _coords():
    return lax.axis_index("x"), lax.axis_index("y"), lax.axis_index("c")


def _other_chips(x, y):
    return [(1 - x, y), (x, 1 - y), (1 - x, 1 - y)]


def _remote(src, dst, send_sems, recv_sems, k, to):
    return pltpu.make_async_remote_copy(src_ref=src, dst_ref=dst, send_sem=send_sems.at[k],
                                        recv_sem=recv_sems.at[k], device_id=to, device_id_type=MESH)


def _all_gather_chips(shards):
    n = len(shards)
    per = 6

    def body(*refs):
        ins, outs = refs[:n], refs[n:2 * n]
        send_sems, recv_sems, local_sems = refs[2 * n:]
        x, y, c = _coords()
        me = 2 * x + y
        sibling = (x, y, 1 - c)
        chips = _other_chips(x, y)
        local = [pltpu.make_async_copy(ins[t], outs[t].at[me], local_sems.at[t]) for t in range(n)]
        for cp in local:
            cp.start()
        sends = []
        for t in range(n):
            for j, (px, py) in enumerate(chips):
                cp = _remote(ins[t].at[c], outs[t].at[me, c], send_sems, recv_sems, per * t + j, (px, py, c))
                cp.start()
                sends.append(cp)
        for t in range(n):
            for j, (px, py) in enumerate(chips):
                landed = outs[t].at[2 * px + py, c]
                _remote(landed, landed, send_sems, recv_sems, per * t + j, (px, py, c)).wait_recv()
                cp = _remote(landed, landed, send_sems, recv_sems, per * t + 3 + j, sibling)
                cp.start()
                sends.append(cp)
        for t in range(n):
            for j, (px, py) in enumerate(chips):
                passed = outs[t].at[2 * px + py, 1 - c]
                _remote(passed, passed, send_sems, recv_sems, per * t + 3 + j, sibling).wait_recv()
        for cp in sends:
            cp.wait_send()
        for cp in local:
            cp.wait()

    return _call(body, tuple(shards), name="all_gather_chips",
                 in_specs=[ANY] * n, out_specs=[ANY] * n,
                 out_shape=[jax.ShapeDtypeStruct((N_CHIPS,) + s.shape, s.dtype) for s in shards],
                 scratch_shapes=[pltpu.SemaphoreType.DMA((per * n,)), pltpu.SemaphoreType.DMA((per * n,)),
                                 pltpu.SemaphoreType.DMA((n,))])


HBM = pl.BlockSpec(memory_space=pltpu.HBM)
SEM = pl.BlockSpec(memory_space=pltpu.SEMAPHORE)
DATAFLOW = pltpu.SideEffectType.DATAFLOW_SIDE_EFFECTING


def _in_hbm(a):
    return pltpu.with_memory_space_constraint(a, pltpu.HBM)


def _push_start(srcs, lands, own_slab, name):
    n = len(srcs)
    same = own_slab and all(s is ld for s, ld in zip(srcs, lands))
    n_in = n if same else 2 * n

    def body(*refs):
        src_refs = refs[:n]
        land_refs = src_refs if same else refs[n:2 * n]
        send_sems, recv_sems = refs[n_in], refs[n_in + 1]
        token = refs[-1]
        x, y, c = _coords()
        me = 2 * x + y
        for t in range(n):
            for px, py in _other_chips(x, y):
                src = src_refs[t].at[me if own_slab else 2 * px + py]
                _remote(src, land_refs[t].at[me], send_sems, recv_sems, t, (px, py, c)).start()
        token[...] = jnp.zeros_like(token)

    operands = tuple(srcs) if same else tuple(srcs) + tuple(lands)
    res = _call(
        body, [_in_hbm(a) for a in operands], name=name,
        out_shape=(pltpu.SemaphoreType.DMA((n,)), pltpu.SemaphoreType.DMA((n,)))
        + tuple(pltpu.HBM(a.shape, a.dtype) for a in operands) + (jax.ShapeDtypeStruct((SUBLANES, LANES), F32),),
        in_specs=[HBM] * n_in, out_specs=(SEM, SEM) + (HBM,) * n_in + (pl.BlockSpec(memory_space=pltpu.VMEM),),
        input_output_aliases={i: 2 + i for i in range(n_in)}, side_effects=DATAFLOW)
    send_sems, recv_sems, token = res[0], res[1], res[-1]
    srcs_thru = res[2:2 + n]
    lands_thru = srcs_thru if same else res[2 + n:2 + 2 * n]
    return send_sems, recv_sems, srcs_thru, lands_thru, token


def _push_wait(send_sems, recv_sems, srcs, lands, after, name):
    n = len(lands)
    same = all(s is ld for s, ld in zip(srcs, lands))
    n_in = n if same else 2 * n

    def body(*refs):
        land_refs = refs[:n] if same else refs[n:2 * n]
        send_sems, recv_sems = refs[n_in], refs[n_in + 1]
        x, y, c = _coords()
        for t in range(n):
            three = land_refs[t].at[pl.ds(0, N_CHIPS - 1)]
            arrivals = _remote(three, three, send_sems, recv_sems, t, (x, y, c))
            arrivals.wait_send()
            arrivals.wait_recv()

    operands = tuple(lands) if same else tuple(srcs) + tuple(lands)
    res = _call(
        body, operands + (send_sems, recv_sems, after), name=name,
        out_shape=tuple(pltpu.HBM(a.shape, a.dtype) for a in operands),
        in_specs=[HBM] * n_in + [SEM, SEM, ANY], out_specs=(HBM,) * n_in,
        input_output_aliases={i: i for i in range(n_in)}, side_effects=DATAFLOW)
    return list(res[n_in - n:])


def _place_own(parts):
    n = len(parts)

    def body(*refs):
        ins, outs, sems = refs[:n], refs[n:2 * n], refs[2 * n]
        x, y, _ = _coords()
        me = 2 * x + y
        cps = [pltpu.make_async_copy(ins[t].at[me], outs[t].at[me], sems.at[t]) for t in range(n)]
        for cp in cps:
            cp.start()
        for cp in cps:
            cp.wait()

    return _call(body, tuple(parts), name="place_own", in_specs=[ANY] * n, out_specs=[ANY] * n,
                 out_shape=[jax.ShapeDtypeStruct(s.shape, s.dtype) for s in parts],
                 scratch_shapes=[pltpu.SemaphoreType.DMA((n,))])


def _cast_to_slab(w, l, chip):
    _, rows, cols = w.shape
    br = _divisor_tile(rows, 16, ELEM_ROWS)

    def body(chip_ref, w_ref, o_ref):
        o_ref[...] = w_ref[...].astype(BF16)

    spec = pltpu.PrefetchScalarGridSpec(
        num_scalar_prefetch=1, grid=(rows // br,),
        in_specs=[pl.BlockSpec((None, br, cols), lambda i, ch: (l, i, 0))],
        out_specs=pl.BlockSpec((None, br, cols), lambda i, ch: (ch[0], i, 0)))
    return _call(body, (chip, w), name="cast_to_slab", grid_spec=spec,
                 out_shape=jax.ShapeDtypeStruct((N_CHIPS, rows, cols), BF16), semantics=("parallel",))


def _swap_with_sibling(arrs):
    n = len(arrs)

    def body(*refs):
        ins, outs = refs[:n], refs[n:2 * n]
        send_sems, recv_sems = refs[2 * n:]
        x, y, c = _coords()
        cps = [_remote(ins[t], outs[t], send_sems, recv_sems, t, (x, y, 1 - c)) for t in range(n)]
        for cp in cps:
            cp.start()
        for cp in cps:
            cp.wait_recv()
        for cp in cps:
            cp.wait_send()

    return _call(body, tuple(arrs), name="swap_with_sibling",
                 in_specs=[ANY] * n, out_specs=[ANY] * n,
                 out_shape=[jax.ShapeDtypeStruct(s.shape, s.dtype) for s in arrs],
                 scratch_shapes=[pltpu.SemaphoreType.DMA((n,)), pltpu.SemaphoreType.DMA((n,))])


def _all_gather_devices(buf):
    per = N_DEV - 1

    def body(in_ref, out_ref, send_sems, recv_sems, local_sem):
        x, y, c = _coords()
        me = 4 * x + 2 * y + c
        local = pltpu.make_async_copy(in_ref, out_ref.at[me], local_sem)
        local.start()
        peers = []
        for k in range(1, N_DEV):
            fx, fy, fc = (k >> 2) & 1, (k >> 1) & 1, k & 1
            peers.append((x ^ fx, y ^ fy, c ^ fc))
        sends = [_remote(in_ref, out_ref.at[me], send_sems, recv_sems, k, peer) for k, peer in enumerate(peers)]
        for cp in sends:
            cp.start()
        for k, (px, py, pc) in enumerate(peers):
            landed = out_ref.at[4 * px + 2 * py + pc]
            _remote(landed, landed, send_sems, recv_sems, k, (px, py, pc)).wait_recv()
        for cp in sends:
            cp.wait_send()
        local.wait()

    return _call(body, (buf,), name="all_gather_devices", in_specs=[ANY], out_specs=ANY,
                 out_shape=jax.ShapeDtypeStruct((N_DEV,) + buf.shape, buf.dtype),
                 scratch_shapes=[pltpu.SemaphoreType.DMA((per,)), pltpu.SemaphoreType.DMA((per,)),
                                 pltpu.SemaphoreType.DMA(())])


ELEM_ROWS = 256


def _sum_slabs(r):
    n, rows, cols = r.shape
    br = _divisor_tile(rows, 16, ELEM_ROWS)

    def body(r_ref, o_ref):
        acc = r_ref[0].astype(F32)
        for j in range(1, n):
            acc = acc + r_ref[j].astype(F32)
        o_ref[...] = acc

    return _call(body, (r,), name="sum_slabs", grid=(rows // br,),
                 in_specs=[pl.BlockSpec((n, br, cols), lambda i: (0, i, 0))],
                 out_specs=pl.BlockSpec((br, cols), lambda i: (i, 0)),
                 out_shape=jax.ShapeDtypeStruct((rows, cols), F32), semantics=("parallel",))


def _adamw_math(w, g, m, v):
    c1 = 1.0 - ADAM_B1 ** ADAM_STEP
    c2 = 1.0 - ADAM_B2 ** ADAM_STEP
    nm = ADAM_B1 * m + (1.0 - ADAM_B1) * g
    nv = ADAM_B2 * v + (1.0 - ADAM_B2) * (g * g)
    delta = -ADAM_LR * ((nm / c1) / (jnp.sqrt(nv / c2) + ADAM_EPS) + ADAM_WD * w)
    return delta, nm, nv


def _adamw(w, g, m, v):
    rows, cols = w.shape
    br = _divisor_tile(rows, 8, ELEM_ROWS)

    def body(w_ref, g_ref, m_ref, v_ref, d_ref, nm_ref, nv_ref):
        d_ref[...], nm_ref[...], nv_ref[...] = _adamw_math(w_ref[...], g_ref[...], m_ref[...], v_ref[...])

    blk = pl.BlockSpec((br, cols), lambda i: (i, 0))
    return _call(body, (w, g, m, v), name="adamw", grid=(rows // br,),
                 in_specs=[blk] * 4, out_specs=[blk] * 3,
                 out_shape=[jax.ShapeDtypeStruct((rows, cols), F32)] * 3, semantics=("parallel",))


def _adamw_layer(w, m, v, l, g_mine, g_theirs, prev):
    _, rows, cols = w.shape
    br = _divisor_tile(rows, 8, ELEM_ROWS)

    def body(w_ref, m_ref, v_ref, ga_ref, gb_ref, *rest):
        g_ref, d_ref, nm_ref, nv_ref = rest[4:]
        g = ga_ref[...] + gb_ref[...]
        g_ref[...] = g
        d_ref[...], nm_ref[...], nv_ref[...] = _adamw_math(w_ref[...], g, m_ref[...], v_ref[...])

    slot = pl.BlockSpec((None, br, cols), lambda i: (l, i, 0))
    blk = pl.BlockSpec((br, cols), lambda i: (i, 0))
    return _call(body, (w, m, v, g_mine, g_theirs) + tuple(prev), name="adamw_layer", grid=(rows // br,),
                 in_specs=[slot] * 3 + [blk] * 2 + [ANY] * 4, out_specs=[slot] * 4,
                 out_shape=[jax.ShapeDtypeStruct(w.shape, F32)] * 4,
                 input_output_aliases={5: 0, 6: 1, 7: 2, 8: 3}, semantics=("parallel",))


BIG = ("w_in", "w_out", "w_up", "w_down")
WEIGHTS = ("meta", "attn_norm_g", "w_in", "b_f", "conv_w", "conv_b", "w_gate_a", "b_gate_a", "w_gate_x",
           "b_gate_x", "lru_L", "attn_out_g", "rec_out_g", "w_out", "mlp_norm_g", "w_up", "w_down", "final_g")
SMALL = tuple(k for k in WEIGHTS if k not in BIG)
COL_SHARDED_SMALL = ("meta", "conv_w")
PACK_UNIT = 16 * LANES


def _pack(arrs):
    flat = jnp.concatenate([a.reshape(-1) for a in arrs])
    total = -(-flat.shape[0] // PACK_UNIT) * PACK_UNIT
    return jnp.pad(flat, (0, total - flat.shape[0])).reshape(-1, LANES)


def _unpack(buf, shapes):
    flat = buf.reshape(-1)
    out, off = [], 0
    for s in shapes:
        size = math.prod(s)
        out.append(flat[off:off + size].reshape(s))
        off += size
    return out


def _halves(a):
    return a.reshape((2, a.shape[0] // 2) + a.shape[1:])


def _cols_from_chips(g):
    return jnp.moveaxis(g, 0, -2).reshape(g.shape[1:-1] + (N_CHIPS * g.shape[-1],))


def kernel(x, meta, attn_norm_g, w_in, b_f, conv_w, conv_b, w_gate_a, b_gate_a, w_gate_x, b_gate_x, lru_L, attn_out_g, rec_out_g, w_out, mlp_norm_g, w_up, w_down, final_g, loss_target, m_meta, m_attn_norm_g, m_w_in, m_b_f, m_conv_w, m_conv_b, m_w_gate_a, m_b_gate_a, m_w_gate_x, m_b_gate_x, m_lru_L, m_attn_out_g, m_rec_out_g, m_w_out, m_mlp_norm_g, m_w_up, m_w_down, m_final_g, v_meta, v_attn_norm_g, v_w_in, v_b_f, v_conv_w, v_conv_b, v_w_gate_a, v_b_gate_a, v_w_gate_x, v_b_gate_x, v_lru_L, v_attn_out_g, v_rec_out_g, v_w_out, v_mlp_norm_g, v_w_up, v_w_down, v_final_g):
    w = dict(meta=meta, attn_norm_g=attn_norm_g, w_in=w_in, b_f=b_f, conv_w=conv_w, conv_b=conv_b,
             w_gate_a=w_gate_a, b_gate_a=b_gate_a, w_gate_x=w_gate_x, b_gate_x=b_gate_x, lru_L=lru_L,
             attn_out_g=attn_out_g, rec_out_g=rec_out_g, w_out=w_out, mlp_norm_g=mlp_norm_g, w_up=w_up,
             w_down=w_down, final_g=final_g)
    m = dict(meta=m_meta, attn_norm_g=m_attn_norm_g, w_in=m_w_in, b_f=m_b_f, conv_w=m_conv_w, conv_b=m_conv_b,
             w_gate_a=m_w_gate_a, b_gate_a=m_b_gate_a, w_gate_x=m_w_gate_x, b_gate_x=m_b_gate_x, lru_L=m_lru_L,
             attn_out_g=m_attn_out_g, rec_out_g=m_rec_out_g, w_out=m_w_out, mlp_norm_g=m_mlp_norm_g,
             w_up=m_w_up, w_down=m_w_down, final_g=m_final_g)
    v = dict(meta=v_meta, attn_norm_g=v_attn_norm_g, w_in=v_w_in, b_f=v_b_f, conv_w=v_conv_w, conv_b=v_conv_b,
             w_gate_a=v_w_gate_a, b_gate_a=v_b_gate_a, w_gate_x=v_w_gate_x, b_gate_x=v_b_gate_x, lru_L=v_lru_L,
             attn_out_g=v_attn_out_g, rec_out_g=v_rec_out_g, w_out=v_w_out, mlp_norm_g=v_mlp_norm_g,
             w_up=v_w_up, w_down=v_w_down, final_g=v_final_g)
    s_len, d = x.shape[1], x.shape[2]
    depth = w_in.shape[0]
    att_w = d // 2
    rec_w = d - att_w
    nh = att_w // HEAD_DIM
    chip = 2 * lax.axis_index("x") + lax.axis_index("y")

    g_conv, g_meta = [g.reshape((N_CHIPS, g.shape[1] * g.shape[2]) + g.shape[3:])
                      for g in _all_gather_chips([_halves(w["conv_w"]), _halves(w["meta"])])]
    p = dict(w)
    p["conv_w"] = _cols_from_chips(g_conv)
    meta_full = jnp.moveaxis(g_meta, 0, 1).reshape(N_META, d)

    chip1 = chip.reshape(1).astype(jnp.int32)
    pushes, tokens = [], []
    for l in range(depth):
        slabs = [_cast_to_slab(w[k], l, chip1) for k in BIG]
        send_sems, recv_sems, _, lands, token = _push_start(slabs, slabs, True, f"weights_start_{l}")
        pushes.append((send_sems, recv_sems, lands))
        tokens.append(token[0, 0])

    t_len = N_META + s_len
    pad = -t_len % SEQ_TILE
    h = jnp.concatenate([meta_full, x[0], jnp.zeros((pad, d), F32)], axis=0)
    tgt = jnp.concatenate([jnp.zeros((N_META, d), F32), loss_target[0], jnp.zeros((pad, d), F32)], axis=0)
    z = _rms_fwd(h, _row(p["attn_norm_g"][0] + sum(tokens)))
    saved = []
    for l in range(depth):
        send_sems, recv_sems, lands = pushes[l]
        g_in, g_out, g_up, g_down = _push_wait(send_sems, recv_sems, lands, lands, h, f"weights_wait_{l}")
        big = dict(w_in_big=_pack_w_in(jnp.moveaxis(g_in, 0, 1).reshape(d, -1), att_w, rec_w, nh),
                   w_out=g_out, w_up=g_up, w_down=g_down)
        g_next = p["attn_norm_g"][l + 1] if l + 1 < depth else p["final_g"]
        h, z, sv = _forward_layer(l, h, z, p, big, g_next)
        saved.append(sv)
    dh, dh_b, dg_final, loss_part = _loss_bwd(h, _row(p["final_g"]), tgt, s_len)

    small = {k: [None] * depth for k in SMALL if k not in ("meta", "final_g")}
    pushes = [None] * depth
    for l in reversed(range(depth)):
        dh, dh_b, big_mlp, sm_mlp = _backward_mlp(l, dh, dh_b, saved[l], p)
        dh, dh_b, big_mix, sm_mix = _backward_mixer(l, dh, dh_b, saved[l], p)
        parts = [{**big_mlp, **big_mix}[k] for k in BIG]
        pushes[l] = _push_start(parts, _place_own(parts), False, f"grads_start_{l}")
        for k, val in {**sm_mlp, **sm_mix}.items():
            small[k][l] = val
    grads = {k: jnp.stack(val) for k, val in small.items()}
    grads["final_g"] = dg_final[0]
    grads["meta"] = dh[:N_META]
    dx = dh[N_META:t_len]

    outs = {k: [lax.empty(w[k].shape, F32) for _ in range(4)] for k in BIG}
    for l in reversed(range(depth)):
        send_sems, recv_sems, parts, lands, _ = pushes[l]
        landed = _push_wait(send_sems, recv_sems, parts, lands, dh, f"grads_wait_{l}")
        sums = [_sum_slabs(r) for r in landed]
        theirs = _swap_with_sibling(sums)
        for k, mine, other in zip(BIG, sums, theirs):
            outs[k] = _adamw_layer(w[k], m[k], v[k], l, mine, other, outs[k])
    out_g, out_d, out_m, out_v = [{k: outs[k][i] for k in BIG} for i in range(4)]

    full_shapes = [grads[k].shape for k in SMALL] + [(1,)]
    packed = _pack([grads[k].astype(F32) for k in SMALL] + [loss_part[0, :1]])
    total = _sum_slabs(_all_gather_devices(packed))
    small_g = dict(zip(SMALL + ("loss",), _unpack(total, full_shapes)))
    for k in COL_SHARDED_SMALL:
        n = w[k].shape[-1]
        small_g[k] = lax.dynamic_slice_in_dim(small_g[k], chip * n, n, axis=small_g[k].ndim - 1)
    local_shapes = [w[k].shape for k in SMALL]
    res = _adamw(_pack([w[k] for k in SMALL]), _pack([small_g[k] for k in SMALL]),
                 _pack([m[k] for k in SMALL]), _pack([v[k] for k in SMALL]))
    out_g.update({k: small_g[k] for k in SMALL})
    for dst, buf in zip((out_d, out_m, out_v), res):
        dst.update(zip(SMALL, _unpack(buf, local_shapes)))

    return (small_g["loss"].reshape(()), dx[None],
            *[out_g[k] for k in WEIGHTS], *[out_d[k] for k in WEIGHTS],
            *[out_m[k] for k in WEIGHTS], *[out_v[k] for k in WEIGHTS])
```

```python
import functools
import math

import jax
import jax.numpy as jnp
from jax import lax
from jax.experimental import pallas as pl
from jax.experimental.pallas import tpu as pltpu

F32 = jnp.float32
BF16 = jnp.bfloat16

N_META = 16
HEAD_DIM = 64
N_REC_BLOCKS = 8
CONV_WIDTH = 4
RG_C = 8.0
NORM_EPS = 1e-6
ADAM_LR = 0.001
ADAM_B1 = 0.9
ADAM_B2 = 0.999
ADAM_EPS = 1e-08
ADAM_WD = 0.01
ADAM_STEP = 10

LANES = 128
SUBLANES = 8
SEQ_TILE = 128
VMEM_CAP = 60 * 2**20
VMEM_SLACK = 6 * 2**20
NEG_BIG = -1e30
N_CHIPS = 4
N_DEV = 8
MESH = pl.DeviceIdType.MESH


def _nbytes(shape, dtype):
    return math.prod(shape) * jnp.dtype(dtype).itemsize


def _call(body, args, *, name, out_shape, grid=(), in_specs=None, out_specs=None, scratch_shapes=(),
          grid_spec=None, semantics=None, vmem_bytes=None, side_effects=None, **kw):
    cp = {}
    if semantics is not None:
        cp["dimension_semantics"] = semantics
    if vmem_bytes is not None:
        cp["vmem_limit_bytes"] = int(min(VMEM_CAP, vmem_bytes + VMEM_SLACK))
    if side_effects is not None:
        cp["has_side_effects"] = side_effects
    if grid_spec is not None:
        kw["grid_spec"] = grid_spec
    else:
        kw.update(grid=grid, in_specs=in_specs, out_specs=out_specs, scratch_shapes=scratch_shapes)
    fn = pl.pallas_call(
        body, name=name, out_shape=out_shape,
        compiler_params=pltpu.CompilerParams(**cp), **kw)
    return fn(*args)


def _divisor_tile(n, unit, target):
    best = None
    for t in range(unit, min(n, target) + 1, unit):
        if n % t == 0:
            best = t
    return n if best is None else best


def _sigmoid(x):
    return 1.0 / (1.0 + jnp.exp(-x))


def _log1p_unit(e):
    series = e * (1.0 - e * (0.5 - e * (1.0 / 3.0)))
    return jnp.where(e < 1e-2, series, jnp.log(1.0 + e))


def _log_sigmoid(x):
    return jnp.minimum(x, 0.0) - _log1p_unit(jnp.exp(-jnp.abs(x)))


def _expm1_nonpos(x):
    small = x * (1.0 + x * (1.0 / 2 + x * (1.0 / 6 + x * (1.0 / 24 + x * (1.0 / 120 + x * (1.0 / 720))))))
    return jnp.where(x > -0.25, small, jnp.exp(x) - 1.0)


_GELU_K = math.sqrt(2.0 / math.pi)
_GELU_C = 0.044715


def _gelu_and_grad(y):
    th = jnp.tanh(_GELU_K * (y + _GELU_C * y * y * y))
    g = 0.5 * y * (1.0 + th)
    dg = 0.5 * (1.0 + th) + 0.5 * y * (1.0 - th * th) * _GELU_K * (1.0 + 3.0 * _GELU_C * y * y)
    return g, dg


def _rstd(x):
    return lax.rsqrt(jnp.mean(x * x, axis=-1, keepdims=True) + NORM_EPS)


def _rms_bwd(dz, x, g):
    rs = _rstd(x)
    xh = x * rs
    dgp = jnp.sum(dz * xh, axis=0, keepdims=True)
    dxh = dz * g
    dx = rs * (dxh - xh * jnp.mean(dxh * xh, axis=-1, keepdims=True))
    return dx, dgp


def _dot(a, b):
    return jnp.dot(a, b, preferred_element_type=F32)


def _dot_nt(a, b):
    return lax.dot_general(a, b, (((1,), (1,)), ((), ())), preferred_element_type=F32)


def _dot_tn(a, b):
    return lax.dot_general(a, b, (((0,), (0,)), ((), ())), preferred_element_type=F32)


def _full(shape):
    nd = len(shape)
    return pl.BlockSpec(shape, lambda *_: (0,) * nd)


def _rms_fwd(h, g):
    tp, d = h.shape
    tm = _divisor_tile(tp, 16, 544)

    def body(h_ref, g_ref, z_ref):
        x = h_ref[...]
        z_ref[...] = (x * _rstd(x) * g_ref[...]).astype(BF16)

    return _call(body, (h, g), name="rms_fwd", grid=(tp // tm,),
                 in_specs=[pl.BlockSpec((tm, d), lambda i: (i, 0)), _full((1, d))],
                 out_specs=pl.BlockSpec((tm, d), lambda i: (i, 0)),
                 out_shape=jax.ShapeDtypeStruct((tp, d), BF16), semantics=("parallel",))


def _proj(z, w_big, att_w):
    tp, d = z.shape
    nb = w_big.shape[1]
    tn = _divisor_tile(nb, LANES, 512)
    assert (3 * att_w) % tn == 0
    n_qkv = 3 * att_w // tn
    scale = 1.0 / math.sqrt(HEAD_DIM)

    def body(z_ref, w_ref, p_ref, qkv_ref):
        j = pl.program_id(0)
        acc = _dot(z_ref[...], w_ref[...])
        p_ref[...] = acc

        @pl.when(j < n_qkv)
        def _():
            col = j * tn + lax.broadcasted_iota(jnp.int32, (1, tn), 1)
            qkv_ref[...] = (acc * jnp.where(col < att_w, scale, 1.0)).astype(BF16)

    vm = 2 * (_nbytes((tp, d), BF16) + _nbytes((d, tn), BF16) + _nbytes((tp, tn), F32) * 2)
    return _call(body, (z, w_big), name="proj", grid=(nb // tn,),
                 in_specs=[_full((tp, d)), pl.BlockSpec((d, tn), lambda j: (0, j))],
                 out_specs=[pl.BlockSpec((tp, tn), lambda j: (0, j)),
                            pl.BlockSpec((tp, tn), lambda j: (0, jnp.minimum(j, n_qkv - 1)))],
                 out_shape=[jax.ShapeDtypeStruct((tp, nb), F32),
                            jax.ShapeDtypeStruct((tp, 3 * att_w), BF16)],
                 semantics=("arbitrary",), vmem_bytes=vm)


def _tile_cumsum(x, row, reverse=False):
    for s in (1, 2, 4):
        if reverse:
            x = x + jnp.where(row < SUBLANES - s, pltpu.roll(x, SUBLANES - s, 0), 0.0)
        else:
            x = x + jnp.where(row >= s, pltpu.roll(x, s, 0), 0.0)
    return x


def _fgate_fwd(proj, b_f_pad, nh):
    tp, nb = proj.shape
    fblk = nb // LANES - 1

    def body(f_ref, b_ref, c_ref, ct_ref):
        b = b_ref[...]
        row = lax.broadcasted_iota(jnp.int32, (SUBLANES, LANES), 0)

        def step(i, carry):
            r0 = pl.multiple_of(i * SUBLANES, SUBLANES)
            lf = _log_sigmoid(f_ref[pl.ds(r0, SUBLANES), :] + b)
            x = _tile_cumsum(lf, row) + carry
            c_ref[pl.ds(r0, SUBLANES), :] = x
            return x[SUBLANES - 1:SUBLANES, :]

        lax.fori_loop(0, tp // SUBLANES, step, jnp.zeros((1, LANES), F32))
        ct_ref[...] = c_ref[...].T[:nh, :]

    return _call(body, (proj, b_f_pad), name="fgate_fwd", grid=(1,),
                 in_specs=[pl.BlockSpec((tp, LANES), lambda i: (0, fblk)), _full((1, LANES))],
                 out_specs=[_full((tp, LANES)), _full((nh, tp))],
                 out_shape=[jax.ShapeDtypeStruct((tp, LANES), F32), jax.ShapeDtypeStruct((nh, tp), F32)],
                 semantics=("arbitrary",))


def _fgate_bwd(proj, b_f_pad, dct):
    tp, nb = proj.shape
    nh = dct.shape[0]
    fblk = nb // LANES - 1

    def body(f_ref, b_ref, dct_ref, df_ref, db_ref, dc_s):
        b = b_ref[...]
        row = lax.broadcasted_iota(jnp.int32, (SUBLANES, LANES), 0)
        nt = tp // SUBLANES
        dc_s[...] = jnp.concatenate([dct_ref[...], jnp.zeros((LANES - nh, tp), F32)], axis=0).T

        def step(i, carry):
            suffix, acc = carry
            r0 = pl.multiple_of((nt - 1 - i) * SUBLANES, SUBLANES)
            dlf = _tile_cumsum(dc_s[pl.ds(r0, SUBLANES), :], row, reverse=True) + suffix
            df = dlf * _sigmoid(-(f_ref[pl.ds(r0, SUBLANES), :] + b))
            dc_s[pl.ds(r0, SUBLANES), :] = df
            return dlf[0:1, :], acc + df

        _, acc = lax.fori_loop(0, nt, step, (jnp.zeros((1, LANES), F32), jnp.zeros((SUBLANES, LANES), F32)))
        df_ref[...] = dc_s[...].astype(BF16)
        db_ref[...] = jnp.broadcast_to(jnp.sum(acc, axis=0, keepdims=True), (SUBLANES, LANES))

    return _call(body, (proj, b_f_pad, dct), name="fgate_bwd", grid=(1,),
                 in_specs=[pl.BlockSpec((tp, LANES), lambda i: (0, fblk)), _full((1, LANES)), _full((nh, tp))],
                 out_specs=[_full((tp, LANES)), _full((SUBLANES, LANES))],
                 out_shape=[jax.ShapeDtypeStruct((tp, LANES), BF16),
                            jax.ShapeDtypeStruct((SUBLANES, LANES), F32)],
                 scratch_shapes=[pltpu.VMEM((tp, LANES), F32)], semantics=("arbitrary",))


ATT_BQ = 128


def _attn_scores(q, k, cq, ck, mask):
    return jnp.where(mask, _dot_nt(q, k) + (cq - ck), NEG_BIG)


ATT_BUCKET = 3


def _key_buckets(nq):
    return [(lo, min(lo + ATT_BUCKET, nq), min(lo + ATT_BUCKET, nq) * ATT_BQ) for lo in range(0, nq, ATT_BUCKET)]


def _for_bucket(i, nq, fn):
    for lo, hi, klen in _key_buckets(nq):
        pl.when(jnp.logical_and(i >= lo, i < hi))(functools.partial(fn, klen))


def _head_column(c_blk, h):
    lane = lax.broadcasted_iota(jnp.int32, c_blk.shape, 1)
    return jnp.sum(jnp.where(lane == h, c_blk, 0.0), axis=1, keepdims=True)


def _causal_mask(i, klen):
    rows = i * ATT_BQ + lax.broadcasted_iota(jnp.int32, (ATT_BQ, klen), 0)
    return lax.broadcasted_iota(jnp.int32, (ATT_BQ, klen), 1) <= rows


def _attn_fwd(qkv, c, c_t, nh):
    tp = qkv.shape[0]
    att_w = nh * HEAD_DIM
    npair = nh // 2
    bq = ATT_BQ
    nq = tp // bq

    def body(q_ref, k_ref, v_ref, c_ref, ct_ref, o_ref, lse_ref):
        p = pl.program_id(0)
        i = pl.program_id(1)

        def compute(klen):
            mask = _causal_mask(i, klen)
            outs, lses = [], []
            for hh in range(2):
                lo = HEAD_DIM * hh
                h = 2 * p + hh
                s = _attn_scores(q_ref[:, lo:lo + HEAD_DIM], k_ref[:klen, lo:lo + HEAD_DIM],
                                 _head_column(c_ref[...], h), ct_ref[pl.ds(h, 1), :klen], mask)
                m = jnp.max(s, axis=1, keepdims=True)
                e = jnp.exp(s - m)
                l = jnp.sum(e, axis=1, keepdims=True)
                outs.append(_dot(e.astype(BF16), v_ref[:klen, lo:lo + HEAD_DIM]) / l)
                lses.append(jnp.broadcast_to(m + jnp.log(l), (bq, HEAD_DIM)))
            o_ref[...] = jnp.concatenate(outs, axis=1)
            lse_ref[...] = jnp.concatenate(lses, axis=1)

        _for_bucket(i, nq, compute)

    blk = pl.BlockSpec((bq, LANES), lambda p, i: (i, p))
    vm = 4 * _nbytes((tp, LANES), BF16) + 8 * _nbytes((bq, tp), F32)
    return _call(body, (qkv, qkv, qkv, c, c_t), name="attn_fwd", grid=(npair, nq),
                 in_specs=[blk,
                           pl.BlockSpec((tp, LANES), lambda p, i: (0, npair + p)),
                           pl.BlockSpec((tp, LANES), lambda p, i: (0, 2 * npair + p)),
                           pl.BlockSpec((bq, LANES), lambda p, i: (i, 0)), _full((nh, tp))],
                 out_specs=[blk, blk],
                 out_shape=[jax.ShapeDtypeStruct((tp, att_w), F32)] * 2,
                 semantics=("parallel", "parallel"), vmem_bytes=vm)


def _attn_bwd(qkv, c, c_t, lse_b, do, nh):
    tp = qkv.shape[0]
    att_w = nh * HEAD_DIM
    npair = nh // 2
    bq = ATT_BQ
    nq = tp // bq
    scale = 1.0 / math.sqrt(HEAD_DIM)

    def body(q_ref, k_ref, v_ref, c_ref, ct_ref, lse_ref, do_ref, dq_ref, dk_ref, dv_ref, dct_ref, dk_s, dv_s):
        p = pl.program_id(0)
        i = pl.program_id(1)

        @pl.when(i == 0)
        def _():
            dk_s[...] = jnp.zeros_like(dk_s)
            dv_s[...] = jnp.zeros_like(dv_s)

        @pl.when(jnp.logical_and(i == 0, p == 0))
        def _():
            dct_ref[...] = jnp.zeros_like(dct_ref)

        def compute(klen):
            mask = _causal_mask(i, klen)
            dqs, dks, dvs = [], [], []
            for hh in range(2):
                lo = HEAD_DIM * hh
                h = 2 * p + hh
                q = q_ref[:, lo:lo + HEAD_DIM]
                k = k_ref[:klen, lo:lo + HEAD_DIM]
                v = v_ref[:klen, lo:lo + HEAD_DIM]
                s = _attn_scores(q, k, _head_column(c_ref[...], h), ct_ref[pl.ds(h, 1), :klen], mask)
                pr = jnp.exp(s - lse_ref[:, lo:lo + 1])
                doutb = do_ref[:, lo:lo + HEAD_DIM].astype(BF16)
                dp = _dot_nt(doutb, v)
                ds = pr * (dp - jnp.sum(pr * dp, axis=1, keepdims=True))
                dsb = ds.astype(BF16)
                dqs.append(_dot(dsb, k) * scale)
                dks.append(_dot_tn(dsb, q))
                dvs.append(_dot_tn(pr.astype(BF16), doutb))
                dct_ref[pl.ds(h, 1), :klen] = dct_ref[pl.ds(h, 1), :klen] - jnp.sum(ds, axis=0, keepdims=True)
            dq_ref[...] = jnp.concatenate(dqs, axis=1).astype(BF16)
            dk_s[:klen, :] += jnp.concatenate(dks, axis=1)
            dv_s[:klen, :] += jnp.concatenate(dvs, axis=1)

        _for_bucket(i, nq, compute)

        @pl.when(i == nq - 1)
        def _():
            dk_ref[...] = dk_s[...].astype(BF16)
            dv_ref[...] = dv_s[...].astype(BF16)

    blk = pl.BlockSpec((bq, LANES), lambda p, i: (i, p))
    col = pl.BlockSpec((tp, LANES), lambda p, i: (0, p))
    vm = 6 * _nbytes((tp, LANES), BF16) + 2 * _nbytes((tp, LANES), F32) + 12 * _nbytes((bq, tp), F32)
    return _call(body, (qkv, qkv, qkv, c, c_t, lse_b, do), name="attn_bwd", grid=(npair, nq),
                 in_specs=[blk,
                           pl.BlockSpec((tp, LANES), lambda p, i: (0, npair + p)),
                           pl.BlockSpec((tp, LANES), lambda p, i: (0, 2 * npair + p)),
                           pl.BlockSpec((bq, LANES), lambda p, i: (i, 0)), _full((nh, tp)), blk, blk],
                 out_specs=[blk, col, col, _full((nh, tp))],
                 out_shape=[jax.ShapeDtypeStruct((tp, att_w), BF16)] * 3 + [jax.ShapeDtypeStruct((nh, tp), F32)],
                 scratch_shapes=[pltpu.VMEM((tp, LANES), F32)] * 2,
                 semantics=("arbitrary", "arbitrary"), vmem_bytes=vm)


REC_ROWS = 128
HALO = SUBLANES


def _conv_taps(cat):
    taps = []
    for k in range(CONV_WIDTH):
        sh = CONV_WIDTH - 1 - k
        taps.append((pltpu.roll(cat, sh, 0) if sh else cat)[HALO:])
    return taps


def _rec_gates(xc, wa_ref, ba_ref, wx_ref, bx_ref, l_ref):
    xcb = xc.astype(BF16)
    r = _sigmoid(_dot(xcb, wa_ref[...]) + ba_ref[...])
    ig = _sigmoid(_dot(xcb, wx_ref[...]) + bx_ref[...])
    ls = _log_sigmoid(l_ref[...])
    log_a = RG_C * r * ls
    return xcb, r, ig, ls, log_a


def _rec_fwd(proj, xr_blk, yr_blk, rec_w, conv_w, conv_b, wa, ba, wx, bx, lru):
    tp = proj.shape[0]
    w = rec_w
    r_rows = REC_ROWS
    nc = tp // r_rows
    cpb = w // LANES

    def body(xr_ref, yr_ref, cw_ref, cb_ref, wa_ref, ba_ref, wx_ref, bx_ref, l_ref,
             hr_ref, rec_ref, prev_s, carry_s, a_s, u_s):
        i = pl.program_id(0)

        @pl.when(i == 0)
        def _():
            prev_s[...] = jnp.zeros_like(prev_s)
            carry_s[...] = jnp.zeros_like(carry_s)

        x = xr_ref[...]
        taps = _conv_taps(jnp.concatenate([prev_s[...], x], axis=0))
        prev_s[...] = x[r_rows - HALO:]
        xc = cb_ref[...]
        for k in range(CONV_WIDTH):
            xc = xc + cw_ref[k:k + 1, :] * taps[k]
        _, r, ig, ls, log_a = _rec_gates(xc, wa_ref, ba_ref, wx_ref, bx_ref, l_ref)
        a_s[...] = jnp.exp(log_a)
        u_s[...] = jnp.sqrt(-_expm1_nonpos(2.0 * log_a)) * ig * xc

        def tile(j, h):
            r0 = pl.multiple_of(j * SUBLANES, SUBLANES)
            at = a_s[pl.ds(r0, SUBLANES), :]
            ut = u_s[pl.ds(r0, SUBLANES), :]
            out = []
            for rr in range(SUBLANES):
                h = at[rr:rr + 1] * h + ut[rr:rr + 1]
                out.append(h)
            hr_ref[pl.ds(r0, SUBLANES), :] = jnp.concatenate(out, axis=0)
            return h

        carry_s[0:1, :] = lax.fori_loop(0, r_rows // SUBLANES, tile, carry_s[0:1, :])
        g, _ = _gelu_and_grad(yr_ref[...])
        rec_ref[...] = hr_ref[...] * g

    blk = pl.BlockSpec((r_rows, w), lambda i: (i, 0))
    vm = 16 * _nbytes((r_rows, w), F32) + 4 * _nbytes((w, w), BF16)
    return _call(body, (proj, proj, conv_w, conv_b, wa, ba, wx, bx, lru), name="rec_fwd", grid=(nc,),
                 in_specs=[pl.BlockSpec((r_rows, w), lambda i: (i, xr_blk)),
                           pl.BlockSpec((r_rows, w), lambda i: (i, yr_blk)),
                           _full((CONV_WIDTH, w)), _full((1, w)), _full((w, w)), _full((1, w)),
                           _full((w, w)), _full((1, w)), _full((1, w))],
                 out_specs=[blk, blk],
                 out_shape=[jax.ShapeDtypeStruct((tp, w), F32)] * 2,
                 scratch_shapes=[pltpu.VMEM((HALO, w), F32), pltpu.VMEM((SUBLANES, w), F32),
                                 pltpu.VMEM((r_rows, w), F32), pltpu.VMEM((r_rows, w), F32)],
                 semantics=("arbitrary",), vmem_bytes=vm)


def _rec_bwd(proj, xr_blk, yr_blk, rec_w, hr, drec, conv_w, conv_b, wa, ba, wx, bx, lru):
    tp = proj.shape[0]
    w = rec_w
    r_rows = REC_ROWS
    nc = tp // r_rows
    hpc = r_rows // HALO

    def body(xr_ref, xh_ref, yr_ref, hr_ref, hh_ref, drec_ref, cw_ref, cb_ref, wa_ref, ba_ref, wx_ref, bx_ref,
             l_ref, dxr_ref, dyr_ref, dwa_ref, dwx_ref, small_ref, lam_s, a_s, dhr_s, carry_s, next_s):
        i = pl.program_id(0)
        first = (nc - 1 - i) == 0

        @pl.when(i == 0)
        def _():
            carry_s[...] = jnp.zeros_like(carry_s)
            next_s[...] = jnp.zeros_like(next_s)
            dwa_ref[...] = jnp.zeros_like(dwa_ref)
            dwx_ref[...] = jnp.zeros_like(dwx_ref)
            small_ref[...] = jnp.zeros_like(small_ref)

        x = xr_ref[...]
        xprev = jnp.where(first, 0.0, xh_ref[...])
        taps = _conv_taps(jnp.concatenate([xprev, x], axis=0))
        xc = cb_ref[...]
        for k in range(CONV_WIDTH):
            xc = xc + cw_ref[k:k + 1, :] * taps[k]
        xcb, r, ig, ls, log_a = _rec_gates(xc, wa_ref, ba_ref, wx_ref, bx_ref, l_ref)
        a = jnp.exp(log_a)
        a2 = jnp.exp(2.0 * log_a)
        mult = jnp.sqrt(-_expm1_nonpos(2.0 * log_a))
        g, dg = _gelu_and_grad(yr_ref[...])
        hr_v = hr_ref[...]
        drec_v = drec_ref[...]
        dhr_s[...] = drec_v * g
        dyr_ref[...] = (drec_v * hr_v * dg).astype(BF16)
        a_s[...] = a

        def tile(jj, carry):
            r0 = pl.multiple_of((r_rows // SUBLANES - 1 - jj) * SUBLANES, SUBLANES)
            at = a_s[pl.ds(r0, SUBLANES), :]
            dt = dhr_s[pl.ds(r0, SUBLANES), :]
            out = [None] * SUBLANES
            for rr in range(SUBLANES - 1, -1, -1):
                lam = dt[rr:rr + 1] + carry
                out[rr] = lam
                carry = at[rr:rr + 1] * lam
            lam_s[pl.ds(r0, SUBLANES), :] = jnp.concatenate(out, axis=0)
            return carry

        carry_s[0:1, :] = lax.fori_loop(0, r_rows // SUBLANES, tile, carry_s[0:1, :])
        lam = lam_s[...]
        hprev = jnp.where(first, 0.0, hh_ref[...])
        hr_prev = pltpu.roll(jnp.concatenate([hprev, hr_v], axis=0), 1, 0)[HALO:]
        da = lam * hr_prev
        dxc = lam * mult * ig
        di = lam * mult * xc
        dmult = lam * ig * xc
        dlog_a = da * a - dmult * a2 / mult
        dr = dlog_a * (RG_C * ls)
        dls = jnp.sum(dlog_a * (RG_C * r), axis=0, keepdims=True)
        dga = dr * r * (1.0 - r)
        dgx = di * ig * (1.0 - ig)
        dgab = dga.astype(BF16)
        dgxb = dgx.astype(BF16)
        dxc = dxc + _dot_nt(dgab, wa_ref[...]) + _dot_nt(dgxb, wx_ref[...])
        dwa_ref[...] += _dot_tn(xcb, dgab)
        dwx_ref[...] += _dot_tn(xcb, dgxb)
        cat = jnp.concatenate([dxc, next_s[...]], axis=0)
        next_s[...] = dxc[0:HALO]
        dxr = cw_ref[CONV_WIDTH - 1:CONV_WIDTH, :] * dxc
        for k in range(CONV_WIDTH - 1):
            sh = CONV_WIDTH - 1 - k
            dxr = dxr + cw_ref[k:k + 1, :] * pltpu.roll(cat, r_rows + HALO - sh, 0)[:r_rows]
        dxr_ref[...] = dxr.astype(BF16)
        rows = [jnp.sum(dxc * taps[k], axis=0, keepdims=True) for k in range(CONV_WIDTH)]
        rows += [jnp.sum(dxc, axis=0, keepdims=True), jnp.sum(dga, axis=0, keepdims=True),
                 jnp.sum(dgx, axis=0, keepdims=True), dls * _sigmoid(-l_ref[...])]
        small_ref[...] += jnp.concatenate(rows, axis=0)

    def rev(i):
        return nc - 1 - i

    def halo(i):
        return jnp.maximum(rev(i) * hpc - 1, 0)

    blk = pl.BlockSpec((r_rows, w), lambda i: (rev(i), 0))
    vm = 40 * _nbytes((r_rows, w), F32) + 6 * _nbytes((w, w), F32)
    return _call(body, (proj, proj, proj, hr, hr, drec, conv_w, conv_b, wa, ba, wx, bx, lru),
                 name="rec_bwd", grid=(nc,),
                 in_specs=[pl.BlockSpec((r_rows, w), lambda i: (rev(i), xr_blk)),
                           pl.BlockSpec((HALO, w), lambda i: (halo(i), xr_blk)),
                           pl.BlockSpec((r_rows, w), lambda i: (rev(i), yr_blk)),
                           blk,
                           pl.BlockSpec((HALO, w), lambda i: (halo(i), 0)),
                           blk,
                           _full((CONV_WIDTH, w)), _full((1, w)), _full((w, w)), _full((1, w)),
                           _full((w, w)), _full((1, w)), _full((1, w))],
                 out_specs=[blk, blk, _full((w, w)), _full((w, w)), _full((SUBLANES, w))],
                 out_shape=[jax.ShapeDtypeStruct((tp, w), BF16)] * 2
                 + [jax.ShapeDtypeStruct((w, w), F32)] * 2 + [jax.ShapeDtypeStruct((SUBLANES, w), F32)],
                 scratch_shapes=[pltpu.VMEM((r_rows, w), F32)] * 3
                 + [pltpu.VMEM((SUBLANES, w), F32), pltpu.VMEM((HALO, w), F32)],
                 semantics=("arbitrary",), vmem_bytes=vm)


ROW_TARGET = 544


def _mixer_out(attn, rec, g_a, g_r, w_out, h, g_next):
    tp, d = h.shape
    aw, rw = attn.shape[1], rec.shape[1]
    kc = d // N_CHIPS
    tm = _divisor_tile(tp, 16, ROW_TARGET)

    def body(a_ref, r_ref, ga_ref, gr_ref, w_ref, h_ref, gn_ref, h1_ref, z_ref, mix_ref):
        a = a_ref[...]
        r = r_ref[...]
        mix = jnp.concatenate([a * _rstd(a) * ga_ref[...], r * _rstd(r) * gr_ref[...]], axis=1).astype(BF16)
        mix_ref[...] = mix
        h1 = h_ref[...]
        for j in range(N_CHIPS):
            h1 = h1 + _dot(mix[:, j * kc:(j + 1) * kc], w_ref[j])
        h1_ref[...] = h1
        z_ref[...] = (h1 * _rstd(h1) * gn_ref[...]).astype(BF16)

    row = lambda wd: pl.BlockSpec((tm, wd), lambda i: (i, 0))
    vm = 2 * _nbytes((d, d), BF16) + 12 * _nbytes((tm, d), F32)
    return _call(body, (attn, rec, g_a, g_r, w_out, h, g_next), name="mixer_out", grid=(tp // tm,),
                 in_specs=[row(aw), row(rw), _full((1, aw)), _full((1, rw)), _full(w_out.shape), row(d),
                           _full((1, d))],
                 out_specs=[row(d), row(d), row(d)],
                 out_shape=[jax.ShapeDtypeStruct((tp, d), F32), jax.ShapeDtypeStruct((tp, d), BF16),
                            jax.ShapeDtypeStruct((tp, d), BF16)],
                 semantics=("parallel",), vmem_bytes=vm)


def _mixer_bwd(dh_b, w_out, attn, rec, g_a, g_r):
    tp, d = dh_b.shape
    aw, rw = attn.shape[1], rec.shape[1]
    tm = _divisor_tile(tp, 16, ROW_TARGET)

    def body(dh_ref, w_ref, a_ref, r_ref, ga_ref, gr_ref, da_ref, dr_ref, dg_ref):
        @pl.when(pl.program_id(0) == 0)
        def _():
            dg_ref[...] = jnp.zeros_like(dg_ref)

        dh = dh_ref[...]
        dmix = jnp.concatenate([_dot_nt(dh, w_ref[j]) for j in range(N_CHIPS)], axis=1)
        da, dga = _rms_bwd(dmix[:, :aw], a_ref[...], ga_ref[...])
        dr, dgr = _rms_bwd(dmix[:, aw:], r_ref[...], gr_ref[...])
        da_ref[...] = da
        dr_ref[...] = dr
        dg_ref[...] += jnp.broadcast_to(jnp.concatenate([dga, dgr], axis=1), (SUBLANES, d))

    row = lambda wd: pl.BlockSpec((tm, wd), lambda i: (i, 0))
    vm = 2 * _nbytes((d, d), BF16) + 12 * _nbytes((tm, d), F32)
    return _call(body, (dh_b, w_out, attn, rec, g_a, g_r), name="mixer_bwd", grid=(tp // tm,),
                 in_specs=[row(d), _full(w_out.shape), row(aw), row(rw), _full((1, aw)), _full((1, rw))],
                 out_specs=[row(aw), row(rw), _full((SUBLANES, d))],
                 out_shape=[jax.ShapeDtypeStruct((tp, aw), F32), jax.ShapeDtypeStruct((tp, rw), F32),
                            jax.ShapeDtypeStruct((SUBLANES, d), F32)],
                 semantics=("arbitrary",), vmem_bytes=vm)


def _mlp_up(z, w_up):
    tp, d = z.shape
    fc = w_up.shape[2]
    ff = N_CHIPS * fc
    tn = _divisor_tile(fc, LANES, 512)
    per = fc // tn

    def body(z_ref, w_ref, act_ref, up_ref):
        up = _dot(z_ref[...], w_ref[...])
        r = jnp.maximum(up, 0.0)
        act_ref[...] = (r * r).astype(BF16)
        up_ref[...] = up.astype(BF16)

    col = pl.BlockSpec((tp, tn), lambda j: (0, j))
    vm = 2 * _nbytes((tp, d), BF16) + 2 * _nbytes((d, tn), BF16) + 8 * _nbytes((tp, tn), F32)
    return _call(body, (z, w_up), name="mlp_up", grid=(ff // tn,),
                 in_specs=[_full((tp, d)), pl.BlockSpec((None, d, tn), lambda j: (j // per, 0, j % per))],
                 out_specs=[col, col],
                 out_shape=[jax.ShapeDtypeStruct((tp, ff), BF16)] * 2,
                 semantics=("parallel",), vmem_bytes=vm)


def _mlp_down(act, w_down, h, g_next):
    tp, d = h.shape
    ff = act.shape[1]
    fc = ff // N_CHIPS
    tm = _divisor_tile(tp, 16, ROW_TARGET)

    def body(a_ref, w_ref, h_ref, gn_ref, h2_ref, z_ref):
        h2 = h_ref[...]
        for j in range(N_CHIPS):
            h2 = h2 + _dot(a_ref[:, j * fc:(j + 1) * fc], w_ref[j])
        h2_ref[...] = h2
        z_ref[...] = (h2 * _rstd(h2) * gn_ref[...]).astype(BF16)

    row = lambda wd: pl.BlockSpec((tm, wd), lambda i: (i, 0))
    vm = 2 * _nbytes((ff, d), BF16) + 2 * _nbytes((tm, ff), BF16) + 10 * _nbytes((tm, d), F32)
    return _call(body, (act, w_down, h, g_next), name="mlp_down", grid=(tp // tm,),
                 in_specs=[row(ff), _full(w_down.shape), row(d), _full((1, d))],
                 out_specs=[row(d), row(d)],
                 out_shape=[jax.ShapeDtypeStruct((tp, d), F32), jax.ShapeDtypeStruct((tp, d), BF16)],
                 semantics=("parallel",), vmem_bytes=vm)


def _loss_bwd(h, g, target, n_real):
    tp, d = h.shape
    tm = _divisor_tile(tp, 16, ROW_TARGET)

    def body(h_ref, g_ref, t_ref, dh_ref, dhb_ref, dg_ref, loss_ref):
        i = pl.program_id(0)

        @pl.when(i == 0)
        def _():
            dg_ref[...] = jnp.zeros_like(dg_ref)
            loss_ref[...] = jnp.zeros_like(loss_ref)

        x = h_ref[...]
        gv = g_ref[...]
        rowi = i * tm + lax.broadcasted_iota(jnp.int32, (tm, 1), 0)
        real = jnp.logical_and(rowi >= N_META, rowi < N_META + n_real)
        err = jnp.where(real, x * _rstd(x) * gv - t_ref[...], 0.0)
        loss_ref[...] += 0.5 * jnp.sum(jnp.mean(err * err, axis=-1, keepdims=True))
        dx, dgp = _rms_bwd(err * (1.0 / d), x, gv)
        dh_ref[...] = dx
        dhb_ref[...] = dx.astype(BF16)
        dg_ref[...] += jnp.broadcast_to(dgp, (SUBLANES, d))

    row = pl.BlockSpec((tm, d), lambda i: (i, 0))
    return _call(body, (h, g, target), name="loss_bwd", grid=(tp // tm,),
                 in_specs=[row, _full((1, d)), row],
                 out_specs=[row, row, _full((SUBLANES, d)), _full((SUBLANES, LANES))],
                 out_shape=[jax.ShapeDtypeStruct((tp, d), F32), jax.ShapeDtypeStruct((tp, d), BF16),
                            jax.ShapeDtypeStruct((SUBLANES, d), F32), jax.ShapeDtypeStruct((SUBLANES, LANES), F32)],
                 semantics=("arbitrary",), vmem_bytes=16 * _nbytes((tm, d), F32))


def _mlp_dup(dh_b, w_down, up):
    tp, d = dh_b.shape
    fc = w_down.shape[1]
    ff = N_CHIPS * fc
    tn = _divisor_tile(fc, LANES, 512)
    per = fc // tn

    def body(dh_ref, w_ref, up_ref, dup_ref):
        dact = _dot_nt(dh_ref[...], w_ref[...])
        dup_ref[...] = (dact * (2.0 * jnp.maximum(up_ref[...].astype(F32), 0.0))).astype(BF16)

    col = pl.BlockSpec((tp, tn), lambda j: (0, j))
    vm = 2 * _nbytes((tp, d), BF16) + 2 * _nbytes((tn, d), BF16) + 8 * _nbytes((tp, tn), F32)
    return _call(body, (dh_b, w_down, up), name="mlp_dup", grid=(ff // tn,),
                 in_specs=[_full((tp, d)), pl.BlockSpec((None, tn, d), lambda j: (j // per, j % per, 0)),
                           col],
                 out_specs=col, out_shape=jax.ShapeDtypeStruct((tp, ff), BF16),
                 semantics=("parallel",), vmem_bytes=vm)


def _grad_w(a, b, cols_to_chips=False):
    tp, k = a.shape
    n = b.shape[1]
    tk = _divisor_tile(k, LANES, 1024)
    nc = n // N_CHIPS if cols_to_chips else n
    tn = _divisor_tile(nc, LANES, 512)
    per = nc // tn

    def body(a_ref, b_ref, o_ref):
        o_ref[...] = _dot_tn(a_ref[...], b_ref[...]).astype(BF16)

    if cols_to_chips:
        out_spec = pl.BlockSpec((None, tk, tn), lambda i, j: (j // per, i, j % per))
        out_shape = jax.ShapeDtypeStruct((N_CHIPS, k, nc), BF16)
    else:
        out_spec = pl.BlockSpec((tk, tn), lambda i, j: (i, j))
        out_shape = jax.ShapeDtypeStruct((k, n), BF16)
    vm = 2 * _nbytes((tp, tk), BF16) + 2 * _nbytes((tp, tn), BF16) + 6 * _nbytes((tk, tn), F32) \
        + 2 * _nbytes((tp, tk), F32)
    return _call(body, (a, b), name="grad_w", grid=(k // tk, n // tn),
                 in_specs=[pl.BlockSpec((tp, tk), lambda i, j: (0, i)), pl.BlockSpec((tp, tn), lambda i, j: (0, j))],
                 out_specs=out_spec, out_shape=out_shape,
                 semantics=("parallel", "parallel"), vmem_bytes=vm)


def _dx_norm_bwd(pieces, w, w_spec, w_piece, h, g, dres):
    tp, d = h.shape
    tm = _divisor_tile(tp, 16, ROW_TARGET)
    n = len(pieces)

    def body(*refs):
        dy_refs = refs[:n]
        w_ref, h_ref, g_ref, dres_ref, dh_ref, dhb_ref, dg_ref = refs[n:]

        @pl.when(pl.program_id(0) == 0)
        def _():
            dg_ref[...] = jnp.zeros_like(dg_ref)

        dz = _dot_nt(dy_refs[0][...], w_piece(w_ref, 0))
        for i in range(1, n):
            dz = dz + _dot_nt(dy_refs[i][...], w_piece(w_ref, i))
        dx, dgp = _rms_bwd(dz, h_ref[...], g_ref[...])
        dh = dres_ref[...] + dx
        dh_ref[...] = dh
        dhb_ref[...] = dh.astype(BF16)
        dg_ref[...] += jnp.broadcast_to(dgp, (SUBLANES, d))

    row = lambda wd: pl.BlockSpec((tm, wd), lambda i: (i, 0))
    kk = sum(wd for _, _, wd in pieces)
    vm = 2 * _nbytes((d, kk), BF16) + 2 * _nbytes((tm, kk), BF16) + 14 * _nbytes((tm, d), F32)
    piece_specs = [pl.BlockSpec((tm, wd), functools.partial(lambda i, cb: (i, cb), cb=cb)) for _, cb, wd in pieces]
    return _call(body, tuple(a for a, _, _ in pieces) + (w, h, g, dres), name="dx_norm_bwd", grid=(tp // tm,),
                 in_specs=piece_specs + [w_spec, row(d), _full((1, d)), row(d)],
                 out_specs=[row(d), row(d), _full((SUBLANES, d))],
                 out_shape=[jax.ShapeDtypeStruct((tp, d), F32), jax.ShapeDtypeStruct((tp, d), BF16),
                            jax.ShapeDtypeStruct((SUBLANES, d), F32)],
                 semantics=("arbitrary",), vmem_bytes=vm)


def _block_diag(wg):
    nb, b, _ = wg.shape
    eye = jnp.eye(nb, dtype=wg.dtype)
    return (eye[:, None, :, None] * wg[:, :, None, :]).reshape(nb * b, nb * b)


def _diag_blocks(dense, nb):
    b = dense.shape[0] // nb
    d4 = dense.reshape(nb, b, nb, b)
    return jnp.stack([d4[i, :, i, :] for i in range(nb)])


def _row(v):
    return v.reshape(1, -1)


def _forward_layer(l, h, z, p, fetch, g_next):
    d = h.shape[1]
    att_w = d // 2
    rec_w = d - att_w
    nh = att_w // HEAD_DIM
    xr_blk = 3 * att_w // rec_w
    wa_d = _block_diag(p["w_gate_a"][l]).astype(BF16)
    wx_d = _block_diag(p["w_gate_x"][l]).astype(BF16)
    b_f_pad = jnp.zeros((1, LANES), F32).at[0, :nh].set(p["b_f"][l])
    big = dict(w_in_big=_pack_w_in(jnp.moveaxis(fetch("w_in", h), 0, 1).reshape(d, -1), att_w, rec_w, nh))
    proj, qkv = _proj(z, big["w_in_big"], att_w)
    c, c_t = _fgate_fwd(proj, b_f_pad, nh)
    attn, lse_b = _attn_fwd(qkv, c, c_t, nh)
    hr, rec = _rec_fwd(proj, xr_blk, xr_blk + 1, rec_w, p["conv_w"][l], _row(p["conv_b"][l]), wa_d,
                       _row(p["b_gate_a"][l]), wx_d, _row(p["b_gate_x"][l]), _row(p["lru_L"][l]))
    big["w_out"] = fetch("w_out", rec)
    h1, z2, mix = _mixer_out(attn, rec, _row(p["attn_out_g"][l]), _row(p["rec_out_g"][l]),
                             big["w_out"], h, _row(p["mlp_norm_g"][l]))
    big["w_up"] = fetch("w_up", h1)
    act, up = _mlp_up(z2, big["w_up"])
    big["w_down"] = fetch("w_down", act)
    h2, z_next = _mlp_down(act, big["w_down"], h1, _row(g_next))
    saved = dict(h0=h, z1=z, proj=proj, qkv=qkv, c=c, c_t=c_t, attn=attn, lse_b=lse_b, hr=hr, rec=rec, h1=h1,
                 z2=z2, mix=mix, act=act, up=up, wa_d=wa_d, wx_d=wx_d, b_f_pad=b_f_pad, big=big)
    return h2, z_next, saved


def _backward_mlp(l, dh, dh_b, sv, p, tok):
    w_up, w_down = sv["big"]["w_up"], sv["big"]["w_down"]
    fc = w_up.shape[2]
    dup = _mlp_dup(dh_b, w_down, sv["up"])
    g_down = _grad_w(sv["act"], dh_b)
    g_up = _grad_w(sv["z2"], dup, cols_to_chips=True)
    dh, dh_b, dg2 = _dx_norm_bwd([(dup, j, fc) for j in range(N_CHIPS)], w_up, _full(w_up.shape),
                                 lambda w_ref, j: w_ref[j], sv["h1"], _row(p["mlp_norm_g"][l] + tok), dh)
    big = dict(w_down=g_down.reshape((N_CHIPS, -1) + g_down.shape[1:]), w_up=g_up)
    return dh, dh_b, big, dict(mlp_norm_g=dg2[0])


def _backward_mixer(l, dh, dh_b, sv, p, tok):
    d = dh.shape[1]
    att_w = d // 2
    rec_w = d - att_w
    nh = att_w // HEAD_DIM
    xr_blk = 3 * att_w // rec_w
    small = {}
    g_out = _grad_w(sv["mix"], dh_b)
    dattn, drec, dg_mix = _mixer_bwd(dh_b, sv["big"]["w_out"], sv["attn"], sv["rec"],
                                     _row(p["attn_out_g"][l] + tok), _row(p["rec_out_g"][l]))
    small["attn_out_g"] = dg_mix[0, :att_w]
    small["rec_out_g"] = dg_mix[0, att_w:]
    dxr, dyr, dwa, dwx, sm = _rec_bwd(
        sv["proj"], xr_blk, xr_blk + 1, rec_w, sv["hr"], drec, p["conv_w"][l], _row(p["conv_b"][l]), sv["wa_d"],
        _row(p["b_gate_a"][l]), sv["wx_d"], _row(p["b_gate_x"][l]), _row(p["lru_L"][l]))
    small.update(conv_w=sm[:CONV_WIDTH], conv_b=sm[4], b_gate_a=sm[5], b_gate_x=sm[6], lru_L=sm[7],
                 w_gate_a=_diag_blocks(dwa, N_REC_BLOCKS), w_gate_x=_diag_blocks(dwx, N_REC_BLOCKS))
    dq, dk, dv, dct = _attn_bwd(sv["qkv"], sv["c"], sv["c_t"], sv["lse_b"], dattn, nh)
    df, db_f = _fgate_bwd(sv["proj"], sv["b_f_pad"], dct)
    small["b_f"] = db_f[0, :nh]
    pieces = [dq, dk, dv, dxr, dyr, df]
    offs = [0, att_w, 2 * att_w, 3 * att_w, 3 * att_w + rec_w, 3 * att_w + 2 * rec_w]
    gq, gk, gv, gxr, gyr, gf = [_grad_w(sv["z1"], pc) for pc in pieces]
    g_in = jnp.concatenate([gq, gk, gv, gf[:, :nh], gxr, gyr], axis=1)
    w_big = sv["big"]["w_in_big"]
    widths = [pc.shape[1] for pc in pieces]
    dh, dh_b, dg1 = _dx_norm_bwd(
        [(pc, 0, wd) for pc, wd in zip(pieces, widths)], w_big, _full(w_big.shape),
        lambda w_ref, i: w_ref[:, offs[i]:offs[i] + widths[i]], sv["h0"], _row(p["attn_norm_g"][l]), dh)
    small["attn_norm_g"] = dg1[0]
    big = dict(w_in=jnp.moveaxis(g_in.reshape(d, N_CHIPS, -1), 1, 0),
               w_out=g_out.reshape((N_CHIPS, -1) + g_out.shape[1:]))
    return dh, dh_b, big, small


def _pack_w_in(w_in, att_w, rec_w, nh):
    qkv = w_in[..., :3 * att_w]
    f = w_in[..., 3 * att_w:3 * att_w + nh]
    xy = w_in[..., 3 * att_w + nh:]
    fpad = jnp.zeros(f.shape[:-1] + (LANES - nh,), w_in.dtype)
    return jnp.concatenate([qkv, xy, f, fpad], axis=-1)


ANY = pl.BlockSpec(memory_space=pl.ANY)


def _coords():
    return lax.axis_index("x"), lax.axis_index("y"), lax.axis_index("c")


def _other_chips(x, y):
    return [(1 - x, y), (x, 1 - y), (1 - x, 1 - y)]


def _remote(src, dst, send_sems, recv_sems, k, to):
    return pltpu.make_async_remote_copy(src_ref=src, dst_ref=dst, send_sem=send_sems.at[k],
                                        recv_sem=recv_sems.at[k], device_id=to, device_id_type=MESH)


def _all_gather_chips(shards):
    n = len(shards)
    per = 6

    def body(*refs):
        ins, outs = refs[:n], refs[n:2 * n]
        send_sems, recv_sems, local_sems = refs[2 * n:]
        x, y, c = _coords()
        me = 2 * x + y
        sibling = (x, y, 1 - c)
        chips = _other_chips(x, y)
        local = [pltpu.make_async_copy(ins[t], outs[t].at[me], local_sems.at[t]) for t in range(n)]
        for cp in local:
            cp.start()
        sends = []
        for t in range(n):
            for j, (px, py) in enumerate(chips):
                cp = _remote(ins[t].at[c], outs[t].at[me, c], send_sems, recv_sems, per * t + j, (px, py, c))
                cp.start()
                sends.append(cp)
        for t in range(n):
            for j, (px, py) in enumerate(chips):
                landed = outs[t].at[2 * px + py, c]
                _remote(landed, landed, send_sems, recv_sems, per * t + j, (px, py, c)).wait_recv()
                cp = _remote(landed, landed, send_sems, recv_sems, per * t + 3 + j, sibling)
                cp.start()
                sends.append(cp)
        for t in range(n):
            for j, (px, py) in enumerate(chips):
                passed = outs[t].at[2 * px + py, 1 - c]
                _remote(passed, passed, send_sems, recv_sems, per * t + 3 + j, sibling).wait_recv()
        for cp in sends:
            cp.wait_send()
        for cp in local:
            cp.wait()

    return _call(body, tuple(shards), name="all_gather_chips",
                 in_specs=[ANY] * n, out_specs=[ANY] * n,
                 out_shape=[jax.ShapeDtypeStruct((N_CHIPS,) + s.shape, s.dtype) for s in shards],
                 scratch_shapes=[pltpu.SemaphoreType.DMA((per * n,)), pltpu.SemaphoreType.DMA((per * n,)),
                                 pltpu.SemaphoreType.DMA((n,))])


HBM = pl.BlockSpec(memory_space=pltpu.HBM)
SEM = pl.BlockSpec(memory_space=pltpu.SEMAPHORE)
DATAFLOW = pltpu.SideEffectType.DATAFLOW_SIDE_EFFECTING


def _in_hbm(a):
    return pltpu.with_memory_space_constraint(a, pltpu.HBM)


def _push_start(srcs, lands, own_slab, name):
    n = len(srcs)
    same = own_slab and all(s is ld for s, ld in zip(srcs, lands))
    n_in = n if same else 2 * n

    def body(*refs):
        src_refs = refs[:n]
        land_refs = src_refs if same else refs[n:2 * n]
        send_sems, recv_sems = refs[n_in], refs[n_in + 1]
        token = refs[-1]
        x, y, c = _coords()
        me = 2 * x + y
        for t in range(n):
            for px, py in _other_chips(x, y):
                src = src_refs[t].at[me if own_slab else 2 * px + py]
                _remote(src, land_refs[t].at[me], send_sems, recv_sems, t, (px, py, c)).start()
        token[...] = jnp.zeros_like(token)

    operands = tuple(srcs) if same else tuple(srcs) + tuple(lands)
    res = _call(
        body, [_in_hbm(a) for a in operands], name=name,
        out_shape=(pltpu.SemaphoreType.DMA((n,)), pltpu.SemaphoreType.DMA((n,)))
        + tuple(pltpu.HBM(a.shape, a.dtype) for a in operands) + (jax.ShapeDtypeStruct((SUBLANES, LANES), F32),),
        in_specs=[HBM] * n_in, out_specs=(SEM, SEM) + (HBM,) * n_in + (pl.BlockSpec(memory_space=pltpu.VMEM),),
        input_output_aliases={i: 2 + i for i in range(n_in)}, side_effects=DATAFLOW)
    send_sems, recv_sems, token = res[0], res[1], res[-1]
    srcs_thru = res[2:2 + n]
    lands_thru = srcs_thru if same else res[2 + n:2 + 2 * n]
    return send_sems, recv_sems, srcs_thru, lands_thru, token


def _push_wait(send_sems, recv_sems, ids, srcs, lands, after, name):
    n = len(lands)
    same = all(s is ld for s, ld in zip(srcs, lands))
    n_in = n if same else 2 * n

    def body(*refs):
        land_refs = refs[:n] if same else refs[n:2 * n]
        send_sems, recv_sems = refs[n_in], refs[n_in + 1]
        x, y, c = _coords()
        for t in range(n):
            three = land_refs[t].at[pl.ds(0, N_CHIPS - 1)]
            arrivals = _remote(three, three, send_sems, recv_sems, ids[t], (x, y, c))
            arrivals.wait_send()
            arrivals.wait_recv()

    operands = tuple(lands) if same else tuple(srcs) + tuple(lands)
    res = _call(
        body, operands + (send_sems, recv_sems, after), name=name,
        out_shape=tuple(pltpu.HBM(a.shape, a.dtype) for a in operands),
        in_specs=[HBM] * n_in + [SEM, SEM, ANY], out_specs=(HBM,) * n_in,
        input_output_aliases={i: i for i in range(n_in)}, side_effects=DATAFLOW)
    return list(res) if same else (list(res[:n]), list(res[n:]))


def _sum_partials(part, landed, chip):
    _, rows, cols = part.shape
    br = _divisor_tile(rows, 16, ELEM_ROWS)

    def body(chip_ref, own_ref, a_ref, b_ref, c_ref, o_ref):
        o_ref[...] = ((own_ref[...].astype(F32) + a_ref[...].astype(F32)) + b_ref[...].astype(F32)) \
            + c_ref[...].astype(F32)

    def other(k):
        return pl.BlockSpec((None, br, cols), lambda i, ch: (jnp.where(ch[0] <= k, k + 1, k), i, 0))

    spec = pltpu.PrefetchScalarGridSpec(
        num_scalar_prefetch=1, grid=(rows // br,),
        in_specs=[pl.BlockSpec((None, br, cols), lambda i, ch: (ch[0], i, 0)), other(0), other(1), other(2)],
        out_specs=pl.BlockSpec((br, cols), lambda i, ch: (i, 0)))
    return _call(body, (chip, part, landed, landed, landed), name="sum_partials", grid_spec=spec,
                 out_shape=jax.ShapeDtypeStruct((rows, cols), F32), semantics=("parallel",))


def _cast_to_slab(w, l, chip):
    _, rows, cols = w.shape
    br = _divisor_tile(rows, 16, ELEM_ROWS)

    def body(chip_ref, w_ref, o_ref):
        o_ref[...] = w_ref[...].astype(BF16)

    spec = pltpu.PrefetchScalarGridSpec(
        num_scalar_prefetch=1, grid=(rows // br,),
        in_specs=[pl.BlockSpec((None, br, cols), lambda i, ch: (l, i, 0))],
        out_specs=pl.BlockSpec((None, br, cols), lambda i, ch: (ch[0], i, 0)))
    return _call(body, (chip, w), name="cast_to_slab", grid_spec=spec,
                 out_shape=jax.ShapeDtypeStruct((N_CHIPS, rows, cols), BF16), semantics=("parallel",))


def _swap_with_sibling(arrs):
    n = len(arrs)

    def body(*refs):
        ins, outs = refs[:n], refs[n:2 * n]
        send_sems, recv_sems = refs[2 * n:]
        x, y, c = _coords()
        cps = [_remote(ins[t], outs[t], send_sems, recv_sems, t, (x, y, 1 - c)) for t in range(n)]
        for cp in cps:
            cp.start()
        for cp in cps:
            cp.wait_recv()
        for cp in cps:
            cp.wait_send()

    return _call(body, tuple(arrs), name="swap_with_sibling",
                 in_specs=[ANY] * n, out_specs=[ANY] * n,
                 out_shape=[jax.ShapeDtypeStruct(s.shape, s.dtype) for s in arrs],
                 scratch_shapes=[pltpu.SemaphoreType.DMA((n,)), pltpu.SemaphoreType.DMA((n,))])


def _all_gather_devices(buf):
    per = N_DEV - 1

    def body(in_ref, out_ref, send_sems, recv_sems, local_sem):
        x, y, c = _coords()
        me = 4 * x + 2 * y + c
        local = pltpu.make_async_copy(in_ref, out_ref.at[me], local_sem)
        local.start()
        peers = []
        for k in range(1, N_DEV):
            fx, fy, fc = (k >> 2) & 1, (k >> 1) & 1, k & 1
            peers.append((x ^ fx, y ^ fy, c ^ fc))
        sends = [_remote(in_ref, out_ref.at[me], send_sems, recv_sems, k, peer) for k, peer in enumerate(peers)]
        for cp in sends:
            cp.start()
        for k, (px, py, pc) in enumerate(peers):
            landed = out_ref.at[4 * px + 2 * py + pc]
            _remote(landed, landed, send_sems, recv_sems, k, (px, py, pc)).wait_recv()
        for cp in sends:
            cp.wait_send()
        local.wait()

    return _call(body, (buf,), name="all_gather_devices", in_specs=[ANY], out_specs=ANY,
                 out_shape=jax.ShapeDtypeStruct((N_DEV,) + buf.shape, buf.dtype),
                 scratch_shapes=[pltpu.SemaphoreType.DMA((per,)), pltpu.SemaphoreType.DMA((per,)),
                                 pltpu.SemaphoreType.DMA(())])


ELEM_ROWS = 256


def _sum_slabs(r):
    n, rows, cols = r.shape
    br = _divisor_tile(rows, 16, ELEM_ROWS)

    def body(r_ref, o_ref):
        acc = r_ref[0].astype(F32)
        for j in range(1, n):
            acc = acc + r_ref[j].astype(F32)
        o_ref[...] = acc

    return _call(body, (r,), name="sum_slabs", grid=(rows // br,),
                 in_specs=[pl.BlockSpec((n, br, cols), lambda i: (0, i, 0))],
                 out_specs=pl.BlockSpec((br, cols), lambda i: (i, 0)),
                 out_shape=jax.ShapeDtypeStruct((rows, cols), F32), semantics=("parallel",))


def _adamw_math(w, g, m, v):
    c1 = 1.0 - ADAM_B1 ** ADAM_STEP
    c2 = 1.0 - ADAM_B2 ** ADAM_STEP
    nm = ADAM_B1 * m + (1.0 - ADAM_B1) * g
    nv = ADAM_B2 * v + (1.0 - ADAM_B2) * (g * g)
    delta = -ADAM_LR * ((nm / c1) / (jnp.sqrt(nv / c2) + ADAM_EPS) + ADAM_WD * w)
    return delta, nm, nv


def _adamw(w, g, m, v):
    rows, cols = w.shape
    br = _divisor_tile(rows, 8, ELEM_ROWS)

    def body(w_ref, g_ref, m_ref, v_ref, d_ref, nm_ref, nv_ref):
        d_ref[...], nm_ref[...], nv_ref[...] = _adamw_math(w_ref[...], g_ref[...], m_ref[...], v_ref[...])

    blk = pl.BlockSpec((br, cols), lambda i: (i, 0))
    return _call(body, (w, g, m, v), name="adamw", grid=(rows // br,),
                 in_specs=[blk] * 4, out_specs=[blk] * 3,
                 out_shape=[jax.ShapeDtypeStruct((rows, cols), F32)] * 3, semantics=("parallel",))


def _adamw_layer(w, m, v, l, g_mine, g_theirs, prev):
    _, rows, cols = w.shape
    br = _divisor_tile(rows, 8, ELEM_ROWS)

    def body(w_ref, m_ref, v_ref, ga_ref, gb_ref, *rest):
        g_ref, d_ref, nm_ref, nv_ref = rest[4:]
        g = ga_ref[...] + gb_ref[...]
        g_ref[...] = g
        d_ref[...], nm_ref[...], nv_ref[...] = _adamw_math(w_ref[...], g, m_ref[...], v_ref[...])

    slot = pl.BlockSpec((None, br, cols), lambda i: (l, i, 0))
    blk = pl.BlockSpec((br, cols), lambda i: (i, 0))
    return _call(body, (w, m, v, g_mine, g_theirs) + tuple(prev), name="adamw_layer", grid=(rows // br,),
                 in_specs=[slot] * 3 + [blk] * 2 + [ANY] * 4, out_specs=[slot] * 4,
                 out_shape=[jax.ShapeDtypeStruct(w.shape, F32)] * 4,
                 input_output_aliases={5: 0, 6: 1, 7: 2, 8: 3}, semantics=("parallel",))


BIG = ("w_in", "w_out", "w_up", "w_down")
WEIGHTS = ("meta", "attn_norm_g", "w_in", "b_f", "conv_w", "conv_b", "w_gate_a", "b_gate_a", "w_gate_x",
           "b_gate_x", "lru_L", "attn_out_g", "rec_out_g", "w_out", "mlp_norm_g", "w_up", "w_down", "final_g")
SMALL = tuple(k for k in WEIGHTS if k not in BIG)
COL_SHARDED_SMALL = ("meta", "conv_w")
PACK_UNIT = ELEM_ROWS * LANES


def _pack(arrs):
    flat = jnp.concatenate([a.reshape(-1) for a in arrs])
    total = -(-flat.shape[0] // PACK_UNIT) * PACK_UNIT
    return jnp.pad(flat, (0, total - flat.shape[0])).reshape(-1, LANES)


def _unpack(buf, shapes):
    flat = buf.reshape(-1)
    out, off = [], 0
    for s in shapes:
        size = math.prod(s)
        out.append(flat[off:off + size].reshape(s))
        off += size
    return out


def _halves(a):
    return a.reshape((2, a.shape[0] // 2) + a.shape[1:])


def _cols_from_chips(g):
    return jnp.moveaxis(g, 0, -2).reshape(g.shape[1:-1] + (N_CHIPS * g.shape[-1],))


def kernel(x, meta, attn_norm_g, w_in, b_f, conv_w, conv_b, w_gate_a, b_gate_a, w_gate_x, b_gate_x, lru_L, attn_out_g, rec_out_g, w_out, mlp_norm_g, w_up, w_down, final_g, loss_target, m_meta, m_attn_norm_g, m_w_in, m_b_f, m_conv_w, m_conv_b, m_w_gate_a, m_b_gate_a, m_w_gate_x, m_b_gate_x, m_lru_L, m_attn_out_g, m_rec_out_g, m_w_out, m_mlp_norm_g, m_w_up, m_w_down, m_final_g, v_meta, v_attn_norm_g, v_w_in, v_b_f, v_conv_w, v_conv_b, v_w_gate_a, v_b_gate_a, v_w_gate_x, v_b_gate_x, v_lru_L, v_attn_out_g, v_rec_out_g, v_w_out, v_mlp_norm_g, v_w_up, v_w_down, v_final_g):
    w = dict(meta=meta, attn_norm_g=attn_norm_g, w_in=w_in, b_f=b_f, conv_w=conv_w, conv_b=conv_b,
             w_gate_a=w_gate_a, b_gate_a=b_gate_a, w_gate_x=w_gate_x, b_gate_x=b_gate_x, lru_L=lru_L,
             attn_out_g=attn_out_g, rec_out_g=rec_out_g, w_out=w_out, mlp_norm_g=mlp_norm_g, w_up=w_up,
             w_down=w_down, final_g=final_g)
    m = dict(meta=m_meta, attn_norm_g=m_attn_norm_g, w_in=m_w_in, b_f=m_b_f, conv_w=m_conv_w, conv_b=m_conv_b,
             w_gate_a=m_w_gate_a, b_gate_a=m_b_gate_a, w_gate_x=m_w_gate_x, b_gate_x=m_b_gate_x, lru_L=m_lru_L,
             attn_out_g=m_attn_out_g, rec_out_g=m_rec_out_g, w_out=m_w_out, mlp_norm_g=m_mlp_norm_g,
             w_up=m_w_up, w_down=m_w_down, final_g=m_final_g)
    v = dict(meta=v_meta, attn_norm_g=v_attn_norm_g, w_in=v_w_in, b_f=v_b_f, conv_w=v_conv_w, conv_b=v_conv_b,
             w_gate_a=v_w_gate_a, b_gate_a=v_b_gate_a, w_gate_x=v_w_gate_x, b_gate_x=v_b_gate_x, lru_L=v_lru_L,
             attn_out_g=v_attn_out_g, rec_out_g=v_rec_out_g, w_out=v_w_out, mlp_norm_g=v_mlp_norm_g,
             w_up=v_w_up, w_down=v_w_down, final_g=v_final_g)
    s_len, d = x.shape[1], x.shape[2]
    depth = w_in.shape[0]
    att_w = d // 2
    rec_w = d - att_w
    nh = att_w // HEAD_DIM
    chip = 2 * lax.axis_index("x") + lax.axis_index("y")

    g_conv, g_meta = [g.reshape((N_CHIPS, g.shape[1] * g.shape[2]) + g.shape[3:])
                      for g in _all_gather_chips([_halves(w["conv_w"]), _halves(w["meta"])])]
    p = dict(w)
    p["conv_w"] = _cols_from_chips(g_conv)
    meta_full = jnp.moveaxis(g_meta, 0, 1).reshape(N_META, d)

    chip1 = chip.reshape(1).astype(jnp.int32)
    pushes, tokens = [], []
    for l in range(depth):
        slabs = [_cast_to_slab(w[k], l, chip1) for k in BIG]
        send_sems, recv_sems, _, lands, token = _push_start(slabs, slabs, True, f"weights_start_{l}")
        pushes.append((send_sems, recv_sems, lands))
        tokens.append(token[0, 0])

    t_len = N_META + s_len
    pad = -t_len % SEQ_TILE
    h = jnp.concatenate([meta_full, x[0], jnp.zeros((pad, d), F32)], axis=0)
    tgt = jnp.concatenate([jnp.zeros((N_META, d), F32), loss_target[0], jnp.zeros((pad, d), F32)], axis=0)
    z = _rms_fwd(h, _row(p["attn_norm_g"][0] + sum(tokens)))
    saved = []
    for l in range(depth):
        send_sems, recv_sems, lands = pushes[l]

        def fetch(k, after, l=l, send_sems=send_sems, recv_sems=recv_sems, lands=lands):
            i = BIG.index(k)
            return _push_wait(send_sems, recv_sems, [i], [lands[i]], [lands[i]], after, f"{k}_wait_{l}")[0]

        g_next = p["attn_norm_g"][l + 1] if l + 1 < depth else p["final_g"]
        h, z, sv = _forward_layer(l, h, z, p, fetch, g_next)
        saved.append(sv)
    dh, dh_b, dg_final, loss_part = _loss_bwd(h, _row(p["final_g"]), tgt, s_len)

    small = {k: [None] * depth for k in SMALL if k not in ("meta", "final_g")}
    pushes = [None] * depth
    tok = 0.0
    for l in reversed(range(depth)):
        dh, dh_b, big_mlp, sm_mlp = _backward_mlp(l, dh, dh_b, saved[l], p, tok)
        parts = [big_mlp["w_down"], big_mlp["w_up"]]
        push_mlp = _push_start(parts, [lax.empty(a.shape, a.dtype) for a in parts], False, f"mlp_grads_start_{l}")
        dh, dh_b, big_mix, sm_mix = _backward_mixer(l, dh, dh_b, saved[l], p, push_mlp[4][0, 0])
        parts = [big_mix["w_out"], big_mix["w_in"]]
        push_mix = _push_start(parts, [lax.empty(a.shape, a.dtype) for a in parts], False, f"mixer_grads_start_{l}")
        tok = push_mix[4][0, 0]
        pushes[l] = {("w_down", "w_up"): push_mlp, ("w_out", "w_in"): push_mix}
        for k, val in {**sm_mlp, **sm_mix}.items():
            small[k][l] = val
    grads = {k: jnp.stack(val) for k, val in small.items()}
    grads["final_g"] = dg_final[0]
    grads["meta"] = dh[:N_META]
    dx = dh[N_META:t_len]

    last_token = pushes[0][("w_out", "w_in")][4]
    outs = {k: [lax.empty(w[k].shape, F32) for _ in range(4)] for k in BIG}
    for l in reversed(range(depth)):
        sums = {}
        for names, (send_sems, recv_sems, parts, lands, _) in pushes[l].items():
            parts, landed = _push_wait(send_sems, recv_sems, [0, 1], parts, lands, last_token,
                                       f"{names[0]}_grads_wait_{l}")
            for k, part, land in zip(names, parts, landed):
                sums[k] = _sum_partials(part, land, chip1)
        theirs = _swap_with_sibling([sums[k] for k in BIG])
        for k, other in zip(BIG, theirs):
            outs[k] = _adamw_layer(w[k], m[k], v[k], l, sums[k], other, outs[k])
    out_g, out_d, out_m, out_v = [{k: outs[k][i] for k in BIG} for i in range(4)]

    full_shapes = [grads[k].shape for k in SMALL] + [(1,)]
    packed = _pack([grads[k].astype(F32) for k in SMALL] + [loss_part[0, :1]])
    total = _sum_slabs(_all_gather_devices(packed))
    small_g = dict(zip(SMALL + ("loss",), _unpack(total, full_shapes)))
    for k in COL_SHARDED_SMALL:
        n = w[k].shape[-1]
        small_g[k] = lax.dynamic_slice_in_dim(small_g[k], chip * n, n, axis=small_g[k].ndim - 1)
    local_shapes = [w[k].shape for k in SMALL]
    res = _adamw(_pack([w[k] for k in SMALL]), _pack([small_g[k] for k in SMALL]),
                 _pack([m[k] for k in SMALL]), _pack([v[k] for k in SMALL]))
    out_g.update({k: small_g[k] for k in SMALL})
    for dst, buf in zip((out_d, out_m, out_v), res):
        dst.update(zip(SMALL, _unpack(buf, local_shapes)))

    return (small_g["loss"].reshape(()), dx[None],
            *[out_g[k] for k in WEIGHTS], *[out_d[k] for k in WEIGHTS],
            *[out_m[k] for k in WEIGHTS], *[out_v[k] for k in WEIGHTS])
```

```python
import functools
import math

import jax
import jax.numpy as jnp
from jax import lax
from jax.experimental import pallas as pl
from jax.experimental.pallas import tpu as pltpu

F32 = jnp.float32
BF16 = jnp.bfloat16

N_META = 16
HEAD_DIM = 64
N_REC_BLOCKS = 8
CONV_WIDTH = 4
RG_C = 8.0
NORM_EPS = 1e-6
ADAM_LR = 0.001
ADAM_B1 = 0.9
ADAM_B2 = 0.999
ADAM_EPS = 1e-08
ADAM_WD = 0.01
ADAM_STEP = 10

LANES = 128
SUBLANES = 8
SEQ_TILE = 128
VMEM_CAP = 60 * 2**20
VMEM_SLACK = 6 * 2**20
NEG_BIG = -1e30
N_CHIPS = 4
N_DEV = 8
MESH = pl.DeviceIdType.MESH


def _nbytes(shape, dtype):
    return math.prod(shape) * jnp.dtype(dtype).itemsize


def _call(body, args, *, name, out_shape, grid=(), in_specs=None, out_specs=None, scratch_shapes=(),
          grid_spec=None, semantics=None, vmem_bytes=None, side_effects=None, hbm_results=True, **kw):
    cp = {}
    if semantics is not None:
        cp["dimension_semantics"] = semantics
    if vmem_bytes is not None:
        cp["vmem_limit_bytes"] = int(min(VMEM_CAP, vmem_bytes + VMEM_SLACK))
    if side_effects is not None:
        cp["has_side_effects"] = side_effects
    if grid_spec is not None:
        kw["grid_spec"] = grid_spec
    else:
        kw.update(grid=grid, in_specs=in_specs, out_specs=out_specs, scratch_shapes=scratch_shapes)
    if hbm_results:
        out_shape = jax.tree.map(
            lambda s: pltpu.HBM(s.shape, s.dtype) if isinstance(s, jax.ShapeDtypeStruct) else s, out_shape)
    fn = pl.pallas_call(
        body, name=name, out_shape=out_shape,
        compiler_params=pltpu.CompilerParams(**cp), **kw)
    return fn(*[_in_hbm(a) if jnp.issubdtype(getattr(a, "dtype", jnp.int32), jnp.floating) else a for a in args])


def _divisor_tile(n, unit, target):
    best = None
    for t in range(unit, min(n, target) + 1, unit):
        if n % t == 0:
            best = t
    return n if best is None else best


def _sigmoid(x):
    return 1.0 / (1.0 + jnp.exp(-x))


def _log1p_unit(e):
    series = e * (1.0 - e * (0.5 - e * (1.0 / 3.0)))
    return jnp.where(e < 1e-2, series, jnp.log(1.0 + e))


def _log_sigmoid(x):
    return jnp.minimum(x, 0.0) - _log1p_unit(jnp.exp(-jnp.abs(x)))


def _expm1_nonpos(x):
    small = x * (1.0 + x * (1.0 / 2 + x * (1.0 / 6 + x * (1.0 / 24 + x * (1.0 / 120 + x * (1.0 / 720))))))
    return jnp.where(x > -0.25, small, jnp.exp(x) - 1.0)


_GELU_K = math.sqrt(2.0 / math.pi)
_GELU_C = 0.044715


def _gelu_and_grad(y):
    th = jnp.tanh(_GELU_K * (y + _GELU_C * y * y * y))
    g = 0.5 * y * (1.0 + th)
    dg = 0.5 * (1.0 + th) + 0.5 * y * (1.0 - th * th) * _GELU_K * (1.0 + 3.0 * _GELU_C * y * y)
    return g, dg


def _rstd(x):
    return lax.rsqrt(jnp.mean(x * x, axis=-1, keepdims=True) + NORM_EPS)


def _rms_bwd(dz, x, g):
    rs = _rstd(x)
    xh = x * rs
    dgp = jnp.sum(dz * xh, axis=0, keepdims=True)
    dxh = dz * g
    dx = rs * (dxh - xh * jnp.mean(dxh * xh, axis=-1, keepdims=True))
    return dx, dgp


def _dot(a, b):
    return jnp.dot(a, b, preferred_element_type=F32)


def _dot_nt(a, b):
    return lax.dot_general(a, b, (((1,), (1,)), ((), ())), preferred_element_type=F32)


def _dot_tn(a, b):
    return lax.dot_general(a, b, (((0,), (0,)), ((), ())), preferred_element_type=F32)


def _full(shape):
    nd = len(shape)
    return pl.BlockSpec(shape, lambda *_: (0,) * nd)


def _rms_fwd(h, g):
    tp, d = h.shape
    tm = _divisor_tile(tp, 16, 544)

    def body(h_ref, g_ref, z_ref):
        x = h_ref[...]
        z_ref[...] = (x * _rstd(x) * g_ref[...]).astype(BF16)

    return _call(body, (h, g), name="rms_fwd", grid=(tp // tm,),
                 in_specs=[pl.BlockSpec((tm, d), lambda i: (i, 0)), _full((1, d))],
                 out_specs=pl.BlockSpec((tm, d), lambda i: (i, 0)),
                 out_shape=jax.ShapeDtypeStruct((tp, d), BF16), semantics=("parallel",))


def _proj(z, w_big, att_w):
    tp, d = z.shape
    nb = w_big.shape[1]
    tn = _divisor_tile(nb, LANES, 512)
    assert (3 * att_w) % tn == 0
    n_qkv = 3 * att_w // tn
    scale = 1.0 / math.sqrt(HEAD_DIM)

    def body(z_ref, w_ref, p_ref, qkv_ref):
        j = pl.program_id(0)
        acc = _dot(z_ref[...], w_ref[...])
        p_ref[...] = acc

        @pl.when(j < n_qkv)
        def _():
            col = j * tn + lax.broadcasted_iota(jnp.int32, (1, tn), 1)
            qkv_ref[...] = (acc * jnp.where(col < att_w, scale, 1.0)).astype(BF16)

    vm = 2 * (_nbytes((tp, d), BF16) + _nbytes((d, tn), BF16) + _nbytes((tp, tn), F32) * 2)
    return _call(body, (z, w_big), name="proj", grid=(nb // tn,),
                 in_specs=[_full((tp, d)), pl.BlockSpec((d, tn), lambda j: (0, j))],
                 out_specs=[pl.BlockSpec((tp, tn), lambda j: (0, j)),
                            pl.BlockSpec((tp, tn), lambda j: (0, jnp.minimum(j, n_qkv - 1)))],
                 out_shape=[jax.ShapeDtypeStruct((tp, nb), F32),
                            jax.ShapeDtypeStruct((tp, 3 * att_w), BF16)],
                 semantics=("arbitrary",), vmem_bytes=vm)


def _tile_cumsum(x, row, reverse=False):
    for s in (1, 2, 4):
        if reverse:
            x = x + jnp.where(row < SUBLANES - s, pltpu.roll(x, SUBLANES - s, 0), 0.0)
        else:
            x = x + jnp.where(row >= s, pltpu.roll(x, s, 0), 0.0)
    return x


def _fgate_fwd(proj, b_f_pad, nh):
    tp, nb = proj.shape
    fblk = nb // LANES - 1

    def body(f_ref, b_ref, c_ref, ct_ref):
        b = b_ref[...]
        row = lax.broadcasted_iota(jnp.int32, (SUBLANES, LANES), 0)

        def step(i, carry):
            r0 = pl.multiple_of(i * SUBLANES, SUBLANES)
            lf = _log_sigmoid(f_ref[pl.ds(r0, SUBLANES), :] + b)
            x = _tile_cumsum(lf, row) + carry
            c_ref[pl.ds(r0, SUBLANES), :] = x
            return x[SUBLANES - 1:SUBLANES, :]

        lax.fori_loop(0, tp // SUBLANES, step, jnp.zeros((1, LANES), F32))
        ct_ref[...] = c_ref[...].T[:nh, :]

    return _call(body, (proj, b_f_pad), name="fgate_fwd", grid=(1,),
                 in_specs=[pl.BlockSpec((tp, LANES), lambda i: (0, fblk)), _full((1, LANES))],
                 out_specs=[_full((tp, LANES)), _full((nh, tp))],
                 out_shape=[jax.ShapeDtypeStruct((tp, LANES), F32), jax.ShapeDtypeStruct((nh, tp), F32)],
                 semantics=("arbitrary",))


def _fgate_bwd(proj, b_f_pad, dct):
    tp, nb = proj.shape
    nh = dct.shape[0]
    fblk = nb // LANES - 1

    def body(f_ref, b_ref, dct_ref, df_ref, db_ref, dc_s):
        b = b_ref[...]
        row = lax.broadcasted_iota(jnp.int32, (SUBLANES, LANES), 0)
        nt = tp // SUBLANES
        dc_s[...] = jnp.concatenate([dct_ref[...], jnp.zeros((LANES - nh, tp), F32)], axis=0).T

        def step(i, carry):
            suffix, acc = carry
            r0 = pl.multiple_of((nt - 1 - i) * SUBLANES, SUBLANES)
            dlf = _tile_cumsum(dc_s[pl.ds(r0, SUBLANES), :], row, reverse=True) + suffix
            df = dlf * _sigmoid(-(f_ref[pl.ds(r0, SUBLANES), :] + b))
            dc_s[pl.ds(r0, SUBLANES), :] = df
            return dlf[0:1, :], acc + df

        _, acc = lax.fori_loop(0, nt, step, (jnp.zeros((1, LANES), F32), jnp.zeros((SUBLANES, LANES), F32)))
        df_ref[...] = dc_s[...].astype(BF16)
        db_ref[...] = jnp.broadcast_to(jnp.sum(acc, axis=0, keepdims=True), (SUBLANES, LANES))

    return _call(body, (proj, b_f_pad, dct), name="fgate_bwd", grid=(1,),
                 in_specs=[pl.BlockSpec((tp, LANES), lambda i: (0, fblk)), _full((1, LANES)), _full((nh, tp))],
                 out_specs=[_full((tp, LANES)), _full((SUBLANES, LANES))],
                 out_shape=[jax.ShapeDtypeStruct((tp, LANES), BF16),
                            jax.ShapeDtypeStruct((SUBLANES, LANES), F32)],
                 scratch_shapes=[pltpu.VMEM((tp, LANES), F32)], semantics=("arbitrary",))


ATT_BQ = 128


def _attn_scores(q, k, cq, ck, mask):
    return jnp.where(mask, _dot_nt(q, k) + (cq - ck), NEG_BIG)


ATT_BUCKET = 3


def _key_buckets(nq):
    return [(lo, min(lo + ATT_BUCKET, nq), min(lo + ATT_BUCKET, nq) * ATT_BQ) for lo in range(0, nq, ATT_BUCKET)]


def _for_bucket(i, nq, fn):
    for lo, hi, klen in _key_buckets(nq):
        pl.when(jnp.logical_and(i >= lo, i < hi))(functools.partial(fn, klen))


def _head_column(c_blk, h):
    lane = lax.broadcasted_iota(jnp.int32, c_blk.shape, 1)
    return jnp.sum(jnp.where(lane == h, c_blk, 0.0), axis=1, keepdims=True)


def _causal_mask(i, klen):
    rows = i * ATT_BQ + lax.broadcasted_iota(jnp.int32, (ATT_BQ, klen), 0)
    return lax.broadcasted_iota(jnp.int32, (ATT_BQ, klen), 1) <= rows


def _attn_fwd(qkv, c, c_t, nh):
    tp = qkv.shape[0]
    att_w = nh * HEAD_DIM
    npair = nh // 2
    bq = ATT_BQ
    nq = tp // bq

    def body(q_ref, k_ref, v_ref, c_ref, ct_ref, o_ref, lse_ref):
        p = pl.program_id(0)
        i = pl.program_id(1)

        def compute(klen):
            mask = _causal_mask(i, klen)
            outs, lses = [], []
            for hh in range(2):
                lo = HEAD_DIM * hh
                h = 2 * p + hh
                s = _attn_scores(q_ref[:, lo:lo + HEAD_DIM], k_ref[:klen, lo:lo + HEAD_DIM],
                                 _head_column(c_ref[...], h), ct_ref[pl.ds(h, 1), :klen], mask)
                m = jnp.max(s, axis=1, keepdims=True)
                e = jnp.exp(s - m)
                l = jnp.sum(e, axis=1, keepdims=True)
                outs.append(_dot(e.astype(BF16), v_ref[:klen, lo:lo + HEAD_DIM]) / l)
                lses.append(jnp.broadcast_to(m + jnp.log(l), (bq, HEAD_DIM)))
            o_ref[...] = jnp.concatenate(outs, axis=1)
            lse_ref[...] = jnp.concatenate(lses, axis=1)

        _for_bucket(i, nq, compute)

    blk = pl.BlockSpec((bq, LANES), lambda p, i: (i, p))
    vm = 4 * _nbytes((tp, LANES), BF16) + 8 * _nbytes((bq, tp), F32)
    return _call(body, (qkv, qkv, qkv, c, c_t), name="attn_fwd", grid=(npair, nq),
                 in_specs=[blk,
                           pl.BlockSpec((tp, LANES), lambda p, i: (0, npair + p)),
                           pl.BlockSpec((tp, LANES), lambda p, i: (0, 2 * npair + p)),
                           pl.BlockSpec((bq, LANES), lambda p, i: (i, 0)), _full((nh, tp))],
                 out_specs=[blk, blk],
                 out_shape=[jax.ShapeDtypeStruct((tp, att_w), F32)] * 2,
                 semantics=("parallel", "parallel"), vmem_bytes=vm)


def _attn_bwd(qkv, c, c_t, lse_b, do, nh):
    tp = qkv.shape[0]
    att_w = nh * HEAD_DIM
    npair = nh // 2
    bq = ATT_BQ
    nq = tp // bq
    scale = 1.0 / math.sqrt(HEAD_DIM)

    def body(q_ref, k_ref, v_ref, c_ref, ct_ref, lse_ref, do_ref, dq_ref, dk_ref, dv_ref, dct_ref, dk_s, dv_s):
        p = pl.program_id(0)
        i = pl.program_id(1)

        @pl.when(i == 0)
        def _():
            dk_s[...] = jnp.zeros_like(dk_s)
            dv_s[...] = jnp.zeros_like(dv_s)

        @pl.when(jnp.logical_and(i == 0, p == 0))
        def _():
            dct_ref[...] = jnp.zeros_like(dct_ref)

        def compute(klen):
            mask = _causal_mask(i, klen)
            dqs, dks, dvs = [], [], []
            for hh in range(2):
                lo = HEAD_DIM * hh
                h = 2 * p + hh
                q = q_ref[:, lo:lo + HEAD_DIM]
                k = k_ref[:klen, lo:lo + HEAD_DIM]
                v = v_ref[:klen, lo:lo + HEAD_DIM]
                s = _attn_scores(q, k, _head_column(c_ref[...], h), ct_ref[pl.ds(h, 1), :klen], mask)
                pr = jnp.exp(s - lse_ref[:, lo:lo + 1])
                doutb = do_ref[:, lo:lo + HEAD_DIM].astype(BF16)
                dp = _dot_nt(doutb, v)
                ds = pr * (dp - jnp.sum(pr * dp, axis=1, keepdims=True))
                dsb = ds.astype(BF16)
                dqs.append(_dot(dsb, k) * scale)
                dks.append(_dot_tn(dsb, q))
                dvs.append(_dot_tn(pr.astype(BF16), doutb))
                dct_ref[pl.ds(h, 1), :klen] = dct_ref[pl.ds(h, 1), :klen] - jnp.sum(ds, axis=0, keepdims=True)
            dq_ref[...] = jnp.concatenate(dqs, axis=1).astype(BF16)
            dk_s[:klen, :] += jnp.concatenate(dks, axis=1)
            dv_s[:klen, :] += jnp.concatenate(dvs, axis=1)

        _for_bucket(i, nq, compute)

        @pl.when(i == nq - 1)
        def _():
            dk_ref[...] = dk_s[...].astype(BF16)
            dv_ref[...] = dv_s[...].astype(BF16)

    blk = pl.BlockSpec((bq, LANES), lambda p, i: (i, p))
    col = pl.BlockSpec((tp, LANES), lambda p, i: (0, p))
    vm = 6 * _nbytes((tp, LANES), BF16) + 2 * _nbytes((tp, LANES), F32) + 12 * _nbytes((bq, tp), F32)
    return _call(body, (qkv, qkv, qkv, c, c_t, lse_b, do), name="attn_bwd", grid=(npair, nq),
                 in_specs=[blk,
                           pl.BlockSpec((tp, LANES), lambda p, i: (0, npair + p)),
                           pl.BlockSpec((tp, LANES), lambda p, i: (0, 2 * npair + p)),
                           pl.BlockSpec((bq, LANES), lambda p, i: (i, 0)), _full((nh, tp)), blk, blk],
                 out_specs=[blk, col, col, _full((nh, tp))],
                 out_shape=[jax.ShapeDtypeStruct((tp, att_w), BF16)] * 3 + [jax.ShapeDtypeStruct((nh, tp), F32)],
                 scratch_shapes=[pltpu.VMEM((tp, LANES), F32)] * 2,
                 semantics=("arbitrary", "arbitrary"), vmem_bytes=vm)


REC_ROWS = 128
HALO = SUBLANES


def _conv_taps(cat):
    taps = []
    for k in range(CONV_WIDTH):
        sh = CONV_WIDTH - 1 - k
        taps.append((pltpu.roll(cat, sh, 0) if sh else cat)[HALO:])
    return taps


def _rec_gates(xc, wa_ref, ba_ref, wx_ref, bx_ref, l_ref):
    xcb = xc.astype(BF16)
    r = _sigmoid(_dot(xcb, wa_ref[...]) + ba_ref[...])
    ig = _sigmoid(_dot(xcb, wx_ref[...]) + bx_ref[...])
    ls = _log_sigmoid(l_ref[...])
    log_a = RG_C * r * ls
    return xcb, r, ig, ls, log_a


def _rec_fwd(proj, xr_blk, yr_blk, rec_w, conv_w, conv_b, wa, ba, wx, bx, lru):
    tp = proj.shape[0]
    w = rec_w
    r_rows = REC_ROWS
    nc = tp // r_rows
    cpb = w // LANES

    def body(xr_ref, yr_ref, cw_ref, cb_ref, wa_ref, ba_ref, wx_ref, bx_ref, l_ref,
             hr_ref, rec_ref, prev_s, carry_s, a_s, u_s):
        i = pl.program_id(0)

        @pl.when(i == 0)
        def _():
            prev_s[...] = jnp.zeros_like(prev_s)
            carry_s[...] = jnp.zeros_like(carry_s)

        x = xr_ref[...]
        taps = _conv_taps(jnp.concatenate([prev_s[...], x], axis=0))
        prev_s[...] = x[r_rows - HALO:]
        xc = cb_ref[...]
        for k in range(CONV_WIDTH):
            xc = xc + cw_ref[k:k + 1, :] * taps[k]
        _, r, ig, ls, log_a = _rec_gates(xc, wa_ref, ba_ref, wx_ref, bx_ref, l_ref)
        a_s[...] = jnp.exp(log_a)
        u_s[...] = jnp.sqrt(-_expm1_nonpos(2.0 * log_a)) * ig * xc

        def tile(j, h):
            r0 = pl.multiple_of(j * SUBLANES, SUBLANES)
            at = a_s[pl.ds(r0, SUBLANES), :]
            ut = u_s[pl.ds(r0, SUBLANES), :]
            out = []
            for rr in range(SUBLANES):
                h = at[rr:rr + 1] * h + ut[rr:rr + 1]
                out.append(h)
            hr_ref[pl.ds(r0, SUBLANES), :] = jnp.concatenate(out, axis=0)
            return h

        carry_s[0:1, :] = lax.fori_loop(0, r_rows // SUBLANES, tile, carry_s[0:1, :])
        g, _ = _gelu_and_grad(yr_ref[...])
        rec_ref[...] = hr_ref[...] * g

    blk = pl.BlockSpec((r_rows, w), lambda i: (i, 0))
    vm = 16 * _nbytes((r_rows, w), F32) + 4 * _nbytes((w, w), BF16)
    return _call(body, (proj, proj, conv_w, conv_b, wa, ba, wx, bx, lru), name="rec_fwd", grid=(nc,),
                 in_specs=[pl.BlockSpec((r_rows, w), lambda i: (i, xr_blk)),
                           pl.BlockSpec((r_rows, w), lambda i: (i, yr_blk)),
                           _full((CONV_WIDTH, w)), _full((1, w)), _full((w, w)), _full((1, w)),
                           _full((w, w)), _full((1, w)), _full((1, w))],
                 out_specs=[blk, blk],
                 out_shape=[jax.ShapeDtypeStruct((tp, w), F32)] * 2,
                 scratch_shapes=[pltpu.VMEM((HALO, w), F32), pltpu.VMEM((SUBLANES, w), F32),
                                 pltpu.VMEM((r_rows, w), F32), pltpu.VMEM((r_rows, w), F32)],
                 semantics=("arbitrary",), vmem_bytes=vm)


def _rec_bwd(proj, xr_blk, yr_blk, rec_w, hr, drec, conv_w, conv_b, wa, ba, wx, bx, lru):
    tp = proj.shape[0]
    w = rec_w
    r_rows = REC_ROWS
    nc = tp // r_rows
    hpc = r_rows // HALO

    def body(xr_ref, xh_ref, yr_ref, hr_ref, hh_ref, drec_ref, cw_ref, cb_ref, wa_ref, ba_ref, wx_ref, bx_ref,
             l_ref, dxr_ref, dyr_ref, dwa_ref, dwx_ref, small_ref, lam_s, a_s, dhr_s, carry_s, next_s):
        i = pl.program_id(0)
        first = (nc - 1 - i) == 0

        @pl.when(i == 0)
        def _():
            carry_s[...] = jnp.zeros_like(carry_s)
            next_s[...] = jnp.zeros_like(next_s)
            dwa_ref[...] = jnp.zeros_like(dwa_ref)
            dwx_ref[...] = jnp.zeros_like(dwx_ref)
            small_ref[...] = jnp.zeros_like(small_ref)

        x = xr_ref[...]
        xprev = jnp.where(first, 0.0, xh_ref[...])
        taps = _conv_taps(jnp.concatenate([xprev, x], axis=0))
        xc = cb_ref[...]
        for k in range(CONV_WIDTH):
            xc = xc + cw_ref[k:k + 1, :] * taps[k]
        xcb, r, ig, ls, log_a = _rec_gates(xc, wa_ref, ba_ref, wx_ref, bx_ref, l_ref)
        a = jnp.exp(log_a)
        a2 = jnp.exp(2.0 * log_a)
        mult = jnp.sqrt(-_expm1_nonpos(2.0 * log_a))
        g, dg = _gelu_and_grad(yr_ref[...])
        hr_v = hr_ref[...]
        drec_v = drec_ref[...]
        dhr_s[...] = drec_v * g
        dyr_ref[...] = (drec_v * hr_v * dg).astype(BF16)
        a_s[...] = a

        def tile(jj, carry):
            r0 = pl.multiple_of((r_rows // SUBLANES - 1 - jj) * SUBLANES, SUBLANES)
            at = a_s[pl.ds(r0, SUBLANES), :]
            dt = dhr_s[pl.ds(r0, SUBLANES), :]
            out = [None] * SUBLANES
            for rr in range(SUBLANES - 1, -1, -1):
                lam = dt[rr:rr + 1] + carry
                out[rr] = lam
                carry = at[rr:rr + 1] * lam
            lam_s[pl.ds(r0, SUBLANES), :] = jnp.concatenate(out, axis=0)
            return carry

        carry_s[0:1, :] = lax.fori_loop(0, r_rows // SUBLANES, tile, carry_s[0:1, :])
        lam = lam_s[...]
        hprev = jnp.where(first, 0.0, hh_ref[...])
        hr_prev = pltpu.roll(jnp.concatenate([hprev, hr_v], axis=0), 1, 0)[HALO:]
        da = lam * hr_prev
        dxc = lam * mult * ig
        di = lam * mult * xc
        dmult = lam * ig * xc
        dlog_a = da * a - dmult * a2 / mult
        dr = dlog_a * (RG_C * ls)
        dls = jnp.sum(dlog_a * (RG_C * r), axis=0, keepdims=True)
        dga = dr * r * (1.0 - r)
        dgx = di * ig * (1.0 - ig)
        dgab = dga.astype(BF16)
        dgxb = dgx.astype(BF16)
        dxc = dxc + _dot_nt(dgab, wa_ref[...]) + _dot_nt(dgxb, wx_ref[...])
        dwa_ref[...] += _dot_tn(xcb, dgab)
        dwx_ref[...] += _dot_tn(xcb, dgxb)
        cat = jnp.concatenate([dxc, next_s[...]], axis=0)
        next_s[...] = dxc[0:HALO]
        dxr = cw_ref[CONV_WIDTH - 1:CONV_WIDTH, :] * dxc
        for k in range(CONV_WIDTH - 1):
            sh = CONV_WIDTH - 1 - k
            dxr = dxr + cw_ref[k:k + 1, :] * pltpu.roll(cat, r_rows + HALO - sh, 0)[:r_rows]
        dxr_ref[...] = dxr.astype(BF16)
        rows = [jnp.sum(dxc * taps[k], axis=0, keepdims=True) for k in range(CONV_WIDTH)]
        rows += [jnp.sum(dxc, axis=0, keepdims=True), jnp.sum(dga, axis=0, keepdims=True),
                 jnp.sum(dgx, axis=0, keepdims=True), dls * _sigmoid(-l_ref[...])]
        small_ref[...] += jnp.concatenate(rows, axis=0)

    def rev(i):
        return nc - 1 - i

    def halo(i):
        return jnp.maximum(rev(i) * hpc - 1, 0)

    blk = pl.BlockSpec((r_rows, w), lambda i: (rev(i), 0))
    vm = 40 * _nbytes((r_rows, w), F32) + 6 * _nbytes((w, w), F32)
    return _call(body, (proj, proj, proj, hr, hr, drec, conv_w, conv_b, wa, ba, wx, bx, lru),
                 name="rec_bwd", grid=(nc,),
                 in_specs=[pl.BlockSpec((r_rows, w), lambda i: (rev(i), xr_blk)),
                           pl.BlockSpec((HALO, w), lambda i: (halo(i), xr_blk)),
                           pl.BlockSpec((r_rows, w), lambda i: (rev(i), yr_blk)),
                           blk,
                           pl.BlockSpec((HALO, w), lambda i: (halo(i), 0)),
                           blk,
                           _full((CONV_WIDTH, w)), _full((1, w)), _full((w, w)), _full((1, w)),
                           _full((w, w)), _full((1, w)), _full((1, w))],
                 out_specs=[blk, blk, _full((w, w)), _full((w, w)), _full((SUBLANES, w))],
                 out_shape=[jax.ShapeDtypeStruct((tp, w), BF16)] * 2
                 + [jax.ShapeDtypeStruct((w, w), F32)] * 2 + [jax.ShapeDtypeStruct((SUBLANES, w), F32)],
                 scratch_shapes=[pltpu.VMEM((r_rows, w), F32)] * 3
                 + [pltpu.VMEM((SUBLANES, w), F32), pltpu.VMEM((HALO, w), F32)],
                 semantics=("arbitrary",), vmem_bytes=vm)


ROW_TARGET = 544


def _mixer_out(attn, rec, g_a, g_r, w_out, h, g_next):
    tp, d = h.shape
    aw, rw = attn.shape[1], rec.shape[1]
    kc = d // N_CHIPS
    tm = _divisor_tile(tp, 16, ROW_TARGET)

    def body(a_ref, r_ref, ga_ref, gr_ref, w_ref, h_ref, gn_ref, h1_ref, z_ref, mix_ref):
        a = a_ref[...]
        r = r_ref[...]
        mix = jnp.concatenate([a * _rstd(a) * ga_ref[...], r * _rstd(r) * gr_ref[...]], axis=1).astype(BF16)
        mix_ref[...] = mix
        h1 = h_ref[...]
        for j in range(N_CHIPS):
            h1 = h1 + _dot(mix[:, j * kc:(j + 1) * kc], w_ref[j])
        h1_ref[...] = h1
        z_ref[...] = (h1 * _rstd(h1) * gn_ref[...]).astype(BF16)

    row = lambda wd: pl.BlockSpec((tm, wd), lambda i: (i, 0))
    vm = 2 * _nbytes((d, d), BF16) + 12 * _nbytes((tm, d), F32)
    return _call(body, (attn, rec, g_a, g_r, w_out, h, g_next), name="mixer_out", grid=(tp // tm,),
                 in_specs=[row(aw), row(rw), _full((1, aw)), _full((1, rw)), _full(w_out.shape), row(d),
                           _full((1, d))],
                 out_specs=[row(d), row(d), row(d)],
                 out_shape=[jax.ShapeDtypeStruct((tp, d), F32), jax.ShapeDtypeStruct((tp, d), BF16),
                            jax.ShapeDtypeStruct((tp, d), BF16)],
                 semantics=("parallel",), vmem_bytes=vm)


def _mixer_bwd(dh_b, w_out, attn, rec, g_a, g_r):
    tp, d = dh_b.shape
    aw, rw = attn.shape[1], rec.shape[1]
    tm = _divisor_tile(tp, 16, ROW_TARGET)

    def body(dh_ref, w_ref, a_ref, r_ref, ga_ref, gr_ref, da_ref, dr_ref, dg_ref):
        @pl.when(pl.program_id(0) == 0)
        def _():
            dg_ref[...] = jnp.zeros_like(dg_ref)

        dh = dh_ref[...]
        dmix = jnp.concatenate([_dot_nt(dh, w_ref[j]) for j in range(N_CHIPS)], axis=1)
        da, dga = _rms_bwd(dmix[:, :aw], a_ref[...], ga_ref[...])
        dr, dgr = _rms_bwd(dmix[:, aw:], r_ref[...], gr_ref[...])
        da_ref[...] = da
        dr_ref[...] = dr
        dg_ref[...] += jnp.broadcast_to(jnp.concatenate([dga, dgr], axis=1), (SUBLANES, d))

    row = lambda wd: pl.BlockSpec((tm, wd), lambda i: (i, 0))
    vm = 2 * _nbytes((d, d), BF16) + 12 * _nbytes((tm, d), F32)
    return _call(body, (dh_b, w_out, attn, rec, g_a, g_r), name="mixer_bwd", grid=(tp // tm,),
                 in_specs=[row(d), _full(w_out.shape), row(aw), row(rw), _full((1, aw)), _full((1, rw))],
                 out_specs=[row(aw), row(rw), _full((SUBLANES, d))],
                 out_shape=[jax.ShapeDtypeStruct((tp, aw), F32), jax.ShapeDtypeStruct((tp, rw), F32),
                            jax.ShapeDtypeStruct((SUBLANES, d), F32)],
                 semantics=("arbitrary",), vmem_bytes=vm)


def _mlp_up(z, w_up):
    tp, d = z.shape
    fc = w_up.shape[2]
    ff = N_CHIPS * fc
    tn = _divisor_tile(fc, LANES, 512)
    per = fc // tn

    def body(z_ref, w_ref, act_ref, up_ref):
        up = _dot(z_ref[...], w_ref[...])
        r = jnp.maximum(up, 0.0)
        act_ref[...] = (r * r).astype(BF16)
        up_ref[...] = up.astype(BF16)

    col = pl.BlockSpec((tp, tn), lambda j: (0, j))
    vm = 2 * _nbytes((tp, d), BF16) + 2 * _nbytes((d, tn), BF16) + 8 * _nbytes((tp, tn), F32)
    return _call(body, (z, w_up), name="mlp_up", grid=(ff // tn,),
                 in_specs=[_full((tp, d)), pl.BlockSpec((None, d, tn), lambda j: (j // per, 0, j % per))],
                 out_specs=[col, col],
                 out_shape=[jax.ShapeDtypeStruct((tp, ff), BF16)] * 2,
                 semantics=("parallel",), vmem_bytes=vm)


def _mlp_down(act, w_down, h, g_next):
    tp, d = h.shape
    ff = act.shape[1]
    fc = ff // N_CHIPS
    tm = _divisor_tile(tp, 16, ROW_TARGET)

    def body(a_ref, w_ref, h_ref, gn_ref, h2_ref, z_ref):
        h2 = h_ref[...]
        for j in range(N_CHIPS):
            h2 = h2 + _dot(a_ref[:, j * fc:(j + 1) * fc], w_ref[j])
        h2_ref[...] = h2
        z_ref[...] = (h2 * _rstd(h2) * gn_ref[...]).astype(BF16)

    row = lambda wd: pl.BlockSpec((tm, wd), lambda i: (i, 0))
    vm = 2 * _nbytes((ff, d), BF16) + 2 * _nbytes((tm, ff), BF16) + 10 * _nbytes((tm, d), F32)
    return _call(body, (act, w_down, h, g_next), name="mlp_down", grid=(tp // tm,),
                 in_specs=[row(ff), _full(w_down.shape), row(d), _full((1, d))],
                 out_specs=[row(d), row(d)],
                 out_shape=[jax.ShapeDtypeStruct((tp, d), F32), jax.ShapeDtypeStruct((tp, d), BF16)],
                 semantics=("parallel",), vmem_bytes=vm)


def _loss_bwd(h, g, target, n_real):
    tp, d = h.shape
    tm = _divisor_tile(tp, 16, ROW_TARGET)

    def body(h_ref, g_ref, t_ref, dh_ref, dhb_ref, dg_ref, loss_ref):
        i = pl.program_id(0)

        @pl.when(i == 0)
        def _():
            dg_ref[...] = jnp.zeros_like(dg_ref)
            loss_ref[...] = jnp.zeros_like(loss_ref)

        x = h_ref[...]
        gv = g_ref[...]
        rowi = i * tm + lax.broadcasted_iota(jnp.int32, (tm, 1), 0)
        real = jnp.logical_and(rowi >= N_META, rowi < N_META + n_real)
        err = jnp.where(real, x * _rstd(x) * gv - t_ref[...], 0.0)
        loss_ref[...] += 0.5 * jnp.sum(jnp.mean(err * err, axis=-1, keepdims=True))
        dx, dgp = _rms_bwd(err * (1.0 / d), x, gv)
        dh_ref[...] = dx
        dhb_ref[...] = dx.astype(BF16)
        dg_ref[...] += jnp.broadcast_to(dgp, (SUBLANES, d))

    row = pl.BlockSpec((tm, d), lambda i: (i, 0))
    return _call(body, (h, g, target), name="loss_bwd", grid=(tp // tm,),
                 in_specs=[row, _full((1, d)), row],
                 out_specs=[row, row, _full((SUBLANES, d)), _full((SUBLANES, LANES))],
                 out_shape=[jax.ShapeDtypeStruct((tp, d), F32), jax.ShapeDtypeStruct((tp, d), BF16),
                            jax.ShapeDtypeStruct((SUBLANES, d), F32), jax.ShapeDtypeStruct((SUBLANES, LANES), F32)],
                 semantics=("arbitrary",), vmem_bytes=16 * _nbytes((tm, d), F32))


def _mlp_dup(dh_b, w_down, up):
    tp, d = dh_b.shape
    fc = w_down.shape[1]
    ff = N_CHIPS * fc
    tn = _divisor_tile(fc, LANES, 512)
    per = fc // tn

    def body(dh_ref, w_ref, up_ref, dup_ref):
        dact = _dot_nt(dh_ref[...], w_ref[...])
        dup_ref[...] = (dact * (2.0 * jnp.maximum(up_ref[...].astype(F32), 0.0))).astype(BF16)

    col = pl.BlockSpec((tp, tn), lambda j: (0, j))
    vm = 2 * _nbytes((tp, d), BF16) + 2 * _nbytes((tn, d), BF16) + 8 * _nbytes((tp, tn), F32)
    return _call(body, (dh_b, w_down, up), name="mlp_dup", grid=(ff // tn,),
                 in_specs=[_full((tp, d)), pl.BlockSpec((None, tn, d), lambda j: (j // per, j % per, 0)),
                           col],
                 out_specs=col, out_shape=jax.ShapeDtypeStruct((tp, ff), BF16),
                 semantics=("parallel",), vmem_bytes=vm)


def _grad_w(a, b, cols_to_chips=False):
    tp, k = a.shape
    n = b.shape[1]
    tk = _divisor_tile(k, LANES, 1024)
    nc = n // N_CHIPS if cols_to_chips else n
    tn = _divisor_tile(nc, LANES, 512)
    per = nc // tn

    def body(a_ref, b_ref, o_ref):
        o_ref[...] = _dot_tn(a_ref[...], b_ref[...]).astype(BF16)

    if cols_to_chips:
        out_spec = pl.BlockSpec((None, tk, tn), lambda i, j: (j // per, i, j % per))
        out_shape = jax.ShapeDtypeStruct((N_CHIPS, k, nc), BF16)
    else:
        out_spec = pl.BlockSpec((tk, tn), lambda i, j: (i, j))
        out_shape = jax.ShapeDtypeStruct((k, n), BF16)
    vm = 2 * _nbytes((tp, tk), BF16) + 2 * _nbytes((tp, tn), BF16) + 6 * _nbytes((tk, tn), F32) \
        + 2 * _nbytes((tp, tk), F32)
    return _call(body, (a, b), name="grad_w", grid=(k // tk, n // tn),
                 in_specs=[pl.BlockSpec((tp, tk), lambda i, j: (0, i)), pl.BlockSpec((tp, tn), lambda i, j: (0, j))],
                 out_specs=out_spec, out_shape=out_shape,
                 semantics=("parallel", "parallel"), vmem_bytes=vm)


def _dx_norm_bwd(pieces, w, w_spec, w_piece, h, g, dres):
    tp, d = h.shape
    tm = _divisor_tile(tp, 16, ROW_TARGET)
    n = len(pieces)

    def body(*refs):
        dy_refs = refs[:n]
        w_ref, h_ref, g_ref, dres_ref, dh_ref, dhb_ref, dg_ref = refs[n:]

        @pl.when(pl.program_id(0) == 0)
        def _():
            dg_ref[...] = jnp.zeros_like(dg_ref)

        dz = _dot_nt(dy_refs[0][...], w_piece(w_ref, 0))
        for i in range(1, n):
            dz = dz + _dot_nt(dy_refs[i][...], w_piece(w_ref, i))
        dx, dgp = _rms_bwd(dz, h_ref[...], g_ref[...])
        dh = dres_ref[...] + dx
        dh_ref[...] = dh
        dhb_ref[...] = dh.astype(BF16)
        dg_ref[...] += jnp.broadcast_to(dgp, (SUBLANES, d))

    row = lambda wd: pl.BlockSpec((tm, wd), lambda i: (i, 0))
    kk = sum(wd for _, _, wd in pieces)
    vm = 2 * _nbytes((d, kk), BF16) + 2 * _nbytes((tm, kk), BF16) + 14 * _nbytes((tm, d), F32)
    piece_specs = [pl.BlockSpec((tm, wd), functools.partial(lambda i, cb: (i, cb), cb=cb)) for _, cb, wd in pieces]
    return _call(body, tuple(a for a, _, _ in pieces) + (w, h, g, dres), name="dx_norm_bwd", grid=(tp // tm,),
                 in_specs=piece_specs + [w_spec, row(d), _full((1, d)), row(d)],
                 out_specs=[row(d), row(d), _full((SUBLANES, d))],
                 out_shape=[jax.ShapeDtypeStruct((tp, d), F32), jax.ShapeDtypeStruct((tp, d), BF16),
                            jax.ShapeDtypeStruct((SUBLANES, d), F32)],
                 semantics=("arbitrary",), vmem_bytes=vm)


def _block_diag(wg):
    nb, b, _ = wg.shape
    eye = jnp.eye(nb, dtype=wg.dtype)
    return (eye[:, None, :, None] * wg[:, :, None, :]).reshape(nb * b, nb * b)


def _diag_blocks(dense, nb):
    b = dense.shape[0] // nb
    d4 = dense.reshape(nb, b, nb, b)
    return jnp.stack([d4[i, :, i, :] for i in range(nb)])


def _row(v):
    return v.reshape(1, -1)


def _forward_layer(l, h, z, p, fetch, g_next):
    d = h.shape[1]
    att_w = d // 2
    rec_w = d - att_w
    nh = att_w // HEAD_DIM
    xr_blk = 3 * att_w // rec_w
    wa_d = _block_diag(p["w_gate_a"][l]).astype(BF16)
    wx_d = _block_diag(p["w_gate_x"][l]).astype(BF16)
    b_f_pad = jnp.zeros((1, LANES), F32).at[0, :nh].set(p["b_f"][l])
    big = dict(w_in_big=_pack_w_in(jnp.moveaxis(fetch("w_in", h), 0, 1).reshape(d, -1), att_w, rec_w, nh))
    proj, qkv = _proj(z, big["w_in_big"], att_w)
    c, c_t = _fgate_fwd(proj, b_f_pad, nh)
    attn, lse_b = _attn_fwd(qkv, c, c_t, nh)
    hr, rec = _rec_fwd(proj, xr_blk, xr_blk + 1, rec_w, p["conv_w"][l], _row(p["conv_b"][l]), wa_d,
                       _row(p["b_gate_a"][l]), wx_d, _row(p["b_gate_x"][l]), _row(p["lru_L"][l]))
    big["w_out"] = fetch("w_out", rec)
    h1, z2, mix = _mixer_out(attn, rec, _row(p["attn_out_g"][l]), _row(p["rec_out_g"][l]),
                             big["w_out"], h, _row(p["mlp_norm_g"][l]))
    big["w_up"] = fetch("w_up", h1)
    act, up = _mlp_up(z2, big["w_up"])
    big["w_down"] = fetch("w_down", act)
    h2, z_next = _mlp_down(act, big["w_down"], h1, _row(g_next))
    saved = dict(h0=h, z1=z, proj=proj, qkv=qkv, c=c, c_t=c_t, attn=attn, lse_b=lse_b, hr=hr, rec=rec, h1=h1,
                 z2=z2, mix=mix, act=act, up=up, wa_d=wa_d, wx_d=wx_d, b_f_pad=b_f_pad, big=big)
    return h2, z_next, saved


def _backward_mlp(l, dh, dh_b, sv, p, tok):
    w_up, w_down = sv["big"]["w_up"], sv["big"]["w_down"]
    fc = w_up.shape[2]
    dup = _mlp_dup(dh_b, w_down, sv["up"])
    g_down = _grad_w(sv["act"], dh_b)
    g_up = _grad_w(sv["z2"], dup, cols_to_chips=True)
    dh, dh_b, dg2 = _dx_norm_bwd([(dup, j, fc) for j in range(N_CHIPS)], w_up, _full(w_up.shape),
                                 lambda w_ref, j: w_ref[j], sv["h1"], _row(p["mlp_norm_g"][l] + tok), dh)
    big = dict(w_down=g_down.reshape((N_CHIPS, -1) + g_down.shape[1:]), w_up=g_up)
    return dh, dh_b, big, dict(mlp_norm_g=dg2[0])


def _backward_mixer(l, dh, dh_b, sv, p, tok):
    d = dh.shape[1]
    att_w = d // 2
    rec_w = d - att_w
    nh = att_w // HEAD_DIM
    xr_blk = 3 * att_w // rec_w
    small = {}
    g_out = _grad_w(sv["mix"], dh_b)
    dattn, drec, dg_mix = _mixer_bwd(dh_b, sv["big"]["w_out"], sv["attn"], sv["rec"],
                                     _row(p["attn_out_g"][l] + tok), _row(p["rec_out_g"][l]))
    small["attn_out_g"] = dg_mix[0, :att_w]
    small["rec_out_g"] = dg_mix[0, att_w:]
    dxr, dyr, dwa, dwx, sm = _rec_bwd(
        sv["proj"], xr_blk, xr_blk + 1, rec_w, sv["hr"], drec, p["conv_w"][l], _row(p["conv_b"][l]), sv["wa_d"],
        _row(p["b_gate_a"][l]), sv["wx_d"], _row(p["b_gate_x"][l]), _row(p["lru_L"][l]))
    small.update(conv_w=sm[:CONV_WIDTH], conv_b=sm[4], b_gate_a=sm[5], b_gate_x=sm[6], lru_L=sm[7],
                 w_gate_a=_diag_blocks(dwa, N_REC_BLOCKS), w_gate_x=_diag_blocks(dwx, N_REC_BLOCKS))
    dq, dk, dv, dct = _attn_bwd(sv["qkv"], sv["c"], sv["c_t"], sv["lse_b"], dattn, nh)
    df, db_f = _fgate_bwd(sv["proj"], sv["b_f_pad"], dct)
    small["b_f"] = db_f[0, :nh]
    pieces = [dq, dk, dv, dxr, dyr, df]
    offs = [0, att_w, 2 * att_w, 3 * att_w, 3 * att_w + rec_w, 3 * att_w + 2 * rec_w]
    gq, gk, gv, gxr, gyr, gf = [_grad_w(sv["z1"], pc) for pc in pieces]
    g_in = jnp.concatenate([gq, gk, gv, gf[:, :nh], gxr, gyr], axis=1)
    w_big = sv["big"]["w_in_big"]
    widths = [pc.shape[1] for pc in pieces]
    dh, dh_b, dg1 = _dx_norm_bwd(
        [(pc, 0, wd) for pc, wd in zip(pieces, widths)], w_big, _full(w_big.shape),
        lambda w_ref, i: w_ref[:, offs[i]:offs[i] + widths[i]], sv["h0"], _row(p["attn_norm_g"][l]), dh)
    small["attn_norm_g"] = dg1[0]
    big = dict(w_in=jnp.moveaxis(g_in.reshape(d, N_CHIPS, -1), 1, 0),
               w_out=g_out.reshape((N_CHIPS, -1) + g_out.shape[1:]))
    return dh, dh_b, big, small


def _pack_w_in(w_in, att_w, rec_w, nh):
    qkv = w_in[..., :3 * att_w]
    f = w_in[..., 3 * att_w:3 * att_w + nh]
    xy = w_in[..., 3 * att_w + nh:]
    fpad = jnp.zeros(f.shape[:-1] + (LANES - nh,), w_in.dtype)
    return jnp.concatenate([qkv, xy, f, fpad], axis=-1)


ANY = pl.BlockSpec(memory_space=pl.ANY)


def _coords():
    return lax.axis_index("x"), lax.axis_index("y"), lax.axis_index("c")


def _other_chips(x, y):
    return [(1 - x, y), (x, 1 - y), (1 - x, 1 - y)]


def _remote(src, dst, send_sems, recv_sems, k, to):
    return pltpu.make_async_remote_copy(src_ref=src, dst_ref=dst, send_sem=send_sems.at[k],
                                        recv_sem=recv_sems.at[k], device_id=to, device_id_type=MESH)


def _all_gather_chips(shards):
    n = len(shards)
    per = 6

    def body(*refs):
        ins, outs = refs[:n], refs[n:2 * n]
        send_sems, recv_sems, local_sems = refs[2 * n:]
        x, y, c = _coords()
        me = 2 * x + y
        sibling = (x, y, 1 - c)
        chips = _other_chips(x, y)
        local = [pltpu.make_async_copy(ins[t], outs[t].at[me], local_sems.at[t]) for t in range(n)]
        for cp in local:
            cp.start()
        sends = []
        for t in range(n):
            for j, (px, py) in enumerate(chips):
                cp = _remote(ins[t].at[c], outs[t].at[me, c], send_sems, recv_sems, per * t + j, (px, py, c))
                cp.start()
                sends.append(cp)
        for t in range(n):
            for j, (px, py) in enumerate(chips):
                landed = outs[t].at[2 * px + py, c]
                _remote(landed, landed, send_sems, recv_sems, per * t + j, (px, py, c)).wait_recv()
                cp = _remote(landed, landed, send_sems, recv_sems, per * t + 3 + j, sibling)
                cp.start()
                sends.append(cp)
        for t in range(n):
            for j, (px, py) in enumerate(chips):
                passed = outs[t].at[2 * px + py, 1 - c]
                _remote(passed, passed, send_sems, recv_sems, per * t + 3 + j, sibling).wait_recv()
        for cp in sends:
            cp.wait_send()
        for cp in local:
            cp.wait()

    return _call(body, tuple(shards), name="all_gather_chips",
                 in_specs=[ANY] * n, out_specs=[ANY] * n,
                 out_shape=[jax.ShapeDtypeStruct((N_CHIPS,) + s.shape, s.dtype) for s in shards],
                 scratch_shapes=[pltpu.SemaphoreType.DMA((per * n,)), pltpu.SemaphoreType.DMA((per * n,)),
                                 pltpu.SemaphoreType.DMA((n,))])


HBM = pl.BlockSpec(memory_space=pltpu.HBM)
SEM = pl.BlockSpec(memory_space=pltpu.SEMAPHORE)
DATAFLOW = pltpu.SideEffectType.DATAFLOW_SIDE_EFFECTING


def _in_hbm(a):
    return pltpu.with_memory_space_constraint(a, pltpu.HBM)


def _push_start(srcs, lands, own_slab, name):
    n = len(srcs)
    same = own_slab and all(s is ld for s, ld in zip(srcs, lands))
    n_in = n if same else 2 * n

    def body(*refs):
        src_refs = refs[:n]
        land_refs = src_refs if same else refs[n:2 * n]
        send_sems, recv_sems = refs[n_in], refs[n_in + 1]
        token = refs[-1]
        x, y, c = _coords()
        me = 2 * x + y
        for t in range(n):
            for px, py in _other_chips(x, y):
                src = src_refs[t].at[me if own_slab else 2 * px + py]
                _remote(src, land_refs[t].at[me], send_sems, recv_sems, t, (px, py, c)).start()
        token[...] = jnp.zeros_like(token)

    operands = tuple(srcs) if same else tuple(srcs) + tuple(lands)
    res = _call(
        body, [_in_hbm(a) for a in operands], name=name,
        out_shape=(pltpu.SemaphoreType.DMA((n,)), pltpu.SemaphoreType.DMA((n,)))
        + tuple(pltpu.HBM(a.shape, a.dtype) for a in operands) + (jax.ShapeDtypeStruct((SUBLANES, LANES), F32),),
        in_specs=[HBM] * n_in, out_specs=(SEM, SEM) + (HBM,) * n_in + (pl.BlockSpec(memory_space=pltpu.VMEM),),
        input_output_aliases={i: 2 + i for i in range(n_in)}, side_effects=DATAFLOW, hbm_results=False)
    send_sems, recv_sems, token = res[0], res[1], res[-1]
    srcs_thru = res[2:2 + n]
    lands_thru = srcs_thru if same else res[2 + n:2 + 2 * n]
    return send_sems, recv_sems, srcs_thru, lands_thru, token


def _push_wait(send_sems, recv_sems, ids, srcs, lands, after, name):
    n = len(lands)
    same = all(s is ld for s, ld in zip(srcs, lands))
    n_in = n if same else 2 * n

    def body(*refs):
        land_refs = refs[:n] if same else refs[n:2 * n]
        send_sems, recv_sems = refs[n_in], refs[n_in + 1]
        x, y, c = _coords()
        for t in range(n):
            three = land_refs[t].at[pl.ds(0, N_CHIPS - 1)]
            arrivals = _remote(three, three, send_sems, recv_sems, ids[t], (x, y, c))
            arrivals.wait_send()
            arrivals.wait_recv()

    operands = tuple(lands) if same else tuple(srcs) + tuple(lands)
    res = _call(
        body, operands + (send_sems, recv_sems, after), name=name,
        out_shape=tuple(pltpu.HBM(a.shape, a.dtype) for a in operands),
        in_specs=[HBM] * n_in + [SEM, SEM, ANY], out_specs=(HBM,) * n_in,
        input_output_aliases={i: i for i in range(n_in)}, side_effects=DATAFLOW)
    return list(res) if same else (list(res[:n]), list(res[n:]))


def _sum_partials(part, landed, chip):
    _, rows, cols = part.shape
    br = _divisor_tile(rows, 16, ELEM_ROWS)

    def body(chip_ref, own_ref, a_ref, b_ref, c_ref, o_ref):
        o_ref[...] = ((own_ref[...].astype(F32) + a_ref[...].astype(F32)) + b_ref[...].astype(F32)) \
            + c_ref[...].astype(F32)

    def other(k):
        return pl.BlockSpec((None, br, cols), lambda i, ch: (jnp.where(ch[0] <= k, k + 1, k), i, 0))

    spec = pltpu.PrefetchScalarGridSpec(
        num_scalar_prefetch=1, grid=(rows // br,),
        in_specs=[pl.BlockSpec((None, br, cols), lambda i, ch: (ch[0], i, 0)), other(0), other(1), other(2)],
        out_specs=pl.BlockSpec((br, cols), lambda i, ch: (i, 0)))
    return _call(body, (chip, part, landed, landed, landed), name="sum_partials", grid_spec=spec,
                 out_shape=jax.ShapeDtypeStruct((rows, cols), F32), semantics=("parallel",))


def _cast_to_slab(w, l, chip):
    _, rows, cols = w.shape
    br = _divisor_tile(rows, 16, ELEM_ROWS)

    def body(chip_ref, w_ref, o_ref):
        o_ref[...] = w_ref[...].astype(BF16)

    spec = pltpu.PrefetchScalarGridSpec(
        num_scalar_prefetch=1, grid=(rows // br,),
        in_specs=[pl.BlockSpec((None, br, cols), lambda i, ch: (l, i, 0))],
        out_specs=pl.BlockSpec((None, br, cols), lambda i, ch: (ch[0], i, 0)))
    return _call(body, (chip, w), name="cast_to_slab", grid_spec=spec,
                 out_shape=jax.ShapeDtypeStruct((N_CHIPS, rows, cols), BF16), semantics=("parallel",))


def _swap_with_sibling(arrs):
    n = len(arrs)

    def body(*refs):
        ins, outs = refs[:n], refs[n:2 * n]
        send_sems, recv_sems = refs[2 * n:]
        x, y, c = _coords()
        cps = [_remote(ins[t], outs[t], send_sems, recv_sems, t, (x, y, 1 - c)) for t in range(n)]
        for cp in cps:
            cp.start()
        for cp in cps:
            cp.wait_recv()
        for cp in cps:
            cp.wait_send()

    return _call(body, tuple(arrs), name="swap_with_sibling",
                 in_specs=[ANY] * n, out_specs=[ANY] * n,
                 out_shape=[jax.ShapeDtypeStruct(s.shape, s.dtype) for s in arrs],
                 scratch_shapes=[pltpu.SemaphoreType.DMA((n,)), pltpu.SemaphoreType.DMA((n,))])


def _all_gather_devices(buf):
    per = N_DEV - 1

    def body(in_ref, out_ref, send_sems, recv_sems, local_sem):
        x, y, c = _coords()
        me = 4 * x + 2 * y + c
        local = pltpu.make_async_copy(in_ref, out_ref.at[me], local_sem)
        local.start()
        peers = []
        for k in range(1, N_DEV):
            fx, fy, fc = (k >> 2) & 1, (k >> 1) & 1, k & 1
            peers.append((x ^ fx, y ^ fy, c ^ fc))
        sends = [_remote(in_ref, out_ref.at[me], send_sems, recv_sems, k, peer) for k, peer in enumerate(peers)]
        for cp in sends:
            cp.start()
        for k, (px, py, pc) in enumerate(peers):
            landed = out_ref.at[4 * px + 2 * py + pc]
            _remote(landed, landed, send_sems, recv_sems, k, (px, py, pc)).wait_recv()
        for cp in sends:
            cp.wait_send()
        local.wait()

    return _call(body, (buf,), name="all_gather_devices", in_specs=[ANY], out_specs=ANY,
                 out_shape=jax.ShapeDtypeStruct((N_DEV,) + buf.shape, buf.dtype),
                 scratch_shapes=[pltpu.SemaphoreType.DMA((per,)), pltpu.SemaphoreType.DMA((per,)),
                                 pltpu.SemaphoreType.DMA(())])


ELEM_ROWS = 256


def _sum_slabs(r):
    n, rows, cols = r.shape
    br = _divisor_tile(rows, 16, ELEM_ROWS)

    def body(r_ref, o_ref):
        acc = r_ref[0].astype(F32)
        for j in range(1, n):
            acc = acc + r_ref[j].astype(F32)
        o_ref[...] = acc

    return _call(body, (r,), name="sum_slabs", grid=(rows // br,),
                 in_specs=[pl.BlockSpec((n, br, cols), lambda i: (0, i, 0))],
                 out_specs=pl.BlockSpec((br, cols), lambda i: (i, 0)),
                 out_shape=jax.ShapeDtypeStruct((rows, cols), F32), semantics=("parallel",))


def _adamw_math(w, g, m, v):
    c1 = 1.0 - ADAM_B1 ** ADAM_STEP
    c2 = 1.0 - ADAM_B2 ** ADAM_STEP
    nm = ADAM_B1 * m + (1.0 - ADAM_B1) * g
    nv = ADAM_B2 * v + (1.0 - ADAM_B2) * (g * g)
    delta = -ADAM_LR * ((nm / c1) / (jnp.sqrt(nv / c2) + ADAM_EPS) + ADAM_WD * w)
    return delta, nm, nv


def _adamw(w, g, m, v):
    rows, cols = w.shape
    br = _divisor_tile(rows, 8, ELEM_ROWS)

    def body(w_ref, g_ref, m_ref, v_ref, d_ref, nm_ref, nv_ref):
        d_ref[...], nm_ref[...], nv_ref[...] = _adamw_math(w_ref[...], g_ref[...], m_ref[...], v_ref[...])

    blk = pl.BlockSpec((br, cols), lambda i: (i, 0))
    return _call(body, (w, g, m, v), name="adamw", grid=(rows // br,),
                 in_specs=[blk] * 4, out_specs=[blk] * 3,
                 out_shape=[jax.ShapeDtypeStruct((rows, cols), F32)] * 3, semantics=("parallel",))


def _adamw_layer(w, m, v, l, g_mine, g_theirs, prev):
    _, rows, cols = w.shape
    br = _divisor_tile(rows, 8, ELEM_ROWS)

    def body(w_ref, m_ref, v_ref, ga_ref, gb_ref, *rest):
        g_ref, d_ref, nm_ref, nv_ref = rest[4:]
        g = ga_ref[...] + gb_ref[...]
        g_ref[...] = g
        d_ref[...], nm_ref[...], nv_ref[...] = _adamw_math(w_ref[...], g, m_ref[...], v_ref[...])

    slot = pl.BlockSpec((None, br, cols), lambda i: (l, i, 0))
    blk = pl.BlockSpec((br, cols), lambda i: (i, 0))
    return _call(body, (w, m, v, g_mine, g_theirs) + tuple(prev), name="adamw_layer", grid=(rows // br,),
                 in_specs=[slot] * 3 + [blk] * 2 + [ANY] * 4, out_specs=[slot] * 4,
                 out_shape=[jax.ShapeDtypeStruct(w.shape, F32)] * 4,
                 input_output_aliases={5: 0, 6: 1, 7: 2, 8: 3}, semantics=("parallel",))


BIG = ("w_in", "w_out", "w_up", "w_down")
WEIGHTS = ("meta", "attn_norm_g", "w_in", "b_f", "conv_w", "conv_b", "w_gate_a", "b_gate_a", "w_gate_x",
           "b_gate_x", "lru_L", "attn_out_g", "rec_out_g", "w_out", "mlp_norm_g", "w_up", "w_down", "final_g")
SMALL = tuple(k for k in WEIGHTS if k not in BIG)
COL_SHARDED_SMALL = ("meta", "conv_w")
PACK_UNIT = ELEM_ROWS * LANES


def _pack(arrs):
    flat = jnp.concatenate([a.reshape(-1) for a in arrs])
    total = -(-flat.shape[0] // PACK_UNIT) * PACK_UNIT
    return jnp.pad(flat, (0, total - flat.shape[0])).reshape(-1, LANES)


def _unpack(buf, shapes):
    flat = buf.reshape(-1)
    out, off = [], 0
    for s in shapes:
        size = math.prod(s)
        out.append(flat[off:off + size].reshape(s))
        off += size
    return out


def _halves(a):
    return a.reshape((2, a.shape[0] // 2) + a.shape[1:])


def _cols_from_chips(g):
    return jnp.moveaxis(g, 0, -2).reshape(g.shape[1:-1] + (N_CHIPS * g.shape[-1],))


def kernel(x, meta, attn_norm_g, w_in, b_f, conv_w, conv_b, w_gate_a, b_gate_a, w_gate_x, b_gate_x, lru_L, attn_out_g, rec_out_g, w_out, mlp_norm_g, w_up, w_down, final_g, loss_target, m_meta, m_attn_norm_g, m_w_in, m_b_f, m_conv_w, m_conv_b, m_w_gate_a, m_b_gate_a, m_w_gate_x, m_b_gate_x, m_lru_L, m_attn_out_g, m_rec_out_g, m_w_out, m_mlp_norm_g, m_w_up, m_w_down, m_final_g, v_meta, v_attn_norm_g, v_w_in, v_b_f, v_conv_w, v_conv_b, v_w_gate_a, v_b_gate_a, v_w_gate_x, v_b_gate_x, v_lru_L, v_attn_out_g, v_rec_out_g, v_w_out, v_mlp_norm_g, v_w_up, v_w_down, v_final_g):
    w = dict(meta=meta, attn_norm_g=attn_norm_g, w_in=w_in, b_f=b_f, conv_w=conv_w, conv_b=conv_b,
             w_gate_a=w_gate_a, b_gate_a=b_gate_a, w_gate_x=w_gate_x, b_gate_x=b_gate_x, lru_L=lru_L,
             attn_out_g=attn_out_g, rec_out_g=rec_out_g, w_out=w_out, mlp_norm_g=mlp_norm_g, w_up=w_up,
             w_down=w_down, final_g=final_g)
    m = dict(meta=m_meta, attn_norm_g=m_attn_norm_g, w_in=m_w_in, b_f=m_b_f, conv_w=m_conv_w, conv_b=m_conv_b,
             w_gate_a=m_w_gate_a, b_gate_a=m_b_gate_a, w_gate_x=m_w_gate_x, b_gate_x=m_b_gate_x, lru_L=m_lru_L,
             attn_out_g=m_attn_out_g, rec_out_g=m_rec_out_g, w_out=m_w_out, mlp_norm_g=m_mlp_norm_g,
             w_up=m_w_up, w_down=m_w_down, final_g=m_final_g)
    v = dict(meta=v_meta, attn_norm_g=v_attn_norm_g, w_in=v_w_in, b_f=v_b_f, conv_w=v_conv_w, conv_b=v_conv_b,
             w_gate_a=v_w_gate_a, b_gate_a=v_b_gate_a, w_gate_x=v_w_gate_x, b_gate_x=v_b_gate_x, lru_L=v_lru_L,
             attn_out_g=v_attn_out_g, rec_out_g=v_rec_out_g, w_out=v_w_out, mlp_norm_g=v_mlp_norm_g,
             w_up=v_w_up, w_down=v_w_down, final_g=v_final_g)
    s_len, d = x.shape[1], x.shape[2]
    depth = w_in.shape[0]
    att_w = d // 2
    rec_w = d - att_w
    nh = att_w // HEAD_DIM
    chip = 2 * lax.axis_index("x") + lax.axis_index("y")

    g_conv, g_meta = [g.reshape((N_CHIPS, g.shape[1] * g.shape[2]) + g.shape[3:])
                      for g in _all_gather_chips([_halves(w["conv_w"]), _halves(w["meta"])])]
    p = dict(w)
    p["conv_w"] = _cols_from_chips(g_conv)
    meta_full = jnp.moveaxis(g_meta, 0, 1).reshape(N_META, d)

    chip1 = chip.reshape(1).astype(jnp.int32)
    pushes, tokens = [], []
    for l in range(depth):
        slabs = [_cast_to_slab(w[k], l, chip1) for k in BIG]
        send_sems, recv_sems, _, lands, token = _push_start(slabs, slabs, True, f"weights_start_{l}")
        pushes.append((send_sems, recv_sems, lands))
        tokens.append(token[0, 0])

    t_len = N_META + s_len
    pad = -t_len % SEQ_TILE
    h = jnp.concatenate([meta_full, x[0], jnp.zeros((pad, d), F32)], axis=0)
    tgt = jnp.concatenate([jnp.zeros((N_META, d), F32), loss_target[0], jnp.zeros((pad, d), F32)], axis=0)
    z = _rms_fwd(h, _row(p["attn_norm_g"][0] + sum(tokens)))
    saved = []
    for l in range(depth):
        send_sems, recv_sems, lands = pushes[l]

        def fetch(k, after, l=l, send_sems=send_sems, recv_sems=recv_sems, lands=lands):
            i = BIG.index(k)
            return _push_wait(send_sems, recv_sems, [i], [lands[i]], [lands[i]], after, f"{k}_wait_{l}")[0]

        g_next = p["attn_norm_g"][l + 1] if l + 1 < depth else p["final_g"]
        h, z, sv = _forward_layer(l, h, z, p, fetch, g_next)
        saved.append(sv)
    dh, dh_b, dg_final, loss_part = _loss_bwd(h, _row(p["final_g"]), tgt, s_len)

    small = {k: [None] * depth for k in SMALL if k not in ("meta", "final_g")}
    pushes = [None] * depth
    tok = 0.0
    for l in reversed(range(depth)):
        dh, dh_b, big_mlp, sm_mlp = _backward_mlp(l, dh, dh_b, saved[l], p, tok)
        parts = [big_mlp["w_down"], big_mlp["w_up"]]
        push_mlp = _push_start(parts, [lax.empty(a.shape, a.dtype) for a in parts], False, f"mlp_grads_start_{l}")
        dh, dh_b, big_mix, sm_mix = _backward_mixer(l, dh, dh_b, saved[l], p, push_mlp[4][0, 0])
        parts = [big_mix["w_out"], big_mix["w_in"]]
        push_mix = _push_start(parts, [lax.empty(a.shape, a.dtype) for a in parts], False, f"mixer_grads_start_{l}")
        tok = push_mix[4][0, 0]
        pushes[l] = {("w_down", "w_up"): push_mlp, ("w_out", "w_in"): push_mix}
        for k, val in {**sm_mlp, **sm_mix}.items():
            small[k][l] = val
    grads = {k: jnp.stack(val) for k, val in small.items()}
    grads["final_g"] = dg_final[0]
    grads["meta"] = dh[:N_META]
    dx = dh[N_META:t_len]

    last_token = pushes[0][("w_out", "w_in")][4]
    outs = {k: [lax.empty(w[k].shape, F32) for _ in range(4)] for k in BIG}
    for l in reversed(range(depth)):
        sums = {}
        for names, (send_sems, recv_sems, parts, lands, _) in pushes[l].items():
            parts, landed = _push_wait(send_sems, recv_sems, [0, 1], parts, lands, last_token,
                                       f"{names[0]}_grads_wait_{l}")
            for k, part, land in zip(names, parts, landed):
                sums[k] = _sum_partials(part, land, chip1)
        theirs = _swap_with_sibling([sums[k] for k in BIG])
        for k, other in zip(BIG, theirs):
            outs[k] = _adamw_layer(w[k], m[k], v[k], l, sums[k], other, outs[k])
    out_g, out_d, out_m, out_v = [{k: outs[k][i] for k in BIG} for i in range(4)]

    full_shapes = [grads[k].shape for k in SMALL] + [(1,)]
    packed = _pack([grads[k].astype(F32) for k in SMALL] + [loss_part[0, :1]])
    total = _sum_slabs(_all_gather_devices(packed))
    small_g = dict(zip(SMALL + ("loss",), _unpack(total, full_shapes)))
    for k in COL_SHARDED_SMALL:
        n = w[k].shape[-1]
        small_g[k] = lax.dynamic_slice_in_dim(small_g[k], chip * n, n, axis=small_g[k].ndim - 1)
    local_shapes = [w[k].shape for k in SMALL]
    res = _adamw(_pack([w[k] for k in SMALL]), _pack([small_g[k] for k in SMALL]),
                 _pack([m[k] for k in SMALL]), _pack([v[k] for k in SMALL]))
    out_g.update({k: small_g[k] for k in SMALL})
    for dst, buf in zip((out_d, out_m, out_v), res):
        dst.update(zip(SMALL, _unpack(buf, local_shapes)))

    return (small_g["loss"].reshape(()), dx[None],
            *[out_g[k] for k in WEIGHTS], *[out_d[k] for k in WEIGHTS],
            *[out_m[k] for k in WEIGHTS], *[out_v[k] for k in WEIGHTS])
```

```python
import functools
import math

import jax
import jax.numpy as jnp
from jax import lax
from jax.experimental import pallas as pl
from jax.experimental.pallas import tpu as pltpu

F32 = jnp.float32
BF16 = jnp.bfloat16

N_META = 16
HEAD_DIM = 64
N_REC_BLOCKS = 8
CONV_WIDTH = 4
RG_C = 8.0
NORM_EPS = 1e-6
ADAM_LR = 0.001
ADAM_B1 = 0.9
ADAM_B2 = 0.999
ADAM_EPS = 1e-08
ADAM_WD = 0.01
ADAM_STEP = 10

LANES = 128
SUBLANES = 8
SEQ_TILE = 128
VMEM_CAP = 60 * 2**20
VMEM_SLACK = 6 * 2**20
NEG_BIG = -1e30
N_CHIPS = 4
N_DEV = 8
MESH = pl.DeviceIdType.MESH


def _nbytes(shape, dtype):
    return math.prod(shape) * jnp.dtype(dtype).itemsize


def _call(body, args, *, name, out_shape, grid=(), in_specs=None, out_specs=None, scratch_shapes=(),
          grid_spec=None, semantics=None, vmem_bytes=None, side_effects=None, hbm_results=True, **kw):
    cp = {}
    if semantics is not None:
        cp["dimension_semantics"] = semantics
    if vmem_bytes is not None:
        cp["vmem_limit_bytes"] = int(min(VMEM_CAP, vmem_bytes + VMEM_SLACK))
    if side_effects is not None:
        cp["has_side_effects"] = side_effects
    if grid_spec is not None:
        kw["grid_spec"] = grid_spec
    else:
        kw.update(grid=grid, in_specs=in_specs, out_specs=out_specs, scratch_shapes=scratch_shapes)
    if hbm_results:
        out_shape = jax.tree.map(
            lambda s: pltpu.HBM(s.shape, s.dtype) if isinstance(s, jax.ShapeDtypeStruct) else s, out_shape)
    fn = pl.pallas_call(
        body, name=name, out_shape=out_shape,
        compiler_params=pltpu.CompilerParams(**cp), **kw)
    return fn(*[_in_hbm(a) if jnp.issubdtype(getattr(a, "dtype", jnp.int32), jnp.floating) else a for a in args])


def _divisor_tile(n, unit, target):
    best = None
    for t in range(unit, min(n, target) + 1, unit):
        if n % t == 0:
            best = t
    return n if best is None else best


def _sigmoid(x):
    return 1.0 / (1.0 + jnp.exp(-x))


def _log1p_unit(e):
    series = e * (1.0 - e * (0.5 - e * (1.0 / 3.0)))
    return jnp.where(e < 1e-2, series, jnp.log(1.0 + e))


def _log_sigmoid(x):
    return jnp.minimum(x, 0.0) - _log1p_unit(jnp.exp(-jnp.abs(x)))


def _expm1_nonpos(x):
    small = x * (1.0 + x * (1.0 / 2 + x * (1.0 / 6 + x * (1.0 / 24 + x * (1.0 / 120 + x * (1.0 / 720))))))
    return jnp.where(x > -0.25, small, jnp.exp(x) - 1.0)


_GELU_K = math.sqrt(2.0 / math.pi)
_GELU_C = 0.044715


def _gelu_and_grad(y):
    th = jnp.tanh(_GELU_K * (y + _GELU_C * y * y * y))
    g = 0.5 * y * (1.0 + th)
    dg = 0.5 * (1.0 + th) + 0.5 * y * (1.0 - th * th) * _GELU_K * (1.0 + 3.0 * _GELU_C * y * y)
    return g, dg


def _rstd(x):
    return lax.rsqrt(jnp.mean(x * x, axis=-1, keepdims=True) + NORM_EPS)


def _rms_bwd(dz, x, g):
    rs = _rstd(x)
    xh = x * rs
    dgp = jnp.sum(dz * xh, axis=0, keepdims=True)
    dxh = dz * g
    dx = rs * (dxh - xh * jnp.mean(dxh * xh, axis=-1, keepdims=True))
    return dx, dgp


def _dot(a, b):
    return jnp.dot(a, b, preferred_element_type=F32)


def _dot_nt(a, b):
    return lax.dot_general(a, b, (((1,), (1,)), ((), ())), preferred_element_type=F32)


def _dot_tn(a, b):
    return lax.dot_general(a, b, (((0,), (0,)), ((), ())), preferred_element_type=F32)


def _full(shape):
    nd = len(shape)
    return pl.BlockSpec(shape, lambda *_: (0,) * nd)


def _rms_fwd(h, g):
    tp, d = h.shape
    tm = _divisor_tile(tp, 16, 544)

    def body(h_ref, g_ref, z_ref):
        x = h_ref[...]
        z_ref[...] = (x * _rstd(x) * g_ref[...]).astype(BF16)

    return _call(body, (h, g), name="rms_fwd", grid=(tp // tm,),
                 in_specs=[pl.BlockSpec((tm, d), lambda i: (i, 0)), _full((1, d))],
                 out_specs=pl.BlockSpec((tm, d), lambda i: (i, 0)),
                 out_shape=jax.ShapeDtypeStruct((tp, d), BF16), semantics=("parallel",))


def _proj(z, w_big, att_w):
    tp, d = z.shape
    nb = w_big.shape[1]
    tn = _divisor_tile(nb, LANES, 512)
    assert (3 * att_w) % tn == 0
    n_qkv = 3 * att_w // tn
    scale = 1.0 / math.sqrt(HEAD_DIM)

    def body(z_ref, w_ref, p_ref, qkv_ref):
        j = pl.program_id(0)
        acc = _dot(z_ref[...], w_ref[...])
        p_ref[...] = acc

        @pl.when(j < n_qkv)
        def _():
            col = j * tn + lax.broadcasted_iota(jnp.int32, (1, tn), 1)
            qkv_ref[...] = (acc * jnp.where(col < att_w, scale, 1.0)).astype(BF16)

    vm = 2 * (_nbytes((tp, d), BF16) + _nbytes((d, tn), BF16) + _nbytes((tp, tn), F32) * 2)
    return _call(body, (z, w_big), name="proj", grid=(nb // tn,),
                 in_specs=[_full((tp, d)), pl.BlockSpec((d, tn), lambda j: (0, j))],
                 out_specs=[pl.BlockSpec((tp, tn), lambda j: (0, j)),
                            pl.BlockSpec((tp, tn), lambda j: (0, jnp.minimum(j, n_qkv - 1)))],
                 out_shape=[jax.ShapeDtypeStruct((tp, nb), F32),
                            jax.ShapeDtypeStruct((tp, 3 * att_w), BF16)],
                 semantics=("arbitrary",), vmem_bytes=vm)


def _tile_cumsum(x, row, reverse=False):
    for s in (1, 2, 4):
        if reverse:
            x = x + jnp.where(row < SUBLANES - s, pltpu.roll(x, SUBLANES - s, 0), 0.0)
        else:
            x = x + jnp.where(row >= s, pltpu.roll(x, s, 0), 0.0)
    return x


def _fgate_fwd(proj, b_f_pad, nh):
    tp, nb = proj.shape
    fblk = nb // LANES - 1

    def body(f_ref, b_ref, c_ref, ct_ref):
        b = b_ref[...]
        row = lax.broadcasted_iota(jnp.int32, (SUBLANES, LANES), 0)

        def step(i, carry):
            r0 = pl.multiple_of(i * SUBLANES, SUBLANES)
            lf = _log_sigmoid(f_ref[pl.ds(r0, SUBLANES), :] + b)
            x = _tile_cumsum(lf, row) + carry
            c_ref[pl.ds(r0, SUBLANES), :] = x
            return x[SUBLANES - 1:SUBLANES, :]

        lax.fori_loop(0, tp // SUBLANES, step, jnp.zeros((1, LANES), F32))
        ct_ref[...] = c_ref[...].T[:nh, :]

    return _call(body, (proj, b_f_pad), name="fgate_fwd", grid=(1,),
                 in_specs=[pl.BlockSpec((tp, LANES), lambda i: (0, fblk)), _full((1, LANES))],
                 out_specs=[_full((tp, LANES)), _full((nh, tp))],
                 out_shape=[jax.ShapeDtypeStruct((tp, LANES), F32), jax.ShapeDtypeStruct((nh, tp), F32)],
                 semantics=("arbitrary",))


def _fgate_bwd(proj, b_f_pad, dct):
    tp, nb = proj.shape
    nh = dct.shape[0]
    fblk = nb // LANES - 1

    def body(f_ref, b_ref, dct_ref, df_ref, db_ref, dc_s):
        b = b_ref[...]
        row = lax.broadcasted_iota(jnp.int32, (SUBLANES, LANES), 0)
        nt = tp // SUBLANES
        dc_s[...] = jnp.concatenate([dct_ref[...], jnp.zeros((LANES - nh, tp), F32)], axis=0).T

        def step(i, carry):
            suffix, acc = carry
            r0 = pl.multiple_of((nt - 1 - i) * SUBLANES, SUBLANES)
            dlf = _tile_cumsum(dc_s[pl.ds(r0, SUBLANES), :], row, reverse=True) + suffix
            df = dlf * _sigmoid(-(f_ref[pl.ds(r0, SUBLANES), :] + b))
            dc_s[pl.ds(r0, SUBLANES), :] = df
            return dlf[0:1, :], acc + df

        _, acc = lax.fori_loop(0, nt, step, (jnp.zeros((1, LANES), F32), jnp.zeros((SUBLANES, LANES), F32)))
        df_ref[...] = dc_s[...].astype(BF16)
        db_ref[...] = jnp.broadcast_to(jnp.sum(acc, axis=0, keepdims=True), (SUBLANES, LANES))

    return _call(body, (proj, b_f_pad, dct), name="fgate_bwd", grid=(1,),
                 in_specs=[pl.BlockSpec((tp, LANES), lambda i: (0, fblk)), _full((1, LANES)), _full((nh, tp))],
                 out_specs=[_full((tp, LANES)), _full((SUBLANES, LANES))],
                 out_shape=[jax.ShapeDtypeStruct((tp, LANES), BF16),
                            jax.ShapeDtypeStruct((SUBLANES, LANES), F32)],
                 scratch_shapes=[pltpu.VMEM((tp, LANES), F32)], semantics=("arbitrary",))


ATT_BQ = 128


ATT_BUCKET = 3
ATT_HEADS = 4


def _for_bucket(i, nq, fn):
    for lo in range(0, nq, ATT_BUCKET):
        hi = min(lo + ATT_BUCKET, nq)
        spans = ([(0, lo * ATT_BQ, False)] if lo else []) + [(lo * ATT_BQ, hi * ATT_BQ, True)]
        pl.when(jnp.logical_and(i >= lo, i < hi))(functools.partial(fn, spans))


def _head_column(c_blk, h):
    lane = lax.broadcasted_iota(jnp.int32, c_blk.shape, 1)
    return jnp.sum(jnp.where(lane == h, c_blk, 0.0), axis=1, keepdims=True)


def _span_logits(q, k_ref, ct_ref, h, lo, i, span):
    k0, k1, needs_mask = span
    t = _dot_nt(q, k_ref[k0:k1, lo:lo + HEAD_DIM]) - ct_ref[pl.ds(h, 1), k0:k1]
    if needs_mask:
        rows = i * ATT_BQ + lax.broadcasted_iota(jnp.int32, (ATT_BQ, k1 - k0), 0)
        cols = k0 + lax.broadcasted_iota(jnp.int32, (ATT_BQ, k1 - k0), 1)
        t = jnp.where(cols <= rows, t, NEG_BIG)
    return t


def _attn_fwd(qkv, c, c_t, nh):
    tp = qkv.shape[0]
    att_w = nh * HEAD_DIM
    ng = nh // ATT_HEADS
    gw = ATT_HEADS * HEAD_DIM
    bq = ATT_BQ
    nq = tp // bq

    def body(q_ref, k_ref, v_ref, c_ref, ct_ref, o_ref, lse_ref):
        p = pl.program_id(0)
        i = pl.program_id(1)

        def compute(spans):
            outs, lses = [], []
            for hh in range(ATT_HEADS):
                lo = HEAD_DIM * hh
                h = ATT_HEADS * p + hh
                q = q_ref[:, lo:lo + HEAD_DIM]
                ts = [_span_logits(q, k_ref, ct_ref, h, lo, i, sp) for sp in spans]
                m = functools.reduce(jnp.maximum, [jnp.max(t, axis=1, keepdims=True) for t in ts])
                es = [jnp.exp(t - m) for t in ts]
                l = sum(jnp.sum(e, axis=1, keepdims=True) for e in es)
                o = sum(_dot(e.astype(BF16), v_ref[k0:k1, lo:lo + HEAD_DIM]) for e, (k0, k1, _) in zip(es, spans))
                outs.append(o / l)
                lses.append(jnp.broadcast_to(m + _head_column(c_ref[...], h) + jnp.log(l), (bq, HEAD_DIM)))
            o_ref[...] = jnp.concatenate(outs, axis=1)
            lse_ref[...] = jnp.concatenate(lses, axis=1)

        _for_bucket(i, nq, compute)

    blk = pl.BlockSpec((bq, gw), lambda p, i: (i, p))
    vm = 4 * _nbytes((tp, gw), BF16) + 8 * ATT_HEADS * _nbytes((bq, tp), F32)
    return _call(body, (qkv, qkv, qkv, c, c_t), name="attn_fwd", grid=(ng, nq),
                 in_specs=[blk,
                           pl.BlockSpec((tp, gw), lambda p, i: (0, ng + p)),
                           pl.BlockSpec((tp, gw), lambda p, i: (0, 2 * ng + p)),
                           pl.BlockSpec((bq, LANES), lambda p, i: (i, 0)), _full((nh, tp))],
                 out_specs=[blk, blk],
                 out_shape=[jax.ShapeDtypeStruct((tp, att_w), F32)] * 2,
                 semantics=("parallel", "parallel"), vmem_bytes=vm)


def _attn_bwd(qkv, c, c_t, lse_b, do, nh):
    tp = qkv.shape[0]
    att_w = nh * HEAD_DIM
    ng = nh // ATT_HEADS
    gw = ATT_HEADS * HEAD_DIM
    bq = ATT_BQ
    nq = tp // bq
    scale = 1.0 / math.sqrt(HEAD_DIM)

    def body(q_ref, k_ref, v_ref, c_ref, ct_ref, lse_ref, do_ref, dq_ref, dk_ref, dv_ref, dct_ref, dk_s, dv_s):
        p = pl.program_id(0)
        i = pl.program_id(1)

        @pl.when(i == 0)
        def _():
            dk_s[...] = jnp.zeros_like(dk_s)
            dv_s[...] = jnp.zeros_like(dv_s)

        @pl.when(jnp.logical_and(i == 0, p == 0))
        def _():
            dct_ref[...] = jnp.zeros_like(dct_ref)

        def compute(spans):
            dqs = []
            dks = [[] for _ in spans]
            dvs = [[] for _ in spans]
            for hh in range(ATT_HEADS):
                lo = HEAD_DIM * hh
                h = ATT_HEADS * p + hh
                q = q_ref[:, lo:lo + HEAD_DIM]
                row_term = lse_ref[:, lo:lo + 1] - _head_column(c_ref[...], h)
                doutb = do_ref[:, lo:lo + HEAD_DIM].astype(BF16)
                prs = [jnp.exp(_span_logits(q, k_ref, ct_ref, h, lo, i, sp) - row_term) for sp in spans]
                dps = [_dot_nt(doutb, v_ref[k0:k1, lo:lo + HEAD_DIM]) for k0, k1, _ in spans]
                row_sum = sum(jnp.sum(pr * dp, axis=1, keepdims=True) for pr, dp in zip(prs, dps))
                dq = 0.0
                for n, ((k0, k1, _), pr, dp) in enumerate(zip(spans, prs, dps)):
                    ds = pr * (dp - row_sum)
                    dsb = ds.astype(BF16)
                    dq = dq + _dot(dsb, k_ref[k0:k1, lo:lo + HEAD_DIM])
                    dks[n].append(_dot_tn(dsb, q))
                    dvs[n].append(_dot_tn(pr.astype(BF16), doutb))
                    dct_ref[pl.ds(h, 1), k0:k1] = dct_ref[pl.ds(h, 1), k0:k1] - jnp.sum(ds, axis=0, keepdims=True)
                dqs.append(dq * scale)
            dq_ref[...] = jnp.concatenate(dqs, axis=1).astype(BF16)
            for n, (k0, k1, _) in enumerate(spans):
                dk_s[k0:k1, :] += jnp.concatenate(dks[n], axis=1)
                dv_s[k0:k1, :] += jnp.concatenate(dvs[n], axis=1)

        _for_bucket(i, nq, compute)

        @pl.when(i == nq - 1)
        def _():
            dk_ref[...] = dk_s[...].astype(BF16)
            dv_ref[...] = dv_s[...].astype(BF16)

    blk = pl.BlockSpec((bq, gw), lambda p, i: (i, p))
    col = pl.BlockSpec((tp, gw), lambda p, i: (0, p))
    vm = 6 * _nbytes((tp, gw), BF16) + 2 * _nbytes((tp, gw), F32) + 12 * ATT_HEADS * _nbytes((bq, tp), F32)
    return _call(body, (qkv, qkv, qkv, c, c_t, lse_b, do), name="attn_bwd", grid=(ng, nq),
                 in_specs=[blk,
                           pl.BlockSpec((tp, gw), lambda p, i: (0, ng + p)),
                           pl.BlockSpec((tp, gw), lambda p, i: (0, 2 * ng + p)),
                           pl.BlockSpec((bq, LANES), lambda p, i: (i, 0)), _full((nh, tp)), blk, blk],
                 out_specs=[blk, col, col, _full((nh, tp))],
                 out_shape=[jax.ShapeDtypeStruct((tp, att_w), BF16)] * 3 + [jax.ShapeDtypeStruct((nh, tp), F32)],
                 scratch_shapes=[pltpu.VMEM((tp, gw), F32)] * 2,
                 semantics=("arbitrary", "arbitrary"), vmem_bytes=vm)


REC_ROWS = 128
HALO = SUBLANES


def _conv_taps(cat):
    taps = []
    for k in range(CONV_WIDTH):
        sh = CONV_WIDTH - 1 - k
        taps.append((pltpu.roll(cat, sh, 0) if sh else cat)[HALO:])
    return taps


def _rec_gates(xc, wa_ref, ba_ref, wx_ref, bx_ref, l_ref):
    xcb = xc.astype(BF16)
    r = _sigmoid(_dot(xcb, wa_ref[...]) + ba_ref[...])
    ig = _sigmoid(_dot(xcb, wx_ref[...]) + bx_ref[...])
    ls = _log_sigmoid(l_ref[...])
    log_a = RG_C * r * ls
    return xcb, r, ig, ls, log_a


def _rec_fwd(proj, xr_blk, yr_blk, rec_w, conv_w, conv_b, wa, ba, wx, bx, lru):
    tp = proj.shape[0]
    w = rec_w
    r_rows = REC_ROWS
    nc = tp // r_rows
    cpb = w // LANES

    def body(xr_ref, yr_ref, cw_ref, cb_ref, wa_ref, ba_ref, wx_ref, bx_ref, l_ref,
             hr_ref, rec_ref, prev_s, carry_s, a_s, u_s):
        i = pl.program_id(0)

        @pl.when(i == 0)
        def _():
            prev_s[...] = jnp.zeros_like(prev_s)
            carry_s[...] = jnp.zeros_like(carry_s)

        x = xr_ref[...]
        taps = _conv_taps(jnp.concatenate([prev_s[...], x], axis=0))
        prev_s[...] = x[r_rows - HALO:]
        xc = cb_ref[...]
        for k in range(CONV_WIDTH):
            xc = xc + cw_ref[k:k + 1, :] * taps[k]
        _, r, ig, ls, log_a = _rec_gates(xc, wa_ref, ba_ref, wx_ref, bx_ref, l_ref)
        a_s[...] = jnp.exp(log_a)
        u_s[...] = jnp.sqrt(-_expm1_nonpos(2.0 * log_a)) * ig * xc

        def tile(j, h):
            r0 = pl.multiple_of(j * SUBLANES, SUBLANES)
            at = a_s[pl.ds(r0, SUBLANES), :]
            ut = u_s[pl.ds(r0, SUBLANES), :]
            out = []
            for rr in range(SUBLANES):
                h = at[rr:rr + 1] * h + ut[rr:rr + 1]
                out.append(h)
            hr_ref[pl.ds(r0, SUBLANES), :] = jnp.concatenate(out, axis=0)
            return h

        carry_s[0:1, :] = lax.fori_loop(0, r_rows // SUBLANES, tile, carry_s[0:1, :])
        g, _ = _gelu_and_grad(yr_ref[...])
        rec_ref[...] = hr_ref[...] * g

    blk = pl.BlockSpec((r_rows, w), lambda i: (i, 0))
    vm = 16 * _nbytes((r_rows, w), F32) + 4 * _nbytes((w, w), BF16)
    return _call(body, (proj, proj, conv_w, conv_b, wa, ba, wx, bx, lru), name="rec_fwd", grid=(nc,),
                 in_specs=[pl.BlockSpec((r_rows, w), lambda i: (i, xr_blk)),
                           pl.BlockSpec((r_rows, w), lambda i: (i, yr_blk)),
                           _full((CONV_WIDTH, w)), _full((1, w)), _full((w, w)), _full((1, w)),
                           _full((w, w)), _full((1, w)), _full((1, w))],
                 out_specs=[blk, blk],
                 out_shape=[jax.ShapeDtypeStruct((tp, w), F32)] * 2,
                 scratch_shapes=[pltpu.VMEM((HALO, w), F32), pltpu.VMEM((SUBLANES, w), F32),
                                 pltpu.VMEM((r_rows, w), F32), pltpu.VMEM((r_rows, w), F32)],
                 semantics=("arbitrary",), vmem_bytes=vm)


def _rec_bwd(proj, xr_blk, yr_blk, rec_w, hr, drec, conv_w, conv_b, wa, ba, wx, bx, lru):
    tp = proj.shape[0]
    w = rec_w
    r_rows = REC_ROWS
    nc = tp // r_rows
    hpc = r_rows // HALO

    def body(xr_ref, xh_ref, yr_ref, hr_ref, hh_ref, drec_ref, cw_ref, cb_ref, wa_ref, ba_ref, wx_ref, bx_ref,
             l_ref, dxr_ref, dyr_ref, dwa_ref, dwx_ref, small_ref, lam_s, a_s, dhr_s, carry_s, next_s):
        i = pl.program_id(0)
        first = (nc - 1 - i) == 0

        @pl.when(i == 0)
        def _():
            carry_s[...] = jnp.zeros_like(carry_s)
            next_s[...] = jnp.zeros_like(next_s)
            dwa_ref[...] = jnp.zeros_like(dwa_ref)
            dwx_ref[...] = jnp.zeros_like(dwx_ref)
            small_ref[...] = jnp.zeros_like(small_ref)

        x = xr_ref[...]
        xprev = jnp.where(first, 0.0, xh_ref[...])
        taps = _conv_taps(jnp.concatenate([xprev, x], axis=0))
        xc = cb_ref[...]
        for k in range(CONV_WIDTH):
            xc = xc + cw_ref[k:k + 1, :] * taps[k]
        xcb, r, ig, ls, log_a = _rec_gates(xc, wa_ref, ba_ref, wx_ref, bx_ref, l_ref)
        a = jnp.exp(log_a)
        a2 = jnp.exp(2.0 * log_a)
        mult = jnp.sqrt(-_expm1_nonpos(2.0 * log_a))
        g, dg = _gelu_and_grad(yr_ref[...])
        hr_v = hr_ref[...]
        drec_v = drec_ref[...]
        dhr_s[...] = drec_v * g
        dyr_ref[...] = (drec_v * hr_v * dg).astype(BF16)
        a_s[...] = a

        def tile(jj, carry):
            r0 = pl.multiple_of((r_rows // SUBLANES - 1 - jj) * SUBLANES, SUBLANES)
            at = a_s[pl.ds(r0, SUBLANES), :]
            dt = dhr_s[pl.ds(r0, SUBLANES), :]
            out = [None] * SUBLANES
            for rr in range(SUBLANES - 1, -1, -1):
                lam = dt[rr:rr + 1] + carry
                out[rr] = lam
                carry = at[rr:rr + 1] * lam
            lam_s[pl.ds(r0, SUBLANES), :] = jnp.concatenate(out, axis=0)
            return carry

        carry_s[0:1, :] = lax.fori_loop(0, r_rows // SUBLANES, tile, carry_s[0:1, :])
        lam = lam_s[...]
        hprev = jnp.where(first, 0.0, hh_ref[...])
        hr_prev = pltpu.roll(jnp.concatenate([hprev, hr_v], axis=0), 1, 0)[HALO:]
        da = lam * hr_prev
        dxc = lam * mult * ig
        di = lam * mult * xc
        dmult = lam * ig * xc
        dlog_a = da * a - dmult * a2 / mult
        dr = dlog_a * (RG_C * ls)
        dls = jnp.sum(dlog_a * (RG_C * r), axis=0, keepdims=True)
        dga = dr * r * (1.0 - r)
        dgx = di * ig * (1.0 - ig)
        dgab = dga.astype(BF16)
        dgxb = dgx.astype(BF16)
        dxc = dxc + _dot_nt(dgab, wa_ref[...]) + _dot_nt(dgxb, wx_ref[...])
        dwa_ref[...] += _dot_tn(xcb, dgab)
        dwx_ref[...] += _dot_tn(xcb, dgxb)
        cat = jnp.concatenate([dxc, next_s[...]], axis=0)
        next_s[...] = dxc[0:HALO]
        dxr = cw_ref[CONV_WIDTH - 1:CONV_WIDTH, :] * dxc
        for k in range(CONV_WIDTH - 1):
            sh = CONV_WIDTH - 1 - k
            dxr = dxr + cw_ref[k:k + 1, :] * pltpu.roll(cat, r_rows + HALO - sh, 0)[:r_rows]
        dxr_ref[...] = dxr.astype(BF16)
        rows = [jnp.sum(dxc * taps[k], axis=0, keepdims=True) for k in range(CONV_WIDTH)]
        rows += [jnp.sum(dxc, axis=0, keepdims=True), jnp.sum(dga, axis=0, keepdims=True),
                 jnp.sum(dgx, axis=0, keepdims=True), dls * _sigmoid(-l_ref[...])]
        small_ref[...] += jnp.concatenate(rows, axis=0)

    def rev(i):
        return nc - 1 - i

    def halo(i):
        return jnp.maximum(rev(i) * hpc - 1, 0)

    blk = pl.BlockSpec((r_rows, w), lambda i: (rev(i), 0))
    vm = 40 * _nbytes((r_rows, w), F32) + 6 * _nbytes((w, w), F32)
    return _call(body, (proj, proj, proj, hr, hr, drec, conv_w, conv_b, wa, ba, wx, bx, lru),
                 name="rec_bwd", grid=(nc,),
                 in_specs=[pl.BlockSpec((r_rows, w), lambda i: (rev(i), xr_blk)),
                           pl.BlockSpec((HALO, w), lambda i: (halo(i), xr_blk)),
                           pl.BlockSpec((r_rows, w), lambda i: (rev(i), yr_blk)),
                           blk,
                           pl.BlockSpec((HALO, w), lambda i: (halo(i), 0)),
                           blk,
                           _full((CONV_WIDTH, w)), _full((1, w)), _full((w, w)), _full((1, w)),
                           _full((w, w)), _full((1, w)), _full((1, w))],
                 out_specs=[blk, blk, _full((w, w)), _full((w, w)), _full((SUBLANES, w))],
                 out_shape=[jax.ShapeDtypeStruct((tp, w), BF16)] * 2
                 + [jax.ShapeDtypeStruct((w, w), F32)] * 2 + [jax.ShapeDtypeStruct((SUBLANES, w), F32)],
                 scratch_shapes=[pltpu.VMEM((r_rows, w), F32)] * 3
                 + [pltpu.VMEM((SUBLANES, w), F32), pltpu.VMEM((HALO, w), F32)],
                 semantics=("arbitrary",), vmem_bytes=vm)


ROW_TARGET = 544


def _mixer_out(attn, rec, g_a, g_r, w_out, h, g_next):
    tp, d = h.shape
    aw, rw = attn.shape[1], rec.shape[1]
    kc = d // N_CHIPS
    tm = _divisor_tile(tp, 16, ROW_TARGET)

    def body(a_ref, r_ref, ga_ref, gr_ref, w_ref, h_ref, gn_ref, h1_ref, z_ref, mix_ref):
        a = a_ref[...]
        r = r_ref[...]
        mix = jnp.concatenate([a * _rstd(a) * ga_ref[...], r * _rstd(r) * gr_ref[...]], axis=1).astype(BF16)
        mix_ref[...] = mix
        h1 = h_ref[...]
        for j in range(N_CHIPS):
            h1 = h1 + _dot(mix[:, j * kc:(j + 1) * kc], w_ref[j])
        h1_ref[...] = h1
        z_ref[...] = (h1 * _rstd(h1) * gn_ref[...]).astype(BF16)

    row = lambda wd: pl.BlockSpec((tm, wd), lambda i: (i, 0))
    vm = 2 * _nbytes((d, d), BF16) + 12 * _nbytes((tm, d), F32)
    return _call(body, (attn, rec, g_a, g_r, w_out, h, g_next), name="mixer_out", grid=(tp // tm,),
                 in_specs=[row(aw), row(rw), _full((1, aw)), _full((1, rw)), _full(w_out.shape), row(d),
                           _full((1, d))],
                 out_specs=[row(d), row(d), row(d)],
                 out_shape=[jax.ShapeDtypeStruct((tp, d), F32), jax.ShapeDtypeStruct((tp, d), BF16),
                            jax.ShapeDtypeStruct((tp, d), BF16)],
                 semantics=("parallel",), vmem_bytes=vm)


def _mixer_bwd(dh_b, w_out, attn, rec, g_a, g_r):
    tp, d = dh_b.shape
    aw, rw = attn.shape[1], rec.shape[1]
    tm = _divisor_tile(tp, 16, ROW_TARGET)

    def body(dh_ref, w_ref, a_ref, r_ref, ga_ref, gr_ref, da_ref, dr_ref, dg_ref):
        @pl.when(pl.program_id(0) == 0)
        def _():
            dg_ref[...] = jnp.zeros_like(dg_ref)

        dh = dh_ref[...]
        dmix = jnp.concatenate([_dot_nt(dh, w_ref[j]) for j in range(N_CHIPS)], axis=1)
        da, dga = _rms_bwd(dmix[:, :aw], a_ref[...], ga_ref[...])
        dr, dgr = _rms_bwd(dmix[:, aw:], r_ref[...], gr_ref[...])
        da_ref[...] = da
        dr_ref[...] = dr
        dg_ref[...] += jnp.broadcast_to(jnp.concatenate([dga, dgr], axis=1), (SUBLANES, d))

    row = lambda wd: pl.BlockSpec((tm, wd), lambda i: (i, 0))
    vm = 2 * _nbytes((d, d), BF16) + 12 * _nbytes((tm, d), F32)
    return _call(body, (dh_b, w_out, attn, rec, g_a, g_r), name="mixer_bwd", grid=(tp // tm,),
                 in_specs=[row(d), _full(w_out.shape), row(aw), row(rw), _full((1, aw)), _full((1, rw))],
                 out_specs=[row(aw), row(rw), _full((SUBLANES, d))],
                 out_shape=[jax.ShapeDtypeStruct((tp, aw), F32), jax.ShapeDtypeStruct((tp, rw), F32),
                            jax.ShapeDtypeStruct((SUBLANES, d), F32)],
                 semantics=("arbitrary",), vmem_bytes=vm)


def _mlp_up(z, w_up):
    tp, d = z.shape
    fc = w_up.shape[2]
    ff = N_CHIPS * fc
    tn = _divisor_tile(fc, LANES, 512)
    per = fc // tn

    def body(z_ref, w_ref, act_ref, up_ref):
        up = _dot(z_ref[...], w_ref[...])
        r = jnp.maximum(up, 0.0)
        act_ref[...] = (r * r).astype(BF16)
        up_ref[...] = up.astype(BF16)

    col = pl.BlockSpec((tp, tn), lambda j: (0, j))
    vm = 2 * _nbytes((tp, d), BF16) + 2 * _nbytes((d, tn), BF16) + 8 * _nbytes((tp, tn), F32)
    return _call(body, (z, w_up), name="mlp_up", grid=(ff // tn,),
                 in_specs=[_full((tp, d)), pl.BlockSpec((None, d, tn), lambda j: (j // per, 0, j % per))],
                 out_specs=[col, col],
                 out_shape=[jax.ShapeDtypeStruct((tp, ff), BF16)] * 2,
                 semantics=("parallel",), vmem_bytes=vm)


def _mlp_down(act, w_down, h, g_next):
    tp, d = h.shape
    ff = act.shape[1]
    fc = ff // N_CHIPS
    tm = _divisor_tile(tp, 16, ROW_TARGET)

    def body(a_ref, w_ref, h_ref, gn_ref, h2_ref, z_ref):
        h2 = h_ref[...]
        for j in range(N_CHIPS):
            h2 = h2 + _dot(a_ref[:, j * fc:(j + 1) * fc], w_ref[j])
        h2_ref[...] = h2
        z_ref[...] = (h2 * _rstd(h2) * gn_ref[...]).astype(BF16)

    row = lambda wd: pl.BlockSpec((tm, wd), lambda i: (i, 0))
    vm = 2 * _nbytes((ff, d), BF16) + 2 * _nbytes((tm, ff), BF16) + 10 * _nbytes((tm, d), F32)
    return _call(body, (act, w_down, h, g_next), name="mlp_down", grid=(tp // tm,),
                 in_specs=[row(ff), _full(w_down.shape), row(d), _full((1, d))],
                 out_specs=[row(d), row(d)],
                 out_shape=[jax.ShapeDtypeStruct((tp, d), F32), jax.ShapeDtypeStruct((tp, d), BF16)],
                 semantics=("parallel",), vmem_bytes=vm)


def _loss_bwd(h, g, target, n_real):
    tp, d = h.shape
    tm = _divisor_tile(tp, 16, ROW_TARGET)

    def body(h_ref, g_ref, t_ref, dh_ref, dhb_ref, dg_ref, loss_ref):
        i = pl.program_id(0)

        @pl.when(i == 0)
        def _():
            dg_ref[...] = jnp.zeros_like(dg_ref)
            loss_ref[...] = jnp.zeros_like(loss_ref)

        x = h_ref[...]
        gv = g_ref[...]
        rowi = i * tm + lax.broadcasted_iota(jnp.int32, (tm, 1), 0)
        real = jnp.logical_and(rowi >= N_META, rowi < N_META + n_real)
        err = jnp.where(real, x * _rstd(x) * gv - t_ref[...], 0.0)
        loss_ref[...] += 0.5 * jnp.sum(jnp.mean(err * err, axis=-1, keepdims=True))
        dx, dgp = _rms_bwd(err * (1.0 / d), x, gv)
        dh_ref[...] = dx
        dhb_ref[...] = dx.astype(BF16)
        dg_ref[...] += jnp.broadcast_to(dgp, (SUBLANES, d))

    row = pl.BlockSpec((tm, d), lambda i: (i, 0))
    return _call(body, (h, g, target), name="loss_bwd", grid=(tp // tm,),
                 in_specs=[row, _full((1, d)), row],
                 out_specs=[row, row, _full((SUBLANES, d)), _full((SUBLANES, LANES))],
                 out_shape=[jax.ShapeDtypeStruct((tp, d), F32), jax.ShapeDtypeStruct((tp, d), BF16),
                            jax.ShapeDtypeStruct((SUBLANES, d), F32), jax.ShapeDtypeStruct((SUBLANES, LANES), F32)],
                 semantics=("arbitrary",), vmem_bytes=16 * _nbytes((tm, d), F32))


def _mlp_dup(dh_b, w_down, up):
    tp, d = dh_b.shape
    fc = w_down.shape[1]
    ff = N_CHIPS * fc
    tn = _divisor_tile(fc, LANES, 512)
    per = fc // tn

    def body(dh_ref, w_ref, up_ref, dup_ref):
        dact = _dot_nt(dh_ref[...], w_ref[...])
        dup_ref[...] = (dact * (2.0 * jnp.maximum(up_ref[...].astype(F32), 0.0))).astype(BF16)

    col = pl.BlockSpec((tp, tn), lambda j: (0, j))
    vm = 2 * _nbytes((tp, d), BF16) + 2 * _nbytes((tn, d), BF16) + 8 * _nbytes((tp, tn), F32)
    return _call(body, (dh_b, w_down, up), name="mlp_dup", grid=(ff // tn,),
                 in_specs=[_full((tp, d)), pl.BlockSpec((None, tn, d), lambda j: (j // per, j % per, 0)),
                           col],
                 out_specs=col, out_shape=jax.ShapeDtypeStruct((tp, ff), BF16),
                 semantics=("parallel",), vmem_bytes=vm)


def _grad_w(a, b, cols_to_chips=False):
    tp, k = a.shape
    n = b.shape[1]
    tk = _divisor_tile(k, LANES, 1024)
    nc = n // N_CHIPS if cols_to_chips else n
    tn = _divisor_tile(nc, LANES, 512)
    per = nc // tn

    def body(a_ref, b_ref, o_ref):
        o_ref[...] = _dot_tn(a_ref[...], b_ref[...]).astype(BF16)

    if cols_to_chips:
        out_spec = pl.BlockSpec((None, tk, tn), lambda i, j: (j // per, i, j % per))
        out_shape = jax.ShapeDtypeStruct((N_CHIPS, k, nc), BF16)
    else:
        out_spec = pl.BlockSpec((tk, tn), lambda i, j: (i, j))
        out_shape = jax.ShapeDtypeStruct((k, n), BF16)
    vm = 2 * _nbytes((tp, tk), BF16) + 2 * _nbytes((tp, tn), BF16) + 6 * _nbytes((tk, tn), F32) \
        + 2 * _nbytes((tp, tk), F32)
    return _call(body, (a, b), name="grad_w", grid=(k // tk, n // tn),
                 in_specs=[pl.BlockSpec((tp, tk), lambda i, j: (0, i)), pl.BlockSpec((tp, tn), lambda i, j: (0, j))],
                 out_specs=out_spec, out_shape=out_shape,
                 semantics=("parallel", "parallel"), vmem_bytes=vm)


def _dx_norm_bwd(pieces, w, w_spec, w_piece, h, g, dres):
    tp, d = h.shape
    tm = _divisor_tile(tp, 16, ROW_TARGET)
    n = len(pieces)

    def body(*refs):
        dy_refs = refs[:n]
        w_ref, h_ref, g_ref, dres_ref, dh_ref, dhb_ref, dg_ref = refs[n:]

        @pl.when(pl.program_id(0) == 0)
        def _():
            dg_ref[...] = jnp.zeros_like(dg_ref)

        dz = _dot_nt(dy_refs[0][...], w_piece(w_ref, 0))
        for i in range(1, n):
            dz = dz + _dot_nt(dy_refs[i][...], w_piece(w_ref, i))
        dx, dgp = _rms_bwd(dz, h_ref[...], g_ref[...])
        dh = dres_ref[...] + dx
        dh_ref[...] = dh
        dhb_ref[...] = dh.astype(BF16)
        dg_ref[...] += jnp.broadcast_to(dgp, (SUBLANES, d))

    row = lambda wd: pl.BlockSpec((tm, wd), lambda i: (i, 0))
    kk = sum(wd for _, _, wd in pieces)
    vm = 2 * _nbytes((d, kk), BF16) + 2 * _nbytes((tm, kk), BF16) + 14 * _nbytes((tm, d), F32)
    piece_specs = [pl.BlockSpec((tm, wd), functools.partial(lambda i, cb: (i, cb), cb=cb)) for _, cb, wd in pieces]
    return _call(body, tuple(a for a, _, _ in pieces) + (w, h, g, dres), name="dx_norm_bwd", grid=(tp // tm,),
                 in_specs=piece_specs + [w_spec, row(d), _full((1, d)), row(d)],
                 out_specs=[row(d), row(d), _full((SUBLANES, d))],
                 out_shape=[jax.ShapeDtypeStruct((tp, d), F32), jax.ShapeDtypeStruct((tp, d), BF16),
                            jax.ShapeDtypeStruct((SUBLANES, d), F32)],
                 semantics=("arbitrary",), vmem_bytes=vm)


def _block_diag(wg):
    nb, b, _ = wg.shape
    eye = jnp.eye(nb, dtype=wg.dtype)
    return (eye[:, None, :, None] * wg[:, :, None, :]).reshape(nb * b, nb * b)


def _diag_blocks(dense, nb):
    b = dense.shape[0] // nb
    d4 = dense.reshape(nb, b, nb, b)
    return jnp.stack([d4[i, :, i, :] for i in range(nb)])


def _row(v):
    return v.reshape(1, -1)


def _forward_layer(l, h, z, p, fetch, g_next):
    d = h.shape[1]
    att_w = d // 2
    rec_w = d - att_w
    nh = att_w // HEAD_DIM
    xr_blk = 3 * att_w // rec_w
    wa_d = _block_diag(p["w_gate_a"][l]).astype(BF16)
    wx_d = _block_diag(p["w_gate_x"][l]).astype(BF16)
    b_f_pad = jnp.zeros((1, LANES), F32).at[0, :nh].set(p["b_f"][l])
    big = dict(w_in_big=_pack_w_in(jnp.moveaxis(fetch("w_in", h), 0, 1).reshape(d, -1), att_w, rec_w, nh))
    proj, qkv = _proj(z, big["w_in_big"], att_w)
    c, c_t = _fgate_fwd(proj, b_f_pad, nh)
    attn, lse_b = _attn_fwd(qkv, c, c_t, nh)
    hr, rec = _rec_fwd(proj, xr_blk, xr_blk + 1, rec_w, p["conv_w"][l], _row(p["conv_b"][l]), wa_d,
                       _row(p["b_gate_a"][l]), wx_d, _row(p["b_gate_x"][l]), _row(p["lru_L"][l]))
    big["w_out"] = fetch("w_out", rec)
    h1, z2, mix = _mixer_out(attn, rec, _row(p["attn_out_g"][l]), _row(p["rec_out_g"][l]),
                             big["w_out"], h, _row(p["mlp_norm_g"][l]))
    big["w_up"] = fetch("w_up", h1)
    act, up = _mlp_up(z2, big["w_up"])
    big["w_down"] = fetch("w_down", act)
    h2, z_next = _mlp_down(act, big["w_down"], h1, _row(g_next))
    saved = dict(h0=h, z1=z, proj=proj, qkv=qkv, c=c, c_t=c_t, attn=attn, lse_b=lse_b, hr=hr, rec=rec, h1=h1,
                 z2=z2, mix=mix, act=act, up=up, wa_d=wa_d, wx_d=wx_d, b_f_pad=b_f_pad, big=big)
    return h2, z_next, saved


def _backward_mlp(l, dh, dh_b, sv, p, tok):
    w_up, w_down = sv["big"]["w_up"], sv["big"]["w_down"]
    fc = w_up.shape[2]
    dup = _mlp_dup(dh_b, w_down, sv["up"])
    g_down = _grad_w(sv["act"], dh_b)
    g_up = _grad_w(sv["z2"], dup, cols_to_chips=True)
    dh, dh_b, dg2 = _dx_norm_bwd([(dup, j, fc) for j in range(N_CHIPS)], w_up, _full(w_up.shape),
                                 lambda w_ref, j: w_ref[j], sv["h1"], _row(p["mlp_norm_g"][l] + tok), dh)
    big = dict(w_down=g_down.reshape((N_CHIPS, -1) + g_down.shape[1:]), w_up=g_up)
    return dh, dh_b, big, dict(mlp_norm_g=dg2[0])


def _backward_mixer(l, dh, dh_b, sv, p, tok):
    d = dh.shape[1]
    att_w = d // 2
    rec_w = d - att_w
    nh = att_w // HEAD_DIM
    xr_blk = 3 * att_w // rec_w
    small = {}
    g_out = _grad_w(sv["mix"], dh_b)
    dattn, drec, dg_mix = _mixer_bwd(dh_b, sv["big"]["w_out"], sv["attn"], sv["rec"],
                                     _row(p["attn_out_g"][l] + tok), _row(p["rec_out_g"][l]))
    small["attn_out_g"] = dg_mix[0, :att_w]
    small["rec_out_g"] = dg_mix[0, att_w:]
    dxr, dyr, dwa, dwx, sm = _rec_bwd(
        sv["proj"], xr_blk, xr_blk + 1, rec_w, sv["hr"], drec, p["conv_w"][l], _row(p["conv_b"][l]), sv["wa_d"],
        _row(p["b_gate_a"][l]), sv["wx_d"], _row(p["b_gate_x"][l]), _row(p["lru_L"][l]))
    small.update(conv_w=sm[:CONV_WIDTH], conv_b=sm[4], b_gate_a=sm[5], b_gate_x=sm[6], lru_L=sm[7],
                 w_gate_a=_diag_blocks(dwa, N_REC_BLOCKS), w_gate_x=_diag_blocks(dwx, N_REC_BLOCKS))
    dq, dk, dv, dct = _attn_bwd(sv["qkv"], sv["c"], sv["c_t"], sv["lse_b"], dattn, nh)
    df, db_f = _fgate_bwd(sv["proj"], sv["b_f_pad"], dct)
    small["b_f"] = db_f[0, :nh]
    pieces = [dq, dk, dv, dxr, dyr, df]
    offs = [0, att_w, 2 * att_w, 3 * att_w, 3 * att_w + rec_w, 3 * att_w + 2 * rec_w]
    gq, gk, gv, gxr, gyr, gf = [_grad_w(sv["z1"], pc) for pc in pieces]
    g_in = jnp.concatenate([gq, gk, gv, gf[:, :nh], gxr, gyr], axis=1)
    w_big = sv["big"]["w_in_big"]
    widths = [pc.shape[1] for pc in pieces]
    dh, dh_b, dg1 = _dx_norm_bwd(
        [(pc, 0, wd) for pc, wd in zip(pieces, widths)], w_big, _full(w_big.shape),
        lambda w_ref, i: w_ref[:, offs[i]:offs[i] + widths[i]], sv["h0"], _row(p["attn_norm_g"][l]), dh)
    small["attn_norm_g"] = dg1[0]
    big = dict(w_in=jnp.moveaxis(g_in.reshape(d, N_CHIPS, -1), 1, 0),
               w_out=g_out.reshape((N_CHIPS, -1) + g_out.shape[1:]))
    return dh, dh_b, big, small


def _pack_w_in(w_in, att_w, rec_w, nh):
    qkv = w_in[..., :3 * att_w]
    f = w_in[..., 3 * att_w:3 * att_w + nh]
    xy = w_in[..., 3 * att_w + nh:]
    fpad = jnp.zeros(f.shape[:-1] + (LANES - nh,), w_in.dtype)
    return jnp.concatenate([qkv, xy, f, fpad], axis=-1)


ANY = pl.BlockSpec(memory_space=pl.ANY)


def _coords():
    return lax.axis_index("x"), lax.axis_index("y"), lax.axis_index("c")


def _other_chips(x, y):
    return [(1 - x, y), (x, 1 - y), (1 - x, 1 - y)]


def _remote(src, dst, send_sems, recv_sems, k, to):
    return pltpu.make_async_remote_copy(src_ref=src, dst_ref=dst, send_sem=send_sems.at[k],
                                        recv_sem=recv_sems.at[k], device_id=to, device_id_type=MESH)


def _all_gather_chips(shards):
    n = len(shards)
    per = 6

    def body(*refs):
        ins, outs = refs[:n], refs[n:2 * n]
        send_sems, recv_sems, local_sems = refs[2 * n:]
        x, y, c = _coords()
        me = 2 * x + y
        sibling = (x, y, 1 - c)
        chips = _other_chips(x, y)
        local = [pltpu.make_async_copy(ins[t], outs[t].at[me], local_sems.at[t]) for t in range(n)]
        for cp in local:
            cp.start()
        sends = []
        for t in range(n):
            for j, (px, py) in enumerate(chips):
                cp = _remote(ins[t].at[c], outs[t].at[me, c], send_sems, recv_sems, per * t + j, (px, py, c))
                cp.start()
                sends.append(cp)
        for t in range(n):
            for j, (px, py) in enumerate(chips):
                landed = outs[t].at[2 * px + py, c]
                _remote(landed, landed, send_sems, recv_sems, per * t + j, (px, py, c)).wait_recv()
                cp = _remote(landed, landed, send_sems, recv_sems, per * t + 3 + j, sibling)
                cp.start()
                sends.append(cp)
        for t in range(n):
            for j, (px, py) in enumerate(chips):
                passed = outs[t].at[2 * px + py, 1 - c]
                _remote(passed, passed, send_sems, recv_sems, per * t + 3 + j, sibling).wait_recv()
        for cp in sends:
            cp.wait_send()
        for cp in local:
            cp.wait()

    return _call(body, tuple(shards), name="all_gather_chips",
                 in_specs=[ANY] * n, out_specs=[ANY] * n,
                 out_shape=[jax.ShapeDtypeStruct((N_CHIPS,) + s.shape, s.dtype) for s in shards],
                 scratch_shapes=[pltpu.SemaphoreType.DMA((per * n,)), pltpu.SemaphoreType.DMA((per * n,)),
                                 pltpu.SemaphoreType.DMA((n,))])


HBM = pl.BlockSpec(memory_space=pltpu.HBM)
SEM = pl.BlockSpec(memory_space=pltpu.SEMAPHORE)
DATAFLOW = pltpu.SideEffectType.DATAFLOW_SIDE_EFFECTING


def _in_hbm(a):
    return pltpu.with_memory_space_constraint(a, pltpu.HBM)


def _push_start(srcs, lands, own_slab, name):
    n = len(srcs)
    same = own_slab and all(s is ld for s, ld in zip(srcs, lands))
    n_in = n if same else 2 * n

    def body(*refs):
        src_refs = refs[:n]
        land_refs = src_refs if same else refs[n:2 * n]
        send_sems, recv_sems = refs[n_in], refs[n_in + 1]
        token = refs[-1]
        x, y, c = _coords()
        me = 2 * x + y
        for t in range(n):
            for px, py in _other_chips(x, y):
                src = src_refs[t].at[me if own_slab else 2 * px + py]
                _remote(src, land_refs[t].at[me], send_sems, recv_sems, t, (px, py, c)).start()
        token[...] = jnp.zeros_like(token)

    operands = tuple(srcs) if same else tuple(srcs) + tuple(lands)
    res = _call(
        body, [_in_hbm(a) for a in operands], name=name,
        out_shape=(pltpu.SemaphoreType.DMA((n,)), pltpu.SemaphoreType.DMA((n,)))
        + tuple(pltpu.HBM(a.shape, a.dtype) for a in operands) + (jax.ShapeDtypeStruct((SUBLANES, LANES), F32),),
        in_specs=[HBM] * n_in, out_specs=(SEM, SEM) + (HBM,) * n_in + (pl.BlockSpec(memory_space=pltpu.VMEM),),
        input_output_aliases={i: 2 + i for i in range(n_in)}, side_effects=DATAFLOW, hbm_results=False)
    send_sems, recv_sems, token = res[0], res[1], res[-1]
    srcs_thru = res[2:2 + n]
    lands_thru = srcs_thru if same else res[2 + n:2 + 2 * n]
    return send_sems, recv_sems, srcs_thru, lands_thru, token


def _push_wait(send_sems, recv_sems, ids, srcs, lands, after, name):
    n = len(lands)
    same = all(s is ld for s, ld in zip(srcs, lands))
    n_in = n if same else 2 * n

    def body(*refs):
        land_refs = refs[:n] if same else refs[n:2 * n]
        send_sems, recv_sems = refs[n_in], refs[n_in + 1]
        x, y, c = _coords()
        for t in range(n):
            three = land_refs[t].at[pl.ds(0, N_CHIPS - 1)]
            arrivals = _remote(three, three, send_sems, recv_sems, ids[t], (x, y, c))
            arrivals.wait_send()
            arrivals.wait_recv()

    operands = tuple(lands) if same else tuple(srcs) + tuple(lands)
    res = _call(
        body, operands + (send_sems, recv_sems, after), name=name,
        out_shape=tuple(pltpu.HBM(a.shape, a.dtype) for a in operands),
        in_specs=[HBM] * n_in + [SEM, SEM, ANY], out_specs=(HBM,) * n_in,
        input_output_aliases={i: i for i in range(n_in)}, side_effects=DATAFLOW)
    return list(res) if same else (list(res[:n]), list(res[n:]))


def _sum_partials(part, landed, chip):
    _, rows, cols = part.shape
    br = _divisor_tile(rows, 16, ELEM_ROWS)

    def body(chip_ref, own_ref, a_ref, b_ref, c_ref, o_ref):
        o_ref[...] = ((own_ref[...].astype(F32) + a_ref[...].astype(F32)) + b_ref[...].astype(F32)) \
            + c_ref[...].astype(F32)

    def other(k):
        return pl.BlockSpec((None, br, cols), lambda i, ch: (jnp.where(ch[0] <= k, k + 1, k), i, 0))

    spec = pltpu.PrefetchScalarGridSpec(
        num_scalar_prefetch=1, grid=(rows // br,),
        in_specs=[pl.BlockSpec((None, br, cols), lambda i, ch: (ch[0], i, 0)), other(0), other(1), other(2)],
        out_specs=pl.BlockSpec((br, cols), lambda i, ch: (i, 0)))
    return _call(body, (chip, part, landed, landed, landed), name="sum_partials", grid_spec=spec,
                 out_shape=jax.ShapeDtypeStruct((rows, cols), F32), semantics=("parallel",))


def _cast_to_slab(w, l, chip):
    _, rows, cols = w.shape
    br = _divisor_tile(rows, 16, ELEM_ROWS)

    def body(chip_ref, w_ref, o_ref):
        o_ref[...] = w_ref[...].astype(BF16)

    spec = pltpu.PrefetchScalarGridSpec(
        num_scalar_prefetch=1, grid=(rows // br,),
        in_specs=[pl.BlockSpec((None, br, cols), lambda i, ch: (l, i, 0))],
        out_specs=pl.BlockSpec((None, br, cols), lambda i, ch: (ch[0], i, 0)))
    return _call(body, (chip, w), name="cast_to_slab", grid_spec=spec,
                 out_shape=jax.ShapeDtypeStruct((N_CHIPS, rows, cols), BF16), semantics=("parallel",))


def _swap_with_sibling(arrs):
    n = len(arrs)

    def body(*refs):
        ins, outs = refs[:n], refs[n:2 * n]
        send_sems, recv_sems = refs[2 * n:]
        x, y, c = _coords()
        cps = [_remote(ins[t], outs[t], send_sems, recv_sems, t, (x, y, 1 - c)) for t in range(n)]
        for cp in cps:
            cp.start()
        for cp in cps:
            cp.wait_recv()
        for cp in cps:
            cp.wait_send()

    return _call(body, tuple(arrs), name="swap_with_sibling",
                 in_specs=[ANY] * n, out_specs=[ANY] * n,
                 out_shape=[jax.ShapeDtypeStruct(s.shape, s.dtype) for s in arrs],
                 scratch_shapes=[pltpu.SemaphoreType.DMA((n,)), pltpu.SemaphoreType.DMA((n,))])


def _all_gather_devices(buf):
    per = N_DEV - 1

    def body(in_ref, out_ref, send_sems, recv_sems, local_sem):
        x, y, c = _coords()
        me = 4 * x + 2 * y + c
        local = pltpu.make_async_copy(in_ref, out_ref.at[me], local_sem)
        local.start()
        peers = []
        for k in range(1, N_DEV):
            fx, fy, fc = (k >> 2) & 1, (k >> 1) & 1, k & 1
            peers.append((x ^ fx, y ^ fy, c ^ fc))
        sends = [_remote(in_ref, out_ref.at[me], send_sems, recv_sems, k, peer) for k, peer in enumerate(peers)]
        for cp in sends:
            cp.start()
        for k, (px, py, pc) in enumerate(peers):
            landed = out_ref.at[4 * px + 2 * py + pc]
            _remote(landed, landed, send_sems, recv_sems, k, (px, py, pc)).wait_recv()
        for cp in sends:
            cp.wait_send()
        local.wait()

    return _call(body, (buf,), name="all_gather_devices", in_specs=[ANY], out_specs=ANY,
                 out_shape=jax.ShapeDtypeStruct((N_DEV,) + buf.shape, buf.dtype),
                 scratch_shapes=[pltpu.SemaphoreType.DMA((per,)), pltpu.SemaphoreType.DMA((per,)),
                                 pltpu.SemaphoreType.DMA(())])


ELEM_ROWS = 256


def _sum_slabs(r):
    n, rows, cols = r.shape
    br = _divisor_tile(rows, 16, ELEM_ROWS)

    def body(r_ref, o_ref):
        acc = r_ref[0].astype(F32)
        for j in range(1, n):
            acc = acc + r_ref[j].astype(F32)
        o_ref[...] = acc

    return _call(body, (r,), name="sum_slabs", grid=(rows // br,),
                 in_specs=[pl.BlockSpec((n, br, cols), lambda i: (0, i, 0))],
                 out_specs=pl.BlockSpec((br, cols), lambda i: (i, 0)),
                 out_shape=jax.ShapeDtypeStruct((rows, cols), F32), semantics=("parallel",))


def _adamw_math(w, g, m, v):
    c1 = 1.0 - ADAM_B1 ** ADAM_STEP
    c2 = 1.0 - ADAM_B2 ** ADAM_STEP
    nm = ADAM_B1 * m + (1.0 - ADAM_B1) * g
    nv = ADAM_B2 * v + (1.0 - ADAM_B2) * (g * g)
    delta = -ADAM_LR * ((nm / c1) / (jnp.sqrt(nv / c2) + ADAM_EPS) + ADAM_WD * w)
    return delta, nm, nv


def _adamw(w, g, m, v):
    rows, cols = w.shape
    br = _divisor_tile(rows, 8, ELEM_ROWS)

    def body(w_ref, g_ref, m_ref, v_ref, d_ref, nm_ref, nv_ref):
        d_ref[...], nm_ref[...], nv_ref[...] = _adamw_math(w_ref[...], g_ref[...], m_ref[...], v_ref[...])

    blk = pl.BlockSpec((br, cols), lambda i: (i, 0))
    return _call(body, (w, g, m, v), name="adamw", grid=(rows // br,),
                 in_specs=[blk] * 4, out_specs=[blk] * 3,
                 out_shape=[jax.ShapeDtypeStruct((rows, cols), F32)] * 3, semantics=("parallel",))


def _adamw_layer(w, m, v, l, g_mine, g_theirs, prev):
    _, rows, cols = w.shape
    br = _divisor_tile(rows, 8, ELEM_ROWS)

    def body(w_ref, m_ref, v_ref, ga_ref, gb_ref, *rest):
        g_ref, d_ref, nm_ref, nv_ref = rest[4:]
        g = ga_ref[...] + gb_ref[...]
        g_ref[...] = g
        d_ref[...], nm_ref[...], nv_ref[...] = _adamw_math(w_ref[...], g, m_ref[...], v_ref[...])

    slot = pl.BlockSpec((None, br, cols), lambda i: (l, i, 0))
    blk = pl.BlockSpec((br, cols), lambda i: (i, 0))
    return _call(body, (w, m, v, g_mine, g_theirs) + tuple(prev), name="adamw_layer", grid=(rows // br,),
                 in_specs=[slot] * 3 + [blk] * 2 + [ANY] * 4, out_specs=[slot] * 4,
                 out_shape=[jax.ShapeDtypeStruct(w.shape, F32)] * 4,
                 input_output_aliases={5: 0, 6: 1, 7: 2, 8: 3}, semantics=("parallel",))


BIG = ("w_in", "w_out", "w_up", "w_down")
WEIGHTS = ("meta", "attn_norm_g", "w_in", "b_f", "conv_w", "conv_b", "w_gate_a", "b_gate_a", "w_gate_x",
           "b_gate_x", "lru_L", "attn_out_g", "rec_out_g", "w_out", "mlp_norm_g", "w_up", "w_down", "final_g")
SMALL = tuple(k for k in WEIGHTS if k not in BIG)
COL_SHARDED_SMALL = ("meta", "conv_w")
PACK_UNIT = ELEM_ROWS * LANES


def _pack(arrs):
    flat = jnp.concatenate([a.reshape(-1) for a in arrs])
    total = -(-flat.shape[0] // PACK_UNIT) * PACK_UNIT
    return jnp.pad(flat, (0, total - flat.shape[0])).reshape(-1, LANES)


def _unpack(buf, shapes):
    flat = buf.reshape(-1)
    out, off = [], 0
    for s in shapes:
        size = math.prod(s)
        out.append(flat[off:off + size].reshape(s))
        off += size
    return out


def _halves(a):
    return a.reshape((2, a.shape[0] // 2) + a.shape[1:])


def _cols_from_chips(g):
    return jnp.moveaxis(g, 0, -2).reshape(g.shape[1:-1] + (N_CHIPS * g.shape[-1],))


def kernel(x, meta, attn_norm_g, w_in, b_f, conv_w, conv_b, w_gate_a, b_gate_a, w_gate_x, b_gate_x, lru_L, attn_out_g, rec_out_g, w_out, mlp_norm_g, w_up, w_down, final_g, loss_target, m_meta, m_attn_norm_g, m_w_in, m_b_f, m_conv_w, m_conv_b, m_w_gate_a, m_b_gate_a, m_w_gate_x, m_b_gate_x, m_lru_L, m_attn_out_g, m_rec_out_g, m_w_out, m_mlp_norm_g, m_w_up, m_w_down, m_final_g, v_meta, v_attn_norm_g, v_w_in, v_b_f, v_conv_w, v_conv_b, v_w_gate_a, v_b_gate_a, v_w_gate_x, v_b_gate_x, v_lru_L, v_attn_out_g, v_rec_out_g, v_w_out, v_mlp_norm_g, v_w_up, v_w_down, v_final_g):
    w = dict(meta=meta, attn_norm_g=attn_norm_g, w_in=w_in, b_f=b_f, conv_w=conv_w, conv_b=conv_b,
             w_gate_a=w_gate_a, b_gate_a=b_gate_a, w_gate_x=w_gate_x, b_gate_x=b_gate_x, lru_L=lru_L,
             attn_out_g=attn_out_g, rec_out_g=rec_out_g, w_out=w_out, mlp_norm_g=mlp_norm_g, w_up=w_up,
             w_down=w_down, final_g=final_g)
    m = dict(meta=m_meta, attn_norm_g=m_attn_norm_g, w_in=m_w_in, b_f=m_b_f, conv_w=m_conv_w, conv_b=m_conv_b,
             w_gate_a=m_w_gate_a, b_gate_a=m_b_gate_a, w_gate_x=m_w_gate_x, b_gate_x=m_b_gate_x, lru_L=m_lru_L,
             attn_out_g=m_attn_out_g, rec_out_g=m_rec_out_g, w_out=m_w_out, mlp_norm_g=m_mlp_norm_g,
             w_up=m_w_up, w_down=m_w_down, final_g=m_final_g)
    v = dict(meta=v_meta, attn_norm_g=v_attn_norm_g, w_in=v_w_in, b_f=v_b_f, conv_w=v_conv_w, conv_b=v_conv_b,
             w_gate_a=v_w_gate_a, b_gate_a=v_b_gate_a, w_gate_x=v_w_gate_x, b_gate_x=v_b_gate_x, lru_L=v_lru_L,
             attn_out_g=v_attn_out_g, rec_out_g=v_rec_out_g, w_out=v_w_out, mlp_norm_g=v_mlp_norm_g,
             w_up=v_w_up, w_down=v_w_down, final_g=v_final_g)
    s_len, d = x.shape[1], x.shape[2]
    depth = w_in.shape[0]
    att_w = d // 2
    rec_w = d - att_w
    nh = att_w // HEAD_DIM
    chip = 2 * lax.axis_index("x") + lax.axis_index("y")

    g_conv, g_meta = [g.reshape((N_CHIPS, g.shape[1] * g.shape[2]) + g.shape[3:])
                      for g in _all_gather_chips([_halves(w["conv_w"]), _halves(w["meta"])])]
    p = dict(w)
    p["conv_w"] = _cols_from_chips(g_conv)
    meta_full = jnp.moveaxis(g_meta, 0, 1).reshape(N_META, d)

    chip1 = chip.reshape(1).astype(jnp.int32)
    pushes, tokens = [], []
    for l in range(depth):
        slabs = [_cast_to_slab(w[k], l, chip1) for k in BIG]
        send_sems, recv_sems, _, lands, token = _push_start(slabs, slabs, True, f"weights_start_{l}")
        pushes.append((send_sems, recv_sems, lands))
        tokens.append(token[0, 0])

    t_len = N_META + s_len
    pad = -t_len % SEQ_TILE
    h = jnp.concatenate([meta_full, x[0], jnp.zeros((pad, d), F32)], axis=0)
    tgt = jnp.concatenate([jnp.zeros((N_META, d), F32), loss_target[0], jnp.zeros((pad, d), F32)], axis=0)
    z = _rms_fwd(h, _row(p["attn_norm_g"][0] + sum(tokens)))
    saved = []
    for l in range(depth):
        send_sems, recv_sems, lands = pushes[l]

        def fetch(k, after, l=l, send_sems=send_sems, recv_sems=recv_sems, lands=lands):
            i = BIG.index(k)
            return _push_wait(send_sems, recv_sems, [i], [lands[i]], [lands[i]], after, f"{k}_wait_{l}")[0]

        g_next = p["attn_norm_g"][l + 1] if l + 1 < depth else p["final_g"]
        h, z, sv = _forward_layer(l, h, z, p, fetch, g_next)
        saved.append(sv)
    dh, dh_b, dg_final, loss_part = _loss_bwd(h, _row(p["final_g"]), tgt, s_len)

    small = {k: [None] * depth for k in SMALL if k not in ("meta", "final_g")}
    pushes = [None] * depth
    tok = 0.0
    for l in reversed(range(depth)):
        dh, dh_b, big_mlp, sm_mlp = _backward_mlp(l, dh, dh_b, saved[l], p, tok)
        parts = [big_mlp["w_down"], big_mlp["w_up"]]
        push_mlp = _push_start(parts, [lax.empty(a.shape, a.dtype) for a in parts], False, f"mlp_grads_start_{l}")
        dh, dh_b, big_mix, sm_mix = _backward_mixer(l, dh, dh_b, saved[l], p, push_mlp[4][0, 0])
        parts = [big_mix["w_out"], big_mix["w_in"]]
        push_mix = _push_start(parts, [lax.empty(a.shape, a.dtype) for a in parts], False, f"mixer_grads_start_{l}")
        tok = push_mix[4][0, 0]
        pushes[l] = {("w_down", "w_up"): push_mlp, ("w_out", "w_in"): push_mix}
        for k, val in {**sm_mlp, **sm_mix}.items():
            small[k][l] = val
    grads = {k: jnp.stack(val) for k, val in small.items()}
    grads["final_g"] = dg_final[0]
    grads["meta"] = dh[:N_META]
    dx = dh[N_META:t_len]

    last_token = pushes[0][("w_out", "w_in")][4]
    outs = {k: [lax.empty(w[k].shape, F32) for _ in range(4)] for k in BIG}
    for l in reversed(range(depth)):
        sums = {}
        for names, (send_sems, recv_sems, parts, lands, _) in pushes[l].items():
            parts, landed = _push_wait(send_sems, recv_sems, [0, 1], parts, lands, last_token,
                                       f"{names[0]}_grads_wait_{l}")
            for k, part, land in zip(names, parts, landed):
                sums[k] = _sum_partials(part, land, chip1)
        theirs = _swap_with_sibling([sums[k] for k in BIG])
        for k, other in zip(BIG, theirs):
            outs[k] = _adamw_layer(w[k], m[k], v[k], l, sums[k], other, outs[k])
    out_g, out_d, out_m, out_v = [{k: outs[k][i] for k in BIG} for i in range(4)]

    full_shapes = [grads[k].shape for k in SMALL] + [(1,)]
    packed = _pack([grads[k].astype(F32) for k in SMALL] + [loss_part[0, :1]])
    total = _sum_slabs(_all_gather_devices(packed))
    small_g = dict(zip(SMALL + ("loss",), _unpack(total, full_shapes)))
    for k in COL_SHARDED_SMALL:
        n = w[k].shape[-1]
        small_g[k] = lax.dynamic_slice_in_dim(small_g[k], chip * n, n, axis=small_g[k].ndim - 1)
    local_shapes = [w[k].shape for k in SMALL]
    res = _adamw(_pack([w[k] for k in SMALL]), _pack([small_g[k] for k in SMALL]),
                 _pack([m[k] for k in SMALL]), _pack([v[k] for k in SMALL]))
    out_g.update({k: small_g[k] for k in SMALL})
    for dst, buf in zip((out_d, out_m, out_v), res):
        dst.update(zip(SMALL, _unpack(buf, local_shapes)))

    return (small_g["loss"].reshape(()), dx[None],
            *[out_g[k] for k in WEIGHTS], *[out_d[k] for k in WEIGHTS],
            *[out_m[k] for k in WEIGHTS], *[out_v[k] for k in WEIGHTS])
```

```python
import functools
import math

import jax
import jax.numpy as jnp
from jax import lax
from jax.experimental import pallas as pl
from jax.experimental.pallas import tpu as pltpu

F32 = jnp.float32
BF16 = jnp.bfloat16

N_META = 16
HEAD_DIM = 64
N_REC_BLOCKS = 8
CONV_WIDTH = 4
RG_C = 8.0
NORM_EPS = 1e-6
ADAM_LR = 0.001
ADAM_B1 = 0.9
ADAM_B2 = 0.999
ADAM_EPS = 1e-08
ADAM_WD = 0.01
ADAM_STEP = 10

LANES = 128
SUBLANES = 8
SEQ_TILE = 128
VMEM_CAP = 60 * 2**20
VMEM_SLACK = 6 * 2**20
NEG_BIG = -1e30
N_CHIPS = 4
N_DEV = 8
MESH = pl.DeviceIdType.MESH


def _nbytes(shape, dtype):
    return math.prod(shape) * jnp.dtype(dtype).itemsize


def _call(body, args, *, name, out_shape, grid=(), in_specs=None, out_specs=None, scratch_shapes=(),
          grid_spec=None, semantics=None, vmem_bytes=None, side_effects=None, hbm_results=True, **kw):
    cp = {}
    if semantics is not None:
        cp["dimension_semantics"] = semantics
    if vmem_bytes is not None:
        cp["vmem_limit_bytes"] = int(min(VMEM_CAP, vmem_bytes + VMEM_SLACK))
    if side_effects is not None:
        cp["has_side_effects"] = side_effects
    if grid_spec is not None:
        kw["grid_spec"] = grid_spec
    else:
        kw.update(grid=grid, in_specs=in_specs, out_specs=out_specs, scratch_shapes=scratch_shapes)
    if hbm_results:
        out_shape = jax.tree.map(
            lambda s: pltpu.HBM(s.shape, s.dtype) if isinstance(s, jax.ShapeDtypeStruct) else s, out_shape)
    fn = pl.pallas_call(
        body, name=name, out_shape=out_shape,
        compiler_params=pltpu.CompilerParams(**cp), **kw)
    return fn(*[_in_hbm(a) if jnp.issubdtype(getattr(a, "dtype", jnp.int32), jnp.floating) else a for a in args])


def _divisor_tile(n, unit, target):
    best = None
    for t in range(unit, min(n, target) + 1, unit):
        if n % t == 0:
            best = t
    return n if best is None else best


def _sigmoid(x):
    return 1.0 / (1.0 + jnp.exp(-x))


def _log1p_unit(e):
    series = e * (1.0 - e * (0.5 - e * (1.0 / 3.0)))
    return jnp.where(e < 1e-2, series, jnp.log(1.0 + e))


def _log_sigmoid(x):
    return jnp.minimum(x, 0.0) - _log1p_unit(jnp.exp(-jnp.abs(x)))


def _expm1_nonpos(x):
    small = x * (1.0 + x * (1.0 / 2 + x * (1.0 / 6 + x * (1.0 / 24 + x * (1.0 / 120 + x * (1.0 / 720))))))
    return jnp.where(x > -0.25, small, jnp.exp(x) - 1.0)


_GELU_K = math.sqrt(2.0 / math.pi)
_GELU_C = 0.044715


def _gelu_and_grad(y):
    th = jnp.tanh(_GELU_K * (y + _GELU_C * y * y * y))
    g = 0.5 * y * (1.0 + th)
    dg = 0.5 * (1.0 + th) + 0.5 * y * (1.0 - th * th) * _GELU_K * (1.0 + 3.0 * _GELU_C * y * y)
    return g, dg


def _rstd(x):
    return lax.rsqrt(jnp.mean(x * x, axis=-1, keepdims=True) + NORM_EPS)


def _rms_bwd(dz, x, g):
    rs = _rstd(x)
    xh = x * rs
    dgp = jnp.sum(dz * xh, axis=0, keepdims=True)
    dxh = dz * g
    dx = rs * (dxh - xh * jnp.mean(dxh * xh, axis=-1, keepdims=True))
    return dx, dgp


def _dot(a, b):
    return jnp.dot(a, b, preferred_element_type=F32)


def _dot_nt(a, b):
    return lax.dot_general(a, b, (((1,), (1,)), ((), ())), preferred_element_type=F32)


def _dot_tn(a, b):
    return lax.dot_general(a, b, (((0,), (0,)), ((), ())), preferred_element_type=F32)


def _full(shape):
    nd = len(shape)
    return pl.BlockSpec(shape, lambda *_: (0,) * nd)


def _rms_fwd(h, g):
    tp, d = h.shape
    tm = _divisor_tile(tp, 16, 544)

    def body(h_ref, g_ref, z_ref):
        x = h_ref[...]
        z_ref[...] = (x * _rstd(x) * g_ref[...]).astype(BF16)

    return _call(body, (h, g), name="rms_fwd", grid=(tp // tm,),
                 in_specs=[pl.BlockSpec((tm, d), lambda i: (i, 0)), _full((1, d))],
                 out_specs=pl.BlockSpec((tm, d), lambda i: (i, 0)),
                 out_shape=jax.ShapeDtypeStruct((tp, d), BF16), semantics=("parallel",))


def _proj(z, w_big, att_w):
    tp, d = z.shape
    nb = w_big.shape[1]
    tn = _divisor_tile(nb, LANES, 512)
    assert (3 * att_w) % tn == 0
    n_qkv = 3 * att_w // tn
    scale = 1.0 / math.sqrt(HEAD_DIM)

    def body(z_ref, w_ref, p_ref, qkv_ref):
        j = pl.program_id(0)
        acc = _dot(z_ref[...], w_ref[...])
        p_ref[...] = acc

        @pl.when(j < n_qkv)
        def _():
            col = j * tn + lax.broadcasted_iota(jnp.int32, (1, tn), 1)
            qkv_ref[...] = (acc * jnp.where(col < att_w, scale, 1.0)).astype(BF16)

    vm = 2 * (_nbytes((tp, d), BF16) + _nbytes((d, tn), BF16) + _nbytes((tp, tn), F32) * 2)
    return _call(body, (z, w_big), name="proj", grid=(nb // tn,),
                 in_specs=[_full((tp, d)), pl.BlockSpec((d, tn), lambda j: (0, j))],
                 out_specs=[pl.BlockSpec((tp, tn), lambda j: (0, j)),
                            pl.BlockSpec((tp, tn), lambda j: (0, jnp.minimum(j, n_qkv - 1)))],
                 out_shape=[jax.ShapeDtypeStruct((tp, nb), F32),
                            jax.ShapeDtypeStruct((tp, 3 * att_w), BF16)],
                 semantics=("arbitrary",), vmem_bytes=vm)


def _tile_cumsum(x, row, reverse=False):
    for s in (1, 2, 4):
        if reverse:
            x = x + jnp.where(row < SUBLANES - s, pltpu.roll(x, SUBLANES - s, 0), 0.0)
        else:
            x = x + jnp.where(row >= s, pltpu.roll(x, s, 0), 0.0)
    return x


def _fgate_fwd(proj, b_f_pad, nh):
    tp, nb = proj.shape
    fblk = nb // LANES - 1

    def body(f_ref, b_ref, c_ref, ct_ref):
        b = b_ref[...]
        row = lax.broadcasted_iota(jnp.int32, (SUBLANES, LANES), 0)

        def step(i, carry):
            r0 = pl.multiple_of(i * SUBLANES, SUBLANES)
            lf = _log_sigmoid(f_ref[pl.ds(r0, SUBLANES), :] + b)
            x = _tile_cumsum(lf, row) + carry
            c_ref[pl.ds(r0, SUBLANES), :] = x
            return x[SUBLANES - 1:SUBLANES, :]

        lax.fori_loop(0, tp // SUBLANES, step, jnp.zeros((1, LANES), F32))
        ct_ref[...] = c_ref[...].T[:nh, :]

    return _call(body, (proj, b_f_pad), name="fgate_fwd", grid=(1,),
                 in_specs=[pl.BlockSpec((tp, LANES), lambda i: (0, fblk)), _full((1, LANES))],
                 out_specs=[_full((tp, LANES)), _full((nh, tp))],
                 out_shape=[jax.ShapeDtypeStruct((tp, LANES), F32), jax.ShapeDtypeStruct((nh, tp), F32)],
                 semantics=("arbitrary",))


def _fgate_bwd(proj, b_f_pad, dct):
    tp, nb = proj.shape
    nh = dct.shape[0]
    fblk = nb // LANES - 1

    def body(f_ref, b_ref, dct_ref, df_ref, db_ref, dc_s):
        b = b_ref[...]
        row = lax.broadcasted_iota(jnp.int32, (SUBLANES, LANES), 0)
        nt = tp // SUBLANES
        dc_s[...] = jnp.concatenate([dct_ref[...], jnp.zeros((LANES - nh, tp), F32)], axis=0).T

        def step(i, carry):
            suffix, acc = carry
            r0 = pl.multiple_of((nt - 1 - i) * SUBLANES, SUBLANES)
            dlf = _tile_cumsum(dc_s[pl.ds(r0, SUBLANES), :], row, reverse=True) + suffix
            df = dlf * _sigmoid(-(f_ref[pl.ds(r0, SUBLANES), :] + b))
            dc_s[pl.ds(r0, SUBLANES), :] = df
            return dlf[0:1, :], acc + df

        _, acc = lax.fori_loop(0, nt, step, (jnp.zeros((1, LANES), F32), jnp.zeros((SUBLANES, LANES), F32)))
        df_ref[...] = dc_s[...].astype(BF16)
        db_ref[...] = jnp.broadcast_to(jnp.sum(acc, axis=0, keepdims=True), (SUBLANES, LANES))

    return _call(body, (proj, b_f_pad, dct), name="fgate_bwd", grid=(1,),
                 in_specs=[pl.BlockSpec((tp, LANES), lambda i: (0, fblk)), _full((1, LANES)), _full((nh, tp))],
                 out_specs=[_full((tp, LANES)), _full((SUBLANES, LANES))],
                 out_shape=[jax.ShapeDtypeStruct((tp, LANES), BF16),
                            jax.ShapeDtypeStruct((SUBLANES, LANES), F32)],
                 scratch_shapes=[pltpu.VMEM((tp, LANES), F32)], semantics=("arbitrary",))


ATT_BQ = 128


ATT_BUCKET = 3
ATT_HEADS = 4


def _for_bucket(i, nq, fn):
    for lo in range(0, nq, ATT_BUCKET):
        hi = min(lo + ATT_BUCKET, nq)
        spans = ([(0, lo * ATT_BQ, False)] if lo else []) + [(lo * ATT_BQ, hi * ATT_BQ, True)]
        pl.when(jnp.logical_and(i >= lo, i < hi))(functools.partial(fn, spans))


def _head_column(c_blk, h):
    lane = lax.broadcasted_iota(jnp.int32, c_blk.shape, 1)
    return jnp.sum(jnp.where(lane == h, c_blk, 0.0), axis=1, keepdims=True)


def _span_logits(q, k_ref, ct_ref, h, lo, i, span):
    k0, k1, needs_mask = span
    t = _dot_nt(q, k_ref[k0:k1, lo:lo + HEAD_DIM]) - ct_ref[pl.ds(h, 1), k0:k1]
    if needs_mask:
        rows = i * ATT_BQ + lax.broadcasted_iota(jnp.int32, (ATT_BQ, k1 - k0), 0)
        cols = k0 + lax.broadcasted_iota(jnp.int32, (ATT_BQ, k1 - k0), 1)
        t = jnp.where(cols <= rows, t, NEG_BIG)
    return t


def _attn_fwd(qkv, c, c_t, nh):
    tp = qkv.shape[0]
    att_w = nh * HEAD_DIM
    ng = nh // ATT_HEADS
    gw = ATT_HEADS * HEAD_DIM
    bq = ATT_BQ
    nq = tp // bq

    def body(q_ref, k_ref, v_ref, c_ref, ct_ref, o_ref, lse_ref):
        p = pl.program_id(0)
        i = pl.program_id(1)

        def compute(spans):
            outs, lses = [], []
            for hh in range(ATT_HEADS):
                lo = HEAD_DIM * hh
                h = ATT_HEADS * p + hh
                q = q_ref[:, lo:lo + HEAD_DIM]
                ts = [_span_logits(q, k_ref, ct_ref, h, lo, i, sp) for sp in spans]
                m = functools.reduce(jnp.maximum, [jnp.max(t, axis=1, keepdims=True) for t in ts])
                es = [jnp.exp(t - m) for t in ts]
                l = sum(jnp.sum(e, axis=1, keepdims=True) for e in es)
                o = sum(_dot(e.astype(BF16), v_ref[k0:k1, lo:lo + HEAD_DIM]) for e, (k0, k1, _) in zip(es, spans))
                outs.append(o / l)
                lses.append(jnp.broadcast_to(m + _head_column(c_ref[...], h) + jnp.log(l), (bq, HEAD_DIM)))
            o_ref[...] = jnp.concatenate(outs, axis=1)
            lse_ref[...] = jnp.concatenate(lses, axis=1)

        _for_bucket(i, nq, compute)

    blk = pl.BlockSpec((bq, gw), lambda p, i: (i, p))
    vm = 4 * _nbytes((tp, gw), BF16) + 8 * ATT_HEADS * _nbytes((bq, tp), F32)
    return _call(body, (qkv, qkv, qkv, c, c_t), name="attn_fwd", grid=(ng, nq),
                 in_specs=[blk,
                           pl.BlockSpec((tp, gw), lambda p, i: (0, ng + p)),
                           pl.BlockSpec((tp, gw), lambda p, i: (0, 2 * ng + p)),
                           pl.BlockSpec((bq, LANES), lambda p, i: (i, 0)), _full((nh, tp))],
                 out_specs=[blk, blk],
                 out_shape=[jax.ShapeDtypeStruct((tp, att_w), F32)] * 2,
                 semantics=("parallel", "parallel"), vmem_bytes=vm)


def _attn_bwd(qkv, c, c_t, lse_b, do, nh):
    tp = qkv.shape[0]
    att_w = nh * HEAD_DIM
    ng = nh // ATT_HEADS
    gw = ATT_HEADS * HEAD_DIM
    bq = ATT_BQ
    nq = tp // bq
    scale = 1.0 / math.sqrt(HEAD_DIM)

    def body(q_ref, k_ref, v_ref, c_ref, ct_ref, lse_ref, do_ref, dq_ref, dk_ref, dv_ref, dct_ref, dk_s, dv_s):
        p = pl.program_id(0)
        i = pl.program_id(1)

        @pl.when(i == 0)
        def _():
            dk_s[...] = jnp.zeros_like(dk_s)
            dv_s[...] = jnp.zeros_like(dv_s)

        @pl.when(jnp.logical_and(i == 0, p == 0))
        def _():
            dct_ref[...] = jnp.zeros_like(dct_ref)

        def compute(spans):
            dqs = []
            dks = [[] for _ in spans]
            dvs = [[] for _ in spans]
            for hh in range(ATT_HEADS):
                lo = HEAD_DIM * hh
                h = ATT_HEADS * p + hh
                q = q_ref[:, lo:lo + HEAD_DIM]
                row_term = lse_ref[:, lo:lo + 1] - _head_column(c_ref[...], h)
                doutb = do_ref[:, lo:lo + HEAD_DIM].astype(BF16)
                prs = [jnp.exp(_span_logits(q, k_ref, ct_ref, h, lo, i, sp) - row_term) for sp in spans]
                dps = [_dot_nt(doutb, v_ref[k0:k1, lo:lo + HEAD_DIM]) for k0, k1, _ in spans]
                row_sum = sum(jnp.sum(pr * dp, axis=1, keepdims=True) for pr, dp in zip(prs, dps))
                dq = 0.0
                for n, ((k0, k1, _), pr, dp) in enumerate(zip(spans, prs, dps)):
                    ds = pr * (dp - row_sum)
                    dsb = ds.astype(BF16)
                    dq = dq + _dot(dsb, k_ref[k0:k1, lo:lo + HEAD_DIM])
                    dks[n].append(_dot_tn(dsb, q))
                    dvs[n].append(_dot_tn(pr.astype(BF16), doutb))
                    dct_ref[pl.ds(h, 1), k0:k1] = dct_ref[pl.ds(h, 1), k0:k1] - jnp.sum(ds, axis=0, keepdims=True)
                dqs.append(dq * scale)
            dq_ref[...] = jnp.concatenate(dqs, axis=1).astype(BF16)
            for n, (k0, k1, _) in enumerate(spans):
                dk_s[k0:k1, :] += jnp.concatenate(dks[n], axis=1)
                dv_s[k0:k1, :] += jnp.concatenate(dvs[n], axis=1)

        _for_bucket(i, nq, compute)

        @pl.when(i == nq - 1)
        def _():
            dk_ref[...] = dk_s[...].astype(BF16)
            dv_ref[...] = dv_s[...].astype(BF16)

    blk = pl.BlockSpec((bq, gw), lambda p, i: (i, p))
    col = pl.BlockSpec((tp, gw), lambda p, i: (0, p))
    vm = 6 * _nbytes((tp, gw), BF16) + 2 * _nbytes((tp, gw), F32) + 12 * ATT_HEADS * _nbytes((bq, tp), F32)
    return _call(body, (qkv, qkv, qkv, c, c_t, lse_b, do), name="attn_bwd", grid=(ng, nq),
                 in_specs=[blk,
                           pl.BlockSpec((tp, gw), lambda p, i: (0, ng + p)),
                           pl.BlockSpec((tp, gw), lambda p, i: (0, 2 * ng + p)),
                           pl.BlockSpec((bq, LANES), lambda p, i: (i, 0)), _full((nh, tp)), blk, blk],
                 out_specs=[blk, col, col, _full((nh, tp))],
                 out_shape=[jax.ShapeDtypeStruct((tp, att_w), BF16)] * 3 + [jax.ShapeDtypeStruct((nh, tp), F32)],
                 scratch_shapes=[pltpu.VMEM((tp, gw), F32)] * 2,
                 semantics=("arbitrary", "arbitrary"), vmem_bytes=vm)


REC_ROWS = 128
HALO = SUBLANES


def _conv_taps(cat):
    taps = []
    for k in range(CONV_WIDTH):
        sh = CONV_WIDTH - 1 - k
        taps.append((pltpu.roll(cat, sh, 0) if sh else cat)[HALO:])
    return taps


def _rec_gates(xc, wa_ref, ba_ref, wx_ref, bx_ref, l_ref):
    xcb = xc.astype(BF16)
    r = _sigmoid(_dot(xcb, wa_ref[...]) + ba_ref[...])
    ig = _sigmoid(_dot(xcb, wx_ref[...]) + bx_ref[...])
    ls = _log_sigmoid(l_ref[...])
    log_a = RG_C * r * ls
    return xcb, r, ig, ls, log_a


def _rec_fwd(proj, xr_blk, yr_blk, rec_w, conv_w, conv_b, wa, ba, wx, bx, lru):
    tp = proj.shape[0]
    w = rec_w
    r_rows = REC_ROWS
    nc = tp // r_rows
    cpb = w // LANES

    def body(xr_ref, yr_ref, cw_ref, cb_ref, wa_ref, ba_ref, wx_ref, bx_ref, l_ref,
             hr_ref, rec_ref, prev_s, carry_s, a_s, u_s):
        i = pl.program_id(0)

        @pl.when(i == 0)
        def _():
            prev_s[...] = jnp.zeros_like(prev_s)
            carry_s[...] = jnp.zeros_like(carry_s)

        x = xr_ref[...]
        taps = _conv_taps(jnp.concatenate([prev_s[...], x], axis=0))
        prev_s[...] = x[r_rows - HALO:]
        xc = cb_ref[...]
        for k in range(CONV_WIDTH):
            xc = xc + cw_ref[k:k + 1, :] * taps[k]
        _, r, ig, ls, log_a = _rec_gates(xc, wa_ref, ba_ref, wx_ref, bx_ref, l_ref)
        a_s[...] = jnp.exp(log_a)
        u_s[...] = jnp.sqrt(-_expm1_nonpos(2.0 * log_a)) * ig * xc

        def tile(j, h):
            r0 = pl.multiple_of(j * SUBLANES, SUBLANES)
            at = a_s[pl.ds(r0, SUBLANES), :]
            ut = u_s[pl.ds(r0, SUBLANES), :]
            out = []
            for rr in range(SUBLANES):
                h = at[rr:rr + 1] * h + ut[rr:rr + 1]
                out.append(h)
            hr_ref[pl.ds(r0, SUBLANES), :] = jnp.concatenate(out, axis=0)
            return h

        carry_s[0:1, :] = lax.fori_loop(0, r_rows // SUBLANES, tile, carry_s[0:1, :])
        g, _ = _gelu_and_grad(yr_ref[...])
        rec_ref[...] = hr_ref[...] * g

    blk = pl.BlockSpec((r_rows, w), lambda i: (i, 0))
    vm = 16 * _nbytes((r_rows, w), F32) + 4 * _nbytes((w, w), BF16)
    return _call(body, (proj, proj, conv_w, conv_b, wa, ba, wx, bx, lru), name="rec_fwd", grid=(nc,),
                 in_specs=[pl.BlockSpec((r_rows, w), lambda i: (i, xr_blk)),
                           pl.BlockSpec((r_rows, w), lambda i: (i, yr_blk)),
                           _full((CONV_WIDTH, w)), _full((1, w)), _full((w, w)), _full((1, w)),
                           _full((w, w)), _full((1, w)), _full((1, w))],
                 out_specs=[blk, blk],
                 out_shape=[jax.ShapeDtypeStruct((tp, w), F32)] * 2,
                 scratch_shapes=[pltpu.VMEM((HALO, w), F32), pltpu.VMEM((SUBLANES, w), F32),
                                 pltpu.VMEM((r_rows, w), F32), pltpu.VMEM((r_rows, w), F32)],
                 semantics=("arbitrary",), vmem_bytes=vm)


def _rec_bwd(proj, xr_blk, yr_blk, rec_w, hr, drec, conv_w, conv_b, wa, ba, wx, bx, lru):
    tp = proj.shape[0]
    w = rec_w
    r_rows = REC_ROWS
    nc = tp // r_rows
    hpc = r_rows // HALO

    def body(xr_ref, xh_ref, yr_ref, hr_ref, hh_ref, drec_ref, cw_ref, cb_ref, wa_ref, ba_ref, wx_ref, bx_ref,
             l_ref, dxr_ref, dyr_ref, dwa_ref, dwx_ref, small_ref, lam_s, a_s, dhr_s, carry_s, next_s):
        i = pl.program_id(0)
        first = (nc - 1 - i) == 0

        @pl.when(i == 0)
        def _():
            carry_s[...] = jnp.zeros_like(carry_s)
            next_s[...] = jnp.zeros_like(next_s)
            dwa_ref[...] = jnp.zeros_like(dwa_ref)
            dwx_ref[...] = jnp.zeros_like(dwx_ref)
            small_ref[...] = jnp.zeros_like(small_ref)

        x = xr_ref[...]
        xprev = jnp.where(first, 0.0, xh_ref[...])
        taps = _conv_taps(jnp.concatenate([xprev, x], axis=0))
        xc = cb_ref[...]
        for k in range(CONV_WIDTH):
            xc = xc + cw_ref[k:k + 1, :] * taps[k]
        xcb, r, ig, ls, log_a = _rec_gates(xc, wa_ref, ba_ref, wx_ref, bx_ref, l_ref)
        a = jnp.exp(log_a)
        a2 = jnp.exp(2.0 * log_a)
        mult = jnp.sqrt(-_expm1_nonpos(2.0 * log_a))
        g, dg = _gelu_and_grad(yr_ref[...])
        hr_v = hr_ref[...]
        drec_v = drec_ref[...]
        dhr_s[...] = drec_v * g
        dyr_ref[...] = (drec_v * hr_v * dg).astype(BF16)
        a_s[...] = a

        def tile(jj, carry):
            r0 = pl.multiple_of((r_rows // SUBLANES - 1 - jj) * SUBLANES, SUBLANES)
            at = a_s[pl.ds(r0, SUBLANES), :]
            dt = dhr_s[pl.ds(r0, SUBLANES), :]
            out = [None] * SUBLANES
            for rr in range(SUBLANES - 1, -1, -1):
                lam = dt[rr:rr + 1] + carry
                out[rr] = lam
                carry = at[rr:rr + 1] * lam
            lam_s[pl.ds(r0, SUBLANES), :] = jnp.concatenate(out, axis=0)
            return carry

        carry_s[0:1, :] = lax.fori_loop(0, r_rows // SUBLANES, tile, carry_s[0:1, :])
        lam = lam_s[...]
        hprev = jnp.where(first, 0.0, hh_ref[...])
        hr_prev = pltpu.roll(jnp.concatenate([hprev, hr_v], axis=0), 1, 0)[HALO:]
        da = lam * hr_prev
        dxc = lam * mult * ig
        di = lam * mult * xc
        dmult = lam * ig * xc
        dlog_a = da * a - dmult * a2 / mult
        dr = dlog_a * (RG_C * ls)
        dls = jnp.sum(dlog_a * (RG_C * r), axis=0, keepdims=True)
        dga = dr * r * (1.0 - r)
        dgx = di * ig * (1.0 - ig)
        dgab = dga.astype(BF16)
        dgxb = dgx.astype(BF16)
        dxc = dxc + _dot_nt(dgab, wa_ref[...]) + _dot_nt(dgxb, wx_ref[...])
        dwa_ref[...] += _dot_tn(xcb, dgab)
        dwx_ref[...] += _dot_tn(xcb, dgxb)
        cat = jnp.concatenate([dxc, next_s[...]], axis=0)
        next_s[...] = dxc[0:HALO]
        dxr = cw_ref[CONV_WIDTH - 1:CONV_WIDTH, :] * dxc
        for k in range(CONV_WIDTH - 1):
            sh = CONV_WIDTH - 1 - k
            dxr = dxr + cw_ref[k:k + 1, :] * pltpu.roll(cat, r_rows + HALO - sh, 0)[:r_rows]
        dxr_ref[...] = dxr.astype(BF16)
        rows = [jnp.sum(dxc * taps[k], axis=0, keepdims=True) for k in range(CONV_WIDTH)]
        rows += [jnp.sum(dxc, axis=0, keepdims=True), jnp.sum(dga, axis=0, keepdims=True),
                 jnp.sum(dgx, axis=0, keepdims=True), dls * _sigmoid(-l_ref[...])]
        small_ref[...] += jnp.concatenate(rows, axis=0)

    def rev(i):
        return nc - 1 - i

    def halo(i):
        return jnp.maximum(rev(i) * hpc - 1, 0)

    blk = pl.BlockSpec((r_rows, w), lambda i: (rev(i), 0))
    vm = 40 * _nbytes((r_rows, w), F32) + 6 * _nbytes((w, w), F32)
    return _call(body, (proj, proj, proj, hr, hr, drec, conv_w, conv_b, wa, ba, wx, bx, lru),
                 name="rec_bwd", grid=(nc,),
                 in_specs=[pl.BlockSpec((r_rows, w), lambda i: (rev(i), xr_blk)),
                           pl.BlockSpec((HALO, w), lambda i: (halo(i), xr_blk)),
                           pl.BlockSpec((r_rows, w), lambda i: (rev(i), yr_blk)),
                           blk,
                           pl.BlockSpec((HALO, w), lambda i: (halo(i), 0)),
                           blk,
                           _full((CONV_WIDTH, w)), _full((1, w)), _full((w, w)), _full((1, w)),
                           _full((w, w)), _full((1, w)), _full((1, w))],
                 out_specs=[blk, blk, _full((w, w)), _full((w, w)), _full((SUBLANES, w))],
                 out_shape=[jax.ShapeDtypeStruct((tp, w), BF16)] * 2
                 + [jax.ShapeDtypeStruct((w, w), F32)] * 2 + [jax.ShapeDtypeStruct((SUBLANES, w), F32)],
                 scratch_shapes=[pltpu.VMEM((r_rows, w), F32)] * 3
                 + [pltpu.VMEM((SUBLANES, w), F32), pltpu.VMEM((HALO, w), F32)],
                 semantics=("arbitrary",), vmem_bytes=vm)


ROW_TARGET = 544


def _mixer_out(attn, rec, g_a, g_r, w_out, h, g_next):
    tp, d = h.shape
    aw, rw = attn.shape[1], rec.shape[1]
    kc = d // N_CHIPS
    tm = _divisor_tile(tp, 16, ROW_TARGET)

    def body(a_ref, r_ref, ga_ref, gr_ref, w_ref, h_ref, gn_ref, h1_ref, z_ref, mix_ref):
        a = a_ref[...]
        r = r_ref[...]
        mix = jnp.concatenate([a * _rstd(a) * ga_ref[...], r * _rstd(r) * gr_ref[...]], axis=1).astype(BF16)
        mix_ref[...] = mix
        h1 = h_ref[...]
        for j in range(N_CHIPS):
            h1 = h1 + _dot(mix[:, j * kc:(j + 1) * kc], w_ref[j])
        h1_ref[...] = h1
        z_ref[...] = (h1 * _rstd(h1) * gn_ref[...]).astype(BF16)

    row = lambda wd: pl.BlockSpec((tm, wd), lambda i: (i, 0))
    vm = 2 * _nbytes((d, d), BF16) + 12 * _nbytes((tm, d), F32)
    return _call(body, (attn, rec, g_a, g_r, w_out, h, g_next), name="mixer_out", grid=(tp // tm,),
                 in_specs=[row(aw), row(rw), _full((1, aw)), _full((1, rw)), _full(w_out.shape), row(d),
                           _full((1, d))],
                 out_specs=[row(d), row(d), row(d)],
                 out_shape=[jax.ShapeDtypeStruct((tp, d), F32), jax.ShapeDtypeStruct((tp, d), BF16),
                            jax.ShapeDtypeStruct((tp, d), BF16)],
                 semantics=("parallel",), vmem_bytes=vm)


def _mixer_bwd(dh_b, w_out, attn, rec, g_a, g_r):
    tp, d = dh_b.shape
    aw, rw = attn.shape[1], rec.shape[1]
    tm = _divisor_tile(tp, 16, ROW_TARGET)

    def body(dh_ref, w_ref, a_ref, r_ref, ga_ref, gr_ref, da_ref, dr_ref, dg_ref):
        @pl.when(pl.program_id(0) == 0)
        def _():
            dg_ref[...] = jnp.zeros_like(dg_ref)

        dh = dh_ref[...]
        dmix = jnp.concatenate([_dot_nt(dh, w_ref[j]) for j in range(N_CHIPS)], axis=1)
        da, dga = _rms_bwd(dmix[:, :aw], a_ref[...], ga_ref[...])
        dr, dgr = _rms_bwd(dmix[:, aw:], r_ref[...], gr_ref[...])
        da_ref[...] = da
        dr_ref[...] = dr
        dg_ref[...] += jnp.broadcast_to(jnp.concatenate([dga, dgr], axis=1), (SUBLANES, d))

    row = lambda wd: pl.BlockSpec((tm, wd), lambda i: (i, 0))
    vm = 2 * _nbytes((d, d), BF16) + 12 * _nbytes((tm, d), F32)
    return _call(body, (dh_b, w_out, attn, rec, g_a, g_r), name="mixer_bwd", grid=(tp // tm,),
                 in_specs=[row(d), _full(w_out.shape), row(aw), row(rw), _full((1, aw)), _full((1, rw))],
                 out_specs=[row(aw), row(rw), _full((SUBLANES, d))],
                 out_shape=[jax.ShapeDtypeStruct((tp, aw), F32), jax.ShapeDtypeStruct((tp, rw), F32),
                            jax.ShapeDtypeStruct((SUBLANES, d), F32)],
                 semantics=("arbitrary",), vmem_bytes=vm)


def _mlp_up(z, w_up):
    tp, d = z.shape
    fc = w_up.shape[2]
    ff = N_CHIPS * fc
    tn = _divisor_tile(fc, LANES, 512)
    per = fc // tn

    def body(z_ref, w_ref, act_ref, up_ref):
        up = _dot(z_ref[...], w_ref[...])
        r = jnp.maximum(up, 0.0)
        act_ref[...] = (r * r).astype(BF16)
        up_ref[...] = up.astype(BF16)

    col = pl.BlockSpec((tp, tn), lambda j: (0, j))
    vm = 2 * _nbytes((tp, d), BF16) + 2 * _nbytes((d, tn), BF16) + 8 * _nbytes((tp, tn), F32)
    return _call(body, (z, w_up), name="mlp_up", grid=(ff // tn,),
                 in_specs=[_full((tp, d)), pl.BlockSpec((None, d, tn), lambda j: (j // per, 0, j % per))],
                 out_specs=[col, col],
                 out_shape=[jax.ShapeDtypeStruct((tp, ff), BF16)] * 2,
                 semantics=("parallel",), vmem_bytes=vm)


def _mlp_down(act, w_down, h, g_next):
    tp, d = h.shape
    ff = act.shape[1]
    fc = ff // N_CHIPS
    tm = _divisor_tile(tp, 16, ROW_TARGET)

    def body(a_ref, w_ref, h_ref, gn_ref, h2_ref, z_ref):
        h2 = h_ref[...]
        for j in range(N_CHIPS):
            h2 = h2 + _dot(a_ref[:, j * fc:(j + 1) * fc], w_ref[j])
        h2_ref[...] = h2
        z_ref[...] = (h2 * _rstd(h2) * gn_ref[...]).astype(BF16)

    row = lambda wd: pl.BlockSpec((tm, wd), lambda i: (i, 0))
    vm = 2 * _nbytes((ff, d), BF16) + 2 * _nbytes((tm, ff), BF16) + 10 * _nbytes((tm, d), F32)
    return _call(body, (act, w_down, h, g_next), name="mlp_down", grid=(tp // tm,),
                 in_specs=[row(ff), _full(w_down.shape), row(d), _full((1, d))],
                 out_specs=[row(d), row(d)],
                 out_shape=[jax.ShapeDtypeStruct((tp, d), F32), jax.ShapeDtypeStruct((tp, d), BF16)],
                 semantics=("parallel",), vmem_bytes=vm)


def _loss_bwd(h, g, target, n_real):
    tp, d = h.shape
    tm = _divisor_tile(tp, 16, ROW_TARGET)

    def body(h_ref, g_ref, t_ref, dh_ref, dhb_ref, dg_ref, loss_ref):
        i = pl.program_id(0)

        @pl.when(i == 0)
        def _():
            dg_ref[...] = jnp.zeros_like(dg_ref)
            loss_ref[...] = jnp.zeros_like(loss_ref)

        x = h_ref[...]
        gv = g_ref[...]
        rowi = i * tm + lax.broadcasted_iota(jnp.int32, (tm, 1), 0)
        real = jnp.logical_and(rowi >= N_META, rowi < N_META + n_real)
        err = jnp.where(real, x * _rstd(x) * gv - t_ref[...], 0.0)
        loss_ref[...] += 0.5 * jnp.sum(jnp.mean(err * err, axis=-1, keepdims=True))
        dx, dgp = _rms_bwd(err * (1.0 / d), x, gv)
        dh_ref[...] = dx
        dhb_ref[...] = dx.astype(BF16)
        dg_ref[...] += jnp.broadcast_to(dgp, (SUBLANES, d))

    row = pl.BlockSpec((tm, d), lambda i: (i, 0))
    return _call(body, (h, g, target), name="loss_bwd", grid=(tp // tm,),
                 in_specs=[row, _full((1, d)), row],
                 out_specs=[row, row, _full((SUBLANES, d)), _full((SUBLANES, LANES))],
                 out_shape=[jax.ShapeDtypeStruct((tp, d), F32), jax.ShapeDtypeStruct((tp, d), BF16),
                            jax.ShapeDtypeStruct((SUBLANES, d), F32), jax.ShapeDtypeStruct((SUBLANES, LANES), F32)],
                 semantics=("arbitrary",), vmem_bytes=16 * _nbytes((tm, d), F32))


def _mlp_dup(dh_b, w_down, up):
    tp, d = dh_b.shape
    fc = w_down.shape[1]
    ff = N_CHIPS * fc
    tn = _divisor_tile(fc, LANES, 512)
    per = fc // tn

    def body(dh_ref, w_ref, up_ref, dup_ref):
        dact = _dot_nt(dh_ref[...], w_ref[...])
        dup_ref[...] = (dact * (2.0 * jnp.maximum(up_ref[...].astype(F32), 0.0))).astype(BF16)

    col = pl.BlockSpec((tp, tn), lambda j: (0, j))
    vm = 2 * _nbytes((tp, d), BF16) + 2 * _nbytes((tn, d), BF16) + 8 * _nbytes((tp, tn), F32)
    return _call(body, (dh_b, w_down, up), name="mlp_dup", grid=(ff // tn,),
                 in_specs=[_full((tp, d)), pl.BlockSpec((None, tn, d), lambda j: (j // per, j % per, 0)),
                           col],
                 out_specs=col, out_shape=jax.ShapeDtypeStruct((tp, ff), BF16),
                 semantics=("parallel",), vmem_bytes=vm)


def _grad_w(a, b, cols_to_chips=False):
    tp, k = a.shape
    n = b.shape[1]
    tk = _divisor_tile(k, LANES, 1024)
    nc = n // N_CHIPS if cols_to_chips else n
    tn = _divisor_tile(nc, LANES, 512)
    per = nc // tn

    def body(a_ref, b_ref, o_ref):
        o_ref[...] = _dot_tn(a_ref[...], b_ref[...]).astype(BF16)

    if cols_to_chips:
        out_spec = pl.BlockSpec((None, tk, tn), lambda i, j: (j // per, i, j % per))
        out_shape = jax.ShapeDtypeStruct((N_CHIPS, k, nc), BF16)
    else:
        out_spec = pl.BlockSpec((tk, tn), lambda i, j: (i, j))
        out_shape = jax.ShapeDtypeStruct((k, n), BF16)
    vm = 2 * _nbytes((tp, tk), BF16) + 2 * _nbytes((tp, tn), BF16) + 6 * _nbytes((tk, tn), F32) \
        + 2 * _nbytes((tp, tk), F32)
    return _call(body, (a, b), name="grad_w", grid=(k // tk, n // tn),
                 in_specs=[pl.BlockSpec((tp, tk), lambda i, j: (0, i)), pl.BlockSpec((tp, tn), lambda i, j: (0, j))],
                 out_specs=out_spec, out_shape=out_shape,
                 semantics=("parallel", "parallel"), vmem_bytes=vm)


def _dx_norm_bwd(pieces, w, w_spec, w_piece, h, g, dres):
    tp, d = h.shape
    tm = _divisor_tile(tp, 16, ROW_TARGET)
    n = len(pieces)

    def body(*refs):
        dy_refs = refs[:n]
        w_ref, h_ref, g_ref, dres_ref, dh_ref, dhb_ref, dg_ref = refs[n:]

        @pl.when(pl.program_id(0) == 0)
        def _():
            dg_ref[...] = jnp.zeros_like(dg_ref)

        dz = _dot_nt(dy_refs[0][...], w_piece(w_ref, 0))
        for i in range(1, n):
            dz = dz + _dot_nt(dy_refs[i][...], w_piece(w_ref, i))
        dx, dgp = _rms_bwd(dz, h_ref[...], g_ref[...])
        dh = dres_ref[...] + dx
        dh_ref[...] = dh
        dhb_ref[...] = dh.astype(BF16)
        dg_ref[...] += jnp.broadcast_to(dgp, (SUBLANES, d))

    row = lambda wd: pl.BlockSpec((tm, wd), lambda i: (i, 0))
    kk = sum(wd for _, _, wd in pieces)
    vm = 2 * _nbytes((d, kk), BF16) + 2 * _nbytes((tm, kk), BF16) + 14 * _nbytes((tm, d), F32)
    piece_specs = [pl.BlockSpec((tm, wd), functools.partial(lambda i, cb: (i, cb), cb=cb)) for _, cb, wd in pieces]
    return _call(body, tuple(a for a, _, _ in pieces) + (w, h, g, dres), name="dx_norm_bwd", grid=(tp // tm,),
                 in_specs=piece_specs + [w_spec, row(d), _full((1, d)), row(d)],
                 out_specs=[row(d), row(d), _full((SUBLANES, d))],
                 out_shape=[jax.ShapeDtypeStruct((tp, d), F32), jax.ShapeDtypeStruct((tp, d), BF16),
                            jax.ShapeDtypeStruct((SUBLANES, d), F32)],
                 semantics=("arbitrary",), vmem_bytes=vm)


def _block_diag(wg):
    nb, b, _ = wg.shape
    eye = jnp.eye(nb, dtype=wg.dtype)
    return (eye[:, None, :, None] * wg[:, :, None, :]).reshape(nb * b, nb * b)


def _diag_blocks(dense, nb):
    b = dense.shape[0] // nb
    d4 = dense.reshape(nb, b, nb, b)
    return jnp.stack([d4[i, :, i, :] for i in range(nb)])


def _row(v):
    return v.reshape(1, -1)


def _forward_layer(l, h, z, p, fetch, g_next):
    d = h.shape[1]
    att_w = d // 2
    rec_w = d - att_w
    nh = att_w // HEAD_DIM
    xr_blk = 3 * att_w // rec_w
    wa_d = _block_diag(p["w_gate_a"][l]).astype(BF16)
    wx_d = _block_diag(p["w_gate_x"][l]).astype(BF16)
    b_f_pad = jnp.zeros((1, LANES), F32).at[0, :nh].set(p["b_f"][l])
    big = dict(w_in_big=_pack_w_in(jnp.moveaxis(fetch("w_in", h), 0, 1).reshape(d, -1), att_w, rec_w, nh))
    proj, qkv = _proj(z, big["w_in_big"], att_w)
    c, c_t = _fgate_fwd(proj, b_f_pad, nh)
    attn, lse_b = _attn_fwd(qkv, c, c_t, nh)
    hr, rec = _rec_fwd(proj, xr_blk, xr_blk + 1, rec_w, p["conv_w"][l], _row(p["conv_b"][l]), wa_d,
                       _row(p["b_gate_a"][l]), wx_d, _row(p["b_gate_x"][l]), _row(p["lru_L"][l]))
    big["w_out"] = fetch("w_out", rec)
    h1, z2, mix = _mixer_out(attn, rec, _row(p["attn_out_g"][l]), _row(p["rec_out_g"][l]),
                             big["w_out"], h, _row(p["mlp_norm_g"][l]))
    big["w_up"] = fetch("w_up", h1)
    act, up = _mlp_up(z2, big["w_up"])
    big["w_down"] = fetch("w_down", act)
    h2, z_next = _mlp_down(act, big["w_down"], h1, _row(g_next))
    saved = dict(h0=h, z1=z, proj=proj, qkv=qkv, c=c, c_t=c_t, attn=attn, lse_b=lse_b, hr=hr, rec=rec, h1=h1,
                 z2=z2, mix=mix, act=act, up=up, wa_d=wa_d, wx_d=wx_d, b_f_pad=b_f_pad, big=big)
    return h2, z_next, saved


def _backward_mlp(l, dh, dh_b, sv, p, tok):
    w_up, w_down = sv["big"]["w_up"], sv["big"]["w_down"]
    fc = w_up.shape[2]
    dup = _mlp_dup(dh_b, w_down, sv["up"])
    g_down = _grad_w(sv["act"], dh_b)
    g_up = _grad_w(sv["z2"], dup, cols_to_chips=True)
    dh, dh_b, dg2 = _dx_norm_bwd([(dup, j, fc) for j in range(N_CHIPS)], w_up, _full(w_up.shape),
                                 lambda w_ref, j: w_ref[j], sv["h1"], _row(p["mlp_norm_g"][l] + tok), dh)
    big = dict(w_down=g_down.reshape((N_CHIPS, -1) + g_down.shape[1:]), w_up=g_up)
    return dh, dh_b, big, dict(mlp_norm_g=dg2[0])


def _backward_mixer(l, dh, dh_b, sv, p, tok):
    d = dh.shape[1]
    att_w = d // 2
    rec_w = d - att_w
    nh = att_w // HEAD_DIM
    xr_blk = 3 * att_w // rec_w
    small = {}
    g_out = _grad_w(sv["mix"], dh_b)
    dattn, drec, dg_mix = _mixer_bwd(dh_b, sv["big"]["w_out"], sv["attn"], sv["rec"],
                                     _row(p["attn_out_g"][l] + tok), _row(p["rec_out_g"][l]))
    small["attn_out_g"] = dg_mix[0, :att_w]
    small["rec_out_g"] = dg_mix[0, att_w:]
    dxr, dyr, dwa, dwx, sm = _rec_bwd(
        sv["proj"], xr_blk, xr_blk + 1, rec_w, sv["hr"], drec, p["conv_w"][l], _row(p["conv_b"][l]), sv["wa_d"],
        _row(p["b_gate_a"][l]), sv["wx_d"], _row(p["b_gate_x"][l]), _row(p["lru_L"][l]))
    small.update(conv_w=sm[:CONV_WIDTH], conv_b=sm[4], b_gate_a=sm[5], b_gate_x=sm[6], lru_L=sm[7],
                 w_gate_a=_diag_blocks(dwa, N_REC_BLOCKS), w_gate_x=_diag_blocks(dwx, N_REC_BLOCKS))
    dq, dk, dv, dct = _attn_bwd(sv["qkv"], sv["c"], sv["c_t"], sv["lse_b"], dattn, nh)
    df, db_f = _fgate_bwd(sv["proj"], sv["b_f_pad"], dct)
    small["b_f"] = db_f[0, :nh]
    pieces = [dq, dk, dv, dxr, dyr, df]
    offs = [0, att_w, 2 * att_w, 3 * att_w, 3 * att_w + rec_w, 3 * att_w + 2 * rec_w]
    gq, gk, gv, gxr, gyr, gf = [_grad_w(sv["z1"], pc) for pc in pieces]
    g_in = jnp.concatenate([gq, gk, gv, gf[:, :nh], gxr, gyr], axis=1)
    w_big = sv["big"]["w_in_big"]
    widths = [pc.shape[1] for pc in pieces]
    dh, dh_b, dg1 = _dx_norm_bwd(
        [(pc, 0, wd) for pc, wd in zip(pieces, widths)], w_big, _full(w_big.shape),
        lambda w_ref, i: w_ref[:, offs[i]:offs[i] + widths[i]], sv["h0"], _row(p["attn_norm_g"][l]), dh)
    small["attn_norm_g"] = dg1[0]
    big = dict(w_in=jnp.moveaxis(g_in.reshape(d, N_CHIPS, -1), 1, 0),
               w_out=g_out.reshape((N_CHIPS, -1) + g_out.shape[1:]))
    return dh, dh_b, big, small


def _pack_w_in(w_in, att_w, rec_w, nh):
    qkv = w_in[..., :3 * att_w]
    f = w_in[..., 3 * att_w:3 * att_w + nh]
    xy = w_in[..., 3 * att_w + nh:]
    fpad = jnp.zeros(f.shape[:-1] + (LANES - nh,), w_in.dtype)
    return jnp.concatenate([qkv, xy, f, fpad], axis=-1)


ANY = pl.BlockSpec(memory_space=pl.ANY)


def _coords():
    return lax.axis_index("x"), lax.axis_index("y"), lax.axis_index("c")


def _other_chips(x, y):
    return [(1 - x, y), (x, 1 - y), (1 - x, 1 - y)]


def _remote(src, dst, send_sems, recv_sems, k, to):
    return pltpu.make_async_remote_copy(src_ref=src, dst_ref=dst, send_sem=send_sems.at[k],
                                        recv_sem=recv_sems.at[k], device_id=to, device_id_type=MESH)


def _all_gather_chips(shards):
    n = len(shards)
    per = 6

    def body(*refs):
        ins, outs = refs[:n], refs[n:2 * n]
        send_sems, recv_sems, local_sems = refs[2 * n:]
        x, y, c = _coords()
        me = 2 * x + y
        sibling = (x, y, 1 - c)
        chips = _other_chips(x, y)
        local = [pltpu.make_async_copy(ins[t], outs[t].at[me], local_sems.at[t]) for t in range(n)]
        for cp in local:
            cp.start()
        sends = []
        for t in range(n):
            for j, (px, py) in enumerate(chips):
                cp = _remote(ins[t].at[c], outs[t].at[me, c], send_sems, recv_sems, per * t + j, (px, py, c))
                cp.start()
                sends.append(cp)
        for t in range(n):
            for j, (px, py) in enumerate(chips):
                landed = outs[t].at[2 * px + py, c]
                _remote(landed, landed, send_sems, recv_sems, per * t + j, (px, py, c)).wait_recv()
                cp = _remote(landed, landed, send_sems, recv_sems, per * t + 3 + j, sibling)
                cp.start()
                sends.append(cp)
        for t in range(n):
            for j, (px, py) in enumerate(chips):
                passed = outs[t].at[2 * px + py, 1 - c]
                _remote(passed, passed, send_sems, recv_sems, per * t + 3 + j, sibling).wait_recv()
        for cp in sends:
            cp.wait_send()
        for cp in local:
            cp.wait()

    return _call(body, tuple(shards), name="all_gather_chips",
                 in_specs=[ANY] * n, out_specs=[ANY] * n,
                 out_shape=[jax.ShapeDtypeStruct((N_CHIPS,) + s.shape, s.dtype) for s in shards],
                 scratch_shapes=[pltpu.SemaphoreType.DMA((per * n,)), pltpu.SemaphoreType.DMA((per * n,)),
                                 pltpu.SemaphoreType.DMA((n,))])


HBM = pl.BlockSpec(memory_space=pltpu.HBM)
SEM = pl.BlockSpec(memory_space=pltpu.SEMAPHORE)
DATAFLOW = pltpu.SideEffectType.DATAFLOW_SIDE_EFFECTING


def _in_hbm(a):
    return pltpu.with_memory_space_constraint(a, pltpu.HBM)


PUSH_ARRIVALS = {"gather_chips": N_CHIPS - 1, "scatter_chips": N_CHIPS - 1, "sibling": 1, "gather_devices": N_DEV - 1}


def _push_copies(mode, src, land, send_sems, recv_sems, t):
    x, y, c = _coords()
    chip = 2 * x + y
    if mode == "gather_chips":
        return [_remote(src.at[chip], land.at[chip], send_sems, recv_sems, t, (px, py, c))
                for px, py in _other_chips(x, y)]
    if mode == "scatter_chips":
        return [_remote(src.at[2 * px + py], land.at[chip], send_sems, recv_sems, t, (px, py, c))
                for px, py in _other_chips(x, y)]
    if mode == "sibling":
        return [_remote(src, land, send_sems, recv_sems, t, (x, y, 1 - c))]
    dev = 4 * x + 2 * y + c
    return [_remote(src.at[dev], land.at[dev], send_sems, recv_sems, t, (x ^ (k >> 2), y ^ ((k >> 1) & 1), c ^ (k & 1)))
            for k in range(1, N_DEV)]


def _push_start(srcs, lands, mode, name):
    n = len(srcs)
    same = all(s is ld for s, ld in zip(srcs, lands))
    n_in = n if same else 2 * n

    def body(*refs):
        src_refs = refs[:n]
        land_refs = src_refs if same else refs[n:2 * n]
        send_sems, recv_sems = refs[n_in], refs[n_in + 1]
        token = refs[-1]
        for t in range(n):
            for cp in _push_copies(mode, src_refs[t], land_refs[t], send_sems, recv_sems, t):
                cp.start()
        token[...] = jnp.zeros_like(token)

    operands = tuple(srcs) if same else tuple(srcs) + tuple(lands)
    res = _call(
        body, [_in_hbm(a) for a in operands], name=name,
        out_shape=(pltpu.SemaphoreType.DMA((n,)), pltpu.SemaphoreType.DMA((n,)))
        + tuple(pltpu.HBM(a.shape, a.dtype) for a in operands) + (jax.ShapeDtypeStruct((SUBLANES, LANES), F32),),
        in_specs=[HBM] * n_in, out_specs=(SEM, SEM) + (HBM,) * n_in + (pl.BlockSpec(memory_space=pltpu.VMEM),),
        input_output_aliases={i: 2 + i for i in range(n_in)}, side_effects=DATAFLOW, hbm_results=False)
    send_sems, recv_sems, token = res[0], res[1], res[-1]
    srcs_thru = res[2:2 + n]
    lands_thru = srcs_thru if same else res[2 + n:2 + 2 * n]
    return send_sems, recv_sems, srcs_thru, lands_thru, token


def _push_wait(send_sems, recv_sems, ids, srcs, lands, mode, after, name):
    n = len(lands)
    same = all(s is ld for s, ld in zip(srcs, lands))
    n_in = n if same else 2 * n

    def body(*refs):
        land_refs = refs[:n] if same else refs[n:2 * n]
        send_sems, recv_sems = refs[n_in], refs[n_in + 1]
        x, y, c = _coords()
        for t in range(n):
            moved = land_refs[t] if mode == "sibling" else land_refs[t].at[pl.ds(0, PUSH_ARRIVALS[mode])]
            arrivals = _remote(moved, moved, send_sems, recv_sems, ids[t], (x, y, c))
            arrivals.wait_send()
            arrivals.wait_recv()

    operands = tuple(lands) if same else tuple(srcs) + tuple(lands)
    res = _call(
        body, operands + (send_sems, recv_sems, after), name=name,
        out_shape=tuple(pltpu.HBM(a.shape, a.dtype) for a in operands),
        in_specs=[HBM] * n_in + [SEM, SEM, ANY], out_specs=(HBM,) * n_in,
        input_output_aliases={i: i for i in range(n_in)}, side_effects=DATAFLOW)
    return list(res) if same else (list(res[:n]), list(res[n:]))


def _sum_partials(part, landed, chip):
    _, rows, cols = part.shape
    br = _divisor_tile(rows, 16, ELEM_ROWS)

    def body(chip_ref, own_ref, a_ref, b_ref, c_ref, o_ref):
        o_ref[...] = ((own_ref[...].astype(F32) + a_ref[...].astype(F32)) + b_ref[...].astype(F32)) \
            + c_ref[...].astype(F32)

    def other(k):
        return pl.BlockSpec((None, br, cols), lambda i, ch: (jnp.where(ch[0] <= k, k + 1, k), i, 0))

    spec = pltpu.PrefetchScalarGridSpec(
        num_scalar_prefetch=1, grid=(rows // br,),
        in_specs=[pl.BlockSpec((None, br, cols), lambda i, ch: (ch[0], i, 0)), other(0), other(1), other(2)],
        out_specs=pl.BlockSpec((br, cols), lambda i, ch: (i, 0)))
    return _call(body, (chip, part, landed, landed, landed), name="sum_partials", grid_spec=spec,
                 out_shape=jax.ShapeDtypeStruct((rows, cols), F32), semantics=("parallel",))


def _cast_to_slab(w, l, chip):
    _, rows, cols = w.shape
    br = _divisor_tile(rows, 16, ELEM_ROWS)

    def body(chip_ref, w_ref, o_ref):
        o_ref[...] = w_ref[...].astype(BF16)

    spec = pltpu.PrefetchScalarGridSpec(
        num_scalar_prefetch=1, grid=(rows // br,),
        in_specs=[pl.BlockSpec((None, br, cols), lambda i, ch: (l, i, 0))],
        out_specs=pl.BlockSpec((None, br, cols), lambda i, ch: (ch[0], i, 0)))
    return _call(body, (chip, w), name="cast_to_slab", grid_spec=spec,
                 out_shape=jax.ShapeDtypeStruct((N_CHIPS, rows, cols), BF16), semantics=("parallel",))


def _place_slab(buf, index, n_slabs):
    rows, cols = buf.shape
    br = _divisor_tile(rows, SUBLANES, ELEM_ROWS)

    def body(index_ref, b_ref, o_ref):
        o_ref[...] = b_ref[...]

    spec = pltpu.PrefetchScalarGridSpec(
        num_scalar_prefetch=1, grid=(rows // br,),
        in_specs=[pl.BlockSpec((br, cols), lambda i, ix: (i, 0))],
        out_specs=pl.BlockSpec((None, br, cols), lambda i, ix: (ix[0], i, 0)))
    return _call(body, (index, buf), name="place_slab", grid_spec=spec,
                 out_shape=jax.ShapeDtypeStruct((n_slabs, rows, cols), buf.dtype), semantics=("parallel",))


ELEM_ROWS = 256


def _sum_slabs(r):
    n, rows, cols = r.shape
    br = _divisor_tile(rows, 16, ELEM_ROWS)

    def body(r_ref, o_ref):
        acc = r_ref[0].astype(F32)
        for j in range(1, n):
            acc = acc + r_ref[j].astype(F32)
        o_ref[...] = acc

    return _call(body, (r,), name="sum_slabs", grid=(rows // br,),
                 in_specs=[pl.BlockSpec((n, br, cols), lambda i: (0, i, 0))],
                 out_specs=pl.BlockSpec((br, cols), lambda i: (i, 0)),
                 out_shape=jax.ShapeDtypeStruct((rows, cols), F32), semantics=("parallel",))


def _adamw_math(w, g, m, v):
    c1 = 1.0 - ADAM_B1 ** ADAM_STEP
    c2 = 1.0 - ADAM_B2 ** ADAM_STEP
    nm = ADAM_B1 * m + (1.0 - ADAM_B1) * g
    nv = ADAM_B2 * v + (1.0 - ADAM_B2) * (g * g)
    delta = -ADAM_LR * ((nm / c1) / (jnp.sqrt(nv / c2) + ADAM_EPS) + ADAM_WD * w)
    return delta, nm, nv


def _adamw(w, g, m, v):
    rows, cols = w.shape
    br = _divisor_tile(rows, 8, ELEM_ROWS)

    def body(w_ref, g_ref, m_ref, v_ref, d_ref, nm_ref, nv_ref):
        d_ref[...], nm_ref[...], nv_ref[...] = _adamw_math(w_ref[...], g_ref[...], m_ref[...], v_ref[...])

    blk = pl.BlockSpec((br, cols), lambda i: (i, 0))
    return _call(body, (w, g, m, v), name="adamw", grid=(rows // br,),
                 in_specs=[blk] * 4, out_specs=[blk] * 3,
                 out_shape=[jax.ShapeDtypeStruct((rows, cols), F32)] * 3, semantics=("parallel",))


def _adamw_layer(w, m, v, l, g_mine, g_theirs, prev):
    _, rows, cols = w.shape
    br = _divisor_tile(rows, 8, ELEM_ROWS)

    def body(w_ref, m_ref, v_ref, ga_ref, gb_ref, *rest):
        g_ref, d_ref, nm_ref, nv_ref = rest[4:]
        g = ga_ref[...] + gb_ref[...]
        g_ref[...] = g
        d_ref[...], nm_ref[...], nv_ref[...] = _adamw_math(w_ref[...], g, m_ref[...], v_ref[...])

    slot = pl.BlockSpec((None, br, cols), lambda i: (l, i, 0))
    blk = pl.BlockSpec((br, cols), lambda i: (i, 0))
    return _call(body, (w, m, v, g_mine, g_theirs) + tuple(prev), name="adamw_layer", grid=(rows // br,),
                 in_specs=[slot] * 3 + [blk] * 2 + [ANY] * 4, out_specs=[slot] * 4,
                 out_shape=[jax.ShapeDtypeStruct(w.shape, F32)] * 4,
                 input_output_aliases={5: 0, 6: 1, 7: 2, 8: 3}, semantics=("parallel",))


BIG = ("w_in", "w_out", "w_up", "w_down")
WEIGHTS = ("meta", "attn_norm_g", "w_in", "b_f", "conv_w", "conv_b", "w_gate_a", "b_gate_a", "w_gate_x",
           "b_gate_x", "lru_L", "attn_out_g", "rec_out_g", "w_out", "mlp_norm_g", "w_up", "w_down", "final_g")
SMALL = tuple(k for k in WEIGHTS if k not in BIG)
COL_SHARDED_SMALL = ("meta", "conv_w")
PACK_UNIT = ELEM_ROWS * LANES


def _pack(arrs):
    flat = jnp.concatenate([a.reshape(-1) for a in arrs])
    total = -(-flat.shape[0] // PACK_UNIT) * PACK_UNIT
    return jnp.pad(flat, (0, total - flat.shape[0])).reshape(-1, LANES)


def _unpack(buf, shapes):
    flat = buf.reshape(-1)
    out, off = [], 0
    for s in shapes:
        size = math.prod(s)
        out.append(flat[off:off + size].reshape(s))
        off += size
    return out


def _halves(a):
    return a.reshape((2, a.shape[0] // 2) + a.shape[1:])


def _cols_from_chips(g):
    return jnp.moveaxis(g, 0, -2).reshape(g.shape[1:-1] + (N_CHIPS * g.shape[-1],))


def kernel(x, meta, attn_norm_g, w_in, b_f, conv_w, conv_b, w_gate_a, b_gate_a, w_gate_x, b_gate_x, lru_L, attn_out_g, rec_out_g, w_out, mlp_norm_g, w_up, w_down, final_g, loss_target, m_meta, m_attn_norm_g, m_w_in, m_b_f, m_conv_w, m_conv_b, m_w_gate_a, m_b_gate_a, m_w_gate_x, m_b_gate_x, m_lru_L, m_attn_out_g, m_rec_out_g, m_w_out, m_mlp_norm_g, m_w_up, m_w_down, m_final_g, v_meta, v_attn_norm_g, v_w_in, v_b_f, v_conv_w, v_conv_b, v_w_gate_a, v_b_gate_a, v_w_gate_x, v_b_gate_x, v_lru_L, v_attn_out_g, v_rec_out_g, v_w_out, v_mlp_norm_g, v_w_up, v_w_down, v_final_g):
    w = dict(meta=meta, attn_norm_g=attn_norm_g, w_in=w_in, b_f=b_f, conv_w=conv_w, conv_b=conv_b,
             w_gate_a=w_gate_a, b_gate_a=b_gate_a, w_gate_x=w_gate_x, b_gate_x=b_gate_x, lru_L=lru_L,
             attn_out_g=attn_out_g, rec_out_g=rec_out_g, w_out=w_out, mlp_norm_g=mlp_norm_g, w_up=w_up,
             w_down=w_down, final_g=final_g)
    m = dict(meta=m_meta, attn_norm_g=m_attn_norm_g, w_in=m_w_in, b_f=m_b_f, conv_w=m_conv_w, conv_b=m_conv_b,
             w_gate_a=m_w_gate_a, b_gate_a=m_b_gate_a, w_gate_x=m_w_gate_x, b_gate_x=m_b_gate_x, lru_L=m_lru_L,
             attn_out_g=m_attn_out_g, rec_out_g=m_rec_out_g, w_out=m_w_out, mlp_norm_g=m_mlp_norm_g,
             w_up=m_w_up, w_down=m_w_down, final_g=m_final_g)
    v = dict(meta=v_meta, attn_norm_g=v_attn_norm_g, w_in=v_w_in, b_f=v_b_f, conv_w=v_conv_w, conv_b=v_conv_b,
             w_gate_a=v_w_gate_a, b_gate_a=v_b_gate_a, w_gate_x=v_w_gate_x, b_gate_x=v_b_gate_x, lru_L=v_lru_L,
             attn_out_g=v_attn_out_g, rec_out_g=v_rec_out_g, w_out=v_w_out, mlp_norm_g=v_mlp_norm_g,
             w_up=v_w_up, w_down=v_w_down, final_g=v_final_g)
    s_len, d = x.shape[1], x.shape[2]
    depth = w_in.shape[0]
    att_w = d // 2
    rec_w = d - att_w
    nh = att_w // HEAD_DIM
    chip = 2 * lax.axis_index("x") + lax.axis_index("y")

    g_conv, g_meta = [g.reshape((N_CHIPS, g.shape[1] * g.shape[2]) + g.shape[3:])
                      for g in _all_gather_chips([_halves(w["conv_w"]), _halves(w["meta"])])]
    p = dict(w)
    p["conv_w"] = _cols_from_chips(g_conv)
    meta_full = jnp.moveaxis(g_meta, 0, 1).reshape(N_META, d)

    chip1 = chip.reshape(1).astype(jnp.int32)
    pushes, tokens = [], []
    for l in range(depth):
        slabs = [_cast_to_slab(w[k], l, chip1) for k in BIG]
        send_sems, recv_sems, _, lands, token = _push_start(slabs, slabs, "gather_chips", f"weights_start_{l}")
        pushes.append((send_sems, recv_sems, lands))
        tokens.append(token[0, 0])

    t_len = N_META + s_len
    pad = -t_len % SEQ_TILE
    h = jnp.concatenate([meta_full, x[0], jnp.zeros((pad, d), F32)], axis=0)
    tgt = jnp.concatenate([jnp.zeros((N_META, d), F32), loss_target[0], jnp.zeros((pad, d), F32)], axis=0)
    z = _rms_fwd(h, _row(p["attn_norm_g"][0] + sum(tokens)))
    saved = []
    for l in range(depth):
        send_sems, recv_sems, lands = pushes[l]

        def fetch(k, after, l=l, send_sems=send_sems, recv_sems=recv_sems, lands=lands):
            i = BIG.index(k)
            return _push_wait(send_sems, recv_sems, [i], [lands[i]], [lands[i]], "gather_chips", after,
                              f"{k}_wait_{l}")[0]

        g_next = p["attn_norm_g"][l + 1] if l + 1 < depth else p["final_g"]
        h, z, sv = _forward_layer(l, h, z, p, fetch, g_next)
        saved.append(sv)
    dh, dh_b, dg_final, loss_part = _loss_bwd(h, _row(p["final_g"]), tgt, s_len)

    small = {k: [None] * depth for k in SMALL if k not in ("meta", "final_g")}
    pushes = [None] * depth
    tok = 0.0
    for l in reversed(range(depth)):
        dh, dh_b, big_mlp, sm_mlp = _backward_mlp(l, dh, dh_b, saved[l], p, tok)
        parts = [big_mlp["w_down"], big_mlp["w_up"]]
        push_mlp = _push_start(parts, [lax.empty(a.shape, a.dtype) for a in parts], "scatter_chips",
                               f"mlp_grads_start_{l}")
        dh, dh_b, big_mix, sm_mix = _backward_mixer(l, dh, dh_b, saved[l], p, push_mlp[4][0, 0])
        parts = [big_mix["w_out"], big_mix["w_in"]]
        push_mix = _push_start(parts, [lax.empty(a.shape, a.dtype) for a in parts], "scatter_chips",
                               f"mixer_grads_start_{l}")
        tok = push_mix[4][0, 0]
        pushes[l] = {("w_down", "w_up"): push_mlp, ("w_out", "w_in"): push_mix}
        for k, val in {**sm_mlp, **sm_mix}.items():
            small[k][l] = val
    grads = {k: jnp.stack(val) for k, val in small.items()}
    grads["final_g"] = dg_final[0]
    grads["meta"] = dh[:N_META]
    dx = dh[N_META:t_len]

    full_shapes = [grads[k].shape for k in SMALL] + [(1,)]
    packed = _pack([grads[k].astype(F32) for k in SMALL] + [loss_part[0, :1] + tok])
    dev1 = (2 * chip + lax.axis_index("c")).reshape(1).astype(jnp.int32)
    slabs = [_place_slab(packed, dev1, N_DEV)]
    small_push = _push_start(slabs, slabs, "gather_devices", "small_grads_start")

    last_token = small_push[4]
    outs = {k: [lax.empty(w[k].shape, F32) for _ in range(4)] for k in BIG}
    swaps = {}

    def finish(l, after):
        send_sems, recv_sems, mine, lands, _ = swaps[l]
        mine, theirs = _push_wait(send_sems, recv_sems, list(range(len(BIG))), mine, lands, "sibling", after,
                                  f"sums_wait_{l}")
        for k, a, b in zip(BIG, mine, theirs):
            outs[k] = _adamw_layer(w[k], m[k], v[k], l, a, b, outs[k])

    for l in reversed(range(depth)):
        sums = {}
        for names, (send_sems, recv_sems, parts, lands, _) in pushes[l].items():
            parts, landed = _push_wait(send_sems, recv_sems, [0, 1], parts, lands, "scatter_chips", last_token,
                                       f"{names[0]}_grads_wait_{l}")
            for k, part, land in zip(names, parts, landed):
                sums[k] = _sum_partials(part, land, chip1)
        mine = [sums[k] for k in BIG]
        swaps[l] = _push_start(mine, [lax.empty(a.shape, a.dtype) for a in mine], "sibling", f"sums_start_{l}")
        if l + 1 < depth:
            finish(l + 1, mine[0])
    finish(0, swaps[0][4])
    out_g, out_d, out_m, out_v = [{k: outs[k][i] for k in BIG} for i in range(4)]

    landed = _push_wait(small_push[0], small_push[1], [0], small_push[3], small_push[3], "gather_devices",
                        out_g["w_in"], "small_grads_wait")
    total = _sum_slabs(landed[0])
    small_g = dict(zip(SMALL + ("loss",), _unpack(total, full_shapes)))
    for k in COL_SHARDED_SMALL:
        n = w[k].shape[-1]
        small_g[k] = lax.dynamic_slice_in_dim(small_g[k], chip * n, n, axis=small_g[k].ndim - 1)
    local_shapes = [w[k].shape for k in SMALL]
    res = _adamw(_pack([w[k] for k in SMALL]), _pack([small_g[k] for k in SMALL]),
                 _pack([m[k] for k in SMALL]), _pack([v[k] for k in SMALL]))
    out_g.update({k: small_g[k] for k in SMALL})
    for dst, buf in zip((out_d, out_m, out_v), res):
        dst.update(zip(SMALL, _unpack(buf, local_shapes)))

    return (small_g["loss"].reshape(()), dx[None],
            *[out_g[k] for k in WEIGHTS], *[out_d[k] for k in WEIGHTS],
            *[out_m[k] for k in WEIGHTS], *[out_v[k] for k in WEIGHTS])
```

```python
import functools
import math

import jax
import jax.numpy as jnp
from jax import lax
from jax.experimental import pallas as pl
from jax.experimental.pallas import tpu as pltpu

F32 = jnp.float32
BF16 = jnp.bfloat16

N_META = 16
HEAD_DIM = 64
N_REC_BLOCKS = 8
CONV_WIDTH = 4
RG_C = 8.0
NORM_EPS = 1e-6
ADAM_LR = 0.001
ADAM_B1 = 0.9
ADAM_B2 = 0.999
ADAM_EPS = 1e-08
ADAM_WD = 0.01
ADAM_STEP = 10

LANES = 128
SUBLANES = 8
SEQ_TILE = 128
VMEM_CAP = 60 * 2**20
VMEM_SLACK = 6 * 2**20
NEG_BIG = -1e30
N_CHIPS = 4
N_DEV = 8
MESH = pl.DeviceIdType.MESH


def _nbytes(shape, dtype):
    return math.prod(shape) * jnp.dtype(dtype).itemsize


def _call(body, args, *, name, out_shape, grid=(), in_specs=None, out_specs=None, scratch_shapes=(),
          grid_spec=None, semantics=None, vmem_bytes=None, side_effects=None, hbm_results=True, **kw):
    cp = {}
    if semantics is not None:
        cp["dimension_semantics"] = semantics
    if vmem_bytes is not None:
        cp["vmem_limit_bytes"] = int(min(VMEM_CAP, vmem_bytes + VMEM_SLACK))
    if side_effects is not None:
        cp["has_side_effects"] = side_effects
    if grid_spec is not None:
        kw["grid_spec"] = grid_spec
    else:
        kw.update(grid=grid, in_specs=in_specs, out_specs=out_specs, scratch_shapes=scratch_shapes)
    if hbm_results:
        out_shape = jax.tree.map(
            lambda s: pltpu.HBM(s.shape, s.dtype) if isinstance(s, jax.ShapeDtypeStruct) else s, out_shape)
    fn = pl.pallas_call(
        body, name=name, out_shape=out_shape,
        compiler_params=pltpu.CompilerParams(**cp), **kw)
    return fn(*[_in_hbm(a) if jnp.issubdtype(getattr(a, "dtype", jnp.int32), jnp.floating) else a for a in args])


def _divisor_tile(n, unit, target):
    best = None
    for t in range(unit, min(n, target) + 1, unit):
        if n % t == 0:
            best = t
    return n if best is None else best


def _sigmoid(x):
    return 1.0 / (1.0 + jnp.exp(-x))


def _log1p_unit(e):
    series = e * (1.0 - e * (0.5 - e * (1.0 / 3.0)))
    return jnp.where(e < 1e-2, series, jnp.log(1.0 + e))


def _log_sigmoid(x):
    return jnp.minimum(x, 0.0) - _log1p_unit(jnp.exp(-jnp.abs(x)))


def _expm1_nonpos(x):
    small = x * (1.0 + x * (1.0 / 2 + x * (1.0 / 6 + x * (1.0 / 24 + x * (1.0 / 120 + x * (1.0 / 720))))))
    return jnp.where(x > -0.25, small, jnp.exp(x) - 1.0)


_GELU_K = math.sqrt(2.0 / math.pi)
_GELU_C = 0.044715


def _gelu_and_grad(y):
    th = jnp.tanh(_GELU_K * (y + _GELU_C * y * y * y))
    g = 0.5 * y * (1.0 + th)
    dg = 0.5 * (1.0 + th) + 0.5 * y * (1.0 - th * th) * _GELU_K * (1.0 + 3.0 * _GELU_C * y * y)
    return g, dg


def _rstd(x):
    return lax.rsqrt(jnp.mean(x * x, axis=-1, keepdims=True) + NORM_EPS)


def _rms_bwd(dz, x, g):
    rs = _rstd(x)
    xh = x * rs
    dgp = jnp.sum(dz * xh, axis=0, keepdims=True)
    dxh = dz * g
    dx = rs * (dxh - xh * jnp.mean(dxh * xh, axis=-1, keepdims=True))
    return dx, dgp


def _dot(a, b):
    return jnp.dot(a, b, preferred_element_type=F32)


def _dot_nt(a, b):
    return lax.dot_general(a, b, (((1,), (1,)), ((), ())), preferred_element_type=F32)


def _dot_tn(a, b):
    return lax.dot_general(a, b, (((0,), (0,)), ((), ())), preferred_element_type=F32)


def _full(shape):
    nd = len(shape)
    return pl.BlockSpec(shape, lambda *_: (0,) * nd)


def _rms_fwd(h, g):
    tp, d = h.shape
    tm = _divisor_tile(tp, 16, 544)

    def body(h_ref, g_ref, z_ref):
        x = h_ref[...]
        z_ref[...] = (x * _rstd(x) * g_ref[...]).astype(BF16)

    return _call(body, (h, g), name="rms_fwd", grid=(tp // tm,),
                 in_specs=[pl.BlockSpec((tm, d), lambda i: (i, 0)), _full((1, d))],
                 out_specs=pl.BlockSpec((tm, d), lambda i: (i, 0)),
                 out_shape=jax.ShapeDtypeStruct((tp, d), BF16), semantics=("parallel",))


def _proj(z, w_big_t, att_w):
    tp, d = z.shape
    nb = w_big_t.shape[0]
    tn = _divisor_tile(nb, LANES, 512)
    assert (3 * att_w) % tn == 0
    n_qkv = 3 * att_w // tn
    scale = 1.0 / math.sqrt(HEAD_DIM)

    def body(z_ref, w_ref, p_ref, qkv_ref):
        j = pl.program_id(0)
        acc = _dot_nt(z_ref[...], w_ref[...])
        p_ref[...] = acc

        @pl.when(j < n_qkv)
        def _():
            col = j * tn + lax.broadcasted_iota(jnp.int32, (1, tn), 1)
            qkv_ref[...] = (acc * jnp.where(col < att_w, scale, 1.0)).astype(BF16)

    vm = 2 * (_nbytes((tp, d), BF16) + _nbytes((d, tn), BF16) + _nbytes((tp, tn), F32) * 2)
    return _call(body, (z, w_big_t), name="proj", grid=(nb // tn,),
                 in_specs=[_full((tp, d)), pl.BlockSpec((tn, d), lambda j: (j, 0))],
                 out_specs=[pl.BlockSpec((tp, tn), lambda j: (0, j)),
                            pl.BlockSpec((tp, tn), lambda j: (0, jnp.minimum(j, n_qkv - 1)))],
                 out_shape=[jax.ShapeDtypeStruct((tp, nb), F32),
                            jax.ShapeDtypeStruct((tp, 3 * att_w), BF16)],
                 semantics=("arbitrary",), vmem_bytes=vm)


def _tile_cumsum(x, row, reverse=False):
    for s in (1, 2, 4):
        if reverse:
            x = x + jnp.where(row < SUBLANES - s, pltpu.roll(x, SUBLANES - s, 0), 0.0)
        else:
            x = x + jnp.where(row >= s, pltpu.roll(x, s, 0), 0.0)
    return x


def _fgate_fwd(proj, b_f_pad, nh):
    tp, nb = proj.shape
    fblk = nb // LANES - 1

    def body(f_ref, b_ref, c_ref, ct_ref):
        b = b_ref[...]
        row = lax.broadcasted_iota(jnp.int32, (SUBLANES, LANES), 0)

        def step(i, carry):
            r0 = pl.multiple_of(i * SUBLANES, SUBLANES)
            lf = _log_sigmoid(f_ref[pl.ds(r0, SUBLANES), :] + b)
            x = _tile_cumsum(lf, row) + carry
            c_ref[pl.ds(r0, SUBLANES), :] = x
            return x[SUBLANES - 1:SUBLANES, :]

        lax.fori_loop(0, tp // SUBLANES, step, jnp.zeros((1, LANES), F32))
        ct_ref[...] = c_ref[...].T[:nh, :]

    return _call(body, (proj, b_f_pad), name="fgate_fwd", grid=(1,),
                 in_specs=[pl.BlockSpec((tp, LANES), lambda i: (0, fblk)), _full((1, LANES))],
                 out_specs=[_full((tp, LANES)), _full((nh, tp))],
                 out_shape=[jax.ShapeDtypeStruct((tp, LANES), F32), jax.ShapeDtypeStruct((nh, tp), F32)],
                 semantics=("arbitrary",))


def _fgate_bwd(proj, b_f_pad, dct):
    tp, nb = proj.shape
    nh = dct.shape[0]
    fblk = nb // LANES - 1

    def body(f_ref, b_ref, dct_ref, df_ref, db_ref, dc_s):
        b = b_ref[...]
        row = lax.broadcasted_iota(jnp.int32, (SUBLANES, LANES), 0)
        nt = tp // SUBLANES
        dc_s[...] = jnp.concatenate([dct_ref[...], jnp.zeros((LANES - nh, tp), F32)], axis=0).T

        def step(i, carry):
            suffix, acc = carry
            r0 = pl.multiple_of((nt - 1 - i) * SUBLANES, SUBLANES)
            dlf = _tile_cumsum(dc_s[pl.ds(r0, SUBLANES), :], row, reverse=True) + suffix
            df = dlf * _sigmoid(-(f_ref[pl.ds(r0, SUBLANES), :] + b))
            dc_s[pl.ds(r0, SUBLANES), :] = df
            return dlf[0:1, :], acc + df

        _, acc = lax.fori_loop(0, nt, step, (jnp.zeros((1, LANES), F32), jnp.zeros((SUBLANES, LANES), F32)))
        df_ref[...] = dc_s[...].astype(BF16)
        db_ref[...] = jnp.broadcast_to(jnp.sum(acc, axis=0, keepdims=True), (SUBLANES, LANES))

    return _call(body, (proj, b_f_pad, dct), name="fgate_bwd", grid=(1,),
                 in_specs=[pl.BlockSpec((tp, LANES), lambda i: (0, fblk)), _full((1, LANES)), _full((nh, tp))],
                 out_specs=[_full((tp, LANES)), _full((SUBLANES, LANES))],
                 out_shape=[jax.ShapeDtypeStruct((tp, LANES), BF16),
                            jax.ShapeDtypeStruct((SUBLANES, LANES), F32)],
                 scratch_shapes=[pltpu.VMEM((tp, LANES), F32)], semantics=("arbitrary",))


ATT_BQ = 128


ATT_BUCKET = 3
ATT_HEADS = 4


def _for_bucket(i, nq, fn):
    for lo in range(0, nq, ATT_BUCKET):
        hi = min(lo + ATT_BUCKET, nq)
        spans = ([(0, lo * ATT_BQ, False)] if lo else []) + [(lo * ATT_BQ, hi * ATT_BQ, True)]
        pl.when(jnp.logical_and(i >= lo, i < hi))(functools.partial(fn, spans))


def _head_column(c_blk, h):
    lane = lax.broadcasted_iota(jnp.int32, c_blk.shape, 1)
    return jnp.sum(jnp.where(lane == h, c_blk, 0.0), axis=1, keepdims=True)


def _span_logits(q, k_ref, ct_ref, h, lo, i, span):
    k0, k1, needs_mask = span
    t = _dot_nt(q, k_ref[k0:k1, lo:lo + HEAD_DIM]) - ct_ref[pl.ds(h, 1), k0:k1]
    if needs_mask:
        rows = i * ATT_BQ + lax.broadcasted_iota(jnp.int32, (ATT_BQ, k1 - k0), 0)
        cols = k0 + lax.broadcasted_iota(jnp.int32, (ATT_BQ, k1 - k0), 1)
        t = jnp.where(cols <= rows, t, NEG_BIG)
    return t


def _attn_fwd(qkv, c, c_t, nh):
    tp = qkv.shape[0]
    att_w = nh * HEAD_DIM
    ng = nh // ATT_HEADS
    gw = ATT_HEADS * HEAD_DIM
    bq = ATT_BQ
    nq = tp // bq

    def body(q_ref, k_ref, v_ref, c_ref, ct_ref, o_ref, lse_ref):
        p = pl.program_id(0)
        i = pl.program_id(1)

        def compute(spans):
            outs, lses = [], []
            for hh in range(ATT_HEADS):
                lo = HEAD_DIM * hh
                h = ATT_HEADS * p + hh
                q = q_ref[:, lo:lo + HEAD_DIM]
                ts = [_span_logits(q, k_ref, ct_ref, h, lo, i, sp) for sp in spans]
                m = functools.reduce(jnp.maximum, [jnp.max(t, axis=1, keepdims=True) for t in ts])
                es = [jnp.exp(t - m) for t in ts]
                l = sum(jnp.sum(e, axis=1, keepdims=True) for e in es)
                o = sum(_dot(e.astype(BF16), v_ref[k0:k1, lo:lo + HEAD_DIM]) for e, (k0, k1, _) in zip(es, spans))
                outs.append(o / l)
                lses.append(jnp.broadcast_to(m + _head_column(c_ref[...], h) + jnp.log(l), (bq, HEAD_DIM)))
            o_ref[...] = jnp.concatenate(outs, axis=1)
            lse_ref[...] = jnp.concatenate(lses, axis=1)

        _for_bucket(i, nq, compute)

    blk = pl.BlockSpec((bq, gw), lambda p, i: (i, p))
    vm = 4 * _nbytes((tp, gw), BF16) + 8 * ATT_HEADS * _nbytes((bq, tp), F32)
    return _call(body, (qkv, qkv, qkv, c, c_t), name="attn_fwd", grid=(ng, nq),
                 in_specs=[blk,
                           pl.BlockSpec((tp, gw), lambda p, i: (0, ng + p)),
                           pl.BlockSpec((tp, gw), lambda p, i: (0, 2 * ng + p)),
                           pl.BlockSpec((bq, LANES), lambda p, i: (i, 0)), _full((nh, tp))],
                 out_specs=[blk, blk],
                 out_shape=[jax.ShapeDtypeStruct((tp, att_w), F32)] * 2,
                 semantics=("parallel", "parallel"), vmem_bytes=vm)


def _attn_bwd(qkv, c, c_t, lse_b, do, nh):
    tp = qkv.shape[0]
    att_w = nh * HEAD_DIM
    ng = nh // ATT_HEADS
    gw = ATT_HEADS * HEAD_DIM
    bq = ATT_BQ
    nq = tp // bq
    scale = 1.0 / math.sqrt(HEAD_DIM)

    def body(q_ref, k_ref, v_ref, c_ref, ct_ref, lse_ref, do_ref, dq_ref, dk_ref, dv_ref, dct_ref, dk_s, dv_s):
        p = pl.program_id(0)
        i = pl.program_id(1)

        @pl.when(i == 0)
        def _():
            dk_s[...] = jnp.zeros_like(dk_s)
            dv_s[...] = jnp.zeros_like(dv_s)

        @pl.when(jnp.logical_and(i == 0, p == 0))
        def _():
            dct_ref[...] = jnp.zeros_like(dct_ref)

        def compute(spans):
            dqs = []
            dks = [[] for _ in spans]
            dvs = [[] for _ in spans]
            for hh in range(ATT_HEADS):
                lo = HEAD_DIM * hh
                h = ATT_HEADS * p + hh
                q = q_ref[:, lo:lo + HEAD_DIM]
                row_term = lse_ref[:, lo:lo + 1] - _head_column(c_ref[...], h)
                doutb = do_ref[:, lo:lo + HEAD_DIM].astype(BF16)
                prs = [jnp.exp(_span_logits(q, k_ref, ct_ref, h, lo, i, sp) - row_term) for sp in spans]
                dps = [_dot_nt(doutb, v_ref[k0:k1, lo:lo + HEAD_DIM]) for k0, k1, _ in spans]
                row_sum = sum(jnp.sum(pr * dp, axis=1, keepdims=True) for pr, dp in zip(prs, dps))
                dq = 0.0
                for n, ((k0, k1, _), pr, dp) in enumerate(zip(spans, prs, dps)):
                    ds = pr * (dp - row_sum)
                    dsb = ds.astype(BF16)
                    dq = dq + _dot(dsb, k_ref[k0:k1, lo:lo + HEAD_DIM])
                    dks[n].append(_dot_tn(dsb, q))
                    dvs[n].append(_dot_tn(pr.astype(BF16), doutb))
                    dct_ref[pl.ds(h, 1), k0:k1] = dct_ref[pl.ds(h, 1), k0:k1] - jnp.sum(ds, axis=0, keepdims=True)
                dqs.append(dq * scale)
            dq_ref[...] = jnp.concatenate(dqs, axis=1).astype(BF16)
            for n, (k0, k1, _) in enumerate(spans):
                dk_s[k0:k1, :] += jnp.concatenate(dks[n], axis=1)
                dv_s[k0:k1, :] += jnp.concatenate(dvs[n], axis=1)

        _for_bucket(i, nq, compute)

        @pl.when(i == nq - 1)
        def _():
            dk_ref[...] = dk_s[...].astype(BF16)
            dv_ref[...] = dv_s[...].astype(BF16)

    blk = pl.BlockSpec((bq, gw), lambda p, i: (i, p))
    col = pl.BlockSpec((tp, gw), lambda p, i: (0, p))
    vm = 6 * _nbytes((tp, gw), BF16) + 2 * _nbytes((tp, gw), F32) + 12 * ATT_HEADS * _nbytes((bq, tp), F32)
    return _call(body, (qkv, qkv, qkv, c, c_t, lse_b, do), name="attn_bwd", grid=(ng, nq),
                 in_specs=[blk,
                           pl.BlockSpec((tp, gw), lambda p, i: (0, ng + p)),
                           pl.BlockSpec((tp, gw), lambda p, i: (0, 2 * ng + p)),
                           pl.BlockSpec((bq, LANES), lambda p, i: (i, 0)), _full((nh, tp)), blk, blk],
                 out_specs=[blk, col, col, _full((nh, tp))],
                 out_shape=[jax.ShapeDtypeStruct((tp, att_w), BF16)] * 3 + [jax.ShapeDtypeStruct((nh, tp), F32)],
                 scratch_shapes=[pltpu.VMEM((tp, gw), F32)] * 2,
                 semantics=("arbitrary", "arbitrary"), vmem_bytes=vm)


REC_ROWS = 128
HALO = SUBLANES


def _conv_taps(cat):
    taps = []
    for k in range(CONV_WIDTH):
        sh = CONV_WIDTH - 1 - k
        taps.append((pltpu.roll(cat, sh, 0) if sh else cat)[HALO:])
    return taps


def _rec_gates(xc, wa_ref, ba_ref, wx_ref, bx_ref, l_ref):
    xcb = xc.astype(BF16)
    r = _sigmoid(_dot(xcb, wa_ref[...]) + ba_ref[...])
    ig = _sigmoid(_dot(xcb, wx_ref[...]) + bx_ref[...])
    ls = _log_sigmoid(l_ref[...])
    log_a = RG_C * r * ls
    return xcb, r, ig, ls, log_a


def _rec_fwd(proj, xr_blk, yr_blk, rec_w, conv_w, conv_b, wa, ba, wx, bx, lru):
    tp = proj.shape[0]
    w = rec_w
    r_rows = REC_ROWS
    nc = tp // r_rows
    cpb = w // LANES

    def body(xr_ref, yr_ref, cw_ref, cb_ref, wa_ref, ba_ref, wx_ref, bx_ref, l_ref,
             hr_ref, rec_ref, prev_s, carry_s, a_s, u_s):
        i = pl.program_id(0)

        @pl.when(i == 0)
        def _():
            prev_s[...] = jnp.zeros_like(prev_s)
            carry_s[...] = jnp.zeros_like(carry_s)

        x = xr_ref[...]
        taps = _conv_taps(jnp.concatenate([prev_s[...], x], axis=0))
        prev_s[...] = x[r_rows - HALO:]
        xc = cb_ref[...]
        for k in range(CONV_WIDTH):
            xc = xc + cw_ref[k:k + 1, :] * taps[k]
        _, r, ig, ls, log_a = _rec_gates(xc, wa_ref, ba_ref, wx_ref, bx_ref, l_ref)
        a_s[...] = jnp.exp(log_a)
        u_s[...] = jnp.sqrt(-_expm1_nonpos(2.0 * log_a)) * ig * xc

        def tile(j, h):
            r0 = pl.multiple_of(j * SUBLANES, SUBLANES)
            at = a_s[pl.ds(r0, SUBLANES), :]
            ut = u_s[pl.ds(r0, SUBLANES), :]
            out = []
            for rr in range(SUBLANES):
                h = at[rr:rr + 1] * h + ut[rr:rr + 1]
                out.append(h)
            hr_ref[pl.ds(r0, SUBLANES), :] = jnp.concatenate(out, axis=0)
            return h

        carry_s[0:1, :] = lax.fori_loop(0, r_rows // SUBLANES, tile, carry_s[0:1, :])
        g, _ = _gelu_and_grad(yr_ref[...])
        rec_ref[...] = hr_ref[...] * g

    blk = pl.BlockSpec((r_rows, w), lambda i: (i, 0))
    vm = 16 * _nbytes((r_rows, w), F32) + 4 * _nbytes((w, w), BF16)
    return _call(body, (proj, proj, conv_w, conv_b, wa, ba, wx, bx, lru), name="rec_fwd", grid=(nc,),
                 in_specs=[pl.BlockSpec((r_rows, w), lambda i: (i, xr_blk)),
                           pl.BlockSpec((r_rows, w), lambda i: (i, yr_blk)),
                           _full((CONV_WIDTH, w)), _full((1, w)), _full((w, w)), _full((1, w)),
                           _full((w, w)), _full((1, w)), _full((1, w))],
                 out_specs=[blk, blk],
                 out_shape=[jax.ShapeDtypeStruct((tp, w), F32)] * 2,
                 scratch_shapes=[pltpu.VMEM((HALO, w), F32), pltpu.VMEM((SUBLANES, w), F32),
                                 pltpu.VMEM((r_rows, w), F32), pltpu.VMEM((r_rows, w), F32)],
                 semantics=("arbitrary",), vmem_bytes=vm)


def _rec_bwd(proj, xr_blk, yr_blk, rec_w, hr, drec, conv_w, conv_b, wa, ba, wx, bx, lru):
    tp = proj.shape[0]
    w = rec_w
    r_rows = REC_ROWS
    nc = tp // r_rows
    hpc = r_rows // HALO

    def body(xr_ref, xh_ref, yr_ref, hr_ref, hh_ref, drec_ref, cw_ref, cb_ref, wa_ref, ba_ref, wx_ref, bx_ref,
             l_ref, dxr_ref, dyr_ref, dwa_ref, dwx_ref, small_ref, lam_s, a_s, dhr_s, carry_s, next_s):
        i = pl.program_id(0)
        first = (nc - 1 - i) == 0

        @pl.when(i == 0)
        def _():
            carry_s[...] = jnp.zeros_like(carry_s)
            next_s[...] = jnp.zeros_like(next_s)
            dwa_ref[...] = jnp.zeros_like(dwa_ref)
            dwx_ref[...] = jnp.zeros_like(dwx_ref)
            small_ref[...] = jnp.zeros_like(small_ref)

        x = xr_ref[...]
        xprev = jnp.where(first, 0.0, xh_ref[...])
        taps = _conv_taps(jnp.concatenate([xprev, x], axis=0))
        xc = cb_ref[...]
        for k in range(CONV_WIDTH):
            xc = xc + cw_ref[k:k + 1, :] * taps[k]
        xcb, r, ig, ls, log_a = _rec_gates(xc, wa_ref, ba_ref, wx_ref, bx_ref, l_ref)
        a = jnp.exp(log_a)
        a2 = jnp.exp(2.0 * log_a)
        mult = jnp.sqrt(-_expm1_nonpos(2.0 * log_a))
        g, dg = _gelu_and_grad(yr_ref[...])
        hr_v = hr_ref[...]
        drec_v = drec_ref[...]
        dhr_s[...] = drec_v * g
        dyr_ref[...] = (drec_v * hr_v * dg).astype(BF16)
        a_s[...] = a

        def tile(jj, carry):
            r0 = pl.multiple_of((r_rows // SUBLANES - 1 - jj) * SUBLANES, SUBLANES)
            at = a_s[pl.ds(r0, SUBLANES), :]
            dt = dhr_s[pl.ds(r0, SUBLANES), :]
            out = [None] * SUBLANES
            for rr in range(SUBLANES - 1, -1, -1):
                lam = dt[rr:rr + 1] + carry
                out[rr] = lam
                carry = at[rr:rr + 1] * lam
            lam_s[pl.ds(r0, SUBLANES), :] = jnp.concatenate(out, axis=0)
            return carry

        carry_s[0:1, :] = lax.fori_loop(0, r_rows // SUBLANES, tile, carry_s[0:1, :])
        lam = lam_s[...]
        hprev = jnp.where(first, 0.0, hh_ref[...])
        hr_prev = pltpu.roll(jnp.concatenate([hprev, hr_v], axis=0), 1, 0)[HALO:]
        da = lam * hr_prev
        dxc = lam * mult * ig
        di = lam * mult * xc
        dmult = lam * ig * xc
        dlog_a = da * a - dmult * a2 / mult
        dr = dlog_a * (RG_C * ls)
        dls = jnp.sum(dlog_a * (RG_C * r), axis=0, keepdims=True)
        dga = dr * r * (1.0 - r)
        dgx = di * ig * (1.0 - ig)
        dgab = dga.astype(BF16)
        dgxb = dgx.astype(BF16)
        dxc = dxc + _dot_nt(dgab, wa_ref[...]) + _dot_nt(dgxb, wx_ref[...])
        dwa_ref[...] += _dot_tn(xcb, dgab)
        dwx_ref[...] += _dot_tn(xcb, dgxb)
        cat = jnp.concatenate([dxc, next_s[...]], axis=0)
        next_s[...] = dxc[0:HALO]
        dxr = cw_ref[CONV_WIDTH - 1:CONV_WIDTH, :] * dxc
        for k in range(CONV_WIDTH - 1):
            sh = CONV_WIDTH - 1 - k
            dxr = dxr + cw_ref[k:k + 1, :] * pltpu.roll(cat, r_rows + HALO - sh, 0)[:r_rows]
        dxr_ref[...] = dxr.astype(BF16)
        rows = [jnp.sum(dxc * taps[k], axis=0, keepdims=True) for k in range(CONV_WIDTH)]
        rows += [jnp.sum(dxc, axis=0, keepdims=True), jnp.sum(dga, axis=0, keepdims=True),
                 jnp.sum(dgx, axis=0, keepdims=True), dls * _sigmoid(-l_ref[...])]
        small_ref[...] += jnp.concatenate(rows, axis=0)

    def rev(i):
        return nc - 1 - i

    def halo(i):
        return jnp.maximum(rev(i) * hpc - 1, 0)

    blk = pl.BlockSpec((r_rows, w), lambda i: (rev(i), 0))
    vm = 40 * _nbytes((r_rows, w), F32) + 6 * _nbytes((w, w), F32)
    return _call(body, (proj, proj, proj, hr, hr, drec, conv_w, conv_b, wa, ba, wx, bx, lru),
                 name="rec_bwd", grid=(nc,),
                 in_specs=[pl.BlockSpec((r_rows, w), lambda i: (rev(i), xr_blk)),
                           pl.BlockSpec((HALO, w), lambda i: (halo(i), xr_blk)),
                           pl.BlockSpec((r_rows, w), lambda i: (rev(i), yr_blk)),
                           blk,
                           pl.BlockSpec((HALO, w), lambda i: (halo(i), 0)),
                           blk,
                           _full((CONV_WIDTH, w)), _full((1, w)), _full((w, w)), _full((1, w)),
                           _full((w, w)), _full((1, w)), _full((1, w))],
                 out_specs=[blk, blk, _full((w, w)), _full((w, w)), _full((SUBLANES, w))],
                 out_shape=[jax.ShapeDtypeStruct((tp, w), BF16)] * 2
                 + [jax.ShapeDtypeStruct((w, w), F32)] * 2 + [jax.ShapeDtypeStruct((SUBLANES, w), F32)],
                 scratch_shapes=[pltpu.VMEM((r_rows, w), F32)] * 3
                 + [pltpu.VMEM((SUBLANES, w), F32), pltpu.VMEM((HALO, w), F32)],
                 semantics=("arbitrary",), vmem_bytes=vm)


ROW_TARGET = 544


def _mixer_out(attn, rec, g_a, g_r, w_out, h, g_next):
    tp, d = h.shape
    aw, rw = attn.shape[1], rec.shape[1]
    kc = d // N_CHIPS
    tm = _divisor_tile(tp, 16, ROW_TARGET)

    def body(a_ref, r_ref, ga_ref, gr_ref, w_ref, h_ref, gn_ref, h1_ref, z_ref, mix_ref):
        a = a_ref[...]
        r = r_ref[...]
        mix = jnp.concatenate([a * _rstd(a) * ga_ref[...], r * _rstd(r) * gr_ref[...]], axis=1).astype(BF16)
        mix_ref[...] = mix
        h1 = h_ref[...]
        for j in range(N_CHIPS):
            h1 = h1 + _dot(mix[:, j * kc:(j + 1) * kc], w_ref[j])
        h1_ref[...] = h1
        z_ref[...] = (h1 * _rstd(h1) * gn_ref[...]).astype(BF16)

    row = lambda wd: pl.BlockSpec((tm, wd), lambda i: (i, 0))
    vm = 2 * _nbytes((d, d), BF16) + 12 * _nbytes((tm, d), F32)
    return _call(body, (attn, rec, g_a, g_r, w_out, h, g_next), name="mixer_out", grid=(tp // tm,),
                 in_specs=[row(aw), row(rw), _full((1, aw)), _full((1, rw)), _full(w_out.shape), row(d),
                           _full((1, d))],
                 out_specs=[row(d), row(d), row(d)],
                 out_shape=[jax.ShapeDtypeStruct((tp, d), F32), jax.ShapeDtypeStruct((tp, d), BF16),
                            jax.ShapeDtypeStruct((tp, d), BF16)],
                 semantics=("parallel",), vmem_bytes=vm)


def _mixer_bwd(dh_b, w_out, attn, rec, g_a, g_r):
    tp, d = dh_b.shape
    aw, rw = attn.shape[1], rec.shape[1]
    tm = _divisor_tile(tp, 16, ROW_TARGET)

    def body(dh_ref, w_ref, a_ref, r_ref, ga_ref, gr_ref, da_ref, dr_ref, dg_ref):
        @pl.when(pl.program_id(0) == 0)
        def _():
            dg_ref[...] = jnp.zeros_like(dg_ref)

        dh = dh_ref[...]
        dmix = jnp.concatenate([_dot_nt(dh, w_ref[j]) for j in range(N_CHIPS)], axis=1)
        da, dga = _rms_bwd(dmix[:, :aw], a_ref[...], ga_ref[...])
        dr, dgr = _rms_bwd(dmix[:, aw:], r_ref[...], gr_ref[...])
        da_ref[...] = da
        dr_ref[...] = dr
        dg_ref[...] += jnp.broadcast_to(jnp.concatenate([dga, dgr], axis=1), (SUBLANES, d))

    row = lambda wd: pl.BlockSpec((tm, wd), lambda i: (i, 0))
    vm = 2 * _nbytes((d, d), BF16) + 12 * _nbytes((tm, d), F32)
    return _call(body, (dh_b, w_out, attn, rec, g_a, g_r), name="mixer_bwd", grid=(tp // tm,),
                 in_specs=[row(d), _full(w_out.shape), row(aw), row(rw), _full((1, aw)), _full((1, rw))],
                 out_specs=[row(aw), row(rw), _full((SUBLANES, d))],
                 out_shape=[jax.ShapeDtypeStruct((tp, aw), F32), jax.ShapeDtypeStruct((tp, rw), F32),
                            jax.ShapeDtypeStruct((SUBLANES, d), F32)],
                 semantics=("arbitrary",), vmem_bytes=vm)


def _mlp_up(z, w_up):
    tp, d = z.shape
    fc = w_up.shape[2]
    ff = N_CHIPS * fc
    tn = _divisor_tile(fc, LANES, 512)
    per = fc // tn

    def body(z_ref, w_ref, act_ref, up_ref):
        up = _dot(z_ref[...], w_ref[...])
        r = jnp.maximum(up, 0.0)
        act_ref[...] = (r * r).astype(BF16)
        up_ref[...] = up.astype(BF16)

    col = pl.BlockSpec((tp, tn), lambda j: (0, j))
    vm = 2 * _nbytes((tp, d), BF16) + 2 * _nbytes((d, tn), BF16) + 8 * _nbytes((tp, tn), F32)
    return _call(body, (z, w_up), name="mlp_up", grid=(ff // tn,),
                 in_specs=[_full((tp, d)), pl.BlockSpec((None, d, tn), lambda j: (j // per, 0, j % per))],
                 out_specs=[col, col],
                 out_shape=[jax.ShapeDtypeStruct((tp, ff), BF16)] * 2,
                 semantics=("parallel",), vmem_bytes=vm)


def _mlp_down(act, w_down, h, g_next):
    tp, d = h.shape
    ff = act.shape[1]
    fc = ff // N_CHIPS
    tm = _divisor_tile(tp, 16, ROW_TARGET)

    def body(a_ref, w_ref, h_ref, gn_ref, h2_ref, z_ref):
        h2 = h_ref[...]
        for j in range(N_CHIPS):
            h2 = h2 + _dot(a_ref[:, j * fc:(j + 1) * fc], w_ref[j])
        h2_ref[...] = h2
        z_ref[...] = (h2 * _rstd(h2) * gn_ref[...]).astype(BF16)

    row = lambda wd: pl.BlockSpec((tm, wd), lambda i: (i, 0))
    vm = 2 * _nbytes((ff, d), BF16) + 2 * _nbytes((tm, ff), BF16) + 10 * _nbytes((tm, d), F32)
    return _call(body, (act, w_down, h, g_next), name="mlp_down", grid=(tp // tm,),
                 in_specs=[row(ff), _full(w_down.shape), row(d), _full((1, d))],
                 out_specs=[row(d), row(d)],
                 out_shape=[jax.ShapeDtypeStruct((tp, d), F32), jax.ShapeDtypeStruct((tp, d), BF16)],
                 semantics=("parallel",), vmem_bytes=vm)


def _loss_bwd(h, g, target, n_real):
    tp, d = h.shape
    tm = _divisor_tile(tp, 16, ROW_TARGET)

    def body(h_ref, g_ref, t_ref, dh_ref, dhb_ref, dg_ref, loss_ref):
        i = pl.program_id(0)

        @pl.when(i == 0)
        def _():
            dg_ref[...] = jnp.zeros_like(dg_ref)
            loss_ref[...] = jnp.zeros_like(loss_ref)

        x = h_ref[...]
        gv = g_ref[...]
        rowi = i * tm + lax.broadcasted_iota(jnp.int32, (tm, 1), 0)
        real = jnp.logical_and(rowi >= N_META, rowi < N_META + n_real)
        err = jnp.where(real, x * _rstd(x) * gv - t_ref[...], 0.0)
        loss_ref[...] += 0.5 * jnp.sum(jnp.mean(err * err, axis=-1, keepdims=True))
        dx, dgp = _rms_bwd(err * (1.0 / d), x, gv)
        dh_ref[...] = dx
        dhb_ref[...] = dx.astype(BF16)
        dg_ref[...] += jnp.broadcast_to(dgp, (SUBLANES, d))

    row = pl.BlockSpec((tm, d), lambda i: (i, 0))
    return _call(body, (h, g, target), name="loss_bwd", grid=(tp // tm,),
                 in_specs=[row, _full((1, d)), row],
                 out_specs=[row, row, _full((SUBLANES, d)), _full((SUBLANES, LANES))],
                 out_shape=[jax.ShapeDtypeStruct((tp, d), F32), jax.ShapeDtypeStruct((tp, d), BF16),
                            jax.ShapeDtypeStruct((SUBLANES, d), F32), jax.ShapeDtypeStruct((SUBLANES, LANES), F32)],
                 semantics=("arbitrary",), vmem_bytes=16 * _nbytes((tm, d), F32))


def _mlp_dup(dh_b, w_down, up):
    tp, d = dh_b.shape
    fc = w_down.shape[1]
    ff = N_CHIPS * fc
    tn = _divisor_tile(fc, LANES, 512)
    per = fc // tn

    def body(dh_ref, w_ref, up_ref, dup_ref):
        dact = _dot_nt(dh_ref[...], w_ref[...])
        dup_ref[...] = (dact * (2.0 * jnp.maximum(up_ref[...].astype(F32), 0.0))).astype(BF16)

    col = pl.BlockSpec((tp, tn), lambda j: (0, j))
    vm = 2 * _nbytes((tp, d), BF16) + 2 * _nbytes((tn, d), BF16) + 8 * _nbytes((tp, tn), F32)
    return _call(body, (dh_b, w_down, up), name="mlp_dup", grid=(ff // tn,),
                 in_specs=[_full((tp, d)), pl.BlockSpec((None, tn, d), lambda j: (j // per, j % per, 0)),
                           col],
                 out_specs=col, out_shape=jax.ShapeDtypeStruct((tp, ff), BF16),
                 semantics=("parallel",), vmem_bytes=vm)


def _grad_w(a, b, cols_to_chips=False):
    tp, k = a.shape
    n = b.shape[1]
    tk = _divisor_tile(k, LANES, 1024)
    nc = n // N_CHIPS if cols_to_chips else n
    tn = _divisor_tile(nc, LANES, 512)
    per = nc // tn

    def body(a_ref, b_ref, o_ref):
        o_ref[...] = _dot_tn(a_ref[...], b_ref[...]).astype(BF16)

    if cols_to_chips:
        out_spec = pl.BlockSpec((None, tk, tn), lambda i, j: (j // per, i, j % per))
        out_shape = jax.ShapeDtypeStruct((N_CHIPS, k, nc), BF16)
    else:
        out_spec = pl.BlockSpec((tk, tn), lambda i, j: (i, j))
        out_shape = jax.ShapeDtypeStruct((k, n), BF16)
    vm = 2 * _nbytes((tp, tk), BF16) + 2 * _nbytes((tp, tn), BF16) + 6 * _nbytes((tk, tn), F32) \
        + 2 * _nbytes((tp, tk), F32)
    return _call(body, (a, b), name="grad_w", grid=(k // tk, n // tn),
                 in_specs=[pl.BlockSpec((tp, tk), lambda i, j: (0, i)), pl.BlockSpec((tp, tn), lambda i, j: (0, j))],
                 out_specs=out_spec, out_shape=out_shape,
                 semantics=("parallel", "parallel"), vmem_bytes=vm)


def _dx_norm_bwd(pieces, w, w_spec, w_piece, h, g, dres, dot=_dot_nt):
    tp, d = h.shape
    tm = _divisor_tile(tp, 16, ROW_TARGET)
    n = len(pieces)

    def body(*refs):
        dy_refs = refs[:n]
        w_ref, h_ref, g_ref, dres_ref, dh_ref, dhb_ref, dg_ref = refs[n:]

        @pl.when(pl.program_id(0) == 0)
        def _():
            dg_ref[...] = jnp.zeros_like(dg_ref)

        dz = dot(dy_refs[0][...], w_piece(w_ref, 0))
        for i in range(1, n):
            dz = dz + dot(dy_refs[i][...], w_piece(w_ref, i))
        dx, dgp = _rms_bwd(dz, h_ref[...], g_ref[...])
        dh = dres_ref[...] + dx
        dh_ref[...] = dh
        dhb_ref[...] = dh.astype(BF16)
        dg_ref[...] += jnp.broadcast_to(dgp, (SUBLANES, d))

    row = lambda wd: pl.BlockSpec((tm, wd), lambda i: (i, 0))
    kk = sum(wd for _, _, wd in pieces)
    vm = 2 * _nbytes((d, kk), BF16) + 2 * _nbytes((tm, kk), BF16) + 14 * _nbytes((tm, d), F32)
    piece_specs = [pl.BlockSpec((tm, wd), functools.partial(lambda i, cb: (i, cb), cb=cb)) for _, cb, wd in pieces]
    return _call(body, tuple(a for a, _, _ in pieces) + (w, h, g, dres), name="dx_norm_bwd", grid=(tp // tm,),
                 in_specs=piece_specs + [w_spec, row(d), _full((1, d)), row(d)],
                 out_specs=[row(d), row(d), _full((SUBLANES, d))],
                 out_shape=[jax.ShapeDtypeStruct((tp, d), F32), jax.ShapeDtypeStruct((tp, d), BF16),
                            jax.ShapeDtypeStruct((SUBLANES, d), F32)],
                 semantics=("arbitrary",), vmem_bytes=vm)


def _block_diag(wg):
    nb, b, _ = wg.shape
    eye = jnp.eye(nb, dtype=wg.dtype)
    return (eye[:, None, :, None] * wg[:, :, None, :]).reshape(nb * b, nb * b)


def _diag_blocks(dense, nb):
    b = dense.shape[0] // nb
    d4 = dense.reshape(nb, b, nb, b)
    return jnp.stack([d4[i, :, i, :] for i in range(nb)])


def _row(v):
    return v.reshape(1, -1)


def _forward_layer(l, h, z, p, fetch, g_next):
    d = h.shape[1]
    att_w = d // 2
    rec_w = d - att_w
    nh = att_w // HEAD_DIM
    xr_blk = 3 * att_w // rec_w
    wa_d = _block_diag(p["w_gate_a"][l]).astype(BF16)
    wx_d = _block_diag(p["w_gate_x"][l]).astype(BF16)
    b_f_pad = jnp.zeros((1, LANES), F32).at[0, :nh].set(p["b_f"][l])
    w_in_t = fetch("w_in", h)
    big = dict(w_in_big=_pack_w_in_t(w_in_t.reshape(-1, d), att_w, nh))
    proj, qkv = _proj(z, big["w_in_big"], att_w)
    c, c_t = _fgate_fwd(proj, b_f_pad, nh)
    attn, lse_b = _attn_fwd(qkv, c, c_t, nh)
    hr, rec = _rec_fwd(proj, xr_blk, xr_blk + 1, rec_w, p["conv_w"][l], _row(p["conv_b"][l]), wa_d,
                       _row(p["b_gate_a"][l]), wx_d, _row(p["b_gate_x"][l]), _row(p["lru_L"][l]))
    big["w_out"] = fetch("w_out", rec)
    h1, z2, mix = _mixer_out(attn, rec, _row(p["attn_out_g"][l]), _row(p["rec_out_g"][l]),
                             big["w_out"], h, _row(p["mlp_norm_g"][l]))
    big["w_up"] = fetch("w_up", h1)
    act, up = _mlp_up(z2, big["w_up"])
    big["w_down"] = fetch("w_down", act)
    h2, z_next = _mlp_down(act, big["w_down"], h1, _row(g_next))
    saved = dict(h0=h, z1=z, proj=proj, qkv=qkv, c=c, c_t=c_t, attn=attn, lse_b=lse_b, hr=hr, rec=rec, h1=h1,
                 z2=z2, mix=mix, act=act, up=up, wa_d=wa_d, wx_d=wx_d, b_f_pad=b_f_pad, big=big)
    return h2, z_next, saved


def _backward_mlp(l, dh, dh_b, sv, p, tok):
    w_up, w_down = sv["big"]["w_up"], sv["big"]["w_down"]
    fc = w_up.shape[2]
    dup = _mlp_dup(dh_b, w_down, sv["up"])
    g_down = _grad_w(sv["act"], dh_b)
    g_up = _grad_w(sv["z2"], dup, cols_to_chips=True)
    dh, dh_b, dg2 = _dx_norm_bwd([(dup, j, fc) for j in range(N_CHIPS)], w_up, _full(w_up.shape),
                                 lambda w_ref, j: w_ref[j], sv["h1"], _row(p["mlp_norm_g"][l] + tok), dh)
    big = dict(w_down=g_down.reshape((N_CHIPS, -1) + g_down.shape[1:]), w_up=g_up)
    return dh, dh_b, big, dict(mlp_norm_g=dg2[0])


def _backward_mixer(l, dh, dh_b, sv, p, tok):
    d = dh.shape[1]
    att_w = d // 2
    rec_w = d - att_w
    nh = att_w // HEAD_DIM
    xr_blk = 3 * att_w // rec_w
    small = {}
    g_out = _grad_w(sv["mix"], dh_b)
    dattn, drec, dg_mix = _mixer_bwd(dh_b, sv["big"]["w_out"], sv["attn"], sv["rec"],
                                     _row(p["attn_out_g"][l] + tok), _row(p["rec_out_g"][l]))
    small["attn_out_g"] = dg_mix[0, :att_w]
    small["rec_out_g"] = dg_mix[0, att_w:]
    dxr, dyr, dwa, dwx, sm = _rec_bwd(
        sv["proj"], xr_blk, xr_blk + 1, rec_w, sv["hr"], drec, p["conv_w"][l], _row(p["conv_b"][l]), sv["wa_d"],
        _row(p["b_gate_a"][l]), sv["wx_d"], _row(p["b_gate_x"][l]), _row(p["lru_L"][l]))
    small.update(conv_w=sm[:CONV_WIDTH], conv_b=sm[4], b_gate_a=sm[5], b_gate_x=sm[6], lru_L=sm[7],
                 w_gate_a=_diag_blocks(dwa, N_REC_BLOCKS), w_gate_x=_diag_blocks(dwx, N_REC_BLOCKS))
    dq, dk, dv, dct = _attn_bwd(sv["qkv"], sv["c"], sv["c_t"], sv["lse_b"], dattn, nh)
    df, db_f = _fgate_bwd(sv["proj"], sv["b_f_pad"], dct)
    small["b_f"] = db_f[0, :nh]
    pieces = [dq, dk, dv, dxr, dyr, df]
    offs = [0, att_w, 2 * att_w, 3 * att_w, 3 * att_w + rec_w, 3 * att_w + 2 * rec_w]
    gq, gk, gv, gxr, gyr, gf = [_grad_w(pc, sv["z1"]) for pc in pieces]
    g_in_t = jnp.concatenate([gq, gk, gv, gf[:nh], gxr, gyr], axis=0)
    w_big = sv["big"]["w_in_big"]
    widths = [pc.shape[1] for pc in pieces]
    dh, dh_b, dg1 = _dx_norm_bwd(
        [(pc, 0, wd) for pc, wd in zip(pieces, widths)], w_big, _full(w_big.shape),
        lambda w_ref, i: w_ref[offs[i]:offs[i] + widths[i], :], sv["h0"], _row(p["attn_norm_g"][l]), dh, dot=_dot)
    small["attn_norm_g"] = dg1[0]
    big = dict(w_in=g_in_t.reshape(N_CHIPS, -1, d), w_out=g_out.reshape((N_CHIPS, -1) + g_out.shape[1:]))
    return dh, dh_b, big, small


def _pack_w_in_t(w_in_t, att_w, nh):
    qkv = w_in_t[:3 * att_w]
    f = w_in_t[3 * att_w:3 * att_w + nh]
    xy = w_in_t[3 * att_w + nh:]
    return jnp.concatenate([qkv, xy, f, jnp.zeros((LANES - nh, w_in_t.shape[1]), w_in_t.dtype)], axis=0)


ANY = pl.BlockSpec(memory_space=pl.ANY)


def _coords():
    return lax.axis_index("x"), lax.axis_index("y"), lax.axis_index("c")


def _other_chips(x, y):
    return [(1 - x, y), (x, 1 - y), (1 - x, 1 - y)]


def _remote(src, dst, send_sems, recv_sems, k, to):
    return pltpu.make_async_remote_copy(src_ref=src, dst_ref=dst, send_sem=send_sems.at[k],
                                        recv_sem=recv_sems.at[k], device_id=to, device_id_type=MESH)


def _all_gather_chips(shards):
    n = len(shards)
    per = 6

    def body(*refs):
        ins, outs = refs[:n], refs[n:2 * n]
        send_sems, recv_sems, local_sems = refs[2 * n:]
        x, y, c = _coords()
        me = 2 * x + y
        sibling = (x, y, 1 - c)
        chips = _other_chips(x, y)
        local = [pltpu.make_async_copy(ins[t], outs[t].at[me], local_sems.at[t]) for t in range(n)]
        for cp in local:
            cp.start()
        sends = []
        for t in range(n):
            for j, (px, py) in enumerate(chips):
                cp = _remote(ins[t].at[c], outs[t].at[me, c], send_sems, recv_sems, per * t + j, (px, py, c))
                cp.start()
                sends.append(cp)
        for t in range(n):
            for j, (px, py) in enumerate(chips):
                landed = outs[t].at[2 * px + py, c]
                _remote(landed, landed, send_sems, recv_sems, per * t + j, (px, py, c)).wait_recv()
                cp = _remote(landed, landed, send_sems, recv_sems, per * t + 3 + j, sibling)
                cp.start()
                sends.append(cp)
        for t in range(n):
            for j, (px, py) in enumerate(chips):
                passed = outs[t].at[2 * px + py, 1 - c]
                _remote(passed, passed, send_sems, recv_sems, per * t + 3 + j, sibling).wait_recv()
        for cp in sends:
            cp.wait_send()
        for cp in local:
            cp.wait()

    return _call(body, tuple(shards), name="all_gather_chips",
                 in_specs=[ANY] * n, out_specs=[ANY] * n,
                 out_shape=[jax.ShapeDtypeStruct((N_CHIPS,) + s.shape, s.dtype) for s in shards],
                 scratch_shapes=[pltpu.SemaphoreType.DMA((per * n,)), pltpu.SemaphoreType.DMA((per * n,)),
                                 pltpu.SemaphoreType.DMA((n,))])


HBM = pl.BlockSpec(memory_space=pltpu.HBM)
SEM = pl.BlockSpec(memory_space=pltpu.SEMAPHORE)
DATAFLOW = pltpu.SideEffectType.DATAFLOW_SIDE_EFFECTING


def _in_hbm(a):
    return pltpu.with_memory_space_constraint(a, pltpu.HBM)


PUSH_ARRIVALS = {"gather_chips": N_CHIPS - 1, "scatter_chips": N_CHIPS - 1, "sibling": 1, "gather_devices": N_DEV - 1}


def _push_copies(mode, src, land, send_sems, recv_sems, t):
    x, y, c = _coords()
    chip = 2 * x + y
    if mode == "gather_chips":
        return [_remote(src.at[chip], land.at[chip], send_sems, recv_sems, t, (px, py, c))
                for px, py in _other_chips(x, y)]
    if mode == "scatter_chips":
        return [_remote(src.at[2 * px + py], land.at[chip], send_sems, recv_sems, t, (px, py, c))
                for px, py in _other_chips(x, y)]
    if mode == "sibling":
        return [_remote(src, land, send_sems, recv_sems, t, (x, y, 1 - c))]
    dev = 4 * x + 2 * y + c
    return [_remote(src.at[dev], land.at[dev], send_sems, recv_sems, t, (x ^ (k >> 2), y ^ ((k >> 1) & 1), c ^ (k & 1)))
            for k in range(1, N_DEV)]


def _push_start(srcs, lands, mode, name):
    n = len(srcs)
    same = all(s is ld for s, ld in zip(srcs, lands))
    n_in = n if same else 2 * n

    def body(*refs):
        src_refs = refs[:n]
        land_refs = src_refs if same else refs[n:2 * n]
        send_sems, recv_sems = refs[n_in], refs[n_in + 1]
        token = refs[-1]
        for t in range(n):
            for cp in _push_copies(mode, src_refs[t], land_refs[t], send_sems, recv_sems, t):
                cp.start()
        token[...] = jnp.zeros_like(token)

    operands = tuple(srcs) if same else tuple(srcs) + tuple(lands)
    res = _call(
        body, [_in_hbm(a) for a in operands], name=name,
        out_shape=(pltpu.SemaphoreType.DMA((n,)), pltpu.SemaphoreType.DMA((n,)))
        + tuple(pltpu.HBM(a.shape, a.dtype) for a in operands) + (jax.ShapeDtypeStruct((SUBLANES, LANES), F32),),
        in_specs=[HBM] * n_in, out_specs=(SEM, SEM) + (HBM,) * n_in + (pl.BlockSpec(memory_space=pltpu.VMEM),),
        input_output_aliases={i: 2 + i for i in range(n_in)}, side_effects=DATAFLOW, hbm_results=False)
    send_sems, recv_sems, token = res[0], res[1], res[-1]
    srcs_thru = res[2:2 + n]
    lands_thru = srcs_thru if same else res[2 + n:2 + 2 * n]
    return send_sems, recv_sems, srcs_thru, lands_thru, token


def _push_wait(send_sems, recv_sems, ids, srcs, lands, mode, after, name):
    n = len(lands)
    same = all(s is ld for s, ld in zip(srcs, lands))
    n_in = n if same else 2 * n

    def body(*refs):
        land_refs = refs[:n] if same else refs[n:2 * n]
        send_sems, recv_sems = refs[n_in], refs[n_in + 1]
        x, y, c = _coords()
        for t in range(n):
            moved = land_refs[t] if mode == "sibling" else land_refs[t].at[pl.ds(0, PUSH_ARRIVALS[mode])]
            arrivals = _remote(moved, moved, send_sems, recv_sems, ids[t], (x, y, c))
            arrivals.wait_send()
            arrivals.wait_recv()

    operands = tuple(lands) if same else tuple(srcs) + tuple(lands)
    res = _call(
        body, operands + (send_sems, recv_sems, after), name=name,
        out_shape=tuple(pltpu.HBM(a.shape, a.dtype) for a in operands),
        in_specs=[HBM] * n_in + [SEM, SEM, ANY], out_specs=(HBM,) * n_in,
        input_output_aliases={i: i for i in range(n_in)}, side_effects=DATAFLOW)
    return list(res) if same else (list(res[:n]), list(res[n:]))


def _sum_partials(part, landed, chip):
    _, rows, cols = part.shape
    br = _divisor_tile(rows, 16, ELEM_ROWS)

    def body(chip_ref, own_ref, a_ref, b_ref, c_ref, o_ref):
        o_ref[...] = ((own_ref[...].astype(F32) + a_ref[...].astype(F32)) + b_ref[...].astype(F32)) \
            + c_ref[...].astype(F32)

    def other(k):
        return pl.BlockSpec((None, br, cols), lambda i, ch: (jnp.where(ch[0] <= k, k + 1, k), i, 0))

    spec = pltpu.PrefetchScalarGridSpec(
        num_scalar_prefetch=1, grid=(rows // br,),
        in_specs=[pl.BlockSpec((None, br, cols), lambda i, ch: (ch[0], i, 0)), other(0), other(1), other(2)],
        out_specs=pl.BlockSpec((br, cols), lambda i, ch: (i, 0)))
    return _call(body, (chip, part, landed, landed, landed), name="sum_partials", grid_spec=spec,
                 out_shape=jax.ShapeDtypeStruct((rows, cols), F32), semantics=("parallel",))


def _cast_to_slab(w, l, chip):
    _, rows, cols = w.shape
    br = _divisor_tile(rows, 16, ELEM_ROWS)

    def body(chip_ref, w_ref, o_ref):
        o_ref[...] = w_ref[...].astype(BF16)

    spec = pltpu.PrefetchScalarGridSpec(
        num_scalar_prefetch=1, grid=(rows // br,),
        in_specs=[pl.BlockSpec((None, br, cols), lambda i, ch: (l, i, 0))],
        out_specs=pl.BlockSpec((None, br, cols), lambda i, ch: (ch[0], i, 0)))
    return _call(body, (chip, w), name="cast_to_slab", grid_spec=spec,
                 out_shape=jax.ShapeDtypeStruct((N_CHIPS, rows, cols), BF16), semantics=("parallel",))


def _cast_w_in_t_to_slabs(w_t, chip):
    rows, depth, d = w_t.shape
    tn = _divisor_tile(d, LANES, 256)

    def body(chip_ref, w_ref, *o_refs):
        for l in range(depth):
            o_refs[l][...] = w_ref[:, l, :].astype(BF16)

    spec = pltpu.PrefetchScalarGridSpec(
        num_scalar_prefetch=1, grid=(d // tn,),
        in_specs=[pl.BlockSpec((rows, depth, tn), lambda j, ch: (0, 0, j))],
        out_specs=[pl.BlockSpec((None, rows, tn), lambda j, ch: (ch[0], 0, j))] * depth)
    return _call(body, (chip, w_t), name="cast_w_in_t_to_slabs", grid_spec=spec,
                 out_shape=[jax.ShapeDtypeStruct((N_CHIPS, rows, d), BF16)] * depth, semantics=("parallel",),
                 vmem_bytes=4 * _nbytes((rows, max(depth, SUBLANES), tn), F32))


def _place_slab(buf, index, n_slabs):
    rows, cols = buf.shape
    br = _divisor_tile(rows, SUBLANES, ELEM_ROWS)

    def body(index_ref, b_ref, o_ref):
        o_ref[...] = b_ref[...]

    spec = pltpu.PrefetchScalarGridSpec(
        num_scalar_prefetch=1, grid=(rows // br,),
        in_specs=[pl.BlockSpec((br, cols), lambda i, ix: (i, 0))],
        out_specs=pl.BlockSpec((None, br, cols), lambda i, ix: (ix[0], i, 0)))
    return _call(body, (index, buf), name="place_slab", grid_spec=spec,
                 out_shape=jax.ShapeDtypeStruct((n_slabs, rows, cols), buf.dtype), semantics=("parallel",))


ELEM_ROWS = 256


def _sum_slabs(r):
    n, rows, cols = r.shape
    br = _divisor_tile(rows, 16, ELEM_ROWS)

    def body(r_ref, o_ref):
        acc = r_ref[0].astype(F32)
        for j in range(1, n):
            acc = acc + r_ref[j].astype(F32)
        o_ref[...] = acc

    return _call(body, (r,), name="sum_slabs", grid=(rows // br,),
                 in_specs=[pl.BlockSpec((n, br, cols), lambda i: (0, i, 0))],
                 out_specs=pl.BlockSpec((br, cols), lambda i: (i, 0)),
                 out_shape=jax.ShapeDtypeStruct((rows, cols), F32), semantics=("parallel",))


def _adamw_math(w, g, m, v):
    c1 = 1.0 - ADAM_B1 ** ADAM_STEP
    c2 = 1.0 - ADAM_B2 ** ADAM_STEP
    nm = ADAM_B1 * m + (1.0 - ADAM_B1) * g
    nv = ADAM_B2 * v + (1.0 - ADAM_B2) * (g * g)
    delta = -ADAM_LR * ((nm / c1) / (jnp.sqrt(nv / c2) + ADAM_EPS) + ADAM_WD * w)
    return delta, nm, nv


def _adamw(w, g, m, v):
    rows, cols = w.shape
    br = _divisor_tile(rows, 8, ELEM_ROWS)

    def body(w_ref, g_ref, m_ref, v_ref, d_ref, nm_ref, nv_ref):
        d_ref[...], nm_ref[...], nv_ref[...] = _adamw_math(w_ref[...], g_ref[...], m_ref[...], v_ref[...])

    blk = pl.BlockSpec((br, cols), lambda i: (i, 0))
    return _call(body, (w, g, m, v), name="adamw", grid=(rows // br,),
                 in_specs=[blk] * 4, out_specs=[blk] * 3,
                 out_shape=[jax.ShapeDtypeStruct((rows, cols), F32)] * 3, semantics=("parallel",))


def _adamw_w_in_t(w_t, m_t, v_t, g_mine, g_theirs):
    rows, depth, d = w_t.shape
    tn = LANES

    def body(w_ref, m_ref, v_ref, *rest):
        ga_refs, gb_refs = rest[:depth], rest[depth:2 * depth]
        g_ref, d_ref, nm_ref, nv_ref = rest[2 * depth:]
        for l in range(depth):
            g = ga_refs[l][...] + gb_refs[l][...]
            g_ref[:, l, :] = g
            d_ref[:, l, :], nm_ref[:, l, :], nv_ref[:, l, :] = _adamw_math(w_ref[:, l, :], g, m_ref[:, l, :],
                                                                           v_ref[:, l, :])

    slab = pl.BlockSpec((rows, depth, tn), lambda j: (0, 0, j))
    gblk = pl.BlockSpec((rows, tn), lambda j: (0, j))
    return _call(body, (w_t, m_t, v_t) + tuple(g_mine) + tuple(g_theirs), name="adamw_w_in_t", grid=(d // tn,),
                 in_specs=[slab] * 3 + [gblk] * (2 * depth), out_specs=[slab] * 4,
                 out_shape=[jax.ShapeDtypeStruct(w_t.shape, F32)] * 4, semantics=("parallel",),
                 vmem_bytes=2 * (7 * _nbytes((rows, max(depth, SUBLANES), tn), F32)
                                 + 2 * depth * _nbytes((rows, tn), F32)))


def _adamw_layer(w, m, v, l, g_mine, g_theirs, prev):
    _, rows, cols = w.shape
    br = _divisor_tile(rows, 8, ELEM_ROWS)

    def body(w_ref, m_ref, v_ref, ga_ref, gb_ref, *rest):
        g_ref, d_ref, nm_ref, nv_ref = rest[4:]
        g = ga_ref[...] + gb_ref[...]
        g_ref[...] = g
        d_ref[...], nm_ref[...], nv_ref[...] = _adamw_math(w_ref[...], g, m_ref[...], v_ref[...])

    slot = pl.BlockSpec((None, br, cols), lambda i: (l, i, 0))
    blk = pl.BlockSpec((br, cols), lambda i: (i, 0))
    return _call(body, (w, m, v, g_mine, g_theirs) + tuple(prev), name="adamw_layer", grid=(rows // br,),
                 in_specs=[slot] * 3 + [blk] * 2 + [ANY] * 4, out_specs=[slot] * 4,
                 out_shape=[jax.ShapeDtypeStruct(w.shape, F32)] * 4,
                 input_output_aliases={5: 0, 6: 1, 7: 2, 8: 3}, semantics=("parallel",))


BIG = ("w_in", "w_out", "w_up", "w_down")
WEIGHTS = ("meta", "attn_norm_g", "w_in", "b_f", "conv_w", "conv_b", "w_gate_a", "b_gate_a", "w_gate_x",
           "b_gate_x", "lru_L", "attn_out_g", "rec_out_g", "w_out", "mlp_norm_g", "w_up", "w_down", "final_g")
SMALL = tuple(k for k in WEIGHTS if k not in BIG)
COL_SHARDED_SMALL = ("meta", "conv_w")
PACK_UNIT = ELEM_ROWS * LANES


def _pack(arrs):
    flat = jnp.concatenate([a.reshape(-1) for a in arrs])
    total = -(-flat.shape[0] // PACK_UNIT) * PACK_UNIT
    return jnp.pad(flat, (0, total - flat.shape[0])).reshape(-1, LANES)


def _unpack(buf, shapes):
    flat = buf.reshape(-1)
    out, off = [], 0
    for s in shapes:
        size = math.prod(s)
        out.append(flat[off:off + size].reshape(s))
        off += size
    return out


def _halves(a):
    return a.reshape((2, a.shape[0] // 2) + a.shape[1:])


def _cols_from_chips(g):
    return jnp.moveaxis(g, 0, -2).reshape(g.shape[1:-1] + (N_CHIPS * g.shape[-1],))


def kernel(x, meta, attn_norm_g, w_in, b_f, conv_w, conv_b, w_gate_a, b_gate_a, w_gate_x, b_gate_x, lru_L, attn_out_g, rec_out_g, w_out, mlp_norm_g, w_up, w_down, final_g, loss_target, m_meta, m_attn_norm_g, m_w_in, m_b_f, m_conv_w, m_conv_b, m_w_gate_a, m_b_gate_a, m_w_gate_x, m_b_gate_x, m_lru_L, m_attn_out_g, m_rec_out_g, m_w_out, m_mlp_norm_g, m_w_up, m_w_down, m_final_g, v_meta, v_attn_norm_g, v_w_in, v_b_f, v_conv_w, v_conv_b, v_w_gate_a, v_b_gate_a, v_w_gate_x, v_b_gate_x, v_lru_L, v_attn_out_g, v_rec_out_g, v_w_out, v_mlp_norm_g, v_w_up, v_w_down, v_final_g):
    w = dict(meta=meta, attn_norm_g=attn_norm_g, w_in=w_in, b_f=b_f, conv_w=conv_w, conv_b=conv_b,
             w_gate_a=w_gate_a, b_gate_a=b_gate_a, w_gate_x=w_gate_x, b_gate_x=b_gate_x, lru_L=lru_L,
             attn_out_g=attn_out_g, rec_out_g=rec_out_g, w_out=w_out, mlp_norm_g=mlp_norm_g, w_up=w_up,
             w_down=w_down, final_g=final_g)
    m = dict(meta=m_meta, attn_norm_g=m_attn_norm_g, w_in=m_w_in, b_f=m_b_f, conv_w=m_conv_w, conv_b=m_conv_b,
             w_gate_a=m_w_gate_a, b_gate_a=m_b_gate_a, w_gate_x=m_w_gate_x, b_gate_x=m_b_gate_x, lru_L=m_lru_L,
             attn_out_g=m_attn_out_g, rec_out_g=m_rec_out_g, w_out=m_w_out, mlp_norm_g=m_mlp_norm_g,
             w_up=m_w_up, w_down=m_w_down, final_g=m_final_g)
    v = dict(meta=v_meta, attn_norm_g=v_attn_norm_g, w_in=v_w_in, b_f=v_b_f, conv_w=v_conv_w, conv_b=v_conv_b,
             w_gate_a=v_w_gate_a, b_gate_a=v_b_gate_a, w_gate_x=v_w_gate_x, b_gate_x=v_b_gate_x, lru_L=v_lru_L,
             attn_out_g=v_attn_out_g, rec_out_g=v_rec_out_g, w_out=v_w_out, mlp_norm_g=v_mlp_norm_g,
             w_up=v_w_up, w_down=v_w_down, final_g=v_final_g)
    s_len, d = x.shape[1], x.shape[2]
    depth = w_in.shape[0]
    att_w = d // 2
    rec_w = d - att_w
    nh = att_w // HEAD_DIM
    chip = 2 * lax.axis_index("x") + lax.axis_index("y")

    g_conv, g_meta = [g.reshape((N_CHIPS, g.shape[1] * g.shape[2]) + g.shape[3:])
                      for g in _all_gather_chips([_halves(w["conv_w"]), _halves(w["meta"])])]
    p = dict(w)
    p["conv_w"] = _cols_from_chips(g_conv)
    meta_full = jnp.moveaxis(g_meta, 0, 1).reshape(N_META, d)

    chip1 = chip.reshape(1).astype(jnp.int32)
    w_in_t, m_in_t, v_in_t = [jnp.transpose(a["w_in"], (2, 0, 1)) for a in (w, m, v)]
    w_in_slabs = _cast_w_in_t_to_slabs(w_in_t, chip1)
    pushes, tokens = [], []
    for l in range(depth):
        slabs = [w_in_slabs[l]] + [_cast_to_slab(w[k], l, chip1) for k in BIG[1:]]
        send_sems, recv_sems, _, lands, token = _push_start(slabs, slabs, "gather_chips", f"weights_start_{l}")
        pushes.append((send_sems, recv_sems, lands))
        tokens.append(token[0, 0])

    t_len = N_META + s_len
    pad = -t_len % SEQ_TILE
    h = jnp.concatenate([meta_full, x[0], jnp.zeros((pad, d), F32)], axis=0)
    tgt = jnp.concatenate([jnp.zeros((N_META, d), F32), loss_target[0], jnp.zeros((pad, d), F32)], axis=0)
    z = _rms_fwd(h, _row(p["attn_norm_g"][0] + sum(tokens)))
    saved = []
    for l in range(depth):
        send_sems, recv_sems, lands = pushes[l]

        def fetch(k, after, l=l, send_sems=send_sems, recv_sems=recv_sems, lands=lands):
            i = BIG.index(k)
            return _push_wait(send_sems, recv_sems, [i], [lands[i]], [lands[i]], "gather_chips", after,
                              f"{k}_wait_{l}")[0]

        g_next = p["attn_norm_g"][l + 1] if l + 1 < depth else p["final_g"]
        h, z, sv = _forward_layer(l, h, z, p, fetch, g_next)
        saved.append(sv)
    dh, dh_b, dg_final, loss_part = _loss_bwd(h, _row(p["final_g"]), tgt, s_len)

    small = {k: [None] * depth for k in SMALL if k not in ("meta", "final_g")}
    pushes = [None] * depth
    tok = 0.0
    for l in reversed(range(depth)):
        dh, dh_b, big_mlp, sm_mlp = _backward_mlp(l, dh, dh_b, saved[l], p, tok)
        parts = [big_mlp["w_down"], big_mlp["w_up"]]
        push_mlp = _push_start(parts, [lax.empty(a.shape, a.dtype) for a in parts], "scatter_chips",
                               f"mlp_grads_start_{l}")
        dh, dh_b, big_mix, sm_mix = _backward_mixer(l, dh, dh_b, saved[l], p, push_mlp[4][0, 0])
        parts = [big_mix["w_out"], big_mix["w_in"]]
        push_mix = _push_start(parts, [lax.empty(a.shape, a.dtype) for a in parts], "scatter_chips",
                               f"mixer_grads_start_{l}")
        tok = push_mix[4][0, 0]
        pushes[l] = {("w_down", "w_up"): push_mlp, ("w_out", "w_in"): push_mix}
        for k, val in {**sm_mlp, **sm_mix}.items():
            small[k][l] = val
    grads = {k: jnp.stack(val) for k, val in small.items()}
    grads["final_g"] = dg_final[0]
    grads["meta"] = dh[:N_META]
    dx = dh[N_META:t_len]

    full_shapes = [grads[k].shape for k in SMALL] + [(1,)]
    packed = _pack([grads[k].astype(F32) for k in SMALL] + [loss_part[0, :1] + tok])
    dev1 = (2 * chip + lax.axis_index("c")).reshape(1).astype(jnp.int32)
    slabs = [_place_slab(packed, dev1, N_DEV)]
    small_push = _push_start(slabs, slabs, "gather_devices", "small_grads_start")

    last_token = small_push[4]
    outs = {k: [lax.empty(w[k].shape, F32) for _ in range(4)] for k in BIG[1:]}
    w_in_sums = [None] * depth
    swaps = {}

    def finish(l, after):
        send_sems, recv_sems, mine, lands, _ = swaps[l]
        mine, theirs = _push_wait(send_sems, recv_sems, list(range(len(BIG))), mine, lands, "sibling", after,
                                  f"sums_wait_{l}")
        w_in_sums[l] = (mine[0], theirs[0])
        for k, a, b in zip(BIG[1:], mine[1:], theirs[1:]):
            outs[k] = _adamw_layer(w[k], m[k], v[k], l, a, b, outs[k])

    for l in reversed(range(depth)):
        sums = {}
        for names, (send_sems, recv_sems, parts, lands, _) in pushes[l].items():
            parts, landed = _push_wait(send_sems, recv_sems, [0, 1], parts, lands, "scatter_chips", last_token,
                                       f"{names[0]}_grads_wait_{l}")
            for k, part, land in zip(names, parts, landed):
                sums[k] = _sum_partials(part, land, chip1)
        mine = [sums[k] for k in BIG]
        swaps[l] = _push_start(mine, [lax.empty(a.shape, a.dtype) for a in mine], "sibling", f"sums_start_{l}")
        if l + 1 < depth:
            finish(l + 1, mine[0])
    finish(0, swaps[0][4])
    outs["w_in"] = [jnp.transpose(r, (1, 2, 0)) for r in _adamw_w_in_t(
        w_in_t, m_in_t, v_in_t, [s[0] for s in w_in_sums], [s[1] for s in w_in_sums])]
    out_g, out_d, out_m, out_v = [{k: outs[k][i] for k in BIG} for i in range(4)]

    landed = _push_wait(small_push[0], small_push[1], [0], small_push[3], small_push[3], "gather_devices",
                        out_g["w_in"], "small_grads_wait")
    total = _sum_slabs(landed[0])
    small_g = dict(zip(SMALL + ("loss",), _unpack(total, full_shapes)))
    for k in COL_SHARDED_SMALL:
        n = w[k].shape[-1]
        small_g[k] = lax.dynamic_slice_in_dim(small_g[k], chip * n, n, axis=small_g[k].ndim - 1)
    local_shapes = [w[k].shape for k in SMALL]
    res = _adamw(_pack([w[k] for k in SMALL]), _pack([small_g[k] for k in SMALL]),
                 _pack([m[k] for k in SMALL]), _pack([v[k] for k in SMALL]))
    out_g.update({k: small_g[k] for k in SMALL})
    for dst, buf in zip((out_d, out_m, out_v), res):
        dst.update(zip(SMALL, _unpack(buf, local_shapes)))

    return (small_g["loss"].reshape(()), dx[None],
            *[out_g[k] for k in WEIGHTS], *[out_d[k] for k in WEIGHTS],
            *[out_m[k] for k in WEIGHTS], *[out_v[k] for k in WEIGHTS])
```

```python
import functools
import math

import jax
import jax.numpy as jnp
from jax import lax
from jax.experimental import pallas as pl
from jax.experimental.pallas import tpu as pltpu

F32 = jnp.float32
BF16 = jnp.bfloat16

N_META = 16
HEAD_DIM = 64
N_REC_BLOCKS = 8
CONV_WIDTH = 4
RG_C = 8.0
NORM_EPS = 1e-6
ADAM_LR = 0.001
ADAM_B1 = 0.9
ADAM_B2 = 0.999
ADAM_EPS = 1e-08
ADAM_WD = 0.01
ADAM_STEP = 10

LANES = 128
SUBLANES = 8
SEQ_TILE = 128
VMEM_CAP = 60 * 2**20
VMEM_SLACK = 6 * 2**20
NEG_BIG = -1e30
N_CHIPS = 4
N_DEV = 8
MESH = pl.DeviceIdType.MESH


def _nbytes(shape, dtype):
    return math.prod(shape) * jnp.dtype(dtype).itemsize


def _call(body, args, *, name, out_shape, grid=(), in_specs=None, out_specs=None, scratch_shapes=(),
          grid_spec=None, semantics=None, vmem_bytes=None, side_effects=None, hbm_results=True, **kw):
    cp = {}
    if semantics is not None:
        cp["dimension_semantics"] = semantics
    if vmem_bytes is not None:
        cp["vmem_limit_bytes"] = int(min(VMEM_CAP, vmem_bytes + VMEM_SLACK))
    if side_effects is not None:
        cp["has_side_effects"] = side_effects
    if grid_spec is not None:
        kw["grid_spec"] = grid_spec
    else:
        kw.update(grid=grid, in_specs=in_specs, out_specs=out_specs, scratch_shapes=scratch_shapes)
    if hbm_results:
        out_shape = jax.tree.map(
            lambda s: pltpu.HBM(s.shape, s.dtype) if isinstance(s, jax.ShapeDtypeStruct) else s, out_shape)
    fn = pl.pallas_call(
        body, name=name, out_shape=out_shape,
        compiler_params=pltpu.CompilerParams(**cp), **kw)
    return fn(*[_in_hbm(a) if jnp.issubdtype(getattr(a, "dtype", jnp.int32), jnp.floating) else a for a in args])


def _divisor_tile(n, unit, target):
    best = None
    for t in range(unit, min(n, target) + 1, unit):
        if n % t == 0:
            best = t
    return n if best is None else best


def _sigmoid(x):
    return 1.0 / (1.0 + jnp.exp(-x))


def _log1p_unit(e):
    series = e * (1.0 - e * (0.5 - e * (1.0 / 3.0)))
    return jnp.where(e < 1e-2, series, jnp.log(1.0 + e))


def _log_sigmoid(x):
    return jnp.minimum(x, 0.0) - _log1p_unit(jnp.exp(-jnp.abs(x)))


def _expm1_nonpos(x):
    small = x * (1.0 + x * (1.0 / 2 + x * (1.0 / 6 + x * (1.0 / 24 + x * (1.0 / 120 + x * (1.0 / 720))))))
    return jnp.where(x > -0.25, small, jnp.exp(x) - 1.0)


_GELU_K = math.sqrt(2.0 / math.pi)
_GELU_C = 0.044715


def _gelu_and_grad(y):
    th = jnp.tanh(_GELU_K * (y + _GELU_C * y * y * y))
    g = 0.5 * y * (1.0 + th)
    dg = 0.5 * (1.0 + th) + 0.5 * y * (1.0 - th * th) * _GELU_K * (1.0 + 3.0 * _GELU_C * y * y)
    return g, dg


def _rstd(x):
    return lax.rsqrt(jnp.mean(x * x, axis=-1, keepdims=True) + NORM_EPS)


def _rms_bwd(dz, x, g):
    rs = _rstd(x)
    xh = x * rs
    dgp = jnp.sum(dz * xh, axis=0, keepdims=True)
    dxh = dz * g
    dx = rs * (dxh - xh * jnp.mean(dxh * xh, axis=-1, keepdims=True))
    return dx, dgp


def _dot(a, b):
    return jnp.dot(a, b, preferred_element_type=F32)


def _dot_nt(a, b):
    return lax.dot_general(a, b, (((1,), (1,)), ((), ())), preferred_element_type=F32)


def _dot_tn(a, b):
    return lax.dot_general(a, b, (((0,), (0,)), ((), ())), preferred_element_type=F32)


def _full(shape):
    nd = len(shape)
    return pl.BlockSpec(shape, lambda *_: (0,) * nd)


def _rms_fwd(h, g):
    tp, d = h.shape
    tm = _divisor_tile(tp, 16, 544)

    def body(h_ref, g_ref, z_ref):
        x = h_ref[...]
        z_ref[...] = (x * _rstd(x) * g_ref[...]).astype(BF16)

    return _call(body, (h, g), name="rms_fwd", grid=(tp // tm,),
                 in_specs=[pl.BlockSpec((tm, d), lambda i: (i, 0)), _full((1, d))],
                 out_specs=pl.BlockSpec((tm, d), lambda i: (i, 0)),
                 out_shape=jax.ShapeDtypeStruct((tp, d), BF16), semantics=("parallel",))


def _proj(z, w_big_t, att_w):
    tp, d = z.shape
    nb = w_big_t.shape[0]
    tn = _divisor_tile(nb, LANES, 512)
    assert (3 * att_w) % tn == 0
    n_qkv = 3 * att_w // tn
    scale = 1.0 / math.sqrt(HEAD_DIM)

    def body(z_ref, w_ref, p_ref, qkv_ref):
        j = pl.program_id(0)
        acc = _dot_nt(z_ref[...], w_ref[...])
        p_ref[...] = acc

        @pl.when(j < n_qkv)
        def _():
            col = j * tn + lax.broadcasted_iota(jnp.int32, (1, tn), 1)
            qkv_ref[...] = (acc * jnp.where(col < att_w, scale, 1.0)).astype(BF16)

    vm = 2 * (_nbytes((tp, d), BF16) + _nbytes((d, tn), BF16) + _nbytes((tp, tn), F32) * 2)
    return _call(body, (z, w_big_t), name="proj", grid=(nb // tn,),
                 in_specs=[_full((tp, d)), pl.BlockSpec((tn, d), lambda j: (j, 0))],
                 out_specs=[pl.BlockSpec((tp, tn), lambda j: (0, j)),
                            pl.BlockSpec((tp, tn), lambda j: (0, jnp.minimum(j, n_qkv - 1)))],
                 out_shape=[jax.ShapeDtypeStruct((tp, nb), F32),
                            jax.ShapeDtypeStruct((tp, 3 * att_w), BF16)],
                 semantics=("arbitrary",), vmem_bytes=vm)


def _tile_cumsum(x, row, reverse=False):
    for s in (1, 2, 4):
        if reverse:
            x = x + jnp.where(row < SUBLANES - s, pltpu.roll(x, SUBLANES - s, 0), 0.0)
        else:
            x = x + jnp.where(row >= s, pltpu.roll(x, s, 0), 0.0)
    return x


def _fgate_fwd(proj, b_f_pad, nh):
    tp, nb = proj.shape
    fblk = nb // LANES - 1

    def body(f_ref, b_ref, c_ref, ct_ref):
        b = b_ref[...]
        row = lax.broadcasted_iota(jnp.int32, (SUBLANES, LANES), 0)

        def step(i, carry):
            r0 = pl.multiple_of(i * SUBLANES, SUBLANES)
            lf = _log_sigmoid(f_ref[pl.ds(r0, SUBLANES), :] + b)
            x = _tile_cumsum(lf, row) + carry
            c_ref[pl.ds(r0, SUBLANES), :] = x
            return x[SUBLANES - 1:SUBLANES, :]

        lax.fori_loop(0, tp // SUBLANES, step, jnp.zeros((1, LANES), F32))
        ct_ref[...] = c_ref[...].T[:nh, :]

    return _call(body, (proj, b_f_pad), name="fgate_fwd", grid=(1,),
                 in_specs=[pl.BlockSpec((tp, LANES), lambda i: (0, fblk)), _full((1, LANES))],
                 out_specs=[_full((tp, LANES)), _full((nh, tp))],
                 out_shape=[jax.ShapeDtypeStruct((tp, LANES), F32), jax.ShapeDtypeStruct((nh, tp), F32)],
                 semantics=("arbitrary",))


def _fgate_bwd(proj, b_f_pad, dc):
    tp, nb = proj.shape
    fblk = nb // LANES - 1

    def body(f_ref, b_ref, dc_ref, df_ref, db_ref, dc_s):
        b = b_ref[...]
        row = lax.broadcasted_iota(jnp.int32, (SUBLANES, LANES), 0)
        nt = tp // SUBLANES

        def step(i, carry):
            suffix, acc = carry
            r0 = pl.multiple_of((nt - 1 - i) * SUBLANES, SUBLANES)
            dlf = _tile_cumsum(dc_ref[pl.ds(r0, SUBLANES), :], row, reverse=True) + suffix
            df = dlf * _sigmoid(-(f_ref[pl.ds(r0, SUBLANES), :] + b))
            dc_s[pl.ds(r0, SUBLANES), :] = df
            return dlf[0:1, :], acc + df

        _, acc = lax.fori_loop(0, nt, step, (jnp.zeros((1, LANES), F32), jnp.zeros((SUBLANES, LANES), F32)))
        df_ref[...] = dc_s[...].astype(BF16)
        db_ref[...] = jnp.broadcast_to(jnp.sum(acc, axis=0, keepdims=True), (SUBLANES, LANES))

    return _call(body, (proj, b_f_pad, dc), name="fgate_bwd", grid=(1,),
                 in_specs=[pl.BlockSpec((tp, LANES), lambda i: (0, fblk)), _full((1, LANES)), _full((tp, LANES))],
                 out_specs=[_full((tp, LANES)), _full((SUBLANES, LANES))],
                 out_shape=[jax.ShapeDtypeStruct((tp, LANES), BF16),
                            jax.ShapeDtypeStruct((SUBLANES, LANES), F32)],
                 scratch_shapes=[pltpu.VMEM((tp, LANES), F32)], semantics=("arbitrary",))


ATT_BQ = 128


ATT_BUCKET = 3
ATT_HEADS = 4


def _for_bucket(i, nq, fn):
    for lo in range(0, nq, ATT_BUCKET):
        hi = min(lo + ATT_BUCKET, nq)
        spans = ([(0, lo * ATT_BQ, False)] if lo else []) + [(lo * ATT_BQ, hi * ATT_BQ, True)]
        pl.when(jnp.logical_and(i >= lo, i < hi))(functools.partial(fn, spans))


def _head_column(c_blk, h):
    lane = lax.broadcasted_iota(jnp.int32, c_blk.shape, 1)
    return jnp.sum(jnp.where(lane == h, c_blk, 0.0), axis=1, keepdims=True)


def _span_logits(q, k_ref, ct_ref, h, lo, i, span):
    k0, k1, needs_mask = span
    t = _dot_nt(q, k_ref[k0:k1, lo:lo + HEAD_DIM]) - ct_ref[pl.ds(h, 1), k0:k1]
    if needs_mask:
        rows = i * ATT_BQ + lax.broadcasted_iota(jnp.int32, (ATT_BQ, k1 - k0), 0)
        cols = k0 + lax.broadcasted_iota(jnp.int32, (ATT_BQ, k1 - k0), 1)
        t = jnp.where(cols <= rows, t, NEG_BIG)
    return t


def _attn_fwd(qkv, c, c_t, nh):
    tp = qkv.shape[0]
    att_w = nh * HEAD_DIM
    ng = nh // ATT_HEADS
    gw = ATT_HEADS * HEAD_DIM
    bq = ATT_BQ
    nq = tp // bq

    def body(q_ref, k_ref, v_ref, c_ref, ct_ref, o_ref, lse_ref):
        p = pl.program_id(0)
        i = pl.program_id(1)

        def compute(spans):
            outs, lses = [], []
            for hh in range(ATT_HEADS):
                lo = HEAD_DIM * hh
                h = ATT_HEADS * p + hh
                q = q_ref[:, lo:lo + HEAD_DIM]
                ts = [_span_logits(q, k_ref, ct_ref, h, lo, i, sp) for sp in spans]
                m = functools.reduce(jnp.maximum, [jnp.max(t, axis=1, keepdims=True) for t in ts])
                es = [jnp.exp(t - m) for t in ts]
                l = sum(jnp.sum(e, axis=1, keepdims=True) for e in es)
                o = sum(_dot(e.astype(BF16), v_ref[k0:k1, lo:lo + HEAD_DIM]) for e, (k0, k1, _) in zip(es, spans))
                outs.append(o / l)
                lses.append(jnp.broadcast_to(m + _head_column(c_ref[...], h) + jnp.log(l), (bq, HEAD_DIM)))
            o_ref[...] = jnp.concatenate(outs, axis=1)
            lse_ref[...] = jnp.concatenate(lses, axis=1)

        _for_bucket(i, nq, compute)

    blk = pl.BlockSpec((bq, gw), lambda p, i: (i, p))
    vm = 4 * _nbytes((tp, gw), BF16) + 8 * ATT_HEADS * _nbytes((bq, tp), F32)
    return _call(body, (qkv, qkv, qkv, c, c_t), name="attn_fwd", grid=(ng, nq),
                 in_specs=[blk,
                           pl.BlockSpec((tp, gw), lambda p, i: (0, ng + p)),
                           pl.BlockSpec((tp, gw), lambda p, i: (0, 2 * ng + p)),
                           pl.BlockSpec((bq, LANES), lambda p, i: (i, 0)), _full((nh, tp))],
                 out_specs=[blk, blk],
                 out_shape=[jax.ShapeDtypeStruct((tp, att_w), F32)] * 2,
                 semantics=("parallel", "parallel"), vmem_bytes=vm)


def _attn_bwd(qkv, c, c_t, lse_b, do, nh):
    tp = qkv.shape[0]
    att_w = nh * HEAD_DIM
    ng = nh // ATT_HEADS
    gw = ATT_HEADS * HEAD_DIM
    bq = ATT_BQ
    nq = tp // bq
    scale = 1.0 / math.sqrt(HEAD_DIM)

    def body(q_ref, k_ref, v_ref, c_ref, ct_ref, lse_ref, do_ref, dq_ref, dk_ref, dv_ref, dc_ref,
             dk_s, dv_s, dc_s, ck_s, kt_s, kh_s, vh_s):
        p = pl.program_id(0)
        i = pl.program_id(1)

        @pl.when(i == 0)
        def _():
            dk_s[...] = jnp.zeros_like(dk_s)
            dv_s[...] = jnp.zeros_like(dv_s)
            dc_s[...] = jnp.zeros_like(dc_s)
            kt_s[...] = k_ref[...].astype(F32).T.astype(BF16)
            for hh in range(ATT_HEADS):
                kh_s[hh] = k_ref[:, HEAD_DIM * hh:HEAD_DIM * (hh + 1)]
                vh_s[hh] = v_ref[:, HEAD_DIM * hh:HEAD_DIM * (hh + 1)]
                ck_s[hh] = jnp.broadcast_to(_head_column(c_ref[...], ATT_HEADS * p + hh), (tp, LANES))

        @pl.when(jnp.logical_and(i == 0, p == 0))
        def _():
            dc_ref[...] = jnp.zeros_like(dc_ref)

        def compute(spans):
            q0 = pl.multiple_of(i * bq, bq)
            lse_rows = lse_ref[...].T
            dq_t = []
            for hh in range(ATT_HEADS):
                lo = HEAD_DIM * hh
                h = ATT_HEADS * p + hh
                q = q_ref[:, lo:lo + HEAD_DIM]
                doutb = do_ref[:, lo:lo + HEAD_DIM].astype(BF16)
                col_term = ct_ref[pl.ds(h, 1), :] - lse_rows[lo:lo + 1, :]
                prs, dps = [], []
                for k0, k1, needs_mask in spans:
                    t = _dot_nt(kh_s[hh, k0:k1, :], q) + (col_term - ck_s[hh, k0:k1, :])
                    if needs_mask:
                        keys = k0 + lax.broadcasted_iota(jnp.int32, (k1 - k0, bq), 0)
                        t = jnp.where(keys <= q0 + lax.broadcasted_iota(jnp.int32, (k1 - k0, bq), 1), t, NEG_BIG)
                    prs.append(jnp.exp(t))
                    dps.append(_dot_nt(vh_s[hh, k0:k1, :], doutb))
                key_sum = sum(jnp.sum(pr * dp, axis=0, keepdims=True) for pr, dp in zip(prs, dps))
                dq_h = 0.0
                for (k0, k1, _), pr, dp in zip(spans, prs, dps):
                    ds = pr * (dp - key_sum)
                    dsb = ds.astype(BF16)
                    dq_h = dq_h + _dot(kt_s[lo:lo + HEAD_DIM, k0:k1], dsb)
                    dk_s[hh, k0:k1, :] += _dot(dsb, q)
                    dv_s[hh, k0:k1, :] += _dot(pr.astype(BF16), doutb)
                    dc_s[hh, k0:k1, :] += jnp.sum(ds, axis=1, keepdims=True)
                dq_t.append(dq_h)
            dq_ref[...] = (jnp.concatenate(dq_t, axis=0) * scale).T.astype(BF16)

        _for_bucket(i, nq, compute)

        @pl.when(i == nq - 1)
        def _():
            dk_ref[...] = jnp.concatenate([dk_s[hh] for hh in range(ATT_HEADS)], axis=1).astype(BF16)
            dv_ref[...] = jnp.concatenate([dv_s[hh] for hh in range(ATT_HEADS)], axis=1).astype(BF16)
            lane = lax.broadcasted_iota(jnp.int32, (tp, LANES), 1)
            dc = dc_ref[...]
            for hh in range(ATT_HEADS):
                dc = jnp.where(lane == ATT_HEADS * p + hh, -dc_s[hh], dc)
            dc_ref[...] = dc

    blk = pl.BlockSpec((bq, gw), lambda p, i: (i, p))
    col = pl.BlockSpec((tp, gw), lambda p, i: (0, p))
    vm = 7 * _nbytes((tp, gw), BF16) + 2 * _nbytes((tp, gw), F32) + 2 * ATT_HEADS * _nbytes((tp, LANES), F32) \
        + 4 * _nbytes((tp, LANES), F32) + 12 * ATT_HEADS * _nbytes((bq, tp), F32)
    return _call(body, (qkv, qkv, qkv, c, c_t, lse_b, do), name="attn_bwd", grid=(ng, nq),
                 in_specs=[blk,
                           pl.BlockSpec((tp, gw), lambda p, i: (0, ng + p)),
                           pl.BlockSpec((tp, gw), lambda p, i: (0, 2 * ng + p)),
                           _full((tp, LANES)), pl.BlockSpec((nh, bq), lambda p, i: (0, i)), blk, blk],
                 out_specs=[blk, col, col, _full((tp, LANES))],
                 out_shape=[jax.ShapeDtypeStruct((tp, att_w), BF16)] * 3 + [jax.ShapeDtypeStruct((tp, LANES), F32)],
                 scratch_shapes=[pltpu.VMEM((ATT_HEADS, tp, HEAD_DIM), F32), pltpu.VMEM((ATT_HEADS, tp, HEAD_DIM), F32),
                                 pltpu.VMEM((ATT_HEADS, tp, 1), F32), pltpu.VMEM((ATT_HEADS, tp, LANES), F32),
                                 pltpu.VMEM((gw, tp), BF16), pltpu.VMEM((ATT_HEADS, tp, HEAD_DIM), BF16),
                                 pltpu.VMEM((ATT_HEADS, tp, HEAD_DIM), BF16)],
                 semantics=("arbitrary", "arbitrary"), vmem_bytes=vm)


REC_ROWS = 128
HALO = SUBLANES


def _conv_taps(cat):
    taps = []
    for k in range(CONV_WIDTH):
        sh = CONV_WIDTH - 1 - k
        taps.append((pltpu.roll(cat, sh, 0) if sh else cat)[HALO:])
    return taps


def _rec_gates(xc, wa_ref, ba_ref, wx_ref, bx_ref, l_ref):
    xcb = xc.astype(BF16)
    r = _sigmoid(_dot(xcb, wa_ref[...]) + ba_ref[...])
    ig = _sigmoid(_dot(xcb, wx_ref[...]) + bx_ref[...])
    ls = _log_sigmoid(l_ref[...])
    log_a = RG_C * r * ls
    return xcb, r, ig, ls, log_a


def _rec_fwd(proj, xr_blk, yr_blk, rec_w, conv_w, conv_b, wa, ba, wx, bx, lru):
    tp = proj.shape[0]
    w = rec_w
    r_rows = REC_ROWS
    nc = tp // r_rows
    cpb = w // LANES

    def body(xr_ref, yr_ref, cw_ref, cb_ref, wa_ref, ba_ref, wx_ref, bx_ref, l_ref,
             hr_ref, rec_ref, prev_s, carry_s, a_s, u_s):
        i = pl.program_id(0)

        @pl.when(i == 0)
        def _():
            prev_s[...] = jnp.zeros_like(prev_s)
            carry_s[...] = jnp.zeros_like(carry_s)

        x = xr_ref[...]
        taps = _conv_taps(jnp.concatenate([prev_s[...], x], axis=0))
        prev_s[...] = x[r_rows - HALO:]
        xc = cb_ref[...]
        for k in range(CONV_WIDTH):
            xc = xc + cw_ref[k:k + 1, :] * taps[k]
        _, r, ig, ls, log_a = _rec_gates(xc, wa_ref, ba_ref, wx_ref, bx_ref, l_ref)
        a_s[...] = jnp.exp(log_a)
        u_s[...] = jnp.sqrt(-_expm1_nonpos(2.0 * log_a)) * ig * xc

        def tile(j, h):
            r0 = pl.multiple_of(j * SUBLANES, SUBLANES)
            at = a_s[pl.ds(r0, SUBLANES), :]
            ut = u_s[pl.ds(r0, SUBLANES), :]
            out = []
            for rr in range(SUBLANES):
                h = at[rr:rr + 1] * h + ut[rr:rr + 1]
                out.append(h)
            hr_ref[pl.ds(r0, SUBLANES), :] = jnp.concatenate(out, axis=0)
            return h

        carry_s[0:1, :] = lax.fori_loop(0, r_rows // SUBLANES, tile, carry_s[0:1, :])
        g, _ = _gelu_and_grad(yr_ref[...])
        rec_ref[...] = hr_ref[...] * g

    blk = pl.BlockSpec((r_rows, w), lambda i: (i, 0))
    vm = 16 * _nbytes((r_rows, w), F32) + 4 * _nbytes((w, w), BF16)
    return _call(body, (proj, proj, conv_w, conv_b, wa, ba, wx, bx, lru), name="rec_fwd", grid=(nc,),
                 in_specs=[pl.BlockSpec((r_rows, w), lambda i: (i, xr_blk)),
                           pl.BlockSpec((r_rows, w), lambda i: (i, yr_blk)),
                           _full((CONV_WIDTH, w)), _full((1, w)), _full((w, w)), _full((1, w)),
                           _full((w, w)), _full((1, w)), _full((1, w))],
                 out_specs=[blk, blk],
                 out_shape=[jax.ShapeDtypeStruct((tp, w), F32)] * 2,
                 scratch_shapes=[pltpu.VMEM((HALO, w), F32), pltpu.VMEM((SUBLANES, w), F32),
                                 pltpu.VMEM((r_rows, w), F32), pltpu.VMEM((r_rows, w), F32)],
                 semantics=("arbitrary",), vmem_bytes=vm)


def _rec_bwd(proj, xr_blk, yr_blk, rec_w, hr, drec, conv_w, conv_b, wa, ba, wx, bx, lru):
    tp = proj.shape[0]
    w = rec_w
    r_rows = REC_ROWS
    nc = tp // r_rows
    hpc = r_rows // HALO

    def body(xr_ref, xh_ref, yr_ref, hr_ref, hh_ref, drec_ref, cw_ref, cb_ref, wa_ref, ba_ref, wx_ref, bx_ref,
             l_ref, dxr_ref, dyr_ref, dwa_ref, dwx_ref, small_ref, lam_s, a_s, dhr_s, carry_s, next_s):
        i = pl.program_id(0)
        first = (nc - 1 - i) == 0

        @pl.when(i == 0)
        def _():
            carry_s[...] = jnp.zeros_like(carry_s)
            next_s[...] = jnp.zeros_like(next_s)
            dwa_ref[...] = jnp.zeros_like(dwa_ref)
            dwx_ref[...] = jnp.zeros_like(dwx_ref)
            small_ref[...] = jnp.zeros_like(small_ref)

        x = xr_ref[...]
        xprev = jnp.where(first, 0.0, xh_ref[...])
        taps = _conv_taps(jnp.concatenate([xprev, x], axis=0))
        xc = cb_ref[...]
        for k in range(CONV_WIDTH):
            xc = xc + cw_ref[k:k + 1, :] * taps[k]
        xcb, r, ig, ls, log_a = _rec_gates(xc, wa_ref, ba_ref, wx_ref, bx_ref, l_ref)
        a = jnp.exp(log_a)
        a2 = jnp.exp(2.0 * log_a)
        mult = jnp.sqrt(-_expm1_nonpos(2.0 * log_a))
        g, dg = _gelu_and_grad(yr_ref[...])
        hr_v = hr_ref[...]
        drec_v = drec_ref[...]
        dhr_s[...] = drec_v * g
        dyr_ref[...] = (drec_v * hr_v * dg).astype(BF16)
        a_s[...] = a

        def tile(jj, carry):
            r0 = pl.multiple_of((r_rows // SUBLANES - 1 - jj) * SUBLANES, SUBLANES)
            at = a_s[pl.ds(r0, SUBLANES), :]
            dt = dhr_s[pl.ds(r0, SUBLANES), :]
            out = [None] * SUBLANES
            for rr in range(SUBLANES - 1, -1, -1):
                lam = dt[rr:rr + 1] + carry
                out[rr] = lam
                carry = at[rr:rr + 1] * lam
            lam_s[pl.ds(r0, SUBLANES), :] = jnp.concatenate(out, axis=0)
            return carry

        carry_s[0:1, :] = lax.fori_loop(0, r_rows // SUBLANES, tile, carry_s[0:1, :])
        lam = lam_s[...]
        hprev = jnp.where(first, 0.0, hh_ref[...])
        hr_prev = pltpu.roll(jnp.concatenate([hprev, hr_v], axis=0), 1, 0)[HALO:]
        da = lam * hr_prev
        dxc = lam * mult * ig
        di = lam * mult * xc
        dmult = lam * ig * xc
        dlog_a = da * a - dmult * a2 / mult
        dr = dlog_a * (RG_C * ls)
        dls = jnp.sum(dlog_a * (RG_C * r), axis=0, keepdims=True)
        dga = dr * r * (1.0 - r)
        dgx = di * ig * (1.0 - ig)
        dgab = dga.astype(BF16)
        dgxb = dgx.astype(BF16)
        dxc = dxc + _dot_nt(dgab, wa_ref[...]) + _dot_nt(dgxb, wx_ref[...])
        dwa_ref[...] += _dot_tn(xcb, dgab)
        dwx_ref[...] += _dot_tn(xcb, dgxb)
        cat = jnp.concatenate([dxc, next_s[...]], axis=0)
        next_s[...] = dxc[0:HALO]
        dxr = cw_ref[CONV_WIDTH - 1:CONV_WIDTH, :] * dxc
        for k in range(CONV_WIDTH - 1):
            sh = CONV_WIDTH - 1 - k
            dxr = dxr + cw_ref[k:k + 1, :] * pltpu.roll(cat, r_rows + HALO - sh, 0)[:r_rows]
        dxr_ref[...] = dxr.astype(BF16)
        rows = [jnp.sum(dxc * taps[k], axis=0, keepdims=True) for k in range(CONV_WIDTH)]
        rows += [jnp.sum(dxc, axis=0, keepdims=True), jnp.sum(dga, axis=0, keepdims=True),
                 jnp.sum(dgx, axis=0, keepdims=True), dls * _sigmoid(-l_ref[...])]
        small_ref[...] += jnp.concatenate(rows, axis=0)

    def rev(i):
        return nc - 1 - i

    def halo(i):
        return jnp.maximum(rev(i) * hpc - 1, 0)

    blk = pl.BlockSpec((r_rows, w), lambda i: (rev(i), 0))
    vm = 40 * _nbytes((r_rows, w), F32) + 6 * _nbytes((w, w), F32)
    return _call(body, (proj, proj, proj, hr, hr, drec, conv_w, conv_b, wa, ba, wx, bx, lru),
                 name="rec_bwd", grid=(nc,),
                 in_specs=[pl.BlockSpec((r_rows, w), lambda i: (rev(i), xr_blk)),
                           pl.BlockSpec((HALO, w), lambda i: (halo(i), xr_blk)),
                           pl.BlockSpec((r_rows, w), lambda i: (rev(i), yr_blk)),
                           blk,
                           pl.BlockSpec((HALO, w), lambda i: (halo(i), 0)),
                           blk,
                           _full((CONV_WIDTH, w)), _full((1, w)), _full((w, w)), _full((1, w)),
                           _full((w, w)), _full((1, w)), _full((1, w))],
                 out_specs=[blk, blk, _full((w, w)), _full((w, w)), _full((SUBLANES, w))],
                 out_shape=[jax.ShapeDtypeStruct((tp, w), BF16)] * 2
                 + [jax.ShapeDtypeStruct((w, w), F32)] * 2 + [jax.ShapeDtypeStruct((SUBLANES, w), F32)],
                 scratch_shapes=[pltpu.VMEM((r_rows, w), F32)] * 3
                 + [pltpu.VMEM((SUBLANES, w), F32), pltpu.VMEM((HALO, w), F32)],
                 semantics=("arbitrary",), vmem_bytes=vm)


ROW_TARGET = 544


def _mixer_out(attn, rec, g_a, g_r, w_out, h, g_next):
    tp, d = h.shape
    aw, rw = attn.shape[1], rec.shape[1]
    kc = d // N_CHIPS
    tm = _divisor_tile(tp, 16, ROW_TARGET)

    def body(a_ref, r_ref, ga_ref, gr_ref, w_ref, h_ref, gn_ref, h1_ref, z_ref, mix_ref):
        a = a_ref[...]
        r = r_ref[...]
        mix = jnp.concatenate([a * _rstd(a) * ga_ref[...], r * _rstd(r) * gr_ref[...]], axis=1).astype(BF16)
        mix_ref[...] = mix
        h1 = h_ref[...]
        for j in range(N_CHIPS):
            h1 = h1 + _dot(mix[:, j * kc:(j + 1) * kc], w_ref[j])
        h1_ref[...] = h1
        z_ref[...] = (h1 * _rstd(h1) * gn_ref[...]).astype(BF16)

    row = lambda wd: pl.BlockSpec((tm, wd), lambda i: (i, 0))
    vm = 2 * _nbytes((d, d), BF16) + 12 * _nbytes((tm, d), F32)
    return _call(body, (attn, rec, g_a, g_r, w_out, h, g_next), name="mixer_out", grid=(tp // tm,),
                 in_specs=[row(aw), row(rw), _full((1, aw)), _full((1, rw)), _full(w_out.shape), row(d),
                           _full((1, d))],
                 out_specs=[row(d), row(d), row(d)],
                 out_shape=[jax.ShapeDtypeStruct((tp, d), F32), jax.ShapeDtypeStruct((tp, d), BF16),
                            jax.ShapeDtypeStruct((tp, d), BF16)],
                 semantics=("parallel",), vmem_bytes=vm)


def _mixer_bwd(dh_b, w_out, attn, rec, g_a, g_r):
    tp, d = dh_b.shape
    aw, rw = attn.shape[1], rec.shape[1]
    tm = _divisor_tile(tp, 16, ROW_TARGET)

    def body(dh_ref, w_ref, a_ref, r_ref, ga_ref, gr_ref, da_ref, dr_ref, dg_ref):
        @pl.when(pl.program_id(0) == 0)
        def _():
            dg_ref[...] = jnp.zeros_like(dg_ref)

        dh = dh_ref[...]
        dmix = jnp.concatenate([_dot_nt(dh, w_ref[j]) for j in range(N_CHIPS)], axis=1)
        da, dga = _rms_bwd(dmix[:, :aw], a_ref[...], ga_ref[...])
        dr, dgr = _rms_bwd(dmix[:, aw:], r_ref[...], gr_ref[...])
        da_ref[...] = da
        dr_ref[...] = dr
        dg_ref[...] += jnp.broadcast_to(jnp.concatenate([dga, dgr], axis=1), (SUBLANES, d))

    row = lambda wd: pl.BlockSpec((tm, wd), lambda i: (i, 0))
    vm = 2 * _nbytes((d, d), BF16) + 12 * _nbytes((tm, d), F32)
    return _call(body, (dh_b, w_out, attn, rec, g_a, g_r), name="mixer_bwd", grid=(tp // tm,),
                 in_specs=[row(d), _full(w_out.shape), row(aw), row(rw), _full((1, aw)), _full((1, rw))],
                 out_specs=[row(aw), row(rw), _full((SUBLANES, d))],
                 out_shape=[jax.ShapeDtypeStruct((tp, aw), F32), jax.ShapeDtypeStruct((tp, rw), F32),
                            jax.ShapeDtypeStruct((SUBLANES, d), F32)],
                 semantics=("arbitrary",), vmem_bytes=vm)


def _mlp_up(z, w_up):
    tp, d = z.shape
    fc = w_up.shape[2]
    ff = N_CHIPS * fc
    tn = _divisor_tile(fc, LANES, 512)
    per = fc // tn

    def body(z_ref, w_ref, act_ref, up_ref):
        up = _dot(z_ref[...], w_ref[...])
        r = jnp.maximum(up, 0.0)
        act_ref[...] = (r * r).astype(BF16)
        up_ref[...] = up.astype(BF16)

    col = pl.BlockSpec((tp, tn), lambda j: (0, j))
    vm = 2 * _nbytes((tp, d), BF16) + 2 * _nbytes((d, tn), BF16) + 8 * _nbytes((tp, tn), F32)
    return _call(body, (z, w_up), name="mlp_up", grid=(ff // tn,),
                 in_specs=[_full((tp, d)), pl.BlockSpec((None, d, tn), lambda j: (j // per, 0, j % per))],
                 out_specs=[col, col],
                 out_shape=[jax.ShapeDtypeStruct((tp, ff), BF16)] * 2,
                 semantics=("parallel",), vmem_bytes=vm)


def _mlp_down(act, w_down, h, g_next):
    tp, d = h.shape
    ff = act.shape[1]
    fc = ff // N_CHIPS
    tm = _divisor_tile(tp, 16, ROW_TARGET)

    def body(a_ref, w_ref, h_ref, gn_ref, h2_ref, z_ref):
        h2 = h_ref[...]
        for j in range(N_CHIPS):
            h2 = h2 + _dot(a_ref[:, j * fc:(j + 1) * fc], w_ref[j])
        h2_ref[...] = h2
        z_ref[...] = (h2 * _rstd(h2) * gn_ref[...]).astype(BF16)

    row = lambda wd: pl.BlockSpec((tm, wd), lambda i: (i, 0))
    vm = 2 * _nbytes((ff, d), BF16) + 2 * _nbytes((tm, ff), BF16) + 10 * _nbytes((tm, d), F32)
    return _call(body, (act, w_down, h, g_next), name="mlp_down", grid=(tp // tm,),
                 in_specs=[row(ff), _full(w_down.shape), row(d), _full((1, d))],
                 out_specs=[row(d), row(d)],
                 out_shape=[jax.ShapeDtypeStruct((tp, d), F32), jax.ShapeDtypeStruct((tp, d), BF16)],
                 semantics=("parallel",), vmem_bytes=vm)


def _loss_bwd(h, g, target, n_real):
    tp, d = h.shape
    tm = _divisor_tile(tp, 16, ROW_TARGET)

    def body(h_ref, g_ref, t_ref, dh_ref, dhb_ref, dg_ref, loss_ref):
        i = pl.program_id(0)

        @pl.when(i == 0)
        def _():
            dg_ref[...] = jnp.zeros_like(dg_ref)
            loss_ref[...] = jnp.zeros_like(loss_ref)

        x = h_ref[...]
        gv = g_ref[...]
        rowi = i * tm + lax.broadcasted_iota(jnp.int32, (tm, 1), 0)
        real = jnp.logical_and(rowi >= N_META, rowi < N_META + n_real)
        err = jnp.where(real, x * _rstd(x) * gv - t_ref[...], 0.0)
        loss_ref[...] += 0.5 * jnp.sum(jnp.mean(err * err, axis=-1, keepdims=True))
        dx, dgp = _rms_bwd(err * (1.0 / d), x, gv)
        dh_ref[...] = dx
        dhb_ref[...] = dx.astype(BF16)
        dg_ref[...] += jnp.broadcast_to(dgp, (SUBLANES, d))

    row = pl.BlockSpec((tm, d), lambda i: (i, 0))
    return _call(body, (h, g, target), name="loss_bwd", grid=(tp // tm,),
                 in_specs=[row, _full((1, d)), row],
                 out_specs=[row, row, _full((SUBLANES, d)), _full((SUBLANES, LANES))],
                 out_shape=[jax.ShapeDtypeStruct((tp, d), F32), jax.ShapeDtypeStruct((tp, d), BF16),
                            jax.ShapeDtypeStruct((SUBLANES, d), F32), jax.ShapeDtypeStruct((SUBLANES, LANES), F32)],
                 semantics=("arbitrary",), vmem_bytes=16 * _nbytes((tm, d), F32))


def _mlp_dup(dh_b, w_down, up):
    tp, d = dh_b.shape
    fc = w_down.shape[1]
    ff = N_CHIPS * fc
    tn = _divisor_tile(fc, LANES, 512)
    per = fc // tn

    def body(dh_ref, w_ref, up_ref, dup_ref):
        dact = _dot_nt(dh_ref[...], w_ref[...])
        dup_ref[...] = (dact * (2.0 * jnp.maximum(up_ref[...].astype(F32), 0.0))).astype(BF16)

    col = pl.BlockSpec((tp, tn), lambda j: (0, j))
    vm = 2 * _nbytes((tp, d), BF16) + 2 * _nbytes((tn, d), BF16) + 8 * _nbytes((tp, tn), F32)
    return _call(body, (dh_b, w_down, up), name="mlp_dup", grid=(ff // tn,),
                 in_specs=[_full((tp, d)), pl.BlockSpec((None, tn, d), lambda j: (j // per, j % per, 0)),
                           col],
                 out_specs=col, out_shape=jax.ShapeDtypeStruct((tp, ff), BF16),
                 semantics=("parallel",), vmem_bytes=vm)


def _grad_w(a, b, cols_to_chips=False):
    tp, k = a.shape
    n = b.shape[1]
    tk = _divisor_tile(k, LANES, 1024)
    nc = n // N_CHIPS if cols_to_chips else n
    tn = _divisor_tile(nc, LANES, 512)
    per = nc // tn

    def body(a_ref, b_ref, o_ref):
        o_ref[...] = _dot_tn(a_ref[...], b_ref[...]).astype(BF16)

    if cols_to_chips:
        out_spec = pl.BlockSpec((None, tk, tn), lambda i, j: (j // per, i, j % per))
        out_shape = jax.ShapeDtypeStruct((N_CHIPS, k, nc), BF16)
    else:
        out_spec = pl.BlockSpec((tk, tn), lambda i, j: (i, j))
        out_shape = jax.ShapeDtypeStruct((k, n), BF16)
    vm = 2 * _nbytes((tp, tk), BF16) + 2 * _nbytes((tp, tn), BF16) + 6 * _nbytes((tk, tn), F32) \
        + 2 * _nbytes((tp, tk), F32)
    return _call(body, (a, b), name="grad_w", grid=(k // tk, n // tn),
                 in_specs=[pl.BlockSpec((tp, tk), lambda i, j: (0, i)), pl.BlockSpec((tp, tn), lambda i, j: (0, j))],
                 out_specs=out_spec, out_shape=out_shape,
                 semantics=("parallel", "parallel"), vmem_bytes=vm)


def _dx_norm_bwd(pieces, w, w_spec, w_piece, h, g, dres, dot=_dot_nt):
    tp, d = h.shape
    tm = _divisor_tile(tp, 16, ROW_TARGET)
    n = len(pieces)

    def body(*refs):
        dy_refs = refs[:n]
        w_ref, h_ref, g_ref, dres_ref, dh_ref, dhb_ref, dg_ref = refs[n:]

        @pl.when(pl.program_id(0) == 0)
        def _():
            dg_ref[...] = jnp.zeros_like(dg_ref)

        dz = dot(dy_refs[0][...], w_piece(w_ref, 0))
        for i in range(1, n):
            dz = dz + dot(dy_refs[i][...], w_piece(w_ref, i))
        dx, dgp = _rms_bwd(dz, h_ref[...], g_ref[...])
        dh = dres_ref[...] + dx
        dh_ref[...] = dh
        dhb_ref[...] = dh.astype(BF16)
        dg_ref[...] += jnp.broadcast_to(dgp, (SUBLANES, d))

    row = lambda wd: pl.BlockSpec((tm, wd), lambda i: (i, 0))
    kk = sum(wd for _, _, wd in pieces)
    vm = 2 * _nbytes((d, kk), BF16) + 2 * _nbytes((tm, kk), BF16) + 14 * _nbytes((tm, d), F32)
    piece_specs = [pl.BlockSpec((tm, wd), functools.partial(lambda i, cb: (i, cb), cb=cb)) for _, cb, wd in pieces]
    return _call(body, tuple(a for a, _, _ in pieces) + (w, h, g, dres), name="dx_norm_bwd", grid=(tp // tm,),
                 in_specs=piece_specs + [w_spec, row(d), _full((1, d)), row(d)],
                 out_specs=[row(d), row(d), _full((SUBLANES, d))],
                 out_shape=[jax.ShapeDtypeStruct((tp, d), F32), jax.ShapeDtypeStruct((tp, d), BF16),
                            jax.ShapeDtypeStruct((SUBLANES, d), F32)],
                 semantics=("arbitrary",), vmem_bytes=vm)


def _block_diag(wg):
    nb, b, _ = wg.shape
    eye = jnp.eye(nb, dtype=wg.dtype)
    return (eye[:, None, :, None] * wg[:, :, None, :]).reshape(nb * b, nb * b)


def _diag_blocks(dense, nb):
    b = dense.shape[0] // nb
    d4 = dense.reshape(nb, b, nb, b)
    return jnp.stack([d4[i, :, i, :] for i in range(nb)])


def _row(v):
    return v.reshape(1, -1)


def _forward_layer(l, h, z, p, fetch, g_next):
    d = h.shape[1]
    att_w = d // 2
    rec_w = d - att_w
    nh = att_w // HEAD_DIM
    xr_blk = 3 * att_w // rec_w
    wa_d = _block_diag(p["w_gate_a"][l]).astype(BF16)
    wx_d = _block_diag(p["w_gate_x"][l]).astype(BF16)
    b_f_pad = jnp.zeros((1, LANES), F32).at[0, :nh].set(p["b_f"][l])
    w_in_t = fetch("w_in", h)
    big = dict(w_in_big=_pack_w_in_t(w_in_t.reshape(-1, d), att_w, nh))
    proj, qkv = _proj(z, big["w_in_big"], att_w)
    c, c_t = _fgate_fwd(proj, b_f_pad, nh)
    attn, lse_b = _attn_fwd(qkv, c, c_t, nh)
    hr, rec = _rec_fwd(proj, xr_blk, xr_blk + 1, rec_w, p["conv_w"][l], _row(p["conv_b"][l]), wa_d,
                       _row(p["b_gate_a"][l]), wx_d, _row(p["b_gate_x"][l]), _row(p["lru_L"][l]))
    big["w_out"] = fetch("w_out", rec)
    h1, z2, mix = _mixer_out(attn, rec, _row(p["attn_out_g"][l]), _row(p["rec_out_g"][l]),
                             big["w_out"], h, _row(p["mlp_norm_g"][l]))
    big["w_up"] = fetch("w_up", h1)
    act, up = _mlp_up(z2, big["w_up"])
    big["w_down"] = fetch("w_down", act)
    h2, z_next = _mlp_down(act, big["w_down"], h1, _row(g_next))
    saved = dict(h0=h, z1=z, proj=proj, qkv=qkv, c=c, c_t=c_t, attn=attn, lse_b=lse_b, hr=hr, rec=rec, h1=h1,
                 z2=z2, mix=mix, act=act, up=up, wa_d=wa_d, wx_d=wx_d, b_f_pad=b_f_pad, big=big)
    return h2, z_next, saved


def _backward_mlp(l, dh, dh_b, sv, p, tok):
    w_up, w_down = sv["big"]["w_up"], sv["big"]["w_down"]
    fc = w_up.shape[2]
    dup = _mlp_dup(dh_b, w_down, sv["up"])
    g_down = _grad_w(sv["act"], dh_b)
    g_up = _grad_w(sv["z2"], dup, cols_to_chips=True)
    dh, dh_b, dg2 = _dx_norm_bwd([(dup, j, fc) for j in range(N_CHIPS)], w_up, _full(w_up.shape),
                                 lambda w_ref, j: w_ref[j], sv["h1"], _row(p["mlp_norm_g"][l] + tok), dh)
    big = dict(w_down=g_down.reshape((N_CHIPS, -1) + g_down.shape[1:]), w_up=g_up)
    return dh, dh_b, big, dict(mlp_norm_g=dg2[0])


def _backward_mixer(l, dh, dh_b, sv, p, tok):
    d = dh.shape[1]
    att_w = d // 2
    rec_w = d - att_w
    nh = att_w // HEAD_DIM
    xr_blk = 3 * att_w // rec_w
    small = {}
    g_out = _grad_w(sv["mix"], dh_b)
    dattn, drec, dg_mix = _mixer_bwd(dh_b, sv["big"]["w_out"], sv["attn"], sv["rec"],
                                     _row(p["attn_out_g"][l] + tok), _row(p["rec_out_g"][l]))
    small["attn_out_g"] = dg_mix[0, :att_w]
    small["rec_out_g"] = dg_mix[0, att_w:]
    dxr, dyr, dwa, dwx, sm = _rec_bwd(
        sv["proj"], xr_blk, xr_blk + 1, rec_w, sv["hr"], drec, p["conv_w"][l], _row(p["conv_b"][l]), sv["wa_d"],
        _row(p["b_gate_a"][l]), sv["wx_d"], _row(p["b_gate_x"][l]), _row(p["lru_L"][l]))
    small.update(conv_w=sm[:CONV_WIDTH], conv_b=sm[4], b_gate_a=sm[5], b_gate_x=sm[6], lru_L=sm[7],
                 w_gate_a=_diag_blocks(dwa, N_REC_BLOCKS), w_gate_x=_diag_blocks(dwx, N_REC_BLOCKS))
    dq, dk, dv, dc = _attn_bwd(sv["qkv"], sv["c"], sv["c_t"], sv["lse_b"], dattn, nh)
    df, db_f = _fgate_bwd(sv["proj"], sv["b_f_pad"], dc)
    small["b_f"] = db_f[0, :nh]
    pieces = [dq, dk, dv, dxr, dyr, df]
    offs = [0, att_w, 2 * att_w, 3 * att_w, 3 * att_w + rec_w, 3 * att_w + 2 * rec_w]
    gq, gk, gv, gxr, gyr, gf = [_grad_w(pc, sv["z1"]) for pc in pieces]
    g_in_t = jnp.concatenate([gq, gk, gv, gf[:nh], gxr, gyr], axis=0)
    w_big = sv["big"]["w_in_big"]
    widths = [pc.shape[1] for pc in pieces]
    dh, dh_b, dg1 = _dx_norm_bwd(
        [(pc, 0, wd) for pc, wd in zip(pieces, widths)], w_big, _full(w_big.shape),
        lambda w_ref, i: w_ref[offs[i]:offs[i] + widths[i], :], sv["h0"], _row(p["attn_norm_g"][l]), dh, dot=_dot)
    small["attn_norm_g"] = dg1[0]
    big = dict(w_in=g_in_t.reshape(N_CHIPS, -1, d), w_out=g_out.reshape((N_CHIPS, -1) + g_out.shape[1:]))
    return dh, dh_b, big, small


def _pack_w_in_t(w_in_t, att_w, nh):
    qkv = w_in_t[:3 * att_w]
    f = w_in_t[3 * att_w:3 * att_w + nh]
    xy = w_in_t[3 * att_w + nh:]
    return jnp.concatenate([qkv, xy, f, jnp.zeros((LANES - nh, w_in_t.shape[1]), w_in_t.dtype)], axis=0)


ANY = pl.BlockSpec(memory_space=pl.ANY)


def _coords():
    return lax.axis_index("x"), lax.axis_index("y"), lax.axis_index("c")


def _other_chips(x, y):
    return [(1 - x, y), (x, 1 - y), (1 - x, 1 - y)]


def _remote(src, dst, send_sems, recv_sems, k, to):
    return pltpu.make_async_remote_copy(src_ref=src, dst_ref=dst, send_sem=send_sems.at[k],
                                        recv_sem=recv_sems.at[k], device_id=to, device_id_type=MESH)


def _all_gather_chips(shards):
    n = len(shards)
    per = 6

    def body(*refs):
        ins, outs = refs[:n], refs[n:2 * n]
        send_sems, recv_sems, local_sems = refs[2 * n:]
        x, y, c = _coords()
        me = 2 * x + y
        sibling = (x, y, 1 - c)
        chips = _other_chips(x, y)
        local = [pltpu.make_async_copy(ins[t], outs[t].at[me], local_sems.at[t]) for t in range(n)]
        for cp in local:
            cp.start()
        sends = []
        for t in range(n):
            for j, (px, py) in enumerate(chips):
                cp = _remote(ins[t].at[c], outs[t].at[me, c], send_sems, recv_sems, per * t + j, (px, py, c))
                cp.start()
                sends.append(cp)
        for t in range(n):
            for j, (px, py) in enumerate(chips):
                landed = outs[t].at[2 * px + py, c]
                _remote(landed, landed, send_sems, recv_sems, per * t + j, (px, py, c)).wait_recv()
                cp = _remote(landed, landed, send_sems, recv_sems, per * t + 3 + j, sibling)
                cp.start()
                sends.append(cp)
        for t in range(n):
            for j, (px, py) in enumerate(chips):
                passed = outs[t].at[2 * px + py, 1 - c]
                _remote(passed, passed, send_sems, recv_sems, per * t + 3 + j, sibling).wait_recv()
        for cp in sends:
            cp.wait_send()
        for cp in local:
            cp.wait()

    return _call(body, tuple(shards), name="all_gather_chips",
                 in_specs=[ANY] * n, out_specs=[ANY] * n,
                 out_shape=[jax.ShapeDtypeStruct((N_CHIPS,) + s.shape, s.dtype) for s in shards],
                 scratch_shapes=[pltpu.SemaphoreType.DMA((per * n,)), pltpu.SemaphoreType.DMA((per * n,)),
                                 pltpu.SemaphoreType.DMA((n,))])


HBM = pl.BlockSpec(memory_space=pltpu.HBM)
SEM = pl.BlockSpec(memory_space=pltpu.SEMAPHORE)
DATAFLOW = pltpu.SideEffectType.DATAFLOW_SIDE_EFFECTING


def _in_hbm(a):
    return pltpu.with_memory_space_constraint(a, pltpu.HBM)


PUSH_ARRIVALS = {"gather_chips": N_CHIPS - 1, "scatter_chips": N_CHIPS - 1, "sibling": 1, "gather_devices": N_DEV - 1}


def _push_copies(mode, src, land, send_sems, recv_sems, t):
    x, y, c = _coords()
    chip = 2 * x + y
    if mode == "gather_chips":
        return [_remote(src.at[chip], land.at[chip], send_sems, recv_sems, t, (px, py, c))
                for px, py in _other_chips(x, y)]
    if mode == "scatter_chips":
        return [_remote(src.at[2 * px + py], land.at[chip], send_sems, recv_sems, t, (px, py, c))
                for px, py in _other_chips(x, y)]
    if mode == "sibling":
        return [_remote(src, land, send_sems, recv_sems, t, (x, y, 1 - c))]
    dev = 4 * x + 2 * y + c
    return [_remote(src.at[dev], land.at[dev], send_sems, recv_sems, t, (x ^ (k >> 2), y ^ ((k >> 1) & 1), c ^ (k & 1)))
            for k in range(1, N_DEV)]


def _push_start(srcs, lands, mode, name):
    n = len(srcs)
    same = all(s is ld for s, ld in zip(srcs, lands))
    n_in = n if same else 2 * n

    def body(*refs):
        src_refs = refs[:n]
        land_refs = src_refs if same else refs[n:2 * n]
        send_sems, recv_sems = refs[n_in], refs[n_in + 1]
        token = refs[-1]
        for t in range(n):
            for cp in _push_copies(mode, src_refs[t], land_refs[t], send_sems, recv_sems, t):
                cp.start()
        token[...] = jnp.zeros_like(token)

    operands = tuple(srcs) if same else tuple(srcs) + tuple(lands)
    res = _call(
        body, [_in_hbm(a) for a in operands], name=name,
        out_shape=(pltpu.SemaphoreType.DMA((n,)), pltpu.SemaphoreType.DMA((n,)))
        + tuple(pltpu.HBM(a.shape, a.dtype) for a in operands) + (jax.ShapeDtypeStruct((SUBLANES, LANES), F32),),
        in_specs=[HBM] * n_in, out_specs=(SEM, SEM) + (HBM,) * n_in + (pl.BlockSpec(memory_space=pltpu.VMEM),),
        input_output_aliases={i: 2 + i for i in range(n_in)}, side_effects=DATAFLOW, hbm_results=False)
    send_sems, recv_sems, token = res[0], res[1], res[-1]
    srcs_thru = res[2:2 + n]
    lands_thru = srcs_thru if same else res[2 + n:2 + 2 * n]
    return send_sems, recv_sems, srcs_thru, lands_thru, token


def _push_wait(send_sems, recv_sems, ids, srcs, lands, mode, after, name):
    n = len(lands)
    same = all(s is ld for s, ld in zip(srcs, lands))
    n_in = n if same else 2 * n

    def body(*refs):
        land_refs = refs[:n] if same else refs[n:2 * n]
        send_sems, recv_sems = refs[n_in], refs[n_in + 1]
        x, y, c = _coords()
        for t in range(n):
            moved = land_refs[t] if mode == "sibling" else land_refs[t].at[pl.ds(0, PUSH_ARRIVALS[mode])]
            arrivals = _remote(moved, moved, send_sems, recv_sems, ids[t], (x, y, c))
            arrivals.wait_send()
            arrivals.wait_recv()

    operands = tuple(lands) if same else tuple(srcs) + tuple(lands)
    res = _call(
        body, operands + (send_sems, recv_sems, after), name=name,
        out_shape=tuple(pltpu.HBM(a.shape, a.dtype) for a in operands),
        in_specs=[HBM] * n_in + [SEM, SEM, ANY], out_specs=(HBM,) * n_in,
        input_output_aliases={i: i for i in range(n_in)}, side_effects=DATAFLOW)
    return list(res) if same else (list(res[:n]), list(res[n:]))


def _sum_partials(part, landed, chip):
    _, rows, cols = part.shape
    br = _divisor_tile(rows, 16, ELEM_ROWS)

    def body(chip_ref, own_ref, a_ref, b_ref, c_ref, o_ref):
        o_ref[...] = ((own_ref[...].astype(F32) + a_ref[...].astype(F32)) + b_ref[...].astype(F32)) \
            + c_ref[...].astype(F32)

    def other(k):
        return pl.BlockSpec((None, br, cols), lambda i, ch: (jnp.where(ch[0] <= k, k + 1, k), i, 0))

    spec = pltpu.PrefetchScalarGridSpec(
        num_scalar_prefetch=1, grid=(rows // br,),
        in_specs=[pl.BlockSpec((None, br, cols), lambda i, ch: (ch[0], i, 0)), other(0), other(1), other(2)],
        out_specs=pl.BlockSpec((br, cols), lambda i, ch: (i, 0)))
    return _call(body, (chip, part, landed, landed, landed), name="sum_partials", grid_spec=spec,
                 out_shape=jax.ShapeDtypeStruct((rows, cols), F32), semantics=("parallel",))


def _cast_to_slab(w, l, chip):
    _, rows, cols = w.shape
    br = _divisor_tile(rows, 16, ELEM_ROWS)

    def body(chip_ref, w_ref, o_ref):
        o_ref[...] = w_ref[...].astype(BF16)

    spec = pltpu.PrefetchScalarGridSpec(
        num_scalar_prefetch=1, grid=(rows // br,),
        in_specs=[pl.BlockSpec((None, br, cols), lambda i, ch: (l, i, 0))],
        out_specs=pl.BlockSpec((None, br, cols), lambda i, ch: (ch[0], i, 0)))
    return _call(body, (chip, w), name="cast_to_slab", grid_spec=spec,
                 out_shape=jax.ShapeDtypeStruct((N_CHIPS, rows, cols), BF16), semantics=("parallel",))


def _cast_w_in_t_to_slabs(w_t, chip):
    rows, depth, d = w_t.shape
    tn = _divisor_tile(d, LANES, 256)

    def body(chip_ref, w_ref, *o_refs):
        for l in range(depth):
            o_refs[l][...] = w_ref[:, l, :].astype(BF16)

    spec = pltpu.PrefetchScalarGridSpec(
        num_scalar_prefetch=1, grid=(d // tn,),
        in_specs=[pl.BlockSpec((rows, depth, tn), lambda j, ch: (0, 0, j))],
        out_specs=[pl.BlockSpec((None, rows, tn), lambda j, ch: (ch[0], 0, j))] * depth)
    return _call(body, (chip, w_t), name="cast_w_in_t_to_slabs", grid_spec=spec,
                 out_shape=[jax.ShapeDtypeStruct((N_CHIPS, rows, d), BF16)] * depth, semantics=("parallel",),
                 vmem_bytes=4 * _nbytes((rows, max(depth, SUBLANES), tn), F32))


def _place_slab(buf, index, n_slabs):
    rows, cols = buf.shape
    br = _divisor_tile(rows, SUBLANES, ELEM_ROWS)

    def body(index_ref, b_ref, o_ref):
        o_ref[...] = b_ref[...]

    spec = pltpu.PrefetchScalarGridSpec(
        num_scalar_prefetch=1, grid=(rows // br,),
        in_specs=[pl.BlockSpec((br, cols), lambda i, ix: (i, 0))],
        out_specs=pl.BlockSpec((None, br, cols), lambda i, ix: (ix[0], i, 0)))
    return _call(body, (index, buf), name="place_slab", grid_spec=spec,
                 out_shape=jax.ShapeDtypeStruct((n_slabs, rows, cols), buf.dtype), semantics=("parallel",))


ELEM_ROWS = 256


def _sum_slabs(r):
    n, rows, cols = r.shape
    br = _divisor_tile(rows, 16, ELEM_ROWS)

    def body(r_ref, o_ref):
        acc = r_ref[0].astype(F32)
        for j in range(1, n):
            acc = acc + r_ref[j].astype(F32)
        o_ref[...] = acc

    return _call(body, (r,), name="sum_slabs", grid=(rows // br,),
                 in_specs=[pl.BlockSpec((n, br, cols), lambda i: (0, i, 0))],
                 out_specs=pl.BlockSpec((br, cols), lambda i: (i, 0)),
                 out_shape=jax.ShapeDtypeStruct((rows, cols), F32), semantics=("parallel",))


def _adamw_math(w, g, m, v):
    c1 = 1.0 - ADAM_B1 ** ADAM_STEP
    c2 = 1.0 - ADAM_B2 ** ADAM_STEP
    nm = ADAM_B1 * m + (1.0 - ADAM_B1) * g
    nv = ADAM_B2 * v + (1.0 - ADAM_B2) * (g * g)
    delta = -ADAM_LR * ((nm / c1) / (jnp.sqrt(nv / c2) + ADAM_EPS) + ADAM_WD * w)
    return delta, nm, nv


def _adamw(w, g, m, v):
    rows, cols = w.shape
    br = _divisor_tile(rows, 8, ELEM_ROWS)

    def body(w_ref, g_ref, m_ref, v_ref, d_ref, nm_ref, nv_ref):
        d_ref[...], nm_ref[...], nv_ref[...] = _adamw_math(w_ref[...], g_ref[...], m_ref[...], v_ref[...])

    blk = pl.BlockSpec((br, cols), lambda i: (i, 0))
    return _call(body, (w, g, m, v), name="adamw", grid=(rows // br,),
                 in_specs=[blk] * 4, out_specs=[blk] * 3,
                 out_shape=[jax.ShapeDtypeStruct((rows, cols), F32)] * 3, semantics=("parallel",))


def _adamw_w_in_t(w_t, m_t, v_t, g_mine, g_theirs):
    rows, depth, d = w_t.shape
    tn = LANES

    def body(w_ref, m_ref, v_ref, ga_ref, gb_ref, g_ref, d_ref, nm_ref, nv_ref):
        g = ga_ref[...] + gb_ref[...]
        g_ref[...] = g
        d_ref[...], nm_ref[...], nv_ref[...] = _adamw_math(w_ref[...], g, m_ref[...], v_ref[...])

    slab = pl.BlockSpec((rows, depth, tn), lambda j: (0, 0, j))
    return _call(body, (w_t, m_t, v_t, g_mine, g_theirs), name="adamw_w_in_t", grid=(d // tn,),
                 in_specs=[slab] * 5, out_specs=[slab] * 4,
                 out_shape=[jax.ShapeDtypeStruct(w_t.shape, F32)] * 4, semantics=("parallel",),
                 vmem_bytes=2 * 9 * _nbytes((rows, max(depth, SUBLANES), tn), F32))


def _adamw_layer(w, m, v, l, g_mine, g_theirs, prev):
    _, rows, cols = w.shape
    br = _divisor_tile(rows, 8, ELEM_ROWS)

    def body(w_ref, m_ref, v_ref, ga_ref, gb_ref, *rest):
        g_ref, d_ref, nm_ref, nv_ref = rest[4:]
        g = ga_ref[...] + gb_ref[...]
        g_ref[...] = g
        d_ref[...], nm_ref[...], nv_ref[...] = _adamw_math(w_ref[...], g, m_ref[...], v_ref[...])

    slot = pl.BlockSpec((None, br, cols), lambda i: (l, i, 0))
    blk = pl.BlockSpec((br, cols), lambda i: (i, 0))
    return _call(body, (w, m, v, g_mine, g_theirs) + tuple(prev), name="adamw_layer", grid=(rows // br,),
                 in_specs=[slot] * 3 + [blk] * 2 + [ANY] * 4, out_specs=[slot] * 4,
                 out_shape=[jax.ShapeDtypeStruct(w.shape, F32)] * 4,
                 input_output_aliases={5: 0, 6: 1, 7: 2, 8: 3}, semantics=("parallel",))


BIG = ("w_in", "w_out", "w_up", "w_down")
WEIGHTS = ("meta", "attn_norm_g", "w_in", "b_f", "conv_w", "conv_b", "w_gate_a", "b_gate_a", "w_gate_x",
           "b_gate_x", "lru_L", "attn_out_g", "rec_out_g", "w_out", "mlp_norm_g", "w_up", "w_down", "final_g")
SMALL = tuple(k for k in WEIGHTS if k not in BIG)
COL_SHARDED_SMALL = ("meta", "conv_w")


def _packed_rows(shape):
    return -(-math.prod(shape) // (SUBLANES * LANES)) * SUBLANES


def _pack(arrs):
    rows = []
    for a in arrs:
        flat = a.reshape(-1)
        rows.append(jnp.pad(flat, (0, _packed_rows(a.shape) * LANES - flat.shape[0])).reshape(-1, LANES))
    used = sum(r.shape[0] for r in rows)
    rows.append(jnp.zeros((-used % ELEM_ROWS, LANES), F32))
    return jnp.concatenate(rows, axis=0)


def _unpack(buf, shapes):
    out, r0 = [], 0
    for s in shapes:
        nr = _packed_rows(s)
        out.append(buf[r0:r0 + nr].reshape(-1)[:math.prod(s)].reshape(s))
        r0 += nr
    return out


def _halves(a):
    return a.reshape((2, a.shape[0] // 2) + a.shape[1:])


def _cols_from_chips(g):
    return jnp.moveaxis(g, 0, -2).reshape(g.shape[1:-1] + (N_CHIPS * g.shape[-1],))


def kernel(x, meta, attn_norm_g, w_in, b_f, conv_w, conv_b, w_gate_a, b_gate_a, w_gate_x, b_gate_x, lru_L, attn_out_g, rec_out_g, w_out, mlp_norm_g, w_up, w_down, final_g, loss_target, m_meta, m_attn_norm_g, m_w_in, m_b_f, m_conv_w, m_conv_b, m_w_gate_a, m_b_gate_a, m_w_gate_x, m_b_gate_x, m_lru_L, m_attn_out_g, m_rec_out_g, m_w_out, m_mlp_norm_g, m_w_up, m_w_down, m_final_g, v_meta, v_attn_norm_g, v_w_in, v_b_f, v_conv_w, v_conv_b, v_w_gate_a, v_b_gate_a, v_w_gate_x, v_b_gate_x, v_lru_L, v_attn_out_g, v_rec_out_g, v_w_out, v_mlp_norm_g, v_w_up, v_w_down, v_final_g):
    w = dict(meta=meta, attn_norm_g=attn_norm_g, w_in=w_in, b_f=b_f, conv_w=conv_w, conv_b=conv_b,
             w_gate_a=w_gate_a, b_gate_a=b_gate_a, w_gate_x=w_gate_x, b_gate_x=b_gate_x, lru_L=lru_L,
             attn_out_g=attn_out_g, rec_out_g=rec_out_g, w_out=w_out, mlp_norm_g=mlp_norm_g, w_up=w_up,
             w_down=w_down, final_g=final_g)
    m = dict(meta=m_meta, attn_norm_g=m_attn_norm_g, w_in=m_w_in, b_f=m_b_f, conv_w=m_conv_w, conv_b=m_conv_b,
             w_gate_a=m_w_gate_a, b_gate_a=m_b_gate_a, w_gate_x=m_w_gate_x, b_gate_x=m_b_gate_x, lru_L=m_lru_L,
             attn_out_g=m_attn_out_g, rec_out_g=m_rec_out_g, w_out=m_w_out, mlp_norm_g=m_mlp_norm_g,
             w_up=m_w_up, w_down=m_w_down, final_g=m_final_g)
    v = dict(meta=v_meta, attn_norm_g=v_attn_norm_g, w_in=v_w_in, b_f=v_b_f, conv_w=v_conv_w, conv_b=v_conv_b,
             w_gate_a=v_w_gate_a, b_gate_a=v_b_gate_a, w_gate_x=v_w_gate_x, b_gate_x=v_b_gate_x, lru_L=v_lru_L,
             attn_out_g=v_attn_out_g, rec_out_g=v_rec_out_g, w_out=v_w_out, mlp_norm_g=v_mlp_norm_g,
             w_up=v_w_up, w_down=v_w_down, final_g=v_final_g)
    s_len, d = x.shape[1], x.shape[2]
    depth = w_in.shape[0]
    att_w = d // 2
    rec_w = d - att_w
    nh = att_w // HEAD_DIM
    chip = 2 * lax.axis_index("x") + lax.axis_index("y")

    g_conv, g_meta = [g.reshape((N_CHIPS, g.shape[1] * g.shape[2]) + g.shape[3:])
                      for g in _all_gather_chips([_halves(w["conv_w"]), _halves(w["meta"])])]
    p = dict(w)
    p["conv_w"] = _cols_from_chips(g_conv)
    meta_full = jnp.moveaxis(g_meta, 0, 1).reshape(N_META, d)

    chip1 = chip.reshape(1).astype(jnp.int32)
    w_in_t, m_in_t, v_in_t = [jnp.transpose(a["w_in"], (2, 0, 1)) for a in (w, m, v)]
    w_in_slabs = _cast_w_in_t_to_slabs(w_in_t, chip1)
    pushes, tokens = [], []
    for l in range(depth):
        slabs = [w_in_slabs[l]] + [_cast_to_slab(w[k], l, chip1) for k in BIG[1:]]
        send_sems, recv_sems, _, lands, token = _push_start(slabs, slabs, "gather_chips", f"weights_start_{l}")
        pushes.append((send_sems, recv_sems, lands))
        tokens.append(token[0, 0])

    t_len = N_META + s_len
    pad = -t_len % SEQ_TILE
    h = jnp.concatenate([meta_full, x[0], jnp.zeros((pad, d), F32)], axis=0)
    tgt = jnp.concatenate([jnp.zeros((N_META, d), F32), loss_target[0], jnp.zeros((pad, d), F32)], axis=0)
    z = _rms_fwd(h, _row(p["attn_norm_g"][0] + sum(tokens)))
    saved = []
    for l in range(depth):
        send_sems, recv_sems, lands = pushes[l]

        def fetch(k, after, l=l, send_sems=send_sems, recv_sems=recv_sems, lands=lands):
            i = BIG.index(k)
            return _push_wait(send_sems, recv_sems, [i], [lands[i]], [lands[i]], "gather_chips", after,
                              f"{k}_wait_{l}")[0]

        g_next = p["attn_norm_g"][l + 1] if l + 1 < depth else p["final_g"]
        h, z, sv = _forward_layer(l, h, z, p, fetch, g_next)
        saved.append(sv)
    dh, dh_b, dg_final, loss_part = _loss_bwd(h, _row(p["final_g"]), tgt, s_len)

    small = {k: [None] * depth for k in SMALL if k not in ("meta", "final_g")}
    pushes = [None] * depth
    tok = 0.0
    for l in reversed(range(depth)):
        dh, dh_b, big_mlp, sm_mlp = _backward_mlp(l, dh, dh_b, saved[l], p, tok)
        parts = [big_mlp["w_down"], big_mlp["w_up"]]
        push_mlp = _push_start(parts, [lax.empty(a.shape, a.dtype) for a in parts], "scatter_chips",
                               f"mlp_grads_start_{l}")
        dh, dh_b, big_mix, sm_mix = _backward_mixer(l, dh, dh_b, saved[l], p, push_mlp[4][0, 0])
        parts = [big_mix["w_out"], big_mix["w_in"]]
        push_mix = _push_start(parts, [lax.empty(a.shape, a.dtype) for a in parts], "scatter_chips",
                               f"mixer_grads_start_{l}")
        tok = push_mix[4][0, 0]
        pushes[l] = {("w_down", "w_up"): push_mlp, ("w_out", "w_in"): push_mix}
        for k, val in {**sm_mlp, **sm_mix}.items():
            small[k][l] = val
    grads = {k: jnp.stack(val) for k, val in small.items()}
    grads["final_g"] = dg_final[0]
    grads["meta"] = dh[:N_META]
    dx = dh[N_META:t_len]

    full_shapes = [grads[k].shape for k in SMALL] + [(1,)]
    packed = _pack([grads[k].astype(F32) for k in SMALL] + [loss_part[0, :1] + tok])
    dev1 = (2 * chip + lax.axis_index("c")).reshape(1).astype(jnp.int32)
    slabs = [_place_slab(packed, dev1, N_DEV)]
    small_push = _push_start(slabs, slabs, "gather_devices", "small_grads_start")

    last_token = small_push[4]
    outs = {k: [lax.empty(w[k].shape, F32) for _ in range(4)] for k in BIG[1:]}
    w_in_sums = [None] * depth
    swaps = {}

    def finish(l, after):
        send_sems, recv_sems, mine, lands, _ = swaps[l]
        mine, theirs = _push_wait(send_sems, recv_sems, list(range(len(BIG))), mine, lands, "sibling", after,
                                  f"sums_wait_{l}")
        w_in_sums[l] = (mine[0], theirs[0])
        for k, a, b in zip(BIG[1:], mine[1:], theirs[1:]):
            outs[k] = _adamw_layer(w[k], m[k], v[k], l, a, b, outs[k])

    for l in reversed(range(depth)):
        sums = {}
        for names, (send_sems, recv_sems, parts, lands, _) in pushes[l].items():
            parts, landed = _push_wait(send_sems, recv_sems, [0, 1], parts, lands, "scatter_chips", last_token,
                                       f"{names[0]}_grads_wait_{l}")
            for k, part, land in zip(names, parts, landed):
                sums[k] = _sum_partials(part, land, chip1)
        mine = [sums[k] for k in BIG]
        swaps[l] = _push_start(mine, [lax.empty(a.shape, a.dtype) for a in mine], "sibling", f"sums_start_{l}")
        if l + 1 < depth:
            finish(l + 1, mine[0])
    finish(0, swaps[0][4])
    outs["w_in"] = [jnp.transpose(r, (1, 2, 0)) for r in _adamw_w_in_t(
        w_in_t, m_in_t, v_in_t, jnp.stack([s[0] for s in w_in_sums], axis=1),
        jnp.stack([s[1] for s in w_in_sums], axis=1))]
    out_g, out_d, out_m, out_v = [{k: outs[k][i] for k in BIG} for i in range(4)]

    landed = _push_wait(small_push[0], small_push[1], [0], small_push[3], small_push[3], "gather_devices",
                        out_g["w_in"], "small_grads_wait")
    total = _sum_slabs(landed[0])
    small_g = dict(zip(SMALL + ("loss",), _unpack(total, full_shapes)))
    for k in COL_SHARDED_SMALL:
        n = w[k].shape[-1]
        small_g[k] = lax.dynamic_slice_in_dim(small_g[k], chip * n, n, axis=small_g[k].ndim - 1)
    local_shapes = [w[k].shape for k in SMALL]
    res = _adamw(_pack([w[k] for k in SMALL]), _pack([small_g[k] for k in SMALL]),
                 _pack([m[k] for k in SMALL]), _pack([v[k] for k in SMALL]))
    out_g.update({k: small_g[k] for k in SMALL})
    for dst, buf in zip((out_d, out_m, out_v), res):
        dst.update(zip(SMALL, _unpack(buf, local_shapes)))

    return (small_g["loss"].reshape(()), dx[None],
            *[out_g[k] for k in WEIGHTS], *[out_d[k] for k in WEIGHTS],
            *[out_m[k] for k in WEIGHTS], *[out_v[k] for k in WEIGHTS])
```

```python
import functools
import math

import jax
import jax.numpy as jnp
from jax import lax
from jax.experimental import pallas as pl
from jax.experimental.pallas import tpu as pltpu

F32 = jnp.float32
BF16 = jnp.bfloat16

N_META = 16
HEAD_DIM = 64
N_REC_BLOCKS = 8
CONV_WIDTH = 4
RG_C = 8.0
NORM_EPS = 1e-6
ADAM_LR = 0.001
ADAM_B1 = 0.9
ADAM_B2 = 0.999
ADAM_EPS = 1e-08
ADAM_WD = 0.01
ADAM_STEP = 10

LANES = 128
SUBLANES = 8
SEQ_TILE = 128
VMEM_CAP = 60 * 2**20
VMEM_SLACK = 6 * 2**20
NEG_BIG = -1e30
N_CHIPS = 4
N_DEV = 8
MESH = pl.DeviceIdType.MESH


def _nbytes(shape, dtype):
    return math.prod(shape) * jnp.dtype(dtype).itemsize


def _call(body, args, *, name, out_shape, grid=(), in_specs=None, out_specs=None, scratch_shapes=(),
          grid_spec=None, semantics=None, vmem_bytes=None, side_effects=None, hbm_results=True, **kw):
    cp = {}
    if semantics is not None:
        cp["dimension_semantics"] = semantics
    if vmem_bytes is not None:
        cp["vmem_limit_bytes"] = int(min(VMEM_CAP, vmem_bytes + VMEM_SLACK))
    if side_effects is not None:
        cp["has_side_effects"] = side_effects
    if grid_spec is not None:
        kw["grid_spec"] = grid_spec
    else:
        kw.update(grid=grid, in_specs=in_specs, out_specs=out_specs, scratch_shapes=scratch_shapes)
    if hbm_results:
        out_shape = jax.tree.map(
            lambda s: pltpu.HBM(s.shape, s.dtype) if isinstance(s, jax.ShapeDtypeStruct) else s, out_shape)
    fn = pl.pallas_call(
        body, name=name, out_shape=out_shape,
        compiler_params=pltpu.CompilerParams(**cp), **kw)
    return fn(*[_in_hbm(a) if jnp.issubdtype(getattr(a, "dtype", jnp.int32), jnp.floating) else a for a in args])


def _divisor_tile(n, unit, target):
    best = None
    for t in range(unit, min(n, target) + 1, unit):
        if n % t == 0:
            best = t
    return n if best is None else best


def _sigmoid(x):
    return 1.0 / (1.0 + jnp.exp(-x))


def _log1p_unit(e):
    series = e * (1.0 - e * (0.5 - e * (1.0 / 3.0)))
    return jnp.where(e < 1e-2, series, jnp.log(1.0 + e))


def _log_sigmoid(x):
    return jnp.minimum(x, 0.0) - _log1p_unit(jnp.exp(-jnp.abs(x)))


def _one_minus_exp(x, exp_x):
    small = -x * (1.0 + x * (1.0 / 2 + x * (1.0 / 6 + x * (1.0 / 24 + x * (1.0 / 120 + x * (1.0 / 720))))))
    return jnp.where(x > -0.25, small, 1.0 - exp_x)


_GELU_K = math.sqrt(2.0 / math.pi)
_GELU_C = 0.044715


def _gelu_and_grad(y):
    th = jnp.tanh(_GELU_K * (y + _GELU_C * y * y * y))
    g = 0.5 * y * (1.0 + th)
    dg = 0.5 * (1.0 + th) + 0.5 * y * (1.0 - th * th) * _GELU_K * (1.0 + 3.0 * _GELU_C * y * y)
    return g, dg


def _rstd(x):
    return lax.rsqrt(jnp.mean(x * x, axis=-1, keepdims=True) + NORM_EPS)


def _rms_bwd(dz, x, g):
    rs = _rstd(x)
    xh = x * rs
    dgp = jnp.sum(dz * xh, axis=0, keepdims=True)
    dxh = dz * g
    dx = rs * (dxh - xh * jnp.mean(dxh * xh, axis=-1, keepdims=True))
    return dx, dgp


def _dot(a, b):
    return jnp.dot(a, b, preferred_element_type=F32)


def _dot_nt(a, b):
    return lax.dot_general(a, b, (((1,), (1,)), ((), ())), preferred_element_type=F32)


def _dot_tn(a, b):
    return lax.dot_general(a, b, (((0,), (0,)), ((), ())), preferred_element_type=F32)


def _full(shape):
    nd = len(shape)
    return pl.BlockSpec(shape, lambda *_: (0,) * nd)


def _rms_fwd(h, g):
    tp, d = h.shape
    tm = _divisor_tile(tp, 16, 544)

    def body(h_ref, g_ref, z_ref):
        x = h_ref[...]
        z_ref[...] = (x * _rstd(x) * g_ref[...]).astype(BF16)

    return _call(body, (h, g), name="rms_fwd", grid=(tp // tm,),
                 in_specs=[pl.BlockSpec((tm, d), lambda i: (i, 0)), _full((1, d))],
                 out_specs=pl.BlockSpec((tm, d), lambda i: (i, 0)),
                 out_shape=jax.ShapeDtypeStruct((tp, d), BF16), semantics=("parallel",))


def _proj(z, w_big_t, att_w):
    tp, d = z.shape
    nb = w_big_t.shape[0]
    tn = _divisor_tile(nb, LANES, 512)
    assert (3 * att_w) % tn == 0
    n_qkv = 3 * att_w // tn
    scale = 1.0 / math.sqrt(HEAD_DIM)

    def body(z_ref, w_ref, p_ref, qkv_ref):
        j = pl.program_id(0)
        acc = _dot_nt(z_ref[...], w_ref[...])
        p_ref[...] = acc

        @pl.when(j < n_qkv)
        def _():
            col = j * tn + lax.broadcasted_iota(jnp.int32, (1, tn), 1)
            qkv_ref[...] = (acc * jnp.where(col < att_w, scale, 1.0)).astype(BF16)

    vm = 2 * (_nbytes((tp, d), BF16) + _nbytes((d, tn), BF16) + _nbytes((tp, tn), F32) * 2)
    return _call(body, (z, w_big_t), name="proj", grid=(nb // tn,),
                 in_specs=[_full((tp, d)), pl.BlockSpec((tn, d), lambda j: (j, 0))],
                 out_specs=[pl.BlockSpec((tp, tn), lambda j: (0, j)),
                            pl.BlockSpec((tp, tn), lambda j: (0, jnp.minimum(j, n_qkv - 1)))],
                 out_shape=[jax.ShapeDtypeStruct((tp, nb), F32),
                            jax.ShapeDtypeStruct((tp, 3 * att_w), BF16)],
                 semantics=("arbitrary",), vmem_bytes=vm)


def _tile_cumsum(x, row, reverse=False):
    for s in (1, 2, 4):
        if reverse:
            x = x + jnp.where(row < SUBLANES - s, pltpu.roll(x, SUBLANES - s, 0), 0.0)
        else:
            x = x + jnp.where(row >= s, pltpu.roll(x, s, 0), 0.0)
    return x


def _fgate_fwd(proj, b_f_pad, nh):
    tp, nb = proj.shape
    fblk = nb // LANES - 1

    def body(f_ref, b_ref, c_ref, ct_ref):
        b = b_ref[...]
        row = lax.broadcasted_iota(jnp.int32, (SUBLANES, LANES), 0)

        def step(i, carry):
            r0 = pl.multiple_of(i * SUBLANES, SUBLANES)
            lf = _log_sigmoid(f_ref[pl.ds(r0, SUBLANES), :] + b)
            x = _tile_cumsum(lf, row) + carry
            c_ref[pl.ds(r0, SUBLANES), :] = x
            return x[SUBLANES - 1:SUBLANES, :]

        lax.fori_loop(0, tp // SUBLANES, step, jnp.zeros((1, LANES), F32))
        ct_ref[...] = c_ref[...].T[:nh, :]

    return _call(body, (proj, b_f_pad), name="fgate_fwd", grid=(1,),
                 in_specs=[pl.BlockSpec((tp, LANES), lambda i: (0, fblk)), _full((1, LANES))],
                 out_specs=[_full((tp, LANES)), _full((nh, tp))],
                 out_shape=[jax.ShapeDtypeStruct((tp, LANES), F32), jax.ShapeDtypeStruct((nh, tp), F32)],
                 semantics=("arbitrary",))


def _fgate_bwd(proj, b_f_pad, dc):
    tp, nb = proj.shape
    fblk = nb // LANES - 1

    def body(f_ref, b_ref, dc_ref, df_ref, db_ref, dc_s):
        b = b_ref[...]
        row = lax.broadcasted_iota(jnp.int32, (SUBLANES, LANES), 0)
        nt = tp // SUBLANES

        def step(i, carry):
            suffix, acc = carry
            r0 = pl.multiple_of((nt - 1 - i) * SUBLANES, SUBLANES)
            dlf = _tile_cumsum(dc_ref[pl.ds(r0, SUBLANES), :], row, reverse=True) + suffix
            df = dlf * _sigmoid(-(f_ref[pl.ds(r0, SUBLANES), :] + b))
            dc_s[pl.ds(r0, SUBLANES), :] = df
            return dlf[0:1, :], acc + df

        _, acc = lax.fori_loop(0, nt, step, (jnp.zeros((1, LANES), F32), jnp.zeros((SUBLANES, LANES), F32)))
        df_ref[...] = dc_s[...].astype(BF16)
        db_ref[...] = jnp.broadcast_to(jnp.sum(acc, axis=0, keepdims=True), (SUBLANES, LANES))

    return _call(body, (proj, b_f_pad, dc), name="fgate_bwd", grid=(1,),
                 in_specs=[pl.BlockSpec((tp, LANES), lambda i: (0, fblk)), _full((1, LANES)), _full((tp, LANES))],
                 out_specs=[_full((tp, LANES)), _full((SUBLANES, LANES))],
                 out_shape=[jax.ShapeDtypeStruct((tp, LANES), BF16),
                            jax.ShapeDtypeStruct((SUBLANES, LANES), F32)],
                 scratch_shapes=[pltpu.VMEM((tp, LANES), F32)], semantics=("arbitrary",))


ATT_BQ = 128


ATT_BUCKET = 3
ATT_HEADS = 4


def _for_bucket(i, nq, fn):
    for lo in range(0, nq, ATT_BUCKET):
        hi = min(lo + ATT_BUCKET, nq)
        spans = ([(0, lo * ATT_BQ, False)] if lo else []) + [(lo * ATT_BQ, hi * ATT_BQ, True)]
        pl.when(jnp.logical_and(i >= lo, i < hi))(functools.partial(fn, spans))


def _head_column(c_blk, h):
    lane = lax.broadcasted_iota(jnp.int32, c_blk.shape, 1)
    return jnp.sum(jnp.where(lane == h, c_blk, 0.0), axis=1, keepdims=True)


def _key_major_logits(kh_ref, ck_ref, hh, q, span, q0):
    k0, k1, needs_mask = span
    t = _dot_nt(kh_ref[hh, k0:k1, :], q) - ck_ref[hh, k0:k1, :]
    if needs_mask:
        keys = k0 + lax.broadcasted_iota(jnp.int32, (k1 - k0, ATT_BQ), 0)
        t = jnp.where(keys <= q0 + lax.broadcasted_iota(jnp.int32, (k1 - k0, ATT_BQ), 1), t, NEG_BIG)
    return t


def _stage_keys(p, k_ref, c_ref, kh_s, ck_s):
    for hh in range(ATT_HEADS):
        kh_s[hh] = k_ref[:, HEAD_DIM * hh:HEAD_DIM * (hh + 1)]
        ck_s[hh] = jnp.broadcast_to(_head_column(c_ref[...], ATT_HEADS * p + hh), ck_s.shape[1:])


def _attn_fwd(qkv, c, c_t, nh):
    tp = qkv.shape[0]
    att_w = nh * HEAD_DIM
    ng = nh // ATT_HEADS
    gw = ATT_HEADS * HEAD_DIM
    bq = ATT_BQ
    nq = tp // bq

    def body(q_ref, k_ref, v_ref, c_ref, ct_ref, o_ref, lse_ref, ck_s, kh_s, vt_s):
        p = pl.program_id(0)
        i = pl.program_id(1)

        @pl.when(i == 0)
        def _():
            _stage_keys(p, k_ref, c_ref, kh_s, ck_s)
            vt_s[...] = v_ref[...].astype(F32).T.astype(BF16)

        def compute(spans):
            q0 = pl.multiple_of(i * bq, bq)
            o_t, lses = [], []
            for hh in range(ATT_HEADS):
                lo = HEAD_DIM * hh
                q = q_ref[:, lo:lo + HEAD_DIM]
                ts = [_key_major_logits(kh_s, ck_s, hh, q, sp, q0) for sp in spans]
                m = functools.reduce(jnp.maximum, [jnp.max(t, axis=0, keepdims=True) for t in ts])
                es = [jnp.exp(t - m) for t in ts]
                l = sum(jnp.sum(e, axis=0, keepdims=True) for e in es)
                o = sum(_dot(vt_s[lo:lo + HEAD_DIM, k0:k1], e.astype(BF16)) for e, (k0, k1, _) in zip(es, spans))
                o_t.append(o / l)
                lses.append(m + ct_ref[pl.ds(ATT_HEADS * p + hh, 1), :] + jnp.log(l))
            o_ref[...] = jnp.concatenate(o_t, axis=0).T
            lse_ref[...] = jnp.concatenate(lses, axis=0)

        _for_bucket(i, nq, compute)

    blk = pl.BlockSpec((bq, gw), lambda p, i: (i, p))
    vm = 6 * _nbytes((tp, gw), BF16) + 2 * ATT_HEADS * _nbytes((tp, LANES), F32) + 2 * _nbytes((tp, LANES), F32) \
        + 8 * ATT_HEADS * _nbytes((bq, tp), F32)
    return _call(body, (qkv, qkv, qkv, c, c_t), name="attn_fwd", grid=(ng, nq),
                 in_specs=[blk,
                           pl.BlockSpec((tp, gw), lambda p, i: (0, ng + p)),
                           pl.BlockSpec((tp, gw), lambda p, i: (0, 2 * ng + p)),
                           _full((tp, LANES)), pl.BlockSpec((nh, bq), lambda p, i: (0, i))],
                 out_specs=[blk, pl.BlockSpec((None, ATT_HEADS, bq), lambda p, i: (p, 0, i))],
                 out_shape=[jax.ShapeDtypeStruct((tp, att_w), F32), jax.ShapeDtypeStruct((ng, ATT_HEADS, tp), F32)],
                 scratch_shapes=[pltpu.VMEM((ATT_HEADS, tp, LANES), F32), pltpu.VMEM((ATT_HEADS, tp, HEAD_DIM), BF16),
                                 pltpu.VMEM((gw, tp), BF16)],
                 semantics=("arbitrary", "arbitrary"), vmem_bytes=vm)


def _attn_bwd(qkv, c, c_t, lse_b, do, nh):
    tp = qkv.shape[0]
    att_w = nh * HEAD_DIM
    ng = nh // ATT_HEADS
    gw = ATT_HEADS * HEAD_DIM
    bq = ATT_BQ
    nq = tp // bq
    scale = 1.0 / math.sqrt(HEAD_DIM)

    def body(q_ref, k_ref, v_ref, c_ref, ct_ref, lse_ref, do_ref, dq_ref, dk_ref, dv_ref, dc_ref,
             dk_s, dv_s, dc_s, ck_s, kt_s, kh_s, vh_s):
        p = pl.program_id(0)
        i = pl.program_id(1)

        @pl.when(i == 0)
        def _():
            dk_s[...] = jnp.zeros_like(dk_s)
            dv_s[...] = jnp.zeros_like(dv_s)
            dc_s[...] = jnp.zeros_like(dc_s)
            kt_s[...] = k_ref[...].astype(F32).T.astype(BF16)
            _stage_keys(p, k_ref, c_ref, kh_s, ck_s)
            for hh in range(ATT_HEADS):
                vh_s[hh] = v_ref[:, HEAD_DIM * hh:HEAD_DIM * (hh + 1)]

        @pl.when(jnp.logical_and(i == 0, p == 0))
        def _():
            dc_ref[...] = jnp.zeros_like(dc_ref)

        def compute(spans):
            q0 = pl.multiple_of(i * bq, bq)
            dq_t = []
            for hh in range(ATT_HEADS):
                lo = HEAD_DIM * hh
                h = ATT_HEADS * p + hh
                q = q_ref[:, lo:lo + HEAD_DIM]
                doutb = do_ref[:, lo:lo + HEAD_DIM].astype(BF16)
                col_term = ct_ref[pl.ds(h, 1), :] - lse_ref[hh:hh + 1, :]
                prs, dps = [], []
                for sp in spans:
                    prs.append(jnp.exp(_key_major_logits(kh_s, ck_s, hh, q, sp, q0) + col_term))
                    dps.append(_dot_nt(vh_s[hh, sp[0]:sp[1], :], doutb))
                key_sum = sum(jnp.sum(pr * dp, axis=0, keepdims=True) for pr, dp in zip(prs, dps))
                dq_h = 0.0
                for (k0, k1, _), pr, dp in zip(spans, prs, dps):
                    ds = pr * (dp - key_sum)
                    dsb = ds.astype(BF16)
                    dq_h = dq_h + _dot(kt_s[lo:lo + HEAD_DIM, k0:k1], dsb)
                    dk_s[hh, k0:k1, :] += _dot(dsb, q)
                    dv_s[hh, k0:k1, :] += _dot(pr.astype(BF16), doutb)
                    dc_s[hh, k0:k1, :] += jnp.sum(ds, axis=1, keepdims=True)
                dq_t.append(dq_h)
            dq_ref[...] = (jnp.concatenate(dq_t, axis=0) * scale).T.astype(BF16)

        _for_bucket(i, nq, compute)

        @pl.when(i == nq - 1)
        def _():
            dk_ref[...] = jnp.concatenate([dk_s[hh] for hh in range(ATT_HEADS)], axis=1).astype(BF16)
            dv_ref[...] = jnp.concatenate([dv_s[hh] for hh in range(ATT_HEADS)], axis=1).astype(BF16)
            lane = lax.broadcasted_iota(jnp.int32, (tp, LANES), 1)
            dc = dc_ref[...]
            for hh in range(ATT_HEADS):
                dc = jnp.where(lane == ATT_HEADS * p + hh, -dc_s[hh], dc)
            dc_ref[...] = dc

    blk = pl.BlockSpec((bq, gw), lambda p, i: (i, p))
    col = pl.BlockSpec((tp, gw), lambda p, i: (0, p))
    vm = 7 * _nbytes((tp, gw), BF16) + 2 * _nbytes((tp, gw), F32) + 2 * ATT_HEADS * _nbytes((tp, LANES), F32) \
        + 4 * _nbytes((tp, LANES), F32) + 12 * ATT_HEADS * _nbytes((bq, tp), F32)
    return _call(body, (qkv, qkv, qkv, c, c_t, lse_b, do), name="attn_bwd", grid=(ng, nq),
                 in_specs=[blk,
                           pl.BlockSpec((tp, gw), lambda p, i: (0, ng + p)),
                           pl.BlockSpec((tp, gw), lambda p, i: (0, 2 * ng + p)),
                           _full((tp, LANES)), pl.BlockSpec((nh, bq), lambda p, i: (0, i)),
                           pl.BlockSpec((None, ATT_HEADS, bq), lambda p, i: (p, 0, i)), blk],
                 out_specs=[blk, col, col, _full((tp, LANES))],
                 out_shape=[jax.ShapeDtypeStruct((tp, att_w), BF16)] * 3 + [jax.ShapeDtypeStruct((tp, LANES), F32)],
                 scratch_shapes=[pltpu.VMEM((ATT_HEADS, tp, HEAD_DIM), F32), pltpu.VMEM((ATT_HEADS, tp, HEAD_DIM), F32),
                                 pltpu.VMEM((ATT_HEADS, tp, 1), F32), pltpu.VMEM((ATT_HEADS, tp, LANES), F32),
                                 pltpu.VMEM((gw, tp), BF16), pltpu.VMEM((ATT_HEADS, tp, HEAD_DIM), BF16),
                                 pltpu.VMEM((ATT_HEADS, tp, HEAD_DIM), BF16)],
                 semantics=("arbitrary", "arbitrary"), vmem_bytes=vm)


REC_ROWS = 128
HALO = SUBLANES


def _conv_taps(cat):
    taps = []
    for k in range(CONV_WIDTH):
        sh = CONV_WIDTH - 1 - k
        taps.append((pltpu.roll(cat, sh, 0) if sh else cat)[HALO:])
    return taps


def _rec_gates(xc, wa_ref, ba_ref, wx_ref, bx_ref, l_ref):
    xcb = xc.astype(BF16)
    r = _sigmoid(_dot(xcb, wa_ref[...]) + ba_ref[...])
    ig = _sigmoid(_dot(xcb, wx_ref[...]) + bx_ref[...])
    ls = _log_sigmoid(l_ref[...])
    log_a = RG_C * r * ls
    return xcb, r, ig, ls, log_a


def _rec_fwd(proj, xr_blk, yr_blk, rec_w, conv_w, conv_b, wa, ba, wx, bx, lru):
    tp = proj.shape[0]
    w = rec_w
    r_rows = REC_ROWS
    nc = tp // r_rows
    cpb = w // LANES

    def body(xr_ref, yr_ref, cw_ref, cb_ref, wa_ref, ba_ref, wx_ref, bx_ref, l_ref,
             hr_ref, rec_ref, prev_s, carry_s, a_s, u_s):
        i = pl.program_id(0)

        @pl.when(i == 0)
        def _():
            prev_s[...] = jnp.zeros_like(prev_s)
            carry_s[...] = jnp.zeros_like(carry_s)

        x = xr_ref[...]
        taps = _conv_taps(jnp.concatenate([prev_s[...], x], axis=0))
        prev_s[...] = x[r_rows - HALO:]
        xc = cb_ref[...]
        for k in range(CONV_WIDTH):
            xc = xc + cw_ref[k:k + 1, :] * taps[k]
        _, r, ig, ls, log_a = _rec_gates(xc, wa_ref, ba_ref, wx_ref, bx_ref, l_ref)
        a = jnp.exp(log_a)
        a_s[...] = a
        u_s[...] = jnp.sqrt(_one_minus_exp(2.0 * log_a, a * a)) * ig * xc

        def tile(j, h):
            r0 = pl.multiple_of(j * SUBLANES, SUBLANES)
            at = a_s[pl.ds(r0, SUBLANES), :]
            ut = u_s[pl.ds(r0, SUBLANES), :]
            out = []
            for rr in range(SUBLANES):
                h = at[rr:rr + 1] * h + ut[rr:rr + 1]
                out.append(h)
            hr_ref[pl.ds(r0, SUBLANES), :] = jnp.concatenate(out, axis=0)
            return h

        carry_s[0:1, :] = lax.fori_loop(0, r_rows // SUBLANES, tile, carry_s[0:1, :])
        g, _ = _gelu_and_grad(yr_ref[...])
        rec_ref[...] = hr_ref[...] * g

    blk = pl.BlockSpec((r_rows, w), lambda i: (i, 0))
    vm = 16 * _nbytes((r_rows, w), F32) + 4 * _nbytes((w, w), BF16)
    return _call(body, (proj, proj, conv_w, conv_b, wa, ba, wx, bx, lru), name="rec_fwd", grid=(nc,),
                 in_specs=[pl.BlockSpec((r_rows, w), lambda i: (i, xr_blk)),
                           pl.BlockSpec((r_rows, w), lambda i: (i, yr_blk)),
                           _full((CONV_WIDTH, w)), _full((1, w)), _full((w, w)), _full((1, w)),
                           _full((w, w)), _full((1, w)), _full((1, w))],
                 out_specs=[blk, blk],
                 out_shape=[jax.ShapeDtypeStruct((tp, w), F32)] * 2,
                 scratch_shapes=[pltpu.VMEM((HALO, w), F32), pltpu.VMEM((SUBLANES, w), F32),
                                 pltpu.VMEM((r_rows, w), F32), pltpu.VMEM((r_rows, w), F32)],
                 semantics=("arbitrary",), vmem_bytes=vm)


def _rec_bwd(proj, xr_blk, yr_blk, rec_w, hr, drec, conv_w, conv_b, wa, ba, wx, bx, lru):
    tp = proj.shape[0]
    w = rec_w
    r_rows = REC_ROWS
    nc = tp // r_rows
    hpc = r_rows // HALO

    def body(xr_ref, xh_ref, yr_ref, hr_ref, hh_ref, drec_ref, cw_ref, cb_ref, wa_ref, ba_ref, wx_ref, bx_ref,
             l_ref, dxr_ref, dyr_ref, dwa_ref, dwx_ref, small_ref, lam_s, a_s, dhr_s, carry_s, next_s):
        i = pl.program_id(0)
        first = (nc - 1 - i) == 0

        @pl.when(i == 0)
        def _():
            carry_s[...] = jnp.zeros_like(carry_s)
            next_s[...] = jnp.zeros_like(next_s)
            dwa_ref[...] = jnp.zeros_like(dwa_ref)
            dwx_ref[...] = jnp.zeros_like(dwx_ref)
            small_ref[...] = jnp.zeros_like(small_ref)

        x = xr_ref[...]
        xprev = jnp.where(first, 0.0, xh_ref[...])
        taps = _conv_taps(jnp.concatenate([xprev, x], axis=0))
        xc = cb_ref[...]
        for k in range(CONV_WIDTH):
            xc = xc + cw_ref[k:k + 1, :] * taps[k]
        xcb, r, ig, ls, log_a = _rec_gates(xc, wa_ref, ba_ref, wx_ref, bx_ref, l_ref)
        a = jnp.exp(log_a)
        a2 = a * a
        mult = jnp.sqrt(_one_minus_exp(2.0 * log_a, a2))
        g, dg = _gelu_and_grad(yr_ref[...])
        hr_v = hr_ref[...]
        drec_v = drec_ref[...]
        dhr_s[...] = drec_v * g
        dyr_ref[...] = (drec_v * hr_v * dg).astype(BF16)
        a_s[...] = a

        def tile(jj, carry):
            r0 = pl.multiple_of((r_rows // SUBLANES - 1 - jj) * SUBLANES, SUBLANES)
            at = a_s[pl.ds(r0, SUBLANES), :]
            dt = dhr_s[pl.ds(r0, SUBLANES), :]
            out = [None] * SUBLANES
            for rr in range(SUBLANES - 1, -1, -1):
                lam = dt[rr:rr + 1] + carry
                out[rr] = lam
                carry = at[rr:rr + 1] * lam
            lam_s[pl.ds(r0, SUBLANES), :] = jnp.concatenate(out, axis=0)
            return carry

        carry_s[0:1, :] = lax.fori_loop(0, r_rows // SUBLANES, tile, carry_s[0:1, :])
        lam = lam_s[...]
        hprev = jnp.where(first, 0.0, hh_ref[...])
        hr_prev = pltpu.roll(jnp.concatenate([hprev, hr_v], axis=0), 1, 0)[HALO:]
        da = lam * hr_prev
        dxc = lam * mult * ig
        di = lam * mult * xc
        dmult = lam * ig * xc
        dlog_a = da * a - dmult * a2 / mult
        dr = dlog_a * (RG_C * ls)
        dls = jnp.sum(dlog_a * (RG_C * r), axis=0, keepdims=True)
        dga = dr * r * (1.0 - r)
        dgx = di * ig * (1.0 - ig)
        dgab = dga.astype(BF16)
        dgxb = dgx.astype(BF16)
        dxc = dxc + _dot_nt(dgab, wa_ref[...]) + _dot_nt(dgxb, wx_ref[...])
        dwa_ref[...] += _dot_tn(xcb, dgab)
        dwx_ref[...] += _dot_tn(xcb, dgxb)
        cat = jnp.concatenate([dxc, next_s[...]], axis=0)
        next_s[...] = dxc[0:HALO]
        dxr = cw_ref[CONV_WIDTH - 1:CONV_WIDTH, :] * dxc
        for k in range(CONV_WIDTH - 1):
            sh = CONV_WIDTH - 1 - k
            dxr = dxr + cw_ref[k:k + 1, :] * pltpu.roll(cat, r_rows + HALO - sh, 0)[:r_rows]
        dxr_ref[...] = dxr.astype(BF16)
        rows = [jnp.sum(dxc * taps[k], axis=0, keepdims=True) for k in range(CONV_WIDTH)]
        rows += [jnp.sum(dxc, axis=0, keepdims=True), jnp.sum(dga, axis=0, keepdims=True),
                 jnp.sum(dgx, axis=0, keepdims=True), dls * _sigmoid(-l_ref[...])]
        small_ref[...] += jnp.concatenate(rows, axis=0)

    def rev(i):
        return nc - 1 - i

    def halo(i):
        return jnp.maximum(rev(i) * hpc - 1, 0)

    blk = pl.BlockSpec((r_rows, w), lambda i: (rev(i), 0))
    vm = 40 * _nbytes((r_rows, w), F32) + 6 * _nbytes((w, w), F32)
    return _call(body, (proj, proj, proj, hr, hr, drec, conv_w, conv_b, wa, ba, wx, bx, lru),
                 name="rec_bwd", grid=(nc,),
                 in_specs=[pl.BlockSpec((r_rows, w), lambda i: (rev(i), xr_blk)),
                           pl.BlockSpec((HALO, w), lambda i: (halo(i), xr_blk)),
                           pl.BlockSpec((r_rows, w), lambda i: (rev(i), yr_blk)),
                           blk,
                           pl.BlockSpec((HALO, w), lambda i: (halo(i), 0)),
                           blk,
                           _full((CONV_WIDTH, w)), _full((1, w)), _full((w, w)), _full((1, w)),
                           _full((w, w)), _full((1, w)), _full((1, w))],
                 out_specs=[blk, blk, _full((w, w)), _full((w, w)), _full((SUBLANES, w))],
                 out_shape=[jax.ShapeDtypeStruct((tp, w), BF16)] * 2
                 + [jax.ShapeDtypeStruct((w, w), F32)] * 2 + [jax.ShapeDtypeStruct((SUBLANES, w), F32)],
                 scratch_shapes=[pltpu.VMEM((r_rows, w), F32)] * 3
                 + [pltpu.VMEM((SUBLANES, w), F32), pltpu.VMEM((HALO, w), F32)],
                 semantics=("arbitrary",), vmem_bytes=vm)


ROW_TARGET = 544


def _mixer_out(attn, rec, g_a, g_r, w_out, h, g_next):
    tp, d = h.shape
    aw, rw = attn.shape[1], rec.shape[1]
    kc = d // N_CHIPS
    tm = _divisor_tile(tp, 16, ROW_TARGET)

    def body(a_ref, r_ref, ga_ref, gr_ref, w_ref, h_ref, gn_ref, h1_ref, z_ref, mix_ref):
        a = a_ref[...]
        r = r_ref[...]
        mix = jnp.concatenate([a * _rstd(a) * ga_ref[...], r * _rstd(r) * gr_ref[...]], axis=1).astype(BF16)
        mix_ref[...] = mix
        h1 = h_ref[...]
        for j in range(N_CHIPS):
            h1 = h1 + _dot(mix[:, j * kc:(j + 1) * kc], w_ref[j])
        h1_ref[...] = h1
        z_ref[...] = (h1 * _rstd(h1) * gn_ref[...]).astype(BF16)

    row = lambda wd: pl.BlockSpec((tm, wd), lambda i: (i, 0))
    vm = 2 * _nbytes((d, d), BF16) + 12 * _nbytes((tm, d), F32)
    return _call(body, (attn, rec, g_a, g_r, w_out, h, g_next), name="mixer_out", grid=(tp // tm,),
                 in_specs=[row(aw), row(rw), _full((1, aw)), _full((1, rw)), _full(w_out.shape), row(d),
                           _full((1, d))],
                 out_specs=[row(d), row(d), row(d)],
                 out_shape=[jax.ShapeDtypeStruct((tp, d), F32), jax.ShapeDtypeStruct((tp, d), BF16),
                            jax.ShapeDtypeStruct((tp, d), BF16)],
                 semantics=("parallel",), vmem_bytes=vm)


def _mixer_bwd(dh_b, w_out, attn, rec, g_a, g_r):
    tp, d = dh_b.shape
    aw, rw = attn.shape[1], rec.shape[1]
    tm = _divisor_tile(tp, 16, ROW_TARGET)

    def body(dh_ref, w_ref, a_ref, r_ref, ga_ref, gr_ref, da_ref, dr_ref, dg_ref):
        @pl.when(pl.program_id(0) == 0)
        def _():
            dg_ref[...] = jnp.zeros_like(dg_ref)

        dh = dh_ref[...]
        dmix = jnp.concatenate([_dot_nt(dh, w_ref[j]) for j in range(N_CHIPS)], axis=1)
        da, dga = _rms_bwd(dmix[:, :aw], a_ref[...], ga_ref[...])
        dr, dgr = _rms_bwd(dmix[:, aw:], r_ref[...], gr_ref[...])
        da_ref[...] = da
        dr_ref[...] = dr
        dg_ref[...] += jnp.broadcast_to(jnp.concatenate([dga, dgr], axis=1), (SUBLANES, d))

    row = lambda wd: pl.BlockSpec((tm, wd), lambda i: (i, 0))
    vm = 2 * _nbytes((d, d), BF16) + 12 * _nbytes((tm, d), F32)
    return _call(body, (dh_b, w_out, attn, rec, g_a, g_r), name="mixer_bwd", grid=(tp // tm,),
                 in_specs=[row(d), _full(w_out.shape), row(aw), row(rw), _full((1, aw)), _full((1, rw))],
                 out_specs=[row(aw), row(rw), _full((SUBLANES, d))],
                 out_shape=[jax.ShapeDtypeStruct((tp, aw), F32), jax.ShapeDtypeStruct((tp, rw), F32),
                            jax.ShapeDtypeStruct((SUBLANES, d), F32)],
                 semantics=("arbitrary",), vmem_bytes=vm)


def _mlp_up(z, w_up):
    tp, d = z.shape
    fc = w_up.shape[2]
    ff = N_CHIPS * fc
    tn = _divisor_tile(fc, LANES, 512)
    per = fc // tn

    def body(z_ref, w_ref, act_ref, up_ref):
        up = _dot(z_ref[...], w_ref[...])
        r = jnp.maximum(up, 0.0)
        act_ref[...] = (r * r).astype(BF16)
        up_ref[...] = up.astype(BF16)

    col = pl.BlockSpec((tp, tn), lambda j: (0, j))
    vm = 2 * _nbytes((tp, d), BF16) + 2 * _nbytes((d, tn), BF16) + 8 * _nbytes((tp, tn), F32)
    return _call(body, (z, w_up), name="mlp_up", grid=(ff // tn,),
                 in_specs=[_full((tp, d)), pl.BlockSpec((None, d, tn), lambda j: (j // per, 0, j % per))],
                 out_specs=[col, col],
                 out_shape=[jax.ShapeDtypeStruct((tp, ff), BF16)] * 2,
                 semantics=("parallel",), vmem_bytes=vm)


def _mlp_down(act, w_down, h, g_next):
    tp, d = h.shape
    ff = act.shape[1]
    fc = ff // N_CHIPS
    tm = _divisor_tile(tp, 16, ROW_TARGET)

    def body(a_ref, w_ref, h_ref, gn_ref, h2_ref, z_ref):
        h2 = h_ref[...]
        for j in range(N_CHIPS):
            h2 = h2 + _dot(a_ref[:, j * fc:(j + 1) * fc], w_ref[j])
        h2_ref[...] = h2
        z_ref[...] = (h2 * _rstd(h2) * gn_ref[...]).astype(BF16)

    row = lambda wd: pl.BlockSpec((tm, wd), lambda i: (i, 0))
    vm = 2 * _nbytes((ff, d), BF16) + 2 * _nbytes((tm, ff), BF16) + 10 * _nbytes((tm, d), F32)
    return _call(body, (act, w_down, h, g_next), name="mlp_down", grid=(tp // tm,),
                 in_specs=[row(ff), _full(w_down.shape), row(d), _full((1, d))],
                 out_specs=[row(d), row(d)],
                 out_shape=[jax.ShapeDtypeStruct((tp, d), F32), jax.ShapeDtypeStruct((tp, d), BF16)],
                 semantics=("parallel",), vmem_bytes=vm)


def _loss_bwd(h, g, target, n_real):
    tp, d = h.shape
    tm = _divisor_tile(tp, 16, ROW_TARGET)

    def body(h_ref, g_ref, t_ref, dh_ref, dhb_ref, dg_ref, loss_ref):
        i = pl.program_id(0)

        @pl.when(i == 0)
        def _():
            dg_ref[...] = jnp.zeros_like(dg_ref)
            loss_ref[...] = jnp.zeros_like(loss_ref)

        x = h_ref[...]
        gv = g_ref[...]
        rowi = i * tm + lax.broadcasted_iota(jnp.int32, (tm, 1), 0)
        real = jnp.logical_and(rowi >= N_META, rowi < N_META + n_real)
        err = jnp.where(real, x * _rstd(x) * gv - t_ref[...], 0.0)
        loss_ref[...] += 0.5 * jnp.sum(jnp.mean(err * err, axis=-1, keepdims=True))
        dx, dgp = _rms_bwd(err * (1.0 / d), x, gv)
        dh_ref[...] = dx
        dhb_ref[...] = dx.astype(BF16)
        dg_ref[...] += jnp.broadcast_to(dgp, (SUBLANES, d))

    row = pl.BlockSpec((tm, d), lambda i: (i, 0))
    return _call(body, (h, g, target), name="loss_bwd", grid=(tp // tm,),
                 in_specs=[row, _full((1, d)), row],
                 out_specs=[row, row, _full((SUBLANES, d)), _full((SUBLANES, LANES))],
                 out_shape=[jax.ShapeDtypeStruct((tp, d), F32), jax.ShapeDtypeStruct((tp, d), BF16),
                            jax.ShapeDtypeStruct((SUBLANES, d), F32), jax.ShapeDtypeStruct((SUBLANES, LANES), F32)],
                 semantics=("arbitrary",), vmem_bytes=16 * _nbytes((tm, d), F32))


def _mlp_dup(dh_b, w_down, up):
    tp, d = dh_b.shape
    fc = w_down.shape[1]
    ff = N_CHIPS * fc
    tn = _divisor_tile(fc, LANES, 512)
    per = fc // tn

    def body(dh_ref, w_ref, up_ref, dup_ref):
        dact = _dot_nt(dh_ref[...], w_ref[...])
        dup_ref[...] = (dact * (2.0 * jnp.maximum(up_ref[...].astype(F32), 0.0))).astype(BF16)

    col = pl.BlockSpec((tp, tn), lambda j: (0, j))
    vm = 2 * _nbytes((tp, d), BF16) + 2 * _nbytes((tn, d), BF16) + 8 * _nbytes((tp, tn), F32)
    return _call(body, (dh_b, w_down, up), name="mlp_dup", grid=(ff // tn,),
                 in_specs=[_full((tp, d)), pl.BlockSpec((None, tn, d), lambda j: (j // per, j % per, 0)),
                           col],
                 out_specs=col, out_shape=jax.ShapeDtypeStruct((tp, ff), BF16),
                 semantics=("parallel",), vmem_bytes=vm)


def _grad_w(a, b, cols_to_chips=False):
    tp, k = a.shape
    n = b.shape[1]
    tk = _divisor_tile(k, LANES, 1024)
    nc = n // N_CHIPS if cols_to_chips else n
    tn = _divisor_tile(nc, LANES, 512)
    per = nc // tn

    def body(a_ref, b_ref, o_ref):
        o_ref[...] = _dot_tn(a_ref[...], b_ref[...]).astype(BF16)

    if cols_to_chips:
        out_spec = pl.BlockSpec((None, tk, tn), lambda i, j: (j // per, i, j % per))
        out_shape = jax.ShapeDtypeStruct((N_CHIPS, k, nc), BF16)
    else:
        out_spec = pl.BlockSpec((tk, tn), lambda i, j: (i, j))
        out_shape = jax.ShapeDtypeStruct((k, n), BF16)
    vm = 2 * _nbytes((tp, tk), BF16) + 2 * _nbytes((tp, tn), BF16) + 6 * _nbytes((tk, tn), F32) \
        + 2 * _nbytes((tp, tk), F32)
    return _call(body, (a, b), name="grad_w", grid=(k // tk, n // tn),
                 in_specs=[pl.BlockSpec((tp, tk), lambda i, j: (0, i)), pl.BlockSpec((tp, tn), lambda i, j: (0, j))],
                 out_specs=out_spec, out_shape=out_shape,
                 semantics=("parallel", "parallel"), vmem_bytes=vm)


def _grad_w_pieces(pieces, b):
    tp, n = b.shape
    tn = _divisor_tile(n, LANES, 512)
    widths = [pc.shape[1] for pc in pieces]

    def body(*refs):
        p_refs, b_ref, o_refs = refs[:len(pieces)], refs[len(pieces)], refs[len(pieces) + 1:]
        for p_ref, o_ref in zip(p_refs, o_refs):
            o_ref[...] = _dot_tn(p_ref[...], b_ref[...]).astype(BF16)

    vm = 2 * sum(_nbytes((tp, wd), BF16) for wd in widths) + 2 * _nbytes((tp, tn), BF16) \
        + 4 * sum(_nbytes((wd, tn), F32) for wd in widths) + 2 * _nbytes((tp, max(widths)), F32)
    return _call(body, tuple(pieces) + (b,), name="grad_w_pieces", grid=(n // tn,),
                 in_specs=[_full(pc.shape) for pc in pieces] + [pl.BlockSpec((tp, tn), lambda j: (0, j))],
                 out_specs=[pl.BlockSpec((wd, tn), lambda j: (0, j)) for wd in widths],
                 out_shape=[jax.ShapeDtypeStruct((wd, n), BF16) for wd in widths],
                 semantics=("parallel",), vmem_bytes=vm)


def _dx_norm_bwd(pieces, w, w_spec, w_piece, h, g, dres, dot=_dot_nt):
    tp, d = h.shape
    tm = _divisor_tile(tp, 16, ROW_TARGET)
    n = len(pieces)

    def body(*refs):
        dy_refs = refs[:n]
        w_ref, h_ref, g_ref, dres_ref, dh_ref, dhb_ref, dg_ref = refs[n:]

        @pl.when(pl.program_id(0) == 0)
        def _():
            dg_ref[...] = jnp.zeros_like(dg_ref)

        dz = dot(dy_refs[0][...], w_piece(w_ref, 0))
        for i in range(1, n):
            dz = dz + dot(dy_refs[i][...], w_piece(w_ref, i))
        dx, dgp = _rms_bwd(dz, h_ref[...], g_ref[...])
        dh = dres_ref[...] + dx
        dh_ref[...] = dh
        dhb_ref[...] = dh.astype(BF16)
        dg_ref[...] += jnp.broadcast_to(dgp, (SUBLANES, d))

    row = lambda wd: pl.BlockSpec((tm, wd), lambda i: (i, 0))
    kk = sum(wd for _, _, wd in pieces)
    vm = 2 * _nbytes((d, kk), BF16) + 2 * _nbytes((tm, kk), BF16) + 14 * _nbytes((tm, d), F32)
    piece_specs = [pl.BlockSpec((tm, wd), functools.partial(lambda i, cb: (i, cb), cb=cb)) for _, cb, wd in pieces]
    return _call(body, tuple(a for a, _, _ in pieces) + (w, h, g, dres), name="dx_norm_bwd", grid=(tp // tm,),
                 in_specs=piece_specs + [w_spec, row(d), _full((1, d)), row(d)],
                 out_specs=[row(d), row(d), _full((SUBLANES, d))],
                 out_shape=[jax.ShapeDtypeStruct((tp, d), F32), jax.ShapeDtypeStruct((tp, d), BF16),
                            jax.ShapeDtypeStruct((SUBLANES, d), F32)],
                 semantics=("arbitrary",), vmem_bytes=vm)


def _block_diag(wg):
    nb, b, _ = wg.shape
    eye = jnp.eye(nb, dtype=wg.dtype)
    return (eye[:, None, :, None] * wg[:, :, None, :]).reshape(nb * b, nb * b)


def _diag_blocks(dense, nb):
    b = dense.shape[0] // nb
    d4 = dense.reshape(nb, b, nb, b)
    return jnp.stack([d4[i, :, i, :] for i in range(nb)])


def _row(v):
    return v.reshape(1, -1)


def _forward_layer(l, h, z, p, fetch, g_next):
    d = h.shape[1]
    att_w = d // 2
    rec_w = d - att_w
    nh = att_w // HEAD_DIM
    xr_blk = 3 * att_w // rec_w
    wa_d = _block_diag(p["w_gate_a"][l]).astype(BF16)
    wx_d = _block_diag(p["w_gate_x"][l]).astype(BF16)
    b_f_pad = jnp.zeros((1, LANES), F32).at[0, :nh].set(p["b_f"][l])
    w_in_t = fetch("w_in", h)
    big = dict(w_in_big=_pack_w_in_t(w_in_t.reshape(-1, d), att_w, nh))
    proj, qkv = _proj(z, big["w_in_big"], att_w)
    c, c_t = _fgate_fwd(proj, b_f_pad, nh)
    attn, lse_b = _attn_fwd(qkv, c, c_t, nh)
    hr, rec = _rec_fwd(proj, xr_blk, xr_blk + 1, rec_w, p["conv_w"][l], _row(p["conv_b"][l]), wa_d,
                       _row(p["b_gate_a"][l]), wx_d, _row(p["b_gate_x"][l]), _row(p["lru_L"][l]))
    big["w_out"] = fetch("w_out", rec)
    h1, z2, mix = _mixer_out(attn, rec, _row(p["attn_out_g"][l]), _row(p["rec_out_g"][l]),
                             big["w_out"], h, _row(p["mlp_norm_g"][l]))
    big["w_up"] = fetch("w_up", h1)
    act, up = _mlp_up(z2, big["w_up"])
    big["w_down"] = fetch("w_down", act)
    h2, z_next = _mlp_down(act, big["w_down"], h1, _row(g_next))
    saved = dict(h0=h, z1=z, proj=proj, qkv=qkv, c=c, c_t=c_t, attn=attn, lse_b=lse_b, hr=hr, rec=rec, h1=h1,
                 z2=z2, mix=mix, act=act, up=up, wa_d=wa_d, wx_d=wx_d, b_f_pad=b_f_pad, big=big)
    return h2, z_next, saved


def _backward_mlp(l, dh, dh_b, sv, p, tok):
    w_up, w_down = sv["big"]["w_up"], sv["big"]["w_down"]
    fc = w_up.shape[2]
    dup = _mlp_dup(dh_b, w_down, sv["up"])
    g_down = _grad_w(sv["act"], dh_b)
    g_up = _grad_w(sv["z2"], dup, cols_to_chips=True)
    dh, dh_b, dg2 = _dx_norm_bwd([(dup, j, fc) for j in range(N_CHIPS)], w_up, _full(w_up.shape),
                                 lambda w_ref, j: w_ref[j], sv["h1"], _row(p["mlp_norm_g"][l] + tok), dh)
    big = dict(w_down=g_down.reshape((N_CHIPS, -1) + g_down.shape[1:]), w_up=g_up)
    return dh, dh_b, big, dict(mlp_norm_g=dg2[0])


def _backward_mixer(l, dh, dh_b, sv, p, tok):
    d = dh.shape[1]
    att_w = d // 2
    rec_w = d - att_w
    nh = att_w // HEAD_DIM
    xr_blk = 3 * att_w // rec_w
    small = {}
    g_out = _grad_w(sv["mix"], dh_b)
    dattn, drec, dg_mix = _mixer_bwd(dh_b, sv["big"]["w_out"], sv["attn"], sv["rec"],
                                     _row(p["attn_out_g"][l] + tok), _row(p["rec_out_g"][l]))
    small["attn_out_g"] = dg_mix[0, :att_w]
    small["rec_out_g"] = dg_mix[0, att_w:]
    dxr, dyr, dwa, dwx, sm = _rec_bwd(
        sv["proj"], xr_blk, xr_blk + 1, rec_w, sv["hr"], drec, p["conv_w"][l], _row(p["conv_b"][l]), sv["wa_d"],
        _row(p["b_gate_a"][l]), sv["wx_d"], _row(p["b_gate_x"][l]), _row(p["lru_L"][l]))
    small.update(conv_w=sm[:CONV_WIDTH], conv_b=sm[4], b_gate_a=sm[5], b_gate_x=sm[6], lru_L=sm[7],
                 w_gate_a=_diag_blocks(dwa, N_REC_BLOCKS), w_gate_x=_diag_blocks(dwx, N_REC_BLOCKS))
    dq, dk, dv, dc = _attn_bwd(sv["qkv"], sv["c"], sv["c_t"], sv["lse_b"], dattn, nh)
    df, db_f = _fgate_bwd(sv["proj"], sv["b_f_pad"], dc)
    small["b_f"] = db_f[0, :nh]
    pieces = [dq, dk, dv, dxr, dyr, df]
    offs = [0, att_w, 2 * att_w, 3 * att_w, 3 * att_w + rec_w, 3 * att_w + 2 * rec_w]
    gq, gk, gv, gxr, gyr, gf = _grad_w_pieces(pieces, sv["z1"])
    g_in_t = jnp.concatenate([gq, gk, gv, gf[:nh], gxr, gyr], axis=0)
    w_big = sv["big"]["w_in_big"]
    widths = [pc.shape[1] for pc in pieces]
    dh, dh_b, dg1 = _dx_norm_bwd(
        [(pc, 0, wd) for pc, wd in zip(pieces, widths)], w_big, _full(w_big.shape),
        lambda w_ref, i: w_ref[offs[i]:offs[i] + widths[i], :], sv["h0"], _row(p["attn_norm_g"][l]), dh, dot=_dot)
    small["attn_norm_g"] = dg1[0]
    big = dict(w_in=g_in_t.reshape(N_CHIPS, -1, d), w_out=g_out.reshape((N_CHIPS, -1) + g_out.shape[1:]))
    return dh, dh_b, big, small


def _pack_w_in_t(w_in_t, att_w, nh):
    qkv = w_in_t[:3 * att_w]
    f = w_in_t[3 * att_w:3 * att_w + nh]
    xy = w_in_t[3 * att_w + nh:]
    return jnp.concatenate([qkv, xy, f, jnp.zeros((LANES - nh, w_in_t.shape[1]), w_in_t.dtype)], axis=0)


ANY = pl.BlockSpec(memory_space=pl.ANY)


def _coords():
    return lax.axis_index("x"), lax.axis_index("y"), lax.axis_index("c")


def _other_chips(x, y):
    return [(1 - x, y), (x, 1 - y), (1 - x, 1 - y)]


def _remote(src, dst, send_sems, recv_sems, k, to):
    return pltpu.make_async_remote_copy(src_ref=src, dst_ref=dst, send_sem=send_sems.at[k],
                                        recv_sem=recv_sems.at[k], device_id=to, device_id_type=MESH)


def _all_gather_chips(shards):
    n = len(shards)
    per = 6

    def body(*refs):
        ins, outs = refs[:n], refs[n:2 * n]
        send_sems, recv_sems, local_sems = refs[2 * n:]
        x, y, c = _coords()
        me = 2 * x + y
        sibling = (x, y, 1 - c)
        chips = _other_chips(x, y)
        local = [pltpu.make_async_copy(ins[t], outs[t].at[me], local_sems.at[t]) for t in range(n)]
        for cp in local:
            cp.start()
        sends = []
        for t in range(n):
            for j, (px, py) in enumerate(chips):
                cp = _remote(ins[t].at[c], outs[t].at[me, c], send_sems, recv_sems, per * t + j, (px, py, c))
                cp.start()
                sends.append(cp)
        for t in range(n):
            for j, (px, py) in enumerate(chips):
                landed = outs[t].at[2 * px + py, c]
                _remote(landed, landed, send_sems, recv_sems, per * t + j, (px, py, c)).wait_recv()
                cp = _remote(landed, landed, send_sems, recv_sems, per * t + 3 + j, sibling)
                cp.start()
                sends.append(cp)
        for t in range(n):
            for j, (px, py) in enumerate(chips):
                passed = outs[t].at[2 * px + py, 1 - c]
                _remote(passed, passed, send_sems, recv_sems, per * t + 3 + j, sibling).wait_recv()
        for cp in sends:
            cp.wait_send()
        for cp in local:
            cp.wait()

    return _call(body, tuple(shards), name="all_gather_chips",
                 in_specs=[ANY] * n, out_specs=[ANY] * n,
                 out_shape=[jax.ShapeDtypeStruct((N_CHIPS,) + s.shape, s.dtype) for s in shards],
                 scratch_shapes=[pltpu.SemaphoreType.DMA((per * n,)), pltpu.SemaphoreType.DMA((per * n,)),
                                 pltpu.SemaphoreType.DMA((n,))])


HBM = pl.BlockSpec(memory_space=pltpu.HBM)
SEM = pl.BlockSpec(memory_space=pltpu.SEMAPHORE)
DATAFLOW = pltpu.SideEffectType.DATAFLOW_SIDE_EFFECTING


def _in_hbm(a):
    return pltpu.with_memory_space_constraint(a, pltpu.HBM)


PUSH_ARRIVALS = {"gather_chips": N_CHIPS - 1, "scatter_chips": N_CHIPS - 1, "sibling": 1, "gather_devices": N_DEV - 1}


def _push_copies(mode, src, land, send_sems, recv_sems, t):
    x, y, c = _coords()
    chip = 2 * x + y
    if mode == "gather_chips":
        return [_remote(src.at[chip], land.at[chip], send_sems, recv_sems, t, (px, py, c))
                for px, py in _other_chips(x, y)]
    if mode == "scatter_chips":
        return [_remote(src.at[2 * px + py], land.at[chip], send_sems, recv_sems, t, (px, py, c))
                for px, py in _other_chips(x, y)]
    if mode == "sibling":
        return [_remote(src, land, send_sems, recv_sems, t, (x, y, 1 - c))]
    dev = 4 * x + 2 * y + c
    return [_remote(src.at[dev], land.at[dev], send_sems, recv_sems, t, (x ^ (k >> 2), y ^ ((k >> 1) & 1), c ^ (k & 1)))
            for k in range(1, N_DEV)]


def _push_start(srcs, lands, mode, name):
    n = len(srcs)
    same = all(s is ld for s, ld in zip(srcs, lands))
    n_in = n if same else 2 * n

    def body(*refs):
        src_refs = refs[:n]
        land_refs = src_refs if same else refs[n:2 * n]
        send_sems, recv_sems = refs[n_in], refs[n_in + 1]
        token = refs[-1]
        for t in range(n):
            for cp in _push_copies(mode, src_refs[t], land_refs[t], send_sems, recv_sems, t):
                cp.start()
        token[...] = jnp.zeros_like(token)

    operands = tuple(srcs) if same else tuple(srcs) + tuple(lands)
    res = _call(
        body, [_in_hbm(a) for a in operands], name=name,
        out_shape=(pltpu.SemaphoreType.DMA((n,)), pltpu.SemaphoreType.DMA((n,)))
        + tuple(pltpu.HBM(a.shape, a.dtype) for a in operands) + (jax.ShapeDtypeStruct((SUBLANES, LANES), F32),),
        in_specs=[HBM] * n_in, out_specs=(SEM, SEM) + (HBM,) * n_in + (pl.BlockSpec(memory_space=pltpu.VMEM),),
        input_output_aliases={i: 2 + i for i in range(n_in)}, side_effects=DATAFLOW, hbm_results=False)
    send_sems, recv_sems, token = res[0], res[1], res[-1]
    srcs_thru = res[2:2 + n]
    lands_thru = srcs_thru if same else res[2 + n:2 + 2 * n]
    return send_sems, recv_sems, srcs_thru, lands_thru, token


def _push_wait(send_sems, recv_sems, ids, srcs, lands, mode, after, name):
    n = len(lands)
    same = all(s is ld for s, ld in zip(srcs, lands))
    n_in = n if same else 2 * n

    def body(*refs):
        land_refs = refs[:n] if same else refs[n:2 * n]
        send_sems, recv_sems = refs[n_in], refs[n_in + 1]
        x, y, c = _coords()
        for t in range(n):
            moved = land_refs[t] if mode == "sibling" else land_refs[t].at[pl.ds(0, PUSH_ARRIVALS[mode])]
            arrivals = _remote(moved, moved, send_sems, recv_sems, ids[t], (x, y, c))
            arrivals.wait_send()
            arrivals.wait_recv()

    operands = tuple(lands) if same else tuple(srcs) + tuple(lands)
    res = _call(
        body, operands + (send_sems, recv_sems, after), name=name,
        out_shape=tuple(pltpu.HBM(a.shape, a.dtype) for a in operands),
        in_specs=[HBM] * n_in + [SEM, SEM, ANY], out_specs=(HBM,) * n_in,
        input_output_aliases={i: i for i in range(n_in)}, side_effects=DATAFLOW)
    return list(res) if same else (list(res[:n]), list(res[n:]))


def _sum_partials(part, landed, chip):
    _, rows, cols = part.shape
    br = _divisor_tile(rows, 16, ELEM_ROWS)

    def body(chip_ref, own_ref, a_ref, b_ref, c_ref, o_ref):
        o_ref[...] = ((own_ref[...].astype(F32) + a_ref[...].astype(F32)) + b_ref[...].astype(F32)) \
            + c_ref[...].astype(F32)

    def other(k):
        return pl.BlockSpec((None, br, cols), lambda i, ch: (jnp.where(ch[0] <= k, k + 1, k), i, 0))

    spec = pltpu.PrefetchScalarGridSpec(
        num_scalar_prefetch=1, grid=(rows // br,),
        in_specs=[pl.BlockSpec((None, br, cols), lambda i, ch: (ch[0], i, 0)), other(0), other(1), other(2)],
        out_specs=pl.BlockSpec((br, cols), lambda i, ch: (i, 0)))
    return _call(body, (chip, part, landed, landed, landed), name="sum_partials", grid_spec=spec,
                 out_shape=jax.ShapeDtypeStruct((rows, cols), F32), semantics=("parallel",))


def _cast_to_slab(w, l, chip):
    _, rows, cols = w.shape
    br = _divisor_tile(rows, 16, ELEM_ROWS)

    def body(chip_ref, w_ref, o_ref):
        o_ref[...] = w_ref[...].astype(BF16)

    spec = pltpu.PrefetchScalarGridSpec(
        num_scalar_prefetch=1, grid=(rows // br,),
        in_specs=[pl.BlockSpec((None, br, cols), lambda i, ch: (l, i, 0))],
        out_specs=pl.BlockSpec((None, br, cols), lambda i, ch: (ch[0], i, 0)))
    return _call(body, (chip, w), name="cast_to_slab", grid_spec=spec,
                 out_shape=jax.ShapeDtypeStruct((N_CHIPS, rows, cols), BF16), semantics=("parallel",))


def _cast_w_in_t_to_slabs(w_t, chip):
    rows, depth, d = w_t.shape
    tn = _divisor_tile(d, LANES, 256)

    def body(chip_ref, w_ref, *o_refs):
        for l in range(depth):
            o_refs[l][...] = w_ref[:, l, :].astype(BF16)

    spec = pltpu.PrefetchScalarGridSpec(
        num_scalar_prefetch=1, grid=(d // tn,),
        in_specs=[pl.BlockSpec((rows, depth, tn), lambda j, ch: (0, 0, j))],
        out_specs=[pl.BlockSpec((None, rows, tn), lambda j, ch: (ch[0], 0, j))] * depth)
    return _call(body, (chip, w_t), name="cast_w_in_t_to_slabs", grid_spec=spec,
                 out_shape=[jax.ShapeDtypeStruct((N_CHIPS, rows, d), BF16)] * depth, semantics=("parallel",),
                 vmem_bytes=4 * _nbytes((rows, max(depth, SUBLANES), tn), F32))


def _place_slab(buf, index, n_slabs):
    rows, cols = buf.shape
    br = _divisor_tile(rows, SUBLANES, ELEM_ROWS)

    def body(index_ref, b_ref, o_ref):
        o_ref[...] = b_ref[...]

    spec = pltpu.PrefetchScalarGridSpec(
        num_scalar_prefetch=1, grid=(rows // br,),
        in_specs=[pl.BlockSpec((br, cols), lambda i, ix: (i, 0))],
        out_specs=pl.BlockSpec((None, br, cols), lambda i, ix: (ix[0], i, 0)))
    return _call(body, (index, buf), name="place_slab", grid_spec=spec,
                 out_shape=jax.ShapeDtypeStruct((n_slabs, rows, cols), buf.dtype), semantics=("parallel",))


ELEM_ROWS = 256


def _sum_slabs(r):
    n, rows, cols = r.shape
    br = _divisor_tile(rows, 16, ELEM_ROWS)

    def body(r_ref, o_ref):
        acc = r_ref[0].astype(F32)
        for j in range(1, n):
            acc = acc + r_ref[j].astype(F32)
        o_ref[...] = acc

    return _call(body, (r,), name="sum_slabs", grid=(rows // br,),
                 in_specs=[pl.BlockSpec((n, br, cols), lambda i: (0, i, 0))],
                 out_specs=pl.BlockSpec((br, cols), lambda i: (i, 0)),
                 out_shape=jax.ShapeDtypeStruct((rows, cols), F32), semantics=("parallel",))


def _adamw_math(w, g, m, v):
    c1 = 1.0 - ADAM_B1 ** ADAM_STEP
    c2 = 1.0 - ADAM_B2 ** ADAM_STEP
    nm = ADAM_B1 * m + (1.0 - ADAM_B1) * g
    nv = ADAM_B2 * v + (1.0 - ADAM_B2) * (g * g)
    delta = -ADAM_LR * ((nm / c1) / (jnp.sqrt(nv / c2) + ADAM_EPS) + ADAM_WD * w)
    return delta, nm, nv


def _adamw(w, g, m, v):
    rows, cols = w.shape
    br = _divisor_tile(rows, 8, ELEM_ROWS)

    def body(w_ref, g_ref, m_ref, v_ref, d_ref, nm_ref, nv_ref):
        d_ref[...], nm_ref[...], nv_ref[...] = _adamw_math(w_ref[...], g_ref[...], m_ref[...], v_ref[...])

    blk = pl.BlockSpec((br, cols), lambda i: (i, 0))
    return _call(body, (w, g, m, v), name="adamw", grid=(rows // br,),
                 in_specs=[blk] * 4, out_specs=[blk] * 3,
                 out_shape=[jax.ShapeDtypeStruct((rows, cols), F32)] * 3, semantics=("parallel",))


def _adamw_w_in_t(w_t, m_t, v_t, g_mine, g_theirs):
    rows, depth, d = w_t.shape
    tn = LANES

    def body(w_ref, m_ref, v_ref, ga_ref, gb_ref, g_ref, d_ref, nm_ref, nv_ref):
        g = ga_ref[...] + gb_ref[...]
        g_ref[...] = g
        d_ref[...], nm_ref[...], nv_ref[...] = _adamw_math(w_ref[...], g, m_ref[...], v_ref[...])

    slab = pl.BlockSpec((rows, depth, tn), lambda j: (0, 0, j))
    return _call(body, (w_t, m_t, v_t, g_mine, g_theirs), name="adamw_w_in_t", grid=(d // tn,),
                 in_specs=[slab] * 5, out_specs=[slab] * 4,
                 out_shape=[jax.ShapeDtypeStruct(w_t.shape, F32)] * 4, semantics=("parallel",),
                 vmem_bytes=2 * 9 * _nbytes((rows, max(depth, SUBLANES), tn), F32))


def _adamw_layer(w, m, v, l, g_mine, g_theirs, prev):
    _, rows, cols = w.shape
    br = _divisor_tile(rows, 8, ELEM_ROWS)

    def body(w_ref, m_ref, v_ref, ga_ref, gb_ref, *rest):
        g_ref, d_ref, nm_ref, nv_ref = rest[4:]
        g = ga_ref[...] + gb_ref[...]
        g_ref[...] = g
        d_ref[...], nm_ref[...], nv_ref[...] = _adamw_math(w_ref[...], g, m_ref[...], v_ref[...])

    slot = pl.BlockSpec((None, br, cols), lambda i: (l, i, 0))
    blk = pl.BlockSpec((br, cols), lambda i: (i, 0))
    return _call(body, (w, m, v, g_mine, g_theirs) + tuple(prev), name="adamw_layer", grid=(rows // br,),
                 in_specs=[slot] * 3 + [blk] * 2 + [ANY] * 4, out_specs=[slot] * 4,
                 out_shape=[jax.ShapeDtypeStruct(w.shape, F32)] * 4,
                 input_output_aliases={5: 0, 6: 1, 7: 2, 8: 3}, semantics=("parallel",))


BIG = ("w_in", "w_out", "w_up", "w_down")
WEIGHTS = ("meta", "attn_norm_g", "w_in", "b_f", "conv_w", "conv_b", "w_gate_a", "b_gate_a", "w_gate_x",
           "b_gate_x", "lru_L", "attn_out_g", "rec_out_g", "w_out", "mlp_norm_g", "w_up", "w_down", "final_g")
SMALL = tuple(k for k in WEIGHTS if k not in BIG)
COL_SHARDED_SMALL = ("meta", "conv_w")


def _packed_rows(shape):
    return -(-math.prod(shape) // (SUBLANES * LANES)) * SUBLANES


def _pack(arrs):
    rows = []
    for a in arrs:
        flat = a.reshape(-1)
        rows.append(jnp.pad(flat, (0, _packed_rows(a.shape) * LANES - flat.shape[0])).reshape(-1, LANES))
    used = sum(r.shape[0] for r in rows)
    rows.append(jnp.zeros((-used % ELEM_ROWS, LANES), F32))
    return jnp.concatenate(rows, axis=0)


def _unpack(buf, shapes):
    out, r0 = [], 0
    for s in shapes:
        nr = _packed_rows(s)
        out.append(buf[r0:r0 + nr].reshape(-1)[:math.prod(s)].reshape(s))
        r0 += nr
    return out


def _halves(a):
    return a.reshape((2, a.shape[0] // 2) + a.shape[1:])


def _cols_from_chips(g):
    return jnp.moveaxis(g, 0, -2).reshape(g.shape[1:-1] + (N_CHIPS * g.shape[-1],))


def kernel(x, meta, attn_norm_g, w_in, b_f, conv_w, conv_b, w_gate_a, b_gate_a, w_gate_x, b_gate_x, lru_L, attn_out_g, rec_out_g, w_out, mlp_norm_g, w_up, w_down, final_g, loss_target, m_meta, m_attn_norm_g, m_w_in, m_b_f, m_conv_w, m_conv_b, m_w_gate_a, m_b_gate_a, m_w_gate_x, m_b_gate_x, m_lru_L, m_attn_out_g, m_rec_out_g, m_w_out, m_mlp_norm_g, m_w_up, m_w_down, m_final_g, v_meta, v_attn_norm_g, v_w_in, v_b_f, v_conv_w, v_conv_b, v_w_gate_a, v_b_gate_a, v_w_gate_x, v_b_gate_x, v_lru_L, v_attn_out_g, v_rec_out_g, v_w_out, v_mlp_norm_g, v_w_up, v_w_down, v_final_g):
    w = dict(meta=meta, attn_norm_g=attn_norm_g, w_in=w_in, b_f=b_f, conv_w=conv_w, conv_b=conv_b,
             w_gate_a=w_gate_a, b_gate_a=b_gate_a, w_gate_x=w_gate_x, b_gate_x=b_gate_x, lru_L=lru_L,
             attn_out_g=attn_out_g, rec_out_g=rec_out_g, w_out=w_out, mlp_norm_g=mlp_norm_g, w_up=w_up,
             w_down=w_down, final_g=final_g)
    m = dict(meta=m_meta, attn_norm_g=m_attn_norm_g, w_in=m_w_in, b_f=m_b_f, conv_w=m_conv_w, conv_b=m_conv_b,
             w_gate_a=m_w_gate_a, b_gate_a=m_b_gate_a, w_gate_x=m_w_gate_x, b_gate_x=m_b_gate_x, lru_L=m_lru_L,
             attn_out_g=m_attn_out_g, rec_out_g=m_rec_out_g, w_out=m_w_out, mlp_norm_g=m_mlp_norm_g,
             w_up=m_w_up, w_down=m_w_down, final_g=m_final_g)
    v = dict(meta=v_meta, attn_norm_g=v_attn_norm_g, w_in=v_w_in, b_f=v_b_f, conv_w=v_conv_w, conv_b=v_conv_b,
             w_gate_a=v_w_gate_a, b_gate_a=v_b_gate_a, w_gate_x=v_w_gate_x, b_gate_x=v_b_gate_x, lru_L=v_lru_L,
             attn_out_g=v_attn_out_g, rec_out_g=v_rec_out_g, w_out=v_w_out, mlp_norm_g=v_mlp_norm_g,
             w_up=v_w_up, w_down=v_w_down, final_g=v_final_g)
    s_len, d = x.shape[1], x.shape[2]
    depth = w_in.shape[0]
    att_w = d // 2
    rec_w = d - att_w
    nh = att_w // HEAD_DIM
    chip = 2 * lax.axis_index("x") + lax.axis_index("y")

    g_conv, g_meta = [g.reshape((N_CHIPS, g.shape[1] * g.shape[2]) + g.shape[3:])
                      for g in _all_gather_chips([_halves(w["conv_w"]), _halves(w["meta"])])]
    p = dict(w)
    p["conv_w"] = _cols_from_chips(g_conv)
    meta_full = jnp.moveaxis(g_meta, 0, 1).reshape(N_META, d)

    chip1 = chip.reshape(1).astype(jnp.int32)
    w_in_t, m_in_t, v_in_t = [jnp.transpose(a["w_in"], (2, 0, 1)) for a in (w, m, v)]
    w_in_slabs = _cast_w_in_t_to_slabs(w_in_t, chip1)
    pushes, tokens = [], []
    for l in range(depth):
        slabs = [w_in_slabs[l]] + [_cast_to_slab(w[k], l, chip1) for k in BIG[1:]]
        send_sems, recv_sems, _, lands, token = _push_start(slabs, slabs, "gather_chips", f"weights_start_{l}")
        pushes.append((send_sems, recv_sems, lands))
        tokens.append(token[0, 0])

    t_len = N_META + s_len
    pad = -t_len % SEQ_TILE
    h = jnp.concatenate([meta_full, x[0], jnp.zeros((pad, d), F32)], axis=0)
    tgt = jnp.concatenate([jnp.zeros((N_META, d), F32), loss_target[0], jnp.zeros((pad, d), F32)], axis=0)
    z = _rms_fwd(h, _row(p["attn_norm_g"][0] + sum(tokens)))
    saved = []
    for l in range(depth):
        send_sems, recv_sems, lands = pushes[l]

        def fetch(k, after, l=l, send_sems=send_sems, recv_sems=recv_sems, lands=lands):
            i = BIG.index(k)
            return _push_wait(send_sems, recv_sems, [i], [lands[i]], [lands[i]], "gather_chips", after,
                              f"{k}_wait_{l}")[0]

        g_next = p["attn_norm_g"][l + 1] if l + 1 < depth else p["final_g"]
        h, z, sv = _forward_layer(l, h, z, p, fetch, g_next)
        saved.append(sv)
    dh, dh_b, dg_final, loss_part = _loss_bwd(h, _row(p["final_g"]), tgt, s_len)

    small = {k: [None] * depth for k in SMALL if k not in ("meta", "final_g")}
    pushes = [None] * depth
    tok = 0.0
    for l in reversed(range(depth)):
        dh, dh_b, big_mlp, sm_mlp = _backward_mlp(l, dh, dh_b, saved[l], p, tok)
        parts = [big_mlp["w_down"], big_mlp["w_up"]]
        push_mlp = _push_start(parts, [lax.empty(a.shape, a.dtype) for a in parts], "scatter_chips",
                               f"mlp_grads_start_{l}")
        dh, dh_b, big_mix, sm_mix = _backward_mixer(l, dh, dh_b, saved[l], p, push_mlp[4][0, 0])
        parts = [big_mix["w_out"], big_mix["w_in"]]
        push_mix = _push_start(parts, [lax.empty(a.shape, a.dtype) for a in parts], "scatter_chips",
                               f"mixer_grads_start_{l}")
        tok = push_mix[4][0, 0]
        pushes[l] = {("w_down", "w_up"): push_mlp, ("w_out", "w_in"): push_mix}
        for k, val in {**sm_mlp, **sm_mix}.items():
            small[k][l] = val
    grads = {k: jnp.stack(val) for k, val in small.items()}
    grads["final_g"] = dg_final[0]
    grads["meta"] = dh[:N_META]
    dx = dh[N_META:t_len]

    full_shapes = [grads[k].shape for k in SMALL] + [(1,)]
    packed = _pack([grads[k].astype(F32) for k in SMALL] + [loss_part[0, :1] + tok])
    dev1 = (2 * chip + lax.axis_index("c")).reshape(1).astype(jnp.int32)
    slabs = [_place_slab(packed, dev1, N_DEV)]
    small_push = _push_start(slabs, slabs, "gather_devices", "small_grads_start")

    last_token = small_push[4]
    outs = {k: [lax.empty(w[k].shape, F32) for _ in range(4)] for k in BIG[1:]}
    w_in_sums = [None] * depth
    swaps = {}

    def finish(l, after):
        send_sems, recv_sems, mine, lands, _ = swaps[l]
        mine, theirs = _push_wait(send_sems, recv_sems, list(range(len(BIG))), mine, lands, "sibling", after,
                                  f"sums_wait_{l}")
        w_in_sums[l] = (mine[0], theirs[0])
        for k, a, b in zip(BIG[1:], mine[1:], theirs[1:]):
            outs[k] = _adamw_layer(w[k], m[k], v[k], l, a, b, outs[k])

    for l in reversed(range(depth)):
        sums = {}
        for names, (send_sems, recv_sems, parts, lands, _) in pushes[l].items():
            parts, landed = _push_wait(send_sems, recv_sems, [0, 1], parts, lands, "scatter_chips", last_token,
                                       f"{names[0]}_grads_wait_{l}")
            for k, part, land in zip(names, parts, landed):
                sums[k] = _sum_partials(part, land, chip1)
        mine = [sums[k] for k in BIG]
        swaps[l] = _push_start(mine, [lax.empty(a.shape, a.dtype) for a in mine], "sibling", f"sums_start_{l}")
        if l + 2 < depth:
            finish(l + 2, mine[0])
    for l in reversed(range(min(2, depth))):
        finish(l, swaps[0][4])
    outs["w_in"] = [jnp.transpose(r, (1, 2, 0)) for r in _adamw_w_in_t(
        w_in_t, m_in_t, v_in_t, jnp.stack([s[0] for s in w_in_sums], axis=1),
        jnp.stack([s[1] for s in w_in_sums], axis=1))]
    out_g, out_d, out_m, out_v = [{k: outs[k][i] for k in BIG} for i in range(4)]

    landed = _push_wait(small_push[0], small_push[1], [0], small_push[3], small_push[3], "gather_devices",
                        out_g["w_in"], "small_grads_wait")
    total = _sum_slabs(landed[0])
    small_g = dict(zip(SMALL + ("loss",), _unpack(total, full_shapes)))
    for k in COL_SHARDED_SMALL:
        n = w[k].shape[-1]
        small_g[k] = lax.dynamic_slice_in_dim(small_g[k], chip * n, n, axis=small_g[k].ndim - 1)
    local_shapes = [w[k].shape for k in SMALL]
    res = _adamw(_pack([w[k] for k in SMALL]), _pack([small_g[k] for k in SMALL]),
                 _pack([m[k] for k in SMALL]), _pack([v[k] for k in SMALL]))
    out_g.update({k: small_g[k] for k in SMALL})
    for dst, buf in zip((out_d, out_m, out_v), res):
        dst.update(zip(SMALL, _unpack(buf, local_shapes)))

    return (small_g["loss"].reshape(()), dx[None],
            *[out_g[k] for k in WEIGHTS], *[out_d[k] for k in WEIGHTS],
            *[out_m[k] for k in WEIGHTS], *[out_v[k] for k in WEIGHTS])
```

```python
import functools
import math

import jax
import jax.numpy as jnp
from jax import lax
from jax.experimental import pallas as pl
from jax.experimental.pallas import tpu as pltpu

F32 = jnp.float32
BF16 = jnp.bfloat16

N_META = 16
HEAD_DIM = 64
N_REC_BLOCKS = 8
CONV_WIDTH = 4
RG_C = 8.0
NORM_EPS = 1e-6
ADAM_LR = 0.001
ADAM_B1 = 0.9
ADAM_B2 = 0.999
ADAM_EPS = 1e-08
ADAM_WD = 0.01
ADAM_STEP = 10

LANES = 128
SUBLANES = 8
SEQ_TILE = 128
VMEM_CAP = 60 * 2**20
VMEM_SLACK = 6 * 2**20
NEG_BIG = -1e30
N_CHIPS = 4
N_DEV = 8
MESH = pl.DeviceIdType.MESH


def _nbytes(shape, dtype):
    return math.prod(shape) * jnp.dtype(dtype).itemsize


def _call(body, args, *, name, out_shape, grid=(), in_specs=None, out_specs=None, scratch_shapes=(),
          grid_spec=None, semantics=None, vmem_bytes=None, side_effects=None, hbm_results=True, **kw):
    cp = {}
    if semantics is not None:
        cp["dimension_semantics"] = semantics
    if vmem_bytes is not None:
        cp["vmem_limit_bytes"] = int(min(VMEM_CAP, vmem_bytes + VMEM_SLACK))
    if side_effects is not None:
        cp["has_side_effects"] = side_effects
    if grid_spec is not None:
        kw["grid_spec"] = grid_spec
    else:
        kw.update(grid=grid, in_specs=in_specs, out_specs=out_specs, scratch_shapes=scratch_shapes)
    if hbm_results:
        out_shape = jax.tree.map(
            lambda s: pltpu.HBM(s.shape, s.dtype) if isinstance(s, jax.ShapeDtypeStruct) else s, out_shape)
    fn = pl.pallas_call(
        body, name=name, out_shape=out_shape,
        compiler_params=pltpu.CompilerParams(**cp), **kw)
    return fn(*[_in_hbm(a) if jnp.issubdtype(getattr(a, "dtype", jnp.int32), jnp.floating) else a for a in args])


def _divisor_tile(n, unit, target):
    best = None
    for t in range(unit, min(n, target) + 1, unit):
        if n % t == 0:
            best = t
    return n if best is None else best


def _sigmoid(x):
    return 1.0 / (1.0 + jnp.exp(-x))


def _log1p_unit(e):
    series = e * (1.0 - e * (0.5 - e * (1.0 / 3.0)))
    return jnp.where(e < 1e-2, series, jnp.log(1.0 + e))


def _log_sigmoid(x):
    return jnp.minimum(x, 0.0) - _log1p_unit(jnp.exp(-jnp.abs(x)))


def _one_minus_exp(x, exp_x):
    small = -x * (1.0 + x * (1.0 / 2 + x * (1.0 / 6 + x * (1.0 / 24 + x * (1.0 / 120 + x * (1.0 / 720))))))
    return jnp.where(x > -0.25, small, 1.0 - exp_x)


_GELU_K = math.sqrt(2.0 / math.pi)
_GELU_C = 0.044715


def _gelu_and_grad(y):
    th = jnp.tanh(_GELU_K * (y + _GELU_C * y * y * y))
    g = 0.5 * y * (1.0 + th)
    dg = 0.5 * (1.0 + th) + 0.5 * y * (1.0 - th * th) * _GELU_K * (1.0 + 3.0 * _GELU_C * y * y)
    return g, dg


def _rstd(x):
    return lax.rsqrt(jnp.mean(x * x, axis=-1, keepdims=True) + NORM_EPS)


def _rms_bwd(dz, x, g):
    rs = _rstd(x)
    xh = x * rs
    dgp = jnp.sum(dz * xh, axis=0, keepdims=True)
    dxh = dz * g
    dx = rs * (dxh - xh * jnp.mean(dxh * xh, axis=-1, keepdims=True))
    return dx, dgp


def _dot(a, b):
    return jnp.dot(a, b, preferred_element_type=F32)


def _dot_nt(a, b):
    return lax.dot_general(a, b, (((1,), (1,)), ((), ())), preferred_element_type=F32)


def _dot_tn(a, b):
    return lax.dot_general(a, b, (((0,), (0,)), ((), ())), preferred_element_type=F32)


def _full(shape):
    nd = len(shape)
    return pl.BlockSpec(shape, lambda *_: (0,) * nd)


def _rms_fwd(h, g):
    tp, d = h.shape
    tm = _divisor_tile(tp, 16, 544)

    def body(h_ref, g_ref, z_ref):
        x = h_ref[...]
        z_ref[...] = (x * _rstd(x) * g_ref[...]).astype(BF16)

    return _call(body, (h, g), name="rms_fwd", grid=(tp // tm,),
                 in_specs=[pl.BlockSpec((tm, d), lambda i: (i, 0)), _full((1, d))],
                 out_specs=pl.BlockSpec((tm, d), lambda i: (i, 0)),
                 out_shape=jax.ShapeDtypeStruct((tp, d), BF16), semantics=("parallel",))


def _proj(z, w_big_t, att_w):
    tp, d = z.shape
    nb = w_big_t.shape[0]
    tn = _divisor_tile(nb, LANES, 512)
    assert (3 * att_w) % tn == 0
    n_qkv = 3 * att_w // tn
    scale = 1.0 / math.sqrt(HEAD_DIM)

    def body(z_ref, w_ref, p_ref, qkv_ref):
        j = pl.program_id(0)
        acc = _dot_nt(z_ref[...], w_ref[...])
        p_ref[...] = acc

        @pl.when(j < n_qkv)
        def _():
            col = j * tn + lax.broadcasted_iota(jnp.int32, (1, tn), 1)
            qkv_ref[...] = (acc * jnp.where(col < att_w, scale, 1.0)).astype(BF16)

    vm = 2 * (_nbytes((tp, d), BF16) + _nbytes((d, tn), BF16) + _nbytes((tp, tn), F32) * 2)
    return _call(body, (z, w_big_t), name="proj", grid=(nb // tn,),
                 in_specs=[_full((tp, d)), pl.BlockSpec((tn, d), lambda j: (j, 0))],
                 out_specs=[pl.BlockSpec((tp, tn), lambda j: (0, j)),
                            pl.BlockSpec((tp, tn), lambda j: (0, jnp.minimum(j, n_qkv - 1)))],
                 out_shape=[jax.ShapeDtypeStruct((tp, nb), F32),
                            jax.ShapeDtypeStruct((tp, 3 * att_w), BF16)],
                 semantics=("arbitrary",), vmem_bytes=vm)


def _tile_cumsum(x, row, reverse=False):
    for s in (1, 2, 4):
        if reverse:
            x = x + jnp.where(row < SUBLANES - s, pltpu.roll(x, SUBLANES - s, 0), 0.0)
        else:
            x = x + jnp.where(row >= s, pltpu.roll(x, s, 0), 0.0)
    return x


def _fgate_fwd(proj, b_f_pad, nh):
    tp, nb = proj.shape
    fblk = nb // LANES - 1

    def body(f_ref, b_ref, c_ref, ct_ref):
        b = b_ref[...]
        row = lax.broadcasted_iota(jnp.int32, (SUBLANES, LANES), 0)

        def step(i, carry):
            r0 = pl.multiple_of(i * SUBLANES, SUBLANES)
            lf = _log_sigmoid(f_ref[pl.ds(r0, SUBLANES), :] + b)
            x = _tile_cumsum(lf, row) + carry
            c_ref[pl.ds(r0, SUBLANES), :] = x
            return x[SUBLANES - 1:SUBLANES, :]

        lax.fori_loop(0, tp // SUBLANES, step, jnp.zeros((1, LANES), F32))
        ct_ref[...] = c_ref[...].T[:nh, :]

    return _call(body, (proj, b_f_pad), name="fgate_fwd", grid=(1,),
                 in_specs=[pl.BlockSpec((tp, LANES), lambda i: (0, fblk)), _full((1, LANES))],
                 out_specs=[_full((tp, LANES)), _full((nh, tp))],
                 out_shape=[jax.ShapeDtypeStruct((tp, LANES), F32), jax.ShapeDtypeStruct((nh, tp), F32)],
                 semantics=("arbitrary",))


def _fgate_bwd(proj, b_f_pad, dc):
    tp, nb = proj.shape
    fblk = nb // LANES - 1

    def body(f_ref, b_ref, dc_ref, df_ref, db_ref, dc_s):
        b = b_ref[...]
        row = lax.broadcasted_iota(jnp.int32, (SUBLANES, LANES), 0)
        nt = tp // SUBLANES

        def step(i, carry):
            suffix, acc = carry
            r0 = pl.multiple_of((nt - 1 - i) * SUBLANES, SUBLANES)
            dlf = _tile_cumsum(dc_ref[pl.ds(r0, SUBLANES), :], row, reverse=True) + suffix
            df = dlf * _sigmoid(-(f_ref[pl.ds(r0, SUBLANES), :] + b))
            dc_s[pl.ds(r0, SUBLANES), :] = df
            return dlf[0:1, :], acc + df

        _, acc = lax.fori_loop(0, nt, step, (jnp.zeros((1, LANES), F32), jnp.zeros((SUBLANES, LANES), F32)))
        df_ref[...] = dc_s[...].astype(BF16)
        db_ref[...] = jnp.broadcast_to(jnp.sum(acc, axis=0, keepdims=True), (SUBLANES, LANES))

    return _call(body, (proj, b_f_pad, dc), name="fgate_bwd", grid=(1,),
                 in_specs=[pl.BlockSpec((tp, LANES), lambda i: (0, fblk)), _full((1, LANES)), _full((tp, LANES))],
                 out_specs=[_full((tp, LANES)), _full((SUBLANES, LANES))],
                 out_shape=[jax.ShapeDtypeStruct((tp, LANES), BF16),
                            jax.ShapeDtypeStruct((SUBLANES, LANES), F32)],
                 scratch_shapes=[pltpu.VMEM((tp, LANES), F32)], semantics=("arbitrary",))


ATT_BQ = 128


ATT_BUCKET = 3
ATT_HEADS = 4


def _for_bucket(i, nq, fn):
    for lo in range(0, nq, ATT_BUCKET):
        hi = min(lo + ATT_BUCKET, nq)
        spans = ([(0, lo * ATT_BQ, False)] if lo else []) + [(lo * ATT_BQ, hi * ATT_BQ, True)]
        pl.when(jnp.logical_and(i >= lo, i < hi))(functools.partial(fn, spans))


def _head_column(c_blk, h):
    lane = lax.broadcasted_iota(jnp.int32, c_blk.shape, 1)
    return jnp.sum(jnp.where(lane == h, c_blk, 0.0), axis=1, keepdims=True)


def _key_major_logits(kh_ref, ck_ref, hh, q, span, q0):
    k0, k1, needs_mask = span
    t = _dot_nt(kh_ref[hh, k0:k1, :], q) - ck_ref[hh, k0:k1, :]
    if needs_mask:
        keys = k0 + lax.broadcasted_iota(jnp.int32, (k1 - k0, ATT_BQ), 0)
        t = jnp.where(keys <= q0 + lax.broadcasted_iota(jnp.int32, (k1 - k0, ATT_BQ), 1), t, NEG_BIG)
    return t


def _stage_keys(p, k_ref, c_ref, kh_s, ck_s):
    for hh in range(ATT_HEADS):
        kh_s[hh] = k_ref[:, HEAD_DIM * hh:HEAD_DIM * (hh + 1)]
        ck_s[hh] = jnp.broadcast_to(_head_column(c_ref[...], ATT_HEADS * p + hh), ck_s.shape[1:])


def _attn_fwd(qkv, c, c_t, nh):
    tp = qkv.shape[0]
    att_w = nh * HEAD_DIM
    ng = nh // ATT_HEADS
    gw = ATT_HEADS * HEAD_DIM
    bq = ATT_BQ
    nq = tp // bq

    def body(q_ref, k_ref, v_ref, c_ref, ct_ref, o_ref, lse_ref, ck_s, kh_s, vt_s):
        p = pl.program_id(0)
        i = pl.program_id(1)

        @pl.when(i == 0)
        def _():
            _stage_keys(p, k_ref, c_ref, kh_s, ck_s)
            vt_s[...] = v_ref[...].astype(F32).T.astype(BF16)

        def compute(spans):
            q0 = pl.multiple_of(i * bq, bq)
            o_t, lses = [], []
            for hh in range(ATT_HEADS):
                lo = HEAD_DIM * hh
                q = q_ref[:, lo:lo + HEAD_DIM]
                ts = [_key_major_logits(kh_s, ck_s, hh, q, sp, q0) for sp in spans]
                m = functools.reduce(jnp.maximum, [jnp.max(t, axis=0, keepdims=True) for t in ts])
                es = [jnp.exp(t - m) for t in ts]
                l = sum(jnp.sum(e, axis=0, keepdims=True) for e in es)
                o = sum(_dot(vt_s[lo:lo + HEAD_DIM, k0:k1], e.astype(BF16)) for e, (k0, k1, _) in zip(es, spans))
                o_t.append(o / l)
                lses.append(m + ct_ref[pl.ds(ATT_HEADS * p + hh, 1), :] + jnp.log(l))
            o_ref[...] = jnp.concatenate(o_t, axis=0).T
            lse_ref[...] = jnp.concatenate(lses, axis=0)

        _for_bucket(i, nq, compute)

    blk = pl.BlockSpec((bq, gw), lambda p, i: (i, p))
    vm = 6 * _nbytes((tp, gw), BF16) + 2 * ATT_HEADS * _nbytes((tp, LANES), F32) + 2 * _nbytes((tp, LANES), F32) \
        + 8 * ATT_HEADS * _nbytes((bq, tp), F32)
    return _call(body, (qkv, qkv, qkv, c, c_t), name="attn_fwd", grid=(ng, nq),
                 in_specs=[blk,
                           pl.BlockSpec((tp, gw), lambda p, i: (0, ng + p)),
                           pl.BlockSpec((tp, gw), lambda p, i: (0, 2 * ng + p)),
                           _full((tp, LANES)), pl.BlockSpec((nh, bq), lambda p, i: (0, i))],
                 out_specs=[blk, pl.BlockSpec((None, ATT_HEADS, bq), lambda p, i: (p, 0, i))],
                 out_shape=[jax.ShapeDtypeStruct((tp, att_w), F32), jax.ShapeDtypeStruct((ng, ATT_HEADS, tp), F32)],
                 scratch_shapes=[pltpu.VMEM((ATT_HEADS, tp, LANES), F32), pltpu.VMEM((ATT_HEADS, tp, HEAD_DIM), BF16),
                                 pltpu.VMEM((gw, tp), BF16)],
                 semantics=("arbitrary", "arbitrary"), vmem_bytes=vm)


def _attn_bwd(qkv, c, c_t, lse_b, do, nh):
    tp = qkv.shape[0]
    att_w = nh * HEAD_DIM
    ng = nh // ATT_HEADS
    gw = ATT_HEADS * HEAD_DIM
    bq = ATT_BQ
    nq = tp // bq
    scale = 1.0 / math.sqrt(HEAD_DIM)

    def body(q_ref, k_ref, v_ref, c_ref, ct_ref, lse_ref, do_ref, dq_ref, dk_ref, dv_ref, dc_ref,
             dk_s, dv_s, dc_s, ck_s, kt_s, kh_s, vh_s):
        p = pl.program_id(0)
        i = pl.program_id(1)

        @pl.when(i == 0)
        def _():
            dk_s[...] = jnp.zeros_like(dk_s)
            dv_s[...] = jnp.zeros_like(dv_s)
            dc_s[...] = jnp.zeros_like(dc_s)
            kt_s[...] = k_ref[...].astype(F32).T.astype(BF16)
            _stage_keys(p, k_ref, c_ref, kh_s, ck_s)
            for hh in range(ATT_HEADS):
                vh_s[hh] = v_ref[:, HEAD_DIM * hh:HEAD_DIM * (hh + 1)]

        @pl.when(jnp.logical_and(i == 0, p == 0))
        def _():
            dc_ref[...] = jnp.zeros_like(dc_ref)

        def compute(spans):
            q0 = pl.multiple_of(i * bq, bq)
            dq_t = []
            for hh in range(ATT_HEADS):
                lo = HEAD_DIM * hh
                h = ATT_HEADS * p + hh
                q = q_ref[:, lo:lo + HEAD_DIM]
                doutb = do_ref[:, lo:lo + HEAD_DIM].astype(BF16)
                col_term = ct_ref[pl.ds(h, 1), :] - lse_ref[hh:hh + 1, :]
                prs, dps = [], []
                for sp in spans:
                    prs.append(jnp.exp(_key_major_logits(kh_s, ck_s, hh, q, sp, q0) + col_term))
                    dps.append(_dot_nt(vh_s[hh, sp[0]:sp[1], :], doutb))
                key_sum = sum(jnp.sum(pr * dp, axis=0, keepdims=True) for pr, dp in zip(prs, dps))
                dq_h = 0.0
                for (k0, k1, _), pr, dp in zip(spans, prs, dps):
                    ds = pr * (dp - key_sum)
                    dsb = ds.astype(BF16)
                    dq_h = dq_h + _dot(kt_s[lo:lo + HEAD_DIM, k0:k1], dsb)
                    dk_s[hh, k0:k1, :] += _dot(dsb, q)
                    dv_s[hh, k0:k1, :] += _dot(pr.astype(BF16), doutb)
                    dc_s[hh, k0:k1, :] += jnp.sum(ds, axis=1, keepdims=True)
                dq_t.append(dq_h)
            dq_ref[...] = (jnp.concatenate(dq_t, axis=0) * scale).T.astype(BF16)

        _for_bucket(i, nq, compute)

        @pl.when(i == nq - 1)
        def _():
            dk_ref[...] = jnp.concatenate([dk_s[hh] for hh in range(ATT_HEADS)], axis=1).astype(BF16)
            dv_ref[...] = jnp.concatenate([dv_s[hh] for hh in range(ATT_HEADS)], axis=1).astype(BF16)
            lane = lax.broadcasted_iota(jnp.int32, (tp, LANES), 1)
            dc = dc_ref[...]
            for hh in range(ATT_HEADS):
                dc = jnp.where(lane == ATT_HEADS * p + hh, -dc_s[hh], dc)
            dc_ref[...] = dc

    blk = pl.BlockSpec((bq, gw), lambda p, i: (i, p))
    col = pl.BlockSpec((tp, gw), lambda p, i: (0, p))
    vm = 7 * _nbytes((tp, gw), BF16) + 2 * _nbytes((tp, gw), F32) + 2 * ATT_HEADS * _nbytes((tp, LANES), F32) \
        + 4 * _nbytes((tp, LANES), F32) + 12 * ATT_HEADS * _nbytes((bq, tp), F32)
    return _call(body, (qkv, qkv, qkv, c, c_t, lse_b, do), name="attn_bwd", grid=(ng, nq),
                 in_specs=[blk,
                           pl.BlockSpec((tp, gw), lambda p, i: (0, ng + p)),
                           pl.BlockSpec((tp, gw), lambda p, i: (0, 2 * ng + p)),
                           _full((tp, LANES)), pl.BlockSpec((nh, bq), lambda p, i: (0, i)),
                           pl.BlockSpec((None, ATT_HEADS, bq), lambda p, i: (p, 0, i)), blk],
                 out_specs=[blk, col, col, _full((tp, LANES))],
                 out_shape=[jax.ShapeDtypeStruct((tp, att_w), BF16)] * 3 + [jax.ShapeDtypeStruct((tp, LANES), F32)],
                 scratch_shapes=[pltpu.VMEM((ATT_HEADS, tp, HEAD_DIM), F32), pltpu.VMEM((ATT_HEADS, tp, HEAD_DIM), F32),
                                 pltpu.VMEM((ATT_HEADS, tp, 1), F32), pltpu.VMEM((ATT_HEADS, tp, LANES), F32),
                                 pltpu.VMEM((gw, tp), BF16), pltpu.VMEM((ATT_HEADS, tp, HEAD_DIM), BF16),
                                 pltpu.VMEM((ATT_HEADS, tp, HEAD_DIM), BF16)],
                 semantics=("arbitrary", "arbitrary"), vmem_bytes=vm)


REC_ROWS = 128
HALO = SUBLANES


def _conv_taps(cat):
    taps = []
    for k in range(CONV_WIDTH):
        sh = CONV_WIDTH - 1 - k
        taps.append((pltpu.roll(cat, sh, 0) if sh else cat)[HALO:])
    return taps


def _rec_gates(xc, wa_ref, ba_ref, wx_ref, bx_ref, l_ref):
    xcb = xc.astype(BF16)
    r = _sigmoid(_dot(xcb, wa_ref[...]) + ba_ref[...])
    ig = _sigmoid(_dot(xcb, wx_ref[...]) + bx_ref[...])
    ls = _log_sigmoid(l_ref[...])
    log_a = RG_C * r * ls
    return xcb, r, ig, ls, log_a


def _rec_fwd(proj, xr_blk, yr_blk, rec_w, conv_w, conv_b, wa, ba, wx, bx, lru):
    tp = proj.shape[0]
    w = rec_w
    r_rows = REC_ROWS
    nc = tp // r_rows
    cpb = w // LANES

    def body(xr_ref, yr_ref, cw_ref, cb_ref, wa_ref, ba_ref, wx_ref, bx_ref, l_ref,
             hr_ref, rec_ref, prev_s, carry_s, a_s, u_s):
        i = pl.program_id(0)

        @pl.when(i == 0)
        def _():
            prev_s[...] = jnp.zeros_like(prev_s)
            carry_s[...] = jnp.zeros_like(carry_s)

        x = xr_ref[...]
        taps = _conv_taps(jnp.concatenate([prev_s[...], x], axis=0))
        prev_s[...] = x[r_rows - HALO:]
        xc = cb_ref[...]
        for k in range(CONV_WIDTH):
            xc = xc + cw_ref[k:k + 1, :] * taps[k]
        _, r, ig, ls, log_a = _rec_gates(xc, wa_ref, ba_ref, wx_ref, bx_ref, l_ref)
        a = jnp.exp(log_a)
        a_s[...] = a
        u_s[...] = jnp.sqrt(_one_minus_exp(2.0 * log_a, a * a)) * ig * xc

        def tile(j, h):
            r0 = pl.multiple_of(j * SUBLANES, SUBLANES)
            at = a_s[pl.ds(r0, SUBLANES), :]
            ut = u_s[pl.ds(r0, SUBLANES), :]
            out = []
            for rr in range(SUBLANES):
                h = at[rr:rr + 1] * h + ut[rr:rr + 1]
                out.append(h)
            hr_ref[pl.ds(r0, SUBLANES), :] = jnp.concatenate(out, axis=0)
            return h

        carry_s[0:1, :] = lax.fori_loop(0, r_rows // SUBLANES, tile, carry_s[0:1, :])
        g, _ = _gelu_and_grad(yr_ref[...])
        rec_ref[...] = hr_ref[...] * g

    blk = pl.BlockSpec((r_rows, w), lambda i: (i, 0))
    vm = 16 * _nbytes((r_rows, w), F32) + 4 * _nbytes((w, w), BF16)
    return _call(body, (proj, proj, conv_w, conv_b, wa, ba, wx, bx, lru), name="rec_fwd", grid=(nc,),
                 in_specs=[pl.BlockSpec((r_rows, w), lambda i: (i, xr_blk)),
                           pl.BlockSpec((r_rows, w), lambda i: (i, yr_blk)),
                           _full((CONV_WIDTH, w)), _full((1, w)), _full((w, w)), _full((1, w)),
                           _full((w, w)), _full((1, w)), _full((1, w))],
                 out_specs=[blk, blk],
                 out_shape=[jax.ShapeDtypeStruct((tp, w), F32)] * 2,
                 scratch_shapes=[pltpu.VMEM((HALO, w), F32), pltpu.VMEM((SUBLANES, w), F32),
                                 pltpu.VMEM((r_rows, w), F32), pltpu.VMEM((r_rows, w), F32)],
                 semantics=("arbitrary",), vmem_bytes=vm)


def _rec_bwd(proj, xr_blk, yr_blk, rec_w, hr, drec, conv_w, conv_b, wa, ba, wx, bx, lru):
    tp = proj.shape[0]
    w = rec_w
    r_rows = REC_ROWS
    nc = tp // r_rows
    hpc = r_rows // HALO

    def body(xr_ref, xh_ref, yr_ref, hr_ref, hh_ref, drec_ref, cw_ref, cb_ref, wa_ref, ba_ref, wx_ref, bx_ref,
             l_ref, dxr_ref, dyr_ref, dwa_ref, dwx_ref, small_ref, lam_s, a_s, dhr_s, carry_s, next_s):
        i = pl.program_id(0)
        first = (nc - 1 - i) == 0

        @pl.when(i == 0)
        def _():
            carry_s[...] = jnp.zeros_like(carry_s)
            next_s[...] = jnp.zeros_like(next_s)
            dwa_ref[...] = jnp.zeros_like(dwa_ref)
            dwx_ref[...] = jnp.zeros_like(dwx_ref)
            small_ref[...] = jnp.zeros_like(small_ref)

        x = xr_ref[...]
        xprev = jnp.where(first, 0.0, xh_ref[...])
        taps = _conv_taps(jnp.concatenate([xprev, x], axis=0))
        xc = cb_ref[...]
        for k in range(CONV_WIDTH):
            xc = xc + cw_ref[k:k + 1, :] * taps[k]
        xcb, r, ig, ls, log_a = _rec_gates(xc, wa_ref, ba_ref, wx_ref, bx_ref, l_ref)
        a = jnp.exp(log_a)
        a2 = a * a
        mult = jnp.sqrt(_one_minus_exp(2.0 * log_a, a2))
        g, dg = _gelu_and_grad(yr_ref[...])
        hr_v = hr_ref[...]
        drec_v = drec_ref[...]
        dhr_s[...] = drec_v * g
        dyr_ref[...] = (drec_v * hr_v * dg).astype(BF16)
        a_s[...] = a

        def tile(jj, carry):
            r0 = pl.multiple_of((r_rows // SUBLANES - 1 - jj) * SUBLANES, SUBLANES)
            at = a_s[pl.ds(r0, SUBLANES), :]
            dt = dhr_s[pl.ds(r0, SUBLANES), :]
            out = [None] * SUBLANES
            for rr in range(SUBLANES - 1, -1, -1):
                lam = dt[rr:rr + 1] + carry
                out[rr] = lam
                carry = at[rr:rr + 1] * lam
            lam_s[pl.ds(r0, SUBLANES), :] = jnp.concatenate(out, axis=0)
            return carry

        carry_s[0:1, :] = lax.fori_loop(0, r_rows // SUBLANES, tile, carry_s[0:1, :])
        lam = lam_s[...]
        hprev = jnp.where(first, 0.0, hh_ref[...])
        hr_prev = pltpu.roll(jnp.concatenate([hprev, hr_v], axis=0), 1, 0)[HALO:]
        da = lam * hr_prev
        dxc = lam * mult * ig
        di = lam * mult * xc
        dmult = lam * ig * xc
        dlog_a = da * a - dmult * a2 / mult
        dr = dlog_a * (RG_C * ls)
        dls = jnp.sum(dlog_a * (RG_C * r), axis=0, keepdims=True)
        dga = dr * r * (1.0 - r)
        dgx = di * ig * (1.0 - ig)
        dgab = dga.astype(BF16)
        dgxb = dgx.astype(BF16)
        dxc = dxc + _dot_nt(dgab, wa_ref[...]) + _dot_nt(dgxb, wx_ref[...])
        dwa_ref[...] += _dot_tn(xcb, dgab)
        dwx_ref[...] += _dot_tn(xcb, dgxb)
        cat = jnp.concatenate([dxc, next_s[...]], axis=0)
        next_s[...] = dxc[0:HALO]
        dxr = cw_ref[CONV_WIDTH - 1:CONV_WIDTH, :] * dxc
        for k in range(CONV_WIDTH - 1):
            sh = CONV_WIDTH - 1 - k
            dxr = dxr + cw_ref[k:k + 1, :] * pltpu.roll(cat, r_rows + HALO - sh, 0)[:r_rows]
        dxr_ref[...] = dxr.astype(BF16)
        rows = [jnp.sum(dxc * taps[k], axis=0, keepdims=True) for k in range(CONV_WIDTH)]
        rows += [jnp.sum(dxc, axis=0, keepdims=True), jnp.sum(dga, axis=0, keepdims=True),
                 jnp.sum(dgx, axis=0, keepdims=True), dls * _sigmoid(-l_ref[...])]
        small_ref[...] += jnp.concatenate(rows, axis=0)

    def rev(i):
        return nc - 1 - i

    def halo(i):
        return jnp.maximum(rev(i) * hpc - 1, 0)

    blk = pl.BlockSpec((r_rows, w), lambda i: (rev(i), 0))
    vm = 40 * _nbytes((r_rows, w), F32) + 6 * _nbytes((w, w), F32)
    return _call(body, (proj, proj, proj, hr, hr, drec, conv_w, conv_b, wa, ba, wx, bx, lru),
                 name="rec_bwd", grid=(nc,),
                 in_specs=[pl.BlockSpec((r_rows, w), lambda i: (rev(i), xr_blk)),
                           pl.BlockSpec((HALO, w), lambda i: (halo(i), xr_blk)),
                           pl.BlockSpec((r_rows, w), lambda i: (rev(i), yr_blk)),
                           blk,
                           pl.BlockSpec((HALO, w), lambda i: (halo(i), 0)),
                           blk,
                           _full((CONV_WIDTH, w)), _full((1, w)), _full((w, w)), _full((1, w)),
                           _full((w, w)), _full((1, w)), _full((1, w))],
                 out_specs=[blk, blk, _full((w, w)), _full((w, w)), _full((SUBLANES, w))],
                 out_shape=[jax.ShapeDtypeStruct((tp, w), BF16)] * 2
                 + [jax.ShapeDtypeStruct((w, w), F32)] * 2 + [jax.ShapeDtypeStruct((SUBLANES, w), F32)],
                 scratch_shapes=[pltpu.VMEM((r_rows, w), F32)] * 3
                 + [pltpu.VMEM((SUBLANES, w), F32), pltpu.VMEM((HALO, w), F32)],
                 semantics=("arbitrary",), vmem_bytes=vm)


ROW_TARGET = 544


def _mixer_out(attn, rec, g_a, g_r, w_out, h, g_next):
    tp, d = h.shape
    aw, rw = attn.shape[1], rec.shape[1]
    kc = d // N_CHIPS
    tm = _divisor_tile(tp, 16, ROW_TARGET)

    def body(a_ref, r_ref, ga_ref, gr_ref, w_ref, h_ref, gn_ref, h1_ref, z_ref, mix_ref):
        a = a_ref[...]
        r = r_ref[...]
        mix = jnp.concatenate([a * _rstd(a) * ga_ref[...], r * _rstd(r) * gr_ref[...]], axis=1).astype(BF16)
        mix_ref[...] = mix
        h1 = h_ref[...]
        for j in range(N_CHIPS):
            h1 = h1 + _dot(mix[:, j * kc:(j + 1) * kc], w_ref[j])
        h1_ref[...] = h1
        z_ref[...] = (h1 * _rstd(h1) * gn_ref[...]).astype(BF16)

    row = lambda wd: pl.BlockSpec((tm, wd), lambda i: (i, 0))
    vm = 2 * _nbytes((d, d), BF16) + 12 * _nbytes((tm, d), F32)
    return _call(body, (attn, rec, g_a, g_r, w_out, h, g_next), name="mixer_out", grid=(tp // tm,),
                 in_specs=[row(aw), row(rw), _full((1, aw)), _full((1, rw)), _full(w_out.shape), row(d),
                           _full((1, d))],
                 out_specs=[row(d), row(d), row(d)],
                 out_shape=[jax.ShapeDtypeStruct((tp, d), F32), jax.ShapeDtypeStruct((tp, d), BF16),
                            jax.ShapeDtypeStruct((tp, d), BF16)],
                 semantics=("parallel",), vmem_bytes=vm)


def _mixer_bwd(dh_b, w_out, attn, rec, g_a, g_r):
    tp, d = dh_b.shape
    aw, rw = attn.shape[1], rec.shape[1]
    tm = _divisor_tile(tp, 16, ROW_TARGET)

    def body(dh_ref, w_ref, a_ref, r_ref, ga_ref, gr_ref, da_ref, dr_ref, dg_ref):
        @pl.when(pl.program_id(0) == 0)
        def _():
            dg_ref[...] = jnp.zeros_like(dg_ref)

        dh = dh_ref[...]
        dmix = jnp.concatenate([_dot_nt(dh, w_ref[j]) for j in range(N_CHIPS)], axis=1)
        da, dga = _rms_bwd(dmix[:, :aw], a_ref[...], ga_ref[...])
        dr, dgr = _rms_bwd(dmix[:, aw:], r_ref[...], gr_ref[...])
        da_ref[...] = da
        dr_ref[...] = dr
        dg_ref[...] += jnp.broadcast_to(jnp.concatenate([dga, dgr], axis=1), (SUBLANES, d))

    row = lambda wd: pl.BlockSpec((tm, wd), lambda i: (i, 0))
    vm = 2 * _nbytes((d, d), BF16) + 12 * _nbytes((tm, d), F32)
    return _call(body, (dh_b, w_out, attn, rec, g_a, g_r), name="mixer_bwd", grid=(tp // tm,),
                 in_specs=[row(d), _full(w_out.shape), row(aw), row(rw), _full((1, aw)), _full((1, rw))],
                 out_specs=[row(aw), row(rw), _full((SUBLANES, d))],
                 out_shape=[jax.ShapeDtypeStruct((tp, aw), F32), jax.ShapeDtypeStruct((tp, rw), F32),
                            jax.ShapeDtypeStruct((SUBLANES, d), F32)],
                 semantics=("arbitrary",), vmem_bytes=vm)


def _mlp_up(z, w_up):
    tp, d = z.shape
    fc = w_up.shape[2]
    ff = N_CHIPS * fc
    tn = _divisor_tile(fc, LANES, 512)
    per = fc // tn

    def body(z_ref, w_ref, act_ref, up_ref):
        up = _dot(z_ref[...], w_ref[...])
        r = jnp.maximum(up, 0.0)
        act_ref[...] = (r * r).astype(BF16)
        up_ref[...] = up.astype(BF16)

    col = pl.BlockSpec((tp, tn), lambda j: (0, j))
    vm = 2 * _nbytes((tp, d), BF16) + 2 * _nbytes((d, tn), BF16) + 8 * _nbytes((tp, tn), F32)
    return _call(body, (z, w_up), name="mlp_up", grid=(ff // tn,),
                 in_specs=[_full((tp, d)), pl.BlockSpec((None, d, tn), lambda j: (j // per, 0, j % per))],
                 out_specs=[col, col],
                 out_shape=[jax.ShapeDtypeStruct((tp, ff), BF16)] * 2,
                 semantics=("parallel",), vmem_bytes=vm)


def _mlp_down(act, w_down, h, g_next):
    tp, d = h.shape
    ff = act.shape[1]
    fc = ff // N_CHIPS
    tm = _divisor_tile(tp, 16, ROW_TARGET)

    def body(a_ref, w_ref, h_ref, gn_ref, h2_ref, z_ref):
        h2 = h_ref[...]
        for j in range(N_CHIPS):
            h2 = h2 + _dot(a_ref[:, j * fc:(j + 1) * fc], w_ref[j])
        h2_ref[...] = h2
        z_ref[...] = (h2 * _rstd(h2) * gn_ref[...]).astype(BF16)

    row = lambda wd: pl.BlockSpec((tm, wd), lambda i: (i, 0))
    vm = 2 * _nbytes((ff, d), BF16) + 2 * _nbytes((tm, ff), BF16) + 10 * _nbytes((tm, d), F32)
    return _call(body, (act, w_down, h, g_next), name="mlp_down", grid=(tp // tm,),
                 in_specs=[row(ff), _full(w_down.shape), row(d), _full((1, d))],
                 out_specs=[row(d), row(d)],
                 out_shape=[jax.ShapeDtypeStruct((tp, d), F32), jax.ShapeDtypeStruct((tp, d), BF16)],
                 semantics=("parallel",), vmem_bytes=vm)


def _loss_bwd(h, g, target, n_real):
    tp, d = h.shape
    tm = _divisor_tile(tp, 16, ROW_TARGET)

    def body(h_ref, g_ref, t_ref, dh_ref, dhb_ref, dg_ref, loss_ref):
        i = pl.program_id(0)

        @pl.when(i == 0)
        def _():
            dg_ref[...] = jnp.zeros_like(dg_ref)
            loss_ref[...] = jnp.zeros_like(loss_ref)

        x = h_ref[...]
        gv = g_ref[...]
        rowi = i * tm + lax.broadcasted_iota(jnp.int32, (tm, 1), 0)
        real = jnp.logical_and(rowi >= N_META, rowi < N_META + n_real)
        err = jnp.where(real, x * _rstd(x) * gv - t_ref[...], 0.0)
        loss_ref[...] += 0.5 * jnp.sum(jnp.mean(err * err, axis=-1, keepdims=True))
        dx, dgp = _rms_bwd(err * (1.0 / d), x, gv)
        dh_ref[...] = dx
        dhb_ref[...] = dx.astype(BF16)
        dg_ref[...] += jnp.broadcast_to(dgp, (SUBLANES, d))

    row = pl.BlockSpec((tm, d), lambda i: (i, 0))
    return _call(body, (h, g, target), name="loss_bwd", grid=(tp // tm,),
                 in_specs=[row, _full((1, d)), row],
                 out_specs=[row, row, _full((SUBLANES, d)), _full((SUBLANES, LANES))],
                 out_shape=[jax.ShapeDtypeStruct((tp, d), F32), jax.ShapeDtypeStruct((tp, d), BF16),
                            jax.ShapeDtypeStruct((SUBLANES, d), F32), jax.ShapeDtypeStruct((SUBLANES, LANES), F32)],
                 semantics=("arbitrary",), vmem_bytes=16 * _nbytes((tm, d), F32))


def _mlp_bwd(dh_b, w_down, up, z2):
    tp, d = dh_b.shape
    fc = w_down.shape[1]
    ff = N_CHIPS * fc
    tn = _divisor_tile(fc, LANES, 512)
    per = fc // tn

    def body(dh_ref, z_ref, w_ref, up_ref, dup_ref, gd_ref, gu_ref):
        dh = dh_ref[...]
        r = jnp.maximum(up_ref[...].astype(F32), 0.0)
        dup = (_dot_nt(dh, w_ref[...]) * (2.0 * r)).astype(BF16)
        dup_ref[...] = dup
        gd_ref[...] = _dot_tn((r * r).astype(BF16), dh).astype(BF16)
        gu_ref[...] = _dot_tn(z_ref[...], dup).astype(BF16)

    col = pl.BlockSpec((tp, tn), lambda j: (0, j))
    vm = 4 * _nbytes((tp, d), BF16) + 4 * _nbytes((tn, d), BF16) + 2 * _nbytes((d, tn), BF16) \
        + 10 * _nbytes((tp, tn), F32) + 4 * _nbytes((tn, d), F32)
    return _call(body, (dh_b, z2, w_down, up), name="mlp_bwd", grid=(ff // tn,),
                 in_specs=[_full((tp, d)), _full((tp, d)),
                           pl.BlockSpec((None, tn, d), lambda j: (j // per, j % per, 0)), col],
                 out_specs=[col, pl.BlockSpec((tn, d), lambda j: (j, 0)),
                            pl.BlockSpec((None, d, tn), lambda j: (j // per, 0, j % per))],
                 out_shape=[jax.ShapeDtypeStruct((tp, ff), BF16), jax.ShapeDtypeStruct((ff, d), BF16),
                            jax.ShapeDtypeStruct((N_CHIPS, d, fc), BF16)],
                 semantics=("parallel",), vmem_bytes=vm)


def _grad_w_pieces(pieces, b):
    tp, n = b.shape
    tn = _divisor_tile(n, LANES, 512)
    widths = [pc.shape[1] for pc in pieces]

    def body(*refs):
        p_refs, b_ref, o_refs = refs[:len(pieces)], refs[len(pieces)], refs[len(pieces) + 1:]
        for p_ref, o_ref in zip(p_refs, o_refs):
            o_ref[...] = _dot_tn(p_ref[...], b_ref[...]).astype(BF16)

    vm = 2 * sum(_nbytes((tp, wd), BF16) for wd in widths) + 2 * _nbytes((tp, tn), BF16) \
        + 4 * sum(_nbytes((wd, tn), F32) for wd in widths) + 2 * _nbytes((tp, max(widths)), F32)
    return _call(body, tuple(pieces) + (b,), name="grad_w_pieces", grid=(n // tn,),
                 in_specs=[_full(pc.shape) for pc in pieces] + [pl.BlockSpec((tp, tn), lambda j: (0, j))],
                 out_specs=[pl.BlockSpec((wd, tn), lambda j: (0, j)) for wd in widths],
                 out_shape=[jax.ShapeDtypeStruct((wd, n), BF16) for wd in widths],
                 semantics=("parallel",), vmem_bytes=vm)


def _dx_norm_bwd(pieces, w, w_spec, w_piece, h, g, dres, dot=_dot_nt):
    tp, d = h.shape
    tm = _divisor_tile(tp, 16, ROW_TARGET)
    n = len(pieces)

    def body(*refs):
        dy_refs = refs[:n]
        w_ref, h_ref, g_ref, dres_ref, dh_ref, dhb_ref, dg_ref = refs[n:]

        @pl.when(pl.program_id(0) == 0)
        def _():
            dg_ref[...] = jnp.zeros_like(dg_ref)

        dz = dot(dy_refs[0][...], w_piece(w_ref, 0))
        for i in range(1, n):
            dz = dz + dot(dy_refs[i][...], w_piece(w_ref, i))
        dx, dgp = _rms_bwd(dz, h_ref[...], g_ref[...])
        dh = dres_ref[...] + dx
        dh_ref[...] = dh
        dhb_ref[...] = dh.astype(BF16)
        dg_ref[...] += jnp.broadcast_to(dgp, (SUBLANES, d))

    row = lambda wd: pl.BlockSpec((tm, wd), lambda i: (i, 0))
    kk = sum(wd for _, _, wd in pieces)
    vm = 2 * _nbytes((d, kk), BF16) + 2 * _nbytes((tm, kk), BF16) + 14 * _nbytes((tm, d), F32)
    piece_specs = [pl.BlockSpec((tm, wd), functools.partial(lambda i, cb: (i, cb), cb=cb)) for _, cb, wd in pieces]
    return _call(body, tuple(a for a, _, _ in pieces) + (w, h, g, dres), name="dx_norm_bwd", grid=(tp // tm,),
                 in_specs=piece_specs + [w_spec, row(d), _full((1, d)), row(d)],
                 out_specs=[row(d), row(d), _full((SUBLANES, d))],
                 out_shape=[jax.ShapeDtypeStruct((tp, d), F32), jax.ShapeDtypeStruct((tp, d), BF16),
                            jax.ShapeDtypeStruct((SUBLANES, d), F32)],
                 semantics=("arbitrary",), vmem_bytes=vm)


def _block_diag(wg):
    nb, b, _ = wg.shape
    eye = jnp.eye(nb, dtype=wg.dtype)
    return (eye[:, None, :, None] * wg[:, :, None, :]).reshape(nb * b, nb * b)


def _diag_blocks(dense, nb):
    b = dense.shape[0] // nb
    d4 = dense.reshape(nb, b, nb, b)
    return jnp.stack([d4[i, :, i, :] for i in range(nb)])


def _row(v):
    return v.reshape(1, -1)


def _forward_layer(l, h, z, p, fetch, g_next):
    d = h.shape[1]
    att_w = d // 2
    rec_w = d - att_w
    nh = att_w // HEAD_DIM
    xr_blk = 3 * att_w // rec_w
    wa_d = _block_diag(p["w_gate_a"][l]).astype(BF16)
    wx_d = _block_diag(p["w_gate_x"][l]).astype(BF16)
    b_f_pad = jnp.zeros((1, LANES), F32).at[0, :nh].set(p["b_f"][l])
    w_in_t = fetch("w_in", h)
    big = dict(w_in_big=_pack_w_in_t(w_in_t.reshape(-1, d), att_w, nh))
    proj, qkv = _proj(z, big["w_in_big"], att_w)
    c, c_t = _fgate_fwd(proj, b_f_pad, nh)
    attn, lse_b = _attn_fwd(qkv, c, c_t, nh)
    hr, rec = _rec_fwd(proj, xr_blk, xr_blk + 1, rec_w, p["conv_w"][l], _row(p["conv_b"][l]), wa_d,
                       _row(p["b_gate_a"][l]), wx_d, _row(p["b_gate_x"][l]), _row(p["lru_L"][l]))
    big["w_out"] = fetch("w_out", rec)
    h1, z2, mix = _mixer_out(attn, rec, _row(p["attn_out_g"][l]), _row(p["rec_out_g"][l]),
                             big["w_out"], h, _row(p["mlp_norm_g"][l]))
    big["w_up"] = fetch("w_up", h1)
    act, up = _mlp_up(z2, big["w_up"])
    big["w_down"] = fetch("w_down", act)
    h2, z_next = _mlp_down(act, big["w_down"], h1, _row(g_next))
    saved = dict(h0=h, z1=z, proj=proj, qkv=qkv, c=c, c_t=c_t, attn=attn, lse_b=lse_b, hr=hr, rec=rec, h1=h1,
                 z2=z2, mix=mix, up=up, wa_d=wa_d, wx_d=wx_d, b_f_pad=b_f_pad, big=big)
    return h2, z_next, saved


def _backward_mlp(l, dh, dh_b, sv, p, tok):
    w_up, w_down = sv["big"]["w_up"], sv["big"]["w_down"]
    fc = w_up.shape[2]
    dup, g_down, g_up = _mlp_bwd(dh_b, w_down, sv["up"], sv["z2"])
    dh, dh_b, dg2 = _dx_norm_bwd([(dup, j, fc) for j in range(N_CHIPS)], w_up, _full(w_up.shape),
                                 lambda w_ref, j: w_ref[j], sv["h1"], _row(p["mlp_norm_g"][l] + tok), dh)
    big = dict(w_down=g_down.reshape((N_CHIPS, -1) + g_down.shape[1:]), w_up=g_up)
    return dh, dh_b, big, dict(mlp_norm_g=dg2[0])


def _backward_mixer(l, dh, dh_b, sv, p, tok):
    d = dh.shape[1]
    att_w = d // 2
    rec_w = d - att_w
    nh = att_w // HEAD_DIM
    xr_blk = 3 * att_w // rec_w
    small = {}
    g_out, = _grad_w_pieces([sv["mix"]], dh_b)
    dattn, drec, dg_mix = _mixer_bwd(dh_b, sv["big"]["w_out"], sv["attn"], sv["rec"],
                                     _row(p["attn_out_g"][l] + tok), _row(p["rec_out_g"][l]))
    small["attn_out_g"] = dg_mix[0, :att_w]
    small["rec_out_g"] = dg_mix[0, att_w:]
    dxr, dyr, dwa, dwx, sm = _rec_bwd(
        sv["proj"], xr_blk, xr_blk + 1, rec_w, sv["hr"], drec, p["conv_w"][l], _row(p["conv_b"][l]), sv["wa_d"],
        _row(p["b_gate_a"][l]), sv["wx_d"], _row(p["b_gate_x"][l]), _row(p["lru_L"][l]))
    small.update(conv_w=sm[:CONV_WIDTH], conv_b=sm[4], b_gate_a=sm[5], b_gate_x=sm[6], lru_L=sm[7],
                 w_gate_a=_diag_blocks(dwa, N_REC_BLOCKS), w_gate_x=_diag_blocks(dwx, N_REC_BLOCKS))
    dq, dk, dv, dc = _attn_bwd(sv["qkv"], sv["c"], sv["c_t"], sv["lse_b"], dattn, nh)
    df, db_f = _fgate_bwd(sv["proj"], sv["b_f_pad"], dc)
    small["b_f"] = db_f[0, :nh]
    pieces = [dq, dk, dv, dxr, dyr, df]
    offs = [0, att_w, 2 * att_w, 3 * att_w, 3 * att_w + rec_w, 3 * att_w + 2 * rec_w]
    gq, gk, gv, gxr, gyr, gf = _grad_w_pieces(pieces, sv["z1"])
    g_in_t = jnp.concatenate([gq, gk, gv, gf[:nh], gxr, gyr], axis=0)
    w_big = sv["big"]["w_in_big"]
    widths = [pc.shape[1] for pc in pieces]
    dh, dh_b, dg1 = _dx_norm_bwd(
        [(pc, 0, wd) for pc, wd in zip(pieces, widths)], w_big, _full(w_big.shape),
        lambda w_ref, i: w_ref[offs[i]:offs[i] + widths[i], :], sv["h0"], _row(p["attn_norm_g"][l]), dh, dot=_dot)
    small["attn_norm_g"] = dg1[0]
    big = dict(w_in=g_in_t.reshape(N_CHIPS, -1, d), w_out=g_out.reshape((N_CHIPS, -1) + g_out.shape[1:]))
    return dh, dh_b, big, small


def _pack_w_in_t(w_in_t, att_w, nh):
    qkv = w_in_t[:3 * att_w]
    f = w_in_t[3 * att_w:3 * att_w + nh]
    xy = w_in_t[3 * att_w + nh:]
    return jnp.concatenate([qkv, xy, f, jnp.zeros((LANES - nh, w_in_t.shape[1]), w_in_t.dtype)], axis=0)


ANY = pl.BlockSpec(memory_space=pl.ANY)


def _coords():
    return lax.axis_index("x"), lax.axis_index("y"), lax.axis_index("c")


def _other_chips(x, y):
    return [(1 - x, y), (x, 1 - y), (1 - x, 1 - y)]


def _remote(src, dst, send_sems, recv_sems, k, to):
    return pltpu.make_async_remote_copy(src_ref=src, dst_ref=dst, send_sem=send_sems.at[k],
                                        recv_sem=recv_sems.at[k], device_id=to, device_id_type=MESH)


def _all_gather_chips(shards):
    n = len(shards)
    per = 6

    def body(*refs):
        ins, outs = refs[:n], refs[n:2 * n]
        send_sems, recv_sems, local_sems = refs[2 * n:]
        x, y, c = _coords()
        me = 2 * x + y
        sibling = (x, y, 1 - c)
        chips = _other_chips(x, y)
        local = [pltpu.make_async_copy(ins[t], outs[t].at[me], local_sems.at[t]) for t in range(n)]
        for cp in local:
            cp.start()
        sends = []
        for t in range(n):
            for j, (px, py) in enumerate(chips):
                cp = _remote(ins[t].at[c], outs[t].at[me, c], send_sems, recv_sems, per * t + j, (px, py, c))
                cp.start()
                sends.append(cp)
        for t in range(n):
            for j, (px, py) in enumerate(chips):
                landed = outs[t].at[2 * px + py, c]
                _remote(landed, landed, send_sems, recv_sems, per * t + j, (px, py, c)).wait_recv()
                cp = _remote(landed, landed, send_sems, recv_sems, per * t + 3 + j, sibling)
                cp.start()
                sends.append(cp)
        for t in range(n):
            for j, (px, py) in enumerate(chips):
                passed = outs[t].at[2 * px + py, 1 - c]
                _remote(passed, passed, send_sems, recv_sems, per * t + 3 + j, sibling).wait_recv()
        for cp in sends:
            cp.wait_send()
        for cp in local:
            cp.wait()

    return _call(body, tuple(shards), name="all_gather_chips",
                 in_specs=[ANY] * n, out_specs=[ANY] * n,
                 out_shape=[jax.ShapeDtypeStruct((N_CHIPS,) + s.shape, s.dtype) for s in shards],
                 scratch_shapes=[pltpu.SemaphoreType.DMA((per * n,)), pltpu.SemaphoreType.DMA((per * n,)),
                                 pltpu.SemaphoreType.DMA((n,))])


HBM = pl.BlockSpec(memory_space=pltpu.HBM)
SEM = pl.BlockSpec(memory_space=pltpu.SEMAPHORE)
DATAFLOW = pltpu.SideEffectType.DATAFLOW_SIDE_EFFECTING


def _in_hbm(a):
    return pltpu.with_memory_space_constraint(a, pltpu.HBM)


PUSH_ARRIVALS = {"gather_chips": N_CHIPS - 1, "scatter_chips": N_CHIPS - 1, "sibling": 1, "gather_devices": N_DEV - 1}


def _push_copies(mode, src, land, send_sems, recv_sems, t):
    x, y, c = _coords()
    chip = 2 * x + y
    if mode == "gather_chips":
        return [_remote(src.at[chip], land.at[chip], send_sems, recv_sems, t, (px, py, c))
                for px, py in _other_chips(x, y)]
    if mode == "scatter_chips":
        return [_remote(src.at[2 * px + py], land.at[chip], send_sems, recv_sems, t, (px, py, c))
                for px, py in _other_chips(x, y)]
    if mode == "sibling":
        return [_remote(src, land, send_sems, recv_sems, t, (x, y, 1 - c))]
    dev = 4 * x + 2 * y + c
    return [_remote(src.at[dev], land.at[dev], send_sems, recv_sems, t, (x ^ (k >> 2), y ^ ((k >> 1) & 1), c ^ (k & 1)))
            for k in range(1, N_DEV)]


def _push_start(srcs, lands, mode, name):
    n = len(srcs)
    same = all(s is ld for s, ld in zip(srcs, lands))
    n_in = n if same else 2 * n

    def body(*refs):
        src_refs = refs[:n]
        land_refs = src_refs if same else refs[n:2 * n]
        send_sems, recv_sems = refs[n_in], refs[n_in + 1]
        token = refs[-1]
        for t in range(n):
            for cp in _push_copies(mode, src_refs[t], land_refs[t], send_sems, recv_sems, t):
                cp.start()
        token[...] = jnp.zeros_like(token)

    operands = tuple(srcs) if same else tuple(srcs) + tuple(lands)
    res = _call(
        body, [_in_hbm(a) for a in operands], name=name,
        out_shape=(pltpu.SemaphoreType.DMA((n,)), pltpu.SemaphoreType.DMA((n,)))
        + tuple(pltpu.HBM(a.shape, a.dtype) for a in operands) + (jax.ShapeDtypeStruct((SUBLANES, LANES), F32),),
        in_specs=[HBM] * n_in, out_specs=(SEM, SEM) + (HBM,) * n_in + (pl.BlockSpec(memory_space=pltpu.VMEM),),
        input_output_aliases={i: 2 + i for i in range(n_in)}, side_effects=DATAFLOW, hbm_results=False)
    send_sems, recv_sems, token = res[0], res[1], res[-1]
    srcs_thru = res[2:2 + n]
    lands_thru = srcs_thru if same else res[2 + n:2 + 2 * n]
    return send_sems, recv_sems, srcs_thru, lands_thru, token


def _push_wait(send_sems, recv_sems, ids, srcs, lands, mode, after, name):
    n = len(lands)
    same = all(s is ld for s, ld in zip(srcs, lands))
    n_in = n if same else 2 * n

    def body(*refs):
        land_refs = refs[:n] if same else refs[n:2 * n]
        send_sems, recv_sems = refs[n_in], refs[n_in + 1]
        x, y, c = _coords()
        for t in range(n):
            moved = land_refs[t] if mode == "sibling" else land_refs[t].at[pl.ds(0, PUSH_ARRIVALS[mode])]
            arrivals = _remote(moved, moved, send_sems, recv_sems, ids[t], (x, y, c))
            arrivals.wait_send()
            arrivals.wait_recv()

    operands = tuple(lands) if same else tuple(srcs) + tuple(lands)
    res = _call(
        body, operands + (send_sems, recv_sems, after), name=name,
        out_shape=tuple(pltpu.HBM(a.shape, a.dtype) for a in operands),
        in_specs=[HBM] * n_in + [SEM, SEM, ANY], out_specs=(HBM,) * n_in,
        input_output_aliases={i: i for i in range(n_in)}, side_effects=DATAFLOW)
    return list(res) if same else (list(res[:n]), list(res[n:]))


def _sum_partials(part, landed, chip):
    _, rows, cols = part.shape
    br = _divisor_tile(rows, 16, ELEM_ROWS)

    def body(chip_ref, own_ref, a_ref, b_ref, c_ref, o_ref):
        o_ref[...] = ((own_ref[...].astype(F32) + a_ref[...].astype(F32)) + b_ref[...].astype(F32)) \
            + c_ref[...].astype(F32)

    def other(k):
        return pl.BlockSpec((None, br, cols), lambda i, ch: (jnp.where(ch[0] <= k, k + 1, k), i, 0))

    spec = pltpu.PrefetchScalarGridSpec(
        num_scalar_prefetch=1, grid=(rows // br,),
        in_specs=[pl.BlockSpec((None, br, cols), lambda i, ch: (ch[0], i, 0)), other(0), other(1), other(2)],
        out_specs=pl.BlockSpec((br, cols), lambda i, ch: (i, 0)))
    return _call(body, (chip, part, landed, landed, landed), name="sum_partials", grid_spec=spec,
                 out_shape=jax.ShapeDtypeStruct((rows, cols), F32), semantics=("parallel",))


def _cast_to_slab(w, l, chip):
    _, rows, cols = w.shape
    br = _divisor_tile(rows, 16, ELEM_ROWS)

    def body(chip_ref, w_ref, o_ref):
        o_ref[...] = w_ref[...].astype(BF16)

    spec = pltpu.PrefetchScalarGridSpec(
        num_scalar_prefetch=1, grid=(rows // br,),
        in_specs=[pl.BlockSpec((None, br, cols), lambda i, ch: (l, i, 0))],
        out_specs=pl.BlockSpec((None, br, cols), lambda i, ch: (ch[0], i, 0)))
    return _call(body, (chip, w), name="cast_to_slab", grid_spec=spec,
                 out_shape=jax.ShapeDtypeStruct((N_CHIPS, rows, cols), BF16), semantics=("parallel",))


def _cast_w_in_t_to_slabs(w_t, chip):
    rows, depth, d = w_t.shape
    tn = _divisor_tile(d, LANES, 256)

    def body(chip_ref, w_ref, *o_refs):
        for l in range(depth):
            o_refs[l][...] = w_ref[:, l, :].astype(BF16)

    spec = pltpu.PrefetchScalarGridSpec(
        num_scalar_prefetch=1, grid=(d // tn,),
        in_specs=[pl.BlockSpec((rows, depth, tn), lambda j, ch: (0, 0, j))],
        out_specs=[pl.BlockSpec((None, rows, tn), lambda j, ch: (ch[0], 0, j))] * depth)
    return _call(body, (chip, w_t), name="cast_w_in_t_to_slabs", grid_spec=spec,
                 out_shape=[jax.ShapeDtypeStruct((N_CHIPS, rows, d), BF16)] * depth, semantics=("parallel",),
                 vmem_bytes=4 * _nbytes((rows, max(depth, SUBLANES), tn), F32))


def _place_slab(buf, index, n_slabs):
    rows, cols = buf.shape
    br = _divisor_tile(rows, SUBLANES, ELEM_ROWS)

    def body(index_ref, b_ref, o_ref):
        o_ref[...] = b_ref[...]

    spec = pltpu.PrefetchScalarGridSpec(
        num_scalar_prefetch=1, grid=(rows // br,),
        in_specs=[pl.BlockSpec((br, cols), lambda i, ix: (i, 0))],
        out_specs=pl.BlockSpec((None, br, cols), lambda i, ix: (ix[0], i, 0)))
    return _call(body, (index, buf), name="place_slab", grid_spec=spec,
                 out_shape=jax.ShapeDtypeStruct((n_slabs, rows, cols), buf.dtype), semantics=("parallel",))


ELEM_ROWS = 256


def _sum_slabs(r):
    n, rows, cols = r.shape
    br = _divisor_tile(rows, 16, ELEM_ROWS)

    def body(r_ref, o_ref):
        acc = r_ref[0].astype(F32)
        for j in range(1, n):
            acc = acc + r_ref[j].astype(F32)
        o_ref[...] = acc

    return _call(body, (r,), name="sum_slabs", grid=(rows // br,),
                 in_specs=[pl.BlockSpec((n, br, cols), lambda i: (0, i, 0))],
                 out_specs=pl.BlockSpec((br, cols), lambda i: (i, 0)),
                 out_shape=jax.ShapeDtypeStruct((rows, cols), F32), semantics=("parallel",))


def _adamw_math(w, g, m, v):
    c1 = 1.0 - ADAM_B1 ** ADAM_STEP
    c2 = 1.0 - ADAM_B2 ** ADAM_STEP
    nm = ADAM_B1 * m + (1.0 - ADAM_B1) * g
    nv = ADAM_B2 * v + (1.0 - ADAM_B2) * (g * g)
    delta = -ADAM_LR * ((nm / c1) / (jnp.sqrt(nv / c2) + ADAM_EPS) + ADAM_WD * w)
    return delta, nm, nv


def _adamw(w, g, m, v):
    rows, cols = w.shape
    br = _divisor_tile(rows, 8, ELEM_ROWS)

    def body(w_ref, g_ref, m_ref, v_ref, d_ref, nm_ref, nv_ref):
        d_ref[...], nm_ref[...], nv_ref[...] = _adamw_math(w_ref[...], g_ref[...], m_ref[...], v_ref[...])

    blk = pl.BlockSpec((br, cols), lambda i: (i, 0))
    return _call(body, (w, g, m, v), name="adamw", grid=(rows // br,),
                 in_specs=[blk] * 4, out_specs=[blk] * 3,
                 out_shape=[jax.ShapeDtypeStruct((rows, cols), F32)] * 3, semantics=("parallel",))


def _adamw_w_in_t(w_t, m_t, v_t, g_mine, g_theirs):
    rows, depth, d = w_t.shape
    tn = LANES

    def body(w_ref, m_ref, v_ref, ga_ref, gb_ref, g_ref, d_ref, nm_ref, nv_ref):
        g = ga_ref[...] + gb_ref[...]
        g_ref[...] = g
        d_ref[...], nm_ref[...], nv_ref[...] = _adamw_math(w_ref[...], g, m_ref[...], v_ref[...])

    slab = pl.BlockSpec((rows, depth, tn), lambda j: (0, 0, j))
    return _call(body, (w_t, m_t, v_t, g_mine, g_theirs), name="adamw_w_in_t", grid=(d // tn,),
                 in_specs=[slab] * 5, out_specs=[slab] * 4,
                 out_shape=[jax.ShapeDtypeStruct(w_t.shape, F32)] * 4, semantics=("parallel",),
                 vmem_bytes=2 * 9 * _nbytes((rows, max(depth, SUBLANES), tn), F32))


def _adamw_layer(w, m, v, l, g_mine, g_theirs, prev):
    _, rows, cols = w.shape
    br = _divisor_tile(rows, 8, ELEM_ROWS)

    def body(w_ref, m_ref, v_ref, ga_ref, gb_ref, *rest):
        g_ref, d_ref, nm_ref, nv_ref = rest[4:]
        g = ga_ref[...] + gb_ref[...]
        g_ref[...] = g
        d_ref[...], nm_ref[...], nv_ref[...] = _adamw_math(w_ref[...], g, m_ref[...], v_ref[...])

    slot = pl.BlockSpec((None, br, cols), lambda i: (l, i, 0))
    blk = pl.BlockSpec((br, cols), lambda i: (i, 0))
    return _call(body, (w, m, v, g_mine, g_theirs) + tuple(prev), name="adamw_layer", grid=(rows // br,),
                 in_specs=[slot] * 3 + [blk] * 2 + [ANY] * 4, out_specs=[slot] * 4,
                 out_shape=[jax.ShapeDtypeStruct(w.shape, F32)] * 4,
                 input_output_aliases={5: 0, 6: 1, 7: 2, 8: 3}, semantics=("parallel",))


BIG = ("w_in", "w_out", "w_up", "w_down")
WEIGHTS = ("meta", "attn_norm_g", "w_in", "b_f", "conv_w", "conv_b", "w_gate_a", "b_gate_a", "w_gate_x",
           "b_gate_x", "lru_L", "attn_out_g", "rec_out_g", "w_out", "mlp_norm_g", "w_up", "w_down", "final_g")
SMALL = tuple(k for k in WEIGHTS if k not in BIG)
COL_SHARDED_SMALL = ("meta", "conv_w")


def _packed_rows(shape):
    return -(-math.prod(shape) // (SUBLANES * LANES)) * SUBLANES


def _pack(arrs):
    rows = []
    for a in arrs:
        flat = a.reshape(-1)
        rows.append(jnp.pad(flat, (0, _packed_rows(a.shape) * LANES - flat.shape[0])).reshape(-1, LANES))
    used = sum(r.shape[0] for r in rows)
    rows.append(jnp.zeros((-used % ELEM_ROWS, LANES), F32))
    return jnp.concatenate(rows, axis=0)


def _unpack(buf, shapes):
    out, r0 = [], 0
    for s in shapes:
        nr = _packed_rows(s)
        out.append(buf[r0:r0 + nr].reshape(-1)[:math.prod(s)].reshape(s))
        r0 += nr
    return out


def _halves(a):
    return a.reshape((2, a.shape[0] // 2) + a.shape[1:])


def _cols_from_chips(g):
    return jnp.moveaxis(g, 0, -2).reshape(g.shape[1:-1] + (N_CHIPS * g.shape[-1],))


def kernel(x, meta, attn_norm_g, w_in, b_f, conv_w, conv_b, w_gate_a, b_gate_a, w_gate_x, b_gate_x, lru_L, attn_out_g, rec_out_g, w_out, mlp_norm_g, w_up, w_down, final_g, loss_target, m_meta, m_attn_norm_g, m_w_in, m_b_f, m_conv_w, m_conv_b, m_w_gate_a, m_b_gate_a, m_w_gate_x, m_b_gate_x, m_lru_L, m_attn_out_g, m_rec_out_g, m_w_out, m_mlp_norm_g, m_w_up, m_w_down, m_final_g, v_meta, v_attn_norm_g, v_w_in, v_b_f, v_conv_w, v_conv_b, v_w_gate_a, v_b_gate_a, v_w_gate_x, v_b_gate_x, v_lru_L, v_attn_out_g, v_rec_out_g, v_w_out, v_mlp_norm_g, v_w_up, v_w_down, v_final_g):
    w = dict(meta=meta, attn_norm_g=attn_norm_g, w_in=w_in, b_f=b_f, conv_w=conv_w, conv_b=conv_b,
             w_gate_a=w_gate_a, b_gate_a=b_gate_a, w_gate_x=w_gate_x, b_gate_x=b_gate_x, lru_L=lru_L,
             attn_out_g=attn_out_g, rec_out_g=rec_out_g, w_out=w_out, mlp_norm_g=mlp_norm_g, w_up=w_up,
             w_down=w_down, final_g=final_g)
    m = dict(meta=m_meta, attn_norm_g=m_attn_norm_g, w_in=m_w_in, b_f=m_b_f, conv_w=m_conv_w, conv_b=m_conv_b,
             w_gate_a=m_w_gate_a, b_gate_a=m_b_gate_a, w_gate_x=m_w_gate_x, b_gate_x=m_b_gate_x, lru_L=m_lru_L,
             attn_out_g=m_attn_out_g, rec_out_g=m_rec_out_g, w_out=m_w_out, mlp_norm_g=m_mlp_norm_g,
             w_up=m_w_up, w_down=m_w_down, final_g=m_final_g)
    v = dict(meta=v_meta, attn_norm_g=v_attn_norm_g, w_in=v_w_in, b_f=v_b_f, conv_w=v_conv_w, conv_b=v_conv_b,
             w_gate_a=v_w_gate_a, b_gate_a=v_b_gate_a, w_gate_x=v_w_gate_x, b_gate_x=v_b_gate_x, lru_L=v_lru_L,
             attn_out_g=v_attn_out_g, rec_out_g=v_rec_out_g, w_out=v_w_out, mlp_norm_g=v_mlp_norm_g,
             w_up=v_w_up, w_down=v_w_down, final_g=v_final_g)
    s_len, d = x.shape[1], x.shape[2]
    depth = w_in.shape[0]
    att_w = d // 2
    rec_w = d - att_w
    nh = att_w // HEAD_DIM
    chip = 2 * lax.axis_index("x") + lax.axis_index("y")

    g_conv, g_meta = [g.reshape((N_CHIPS, g.shape[1] * g.shape[2]) + g.shape[3:])
                      for g in _all_gather_chips([_halves(w["conv_w"]), _halves(w["meta"])])]
    p = dict(w)
    p["conv_w"] = _cols_from_chips(g_conv)
    meta_full = jnp.moveaxis(g_meta, 0, 1).reshape(N_META, d)

    chip1 = chip.reshape(1).astype(jnp.int32)
    w_in_t, m_in_t, v_in_t = [jnp.transpose(a["w_in"], (2, 0, 1)) for a in (w, m, v)]
    w_in_slabs = _cast_w_in_t_to_slabs(w_in_t, chip1)
    pushes, tokens = [], []
    for l in range(depth):
        slabs = [w_in_slabs[l]] + [_cast_to_slab(w[k], l, chip1) for k in BIG[1:]]
        send_sems, recv_sems, _, lands, token = _push_start(slabs, slabs, "gather_chips", f"weights_start_{l}")
        pushes.append((send_sems, recv_sems, lands))
        tokens.append(token[0, 0])

    t_len = N_META + s_len
    pad = -t_len % SEQ_TILE
    h = jnp.concatenate([meta_full, x[0], jnp.zeros((pad, d), F32)], axis=0)
    tgt = jnp.concatenate([jnp.zeros((N_META, d), F32), loss_target[0], jnp.zeros((pad, d), F32)], axis=0)
    z = _rms_fwd(h, _row(p["attn_norm_g"][0] + sum(tokens)))
    saved = []
    for l in range(depth):
        send_sems, recv_sems, lands = pushes[l]

        def fetch(k, after, l=l, send_sems=send_sems, recv_sems=recv_sems, lands=lands):
            i = BIG.index(k)
            return _push_wait(send_sems, recv_sems, [i], [lands[i]], [lands[i]], "gather_chips", after,
                              f"{k}_wait_{l}")[0]

        g_next = p["attn_norm_g"][l + 1] if l + 1 < depth else p["final_g"]
        h, z, sv = _forward_layer(l, h, z, p, fetch, g_next)
        saved.append(sv)
    dh, dh_b, dg_final, loss_part = _loss_bwd(h, _row(p["final_g"]), tgt, s_len)

    small = {k: [None] * depth for k in SMALL if k not in ("meta", "final_g")}
    pushes = [None] * depth
    tok = 0.0
    for l in reversed(range(depth)):
        dh, dh_b, big_mlp, sm_mlp = _backward_mlp(l, dh, dh_b, saved[l], p, tok)
        parts = [big_mlp["w_down"], big_mlp["w_up"]]
        push_mlp = _push_start(parts, [lax.empty(a.shape, a.dtype) for a in parts], "scatter_chips",
                               f"mlp_grads_start_{l}")
        dh, dh_b, big_mix, sm_mix = _backward_mixer(l, dh, dh_b, saved[l], p, push_mlp[4][0, 0])
        parts = [big_mix["w_out"], big_mix["w_in"]]
        push_mix = _push_start(parts, [lax.empty(a.shape, a.dtype) for a in parts], "scatter_chips",
                               f"mixer_grads_start_{l}")
        tok = push_mix[4][0, 0]
        pushes[l] = {("w_down", "w_up"): push_mlp, ("w_out", "w_in"): push_mix}
        for k, val in {**sm_mlp, **sm_mix}.items():
            small[k][l] = val
    grads = {k: jnp.stack(val) for k, val in small.items()}
    grads["final_g"] = dg_final[0]
    grads["meta"] = dh[:N_META]
    dx = dh[N_META:t_len]

    full_shapes = [grads[k].shape for k in SMALL] + [(1,)]
    packed = _pack([grads[k].astype(F32) for k in SMALL] + [loss_part[0, :1] + tok])
    dev1 = (2 * chip + lax.axis_index("c")).reshape(1).astype(jnp.int32)
    slabs = [_place_slab(packed, dev1, N_DEV)]
    small_push = _push_start(slabs, slabs, "gather_devices", "small_grads_start")

    last_token = small_push[4]
    outs = {k: [lax.empty(w[k].shape, F32) for _ in range(4)] for k in BIG[1:]}
    w_in_sums = [None] * depth
    swaps = {}

    def finish(l, after):
        send_sems, recv_sems, mine, lands, _ = swaps[l]
        mine, theirs = _push_wait(send_sems, recv_sems, list(range(len(BIG))), mine, lands, "sibling", after,
                                  f"sums_wait_{l}")
        w_in_sums[l] = (mine[0], theirs[0])
        for k, a, b in zip(BIG[1:], mine[1:], theirs[1:]):
            outs[k] = _adamw_layer(w[k], m[k], v[k], l, a, b, outs[k])

    for l in reversed(range(depth)):
        sums = {}
        for names, (send_sems, recv_sems, parts, lands, _) in pushes[l].items():
            parts, landed = _push_wait(send_sems, recv_sems, [0, 1], parts, lands, "scatter_chips", last_token,
                                       f"{names[0]}_grads_wait_{l}")
            for k, part, land in zip(names, parts, landed):
                sums[k] = _sum_partials(part, land, chip1)
        mine = [sums[k] for k in BIG]
        swaps[l] = _push_start(mine, [lax.empty(a.shape, a.dtype) for a in mine], "sibling", f"sums_start_{l}")
        if l + 1 < depth:
            finish(l + 1, outs["w_down"][0] if l + 2 < depth else mine[0])
    finish(0, outs["w_down"][0] if depth > 1 else swaps[0][4])
    outs["w_in"] = [jnp.transpose(r, (1, 2, 0)) for r in _adamw_w_in_t(
        w_in_t, m_in_t, v_in_t, jnp.stack([s[0] for s in w_in_sums], axis=1),
        jnp.stack([s[1] for s in w_in_sums], axis=1))]
    out_g, out_d, out_m, out_v = [{k: outs[k][i] for k in BIG} for i in range(4)]

    landed = _push_wait(small_push[0], small_push[1], [0], small_push[3], small_push[3], "gather_devices",
                        out_g["w_in"], "small_grads_wait")
    total = _sum_slabs(landed[0])
    small_g = dict(zip(SMALL + ("loss",), _unpack(total, full_shapes)))
    for k in COL_SHARDED_SMALL:
        n = w[k].shape[-1]
        small_g[k] = lax.dynamic_slice_in_dim(small_g[k], chip * n, n, axis=small_g[k].ndim - 1)
    local_shapes = [w[k].shape for k in SMALL]
    res = _adamw(_pack([w[k] for k in SMALL]), _pack([small_g[k] for k in SMALL]),
                 _pack([m[k] for k in SMALL]), _pack([v[k] for k in SMALL]))
    out_g.update({k: small_g[k] for k in SMALL})
    for dst, buf in zip((out_d, out_m, out_v), res):
        dst.update(zip(SMALL, _unpack(buf, local_shapes)))

    return (small_g["loss"].reshape(()), dx[None],
            *[out_g[k] for k in WEIGHTS], *[out_d[k] for k in WEIGHTS],
            *[out_m[k] for k in WEIGHTS], *[out_v[k] for k in WEIGHTS])
```

```python
import functools
import math

import jax
import jax.numpy as jnp
from jax import lax
from jax.experimental import pallas as pl
from jax.experimental.pallas import tpu as pltpu

F32 = jnp.float32
BF16 = jnp.bfloat16

N_META = 16
HEAD_DIM = 64
N_REC_BLOCKS = 8
CONV_WIDTH = 4
RG_C = 8.0
NORM_EPS = 1e-6
ADAM_LR = 0.001
ADAM_B1 = 0.9
ADAM_B2 = 0.999
ADAM_EPS = 1e-08
ADAM_WD = 0.01
ADAM_STEP = 10

LANES = 128
SUBLANES = 8
SEQ_TILE = 128
VMEM_CAP = 60 * 2**20
VMEM_SLACK = 6 * 2**20
NEG_BIG = -1e30
N_CHIPS = 4
N_DEV = 8
MESH = pl.DeviceIdType.MESH


def _nbytes(shape, dtype):
    return math.prod(shape) * jnp.dtype(dtype).itemsize


def _call(body, args, *, name, out_shape, grid=(), in_specs=None, out_specs=None, scratch_shapes=(),
          grid_spec=None, semantics=None, vmem_bytes=None, side_effects=None, hbm_results=True, **kw):
    cp = {}
    if semantics is not None:
        cp["dimension_semantics"] = semantics
    if vmem_bytes is not None:
        cp["vmem_limit_bytes"] = int(min(VMEM_CAP, vmem_bytes + VMEM_SLACK))
    if side_effects is not None:
        cp["has_side_effects"] = side_effects
    if grid_spec is not None:
        kw["grid_spec"] = grid_spec
    else:
        kw.update(grid=grid, in_specs=in_specs, out_specs=out_specs, scratch_shapes=scratch_shapes)
    if hbm_results:
        out_shape = jax.tree.map(
            lambda s: pltpu.HBM(s.shape, s.dtype) if isinstance(s, jax.ShapeDtypeStruct) else s, out_shape)
    fn = pl.pallas_call(
        body, name=name, out_shape=out_shape,
        compiler_params=pltpu.CompilerParams(**cp), **kw)
    return fn(*[_in_hbm(a) if jnp.issubdtype(getattr(a, "dtype", jnp.int32), jnp.floating) else a for a in args])


def _divisor_tile(n, unit, target):
    best = None
    for t in range(unit, min(n, target) + 1, unit):
        if n % t == 0:
            best = t
    return n if best is None else best


def _sigmoid(x):
    return 1.0 / (1.0 + jnp.exp(-x))


def _log1p_unit(e):
    series = e * (1.0 - e * (0.5 - e * (1.0 / 3.0)))
    return jnp.where(e < 1e-2, series, jnp.log(1.0 + e))


def _log_sigmoid(x):
    return jnp.minimum(x, 0.0) - _log1p_unit(jnp.exp(-jnp.abs(x)))


def _one_minus_exp(x, exp_x):
    small = -x * (1.0 + x * (1.0 / 2 + x * (1.0 / 6 + x * (1.0 / 24 + x * (1.0 / 120 + x * (1.0 / 720))))))
    return jnp.where(x > -0.25, small, 1.0 - exp_x)


_GELU_K = math.sqrt(2.0 / math.pi)
_GELU_C = 0.044715


def _gelu_and_grad(y):
    th = jnp.tanh(_GELU_K * (y + _GELU_C * y * y * y))
    g = 0.5 * y * (1.0 + th)
    dg = 0.5 * (1.0 + th) + 0.5 * y * (1.0 - th * th) * _GELU_K * (1.0 + 3.0 * _GELU_C * y * y)
    return g, dg


def _rstd(x):
    return lax.rsqrt(jnp.mean(x * x, axis=-1, keepdims=True) + NORM_EPS)


def _rms_bwd(dz, x, g):
    rs = _rstd(x)
    xh = x * rs
    dgp = jnp.sum(dz * xh, axis=0, keepdims=True)
    dxh = dz * g
    dx = rs * (dxh - xh * jnp.mean(dxh * xh, axis=-1, keepdims=True))
    return dx, dgp


def _dot(a, b):
    return jnp.dot(a, b, preferred_element_type=F32)


def _dot_nt(a, b):
    return lax.dot_general(a, b, (((1,), (1,)), ((), ())), preferred_element_type=F32)


def _dot_tn(a, b):
    return lax.dot_general(a, b, (((0,), (0,)), ((), ())), preferred_element_type=F32)


def _full(shape):
    nd = len(shape)
    return pl.BlockSpec(shape, lambda *_: (0,) * nd)


def _rms_fwd(h, g):
    tp, d = h.shape
    tm = _divisor_tile(tp, 16, 544)

    def body(h_ref, g_ref, z_ref):
        x = h_ref[...]
        z_ref[...] = (x * _rstd(x) * g_ref[...]).astype(BF16)

    return _call(body, (h, g), name="rms_fwd", grid=(tp // tm,),
                 in_specs=[pl.BlockSpec((tm, d), lambda i: (i, 0)), _full((1, d))],
                 out_specs=pl.BlockSpec((tm, d), lambda i: (i, 0)),
                 out_shape=jax.ShapeDtypeStruct((tp, d), BF16), semantics=("parallel",))


def _proj(z, w_big_t, att_w):
    tp, d = z.shape
    nb = w_big_t.shape[0]
    tn = _divisor_tile(nb, LANES, 512)
    assert (3 * att_w) % tn == 0
    n_qkv = 3 * att_w // tn
    scale = 1.0 / math.sqrt(HEAD_DIM)

    def body(z_ref, w_ref, qkv_ref, p_ref):
        j = pl.program_id(0)
        acc = _dot_nt(z_ref[...], w_ref[...])

        @pl.when(j < n_qkv)
        def _():
            col = j * tn + lax.broadcasted_iota(jnp.int32, (1, tn), 1)
            qkv_ref[...] = (acc * jnp.where(col < att_w, scale, 1.0)).astype(BF16)

        @pl.when(j >= n_qkv)
        def _():
            p_ref[...] = acc

    vm = 2 * (_nbytes((tp, d), BF16) + _nbytes((d, tn), BF16) + _nbytes((tp, tn), F32) * 2)
    return _call(body, (z, w_big_t), name="proj", grid=(nb // tn,),
                 in_specs=[_full((tp, d)), pl.BlockSpec((tn, d), lambda j: (j, 0))],
                 out_specs=[pl.BlockSpec((tp, tn), lambda j: (0, jnp.minimum(j, n_qkv - 1))),
                            pl.BlockSpec((tp, tn), lambda j: (0, jnp.maximum(j - n_qkv, 0)))],
                 out_shape=[jax.ShapeDtypeStruct((tp, 3 * att_w), BF16),
                            jax.ShapeDtypeStruct((tp, nb - 3 * att_w), F32)],
                 semantics=("arbitrary",), vmem_bytes=vm)


def _tile_cumsum(x, row, reverse=False):
    for s in (1, 2, 4):
        if reverse:
            x = x + jnp.where(row < SUBLANES - s, pltpu.roll(x, SUBLANES - s, 0), 0.0)
        else:
            x = x + jnp.where(row >= s, pltpu.roll(x, s, 0), 0.0)
    return x


def _fgate_fwd(proj, b_f_pad, nh):
    tp, nb = proj.shape
    fblk = nb // LANES - 1

    def body(f_ref, b_ref, c_ref, ct_ref):
        b = b_ref[...]
        row = lax.broadcasted_iota(jnp.int32, (SUBLANES, LANES), 0)

        def step(i, carry):
            r0 = pl.multiple_of(i * SUBLANES, SUBLANES)
            lf = _log_sigmoid(f_ref[pl.ds(r0, SUBLANES), :] + b)
            x = _tile_cumsum(lf, row) + carry
            c_ref[pl.ds(r0, SUBLANES), :] = x
            return x[SUBLANES - 1:SUBLANES, :]

        lax.fori_loop(0, tp // SUBLANES, step, jnp.zeros((1, LANES), F32))
        ct_ref[...] = c_ref[...].T[:nh, :]

    return _call(body, (proj, b_f_pad), name="fgate_fwd", grid=(1,),
                 in_specs=[pl.BlockSpec((tp, LANES), lambda i: (0, fblk)), _full((1, LANES))],
                 out_specs=[_full((tp, LANES)), _full((nh, tp))],
                 out_shape=[jax.ShapeDtypeStruct((tp, LANES), F32), jax.ShapeDtypeStruct((nh, tp), F32)],
                 semantics=("arbitrary",))


def _fgate_bwd(proj, b_f_pad, dc):
    tp, nb = proj.shape
    fblk = nb // LANES - 1

    def body(f_ref, b_ref, dc_ref, df_ref, db_ref, dc_s):
        b = b_ref[...]
        row = lax.broadcasted_iota(jnp.int32, (SUBLANES, LANES), 0)
        nt = tp // SUBLANES

        def step(i, carry):
            suffix, acc = carry
            r0 = pl.multiple_of((nt - 1 - i) * SUBLANES, SUBLANES)
            dlf = _tile_cumsum(dc_ref[pl.ds(r0, SUBLANES), :], row, reverse=True) + suffix
            df = dlf * _sigmoid(-(f_ref[pl.ds(r0, SUBLANES), :] + b))
            dc_s[pl.ds(r0, SUBLANES), :] = df
            return dlf[0:1, :], acc + df

        _, acc = lax.fori_loop(0, nt, step, (jnp.zeros((1, LANES), F32), jnp.zeros((SUBLANES, LANES), F32)))
        df_ref[...] = dc_s[...].astype(BF16)
        db_ref[...] = jnp.broadcast_to(jnp.sum(acc, axis=0, keepdims=True), (SUBLANES, LANES))

    return _call(body, (proj, b_f_pad, dc), name="fgate_bwd", grid=(1,),
                 in_specs=[pl.BlockSpec((tp, LANES), lambda i: (0, fblk)), _full((1, LANES)), _full((tp, LANES))],
                 out_specs=[_full((tp, LANES)), _full((SUBLANES, LANES))],
                 out_shape=[jax.ShapeDtypeStruct((tp, LANES), BF16),
                            jax.ShapeDtypeStruct((SUBLANES, LANES), F32)],
                 scratch_shapes=[pltpu.VMEM((tp, LANES), F32)], semantics=("arbitrary",))


ATT_BQ = 128


ATT_BUCKET = 3
ATT_HEADS = 4


def _for_bucket(i, nq, fn):
    for lo in range(0, nq, ATT_BUCKET):
        hi = min(lo + ATT_BUCKET, nq)
        spans = ([(0, lo * ATT_BQ, False)] if lo else []) + [(lo * ATT_BQ, hi * ATT_BQ, True)]
        pl.when(jnp.logical_and(i >= lo, i < hi))(functools.partial(fn, spans))


def _head_column(c_blk, h):
    lane = lax.broadcasted_iota(jnp.int32, c_blk.shape, 1)
    return jnp.sum(jnp.where(lane == h, c_blk, 0.0), axis=1, keepdims=True)


def _key_major_logits(kh_ref, ck_ref, hh, q, span, q0):
    k0, k1, needs_mask = span
    t = _dot_nt(kh_ref[hh, k0:k1, :], q) - ck_ref[hh, k0:k1, :]
    if needs_mask:
        keys = k0 + lax.broadcasted_iota(jnp.int32, (k1 - k0, ATT_BQ), 0)
        t = jnp.where(keys <= q0 + lax.broadcasted_iota(jnp.int32, (k1 - k0, ATT_BQ), 1), t, NEG_BIG)
    return t


def _stage_keys(p, k_ref, c_ref, kh_s, ck_s):
    for hh in range(ATT_HEADS):
        kh_s[hh] = k_ref[:, HEAD_DIM * hh:HEAD_DIM * (hh + 1)]
        ck_s[hh] = jnp.broadcast_to(_head_column(c_ref[...], ATT_HEADS * p + hh), ck_s.shape[1:])


def _attn_fwd(qkv, c, c_t, nh):
    tp = qkv.shape[0]
    att_w = nh * HEAD_DIM
    ng = nh // ATT_HEADS
    gw = ATT_HEADS * HEAD_DIM
    bq = ATT_BQ
    nq = tp // bq

    def body(q_ref, k_ref, v_ref, c_ref, ct_ref, o_ref, lse_ref, ck_s, kh_s, vt_s):
        p = pl.program_id(0)
        i = pl.program_id(1)

        @pl.when(i == 0)
        def _():
            _stage_keys(p, k_ref, c_ref, kh_s, ck_s)
            vt_s[...] = v_ref[...].astype(F32).T.astype(BF16)

        def compute(spans):
            q0 = pl.multiple_of(i * bq, bq)
            o_t, lses = [], []
            for hh in range(ATT_HEADS):
                lo = HEAD_DIM * hh
                q = q_ref[:, lo:lo + HEAD_DIM]
                ts = [_key_major_logits(kh_s, ck_s, hh, q, sp, q0) for sp in spans]
                m = functools.reduce(jnp.maximum, [jnp.max(t, axis=0, keepdims=True) for t in ts])
                es = [jnp.exp(t - m) for t in ts]
                l = sum(jnp.sum(e, axis=0, keepdims=True) for e in es)
                o = sum(_dot(vt_s[lo:lo + HEAD_DIM, k0:k1], e.astype(BF16)) for e, (k0, k1, _) in zip(es, spans))
                o_t.append(o / l)
                lses.append(m + ct_ref[pl.ds(ATT_HEADS * p + hh, 1), :] + jnp.log(l))
            o_ref[...] = jnp.concatenate(o_t, axis=0).T
            lse_ref[...] = jnp.concatenate(lses, axis=0)

        _for_bucket(i, nq, compute)

    blk = pl.BlockSpec((bq, gw), lambda p, i: (i, p))
    vm = 6 * _nbytes((tp, gw), BF16) + 2 * ATT_HEADS * _nbytes((tp, LANES), F32) + 2 * _nbytes((tp, LANES), F32) \
        + 8 * ATT_HEADS * _nbytes((bq, tp), F32)
    return _call(body, (qkv, qkv, qkv, c, c_t), name="attn_fwd", grid=(ng, nq),
                 in_specs=[blk,
                           pl.BlockSpec((tp, gw), lambda p, i: (0, ng + p)),
                           pl.BlockSpec((tp, gw), lambda p, i: (0, 2 * ng + p)),
                           _full((tp, LANES)), pl.BlockSpec((nh, bq), lambda p, i: (0, i))],
                 out_specs=[blk, pl.BlockSpec((None, ATT_HEADS, bq), lambda p, i: (p, 0, i))],
                 out_shape=[jax.ShapeDtypeStruct((tp, att_w), F32), jax.ShapeDtypeStruct((ng, ATT_HEADS, tp), F32)],
                 scratch_shapes=[pltpu.VMEM((ATT_HEADS, tp, LANES), F32), pltpu.VMEM((ATT_HEADS, tp, HEAD_DIM), BF16),
                                 pltpu.VMEM((gw, tp), BF16)],
                 semantics=("arbitrary", "arbitrary"), vmem_bytes=vm)


def _attn_bwd(qkv, c, c_t, lse_b, do, nh):
    tp = qkv.shape[0]
    att_w = nh * HEAD_DIM
    ng = nh // ATT_HEADS
    gw = ATT_HEADS * HEAD_DIM
    bq = ATT_BQ
    nq = tp // bq
    scale = 1.0 / math.sqrt(HEAD_DIM)

    def body(q_ref, k_ref, v_ref, c_ref, ct_ref, lse_ref, do_ref, dq_ref, dk_ref, dv_ref, dc_ref,
             dk_s, dv_s, dc_s, ck_s, kt_s, kh_s, vh_s):
        p = pl.program_id(0)
        i = pl.program_id(1)

        @pl.when(i == 0)
        def _():
            dk_s[...] = jnp.zeros_like(dk_s)
            dv_s[...] = jnp.zeros_like(dv_s)
            dc_s[...] = jnp.zeros_like(dc_s)
            kt_s[...] = k_ref[...].astype(F32).T.astype(BF16)
            _stage_keys(p, k_ref, c_ref, kh_s, ck_s)
            for hh in range(ATT_HEADS):
                vh_s[hh] = v_ref[:, HEAD_DIM * hh:HEAD_DIM * (hh + 1)]

        @pl.when(jnp.logical_and(i == 0, p == 0))
        def _():
            dc_ref[...] = jnp.zeros_like(dc_ref)

        def compute(spans):
            q0 = pl.multiple_of(i * bq, bq)
            dq_t = []
            for hh in range(ATT_HEADS):
                lo = HEAD_DIM * hh
                h = ATT_HEADS * p + hh
                q = q_ref[:, lo:lo + HEAD_DIM]
                doutb = do_ref[:, lo:lo + HEAD_DIM].astype(BF16)
                col_term = ct_ref[pl.ds(h, 1), :] - lse_ref[hh:hh + 1, :]
                prs, dps = [], []
                for sp in spans:
                    prs.append(jnp.exp(_key_major_logits(kh_s, ck_s, hh, q, sp, q0) + col_term))
                    dps.append(_dot_nt(vh_s[hh, sp[0]:sp[1], :], doutb))
                key_sum = sum(jnp.sum(pr * dp, axis=0, keepdims=True) for pr, dp in zip(prs, dps))
                dq_h = 0.0
                for (k0, k1, _), pr, dp in zip(spans, prs, dps):
                    ds = pr * (dp - key_sum)
                    dsb = ds.astype(BF16)
                    dq_h = dq_h + _dot(kt_s[lo:lo + HEAD_DIM, k0:k1], dsb)
                    dk_s[hh, k0:k1, :] += _dot(dsb, q)
                    dv_s[hh, k0:k1, :] += _dot(pr.astype(BF16), doutb)
                    dc_s[hh, k0:k1, :] += jnp.sum(ds, axis=1, keepdims=True)
                dq_t.append(dq_h)
            dq_ref[...] = (jnp.concatenate(dq_t, axis=0) * scale).T.astype(BF16)

        _for_bucket(i, nq, compute)

        @pl.when(i == nq - 1)
        def _():
            dk_ref[...] = jnp.concatenate([dk_s[hh] for hh in range(ATT_HEADS)], axis=1).astype(BF16)
            dv_ref[...] = jnp.concatenate([dv_s[hh] for hh in range(ATT_HEADS)], axis=1).astype(BF16)
            lane = lax.broadcasted_iota(jnp.int32, (tp, LANES), 1)
            dc = dc_ref[...]
            for hh in range(ATT_HEADS):
                dc = jnp.where(lane == ATT_HEADS * p + hh, -dc_s[hh], dc)
            dc_ref[...] = dc

    blk = pl.BlockSpec((bq, gw), lambda p, i: (i, p))
    col = pl.BlockSpec((tp, gw), lambda p, i: (0, p))
    vm = 7 * _nbytes((tp, gw), BF16) + 2 * _nbytes((tp, gw), F32) + 2 * ATT_HEADS * _nbytes((tp, LANES), F32) \
        + 4 * _nbytes((tp, LANES), F32) + 12 * ATT_HEADS * _nbytes((bq, tp), F32)
    return _call(body, (qkv, qkv, qkv, c, c_t, lse_b, do), name="attn_bwd", grid=(ng, nq),
                 in_specs=[blk,
                           pl.BlockSpec((tp, gw), lambda p, i: (0, ng + p)),
                           pl.BlockSpec((tp, gw), lambda p, i: (0, 2 * ng + p)),
                           _full((tp, LANES)), pl.BlockSpec((nh, bq), lambda p, i: (0, i)),
                           pl.BlockSpec((None, ATT_HEADS, bq), lambda p, i: (p, 0, i)), blk],
                 out_specs=[blk, col, col, _full((tp, LANES))],
                 out_shape=[jax.ShapeDtypeStruct((tp, att_w), BF16)] * 3 + [jax.ShapeDtypeStruct((tp, LANES), F32)],
                 scratch_shapes=[pltpu.VMEM((ATT_HEADS, tp, HEAD_DIM), F32), pltpu.VMEM((ATT_HEADS, tp, HEAD_DIM), F32),
                                 pltpu.VMEM((ATT_HEADS, tp, 1), F32), pltpu.VMEM((ATT_HEADS, tp, LANES), F32),
                                 pltpu.VMEM((gw, tp), BF16), pltpu.VMEM((ATT_HEADS, tp, HEAD_DIM), BF16),
                                 pltpu.VMEM((ATT_HEADS, tp, HEAD_DIM), BF16)],
                 semantics=("arbitrary", "arbitrary"), vmem_bytes=vm)


REC_ROWS = 128
HALO = SUBLANES


def _conv_taps(cat):
    taps = []
    for k in range(CONV_WIDTH):
        sh = CONV_WIDTH - 1 - k
        taps.append((pltpu.roll(cat, sh, 0) if sh else cat)[HALO:])
    return taps


def _rec_gates(xc, wa_ref, ba_ref, wx_ref, bx_ref, l_ref):
    xcb = xc.astype(BF16)
    r = _sigmoid(_dot(xcb, wa_ref[...]) + ba_ref[...])
    ig = _sigmoid(_dot(xcb, wx_ref[...]) + bx_ref[...])
    ls = _log_sigmoid(l_ref[...])
    log_a = RG_C * r * ls
    return xcb, r, ig, ls, log_a


def _rec_fwd(proj, xr_blk, yr_blk, rec_w, conv_w, conv_b, wa, ba, wx, bx, lru):
    tp = proj.shape[0]
    w = rec_w
    r_rows = REC_ROWS
    nc = tp // r_rows
    cpb = w // LANES

    def body(xr_ref, yr_ref, cw_ref, cb_ref, wa_ref, ba_ref, wx_ref, bx_ref, l_ref,
             hr_ref, rec_ref, prev_s, carry_s, a_s, u_s):
        i = pl.program_id(0)

        @pl.when(i == 0)
        def _():
            prev_s[...] = jnp.zeros_like(prev_s)
            carry_s[...] = jnp.zeros_like(carry_s)

        x = xr_ref[...]
        taps = _conv_taps(jnp.concatenate([prev_s[...], x], axis=0))
        prev_s[...] = x[r_rows - HALO:]
        xc = cb_ref[...]
        for k in range(CONV_WIDTH):
            xc = xc + cw_ref[k:k + 1, :] * taps[k]
        _, r, ig, ls, log_a = _rec_gates(xc, wa_ref, ba_ref, wx_ref, bx_ref, l_ref)
        a = jnp.exp(log_a)
        a_s[...] = a
        u_s[...] = jnp.sqrt(_one_minus_exp(2.0 * log_a, a * a)) * ig * xc

        def tile(j, h):
            r0 = pl.multiple_of(j * SUBLANES, SUBLANES)
            at = a_s[pl.ds(r0, SUBLANES), :]
            ut = u_s[pl.ds(r0, SUBLANES), :]
            out = []
            for rr in range(SUBLANES):
                h = at[rr:rr + 1] * h + ut[rr:rr + 1]
                out.append(h)
            hr_ref[pl.ds(r0, SUBLANES), :] = jnp.concatenate(out, axis=0)
            return h

        carry_s[0:1, :] = lax.fori_loop(0, r_rows // SUBLANES, tile, carry_s[0:1, :])
        g, _ = _gelu_and_grad(yr_ref[...])
        rec_ref[...] = hr_ref[...] * g

    blk = pl.BlockSpec((r_rows, w), lambda i: (i, 0))
    vm = 16 * _nbytes((r_rows, w), F32) + 4 * _nbytes((w, w), BF16)
    return _call(body, (proj, proj, conv_w, conv_b, wa, ba, wx, bx, lru), name="rec_fwd", grid=(nc,),
                 in_specs=[pl.BlockSpec((r_rows, w), lambda i: (i, xr_blk)),
                           pl.BlockSpec((r_rows, w), lambda i: (i, yr_blk)),
                           _full((CONV_WIDTH, w)), _full((1, w)), _full((w, w)), _full((1, w)),
                           _full((w, w)), _full((1, w)), _full((1, w))],
                 out_specs=[blk, blk],
                 out_shape=[jax.ShapeDtypeStruct((tp, w), F32)] * 2,
                 scratch_shapes=[pltpu.VMEM((HALO, w), F32), pltpu.VMEM((SUBLANES, w), F32),
                                 pltpu.VMEM((r_rows, w), F32), pltpu.VMEM((r_rows, w), F32)],
                 semantics=("arbitrary",), vmem_bytes=vm)


def _rec_bwd(proj, xr_blk, yr_blk, rec_w, hr, drec, conv_w, conv_b, wa, ba, wx, bx, lru):
    tp = proj.shape[0]
    w = rec_w
    r_rows = REC_ROWS
    nc = tp // r_rows
    hpc = r_rows // HALO

    def body(xr_ref, xh_ref, yr_ref, hr_ref, hh_ref, drec_ref, cw_ref, cb_ref, wa_ref, ba_ref, wx_ref, bx_ref,
             l_ref, dxr_ref, dyr_ref, dwa_ref, dwx_ref, small_ref, lam_s, a_s, dhr_s, carry_s, next_s):
        i = pl.program_id(0)
        first = (nc - 1 - i) == 0

        @pl.when(i == 0)
        def _():
            carry_s[...] = jnp.zeros_like(carry_s)
            next_s[...] = jnp.zeros_like(next_s)
            dwa_ref[...] = jnp.zeros_like(dwa_ref)
            dwx_ref[...] = jnp.zeros_like(dwx_ref)
            small_ref[...] = jnp.zeros_like(small_ref)

        x = xr_ref[...]
        xprev = jnp.where(first, 0.0, xh_ref[...])
        taps = _conv_taps(jnp.concatenate([xprev, x], axis=0))
        xc = cb_ref[...]
        for k in range(CONV_WIDTH):
            xc = xc + cw_ref[k:k + 1, :] * taps[k]
        xcb, r, ig, ls, log_a = _rec_gates(xc, wa_ref, ba_ref, wx_ref, bx_ref, l_ref)
        a = jnp.exp(log_a)
        a2 = a * a
        mult = jnp.sqrt(_one_minus_exp(2.0 * log_a, a2))
        g, dg = _gelu_and_grad(yr_ref[...])
        hr_v = hr_ref[...]
        drec_v = drec_ref[...]
        dhr_s[...] = drec_v * g
        dyr_ref[...] = (drec_v * hr_v * dg).astype(BF16)
        a_s[...] = a

        def tile(jj, carry):
            r0 = pl.multiple_of((r_rows // SUBLANES - 1 - jj) * SUBLANES, SUBLANES)
            at = a_s[pl.ds(r0, SUBLANES), :]
            dt = dhr_s[pl.ds(r0, SUBLANES), :]
            out = [None] * SUBLANES
            for rr in range(SUBLANES - 1, -1, -1):
                lam = dt[rr:rr + 1] + carry
                out[rr] = lam
                carry = at[rr:rr + 1] * lam
            lam_s[pl.ds(r0, SUBLANES), :] = jnp.concatenate(out, axis=0)
            return carry

        carry_s[0:1, :] = lax.fori_loop(0, r_rows // SUBLANES, tile, carry_s[0:1, :])
        lam = lam_s[...]
        hprev = jnp.where(first, 0.0, hh_ref[...])
        hr_prev = pltpu.roll(jnp.concatenate([hprev, hr_v], axis=0), 1, 0)[HALO:]
        da = lam * hr_prev
        dxc = lam * mult * ig
        di = lam * mult * xc
        dmult = lam * ig * xc
        dlog_a = da * a - dmult * a2 / mult
        dr = dlog_a * (RG_C * ls)
        dls = jnp.sum(dlog_a * (RG_C * r), axis=0, keepdims=True)
        dga = dr * r * (1.0 - r)
        dgx = di * ig * (1.0 - ig)
        dgab = dga.astype(BF16)
        dgxb = dgx.astype(BF16)
        dxc = dxc + _dot_nt(dgab, wa_ref[...]) + _dot_nt(dgxb, wx_ref[...])
        dwa_ref[...] += _dot_tn(xcb, dgab)
        dwx_ref[...] += _dot_tn(xcb, dgxb)
        cat = jnp.concatenate([dxc, next_s[...]], axis=0)
        next_s[...] = dxc[0:HALO]
        dxr = cw_ref[CONV_WIDTH - 1:CONV_WIDTH, :] * dxc
        for k in range(CONV_WIDTH - 1):
            sh = CONV_WIDTH - 1 - k
            dxr = dxr + cw_ref[k:k + 1, :] * pltpu.roll(cat, r_rows + HALO - sh, 0)[:r_rows]
        dxr_ref[...] = dxr.astype(BF16)
        rows = [jnp.sum(dxc * taps[k], axis=0, keepdims=True) for k in range(CONV_WIDTH)]
        rows += [jnp.sum(dxc, axis=0, keepdims=True), jnp.sum(dga, axis=0, keepdims=True),
                 jnp.sum(dgx, axis=0, keepdims=True), dls * _sigmoid(-l_ref[...])]
        small_ref[...] += jnp.concatenate(rows, axis=0)

    def rev(i):
        return nc - 1 - i

    def halo(i):
        return jnp.maximum(rev(i) * hpc - 1, 0)

    blk = pl.BlockSpec((r_rows, w), lambda i: (rev(i), 0))
    vm = 40 * _nbytes((r_rows, w), F32) + 6 * _nbytes((w, w), F32)
    return _call(body, (proj, proj, proj, hr, hr, drec, conv_w, conv_b, wa, ba, wx, bx, lru),
                 name="rec_bwd", grid=(nc,),
                 in_specs=[pl.BlockSpec((r_rows, w), lambda i: (rev(i), xr_blk)),
                           pl.BlockSpec((HALO, w), lambda i: (halo(i), xr_blk)),
                           pl.BlockSpec((r_rows, w), lambda i: (rev(i), yr_blk)),
                           blk,
                           pl.BlockSpec((HALO, w), lambda i: (halo(i), 0)),
                           blk,
                           _full((CONV_WIDTH, w)), _full((1, w)), _full((w, w)), _full((1, w)),
                           _full((w, w)), _full((1, w)), _full((1, w))],
                 out_specs=[blk, blk, _full((w, w)), _full((w, w)), _full((SUBLANES, w))],
                 out_shape=[jax.ShapeDtypeStruct((tp, w), BF16)] * 2
                 + [jax.ShapeDtypeStruct((w, w), F32)] * 2 + [jax.ShapeDtypeStruct((SUBLANES, w), F32)],
                 scratch_shapes=[pltpu.VMEM((r_rows, w), F32)] * 3
                 + [pltpu.VMEM((SUBLANES, w), F32), pltpu.VMEM((HALO, w), F32)],
                 semantics=("arbitrary",), vmem_bytes=vm)


ROW_TARGET = 544


def _mixer_out(attn, rec, g_a, g_r, w_out, h, g_next):
    tp, d = h.shape
    aw, rw = attn.shape[1], rec.shape[1]
    kc = d // N_CHIPS
    tm = _divisor_tile(tp, 16, ROW_TARGET)

    def body(a_ref, r_ref, ga_ref, gr_ref, w_ref, h_ref, gn_ref, h1_ref, z_ref, mix_ref):
        a = a_ref[...]
        r = r_ref[...]
        mix = jnp.concatenate([a * _rstd(a) * ga_ref[...], r * _rstd(r) * gr_ref[...]], axis=1).astype(BF16)
        mix_ref[...] = mix
        h1 = h_ref[...]
        for j in range(N_CHIPS):
            h1 = h1 + _dot(mix[:, j * kc:(j + 1) * kc], w_ref[j])
        h1_ref[...] = h1
        z_ref[...] = (h1 * _rstd(h1) * gn_ref[...]).astype(BF16)

    row = lambda wd: pl.BlockSpec((tm, wd), lambda i: (i, 0))
    vm = 2 * _nbytes((d, d), BF16) + 12 * _nbytes((tm, d), F32)
    return _call(body, (attn, rec, g_a, g_r, w_out, h, g_next), name="mixer_out", grid=(tp // tm,),
                 in_specs=[row(aw), row(rw), _full((1, aw)), _full((1, rw)), _full(w_out.shape), row(d),
                           _full((1, d))],
                 out_specs=[row(d), row(d), row(d)],
                 out_shape=[jax.ShapeDtypeStruct((tp, d), F32), jax.ShapeDtypeStruct((tp, d), BF16),
                            jax.ShapeDtypeStruct((tp, d), BF16)],
                 semantics=("parallel",), vmem_bytes=vm)


def _mixer_bwd(dh_b, w_out, attn, rec, g_a, g_r):
    tp, d = dh_b.shape
    aw, rw = attn.shape[1], rec.shape[1]
    tm = _divisor_tile(tp, 16, ROW_TARGET)

    def body(dh_ref, w_ref, a_ref, r_ref, ga_ref, gr_ref, da_ref, dr_ref, dg_ref):
        @pl.when(pl.program_id(0) == 0)
        def _():
            dg_ref[...] = jnp.zeros_like(dg_ref)

        dh = dh_ref[...]
        dmix = jnp.concatenate([_dot_nt(dh, w_ref[j]) for j in range(N_CHIPS)], axis=1)
        da, dga = _rms_bwd(dmix[:, :aw], a_ref[...], ga_ref[...])
        dr, dgr = _rms_bwd(dmix[:, aw:], r_ref[...], gr_ref[...])
        da_ref[...] = da
        dr_ref[...] = dr
        dg_ref[...] += jnp.broadcast_to(jnp.concatenate([dga, dgr], axis=1), (SUBLANES, d))

    row = lambda wd: pl.BlockSpec((tm, wd), lambda i: (i, 0))
    vm = 2 * _nbytes((d, d), BF16) + 12 * _nbytes((tm, d), F32)
    return _call(body, (dh_b, w_out, attn, rec, g_a, g_r), name="mixer_bwd", grid=(tp // tm,),
                 in_specs=[row(d), _full(w_out.shape), row(aw), row(rw), _full((1, aw)), _full((1, rw))],
                 out_specs=[row(aw), row(rw), _full((SUBLANES, d))],
                 out_shape=[jax.ShapeDtypeStruct((tp, aw), F32), jax.ShapeDtypeStruct((tp, rw), F32),
                            jax.ShapeDtypeStruct((SUBLANES, d), F32)],
                 semantics=("arbitrary",), vmem_bytes=vm)


def _mlp_up(z, w_up):
    tp, d = z.shape
    fc = w_up.shape[2]
    ff = N_CHIPS * fc
    tn = _divisor_tile(fc, LANES, 512)
    per = fc // tn

    def body(z_ref, w_ref, act_ref, up_ref):
        up = _dot(z_ref[...], w_ref[...])
        r = jnp.maximum(up, 0.0)
        act_ref[...] = (r * r).astype(BF16)
        up_ref[...] = up.astype(BF16)

    col = pl.BlockSpec((tp, tn), lambda j: (0, j))
    vm = 2 * _nbytes((tp, d), BF16) + 2 * _nbytes((d, tn), BF16) + 8 * _nbytes((tp, tn), F32)
    return _call(body, (z, w_up), name="mlp_up", grid=(ff // tn,),
                 in_specs=[_full((tp, d)), pl.BlockSpec((None, d, tn), lambda j: (j // per, 0, j % per))],
                 out_specs=[col, col],
                 out_shape=[jax.ShapeDtypeStruct((tp, ff), BF16)] * 2,
                 semantics=("parallel",), vmem_bytes=vm)


def _mlp_down(act, w_down, h, g_next):
    tp, d = h.shape
    ff = act.shape[1]
    fc = ff // N_CHIPS
    tm = _divisor_tile(tp, 16, ROW_TARGET)

    def body(a_ref, w_ref, h_ref, gn_ref, h2_ref, z_ref):
        h2 = h_ref[...]
        for j in range(N_CHIPS):
            h2 = h2 + _dot(a_ref[:, j * fc:(j + 1) * fc], w_ref[j])
        h2_ref[...] = h2
        z_ref[...] = (h2 * _rstd(h2) * gn_ref[...]).astype(BF16)

    row = lambda wd: pl.BlockSpec((tm, wd), lambda i: (i, 0))
    vm = 2 * _nbytes((ff, d), BF16) + 2 * _nbytes((tm, ff), BF16) + 10 * _nbytes((tm, d), F32)
    return _call(body, (act, w_down, h, g_next), name="mlp_down", grid=(tp // tm,),
                 in_specs=[row(ff), _full(w_down.shape), row(d), _full((1, d))],
                 out_specs=[row(d), row(d)],
                 out_shape=[jax.ShapeDtypeStruct((tp, d), F32), jax.ShapeDtypeStruct((tp, d), BF16)],
                 semantics=("parallel",), vmem_bytes=vm)


def _loss_bwd(h, g, target, n_real):
    tp, d = h.shape
    tm = _divisor_tile(tp, 16, ROW_TARGET)

    def body(h_ref, g_ref, t_ref, dh_ref, dhb_ref, dg_ref, loss_ref):
        i = pl.program_id(0)

        @pl.when(i == 0)
        def _():
            dg_ref[...] = jnp.zeros_like(dg_ref)
            loss_ref[...] = jnp.zeros_like(loss_ref)

        x = h_ref[...]
        gv = g_ref[...]
        rowi = i * tm + lax.broadcasted_iota(jnp.int32, (tm, 1), 0)
        real = jnp.logical_and(rowi >= N_META, rowi < N_META + n_real)
        err = jnp.where(real, x * _rstd(x) * gv - t_ref[...], 0.0)
        loss_ref[...] += 0.5 * jnp.sum(jnp.mean(err * err, axis=-1, keepdims=True))
        dx, dgp = _rms_bwd(err * (1.0 / d), x, gv)
        dh_ref[...] = dx
        dhb_ref[...] = dx.astype(BF16)
        dg_ref[...] += jnp.broadcast_to(dgp, (SUBLANES, d))

    row = pl.BlockSpec((tm, d), lambda i: (i, 0))
    return _call(body, (h, g, target), name="loss_bwd", grid=(tp // tm,),
                 in_specs=[row, _full((1, d)), row],
                 out_specs=[row, row, _full((SUBLANES, d)), _full((SUBLANES, LANES))],
                 out_shape=[jax.ShapeDtypeStruct((tp, d), F32), jax.ShapeDtypeStruct((tp, d), BF16),
                            jax.ShapeDtypeStruct((SUBLANES, d), F32), jax.ShapeDtypeStruct((SUBLANES, LANES), F32)],
                 semantics=("arbitrary",), vmem_bytes=16 * _nbytes((tm, d), F32))


def _mlp_bwd(dh_b, w_down, up, z2):
    tp, d = dh_b.shape
    fc = w_down.shape[1]
    ff = N_CHIPS * fc
    tn = _divisor_tile(fc, LANES, 512)
    per = fc // tn

    def body(dh_ref, z_ref, w_ref, up_ref, dup_ref, gd_ref, gu_ref):
        dh = dh_ref[...]
        r = jnp.maximum(up_ref[...].astype(F32), 0.0)
        dup = (_dot_nt(dh, w_ref[...]) * (2.0 * r)).astype(BF16)
        dup_ref[...] = dup
        gd_ref[...] = _dot_tn((r * r).astype(BF16), dh).astype(BF16)
        gu_ref[...] = _dot_tn(z_ref[...], dup).astype(BF16)

    col = pl.BlockSpec((tp, tn), lambda j: (0, j))
    vm = 4 * _nbytes((tp, d), BF16) + 4 * _nbytes((tn, d), BF16) + 2 * _nbytes((d, tn), BF16) \
        + 10 * _nbytes((tp, tn), F32) + 4 * _nbytes((tn, d), F32)
    return _call(body, (dh_b, z2, w_down, up), name="mlp_bwd", grid=(ff // tn,),
                 in_specs=[_full((tp, d)), _full((tp, d)),
                           pl.BlockSpec((None, tn, d), lambda j: (j // per, j % per, 0)), col],
                 out_specs=[col, pl.BlockSpec((tn, d), lambda j: (j, 0)),
                            pl.BlockSpec((None, d, tn), lambda j: (j // per, 0, j % per))],
                 out_shape=[jax.ShapeDtypeStruct((tp, ff), BF16), jax.ShapeDtypeStruct((ff, d), BF16),
                            jax.ShapeDtypeStruct((N_CHIPS, d, fc), BF16)],
                 semantics=("parallel",), vmem_bytes=vm)


def _grad_w_pieces(pieces, b):
    tp, n = b.shape
    tn = _divisor_tile(n, LANES, 512)
    widths = [pc.shape[1] for pc in pieces]

    def body(*refs):
        p_refs, b_ref, o_refs = refs[:len(pieces)], refs[len(pieces)], refs[len(pieces) + 1:]
        for p_ref, o_ref in zip(p_refs, o_refs):
            o_ref[...] = _dot_tn(p_ref[...], b_ref[...]).astype(BF16)

    vm = 2 * sum(_nbytes((tp, wd), BF16) for wd in widths) + 2 * _nbytes((tp, tn), BF16) \
        + 4 * sum(_nbytes((wd, tn), F32) for wd in widths) + 2 * _nbytes((tp, max(widths)), F32)
    return _call(body, tuple(pieces) + (b,), name="grad_w_pieces", grid=(n // tn,),
                 in_specs=[_full(pc.shape) for pc in pieces] + [pl.BlockSpec((tp, tn), lambda j: (0, j))],
                 out_specs=[pl.BlockSpec((wd, tn), lambda j: (0, j)) for wd in widths],
                 out_shape=[jax.ShapeDtypeStruct((wd, n), BF16) for wd in widths],
                 semantics=("parallel",), vmem_bytes=vm)


def _dx_norm_bwd(pieces, w, w_spec, w_piece, h, g, dres, dot=_dot_nt):
    tp, d = h.shape
    tm = _divisor_tile(tp, 16, ROW_TARGET)
    n = len(pieces)

    def body(*refs):
        dy_refs = refs[:n]
        w_ref, h_ref, g_ref, dres_ref, dh_ref, dhb_ref, dg_ref = refs[n:]

        @pl.when(pl.program_id(0) == 0)
        def _():
            dg_ref[...] = jnp.zeros_like(dg_ref)

        dz = dot(dy_refs[0][...], w_piece(w_ref, 0))
        for i in range(1, n):
            dz = dz + dot(dy_refs[i][...], w_piece(w_ref, i))
        dx, dgp = _rms_bwd(dz, h_ref[...], g_ref[...])
        dh = dres_ref[...] + dx
        dh_ref[...] = dh
        dhb_ref[...] = dh.astype(BF16)
        dg_ref[...] += jnp.broadcast_to(dgp, (SUBLANES, d))

    row = lambda wd: pl.BlockSpec((tm, wd), lambda i: (i, 0))
    kk = sum(wd for _, _, wd in pieces)
    vm = 2 * _nbytes((d, kk), BF16) + 2 * _nbytes((tm, kk), BF16) + 14 * _nbytes((tm, d), F32)
    piece_specs = [pl.BlockSpec((tm, wd), functools.partial(lambda i, cb: (i, cb), cb=cb)) for _, cb, wd in pieces]
    return _call(body, tuple(a for a, _, _ in pieces) + (w, h, g, dres), name="dx_norm_bwd", grid=(tp // tm,),
                 in_specs=piece_specs + [w_spec, row(d), _full((1, d)), row(d)],
                 out_specs=[row(d), row(d), _full((SUBLANES, d))],
                 out_shape=[jax.ShapeDtypeStruct((tp, d), F32), jax.ShapeDtypeStruct((tp, d), BF16),
                            jax.ShapeDtypeStruct((SUBLANES, d), F32)],
                 semantics=("arbitrary",), vmem_bytes=vm)


def _block_diag(wg):
    nb, b, _ = wg.shape
    eye = jnp.eye(nb, dtype=wg.dtype)
    return (eye[:, None, :, None] * wg[:, :, None, :]).reshape(nb * b, nb * b)


def _diag_blocks(dense, nb):
    b = dense.shape[0] // nb
    d4 = dense.reshape(nb, b, nb, b)
    return jnp.stack([d4[i, :, i, :] for i in range(nb)])


def _row(v):
    return v.reshape(1, -1)


def _forward_layer(l, h, z, p, fetch, g_next):
    d = h.shape[1]
    att_w = d // 2
    rec_w = d - att_w
    nh = att_w // HEAD_DIM
    wa_d = _block_diag(p["w_gate_a"][l]).astype(BF16)
    wx_d = _block_diag(p["w_gate_x"][l]).astype(BF16)
    b_f_pad = jnp.zeros((1, LANES), F32).at[0, :nh].set(p["b_f"][l])
    w_in_t = fetch("w_in", z)
    big = dict(w_in_big=_pack_w_in_t(w_in_t.reshape(-1, d), att_w, nh))
    qkv, proj = _proj(z, big["w_in_big"], att_w)
    c, c_t = _fgate_fwd(proj, b_f_pad, nh)
    attn, lse_b = _attn_fwd(qkv, c, c_t, nh)
    hr, rec = _rec_fwd(proj, 0, 1, rec_w, p["conv_w"][l], _row(p["conv_b"][l]), wa_d,
                       _row(p["b_gate_a"][l]), wx_d, _row(p["b_gate_x"][l]), _row(p["lru_L"][l]))
    big["w_out"] = fetch("w_out", rec)
    h1, z2, mix = _mixer_out(attn, rec, _row(p["attn_out_g"][l]), _row(p["rec_out_g"][l]),
                             big["w_out"], h, _row(p["mlp_norm_g"][l]))
    big["w_up"] = fetch("w_up", h1)
    act, up = _mlp_up(z2, big["w_up"])
    big["w_down"] = fetch("w_down", act)
    h2, z_next = _mlp_down(act, big["w_down"], h1, _row(g_next))
    saved = dict(h0=h, z1=z, proj=proj, qkv=qkv, c=c, c_t=c_t, attn=attn, lse_b=lse_b, hr=hr, rec=rec, h1=h1,
                 z2=z2, mix=mix, up=up, wa_d=wa_d, wx_d=wx_d, b_f_pad=b_f_pad, big=big)
    return h2, z_next, saved


def _backward_mlp(l, dh, dh_b, sv, p, tok):
    w_up, w_down = sv["big"]["w_up"], sv["big"]["w_down"]
    fc = w_up.shape[2]
    dup, g_down, g_up = _mlp_bwd(dh_b, w_down, sv["up"], sv["z2"])
    dh, dh_b, dg2 = _dx_norm_bwd([(dup, j, fc) for j in range(N_CHIPS)], w_up, _full(w_up.shape),
                                 lambda w_ref, j: w_ref[j], sv["h1"], _row(p["mlp_norm_g"][l] + tok), dh)
    big = dict(w_down=g_down.reshape((N_CHIPS, -1) + g_down.shape[1:]), w_up=g_up)
    return dh, dh_b, big, dict(mlp_norm_g=dg2[0])


def _backward_mixer(l, dh, dh_b, sv, p, tok):
    d = dh.shape[1]
    att_w = d // 2
    rec_w = d - att_w
    nh = att_w // HEAD_DIM
    small = {}
    g_out, = _grad_w_pieces([sv["mix"]], dh_b)
    dattn, drec, dg_mix = _mixer_bwd(dh_b, sv["big"]["w_out"], sv["attn"], sv["rec"],
                                     _row(p["attn_out_g"][l] + tok), _row(p["rec_out_g"][l]))
    small["attn_out_g"] = dg_mix[0, :att_w]
    small["rec_out_g"] = dg_mix[0, att_w:]
    dxr, dyr, dwa, dwx, sm = _rec_bwd(
        sv["proj"], 0, 1, rec_w, sv["hr"], drec, p["conv_w"][l], _row(p["conv_b"][l]), sv["wa_d"],
        _row(p["b_gate_a"][l]), sv["wx_d"], _row(p["b_gate_x"][l]), _row(p["lru_L"][l]))
    small.update(conv_w=sm[:CONV_WIDTH], conv_b=sm[4], b_gate_a=sm[5], b_gate_x=sm[6], lru_L=sm[7],
                 w_gate_a=_diag_blocks(dwa, N_REC_BLOCKS), w_gate_x=_diag_blocks(dwx, N_REC_BLOCKS))
    dq, dk, dv, dc = _attn_bwd(sv["qkv"], sv["c"], sv["c_t"], sv["lse_b"], dattn, nh)
    df, db_f = _fgate_bwd(sv["proj"], sv["b_f_pad"], dc)
    small["b_f"] = db_f[0, :nh]
    pieces = [dq, dk, dv, dxr, dyr, df]
    offs = [0, att_w, 2 * att_w, 3 * att_w, 3 * att_w + rec_w, 3 * att_w + 2 * rec_w]
    gq, gk, gv, gxr, gyr, gf = _grad_w_pieces(pieces, sv["z1"])
    g_in_t = jnp.concatenate([gq, gk, gv, gf[:nh], gxr, gyr], axis=0)
    w_big = sv["big"]["w_in_big"]
    widths = [pc.shape[1] for pc in pieces]
    dh, dh_b, dg1 = _dx_norm_bwd(
        [(pc, 0, wd) for pc, wd in zip(pieces, widths)], w_big, _full(w_big.shape),
        lambda w_ref, i: w_ref[offs[i]:offs[i] + widths[i], :], sv["h0"], _row(p["attn_norm_g"][l]), dh, dot=_dot)
    small["attn_norm_g"] = dg1[0]
    big = dict(w_in=g_in_t.reshape(N_CHIPS, -1, d), w_out=g_out.reshape((N_CHIPS, -1) + g_out.shape[1:]))
    return dh, dh_b, big, small


def _pack_w_in_t(w_in_t, att_w, nh):
    qkv = w_in_t[:3 * att_w]
    f = w_in_t[3 * att_w:3 * att_w + nh]
    xy = w_in_t[3 * att_w + nh:]
    return jnp.concatenate([qkv, xy, f, jnp.zeros((LANES - nh, w_in_t.shape[1]), w_in_t.dtype)], axis=0)


ANY = pl.BlockSpec(memory_space=pl.ANY)


def _coords():
    return lax.axis_index("x"), lax.axis_index("y"), lax.axis_index("c")


def _other_chips(x, y):
    return [(1 - x, y), (x, 1 - y), (1 - x, 1 - y)]


def _remote(src, dst, send_sems, recv_sems, k, to):
    return pltpu.make_async_remote_copy(src_ref=src, dst_ref=dst, send_sem=send_sems.at[k],
                                        recv_sem=recv_sems.at[k], device_id=to, device_id_type=MESH)


def _all_gather_chips(shards):
    n = len(shards)
    per = 6

    def body(*refs):
        ins, outs = refs[:n], refs[n:2 * n]
        send_sems, recv_sems, local_sems = refs[2 * n:]
        x, y, c = _coords()
        me = 2 * x + y
        sibling = (x, y, 1 - c)
        chips = _other_chips(x, y)
        local = [pltpu.make_async_copy(ins[t], outs[t].at[me], local_sems.at[t]) for t in range(n)]
        for cp in local:
            cp.start()
        sends = []
        for t in range(n):
            for j, (px, py) in enumerate(chips):
                cp = _remote(ins[t].at[c], outs[t].at[me, c], send_sems, recv_sems, per * t + j, (px, py, c))
                cp.start()
                sends.append(cp)
        for t in range(n):
            for j, (px, py) in enumerate(chips):
                landed = outs[t].at[2 * px + py, c]
                _remote(landed, landed, send_sems, recv_sems, per * t + j, (px, py, c)).wait_recv()
                cp = _remote(landed, landed, send_sems, recv_sems, per * t + 3 + j, sibling)
                cp.start()
                sends.append(cp)
        for t in range(n):
            for j, (px, py) in enumerate(chips):
                passed = outs[t].at[2 * px + py, 1 - c]
                _remote(passed, passed, send_sems, recv_sems, per * t + 3 + j, sibling).wait_recv()
        for cp in sends:
            cp.wait_send()
        for cp in local:
            cp.wait()

    return _call(body, tuple(shards), name="all_gather_chips",
                 in_specs=[ANY] * n, out_specs=[ANY] * n,
                 out_shape=[jax.ShapeDtypeStruct((N_CHIPS,) + s.shape, s.dtype) for s in shards],
                 scratch_shapes=[pltpu.SemaphoreType.DMA((per * n,)), pltpu.SemaphoreType.DMA((per * n,)),
                                 pltpu.SemaphoreType.DMA((n,))])


HBM = pl.BlockSpec(memory_space=pltpu.HBM)
SEM = pl.BlockSpec(memory_space=pltpu.SEMAPHORE)
DATAFLOW = pltpu.SideEffectType.DATAFLOW_SIDE_EFFECTING


def _in_hbm(a):
    return pltpu.with_memory_space_constraint(a, pltpu.HBM)


PUSH_ARRIVALS = {"gather_chips": N_CHIPS - 1, "scatter_chips": N_CHIPS - 1, "sibling": 1, "gather_devices": N_DEV - 1}


def _push_copies(mode, src, land, send_sems, recv_sems, t):
    x, y, c = _coords()
    chip = 2 * x + y
    if mode == "gather_chips":
        return [_remote(src.at[chip], land.at[chip], send_sems, recv_sems, t, (px, py, c))
                for px, py in _other_chips(x, y)]
    if mode == "scatter_chips":
        return [_remote(src.at[2 * px + py], land.at[chip], send_sems, recv_sems, t, (px, py, c))
                for px, py in _other_chips(x, y)]
    if mode == "sibling":
        return [_remote(src, land, send_sems, recv_sems, t, (x, y, 1 - c))]
    dev = 4 * x + 2 * y + c
    return [_remote(src.at[dev], land.at[dev], send_sems, recv_sems, t, (x ^ (k >> 2), y ^ ((k >> 1) & 1), c ^ (k & 1)))
            for k in range(1, N_DEV)]


def _push_start(srcs, lands, mode, name):
    n = len(srcs)
    same = all(s is ld for s, ld in zip(srcs, lands))
    n_in = n if same else 2 * n

    def body(*refs):
        src_refs = refs[:n]
        land_refs = src_refs if same else refs[n:2 * n]
        send_sems, recv_sems = refs[n_in], refs[n_in + 1]
        token = refs[-1]
        for t in range(n):
            for cp in _push_copies(mode, src_refs[t], land_refs[t], send_sems, recv_sems, t):
                cp.start()
        token[...] = jnp.zeros_like(token)

    operands = tuple(srcs) if same else tuple(srcs) + tuple(lands)
    res = _call(
        body, [_in_hbm(a) for a in operands], name=name,
        out_shape=(pltpu.SemaphoreType.DMA((n,)), pltpu.SemaphoreType.DMA((n,)))
        + tuple(pltpu.HBM(a.shape, a.dtype) for a in operands) + (jax.ShapeDtypeStruct((SUBLANES, LANES), F32),),
        in_specs=[HBM] * n_in, out_specs=(SEM, SEM) + (HBM,) * n_in + (pl.BlockSpec(memory_space=pltpu.VMEM),),
        input_output_aliases={i: 2 + i for i in range(n_in)}, side_effects=DATAFLOW, hbm_results=False)
    send_sems, recv_sems, token = res[0], res[1], res[-1]
    srcs_thru = res[2:2 + n]
    lands_thru = srcs_thru if same else res[2 + n:2 + 2 * n]
    return send_sems, recv_sems, srcs_thru, lands_thru, token


def _push_wait(send_sems, recv_sems, ids, srcs, lands, mode, after, name):
    n = len(lands)
    same = all(s is ld for s, ld in zip(srcs, lands))
    n_in = n if same else 2 * n

    def body(*refs):
        land_refs = refs[:n] if same else refs[n:2 * n]
        send_sems, recv_sems = refs[n_in], refs[n_in + 1]
        x, y, c = _coords()
        for t in range(n):
            moved = land_refs[t] if mode == "sibling" else land_refs[t].at[pl.ds(0, PUSH_ARRIVALS[mode])]
            arrivals = _remote(moved, moved, send_sems, recv_sems, ids[t], (x, y, c))
            arrivals.wait_send()
            arrivals.wait_recv()

    operands = tuple(lands) if same else tuple(srcs) + tuple(lands)
    res = _call(
        body, operands + (send_sems, recv_sems, after), name=name,
        out_shape=tuple(pltpu.HBM(a.shape, a.dtype) for a in operands),
        in_specs=[HBM] * n_in + [SEM, SEM, ANY], out_specs=(HBM,) * n_in,
        input_output_aliases={i: i for i in range(n_in)}, side_effects=DATAFLOW)
    return list(res) if same else (list(res[:n]), list(res[n:]))


def _sum_partials(part, landed, chip):
    _, rows, cols = part.shape
    br = _divisor_tile(rows, 16, ELEM_ROWS)

    def body(chip_ref, own_ref, a_ref, b_ref, c_ref, o_ref):
        o_ref[...] = ((own_ref[...].astype(F32) + a_ref[...].astype(F32)) + b_ref[...].astype(F32)) \
            + c_ref[...].astype(F32)

    def other(k):
        return pl.BlockSpec((None, br, cols), lambda i, ch: (jnp.where(ch[0] <= k, k + 1, k), i, 0))

    spec = pltpu.PrefetchScalarGridSpec(
        num_scalar_prefetch=1, grid=(rows // br,),
        in_specs=[pl.BlockSpec((None, br, cols), lambda i, ch: (ch[0], i, 0)), other(0), other(1), other(2)],
        out_specs=pl.BlockSpec((br, cols), lambda i, ch: (i, 0)))
    return _call(body, (chip, part, landed, landed, landed), name="sum_partials", grid_spec=spec,
                 out_shape=jax.ShapeDtypeStruct((rows, cols), F32), semantics=("parallel",))


def _cast_to_slab(w, l, chip):
    _, rows, cols = w.shape
    br = _divisor_tile(rows, 16, ELEM_ROWS)

    def body(chip_ref, w_ref, o_ref):
        o_ref[...] = w_ref[...].astype(BF16)

    spec = pltpu.PrefetchScalarGridSpec(
        num_scalar_prefetch=1, grid=(rows // br,),
        in_specs=[pl.BlockSpec((None, br, cols), lambda i, ch: (l, i, 0))],
        out_specs=pl.BlockSpec((None, br, cols), lambda i, ch: (ch[0], i, 0)))
    return _call(body, (chip, w), name="cast_to_slab", grid_spec=spec,
                 out_shape=jax.ShapeDtypeStruct((N_CHIPS, rows, cols), BF16), semantics=("parallel",))


def _cast_w_in_t_to_slabs(w_t, chip):
    rows, depth, d = w_t.shape
    tn = _divisor_tile(d, LANES, 256)

    def body(chip_ref, w_ref, *o_refs):
        for l in range(depth):
            o_refs[l][...] = w_ref[:, l, :].astype(BF16)

    spec = pltpu.PrefetchScalarGridSpec(
        num_scalar_prefetch=1, grid=(d // tn,),
        in_specs=[pl.BlockSpec((rows, depth, tn), lambda j, ch: (0, 0, j))],
        out_specs=[pl.BlockSpec((None, rows, tn), lambda j, ch: (ch[0], 0, j))] * depth)
    return _call(body, (chip, w_t), name="cast_w_in_t_to_slabs", grid_spec=spec,
                 out_shape=[jax.ShapeDtypeStruct((N_CHIPS, rows, d), BF16)] * depth, semantics=("parallel",),
                 vmem_bytes=4 * _nbytes((rows, max(depth, SUBLANES), tn), F32))


def _place_slab(buf, index, n_slabs):
    rows, cols = buf.shape
    br = _divisor_tile(rows, SUBLANES, ELEM_ROWS)

    def body(index_ref, b_ref, o_ref):
        o_ref[...] = b_ref[...]

    spec = pltpu.PrefetchScalarGridSpec(
        num_scalar_prefetch=1, grid=(rows // br,),
        in_specs=[pl.BlockSpec((br, cols), lambda i, ix: (i, 0))],
        out_specs=pl.BlockSpec((None, br, cols), lambda i, ix: (ix[0], i, 0)))
    return _call(body, (index, buf), name="place_slab", grid_spec=spec,
                 out_shape=jax.ShapeDtypeStruct((n_slabs, rows, cols), buf.dtype), semantics=("parallel",))


ELEM_ROWS = 256


def _sum_slabs(r):
    n, rows, cols = r.shape
    br = _divisor_tile(rows, 16, ELEM_ROWS)

    def body(r_ref, o_ref):
        acc = r_ref[0].astype(F32)
        for j in range(1, n):
            acc = acc + r_ref[j].astype(F32)
        o_ref[...] = acc

    return _call(body, (r,), name="sum_slabs", grid=(rows // br,),
                 in_specs=[pl.BlockSpec((n, br, cols), lambda i: (0, i, 0))],
                 out_specs=pl.BlockSpec((br, cols), lambda i: (i, 0)),
                 out_shape=jax.ShapeDtypeStruct((rows, cols), F32), semantics=("parallel",))


def _adamw_math(w, g, m, v):
    c1 = 1.0 - ADAM_B1 ** ADAM_STEP
    c2 = 1.0 - ADAM_B2 ** ADAM_STEP
    nm = ADAM_B1 * m + (1.0 - ADAM_B1) * g
    nv = ADAM_B2 * v + (1.0 - ADAM_B2) * (g * g)
    delta = -ADAM_LR * ((nm / c1) / (jnp.sqrt(nv / c2) + ADAM_EPS) + ADAM_WD * w)
    return delta, nm, nv


def _adamw(w, g, m, v):
    rows, cols = w.shape
    br = _divisor_tile(rows, 8, ELEM_ROWS)

    def body(w_ref, g_ref, m_ref, v_ref, d_ref, nm_ref, nv_ref):
        d_ref[...], nm_ref[...], nv_ref[...] = _adamw_math(w_ref[...], g_ref[...], m_ref[...], v_ref[...])

    blk = pl.BlockSpec((br, cols), lambda i: (i, 0))
    return _call(body, (w, g, m, v), name="adamw", grid=(rows // br,),
                 in_specs=[blk] * 4, out_specs=[blk] * 3,
                 out_shape=[jax.ShapeDtypeStruct((rows, cols), F32)] * 3, semantics=("parallel",))


def _adamw_w_in_t(w_t, m_t, v_t, g_mine, g_theirs):
    rows, depth, d = w_t.shape
    tn = LANES

    def body(w_ref, m_ref, v_ref, ga_ref, gb_ref, g_ref, d_ref, nm_ref, nv_ref):
        g = ga_ref[...] + gb_ref[...]
        g_ref[...] = g
        d_ref[...], nm_ref[...], nv_ref[...] = _adamw_math(w_ref[...], g, m_ref[...], v_ref[...])

    slab = pl.BlockSpec((rows, depth, tn), lambda j: (0, 0, j))
    return _call(body, (w_t, m_t, v_t, g_mine, g_theirs), name="adamw_w_in_t", grid=(d // tn,),
                 in_specs=[slab] * 5, out_specs=[slab] * 4,
                 out_shape=[jax.ShapeDtypeStruct(w_t.shape, F32)] * 4, semantics=("parallel",),
                 vmem_bytes=2 * 9 * _nbytes((rows, max(depth, SUBLANES), tn), F32))


def _adamw_layer(w, m, v, l, g_mine, g_theirs, prev, after):
    _, rows, cols = w.shape
    br = _divisor_tile(rows, 8, ELEM_ROWS)

    def body(w_ref, m_ref, v_ref, ga_ref, gb_ref, *rest):
        g_ref, d_ref, nm_ref, nv_ref = rest[5:]
        g = ga_ref[...] + gb_ref[...]
        g_ref[...] = g
        d_ref[...], nm_ref[...], nv_ref[...] = _adamw_math(w_ref[...], g, m_ref[...], v_ref[...])

    slot = pl.BlockSpec((None, br, cols), lambda i: (l, i, 0))
    blk = pl.BlockSpec((br, cols), lambda i: (i, 0))
    return _call(body, (w, m, v, g_mine, g_theirs) + tuple(prev) + (after,), name="adamw_layer",
                 grid=(rows // br,), in_specs=[slot] * 3 + [blk] * 2 + [ANY] * 5, out_specs=[slot] * 4,
                 out_shape=[jax.ShapeDtypeStruct(w.shape, F32)] * 4,
                 input_output_aliases={5: 0, 6: 1, 7: 2, 8: 3}, semantics=("parallel",))


BIG = ("w_in", "w_out", "w_up", "w_down")
WEIGHTS = ("meta", "attn_norm_g", "w_in", "b_f", "conv_w", "conv_b", "w_gate_a", "b_gate_a", "w_gate_x",
           "b_gate_x", "lru_L", "attn_out_g", "rec_out_g", "w_out", "mlp_norm_g", "w_up", "w_down", "final_g")
SMALL = tuple(k for k in WEIGHTS if k not in BIG)
COL_SHARDED_SMALL = ("meta", "conv_w")


def _packed_rows(shape):
    return -(-math.prod(shape) // (SUBLANES * LANES)) * SUBLANES


def _pack(arrs):
    rows = []
    for a in arrs:
        flat = a.reshape(-1)
        rows.append(jnp.pad(flat, (0, _packed_rows(a.shape) * LANES - flat.shape[0])).reshape(-1, LANES))
    used = sum(r.shape[0] for r in rows)
    rows.append(jnp.zeros((-used % ELEM_ROWS, LANES), F32))
    return jnp.concatenate(rows, axis=0)


def _unpack(buf, shapes):
    out, r0 = [], 0
    for s in shapes:
        nr = _packed_rows(s)
        out.append(buf[r0:r0 + nr].reshape(-1)[:math.prod(s)].reshape(s))
        r0 += nr
    return out


def _halves(a):
    return a.reshape((2, a.shape[0] // 2) + a.shape[1:])


def _cols_from_chips(g):
    return jnp.moveaxis(g, 0, -2).reshape(g.shape[1:-1] + (N_CHIPS * g.shape[-1],))


def kernel(x, meta, attn_norm_g, w_in, b_f, conv_w, conv_b, w_gate_a, b_gate_a, w_gate_x, b_gate_x, lru_L, attn_out_g, rec_out_g, w_out, mlp_norm_g, w_up, w_down, final_g, loss_target, m_meta, m_attn_norm_g, m_w_in, m_b_f, m_conv_w, m_conv_b, m_w_gate_a, m_b_gate_a, m_w_gate_x, m_b_gate_x, m_lru_L, m_attn_out_g, m_rec_out_g, m_w_out, m_mlp_norm_g, m_w_up, m_w_down, m_final_g, v_meta, v_attn_norm_g, v_w_in, v_b_f, v_conv_w, v_conv_b, v_w_gate_a, v_b_gate_a, v_w_gate_x, v_b_gate_x, v_lru_L, v_attn_out_g, v_rec_out_g, v_w_out, v_mlp_norm_g, v_w_up, v_w_down, v_final_g):
    w = dict(meta=meta, attn_norm_g=attn_norm_g, w_in=w_in, b_f=b_f, conv_w=conv_w, conv_b=conv_b,
             w_gate_a=w_gate_a, b_gate_a=b_gate_a, w_gate_x=w_gate_x, b_gate_x=b_gate_x, lru_L=lru_L,
             attn_out_g=attn_out_g, rec_out_g=rec_out_g, w_out=w_out, mlp_norm_g=mlp_norm_g, w_up=w_up,
             w_down=w_down, final_g=final_g)
    m = dict(meta=m_meta, attn_norm_g=m_attn_norm_g, w_in=m_w_in, b_f=m_b_f, conv_w=m_conv_w, conv_b=m_conv_b,
             w_gate_a=m_w_gate_a, b_gate_a=m_b_gate_a, w_gate_x=m_w_gate_x, b_gate_x=m_b_gate_x, lru_L=m_lru_L,
             attn_out_g=m_attn_out_g, rec_out_g=m_rec_out_g, w_out=m_w_out, mlp_norm_g=m_mlp_norm_g,
             w_up=m_w_up, w_down=m_w_down, final_g=m_final_g)
    v = dict(meta=v_meta, attn_norm_g=v_attn_norm_g, w_in=v_w_in, b_f=v_b_f, conv_w=v_conv_w, conv_b=v_conv_b,
             w_gate_a=v_w_gate_a, b_gate_a=v_b_gate_a, w_gate_x=v_w_gate_x, b_gate_x=v_b_gate_x, lru_L=v_lru_L,
             attn_out_g=v_attn_out_g, rec_out_g=v_rec_out_g, w_out=v_w_out, mlp_norm_g=v_mlp_norm_g,
             w_up=v_w_up, w_down=v_w_down, final_g=v_final_g)
    s_len, d = x.shape[1], x.shape[2]
    depth = w_in.shape[0]
    att_w = d // 2
    rec_w = d - att_w
    nh = att_w // HEAD_DIM
    chip = 2 * lax.axis_index("x") + lax.axis_index("y")

    g_conv, g_meta = [g.reshape((N_CHIPS, g.shape[1] * g.shape[2]) + g.shape[3:])
                      for g in _all_gather_chips([_halves(w["conv_w"]), _halves(w["meta"])])]
    p = dict(w)
    p["conv_w"] = _cols_from_chips(g_conv)
    meta_full = jnp.moveaxis(g_meta, 0, 1).reshape(N_META, d)

    chip1 = chip.reshape(1).astype(jnp.int32)
    w_in_t, m_in_t, v_in_t = [jnp.transpose(a["w_in"], (2, 0, 1)) for a in (w, m, v)]
    w_in_slabs = _cast_w_in_t_to_slabs(w_in_t, chip1)
    pushes, tokens = [], []
    for l in range(depth):
        slabs = [w_in_slabs[l]] + [_cast_to_slab(w[k], l, chip1) for k in BIG[1:]]
        send_sems, recv_sems, _, lands, token = _push_start(slabs, slabs, "gather_chips", f"weights_start_{l}")
        pushes.append((send_sems, recv_sems, lands))
        tokens.append(token[0, 0])

    t_len = N_META + s_len
    pad = -t_len % SEQ_TILE
    h = jnp.concatenate([meta_full, x[0], jnp.zeros((pad, d), F32)], axis=0)
    tgt = jnp.concatenate([jnp.zeros((N_META, d), F32), loss_target[0], jnp.zeros((pad, d), F32)], axis=0)
    z = _rms_fwd(h, _row(p["attn_norm_g"][0] + sum(tokens)))
    saved = []
    for l in range(depth):
        send_sems, recv_sems, lands = pushes[l]

        def fetch(k, after, l=l, send_sems=send_sems, recv_sems=recv_sems, lands=lands):
            i = BIG.index(k)
            return _push_wait(send_sems, recv_sems, [i], [lands[i]], [lands[i]], "gather_chips", after,
                              f"{k}_wait_{l}")[0]

        g_next = p["attn_norm_g"][l + 1] if l + 1 < depth else p["final_g"]
        h, z, sv = _forward_layer(l, h, z, p, fetch, g_next)
        saved.append(sv)
    dh, dh_b, dg_final, loss_part = _loss_bwd(h, _row(p["final_g"]), tgt, s_len)

    small = {k: [None] * depth for k in SMALL if k not in ("meta", "final_g")}
    pushes = [None] * depth
    tok = 0.0
    for l in reversed(range(depth)):
        dh, dh_b, big_mlp, sm_mlp = _backward_mlp(l, dh, dh_b, saved[l], p, tok)
        parts = [big_mlp["w_down"], big_mlp["w_up"]]
        push_mlp = _push_start(parts, [lax.empty(a.shape, a.dtype) for a in parts], "scatter_chips",
                               f"mlp_grads_start_{l}")
        dh, dh_b, big_mix, sm_mix = _backward_mixer(l, dh, dh_b, saved[l], p, push_mlp[4][0, 0])
        parts = [big_mix["w_out"], big_mix["w_in"]]
        push_mix = _push_start(parts, [lax.empty(a.shape, a.dtype) for a in parts], "scatter_chips",
                               f"mixer_grads_start_{l}")
        tok = push_mix[4][0, 0]
        pushes[l] = {("w_down", "w_up"): push_mlp, ("w_out", "w_in"): push_mix}
        for k, val in {**sm_mlp, **sm_mix}.items():
            small[k][l] = val
    grads = {k: jnp.stack(val) for k, val in small.items()}
    grads["final_g"] = dg_final[0]
    grads["meta"] = dh[:N_META]
    dx = dh[N_META:t_len]

    full_shapes = [grads[k].shape for k in SMALL] + [(1,)]
    packed = _pack([grads[k].astype(F32) for k in SMALL] + [loss_part[0, :1] + tok])
    dev1 = (2 * chip + lax.axis_index("c")).reshape(1).astype(jnp.int32)
    slabs = [_place_slab(packed, dev1, N_DEV)]
    small_push = _push_start(slabs, slabs, "gather_devices", "small_grads_start")

    last_token = small_push[4]
    outs = {k: [lax.empty(w[k].shape, F32) for _ in range(4)] for k in BIG[1:]}
    w_in_sums = [None] * depth
    swaps = {}

    def finish(l, wait_after, adam_after):
        send_sems, recv_sems, mine, lands, _ = swaps[l]
        mine, theirs = _push_wait(send_sems, recv_sems, list(range(len(BIG))), mine, lands, "sibling", wait_after,
                                  f"sums_wait_{l}")
        w_in_sums[l] = (mine[0], theirs[0])
        for k, a, b in zip(BIG[1:], mine[1:], theirs[1:]):
            outs[k] = _adamw_layer(w[k], m[k], v[k], l, a, b, outs[k], adam_after)

    for l in reversed(range(depth)):
        sums = {}
        for names, (send_sems, recv_sems, parts, lands, _) in pushes[l].items():
            parts, landed = _push_wait(send_sems, recv_sems, [0, 1], parts, lands, "scatter_chips", last_token,
                                       f"{names[0]}_grads_wait_{l}")
            for k, part, land in zip(names, parts, landed):
                sums[k] = _sum_partials(part, land, chip1)
        mine = [sums[k] for k in BIG]
        swaps[l] = _push_start(mine, [lax.empty(a.shape, a.dtype) for a in mine], "sibling", f"sums_start_{l}")
        if l + 1 < depth:
            finish(l + 1, outs["w_down"][0] if l + 2 < depth else mine[0], swaps[l][4])
    finish(0, outs["w_down"][0] if depth > 1 else swaps[0][4], swaps[0][4])
    outs["w_in"] = [jnp.transpose(r, (1, 2, 0)) for r in _adamw_w_in_t(
        w_in_t, m_in_t, v_in_t, jnp.stack([s[0] for s in w_in_sums], axis=1),
        jnp.stack([s[1] for s in w_in_sums], axis=1))]
    out_g, out_d, out_m, out_v = [{k: outs[k][i] for k in BIG} for i in range(4)]

    landed = _push_wait(small_push[0], small_push[1], [0], small_push[3], small_push[3], "gather_devices",
                        out_g["w_in"], "small_grads_wait")
    total = _sum_slabs(landed[0])
    small_g = dict(zip(SMALL + ("loss",), _unpack(total, full_shapes)))
    for k in COL_SHARDED_SMALL:
        n = w[k].shape[-1]
        small_g[k] = lax.dynamic_slice_in_dim(small_g[k], chip * n, n, axis=small_g[k].ndim - 1)
    local_shapes = [w[k].shape for k in SMALL]
    res = _adamw(_pack([w[k] for k in SMALL]), _pack([small_g[k] for k in SMALL]),
                 _pack([m[k] for k in SMALL]), _pack([v[k] for k in SMALL]))
    out_g.update({k: small_g[k] for k in SMALL})
    for dst, buf in zip((out_d, out_m, out_v), res):
        dst.update(zip(SMALL, _unpack(buf, local_shapes)))

    return (small_g["loss"].reshape(()), dx[None],
            *[out_g[k] for k in WEIGHTS], *[out_d[k] for k in WEIGHTS],
            *[out_m[k] for k in WEIGHTS], *[out_v[k] for k in WEIGHTS])
```

```python
import functools
import math

import jax
import jax.numpy as jnp
from jax import lax
from jax.experimental import pallas as pl
from jax.experimental.pallas import tpu as pltpu

F32 = jnp.float32
BF16 = jnp.bfloat16

N_META = 16
HEAD_DIM = 64
N_REC_BLOCKS = 8
CONV_WIDTH = 4
RG_C = 8.0
NORM_EPS = 1e-6
ADAM_LR = 0.001
ADAM_B1 = 0.9
ADAM_B2 = 0.999
ADAM_EPS = 1e-08
ADAM_WD = 0.01
ADAM_STEP = 10

LANES = 128
SUBLANES = 8
SEQ_TILE = 128
VMEM_CAP = 60 * 2**20
VMEM_SLACK = 6 * 2**20
NEG_BIG = -1e30
N_CHIPS = 4
N_DEV = 8
MESH = pl.DeviceIdType.MESH


def _nbytes(shape, dtype):
    return math.prod(shape) * jnp.dtype(dtype).itemsize


def _call(body, args, *, name, out_shape, grid=(), in_specs=None, out_specs=None, scratch_shapes=(),
          grid_spec=None, semantics=None, vmem_bytes=None, side_effects=None, hbm_results=True, **kw):
    cp = {}
    if semantics is not None:
        cp["dimension_semantics"] = semantics
    if vmem_bytes is not None:
        cp["vmem_limit_bytes"] = int(min(VMEM_CAP, vmem_bytes + VMEM_SLACK))
    if side_effects is not None:
        cp["has_side_effects"] = side_effects
    if grid_spec is not None:
        kw["grid_spec"] = grid_spec
    else:
        kw.update(grid=grid, in_specs=in_specs, out_specs=out_specs, scratch_shapes=scratch_shapes)
    if hbm_results:
        out_shape = jax.tree.map(
            lambda s: pltpu.HBM(s.shape, s.dtype) if isinstance(s, jax.ShapeDtypeStruct) else s, out_shape)
    fn = pl.pallas_call(
        body, name=name, out_shape=out_shape,
        compiler_params=pltpu.CompilerParams(**cp), **kw)
    return fn(*[_in_hbm(a) if jnp.issubdtype(getattr(a, "dtype", jnp.int32), jnp.floating) else a for a in args])


def _divisor_tile(n, unit, target):
    best = None
    for t in range(unit, min(n, target) + 1, unit):
        if n % t == 0:
            best = t
    return n if best is None else best


def _sigmoid(x):
    return 1.0 / (1.0 + jnp.exp(-x))


def _log1p_unit(e):
    series = e * (1.0 - e * (0.5 - e * (1.0 / 3.0)))
    return jnp.where(e < 1e-2, series, jnp.log(1.0 + e))


def _log_sigmoid(x):
    return jnp.minimum(x, 0.0) - _log1p_unit(jnp.exp(-jnp.abs(x)))


def _one_minus_exp(x, exp_x):
    small = -x * (1.0 + x * (1.0 / 2 + x * (1.0 / 6 + x * (1.0 / 24 + x * (1.0 / 120 + x * (1.0 / 720))))))
    return jnp.where(x > -0.25, small, 1.0 - exp_x)


_GELU_K = math.sqrt(2.0 / math.pi)
_GELU_C = 0.044715


def _gelu_and_grad(y):
    th = jnp.tanh(_GELU_K * (y + _GELU_C * y * y * y))
    g = 0.5 * y * (1.0 + th)
    dg = 0.5 * (1.0 + th) + 0.5 * y * (1.0 - th * th) * _GELU_K * (1.0 + 3.0 * _GELU_C * y * y)
    return g, dg


def _rstd(x):
    return lax.rsqrt(jnp.mean(x * x, axis=-1, keepdims=True) + NORM_EPS)


def _rms_bwd(dz, x, g):
    rs = _rstd(x)
    xh = x * rs
    dgp = jnp.sum(dz * xh, axis=0, keepdims=True)
    dxh = dz * g
    dx = rs * (dxh - xh * jnp.mean(dxh * xh, axis=-1, keepdims=True))
    return dx, dgp


def _dot(a, b):
    return jnp.dot(a, b, preferred_element_type=F32)


def _dot_nt(a, b):
    return lax.dot_general(a, b, (((1,), (1,)), ((), ())), preferred_element_type=F32)


def _dot_tn(a, b):
    return lax.dot_general(a, b, (((0,), (0,)), ((), ())), preferred_element_type=F32)


def _full(shape):
    nd = len(shape)
    return pl.BlockSpec(shape, lambda *_: (0,) * nd)


def _rms_fwd(h, g):
    tp, d = h.shape
    tm = _divisor_tile(tp, 16, 544)

    def body(h_ref, g_ref, z_ref):
        x = h_ref[...]
        z_ref[...] = (x * _rstd(x) * g_ref[...]).astype(BF16)

    return _call(body, (h, g), name="rms_fwd", grid=(tp // tm,),
                 in_specs=[pl.BlockSpec((tm, d), lambda i: (i, 0)), _full((1, d))],
                 out_specs=pl.BlockSpec((tm, d), lambda i: (i, 0)),
                 out_shape=jax.ShapeDtypeStruct((tp, d), BF16), semantics=("parallel",))


def _proj(z, w_big_t, att_w):
    tp, d = z.shape
    nb = w_big_t.shape[0]
    tn = _divisor_tile(nb, LANES, 512)
    assert (3 * att_w) % tn == 0
    n_qkv = 3 * att_w // tn
    scale = 1.0 / math.sqrt(HEAD_DIM)

    def body(z_ref, w_ref, qkv_ref, p_ref):
        j = pl.program_id(0)
        acc = _dot_nt(z_ref[...], w_ref[...])

        @pl.when(j < n_qkv)
        def _():
            col = j * tn + lax.broadcasted_iota(jnp.int32, (1, tn), 1)
            qkv_ref[...] = (acc * jnp.where(col < att_w, scale, 1.0)).astype(BF16)

        @pl.when(j >= n_qkv)
        def _():
            p_ref[...] = acc

    vm = 2 * (_nbytes((tp, d), BF16) + _nbytes((d, tn), BF16) + _nbytes((tp, tn), F32) * 2)
    return _call(body, (z, w_big_t), name="proj", grid=(nb // tn,),
                 in_specs=[_full((tp, d)), pl.BlockSpec((tn, d), lambda j: (j, 0))],
                 out_specs=[pl.BlockSpec((tp, tn), lambda j: (0, jnp.minimum(j, n_qkv - 1))),
                            pl.BlockSpec((tp, tn), lambda j: (0, jnp.maximum(j - n_qkv, 0)))],
                 out_shape=[jax.ShapeDtypeStruct((tp, 3 * att_w), BF16),
                            jax.ShapeDtypeStruct((tp, nb - 3 * att_w), F32)],
                 semantics=("arbitrary",), vmem_bytes=vm)


def _tile_cumsum(x, row, reverse=False):
    for s in (1, 2, 4):
        if reverse:
            x = x + jnp.where(row < SUBLANES - s, pltpu.roll(x, SUBLANES - s, 0), 0.0)
        else:
            x = x + jnp.where(row >= s, pltpu.roll(x, s, 0), 0.0)
    return x


def _fgate_fwd(proj, b_f_pad, nh):
    tp, nb = proj.shape
    fblk = nb // LANES - 1

    def body(f_ref, b_ref, c_ref, ct_ref):
        b = b_ref[...]
        row = lax.broadcasted_iota(jnp.int32, (SUBLANES, LANES), 0)

        def step(i, carry):
            r0 = pl.multiple_of(i * SUBLANES, SUBLANES)
            lf = _log_sigmoid(f_ref[pl.ds(r0, SUBLANES), :] + b)
            x = _tile_cumsum(lf, row) + carry
            c_ref[pl.ds(r0, SUBLANES), :] = x
            return x[SUBLANES - 1:SUBLANES, :]

        lax.fori_loop(0, tp // SUBLANES, step, jnp.zeros((1, LANES), F32))
        ct_ref[...] = c_ref[...].T[:nh, :]

    return _call(body, (proj, b_f_pad), name="fgate_fwd", grid=(1,),
                 in_specs=[pl.BlockSpec((tp, LANES), lambda i: (0, fblk)), _full((1, LANES))],
                 out_specs=[_full((tp, LANES)), _full((nh, tp))],
                 out_shape=[jax.ShapeDtypeStruct((tp, LANES), F32), jax.ShapeDtypeStruct((nh, tp), F32)],
                 semantics=("arbitrary",))


def _fgate_bwd(proj, b_f_pad, dc):
    tp, nb = proj.shape
    fblk = nb // LANES - 1

    def body(f_ref, b_ref, dc_ref, df_ref, db_ref, dc_s):
        b = b_ref[...]
        row = lax.broadcasted_iota(jnp.int32, (SUBLANES, LANES), 0)
        nt = tp // SUBLANES

        def step(i, carry):
            suffix, acc = carry
            r0 = pl.multiple_of((nt - 1 - i) * SUBLANES, SUBLANES)
            dlf = _tile_cumsum(dc_ref[pl.ds(r0, SUBLANES), :], row, reverse=True) + suffix
            df = dlf * _sigmoid(-(f_ref[pl.ds(r0, SUBLANES), :] + b))
            dc_s[pl.ds(r0, SUBLANES), :] = df
            return dlf[0:1, :], acc + df

        _, acc = lax.fori_loop(0, nt, step, (jnp.zeros((1, LANES), F32), jnp.zeros((SUBLANES, LANES), F32)))
        df_ref[...] = dc_s[...].astype(BF16)
        db_ref[...] = jnp.broadcast_to(jnp.sum(acc, axis=0, keepdims=True), (SUBLANES, LANES))

    return _call(body, (proj, b_f_pad, dc), name="fgate_bwd", grid=(1,),
                 in_specs=[pl.BlockSpec((tp, LANES), lambda i: (0, fblk)), _full((1, LANES)), _full((tp, LANES))],
                 out_specs=[_full((tp, LANES)), _full((SUBLANES, LANES))],
                 out_shape=[jax.ShapeDtypeStruct((tp, LANES), BF16),
                            jax.ShapeDtypeStruct((SUBLANES, LANES), F32)],
                 scratch_shapes=[pltpu.VMEM((tp, LANES), F32)], semantics=("arbitrary",))


ATT_BQ = 128


ATT_BUCKET = 3
ATT_HEADS = 4


def _for_bucket(i, nq, fn):
    for lo in range(0, nq, ATT_BUCKET):
        hi = min(lo + ATT_BUCKET, nq)
        spans = ([(0, lo * ATT_BQ, False)] if lo else []) + [(lo * ATT_BQ, hi * ATT_BQ, True)]
        pl.when(jnp.logical_and(i >= lo, i < hi))(functools.partial(fn, spans))


def _head_column(c_blk, h):
    lane = lax.broadcasted_iota(jnp.int32, c_blk.shape, 1)
    return jnp.sum(jnp.where(lane == h, c_blk, 0.0), axis=1, keepdims=True)


def _key_major_logits(kh_ref, ck_ref, hh, q, span, q0):
    k0, k1, needs_mask = span
    t = _dot_nt(kh_ref[hh, k0:k1, :], q) - ck_ref[hh, k0:k1, :]
    if needs_mask:
        keys = k0 + lax.broadcasted_iota(jnp.int32, (k1 - k0, ATT_BQ), 0)
        t = jnp.where(keys <= q0 + lax.broadcasted_iota(jnp.int32, (k1 - k0, ATT_BQ), 1), t, NEG_BIG)
    return t


def _stage_keys(p, k_ref, c_ref, kh_s, ck_s):
    for hh in range(ATT_HEADS):
        kh_s[hh] = k_ref[:, HEAD_DIM * hh:HEAD_DIM * (hh + 1)]
        ck_s[hh] = jnp.broadcast_to(_head_column(c_ref[...], ATT_HEADS * p + hh), ck_s.shape[1:])


def _attn_fwd(qkv, c, c_t, nh):
    tp = qkv.shape[0]
    att_w = nh * HEAD_DIM
    ng = nh // ATT_HEADS
    gw = ATT_HEADS * HEAD_DIM
    bq = ATT_BQ
    nq = tp // bq

    def body(q_ref, k_ref, v_ref, c_ref, ct_ref, o_ref, lse_ref, ck_s, kh_s, vt_s):
        p = pl.program_id(0)
        i = pl.program_id(1)

        @pl.when(i == 0)
        def _():
            _stage_keys(p, k_ref, c_ref, kh_s, ck_s)
            vt_s[...] = v_ref[...].astype(F32).T.astype(BF16)

        def compute(spans):
            q0 = pl.multiple_of(i * bq, bq)
            o_t, lses = [], []
            for hh in range(ATT_HEADS):
                lo = HEAD_DIM * hh
                q = q_ref[:, lo:lo + HEAD_DIM]
                ts = [_key_major_logits(kh_s, ck_s, hh, q, sp, q0) for sp in spans]
                m = functools.reduce(jnp.maximum, [jnp.max(t, axis=0, keepdims=True) for t in ts])
                es = [jnp.exp(t - m) for t in ts]
                l = sum(jnp.sum(e, axis=0, keepdims=True) for e in es)
                o = sum(_dot(vt_s[lo:lo + HEAD_DIM, k0:k1], e.astype(BF16)) for e, (k0, k1, _) in zip(es, spans))
                o_t.append(o / l)
                lses.append(m + ct_ref[pl.ds(ATT_HEADS * p + hh, 1), :] + jnp.log(l))
            o_ref[...] = jnp.concatenate(o_t, axis=0).T
            lse_ref[...] = jnp.concatenate(lses, axis=0)

        _for_bucket(i, nq, compute)

    blk = pl.BlockSpec((bq, gw), lambda p, i: (i, p))
    vm = 6 * _nbytes((tp, gw), BF16) + 2 * ATT_HEADS * _nbytes((tp, LANES), F32) + 2 * _nbytes((tp, LANES), F32) \
        + 8 * ATT_HEADS * _nbytes((bq, tp), F32)
    return _call(body, (qkv, qkv, qkv, c, c_t), name="attn_fwd", grid=(ng, nq),
                 in_specs=[blk,
                           pl.BlockSpec((tp, gw), lambda p, i: (0, ng + p)),
                           pl.BlockSpec((tp, gw), lambda p, i: (0, 2 * ng + p)),
                           _full((tp, LANES)), pl.BlockSpec((nh, bq), lambda p, i: (0, i))],
                 out_specs=[blk, pl.BlockSpec((None, ATT_HEADS, bq), lambda p, i: (p, 0, i))],
                 out_shape=[jax.ShapeDtypeStruct((tp, att_w), F32), jax.ShapeDtypeStruct((ng, ATT_HEADS, tp), F32)],
                 scratch_shapes=[pltpu.VMEM((ATT_HEADS, tp, LANES), F32), pltpu.VMEM((ATT_HEADS, tp, HEAD_DIM), BF16),
                                 pltpu.VMEM((gw, tp), BF16)],
                 semantics=("arbitrary", "arbitrary"), vmem_bytes=vm)


def _attn_bwd(qkv, c, c_t, lse, do, nh):
    tp = qkv.shape[0]
    att_w = nh * HEAD_DIM
    ng = nh // ATT_HEADS
    gw = ATT_HEADS * HEAD_DIM
    bq = ATT_BQ
    nq = tp // bq
    pair = 2 * HEAD_DIM
    assert pair == LANES and ATT_HEADS % 2 == 0
    scale = 1.0 / math.sqrt(HEAD_DIM)

    def body(q_ref, k_ref, v_ref, c_ref, ct_ref, lse_ref, do_ref, dq_ref, dk_ref, dv_ref, dc_ref,
             dk_s, dv_s, dc_s, ck_s, kt_s):
        p = pl.program_id(0)
        i = pl.program_id(1)

        @pl.when(i == 0)
        def _():
            dk_s[...] = jnp.zeros_like(dk_s)
            dv_s[...] = jnp.zeros_like(dv_s)
            dc_s[...] = jnp.zeros_like(dc_s)
            kt_s[...] = k_ref[...].astype(F32).T.astype(BF16)
            for hh in range(ATT_HEADS):
                ck_s[hh] = jnp.broadcast_to(_head_column(c_ref[...], ATT_HEADS * p + hh), (tp, LANES))

        @pl.when(jnp.logical_and(i == 0, p == 0))
        def _():
            dc_ref[...] = jnp.zeros_like(dc_ref)

        def compute(spans):
            q0 = pl.multiple_of(i * bq, bq)
            top = lax.broadcasted_iota(jnp.int32, (pair, bq), 0) < HEAD_DIM
            low = lax.broadcasted_iota(jnp.int32, (bq, pair), 1) < HEAD_DIM

            def diag_cols(x2):
                xt = x2.astype(F32).T.astype(BF16)
                return jnp.concatenate([jnp.where(top, xt, 0), jnp.where(top, 0, xt)], axis=1)

            def diag_rows(x2):
                return jnp.concatenate([jnp.where(low, x2, 0), jnp.where(low, 0, x2)], axis=0)

            dq_t = []
            for pi in range(ATT_HEADS // 2):
                lo = pair * pi
                q2 = q_ref[:, lo:lo + pair]
                do2 = do_ref[:, lo:lo + pair].astype(BF16)
                q_cols, do_cols = diag_cols(q2), diag_cols(do2)
                q_rows, do_rows = diag_rows(q2), diag_rows(do2)
                heads = (2 * pi, 2 * pi + 1)
                col_terms = [ct_ref[pl.ds(ATT_HEADS * p + hh, 1), :] - lse_ref[hh:hh + 1, :] for hh in heads]
                prs, dps = [], []
                for k0, k1, needs_mask in spans:
                    t2 = _dot(k_ref[k0:k1, lo:lo + pair], q_cols)
                    dp2 = _dot(v_ref[k0:k1, lo:lo + pair], do_cols)
                    if needs_mask:
                        keys = k0 + lax.broadcasted_iota(jnp.int32, (k1 - k0, bq), 0)
                        seen = keys <= q0 + lax.broadcasted_iota(jnp.int32, (k1 - k0, bq), 1)
                    pr_e, dp_e = [], []
                    for e, hh in enumerate(heads):
                        t = t2[:, e * bq:(e + 1) * bq] - ck_s[hh, k0:k1, :]
                        if needs_mask:
                            t = jnp.where(seen, t, NEG_BIG)
                        pr_e.append(jnp.exp(t + col_terms[e]))
                        dp_e.append(dp2[:, e * bq:(e + 1) * bq])
                    prs.append(pr_e)
                    dps.append(dp_e)
                key_sums = [sum(jnp.sum(pr[e] * dp[e], axis=0, keepdims=True) for pr, dp in zip(prs, dps))
                            for e in range(2)]
                dq2 = 0.0
                for (k0, k1, _), pr, dp in zip(spans, prs, dps):
                    ds = [pr[e] * (dp[e] - key_sums[e]) for e in range(2)]
                    for e, hh in enumerate(heads):
                        dc_s[hh, k0:k1, :] += jnp.sum(ds[e], axis=1, keepdims=True)
                    ds2 = jnp.concatenate([ds[0].astype(BF16), ds[1].astype(BF16)], axis=1)
                    pr2 = jnp.concatenate([pr[0].astype(BF16), pr[1].astype(BF16)], axis=1)
                    dk_s[k0:k1, lo:lo + pair] += _dot(ds2, q_rows)
                    dv_s[k0:k1, lo:lo + pair] += _dot(pr2, do_rows)
                    dq2 = dq2 + _dot(kt_s[lo:lo + pair, k0:k1], ds2)
                dq_t.append(jnp.concatenate([dq2[:HEAD_DIM, :bq], dq2[HEAD_DIM:, bq:]], axis=0))
            dq_ref[...] = (jnp.concatenate(dq_t, axis=0) * scale).T.astype(BF16)

        _for_bucket(i, nq, compute)

        @pl.when(i == nq - 1)
        def _():
            dk_ref[...] = dk_s[...].astype(BF16)
            dv_ref[...] = dv_s[...].astype(BF16)
            lane = lax.broadcasted_iota(jnp.int32, (tp, LANES), 1)
            dc = dc_ref[...]
            for hh in range(ATT_HEADS):
                dc = jnp.where(lane == ATT_HEADS * p + hh, -dc_s[hh], dc)
            dc_ref[...] = dc

    blk = pl.BlockSpec((bq, gw), lambda p, i: (i, p))
    col = pl.BlockSpec((tp, gw), lambda p, i: (0, p))
    vm = 7 * _nbytes((tp, gw), BF16) + 2 * _nbytes((tp, gw), F32) + 2 * ATT_HEADS * _nbytes((tp, LANES), F32) \
        + 2 * _nbytes((tp, LANES), F32) + 12 * ATT_HEADS * _nbytes((bq, tp), F32)
    return _call(body, (qkv, qkv, qkv, c, c_t, lse, do), name="attn_bwd", grid=(ng, nq),
                 in_specs=[blk,
                           pl.BlockSpec((tp, gw), lambda p, i: (0, ng + p)),
                           pl.BlockSpec((tp, gw), lambda p, i: (0, 2 * ng + p)),
                           _full((tp, LANES)), pl.BlockSpec((nh, bq), lambda p, i: (0, i)),
                           pl.BlockSpec((None, ATT_HEADS, bq), lambda p, i: (p, 0, i)), blk],
                 out_specs=[blk, col, col, _full((tp, LANES))],
                 out_shape=[jax.ShapeDtypeStruct((tp, att_w), BF16)] * 3 + [jax.ShapeDtypeStruct((tp, LANES), F32)],
                 scratch_shapes=[pltpu.VMEM((tp, gw), F32), pltpu.VMEM((tp, gw), F32),
                                 pltpu.VMEM((ATT_HEADS, tp, 1), F32), pltpu.VMEM((ATT_HEADS, tp, LANES), F32),
                                 pltpu.VMEM((gw, tp), BF16)],
                 semantics=("arbitrary", "arbitrary"), vmem_bytes=vm)


REC_ROWS = 128
HALO = SUBLANES


def _conv_taps(cat):
    taps = []
    for k in range(CONV_WIDTH):
        sh = CONV_WIDTH - 1 - k
        taps.append((pltpu.roll(cat, sh, 0) if sh else cat)[HALO:])
    return taps


def _rec_gates(xc, wa_ref, ba_ref, wx_ref, bx_ref, l_ref):
    xcb = xc.astype(BF16)
    r = _sigmoid(_dot(xcb, wa_ref[...]) + ba_ref[...])
    ig = _sigmoid(_dot(xcb, wx_ref[...]) + bx_ref[...])
    ls = _log_sigmoid(l_ref[...])
    log_a = RG_C * r * ls
    return xcb, r, ig, ls, log_a


def _rec_fwd(proj, xr_blk, yr_blk, rec_w, conv_w, conv_b, wa, ba, wx, bx, lru):
    tp = proj.shape[0]
    w = rec_w
    r_rows = REC_ROWS
    nc = tp // r_rows
    cpb = w // LANES

    def body(xr_ref, yr_ref, cw_ref, cb_ref, wa_ref, ba_ref, wx_ref, bx_ref, l_ref,
             hr_ref, rec_ref, prev_s, carry_s, a_s, u_s):
        i = pl.program_id(0)

        @pl.when(i == 0)
        def _():
            prev_s[...] = jnp.zeros_like(prev_s)
            carry_s[...] = jnp.zeros_like(carry_s)

        x = xr_ref[...]
        taps = _conv_taps(jnp.concatenate([prev_s[...], x], axis=0))
        prev_s[...] = x[r_rows - HALO:]
        xc = cb_ref[...]
        for k in range(CONV_WIDTH):
            xc = xc + cw_ref[k:k + 1, :] * taps[k]
        _, r, ig, ls, log_a = _rec_gates(xc, wa_ref, ba_ref, wx_ref, bx_ref, l_ref)
        a = jnp.exp(log_a)
        a_s[...] = a
        u_s[...] = jnp.sqrt(_one_minus_exp(2.0 * log_a, a * a)) * ig * xc

        def tile(j, h):
            r0 = pl.multiple_of(j * SUBLANES, SUBLANES)
            at = a_s[pl.ds(r0, SUBLANES), :]
            ut = u_s[pl.ds(r0, SUBLANES), :]
            out = []
            for rr in range(SUBLANES):
                h = at[rr:rr + 1] * h + ut[rr:rr + 1]
                out.append(h)
            hr_ref[pl.ds(r0, SUBLANES), :] = jnp.concatenate(out, axis=0)
            return h

        carry_s[0:1, :] = lax.fori_loop(0, r_rows // SUBLANES, tile, carry_s[0:1, :])
        g, _ = _gelu_and_grad(yr_ref[...])
        rec_ref[...] = hr_ref[...] * g

    blk = pl.BlockSpec((r_rows, w), lambda i: (i, 0))
    vm = 16 * _nbytes((r_rows, w), F32) + 4 * _nbytes((w, w), BF16)
    return _call(body, (proj, proj, conv_w, conv_b, wa, ba, wx, bx, lru), name="rec_fwd", grid=(nc,),
                 in_specs=[pl.BlockSpec((r_rows, w), lambda i: (i, xr_blk)),
                           pl.BlockSpec((r_rows, w), lambda i: (i, yr_blk)),
                           _full((CONV_WIDTH, w)), _full((1, w)), _full((w, w)), _full((1, w)),
                           _full((w, w)), _full((1, w)), _full((1, w))],
                 out_specs=[blk, blk],
                 out_shape=[jax.ShapeDtypeStruct((tp, w), F32)] * 2,
                 scratch_shapes=[pltpu.VMEM((HALO, w), F32), pltpu.VMEM((SUBLANES, w), F32),
                                 pltpu.VMEM((r_rows, w), F32), pltpu.VMEM((r_rows, w), F32)],
                 semantics=("arbitrary",), vmem_bytes=vm)


def _rec_bwd(proj, xr_blk, yr_blk, rec_w, hr, drec, conv_w, conv_b, wa, ba, wx, bx, lru):
    tp = proj.shape[0]
    w = rec_w
    r_rows = REC_ROWS
    nc = tp // r_rows
    hpc = r_rows // HALO

    def body(xr_ref, xh_ref, yr_ref, hr_ref, hh_ref, drec_ref, cw_ref, cb_ref, wa_ref, ba_ref, wx_ref, bx_ref,
             l_ref, dxr_ref, dyr_ref, dwa_ref, dwx_ref, small_ref, lam_s, a_s, dhr_s, carry_s, next_s):
        i = pl.program_id(0)
        first = (nc - 1 - i) == 0

        @pl.when(i == 0)
        def _():
            carry_s[...] = jnp.zeros_like(carry_s)
            next_s[...] = jnp.zeros_like(next_s)
            dwa_ref[...] = jnp.zeros_like(dwa_ref)
            dwx_ref[...] = jnp.zeros_like(dwx_ref)
            small_ref[...] = jnp.zeros_like(small_ref)

        x = xr_ref[...]
        xprev = jnp.where(first, 0.0, xh_ref[...])
        taps = _conv_taps(jnp.concatenate([xprev, x], axis=0))
        xc = cb_ref[...]
        for k in range(CONV_WIDTH):
            xc = xc + cw_ref[k:k + 1, :] * taps[k]
        xcb, r, ig, ls, log_a = _rec_gates(xc, wa_ref, ba_ref, wx_ref, bx_ref, l_ref)
        a = jnp.exp(log_a)
        a2 = a * a
        mult = jnp.sqrt(_one_minus_exp(2.0 * log_a, a2))
        g, dg = _gelu_and_grad(yr_ref[...])
        hr_v = hr_ref[...]
        drec_v = drec_ref[...]
        dhr_s[...] = drec_v * g
        dyr_ref[...] = (drec_v * hr_v * dg).astype(BF16)
        a_s[...] = a

        def tile(jj, carry):
            r0 = pl.multiple_of((r_rows // SUBLANES - 1 - jj) * SUBLANES, SUBLANES)
            at = a_s[pl.ds(r0, SUBLANES), :]
            dt = dhr_s[pl.ds(r0, SUBLANES), :]
            out = [None] * SUBLANES
            for rr in range(SUBLANES - 1, -1, -1):
                lam = dt[rr:rr + 1] + carry
                out[rr] = lam
                carry = at[rr:rr + 1] * lam
            lam_s[pl.ds(r0, SUBLANES), :] = jnp.concatenate(out, axis=0)
            return carry

        carry_s[0:1, :] = lax.fori_loop(0, r_rows // SUBLANES, tile, carry_s[0:1, :])
        lam = lam_s[...]
        hprev = jnp.where(first, 0.0, hh_ref[...])
        hr_prev = pltpu.roll(jnp.concatenate([hprev, hr_v], axis=0), 1, 0)[HALO:]
        da = lam * hr_prev
        dxc = lam * mult * ig
        di = lam * mult * xc
        dmult = lam * ig * xc
        dlog_a = da * a - dmult * a2 / mult
        dr = dlog_a * (RG_C * ls)
        dls = jnp.sum(dlog_a * (RG_C * r), axis=0, keepdims=True)
        dga = dr * r * (1.0 - r)
        dgx = di * ig * (1.0 - ig)
        dgab = dga.astype(BF16)
        dgxb = dgx.astype(BF16)
        dxc = dxc + _dot_nt(dgab, wa_ref[...]) + _dot_nt(dgxb, wx_ref[...])
        dwa_ref[...] += _dot_tn(xcb, dgab)
        dwx_ref[...] += _dot_tn(xcb, dgxb)
        cat = jnp.concatenate([dxc, next_s[...]], axis=0)
        next_s[...] = dxc[0:HALO]
        dxr = cw_ref[CONV_WIDTH - 1:CONV_WIDTH, :] * dxc
        for k in range(CONV_WIDTH - 1):
            sh = CONV_WIDTH - 1 - k
            dxr = dxr + cw_ref[k:k + 1, :] * pltpu.roll(cat, r_rows + HALO - sh, 0)[:r_rows]
        dxr_ref[...] = dxr.astype(BF16)
        rows = [jnp.sum(dxc * taps[k], axis=0, keepdims=True) for k in range(CONV_WIDTH)]
        rows += [jnp.sum(dxc, axis=0, keepdims=True), jnp.sum(dga, axis=0, keepdims=True),
                 jnp.sum(dgx, axis=0, keepdims=True), dls * _sigmoid(-l_ref[...])]
        small_ref[...] += jnp.concatenate(rows, axis=0)

    def rev(i):
        return nc - 1 - i

    def halo(i):
        return jnp.maximum(rev(i) * hpc - 1, 0)

    blk = pl.BlockSpec((r_rows, w), lambda i: (rev(i), 0))
    vm = 40 * _nbytes((r_rows, w), F32) + 6 * _nbytes((w, w), F32)
    return _call(body, (proj, proj, proj, hr, hr, drec, conv_w, conv_b, wa, ba, wx, bx, lru),
                 name="rec_bwd", grid=(nc,),
                 in_specs=[pl.BlockSpec((r_rows, w), lambda i: (rev(i), xr_blk)),
                           pl.BlockSpec((HALO, w), lambda i: (halo(i), xr_blk)),
                           pl.BlockSpec((r_rows, w), lambda i: (rev(i), yr_blk)),
                           blk,
                           pl.BlockSpec((HALO, w), lambda i: (halo(i), 0)),
                           blk,
                           _full((CONV_WIDTH, w)), _full((1, w)), _full((w, w)), _full((1, w)),
                           _full((w, w)), _full((1, w)), _full((1, w))],
                 out_specs=[blk, blk, _full((w, w)), _full((w, w)), _full((SUBLANES, w))],
                 out_shape=[jax.ShapeDtypeStruct((tp, w), BF16)] * 2
                 + [jax.ShapeDtypeStruct((w, w), F32)] * 2 + [jax.ShapeDtypeStruct((SUBLANES, w), F32)],
                 scratch_shapes=[pltpu.VMEM((r_rows, w), F32)] * 3
                 + [pltpu.VMEM((SUBLANES, w), F32), pltpu.VMEM((HALO, w), F32)],
                 semantics=("arbitrary",), vmem_bytes=vm)


ROW_TARGET = 544


def _mixer_out(attn, rec, g_a, g_r, w_out, h, g_next):
    tp, d = h.shape
    aw, rw = attn.shape[1], rec.shape[1]
    kc = d // N_CHIPS
    tm = _divisor_tile(tp, 16, ROW_TARGET)

    def body(a_ref, r_ref, ga_ref, gr_ref, w_ref, h_ref, gn_ref, h1_ref, z_ref, mix_ref):
        a = a_ref[...]
        r = r_ref[...]
        mix = jnp.concatenate([a * _rstd(a) * ga_ref[...], r * _rstd(r) * gr_ref[...]], axis=1).astype(BF16)
        mix_ref[...] = mix
        h1 = h_ref[...]
        for j in range(N_CHIPS):
            h1 = h1 + _dot(mix[:, j * kc:(j + 1) * kc], w_ref[j])
        h1_ref[...] = h1
        z_ref[...] = (h1 * _rstd(h1) * gn_ref[...]).astype(BF16)

    row = lambda wd: pl.BlockSpec((tm, wd), lambda i: (i, 0))
    vm = 2 * _nbytes((d, d), BF16) + 12 * _nbytes((tm, d), F32)
    return _call(body, (attn, rec, g_a, g_r, w_out, h, g_next), name="mixer_out", grid=(tp // tm,),
                 in_specs=[row(aw), row(rw), _full((1, aw)), _full((1, rw)), _full(w_out.shape), row(d),
                           _full((1, d))],
                 out_specs=[row(d), row(d), row(d)],
                 out_shape=[jax.ShapeDtypeStruct((tp, d), F32), jax.ShapeDtypeStruct((tp, d), BF16),
                            jax.ShapeDtypeStruct((tp, d), BF16)],
                 semantics=("parallel",), vmem_bytes=vm)


def _mixer_bwd(dh_b, w_out, attn, rec, g_a, g_r):
    tp, d = dh_b.shape
    aw, rw = attn.shape[1], rec.shape[1]
    tm = _divisor_tile(tp, 16, ROW_TARGET)

    def body(dh_ref, w_ref, a_ref, r_ref, ga_ref, gr_ref, da_ref, dr_ref, dg_ref):
        @pl.when(pl.program_id(0) == 0)
        def _():
            dg_ref[...] = jnp.zeros_like(dg_ref)

        dh = dh_ref[...]
        dmix = jnp.concatenate([_dot_nt(dh, w_ref[j]) for j in range(N_CHIPS)], axis=1)
        da, dga = _rms_bwd(dmix[:, :aw], a_ref[...], ga_ref[...])
        dr, dgr = _rms_bwd(dmix[:, aw:], r_ref[...], gr_ref[...])
        da_ref[...] = da
        dr_ref[...] = dr
        dg_ref[...] += jnp.broadcast_to(jnp.concatenate([dga, dgr], axis=1), (SUBLANES, d))

    row = lambda wd: pl.BlockSpec((tm, wd), lambda i: (i, 0))
    vm = 2 * _nbytes((d, d), BF16) + 12 * _nbytes((tm, d), F32)
    return _call(body, (dh_b, w_out, attn, rec, g_a, g_r), name="mixer_bwd", grid=(tp // tm,),
                 in_specs=[row(d), _full(w_out.shape), row(aw), row(rw), _full((1, aw)), _full((1, rw))],
                 out_specs=[row(aw), row(rw), _full((SUBLANES, d))],
                 out_shape=[jax.ShapeDtypeStruct((tp, aw), F32), jax.ShapeDtypeStruct((tp, rw), F32),
                            jax.ShapeDtypeStruct((SUBLANES, d), F32)],
                 semantics=("arbitrary",), vmem_bytes=vm)


def _mlp_up(z, w_up):
    tp, d = z.shape
    fc = w_up.shape[2]
    ff = N_CHIPS * fc
    tn = _divisor_tile(fc, LANES, 512)
    per = fc // tn

    def body(z_ref, w_ref, act_ref, up_ref):
        up = _dot(z_ref[...], w_ref[...])
        r = jnp.maximum(up, 0.0)
        act_ref[...] = (r * r).astype(BF16)
        up_ref[...] = up.astype(BF16)

    col = pl.BlockSpec((tp, tn), lambda j: (0, j))
    vm = 2 * _nbytes((tp, d), BF16) + 2 * _nbytes((d, tn), BF16) + 8 * _nbytes((tp, tn), F32)
    return _call(body, (z, w_up), name="mlp_up", grid=(ff // tn,),
                 in_specs=[_full((tp, d)), pl.BlockSpec((None, d, tn), lambda j: (j // per, 0, j % per))],
                 out_specs=[col, col],
                 out_shape=[jax.ShapeDtypeStruct((tp, ff), BF16)] * 2,
                 semantics=("parallel",), vmem_bytes=vm)


def _mlp_down(act, w_down, h, g_next):
    tp, d = h.shape
    ff = act.shape[1]
    fc = ff // N_CHIPS
    tm = _divisor_tile(tp, 16, ROW_TARGET)

    def body(a_ref, w_ref, h_ref, gn_ref, h2_ref, z_ref):
        h2 = h_ref[...]
        for j in range(N_CHIPS):
            h2 = h2 + _dot(a_ref[:, j * fc:(j + 1) * fc], w_ref[j])
        h2_ref[...] = h2
        z_ref[...] = (h2 * _rstd(h2) * gn_ref[...]).astype(BF16)

    row = lambda wd: pl.BlockSpec((tm, wd), lambda i: (i, 0))
    vm = 2 * _nbytes((ff, d), BF16) + 2 * _nbytes((tm, ff), BF16) + 10 * _nbytes((tm, d), F32)
    return _call(body, (act, w_down, h, g_next), name="mlp_down", grid=(tp // tm,),
                 in_specs=[row(ff), _full(w_down.shape), row(d), _full((1, d))],
                 out_specs=[row(d), row(d)],
                 out_shape=[jax.ShapeDtypeStruct((tp, d), F32), jax.ShapeDtypeStruct((tp, d), BF16)],
                 semantics=("parallel",), vmem_bytes=vm)


def _loss_bwd(h, g, target, n_real):
    tp, d = h.shape
    tm = _divisor_tile(tp, 16, ROW_TARGET)

    def body(h_ref, g_ref, t_ref, dh_ref, dhb_ref, dg_ref, loss_ref):
        i = pl.program_id(0)

        @pl.when(i == 0)
        def _():
            dg_ref[...] = jnp.zeros_like(dg_ref)
            loss_ref[...] = jnp.zeros_like(loss_ref)

        x = h_ref[...]
        gv = g_ref[...]
        rowi = i * tm + lax.broadcasted_iota(jnp.int32, (tm, 1), 0)
        real = jnp.logical_and(rowi >= N_META, rowi < N_META + n_real)
        err = jnp.where(real, x * _rstd(x) * gv - t_ref[...], 0.0)
        loss_ref[...] += 0.5 * jnp.sum(jnp.mean(err * err, axis=-1, keepdims=True))
        dx, dgp = _rms_bwd(err * (1.0 / d), x, gv)
        dh_ref[...] = dx
        dhb_ref[...] = dx.astype(BF16)
        dg_ref[...] += jnp.broadcast_to(dgp, (SUBLANES, d))

    row = pl.BlockSpec((tm, d), lambda i: (i, 0))
    return _call(body, (h, g, target), name="loss_bwd", grid=(tp // tm,),
                 in_specs=[row, _full((1, d)), row],
                 out_specs=[row, row, _full((SUBLANES, d)), _full((SUBLANES, LANES))],
                 out_shape=[jax.ShapeDtypeStruct((tp, d), F32), jax.ShapeDtypeStruct((tp, d), BF16),
                            jax.ShapeDtypeStruct((SUBLANES, d), F32), jax.ShapeDtypeStruct((SUBLANES, LANES), F32)],
                 semantics=("arbitrary",), vmem_bytes=16 * _nbytes((tm, d), F32))


def _mlp_bwd(dh_b, w_down, up, z2):
    tp, d = dh_b.shape
    fc = w_down.shape[1]
    ff = N_CHIPS * fc
    tn = _divisor_tile(fc, LANES, 512)
    per = fc // tn

    def body(dh_ref, z_ref, w_ref, up_ref, dup_ref, gd_ref, gu_ref):
        dh = dh_ref[...]
        r = jnp.maximum(up_ref[...].astype(F32), 0.0)
        dup = (_dot_nt(dh, w_ref[...]) * (2.0 * r)).astype(BF16)
        dup_ref[...] = dup
        gd_ref[...] = _dot_tn((r * r).astype(BF16), dh).astype(BF16)
        gu_ref[...] = _dot_tn(z_ref[...], dup).astype(BF16)

    col = pl.BlockSpec((tp, tn), lambda j: (0, j))
    vm = 4 * _nbytes((tp, d), BF16) + 4 * _nbytes((tn, d), BF16) + 2 * _nbytes((d, tn), BF16) \
        + 10 * _nbytes((tp, tn), F32) + 4 * _nbytes((tn, d), F32)
    return _call(body, (dh_b, z2, w_down, up), name="mlp_bwd", grid=(ff // tn,),
                 in_specs=[_full((tp, d)), _full((tp, d)),
                           pl.BlockSpec((None, tn, d), lambda j: (j // per, j % per, 0)), col],
                 out_specs=[col, pl.BlockSpec((tn, d), lambda j: (j, 0)),
                            pl.BlockSpec((None, d, tn), lambda j: (j // per, 0, j % per))],
                 out_shape=[jax.ShapeDtypeStruct((tp, ff), BF16), jax.ShapeDtypeStruct((ff, d), BF16),
                            jax.ShapeDtypeStruct((N_CHIPS, d, fc), BF16)],
                 semantics=("parallel",), vmem_bytes=vm)


def _grad_w_pieces(pieces, b):
    tp, n = b.shape
    tn = _divisor_tile(n, LANES, 512)
    widths = [pc.shape[1] for pc in pieces]

    def body(*refs):
        p_refs, b_ref, o_refs = refs[:len(pieces)], refs[len(pieces)], refs[len(pieces) + 1:]
        for p_ref, o_ref in zip(p_refs, o_refs):
            o_ref[...] = _dot_tn(p_ref[...], b_ref[...]).astype(BF16)

    vm = 2 * sum(_nbytes((tp, wd), BF16) for wd in widths) + 2 * _nbytes((tp, tn), BF16) \
        + 4 * sum(_nbytes((wd, tn), F32) for wd in widths) + 2 * _nbytes((tp, max(widths)), F32)
    return _call(body, tuple(pieces) + (b,), name="grad_w_pieces", grid=(n // tn,),
                 in_specs=[_full(pc.shape) for pc in pieces] + [pl.BlockSpec((tp, tn), lambda j: (0, j))],
                 out_specs=[pl.BlockSpec((wd, tn), lambda j: (0, j)) for wd in widths],
                 out_shape=[jax.ShapeDtypeStruct((wd, n), BF16) for wd in widths],
                 semantics=("parallel",), vmem_bytes=vm)


def _dx_norm_bwd(pieces, w, w_spec, w_piece, h, g, dres, dot=_dot_nt):
    tp, d = h.shape
    tm = _divisor_tile(tp, 16, ROW_TARGET)
    n = len(pieces)

    def body(*refs):
        dy_refs = refs[:n]
        w_ref, h_ref, g_ref, dres_ref, dh_ref, dhb_ref, dg_ref = refs[n:]

        @pl.when(pl.program_id(0) == 0)
        def _():
            dg_ref[...] = jnp.zeros_like(dg_ref)

        dz = dot(dy_refs[0][...], w_piece(w_ref, 0))
        for i in range(1, n):
            dz = dz + dot(dy_refs[i][...], w_piece(w_ref, i))
        dx, dgp = _rms_bwd(dz, h_ref[...], g_ref[...])
        dh = dres_ref[...] + dx
        dh_ref[...] = dh
        dhb_ref[...] = dh.astype(BF16)
        dg_ref[...] += jnp.broadcast_to(dgp, (SUBLANES, d))

    row = lambda wd: pl.BlockSpec((tm, wd), lambda i: (i, 0))
    kk = sum(wd for _, _, wd in pieces)
    vm = 2 * _nbytes((d, kk), BF16) + 2 * _nbytes((tm, kk), BF16) + 14 * _nbytes((tm, d), F32)
    piece_specs = [pl.BlockSpec((tm, wd), functools.partial(lambda i, cb: (i, cb), cb=cb)) for _, cb, wd in pieces]
    return _call(body, tuple(a for a, _, _ in pieces) + (w, h, g, dres), name="dx_norm_bwd", grid=(tp // tm,),
                 in_specs=piece_specs + [w_spec, row(d), _full((1, d)), row(d)],
                 out_specs=[row(d), row(d), _full((SUBLANES, d))],
                 out_shape=[jax.ShapeDtypeStruct((tp, d), F32), jax.ShapeDtypeStruct((tp, d), BF16),
                            jax.ShapeDtypeStruct((SUBLANES, d), F32)],
                 semantics=("arbitrary",), vmem_bytes=vm)


def _block_diag(wg):
    nb, b, _ = wg.shape
    eye = jnp.eye(nb, dtype=wg.dtype)
    return (eye[:, None, :, None] * wg[:, :, None, :]).reshape(nb * b, nb * b)


def _diag_blocks(dense, nb):
    b = dense.shape[0] // nb
    d4 = dense.reshape(nb, b, nb, b)
    return jnp.stack([d4[i, :, i, :] for i in range(nb)])


def _row(v):
    return v.reshape(1, -1)


def _forward_layer(l, h, z, p, fetch, g_next):
    d = h.shape[1]
    att_w = d // 2
    rec_w = d - att_w
    nh = att_w // HEAD_DIM
    wa_d = _block_diag(p["w_gate_a"][l]).astype(BF16)
    wx_d = _block_diag(p["w_gate_x"][l]).astype(BF16)
    b_f_pad = jnp.zeros((1, LANES), F32).at[0, :nh].set(p["b_f"][l])
    w_in_t = fetch("w_in", z)
    big = dict(w_in_big=_pack_w_in_t(w_in_t.reshape(-1, d), att_w, nh))
    qkv, proj = _proj(z, big["w_in_big"], att_w)
    c, c_t = _fgate_fwd(proj, b_f_pad, nh)
    attn, lse_b = _attn_fwd(qkv, c, c_t, nh)
    hr, rec = _rec_fwd(proj, 0, 1, rec_w, p["conv_w"][l], _row(p["conv_b"][l]), wa_d,
                       _row(p["b_gate_a"][l]), wx_d, _row(p["b_gate_x"][l]), _row(p["lru_L"][l]))
    big["w_out"] = fetch("w_out", rec)
    h1, z2, mix = _mixer_out(attn, rec, _row(p["attn_out_g"][l]), _row(p["rec_out_g"][l]),
                             big["w_out"], h, _row(p["mlp_norm_g"][l]))
    big["w_up"] = fetch("w_up", h1)
    act, up = _mlp_up(z2, big["w_up"])
    big["w_down"] = fetch("w_down", act)
    h2, z_next = _mlp_down(act, big["w_down"], h1, _row(g_next))
    saved = dict(h0=h, z1=z, proj=proj, qkv=qkv, c=c, c_t=c_t, attn=attn, lse_b=lse_b, hr=hr, rec=rec, h1=h1,
                 z2=z2, mix=mix, up=up, wa_d=wa_d, wx_d=wx_d, b_f_pad=b_f_pad, big=big)
    return h2, z_next, saved


def _backward_mlp(l, dh, dh_b, sv, p, tok):
    w_up, w_down = sv["big"]["w_up"], sv["big"]["w_down"]
    fc = w_up.shape[2]
    dup, g_down, g_up = _mlp_bwd(dh_b, w_down, sv["up"], sv["z2"])
    dh, dh_b, dg2 = _dx_norm_bwd([(dup, j, fc) for j in range(N_CHIPS)], w_up, _full(w_up.shape),
                                 lambda w_ref, j: w_ref[j], sv["h1"], _row(p["mlp_norm_g"][l] + tok), dh)
    big = dict(w_down=g_down.reshape((N_CHIPS, -1) + g_down.shape[1:]), w_up=g_up)
    return dh, dh_b, big, dict(mlp_norm_g=dg2[0])


def _backward_mixer(l, dh, dh_b, sv, p, tok):
    d = dh.shape[1]
    att_w = d // 2
    rec_w = d - att_w
    nh = att_w // HEAD_DIM
    small = {}
    g_out, = _grad_w_pieces([sv["mix"]], dh_b)
    dattn, drec, dg_mix = _mixer_bwd(dh_b, sv["big"]["w_out"], sv["attn"], sv["rec"],
                                     _row(p["attn_out_g"][l] + tok), _row(p["rec_out_g"][l]))
    small["attn_out_g"] = dg_mix[0, :att_w]
    small["rec_out_g"] = dg_mix[0, att_w:]
    dxr, dyr, dwa, dwx, sm = _rec_bwd(
        sv["proj"], 0, 1, rec_w, sv["hr"], drec, p["conv_w"][l], _row(p["conv_b"][l]), sv["wa_d"],
        _row(p["b_gate_a"][l]), sv["wx_d"], _row(p["b_gate_x"][l]), _row(p["lru_L"][l]))
    small.update(conv_w=sm[:CONV_WIDTH], conv_b=sm[4], b_gate_a=sm[5], b_gate_x=sm[6], lru_L=sm[7],
                 w_gate_a=_diag_blocks(dwa, N_REC_BLOCKS), w_gate_x=_diag_blocks(dwx, N_REC_BLOCKS))
    dq, dk, dv, dc = _attn_bwd(sv["qkv"], sv["c"], sv["c_t"], sv["lse_b"], dattn, nh)
    df, db_f = _fgate_bwd(sv["proj"], sv["b_f_pad"], dc)
    small["b_f"] = db_f[0, :nh]
    pieces = [dq, dk, dv, dxr, dyr, df]
    offs = [0, att_w, 2 * att_w, 3 * att_w, 3 * att_w + rec_w, 3 * att_w + 2 * rec_w]
    gq, gk, gv, gxr, gyr, gf = _grad_w_pieces(pieces, sv["z1"])
    g_in_t = jnp.concatenate([gq, gk, gv, gf[:nh], gxr, gyr], axis=0)
    w_big = sv["big"]["w_in_big"]
    widths = [pc.shape[1] for pc in pieces]
    dh, dh_b, dg1 = _dx_norm_bwd(
        [(pc, 0, wd) for pc, wd in zip(pieces, widths)], w_big, _full(w_big.shape),
        lambda w_ref, i: w_ref[offs[i]:offs[i] + widths[i], :], sv["h0"], _row(p["attn_norm_g"][l]), dh, dot=_dot)
    small["attn_norm_g"] = dg1[0]
    big = dict(w_in=g_in_t.reshape(N_CHIPS, -1, d), w_out=g_out.reshape((N_CHIPS, -1) + g_out.shape[1:]))
    return dh, dh_b, big, small


def _pack_w_in_t(w_in_t, att_w, nh):
    qkv = w_in_t[:3 * att_w]
    f = w_in_t[3 * att_w:3 * att_w + nh]
    xy = w_in_t[3 * att_w + nh:]
    return jnp.concatenate([qkv, xy, f, jnp.zeros((LANES - nh, w_in_t.shape[1]), w_in_t.dtype)], axis=0)


ANY = pl.BlockSpec(memory_space=pl.ANY)


def _coords():
    return lax.axis_index("x"), lax.axis_index("y"), lax.axis_index("c")


def _other_chips(x, y):
    return [(1 - x, y), (x, 1 - y), (1 - x, 1 - y)]


def _remote(src, dst, send_sems, recv_sems, k, to):
    return pltpu.make_async_remote_copy(src_ref=src, dst_ref=dst, send_sem=send_sems.at[k],
                                        recv_sem=recv_sems.at[k], device_id=to, device_id_type=MESH)


def _all_gather_chips(shards):
    n = len(shards)
    per = 6

    def body(*refs):
        ins, outs = refs[:n], refs[n:2 * n]
        send_sems, recv_sems, local_sems = refs[2 * n:]
        x, y, c = _coords()
        me = 2 * x + y
        sibling = (x, y, 1 - c)
        chips = _other_chips(x, y)
        local = [pltpu.make_async_copy(ins[t], outs[t].at[me], local_sems.at[t]) for t in range(n)]
        for cp in local:
            cp.start()
        sends = []
        for t in range(n):
            for j, (px, py) in enumerate(chips):
                cp = _remote(ins[t].at[c], outs[t].at[me, c], send_sems, recv_sems, per * t + j, (px, py, c))
                cp.start()
                sends.append(cp)
        for t in range(n):
            for j, (px, py) in enumerate(chips):
                landed = outs[t].at[2 * px + py, c]
                _remote(landed, landed, send_sems, recv_sems, per * t + j, (px, py, c)).wait_recv()
                cp = _remote(landed, landed, send_sems, recv_sems, per * t + 3 + j, sibling)
                cp.start()
                sends.append(cp)
        for t in range(n):
            for j, (px, py) in enumerate(chips):
                passed = outs[t].at[2 * px + py, 1 - c]
                _remote(passed, passed, send_sems, recv_sems, per * t + 3 + j, sibling).wait_recv()
        for cp in sends:
            cp.wait_send()
        for cp in local:
            cp.wait()

    return _call(body, tuple(shards), name="all_gather_chips",
                 in_specs=[ANY] * n, out_specs=[ANY] * n,
                 out_shape=[jax.ShapeDtypeStruct((N_CHIPS,) + s.shape, s.dtype) for s in shards],
                 scratch_shapes=[pltpu.SemaphoreType.DMA((per * n,)), pltpu.SemaphoreType.DMA((per * n,)),
                                 pltpu.SemaphoreType.DMA((n,))])


HBM = pl.BlockSpec(memory_space=pltpu.HBM)
SEM = pl.BlockSpec(memory_space=pltpu.SEMAPHORE)
DATAFLOW = pltpu.SideEffectType.DATAFLOW_SIDE_EFFECTING


def _in_hbm(a):
    return pltpu.with_memory_space_constraint(a, pltpu.HBM)


PUSH_ARRIVALS = {"gather_chips": N_CHIPS - 1, "scatter_chips": N_CHIPS - 1, "sibling": 1, "gather_devices": N_DEV - 1}


def _push_copies(mode, src, land, send_sems, recv_sems, t):
    x, y, c = _coords()
    chip = 2 * x + y
    if mode == "gather_chips":
        return [_remote(src.at[chip], land.at[chip], send_sems, recv_sems, t, (px, py, c))
                for px, py in _other_chips(x, y)]
    if mode == "scatter_chips":
        return [_remote(src.at[2 * px + py], land.at[chip], send_sems, recv_sems, t, (px, py, c))
                for px, py in _other_chips(x, y)]
    if mode == "sibling":
        return [_remote(src, land, send_sems, recv_sems, t, (x, y, 1 - c))]
    dev = 4 * x + 2 * y + c
    return [_remote(src.at[dev], land.at[dev], send_sems, recv_sems, t, (x ^ (k >> 2), y ^ ((k >> 1) & 1), c ^ (k & 1)))
            for k in range(1, N_DEV)]


def _push_start(srcs, lands, mode, name):
    n = len(srcs)
    same = all(s is ld for s, ld in zip(srcs, lands))
    n_in = n if same else 2 * n

    def body(*refs):
        src_refs = refs[:n]
        land_refs = src_refs if same else refs[n:2 * n]
        send_sems, recv_sems = refs[n_in], refs[n_in + 1]
        token = refs[-1]
        for t in range(n):
            for cp in _push_copies(mode, src_refs[t], land_refs[t], send_sems, recv_sems, t):
                cp.start()
        token[...] = jnp.zeros_like(token)

    operands = tuple(srcs) if same else tuple(srcs) + tuple(lands)
    res = _call(
        body, [_in_hbm(a) for a in operands], name=name,
        out_shape=(pltpu.SemaphoreType.DMA((n,)), pltpu.SemaphoreType.DMA((n,)))
        + tuple(pltpu.HBM(a.shape, a.dtype) for a in operands) + (jax.ShapeDtypeStruct((SUBLANES, LANES), F32),),
        in_specs=[HBM] * n_in, out_specs=(SEM, SEM) + (HBM,) * n_in + (pl.BlockSpec(memory_space=pltpu.VMEM),),
        input_output_aliases={i: 2 + i for i in range(n_in)}, side_effects=DATAFLOW, hbm_results=False)
    send_sems, recv_sems, token = res[0], res[1], res[-1]
    srcs_thru = res[2:2 + n]
    lands_thru = srcs_thru if same else res[2 + n:2 + 2 * n]
    return send_sems, recv_sems, srcs_thru, lands_thru, token


def _push_wait(send_sems, recv_sems, ids, srcs, lands, mode, after, name):
    n = len(lands)
    same = all(s is ld for s, ld in zip(srcs, lands))
    n_in = n if same else 2 * n

    def body(*refs):
        land_refs = refs[:n] if same else refs[n:2 * n]
        send_sems, recv_sems = refs[n_in], refs[n_in + 1]
        x, y, c = _coords()
        for t in range(n):
            moved = land_refs[t] if mode == "sibling" else land_refs[t].at[pl.ds(0, PUSH_ARRIVALS[mode])]
            arrivals = _remote(moved, moved, send_sems, recv_sems, ids[t], (x, y, c))
            arrivals.wait_send()
            arrivals.wait_recv()

    operands = tuple(lands) if same else tuple(srcs) + tuple(lands)
    res = _call(
        body, operands + (send_sems, recv_sems, after), name=name,
        out_shape=tuple(pltpu.HBM(a.shape, a.dtype) for a in operands),
        in_specs=[HBM] * n_in + [SEM, SEM, ANY], out_specs=(HBM,) * n_in,
        input_output_aliases={i: i for i in range(n_in)}, side_effects=DATAFLOW)
    return list(res) if same else (list(res[:n]), list(res[n:]))


def _sum_partials(part, landed, chip):
    _, rows, cols = part.shape
    br = _divisor_tile(rows, 16, ELEM_ROWS)

    def body(chip_ref, own_ref, a_ref, b_ref, c_ref, o_ref):
        o_ref[...] = ((own_ref[...].astype(F32) + a_ref[...].astype(F32)) + b_ref[...].astype(F32)) \
            + c_ref[...].astype(F32)

    def other(k):
        return pl.BlockSpec((None, br, cols), lambda i, ch: (jnp.where(ch[0] <= k, k + 1, k), i, 0))

    spec = pltpu.PrefetchScalarGridSpec(
        num_scalar_prefetch=1, grid=(rows // br,),
        in_specs=[pl.BlockSpec((None, br, cols), lambda i, ch: (ch[0], i, 0)), other(0), other(1), other(2)],
        out_specs=pl.BlockSpec((br, cols), lambda i, ch: (i, 0)))
    return _call(body, (chip, part, landed, landed, landed), name="sum_partials", grid_spec=spec,
                 out_shape=jax.ShapeDtypeStruct((rows, cols), F32), semantics=("parallel",))


def _cast_to_slab(w, l, chip):
    _, rows, cols = w.shape
    br = _divisor_tile(rows, 16, ELEM_ROWS)

    def body(chip_ref, w_ref, o_ref):
        o_ref[...] = w_ref[...].astype(BF16)

    spec = pltpu.PrefetchScalarGridSpec(
        num_scalar_prefetch=1, grid=(rows // br,),
        in_specs=[pl.BlockSpec((None, br, cols), lambda i, ch: (l, i, 0))],
        out_specs=pl.BlockSpec((None, br, cols), lambda i, ch: (ch[0], i, 0)))
    return _call(body, (chip, w), name="cast_to_slab", grid_spec=spec,
                 out_shape=jax.ShapeDtypeStruct((N_CHIPS, rows, cols), BF16), semantics=("parallel",))


def _cast_w_in_t_to_slabs(w_t, chip):
    rows, depth, d = w_t.shape
    tn = _divisor_tile(d, LANES, 256)

    def body(chip_ref, w_ref, *o_refs):
        for l in range(depth):
            o_refs[l][...] = w_ref[:, l, :].astype(BF16)

    spec = pltpu.PrefetchScalarGridSpec(
        num_scalar_prefetch=1, grid=(d // tn,),
        in_specs=[pl.BlockSpec((rows, depth, tn), lambda j, ch: (0, 0, j))],
        out_specs=[pl.BlockSpec((None, rows, tn), lambda j, ch: (ch[0], 0, j))] * depth)
    return _call(body, (chip, w_t), name="cast_w_in_t_to_slabs", grid_spec=spec,
                 out_shape=[jax.ShapeDtypeStruct((N_CHIPS, rows, d), BF16)] * depth, semantics=("parallel",),
                 vmem_bytes=4 * _nbytes((rows, max(depth, SUBLANES), tn), F32))


def _place_slab(buf, index, n_slabs):
    rows, cols = buf.shape
    br = _divisor_tile(rows, SUBLANES, ELEM_ROWS)

    def body(index_ref, b_ref, o_ref):
        o_ref[...] = b_ref[...]

    spec = pltpu.PrefetchScalarGridSpec(
        num_scalar_prefetch=1, grid=(rows // br,),
        in_specs=[pl.BlockSpec((br, cols), lambda i, ix: (i, 0))],
        out_specs=pl.BlockSpec((None, br, cols), lambda i, ix: (ix[0], i, 0)))
    return _call(body, (index, buf), name="place_slab", grid_spec=spec,
                 out_shape=jax.ShapeDtypeStruct((n_slabs, rows, cols), buf.dtype), semantics=("parallel",))


ELEM_ROWS = 256


def _sum_slabs(r):
    n, rows, cols = r.shape
    br = _divisor_tile(rows, 16, ELEM_ROWS)

    def body(r_ref, o_ref):
        acc = r_ref[0].astype(F32)
        for j in range(1, n):
            acc = acc + r_ref[j].astype(F32)
        o_ref[...] = acc

    return _call(body, (r,), name="sum_slabs", grid=(rows // br,),
                 in_specs=[pl.BlockSpec((n, br, cols), lambda i: (0, i, 0))],
                 out_specs=pl.BlockSpec((br, cols), lambda i: (i, 0)),
                 out_shape=jax.ShapeDtypeStruct((rows, cols), F32), semantics=("parallel",))


def _adamw_math(w, g, m, v):
    c1 = 1.0 - ADAM_B1 ** ADAM_STEP
    c2 = 1.0 - ADAM_B2 ** ADAM_STEP
    nm = ADAM_B1 * m + (1.0 - ADAM_B1) * g
    nv = ADAM_B2 * v + (1.0 - ADAM_B2) * (g * g)
    delta = -ADAM_LR * ((nm / c1) / (jnp.sqrt(nv / c2) + ADAM_EPS) + ADAM_WD * w)
    return delta, nm, nv


def _adamw(w, g, m, v):
    rows, cols = w.shape
    br = _divisor_tile(rows, 8, ELEM_ROWS)

    def body(w_ref, g_ref, m_ref, v_ref, d_ref, nm_ref, nv_ref):
        d_ref[...], nm_ref[...], nv_ref[...] = _adamw_math(w_ref[...], g_ref[...], m_ref[...], v_ref[...])

    blk = pl.BlockSpec((br, cols), lambda i: (i, 0))
    return _call(body, (w, g, m, v), name="adamw", grid=(rows // br,),
                 in_specs=[blk] * 4, out_specs=[blk] * 3,
                 out_shape=[jax.ShapeDtypeStruct((rows, cols), F32)] * 3, semantics=("parallel",))


def _adamw_w_in_t(w_t, m_t, v_t, g_mine, g_theirs):
    rows, depth, d = w_t.shape
    tn = LANES

    def body(w_ref, m_ref, v_ref, ga_ref, gb_ref, g_ref, d_ref, nm_ref, nv_ref):
        g = ga_ref[...] + gb_ref[...]
        g_ref[...] = g
        d_ref[...], nm_ref[...], nv_ref[...] = _adamw_math(w_ref[...], g, m_ref[...], v_ref[...])

    slab = pl.BlockSpec((rows, depth, tn), lambda j: (0, 0, j))
    return _call(body, (w_t, m_t, v_t, g_mine, g_theirs), name="adamw_w_in_t", grid=(d // tn,),
                 in_specs=[slab] * 5, out_specs=[slab] * 4,
                 out_shape=[jax.ShapeDtypeStruct(w_t.shape, F32)] * 4, semantics=("parallel",),
                 vmem_bytes=2 * 9 * _nbytes((rows, max(depth, SUBLANES), tn), F32))


def _adamw_layer(w, m, v, l, g_mine, g_theirs, prev, after):
    _, rows, cols = w.shape
    br = _divisor_tile(rows, 8, ELEM_ROWS)

    def body(w_ref, m_ref, v_ref, ga_ref, gb_ref, *rest):
        g_ref, d_ref, nm_ref, nv_ref = rest[5:]
        g = ga_ref[...] + gb_ref[...]
        g_ref[...] = g
        d_ref[...], nm_ref[...], nv_ref[...] = _adamw_math(w_ref[...], g, m_ref[...], v_ref[...])

    slot = pl.BlockSpec((None, br, cols), lambda i: (l, i, 0))
    blk = pl.BlockSpec((br, cols), lambda i: (i, 0))
    return _call(body, (w, m, v, g_mine, g_theirs) + tuple(prev) + (after,), name="adamw_layer",
                 grid=(rows // br,), in_specs=[slot] * 3 + [blk] * 2 + [ANY] * 5, out_specs=[slot] * 4,
                 out_shape=[jax.ShapeDtypeStruct(w.shape, F32)] * 4,
                 input_output_aliases={5: 0, 6: 1, 7: 2, 8: 3}, semantics=("parallel",))


BIG = ("w_in", "w_out", "w_up", "w_down")
WEIGHTS = ("meta", "attn_norm_g", "w_in", "b_f", "conv_w", "conv_b", "w_gate_a", "b_gate_a", "w_gate_x",
           "b_gate_x", "lru_L", "attn_out_g", "rec_out_g", "w_out", "mlp_norm_g", "w_up", "w_down", "final_g")
SMALL = tuple(k for k in WEIGHTS if k not in BIG)
COL_SHARDED_SMALL = ("meta", "conv_w")


def _packed_rows(shape):
    return -(-math.prod(shape) // (SUBLANES * LANES)) * SUBLANES


def _pack(arrs):
    rows = []
    for a in arrs:
        flat = a.reshape(-1)
        rows.append(jnp.pad(flat, (0, _packed_rows(a.shape) * LANES - flat.shape[0])).reshape(-1, LANES))
    used = sum(r.shape[0] for r in rows)
    rows.append(jnp.zeros((-used % ELEM_ROWS, LANES), F32))
    return jnp.concatenate(rows, axis=0)


def _unpack(buf, shapes):
    out, r0 = [], 0
    for s in shapes:
        nr = _packed_rows(s)
        out.append(buf[r0:r0 + nr].reshape(-1)[:math.prod(s)].reshape(s))
        r0 += nr
    return out


def _halves(a):
    return a.reshape((2, a.shape[0] // 2) + a.shape[1:])


def _cols_from_chips(g):
    return jnp.moveaxis(g, 0, -2).reshape(g.shape[1:-1] + (N_CHIPS * g.shape[-1],))


def kernel(x, meta, attn_norm_g, w_in, b_f, conv_w, conv_b, w_gate_a, b_gate_a, w_gate_x, b_gate_x, lru_L, attn_out_g, rec_out_g, w_out, mlp_norm_g, w_up, w_down, final_g, loss_target, m_meta, m_attn_norm_g, m_w_in, m_b_f, m_conv_w, m_conv_b, m_w_gate_a, m_b_gate_a, m_w_gate_x, m_b_gate_x, m_lru_L, m_attn_out_g, m_rec_out_g, m_w_out, m_mlp_norm_g, m_w_up, m_w_down, m_final_g, v_meta, v_attn_norm_g, v_w_in, v_b_f, v_conv_w, v_conv_b, v_w_gate_a, v_b_gate_a, v_w_gate_x, v_b_gate_x, v_lru_L, v_attn_out_g, v_rec_out_g, v_w_out, v_mlp_norm_g, v_w_up, v_w_down, v_final_g):
    w = dict(meta=meta, attn_norm_g=attn_norm_g, w_in=w_in, b_f=b_f, conv_w=conv_w, conv_b=conv_b,
             w_gate_a=w_gate_a, b_gate_a=b_gate_a, w_gate_x=w_gate_x, b_gate_x=b_gate_x, lru_L=lru_L,
             attn_out_g=attn_out_g, rec_out_g=rec_out_g, w_out=w_out, mlp_norm_g=mlp_norm_g, w_up=w_up,
             w_down=w_down, final_g=final_g)
    m = dict(meta=m_meta, attn_norm_g=m_attn_norm_g, w_in=m_w_in, b_f=m_b_f, conv_w=m_conv_w, conv_b=m_conv_b,
             w_gate_a=m_w_gate_a, b_gate_a=m_b_gate_a, w_gate_x=m_w_gate_x, b_gate_x=m_b_gate_x, lru_L=m_lru_L,
             attn_out_g=m_attn_out_g, rec_out_g=m_rec_out_g, w_out=m_w_out, mlp_norm_g=m_mlp_norm_g,
             w_up=m_w_up, w_down=m_w_down, final_g=m_final_g)
    v = dict(meta=v_meta, attn_norm_g=v_attn_norm_g, w_in=v_w_in, b_f=v_b_f, conv_w=v_conv_w, conv_b=v_conv_b,
             w_gate_a=v_w_gate_a, b_gate_a=v_b_gate_a, w_gate_x=v_w_gate_x, b_gate_x=v_b_gate_x, lru_L=v_lru_L,
             attn_out_g=v_attn_out_g, rec_out_g=v_rec_out_g, w_out=v_w_out, mlp_norm_g=v_mlp_norm_g,
             w_up=v_w_up, w_down=v_w_down, final_g=v_final_g)
    s_len, d = x.shape[1], x.shape[2]
    depth = w_in.shape[0]
    att_w = d // 2
    rec_w = d - att_w
    nh = att_w // HEAD_DIM
    chip = 2 * lax.axis_index("x") + lax.axis_index("y")

    g_conv, g_meta = [g.reshape((N_CHIPS, g.shape[1] * g.shape[2]) + g.shape[3:])
                      for g in _all_gather_chips([_halves(w["conv_w"]), _halves(w["meta"])])]
    p = dict(w)
    p["conv_w"] = _cols_from_chips(g_conv)
    meta_full = jnp.moveaxis(g_meta, 0, 1).reshape(N_META, d)

    chip1 = chip.reshape(1).astype(jnp.int32)
    w_in_t, m_in_t, v_in_t = [jnp.transpose(a["w_in"], (2, 0, 1)) for a in (w, m, v)]
    w_in_slabs = _cast_w_in_t_to_slabs(w_in_t, chip1)
    pushes, tokens = [], []
    for l in range(depth):
        slabs = [w_in_slabs[l]] + [_cast_to_slab(w[k], l, chip1) for k in BIG[1:]]
        send_sems, recv_sems, _, lands, token = _push_start(slabs, slabs, "gather_chips", f"weights_start_{l}")
        pushes.append((send_sems, recv_sems, lands))
        tokens.append(token[0, 0])

    t_len = N_META + s_len
    pad = -t_len % SEQ_TILE
    h = jnp.concatenate([meta_full, x[0], jnp.zeros((pad, d), F32)], axis=0)
    tgt = jnp.concatenate([jnp.zeros((N_META, d), F32), loss_target[0], jnp.zeros((pad, d), F32)], axis=0)
    z = _rms_fwd(h, _row(p["attn_norm_g"][0] + sum(tokens)))
    saved = []
    for l in range(depth):
        send_sems, recv_sems, lands = pushes[l]

        def fetch(k, after, l=l, send_sems=send_sems, recv_sems=recv_sems, lands=lands):
            i = BIG.index(k)
            return _push_wait(send_sems, recv_sems, [i], [lands[i]], [lands[i]], "gather_chips", after,
                              f"{k}_wait_{l}")[0]

        g_next = p["attn_norm_g"][l + 1] if l + 1 < depth else p["final_g"]
        h, z, sv = _forward_layer(l, h, z, p, fetch, g_next)
        saved.append(sv)
    dh, dh_b, dg_final, loss_part = _loss_bwd(h, _row(p["final_g"]), tgt, s_len)

    small = {k: [None] * depth for k in SMALL if k not in ("meta", "final_g")}
    pushes = [None] * depth
    tok = 0.0
    for l in reversed(range(depth)):
        dh, dh_b, big_mlp, sm_mlp = _backward_mlp(l, dh, dh_b, saved[l], p, tok)
        parts = [big_mlp["w_down"], big_mlp["w_up"]]
        push_mlp = _push_start(parts, [lax.empty(a.shape, a.dtype) for a in parts], "scatter_chips",
                               f"mlp_grads_start_{l}")
        dh, dh_b, big_mix, sm_mix = _backward_mixer(l, dh, dh_b, saved[l], p, push_mlp[4][0, 0])
        parts = [big_mix["w_out"], big_mix["w_in"]]
        push_mix = _push_start(parts, [lax.empty(a.shape, a.dtype) for a in parts], "scatter_chips",
                               f"mixer_grads_start_{l}")
        tok = push_mix[4][0, 0]
        pushes[l] = {("w_down", "w_up"): push_mlp, ("w_out", "w_in"): push_mix}
        for k, val in {**sm_mlp, **sm_mix}.items():
            small[k][l] = val
    grads = {k: jnp.stack(val) for k, val in small.items()}
    grads["final_g"] = dg_final[0]
    grads["meta"] = dh[:N_META]
    dx = dh[N_META:t_len]

    full_shapes = [grads[k].shape for k in SMALL] + [(1,)]
    packed = _pack([grads[k].astype(F32) for k in SMALL] + [loss_part[0, :1] + tok])
    dev1 = (2 * chip + lax.axis_index("c")).reshape(1).astype(jnp.int32)
    slabs = [_place_slab(packed, dev1, N_DEV)]
    small_push = _push_start(slabs, slabs, "gather_devices", "small_grads_start")

    last_token = small_push[4]
    outs = {k: [lax.empty(w[k].shape, F32) for _ in range(4)] for k in BIG[1:]}
    w_in_sums = [None] * depth
    swaps = {}

    def finish(l, wait_after, adam_after):
        send_sems, recv_sems, mine, lands, _ = swaps[l]
        mine, theirs = _push_wait(send_sems, recv_sems, list(range(len(BIG))), mine, lands, "sibling", wait_after,
                                  f"sums_wait_{l}")
        w_in_sums[l] = (mine[0], theirs[0])
        for k, a, b in zip(BIG[1:], mine[1:], theirs[1:]):
            outs[k] = _adamw_layer(w[k], m[k], v[k], l, a, b, outs[k], adam_after)

    for l in reversed(range(depth)):
        sums = {}
        for names, (send_sems, recv_sems, parts, lands, _) in pushes[l].items():
            parts, landed = _push_wait(send_sems, recv_sems, [0, 1], parts, lands, "scatter_chips", last_token,
                                       f"{names[0]}_grads_wait_{l}")
            for k, part, land in zip(names, parts, landed):
                sums[k] = _sum_partials(part, land, chip1)
        mine = [sums[k] for k in BIG]
        swaps[l] = _push_start(mine, [lax.empty(a.shape, a.dtype) for a in mine], "sibling", f"sums_start_{l}")
        if l + 1 < depth:
            finish(l + 1, outs["w_down"][0] if l + 2 < depth else mine[0], swaps[l][4])
    finish(0, outs["w_down"][0] if depth > 1 else swaps[0][4], swaps[0][4])
    outs["w_in"] = [jnp.transpose(r, (1, 2, 0)) for r in _adamw_w_in_t(
        w_in_t, m_in_t, v_in_t, jnp.stack([s[0] for s in w_in_sums], axis=1),
        jnp.stack([s[1] for s in w_in_sums], axis=1))]
    out_g, out_d, out_m, out_v = [{k: outs[k][i] for k in BIG} for i in range(4)]

    landed = _push_wait(small_push[0], small_push[1], [0], small_push[3], small_push[3], "gather_devices",
                        out_g["w_in"], "small_grads_wait")
    total = _sum_slabs(landed[0])
    small_g = dict(zip(SMALL + ("loss",), _unpack(total, full_shapes)))
    for k in COL_SHARDED_SMALL:
        n = w[k].shape[-1]
        small_g[k] = lax.dynamic_slice_in_dim(small_g[k], chip * n, n, axis=small_g[k].ndim - 1)
    local_shapes = [w[k].shape for k in SMALL]
    res = _adamw(_pack([w[k] for k in SMALL]), _pack([small_g[k] for k in SMALL]),
                 _pack([m[k] for k in SMALL]), _pack([v[k] for k in SMALL]))
    out_g.update({k: small_g[k] for k in SMALL})
    for dst, buf in zip((out_d, out_m, out_v), res):
        dst.update(zip(SMALL, _unpack(buf, local_shapes)))

    return (small_g["loss"].reshape(()), dx[None],
            *[out_g[k] for k in WEIGHTS], *[out_d[k] for k in WEIGHTS],
            *[out_m[k] for k in WEIGHTS], *[out_v[k] for k in WEIGHTS])
```

```python
import functools
import math

import jax
import jax.numpy as jnp
from jax import lax
from jax.experimental import pallas as pl
from jax.experimental.pallas import tpu as pltpu

F32 = jnp.float32
BF16 = jnp.bfloat16

N_META = 16
HEAD_DIM = 64
N_REC_BLOCKS = 8
CONV_WIDTH = 4
RG_C = 8.0
NORM_EPS = 1e-6
ADAM_LR = 0.001
ADAM_B1 = 0.9
ADAM_B2 = 0.999
ADAM_EPS = 1e-08
ADAM_WD = 0.01
ADAM_STEP = 10

LANES = 128
SUBLANES = 8
SEQ_TILE = 128
VMEM_CAP = 60 * 2**20
VMEM_SLACK = 6 * 2**20
NEG_BIG = -1e30
N_CHIPS = 4
N_DEV = 8
MESH = pl.DeviceIdType.MESH


def _nbytes(shape, dtype):
    return math.prod(shape) * jnp.dtype(dtype).itemsize


def _call(body, args, *, name, out_shape, grid=(), in_specs=None, out_specs=None, scratch_shapes=(),
          grid_spec=None, semantics=None, vmem_bytes=None, side_effects=None, hbm_results=True, **kw):
    cp = {}
    if semantics is not None:
        cp["dimension_semantics"] = semantics
    if vmem_bytes is not None:
        cp["vmem_limit_bytes"] = int(min(VMEM_CAP, vmem_bytes + VMEM_SLACK))
    if side_effects is not None:
        cp["has_side_effects"] = side_effects
    if grid_spec is not None:
        kw["grid_spec"] = grid_spec
    else:
        kw.update(grid=grid, in_specs=in_specs, out_specs=out_specs, scratch_shapes=scratch_shapes)
    if hbm_results:
        out_shape = jax.tree.map(
            lambda s: pltpu.HBM(s.shape, s.dtype) if isinstance(s, jax.ShapeDtypeStruct) else s, out_shape)
    fn = pl.pallas_call(
        body, name=name, out_shape=out_shape,
        compiler_params=pltpu.CompilerParams(**cp), **kw)
    return fn(*[_in_hbm(a) if jnp.issubdtype(getattr(a, "dtype", jnp.int32), jnp.floating) else a for a in args])


def _divisor_tile(n, unit, target):
    best = None
    for t in range(unit, min(n, target) + 1, unit):
        if n % t == 0:
            best = t
    return n if best is None else best


def _sigmoid(x):
    return 1.0 / (1.0 + jnp.exp(-x))


def _log1p_unit(e):
    series = e * (1.0 - e * (0.5 - e * (1.0 / 3.0)))
    return jnp.where(e < 1e-2, series, jnp.log(1.0 + e))


def _log_sigmoid(x):
    return jnp.minimum(x, 0.0) - _log1p_unit(jnp.exp(-jnp.abs(x)))


def _one_minus_exp(x, exp_x):
    small = -x * (1.0 + x * (1.0 / 2 + x * (1.0 / 6 + x * (1.0 / 24 + x * (1.0 / 120 + x * (1.0 / 720))))))
    return jnp.where(x > -0.25, small, 1.0 - exp_x)


_GELU_K = math.sqrt(2.0 / math.pi)
_GELU_C = 0.044715


def _gelu_and_grad(y):
    th = jnp.tanh(_GELU_K * (y + _GELU_C * y * y * y))
    g = 0.5 * y * (1.0 + th)
    dg = 0.5 * (1.0 + th) + 0.5 * y * (1.0 - th * th) * _GELU_K * (1.0 + 3.0 * _GELU_C * y * y)
    return g, dg


def _rstd(x):
    return lax.rsqrt(jnp.mean(x * x, axis=-1, keepdims=True) + NORM_EPS)


def _rms_bwd(dz, x, g):
    rs = _rstd(x)
    xh = x * rs
    dgp = jnp.sum(dz * xh, axis=0, keepdims=True)
    dxh = dz * g
    dx = rs * (dxh - xh * jnp.mean(dxh * xh, axis=-1, keepdims=True))
    return dx, dgp


def _dot(a, b):
    return jnp.dot(a, b, preferred_element_type=F32)


def _dot_nt(a, b):
    return lax.dot_general(a, b, (((1,), (1,)), ((), ())), preferred_element_type=F32)


def _dot_tn(a, b):
    return lax.dot_general(a, b, (((0,), (0,)), ((), ())), preferred_element_type=F32)


def _full(shape):
    nd = len(shape)
    return pl.BlockSpec(shape, lambda *_: (0,) * nd)


def _rms_fwd(h, g):
    tp, d = h.shape
    tm = _divisor_tile(tp, 16, 544)

    def body(h_ref, g_ref, z_ref):
        x = h_ref[...]
        z_ref[...] = (x * _rstd(x) * g_ref[...]).astype(BF16)

    return _call(body, (h, g), name="rms_fwd", grid=(tp // tm,),
                 in_specs=[pl.BlockSpec((tm, d), lambda i: (i, 0)), _full((1, d))],
                 out_specs=pl.BlockSpec((tm, d), lambda i: (i, 0)),
                 out_shape=jax.ShapeDtypeStruct((tp, d), BF16), semantics=("parallel",))


def _proj(z, w_big_t, att_w):
    tp, d = z.shape
    nb = w_big_t.shape[0]
    tn = _divisor_tile(nb, LANES, 512)
    assert (3 * att_w) % tn == 0
    n_qkv = 3 * att_w // tn
    scale = 1.0 / math.sqrt(HEAD_DIM)

    def body(z_ref, w_ref, qkv_ref, p_ref):
        j = pl.program_id(0)
        acc = _dot_nt(z_ref[...], w_ref[...])

        @pl.when(j < n_qkv)
        def _():
            col = j * tn + lax.broadcasted_iota(jnp.int32, (1, tn), 1)
            qkv_ref[...] = (acc * jnp.where(col < att_w, scale, 1.0)).astype(BF16)

        @pl.when(j >= n_qkv)
        def _():
            p_ref[...] = acc

    vm = 2 * (_nbytes((tp, d), BF16) + _nbytes((d, tn), BF16) + _nbytes((tp, tn), F32) * 2)
    return _call(body, (z, w_big_t), name="proj", grid=(nb // tn,),
                 in_specs=[_full((tp, d)), pl.BlockSpec((tn, d), lambda j: (j, 0))],
                 out_specs=[pl.BlockSpec((tp, tn), lambda j: (0, jnp.minimum(j, n_qkv - 1))),
                            pl.BlockSpec((tp, tn), lambda j: (0, jnp.maximum(j - n_qkv, 0)))],
                 out_shape=[jax.ShapeDtypeStruct((tp, 3 * att_w), BF16),
                            jax.ShapeDtypeStruct((tp, nb - 3 * att_w), F32)],
                 semantics=("arbitrary",), vmem_bytes=vm)


def _tile_cumsum(x, row, reverse=False):
    for s in (1, 2, 4):
        if reverse:
            x = x + jnp.where(row < SUBLANES - s, pltpu.roll(x, SUBLANES - s, 0), 0.0)
        else:
            x = x + jnp.where(row >= s, pltpu.roll(x, s, 0), 0.0)
    return x


def _fgate_fwd(proj, b_f_pad, nh):
    tp, nb = proj.shape
    fblk = nb // LANES - 1

    def body(f_ref, b_ref, c_ref, ct_ref):
        b = b_ref[...]
        row = lax.broadcasted_iota(jnp.int32, (SUBLANES, LANES), 0)

        def step(i, carry):
            r0 = pl.multiple_of(i * SUBLANES, SUBLANES)
            lf = _log_sigmoid(f_ref[pl.ds(r0, SUBLANES), :] + b)
            x = _tile_cumsum(lf, row) + carry
            c_ref[pl.ds(r0, SUBLANES), :] = x
            return x[SUBLANES - 1:SUBLANES, :]

        lax.fori_loop(0, tp // SUBLANES, step, jnp.zeros((1, LANES), F32))
        ct_ref[...] = c_ref[...].T[:nh, :]

    return _call(body, (proj, b_f_pad), name="fgate_fwd", grid=(1,),
                 in_specs=[pl.BlockSpec((tp, LANES), lambda i: (0, fblk)), _full((1, LANES))],
                 out_specs=[_full((tp, LANES)), _full((nh, tp))],
                 out_shape=[jax.ShapeDtypeStruct((tp, LANES), F32), jax.ShapeDtypeStruct((nh, tp), F32)],
                 semantics=("arbitrary",))


def _fgate_bwd(proj, b_f_pad, dc):
    tp, nb = proj.shape
    fblk = nb // LANES - 1

    def body(f_ref, b_ref, dc_ref, df_ref, db_ref, dc_s):
        b = b_ref[...]
        row = lax.broadcasted_iota(jnp.int32, (SUBLANES, LANES), 0)
        nt = tp // SUBLANES

        def step(i, carry):
            suffix, acc = carry
            r0 = pl.multiple_of((nt - 1 - i) * SUBLANES, SUBLANES)
            dlf = _tile_cumsum(dc_ref[pl.ds(r0, SUBLANES), :], row, reverse=True) + suffix
            df = dlf * _sigmoid(-(f_ref[pl.ds(r0, SUBLANES), :] + b))
            dc_s[pl.ds(r0, SUBLANES), :] = df
            return dlf[0:1, :], acc + df

        _, acc = lax.fori_loop(0, nt, step, (jnp.zeros((1, LANES), F32), jnp.zeros((SUBLANES, LANES), F32)))
        df_ref[...] = dc_s[...].astype(BF16)
        db_ref[...] = jnp.broadcast_to(jnp.sum(acc, axis=0, keepdims=True), (SUBLANES, LANES))

    return _call(body, (proj, b_f_pad, dc), name="fgate_bwd", grid=(1,),
                 in_specs=[pl.BlockSpec((tp, LANES), lambda i: (0, fblk)), _full((1, LANES)), _full((tp, LANES))],
                 out_specs=[_full((tp, LANES)), _full((SUBLANES, LANES))],
                 out_shape=[jax.ShapeDtypeStruct((tp, LANES), BF16),
                            jax.ShapeDtypeStruct((SUBLANES, LANES), F32)],
                 scratch_shapes=[pltpu.VMEM((tp, LANES), F32)], semantics=("arbitrary",))


ATT_BQ = 128


ATT_BUCKET = 3
ATT_HEADS = 4


def _for_bucket(i, nq, fn):
    for lo in range(0, nq, ATT_BUCKET):
        hi = min(lo + ATT_BUCKET, nq)
        spans = ([(0, lo * ATT_BQ, False)] if lo else []) + [(lo * ATT_BQ, hi * ATT_BQ, True)]
        pl.when(jnp.logical_and(i >= lo, i < hi))(functools.partial(fn, spans))


def _head_column(c_blk, h):
    lane = lax.broadcasted_iota(jnp.int32, c_blk.shape, 1)
    return jnp.sum(jnp.where(lane == h, c_blk, 0.0), axis=1, keepdims=True)


def _key_major_logits(kh_ref, ck_ref, hh, q, span, q0):
    k0, k1, needs_mask = span
    t = _dot_nt(kh_ref[hh, k0:k1, :], q) - ck_ref[hh, k0:k1, :]
    if needs_mask:
        keys = k0 + lax.broadcasted_iota(jnp.int32, (k1 - k0, ATT_BQ), 0)
        t = jnp.where(keys <= q0 + lax.broadcasted_iota(jnp.int32, (k1 - k0, ATT_BQ), 1), t, NEG_BIG)
    return t


def _stage_keys(p, k_ref, c_ref, kh_s, ck_s):
    for hh in range(ATT_HEADS):
        kh_s[hh] = k_ref[:, HEAD_DIM * hh:HEAD_DIM * (hh + 1)]
        ck_s[hh] = jnp.broadcast_to(_head_column(c_ref[...], ATT_HEADS * p + hh), ck_s.shape[1:])


def _attn_fwd(qkv, c, c_t, nh):
    tp = qkv.shape[0]
    att_w = nh * HEAD_DIM
    ng = nh // ATT_HEADS
    gw = ATT_HEADS * HEAD_DIM
    bq = ATT_BQ
    nq = tp // bq

    def body(q_ref, k_ref, v_ref, c_ref, ct_ref, o_ref, lse_ref, ck_s, kh_s, vt_s):
        p = pl.program_id(0)
        i = pl.program_id(1)

        @pl.when(i == 0)
        def _():
            _stage_keys(p, k_ref, c_ref, kh_s, ck_s)
            vt_s[...] = v_ref[...].astype(F32).T.astype(BF16)

        def compute(spans):
            q0 = pl.multiple_of(i * bq, bq)
            o_t, lses = [], []
            for hh in range(ATT_HEADS):
                lo = HEAD_DIM * hh
                q = q_ref[:, lo:lo + HEAD_DIM]
                ts = [_key_major_logits(kh_s, ck_s, hh, q, sp, q0) for sp in spans]
                m = functools.reduce(jnp.maximum, [jnp.max(t, axis=0, keepdims=True) for t in ts])
                es = [jnp.exp(t - m) for t in ts]
                l = sum(jnp.sum(e, axis=0, keepdims=True) for e in es)
                o = sum(_dot(vt_s[lo:lo + HEAD_DIM, k0:k1], e.astype(BF16)) for e, (k0, k1, _) in zip(es, spans))
                o_t.append(o / l)
                lses.append(m + ct_ref[pl.ds(ATT_HEADS * p + hh, 1), :] + jnp.log(l))
            o_ref[...] = jnp.concatenate(o_t, axis=0).T
            lse_ref[...] = jnp.concatenate(lses, axis=0)

        _for_bucket(i, nq, compute)

    blk = pl.BlockSpec((bq, gw), lambda p, i: (i, p))
    vm = 6 * _nbytes((tp, gw), BF16) + 2 * ATT_HEADS * _nbytes((tp, LANES), F32) + 2 * _nbytes((tp, LANES), F32) \
        + 8 * ATT_HEADS * _nbytes((bq, tp), F32)
    return _call(body, (qkv, qkv, qkv, c, c_t), name="attn_fwd", grid=(ng, nq),
                 in_specs=[blk,
                           pl.BlockSpec((tp, gw), lambda p, i: (0, ng + p)),
                           pl.BlockSpec((tp, gw), lambda p, i: (0, 2 * ng + p)),
                           _full((tp, LANES)), pl.BlockSpec((nh, bq), lambda p, i: (0, i))],
                 out_specs=[blk, pl.BlockSpec((None, ATT_HEADS, bq), lambda p, i: (p, 0, i))],
                 out_shape=[jax.ShapeDtypeStruct((tp, att_w), F32), jax.ShapeDtypeStruct((ng, ATT_HEADS, tp), F32)],
                 scratch_shapes=[pltpu.VMEM((ATT_HEADS, tp, LANES), F32), pltpu.VMEM((ATT_HEADS, tp, HEAD_DIM), BF16),
                                 pltpu.VMEM((gw, tp), BF16)],
                 semantics=("arbitrary", "arbitrary"), vmem_bytes=vm)


def _attn_bwd(qkv, c, c_t, lse, do, nh):
    tp = qkv.shape[0]
    att_w = nh * HEAD_DIM
    ng = nh // ATT_HEADS
    gw = ATT_HEADS * HEAD_DIM
    bq = ATT_BQ
    nq = tp // bq
    pair = 2 * HEAD_DIM
    assert pair == LANES and ATT_HEADS % 2 == 0
    scale = 1.0 / math.sqrt(HEAD_DIM)

    def body(q_ref, k_ref, v_ref, c_ref, ct_ref, lse_ref, do_ref, dq_ref, dk_ref, dv_ref, dc_ref,
             dk_s, dv_s, dc_s, ck_s, kt_s):
        p = pl.program_id(0)
        i = pl.program_id(1)

        @pl.when(i == 0)
        def _():
            dk_s[...] = jnp.zeros_like(dk_s)
            dv_s[...] = jnp.zeros_like(dv_s)
            dc_s[...] = jnp.zeros_like(dc_s)
            kt_s[...] = k_ref[...].astype(F32).T.astype(BF16)
            for hh in range(ATT_HEADS):
                ck_s[hh] = jnp.broadcast_to(_head_column(c_ref[...], ATT_HEADS * p + hh), (tp, LANES))

        @pl.when(jnp.logical_and(i == 0, p == 0))
        def _():
            dc_ref[...] = jnp.zeros_like(dc_ref)

        def compute(spans):
            q0 = pl.multiple_of(i * bq, bq)
            top = lax.broadcasted_iota(jnp.int32, (pair, bq), 0) < HEAD_DIM
            low = lax.broadcasted_iota(jnp.int32, (bq, pair), 1) < HEAD_DIM

            def diag_cols(x2):
                xt = x2.astype(F32).T.astype(BF16)
                return jnp.concatenate([jnp.where(top, xt, 0), jnp.where(top, 0, xt)], axis=1)

            def diag_rows(x2):
                return jnp.concatenate([jnp.where(low, x2, 0), jnp.where(low, 0, x2)], axis=0)

            dq_t = []
            for pi in range(ATT_HEADS // 2):
                lo = pair * pi
                q2 = q_ref[:, lo:lo + pair]
                do2 = do_ref[:, lo:lo + pair].astype(BF16)
                q_cols, do_cols = diag_cols(q2), diag_cols(do2)
                q_rows, do_rows = diag_rows(q2), diag_rows(do2)
                heads = (2 * pi, 2 * pi + 1)
                col_terms = [ct_ref[pl.ds(ATT_HEADS * p + hh, 1), :] - lse_ref[hh:hh + 1, :] for hh in heads]
                prs, dps = [], []
                for k0, k1, needs_mask in spans:
                    t2 = _dot(k_ref[k0:k1, lo:lo + pair], q_cols)
                    dp2 = _dot(v_ref[k0:k1, lo:lo + pair], do_cols)
                    if needs_mask:
                        keys = k0 + lax.broadcasted_iota(jnp.int32, (k1 - k0, bq), 0)
                        seen = keys <= q0 + lax.broadcasted_iota(jnp.int32, (k1 - k0, bq), 1)
                    pr_e, dp_e = [], []
                    for e, hh in enumerate(heads):
                        t = t2[:, e * bq:(e + 1) * bq] - ck_s[hh, k0:k1, :]
                        if needs_mask:
                            t = jnp.where(seen, t, NEG_BIG)
                        pr_e.append(jnp.exp(t + col_terms[e]))
                        dp_e.append(dp2[:, e * bq:(e + 1) * bq])
                    prs.append(pr_e)
                    dps.append(dp_e)
                key_sums = [sum(jnp.sum(pr[e] * dp[e], axis=0, keepdims=True) for pr, dp in zip(prs, dps))
                            for e in range(2)]
                dq2 = 0.0
                for (k0, k1, _), pr, dp in zip(spans, prs, dps):
                    ds = [pr[e] * (dp[e] - key_sums[e]) for e in range(2)]
                    for e, hh in enumerate(heads):
                        dc_s[hh, k0:k1, :] += jnp.sum(ds[e], axis=1, keepdims=True)
                    ds2 = jnp.concatenate([ds[0].astype(BF16), ds[1].astype(BF16)], axis=1)
                    pr2 = jnp.concatenate([pr[0].astype(BF16), pr[1].astype(BF16)], axis=1)
                    dk_s[k0:k1, lo:lo + pair] += _dot(ds2, q_rows)
                    dv_s[k0:k1, lo:lo + pair] += _dot(pr2, do_rows)
                    dq2 = dq2 + _dot(kt_s[lo:lo + pair, k0:k1], ds2)
                dq_t.append(jnp.concatenate([dq2[:HEAD_DIM, :bq], dq2[HEAD_DIM:, bq:]], axis=0))
            dq_ref[...] = (jnp.concatenate(dq_t, axis=0) * scale).T.astype(BF16)

        _for_bucket(i, nq, compute)

        @pl.when(i == nq - 1)
        def _():
            dk_ref[...] = dk_s[...].astype(BF16)
            dv_ref[...] = dv_s[...].astype(BF16)
            lane = lax.broadcasted_iota(jnp.int32, (tp, LANES), 1)
            dc = dc_ref[...]
            for hh in range(ATT_HEADS):
                dc = jnp.where(lane == ATT_HEADS * p + hh, -dc_s[hh], dc)
            dc_ref[...] = dc

    blk = pl.BlockSpec((bq, gw), lambda p, i: (i, p))
    col = pl.BlockSpec((tp, gw), lambda p, i: (0, p))
    vm = 7 * _nbytes((tp, gw), BF16) + 2 * _nbytes((tp, gw), F32) + 2 * ATT_HEADS * _nbytes((tp, LANES), F32) \
        + 2 * _nbytes((tp, LANES), F32) + 12 * ATT_HEADS * _nbytes((bq, tp), F32)
    return _call(body, (qkv, qkv, qkv, c, c_t, lse, do), name="attn_bwd", grid=(ng, nq),
                 in_specs=[blk,
                           pl.BlockSpec((tp, gw), lambda p, i: (0, ng + p)),
                           pl.BlockSpec((tp, gw), lambda p, i: (0, 2 * ng + p)),
                           _full((tp, LANES)), pl.BlockSpec((nh, bq), lambda p, i: (0, i)),
                           pl.BlockSpec((None, ATT_HEADS, bq), lambda p, i: (p, 0, i)), blk],
                 out_specs=[blk, col, col, _full((tp, LANES))],
                 out_shape=[jax.ShapeDtypeStruct((tp, att_w), BF16)] * 3 + [jax.ShapeDtypeStruct((tp, LANES), F32)],
                 scratch_shapes=[pltpu.VMEM((tp, gw), F32), pltpu.VMEM((tp, gw), F32),
                                 pltpu.VMEM((ATT_HEADS, tp, 1), F32), pltpu.VMEM((ATT_HEADS, tp, LANES), F32),
                                 pltpu.VMEM((gw, tp), BF16)],
                 semantics=("arbitrary", "arbitrary"), vmem_bytes=vm)


REC_ROWS = 128
HALO = SUBLANES


def _conv_taps(cat):
    taps = []
    for k in range(CONV_WIDTH):
        sh = CONV_WIDTH - 1 - k
        taps.append((pltpu.roll(cat, sh, 0) if sh else cat)[HALO:])
    return taps


def _rec_gates(xc, wa_ref, ba_ref, wx_ref, bx_ref, l_ref):
    xcb = xc.astype(BF16)
    r = _sigmoid(_dot(xcb, wa_ref[...]) + ba_ref[...])
    ig = _sigmoid(_dot(xcb, wx_ref[...]) + bx_ref[...])
    ls = _log_sigmoid(l_ref[...])
    log_a = RG_C * r * ls
    return xcb, r, ig, ls, log_a


def _rec_fwd(proj, xr_blk, yr_blk, rec_w, conv_w, conv_b, wa, ba, wx, bx, lru):
    tp = proj.shape[0]
    w = rec_w
    r_rows = REC_ROWS
    nc = tp // r_rows
    cpb = w // LANES

    def body(xr_ref, yr_ref, cw_ref, cb_ref, wa_ref, ba_ref, wx_ref, bx_ref, l_ref,
             hr_ref, rec_ref, prev_s, carry_s, a_s, u_s):
        i = pl.program_id(0)

        @pl.when(i == 0)
        def _():
            prev_s[...] = jnp.zeros_like(prev_s)
            carry_s[...] = jnp.zeros_like(carry_s)

        x = xr_ref[...]
        taps = _conv_taps(jnp.concatenate([prev_s[...], x], axis=0))
        prev_s[...] = x[r_rows - HALO:]
        xc = cb_ref[...]
        for k in range(CONV_WIDTH):
            xc = xc + cw_ref[k:k + 1, :] * taps[k]
        _, r, ig, ls, log_a = _rec_gates(xc, wa_ref, ba_ref, wx_ref, bx_ref, l_ref)
        a = jnp.exp(log_a)
        a_s[...] = a
        u_s[...] = jnp.sqrt(_one_minus_exp(2.0 * log_a, a * a)) * ig * xc

        def tile(j, h):
            r0 = pl.multiple_of(j * SUBLANES, SUBLANES)
            at = a_s[pl.ds(r0, SUBLANES), :]
            ut = u_s[pl.ds(r0, SUBLANES), :]
            out = []
            for rr in range(SUBLANES):
                h = at[rr:rr + 1] * h + ut[rr:rr + 1]
                out.append(h)
            hr_ref[pl.ds(r0, SUBLANES), :] = jnp.concatenate(out, axis=0)
            return h

        carry_s[0:1, :] = lax.fori_loop(0, r_rows // SUBLANES, tile, carry_s[0:1, :])
        g, _ = _gelu_and_grad(yr_ref[...])
        rec_ref[...] = hr_ref[...] * g

    blk = pl.BlockSpec((r_rows, w), lambda i: (i, 0))
    vm = 16 * _nbytes((r_rows, w), F32) + 4 * _nbytes((w, w), BF16)
    return _call(body, (proj, proj, conv_w, conv_b, wa, ba, wx, bx, lru), name="rec_fwd", grid=(nc,),
                 in_specs=[pl.BlockSpec((r_rows, w), lambda i: (i, xr_blk)),
                           pl.BlockSpec((r_rows, w), lambda i: (i, yr_blk)),
                           _full((CONV_WIDTH, w)), _full((1, w)), _full((w, w)), _full((1, w)),
                           _full((w, w)), _full((1, w)), _full((1, w))],
                 out_specs=[blk, blk],
                 out_shape=[jax.ShapeDtypeStruct((tp, w), F32)] * 2,
                 scratch_shapes=[pltpu.VMEM((HALO, w), F32), pltpu.VMEM((SUBLANES, w), F32),
                                 pltpu.VMEM((r_rows, w), F32), pltpu.VMEM((r_rows, w), F32)],
                 semantics=("arbitrary",), vmem_bytes=vm)


def _rec_bwd(proj, xr_blk, yr_blk, rec_w, hr, drec, conv_w, conv_b, wa, ba, wx, bx, lru):
    tp = proj.shape[0]
    w = rec_w
    r_rows = REC_ROWS
    nc = tp // r_rows
    hpc = r_rows // HALO

    def body(xr_ref, xh_ref, yr_ref, hr_ref, hh_ref, drec_ref, cw_ref, cb_ref, wa_ref, ba_ref, wx_ref, bx_ref,
             l_ref, dxr_ref, dyr_ref, dwa_ref, dwx_ref, small_ref, lam_s, a_s, dhr_s, carry_s, next_s):
        i = pl.program_id(0)
        first = (nc - 1 - i) == 0

        @pl.when(i == 0)
        def _():
            carry_s[...] = jnp.zeros_like(carry_s)
            next_s[...] = jnp.zeros_like(next_s)
            dwa_ref[...] = jnp.zeros_like(dwa_ref)
            dwx_ref[...] = jnp.zeros_like(dwx_ref)
            small_ref[...] = jnp.zeros_like(small_ref)

        x = xr_ref[...]
        xprev = jnp.where(first, 0.0, xh_ref[...])
        taps = _conv_taps(jnp.concatenate([xprev, x], axis=0))
        xc = cb_ref[...]
        for k in range(CONV_WIDTH):
            xc = xc + cw_ref[k:k + 1, :] * taps[k]
        xcb, r, ig, ls, log_a = _rec_gates(xc, wa_ref, ba_ref, wx_ref, bx_ref, l_ref)
        a = jnp.exp(log_a)
        a2 = a * a
        mult = jnp.sqrt(_one_minus_exp(2.0 * log_a, a2))
        g, dg = _gelu_and_grad(yr_ref[...])
        hr_v = hr_ref[...]
        drec_v = drec_ref[...]
        dhr_s[...] = drec_v * g
        dyr_ref[...] = (drec_v * hr_v * dg).astype(BF16)
        a_s[...] = a

        def tile(jj, carry):
            r0 = pl.multiple_of((r_rows // SUBLANES - 1 - jj) * SUBLANES, SUBLANES)
            at = a_s[pl.ds(r0, SUBLANES), :]
            dt = dhr_s[pl.ds(r0, SUBLANES), :]
            out = [None] * SUBLANES
            for rr in range(SUBLANES - 1, -1, -1):
                lam = dt[rr:rr + 1] + carry
                out[rr] = lam
                carry = at[rr:rr + 1] * lam
            lam_s[pl.ds(r0, SUBLANES), :] = jnp.concatenate(out, axis=0)
            return carry

        carry_s[0:1, :] = lax.fori_loop(0, r_rows // SUBLANES, tile, carry_s[0:1, :])
        lam = lam_s[...]
        hprev = jnp.where(first, 0.0, hh_ref[...])
        hr_prev = pltpu.roll(jnp.concatenate([hprev, hr_v], axis=0), 1, 0)[HALO:]
        da = lam * hr_prev
        dxc = lam * mult * ig
        di = lam * mult * xc
        dmult = lam * ig * xc
        dlog_a = da * a - dmult * a2 / mult
        dr = dlog_a * (RG_C * ls)
        dls = jnp.sum(dlog_a * (RG_C * r), axis=0, keepdims=True)
        dga = dr * r * (1.0 - r)
        dgx = di * ig * (1.0 - ig)
        dgab = dga.astype(BF16)
        dgxb = dgx.astype(BF16)
        dxc = dxc + _dot_nt(dgab, wa_ref[...]) + _dot_nt(dgxb, wx_ref[...])
        dwa_ref[...] += _dot_tn(xcb, dgab)
        dwx_ref[...] += _dot_tn(xcb, dgxb)
        cat = jnp.concatenate([dxc, next_s[...]], axis=0)
        next_s[...] = dxc[0:HALO]
        dxr = cw_ref[CONV_WIDTH - 1:CONV_WIDTH, :] * dxc
        for k in range(CONV_WIDTH - 1):
            sh = CONV_WIDTH - 1 - k
            dxr = dxr + cw_ref[k:k + 1, :] * pltpu.roll(cat, r_rows + HALO - sh, 0)[:r_rows]
        dxr_ref[...] = dxr.astype(BF16)
        rows = [jnp.sum(dxc * taps[k], axis=0, keepdims=True) for k in range(CONV_WIDTH)]
        rows += [jnp.sum(dxc, axis=0, keepdims=True), jnp.sum(dga, axis=0, keepdims=True),
                 jnp.sum(dgx, axis=0, keepdims=True), dls * _sigmoid(-l_ref[...])]
        small_ref[...] += jnp.concatenate(rows, axis=0)

    def rev(i):
        return nc - 1 - i

    def halo(i):
        return jnp.maximum(rev(i) * hpc - 1, 0)

    blk = pl.BlockSpec((r_rows, w), lambda i: (rev(i), 0))
    vm = 40 * _nbytes((r_rows, w), F32) + 6 * _nbytes((w, w), F32)
    return _call(body, (proj, proj, proj, hr, hr, drec, conv_w, conv_b, wa, ba, wx, bx, lru),
                 name="rec_bwd", grid=(nc,),
                 in_specs=[pl.BlockSpec((r_rows, w), lambda i: (rev(i), xr_blk)),
                           pl.BlockSpec((HALO, w), lambda i: (halo(i), xr_blk)),
                           pl.BlockSpec((r_rows, w), lambda i: (rev(i), yr_blk)),
                           blk,
                           pl.BlockSpec((HALO, w), lambda i: (halo(i), 0)),
                           blk,
                           _full((CONV_WIDTH, w)), _full((1, w)), _full((w, w)), _full((1, w)),
                           _full((w, w)), _full((1, w)), _full((1, w))],
                 out_specs=[blk, blk, _full((w, w)), _full((w, w)), _full((SUBLANES, w))],
                 out_shape=[jax.ShapeDtypeStruct((tp, w), BF16)] * 2
                 + [jax.ShapeDtypeStruct((w, w), F32)] * 2 + [jax.ShapeDtypeStruct((SUBLANES, w), F32)],
                 scratch_shapes=[pltpu.VMEM((r_rows, w), F32)] * 3
                 + [pltpu.VMEM((SUBLANES, w), F32), pltpu.VMEM((HALO, w), F32)],
                 semantics=("arbitrary",), vmem_bytes=vm)


ROW_TARGET = 544


def _mixer_out(attn, rec, g_a, g_r, w_out, h, g_next):
    tp, d = h.shape
    aw, rw = attn.shape[1], rec.shape[1]
    kc = d // N_CHIPS
    tm = _divisor_tile(tp, 16, ROW_TARGET)

    def body(a_ref, r_ref, ga_ref, gr_ref, w_ref, h_ref, gn_ref, h1_ref, z_ref, mix_ref):
        a = a_ref[...]
        r = r_ref[...]
        mix = jnp.concatenate([a * _rstd(a) * ga_ref[...], r * _rstd(r) * gr_ref[...]], axis=1).astype(BF16)
        mix_ref[...] = mix
        h1 = h_ref[...]
        for j in range(N_CHIPS):
            h1 = h1 + _dot(mix[:, j * kc:(j + 1) * kc], w_ref[j])
        h1_ref[...] = h1
        z_ref[...] = (h1 * _rstd(h1) * gn_ref[...]).astype(BF16)

    row = lambda wd: pl.BlockSpec((tm, wd), lambda i: (i, 0))
    vm = 2 * _nbytes((d, d), BF16) + 12 * _nbytes((tm, d), F32)
    return _call(body, (attn, rec, g_a, g_r, w_out, h, g_next), name="mixer_out", grid=(tp // tm,),
                 in_specs=[row(aw), row(rw), _full((1, aw)), _full((1, rw)), _full(w_out.shape), row(d),
                           _full((1, d))],
                 out_specs=[row(d), row(d), row(d)],
                 out_shape=[jax.ShapeDtypeStruct((tp, d), F32), jax.ShapeDtypeStruct((tp, d), BF16),
                            jax.ShapeDtypeStruct((tp, d), BF16)],
                 semantics=("parallel",), vmem_bytes=vm)


def _mixer_bwd(dh_b, w_out, attn, rec, g_a, g_r):
    tp, d = dh_b.shape
    aw, rw = attn.shape[1], rec.shape[1]
    tm = _divisor_tile(tp, 16, ROW_TARGET)

    def body(dh_ref, w_ref, a_ref, r_ref, ga_ref, gr_ref, da_ref, dr_ref, dg_ref):
        @pl.when(pl.program_id(0) == 0)
        def _():
            dg_ref[...] = jnp.zeros_like(dg_ref)

        dh = dh_ref[...]
        dmix = jnp.concatenate([_dot_nt(dh, w_ref[j]) for j in range(N_CHIPS)], axis=1)
        da, dga = _rms_bwd(dmix[:, :aw], a_ref[...], ga_ref[...])
        dr, dgr = _rms_bwd(dmix[:, aw:], r_ref[...], gr_ref[...])
        da_ref[...] = da
        dr_ref[...] = dr
        dg_ref[...] += jnp.broadcast_to(jnp.concatenate([dga, dgr], axis=1), (SUBLANES, d))

    row = lambda wd: pl.BlockSpec((tm, wd), lambda i: (i, 0))
    vm = 2 * _nbytes((d, d), BF16) + 12 * _nbytes((tm, d), F32)
    return _call(body, (dh_b, w_out, attn, rec, g_a, g_r), name="mixer_bwd", grid=(tp // tm,),
                 in_specs=[row(d), _full(w_out.shape), row(aw), row(rw), _full((1, aw)), _full((1, rw))],
                 out_specs=[row(aw), row(rw), _full((SUBLANES, d))],
                 out_shape=[jax.ShapeDtypeStruct((tp, aw), F32), jax.ShapeDtypeStruct((tp, rw), F32),
                            jax.ShapeDtypeStruct((SUBLANES, d), F32)],
                 semantics=("arbitrary",), vmem_bytes=vm)


def _mlp_up(z, w_up):
    tp, d = z.shape
    fc = w_up.shape[2]
    ff = N_CHIPS * fc
    tn = _divisor_tile(fc, LANES, 512)
    per = fc // tn

    def body(z_ref, w_ref, act_ref, up_ref):
        up = _dot(z_ref[...], w_ref[...])
        r = jnp.maximum(up, 0.0)
        act_ref[...] = (r * r).astype(BF16)
        up_ref[...] = up.astype(BF16)

    col = pl.BlockSpec((tp, tn), lambda j: (0, j))
    vm = 2 * _nbytes((tp, d), BF16) + 2 * _nbytes((d, tn), BF16) + 8 * _nbytes((tp, tn), F32)
    return _call(body, (z, w_up), name="mlp_up", grid=(ff // tn,),
                 in_specs=[_full((tp, d)), pl.BlockSpec((None, d, tn), lambda j: (j // per, 0, j % per))],
                 out_specs=[col, col],
                 out_shape=[jax.ShapeDtypeStruct((tp, ff), BF16)] * 2,
                 semantics=("parallel",), vmem_bytes=vm)


def _mlp_down(act, w_down, h, g_next):
    tp, d = h.shape
    ff = act.shape[1]
    fc = ff // N_CHIPS
    tm = _divisor_tile(tp, 16, ROW_TARGET)

    def body(a_ref, w_ref, h_ref, gn_ref, h2_ref, z_ref):
        h2 = h_ref[...]
        for j in range(N_CHIPS):
            h2 = h2 + _dot(a_ref[:, j * fc:(j + 1) * fc], w_ref[j])
        h2_ref[...] = h2
        z_ref[...] = (h2 * _rstd(h2) * gn_ref[...]).astype(BF16)

    row = lambda wd: pl.BlockSpec((tm, wd), lambda i: (i, 0))
    vm = 2 * _nbytes((ff, d), BF16) + 2 * _nbytes((tm, ff), BF16) + 10 * _nbytes((tm, d), F32)
    return _call(body, (act, w_down, h, g_next), name="mlp_down", grid=(tp // tm,),
                 in_specs=[row(ff), _full(w_down.shape), row(d), _full((1, d))],
                 out_specs=[row(d), row(d)],
                 out_shape=[jax.ShapeDtypeStruct((tp, d), F32), jax.ShapeDtypeStruct((tp, d), BF16)],
                 semantics=("parallel",), vmem_bytes=vm)


def _loss_bwd(h, g, target, n_real):
    tp, d = h.shape
    tm = _divisor_tile(tp, 16, ROW_TARGET)

    def body(h_ref, g_ref, t_ref, dh_ref, dhb_ref, dg_ref, loss_ref):
        i = pl.program_id(0)

        @pl.when(i == 0)
        def _():
            dg_ref[...] = jnp.zeros_like(dg_ref)
            loss_ref[...] = jnp.zeros_like(loss_ref)

        x = h_ref[...]
        gv = g_ref[...]
        rowi = i * tm + lax.broadcasted_iota(jnp.int32, (tm, 1), 0)
        real = jnp.logical_and(rowi >= N_META, rowi < N_META + n_real)
        err = jnp.where(real, x * _rstd(x) * gv - t_ref[...], 0.0)
        loss_ref[...] += 0.5 * jnp.sum(jnp.mean(err * err, axis=-1, keepdims=True))
        dx, dgp = _rms_bwd(err * (1.0 / d), x, gv)
        dh_ref[...] = dx
        dhb_ref[...] = dx.astype(BF16)
        dg_ref[...] += jnp.broadcast_to(dgp, (SUBLANES, d))

    row = pl.BlockSpec((tm, d), lambda i: (i, 0))
    return _call(body, (h, g, target), name="loss_bwd", grid=(tp // tm,),
                 in_specs=[row, _full((1, d)), row],
                 out_specs=[row, row, _full((SUBLANES, d)), _full((SUBLANES, LANES))],
                 out_shape=[jax.ShapeDtypeStruct((tp, d), F32), jax.ShapeDtypeStruct((tp, d), BF16),
                            jax.ShapeDtypeStruct((SUBLANES, d), F32), jax.ShapeDtypeStruct((SUBLANES, LANES), F32)],
                 semantics=("arbitrary",), vmem_bytes=16 * _nbytes((tm, d), F32))


def _mlp_bwd(dh_b, w_down, up, z2):
    tp, d = dh_b.shape
    fc = w_down.shape[1]
    ff = N_CHIPS * fc
    tn = _divisor_tile(fc, LANES, 512)
    per = fc // tn

    def body(dh_ref, z_ref, w_ref, up_ref, dup_ref, gd_ref, gu_ref):
        dh = dh_ref[...]
        r = jnp.maximum(up_ref[...].astype(F32), 0.0)
        dup = (_dot_nt(dh, w_ref[...]) * (2.0 * r)).astype(BF16)
        dup_ref[...] = dup
        gd_ref[...] = _dot_tn((r * r).astype(BF16), dh).astype(BF16)
        gu_ref[...] = _dot_tn(z_ref[...], dup).astype(BF16)

    col = pl.BlockSpec((tp, tn), lambda j: (0, j))
    vm = 4 * _nbytes((tp, d), BF16) + 4 * _nbytes((tn, d), BF16) + 2 * _nbytes((d, tn), BF16) \
        + 10 * _nbytes((tp, tn), F32) + 4 * _nbytes((tn, d), F32)
    return _call(body, (dh_b, z2, w_down, up), name="mlp_bwd", grid=(ff // tn,),
                 in_specs=[_full((tp, d)), _full((tp, d)),
                           pl.BlockSpec((None, tn, d), lambda j: (j // per, j % per, 0)), col],
                 out_specs=[col, pl.BlockSpec((tn, d), lambda j: (j, 0)),
                            pl.BlockSpec((None, d, tn), lambda j: (j // per, 0, j % per))],
                 out_shape=[jax.ShapeDtypeStruct((tp, ff), BF16), jax.ShapeDtypeStruct((ff, d), BF16),
                            jax.ShapeDtypeStruct((N_CHIPS, d, fc), BF16)],
                 semantics=("parallel",), vmem_bytes=vm)


def _grad_w_pieces(pieces, b):
    tp, n = b.shape
    tn = _divisor_tile(n, LANES, 512)
    widths = [pc.shape[1] for pc in pieces]

    def body(*refs):
        p_refs, b_ref, o_refs = refs[:len(pieces)], refs[len(pieces)], refs[len(pieces) + 1:]
        for p_ref, o_ref in zip(p_refs, o_refs):
            o_ref[...] = _dot_tn(p_ref[...], b_ref[...]).astype(BF16)

    vm = 2 * sum(_nbytes((tp, wd), BF16) for wd in widths) + 2 * _nbytes((tp, tn), BF16) \
        + 4 * sum(_nbytes((wd, tn), F32) for wd in widths) + 2 * _nbytes((tp, max(widths)), F32)
    return _call(body, tuple(pieces) + (b,), name="grad_w_pieces", grid=(n // tn,),
                 in_specs=[_full(pc.shape) for pc in pieces] + [pl.BlockSpec((tp, tn), lambda j: (0, j))],
                 out_specs=[pl.BlockSpec((wd, tn), lambda j: (0, j)) for wd in widths],
                 out_shape=[jax.ShapeDtypeStruct((wd, n), BF16) for wd in widths],
                 semantics=("parallel",), vmem_bytes=vm)


def _dx_norm_bwd(pieces, w, w_spec, w_piece, h, g, dres, dot=_dot_nt):
    tp, d = h.shape
    tm = _divisor_tile(tp, 16, ROW_TARGET)
    n = len(pieces)

    def body(*refs):
        dy_refs = refs[:n]
        w_ref, h_ref, g_ref, dres_ref, dh_ref, dhb_ref, dg_ref = refs[n:]

        @pl.when(pl.program_id(0) == 0)
        def _():
            dg_ref[...] = jnp.zeros_like(dg_ref)

        dz = dot(dy_refs[0][...], w_piece(w_ref, 0))
        for i in range(1, n):
            dz = dz + dot(dy_refs[i][...], w_piece(w_ref, i))
        dx, dgp = _rms_bwd(dz, h_ref[...], g_ref[...])
        dh = dres_ref[...] + dx
        dh_ref[...] = dh
        dhb_ref[...] = dh.astype(BF16)
        dg_ref[...] += jnp.broadcast_to(dgp, (SUBLANES, d))

    row = lambda wd: pl.BlockSpec((tm, wd), lambda i: (i, 0))
    kk = sum(wd for _, _, wd in pieces)
    vm = 2 * _nbytes((d, kk), BF16) + 2 * _nbytes((tm, kk), BF16) + 14 * _nbytes((tm, d), F32)
    piece_specs = [pl.BlockSpec((tm, wd), functools.partial(lambda i, cb: (i, cb), cb=cb)) for _, cb, wd in pieces]
    return _call(body, tuple(a for a, _, _ in pieces) + (w, h, g, dres), name="dx_norm_bwd", grid=(tp // tm,),
                 in_specs=piece_specs + [w_spec, row(d), _full((1, d)), row(d)],
                 out_specs=[row(d), row(d), _full((SUBLANES, d))],
                 out_shape=[jax.ShapeDtypeStruct((tp, d), F32), jax.ShapeDtypeStruct((tp, d), BF16),
                            jax.ShapeDtypeStruct((SUBLANES, d), F32)],
                 semantics=("arbitrary",), vmem_bytes=vm)


def _block_diag(wg):
    nb, b, _ = wg.shape
    eye = jnp.eye(nb, dtype=wg.dtype)
    return (eye[:, None, :, None] * wg[:, :, None, :]).reshape(nb * b, nb * b)


def _diag_blocks(dense, nb):
    b = dense.shape[0] // nb
    d4 = dense.reshape(nb, b, nb, b)
    return jnp.stack([d4[i, :, i, :] for i in range(nb)])


def _row(v):
    return v.reshape(1, -1)


def _forward_layer(l, h, z, p, fetch, stage_next, g_next):
    d = h.shape[1]
    att_w = d // 2
    rec_w = d - att_w
    nh = att_w // HEAD_DIM
    wa_d = _block_diag(p["w_gate_a"][l]).astype(BF16)
    wx_d = _block_diag(p["w_gate_x"][l]).astype(BF16)
    b_f_pad = jnp.zeros((1, LANES), F32).at[0, :nh].set(p["b_f"][l])
    w_in_t = fetch("w_in", z)
    big = dict(w_in_big=_pack_w_in_t(w_in_t.reshape(-1, d), att_w, nh))
    qkv, proj = _proj(z, big["w_in_big"], att_w)
    c, c_t = _fgate_fwd(proj, b_f_pad, nh)
    attn, lse_b = _attn_fwd(qkv, c, c_t, nh)
    hr, rec = _rec_fwd(proj, 0, 1, rec_w, p["conv_w"][l], _row(p["conv_b"][l]), wa_d,
                       _row(p["b_gate_a"][l]), wx_d, _row(p["b_gate_x"][l]), _row(p["lru_L"][l]))
    tok = stage_next(attn)
    big["w_out"] = fetch("w_out", rec)
    h1, z2, mix = _mixer_out(attn, rec, _row(p["attn_out_g"][l] + tok), _row(p["rec_out_g"][l]),
                             big["w_out"], h, _row(p["mlp_norm_g"][l]))
    big["w_up"] = fetch("w_up", h1)
    act, up = _mlp_up(z2, big["w_up"])
    big["w_down"] = fetch("w_down", act)
    h2, z_next = _mlp_down(act, big["w_down"], h1, _row(g_next))
    saved = dict(h0=h, z1=z, proj=proj, qkv=qkv, c=c, c_t=c_t, attn=attn, lse_b=lse_b, hr=hr, rec=rec, h1=h1,
                 z2=z2, mix=mix, up=up, wa_d=wa_d, wx_d=wx_d, b_f_pad=b_f_pad, big=big)
    return h2, z_next, saved


def _backward_mlp(l, dh, dh_b, sv, p, tok):
    w_up, w_down = sv["big"]["w_up"], sv["big"]["w_down"]
    fc = w_up.shape[2]
    dup, g_down, g_up = _mlp_bwd(dh_b, w_down, sv["up"], sv["z2"])
    dh, dh_b, dg2 = _dx_norm_bwd([(dup, j, fc) for j in range(N_CHIPS)], w_up, _full(w_up.shape),
                                 lambda w_ref, j: w_ref[j], sv["h1"], _row(p["mlp_norm_g"][l] + tok), dh)
    big = dict(w_down=g_down.reshape((N_CHIPS, -1) + g_down.shape[1:]), w_up=g_up)
    return dh, dh_b, big, dict(mlp_norm_g=dg2[0])


def _backward_mixer(l, dh, dh_b, sv, p, tok):
    d = dh.shape[1]
    att_w = d // 2
    rec_w = d - att_w
    nh = att_w // HEAD_DIM
    small = {}
    g_out, = _grad_w_pieces([sv["mix"]], dh_b)
    dattn, drec, dg_mix = _mixer_bwd(dh_b, sv["big"]["w_out"], sv["attn"], sv["rec"],
                                     _row(p["attn_out_g"][l] + tok), _row(p["rec_out_g"][l]))
    small["attn_out_g"] = dg_mix[0, :att_w]
    small["rec_out_g"] = dg_mix[0, att_w:]
    dxr, dyr, dwa, dwx, sm = _rec_bwd(
        sv["proj"], 0, 1, rec_w, sv["hr"], drec, p["conv_w"][l], _row(p["conv_b"][l]), sv["wa_d"],
        _row(p["b_gate_a"][l]), sv["wx_d"], _row(p["b_gate_x"][l]), _row(p["lru_L"][l]))
    small.update(conv_w=sm[:CONV_WIDTH], conv_b=sm[4], b_gate_a=sm[5], b_gate_x=sm[6], lru_L=sm[7],
                 w_gate_a=_diag_blocks(dwa, N_REC_BLOCKS), w_gate_x=_diag_blocks(dwx, N_REC_BLOCKS))
    dq, dk, dv, dc = _attn_bwd(sv["qkv"], sv["c"], sv["c_t"], sv["lse_b"], dattn, nh)
    df, db_f = _fgate_bwd(sv["proj"], sv["b_f_pad"], dc)
    small["b_f"] = db_f[0, :nh]
    pieces = [dq, dk, dv, dxr, dyr, df]
    offs = [0, att_w, 2 * att_w, 3 * att_w, 3 * att_w + rec_w, 3 * att_w + 2 * rec_w]
    gq, gk, gv, gxr, gyr, gf = _grad_w_pieces(pieces, sv["z1"])
    g_in_t = jnp.concatenate([gq, gk, gv, gf[:nh], gxr, gyr], axis=0)
    w_big = sv["big"]["w_in_big"]
    widths = [pc.shape[1] for pc in pieces]
    dh, dh_b, dg1 = _dx_norm_bwd(
        [(pc, 0, wd) for pc, wd in zip(pieces, widths)], w_big, _full(w_big.shape),
        lambda w_ref, i: w_ref[offs[i]:offs[i] + widths[i], :], sv["h0"], _row(p["attn_norm_g"][l]), dh, dot=_dot)
    small["attn_norm_g"] = dg1[0]
    big = dict(w_in=g_in_t.reshape(N_CHIPS, -1, d), w_out=g_out.reshape((N_CHIPS, -1) + g_out.shape[1:]))
    return dh, dh_b, big, small


def _pack_w_in_t(w_in_t, att_w, nh):
    qkv = w_in_t[:3 * att_w]
    f = w_in_t[3 * att_w:3 * att_w + nh]
    xy = w_in_t[3 * att_w + nh:]
    return jnp.concatenate([qkv, xy, f, jnp.zeros((LANES - nh, w_in_t.shape[1]), w_in_t.dtype)], axis=0)


ANY = pl.BlockSpec(memory_space=pl.ANY)


def _coords():
    return lax.axis_index("x"), lax.axis_index("y"), lax.axis_index("c")


def _other_chips(x, y):
    return [(1 - x, y), (x, 1 - y), (1 - x, 1 - y)]


def _remote(src, dst, send_sems, recv_sems, k, to):
    return pltpu.make_async_remote_copy(src_ref=src, dst_ref=dst, send_sem=send_sems.at[k],
                                        recv_sem=recv_sems.at[k], device_id=to, device_id_type=MESH)


def _all_gather_chips(shards):
    n = len(shards)
    per = 6

    def body(*refs):
        ins, outs = refs[:n], refs[n:2 * n]
        send_sems, recv_sems, local_sems = refs[2 * n:]
        x, y, c = _coords()
        me = 2 * x + y
        sibling = (x, y, 1 - c)
        chips = _other_chips(x, y)
        local = [pltpu.make_async_copy(ins[t], outs[t].at[me], local_sems.at[t]) for t in range(n)]
        for cp in local:
            cp.start()
        sends = []
        for t in range(n):
            for j, (px, py) in enumerate(chips):
                cp = _remote(ins[t].at[c], outs[t].at[me, c], send_sems, recv_sems, per * t + j, (px, py, c))
                cp.start()
                sends.append(cp)
        for t in range(n):
            for j, (px, py) in enumerate(chips):
                landed = outs[t].at[2 * px + py, c]
                _remote(landed, landed, send_sems, recv_sems, per * t + j, (px, py, c)).wait_recv()
                cp = _remote(landed, landed, send_sems, recv_sems, per * t + 3 + j, sibling)
                cp.start()
                sends.append(cp)
        for t in range(n):
            for j, (px, py) in enumerate(chips):
                passed = outs[t].at[2 * px + py, 1 - c]
                _remote(passed, passed, send_sems, recv_sems, per * t + 3 + j, sibling).wait_recv()
        for cp in sends:
            cp.wait_send()
        for cp in local:
            cp.wait()

    return _call(body, tuple(shards), name="all_gather_chips",
                 in_specs=[ANY] * n, out_specs=[ANY] * n,
                 out_shape=[jax.ShapeDtypeStruct((N_CHIPS,) + s.shape, s.dtype) for s in shards],
                 scratch_shapes=[pltpu.SemaphoreType.DMA((per * n,)), pltpu.SemaphoreType.DMA((per * n,)),
                                 pltpu.SemaphoreType.DMA((n,))])


HBM = pl.BlockSpec(memory_space=pltpu.HBM)
SEM = pl.BlockSpec(memory_space=pltpu.SEMAPHORE)
DATAFLOW = pltpu.SideEffectType.DATAFLOW_SIDE_EFFECTING


def _in_hbm(a):
    return pltpu.with_memory_space_constraint(a, pltpu.HBM)


PUSH_ARRIVALS = {"gather_chips_half": N_CHIPS - 1, "pass_halves": N_CHIPS - 1, "scatter_chips": N_CHIPS - 1,
                 "sibling": 1, "gather_devices": N_DEV - 1}


def _column_half(ref3, slab, c):
    hw = ref3.shape[2] // 2
    return ref3.at[slab, :, pl.ds(pl.multiple_of(c * hw, LANES), hw)]


def _push_copies(mode, src, land, send_sems, recv_sems, t):
    x, y, c = _coords()
    chip = 2 * x + y
    if mode == "gather_chips_half":
        return [_remote(_column_half(src, chip, c), _column_half(land, chip, c), send_sems, recv_sems, t, (px, py, c))
                for px, py in _other_chips(x, y)]
    if mode == "pass_halves":
        return [_remote(_column_half(src, 2 * px + py, c), _column_half(land, 2 * px + py, c), send_sems, recv_sems, t,
                        (x, y, 1 - c)) for px, py in _other_chips(x, y)]
    if mode == "scatter_chips":
        return [_remote(src.at[2 * px + py], land.at[chip], send_sems, recv_sems, t, (px, py, c))
                for px, py in _other_chips(x, y)]
    if mode == "sibling":
        return [_remote(src, land, send_sems, recv_sems, t, (x, y, 1 - c))]
    dev = 4 * x + 2 * y + c
    return [_remote(src.at[dev], land.at[dev], send_sems, recv_sems, t, (x ^ (k >> 2), y ^ ((k >> 1) & 1), c ^ (k & 1)))
            for k in range(1, N_DEV)]


def _push_start(srcs, lands, mode, name):
    n = len(srcs)
    same = all(s is ld for s, ld in zip(srcs, lands))
    n_in = n if same else 2 * n

    def body(*refs):
        src_refs = refs[:n]
        land_refs = src_refs if same else refs[n:2 * n]
        send_sems, recv_sems = refs[n_in], refs[n_in + 1]
        token = refs[-1]
        for t in range(n):
            for cp in _push_copies(mode, src_refs[t], land_refs[t], send_sems, recv_sems, t):
                cp.start()
        token[...] = jnp.zeros_like(token)

    operands = tuple(srcs) if same else tuple(srcs) + tuple(lands)
    res = _call(
        body, [_in_hbm(a) for a in operands], name=name,
        out_shape=(pltpu.SemaphoreType.DMA((n,)), pltpu.SemaphoreType.DMA((n,)))
        + tuple(pltpu.HBM(a.shape, a.dtype) for a in operands) + (jax.ShapeDtypeStruct((SUBLANES, LANES), F32),),
        in_specs=[HBM] * n_in, out_specs=(SEM, SEM) + (HBM,) * n_in + (pl.BlockSpec(memory_space=pltpu.VMEM),),
        input_output_aliases={i: 2 + i for i in range(n_in)}, side_effects=DATAFLOW, hbm_results=False)
    send_sems, recv_sems, token = res[0], res[1], res[-1]
    srcs_thru = res[2:2 + n]
    lands_thru = srcs_thru if same else res[2 + n:2 + 2 * n]
    return send_sems, recv_sems, srcs_thru, lands_thru, token


def _push_wait(send_sems, recv_sems, ids, srcs, lands, mode, after, name):
    n = len(lands)
    same = all(s is ld for s, ld in zip(srcs, lands))
    n_in = n if same else 2 * n

    def body(*refs):
        land_refs = refs[:n] if same else refs[n:2 * n]
        send_sems, recv_sems = refs[n_in], refs[n_in + 1]
        x, y, c = _coords()
        for t in range(n):
            if mode == "sibling":
                moved = land_refs[t]
            elif mode in ("gather_chips_half", "pass_halves"):
                moved = land_refs[t].at[pl.ds(0, PUSH_ARRIVALS[mode]), :, pl.ds(0, land_refs[t].shape[2] // 2)]
            else:
                moved = land_refs[t].at[pl.ds(0, PUSH_ARRIVALS[mode])]
            arrivals = _remote(moved, moved, send_sems, recv_sems, ids[t], (x, y, c))
            arrivals.wait_send()
            arrivals.wait_recv()

    operands = tuple(lands) if same else tuple(srcs) + tuple(lands)
    res = _call(
        body, operands + (send_sems, recv_sems, after), name=name,
        out_shape=tuple(pltpu.HBM(a.shape, a.dtype) for a in operands),
        in_specs=[HBM] * n_in + [SEM, SEM, ANY], out_specs=(HBM,) * n_in,
        input_output_aliases={i: i for i in range(n_in)}, side_effects=DATAFLOW)
    return list(res) if same else (list(res[:n]), list(res[n:]))


def _sum_partials(part, landed, chip):
    _, rows, cols = part.shape
    br = _divisor_tile(rows, 16, ELEM_ROWS)

    def body(chip_ref, own_ref, a_ref, b_ref, c_ref, o_ref):
        o_ref[...] = ((own_ref[...].astype(F32) + a_ref[...].astype(F32)) + b_ref[...].astype(F32)) \
            + c_ref[...].astype(F32)

    def other(k):
        return pl.BlockSpec((None, br, cols), lambda i, ch: (jnp.where(ch[0] <= k, k + 1, k), i, 0))

    spec = pltpu.PrefetchScalarGridSpec(
        num_scalar_prefetch=1, grid=(rows // br,),
        in_specs=[pl.BlockSpec((None, br, cols), lambda i, ch: (ch[0], i, 0)), other(0), other(1), other(2)],
        out_specs=pl.BlockSpec((br, cols), lambda i, ch: (i, 0)))
    return _call(body, (chip, part, landed, landed, landed), name="sum_partials", grid_spec=spec,
                 out_shape=jax.ShapeDtypeStruct((rows, cols), F32), semantics=("parallel",))


def _cast_to_slab(w, l, chip):
    _, rows, cols = w.shape
    br = _divisor_tile(rows, 16, ELEM_ROWS)

    def body(chip_ref, w_ref, o_ref):
        o_ref[...] = w_ref[...].astype(BF16)

    spec = pltpu.PrefetchScalarGridSpec(
        num_scalar_prefetch=1, grid=(rows // br,),
        in_specs=[pl.BlockSpec((None, br, cols), lambda i, ch: (l, i, 0))],
        out_specs=pl.BlockSpec((None, br, cols), lambda i, ch: (ch[0], i, 0)))
    return _call(body, (chip, w), name="cast_to_slab", grid_spec=spec,
                 out_shape=jax.ShapeDtypeStruct((N_CHIPS, rows, cols), BF16), semantics=("parallel",))


def _cast_w_in_t_to_slabs(w_t, chip):
    rows, depth, d = w_t.shape
    tn = _divisor_tile(d, LANES, 256)

    def body(chip_ref, w_ref, *o_refs):
        for l in range(depth):
            o_refs[l][...] = w_ref[:, l, :].astype(BF16)

    spec = pltpu.PrefetchScalarGridSpec(
        num_scalar_prefetch=1, grid=(d // tn,),
        in_specs=[pl.BlockSpec((rows, depth, tn), lambda j, ch: (0, 0, j))],
        out_specs=[pl.BlockSpec((None, rows, tn), lambda j, ch: (ch[0], 0, j))] * depth)
    return _call(body, (chip, w_t), name="cast_w_in_t_to_slabs", grid_spec=spec,
                 out_shape=[jax.ShapeDtypeStruct((N_CHIPS, rows, d), BF16)] * depth, semantics=("parallel",),
                 vmem_bytes=4 * _nbytes((rows, max(depth, SUBLANES), tn), F32))


def _place_slab(buf, index, n_slabs):
    rows, cols = buf.shape
    br = _divisor_tile(rows, SUBLANES, ELEM_ROWS)

    def body(index_ref, b_ref, o_ref):
        o_ref[...] = b_ref[...]

    spec = pltpu.PrefetchScalarGridSpec(
        num_scalar_prefetch=1, grid=(rows // br,),
        in_specs=[pl.BlockSpec((br, cols), lambda i, ix: (i, 0))],
        out_specs=pl.BlockSpec((None, br, cols), lambda i, ix: (ix[0], i, 0)))
    return _call(body, (index, buf), name="place_slab", grid_spec=spec,
                 out_shape=jax.ShapeDtypeStruct((n_slabs, rows, cols), buf.dtype), semantics=("parallel",))


ELEM_ROWS = 256


def _sum_slabs(r):
    n, rows, cols = r.shape
    br = _divisor_tile(rows, 16, ELEM_ROWS)

    def body(r_ref, o_ref):
        acc = r_ref[0].astype(F32)
        for j in range(1, n):
            acc = acc + r_ref[j].astype(F32)
        o_ref[...] = acc

    return _call(body, (r,), name="sum_slabs", grid=(rows // br,),
                 in_specs=[pl.BlockSpec((n, br, cols), lambda i: (0, i, 0))],
                 out_specs=pl.BlockSpec((br, cols), lambda i: (i, 0)),
                 out_shape=jax.ShapeDtypeStruct((rows, cols), F32), semantics=("parallel",))


def _adamw_math(w, g, m, v):
    c1 = 1.0 - ADAM_B1 ** ADAM_STEP
    c2 = 1.0 - ADAM_B2 ** ADAM_STEP
    nm = ADAM_B1 * m + (1.0 - ADAM_B1) * g
    nv = ADAM_B2 * v + (1.0 - ADAM_B2) * (g * g)
    delta = -ADAM_LR * ((nm / c1) / (jnp.sqrt(nv / c2) + ADAM_EPS) + ADAM_WD * w)
    return delta, nm, nv


def _adamw(w, g, m, v):
    rows, cols = w.shape
    br = _divisor_tile(rows, 8, ELEM_ROWS)

    def body(w_ref, g_ref, m_ref, v_ref, d_ref, nm_ref, nv_ref):
        d_ref[...], nm_ref[...], nv_ref[...] = _adamw_math(w_ref[...], g_ref[...], m_ref[...], v_ref[...])

    blk = pl.BlockSpec((br, cols), lambda i: (i, 0))
    return _call(body, (w, g, m, v), name="adamw", grid=(rows // br,),
                 in_specs=[blk] * 4, out_specs=[blk] * 3,
                 out_shape=[jax.ShapeDtypeStruct((rows, cols), F32)] * 3, semantics=("parallel",))


def _adamw_w_in_t(w_t, m_t, v_t, g_mine, g_theirs):
    rows, depth, d = w_t.shape
    tn = LANES

    def body(w_ref, m_ref, v_ref, ga_ref, gb_ref, g_ref, d_ref, nm_ref, nv_ref):
        g = ga_ref[...] + gb_ref[...]
        g_ref[...] = g
        d_ref[...], nm_ref[...], nv_ref[...] = _adamw_math(w_ref[...], g, m_ref[...], v_ref[...])

    slab = pl.BlockSpec((rows, depth, tn), lambda j: (0, 0, j))
    return _call(body, (w_t, m_t, v_t, g_mine, g_theirs), name="adamw_w_in_t", grid=(d // tn,),
                 in_specs=[slab] * 5, out_specs=[slab] * 4,
                 out_shape=[jax.ShapeDtypeStruct(w_t.shape, F32)] * 4, semantics=("parallel",),
                 vmem_bytes=2 * 9 * _nbytes((rows, max(depth, SUBLANES), tn), F32))


def _adamw_layer(w, m, v, l, g_mine, g_theirs, prev, after):
    _, rows, cols = w.shape
    br = _divisor_tile(rows, 8, ELEM_ROWS)

    def body(w_ref, m_ref, v_ref, ga_ref, gb_ref, *rest):
        g_ref, d_ref, nm_ref, nv_ref = rest[5:]
        g = ga_ref[...] + gb_ref[...]
        g_ref[...] = g
        d_ref[...], nm_ref[...], nv_ref[...] = _adamw_math(w_ref[...], g, m_ref[...], v_ref[...])

    slot = pl.BlockSpec((None, br, cols), lambda i: (l, i, 0))
    blk = pl.BlockSpec((br, cols), lambda i: (i, 0))
    return _call(body, (w, m, v, g_mine, g_theirs) + tuple(prev) + (after,), name="adamw_layer",
                 grid=(rows // br,), in_specs=[slot] * 3 + [blk] * 2 + [ANY] * 5, out_specs=[slot] * 4,
                 out_shape=[jax.ShapeDtypeStruct(w.shape, F32)] * 4,
                 input_output_aliases={5: 0, 6: 1, 7: 2, 8: 3}, semantics=("parallel",))


BIG = ("w_in", "w_out", "w_up", "w_down")
WEIGHTS = ("meta", "attn_norm_g", "w_in", "b_f", "conv_w", "conv_b", "w_gate_a", "b_gate_a", "w_gate_x",
           "b_gate_x", "lru_L", "attn_out_g", "rec_out_g", "w_out", "mlp_norm_g", "w_up", "w_down", "final_g")
SMALL = tuple(k for k in WEIGHTS if k not in BIG)
COL_SHARDED_SMALL = ("meta", "conv_w")


def _packed_rows(shape):
    return -(-math.prod(shape) // (SUBLANES * LANES)) * SUBLANES


def _pack(arrs):
    rows = []
    for a in arrs:
        flat = a.reshape(-1)
        rows.append(jnp.pad(flat, (0, _packed_rows(a.shape) * LANES - flat.shape[0])).reshape(-1, LANES))
    used = sum(r.shape[0] for r in rows)
    rows.append(jnp.zeros((-used % ELEM_ROWS, LANES), F32))
    return jnp.concatenate(rows, axis=0)


def _unpack(buf, shapes):
    out, r0 = [], 0
    for s in shapes:
        nr = _packed_rows(s)
        out.append(buf[r0:r0 + nr].reshape(-1)[:math.prod(s)].reshape(s))
        r0 += nr
    return out


def _halves(a):
    return a.reshape((2, a.shape[0] // 2) + a.shape[1:])


def _cols_from_chips(g):
    return jnp.moveaxis(g, 0, -2).reshape(g.shape[1:-1] + (N_CHIPS * g.shape[-1],))


def kernel(x, meta, attn_norm_g, w_in, b_f, conv_w, conv_b, w_gate_a, b_gate_a, w_gate_x, b_gate_x, lru_L, attn_out_g, rec_out_g, w_out, mlp_norm_g, w_up, w_down, final_g, loss_target, m_meta, m_attn_norm_g, m_w_in, m_b_f, m_conv_w, m_conv_b, m_w_gate_a, m_b_gate_a, m_w_gate_x, m_b_gate_x, m_lru_L, m_attn_out_g, m_rec_out_g, m_w_out, m_mlp_norm_g, m_w_up, m_w_down, m_final_g, v_meta, v_attn_norm_g, v_w_in, v_b_f, v_conv_w, v_conv_b, v_w_gate_a, v_b_gate_a, v_w_gate_x, v_b_gate_x, v_lru_L, v_attn_out_g, v_rec_out_g, v_w_out, v_mlp_norm_g, v_w_up, v_w_down, v_final_g):
    w = dict(meta=meta, attn_norm_g=attn_norm_g, w_in=w_in, b_f=b_f, conv_w=conv_w, conv_b=conv_b,
             w_gate_a=w_gate_a, b_gate_a=b_gate_a, w_gate_x=w_gate_x, b_gate_x=b_gate_x, lru_L=lru_L,
             attn_out_g=attn_out_g, rec_out_g=rec_out_g, w_out=w_out, mlp_norm_g=mlp_norm_g, w_up=w_up,
             w_down=w_down, final_g=final_g)
    m = dict(meta=m_meta, attn_norm_g=m_attn_norm_g, w_in=m_w_in, b_f=m_b_f, conv_w=m_conv_w, conv_b=m_conv_b,
             w_gate_a=m_w_gate_a, b_gate_a=m_b_gate_a, w_gate_x=m_w_gate_x, b_gate_x=m_b_gate_x, lru_L=m_lru_L,
             attn_out_g=m_attn_out_g, rec_out_g=m_rec_out_g, w_out=m_w_out, mlp_norm_g=m_mlp_norm_g,
             w_up=m_w_up, w_down=m_w_down, final_g=m_final_g)
    v = dict(meta=v_meta, attn_norm_g=v_attn_norm_g, w_in=v_w_in, b_f=v_b_f, conv_w=v_conv_w, conv_b=v_conv_b,
             w_gate_a=v_w_gate_a, b_gate_a=v_b_gate_a, w_gate_x=v_w_gate_x, b_gate_x=v_b_gate_x, lru_L=v_lru_L,
             attn_out_g=v_attn_out_g, rec_out_g=v_rec_out_g, w_out=v_w_out, mlp_norm_g=v_mlp_norm_g,
             w_up=v_w_up, w_down=v_w_down, final_g=v_final_g)
    s_len, d = x.shape[1], x.shape[2]
    depth = w_in.shape[0]
    att_w = d // 2
    rec_w = d - att_w
    nh = att_w // HEAD_DIM
    chip = 2 * lax.axis_index("x") + lax.axis_index("y")

    g_conv, g_meta = [g.reshape((N_CHIPS, g.shape[1] * g.shape[2]) + g.shape[3:])
                      for g in _all_gather_chips([_halves(w["conv_w"]), _halves(w["meta"])])]
    p = dict(w)
    p["conv_w"] = _cols_from_chips(g_conv)
    meta_full = jnp.moveaxis(g_meta, 0, 1).reshape(N_META, d)

    chip1 = chip.reshape(1).astype(jnp.int32)
    w_in_t, m_in_t, v_in_t = [jnp.transpose(a["w_in"], (2, 0, 1)) for a in (w, m, v)]
    w_in_slabs = _cast_w_in_t_to_slabs(w_in_t, chip1)
    pushes, tokens = [], []
    for l in range(depth):
        slabs = [w_in_slabs[l]] + [_cast_to_slab(w[k], l, chip1) for k in BIG[1:]]
        send_sems, recv_sems, _, lands, token = _push_start(slabs, slabs, "gather_chips_half", f"weights_start_{l}")
        pushes.append((send_sems, recv_sems, lands))
        tokens.append(token[0, 0])
    passed = [None] * depth

    def stage(l, after):
        send_sems, recv_sems, lands = pushes[l]
        lands = _push_wait(send_sems, recv_sems, list(range(len(BIG))), lands, lands, "gather_chips_half", after,
                           f"weights_wait_{l}")
        send_sems, recv_sems, _, lands, token = _push_start(lands, lands, "pass_halves", f"weights_pass_{l}")
        passed[l] = (send_sems, recv_sems, lands)
        return token[0, 0]

    t_len = N_META + s_len
    pad = -t_len % SEQ_TILE
    h = jnp.concatenate([meta_full, x[0], jnp.zeros((pad, d), F32)], axis=0)
    tgt = jnp.concatenate([jnp.zeros((N_META, d), F32), loss_target[0], jnp.zeros((pad, d), F32)], axis=0)
    z = _rms_fwd(h, _row(p["attn_norm_g"][0] + sum(tokens)))
    stage(0, z)
    saved = []
    for l in range(depth):
        def fetch(k, after, l=l):
            send_sems, recv_sems, lands = passed[l]
            i = BIG.index(k)
            return _push_wait(send_sems, recv_sems, [i], [lands[i]], [lands[i]], "pass_halves", after,
                              f"{k}_wait_{l}")[0]

        def stage_next(after, l=l):
            return stage(l + 1, after) if l + 1 < depth else 0.0

        g_next = p["attn_norm_g"][l + 1] if l + 1 < depth else p["final_g"]
        h, z, sv = _forward_layer(l, h, z, p, fetch, stage_next, g_next)
        saved.append(sv)
    dh, dh_b, dg_final, loss_part = _loss_bwd(h, _row(p["final_g"]), tgt, s_len)

    small = {k: [None] * depth for k in SMALL if k not in ("meta", "final_g")}
    pushes = [None] * depth
    tok = 0.0
    for l in reversed(range(depth)):
        dh, dh_b, big_mlp, sm_mlp = _backward_mlp(l, dh, dh_b, saved[l], p, tok)
        parts = [big_mlp["w_down"], big_mlp["w_up"]]
        push_mlp = _push_start(parts, [lax.empty(a.shape, a.dtype) for a in parts], "scatter_chips",
                               f"mlp_grads_start_{l}")
        dh, dh_b, big_mix, sm_mix = _backward_mixer(l, dh, dh_b, saved[l], p, push_mlp[4][0, 0])
        parts = [big_mix["w_out"], big_mix["w_in"]]
        push_mix = _push_start(parts, [lax.empty(a.shape, a.dtype) for a in parts], "scatter_chips",
                               f"mixer_grads_start_{l}")
        tok = push_mix[4][0, 0]
        pushes[l] = {("w_down", "w_up"): push_mlp, ("w_out", "w_in"): push_mix}
        for k, val in {**sm_mlp, **sm_mix}.items():
            small[k][l] = val
    grads = {k: jnp.stack(val) for k, val in small.items()}
    grads["final_g"] = dg_final[0]
    grads["meta"] = dh[:N_META]
    dx = dh[N_META:t_len]

    full_shapes = [grads[k].shape for k in SMALL] + [(1,)]
    packed = _pack([grads[k].astype(F32) for k in SMALL] + [loss_part[0, :1] + tok])
    dev1 = (2 * chip + lax.axis_index("c")).reshape(1).astype(jnp.int32)
    slabs = [_place_slab(packed, dev1, N_DEV)]
    small_push = _push_start(slabs, slabs, "gather_devices", "small_grads_start")

    last_token = small_push[4]
    outs = {k: [lax.empty(w[k].shape, F32) for _ in range(4)] for k in BIG[1:]}
    w_in_sums = [None] * depth
    swaps = {}

    def finish(l, wait_after, adam_after):
        send_sems, recv_sems, mine, lands, _ = swaps[l]
        mine, theirs = _push_wait(send_sems, recv_sems, list(range(len(BIG))), mine, lands, "sibling", wait_after,
                                  f"sums_wait_{l}")
        w_in_sums[l] = (mine[0], theirs[0])
        for k, a, b in zip(BIG[1:], mine[1:], theirs[1:]):
            outs[k] = _adamw_layer(w[k], m[k], v[k], l, a, b, outs[k], adam_after)

    for l in reversed(range(depth)):
        sums = {}
        for names, (send_sems, recv_sems, parts, lands, _) in pushes[l].items():
            parts, landed = _push_wait(send_sems, recv_sems, [0, 1], parts, lands, "scatter_chips", last_token,
                                       f"{names[0]}_grads_wait_{l}")
            for k, part, land in zip(names, parts, landed):
                sums[k] = _sum_partials(part, land, chip1)
        mine = [sums[k] for k in BIG]
        swaps[l] = _push_start(mine, [lax.empty(a.shape, a.dtype) for a in mine], "sibling", f"sums_start_{l}")
        if l + 1 < depth:
            finish(l + 1, outs["w_down"][0] if l + 2 < depth else mine[0], swaps[l][4])
    finish(0, outs["w_down"][0] if depth > 1 else swaps[0][4], swaps[0][4])
    outs["w_in"] = [jnp.transpose(r, (1, 2, 0)) for r in _adamw_w_in_t(
        w_in_t, m_in_t, v_in_t, jnp.stack([s[0] for s in w_in_sums], axis=1),
        jnp.stack([s[1] for s in w_in_sums], axis=1))]
    out_g, out_d, out_m, out_v = [{k: outs[k][i] for k in BIG} for i in range(4)]

    landed = _push_wait(small_push[0], small_push[1], [0], small_push[3], small_push[3], "gather_devices",
                        out_g["w_in"], "small_grads_wait")
    total = _sum_slabs(landed[0])
    small_g = dict(zip(SMALL + ("loss",), _unpack(total, full_shapes)))
    for k in COL_SHARDED_SMALL:
        n = w[k].shape[-1]
        small_g[k] = lax.dynamic_slice_in_dim(small_g[k], chip * n, n, axis=small_g[k].ndim - 1)
    local_shapes = [w[k].shape for k in SMALL]
    res = _adamw(_pack([w[k] for k in SMALL]), _pack([small_g[k] for k in SMALL]),
                 _pack([m[k] for k in SMALL]), _pack([v[k] for k in SMALL]))
    out_g.update({k: small_g[k] for k in SMALL})
    for dst, buf in zip((out_d, out_m, out_v), res):
        dst.update(zip(SMALL, _unpack(buf, local_shapes)))

    return (small_g["loss"].reshape(()), dx[None],
            *[out_g[k] for k in WEIGHTS], *[out_d[k] for k in WEIGHTS],
            *[out_m[k] for k in WEIGHTS], *[out_v[k] for k in WEIGHTS])
```

```python
import functools
import math

import jax
import jax.numpy as jnp
from jax import lax
from jax.experimental import pallas as pl
from jax.experimental.pallas import tpu as pltpu

F32 = jnp.float32
BF16 = jnp.bfloat16

N_META = 16
HEAD_DIM = 64
N_REC_BLOCKS = 8
CONV_WIDTH = 4
RG_C = 8.0
NORM_EPS = 1e-6
ADAM_LR = 0.001
ADAM_B1 = 0.9
ADAM_B2 = 0.999
ADAM_EPS = 1e-08
ADAM_WD = 0.01
ADAM_STEP = 10

LANES = 128
SUBLANES = 8
SEQ_TILE = 128
VMEM_CAP = 60 * 2**20
VMEM_SLACK = 6 * 2**20
NEG_BIG = -1e30
N_CHIPS = 4
N_DEV = 8
MESH = pl.DeviceIdType.MESH


def _nbytes(shape, dtype):
    return math.prod(shape) * jnp.dtype(dtype).itemsize


def _call(body, args, *, name, out_shape, grid=(), in_specs=None, out_specs=None, scratch_shapes=(),
          grid_spec=None, semantics=None, vmem_bytes=None, side_effects=None, hbm_results=True, **kw):
    cp = {}
    if semantics is not None:
        cp["dimension_semantics"] = semantics
    if vmem_bytes is not None:
        cp["vmem_limit_bytes"] = int(min(VMEM_CAP, vmem_bytes + VMEM_SLACK))
    if side_effects is not None:
        cp["has_side_effects"] = side_effects
    if grid_spec is not None:
        kw["grid_spec"] = grid_spec
    else:
        kw.update(grid=grid, in_specs=in_specs, out_specs=out_specs, scratch_shapes=scratch_shapes)
    if hbm_results:
        out_shape = jax.tree.map(
            lambda s: pltpu.HBM(s.shape, s.dtype) if isinstance(s, jax.ShapeDtypeStruct) else s, out_shape)
    fn = pl.pallas_call(
        body, name=name, out_shape=out_shape,
        compiler_params=pltpu.CompilerParams(**cp), **kw)
    return fn(*[_in_hbm(a) if jnp.issubdtype(getattr(a, "dtype", jnp.int32), jnp.floating) else a for a in args])


def _divisor_tile(n, unit, target):
    best = None
    for t in range(unit, min(n, target) + 1, unit):
        if n % t == 0:
            best = t
    return n if best is None else best


def _sigmoid(x):
    return 1.0 / (1.0 + jnp.exp(-x))


def _log1p_unit(e):
    series = e * (1.0 - e * (0.5 - e * (1.0 / 3.0)))
    return jnp.where(e < 1e-2, series, jnp.log(1.0 + e))


def _log_sigmoid(x):
    return jnp.minimum(x, 0.0) - _log1p_unit(jnp.exp(-jnp.abs(x)))


def _one_minus_exp(x, exp_x):
    small = -x * (1.0 + x * (1.0 / 2 + x * (1.0 / 6 + x * (1.0 / 24 + x * (1.0 / 120 + x * (1.0 / 720))))))
    return jnp.where(x > -0.25, small, 1.0 - exp_x)


_GELU_K = math.sqrt(2.0 / math.pi)
_GELU_C = 0.044715


def _gelu_and_grad(y):
    th = jnp.tanh(_GELU_K * (y + _GELU_C * y * y * y))
    g = 0.5 * y * (1.0 + th)
    dg = 0.5 * (1.0 + th) + 0.5 * y * (1.0 - th * th) * _GELU_K * (1.0 + 3.0 * _GELU_C * y * y)
    return g, dg


def _rstd(x):
    return lax.rsqrt(jnp.mean(x * x, axis=-1, keepdims=True) + NORM_EPS)


def _rms_bwd(dz, x, g):
    rs = _rstd(x)
    xh = x * rs
    dgp = jnp.sum(dz * xh, axis=0, keepdims=True)
    dxh = dz * g
    dx = rs * (dxh - xh * jnp.mean(dxh * xh, axis=-1, keepdims=True))
    return dx, dgp


def _dot(a, b):
    return jnp.dot(a, b, preferred_element_type=F32)


def _dot_nt(a, b):
    return lax.dot_general(a, b, (((1,), (1,)), ((), ())), preferred_element_type=F32)


def _dot_tn(a, b):
    return lax.dot_general(a, b, (((0,), (0,)), ((), ())), preferred_element_type=F32)


def _full(shape):
    nd = len(shape)
    return pl.BlockSpec(shape, lambda *_: (0,) * nd)


def _rms_fwd(h, g):
    tp, d = h.shape
    tm = _divisor_tile(tp, 16, 544)

    def body(h_ref, g_ref, z_ref):
        x = h_ref[...]
        z_ref[...] = (x * _rstd(x) * g_ref[...]).astype(BF16)

    return _call(body, (h, g), name="rms_fwd", grid=(tp // tm,),
                 in_specs=[pl.BlockSpec((tm, d), lambda i: (i, 0)), _full((1, d))],
                 out_specs=pl.BlockSpec((tm, d), lambda i: (i, 0)),
                 out_shape=jax.ShapeDtypeStruct((tp, d), BF16), semantics=("parallel",))


def _proj(z, w_big_t, att_w):
    tp, d = z.shape
    nb = w_big_t.shape[0]
    tn = _divisor_tile(nb, LANES, 512)
    assert (3 * att_w) % tn == 0
    n_qkv = 3 * att_w // tn
    scale = 1.0 / math.sqrt(HEAD_DIM)

    def body(z_ref, w_ref, qkv_ref, p_ref):
        j = pl.program_id(0)
        acc = _dot_nt(z_ref[...], w_ref[...])

        @pl.when(j < n_qkv)
        def _():
            col = j * tn + lax.broadcasted_iota(jnp.int32, (1, tn), 1)
            qkv_ref[...] = (acc * jnp.where(col < att_w, scale, 1.0)).astype(BF16)

        @pl.when(j >= n_qkv)
        def _():
            p_ref[...] = acc

    vm = 2 * (_nbytes((tp, d), BF16) + _nbytes((d, tn), BF16) + _nbytes((tp, tn), F32) * 2)
    return _call(body, (z, w_big_t), name="proj", grid=(nb // tn,),
                 in_specs=[_full((tp, d)), pl.BlockSpec((tn, d), lambda j: (j, 0))],
                 out_specs=[pl.BlockSpec((tp, tn), lambda j: (0, jnp.minimum(j, n_qkv - 1))),
                            pl.BlockSpec((tp, tn), lambda j: (0, jnp.maximum(j - n_qkv, 0)))],
                 out_shape=[jax.ShapeDtypeStruct((tp, 3 * att_w), BF16),
                            jax.ShapeDtypeStruct((tp, nb - 3 * att_w), F32)],
                 semantics=("arbitrary",), vmem_bytes=vm)


def _tile_cumsum(x, row, reverse=False):
    for s in (1, 2, 4):
        if reverse:
            x = x + jnp.where(row < SUBLANES - s, pltpu.roll(x, SUBLANES - s, 0), 0.0)
        else:
            x = x + jnp.where(row >= s, pltpu.roll(x, s, 0), 0.0)
    return x


def _fgate_fwd(proj, b_f_pad, nh):
    tp, nb = proj.shape
    fblk = nb // LANES - 1

    def body(f_ref, b_ref, c_ref, ct_ref):
        b = b_ref[...]
        row = lax.broadcasted_iota(jnp.int32, (SUBLANES, LANES), 0)

        def step(i, carry):
            r0 = pl.multiple_of(i * SUBLANES, SUBLANES)
            lf = _log_sigmoid(f_ref[pl.ds(r0, SUBLANES), :] + b)
            x = _tile_cumsum(lf, row) + carry
            c_ref[pl.ds(r0, SUBLANES), :] = x
            return x[SUBLANES - 1:SUBLANES, :]

        lax.fori_loop(0, tp // SUBLANES, step, jnp.zeros((1, LANES), F32))
        ct_ref[...] = c_ref[...].T[:nh, :]

    return _call(body, (proj, b_f_pad), name="fgate_fwd", grid=(1,),
                 in_specs=[pl.BlockSpec((tp, LANES), lambda i: (0, fblk)), _full((1, LANES))],
                 out_specs=[_full((tp, LANES)), _full((nh, tp))],
                 out_shape=[jax.ShapeDtypeStruct((tp, LANES), F32), jax.ShapeDtypeStruct((nh, tp), F32)],
                 semantics=("arbitrary",))


def _fgate_bwd(proj, b_f_pad, dc):
    tp, nb = proj.shape
    fblk = nb // LANES - 1

    def body(f_ref, b_ref, dc_ref, df_ref, db_ref, dc_s):
        b = b_ref[...]
        row = lax.broadcasted_iota(jnp.int32, (SUBLANES, LANES), 0)
        nt = tp // SUBLANES

        def step(i, carry):
            suffix, acc = carry
            r0 = pl.multiple_of((nt - 1 - i) * SUBLANES, SUBLANES)
            dlf = _tile_cumsum(dc_ref[pl.ds(r0, SUBLANES), :], row, reverse=True) + suffix
            df = dlf * _sigmoid(-(f_ref[pl.ds(r0, SUBLANES), :] + b))
            dc_s[pl.ds(r0, SUBLANES), :] = df
            return dlf[0:1, :], acc + df

        _, acc = lax.fori_loop(0, nt, step, (jnp.zeros((1, LANES), F32), jnp.zeros((SUBLANES, LANES), F32)))
        df_ref[...] = dc_s[...].astype(BF16)
        db_ref[...] = jnp.broadcast_to(jnp.sum(acc, axis=0, keepdims=True), (SUBLANES, LANES))

    return _call(body, (proj, b_f_pad, dc), name="fgate_bwd", grid=(1,),
                 in_specs=[pl.BlockSpec((tp, LANES), lambda i: (0, fblk)), _full((1, LANES)), _full((tp, LANES))],
                 out_specs=[_full((tp, LANES)), _full((SUBLANES, LANES))],
                 out_shape=[jax.ShapeDtypeStruct((tp, LANES), BF16),
                            jax.ShapeDtypeStruct((SUBLANES, LANES), F32)],
                 scratch_shapes=[pltpu.VMEM((tp, LANES), F32)], semantics=("arbitrary",))


ATT_BQ = 128


ATT_BUCKET = 3
ATT_HEADS = 4


def _for_bucket(i, nq, fn):
    for lo in range(0, nq, ATT_BUCKET):
        hi = min(lo + ATT_BUCKET, nq)
        spans = ([(0, lo * ATT_BQ, False)] if lo else []) + [(lo * ATT_BQ, hi * ATT_BQ, True)]
        pl.when(jnp.logical_and(i >= lo, i < hi))(functools.partial(fn, spans))


def _head_column(c_blk, h):
    lane = lax.broadcasted_iota(jnp.int32, c_blk.shape, 1)
    return jnp.sum(jnp.where(lane == h, c_blk, 0.0), axis=1, keepdims=True)


def _key_major_logits(kh_ref, ck_ref, hh, q, span, q0):
    k0, k1, needs_mask = span
    t = _dot_nt(kh_ref[hh, k0:k1, :], q) - ck_ref[hh, k0:k1, :]
    if needs_mask:
        keys = k0 + lax.broadcasted_iota(jnp.int32, (k1 - k0, ATT_BQ), 0)
        t = jnp.where(keys <= q0 + lax.broadcasted_iota(jnp.int32, (k1 - k0, ATT_BQ), 1), t, NEG_BIG)
    return t


def _stage_keys(p, k_ref, c_ref, kh_s, ck_s):
    for hh in range(ATT_HEADS):
        kh_s[hh] = k_ref[:, HEAD_DIM * hh:HEAD_DIM * (hh + 1)]
        ck_s[hh] = jnp.broadcast_to(_head_column(c_ref[...], ATT_HEADS * p + hh), ck_s.shape[1:])


def _attn_fwd(qkv, c, c_t, nh):
    tp = qkv.shape[0]
    att_w = nh * HEAD_DIM
    ng = nh // ATT_HEADS
    gw = ATT_HEADS * HEAD_DIM
    bq = ATT_BQ
    nq = tp // bq

    def body(q_ref, k_ref, v_ref, c_ref, ct_ref, o_ref, lse_ref, ck_s, kh_s, vt_s):
        p = pl.program_id(0)
        i = pl.program_id(1)

        @pl.when(i == 0)
        def _():
            _stage_keys(p, k_ref, c_ref, kh_s, ck_s)
            vt_s[...] = v_ref[...].astype(F32).T.astype(BF16)

        def compute(spans):
            q0 = pl.multiple_of(i * bq, bq)
            o_t, lses = [], []
            for hh in range(ATT_HEADS):
                lo = HEAD_DIM * hh
                q = q_ref[:, lo:lo + HEAD_DIM]
                ts = [_key_major_logits(kh_s, ck_s, hh, q, sp, q0) for sp in spans]
                m = functools.reduce(jnp.maximum, [jnp.max(t, axis=0, keepdims=True) for t in ts])
                es = [jnp.exp(t - m) for t in ts]
                l = sum(jnp.sum(e, axis=0, keepdims=True) for e in es)
                o = sum(_dot(vt_s[lo:lo + HEAD_DIM, k0:k1], e.astype(BF16)) for e, (k0, k1, _) in zip(es, spans))
                o_t.append(o / l)
                lses.append(m + ct_ref[pl.ds(ATT_HEADS * p + hh, 1), :] + jnp.log(l))
            o_ref[...] = jnp.concatenate(o_t, axis=0).T
            lse_ref[...] = jnp.concatenate(lses, axis=0)

        _for_bucket(i, nq, compute)

    blk = pl.BlockSpec((bq, gw), lambda p, i: (i, p))
    vm = 6 * _nbytes((tp, gw), BF16) + 2 * ATT_HEADS * _nbytes((tp, LANES), F32) + 2 * _nbytes((tp, LANES), F32) \
        + 8 * ATT_HEADS * _nbytes((bq, tp), F32)
    return _call(body, (qkv, qkv, qkv, c, c_t), name="attn_fwd", grid=(ng, nq),
                 in_specs=[blk,
                           pl.BlockSpec((tp, gw), lambda p, i: (0, ng + p)),
                           pl.BlockSpec((tp, gw), lambda p, i: (0, 2 * ng + p)),
                           _full((tp, LANES)), pl.BlockSpec((nh, bq), lambda p, i: (0, i))],
                 out_specs=[blk, pl.BlockSpec((None, ATT_HEADS, bq), lambda p, i: (p, 0, i))],
                 out_shape=[jax.ShapeDtypeStruct((tp, att_w), F32), jax.ShapeDtypeStruct((ng, ATT_HEADS, tp), F32)],
                 scratch_shapes=[pltpu.VMEM((ATT_HEADS, tp, LANES), F32), pltpu.VMEM((ATT_HEADS, tp, HEAD_DIM), BF16),
                                 pltpu.VMEM((gw, tp), BF16)],
                 semantics=("arbitrary", "arbitrary"), vmem_bytes=vm)


def _attn_bwd(qkv, c, c_t, lse, do, nh):
    tp = qkv.shape[0]
    att_w = nh * HEAD_DIM
    ng = nh // ATT_HEADS
    gw = ATT_HEADS * HEAD_DIM
    bq = ATT_BQ
    nq = tp // bq
    pair = 2 * HEAD_DIM
    assert pair == LANES and ATT_HEADS % 2 == 0
    scale = 1.0 / math.sqrt(HEAD_DIM)

    def body(q_ref, k_ref, v_ref, c_ref, ct_ref, lse_ref, do_ref, dq_ref, dk_ref, dv_ref, dc_ref,
             dk_s, dv_s, dc_s, ck_s, kt_s):
        p = pl.program_id(0)
        i = pl.program_id(1)

        @pl.when(i == 0)
        def _():
            dk_s[...] = jnp.zeros_like(dk_s)
            dv_s[...] = jnp.zeros_like(dv_s)
            dc_s[...] = jnp.zeros_like(dc_s)
            kt_s[...] = k_ref[...].astype(F32).T.astype(BF16)
            for hh in range(ATT_HEADS):
                ck_s[hh] = jnp.broadcast_to(_head_column(c_ref[...], ATT_HEADS * p + hh), (tp, LANES))

        @pl.when(jnp.logical_and(i == 0, p == 0))
        def _():
            dc_ref[...] = jnp.zeros_like(dc_ref)

        def compute(spans):
            q0 = pl.multiple_of(i * bq, bq)
            top = lax.broadcasted_iota(jnp.int32, (pair, bq), 0) < HEAD_DIM
            low = lax.broadcasted_iota(jnp.int32, (bq, pair), 1) < HEAD_DIM

            def diag_cols(x2):
                xt = x2.astype(F32).T.astype(BF16)
                return jnp.concatenate([jnp.where(top, xt, 0), jnp.where(top, 0, xt)], axis=1)

            def diag_rows(x2):
                return jnp.concatenate([jnp.where(low, x2, 0), jnp.where(low, 0, x2)], axis=0)

            dq_t = []
            for pi in range(ATT_HEADS // 2):
                lo = pair * pi
                q2 = q_ref[:, lo:lo + pair]
                do2 = do_ref[:, lo:lo + pair].astype(BF16)
                q_cols, do_cols = diag_cols(q2), diag_cols(do2)
                q_rows, do_rows = diag_rows(q2), diag_rows(do2)
                heads = (2 * pi, 2 * pi + 1)
                col_terms = [ct_ref[pl.ds(ATT_HEADS * p + hh, 1), :] - lse_ref[hh:hh + 1, :] for hh in heads]
                prs, dps = [], []
                for k0, k1, needs_mask in spans:
                    t2 = _dot(k_ref[k0:k1, lo:lo + pair], q_cols)
                    dp2 = _dot(v_ref[k0:k1, lo:lo + pair], do_cols)
                    if needs_mask:
                        keys = k0 + lax.broadcasted_iota(jnp.int32, (k1 - k0, bq), 0)
                        seen = keys <= q0 + lax.broadcasted_iota(jnp.int32, (k1 - k0, bq), 1)
                    pr_e, dp_e = [], []
                    for e, hh in enumerate(heads):
                        t = t2[:, e * bq:(e + 1) * bq] - ck_s[hh, k0:k1, :]
                        if needs_mask:
                            t = jnp.where(seen, t, NEG_BIG)
                        pr_e.append(jnp.exp(t + col_terms[e]))
                        dp_e.append(dp2[:, e * bq:(e + 1) * bq])
                    prs.append(pr_e)
                    dps.append(dp_e)
                key_sums = [sum(jnp.sum(pr[e] * dp[e], axis=0, keepdims=True) for pr, dp in zip(prs, dps))
                            for e in range(2)]
                dq2 = 0.0
                for (k0, k1, _), pr, dp in zip(spans, prs, dps):
                    ds = [pr[e] * (dp[e] - key_sums[e]) for e in range(2)]
                    for e, hh in enumerate(heads):
                        dc_s[hh, k0:k1, :] += jnp.sum(ds[e], axis=1, keepdims=True)
                    ds2 = jnp.concatenate([ds[0].astype(BF16), ds[1].astype(BF16)], axis=1)
                    pr2 = jnp.concatenate([pr[0].astype(BF16), pr[1].astype(BF16)], axis=1)
                    dk_s[k0:k1, lo:lo + pair] += _dot(ds2, q_rows)
                    dv_s[k0:k1, lo:lo + pair] += _dot(pr2, do_rows)
                    dq2 = dq2 + _dot(kt_s[lo:lo + pair, k0:k1], ds2)
                dq_t.append(jnp.concatenate([dq2[:HEAD_DIM, :bq], dq2[HEAD_DIM:, bq:]], axis=0))
            dq_ref[...] = (jnp.concatenate(dq_t, axis=0) * scale).T.astype(BF16)

        _for_bucket(i, nq, compute)

        @pl.when(i == nq - 1)
        def _():
            dk_ref[...] = dk_s[...].astype(BF16)
            dv_ref[...] = dv_s[...].astype(BF16)
            lane = lax.broadcasted_iota(jnp.int32, (tp, LANES), 1)
            dc = dc_ref[...]
            for hh in range(ATT_HEADS):
                dc = jnp.where(lane == ATT_HEADS * p + hh, -dc_s[hh], dc)
            dc_ref[...] = dc

    blk = pl.BlockSpec((bq, gw), lambda p, i: (i, p))
    col = pl.BlockSpec((tp, gw), lambda p, i: (0, p))
    vm = 7 * _nbytes((tp, gw), BF16) + 2 * _nbytes((tp, gw), F32) + 2 * ATT_HEADS * _nbytes((tp, LANES), F32) \
        + 2 * _nbytes((tp, LANES), F32) + 12 * ATT_HEADS * _nbytes((bq, tp), F32)
    return _call(body, (qkv, qkv, qkv, c, c_t, lse, do), name="attn_bwd", grid=(ng, nq),
                 in_specs=[blk,
                           pl.BlockSpec((tp, gw), lambda p, i: (0, ng + p)),
                           pl.BlockSpec((tp, gw), lambda p, i: (0, 2 * ng + p)),
                           _full((tp, LANES)), pl.BlockSpec((nh, bq), lambda p, i: (0, i)),
                           pl.BlockSpec((None, ATT_HEADS, bq), lambda p, i: (p, 0, i)), blk],
                 out_specs=[blk, col, col, _full((tp, LANES))],
                 out_shape=[jax.ShapeDtypeStruct((tp, att_w), BF16)] * 3 + [jax.ShapeDtypeStruct((tp, LANES), F32)],
                 scratch_shapes=[pltpu.VMEM((tp, gw), F32), pltpu.VMEM((tp, gw), F32),
                                 pltpu.VMEM((ATT_HEADS, tp, 1), F32), pltpu.VMEM((ATT_HEADS, tp, LANES), F32),
                                 pltpu.VMEM((gw, tp), BF16)],
                 semantics=("arbitrary", "arbitrary"), vmem_bytes=vm)


REC_ROWS = 128
HALO = SUBLANES


def _conv_taps(cat):
    taps = []
    for k in range(CONV_WIDTH):
        sh = CONV_WIDTH - 1 - k
        taps.append((pltpu.roll(cat, sh, 0) if sh else cat)[HALO:])
    return taps


def _rec_gates(xc, wa_ref, ba_ref, wx_ref, bx_ref, l_ref):
    xcb = xc.astype(BF16)
    r = _sigmoid(_dot(xcb, wa_ref[...]) + ba_ref[...])
    ig = _sigmoid(_dot(xcb, wx_ref[...]) + bx_ref[...])
    ls = _log_sigmoid(l_ref[...])
    log_a = RG_C * r * ls
    return xcb, r, ig, ls, log_a


def _rec_fwd(proj, xr_blk, yr_blk, rec_w, conv_w, conv_b, wa, ba, wx, bx, lru):
    tp = proj.shape[0]
    w = rec_w
    r_rows = REC_ROWS
    nc = tp // r_rows
    cpb = w // LANES

    def body(xr_ref, yr_ref, cw_ref, cb_ref, wa_ref, ba_ref, wx_ref, bx_ref, l_ref,
             hr_ref, rec_ref, prev_s, carry_s, a_s, u_s):
        i = pl.program_id(0)

        @pl.when(i == 0)
        def _():
            prev_s[...] = jnp.zeros_like(prev_s)
            carry_s[...] = jnp.zeros_like(carry_s)

        x = xr_ref[...]
        taps = _conv_taps(jnp.concatenate([prev_s[...], x], axis=0))
        prev_s[...] = x[r_rows - HALO:]
        xc = cb_ref[...]
        for k in range(CONV_WIDTH):
            xc = xc + cw_ref[k:k + 1, :] * taps[k]
        _, r, ig, ls, log_a = _rec_gates(xc, wa_ref, ba_ref, wx_ref, bx_ref, l_ref)
        a = jnp.exp(log_a)
        a_s[...] = a
        u_s[...] = jnp.sqrt(_one_minus_exp(2.0 * log_a, a * a)) * ig * xc

        def tile(j, h):
            r0 = pl.multiple_of(j * SUBLANES, SUBLANES)
            at = a_s[pl.ds(r0, SUBLANES), :]
            ut = u_s[pl.ds(r0, SUBLANES), :]
            out = []
            for rr in range(SUBLANES):
                h = at[rr:rr + 1] * h + ut[rr:rr + 1]
                out.append(h)
            hr_ref[pl.ds(r0, SUBLANES), :] = jnp.concatenate(out, axis=0)
            return h

        carry_s[0:1, :] = lax.fori_loop(0, r_rows // SUBLANES, tile, carry_s[0:1, :])
        g, _ = _gelu_and_grad(yr_ref[...])
        rec_ref[...] = hr_ref[...] * g

    blk = pl.BlockSpec((r_rows, w), lambda i: (i, 0))
    vm = 16 * _nbytes((r_rows, w), F32) + 4 * _nbytes((w, w), BF16)
    return _call(body, (proj, proj, conv_w, conv_b, wa, ba, wx, bx, lru), name="rec_fwd", grid=(nc,),
                 in_specs=[pl.BlockSpec((r_rows, w), lambda i: (i, xr_blk)),
                           pl.BlockSpec((r_rows, w), lambda i: (i, yr_blk)),
                           _full((CONV_WIDTH, w)), _full((1, w)), _full((w, w)), _full((1, w)),
                           _full((w, w)), _full((1, w)), _full((1, w))],
                 out_specs=[blk, blk],
                 out_shape=[jax.ShapeDtypeStruct((tp, w), F32)] * 2,
                 scratch_shapes=[pltpu.VMEM((HALO, w), F32), pltpu.VMEM((SUBLANES, w), F32),
                                 pltpu.VMEM((r_rows, w), F32), pltpu.VMEM((r_rows, w), F32)],
                 semantics=("arbitrary",), vmem_bytes=vm)


def _rec_bwd(proj, xr_blk, yr_blk, rec_w, hr, drec, conv_w, conv_b, wa, ba, wx, bx, lru):
    tp = proj.shape[0]
    w = rec_w
    r_rows = REC_ROWS
    nc = tp // r_rows
    hpc = r_rows // HALO

    def body(xr_ref, xh_ref, yr_ref, hr_ref, hh_ref, drec_ref, cw_ref, cb_ref, wa_ref, ba_ref, wx_ref, bx_ref,
             l_ref, dxr_ref, dyr_ref, dwa_ref, dwx_ref, small_ref, lam_s, a_s, dhr_s, carry_s, next_s):
        i = pl.program_id(0)
        first = (nc - 1 - i) == 0

        @pl.when(i == 0)
        def _():
            carry_s[...] = jnp.zeros_like(carry_s)
            next_s[...] = jnp.zeros_like(next_s)
            dwa_ref[...] = jnp.zeros_like(dwa_ref)
            dwx_ref[...] = jnp.zeros_like(dwx_ref)
            small_ref[...] = jnp.zeros_like(small_ref)

        x = xr_ref[...]
        xprev = jnp.where(first, 0.0, xh_ref[...])
        taps = _conv_taps(jnp.concatenate([xprev, x], axis=0))
        xc = cb_ref[...]
        for k in range(CONV_WIDTH):
            xc = xc + cw_ref[k:k + 1, :] * taps[k]
        xcb, r, ig, ls, log_a = _rec_gates(xc, wa_ref, ba_ref, wx_ref, bx_ref, l_ref)
        a = jnp.exp(log_a)
        a2 = a * a
        mult = jnp.sqrt(_one_minus_exp(2.0 * log_a, a2))
        g, dg = _gelu_and_grad(yr_ref[...])
        hr_v = hr_ref[...]
        drec_v = drec_ref[...]
        dhr_s[...] = drec_v * g
        dyr_ref[...] = (drec_v * hr_v * dg).astype(BF16)
        a_s[...] = a

        def tile(jj, carry):
            r0 = pl.multiple_of((r_rows // SUBLANES - 1 - jj) * SUBLANES, SUBLANES)
            at = a_s[pl.ds(r0, SUBLANES), :]
            dt = dhr_s[pl.ds(r0, SUBLANES), :]
            out = [None] * SUBLANES
            for rr in range(SUBLANES - 1, -1, -1):
                lam = dt[rr:rr + 1] + carry
                out[rr] = lam
                carry = at[rr:rr + 1] * lam
            lam_s[pl.ds(r0, SUBLANES), :] = jnp.concatenate(out, axis=0)
            return carry

        carry_s[0:1, :] = lax.fori_loop(0, r_rows // SUBLANES, tile, carry_s[0:1, :])
        lam = lam_s[...]
        hprev = jnp.where(first, 0.0, hh_ref[...])
        hr_prev = pltpu.roll(jnp.concatenate([hprev, hr_v], axis=0), 1, 0)[HALO:]
        da = lam * hr_prev
        dxc = lam * mult * ig
        di = lam * mult * xc
        dmult = lam * ig * xc
        dlog_a = da * a - dmult * a2 / mult
        dr = dlog_a * (RG_C * ls)
        dls = jnp.sum(dlog_a * (RG_C * r), axis=0, keepdims=True)
        dga = dr * r * (1.0 - r)
        dgx = di * ig * (1.0 - ig)
        dgab = dga.astype(BF16)
        dgxb = dgx.astype(BF16)
        dxc = dxc + _dot_nt(dgab, wa_ref[...]) + _dot_nt(dgxb, wx_ref[...])
        dwa_ref[...] += _dot_tn(xcb, dgab)
        dwx_ref[...] += _dot_tn(xcb, dgxb)
        cat = jnp.concatenate([dxc, next_s[...]], axis=0)
        next_s[...] = dxc[0:HALO]
        dxr = cw_ref[CONV_WIDTH - 1:CONV_WIDTH, :] * dxc
        for k in range(CONV_WIDTH - 1):
            sh = CONV_WIDTH - 1 - k
            dxr = dxr + cw_ref[k:k + 1, :] * pltpu.roll(cat, r_rows + HALO - sh, 0)[:r_rows]
        dxr_ref[...] = dxr.astype(BF16)
        rows = [jnp.sum(dxc * taps[k], axis=0, keepdims=True) for k in range(CONV_WIDTH)]
        rows += [jnp.sum(dxc, axis=0, keepdims=True), jnp.sum(dga, axis=0, keepdims=True),
                 jnp.sum(dgx, axis=0, keepdims=True), dls * _sigmoid(-l_ref[...])]
        small_ref[...] += jnp.concatenate(rows, axis=0)

    def rev(i):
        return nc - 1 - i

    def halo(i):
        return jnp.maximum(rev(i) * hpc - 1, 0)

    blk = pl.BlockSpec((r_rows, w), lambda i: (rev(i), 0))
    vm = 40 * _nbytes((r_rows, w), F32) + 6 * _nbytes((w, w), F32)
    return _call(body, (proj, proj, proj, hr, hr, drec, conv_w, conv_b, wa, ba, wx, bx, lru),
                 name="rec_bwd", grid=(nc,),
                 in_specs=[pl.BlockSpec((r_rows, w), lambda i: (rev(i), xr_blk)),
                           pl.BlockSpec((HALO, w), lambda i: (halo(i), xr_blk)),
                           pl.BlockSpec((r_rows, w), lambda i: (rev(i), yr_blk)),
                           blk,
                           pl.BlockSpec((HALO, w), lambda i: (halo(i), 0)),
                           blk,
                           _full((CONV_WIDTH, w)), _full((1, w)), _full((w, w)), _full((1, w)),
                           _full((w, w)), _full((1, w)), _full((1, w))],
                 out_specs=[blk, blk, _full((w, w)), _full((w, w)), _full((SUBLANES, w))],
                 out_shape=[jax.ShapeDtypeStruct((tp, w), BF16)] * 2
                 + [jax.ShapeDtypeStruct((w, w), F32)] * 2 + [jax.ShapeDtypeStruct((SUBLANES, w), F32)],
                 scratch_shapes=[pltpu.VMEM((r_rows, w), F32)] * 3
                 + [pltpu.VMEM((SUBLANES, w), F32), pltpu.VMEM((HALO, w), F32)],
                 semantics=("arbitrary",), vmem_bytes=vm)


ROW_TARGET = 544


def _mixer_out(attn, rec, g_a, g_r, w_out, h, g_next):
    tp, d = h.shape
    aw, rw = attn.shape[1], rec.shape[1]
    kc = d // N_CHIPS
    tm = _divisor_tile(tp, 16, ROW_TARGET)

    def body(a_ref, r_ref, ga_ref, gr_ref, w_ref, h_ref, gn_ref, h1_ref, z_ref, mix_ref):
        a = a_ref[...]
        r = r_ref[...]
        mix = jnp.concatenate([a * _rstd(a) * ga_ref[...], r * _rstd(r) * gr_ref[...]], axis=1).astype(BF16)
        mix_ref[...] = mix
        h1 = h_ref[...]
        for j in range(N_CHIPS):
            h1 = h1 + _dot(mix[:, j * kc:(j + 1) * kc], w_ref[j])
        h1_ref[...] = h1
        z_ref[...] = (h1 * _rstd(h1) * gn_ref[...]).astype(BF16)

    row = lambda wd: pl.BlockSpec((tm, wd), lambda i: (i, 0))
    vm = 2 * _nbytes((d, d), BF16) + 12 * _nbytes((tm, d), F32)
    return _call(body, (attn, rec, g_a, g_r, w_out, h, g_next), name="mixer_out", grid=(tp // tm,),
                 in_specs=[row(aw), row(rw), _full((1, aw)), _full((1, rw)), _full(w_out.shape), row(d),
                           _full((1, d))],
                 out_specs=[row(d), row(d), row(d)],
                 out_shape=[jax.ShapeDtypeStruct((tp, d), F32), jax.ShapeDtypeStruct((tp, d), BF16),
                            jax.ShapeDtypeStruct((tp, d), BF16)],
                 semantics=("parallel",), vmem_bytes=vm)


def _mixer_bwd(dh_b, w_out, attn, rec, g_a, g_r):
    tp, d = dh_b.shape
    aw, rw = attn.shape[1], rec.shape[1]
    tm = _divisor_tile(tp, 16, ROW_TARGET)

    def body(dh_ref, w_ref, a_ref, r_ref, ga_ref, gr_ref, da_ref, dr_ref, dg_ref):
        @pl.when(pl.program_id(0) == 0)
        def _():
            dg_ref[...] = jnp.zeros_like(dg_ref)

        dh = dh_ref[...]
        dmix = jnp.concatenate([_dot_nt(dh, w_ref[j]) for j in range(N_CHIPS)], axis=1)
        da, dga = _rms_bwd(dmix[:, :aw], a_ref[...], ga_ref[...])
        dr, dgr = _rms_bwd(dmix[:, aw:], r_ref[...], gr_ref[...])
        da_ref[...] = da
        dr_ref[...] = dr
        dg_ref[...] += jnp.broadcast_to(jnp.concatenate([dga, dgr], axis=1), (SUBLANES, d))

    row = lambda wd: pl.BlockSpec((tm, wd), lambda i: (i, 0))
    vm = 2 * _nbytes((d, d), BF16) + 12 * _nbytes((tm, d), F32)
    return _call(body, (dh_b, w_out, attn, rec, g_a, g_r), name="mixer_bwd", grid=(tp // tm,),
                 in_specs=[row(d), _full(w_out.shape), row(aw), row(rw), _full((1, aw)), _full((1, rw))],
                 out_specs=[row(aw), row(rw), _full((SUBLANES, d))],
                 out_shape=[jax.ShapeDtypeStruct((tp, aw), F32), jax.ShapeDtypeStruct((tp, rw), F32),
                            jax.ShapeDtypeStruct((SUBLANES, d), F32)],
                 semantics=("arbitrary",), vmem_bytes=vm)


def _mlp_up(z, w_up):
    tp, d = z.shape
    fc = w_up.shape[2]
    ff = N_CHIPS * fc
    tn = _divisor_tile(fc, LANES, 512)
    per = fc // tn

    def body(z_ref, w_ref, act_ref, up_ref):
        up = _dot(z_ref[...], w_ref[...])
        r = jnp.maximum(up, 0.0)
        act_ref[...] = (r * r).astype(BF16)
        up_ref[...] = up.astype(BF16)

    col = pl.BlockSpec((tp, tn), lambda j: (0, j))
    vm = 2 * _nbytes((tp, d), BF16) + 2 * _nbytes((d, tn), BF16) + 8 * _nbytes((tp, tn), F32)
    return _call(body, (z, w_up), name="mlp_up", grid=(ff // tn,),
                 in_specs=[_full((tp, d)), pl.BlockSpec((None, d, tn), lambda j: (j // per, 0, j % per))],
                 out_specs=[col, col],
                 out_shape=[jax.ShapeDtypeStruct((tp, ff), BF16)] * 2,
                 semantics=("parallel",), vmem_bytes=vm)


def _mlp_down(act, w_down, h, g_next):
    tp, d = h.shape
    ff = act.shape[1]
    fc = ff // N_CHIPS
    tm = _divisor_tile(tp, 16, ROW_TARGET)

    def body(a_ref, w_ref, h_ref, gn_ref, h2_ref, z_ref):
        h2 = h_ref[...]
        for j in range(N_CHIPS):
            h2 = h2 + _dot(a_ref[:, j * fc:(j + 1) * fc], w_ref[j])
        h2_ref[...] = h2
        z_ref[...] = (h2 * _rstd(h2) * gn_ref[...]).astype(BF16)

    row = lambda wd: pl.BlockSpec((tm, wd), lambda i: (i, 0))
    vm = 2 * _nbytes((ff, d), BF16) + 2 * _nbytes((tm, ff), BF16) + 10 * _nbytes((tm, d), F32)
    return _call(body, (act, w_down, h, g_next), name="mlp_down", grid=(tp // tm,),
                 in_specs=[row(ff), _full(w_down.shape), row(d), _full((1, d))],
                 out_specs=[row(d), row(d)],
                 out_shape=[jax.ShapeDtypeStruct((tp, d), F32), jax.ShapeDtypeStruct((tp, d), BF16)],
                 semantics=("parallel",), vmem_bytes=vm)


def _loss_bwd(h, g, target, n_real):
    tp, d = h.shape
    tm = _divisor_tile(tp, 16, ROW_TARGET)

    def body(h_ref, g_ref, t_ref, dh_ref, dhb_ref, dg_ref, loss_ref):
        i = pl.program_id(0)

        @pl.when(i == 0)
        def _():
            dg_ref[...] = jnp.zeros_like(dg_ref)
            loss_ref[...] = jnp.zeros_like(loss_ref)

        x = h_ref[...]
        gv = g_ref[...]
        rowi = i * tm + lax.broadcasted_iota(jnp.int32, (tm, 1), 0)
        real = jnp.logical_and(rowi >= N_META, rowi < N_META + n_real)
        err = jnp.where(real, x * _rstd(x) * gv - t_ref[...], 0.0)
        loss_ref[...] += 0.5 * jnp.sum(jnp.mean(err * err, axis=-1, keepdims=True))
        dx, dgp = _rms_bwd(err * (1.0 / d), x, gv)
        dh_ref[...] = dx
        dhb_ref[...] = dx.astype(BF16)
        dg_ref[...] += jnp.broadcast_to(dgp, (SUBLANES, d))

    row = pl.BlockSpec((tm, d), lambda i: (i, 0))
    return _call(body, (h, g, target), name="loss_bwd", grid=(tp // tm,),
                 in_specs=[row, _full((1, d)), row],
                 out_specs=[row, row, _full((SUBLANES, d)), _full((SUBLANES, LANES))],
                 out_shape=[jax.ShapeDtypeStruct((tp, d), F32), jax.ShapeDtypeStruct((tp, d), BF16),
                            jax.ShapeDtypeStruct((SUBLANES, d), F32), jax.ShapeDtypeStruct((SUBLANES, LANES), F32)],
                 semantics=("arbitrary",), vmem_bytes=16 * _nbytes((tm, d), F32))


def _mlp_bwd(dh_b, w_down, up, z2):
    tp, d = dh_b.shape
    fc = w_down.shape[1]
    ff = N_CHIPS * fc
    tn = _divisor_tile(fc, LANES, 512)
    per = fc // tn

    def body(dh_ref, z_ref, w_ref, up_ref, dup_ref, gd_ref, gu_ref):
        dh = dh_ref[...]
        r = jnp.maximum(up_ref[...].astype(F32), 0.0)
        dup = (_dot_nt(dh, w_ref[...]) * (2.0 * r)).astype(BF16)
        dup_ref[...] = dup
        gd_ref[...] = _dot_tn((r * r).astype(BF16), dh).astype(BF16)
        gu_ref[...] = _dot_tn(z_ref[...], dup).astype(BF16)

    col = pl.BlockSpec((tp, tn), lambda j: (0, j))
    vm = 4 * _nbytes((tp, d), BF16) + 4 * _nbytes((tn, d), BF16) + 2 * _nbytes((d, tn), BF16) \
        + 10 * _nbytes((tp, tn), F32) + 4 * _nbytes((tn, d), F32)
    return _call(body, (dh_b, z2, w_down, up), name="mlp_bwd", grid=(ff // tn,),
                 in_specs=[_full((tp, d)), _full((tp, d)),
                           pl.BlockSpec((None, tn, d), lambda j: (j // per, j % per, 0)), col],
                 out_specs=[col, pl.BlockSpec((tn, d), lambda j: (j, 0)),
                            pl.BlockSpec((None, d, tn), lambda j: (j // per, 0, j % per))],
                 out_shape=[jax.ShapeDtypeStruct((tp, ff), BF16), jax.ShapeDtypeStruct((ff, d), BF16),
                            jax.ShapeDtypeStruct((N_CHIPS, d, fc), BF16)],
                 semantics=("parallel",), vmem_bytes=vm)


def _grad_w_pieces(pieces, b):
    tp, n = b.shape
    tn = _divisor_tile(n, LANES, 512)
    widths = [pc.shape[1] for pc in pieces]

    def body(*refs):
        p_refs, b_ref, o_refs = refs[:len(pieces)], refs[len(pieces)], refs[len(pieces) + 1:]
        for p_ref, o_ref in zip(p_refs, o_refs):
            o_ref[...] = _dot_tn(p_ref[...], b_ref[...]).astype(BF16)

    vm = 2 * sum(_nbytes((tp, wd), BF16) for wd in widths) + 2 * _nbytes((tp, tn), BF16) \
        + 4 * sum(_nbytes((wd, tn), F32) for wd in widths) + 2 * _nbytes((tp, max(widths)), F32)
    return _call(body, tuple(pieces) + (b,), name="grad_w_pieces", grid=(n // tn,),
                 in_specs=[_full(pc.shape) for pc in pieces] + [pl.BlockSpec((tp, tn), lambda j: (0, j))],
                 out_specs=[pl.BlockSpec((wd, tn), lambda j: (0, j)) for wd in widths],
                 out_shape=[jax.ShapeDtypeStruct((wd, n), BF16) for wd in widths],
                 semantics=("parallel",), vmem_bytes=vm)


def _dx_norm_bwd(pieces, w, w_spec, w_piece, h, g, dres, dot=_dot_nt):
    tp, d = h.shape
    tm = _divisor_tile(tp, 16, ROW_TARGET)
    n = len(pieces)

    def body(*refs):
        dy_refs = refs[:n]
        w_ref, h_ref, g_ref, dres_ref, dh_ref, dhb_ref, dg_ref = refs[n:]

        @pl.when(pl.program_id(0) == 0)
        def _():
            dg_ref[...] = jnp.zeros_like(dg_ref)

        dz = dot(dy_refs[0][...], w_piece(w_ref, 0))
        for i in range(1, n):
            dz = dz + dot(dy_refs[i][...], w_piece(w_ref, i))
        dx, dgp = _rms_bwd(dz, h_ref[...], g_ref[...])
        dh = dres_ref[...] + dx
        dh_ref[...] = dh
        dhb_ref[...] = dh.astype(BF16)
        dg_ref[...] += jnp.broadcast_to(dgp, (SUBLANES, d))

    row = lambda wd: pl.BlockSpec((tm, wd), lambda i: (i, 0))
    kk = sum(wd for _, _, wd in pieces)
    vm = 2 * _nbytes((d, kk), BF16) + 2 * _nbytes((tm, kk), BF16) + 14 * _nbytes((tm, d), F32)
    piece_specs = [pl.BlockSpec((tm, wd), functools.partial(lambda i, cb: (i, cb), cb=cb)) for _, cb, wd in pieces]
    return _call(body, tuple(a for a, _, _ in pieces) + (w, h, g, dres), name="dx_norm_bwd", grid=(tp // tm,),
                 in_specs=piece_specs + [w_spec, row(d), _full((1, d)), row(d)],
                 out_specs=[row(d), row(d), _full((SUBLANES, d))],
                 out_shape=[jax.ShapeDtypeStruct((tp, d), F32), jax.ShapeDtypeStruct((tp, d), BF16),
                            jax.ShapeDtypeStruct((SUBLANES, d), F32)],
                 semantics=("arbitrary",), vmem_bytes=vm)


def _block_diag(wg):
    nb, b, _ = wg.shape
    eye = jnp.eye(nb, dtype=wg.dtype)
    return (eye[:, None, :, None] * wg[:, :, None, :]).reshape(nb * b, nb * b)


def _diag_blocks(dense, nb):
    b = dense.shape[0] // nb
    d4 = dense.reshape(nb, b, nb, b)
    return jnp.stack([d4[i, :, i, :] for i in range(nb)])


def _row(v):
    return v.reshape(1, -1)


def _forward_layer(l, h, z, p, fetch, stage_next, g_next):
    d = h.shape[1]
    att_w = d // 2
    rec_w = d - att_w
    nh = att_w // HEAD_DIM
    wa_d = _block_diag(p["w_gate_a"][l]).astype(BF16)
    wx_d = _block_diag(p["w_gate_x"][l]).astype(BF16)
    b_f_pad = jnp.zeros((1, LANES), F32).at[0, :nh].set(p["b_f"][l])
    w_in_t = fetch("w_in", z)
    big = dict(w_in_big=_pack_w_in_t(w_in_t.reshape(-1, d), att_w, nh))
    qkv, proj = _proj(z, big["w_in_big"], att_w)
    c, c_t = _fgate_fwd(proj, b_f_pad, nh)
    attn, lse_b = _attn_fwd(qkv, c, c_t, nh)
    hr, rec = _rec_fwd(proj, 0, 1, rec_w, p["conv_w"][l], _row(p["conv_b"][l]), wa_d,
                       _row(p["b_gate_a"][l]), wx_d, _row(p["b_gate_x"][l]), _row(p["lru_L"][l]))
    tok = stage_next(attn, "own")
    big["w_out"] = fetch("w_out", rec)
    h1, z2, mix = _mixer_out(attn, rec, _row(p["attn_out_g"][l] + tok), _row(p["rec_out_g"][l]),
                             big["w_out"], h, _row(p["mlp_norm_g"][l]))
    big["w_up"] = fetch("w_up", h1)
    act, up = _mlp_up(z2, big["w_up"])
    tok = stage_next(act, "next")
    big["w_down"] = fetch("w_down", act)
    h2, z_next = _mlp_down(act, big["w_down"], h1, _row(g_next + tok))
    saved = dict(h0=h, z1=z, proj=proj, qkv=qkv, c=c, c_t=c_t, attn=attn, lse_b=lse_b, hr=hr, rec=rec, h1=h1,
                 z2=z2, mix=mix, up=up, wa_d=wa_d, wx_d=wx_d, b_f_pad=b_f_pad, big=big)
    return h2, z_next, saved


def _backward_mlp(l, dh, dh_b, sv, p, tok):
    w_up, w_down = sv["big"]["w_up"], sv["big"]["w_down"]
    fc = w_up.shape[2]
    dup, g_down, g_up = _mlp_bwd(dh_b, w_down, sv["up"], sv["z2"])
    dh, dh_b, dg2 = _dx_norm_bwd([(dup, j, fc) for j in range(N_CHIPS)], w_up, _full(w_up.shape),
                                 lambda w_ref, j: w_ref[j], sv["h1"], _row(p["mlp_norm_g"][l] + tok), dh)
    big = dict(w_down=g_down.reshape((N_CHIPS, -1) + g_down.shape[1:]), w_up=g_up)
    return dh, dh_b, big, dict(mlp_norm_g=dg2[0])


def _backward_mixer(l, dh, dh_b, sv, p, tok):
    d = dh.shape[1]
    att_w = d // 2
    rec_w = d - att_w
    nh = att_w // HEAD_DIM
    small = {}
    g_out, = _grad_w_pieces([sv["mix"]], dh_b)
    dattn, drec, dg_mix = _mixer_bwd(dh_b, sv["big"]["w_out"], sv["attn"], sv["rec"],
                                     _row(p["attn_out_g"][l] + tok), _row(p["rec_out_g"][l]))
    small["attn_out_g"] = dg_mix[0, :att_w]
    small["rec_out_g"] = dg_mix[0, att_w:]
    dxr, dyr, dwa, dwx, sm = _rec_bwd(
        sv["proj"], 0, 1, rec_w, sv["hr"], drec, p["conv_w"][l], _row(p["conv_b"][l]), sv["wa_d"],
        _row(p["b_gate_a"][l]), sv["wx_d"], _row(p["b_gate_x"][l]), _row(p["lru_L"][l]))
    small.update(conv_w=sm[:CONV_WIDTH], conv_b=sm[4], b_gate_a=sm[5], b_gate_x=sm[6], lru_L=sm[7],
                 w_gate_a=_diag_blocks(dwa, N_REC_BLOCKS), w_gate_x=_diag_blocks(dwx, N_REC_BLOCKS))
    dq, dk, dv, dc = _attn_bwd(sv["qkv"], sv["c"], sv["c_t"], sv["lse_b"], dattn, nh)
    df, db_f = _fgate_bwd(sv["proj"], sv["b_f_pad"], dc)
    small["b_f"] = db_f[0, :nh]
    pieces = [dq, dk, dv, dxr, dyr, df]
    offs = [0, att_w, 2 * att_w, 3 * att_w, 3 * att_w + rec_w, 3 * att_w + 2 * rec_w]
    gq, gk, gv, gxr, gyr, gf = _grad_w_pieces(pieces, sv["z1"])
    g_in_t = jnp.concatenate([gq, gk, gv, gf[:nh], gxr, gyr], axis=0)
    w_big = sv["big"]["w_in_big"]
    widths = [pc.shape[1] for pc in pieces]
    dh, dh_b, dg1 = _dx_norm_bwd(
        [(pc, 0, wd) for pc, wd in zip(pieces, widths)], w_big, _full(w_big.shape),
        lambda w_ref, i: w_ref[offs[i]:offs[i] + widths[i], :], sv["h0"], _row(p["attn_norm_g"][l]), dh, dot=_dot)
    small["attn_norm_g"] = dg1[0]
    big = dict(w_in=g_in_t.reshape(N_CHIPS, -1, d), w_out=g_out.reshape((N_CHIPS, -1) + g_out.shape[1:]))
    return dh, dh_b, big, small


def _pack_w_in_t(w_in_t, att_w, nh):
    qkv = w_in_t[:3 * att_w]
    f = w_in_t[3 * att_w:3 * att_w + nh]
    xy = w_in_t[3 * att_w + nh:]
    return jnp.concatenate([qkv, xy, f, jnp.zeros((LANES - nh, w_in_t.shape[1]), w_in_t.dtype)], axis=0)


ANY = pl.BlockSpec(memory_space=pl.ANY)


def _coords():
    return lax.axis_index("x"), lax.axis_index("y"), lax.axis_index("c")


def _other_chips(x, y):
    return [(1 - x, y), (x, 1 - y), (1 - x, 1 - y)]


def _remote(src, dst, send_sems, recv_sems, k, to):
    return pltpu.make_async_remote_copy(src_ref=src, dst_ref=dst, send_sem=send_sems.at[k],
                                        recv_sem=recv_sems.at[k], device_id=to, device_id_type=MESH)


def _all_gather_chips(shards):
    n = len(shards)
    per = 6

    def body(*refs):
        ins, outs = refs[:n], refs[n:2 * n]
        send_sems, recv_sems, local_sems = refs[2 * n:]
        x, y, c = _coords()
        me = 2 * x + y
        sibling = (x, y, 1 - c)
        chips = _other_chips(x, y)
        local = [pltpu.make_async_copy(ins[t], outs[t].at[me], local_sems.at[t]) for t in range(n)]
        for cp in local:
            cp.start()
        sends = []
        for t in range(n):
            for j, (px, py) in enumerate(chips):
                cp = _remote(ins[t].at[c], outs[t].at[me, c], send_sems, recv_sems, per * t + j, (px, py, c))
                cp.start()
                sends.append(cp)
        for t in range(n):
            for j, (px, py) in enumerate(chips):
                landed = outs[t].at[2 * px + py, c]
                _remote(landed, landed, send_sems, recv_sems, per * t + j, (px, py, c)).wait_recv()
                cp = _remote(landed, landed, send_sems, recv_sems, per * t + 3 + j, sibling)
                cp.start()
                sends.append(cp)
        for t in range(n):
            for j, (px, py) in enumerate(chips):
                passed = outs[t].at[2 * px + py, 1 - c]
                _remote(passed, passed, send_sems, recv_sems, per * t + 3 + j, sibling).wait_recv()
        for cp in sends:
            cp.wait_send()
        for cp in local:
            cp.wait()

    return _call(body, tuple(shards), name="all_gather_chips",
                 in_specs=[ANY] * n, out_specs=[ANY] * n,
                 out_shape=[jax.ShapeDtypeStruct((N_CHIPS,) + s.shape, s.dtype) for s in shards],
                 scratch_shapes=[pltpu.SemaphoreType.DMA((per * n,)), pltpu.SemaphoreType.DMA((per * n,)),
                                 pltpu.SemaphoreType.DMA((n,))])


HBM = pl.BlockSpec(memory_space=pltpu.HBM)
SEM = pl.BlockSpec(memory_space=pltpu.SEMAPHORE)
DATAFLOW = pltpu.SideEffectType.DATAFLOW_SIDE_EFFECTING


def _in_hbm(a):
    return pltpu.with_memory_space_constraint(a, pltpu.HBM)


PUSH_ARRIVALS = {"gather_chips_half": N_CHIPS - 1, "pass_halves": N_CHIPS - 1, "scatter_chips": N_CHIPS - 1,
                 "sibling": 1, "gather_devices": N_DEV - 1}


def _column_half(ref3, slab, c):
    hw = ref3.shape[2] // 2
    return ref3.at[slab, :, pl.ds(pl.multiple_of(c * hw, LANES), hw)]


def _push_copies(mode, src, land, send_sems, recv_sems, t):
    x, y, c = _coords()
    chip = 2 * x + y
    if mode == "gather_chips_half":
        return [_remote(_column_half(src, chip, c), _column_half(land, chip, c), send_sems, recv_sems, t, (px, py, c))
                for px, py in _other_chips(x, y)]
    if mode == "pass_halves":
        return [_remote(_column_half(src, 2 * px + py, c), _column_half(land, 2 * px + py, c), send_sems, recv_sems, t,
                        (x, y, 1 - c)) for px, py in _other_chips(x, y)]
    if mode == "scatter_chips":
        return [_remote(src.at[2 * px + py], land.at[chip], send_sems, recv_sems, t, (px, py, c))
                for px, py in _other_chips(x, y)]
    if mode == "sibling":
        return [_remote(src, land, send_sems, recv_sems, t, (x, y, 1 - c))]
    dev = 4 * x + 2 * y + c
    return [_remote(src.at[dev], land.at[dev], send_sems, recv_sems, t, (x ^ (k >> 2), y ^ ((k >> 1) & 1), c ^ (k & 1)))
            for k in range(1, N_DEV)]


def _push_start(srcs, lands, mode, name):
    n = len(srcs)
    same = all(s is ld for s, ld in zip(srcs, lands))
    n_in = n if same else 2 * n

    def body(*refs):
        src_refs = refs[:n]
        land_refs = src_refs if same else refs[n:2 * n]
        send_sems, recv_sems = refs[n_in], refs[n_in + 1]
        token = refs[-1]
        for t in range(n):
            for cp in _push_copies(mode, src_refs[t], land_refs[t], send_sems, recv_sems, t):
                cp.start()
        token[...] = jnp.zeros_like(token)

    operands = tuple(srcs) if same else tuple(srcs) + tuple(lands)
    res = _call(
        body, [_in_hbm(a) for a in operands], name=name,
        out_shape=(pltpu.SemaphoreType.DMA((n,)), pltpu.SemaphoreType.DMA((n,)))
        + tuple(pltpu.HBM(a.shape, a.dtype) for a in operands) + (jax.ShapeDtypeStruct((SUBLANES, LANES), F32),),
        in_specs=[HBM] * n_in, out_specs=(SEM, SEM) + (HBM,) * n_in + (pl.BlockSpec(memory_space=pltpu.VMEM),),
        input_output_aliases={i: 2 + i for i in range(n_in)}, side_effects=DATAFLOW, hbm_results=False)
    send_sems, recv_sems, token = res[0], res[1], res[-1]
    srcs_thru = res[2:2 + n]
    lands_thru = srcs_thru if same else res[2 + n:2 + 2 * n]
    return send_sems, recv_sems, srcs_thru, lands_thru, token


def _push_wait(send_sems, recv_sems, ids, srcs, lands, mode, after, name):
    n = len(lands)
    same = all(s is ld for s, ld in zip(srcs, lands))
    n_in = n if same else 2 * n

    def body(*refs):
        land_refs = refs[:n] if same else refs[n:2 * n]
        send_sems, recv_sems = refs[n_in], refs[n_in + 1]
        x, y, c = _coords()
        for t in range(n):
            if mode == "sibling":
                moved = land_refs[t]
            elif mode in ("gather_chips_half", "pass_halves"):
                moved = land_refs[t].at[pl.ds(0, PUSH_ARRIVALS[mode]), :, pl.ds(0, land_refs[t].shape[2] // 2)]
            else:
                moved = land_refs[t].at[pl.ds(0, PUSH_ARRIVALS[mode])]
            arrivals = _remote(moved, moved, send_sems, recv_sems, ids[t], (x, y, c))
            arrivals.wait_send()
            arrivals.wait_recv()

    operands = tuple(lands) if same else tuple(srcs) + tuple(lands)
    res = _call(
        body, operands + (send_sems, recv_sems, after), name=name,
        out_shape=tuple(pltpu.HBM(a.shape, a.dtype) for a in operands),
        in_specs=[HBM] * n_in + [SEM, SEM, ANY], out_specs=(HBM,) * n_in,
        input_output_aliases={i: i for i in range(n_in)}, side_effects=DATAFLOW)
    return list(res) if same else (list(res[:n]), list(res[n:]))


def _sum_partials(part, landed, chip):
    _, rows, cols = part.shape
    br = _divisor_tile(rows, 16, ELEM_ROWS)

    def body(chip_ref, own_ref, a_ref, b_ref, c_ref, o_ref):
        o_ref[...] = ((own_ref[...].astype(F32) + a_ref[...].astype(F32)) + b_ref[...].astype(F32)) \
            + c_ref[...].astype(F32)

    def other(k):
        return pl.BlockSpec((None, br, cols), lambda i, ch: (jnp.where(ch[0] <= k, k + 1, k), i, 0))

    spec = pltpu.PrefetchScalarGridSpec(
        num_scalar_prefetch=1, grid=(rows // br,),
        in_specs=[pl.BlockSpec((None, br, cols), lambda i, ch: (ch[0], i, 0)), other(0), other(1), other(2)],
        out_specs=pl.BlockSpec((br, cols), lambda i, ch: (i, 0)))
    return _call(body, (chip, part, landed, landed, landed), name="sum_partials", grid_spec=spec,
                 out_shape=jax.ShapeDtypeStruct((rows, cols), F32), semantics=("parallel",))


def _cast_to_slab(w, l, chip):
    _, rows, cols = w.shape
    br = _divisor_tile(rows, 16, ELEM_ROWS)

    def body(chip_ref, w_ref, o_ref):
        o_ref[...] = w_ref[...].astype(BF16)

    spec = pltpu.PrefetchScalarGridSpec(
        num_scalar_prefetch=1, grid=(rows // br,),
        in_specs=[pl.BlockSpec((None, br, cols), lambda i, ch: (l, i, 0))],
        out_specs=pl.BlockSpec((None, br, cols), lambda i, ch: (ch[0], i, 0)))
    return _call(body, (chip, w), name="cast_to_slab", grid_spec=spec,
                 out_shape=jax.ShapeDtypeStruct((N_CHIPS, rows, cols), BF16), semantics=("parallel",))


def _cast_w_in_t_to_slabs(w_t, chip):
    rows, depth, d = w_t.shape
    tn = _divisor_tile(d, LANES, 256)

    def body(chip_ref, w_ref, *o_refs):
        for l in range(depth):
            o_refs[l][...] = w_ref[:, l, :].astype(BF16)

    spec = pltpu.PrefetchScalarGridSpec(
        num_scalar_prefetch=1, grid=(d // tn,),
        in_specs=[pl.BlockSpec((rows, depth, tn), lambda j, ch: (0, 0, j))],
        out_specs=[pl.BlockSpec((None, rows, tn), lambda j, ch: (ch[0], 0, j))] * depth)
    return _call(body, (chip, w_t), name="cast_w_in_t_to_slabs", grid_spec=spec,
                 out_shape=[jax.ShapeDtypeStruct((N_CHIPS, rows, d), BF16)] * depth, semantics=("parallel",),
                 vmem_bytes=4 * _nbytes((rows, max(depth, SUBLANES), tn), F32))


def _place_slab(buf, index, n_slabs):
    rows, cols = buf.shape
    br = _divisor_tile(rows, SUBLANES, ELEM_ROWS)

    def body(index_ref, b_ref, o_ref):
        o_ref[...] = b_ref[...]

    spec = pltpu.PrefetchScalarGridSpec(
        num_scalar_prefetch=1, grid=(rows // br,),
        in_specs=[pl.BlockSpec((br, cols), lambda i, ix: (i, 0))],
        out_specs=pl.BlockSpec((None, br, cols), lambda i, ix: (ix[0], i, 0)))
    return _call(body, (index, buf), name="place_slab", grid_spec=spec,
                 out_shape=jax.ShapeDtypeStruct((n_slabs, rows, cols), buf.dtype), semantics=("parallel",))


ELEM_ROWS = 256


def _sum_slabs(r):
    n, rows, cols = r.shape
    br = _divisor_tile(rows, 16, ELEM_ROWS)

    def body(r_ref, o_ref):
        acc = r_ref[0].astype(F32)
        for j in range(1, n):
            acc = acc + r_ref[j].astype(F32)
        o_ref[...] = acc

    return _call(body, (r,), name="sum_slabs", grid=(rows // br,),
                 in_specs=[pl.BlockSpec((n, br, cols), lambda i: (0, i, 0))],
                 out_specs=pl.BlockSpec((br, cols), lambda i: (i, 0)),
                 out_shape=jax.ShapeDtypeStruct((rows, cols), F32), semantics=("parallel",))


def _adamw_math(w, g, m, v):
    c1 = 1.0 - ADAM_B1 ** ADAM_STEP
    c2 = 1.0 - ADAM_B2 ** ADAM_STEP
    nm = ADAM_B1 * m + (1.0 - ADAM_B1) * g
    nv = ADAM_B2 * v + (1.0 - ADAM_B2) * (g * g)
    delta = -ADAM_LR * ((nm / c1) / (jnp.sqrt(nv / c2) + ADAM_EPS) + ADAM_WD * w)
    return delta, nm, nv


def _adamw(w, g, m, v):
    rows, cols = w.shape
    br = _divisor_tile(rows, 8, ELEM_ROWS)

    def body(w_ref, g_ref, m_ref, v_ref, d_ref, nm_ref, nv_ref):
        d_ref[...], nm_ref[...], nv_ref[...] = _adamw_math(w_ref[...], g_ref[...], m_ref[...], v_ref[...])

    blk = pl.BlockSpec((br, cols), lambda i: (i, 0))
    return _call(body, (w, g, m, v), name="adamw", grid=(rows // br,),
                 in_specs=[blk] * 4, out_specs=[blk] * 3,
                 out_shape=[jax.ShapeDtypeStruct((rows, cols), F32)] * 3, semantics=("parallel",))


def _adamw_w_in_t(w_t, m_t, v_t, g_mine, g_theirs):
    rows, depth, d = w_t.shape
    tn = LANES

    def body(w_ref, m_ref, v_ref, ga_ref, gb_ref, g_ref, d_ref, nm_ref, nv_ref):
        g = ga_ref[...] + gb_ref[...]
        g_ref[...] = g
        d_ref[...], nm_ref[...], nv_ref[...] = _adamw_math(w_ref[...], g, m_ref[...], v_ref[...])

    slab = pl.BlockSpec((rows, depth, tn), lambda j: (0, 0, j))
    return _call(body, (w_t, m_t, v_t, g_mine, g_theirs), name="adamw_w_in_t", grid=(d // tn,),
                 in_specs=[slab] * 5, out_specs=[slab] * 4,
                 out_shape=[jax.ShapeDtypeStruct(w_t.shape, F32)] * 4, semantics=("parallel",),
                 vmem_bytes=2 * 9 * _nbytes((rows, max(depth, SUBLANES), tn), F32))


def _adamw_layer(w, m, v, l, g_mine, g_theirs, prev, after):
    _, rows, cols = w.shape
    br = _divisor_tile(rows, 8, ELEM_ROWS)

    def body(w_ref, m_ref, v_ref, ga_ref, gb_ref, *rest):
        g_ref, d_ref, nm_ref, nv_ref = rest[5:]
        g = ga_ref[...] + gb_ref[...]
        g_ref[...] = g
        d_ref[...], nm_ref[...], nv_ref[...] = _adamw_math(w_ref[...], g, m_ref[...], v_ref[...])

    slot = pl.BlockSpec((None, br, cols), lambda i: (l, i, 0))
    blk = pl.BlockSpec((br, cols), lambda i: (i, 0))
    return _call(body, (w, m, v, g_mine, g_theirs) + tuple(prev) + (after,), name="adamw_layer",
                 grid=(rows // br,), in_specs=[slot] * 3 + [blk] * 2 + [ANY] * 5, out_specs=[slot] * 4,
                 out_shape=[jax.ShapeDtypeStruct(w.shape, F32)] * 4,
                 input_output_aliases={5: 0, 6: 1, 7: 2, 8: 3}, semantics=("parallel",))


BIG = ("w_in", "w_out", "w_up", "w_down")
WEIGHTS = ("meta", "attn_norm_g", "w_in", "b_f", "conv_w", "conv_b", "w_gate_a", "b_gate_a", "w_gate_x",
           "b_gate_x", "lru_L", "attn_out_g", "rec_out_g", "w_out", "mlp_norm_g", "w_up", "w_down", "final_g")
SMALL = tuple(k for k in WEIGHTS if k not in BIG)
COL_SHARDED_SMALL = ("meta", "conv_w")


def _packed_rows(shape):
    return -(-math.prod(shape) // (SUBLANES * LANES)) * SUBLANES


def _pack(arrs):
    rows = []
    for a in arrs:
        flat = a.reshape(-1)
        rows.append(jnp.pad(flat, (0, _packed_rows(a.shape) * LANES - flat.shape[0])).reshape(-1, LANES))
    used = sum(r.shape[0] for r in rows)
    rows.append(jnp.zeros((-used % ELEM_ROWS, LANES), F32))
    return jnp.concatenate(rows, axis=0)


def _unpack(buf, shapes):
    out, r0 = [], 0
    for s in shapes:
        nr = _packed_rows(s)
        out.append(buf[r0:r0 + nr].reshape(-1)[:math.prod(s)].reshape(s))
        r0 += nr
    return out


def _halves(a):
    return a.reshape((2, a.shape[0] // 2) + a.shape[1:])


def _cols_from_chips(g):
    return jnp.moveaxis(g, 0, -2).reshape(g.shape[1:-1] + (N_CHIPS * g.shape[-1],))


def kernel(x, meta, attn_norm_g, w_in, b_f, conv_w, conv_b, w_gate_a, b_gate_a, w_gate_x, b_gate_x, lru_L, attn_out_g, rec_out_g, w_out, mlp_norm_g, w_up, w_down, final_g, loss_target, m_meta, m_attn_norm_g, m_w_in, m_b_f, m_conv_w, m_conv_b, m_w_gate_a, m_b_gate_a, m_w_gate_x, m_b_gate_x, m_lru_L, m_attn_out_g, m_rec_out_g, m_w_out, m_mlp_norm_g, m_w_up, m_w_down, m_final_g, v_meta, v_attn_norm_g, v_w_in, v_b_f, v_conv_w, v_conv_b, v_w_gate_a, v_b_gate_a, v_w_gate_x, v_b_gate_x, v_lru_L, v_attn_out_g, v_rec_out_g, v_w_out, v_mlp_norm_g, v_w_up, v_w_down, v_final_g):
    w = dict(meta=meta, attn_norm_g=attn_norm_g, w_in=w_in, b_f=b_f, conv_w=conv_w, conv_b=conv_b,
             w_gate_a=w_gate_a, b_gate_a=b_gate_a, w_gate_x=w_gate_x, b_gate_x=b_gate_x, lru_L=lru_L,
             attn_out_g=attn_out_g, rec_out_g=rec_out_g, w_out=w_out, mlp_norm_g=mlp_norm_g, w_up=w_up,
             w_down=w_down, final_g=final_g)
    m = dict(meta=m_meta, attn_norm_g=m_attn_norm_g, w_in=m_w_in, b_f=m_b_f, conv_w=m_conv_w, conv_b=m_conv_b,
             w_gate_a=m_w_gate_a, b_gate_a=m_b_gate_a, w_gate_x=m_w_gate_x, b_gate_x=m_b_gate_x, lru_L=m_lru_L,
             attn_out_g=m_attn_out_g, rec_out_g=m_rec_out_g, w_out=m_w_out, mlp_norm_g=m_mlp_norm_g,
             w_up=m_w_up, w_down=m_w_down, final_g=m_final_g)
    v = dict(meta=v_meta, attn_norm_g=v_attn_norm_g, w_in=v_w_in, b_f=v_b_f, conv_w=v_conv_w, conv_b=v_conv_b,
             w_gate_a=v_w_gate_a, b_gate_a=v_b_gate_a, w_gate_x=v_w_gate_x, b_gate_x=v_b_gate_x, lru_L=v_lru_L,
             attn_out_g=v_attn_out_g, rec_out_g=v_rec_out_g, w_out=v_w_out, mlp_norm_g=v_mlp_norm_g,
             w_up=v_w_up, w_down=v_w_down, final_g=v_final_g)
    s_len, d = x.shape[1], x.shape[2]
    depth = w_in.shape[0]
    att_w = d // 2
    rec_w = d - att_w
    nh = att_w // HEAD_DIM
    chip = 2 * lax.axis_index("x") + lax.axis_index("y")

    g_conv, g_meta = [g.reshape((N_CHIPS, g.shape[1] * g.shape[2]) + g.shape[3:])
                      for g in _all_gather_chips([_halves(w["conv_w"]), _halves(w["meta"])])]
    p = dict(w)
    p["conv_w"] = _cols_from_chips(g_conv)
    meta_full = jnp.moveaxis(g_meta, 0, 1).reshape(N_META, d)

    chip1 = chip.reshape(1).astype(jnp.int32)
    w_in_t, m_in_t, v_in_t = [jnp.transpose(a["w_in"], (2, 0, 1)) for a in (w, m, v)]
    w_in_slabs = _cast_w_in_t_to_slabs(w_in_t, chip1)
    pushes, tokens = [], []
    for l in range(depth):
        slabs = [w_in_slabs[l]] + [_cast_to_slab(w[k], l, chip1) for k in BIG[1:]]
        send_sems, recv_sems, _, lands, token = _push_start(slabs, slabs, "gather_chips_half", f"weights_start_{l}")
        pushes.append((send_sems, recv_sems, lands))
        tokens.append(token[0, 0])
    passed = [{} for _ in range(depth)]

    def stage(l, after, ids):
        send_sems, recv_sems, lands = pushes[l]
        tag = "_".join(BIG[i] for i in ids)
        sub = [lands[i] for i in ids]
        sub = _push_wait(send_sems, recv_sems, ids, sub, sub, "gather_chips_half", after, f"{tag}_wait_{l}")
        send_sems, recv_sems, _, sub, token = _push_start(sub, sub, "pass_halves", f"{tag}_pass_{l}")
        for j, i in enumerate(ids):
            passed[l][i] = (send_sems, recv_sems, sub[j], j)
        return token[0, 0]

    t_len = N_META + s_len
    pad = -t_len % SEQ_TILE
    h = jnp.concatenate([meta_full, x[0], jnp.zeros((pad, d), F32)], axis=0)
    tgt = jnp.concatenate([jnp.zeros((N_META, d), F32), loss_target[0], jnp.zeros((pad, d), F32)], axis=0)
    z = _rms_fwd(h, _row(p["attn_norm_g"][0] + sum(tokens)))
    stage(0, z, [0])
    saved = []
    for l in range(depth):
        def fetch(k, after, l=l):
            send_sems, recv_sems, land, j = passed[l][BIG.index(k)]
            return _push_wait(send_sems, recv_sems, [j], [land], [land], "pass_halves", after,
                              f"{k}_here_{l}")[0]

        def stage_next(after, which, l=l):
            if which == "own":
                return stage(l, after, [1, 2, 3])
            return stage(l + 1, after, [0]) if l + 1 < depth else 0.0

        g_next = p["attn_norm_g"][l + 1] if l + 1 < depth else p["final_g"]
        h, z, sv = _forward_layer(l, h, z, p, fetch, stage_next, g_next)
        saved.append(sv)
    dh, dh_b, dg_final, loss_part = _loss_bwd(h, _row(p["final_g"]), tgt, s_len)

    small = {k: [None] * depth for k in SMALL if k not in ("meta", "final_g")}
    pushes = [None] * depth
    tok = 0.0
    for l in reversed(range(depth)):
        dh, dh_b, big_mlp, sm_mlp = _backward_mlp(l, dh, dh_b, saved[l], p, tok)
        parts = [big_mlp["w_down"], big_mlp["w_up"]]
        push_mlp = _push_start(parts, [lax.empty(a.shape, a.dtype) for a in parts], "scatter_chips",
                               f"mlp_grads_start_{l}")
        dh, dh_b, big_mix, sm_mix = _backward_mixer(l, dh, dh_b, saved[l], p, push_mlp[4][0, 0])
        parts = [big_mix["w_out"], big_mix["w_in"]]
        push_mix = _push_start(parts, [lax.empty(a.shape, a.dtype) for a in parts], "scatter_chips",
                               f"mixer_grads_start_{l}")
        tok = push_mix[4][0, 0]
        pushes[l] = {("w_down", "w_up"): push_mlp, ("w_out", "w_in"): push_mix}
        for k, val in {**sm_mlp, **sm_mix}.items():
            small[k][l] = val
    grads = {k: jnp.stack(val) for k, val in small.items()}
    grads["final_g"] = dg_final[0]
    grads["meta"] = dh[:N_META]
    dx = dh[N_META:t_len]

    full_shapes = [grads[k].shape for k in SMALL] + [(1,)]
    packed = _pack([grads[k].astype(F32) for k in SMALL] + [loss_part[0, :1] + tok])
    dev1 = (2 * chip + lax.axis_index("c")).reshape(1).astype(jnp.int32)
    slabs = [_place_slab(packed, dev1, N_DEV)]
    small_push = _push_start(slabs, slabs, "gather_devices", "small_grads_start")

    last_token = small_push[4]
    outs = {k: [lax.empty(w[k].shape, F32) for _ in range(4)] for k in BIG[1:]}
    w_in_sums = [None] * depth
    swaps = {}

    def finish(l, wait_after, adam_after):
        send_sems, recv_sems, mine, lands, _ = swaps[l]
        mine, theirs = _push_wait(send_sems, recv_sems, list(range(len(BIG))), mine, lands, "sibling", wait_after,
                                  f"sums_wait_{l}")
        w_in_sums[l] = (mine[0], theirs[0])
        for k, a, b in zip(BIG[1:], mine[1:], theirs[1:]):
            outs[k] = _adamw_layer(w[k], m[k], v[k], l, a, b, outs[k], adam_after)

    for l in reversed(range(depth)):
        sums = {}
        for names, (send_sems, recv_sems, parts, lands, _) in pushes[l].items():
            parts, landed = _push_wait(send_sems, recv_sems, [0, 1], parts, lands, "scatter_chips", last_token,
                                       f"{names[0]}_grads_wait_{l}")
            for k, part, land in zip(names, parts, landed):
                sums[k] = _sum_partials(part, land, chip1)
        mine = [sums[k] for k in BIG]
        swaps[l] = _push_start(mine, [lax.empty(a.shape, a.dtype) for a in mine], "sibling", f"sums_start_{l}")
        if l + 1 < depth:
            finish(l + 1, outs["w_down"][0] if l + 2 < depth else mine[0], swaps[l][4])
    finish(0, outs["w_down"][0] if depth > 1 else swaps[0][4], swaps[0][4])
    outs["w_in"] = [jnp.transpose(r, (1, 2, 0)) for r in _adamw_w_in_t(
        w_in_t, m_in_t, v_in_t, jnp.stack([s[0] for s in w_in_sums], axis=1),
        jnp.stack([s[1] for s in w_in_sums], axis=1))]
    out_g, out_d, out_m, out_v = [{k: outs[k][i] for k in BIG} for i in range(4)]

    landed = _push_wait(small_push[0], small_push[1], [0], small_push[3], small_push[3], "gather_devices",
                        out_g["w_in"], "small_grads_wait")
    total = _sum_slabs(landed[0])
    small_g = dict(zip(SMALL + ("loss",), _unpack(total, full_shapes)))
    for k in COL_SHARDED_SMALL:
        n = w[k].shape[-1]
        small_g[k] = lax.dynamic_slice_in_dim(small_g[k], chip * n, n, axis=small_g[k].ndim - 1)
    local_shapes = [w[k].shape for k in SMALL]
    res = _adamw(_pack([w[k] for k in SMALL]), _pack([small_g[k] for k in SMALL]),
                 _pack([m[k] for k in SMALL]), _pack([v[k] for k in SMALL]))
    out_g.update({k: small_g[k] for k in SMALL})
    for dst, buf in zip((out_d, out_m, out_v), res):
        dst.update(zip(SMALL, _unpack(buf, local_shapes)))

    return (small_g["loss"].reshape(()), dx[None],
            *[out_g[k] for k in WEIGHTS], *[out_d[k] for k in WEIGHTS],
            *[out_m[k] for k in WEIGHTS], *[out_v[k] for k in WEIGHTS])
```

```python
import functools
import math

import jax
import jax.numpy as jnp
from jax import lax
from jax.experimental import pallas as pl
from jax.experimental.pallas import tpu as pltpu

F32 = jnp.float32
BF16 = jnp.bfloat16

N_META = 16
HEAD_DIM = 64
N_REC_BLOCKS = 8
CONV_WIDTH = 4
RG_C = 8.0
NORM_EPS = 1e-6
ADAM_LR = 0.001
ADAM_B1 = 0.9
ADAM_B2 = 0.999
ADAM_EPS = 1e-08
ADAM_WD = 0.01
ADAM_STEP = 10

LANES = 128
SUBLANES = 8
SEQ_TILE = 128
VMEM_CAP = 60 * 2**20
VMEM_SLACK = 6 * 2**20
NEG_BIG = -1e30
N_CHIPS = 4
N_DEV = 8
MESH = pl.DeviceIdType.MESH


def _nbytes(shape, dtype):
    return math.prod(shape) * jnp.dtype(dtype).itemsize


def _call(body, args, *, name, out_shape, grid=(), in_specs=None, out_specs=None, scratch_shapes=(),
          grid_spec=None, semantics=None, vmem_bytes=None, side_effects=None, hbm_results=True, **kw):
    cp = {}
    if semantics is not None:
        cp["dimension_semantics"] = semantics
    if vmem_bytes is not None:
        cp["vmem_limit_bytes"] = int(min(VMEM_CAP, vmem_bytes + VMEM_SLACK))
    if side_effects is not None:
        cp["has_side_effects"] = side_effects
    if grid_spec is not None:
        kw["grid_spec"] = grid_spec
    else:
        kw.update(grid=grid, in_specs=in_specs, out_specs=out_specs, scratch_shapes=scratch_shapes)
    if hbm_results:
        out_shape = jax.tree.map(
            lambda s: pltpu.HBM(s.shape, s.dtype) if isinstance(s, jax.ShapeDtypeStruct) else s, out_shape)
    fn = pl.pallas_call(
        body, name=name, out_shape=out_shape,
        compiler_params=pltpu.CompilerParams(**cp), **kw)
    return fn(*[_in_hbm(a) if jnp.issubdtype(getattr(a, "dtype", jnp.int32), jnp.floating) else a for a in args])


def _divisor_tile(n, unit, target):
    best = None
    for t in range(unit, min(n, target) + 1, unit):
        if n % t == 0:
            best = t
    return n if best is None else best


def _sigmoid(x):
    return 1.0 / (1.0 + jnp.exp(-x))


def _log1p_unit(e):
    series = e * (1.0 - e * (0.5 - e * (1.0 / 3.0)))
    return jnp.where(e < 1e-2, series, jnp.log(1.0 + e))


def _log_sigmoid(x):
    return jnp.minimum(x, 0.0) - _log1p_unit(jnp.exp(-jnp.abs(x)))


def _one_minus_exp(x, exp_x):
    small = -x * (1.0 + x * (1.0 / 2 + x * (1.0 / 6 + x * (1.0 / 24 + x * (1.0 / 120 + x * (1.0 / 720))))))
    return jnp.where(x > -0.25, small, 1.0 - exp_x)


_GELU_K = math.sqrt(2.0 / math.pi)
_GELU_C = 0.044715


def _gelu_and_grad(y):
    th = jnp.tanh(_GELU_K * (y + _GELU_C * y * y * y))
    g = 0.5 * y * (1.0 + th)
    dg = 0.5 * (1.0 + th) + 0.5 * y * (1.0 - th * th) * _GELU_K * (1.0 + 3.0 * _GELU_C * y * y)
    return g, dg


def _rstd(x):
    return lax.rsqrt(jnp.mean(x * x, axis=-1, keepdims=True) + NORM_EPS)


def _rms_bwd(dz, x, g):
    rs = _rstd(x)
    xh = x * rs
    dgp = jnp.sum(dz * xh, axis=0, keepdims=True)
    dxh = dz * g
    dx = rs * (dxh - xh * jnp.mean(dxh * xh, axis=-1, keepdims=True))
    return dx, dgp


def _dot(a, b):
    return jnp.dot(a, b, preferred_element_type=F32)


def _dot_nt(a, b):
    return lax.dot_general(a, b, (((1,), (1,)), ((), ())), preferred_element_type=F32)


def _dot_tn(a, b):
    return lax.dot_general(a, b, (((0,), (0,)), ((), ())), preferred_element_type=F32)


def _full(shape):
    nd = len(shape)
    return pl.BlockSpec(shape, lambda *_: (0,) * nd)


def _rms_fwd(h, g):
    tp, d = h.shape
    tm = _divisor_tile(tp, 16, 544)

    def body(h_ref, g_ref, z_ref):
        x = h_ref[...]
        z_ref[...] = (x * _rstd(x) * g_ref[...]).astype(BF16)

    return _call(body, (h, g), name="rms_fwd", grid=(tp // tm,),
                 in_specs=[pl.BlockSpec((tm, d), lambda i: (i, 0)), _full((1, d))],
                 out_specs=pl.BlockSpec((tm, d), lambda i: (i, 0)),
                 out_shape=jax.ShapeDtypeStruct((tp, d), BF16), semantics=("parallel",))


def _proj(z, w_big_t, att_w):
    tp, d = z.shape
    nb = w_big_t.shape[0]
    tn = _divisor_tile(nb, LANES, 512)
    assert (3 * att_w) % tn == 0
    n_qkv = 3 * att_w // tn
    scale = 1.0 / math.sqrt(HEAD_DIM)

    def body(z_ref, w_ref, qkv_ref, p_ref):
        j = pl.program_id(0)
        acc = _dot_nt(z_ref[...], w_ref[...])

        @pl.when(j < n_qkv)
        def _():
            col = j * tn + lax.broadcasted_iota(jnp.int32, (1, tn), 1)
            qkv_ref[...] = (acc * jnp.where(col < att_w, scale, 1.0)).astype(BF16)

        @pl.when(j >= n_qkv)
        def _():
            p_ref[...] = acc

    vm = 2 * (_nbytes((tp, d), BF16) + _nbytes((d, tn), BF16) + _nbytes((tp, tn), F32) * 2)
    return _call(body, (z, w_big_t), name="proj", grid=(nb // tn,),
                 in_specs=[_full((tp, d)), pl.BlockSpec((tn, d), lambda j: (j, 0))],
                 out_specs=[pl.BlockSpec((tp, tn), lambda j: (0, jnp.minimum(j, n_qkv - 1))),
                            pl.BlockSpec((tp, tn), lambda j: (0, jnp.maximum(j - n_qkv, 0)))],
                 out_shape=[jax.ShapeDtypeStruct((tp, 3 * att_w), BF16),
                            jax.ShapeDtypeStruct((tp, nb - 3 * att_w), F32)],
                 semantics=("arbitrary",), vmem_bytes=vm)


def _tile_cumsum(x, row, reverse=False):
    for s in (1, 2, 4):
        if reverse:
            x = x + jnp.where(row < SUBLANES - s, pltpu.roll(x, SUBLANES - s, 0), 0.0)
        else:
            x = x + jnp.where(row >= s, pltpu.roll(x, s, 0), 0.0)
    return x


def _fgate_fwd(proj, b_f_pad, nh):
    tp, nb = proj.shape
    fblk = nb // LANES - 1

    def body(f_ref, b_ref, c_ref, ct_ref):
        b = b_ref[...]
        row = lax.broadcasted_iota(jnp.int32, (SUBLANES, LANES), 0)

        def step(i, carry):
            r0 = pl.multiple_of(i * SUBLANES, SUBLANES)
            lf = _log_sigmoid(f_ref[pl.ds(r0, SUBLANES), :] + b)
            x = _tile_cumsum(lf, row) + carry
            c_ref[pl.ds(r0, SUBLANES), :] = x
            return x[SUBLANES - 1:SUBLANES, :]

        lax.fori_loop(0, tp // SUBLANES, step, jnp.zeros((1, LANES), F32))
        ct_ref[...] = c_ref[...].T[:nh, :]

    return _call(body, (proj, b_f_pad), name="fgate_fwd", grid=(1,),
                 in_specs=[pl.BlockSpec((tp, LANES), lambda i: (0, fblk)), _full((1, LANES))],
                 out_specs=[_full((tp, LANES)), _full((nh, tp))],
                 out_shape=[jax.ShapeDtypeStruct((tp, LANES), F32), jax.ShapeDtypeStruct((nh, tp), F32)],
                 semantics=("arbitrary",))


def _fgate_bwd(proj, b_f_pad, dc):
    tp, nb = proj.shape
    fblk = nb // LANES - 1

    def body(f_ref, b_ref, dc_ref, df_ref, db_ref, dc_s):
        b = b_ref[...]
        row = lax.broadcasted_iota(jnp.int32, (SUBLANES, LANES), 0)
        nt = tp // SUBLANES

        def step(i, carry):
            suffix, acc = carry
            r0 = pl.multiple_of((nt - 1 - i) * SUBLANES, SUBLANES)
            dlf = _tile_cumsum(dc_ref[pl.ds(r0, SUBLANES), :], row, reverse=True) + suffix
            df = dlf * _sigmoid(-(f_ref[pl.ds(r0, SUBLANES), :] + b))
            dc_s[pl.ds(r0, SUBLANES), :] = df
            return dlf[0:1, :], acc + df

        _, acc = lax.fori_loop(0, nt, step, (jnp.zeros((1, LANES), F32), jnp.zeros((SUBLANES, LANES), F32)))
        df_ref[...] = dc_s[...].astype(BF16)
        db_ref[...] = jnp.broadcast_to(jnp.sum(acc, axis=0, keepdims=True), (SUBLANES, LANES))

    return _call(body, (proj, b_f_pad, dc), name="fgate_bwd", grid=(1,),
                 in_specs=[pl.BlockSpec((tp, LANES), lambda i: (0, fblk)), _full((1, LANES)), _full((tp, LANES))],
                 out_specs=[_full((tp, LANES)), _full((SUBLANES, LANES))],
                 out_shape=[jax.ShapeDtypeStruct((tp, LANES), BF16),
                            jax.ShapeDtypeStruct((SUBLANES, LANES), F32)],
                 scratch_shapes=[pltpu.VMEM((tp, LANES), F32)], semantics=("arbitrary",))


ATT_BQ = 128


ATT_BUCKET = 3
ATT_HEADS = 4


def _for_bucket(i, nq, fn):
    for lo in range(0, nq, ATT_BUCKET):
        hi = min(lo + ATT_BUCKET, nq)
        spans = ([(0, lo * ATT_BQ, False)] if lo else []) + [(lo * ATT_BQ, hi * ATT_BQ, True)]
        pl.when(jnp.logical_and(i >= lo, i < hi))(functools.partial(fn, spans))


def _head_column(c_blk, h):
    lane = lax.broadcasted_iota(jnp.int32, c_blk.shape, 1)
    return jnp.sum(jnp.where(lane == h, c_blk, 0.0), axis=1, keepdims=True)


def _head_columns_into(p, c_ref, ck_s):
    for hh in range(ATT_HEADS):
        ck_s[hh] = jnp.broadcast_to(_head_column(c_ref[...], ATT_HEADS * p + hh), ck_s.shape[1:])


def _pair_diag_cols(x2):
    top = lax.broadcasted_iota(jnp.int32, (LANES, ATT_BQ), 0) < HEAD_DIM
    xt = x2.astype(F32).T.astype(BF16)
    return jnp.concatenate([jnp.where(top, xt, 0), jnp.where(top, 0, xt)], axis=1)


def _pair_diag_rows(x2):
    low = lax.broadcasted_iota(jnp.int32, (ATT_BQ, LANES), 1) < HEAD_DIM
    return jnp.concatenate([jnp.where(low, x2, 0), jnp.where(low, 0, x2)], axis=0)


def _seen_keys(k0, k1, q0):
    keys = k0 + lax.broadcasted_iota(jnp.int32, (k1 - k0, ATT_BQ), 0)
    return keys <= q0 + lax.broadcasted_iota(jnp.int32, (k1 - k0, ATT_BQ), 1)


def _attn_fwd(qkv, c, c_t, nh):
    tp = qkv.shape[0]
    att_w = nh * HEAD_DIM
    ng = nh // ATT_HEADS
    gw = ATT_HEADS * HEAD_DIM
    bq = ATT_BQ
    nq = tp // bq
    pair = 2 * HEAD_DIM
    assert pair == LANES and ATT_HEADS % 2 == 0

    def body(q_ref, k_ref, v_ref, c_ref, ct_ref, o_ref, lse_ref, ck_s, vt_s):
        p = pl.program_id(0)
        i = pl.program_id(1)

        @pl.when(i == 0)
        def _():
            _head_columns_into(p, c_ref, ck_s)
            vt_s[...] = v_ref[...].astype(F32).T.astype(BF16)

        def compute(spans):
            q0 = pl.multiple_of(i * bq, bq)
            o_t, lses = [], []
            for pi in range(ATT_HEADS // 2):
                lo = pair * pi
                heads = (2 * pi, 2 * pi + 1)
                q_cols = _pair_diag_cols(q_ref[:, lo:lo + pair])
                ts = []
                for k0, k1, needs_mask in spans:
                    t2 = _dot(k_ref[k0:k1, lo:lo + pair], q_cols)
                    t_e = [t2[:, e * bq:(e + 1) * bq] - ck_s[hh, k0:k1, :] for e, hh in enumerate(heads)]
                    if needs_mask:
                        seen = _seen_keys(k0, k1, q0)
                        t_e = [jnp.where(seen, t, NEG_BIG) for t in t_e]
                    ts.append(t_e)
                ms = [functools.reduce(jnp.maximum, [jnp.max(t[e], axis=0, keepdims=True) for t in ts])
                      for e in range(2)]
                es = [[jnp.exp(t[e] - ms[e]) for e in range(2)] for t in ts]
                ls = [sum(jnp.sum(e_[e], axis=0, keepdims=True) for e_ in es) for e in range(2)]
                o2 = sum(_dot(vt_s[lo:lo + pair, k0:k1],
                              jnp.concatenate([e_[0].astype(BF16), e_[1].astype(BF16)], axis=1))
                         for e_, (k0, k1, _) in zip(es, spans))
                o_t += [o2[:HEAD_DIM, :bq] / ls[0], o2[HEAD_DIM:, bq:] / ls[1]]
                lses += [ms[e] + ct_ref[pl.ds(ATT_HEADS * p + hh, 1), :] + jnp.log(ls[e])
                         for e, hh in enumerate(heads)]
            o_ref[...] = jnp.concatenate(o_t, axis=0).T
            lse_ref[...] = jnp.concatenate(lses, axis=0)

        _for_bucket(i, nq, compute)

    blk = pl.BlockSpec((bq, gw), lambda p, i: (i, p))
    vm = 6 * _nbytes((tp, gw), BF16) + 2 * ATT_HEADS * _nbytes((tp, LANES), F32) + 2 * _nbytes((tp, LANES), F32) \
        + 8 * ATT_HEADS * _nbytes((bq, tp), F32)
    return _call(body, (qkv, qkv, qkv, c, c_t), name="attn_fwd", grid=(ng, nq),
                 in_specs=[blk,
                           pl.BlockSpec((tp, gw), lambda p, i: (0, ng + p)),
                           pl.BlockSpec((tp, gw), lambda p, i: (0, 2 * ng + p)),
                           _full((tp, LANES)), pl.BlockSpec((nh, bq), lambda p, i: (0, i))],
                 out_specs=[blk, pl.BlockSpec((None, ATT_HEADS, bq), lambda p, i: (p, 0, i))],
                 out_shape=[jax.ShapeDtypeStruct((tp, att_w), F32), jax.ShapeDtypeStruct((ng, ATT_HEADS, tp), F32)],
                 scratch_shapes=[pltpu.VMEM((ATT_HEADS, tp, LANES), F32), pltpu.VMEM((gw, tp), BF16)],
                 semantics=("arbitrary", "arbitrary"), vmem_bytes=vm)


def _attn_bwd(qkv, c, c_t, lse, do, nh):
    tp = qkv.shape[0]
    att_w = nh * HEAD_DIM
    ng = nh // ATT_HEADS
    gw = ATT_HEADS * HEAD_DIM
    bq = ATT_BQ
    nq = tp // bq
    pair = 2 * HEAD_DIM
    assert pair == LANES and ATT_HEADS % 2 == 0
    scale = 1.0 / math.sqrt(HEAD_DIM)

    def body(q_ref, k_ref, v_ref, c_ref, ct_ref, lse_ref, do_ref, dq_ref, dk_ref, dv_ref, dc_ref,
             dk_s, dv_s, dc_s, ck_s, kt_s):
        p = pl.program_id(0)
        i = pl.program_id(1)

        @pl.when(i == 0)
        def _():
            dk_s[...] = jnp.zeros_like(dk_s)
            dv_s[...] = jnp.zeros_like(dv_s)
            dc_s[...] = jnp.zeros_like(dc_s)
            kt_s[...] = k_ref[...].astype(F32).T.astype(BF16)
            _head_columns_into(p, c_ref, ck_s)

        @pl.when(jnp.logical_and(i == 0, p == 0))
        def _():
            dc_ref[...] = jnp.zeros_like(dc_ref)

        def compute(spans):
            q0 = pl.multiple_of(i * bq, bq)
            dq_t = []
            for pi in range(ATT_HEADS // 2):
                lo = pair * pi
                q2 = q_ref[:, lo:lo + pair]
                do2 = do_ref[:, lo:lo + pair].astype(BF16)
                q_cols, do_cols = _pair_diag_cols(q2), _pair_diag_cols(do2)
                q_rows, do_rows = _pair_diag_rows(q2), _pair_diag_rows(do2)
                heads = (2 * pi, 2 * pi + 1)
                col_terms = [ct_ref[pl.ds(ATT_HEADS * p + hh, 1), :] - lse_ref[hh:hh + 1, :] for hh in heads]
                prs, dps = [], []
                for k0, k1, needs_mask in spans:
                    t2 = _dot(k_ref[k0:k1, lo:lo + pair], q_cols)
                    dp2 = _dot(v_ref[k0:k1, lo:lo + pair], do_cols)
                    if needs_mask:
                        seen = _seen_keys(k0, k1, q0)
                    pr_e, dp_e = [], []
                    for e, hh in enumerate(heads):
                        t = t2[:, e * bq:(e + 1) * bq] - ck_s[hh, k0:k1, :]
                        if needs_mask:
                            t = jnp.where(seen, t, NEG_BIG)
                        pr_e.append(jnp.exp(t + col_terms[e]))
                        dp_e.append(dp2[:, e * bq:(e + 1) * bq])
                    prs.append(pr_e)
                    dps.append(dp_e)
                key_sums = [sum(jnp.sum(pr[e] * dp[e], axis=0, keepdims=True) for pr, dp in zip(prs, dps))
                            for e in range(2)]
                dq2 = 0.0
                for (k0, k1, _), pr, dp in zip(spans, prs, dps):
                    ds = [pr[e] * (dp[e] - key_sums[e]) for e in range(2)]
                    for e, hh in enumerate(heads):
                        dc_s[hh, k0:k1, :] += jnp.sum(ds[e], axis=1, keepdims=True)
                    ds2 = jnp.concatenate([ds[0].astype(BF16), ds[1].astype(BF16)], axis=1)
                    pr2 = jnp.concatenate([pr[0].astype(BF16), pr[1].astype(BF16)], axis=1)
                    dk_s[k0:k1, lo:lo + pair] += _dot(ds2, q_rows)
                    dv_s[k0:k1, lo:lo + pair] += _dot(pr2, do_rows)
                    dq2 = dq2 + _dot(kt_s[lo:lo + pair, k0:k1], ds2)
                dq_t.append(jnp.concatenate([dq2[:HEAD_DIM, :bq], dq2[HEAD_DIM:, bq:]], axis=0))
            dq_ref[...] = (jnp.concatenate(dq_t, axis=0) * scale).T.astype(BF16)

        _for_bucket(i, nq, compute)

        @pl.when(i == nq - 1)
        def _():
            dk_ref[...] = dk_s[...].astype(BF16)
            dv_ref[...] = dv_s[...].astype(BF16)
            lane = lax.broadcasted_iota(jnp.int32, (tp, LANES), 1)
            dc = dc_ref[...]
            for hh in range(ATT_HEADS):
                dc = jnp.where(lane == ATT_HEADS * p + hh, -dc_s[hh], dc)
            dc_ref[...] = dc

    blk = pl.BlockSpec((bq, gw), lambda p, i: (i, p))
    col = pl.BlockSpec((tp, gw), lambda p, i: (0, p))
    vm = 7 * _nbytes((tp, gw), BF16) + 2 * _nbytes((tp, gw), F32) + 2 * ATT_HEADS * _nbytes((tp, LANES), F32) \
        + 2 * _nbytes((tp, LANES), F32) + 12 * ATT_HEADS * _nbytes((bq, tp), F32)
    return _call(body, (qkv, qkv, qkv, c, c_t, lse, do), name="attn_bwd", grid=(ng, nq),
                 in_specs=[blk,
                           pl.BlockSpec((tp, gw), lambda p, i: (0, ng + p)),
                           pl.BlockSpec((tp, gw), lambda p, i: (0, 2 * ng + p)),
                           _full((tp, LANES)), pl.BlockSpec((nh, bq), lambda p, i: (0, i)),
                           pl.BlockSpec((None, ATT_HEADS, bq), lambda p, i: (p, 0, i)), blk],
                 out_specs=[blk, col, col, _full((tp, LANES))],
                 out_shape=[jax.ShapeDtypeStruct((tp, att_w), BF16)] * 3 + [jax.ShapeDtypeStruct((tp, LANES), F32)],
                 scratch_shapes=[pltpu.VMEM((tp, gw), F32), pltpu.VMEM((tp, gw), F32),
                                 pltpu.VMEM((ATT_HEADS, tp, 1), F32), pltpu.VMEM((ATT_HEADS, tp, LANES), F32),
                                 pltpu.VMEM((gw, tp), BF16)],
                 semantics=("arbitrary", "arbitrary"), vmem_bytes=vm)


REC_ROWS = 128
HALO = SUBLANES


def _conv_taps(cat):
    taps = []
    for k in range(CONV_WIDTH):
        sh = CONV_WIDTH - 1 - k
        taps.append((pltpu.roll(cat, sh, 0) if sh else cat)[HALO:])
    return taps


def _rec_gates(xc, wa_ref, ba_ref, wx_ref, bx_ref, l_ref):
    xcb = xc.astype(BF16)
    r = _sigmoid(_dot(xcb, wa_ref[...]) + ba_ref[...])
    ig = _sigmoid(_dot(xcb, wx_ref[...]) + bx_ref[...])
    ls = _log_sigmoid(l_ref[...])
    log_a = RG_C * r * ls
    return xcb, r, ig, ls, log_a


def _rec_fwd(proj, xr_blk, yr_blk, rec_w, conv_w, conv_b, wa, ba, wx, bx, lru):
    tp = proj.shape[0]
    w = rec_w
    r_rows = REC_ROWS
    nc = tp // r_rows
    cpb = w // LANES

    def body(xr_ref, yr_ref, cw_ref, cb_ref, wa_ref, ba_ref, wx_ref, bx_ref, l_ref,
             hr_ref, rec_ref, prev_s, carry_s, a_s, u_s):
        i = pl.program_id(0)

        @pl.when(i == 0)
        def _():
            prev_s[...] = jnp.zeros_like(prev_s)
            carry_s[...] = jnp.zeros_like(carry_s)

        x = xr_ref[...]
        taps = _conv_taps(jnp.concatenate([prev_s[...], x], axis=0))
        prev_s[...] = x[r_rows - HALO:]
        xc = cb_ref[...]
        for k in range(CONV_WIDTH):
            xc = xc + cw_ref[k:k + 1, :] * taps[k]
        _, r, ig, ls, log_a = _rec_gates(xc, wa_ref, ba_ref, wx_ref, bx_ref, l_ref)
        a = jnp.exp(log_a)
        a_s[...] = a
        u_s[...] = jnp.sqrt(_one_minus_exp(2.0 * log_a, a * a)) * ig * xc

        def tile(j, h):
            r0 = pl.multiple_of(j * SUBLANES, SUBLANES)
            at = a_s[pl.ds(r0, SUBLANES), :]
            ut = u_s[pl.ds(r0, SUBLANES), :]
            out = []
            for rr in range(SUBLANES):
                h = at[rr:rr + 1] * h + ut[rr:rr + 1]
                out.append(h)
            hr_ref[pl.ds(r0, SUBLANES), :] = jnp.concatenate(out, axis=0)
            return h

        carry_s[0:1, :] = lax.fori_loop(0, r_rows // SUBLANES, tile, carry_s[0:1, :])
        g, _ = _gelu_and_grad(yr_ref[...])
        rec_ref[...] = hr_ref[...] * g

    blk = pl.BlockSpec((r_rows, w), lambda i: (i, 0))
    vm = 16 * _nbytes((r_rows, w), F32) + 4 * _nbytes((w, w), BF16)
    return _call(body, (proj, proj, conv_w, conv_b, wa, ba, wx, bx, lru), name="rec_fwd", grid=(nc,),
                 in_specs=[pl.BlockSpec((r_rows, w), lambda i: (i, xr_blk)),
                           pl.BlockSpec((r_rows, w), lambda i: (i, yr_blk)),
                           _full((CONV_WIDTH, w)), _full((1, w)), _full((w, w)), _full((1, w)),
                           _full((w, w)), _full((1, w)), _full((1, w))],
                 out_specs=[blk, blk],
                 out_shape=[jax.ShapeDtypeStruct((tp, w), F32)] * 2,
                 scratch_shapes=[pltpu.VMEM((HALO, w), F32), pltpu.VMEM((SUBLANES, w), F32),
                                 pltpu.VMEM((r_rows, w), F32), pltpu.VMEM((r_rows, w), F32)],
                 semantics=("arbitrary",), vmem_bytes=vm)


def _rec_bwd(proj, xr_blk, yr_blk, rec_w, hr, drec, conv_w, conv_b, wa, ba, wx, bx, lru):
    tp = proj.shape[0]
    w = rec_w
    r_rows = REC_ROWS
    nc = tp // r_rows
    hpc = r_rows // HALO

    def body(xr_ref, xh_ref, yr_ref, hr_ref, hh_ref, drec_ref, cw_ref, cb_ref, wa_ref, ba_ref, wx_ref, bx_ref,
             l_ref, dxr_ref, dyr_ref, dwa_ref, dwx_ref, small_ref, lam_s, a_s, dhr_s, carry_s, next_s):
        i = pl.program_id(0)
        first = (nc - 1 - i) == 0

        @pl.when(i == 0)
        def _():
            carry_s[...] = jnp.zeros_like(carry_s)
            next_s[...] = jnp.zeros_like(next_s)
            dwa_ref[...] = jnp.zeros_like(dwa_ref)
            dwx_ref[...] = jnp.zeros_like(dwx_ref)
            small_ref[...] = jnp.zeros_like(small_ref)

        x = xr_ref[...]
        xprev = jnp.where(first, 0.0, xh_ref[...])
        taps = _conv_taps(jnp.concatenate([xprev, x], axis=0))
        xc = cb_ref[...]
        for k in range(CONV_WIDTH):
            xc = xc + cw_ref[k:k + 1, :] * taps[k]
        xcb, r, ig, ls, log_a = _rec_gates(xc, wa_ref, ba_ref, wx_ref, bx_ref, l_ref)
        a = jnp.exp(log_a)
        a2 = a * a
        mult = jnp.sqrt(_one_minus_exp(2.0 * log_a, a2))
        g, dg = _gelu_and_grad(yr_ref[...])
        hr_v = hr_ref[...]
        drec_v = drec_ref[...]
        dhr_s[...] = drec_v * g
        dyr_ref[...] = (drec_v * hr_v * dg).astype(BF16)
        a_s[...] = a

        def tile(jj, carry):
            r0 = pl.multiple_of((r_rows // SUBLANES - 1 - jj) * SUBLANES, SUBLANES)
            at = a_s[pl.ds(r0, SUBLANES), :]
            dt = dhr_s[pl.ds(r0, SUBLANES), :]
            out = [None] * SUBLANES
            for rr in range(SUBLANES - 1, -1, -1):
                lam = dt[rr:rr + 1] + carry
                out[rr] = lam
                carry = at[rr:rr + 1] * lam
            lam_s[pl.ds(r0, SUBLANES), :] = jnp.concatenate(out, axis=0)
            return carry

        carry_s[0:1, :] = lax.fori_loop(0, r_rows // SUBLANES, tile, carry_s[0:1, :])
        lam = lam_s[...]
        hprev = jnp.where(first, 0.0, hh_ref[...])
        hr_prev = pltpu.roll(jnp.concatenate([hprev, hr_v], axis=0), 1, 0)[HALO:]
        da = lam * hr_prev
        dxc = lam * mult * ig
        di = lam * mult * xc
        dmult = lam * ig * xc
        dlog_a = da * a - dmult * a2 / mult
        dr = dlog_a * (RG_C * ls)
        dls = jnp.sum(dlog_a * (RG_C * r), axis=0, keepdims=True)
        dga = dr * r * (1.0 - r)
        dgx = di * ig * (1.0 - ig)
        dgab = dga.astype(BF16)
        dgxb = dgx.astype(BF16)
        dxc = dxc + _dot_nt(dgab, wa_ref[...]) + _dot_nt(dgxb, wx_ref[...])
        dwa_ref[...] += _dot_tn(xcb, dgab)
        dwx_ref[...] += _dot_tn(xcb, dgxb)
        cat = jnp.concatenate([dxc, next_s[...]], axis=0)
        next_s[...] = dxc[0:HALO]
        dxr = cw_ref[CONV_WIDTH - 1:CONV_WIDTH, :] * dxc
        for k in range(CONV_WIDTH - 1):
            sh = CONV_WIDTH - 1 - k
            dxr = dxr + cw_ref[k:k + 1, :] * pltpu.roll(cat, r_rows + HALO - sh, 0)[:r_rows]
        dxr_ref[...] = dxr.astype(BF16)
        rows = [jnp.sum(dxc * taps[k], axis=0, keepdims=True) for k in range(CONV_WIDTH)]
        rows += [jnp.sum(dxc, axis=0, keepdims=True), jnp.sum(dga, axis=0, keepdims=True),
                 jnp.sum(dgx, axis=0, keepdims=True), dls * _sigmoid(-l_ref[...])]
        small_ref[...] += jnp.concatenate(rows, axis=0)

    def rev(i):
        return nc - 1 - i

    def halo(i):
        return jnp.maximum(rev(i) * hpc - 1, 0)

    blk = pl.BlockSpec((r_rows, w), lambda i: (rev(i), 0))
    vm = 40 * _nbytes((r_rows, w), F32) + 6 * _nbytes((w, w), F32)
    return _call(body, (proj, proj, proj, hr, hr, drec, conv_w, conv_b, wa, ba, wx, bx, lru),
                 name="rec_bwd", grid=(nc,),
                 in_specs=[pl.BlockSpec((r_rows, w), lambda i: (rev(i), xr_blk)),
                           pl.BlockSpec((HALO, w), lambda i: (halo(i), xr_blk)),
                           pl.BlockSpec((r_rows, w), lambda i: (rev(i), yr_blk)),
                           blk,
                           pl.BlockSpec((HALO, w), lambda i: (halo(i), 0)),
                           blk,
                           _full((CONV_WIDTH, w)), _full((1, w)), _full((w, w)), _full((1, w)),
                           _full((w, w)), _full((1, w)), _full((1, w))],
                 out_specs=[blk, blk, _full((w, w)), _full((w, w)), _full((SUBLANES, w))],
                 out_shape=[jax.ShapeDtypeStruct((tp, w), BF16)] * 2
                 + [jax.ShapeDtypeStruct((w, w), F32)] * 2 + [jax.ShapeDtypeStruct((SUBLANES, w), F32)],
                 scratch_shapes=[pltpu.VMEM((r_rows, w), F32)] * 3
                 + [pltpu.VMEM((SUBLANES, w), F32), pltpu.VMEM((HALO, w), F32)],
                 semantics=("arbitrary",), vmem_bytes=vm)


ROW_TARGET = 544


def _mixer_out(attn, rec, g_a, g_r, w_out, h, g_next):
    tp, d = h.shape
    aw, rw = attn.shape[1], rec.shape[1]
    kc = d // N_CHIPS
    tm = _divisor_tile(tp, 16, ROW_TARGET)

    def body(a_ref, r_ref, ga_ref, gr_ref, w_ref, h_ref, gn_ref, h1_ref, z_ref, mix_ref):
        a = a_ref[...]
        r = r_ref[...]
        mix = jnp.concatenate([a * _rstd(a) * ga_ref[...], r * _rstd(r) * gr_ref[...]], axis=1).astype(BF16)
        mix_ref[...] = mix
        h1 = h_ref[...]
        for j in range(N_CHIPS):
            h1 = h1 + _dot(mix[:, j * kc:(j + 1) * kc], w_ref[j])
        h1_ref[...] = h1
        z_ref[...] = (h1 * _rstd(h1) * gn_ref[...]).astype(BF16)

    row = lambda wd: pl.BlockSpec((tm, wd), lambda i: (i, 0))
    vm = 2 * _nbytes((d, d), BF16) + 12 * _nbytes((tm, d), F32)
    return _call(body, (attn, rec, g_a, g_r, w_out, h, g_next), name="mixer_out", grid=(tp // tm,),
                 in_specs=[row(aw), row(rw), _full((1, aw)), _full((1, rw)), _full(w_out.shape), row(d),
                           _full((1, d))],
                 out_specs=[row(d), row(d), row(d)],
                 out_shape=[jax.ShapeDtypeStruct((tp, d), F32), jax.ShapeDtypeStruct((tp, d), BF16),
                            jax.ShapeDtypeStruct((tp, d), BF16)],
                 semantics=("parallel",), vmem_bytes=vm)


def _mixer_bwd(dh_b, w_out, attn, rec, g_a, g_r):
    tp, d = dh_b.shape
    aw, rw = attn.shape[1], rec.shape[1]
    tm = _divisor_tile(tp, 16, ROW_TARGET)

    def body(dh_ref, w_ref, a_ref, r_ref, ga_ref, gr_ref, da_ref, dr_ref, dg_ref):
        @pl.when(pl.program_id(0) == 0)
        def _():
            dg_ref[...] = jnp.zeros_like(dg_ref)

        dh = dh_ref[...]
        dmix = jnp.concatenate([_dot_nt(dh, w_ref[j]) for j in range(N_CHIPS)], axis=1)
        da, dga = _rms_bwd(dmix[:, :aw], a_ref[...], ga_ref[...])
        dr, dgr = _rms_bwd(dmix[:, aw:], r_ref[...], gr_ref[...])
        da_ref[...] = da
        dr_ref[...] = dr
        dg_ref[...] += jnp.broadcast_to(jnp.concatenate([dga, dgr], axis=1), (SUBLANES, d))

    row = lambda wd: pl.BlockSpec((tm, wd), lambda i: (i, 0))
    vm = 2 * _nbytes((d, d), BF16) + 12 * _nbytes((tm, d), F32)
    return _call(body, (dh_b, w_out, attn, rec, g_a, g_r), name="mixer_bwd", grid=(tp // tm,),
                 in_specs=[row(d), _full(w_out.shape), row(aw), row(rw), _full((1, aw)), _full((1, rw))],
                 out_specs=[row(aw), row(rw), _full((SUBLANES, d))],
                 out_shape=[jax.ShapeDtypeStruct((tp, aw), F32), jax.ShapeDtypeStruct((tp, rw), F32),
                            jax.ShapeDtypeStruct((SUBLANES, d), F32)],
                 semantics=("arbitrary",), vmem_bytes=vm)


def _mlp_up(z, w_up):
    tp, d = z.shape
    fc = w_up.shape[2]
    ff = N_CHIPS * fc
    tn = _divisor_tile(fc, LANES, 512)
    per = fc // tn

    def body(z_ref, w_ref, act_ref, up_ref):
        up = _dot(z_ref[...], w_ref[...])
        r = jnp.maximum(up, 0.0)
        act_ref[...] = (r * r).astype(BF16)
        up_ref[...] = up.astype(BF16)

    col = pl.BlockSpec((tp, tn), lambda j: (0, j))
    vm = 2 * _nbytes((tp, d), BF16) + 2 * _nbytes((d, tn), BF16) + 8 * _nbytes((tp, tn), F32)
    return _call(body, (z, w_up), name="mlp_up", grid=(ff // tn,),
                 in_specs=[_full((tp, d)), pl.BlockSpec((None, d, tn), lambda j: (j // per, 0, j % per))],
                 out_specs=[col, col],
                 out_shape=[jax.ShapeDtypeStruct((tp, ff), BF16)] * 2,
                 semantics=("parallel",), vmem_bytes=vm)


def _mlp_down(act, w_down, h, g_next):
    tp, d = h.shape
    ff = act.shape[1]
    fc = ff // N_CHIPS
    tm = _divisor_tile(tp, 16, ROW_TARGET)

    def body(a_ref, w_ref, h_ref, gn_ref, h2_ref, z_ref):
        h2 = h_ref[...]
        for j in range(N_CHIPS):
            h2 = h2 + _dot(a_ref[:, j * fc:(j + 1) * fc], w_ref[j])
        h2_ref[...] = h2
        z_ref[...] = (h2 * _rstd(h2) * gn_ref[...]).astype(BF16)

    row = lambda wd: pl.BlockSpec((tm, wd), lambda i: (i, 0))
    vm = 2 * _nbytes((ff, d), BF16) + 2 * _nbytes((tm, ff), BF16) + 10 * _nbytes((tm, d), F32)
    return _call(body, (act, w_down, h, g_next), name="mlp_down", grid=(tp // tm,),
                 in_specs=[row(ff), _full(w_down.shape), row(d), _full((1, d))],
                 out_specs=[row(d), row(d)],
                 out_shape=[jax.ShapeDtypeStruct((tp, d), F32), jax.ShapeDtypeStruct((tp, d), BF16)],
                 semantics=("parallel",), vmem_bytes=vm)


def _loss_bwd(h, g, target, n_real):
    tp, d = h.shape
    tm = _divisor_tile(tp, 16, ROW_TARGET)

    def body(h_ref, g_ref, t_ref, dh_ref, dhb_ref, dg_ref, loss_ref):
        i = pl.program_id(0)

        @pl.when(i == 0)
        def _():
            dg_ref[...] = jnp.zeros_like(dg_ref)
            loss_ref[...] = jnp.zeros_like(loss_ref)

        x = h_ref[...]
        gv = g_ref[...]
        rowi = i * tm + lax.broadcasted_iota(jnp.int32, (tm, 1), 0)
        real = jnp.logical_and(rowi >= N_META, rowi < N_META + n_real)
        err = jnp.where(real, x * _rstd(x) * gv - t_ref[...], 0.0)
        loss_ref[...] += 0.5 * jnp.sum(jnp.mean(err * err, axis=-1, keepdims=True))
        dx, dgp = _rms_bwd(err * (1.0 / d), x, gv)
        dh_ref[...] = dx
        dhb_ref[...] = dx.astype(BF16)
        dg_ref[...] += jnp.broadcast_to(dgp, (SUBLANES, d))

    row = pl.BlockSpec((tm, d), lambda i: (i, 0))
    return _call(body, (h, g, target), name="loss_bwd", grid=(tp // tm,),
                 in_specs=[row, _full((1, d)), row],
                 out_specs=[row, row, _full((SUBLANES, d)), _full((SUBLANES, LANES))],
                 out_shape=[jax.ShapeDtypeStruct((tp, d), F32), jax.ShapeDtypeStruct((tp, d), BF16),
                            jax.ShapeDtypeStruct((SUBLANES, d), F32), jax.ShapeDtypeStruct((SUBLANES, LANES), F32)],
                 semantics=("arbitrary",), vmem_bytes=16 * _nbytes((tm, d), F32))


def _mlp_bwd(dh_b, w_down, up, z2):
    tp, d = dh_b.shape
    fc = w_down.shape[1]
    ff = N_CHIPS * fc
    tn = _divisor_tile(fc, LANES, 512)
    per = fc // tn

    def body(dh_ref, z_ref, w_ref, up_ref, dup_ref, gd_ref, gu_ref):
        dh = dh_ref[...]
        r = jnp.maximum(up_ref[...].astype(F32), 0.0)
        dup = (_dot_nt(dh, w_ref[...]) * (2.0 * r)).astype(BF16)
        dup_ref[...] = dup
        gd_ref[...] = _dot_tn((r * r).astype(BF16), dh).astype(BF16)
        gu_ref[...] = _dot_tn(z_ref[...], dup).astype(BF16)

    col = pl.BlockSpec((tp, tn), lambda j: (0, j))
    vm = 4 * _nbytes((tp, d), BF16) + 4 * _nbytes((tn, d), BF16) + 2 * _nbytes((d, tn), BF16) \
        + 10 * _nbytes((tp, tn), F32) + 4 * _nbytes((tn, d), F32)
    return _call(body, (dh_b, z2, w_down, up), name="mlp_bwd", grid=(ff // tn,),
                 in_specs=[_full((tp, d)), _full((tp, d)),
                           pl.BlockSpec((None, tn, d), lambda j: (j // per, j % per, 0)), col],
                 out_specs=[col, pl.BlockSpec((tn, d), lambda j: (j, 0)),
                            pl.BlockSpec((None, d, tn), lambda j: (j // per, 0, j % per))],
                 out_shape=[jax.ShapeDtypeStruct((tp, ff), BF16), jax.ShapeDtypeStruct((ff, d), BF16),
                            jax.ShapeDtypeStruct((N_CHIPS, d, fc), BF16)],
                 semantics=("parallel",), vmem_bytes=vm)


def _grad_w_pieces(pieces, b):
    tp, n = b.shape
    tn = _divisor_tile(n, LANES, 512)
    widths = [pc.shape[1] for pc in pieces]

    def body(*refs):
        p_refs, b_ref, o_refs = refs[:len(pieces)], refs[len(pieces)], refs[len(pieces) + 1:]
        for p_ref, o_ref in zip(p_refs, o_refs):
            o_ref[...] = _dot_tn(p_ref[...], b_ref[...]).astype(BF16)

    vm = 2 * sum(_nbytes((tp, wd), BF16) for wd in widths) + 2 * _nbytes((tp, tn), BF16) \
        + 4 * sum(_nbytes((wd, tn), F32) for wd in widths) + 2 * _nbytes((tp, max(widths)), F32)
    return _call(body, tuple(pieces) + (b,), name="grad_w_pieces", grid=(n // tn,),
                 in_specs=[_full(pc.shape) for pc in pieces] + [pl.BlockSpec((tp, tn), lambda j: (0, j))],
                 out_specs=[pl.BlockSpec((wd, tn), lambda j: (0, j)) for wd in widths],
                 out_shape=[jax.ShapeDtypeStruct((wd, n), BF16) for wd in widths],
                 semantics=("parallel",), vmem_bytes=vm)


def _dx_norm_bwd(pieces, w, w_spec, w_piece, h, g, dres, dot=_dot_nt):
    tp, d = h.shape
    tm = _divisor_tile(tp, 16, ROW_TARGET)
    n = len(pieces)

    def body(*refs):
        dy_refs = refs[:n]
        w_ref, h_ref, g_ref, dres_ref, dh_ref, dhb_ref, dg_ref = refs[n:]

        @pl.when(pl.program_id(0) == 0)
        def _():
            dg_ref[...] = jnp.zeros_like(dg_ref)

        dz = dot(dy_refs[0][...], w_piece(w_ref, 0))
        for i in range(1, n):
            dz = dz + dot(dy_refs[i][...], w_piece(w_ref, i))
        dx, dgp = _rms_bwd(dz, h_ref[...], g_ref[...])
        dh = dres_ref[...] + dx
        dh_ref[...] = dh
        dhb_ref[...] = dh.astype(BF16)
        dg_ref[...] += jnp.broadcast_to(dgp, (SUBLANES, d))

    row = lambda wd: pl.BlockSpec((tm, wd), lambda i: (i, 0))
    kk = sum(wd for _, _, wd in pieces)
    vm = 2 * _nbytes((d, kk), BF16) + 2 * _nbytes((tm, kk), BF16) + 14 * _nbytes((tm, d), F32)
    piece_specs = [pl.BlockSpec((tm, wd), functools.partial(lambda i, cb: (i, cb), cb=cb)) for _, cb, wd in pieces]
    return _call(body, tuple(a for a, _, _ in pieces) + (w, h, g, dres), name="dx_norm_bwd", grid=(tp // tm,),
                 in_specs=piece_specs + [w_spec, row(d), _full((1, d)), row(d)],
                 out_specs=[row(d), row(d), _full((SUBLANES, d))],
                 out_shape=[jax.ShapeDtypeStruct((tp, d), F32), jax.ShapeDtypeStruct((tp, d), BF16),
                            jax.ShapeDtypeStruct((SUBLANES, d), F32)],
                 semantics=("arbitrary",), vmem_bytes=vm)


def _block_diag(wg):
    nb, b, _ = wg.shape
    eye = jnp.eye(nb, dtype=wg.dtype)
    return (eye[:, None, :, None] * wg[:, :, None, :]).reshape(nb * b, nb * b)


def _diag_blocks(dense, nb):
    b = dense.shape[0] // nb
    d4 = dense.reshape(nb, b, nb, b)
    return jnp.stack([d4[i, :, i, :] for i in range(nb)])


def _row(v):
    return v.reshape(1, -1)


def _forward_layer(l, h, z, p, fetch, stage_next, g_next):
    d = h.shape[1]
    att_w = d // 2
    rec_w = d - att_w
    nh = att_w // HEAD_DIM
    wa_d = _block_diag(p["w_gate_a"][l]).astype(BF16)
    wx_d = _block_diag(p["w_gate_x"][l]).astype(BF16)
    b_f_pad = jnp.zeros((1, LANES), F32).at[0, :nh].set(p["b_f"][l])
    w_in_t = fetch("w_in", z)
    big = dict(w_in_big=_pack_w_in_t(w_in_t.reshape(-1, d), att_w, nh))
    qkv, proj = _proj(z, big["w_in_big"], att_w)
    c, c_t = _fgate_fwd(proj, b_f_pad, nh)
    attn, lse_b = _attn_fwd(qkv, c, c_t, nh)
    hr, rec = _rec_fwd(proj, 0, 1, rec_w, p["conv_w"][l], _row(p["conv_b"][l]), wa_d,
                       _row(p["b_gate_a"][l]), wx_d, _row(p["b_gate_x"][l]), _row(p["lru_L"][l]))
    tok = stage_next(attn, "own")
    big["w_out"] = fetch("w_out", rec)
    h1, z2, mix = _mixer_out(attn, rec, _row(p["attn_out_g"][l] + tok), _row(p["rec_out_g"][l]),
                             big["w_out"], h, _row(p["mlp_norm_g"][l]))
    big["w_up"] = fetch("w_up", h1)
    act, up = _mlp_up(z2, big["w_up"])
    tok = stage_next(act, "next")
    big["w_down"] = fetch("w_down", act)
    h2, z_next = _mlp_down(act, big["w_down"], h1, _row(g_next + tok))
    saved = dict(h0=h, z1=z, proj=proj, qkv=qkv, c=c, c_t=c_t, attn=attn, lse_b=lse_b, hr=hr, rec=rec, h1=h1,
                 z2=z2, mix=mix, up=up, wa_d=wa_d, wx_d=wx_d, b_f_pad=b_f_pad, big=big)
    return h2, z_next, saved


def _backward_mlp(l, dh, dh_b, sv, p, tok):
    w_up, w_down = sv["big"]["w_up"], sv["big"]["w_down"]
    fc = w_up.shape[2]
    dup, g_down, g_up = _mlp_bwd(dh_b, w_down, sv["up"], sv["z2"])
    dh, dh_b, dg2 = _dx_norm_bwd([(dup, j, fc) for j in range(N_CHIPS)], w_up, _full(w_up.shape),
                                 lambda w_ref, j: w_ref[j], sv["h1"], _row(p["mlp_norm_g"][l] + tok), dh)
    big = dict(w_down=g_down.reshape((N_CHIPS, -1) + g_down.shape[1:]), w_up=g_up)
    return dh, dh_b, big, dict(mlp_norm_g=dg2[0])


def _backward_mixer(l, dh, dh_b, sv, p, tok):
    d = dh.shape[1]
    att_w = d // 2
    rec_w = d - att_w
    nh = att_w // HEAD_DIM
    small = {}
    g_out, = _grad_w_pieces([sv["mix"]], dh_b)
    dattn, drec, dg_mix = _mixer_bwd(dh_b, sv["big"]["w_out"], sv["attn"], sv["rec"],
                                     _row(p["attn_out_g"][l] + tok), _row(p["rec_out_g"][l]))
    small["attn_out_g"] = dg_mix[0, :att_w]
    small["rec_out_g"] = dg_mix[0, att_w:]
    dxr, dyr, dwa, dwx, sm = _rec_bwd(
        sv["proj"], 0, 1, rec_w, sv["hr"], drec, p["conv_w"][l], _row(p["conv_b"][l]), sv["wa_d"],
        _row(p["b_gate_a"][l]), sv["wx_d"], _row(p["b_gate_x"][l]), _row(p["lru_L"][l]))
    small.update(conv_w=sm[:CONV_WIDTH], conv_b=sm[4], b_gate_a=sm[5], b_gate_x=sm[6], lru_L=sm[7],
                 w_gate_a=_diag_blocks(dwa, N_REC_BLOCKS), w_gate_x=_diag_blocks(dwx, N_REC_BLOCKS))
    dq, dk, dv, dc = _attn_bwd(sv["qkv"], sv["c"], sv["c_t"], sv["lse_b"], dattn, nh)
    df, db_f = _fgate_bwd(sv["proj"], sv["b_f_pad"], dc)
    small["b_f"] = db_f[0, :nh]
    pieces = [dq, dk, dv, dxr, dyr, df]
    offs = [0, att_w, 2 * att_w, 3 * att_w, 3 * att_w + rec_w, 3 * att_w + 2 * rec_w]
    gq, gk, gv, gxr, gyr, gf = _grad_w_pieces(pieces, sv["z1"])
    g_in_t = jnp.concatenate([gq, gk, gv, gf[:nh], gxr, gyr], axis=0)
    w_big = sv["big"]["w_in_big"]
    widths = [pc.shape[1] for pc in pieces]
    dh, dh_b, dg1 = _dx_norm_bwd(
        [(pc, 0, wd) for pc, wd in zip(pieces, widths)], w_big, _full(w_big.shape),
        lambda w_ref, i: w_ref[offs[i]:offs[i] + widths[i], :], sv["h0"], _row(p["attn_norm_g"][l]), dh, dot=_dot)
    small["attn_norm_g"] = dg1[0]
    big = dict(w_in=g_in_t.reshape(N_CHIPS, -1, d), w_out=g_out.reshape((N_CHIPS, -1) + g_out.shape[1:]))
    return dh, dh_b, big, small


def _pack_w_in_t(w_in_t, att_w, nh):
    qkv = w_in_t[:3 * att_w]
    f = w_in_t[3 * att_w:3 * att_w + nh]
    xy = w_in_t[3 * att_w + nh:]
    return jnp.concatenate([qkv, xy, f, jnp.zeros((LANES - nh, w_in_t.shape[1]), w_in_t.dtype)], axis=0)


ANY = pl.BlockSpec(memory_space=pl.ANY)


def _coords():
    return lax.axis_index("x"), lax.axis_index("y"), lax.axis_index("c")


def _other_chips(x, y):
    return [(1 - x, y), (x, 1 - y), (1 - x, 1 - y)]


def _remote(src, dst, send_sems, recv_sems, k, to):
    return pltpu.make_async_remote_copy(src_ref=src, dst_ref=dst, send_sem=send_sems.at[k],
                                        recv_sem=recv_sems.at[k], device_id=to, device_id_type=MESH)


def _all_gather_chips(shards):
    n = len(shards)
    per = 6

    def body(*refs):
        ins, outs = refs[:n], refs[n:2 * n]
        send_sems, recv_sems, local_sems = refs[2 * n:]
        x, y, c = _coords()
        me = 2 * x + y
        sibling = (x, y, 1 - c)
        chips = _other_chips(x, y)
        local = [pltpu.make_async_copy(ins[t], outs[t].at[me], local_sems.at[t]) for t in range(n)]
        for cp in local:
            cp.start()
        sends = []
        for t in range(n):
            for j, (px, py) in enumerate(chips):
                cp = _remote(ins[t].at[c], outs[t].at[me, c], send_sems, recv_sems, per * t + j, (px, py, c))
                cp.start()
                sends.append(cp)
        for t in range(n):
            for j, (px, py) in enumerate(chips):
                landed = outs[t].at[2 * px + py, c]
                _remote(landed, landed, send_sems, recv_sems, per * t + j, (px, py, c)).wait_recv()
                cp = _remote(landed, landed, send_sems, recv_sems, per * t + 3 + j, sibling)
                cp.start()
                sends.append(cp)
        for t in range(n):
            for j, (px, py) in enumerate(chips):
                passed = outs[t].at[2 * px + py, 1 - c]
                _remote(passed, passed, send_sems, recv_sems, per * t + 3 + j, sibling).wait_recv()
        for cp in sends:
            cp.wait_send()
        for cp in local:
            cp.wait()

    return _call(body, tuple(shards), name="all_gather_chips",
                 in_specs=[ANY] * n, out_specs=[ANY] * n,
                 out_shape=[jax.ShapeDtypeStruct((N_CHIPS,) + s.shape, s.dtype) for s in shards],
                 scratch_shapes=[pltpu.SemaphoreType.DMA((per * n,)), pltpu.SemaphoreType.DMA((per * n,)),
                                 pltpu.SemaphoreType.DMA((n,))])


HBM = pl.BlockSpec(memory_space=pltpu.HBM)
SEM = pl.BlockSpec(memory_space=pltpu.SEMAPHORE)
DATAFLOW = pltpu.SideEffectType.DATAFLOW_SIDE_EFFECTING


def _in_hbm(a):
    return pltpu.with_memory_space_constraint(a, pltpu.HBM)


PUSH_ARRIVALS = {"gather_chips_half": N_CHIPS - 1, "pass_halves": N_CHIPS - 1, "scatter_chips": N_CHIPS - 1,
                 "sibling": 1, "gather_devices": N_DEV - 1}


def _column_half(ref3, slab, c):
    hw = ref3.shape[2] // 2
    return ref3.at[slab, :, pl.ds(pl.multiple_of(c * hw, LANES), hw)]


def _push_copies(mode, src, land, send_sems, recv_sems, t):
    x, y, c = _coords()
    chip = 2 * x + y
    if mode == "gather_chips_half":
        return [_remote(_column_half(src, chip, c), _column_half(land, chip, c), send_sems, recv_sems, t, (px, py, c))
                for px, py in _other_chips(x, y)]
    if mode == "pass_halves":
        return [_remote(_column_half(src, 2 * px + py, c), _column_half(land, 2 * px + py, c), send_sems, recv_sems, t,
                        (x, y, 1 - c)) for px, py in _other_chips(x, y)]
    if mode == "scatter_chips":
        return [_remote(src.at[2 * px + py], land.at[chip], send_sems, recv_sems, t, (px, py, c))
                for px, py in _other_chips(x, y)]
    if mode == "sibling":
        return [_remote(src, land, send_sems, recv_sems, t, (x, y, 1 - c))]
    dev = 4 * x + 2 * y + c
    return [_remote(src.at[dev], land.at[dev], send_sems, recv_sems, t, (x ^ (k >> 2), y ^ ((k >> 1) & 1), c ^ (k & 1)))
            for k in range(1, N_DEV)]


def _push_start(srcs, lands, mode, name):
    n = len(srcs)
    same = all(s is ld for s, ld in zip(srcs, lands))
    n_in = n if same else 2 * n

    def body(*refs):
        src_refs = refs[:n]
        land_refs = src_refs if same else refs[n:2 * n]
        send_sems, recv_sems = refs[n_in], refs[n_in + 1]
        token = refs[-1]
        for t in range(n):
            for cp in _push_copies(mode, src_refs[t], land_refs[t], send_sems, recv_sems, t):
                cp.start()
        token[...] = jnp.zeros_like(token)

    operands = tuple(srcs) if same else tuple(srcs) + tuple(lands)
    res = _call(
        body, [_in_hbm(a) for a in operands], name=name,
        out_shape=(pltpu.SemaphoreType.DMA((n,)), pltpu.SemaphoreType.DMA((n,)))
        + tuple(pltpu.HBM(a.shape, a.dtype) for a in operands) + (jax.ShapeDtypeStruct((SUBLANES, LANES), F32),),
        in_specs=[HBM] * n_in, out_specs=(SEM, SEM) + (HBM,) * n_in + (pl.BlockSpec(memory_space=pltpu.VMEM),),
        input_output_aliases={i: 2 + i for i in range(n_in)}, side_effects=DATAFLOW, hbm_results=False)
    send_sems, recv_sems, token = res[0], res[1], res[-1]
    srcs_thru = res[2:2 + n]
    lands_thru = srcs_thru if same else res[2 + n:2 + 2 * n]
    return send_sems, recv_sems, srcs_thru, lands_thru, token


def _push_wait(send_sems, recv_sems, ids, srcs, lands, mode, after, name):
    n = len(lands)
    same = all(s is ld for s, ld in zip(srcs, lands))
    n_in = n if same else 2 * n

    def body(*refs):
        land_refs = refs[:n] if same else refs[n:2 * n]
        send_sems, recv_sems = refs[n_in], refs[n_in + 1]
        x, y, c = _coords()
        for t in range(n):
            if mode == "sibling":
                moved = land_refs[t]
            elif mode in ("gather_chips_half", "pass_halves"):
                moved = land_refs[t].at[pl.ds(0, PUSH_ARRIVALS[mode]), :, pl.ds(0, land_refs[t].shape[2] // 2)]
            else:
                moved = land_refs[t].at[pl.ds(0, PUSH_ARRIVALS[mode])]
            arrivals = _remote(moved, moved, send_sems, recv_sems, ids[t], (x, y, c))
            arrivals.wait_send()
            arrivals.wait_recv()

    operands = tuple(lands) if same else tuple(srcs) + tuple(lands)
    res = _call(
        body, operands + (send_sems, recv_sems, after), name=name,
        out_shape=tuple(pltpu.HBM(a.shape, a.dtype) for a in operands),
        in_specs=[HBM] * n_in + [SEM, SEM, ANY], out_specs=(HBM,) * n_in,
        input_output_aliases={i: i for i in range(n_in)}, side_effects=DATAFLOW)
    return list(res) if same else (list(res[:n]), list(res[n:]))


def _sum_partials(part, landed, chip):
    _, rows, cols = part.shape
    br = _divisor_tile(rows, 16, ELEM_ROWS)

    def body(chip_ref, own_ref, a_ref, b_ref, c_ref, o_ref):
        o_ref[...] = ((own_ref[...].astype(F32) + a_ref[...].astype(F32)) + b_ref[...].astype(F32)) \
            + c_ref[...].astype(F32)

    def other(k):
        return pl.BlockSpec((None, br, cols), lambda i, ch: (jnp.where(ch[0] <= k, k + 1, k), i, 0))

    spec = pltpu.PrefetchScalarGridSpec(
        num_scalar_prefetch=1, grid=(rows // br,),
        in_specs=[pl.BlockSpec((None, br, cols), lambda i, ch: (ch[0], i, 0)), other(0), other(1), other(2)],
        out_specs=pl.BlockSpec((br, cols), lambda i, ch: (i, 0)))
    return _call(body, (chip, part, landed, landed, landed), name="sum_partials", grid_spec=spec,
                 out_shape=jax.ShapeDtypeStruct((rows, cols), F32), semantics=("parallel",))


def _cast_to_slab(w, l, chip):
    _, rows, cols = w.shape
    br = _divisor_tile(rows, 16, ELEM_ROWS)

    def body(chip_ref, w_ref, o_ref):
        o_ref[...] = w_ref[...].astype(BF16)

    spec = pltpu.PrefetchScalarGridSpec(
        num_scalar_prefetch=1, grid=(rows // br,),
        in_specs=[pl.BlockSpec((None, br, cols), lambda i, ch: (l, i, 0))],
        out_specs=pl.BlockSpec((None, br, cols), lambda i, ch: (ch[0], i, 0)))
    return _call(body, (chip, w), name="cast_to_slab", grid_spec=spec,
                 out_shape=jax.ShapeDtypeStruct((N_CHIPS, rows, cols), BF16), semantics=("parallel",))


def _cast_w_in_t_to_slabs(w_t, chip):
    rows, depth, d = w_t.shape
    tn = _divisor_tile(d, LANES, 256)

    def body(chip_ref, w_ref, *o_refs):
        for l in range(depth):
            o_refs[l][...] = w_ref[:, l, :].astype(BF16)

    spec = pltpu.PrefetchScalarGridSpec(
        num_scalar_prefetch=1, grid=(d // tn,),
        in_specs=[pl.BlockSpec((rows, depth, tn), lambda j, ch: (0, 0, j))],
        out_specs=[pl.BlockSpec((None, rows, tn), lambda j, ch: (ch[0], 0, j))] * depth)
    return _call(body, (chip, w_t), name="cast_w_in_t_to_slabs", grid_spec=spec,
                 out_shape=[jax.ShapeDtypeStruct((N_CHIPS, rows, d), BF16)] * depth, semantics=("parallel",),
                 vmem_bytes=4 * _nbytes((rows, max(depth, SUBLANES), tn), F32))


def _place_slab(buf, index, n_slabs):
    rows, cols = buf.shape
    br = _divisor_tile(rows, SUBLANES, ELEM_ROWS)

    def body(index_ref, b_ref, o_ref):
        o_ref[...] = b_ref[...]

    spec = pltpu.PrefetchScalarGridSpec(
        num_scalar_prefetch=1, grid=(rows // br,),
        in_specs=[pl.BlockSpec((br, cols), lambda i, ix: (i, 0))],
        out_specs=pl.BlockSpec((None, br, cols), lambda i, ix: (ix[0], i, 0)))
    return _call(body, (index, buf), name="place_slab", grid_spec=spec,
                 out_shape=jax.ShapeDtypeStruct((n_slabs, rows, cols), buf.dtype), semantics=("parallel",))


ELEM_ROWS = 256


def _sum_slabs(r):
    n, rows, cols = r.shape
    br = _divisor_tile(rows, 16, ELEM_ROWS)

    def body(r_ref, o_ref):
        acc = r_ref[0].astype(F32)
        for j in range(1, n):
            acc = acc + r_ref[j].astype(F32)
        o_ref[...] = acc

    return _call(body, (r,), name="sum_slabs", grid=(rows // br,),
                 in_specs=[pl.BlockSpec((n, br, cols), lambda i: (0, i, 0))],
                 out_specs=pl.BlockSpec((br, cols), lambda i: (i, 0)),
                 out_shape=jax.ShapeDtypeStruct((rows, cols), F32), semantics=("parallel",))


def _adamw_math(w, g, m, v):
    c1 = 1.0 - ADAM_B1 ** ADAM_STEP
    c2 = 1.0 - ADAM_B2 ** ADAM_STEP
    nm = ADAM_B1 * m + (1.0 - ADAM_B1) * g
    nv = ADAM_B2 * v + (1.0 - ADAM_B2) * (g * g)
    delta = -ADAM_LR * ((nm / c1) / (jnp.sqrt(nv / c2) + ADAM_EPS) + ADAM_WD * w)
    return delta, nm, nv


def _adamw(w, g, m, v):
    rows, cols = w.shape
    br = _divisor_tile(rows, 8, ELEM_ROWS)

    def body(w_ref, g_ref, m_ref, v_ref, d_ref, nm_ref, nv_ref):
        d_ref[...], nm_ref[...], nv_ref[...] = _adamw_math(w_ref[...], g_ref[...], m_ref[...], v_ref[...])

    blk = pl.BlockSpec((br, cols), lambda i: (i, 0))
    return _call(body, (w, g, m, v), name="adamw", grid=(rows // br,),
                 in_specs=[blk] * 4, out_specs=[blk] * 3,
                 out_shape=[jax.ShapeDtypeStruct((rows, cols), F32)] * 3, semantics=("parallel",))


def _adamw_w_in_t(w_t, m_t, v_t, g_mine, g_theirs):
    rows, depth, d = w_t.shape
    tn = LANES

    def body(w_ref, m_ref, v_ref, ga_ref, gb_ref, g_ref, d_ref, nm_ref, nv_ref):
        g = ga_ref[...] + gb_ref[...]
        g_ref[...] = g
        d_ref[...], nm_ref[...], nv_ref[...] = _adamw_math(w_ref[...], g, m_ref[...], v_ref[...])

    slab = pl.BlockSpec((rows, depth, tn), lambda j: (0, 0, j))
    return _call(body, (w_t, m_t, v_t, g_mine, g_theirs), name="adamw_w_in_t", grid=(d // tn,),
                 in_specs=[slab] * 5, out_specs=[slab] * 4,
                 out_shape=[jax.ShapeDtypeStruct(w_t.shape, F32)] * 4, semantics=("parallel",),
                 vmem_bytes=2 * 9 * _nbytes((rows, max(depth, SUBLANES), tn), F32))


def _adamw_layer(w, m, v, l, g_mine, g_theirs, prev, after):
    _, rows, cols = w.shape
    br = _divisor_tile(rows, 8, ELEM_ROWS)

    def body(w_ref, m_ref, v_ref, ga_ref, gb_ref, *rest):
        g_ref, d_ref, nm_ref, nv_ref = rest[5:]
        g = ga_ref[...] + gb_ref[...]
        g_ref[...] = g
        d_ref[...], nm_ref[...], nv_ref[...] = _adamw_math(w_ref[...], g, m_ref[...], v_ref[...])

    slot = pl.BlockSpec((None, br, cols), lambda i: (l, i, 0))
    blk = pl.BlockSpec((br, cols), lambda i: (i, 0))
    return _call(body, (w, m, v, g_mine, g_theirs) + tuple(prev) + (after,), name="adamw_layer",
                 grid=(rows // br,), in_specs=[slot] * 3 + [blk] * 2 + [ANY] * 5, out_specs=[slot] * 4,
                 out_shape=[jax.ShapeDtypeStruct(w.shape, F32)] * 4,
                 input_output_aliases={5: 0, 6: 1, 7: 2, 8: 3}, semantics=("parallel",))


BIG = ("w_in", "w_out", "w_up", "w_down")
WEIGHTS = ("meta", "attn_norm_g", "w_in", "b_f", "conv_w", "conv_b", "w_gate_a", "b_gate_a", "w_gate_x",
           "b_gate_x", "lru_L", "attn_out_g", "rec_out_g", "w_out", "mlp_norm_g", "w_up", "w_down", "final_g")
SMALL = tuple(k for k in WEIGHTS if k not in BIG)
COL_SHARDED_SMALL = ("meta", "conv_w")


def _packed_rows(shape):
    return -(-math.prod(shape) // (SUBLANES * LANES)) * SUBLANES


def _pack(arrs):
    rows = []
    for a in arrs:
        flat = a.reshape(-1)
        rows.append(jnp.pad(flat, (0, _packed_rows(a.shape) * LANES - flat.shape[0])).reshape(-1, LANES))
    used = sum(r.shape[0] for r in rows)
    rows.append(jnp.zeros((-used % ELEM_ROWS, LANES), F32))
    return jnp.concatenate(rows, axis=0)


def _unpack(buf, shapes):
    out, r0 = [], 0
    for s in shapes:
        nr = _packed_rows(s)
        out.append(buf[r0:r0 + nr].reshape(-1)[:math.prod(s)].reshape(s))
        r0 += nr
    return out


def _halves(a):
    return a.reshape((2, a.shape[0] // 2) + a.shape[1:])


def _cols_from_chips(g):
    return jnp.moveaxis(g, 0, -2).reshape(g.shape[1:-1] + (N_CHIPS * g.shape[-1],))


def kernel(x, meta, attn_norm_g, w_in, b_f, conv_w, conv_b, w_gate_a, b_gate_a, w_gate_x, b_gate_x, lru_L, attn_out_g, rec_out_g, w_out, mlp_norm_g, w_up, w_down, final_g, loss_target, m_meta, m_attn_norm_g, m_w_in, m_b_f, m_conv_w, m_conv_b, m_w_gate_a, m_b_gate_a, m_w_gate_x, m_b_gate_x, m_lru_L, m_attn_out_g, m_rec_out_g, m_w_out, m_mlp_norm_g, m_w_up, m_w_down, m_final_g, v_meta, v_attn_norm_g, v_w_in, v_b_f, v_conv_w, v_conv_b, v_w_gate_a, v_b_gate_a, v_w_gate_x, v_b_gate_x, v_lru_L, v_attn_out_g, v_rec_out_g, v_w_out, v_mlp_norm_g, v_w_up, v_w_down, v_final_g):
    w = dict(meta=meta, attn_norm_g=attn_norm_g, w_in=w_in, b_f=b_f, conv_w=conv_w, conv_b=conv_b,
             w_gate_a=w_gate_a, b_gate_a=b_gate_a, w_gate_x=w_gate_x, b_gate_x=b_gate_x, lru_L=lru_L,
             attn_out_g=attn_out_g, rec_out_g=rec_out_g, w_out=w_out, mlp_norm_g=mlp_norm_g, w_up=w_up,
             w_down=w_down, final_g=final_g)
    m = dict(meta=m_meta, attn_norm_g=m_attn_norm_g, w_in=m_w_in, b_f=m_b_f, conv_w=m_conv_w, conv_b=m_conv_b,
             w_gate_a=m_w_gate_a, b_gate_a=m_b_gate_a, w_gate_x=m_w_gate_x, b_gate_x=m_b_gate_x, lru_L=m_lru_L,
             attn_out_g=m_attn_out_g, rec_out_g=m_rec_out_g, w_out=m_w_out, mlp_norm_g=m_mlp_norm_g,
             w_up=m_w_up, w_down=m_w_down, final_g=m_final_g)
    v = dict(meta=v_meta, attn_norm_g=v_attn_norm_g, w_in=v_w_in, b_f=v_b_f, conv_w=v_conv_w, conv_b=v_conv_b,
             w_gate_a=v_w_gate_a, b_gate_a=v_b_gate_a, w_gate_x=v_w_gate_x, b_gate_x=v_b_gate_x, lru_L=v_lru_L,
             attn_out_g=v_attn_out_g, rec_out_g=v_rec_out_g, w_out=v_w_out, mlp_norm_g=v_mlp_norm_g,
             w_up=v_w_up, w_down=v_w_down, final_g=v_final_g)
    s_len, d = x.shape[1], x.shape[2]
    depth = w_in.shape[0]
    att_w = d // 2
    rec_w = d - att_w
    nh = att_w // HEAD_DIM
    chip = 2 * lax.axis_index("x") + lax.axis_index("y")

    g_conv, g_meta = [g.reshape((N_CHIPS, g.shape[1] * g.shape[2]) + g.shape[3:])
                      for g in _all_gather_chips([_halves(w["conv_w"]), _halves(w["meta"])])]
    p = dict(w)
    p["conv_w"] = _cols_from_chips(g_conv)
    meta_full = jnp.moveaxis(g_meta, 0, 1).reshape(N_META, d)

    chip1 = chip.reshape(1).astype(jnp.int32)
    w_in_t, m_in_t, v_in_t = [jnp.transpose(a["w_in"], (2, 0, 1)) for a in (w, m, v)]
    w_in_slabs = _cast_w_in_t_to_slabs(w_in_t, chip1)
    pushes, tokens = [], []
    for l in range(depth):
        slabs = [w_in_slabs[l]] + [_cast_to_slab(w[k], l, chip1) for k in BIG[1:]]
        send_sems, recv_sems, _, lands, token = _push_start(slabs, slabs, "gather_chips_half", f"weights_start_{l}")
        pushes.append((send_sems, recv_sems, lands))
        tokens.append(token[0, 0])
    passed = [{} for _ in range(depth)]

    def stage(l, after, ids):
        send_sems, recv_sems, lands = pushes[l]
        tag = "_".join(BIG[i] for i in ids)
        sub = [lands[i] for i in ids]
        sub = _push_wait(send_sems, recv_sems, ids, sub, sub, "gather_chips_half", after, f"{tag}_wait_{l}")
        send_sems, recv_sems, _, sub, token = _push_start(sub, sub, "pass_halves", f"{tag}_pass_{l}")
        for j, i in enumerate(ids):
            passed[l][i] = (send_sems, recv_sems, sub[j], j)
        return token[0, 0]

    t_len = N_META + s_len
    pad = -t_len % SEQ_TILE
    h = jnp.concatenate([meta_full, x[0], jnp.zeros((pad, d), F32)], axis=0)
    tgt = jnp.concatenate([jnp.zeros((N_META, d), F32), loss_target[0], jnp.zeros((pad, d), F32)], axis=0)
    z = _rms_fwd(h, _row(p["attn_norm_g"][0] + sum(tokens)))
    stage(0, z, [0])
    saved = []
    for l in range(depth):
        def fetch(k, after, l=l):
            send_sems, recv_sems, land, j = passed[l][BIG.index(k)]
            return _push_wait(send_sems, recv_sems, [j], [land], [land], "pass_halves", after,
                              f"{k}_here_{l}")[0]

        def stage_next(after, which, l=l):
            if which == "own":
                return stage(l, after, [1, 2, 3])
            return stage(l + 1, after, [0]) if l + 1 < depth else 0.0

        g_next = p["attn_norm_g"][l + 1] if l + 1 < depth else p["final_g"]
        h, z, sv = _forward_layer(l, h, z, p, fetch, stage_next, g_next)
        saved.append(sv)
    dh, dh_b, dg_final, loss_part = _loss_bwd(h, _row(p["final_g"]), tgt, s_len)

    small = {k: [None] * depth for k in SMALL if k not in ("meta", "final_g")}
    pushes = [None] * depth
    tok = 0.0
    for l in reversed(range(depth)):
        dh, dh_b, big_mlp, sm_mlp = _backward_mlp(l, dh, dh_b, saved[l], p, tok)
        parts = [big_mlp["w_down"], big_mlp["w_up"]]
        push_mlp = _push_start(parts, [lax.empty(a.shape, a.dtype) for a in parts], "scatter_chips",
                               f"mlp_grads_start_{l}")
        dh, dh_b, big_mix, sm_mix = _backward_mixer(l, dh, dh_b, saved[l], p, push_mlp[4][0, 0])
        parts = [big_mix["w_out"], big_mix["w_in"]]
        push_mix = _push_start(parts, [lax.empty(a.shape, a.dtype) for a in parts], "scatter_chips",
                               f"mixer_grads_start_{l}")
        tok = push_mix[4][0, 0]
        pushes[l] = {("w_down", "w_up"): push_mlp, ("w_out", "w_in"): push_mix}
        for k, val in {**sm_mlp, **sm_mix}.items():
            small[k][l] = val
    grads = {k: jnp.stack(val) for k, val in small.items()}
    grads["final_g"] = dg_final[0]
    grads["meta"] = dh[:N_META]
    dx = dh[N_META:t_len]

    full_shapes = [grads[k].shape for k in SMALL] + [(1,)]
    packed = _pack([grads[k].astype(F32) for k in SMALL] + [loss_part[0, :1] + tok])
    dev1 = (2 * chip + lax.axis_index("c")).reshape(1).astype(jnp.int32)
    slabs = [_place_slab(packed, dev1, N_DEV)]
    small_push = _push_start(slabs, slabs, "gather_devices", "small_grads_start")

    last_token = small_push[4]
    outs = {k: [lax.empty(w[k].shape, F32) for _ in range(4)] for k in BIG[1:]}
    w_in_sums = [None] * depth
    swaps = {}

    def finish(l, wait_after, adam_after):
        send_sems, recv_sems, mine, lands, _ = swaps[l]
        mine, theirs = _push_wait(send_sems, recv_sems, list(range(len(BIG))), mine, lands, "sibling", wait_after,
                                  f"sums_wait_{l}")
        w_in_sums[l] = (mine[0], theirs[0])
        for k, a, b in zip(BIG[1:], mine[1:], theirs[1:]):
            outs[k] = _adamw_layer(w[k], m[k], v[k], l, a, b, outs[k], adam_after)

    for l in reversed(range(depth)):
        sums = {}
        for names, (send_sems, recv_sems, parts, lands, _) in pushes[l].items():
            parts, landed = _push_wait(send_sems, recv_sems, [0, 1], parts, lands, "scatter_chips", last_token,
                                       f"{names[0]}_grads_wait_{l}")
            for k, part, land in zip(names, parts, landed):
                sums[k] = _sum_partials(part, land, chip1)
        mine = [sums[k] for k in BIG]
        swaps[l] = _push_start(mine, [lax.empty(a.shape, a.dtype) for a in mine], "sibling", f"sums_start_{l}")
        if l + 1 < depth:
            finish(l + 1, outs["w_down"][0] if l + 2 < depth else mine[0], swaps[l][4])
    finish(0, outs["w_down"][0] if depth > 1 else swaps[0][4], swaps[0][4])
    outs["w_in"] = [jnp.transpose(r, (1, 2, 0)) for r in _adamw_w_in_t(
        w_in_t, m_in_t, v_in_t, jnp.stack([s[0] for s in w_in_sums], axis=1),
        jnp.stack([s[1] for s in w_in_sums], axis=1))]
    out_g, out_d, out_m, out_v = [{k: outs[k][i] for k in BIG} for i in range(4)]

    landed = _push_wait(small_push[0], small_push[1], [0], small_push[3], small_push[3], "gather_devices",
                        out_g["w_in"], "small_grads_wait")
    total = _sum_slabs(landed[0])
    small_g = dict(zip(SMALL + ("loss",), _unpack(total, full_shapes)))
    for k in COL_SHARDED_SMALL:
        n = w[k].shape[-1]
        small_g[k] = lax.dynamic_slice_in_dim(small_g[k], chip * n, n, axis=small_g[k].ndim - 1)
    local_shapes = [w[k].shape for k in SMALL]
    res = _adamw(_pack([w[k] for k in SMALL]), _pack([small_g[k] for k in SMALL]),
                 _pack([m[k] for k in SMALL]), _pack([v[k] for k in SMALL]))
    out_g.update({k: small_g[k] for k in SMALL})
    for dst, buf in zip((out_d, out_m, out_v), res):
        dst.update(zip(SMALL, _unpack(buf, local_shapes)))

    return (small_g["loss"].reshape(()), dx[None],
            *[out_g[k] for k in WEIGHTS], *[out_d[k] for k in WEIGHTS],
            *[out_m[k] for k in WEIGHTS], *[out_v[k] for k in WEIGHTS])
```

```python
import functools
import math

import jax
import jax.numpy as jnp
from jax import lax
from jax.experimental import pallas as pl
from jax.experimental.pallas import tpu as pltpu

F32 = jnp.float32
BF16 = jnp.bfloat16

N_META = 16
HEAD_DIM = 64
N_REC_BLOCKS = 8
CONV_WIDTH = 4
RG_C = 8.0
NORM_EPS = 1e-6
ADAM_LR = 0.001
ADAM_B1 = 0.9
ADAM_B2 = 0.999
ADAM_EPS = 1e-08
ADAM_WD = 0.01
ADAM_STEP = 10

LANES = 128
SUBLANES = 8
SEQ_TILE = 128
VMEM_CAP = 60 * 2**20
VMEM_SLACK = 6 * 2**20
NEG_BIG = -1e30
N_CHIPS = 4
N_DEV = 8
MESH = pl.DeviceIdType.MESH


def _nbytes(shape, dtype):
    return math.prod(shape) * jnp.dtype(dtype).itemsize


def _call(body, args, *, name, out_shape, grid=(), in_specs=None, out_specs=None, scratch_shapes=(),
          grid_spec=None, semantics=None, vmem_bytes=None, side_effects=None, hbm_results=True, **kw):
    cp = {}
    if semantics is not None:
        cp["dimension_semantics"] = semantics
    if vmem_bytes is not None:
        cp["vmem_limit_bytes"] = int(min(VMEM_CAP, vmem_bytes + VMEM_SLACK))
    if side_effects is not None:
        cp["has_side_effects"] = side_effects
    if grid_spec is not None:
        kw["grid_spec"] = grid_spec
    else:
        kw.update(grid=grid, in_specs=in_specs, out_specs=out_specs, scratch_shapes=scratch_shapes)
    if hbm_results:
        out_shape = jax.tree.map(
            lambda s: pltpu.HBM(s.shape, s.dtype) if isinstance(s, jax.ShapeDtypeStruct) else s, out_shape)
    fn = pl.pallas_call(
        body, name=name, out_shape=out_shape,
        compiler_params=pltpu.CompilerParams(**cp), **kw)
    return fn(*[_in_hbm(a) if jnp.issubdtype(getattr(a, "dtype", jnp.int32), jnp.floating) else a for a in args])


def _divisor_tile(n, unit, target):
    best = None
    for t in range(unit, min(n, target) + 1, unit):
        if n % t == 0:
            best = t
    return n if best is None else best


def _sigmoid(x):
    return 1.0 / (1.0 + jnp.exp(-x))


def _log1p_unit(e):
    series = e * (1.0 - e * (0.5 - e * (1.0 / 3.0)))
    return jnp.where(e < 1e-2, series, jnp.log(1.0 + e))


def _log_sigmoid(x):
    return jnp.minimum(x, 0.0) - _log1p_unit(jnp.exp(-jnp.abs(x)))


def _one_minus_exp(x, exp_x):
    small = -x * (1.0 + x * (1.0 / 2 + x * (1.0 / 6 + x * (1.0 / 24 + x * (1.0 / 120 + x * (1.0 / 720))))))
    return jnp.where(x > -0.25, small, 1.0 - exp_x)


_GELU_K = math.sqrt(2.0 / math.pi)
_GELU_C = 0.044715


def _gelu_and_grad(y):
    th = jnp.tanh(_GELU_K * (y + _GELU_C * y * y * y))
    g = 0.5 * y * (1.0 + th)
    dg = 0.5 * (1.0 + th) + 0.5 * y * (1.0 - th * th) * _GELU_K * (1.0 + 3.0 * _GELU_C * y * y)
    return g, dg


def _rstd(x):
    return lax.rsqrt(jnp.mean(x * x, axis=-1, keepdims=True) + NORM_EPS)


def _rms_bwd(dz, x, g):
    rs = _rstd(x)
    xh = x * rs
    dgp = jnp.sum(dz * xh, axis=0, keepdims=True)
    dxh = dz * g
    dx = rs * (dxh - xh * jnp.mean(dxh * xh, axis=-1, keepdims=True))
    return dx, dgp


def _dot(a, b):
    return jnp.dot(a, b, preferred_element_type=F32)


def _dot_nt(a, b):
    return lax.dot_general(a, b, (((1,), (1,)), ((), ())), preferred_element_type=F32)


def _dot_tn(a, b):
    return lax.dot_general(a, b, (((0,), (0,)), ((), ())), preferred_element_type=F32)


def _full(shape):
    nd = len(shape)
    return pl.BlockSpec(shape, lambda *_: (0,) * nd)


def _rms_fwd(h, g):
    tp, d = h.shape
    tm = _divisor_tile(tp, 16, 544)

    def body(h_ref, g_ref, z_ref):
        x = h_ref[...]
        z_ref[...] = (x * _rstd(x) * g_ref[...]).astype(BF16)

    return _call(body, (h, g), name="rms_fwd", grid=(tp // tm,),
                 in_specs=[pl.BlockSpec((tm, d), lambda i: (i, 0)), _full((1, d))],
                 out_specs=pl.BlockSpec((tm, d), lambda i: (i, 0)),
                 out_shape=jax.ShapeDtypeStruct((tp, d), BF16), semantics=("parallel",))


def _proj(z, w_big_t, att_w):
    tp, d = z.shape
    nb = w_big_t.shape[0]
    tn = _divisor_tile(nb, LANES, 512)
    assert (3 * att_w) % tn == 0
    n_qkv = 3 * att_w // tn
    scale = 1.0 / math.sqrt(HEAD_DIM)

    def body(z_ref, w_ref, qkv_ref, p_ref):
        j = pl.program_id(0)
        acc = _dot_nt(z_ref[...], w_ref[...])

        @pl.when(j < n_qkv)
        def _():
            col = j * tn + lax.broadcasted_iota(jnp.int32, (1, tn), 1)
            qkv_ref[...] = (acc * jnp.where(col < att_w, scale, 1.0)).astype(BF16)

        @pl.when(j >= n_qkv)
        def _():
            p_ref[...] = acc

    vm = 2 * (_nbytes((tp, d), BF16) + _nbytes((d, tn), BF16) + _nbytes((tp, tn), F32) * 2)
    return _call(body, (z, w_big_t), name="proj", grid=(nb // tn,),
                 in_specs=[_full((tp, d)), pl.BlockSpec((tn, d), lambda j: (j, 0))],
                 out_specs=[pl.BlockSpec((tp, tn), lambda j: (0, jnp.minimum(j, n_qkv - 1))),
                            pl.BlockSpec((tp, tn), lambda j: (0, jnp.maximum(j - n_qkv, 0)))],
                 out_shape=[jax.ShapeDtypeStruct((tp, 3 * att_w), BF16),
                            jax.ShapeDtypeStruct((tp, nb - 3 * att_w), F32)],
                 semantics=("arbitrary",), vmem_bytes=vm)


def _tile_cumsum(x, row, reverse=False):
    for s in (1, 2, 4):
        if reverse:
            x = x + jnp.where(row < SUBLANES - s, pltpu.roll(x, SUBLANES - s, 0), 0.0)
        else:
            x = x + jnp.where(row >= s, pltpu.roll(x, s, 0), 0.0)
    return x


def _fgate_fwd(proj, b_f_pad, nh):
    tp, nb = proj.shape
    fblk = nb // LANES - 1

    def body(f_ref, b_ref, c_ref, ct_ref):
        b = b_ref[...]
        row = lax.broadcasted_iota(jnp.int32, (SUBLANES, LANES), 0)

        def step(i, carry):
            r0 = pl.multiple_of(i * SUBLANES, SUBLANES)
            lf = _log_sigmoid(f_ref[pl.ds(r0, SUBLANES), :] + b)
            x = _tile_cumsum(lf, row) + carry
            c_ref[pl.ds(r0, SUBLANES), :] = x
            return x[SUBLANES - 1:SUBLANES, :]

        lax.fori_loop(0, tp // SUBLANES, step, jnp.zeros((1, LANES), F32))
        ct_ref[...] = c_ref[...].T[:nh, :]

    return _call(body, (proj, b_f_pad), name="fgate_fwd", grid=(1,),
                 in_specs=[pl.BlockSpec((tp, LANES), lambda i: (0, fblk)), _full((1, LANES))],
                 out_specs=[_full((tp, LANES)), _full((nh, tp))],
                 out_shape=[jax.ShapeDtypeStruct((tp, LANES), F32), jax.ShapeDtypeStruct((nh, tp), F32)],
                 semantics=("arbitrary",))


def _fgate_bwd(proj, b_f_pad, dc):
    tp, nb = proj.shape
    fblk = nb // LANES - 1

    def body(f_ref, b_ref, dc_ref, df_ref, db_ref, dc_s):
        b = b_ref[...]
        row = lax.broadcasted_iota(jnp.int32, (SUBLANES, LANES), 0)
        nt = tp // SUBLANES

        def step(i, carry):
            suffix, acc = carry
            r0 = pl.multiple_of((nt - 1 - i) * SUBLANES, SUBLANES)
            dlf = _tile_cumsum(dc_ref[pl.ds(r0, SUBLANES), :], row, reverse=True) + suffix
            df = dlf * _sigmoid(-(f_ref[pl.ds(r0, SUBLANES), :] + b))
            dc_s[pl.ds(r0, SUBLANES), :] = df
            return dlf[0:1, :], acc + df

        _, acc = lax.fori_loop(0, nt, step, (jnp.zeros((1, LANES), F32), jnp.zeros((SUBLANES, LANES), F32)))
        df_ref[...] = dc_s[...].astype(BF16)
        db_ref[...] = jnp.broadcast_to(jnp.sum(acc, axis=0, keepdims=True), (SUBLANES, LANES))

    return _call(body, (proj, b_f_pad, dc), name="fgate_bwd", grid=(1,),
                 in_specs=[pl.BlockSpec((tp, LANES), lambda i: (0, fblk)), _full((1, LANES)), _full((tp, LANES))],
                 out_specs=[_full((tp, LANES)), _full((SUBLANES, LANES))],
                 out_shape=[jax.ShapeDtypeStruct((tp, LANES), BF16),
                            jax.ShapeDtypeStruct((SUBLANES, LANES), F32)],
                 scratch_shapes=[pltpu.VMEM((tp, LANES), F32)], semantics=("arbitrary",))


ATT_BQ = 128


ATT_BUCKET = 3
ATT_HEADS = 8


def _for_bucket(i, nq, fn):
    for lo in range(0, nq, ATT_BUCKET):
        hi = min(lo + ATT_BUCKET, nq)
        spans = ([(0, lo * ATT_BQ, False)] if lo else []) + [(lo * ATT_BQ, hi * ATT_BQ, True)]
        pl.when(jnp.logical_and(i >= lo, i < hi))(functools.partial(fn, spans))


def _head_column(c_blk, h):
    lane = lax.broadcasted_iota(jnp.int32, c_blk.shape, 1)
    return jnp.sum(jnp.where(lane == h, c_blk, 0.0), axis=1, keepdims=True)


def _head_columns_into(p, c_ref, ck_s):
    for hh in range(ATT_HEADS):
        ck_s[hh] = jnp.broadcast_to(_head_column(c_ref[...], ATT_HEADS * p + hh), ck_s.shape[1:])


def _pair_diag_cols(x2):
    top = lax.broadcasted_iota(jnp.int32, (LANES, ATT_BQ), 0) < HEAD_DIM
    xt = x2.astype(F32).T.astype(BF16)
    return jnp.concatenate([jnp.where(top, xt, 0), jnp.where(top, 0, xt)], axis=1)


def _pair_diag_rows(x2):
    low = lax.broadcasted_iota(jnp.int32, (ATT_BQ, LANES), 1) < HEAD_DIM
    return jnp.concatenate([jnp.where(low, x2, 0), jnp.where(low, 0, x2)], axis=0)


def _seen_keys(k0, k1, q0):
    keys = k0 + lax.broadcasted_iota(jnp.int32, (k1 - k0, ATT_BQ), 0)
    return keys <= q0 + lax.broadcasted_iota(jnp.int32, (k1 - k0, ATT_BQ), 1)


def _attn_fwd(qkv, c, c_t, nh):
    tp = qkv.shape[0]
    att_w = nh * HEAD_DIM
    ng = nh // ATT_HEADS
    gw = ATT_HEADS * HEAD_DIM
    bq = ATT_BQ
    nq = tp // bq
    pair = 2 * HEAD_DIM
    assert pair == LANES and ATT_HEADS % 2 == 0

    def body(q_ref, k_ref, v_ref, c_ref, ct_ref, o_ref, lse_ref, ck_s, vt_s):
        p = pl.program_id(0)
        i = pl.program_id(1)

        @pl.when(i == 0)
        def _():
            _head_columns_into(p, c_ref, ck_s)
            vt_s[...] = v_ref[...].astype(F32).T.astype(BF16)

        def compute(spans):
            q0 = pl.multiple_of(i * bq, bq)
            o_t, lses = [], []
            for pi in range(ATT_HEADS // 2):
                lo = pair * pi
                heads = (2 * pi, 2 * pi + 1)
                q_cols = _pair_diag_cols(q_ref[:, lo:lo + pair])
                ts = []
                for k0, k1, needs_mask in spans:
                    t2 = _dot(k_ref[k0:k1, lo:lo + pair], q_cols)
                    t_e = [t2[:, e * bq:(e + 1) * bq] - ck_s[hh, k0:k1, :] for e, hh in enumerate(heads)]
                    if needs_mask:
                        seen = _seen_keys(k0, k1, q0)
                        t_e = [jnp.where(seen, t, NEG_BIG) for t in t_e]
                    ts.append(t_e)
                ms = [functools.reduce(jnp.maximum, [jnp.max(t[e], axis=0, keepdims=True) for t in ts])
                      for e in range(2)]
                es = [[jnp.exp(t[e] - ms[e]) for e in range(2)] for t in ts]
                ls = [sum(jnp.sum(e_[e], axis=0, keepdims=True) for e_ in es) for e in range(2)]
                o2 = sum(_dot(vt_s[lo:lo + pair, k0:k1],
                              jnp.concatenate([e_[0].astype(BF16), e_[1].astype(BF16)], axis=1))
                         for e_, (k0, k1, _) in zip(es, spans))
                o_t += [o2[:HEAD_DIM, :bq] / ls[0], o2[HEAD_DIM:, bq:] / ls[1]]
                lses += [ms[e] + ct_ref[pl.ds(ATT_HEADS * p + hh, 1), :] + jnp.log(ls[e])
                         for e, hh in enumerate(heads)]
            o_ref[...] = jnp.concatenate(o_t, axis=0).T
            lse_ref[...] = jnp.concatenate(lses, axis=0)

        _for_bucket(i, nq, compute)

    blk = pl.BlockSpec((bq, gw), lambda p, i: (i, p))
    vm = 6 * _nbytes((tp, gw), BF16) + 2 * ATT_HEADS * _nbytes((tp, LANES), F32) + 2 * _nbytes((tp, LANES), F32) \
        + 8 * ATT_HEADS * _nbytes((bq, tp), F32)
    return _call(body, (qkv, qkv, qkv, c, c_t), name="attn_fwd", grid=(ng, nq),
                 in_specs=[blk,
                           pl.BlockSpec((tp, gw), lambda p, i: (0, ng + p)),
                           pl.BlockSpec((tp, gw), lambda p, i: (0, 2 * ng + p)),
                           _full((tp, LANES)), pl.BlockSpec((nh, bq), lambda p, i: (0, i))],
                 out_specs=[blk, pl.BlockSpec((None, ATT_HEADS, bq), lambda p, i: (p, 0, i))],
                 out_shape=[jax.ShapeDtypeStruct((tp, att_w), F32), jax.ShapeDtypeStruct((ng, ATT_HEADS, tp), F32)],
                 scratch_shapes=[pltpu.VMEM((ATT_HEADS, tp, LANES), F32), pltpu.VMEM((gw, tp), BF16)],
                 semantics=("arbitrary", "arbitrary"), vmem_bytes=vm)


def _attn_bwd(qkv, c, c_t, lse, do, nh):
    tp = qkv.shape[0]
    att_w = nh * HEAD_DIM
    ng = nh // ATT_HEADS
    gw = ATT_HEADS * HEAD_DIM
    bq = ATT_BQ
    nq = tp // bq
    pair = 2 * HEAD_DIM
    assert pair == LANES and ATT_HEADS % 2 == 0
    scale = 1.0 / math.sqrt(HEAD_DIM)

    def body(q_ref, k_ref, v_ref, c_ref, ct_ref, lse_ref, do_ref, dq_ref, dk_ref, dv_ref, dc_ref,
             dk_s, dv_s, dc_s, ck_s, kt_s):
        p = pl.program_id(0)
        i = pl.program_id(1)

        @pl.when(i == 0)
        def _():
            dk_s[...] = jnp.zeros_like(dk_s)
            dv_s[...] = jnp.zeros_like(dv_s)
            dc_s[...] = jnp.zeros_like(dc_s)
            kt_s[...] = k_ref[...].astype(F32).T.astype(BF16)
            _head_columns_into(p, c_ref, ck_s)

        @pl.when(jnp.logical_and(i == 0, p == 0))
        def _():
            dc_ref[...] = jnp.zeros_like(dc_ref)

        def compute(spans):
            q0 = pl.multiple_of(i * bq, bq)
            dq_t = []
            for pi in range(ATT_HEADS // 2):
                lo = pair * pi
                q2 = q_ref[:, lo:lo + pair]
                do2 = do_ref[:, lo:lo + pair].astype(BF16)
                q_cols, do_cols = _pair_diag_cols(q2), _pair_diag_cols(do2)
                q_rows, do_rows = _pair_diag_rows(q2), _pair_diag_rows(do2)
                heads = (2 * pi, 2 * pi + 1)
                col_terms = [ct_ref[pl.ds(ATT_HEADS * p + hh, 1), :] - lse_ref[hh:hh + 1, :] for hh in heads]
                prs, dps = [], []
                for k0, k1, needs_mask in spans:
                    t2 = _dot(k_ref[k0:k1, lo:lo + pair], q_cols)
                    dp2 = _dot(v_ref[k0:k1, lo:lo + pair], do_cols)
                    if needs_mask:
                        seen = _seen_keys(k0, k1, q0)
                    pr_e, dp_e = [], []
                    for e, hh in enumerate(heads):
                        t = t2[:, e * bq:(e + 1) * bq] - ck_s[hh, k0:k1, :]
                        if needs_mask:
                            t = jnp.where(seen, t, NEG_BIG)
                        pr_e.append(jnp.exp(t + col_terms[e]))
                        dp_e.append(dp2[:, e * bq:(e + 1) * bq])
                    prs.append(pr_e)
                    dps.append(dp_e)
                key_sums = [sum(jnp.sum(pr[e] * dp[e], axis=0, keepdims=True) for pr, dp in zip(prs, dps))
                            for e in range(2)]
                dq2 = 0.0
                for (k0, k1, _), pr, dp in zip(spans, prs, dps):
                    ds = [pr[e] * (dp[e] - key_sums[e]) for e in range(2)]
                    for e, hh in enumerate(heads):
                        dc_s[hh, k0:k1, :] += jnp.sum(ds[e], axis=1, keepdims=True)
                    ds2 = jnp.concatenate([ds[0].astype(BF16), ds[1].astype(BF16)], axis=1)
                    pr2 = jnp.concatenate([pr[0].astype(BF16), pr[1].astype(BF16)], axis=1)
                    dk_s[k0:k1, lo:lo + pair] += _dot(ds2, q_rows)
                    dv_s[k0:k1, lo:lo + pair] += _dot(pr2, do_rows)
                    dq2 = dq2 + _dot(kt_s[lo:lo + pair, k0:k1], ds2)
                dq_t.append(jnp.concatenate([dq2[:HEAD_DIM, :bq], dq2[HEAD_DIM:, bq:]], axis=0))
            dq_ref[...] = (jnp.concatenate(dq_t, axis=0) * scale).T.astype(BF16)

        _for_bucket(i, nq, compute)

        @pl.when(i == nq - 1)
        def _():
            dk_ref[...] = dk_s[...].astype(BF16)
            dv_ref[...] = dv_s[...].astype(BF16)
            lane = lax.broadcasted_iota(jnp.int32, (tp, LANES), 1)
            dc = dc_ref[...]
            for hh in range(ATT_HEADS):
                dc = jnp.where(lane == ATT_HEADS * p + hh, -dc_s[hh], dc)
            dc_ref[...] = dc

    blk = pl.BlockSpec((bq, gw), lambda p, i: (i, p))
    col = pl.BlockSpec((tp, gw), lambda p, i: (0, p))
    vm = 7 * _nbytes((tp, gw), BF16) + 2 * _nbytes((tp, gw), F32) + 2 * ATT_HEADS * _nbytes((tp, LANES), F32) \
        + 2 * _nbytes((tp, LANES), F32) + 12 * ATT_HEADS * _nbytes((bq, tp), F32)
    return _call(body, (qkv, qkv, qkv, c, c_t, lse, do), name="attn_bwd", grid=(ng, nq),
                 in_specs=[blk,
                           pl.BlockSpec((tp, gw), lambda p, i: (0, ng + p)),
                           pl.BlockSpec((tp, gw), lambda p, i: (0, 2 * ng + p)),
                           _full((tp, LANES)), pl.BlockSpec((nh, bq), lambda p, i: (0, i)),
                           pl.BlockSpec((None, ATT_HEADS, bq), lambda p, i: (p, 0, i)), blk],
                 out_specs=[blk, col, col, _full((tp, LANES))],
                 out_shape=[jax.ShapeDtypeStruct((tp, att_w), BF16)] * 3 + [jax.ShapeDtypeStruct((tp, LANES), F32)],
                 scratch_shapes=[pltpu.VMEM((tp, gw), F32), pltpu.VMEM((tp, gw), F32),
                                 pltpu.VMEM((ATT_HEADS, tp, 1), F32), pltpu.VMEM((ATT_HEADS, tp, LANES), F32),
                                 pltpu.VMEM((gw, tp), BF16)],
                 semantics=("arbitrary", "arbitrary"), vmem_bytes=vm)


REC_ROWS = 128
HALO = SUBLANES


def _conv_taps(cat):
    taps = []
    for k in range(CONV_WIDTH):
        sh = CONV_WIDTH - 1 - k
        taps.append((pltpu.roll(cat, sh, 0) if sh else cat)[HALO:])
    return taps


def _rec_gates(xc, wa_ref, ba_ref, wx_ref, bx_ref, l_ref):
    xcb = xc.astype(BF16)
    r = _sigmoid(_dot(xcb, wa_ref[...]) + ba_ref[...])
    ig = _sigmoid(_dot(xcb, wx_ref[...]) + bx_ref[...])
    ls = _log_sigmoid(l_ref[...])
    log_a = RG_C * r * ls
    return xcb, r, ig, ls, log_a


def _rec_fwd(proj, xr_blk, yr_blk, rec_w, conv_w, conv_b, wa, ba, wx, bx, lru):
    tp = proj.shape[0]
    w = rec_w
    r_rows = REC_ROWS
    nc = tp // r_rows
    cpb = w // LANES

    def body(xr_ref, yr_ref, cw_ref, cb_ref, wa_ref, ba_ref, wx_ref, bx_ref, l_ref,
             hr_ref, rec_ref, prev_s, carry_s, a_s, u_s):
        i = pl.program_id(0)

        @pl.when(i == 0)
        def _():
            prev_s[...] = jnp.zeros_like(prev_s)
            carry_s[...] = jnp.zeros_like(carry_s)

        x = xr_ref[...]
        taps = _conv_taps(jnp.concatenate([prev_s[...], x], axis=0))
        prev_s[...] = x[r_rows - HALO:]
        xc = cb_ref[...]
        for k in range(CONV_WIDTH):
            xc = xc + cw_ref[k:k + 1, :] * taps[k]
        _, r, ig, ls, log_a = _rec_gates(xc, wa_ref, ba_ref, wx_ref, bx_ref, l_ref)
        a = jnp.exp(log_a)
        a_s[...] = a
        u_s[...] = jnp.sqrt(_one_minus_exp(2.0 * log_a, a * a)) * ig * xc

        def tile(j, h):
            r0 = pl.multiple_of(j * SUBLANES, SUBLANES)
            at = a_s[pl.ds(r0, SUBLANES), :]
            ut = u_s[pl.ds(r0, SUBLANES), :]
            out = []
            for rr in range(SUBLANES):
                h = at[rr:rr + 1] * h + ut[rr:rr + 1]
                out.append(h)
            hr_ref[pl.ds(r0, SUBLANES), :] = jnp.concatenate(out, axis=0)
            return h

        carry_s[0:1, :] = lax.fori_loop(0, r_rows // SUBLANES, tile, carry_s[0:1, :])
        g, _ = _gelu_and_grad(yr_ref[...])
        rec_ref[...] = hr_ref[...] * g

    blk = pl.BlockSpec((r_rows, w), lambda i: (i, 0))
    vm = 16 * _nbytes((r_rows, w), F32) + 4 * _nbytes((w, w), BF16)
    return _call(body, (proj, proj, conv_w, conv_b, wa, ba, wx, bx, lru), name="rec_fwd", grid=(nc,),
                 in_specs=[pl.BlockSpec((r_rows, w), lambda i: (i, xr_blk)),
                           pl.BlockSpec((r_rows, w), lambda i: (i, yr_blk)),
                           _full((CONV_WIDTH, w)), _full((1, w)), _full((w, w)), _full((1, w)),
                           _full((w, w)), _full((1, w)), _full((1, w))],
                 out_specs=[blk, blk],
                 out_shape=[jax.ShapeDtypeStruct((tp, w), F32)] * 2,
                 scratch_shapes=[pltpu.VMEM((HALO, w), F32), pltpu.VMEM((SUBLANES, w), F32),
                                 pltpu.VMEM((r_rows, w), F32), pltpu.VMEM((r_rows, w), F32)],
                 semantics=("arbitrary",), vmem_bytes=vm)


def _rec_bwd(proj, xr_blk, yr_blk, rec_w, hr, drec, conv_w, conv_b, wa, ba, wx, bx, lru):
    tp = proj.shape[0]
    w = rec_w
    r_rows = REC_ROWS
    nc = tp // r_rows
    hpc = r_rows // HALO

    def body(xr_ref, xh_ref, yr_ref, hr_ref, hh_ref, drec_ref, cw_ref, cb_ref, wa_ref, ba_ref, wx_ref, bx_ref,
             l_ref, dxr_ref, dyr_ref, dwa_ref, dwx_ref, small_ref, lam_s, a_s, dhr_s, carry_s, next_s):
        i = pl.program_id(0)
        first = (nc - 1 - i) == 0

        @pl.when(i == 0)
        def _():
            carry_s[...] = jnp.zeros_like(carry_s)
            next_s[...] = jnp.zeros_like(next_s)
            dwa_ref[...] = jnp.zeros_like(dwa_ref)
            dwx_ref[...] = jnp.zeros_like(dwx_ref)
            small_ref[...] = jnp.zeros_like(small_ref)

        x = xr_ref[...]
        xprev = jnp.where(first, 0.0, xh_ref[...])
        taps = _conv_taps(jnp.concatenate([xprev, x], axis=0))
        xc = cb_ref[...]
        for k in range(CONV_WIDTH):
            xc = xc + cw_ref[k:k + 1, :] * taps[k]
        xcb, r, ig, ls, log_a = _rec_gates(xc, wa_ref, ba_ref, wx_ref, bx_ref, l_ref)
        a = jnp.exp(log_a)
        a2 = a * a
        mult = jnp.sqrt(_one_minus_exp(2.0 * log_a, a2))
        g, dg = _gelu_and_grad(yr_ref[...])
        hr_v = hr_ref[...]
        drec_v = drec_ref[...]
        dhr_s[...] = drec_v * g
        dyr_ref[...] = (drec_v * hr_v * dg).astype(BF16)
        a_s[...] = a

        def tile(jj, carry):
            r0 = pl.multiple_of((r_rows // SUBLANES - 1 - jj) * SUBLANES, SUBLANES)
            at = a_s[pl.ds(r0, SUBLANES), :]
            dt = dhr_s[pl.ds(r0, SUBLANES), :]
            out = [None] * SUBLANES
            for rr in range(SUBLANES - 1, -1, -1):
                lam = dt[rr:rr + 1] + carry
                out[rr] = lam
                carry = at[rr:rr + 1] * lam
            lam_s[pl.ds(r0, SUBLANES), :] = jnp.concatenate(out, axis=0)
            return carry

        carry_s[0:1, :] = lax.fori_loop(0, r_rows // SUBLANES, tile, carry_s[0:1, :])
        lam = lam_s[...]
        hprev = jnp.where(first, 0.0, hh_ref[...])
        hr_prev = pltpu.roll(jnp.concatenate([hprev, hr_v], axis=0), 1, 0)[HALO:]
        da = lam * hr_prev
        dxc = lam * mult * ig
        di = lam * mult * xc
        dmult = lam * ig * xc
        dlog_a = da * a - dmult * a2 / mult
        dr = dlog_a * (RG_C * ls)
        dls = jnp.sum(dlog_a * (RG_C * r), axis=0, keepdims=True)
        dga = dr * r * (1.0 - r)
        dgx = di * ig * (1.0 - ig)
        dgab = dga.astype(BF16)
        dgxb = dgx.astype(BF16)
        dxc = dxc + _dot_nt(dgab, wa_ref[...]) + _dot_nt(dgxb, wx_ref[...])
        dwa_ref[...] += _dot_tn(xcb, dgab)
        dwx_ref[...] += _dot_tn(xcb, dgxb)
        cat = jnp.concatenate([dxc, next_s[...]], axis=0)
        next_s[...] = dxc[0:HALO]
        dxr = cw_ref[CONV_WIDTH - 1:CONV_WIDTH, :] * dxc
        for k in range(CONV_WIDTH - 1):
            sh = CONV_WIDTH - 1 - k
            dxr = dxr + cw_ref[k:k + 1, :] * pltpu.roll(cat, r_rows + HALO - sh, 0)[:r_rows]
        dxr_ref[...] = dxr.astype(BF16)
        rows = [jnp.sum(dxc * taps[k], axis=0, keepdims=True) for k in range(CONV_WIDTH)]
        rows += [jnp.sum(dxc, axis=0, keepdims=True), jnp.sum(dga, axis=0, keepdims=True),
                 jnp.sum(dgx, axis=0, keepdims=True), dls * _sigmoid(-l_ref[...])]
        small_ref[...] += jnp.concatenate(rows, axis=0)

    def rev(i):
        return nc - 1 - i

    def halo(i):
        return jnp.maximum(rev(i) * hpc - 1, 0)

    blk = pl.BlockSpec((r_rows, w), lambda i: (rev(i), 0))
    vm = 40 * _nbytes((r_rows, w), F32) + 6 * _nbytes((w, w), F32)
    return _call(body, (proj, proj, proj, hr, hr, drec, conv_w, conv_b, wa, ba, wx, bx, lru),
                 name="rec_bwd", grid=(nc,),
                 in_specs=[pl.BlockSpec((r_rows, w), lambda i: (rev(i), xr_blk)),
                           pl.BlockSpec((HALO, w), lambda i: (halo(i), xr_blk)),
                           pl.BlockSpec((r_rows, w), lambda i: (rev(i), yr_blk)),
                           blk,
                           pl.BlockSpec((HALO, w), lambda i: (halo(i), 0)),
                           blk,
                           _full((CONV_WIDTH, w)), _full((1, w)), _full((w, w)), _full((1, w)),
                           _full((w, w)), _full((1, w)), _full((1, w))],
                 out_specs=[blk, blk, _full((w, w)), _full((w, w)), _full((SUBLANES, w))],
                 out_shape=[jax.ShapeDtypeStruct((tp, w), BF16)] * 2
                 + [jax.ShapeDtypeStruct((w, w), F32)] * 2 + [jax.ShapeDtypeStruct((SUBLANES, w), F32)],
                 scratch_shapes=[pltpu.VMEM((r_rows, w), F32)] * 3
                 + [pltpu.VMEM((SUBLANES, w), F32), pltpu.VMEM((HALO, w), F32)],
                 semantics=("arbitrary",), vmem_bytes=vm)


ROW_TARGET = 544


def _mixer_out(attn, rec, g_a, g_r, w_out, h, g_next):
    tp, d = h.shape
    aw, rw = attn.shape[1], rec.shape[1]
    kc = d // N_CHIPS
    tm = _divisor_tile(tp, 16, ROW_TARGET)

    def body(a_ref, r_ref, ga_ref, gr_ref, w_ref, h_ref, gn_ref, h1_ref, z_ref, mix_ref):
        a = a_ref[...]
        r = r_ref[...]
        mix = jnp.concatenate([a * _rstd(a) * ga_ref[...], r * _rstd(r) * gr_ref[...]], axis=1).astype(BF16)
        mix_ref[...] = mix
        h1 = h_ref[...]
        for j in range(N_CHIPS):
            h1 = h1 + _dot(mix[:, j * kc:(j + 1) * kc], w_ref[j])
        h1_ref[...] = h1
        z_ref[...] = (h1 * _rstd(h1) * gn_ref[...]).astype(BF16)

    row = lambda wd: pl.BlockSpec((tm, wd), lambda i: (i, 0))
    vm = 2 * _nbytes((d, d), BF16) + 12 * _nbytes((tm, d), F32)
    return _call(body, (attn, rec, g_a, g_r, w_out, h, g_next), name="mixer_out", grid=(tp // tm,),
                 in_specs=[row(aw), row(rw), _full((1, aw)), _full((1, rw)), _full(w_out.shape), row(d),
                           _full((1, d))],
                 out_specs=[row(d), row(d), row(d)],
                 out_shape=[jax.ShapeDtypeStruct((tp, d), F32), jax.ShapeDtypeStruct((tp, d), BF16),
                            jax.ShapeDtypeStruct((tp, d), BF16)],
                 semantics=("parallel",), vmem_bytes=vm)


def _mixer_bwd(dh_b, w_out, attn, rec, g_a, g_r):
    tp, d = dh_b.shape
    aw, rw = attn.shape[1], rec.shape[1]
    tm = _divisor_tile(tp, 16, ROW_TARGET)

    def body(dh_ref, w_ref, a_ref, r_ref, ga_ref, gr_ref, da_ref, dr_ref, dg_ref):
        @pl.when(pl.program_id(0) == 0)
        def _():
            dg_ref[...] = jnp.zeros_like(dg_ref)

        dh = dh_ref[...]
        dmix = jnp.concatenate([_dot_nt(dh, w_ref[j]) for j in range(N_CHIPS)], axis=1)
        da, dga = _rms_bwd(dmix[:, :aw], a_ref[...], ga_ref[...])
        dr, dgr = _rms_bwd(dmix[:, aw:], r_ref[...], gr_ref[...])
        da_ref[...] = da
        dr_ref[...] = dr
        dg_ref[...] += jnp.broadcast_to(jnp.concatenate([dga, dgr], axis=1), (SUBLANES, d))

    row = lambda wd: pl.BlockSpec((tm, wd), lambda i: (i, 0))
    vm = 2 * _nbytes((d, d), BF16) + 12 * _nbytes((tm, d), F32)
    return _call(body, (dh_b, w_out, attn, rec, g_a, g_r), name="mixer_bwd", grid=(tp // tm,),
                 in_specs=[row(d), _full(w_out.shape), row(aw), row(rw), _full((1, aw)), _full((1, rw))],
                 out_specs=[row(aw), row(rw), _full((SUBLANES, d))],
                 out_shape=[jax.ShapeDtypeStruct((tp, aw), F32), jax.ShapeDtypeStruct((tp, rw), F32),
                            jax.ShapeDtypeStruct((SUBLANES, d), F32)],
                 semantics=("arbitrary",), vmem_bytes=vm)


def _mlp_up(z, w_up):
    tp, d = z.shape
    fc = w_up.shape[2]
    ff = N_CHIPS * fc
    tn = _divisor_tile(fc, LANES, 512)
    per = fc // tn

    def body(z_ref, w_ref, act_ref, up_ref):
        up = _dot(z_ref[...], w_ref[...])
        r = jnp.maximum(up, 0.0)
        act_ref[...] = (r * r).astype(BF16)
        up_ref[...] = up.astype(BF16)

    col = pl.BlockSpec((tp, tn), lambda j: (0, j))
    vm = 2 * _nbytes((tp, d), BF16) + 2 * _nbytes((d, tn), BF16) + 8 * _nbytes((tp, tn), F32)
    return _call(body, (z, w_up), name="mlp_up", grid=(ff // tn,),
                 in_specs=[_full((tp, d)), pl.BlockSpec((None, d, tn), lambda j: (j // per, 0, j % per))],
                 out_specs=[col, col],
                 out_shape=[jax.ShapeDtypeStruct((tp, ff), BF16)] * 2,
                 semantics=("parallel",), vmem_bytes=vm)


def _mlp_down(act, w_down, h, g_next):
    tp, d = h.shape
    ff = act.shape[1]
    fc = ff // N_CHIPS
    tm = _divisor_tile(tp, 16, ROW_TARGET)

    def body(a_ref, w_ref, h_ref, gn_ref, h2_ref, z_ref):
        h2 = h_ref[...]
        for j in range(N_CHIPS):
            h2 = h2 + _dot(a_ref[:, j * fc:(j + 1) * fc], w_ref[j])
        h2_ref[...] = h2
        z_ref[...] = (h2 * _rstd(h2) * gn_ref[...]).astype(BF16)

    row = lambda wd: pl.BlockSpec((tm, wd), lambda i: (i, 0))
    vm = 2 * _nbytes((ff, d), BF16) + 2 * _nbytes((tm, ff), BF16) + 10 * _nbytes((tm, d), F32)
    return _call(body, (act, w_down, h, g_next), name="mlp_down", grid=(tp // tm,),
                 in_specs=[row(ff), _full(w_down.shape), row(d), _full((1, d))],
                 out_specs=[row(d), row(d)],
                 out_shape=[jax.ShapeDtypeStruct((tp, d), F32), jax.ShapeDtypeStruct((tp, d), BF16)],
                 semantics=("parallel",), vmem_bytes=vm)


def _loss_bwd(h, g, target, n_real):
    tp, d = h.shape
    tm = _divisor_tile(tp, 16, ROW_TARGET)

    def body(h_ref, g_ref, t_ref, dh_ref, dhb_ref, dg_ref, loss_ref):
        i = pl.program_id(0)

        @pl.when(i == 0)
        def _():
            dg_ref[...] = jnp.zeros_like(dg_ref)
            loss_ref[...] = jnp.zeros_like(loss_ref)

        x = h_ref[...]
        gv = g_ref[...]
        rowi = i * tm + lax.broadcasted_iota(jnp.int32, (tm, 1), 0)
        real = jnp.logical_and(rowi >= N_META, rowi < N_META + n_real)
        err = jnp.where(real, x * _rstd(x) * gv - t_ref[...], 0.0)
        loss_ref[...] += 0.5 * jnp.sum(jnp.mean(err * err, axis=-1, keepdims=True))
        dx, dgp = _rms_bwd(err * (1.0 / d), x, gv)
        dh_ref[...] = dx
        dhb_ref[...] = dx.astype(BF16)
        dg_ref[...] += jnp.broadcast_to(dgp, (SUBLANES, d))

    row = pl.BlockSpec((tm, d), lambda i: (i, 0))
    return _call(body, (h, g, target), name="loss_bwd", grid=(tp // tm,),
                 in_specs=[row, _full((1, d)), row],
                 out_specs=[row, row, _full((SUBLANES, d)), _full((SUBLANES, LANES))],
                 out_shape=[jax.ShapeDtypeStruct((tp, d), F32), jax.ShapeDtypeStruct((tp, d), BF16),
                            jax.ShapeDtypeStruct((SUBLANES, d), F32), jax.ShapeDtypeStruct((SUBLANES, LANES), F32)],
                 semantics=("arbitrary",), vmem_bytes=16 * _nbytes((tm, d), F32))


def _mlp_bwd(dh_b, w_down, up, z2):
    tp, d = dh_b.shape
    fc = w_down.shape[1]
    ff = N_CHIPS * fc
    tn = _divisor_tile(fc, LANES, 512)
    per = fc // tn

    def body(dh_ref, z_ref, w_ref, up_ref, dup_ref, gd_ref, gu_ref):
        dh = dh_ref[...]
        r = jnp.maximum(up_ref[...].astype(F32), 0.0)
        dup = (_dot_nt(dh, w_ref[...]) * (2.0 * r)).astype(BF16)
        dup_ref[...] = dup
        gd_ref[...] = _dot_tn((r * r).astype(BF16), dh).astype(BF16)
        gu_ref[...] = _dot_tn(z_ref[...], dup).astype(BF16)

    col = pl.BlockSpec((tp, tn), lambda j: (0, j))
    vm = 4 * _nbytes((tp, d), BF16) + 4 * _nbytes((tn, d), BF16) + 2 * _nbytes((d, tn), BF16) \
        + 10 * _nbytes((tp, tn), F32) + 4 * _nbytes((tn, d), F32)
    return _call(body, (dh_b, z2, w_down, up), name="mlp_bwd", grid=(ff // tn,),
                 in_specs=[_full((tp, d)), _full((tp, d)),
                           pl.BlockSpec((None, tn, d), lambda j: (j // per, j % per, 0)), col],
                 out_specs=[col, pl.BlockSpec((tn, d), lambda j: (j, 0)),
                            pl.BlockSpec((None, d, tn), lambda j: (j // per, 0, j % per))],
                 out_shape=[jax.ShapeDtypeStruct((tp, ff), BF16), jax.ShapeDtypeStruct((ff, d), BF16),
                            jax.ShapeDtypeStruct((N_CHIPS, d, fc), BF16)],
                 semantics=("parallel",), vmem_bytes=vm)


def _grad_w_pieces(pieces, b):
    tp, n = b.shape
    tn = _divisor_tile(n, LANES, 512)
    widths = [pc.shape[1] for pc in pieces]

    def body(*refs):
        p_refs, b_ref, o_refs = refs[:len(pieces)], refs[len(pieces)], refs[len(pieces) + 1:]
        for p_ref, o_ref in zip(p_refs, o_refs):
            o_ref[...] = _dot_tn(p_ref[...], b_ref[...]).astype(BF16)

    vm = 2 * sum(_nbytes((tp, wd), BF16) for wd in widths) + 2 * _nbytes((tp, tn), BF16) \
        + 4 * sum(_nbytes((wd, tn), F32) for wd in widths) + 2 * _nbytes((tp, max(widths)), F32)
    return _call(body, tuple(pieces) + (b,), name="grad_w_pieces", grid=(n // tn,),
                 in_specs=[_full(pc.shape) for pc in pieces] + [pl.BlockSpec((tp, tn), lambda j: (0, j))],
                 out_specs=[pl.BlockSpec((wd, tn), lambda j: (0, j)) for wd in widths],
                 out_shape=[jax.ShapeDtypeStruct((wd, n), BF16) for wd in widths],
                 semantics=("parallel",), vmem_bytes=vm)


def _dx_norm_bwd(pieces, w, w_spec, w_piece, h, g, dres, dot=_dot_nt):
    tp, d = h.shape
    tm = _divisor_tile(tp, 16, ROW_TARGET)
    n = len(pieces)

    def body(*refs):
        dy_refs = refs[:n]
        w_ref, h_ref, g_ref, dres_ref, dh_ref, dhb_ref, dg_ref = refs[n:]

        @pl.when(pl.program_id(0) == 0)
        def _():
            dg_ref[...] = jnp.zeros_like(dg_ref)

        dz = dot(dy_refs[0][...], w_piece(w_ref, 0))
        for i in range(1, n):
            dz = dz + dot(dy_refs[i][...], w_piece(w_ref, i))
        dx, dgp = _rms_bwd(dz, h_ref[...], g_ref[...])
        dh = dres_ref[...] + dx
        dh_ref[...] = dh
        dhb_ref[...] = dh.astype(BF16)
        dg_ref[...] += jnp.broadcast_to(dgp, (SUBLANES, d))

    row = lambda wd: pl.BlockSpec((tm, wd), lambda i: (i, 0))
    kk = sum(wd for _, _, wd in pieces)
    vm = 2 * _nbytes((d, kk), BF16) + 2 * _nbytes((tm, kk), BF16) + 14 * _nbytes((tm, d), F32)
    piece_specs = [pl.BlockSpec((tm, wd), functools.partial(lambda i, cb: (i, cb), cb=cb)) for _, cb, wd in pieces]
    return _call(body, tuple(a for a, _, _ in pieces) + (w, h, g, dres), name="dx_norm_bwd", grid=(tp // tm,),
                 in_specs=piece_specs + [w_spec, row(d), _full((1, d)), row(d)],
                 out_specs=[row(d), row(d), _full((SUBLANES, d))],
                 out_shape=[jax.ShapeDtypeStruct((tp, d), F32), jax.ShapeDtypeStruct((tp, d), BF16),
                            jax.ShapeDtypeStruct((SUBLANES, d), F32)],
                 semantics=("arbitrary",), vmem_bytes=vm)


def _block_diag(wg):
    nb, b, _ = wg.shape
    eye = jnp.eye(nb, dtype=wg.dtype)
    return (eye[:, None, :, None] * wg[:, :, None, :]).reshape(nb * b, nb * b)


def _diag_blocks(dense, nb):
    b = dense.shape[0] // nb
    d4 = dense.reshape(nb, b, nb, b)
    return jnp.stack([d4[i, :, i, :] for i in range(nb)])


def _row(v):
    return v.reshape(1, -1)


def _forward_layer(l, h, z, p, fetch, stage_next, g_next):
    d = h.shape[1]
    att_w = d // 2
    rec_w = d - att_w
    nh = att_w // HEAD_DIM
    wa_d = _block_diag(p["w_gate_a"][l]).astype(BF16)
    wx_d = _block_diag(p["w_gate_x"][l]).astype(BF16)
    b_f_pad = jnp.zeros((1, LANES), F32).at[0, :nh].set(p["b_f"][l])
    w_in_t = fetch("w_in", z)
    big = dict(w_in_big=_pack_w_in_t(w_in_t.reshape(-1, d), att_w, nh))
    qkv, proj = _proj(z, big["w_in_big"], att_w)
    c, c_t = _fgate_fwd(proj, b_f_pad, nh)
    attn, lse_b = _attn_fwd(qkv, c, c_t, nh)
    hr, rec = _rec_fwd(proj, 0, 1, rec_w, p["conv_w"][l], _row(p["conv_b"][l]), wa_d,
                       _row(p["b_gate_a"][l]), wx_d, _row(p["b_gate_x"][l]), _row(p["lru_L"][l]))
    tok = stage_next(attn, "own")
    big["w_out"] = fetch("w_out", rec)
    h1, z2, mix = _mixer_out(attn, rec, _row(p["attn_out_g"][l] + tok), _row(p["rec_out_g"][l]),
                             big["w_out"], h, _row(p["mlp_norm_g"][l]))
    big["w_up"] = fetch("w_up", h1)
    act, up = _mlp_up(z2, big["w_up"])
    tok = stage_next(act, "next")
    big["w_down"] = fetch("w_down", act)
    h2, z_next = _mlp_down(act, big["w_down"], h1, _row(g_next + tok))
    saved = dict(h0=h, z1=z, proj=proj, qkv=qkv, c=c, c_t=c_t, attn=attn, lse_b=lse_b, hr=hr, rec=rec, h1=h1,
                 z2=z2, mix=mix, up=up, wa_d=wa_d, wx_d=wx_d, b_f_pad=b_f_pad, big=big)
    return h2, z_next, saved


def _backward_mlp(l, dh, dh_b, sv, p, tok):
    w_up, w_down = sv["big"]["w_up"], sv["big"]["w_down"]
    fc = w_up.shape[2]
    dup, g_down, g_up = _mlp_bwd(dh_b, w_down, sv["up"], sv["z2"])
    dh, dh_b, dg2 = _dx_norm_bwd([(dup, j, fc) for j in range(N_CHIPS)], w_up, _full(w_up.shape),
                                 lambda w_ref, j: w_ref[j], sv["h1"], _row(p["mlp_norm_g"][l] + tok), dh)
    big = dict(w_down=g_down.reshape((N_CHIPS, -1) + g_down.shape[1:]), w_up=g_up)
    return dh, dh_b, big, dict(mlp_norm_g=dg2[0])


def _backward_mixer(l, dh, dh_b, sv, p, tok):
    d = dh.shape[1]
    att_w = d // 2
    rec_w = d - att_w
    nh = att_w // HEAD_DIM
    small = {}
    g_out, = _grad_w_pieces([sv["mix"]], dh_b)
    dattn, drec, dg_mix = _mixer_bwd(dh_b, sv["big"]["w_out"], sv["attn"], sv["rec"],
                                     _row(p["attn_out_g"][l] + tok), _row(p["rec_out_g"][l]))
    small["attn_out_g"] = dg_mix[0, :att_w]
    small["rec_out_g"] = dg_mix[0, att_w:]
    dxr, dyr, dwa, dwx, sm = _rec_bwd(
        sv["proj"], 0, 1, rec_w, sv["hr"], drec, p["conv_w"][l], _row(p["conv_b"][l]), sv["wa_d"],
        _row(p["b_gate_a"][l]), sv["wx_d"], _row(p["b_gate_x"][l]), _row(p["lru_L"][l]))
    small.update(conv_w=sm[:CONV_WIDTH], conv_b=sm[4], b_gate_a=sm[5], b_gate_x=sm[6], lru_L=sm[7],
                 w_gate_a=_diag_blocks(dwa, N_REC_BLOCKS), w_gate_x=_diag_blocks(dwx, N_REC_BLOCKS))
    dq, dk, dv, dc = _attn_bwd(sv["qkv"], sv["c"], sv["c_t"], sv["lse_b"], dattn, nh)
    df, db_f = _fgate_bwd(sv["proj"], sv["b_f_pad"], dc)
    small["b_f"] = db_f[0, :nh]
    pieces = [dq, dk, dv, dxr, dyr, df]
    offs = [0, att_w, 2 * att_w, 3 * att_w, 3 * att_w + rec_w, 3 * att_w + 2 * rec_w]
    gq, gk, gv, gxr, gyr, gf = _grad_w_pieces(pieces, sv["z1"])
    g_in_t = jnp.concatenate([gq, gk, gv, gf[:nh], gxr, gyr], axis=0)
    w_big = sv["big"]["w_in_big"]
    widths = [pc.shape[1] for pc in pieces]
    dh, dh_b, dg1 = _dx_norm_bwd(
        [(pc, 0, wd) for pc, wd in zip(pieces, widths)], w_big, _full(w_big.shape),
        lambda w_ref, i: w_ref[offs[i]:offs[i] + widths[i], :], sv["h0"], _row(p["attn_norm_g"][l]), dh, dot=_dot)
    small["attn_norm_g"] = dg1[0]
    big = dict(w_in=g_in_t.reshape(N_CHIPS, -1, d), w_out=g_out.reshape((N_CHIPS, -1) + g_out.shape[1:]))
    return dh, dh_b, big, small


def _pack_w_in_t(w_in_t, att_w, nh):
    qkv = w_in_t[:3 * att_w]
    f = w_in_t[3 * att_w:3 * att_w + nh]
    xy = w_in_t[3 * att_w + nh:]
    return jnp.concatenate([qkv, xy, f, jnp.zeros((LANES - nh, w_in_t.shape[1]), w_in_t.dtype)], axis=0)


ANY = pl.BlockSpec(memory_space=pl.ANY)


def _coords():
    return lax.axis_index("x"), lax.axis_index("y"), lax.axis_index("c")


def _other_chips(x, y):
    return [(1 - x, y), (x, 1 - y), (1 - x, 1 - y)]


def _remote(src, dst, send_sems, recv_sems, k, to):
    return pltpu.make_async_remote_copy(src_ref=src, dst_ref=dst, send_sem=send_sems.at[k],
                                        recv_sem=recv_sems.at[k], device_id=to, device_id_type=MESH)


def _all_gather_chips(shards):
    n = len(shards)
    per = 6

    def body(*refs):
        ins, outs = refs[:n], refs[n:2 * n]
        send_sems, recv_sems, local_sems = refs[2 * n:]
        x, y, c = _coords()
        me = 2 * x + y
        sibling = (x, y, 1 - c)
        chips = _other_chips(x, y)
        local = [pltpu.make_async_copy(ins[t], outs[t].at[me], local_sems.at[t]) for t in range(n)]
        for cp in local:
            cp.start()
        sends = []
        for t in range(n):
            for j, (px, py) in enumerate(chips):
                cp = _remote(ins[t].at[c], outs[t].at[me, c], send_sems, recv_sems, per * t + j, (px, py, c))
                cp.start()
                sends.append(cp)
        for t in range(n):
            for j, (px, py) in enumerate(chips):
                landed = outs[t].at[2 * px + py, c]
                _remote(landed, landed, send_sems, recv_sems, per * t + j, (px, py, c)).wait_recv()
                cp = _remote(landed, landed, send_sems, recv_sems, per * t + 3 + j, sibling)
                cp.start()
                sends.append(cp)
        for t in range(n):
            for j, (px, py) in enumerate(chips):
                passed = outs[t].at[2 * px + py, 1 - c]
                _remote(passed, passed, send_sems, recv_sems, per * t + 3 + j, sibling).wait_recv()
        for cp in sends:
            cp.wait_send()
        for cp in local:
            cp.wait()

    return _call(body, tuple(shards), name="all_gather_chips",
                 in_specs=[ANY] * n, out_specs=[ANY] * n,
                 out_shape=[jax.ShapeDtypeStruct((N_CHIPS,) + s.shape, s.dtype) for s in shards],
                 scratch_shapes=[pltpu.SemaphoreType.DMA((per * n,)), pltpu.SemaphoreType.DMA((per * n,)),
                                 pltpu.SemaphoreType.DMA((n,))])


HBM = pl.BlockSpec(memory_space=pltpu.HBM)
SEM = pl.BlockSpec(memory_space=pltpu.SEMAPHORE)
DATAFLOW = pltpu.SideEffectType.DATAFLOW_SIDE_EFFECTING


def _in_hbm(a):
    return pltpu.with_memory_space_constraint(a, pltpu.HBM)


PUSH_ARRIVALS = {"gather_chips_half": N_CHIPS - 1, "pass_halves": N_CHIPS - 1, "scatter_chips": N_CHIPS - 1,
                 "sibling": 1, "gather_devices": N_DEV - 1}


def _column_half(ref3, slab, c):
    hw = ref3.shape[2] // 2
    return ref3.at[slab, :, pl.ds(pl.multiple_of(c * hw, LANES), hw)]


def _push_copies(mode, src, land, send_sems, recv_sems, t):
    x, y, c = _coords()
    chip = 2 * x + y
    if mode == "gather_chips_half":
        return [_remote(_column_half(src, chip, c), _column_half(land, chip, c), send_sems, recv_sems, t, (px, py, c))
                for px, py in _other_chips(x, y)]
    if mode == "pass_halves":
        return [_remote(_column_half(src, 2 * px + py, c), _column_half(land, 2 * px + py, c), send_sems, recv_sems, t,
                        (x, y, 1 - c)) for px, py in _other_chips(x, y)]
    if mode == "scatter_chips":
        return [_remote(src.at[2 * px + py], land.at[chip], send_sems, recv_sems, t, (px, py, c))
                for px, py in _other_chips(x, y)]
    if mode == "sibling":
        return [_remote(src, land, send_sems, recv_sems, t, (x, y, 1 - c))]
    dev = 4 * x + 2 * y + c
    return [_remote(src.at[dev], land.at[dev], send_sems, recv_sems, t, (x ^ (k >> 2), y ^ ((k >> 1) & 1), c ^ (k & 1)))
            for k in range(1, N_DEV)]


def _push_start(srcs, lands, mode, name):
    n = len(srcs)
    same = all(s is ld for s, ld in zip(srcs, lands))
    n_in = n if same else 2 * n

    def body(*refs):
        src_refs = refs[:n]
        land_refs = src_refs if same else refs[n:2 * n]
        send_sems, recv_sems = refs[n_in], refs[n_in + 1]
        token = refs[-1]
        for t in range(n):
            for cp in _push_copies(mode, src_refs[t], land_refs[t], send_sems, recv_sems, t):
                cp.start()
        token[...] = jnp.zeros_like(token)

    operands = tuple(srcs) if same else tuple(srcs) + tuple(lands)
    res = _call(
        body, [_in_hbm(a) for a in operands], name=name,
        out_shape=(pltpu.SemaphoreType.DMA((n,)), pltpu.SemaphoreType.DMA((n,)))
        + tuple(pltpu.HBM(a.shape, a.dtype) for a in operands) + (jax.ShapeDtypeStruct((SUBLANES, LANES), F32),),
        in_specs=[HBM] * n_in, out_specs=(SEM, SEM) + (HBM,) * n_in + (pl.BlockSpec(memory_space=pltpu.VMEM),),
        input_output_aliases={i: 2 + i for i in range(n_in)}, side_effects=DATAFLOW, hbm_results=False)
    send_sems, recv_sems, token = res[0], res[1], res[-1]
    srcs_thru = res[2:2 + n]
    lands_thru = srcs_thru if same else res[2 + n:2 + 2 * n]
    return send_sems, recv_sems, srcs_thru, lands_thru, token


def _push_wait(send_sems, recv_sems, ids, srcs, lands, mode, after, name):
    n = len(lands)
    same = all(s is ld for s, ld in zip(srcs, lands))
    n_in = n if same else 2 * n

    def body(*refs):
        land_refs = refs[:n] if same else refs[n:2 * n]
        send_sems, recv_sems = refs[n_in], refs[n_in + 1]
        x, y, c = _coords()
        for t in range(n):
            if mode == "sibling":
                moved = land_refs[t]
            elif mode in ("gather_chips_half", "pass_halves"):
                moved = land_refs[t].at[pl.ds(0, PUSH_ARRIVALS[mode]), :, pl.ds(0, land_refs[t].shape[2] // 2)]
            else:
                moved = land_refs[t].at[pl.ds(0, PUSH_ARRIVALS[mode])]
            arrivals = _remote(moved, moved, send_sems, recv_sems, ids[t], (x, y, c))
            arrivals.wait_send()
            arrivals.wait_recv()

    operands = tuple(lands) if same else tuple(srcs) + tuple(lands)
    res = _call(
        body, operands + (send_sems, recv_sems, after), name=name,
        out_shape=tuple(pltpu.HBM(a.shape, a.dtype) for a in operands),
        in_specs=[HBM] * n_in + [SEM, SEM, ANY], out_specs=(HBM,) * n_in,
        input_output_aliases={i: i for i in range(n_in)}, side_effects=DATAFLOW)
    return list(res) if same else (list(res[:n]), list(res[n:]))


def _sum_partials(part, landed, chip):
    _, rows, cols = part.shape
    br = _divisor_tile(rows, 16, ELEM_ROWS)

    def body(chip_ref, own_ref, a_ref, b_ref, c_ref, o_ref):
        o_ref[...] = ((own_ref[...].astype(F32) + a_ref[...].astype(F32)) + b_ref[...].astype(F32)) \
            + c_ref[...].astype(F32)

    def other(k):
        return pl.BlockSpec((None, br, cols), lambda i, ch: (jnp.where(ch[0] <= k, k + 1, k), i, 0))

    spec = pltpu.PrefetchScalarGridSpec(
        num_scalar_prefetch=1, grid=(rows // br,),
        in_specs=[pl.BlockSpec((None, br, cols), lambda i, ch: (ch[0], i, 0)), other(0), other(1), other(2)],
        out_specs=pl.BlockSpec((br, cols), lambda i, ch: (i, 0)))
    return _call(body, (chip, part, landed, landed, landed), name="sum_partials", grid_spec=spec,
                 out_shape=jax.ShapeDtypeStruct((rows, cols), F32), semantics=("parallel",))


def _cast_to_slab(w, l, chip):
    _, rows, cols = w.shape
    br = _divisor_tile(rows, 16, ELEM_ROWS)

    def body(chip_ref, w_ref, o_ref):
        o_ref[...] = w_ref[...].astype(BF16)

    spec = pltpu.PrefetchScalarGridSpec(
        num_scalar_prefetch=1, grid=(rows // br,),
        in_specs=[pl.BlockSpec((None, br, cols), lambda i, ch: (l, i, 0))],
        out_specs=pl.BlockSpec((None, br, cols), lambda i, ch: (ch[0], i, 0)))
    return _call(body, (chip, w), name="cast_to_slab", grid_spec=spec,
                 out_shape=jax.ShapeDtypeStruct((N_CHIPS, rows, cols), BF16), semantics=("parallel",))


def _cast_w_in_t_to_slabs(w_t, chip):
    rows, depth, d = w_t.shape
    tn = _divisor_tile(d, LANES, 256)

    def body(chip_ref, w_ref, *o_refs):
        for l in range(depth):
            o_refs[l][...] = w_ref[:, l, :].astype(BF16)

    spec = pltpu.PrefetchScalarGridSpec(
        num_scalar_prefetch=1, grid=(d // tn,),
        in_specs=[pl.BlockSpec((rows, depth, tn), lambda j, ch: (0, 0, j))],
        out_specs=[pl.BlockSpec((None, rows, tn), lambda j, ch: (ch[0], 0, j))] * depth)
    return _call(body, (chip, w_t), name="cast_w_in_t_to_slabs", grid_spec=spec,
                 out_shape=[jax.ShapeDtypeStruct((N_CHIPS, rows, d), BF16)] * depth, semantics=("parallel",),
                 vmem_bytes=4 * _nbytes((rows, max(depth, SUBLANES), tn), F32))


def _place_slab(buf, index, n_slabs):
    rows, cols = buf.shape
    br = _divisor_tile(rows, SUBLANES, ELEM_ROWS)

    def body(index_ref, b_ref, o_ref):
        o_ref[...] = b_ref[...]

    spec = pltpu.PrefetchScalarGridSpec(
        num_scalar_prefetch=1, grid=(rows // br,),
        in_specs=[pl.BlockSpec((br, cols), lambda i, ix: (i, 0))],
        out_specs=pl.BlockSpec((None, br, cols), lambda i, ix: (ix[0], i, 0)))
    return _call(body, (index, buf), name="place_slab", grid_spec=spec,
                 out_shape=jax.ShapeDtypeStruct((n_slabs, rows, cols), buf.dtype), semantics=("parallel",))


ELEM_ROWS = 256


def _sum_slabs(r):
    n, rows, cols = r.shape
    br = _divisor_tile(rows, 16, ELEM_ROWS)

    def body(r_ref, o_ref):
        acc = r_ref[0].astype(F32)
        for j in range(1, n):
            acc = acc + r_ref[j].astype(F32)
        o_ref[...] = acc

    return _call(body, (r,), name="sum_slabs", grid=(rows // br,),
                 in_specs=[pl.BlockSpec((n, br, cols), lambda i: (0, i, 0))],
                 out_specs=pl.BlockSpec((br, cols), lambda i: (i, 0)),
                 out_shape=jax.ShapeDtypeStruct((rows, cols), F32), semantics=("parallel",))


def _adamw_math(w, g, m, v):
    c1 = 1.0 - ADAM_B1 ** ADAM_STEP
    c2 = 1.0 - ADAM_B2 ** ADAM_STEP
    nm = ADAM_B1 * m + (1.0 - ADAM_B1) * g
    nv = ADAM_B2 * v + (1.0 - ADAM_B2) * (g * g)
    delta = -ADAM_LR * ((nm / c1) / (jnp.sqrt(nv / c2) + ADAM_EPS) + ADAM_WD * w)
    return delta, nm, nv


def _adamw(w, g, m, v):
    rows, cols = w.shape
    br = _divisor_tile(rows, 8, ELEM_ROWS)

    def body(w_ref, g_ref, m_ref, v_ref, d_ref, nm_ref, nv_ref):
        d_ref[...], nm_ref[...], nv_ref[...] = _adamw_math(w_ref[...], g_ref[...], m_ref[...], v_ref[...])

    blk = pl.BlockSpec((br, cols), lambda i: (i, 0))
    return _call(body, (w, g, m, v), name="adamw", grid=(rows // br,),
                 in_specs=[blk] * 4, out_specs=[blk] * 3,
                 out_shape=[jax.ShapeDtypeStruct((rows, cols), F32)] * 3, semantics=("parallel",))


def _adamw_w_in_t(w_t, m_t, v_t, g_mine, g_theirs):
    rows, depth, d = w_t.shape
    tn = LANES

    def body(w_ref, m_ref, v_ref, ga_ref, gb_ref, g_ref, d_ref, nm_ref, nv_ref):
        g = ga_ref[...] + gb_ref[...]
        g_ref[...] = g
        d_ref[...], nm_ref[...], nv_ref[...] = _adamw_math(w_ref[...], g, m_ref[...], v_ref[...])

    slab = pl.BlockSpec((rows, depth, tn), lambda j: (0, 0, j))
    return _call(body, (w_t, m_t, v_t, g_mine, g_theirs), name="adamw_w_in_t", grid=(d // tn,),
                 in_specs=[slab] * 5, out_specs=[slab] * 4,
                 out_shape=[jax.ShapeDtypeStruct(w_t.shape, F32)] * 4, semantics=("parallel",),
                 vmem_bytes=2 * 9 * _nbytes((rows, max(depth, SUBLANES), tn), F32))


def _adamw_layer(w, m, v, l, g_mine, g_theirs, prev, after):
    _, rows, cols = w.shape
    br = _divisor_tile(rows, 8, ELEM_ROWS)

    def body(w_ref, m_ref, v_ref, ga_ref, gb_ref, *rest):
        g_ref, d_ref, nm_ref, nv_ref = rest[5:]
        g = ga_ref[...] + gb_ref[...]
        g_ref[...] = g
        d_ref[...], nm_ref[...], nv_ref[...] = _adamw_math(w_ref[...], g, m_ref[...], v_ref[...])

    slot = pl.BlockSpec((None, br, cols), lambda i: (l, i, 0))
    blk = pl.BlockSpec((br, cols), lambda i: (i, 0))
    return _call(body, (w, m, v, g_mine, g_theirs) + tuple(prev) + (after,), name="adamw_layer",
                 grid=(rows // br,), in_specs=[slot] * 3 + [blk] * 2 + [ANY] * 5, out_specs=[slot] * 4,
                 out_shape=[jax.ShapeDtypeStruct(w.shape, F32)] * 4,
                 input_output_aliases={5: 0, 6: 1, 7: 2, 8: 3}, semantics=("parallel",))


BIG = ("w_in", "w_out", "w_up", "w_down")
WEIGHTS = ("meta", "attn_norm_g", "w_in", "b_f", "conv_w", "conv_b", "w_gate_a", "b_gate_a", "w_gate_x",
           "b_gate_x", "lru_L", "attn_out_g", "rec_out_g", "w_out", "mlp_norm_g", "w_up", "w_down", "final_g")
SMALL = tuple(k for k in WEIGHTS if k not in BIG)
COL_SHARDED_SMALL = ("meta", "conv_w")


def _packed_rows(shape):
    return -(-math.prod(shape) // (SUBLANES * LANES)) * SUBLANES


def _pack(arrs):
    rows = []
    for a in arrs:
        flat = a.reshape(-1)
        rows.append(jnp.pad(flat, (0, _packed_rows(a.shape) * LANES - flat.shape[0])).reshape(-1, LANES))
    used = sum(r.shape[0] for r in rows)
    rows.append(jnp.zeros((-used % ELEM_ROWS, LANES), F32))
    return jnp.concatenate(rows, axis=0)


def _unpack(buf, shapes):
    out, r0 = [], 0
    for s in shapes:
        nr = _packed_rows(s)
        out.append(buf[r0:r0 + nr].reshape(-1)[:math.prod(s)].reshape(s))
        r0 += nr
    return out


def _halves(a):
    return a.reshape((2, a.shape[0] // 2) + a.shape[1:])


def _cols_from_chips(g):
    return jnp.moveaxis(g, 0, -2).reshape(g.shape[1:-1] + (N_CHIPS * g.shape[-1],))


def kernel(x, meta, attn_norm_g, w_in, b_f, conv_w, conv_b, w_gate_a, b_gate_a, w_gate_x, b_gate_x, lru_L, attn_out_g, rec_out_g, w_out, mlp_norm_g, w_up, w_down, final_g, loss_target, m_meta, m_attn_norm_g, m_w_in, m_b_f, m_conv_w, m_conv_b, m_w_gate_a, m_b_gate_a, m_w_gate_x, m_b_gate_x, m_lru_L, m_attn_out_g, m_rec_out_g, m_w_out, m_mlp_norm_g, m_w_up, m_w_down, m_final_g, v_meta, v_attn_norm_g, v_w_in, v_b_f, v_conv_w, v_conv_b, v_w_gate_a, v_b_gate_a, v_w_gate_x, v_b_gate_x, v_lru_L, v_attn_out_g, v_rec_out_g, v_w_out, v_mlp_norm_g, v_w_up, v_w_down, v_final_g):
    w = dict(meta=meta, attn_norm_g=attn_norm_g, w_in=w_in, b_f=b_f, conv_w=conv_w, conv_b=conv_b,
             w_gate_a=w_gate_a, b_gate_a=b_gate_a, w_gate_x=w_gate_x, b_gate_x=b_gate_x, lru_L=lru_L,
             attn_out_g=attn_out_g, rec_out_g=rec_out_g, w_out=w_out, mlp_norm_g=mlp_norm_g, w_up=w_up,
             w_down=w_down, final_g=final_g)
    m = dict(meta=m_meta, attn_norm_g=m_attn_norm_g, w_in=m_w_in, b_f=m_b_f, conv_w=m_conv_w, conv_b=m_conv_b,
             w_gate_a=m_w_gate_a, b_gate_a=m_b_gate_a, w_gate_x=m_w_gate_x, b_gate_x=m_b_gate_x, lru_L=m_lru_L,
             attn_out_g=m_attn_out_g, rec_out_g=m_rec_out_g, w_out=m_w_out, mlp_norm_g=m_mlp_norm_g,
             w_up=m_w_up, w_down=m_w_down, final_g=m_final_g)
    v = dict(meta=v_meta, attn_norm_g=v_attn_norm_g, w_in=v_w_in, b_f=v_b_f, conv_w=v_conv_w, conv_b=v_conv_b,
             w_gate_a=v_w_gate_a, b_gate_a=v_b_gate_a, w_gate_x=v_w_gate_x, b_gate_x=v_b_gate_x, lru_L=v_lru_L,
             attn_out_g=v_attn_out_g, rec_out_g=v_rec_out_g, w_out=v_w_out, mlp_norm_g=v_mlp_norm_g,
             w_up=v_w_up, w_down=v_w_down, final_g=v_final_g)
    s_len, d = x.shape[1], x.shape[2]
    depth = w_in.shape[0]
    att_w = d // 2
    rec_w = d - att_w
    nh = att_w // HEAD_DIM
    chip = 2 * lax.axis_index("x") + lax.axis_index("y")

    g_conv, g_meta = [g.reshape((N_CHIPS, g.shape[1] * g.shape[2]) + g.shape[3:])
                      for g in _all_gather_chips([_halves(w["conv_w"]), _halves(w["meta"])])]
    p = dict(w)
    p["conv_w"] = _cols_from_chips(g_conv)
    meta_full = jnp.moveaxis(g_meta, 0, 1).reshape(N_META, d)

    chip1 = chip.reshape(1).astype(jnp.int32)
    w_in_t, m_in_t, v_in_t = [jnp.transpose(a["w_in"], (2, 0, 1)) for a in (w, m, v)]
    w_in_slabs = _cast_w_in_t_to_slabs(w_in_t, chip1)
    pushes, tokens = [], []
    for l in range(depth):
        slabs = [w_in_slabs[l]] + [_cast_to_slab(w[k], l, chip1) for k in BIG[1:]]
        send_sems, recv_sems, _, lands, token = _push_start(slabs, slabs, "gather_chips_half", f"weights_start_{l}")
        pushes.append((send_sems, recv_sems, lands))
        tokens.append(token[0, 0])
    passed = [{} for _ in range(depth)]

    def stage(l, after, ids):
        send_sems, recv_sems, lands = pushes[l]
        tag = "_".join(BIG[i] for i in ids)
        sub = [lands[i] for i in ids]
        sub = _push_wait(send_sems, recv_sems, ids, sub, sub, "gather_chips_half", after, f"{tag}_wait_{l}")
        send_sems, recv_sems, _, sub, token = _push_start(sub, sub, "pass_halves", f"{tag}_pass_{l}")
        for j, i in enumerate(ids):
            passed[l][i] = (send_sems, recv_sems, sub[j], j)
        return token[0, 0]

    t_len = N_META + s_len
    pad = -t_len % SEQ_TILE
    h = jnp.concatenate([meta_full, x[0], jnp.zeros((pad, d), F32)], axis=0)
    tgt = jnp.concatenate([jnp.zeros((N_META, d), F32), loss_target[0], jnp.zeros((pad, d), F32)], axis=0)
    z = _rms_fwd(h, _row(p["attn_norm_g"][0] + sum(tokens)))
    stage(0, z, [0])
    saved = []
    for l in range(depth):
        def fetch(k, after, l=l):
            send_sems, recv_sems, land, j = passed[l][BIG.index(k)]
            return _push_wait(send_sems, recv_sems, [j], [land], [land], "pass_halves", after,
                              f"{k}_here_{l}")[0]

        def stage_next(after, which, l=l):
            if which == "own":
                return stage(l, after, [1, 2, 3])
            return stage(l + 1, after, [0]) if l + 1 < depth else 0.0

        g_next = p["attn_norm_g"][l + 1] if l + 1 < depth else p["final_g"]
        h, z, sv = _forward_layer(l, h, z, p, fetch, stage_next, g_next)
        saved.append(sv)
    dh, dh_b, dg_final, loss_part = _loss_bwd(h, _row(p["final_g"]), tgt, s_len)

    small = {k: [None] * depth for k in SMALL if k not in ("meta", "final_g")}
    pushes = [None] * depth
    tok = 0.0
    for l in reversed(range(depth)):
        dh, dh_b, big_mlp, sm_mlp = _backward_mlp(l, dh, dh_b, saved[l], p, tok)
        parts = [big_mlp["w_down"], big_mlp["w_up"]]
        push_mlp = _push_start(parts, [lax.empty(a.shape, a.dtype) for a in parts], "scatter_chips",
                               f"mlp_grads_start_{l}")
        dh, dh_b, big_mix, sm_mix = _backward_mixer(l, dh, dh_b, saved[l], p, push_mlp[4][0, 0])
        parts = [big_mix["w_out"], big_mix["w_in"]]
        push_mix = _push_start(parts, [lax.empty(a.shape, a.dtype) for a in parts], "scatter_chips",
                               f"mixer_grads_start_{l}")
        tok = push_mix[4][0, 0]
        pushes[l] = {("w_down", "w_up"): push_mlp, ("w_out", "w_in"): push_mix}
        for k, val in {**sm_mlp, **sm_mix}.items():
            small[k][l] = val
    grads = {k: jnp.stack(val) for k, val in small.items()}
    grads["final_g"] = dg_final[0]
    grads["meta"] = dh[:N_META]
    dx = dh[N_META:t_len]

    full_shapes = [grads[k].shape for k in SMALL] + [(1,)]
    packed = _pack([grads[k].astype(F32) for k in SMALL] + [loss_part[0, :1] + tok])
    dev1 = (2 * chip + lax.axis_index("c")).reshape(1).astype(jnp.int32)
    slabs = [_place_slab(packed, dev1, N_DEV)]
    small_push = _push_start(slabs, slabs, "gather_devices", "small_grads_start")

    last_token = small_push[4]
    outs = {k: [lax.empty(w[k].shape, F32) for _ in range(4)] for k in BIG[1:]}
    w_in_sums = [None] * depth
    swaps = {}

    def finish(l, wait_after, adam_after):
        send_sems, recv_sems, mine, lands, _ = swaps[l]
        mine, theirs = _push_wait(send_sems, recv_sems, list(range(len(BIG))), mine, lands, "sibling", wait_after,
                                  f"sums_wait_{l}")
        w_in_sums[l] = (mine[0], theirs[0])
        for k, a, b in zip(BIG[1:], mine[1:], theirs[1:]):
            outs[k] = _adamw_layer(w[k], m[k], v[k], l, a, b, outs[k], adam_after)

    for l in reversed(range(depth)):
        sums = {}
        for names, (send_sems, recv_sems, parts, lands, _) in pushes[l].items():
            parts, landed = _push_wait(send_sems, recv_sems, [0, 1], parts, lands, "scatter_chips", last_token,
                                       f"{names[0]}_grads_wait_{l}")
            for k, part, land in zip(names, parts, landed):
                sums[k] = _sum_partials(part, land, chip1)
        mine = [sums[k] for k in BIG]
        swaps[l] = _push_start(mine, [lax.empty(a.shape, a.dtype) for a in mine], "sibling", f"sums_start_{l}")
        if l + 1 < depth:
            finish(l + 1, outs["w_down"][0] if l + 2 < depth else mine[0], swaps[l][4])
    finish(0, outs["w_down"][0] if depth > 1 else swaps[0][4], swaps[0][4])
    outs["w_in"] = [jnp.transpose(r, (1, 2, 0)) for r in _adamw_w_in_t(
        w_in_t, m_in_t, v_in_t, jnp.stack([s[0] for s in w_in_sums], axis=1),
        jnp.stack([s[1] for s in w_in_sums], axis=1))]
    out_g, out_d, out_m, out_v = [{k: outs[k][i] for k in BIG} for i in range(4)]

    landed = _push_wait(small_push[0], small_push[1], [0], small_push[3], small_push[3], "gather_devices",
                        out_g["w_in"], "small_grads_wait")
    total = _sum_slabs(landed[0])
    small_g = dict(zip(SMALL + ("loss",), _unpack(total, full_shapes)))
    for k in COL_SHARDED_SMALL:
        n = w[k].shape[-1]
        small_g[k] = lax.dynamic_slice_in_dim(small_g[k], chip * n, n, axis=small_g[k].ndim - 1)
    local_shapes = [w[k].shape for k in SMALL]
    res = _adamw(_pack([w[k] for k in SMALL]), _pack([small_g[k] for k in SMALL]),
                 _pack([m[k] for k in SMALL]), _pack([v[k] for k in SMALL]))
    out_g.update({k: small_g[k] for k in SMALL})
    for dst, buf in zip((out_d, out_m, out_v), res):
        dst.update(zip(SMALL, _unpack(buf, local_shapes)))

    return (small_g["loss"].reshape(()), dx[None],
            *[out_g[k] for k in WEIGHTS], *[out_d[k] for k in WEIGHTS],
            *[out_m[k] for k in WEIGHTS], *[out_v[k] for k in WEIGHTS])
```

```python
import functools
import math

import jax
import jax.numpy as jnp
from jax import lax
from jax.experimental import pallas as pl
from jax.experimental.pallas import tpu as pltpu

F32 = jnp.float32
BF16 = jnp.bfloat16

N_META = 16
HEAD_DIM = 64
N_REC_BLOCKS = 8
CONV_WIDTH = 4
RG_C = 8.0
NORM_EPS = 1e-6
ADAM_LR = 0.001
ADAM_B1 = 0.9
ADAM_B2 = 0.999
ADAM_EPS = 1e-08
ADAM_WD = 0.01
ADAM_STEP = 10

LANES = 128
SUBLANES = 8
SEQ_TILE = 128
VMEM_CAP = 60 * 2**20
VMEM_SLACK = 6 * 2**20
NEG_BIG = -1e30
N_CHIPS = 4
N_DEV = 8
MESH = pl.DeviceIdType.MESH


def _nbytes(shape, dtype):
    return math.prod(shape) * jnp.dtype(dtype).itemsize


def _call(body, args, *, name, out_shape, grid=(), in_specs=None, out_specs=None, scratch_shapes=(),
          grid_spec=None, semantics=None, vmem_bytes=None, side_effects=None, hbm_results=True, **kw):
    cp = {}
    if semantics is not None:
        cp["dimension_semantics"] = semantics
    if vmem_bytes is not None:
        cp["vmem_limit_bytes"] = int(min(VMEM_CAP, vmem_bytes + VMEM_SLACK))
    if side_effects is not None:
        cp["has_side_effects"] = side_effects
    if grid_spec is not None:
        kw["grid_spec"] = grid_spec
    else:
        kw.update(grid=grid, in_specs=in_specs, out_specs=out_specs, scratch_shapes=scratch_shapes)
    if hbm_results:
        out_shape = jax.tree.map(
            lambda s: pltpu.HBM(s.shape, s.dtype) if isinstance(s, jax.ShapeDtypeStruct) else s, out_shape)
    fn = pl.pallas_call(
        body, name=name, out_shape=out_shape,
        compiler_params=pltpu.CompilerParams(**cp), **kw)
    return fn(*[_in_hbm(a) if jnp.issubdtype(getattr(a, "dtype", jnp.int32), jnp.floating) else a for a in args])


def _divisor_tile(n, unit, target):
    best = None
    for t in range(unit, min(n, target) + 1, unit):
        if n % t == 0:
            best = t
    return n if best is None else best


def _sigmoid(x):
    return 1.0 / (1.0 + jnp.exp(-x))


def _log1p_unit(e):
    series = e * (1.0 - e * (0.5 - e * (1.0 / 3.0)))
    return jnp.where(e < 1e-2, series, jnp.log(1.0 + e))


def _log_sigmoid(x):
    return jnp.minimum(x, 0.0) - _log1p_unit(jnp.exp(-jnp.abs(x)))


def _one_minus_exp(x, exp_x):
    small = -x * (1.0 + x * (1.0 / 2 + x * (1.0 / 6 + x * (1.0 / 24 + x * (1.0 / 120 + x * (1.0 / 720))))))
    return jnp.where(x > -0.25, small, 1.0 - exp_x)


_GELU_K = math.sqrt(2.0 / math.pi)
_GELU_C = 0.044715


def _gelu_and_grad(y):
    th = jnp.tanh(_GELU_K * (y + _GELU_C * y * y * y))
    g = 0.5 * y * (1.0 + th)
    dg = 0.5 * (1.0 + th) + 0.5 * y * (1.0 - th * th) * _GELU_K * (1.0 + 3.0 * _GELU_C * y * y)
    return g, dg


def _rstd(x):
    return lax.rsqrt(jnp.mean(x * x, axis=-1, keepdims=True) + NORM_EPS)


def _rms_bwd(dz, x, g):
    rs = _rstd(x)
    xh = x * rs
    dgp = jnp.sum(dz * xh, axis=0, keepdims=True)
    dxh = dz * g
    dx = rs * (dxh - xh * jnp.mean(dxh * xh, axis=-1, keepdims=True))
    return dx, dgp


def _dot(a, b):
    return jnp.dot(a, b, preferred_element_type=F32)


def _dot_nt(a, b):
    return lax.dot_general(a, b, (((1,), (1,)), ((), ())), preferred_element_type=F32)


def _dot_tn(a, b):
    return lax.dot_general(a, b, (((0,), (0,)), ((), ())), preferred_element_type=F32)


def _full(shape):
    nd = len(shape)
    return pl.BlockSpec(shape, lambda *_: (0,) * nd)


def _rms_fwd(h, g):
    tp, d = h.shape
    tm = _divisor_tile(tp, 16, 544)

    def body(h_ref, g_ref, z_ref):
        x = h_ref[...]
        z_ref[...] = (x * _rstd(x) * g_ref[...]).astype(BF16)

    return _call(body, (h, g), name="rms_fwd", grid=(tp // tm,),
                 in_specs=[pl.BlockSpec((tm, d), lambda i: (i, 0)), _full((1, d))],
                 out_specs=pl.BlockSpec((tm, d), lambda i: (i, 0)),
                 out_shape=jax.ShapeDtypeStruct((tp, d), BF16), semantics=("parallel",))


def _proj(z, w_big_t, att_w):
    tp, d = z.shape
    nb = w_big_t.shape[0]
    tn = _divisor_tile(nb, LANES, 512)
    assert (3 * att_w) % tn == 0
    n_qkv = 3 * att_w // tn
    scale = 1.0 / math.sqrt(HEAD_DIM)

    def body(z_ref, w_ref, qkv_ref, p_ref):
        j = pl.program_id(0)
        acc = _dot_nt(z_ref[...], w_ref[...])

        @pl.when(j < n_qkv)
        def _():
            col = j * tn + lax.broadcasted_iota(jnp.int32, (1, tn), 1)
            qkv_ref[...] = (acc * jnp.where(col < att_w, scale, 1.0)).astype(BF16)

        @pl.when(j >= n_qkv)
        def _():
            p_ref[...] = acc

    vm = 2 * (_nbytes((tp, d), BF16) + _nbytes((d, tn), BF16) + _nbytes((tp, tn), F32) * 2)
    return _call(body, (z, w_big_t), name="proj", grid=(nb // tn,),
                 in_specs=[_full((tp, d)), pl.BlockSpec((tn, d), lambda j: (j, 0))],
                 out_specs=[pl.BlockSpec((tp, tn), lambda j: (0, jnp.minimum(j, n_qkv - 1))),
                            pl.BlockSpec((tp, tn), lambda j: (0, jnp.maximum(j - n_qkv, 0)))],
                 out_shape=[jax.ShapeDtypeStruct((tp, 3 * att_w), BF16),
                            jax.ShapeDtypeStruct((tp, nb - 3 * att_w), F32)],
                 semantics=("arbitrary",), vmem_bytes=vm)


def _tile_cumsum(x, row, reverse=False):
    for s in (1, 2, 4):
        if reverse:
            x = x + jnp.where(row < SUBLANES - s, pltpu.roll(x, SUBLANES - s, 0), 0.0)
        else:
            x = x + jnp.where(row >= s, pltpu.roll(x, s, 0), 0.0)
    return x


def _fgate_fwd(proj, b_f_pad, nh):
    tp, nb = proj.shape
    fblk = nb // LANES - 1

    def body(f_ref, b_ref, c_ref, ct_ref):
        b = b_ref[...]
        row = lax.broadcasted_iota(jnp.int32, (SUBLANES, LANES), 0)

        def step(i, carry):
            r0 = pl.multiple_of(i * SUBLANES, SUBLANES)
            lf = _log_sigmoid(f_ref[pl.ds(r0, SUBLANES), :] + b)
            x = _tile_cumsum(lf, row) + carry
            c_ref[pl.ds(r0, SUBLANES), :] = x
            return x[SUBLANES - 1:SUBLANES, :]

        lax.fori_loop(0, tp // SUBLANES, step, jnp.zeros((1, LANES), F32))
        ct_ref[...] = c_ref[...].T[:nh, :]

    return _call(body, (proj, b_f_pad), name="fgate_fwd", grid=(1,),
                 in_specs=[pl.BlockSpec((tp, LANES), lambda i: (0, fblk)), _full((1, LANES))],
                 out_specs=[_full((tp, LANES)), _full((nh, tp))],
                 out_shape=[jax.ShapeDtypeStruct((tp, LANES), F32), jax.ShapeDtypeStruct((nh, tp), F32)],
                 semantics=("arbitrary",))


def _fgate_bwd(proj, b_f_pad, dc):
    tp, nb = proj.shape
    fblk = nb // LANES - 1

    def body(f_ref, b_ref, dc_ref, df_ref, db_ref, dc_s):
        b = b_ref[...]
        row = lax.broadcasted_iota(jnp.int32, (SUBLANES, LANES), 0)
        nt = tp // SUBLANES

        def step(i, carry):
            suffix, acc = carry
            r0 = pl.multiple_of((nt - 1 - i) * SUBLANES, SUBLANES)
            dlf = _tile_cumsum(dc_ref[pl.ds(r0, SUBLANES), :], row, reverse=True) + suffix
            df = dlf * _sigmoid(-(f_ref[pl.ds(r0, SUBLANES), :] + b))
            dc_s[pl.ds(r0, SUBLANES), :] = df
            return dlf[0:1, :], acc + df

        _, acc = lax.fori_loop(0, nt, step, (jnp.zeros((1, LANES), F32), jnp.zeros((SUBLANES, LANES), F32)))
        df_ref[...] = dc_s[...].astype(BF16)
        db_ref[...] = jnp.broadcast_to(jnp.sum(acc, axis=0, keepdims=True), (SUBLANES, LANES))

    return _call(body, (proj, b_f_pad, dc), name="fgate_bwd", grid=(1,),
                 in_specs=[pl.BlockSpec((tp, LANES), lambda i: (0, fblk)), _full((1, LANES)), _full((tp, LANES))],
                 out_specs=[_full((tp, LANES)), _full((SUBLANES, LANES))],
                 out_shape=[jax.ShapeDtypeStruct((tp, LANES), BF16),
                            jax.ShapeDtypeStruct((SUBLANES, LANES), F32)],
                 scratch_shapes=[pltpu.VMEM((tp, LANES), F32)], semantics=("arbitrary",))


ATT_BQ = 128


ATT_BUCKET = 3
ATT_HEADS = 4


def _for_bucket(i, nq, fn):
    for lo in range(0, nq, ATT_BUCKET):
        hi = min(lo + ATT_BUCKET, nq)
        spans = ([(0, lo * ATT_BQ, False)] if lo else []) + [(lo * ATT_BQ, hi * ATT_BQ, True)]
        pl.when(jnp.logical_and(i >= lo, i < hi))(functools.partial(fn, spans))


def _head_column(c_blk, h):
    lane = lax.broadcasted_iota(jnp.int32, c_blk.shape, 1)
    return jnp.sum(jnp.where(lane == h, c_blk, 0.0), axis=1, keepdims=True)


def _head_columns_into(p, c_ref, ck_s):
    for hh in range(ATT_HEADS):
        ck_s[hh] = jnp.broadcast_to(_head_column(c_ref[...], ATT_HEADS * p + hh), ck_s.shape[1:])


def _pair_diag_cols(x2):
    top = lax.broadcasted_iota(jnp.int32, (LANES, ATT_BQ), 0) < HEAD_DIM
    xt = x2.astype(F32).T.astype(BF16)
    return jnp.concatenate([jnp.where(top, xt, 0), jnp.where(top, 0, xt)], axis=1)


def _pair_diag_rows(x2):
    low = lax.broadcasted_iota(jnp.int32, (ATT_BQ, LANES), 1) < HEAD_DIM
    return jnp.concatenate([jnp.where(low, x2, 0), jnp.where(low, 0, x2)], axis=0)


def _seen_keys(k0, k1, q0):
    keys = k0 + lax.broadcasted_iota(jnp.int32, (k1 - k0, ATT_BQ), 0)
    return keys <= q0 + lax.broadcasted_iota(jnp.int32, (k1 - k0, ATT_BQ), 1)


def _attn_fwd(qkv, c, c_t, nh):
    tp = qkv.shape[0]
    att_w = nh * HEAD_DIM
    ng = nh // ATT_HEADS
    gw = ATT_HEADS * HEAD_DIM
    bq = ATT_BQ
    nq = tp // bq
    pair = 2 * HEAD_DIM
    assert pair == LANES and ATT_HEADS % 2 == 0

    def body(q_ref, k_ref, v_ref, c_ref, ct_ref, o_ref, lse_ref, ck_s, vt_s):
        p = pl.program_id(0)
        i = pl.program_id(1)

        @pl.when(i == 0)
        def _():
            _head_columns_into(p, c_ref, ck_s)
            vt_s[...] = v_ref[...].astype(F32).T.astype(BF16)

        def compute(spans):
            q0 = pl.multiple_of(i * bq, bq)
            o_t, lses = [], []
            for pi in range(ATT_HEADS // 2):
                lo = pair * pi
                heads = (2 * pi, 2 * pi + 1)
                q_cols = _pair_diag_cols(q_ref[:, lo:lo + pair])
                ts = []
                for k0, k1, needs_mask in spans:
                    t2 = _dot(k_ref[k0:k1, lo:lo + pair], q_cols)
                    t_e = [t2[:, e * bq:(e + 1) * bq] - ck_s[hh, k0:k1, :] for e, hh in enumerate(heads)]
                    if needs_mask:
                        seen = _seen_keys(k0, k1, q0)
                        t_e = [jnp.where(seen, t, NEG_BIG) for t in t_e]
                    ts.append(t_e)
                ms = [functools.reduce(jnp.maximum, [jnp.max(t[e], axis=0, keepdims=True) for t in ts])
                      for e in range(2)]
                es = [[jnp.exp(t[e] - ms[e]) for e in range(2)] for t in ts]
                ls = [sum(jnp.sum(e_[e], axis=0, keepdims=True) for e_ in es) for e in range(2)]
                o2 = sum(_dot(vt_s[lo:lo + pair, k0:k1],
                              jnp.concatenate([e_[0].astype(BF16), e_[1].astype(BF16)], axis=1))
                         for e_, (k0, k1, _) in zip(es, spans))
                o_t += [o2[:HEAD_DIM, :bq] / ls[0], o2[HEAD_DIM:, bq:] / ls[1]]
                lses += [ms[e] + ct_ref[pl.ds(ATT_HEADS * p + hh, 1), :] + jnp.log(ls[e])
                         for e, hh in enumerate(heads)]
            o_ref[...] = jnp.concatenate(o_t, axis=0).T
            lse_ref[...] = jnp.concatenate(lses, axis=0)

        _for_bucket(i, nq, compute)

    blk = pl.BlockSpec((bq, gw), lambda p, i: (i, p))
    vm = 6 * _nbytes((tp, gw), BF16) + 2 * ATT_HEADS * _nbytes((tp, LANES), F32) + 2 * _nbytes((tp, LANES), F32) \
        + 8 * ATT_HEADS * _nbytes((bq, tp), F32)
    return _call(body, (qkv, qkv, qkv, c, c_t), name="attn_fwd", grid=(ng, nq),
                 in_specs=[blk,
                           pl.BlockSpec((tp, gw), lambda p, i: (0, ng + p)),
                           pl.BlockSpec((tp, gw), lambda p, i: (0, 2 * ng + p)),
                           _full((tp, LANES)), pl.BlockSpec((nh, bq), lambda p, i: (0, i))],
                 out_specs=[blk, pl.BlockSpec((None, ATT_HEADS, bq), lambda p, i: (p, 0, i))],
                 out_shape=[jax.ShapeDtypeStruct((tp, att_w), F32), jax.ShapeDtypeStruct((ng, ATT_HEADS, tp), F32)],
                 scratch_shapes=[pltpu.VMEM((ATT_HEADS, tp, LANES), F32), pltpu.VMEM((gw, tp), BF16)],
                 semantics=("arbitrary", "arbitrary"), vmem_bytes=vm)


def _attn_bwd(qkv, c, c_t, lse, do, nh):
    tp = qkv.shape[0]
    att_w = nh * HEAD_DIM
    ng = nh // ATT_HEADS
    gw = ATT_HEADS * HEAD_DIM
    bq = ATT_BQ
    nq = tp // bq
    pair = 2 * HEAD_DIM
    assert pair == LANES and ATT_HEADS % 2 == 0
    scale = 1.0 / math.sqrt(HEAD_DIM)

    def body(q_ref, k_ref, v_ref, c_ref, ct_ref, lse_ref, do_ref, dq_ref, dk_ref, dv_ref, dc_ref,
             dk_s, dv_s, dc_s, ck_s, kt_s):
        p = pl.program_id(0)
        i = pl.program_id(1)

        @pl.when(i == 0)
        def _():
            dk_s[...] = jnp.zeros_like(dk_s)
            dv_s[...] = jnp.zeros_like(dv_s)
            dc_s[...] = jnp.zeros_like(dc_s)
            kt_s[...] = k_ref[...].astype(F32).T.astype(BF16)
            _head_columns_into(p, c_ref, ck_s)

        @pl.when(jnp.logical_and(i == 0, p == 0))
        def _():
            dc_ref[...] = jnp.zeros_like(dc_ref)

        def compute(spans):
            q0 = pl.multiple_of(i * bq, bq)
            dq_t = []
            for pi in range(ATT_HEADS // 2):
                lo = pair * pi
                q2 = q_ref[:, lo:lo + pair]
                do2 = do_ref[:, lo:lo + pair].astype(BF16)
                q_cols, do_cols = _pair_diag_cols(q2), _pair_diag_cols(do2)
                q_rows, do_rows = _pair_diag_rows(q2), _pair_diag_rows(do2)
                heads = (2 * pi, 2 * pi + 1)
                col_terms = [ct_ref[pl.ds(ATT_HEADS * p + hh, 1), :] - lse_ref[hh:hh + 1, :] for hh in heads]
                prs, dps = [], []
                for k0, k1, needs_mask in spans:
                    t2 = _dot(k_ref[k0:k1, lo:lo + pair], q_cols)
                    dp2 = _dot(v_ref[k0:k1, lo:lo + pair], do_cols)
                    if needs_mask:
                        seen = _seen_keys(k0, k1, q0)
                    pr_e, dp_e = [], []
                    for e, hh in enumerate(heads):
                        t = t2[:, e * bq:(e + 1) * bq] - ck_s[hh, k0:k1, :]
                        if needs_mask:
                            t = jnp.where(seen, t, NEG_BIG)
                        pr_e.append(jnp.exp(t + col_terms[e]))
                        dp_e.append(dp2[:, e * bq:(e + 1) * bq])
                    prs.append(pr_e)
                    dps.append(dp_e)
                key_sums = [sum(jnp.sum(pr[e] * dp[e], axis=0, keepdims=True) for pr, dp in zip(prs, dps))
                            for e in range(2)]
                dq2 = 0.0
                for (k0, k1, _), pr, dp in zip(spans, prs, dps):
                    ds = [pr[e] * (dp[e] - key_sums[e]) for e in range(2)]
                    for e, hh in enumerate(heads):
                        dc_s[hh, k0:k1, :] += jnp.sum(ds[e], axis=1, keepdims=True)
                    ds2 = jnp.concatenate([ds[0].astype(BF16), ds[1].astype(BF16)], axis=1)
                    pr2 = jnp.concatenate([pr[0].astype(BF16), pr[1].astype(BF16)], axis=1)
                    dk_s[k0:k1, lo:lo + pair] += _dot(ds2, q_rows)
                    dv_s[k0:k1, lo:lo + pair] += _dot(pr2, do_rows)
                    dq2 = dq2 + _dot(kt_s[lo:lo + pair, k0:k1], ds2)
                dq_t.append(jnp.concatenate([dq2[:HEAD_DIM, :bq], dq2[HEAD_DIM:, bq:]], axis=0))
            dq_ref[...] = (jnp.concatenate(dq_t, axis=0) * scale).T.astype(BF16)

        _for_bucket(i, nq, compute)

        @pl.when(i == nq - 1)
        def _():
            dk_ref[...] = dk_s[...].astype(BF16)
            dv_ref[...] = dv_s[...].astype(BF16)
            lane = lax.broadcasted_iota(jnp.int32, (tp, LANES), 1)
            dc = dc_ref[...]
            for hh in range(ATT_HEADS):
                dc = jnp.where(lane == ATT_HEADS * p + hh, -dc_s[hh], dc)
            dc_ref[...] = dc

    blk = pl.BlockSpec((bq, gw), lambda p, i: (i, p))
    col = pl.BlockSpec((tp, gw), lambda p, i: (0, p))
    vm = 7 * _nbytes((tp, gw), BF16) + 2 * _nbytes((tp, gw), F32) + 2 * ATT_HEADS * _nbytes((tp, LANES), F32) \
        + 2 * _nbytes((tp, LANES), F32) + 12 * ATT_HEADS * _nbytes((bq, tp), F32)
    return _call(body, (qkv, qkv, qkv, c, c_t, lse, do), name="attn_bwd", grid=(ng, nq),
                 in_specs=[blk,
                           pl.BlockSpec((tp, gw), lambda p, i: (0, ng + p)),
                           pl.BlockSpec((tp, gw), lambda p, i: (0, 2 * ng + p)),
                           _full((tp, LANES)), pl.BlockSpec((nh, bq), lambda p, i: (0, i)),
                           pl.BlockSpec((None, ATT_HEADS, bq), lambda p, i: (p, 0, i)), blk],
                 out_specs=[blk, col, col, _full((tp, LANES))],
                 out_shape=[jax.ShapeDtypeStruct((tp, att_w), BF16)] * 3 + [jax.ShapeDtypeStruct((tp, LANES), F32)],
                 scratch_shapes=[pltpu.VMEM((tp, gw), F32), pltpu.VMEM((tp, gw), F32),
                                 pltpu.VMEM((ATT_HEADS, tp, 1), F32), pltpu.VMEM((ATT_HEADS, tp, LANES), F32),
                                 pltpu.VMEM((gw, tp), BF16)],
                 semantics=("arbitrary", "arbitrary"), vmem_bytes=vm)


REC_ROWS = 128
HALO = SUBLANES


def _conv_taps(cat):
    taps = []
    for k in range(CONV_WIDTH):
        sh = CONV_WIDTH - 1 - k
        taps.append((pltpu.roll(cat, sh, 0) if sh else cat)[HALO:])
    return taps


def _rec_gates(xc, wa_ref, ba_ref, wx_ref, bx_ref, l_ref):
    xcb = xc.astype(BF16)
    r = _sigmoid(_dot(xcb, wa_ref[...]) + ba_ref[...])
    ig = _sigmoid(_dot(xcb, wx_ref[...]) + bx_ref[...])
    ls = _log_sigmoid(l_ref[...])
    log_a = RG_C * r * ls
    return xcb, r, ig, ls, log_a


def _rec_fwd(proj, xr_blk, yr_blk, rec_w, conv_w, conv_b, wa, ba, wx, bx, lru):
    tp = proj.shape[0]
    w = rec_w
    r_rows = REC_ROWS
    nc = tp // r_rows
    cpb = w // LANES

    def body(xr_ref, yr_ref, cw_ref, cb_ref, wa_ref, ba_ref, wx_ref, bx_ref, l_ref,
             hr_ref, rec_ref, prev_s, carry_s, a_s, u_s):
        i = pl.program_id(0)

        @pl.when(i == 0)
        def _():
            prev_s[...] = jnp.zeros_like(prev_s)
            carry_s[...] = jnp.zeros_like(carry_s)

        x = xr_ref[...]
        taps = _conv_taps(jnp.concatenate([prev_s[...], x], axis=0))
        prev_s[...] = x[r_rows - HALO:]
        xc = cb_ref[...]
        for k in range(CONV_WIDTH):
            xc = xc + cw_ref[k:k + 1, :] * taps[k]
        _, r, ig, ls, log_a = _rec_gates(xc, wa_ref, ba_ref, wx_ref, bx_ref, l_ref)
        a = jnp.exp(log_a)
        a_s[...] = a
        u_s[...] = jnp.sqrt(_one_minus_exp(2.0 * log_a, a * a)) * ig * xc

        def tile(j, h):
            r0 = pl.multiple_of(j * SUBLANES, SUBLANES)
            at = a_s[pl.ds(r0, SUBLANES), :]
            ut = u_s[pl.ds(r0, SUBLANES), :]
            out = []
            for rr in range(SUBLANES):
                h = at[rr:rr + 1] * h + ut[rr:rr + 1]
                out.append(h)
            hr_ref[pl.ds(r0, SUBLANES), :] = jnp.concatenate(out, axis=0)
            return h

        carry_s[0:1, :] = lax.fori_loop(0, r_rows // SUBLANES, tile, carry_s[0:1, :])
        g, _ = _gelu_and_grad(yr_ref[...])
        rec_ref[...] = hr_ref[...] * g

    blk = pl.BlockSpec((r_rows, w), lambda i: (i, 0))
    vm = 16 * _nbytes((r_rows, w), F32) + 4 * _nbytes((w, w), BF16)
    return _call(body, (proj, proj, conv_w, conv_b, wa, ba, wx, bx, lru), name="rec_fwd", grid=(nc,),
                 in_specs=[pl.BlockSpec((r_rows, w), lambda i: (i, xr_blk)),
                           pl.BlockSpec((r_rows, w), lambda i: (i, yr_blk)),
                           _full((CONV_WIDTH, w)), _full((1, w)), _full((w, w)), _full((1, w)),
                           _full((w, w)), _full((1, w)), _full((1, w))],
                 out_specs=[blk, blk],
                 out_shape=[jax.ShapeDtypeStruct((tp, w), F32)] * 2,
                 scratch_shapes=[pltpu.VMEM((HALO, w), F32), pltpu.VMEM((SUBLANES, w), F32),
                                 pltpu.VMEM((r_rows, w), F32), pltpu.VMEM((r_rows, w), F32)],
                 semantics=("arbitrary",), vmem_bytes=vm)


def _rec_bwd(proj, xr_blk, yr_blk, rec_w, hr, drec, conv_w, conv_b, wa, ba, wx, bx, lru):
    tp = proj.shape[0]
    w = rec_w
    r_rows = REC_ROWS
    nc = tp // r_rows
    hpc = r_rows // HALO

    def body(xr_ref, xh_ref, yr_ref, hr_ref, hh_ref, drec_ref, cw_ref, cb_ref, wa_ref, ba_ref, wx_ref, bx_ref,
             l_ref, dxr_ref, dyr_ref, dwa_ref, dwx_ref, small_ref, lam_s, a_s, dhr_s, carry_s, next_s):
        i = pl.program_id(0)
        first = (nc - 1 - i) == 0

        @pl.when(i == 0)
        def _():
            carry_s[...] = jnp.zeros_like(carry_s)
            next_s[...] = jnp.zeros_like(next_s)
            dwa_ref[...] = jnp.zeros_like(dwa_ref)
            dwx_ref[...] = jnp.zeros_like(dwx_ref)
            small_ref[...] = jnp.zeros_like(small_ref)

        x = xr_ref[...]
        xprev = jnp.where(first, 0.0, xh_ref[...])
        taps = _conv_taps(jnp.concatenate([xprev, x], axis=0))
        xc = cb_ref[...]
        for k in range(CONV_WIDTH):
            xc = xc + cw_ref[k:k + 1, :] * taps[k]
        xcb, r, ig, ls, log_a = _rec_gates(xc, wa_ref, ba_ref, wx_ref, bx_ref, l_ref)
        a = jnp.exp(log_a)
        a2 = a * a
        mult = jnp.sqrt(_one_minus_exp(2.0 * log_a, a2))
        g, dg = _gelu_and_grad(yr_ref[...])
        hr_v = hr_ref[...]
        drec_v = drec_ref[...]
        dhr_s[...] = drec_v * g
        dyr_ref[...] = (drec_v * hr_v * dg).astype(BF16)
        a_s[...] = a

        def tile(jj, carry):
            r0 = pl.multiple_of((r_rows // SUBLANES - 1 - jj) * SUBLANES, SUBLANES)
            at = a_s[pl.ds(r0, SUBLANES), :]
            dt = dhr_s[pl.ds(r0, SUBLANES), :]
            out = [None] * SUBLANES
            for rr in range(SUBLANES - 1, -1, -1):
                lam = dt[rr:rr + 1] + carry
                out[rr] = lam
                carry = at[rr:rr + 1] * lam
            lam_s[pl.ds(r0, SUBLANES), :] = jnp.concatenate(out, axis=0)
            return carry

        carry_s[0:1, :] = lax.fori_loop(0, r_rows // SUBLANES, tile, carry_s[0:1, :])
        lam = lam_s[...]
        hprev = jnp.where(first, 0.0, hh_ref[...])
        hr_prev = pltpu.roll(jnp.concatenate([hprev, hr_v], axis=0), 1, 0)[HALO:]
        da = lam * hr_prev
        dxc = lam * mult * ig
        di = lam * mult * xc
        dmult = lam * ig * xc
        dlog_a = da * a - dmult * a2 / mult
        dr = dlog_a * (RG_C * ls)
        dls = jnp.sum(dlog_a * (RG_C * r), axis=0, keepdims=True)
        dga = dr * r * (1.0 - r)
        dgx = di * ig * (1.0 - ig)
        dgab = dga.astype(BF16)
        dgxb = dgx.astype(BF16)
        dxc = dxc + _dot_nt(dgab, wa_ref[...]) + _dot_nt(dgxb, wx_ref[...])
        dwa_ref[...] += _dot_tn(xcb, dgab)
        dwx_ref[...] += _dot_tn(xcb, dgxb)
        cat = jnp.concatenate([dxc, next_s[...]], axis=0)
        next_s[...] = dxc[0:HALO]
        dxr = cw_ref[CONV_WIDTH - 1:CONV_WIDTH, :] * dxc
        for k in range(CONV_WIDTH - 1):
            sh = CONV_WIDTH - 1 - k
            dxr = dxr + cw_ref[k:k + 1, :] * pltpu.roll(cat, r_rows + HALO - sh, 0)[:r_rows]
        dxr_ref[...] = dxr.astype(BF16)
        rows = [jnp.sum(dxc * taps[k], axis=0, keepdims=True) for k in range(CONV_WIDTH)]
        rows += [jnp.sum(dxc, axis=0, keepdims=True), jnp.sum(dga, axis=0, keepdims=True),
                 jnp.sum(dgx, axis=0, keepdims=True), dls * _sigmoid(-l_ref[...])]
        small_ref[...] += jnp.concatenate(rows, axis=0)

    def rev(i):
        return nc - 1 - i

    def halo(i):
        return jnp.maximum(rev(i) * hpc - 1, 0)

    blk = pl.BlockSpec((r_rows, w), lambda i: (rev(i), 0))
    vm = 40 * _nbytes((r_rows, w), F32) + 6 * _nbytes((w, w), F32)
    return _call(body, (proj, proj, proj, hr, hr, drec, conv_w, conv_b, wa, ba, wx, bx, lru),
                 name="rec_bwd", grid=(nc,),
                 in_specs=[pl.BlockSpec((r_rows, w), lambda i: (rev(i), xr_blk)),
                           pl.BlockSpec((HALO, w), lambda i: (halo(i), xr_blk)),
                           pl.BlockSpec((r_rows, w), lambda i: (rev(i), yr_blk)),
                           blk,
                           pl.BlockSpec((HALO, w), lambda i: (halo(i), 0)),
                           blk,
                           _full((CONV_WIDTH, w)), _full((1, w)), _full((w, w)), _full((1, w)),
                           _full((w, w)), _full((1, w)), _full((1, w))],
                 out_specs=[blk, blk, _full((w, w)), _full((w, w)), _full((SUBLANES, w))],
                 out_shape=[jax.ShapeDtypeStruct((tp, w), BF16)] * 2
                 + [jax.ShapeDtypeStruct((w, w), F32)] * 2 + [jax.ShapeDtypeStruct((SUBLANES, w), F32)],
                 scratch_shapes=[pltpu.VMEM((r_rows, w), F32)] * 3
                 + [pltpu.VMEM((SUBLANES, w), F32), pltpu.VMEM((HALO, w), F32)],
                 semantics=("arbitrary",), vmem_bytes=vm)


ROW_TARGET = 544


def _mixer_out(attn, rec, g_a, g_r, w_out, h, g_next):
    tp, d = h.shape
    aw, rw = attn.shape[1], rec.shape[1]
    kc = d // N_CHIPS
    tm = _divisor_tile(tp, 16, ROW_TARGET)

    def body(a_ref, r_ref, ga_ref, gr_ref, w_ref, h_ref, gn_ref, h1_ref, z_ref, mix_ref):
        a = a_ref[...]
        r = r_ref[...]
        mix = jnp.concatenate([a * _rstd(a) * ga_ref[...], r * _rstd(r) * gr_ref[...]], axis=1).astype(BF16)
        mix_ref[...] = mix
        h1 = h_ref[...]
        for j in range(N_CHIPS):
            h1 = h1 + _dot(mix[:, j * kc:(j + 1) * kc], w_ref[j])
        h1_ref[...] = h1
        z_ref[...] = (h1 * _rstd(h1) * gn_ref[...]).astype(BF16)

    row = lambda wd: pl.BlockSpec((tm, wd), lambda i: (i, 0))
    vm = 2 * _nbytes((d, d), BF16) + 12 * _nbytes((tm, d), F32)
    return _call(body, (attn, rec, g_a, g_r, w_out, h, g_next), name="mixer_out", grid=(tp // tm,),
                 in_specs=[row(aw), row(rw), _full((1, aw)), _full((1, rw)), _full(w_out.shape), row(d),
                           _full((1, d))],
                 out_specs=[row(d), row(d), row(d)],
                 out_shape=[jax.ShapeDtypeStruct((tp, d), F32), jax.ShapeDtypeStruct((tp, d), BF16),
                            jax.ShapeDtypeStruct((tp, d), BF16)],
                 semantics=("parallel",), vmem_bytes=vm)


def _mixer_bwd(dh_b, w_out, attn, rec, g_a, g_r):
    tp, d = dh_b.shape
    aw, rw = attn.shape[1], rec.shape[1]
    tm = _divisor_tile(tp, 16, ROW_TARGET)

    def body(dh_ref, w_ref, a_ref, r_ref, ga_ref, gr_ref, da_ref, dr_ref, dg_ref):
        @pl.when(pl.program_id(0) == 0)
        def _():
            dg_ref[...] = jnp.zeros_like(dg_ref)

        dh = dh_ref[...]
        dmix = jnp.concatenate([_dot_nt(dh, w_ref[j]) for j in range(N_CHIPS)], axis=1)
        da, dga = _rms_bwd(dmix[:, :aw], a_ref[...], ga_ref[...])
        dr, dgr = _rms_bwd(dmix[:, aw:], r_ref[...], gr_ref[...])
        da_ref[...] = da
        dr_ref[...] = dr
        dg_ref[...] += jnp.broadcast_to(jnp.concatenate([dga, dgr], axis=1), (SUBLANES, d))

    row = lambda wd: pl.BlockSpec((tm, wd), lambda i: (i, 0))
    vm = 2 * _nbytes((d, d), BF16) + 12 * _nbytes((tm, d), F32)
    return _call(body, (dh_b, w_out, attn, rec, g_a, g_r), name="mixer_bwd", grid=(tp // tm,),
                 in_specs=[row(d), _full(w_out.shape), row(aw), row(rw), _full((1, aw)), _full((1, rw))],
                 out_specs=[row(aw), row(rw), _full((SUBLANES, d))],
                 out_shape=[jax.ShapeDtypeStruct((tp, aw), F32), jax.ShapeDtypeStruct((tp, rw), F32),
                            jax.ShapeDtypeStruct((SUBLANES, d), F32)],
                 semantics=("arbitrary",), vmem_bytes=vm)


def _mlp_up(z, w_up):
    tp, d = z.shape
    fc = w_up.shape[2]
    ff = N_CHIPS * fc
    tn = _divisor_tile(fc, LANES, 512)
    per = fc // tn

    def body(z_ref, w_ref, act_ref, up_ref):
        up = _dot(z_ref[...], w_ref[...])
        r = jnp.maximum(up, 0.0)
        act_ref[...] = (r * r).astype(BF16)
        up_ref[...] = up.astype(BF16)

    col = pl.BlockSpec((tp, tn), lambda j: (0, j))
    vm = 2 * _nbytes((tp, d), BF16) + 2 * _nbytes((d, tn), BF16) + 8 * _nbytes((tp, tn), F32)
    return _call(body, (z, w_up), name="mlp_up", grid=(ff // tn,),
                 in_specs=[_full((tp, d)), pl.BlockSpec((None, d, tn), lambda j: (j // per, 0, j % per))],
                 out_specs=[col, col],
                 out_shape=[jax.ShapeDtypeStruct((tp, ff), BF16)] * 2,
                 semantics=("parallel",), vmem_bytes=vm)


def _mlp_down(act, w_down, h, g_next):
    tp, d = h.shape
    ff = act.shape[1]
    fc = ff // N_CHIPS
    tm = _divisor_tile(tp, 16, ROW_TARGET)

    def body(a_ref, w_ref, h_ref, gn_ref, h2_ref, z_ref):
        h2 = h_ref[...]
        for j in range(N_CHIPS):
            h2 = h2 + _dot(a_ref[:, j * fc:(j + 1) * fc], w_ref[j])
        h2_ref[...] = h2
        z_ref[...] = (h2 * _rstd(h2) * gn_ref[...]).astype(BF16)

    row = lambda wd: pl.BlockSpec((tm, wd), lambda i: (i, 0))
    vm = 2 * _nbytes((ff, d), BF16) + 2 * _nbytes((tm, ff), BF16) + 10 * _nbytes((tm, d), F32)
    return _call(body, (act, w_down, h, g_next), name="mlp_down", grid=(tp // tm,),
                 in_specs=[row(ff), _full(w_down.shape), row(d), _full((1, d))],
                 out_specs=[row(d), row(d)],
                 out_shape=[jax.ShapeDtypeStruct((tp, d), F32), jax.ShapeDtypeStruct((tp, d), BF16)],
                 semantics=("parallel",), vmem_bytes=vm)


def _loss_bwd(h, g, target, n_real):
    tp, d = h.shape
    tm = _divisor_tile(tp, 16, ROW_TARGET)

    def body(h_ref, g_ref, t_ref, dh_ref, dhb_ref, dg_ref, loss_ref):
        i = pl.program_id(0)

        @pl.when(i == 0)
        def _():
            dg_ref[...] = jnp.zeros_like(dg_ref)
            loss_ref[...] = jnp.zeros_like(loss_ref)

        x = h_ref[...]
        gv = g_ref[...]
        rowi = i * tm + lax.broadcasted_iota(jnp.int32, (tm, 1), 0)
        real = jnp.logical_and(rowi >= N_META, rowi < N_META + n_real)
        err = jnp.where(real, x * _rstd(x) * gv - t_ref[...], 0.0)
        loss_ref[...] += 0.5 * jnp.sum(jnp.mean(err * err, axis=-1, keepdims=True))
        dx, dgp = _rms_bwd(err * (1.0 / d), x, gv)
        dh_ref[...] = dx
        dhb_ref[...] = dx.astype(BF16)
        dg_ref[...] += jnp.broadcast_to(dgp, (SUBLANES, d))

    row = pl.BlockSpec((tm, d), lambda i: (i, 0))
    return _call(body, (h, g, target), name="loss_bwd", grid=(tp // tm,),
                 in_specs=[row, _full((1, d)), row],
                 out_specs=[row, row, _full((SUBLANES, d)), _full((SUBLANES, LANES))],
                 out_shape=[jax.ShapeDtypeStruct((tp, d), F32), jax.ShapeDtypeStruct((tp, d), BF16),
                            jax.ShapeDtypeStruct((SUBLANES, d), F32), jax.ShapeDtypeStruct((SUBLANES, LANES), F32)],
                 semantics=("arbitrary",), vmem_bytes=16 * _nbytes((tm, d), F32))


def _mlp_bwd(dh_b, w_down, up, z2):
    tp, d = dh_b.shape
    fc = w_down.shape[1]
    ff = N_CHIPS * fc
    tn = _divisor_tile(fc, LANES, 512)
    per = fc // tn

    def body(dh_ref, z_ref, w_ref, up_ref, dup_ref, gd_ref, gu_ref):
        dh = dh_ref[...]
        r = jnp.maximum(up_ref[...].astype(F32), 0.0)
        dup = (_dot_nt(dh, w_ref[...]) * (2.0 * r)).astype(BF16)
        dup_ref[...] = dup
        gd_ref[...] = _dot_tn((r * r).astype(BF16), dh).astype(BF16)
        gu_ref[...] = _dot_tn(z_ref[...], dup).astype(BF16)

    col = pl.BlockSpec((tp, tn), lambda j: (0, j))
    vm = 4 * _nbytes((tp, d), BF16) + 4 * _nbytes((tn, d), BF16) + 2 * _nbytes((d, tn), BF16) \
        + 10 * _nbytes((tp, tn), F32) + 4 * _nbytes((tn, d), F32)
    return _call(body, (dh_b, z2, w_down, up), name="mlp_bwd", grid=(ff // tn,),
                 in_specs=[_full((tp, d)), _full((tp, d)),
                           pl.BlockSpec((None, tn, d), lambda j: (j // per, j % per, 0)), col],
                 out_specs=[col, pl.BlockSpec((tn, d), lambda j: (j, 0)),
                            pl.BlockSpec((None, d, tn), lambda j: (j // per, 0, j % per))],
                 out_shape=[jax.ShapeDtypeStruct((tp, ff), BF16), jax.ShapeDtypeStruct((ff, d), BF16),
                            jax.ShapeDtypeStruct((N_CHIPS, d, fc), BF16)],
                 semantics=("parallel",), vmem_bytes=vm)


def _grad_w_pieces(pieces, b):
    tp, n = b.shape
    tn = _divisor_tile(n, LANES, 512)
    widths = [pc.shape[1] for pc in pieces]

    def body(*refs):
        p_refs, b_ref, o_refs = refs[:len(pieces)], refs[len(pieces)], refs[len(pieces) + 1:]
        for p_ref, o_ref in zip(p_refs, o_refs):
            o_ref[...] = _dot_tn(p_ref[...], b_ref[...]).astype(BF16)

    vm = 2 * sum(_nbytes((tp, wd), BF16) for wd in widths) + 2 * _nbytes((tp, tn), BF16) \
        + 4 * sum(_nbytes((wd, tn), F32) for wd in widths) + 2 * _nbytes((tp, max(widths)), F32)
    return _call(body, tuple(pieces) + (b,), name="grad_w_pieces", grid=(n // tn,),
                 in_specs=[_full(pc.shape) for pc in pieces] + [pl.BlockSpec((tp, tn), lambda j: (0, j))],
                 out_specs=[pl.BlockSpec((wd, tn), lambda j: (0, j)) for wd in widths],
                 out_shape=[jax.ShapeDtypeStruct((wd, n), BF16) for wd in widths],
                 semantics=("parallel",), vmem_bytes=vm)


def _dx_norm_bwd(pieces, w, w_spec, w_piece, h, g, dres, dot=_dot_nt):
    tp, d = h.shape
    tm = _divisor_tile(tp, 16, ROW_TARGET)
    n = len(pieces)

    def body(*refs):
        dy_refs = refs[:n]
        w_ref, h_ref, g_ref, dres_ref, dh_ref, dhb_ref, dg_ref = refs[n:]

        @pl.when(pl.program_id(0) == 0)
        def _():
            dg_ref[...] = jnp.zeros_like(dg_ref)

        dz = dot(dy_refs[0][...], w_piece(w_ref, 0))
        for i in range(1, n):
            dz = dz + dot(dy_refs[i][...], w_piece(w_ref, i))
        dx, dgp = _rms_bwd(dz, h_ref[...], g_ref[...])
        dh = dres_ref[...] + dx
        dh_ref[...] = dh
        dhb_ref[...] = dh.astype(BF16)
        dg_ref[...] += jnp.broadcast_to(dgp, (SUBLANES, d))

    row = lambda wd: pl.BlockSpec((tm, wd), lambda i: (i, 0))
    kk = sum(wd for _, _, wd in pieces)
    vm = 2 * _nbytes((d, kk), BF16) + 2 * _nbytes((tm, kk), BF16) + 14 * _nbytes((tm, d), F32)
    piece_specs = [pl.BlockSpec((tm, wd), functools.partial(lambda i, cb: (i, cb), cb=cb)) for _, cb, wd in pieces]
    return _call(body, tuple(a for a, _, _ in pieces) + (w, h, g, dres), name="dx_norm_bwd", grid=(tp // tm,),
                 in_specs=piece_specs + [w_spec, row(d), _full((1, d)), row(d)],
                 out_specs=[row(d), row(d), _full((SUBLANES, d))],
                 out_shape=[jax.ShapeDtypeStruct((tp, d), F32), jax.ShapeDtypeStruct((tp, d), BF16),
                            jax.ShapeDtypeStruct((SUBLANES, d), F32)],
                 semantics=("arbitrary",), vmem_bytes=vm)


def _block_diag(wg):
    nb, b, _ = wg.shape
    eye = jnp.eye(nb, dtype=wg.dtype)
    return (eye[:, None, :, None] * wg[:, :, None, :]).reshape(nb * b, nb * b)


def _diag_blocks(dense, nb):
    b = dense.shape[0] // nb
    d4 = dense.reshape(nb, b, nb, b)
    return jnp.stack([d4[i, :, i, :] for i in range(nb)])


def _row(v):
    return v.reshape(1, -1)


def _forward_layer(l, h, z, p, fetch, stage_next, g_next):
    d = h.shape[1]
    att_w = d // 2
    rec_w = d - att_w
    nh = att_w // HEAD_DIM
    wa_d = _block_diag(p["w_gate_a"][l]).astype(BF16)
    wx_d = _block_diag(p["w_gate_x"][l]).astype(BF16)
    b_f_pad = jnp.zeros((1, LANES), F32).at[0, :nh].set(p["b_f"][l])
    w_in_t = fetch("w_in", z)
    big = dict(w_in_big=_pack_w_in_t(w_in_t.reshape(-1, d), att_w, nh))
    qkv, proj = _proj(z, big["w_in_big"], att_w)
    c, c_t = _fgate_fwd(proj, b_f_pad, nh)
    attn, lse_b = _attn_fwd(qkv, c, c_t, nh)
    hr, rec = _rec_fwd(proj, 0, 1, rec_w, p["conv_w"][l], _row(p["conv_b"][l]), wa_d,
                       _row(p["b_gate_a"][l]), wx_d, _row(p["b_gate_x"][l]), _row(p["lru_L"][l]))
    tok = stage_next(attn, "own")
    big["w_out"] = fetch("w_out", rec)
    h1, z2, mix = _mixer_out(attn, rec, _row(p["attn_out_g"][l] + tok), _row(p["rec_out_g"][l]),
                             big["w_out"], h, _row(p["mlp_norm_g"][l]))
    big["w_up"] = fetch("w_up", h1)
    act, up = _mlp_up(z2, big["w_up"])
    tok = stage_next(act, "next")
    big["w_down"] = fetch("w_down", act)
    h2, z_next = _mlp_down(act, big["w_down"], h1, _row(g_next + tok))
    saved = dict(h0=h, z1=z, proj=proj, qkv=qkv, c=c, c_t=c_t, attn=attn, lse_b=lse_b, hr=hr, rec=rec, h1=h1,
                 z2=z2, mix=mix, up=up, wa_d=wa_d, wx_d=wx_d, b_f_pad=b_f_pad, big=big)
    return h2, z_next, saved


def _backward_mlp(l, dh, dh_b, sv, p, tok):
    w_up, w_down = sv["big"]["w_up"], sv["big"]["w_down"]
    fc = w_up.shape[2]
    dup, g_down, g_up = _mlp_bwd(dh_b, w_down, sv["up"], sv["z2"])
    dh, dh_b, dg2 = _dx_norm_bwd([(dup, j, fc) for j in range(N_CHIPS)], w_up, _full(w_up.shape),
                                 lambda w_ref, j: w_ref[j], sv["h1"], _row(p["mlp_norm_g"][l] + tok), dh)
    big = dict(w_down=g_down.reshape((N_CHIPS, -1) + g_down.shape[1:]), w_up=g_up)
    return dh, dh_b, big, dict(mlp_norm_g=dg2[0])


def _backward_mixer(l, dh, dh_b, sv, p, tok):
    d = dh.shape[1]
    att_w = d // 2
    rec_w = d - att_w
    nh = att_w // HEAD_DIM
    small = {}
    g_out, = _grad_w_pieces([sv["mix"]], dh_b)
    dattn, drec, dg_mix = _mixer_bwd(dh_b, sv["big"]["w_out"], sv["attn"], sv["rec"],
                                     _row(p["attn_out_g"][l] + tok), _row(p["rec_out_g"][l]))
    small["attn_out_g"] = dg_mix[0, :att_w]
    small["rec_out_g"] = dg_mix[0, att_w:]
    dxr, dyr, dwa, dwx, sm = _rec_bwd(
        sv["proj"], 0, 1, rec_w, sv["hr"], drec, p["conv_w"][l], _row(p["conv_b"][l]), sv["wa_d"],
        _row(p["b_gate_a"][l]), sv["wx_d"], _row(p["b_gate_x"][l]), _row(p["lru_L"][l]))
    small.update(conv_w=sm[:CONV_WIDTH], conv_b=sm[4], b_gate_a=sm[5], b_gate_x=sm[6], lru_L=sm[7],
                 w_gate_a=_diag_blocks(dwa, N_REC_BLOCKS), w_gate_x=_diag_blocks(dwx, N_REC_BLOCKS))
    dq, dk, dv, dc = _attn_bwd(sv["qkv"], sv["c"], sv["c_t"], sv["lse_b"], dattn, nh)
    df, db_f = _fgate_bwd(sv["proj"], sv["b_f_pad"], dc)
    small["b_f"] = db_f[0, :nh]
    pieces = [dq, dk, dv, dxr, dyr, df]
    offs = [0, att_w, 2 * att_w, 3 * att_w, 3 * att_w + rec_w, 3 * att_w + 2 * rec_w]
    gq, gk, gv, gxr, gyr, gf = _grad_w_pieces(pieces, sv["z1"])
    g_in_t = jnp.concatenate([gq, gk, gv, gf[:nh], gxr, gyr], axis=0)
    w_big = sv["big"]["w_in_big"]
    widths = [pc.shape[1] for pc in pieces]
    dh, dh_b, dg1 = _dx_norm_bwd(
        [(pc, 0, wd) for pc, wd in zip(pieces, widths)], w_big, _full(w_big.shape),
        lambda w_ref, i: w_ref[offs[i]:offs[i] + widths[i], :], sv["h0"], _row(p["attn_norm_g"][l]), dh, dot=_dot)
    small["attn_norm_g"] = dg1[0]
    big = dict(w_in=g_in_t.reshape(N_CHIPS, -1, d), w_out=g_out.reshape((N_CHIPS, -1) + g_out.shape[1:]))
    return dh, dh_b, big, small


def _pack_w_in_t(w_in_t, att_w, nh):
    qkv = w_in_t[:3 * att_w]
    f = w_in_t[3 * att_w:3 * att_w + nh]
    xy = w_in_t[3 * att_w + nh:]
    return jnp.concatenate([qkv, xy, f, jnp.zeros((LANES - nh, w_in_t.shape[1]), w_in_t.dtype)], axis=0)


ANY = pl.BlockSpec(memory_space=pl.ANY)


def _coords():
    return lax.axis_index("x"), lax.axis_index("y"), lax.axis_index("c")


def _other_chips(x, y):
    return [(1 - x, y), (x, 1 - y), (1 - x, 1 - y)]


def _remote(src, dst, send_sems, recv_sems, k, to):
    return pltpu.make_async_remote_copy(src_ref=src, dst_ref=dst, send_sem=send_sems.at[k],
                                        recv_sem=recv_sems.at[k], device_id=to, device_id_type=MESH)


def _all_gather_chips(shards):
    n = len(shards)
    per = 6

    def body(*refs):
        ins, outs = refs[:n], refs[n:2 * n]
        send_sems, recv_sems, local_sems = refs[2 * n:]
        x, y, c = _coords()
        me = 2 * x + y
        sibling = (x, y, 1 - c)
        chips = _other_chips(x, y)
        local = [pltpu.make_async_copy(ins[t], outs[t].at[me], local_sems.at[t]) for t in range(n)]
        for cp in local:
            cp.start()
        sends = []
        for t in range(n):
            for j, (px, py) in enumerate(chips):
                cp = _remote(ins[t].at[c], outs[t].at[me, c], send_sems, recv_sems, per * t + j, (px, py, c))
                cp.start()
                sends.append(cp)
        for t in range(n):
            for j, (px, py) in enumerate(chips):
                landed = outs[t].at[2 * px + py, c]
                _remote(landed, landed, send_sems, recv_sems, per * t + j, (px, py, c)).wait_recv()
                cp = _remote(landed, landed, send_sems, recv_sems, per * t + 3 + j, sibling)
                cp.start()
                sends.append(cp)
        for t in range(n):
            for j, (px, py) in enumerate(chips):
                passed = outs[t].at[2 * px + py, 1 - c]
                _remote(passed, passed, send_sems, recv_sems, per * t + 3 + j, sibling).wait_recv()
        for cp in sends:
            cp.wait_send()
        for cp in local:
            cp.wait()

    return _call(body, tuple(shards), name="all_gather_chips",
                 in_specs=[ANY] * n, out_specs=[ANY] * n,
                 out_shape=[jax.ShapeDtypeStruct((N_CHIPS,) + s.shape, s.dtype) for s in shards],
                 scratch_shapes=[pltpu.SemaphoreType.DMA((per * n,)), pltpu.SemaphoreType.DMA((per * n,)),
                                 pltpu.SemaphoreType.DMA((n,))])


HBM = pl.BlockSpec(memory_space=pltpu.HBM)
SEM = pl.BlockSpec(memory_space=pltpu.SEMAPHORE)
DATAFLOW = pltpu.SideEffectType.DATAFLOW_SIDE_EFFECTING


def _in_hbm(a):
    return pltpu.with_memory_space_constraint(a, pltpu.HBM)


PUSH_ARRIVALS = {"gather_chips_half": N_CHIPS - 1, "pass_halves": N_CHIPS - 1, "scatter_chips": N_CHIPS - 1,
                 "sibling": 1, "gather_devices": N_DEV - 1}


def _column_half(ref3, slab, c):
    hw = ref3.shape[2] // 2
    return ref3.at[slab, :, pl.ds(pl.multiple_of(c * hw, LANES), hw)]


def _push_copies(mode, src, land, send_sems, recv_sems, t):
    x, y, c = _coords()
    chip = 2 * x + y
    if mode == "gather_chips_half":
        return [_remote(_column_half(src, chip, c), _column_half(land, chip, c), send_sems, recv_sems, t, (px, py, c))
                for px, py in _other_chips(x, y)]
    if mode == "pass_halves":
        return [_remote(_column_half(src, 2 * px + py, c), _column_half(land, 2 * px + py, c), send_sems, recv_sems, t,
                        (x, y, 1 - c)) for px, py in _other_chips(x, y)]
    if mode == "scatter_chips":
        return [_remote(src.at[2 * px + py], land.at[chip], send_sems, recv_sems, t, (px, py, c))
                for px, py in _other_chips(x, y)]
    if mode == "sibling":
        return [_remote(src, land, send_sems, recv_sems, t, (x, y, 1 - c))]
    dev = 4 * x + 2 * y + c
    return [_remote(src.at[dev], land.at[dev], send_sems, recv_sems, t, (x ^ (k >> 2), y ^ ((k >> 1) & 1), c ^ (k & 1)))
            for k in range(1, N_DEV)]


def _push_start(srcs, lands, mode, name):
    n = len(srcs)
    same = all(s is ld for s, ld in zip(srcs, lands))
    n_in = n if same else 2 * n

    def body(*refs):
        src_refs = refs[:n]
        land_refs = src_refs if same else refs[n:2 * n]
        send_sems, recv_sems = refs[n_in], refs[n_in + 1]
        token = refs[-1]
        for t in range(n):
            for cp in _push_copies(mode, src_refs[t], land_refs[t], send_sems, recv_sems, t):
                cp.start()
        token[...] = jnp.zeros_like(token)

    operands = tuple(srcs) if same else tuple(srcs) + tuple(lands)
    res = _call(
        body, [_in_hbm(a) for a in operands], name=name,
        out_shape=(pltpu.SemaphoreType.DMA((n,)), pltpu.SemaphoreType.DMA((n,)))
        + tuple(pltpu.HBM(a.shape, a.dtype) for a in operands) + (jax.ShapeDtypeStruct((SUBLANES, LANES), F32),),
        in_specs=[HBM] * n_in, out_specs=(SEM, SEM) + (HBM,) * n_in + (pl.BlockSpec(memory_space=pltpu.VMEM),),
        input_output_aliases={i: 2 + i for i in range(n_in)}, side_effects=DATAFLOW, hbm_results=False)
    send_sems, recv_sems, token = res[0], res[1], res[-1]
    srcs_thru = res[2:2 + n]
    lands_thru = srcs_thru if same else res[2 + n:2 + 2 * n]
    return send_sems, recv_sems, srcs_thru, lands_thru, token


def _push_wait(send_sems, recv_sems, ids, srcs, lands, mode, after, name):
    n = len(lands)
    same = all(s is ld for s, ld in zip(srcs, lands))
    n_in = n if same else 2 * n

    def body(*refs):
        land_refs = refs[:n] if same else refs[n:2 * n]
        send_sems, recv_sems = refs[n_in], refs[n_in + 1]
        x, y, c = _coords()
        for t in range(n):
            if mode == "sibling":
                moved = land_refs[t]
            elif mode in ("gather_chips_half", "pass_halves"):
                moved = land_refs[t].at[pl.ds(0, PUSH_ARRIVALS[mode]), :, pl.ds(0, land_refs[t].shape[2] // 2)]
            else:
                moved = land_refs[t].at[pl.ds(0, PUSH_ARRIVALS[mode])]
            arrivals = _remote(moved, moved, send_sems, recv_sems, ids[t], (x, y, c))
            arrivals.wait_send()
            arrivals.wait_recv()

    operands = tuple(lands) if same else tuple(srcs) + tuple(lands)
    res = _call(
        body, operands + (send_sems, recv_sems, after), name=name,
        out_shape=tuple(pltpu.HBM(a.shape, a.dtype) for a in operands),
        in_specs=[HBM] * n_in + [SEM, SEM, ANY], out_specs=(HBM,) * n_in,
        input_output_aliases={i: i for i in range(n_in)}, side_effects=DATAFLOW)
    return list(res) if same else (list(res[:n]), list(res[n:]))


def _sum_partials(part, landed, chip):
    _, rows, cols = part.shape
    br = _divisor_tile(rows, 16, ELEM_ROWS)

    def body(chip_ref, own_ref, a_ref, b_ref, c_ref, o_ref):
        o_ref[...] = ((own_ref[...].astype(F32) + a_ref[...].astype(F32)) + b_ref[...].astype(F32)) \
            + c_ref[...].astype(F32)

    def other(k):
        return pl.BlockSpec((None, br, cols), lambda i, ch: (jnp.where(ch[0] <= k, k + 1, k), i, 0))

    spec = pltpu.PrefetchScalarGridSpec(
        num_scalar_prefetch=1, grid=(rows // br,),
        in_specs=[pl.BlockSpec((None, br, cols), lambda i, ch: (ch[0], i, 0)), other(0), other(1), other(2)],
        out_specs=pl.BlockSpec((br, cols), lambda i, ch: (i, 0)))
    return _call(body, (chip, part, landed, landed, landed), name="sum_partials", grid_spec=spec,
                 out_shape=jax.ShapeDtypeStruct((rows, cols), F32), semantics=("parallel",))


def _cast_to_slab(w, l, chip, after):
    _, rows, cols = w.shape
    br = _divisor_tile(rows, 16, ELEM_ROWS)

    def body(chip_ref, w_ref, after_ref, o_ref):
        o_ref[...] = w_ref[...].astype(BF16)

    spec = pltpu.PrefetchScalarGridSpec(
        num_scalar_prefetch=1, grid=(rows // br,),
        in_specs=[pl.BlockSpec((None, br, cols), lambda i, ch: (l, i, 0)), ANY],
        out_specs=pl.BlockSpec((None, br, cols), lambda i, ch: (ch[0], i, 0)))
    return _call(body, (chip, w, after), name="cast_to_slab", grid_spec=spec,
                 out_shape=jax.ShapeDtypeStruct((N_CHIPS, rows, cols), BF16), semantics=("parallel",))


def _cast_w_in_t_to_slabs(w_t, chip):
    rows, depth, d = w_t.shape
    tn = _divisor_tile(d, LANES, 256)

    def body(chip_ref, w_ref, *o_refs):
        for l in range(depth):
            o_refs[l][...] = w_ref[:, l, :].astype(BF16)

    spec = pltpu.PrefetchScalarGridSpec(
        num_scalar_prefetch=1, grid=(d // tn,),
        in_specs=[pl.BlockSpec((rows, depth, tn), lambda j, ch: (0, 0, j))],
        out_specs=[pl.BlockSpec((None, rows, tn), lambda j, ch: (ch[0], 0, j))] * depth)
    return _call(body, (chip, w_t), name="cast_w_in_t_to_slabs", grid_spec=spec,
                 out_shape=[jax.ShapeDtypeStruct((N_CHIPS, rows, d), BF16)] * depth, semantics=("parallel",),
                 vmem_bytes=4 * _nbytes((rows, max(depth, SUBLANES), tn), F32))


def _place_slab(buf, index, n_slabs):
    rows, cols = buf.shape
    br = _divisor_tile(rows, SUBLANES, ELEM_ROWS)

    def body(index_ref, b_ref, o_ref):
        o_ref[...] = b_ref[...]

    spec = pltpu.PrefetchScalarGridSpec(
        num_scalar_prefetch=1, grid=(rows // br,),
        in_specs=[pl.BlockSpec((br, cols), lambda i, ix: (i, 0))],
        out_specs=pl.BlockSpec((None, br, cols), lambda i, ix: (ix[0], i, 0)))
    return _call(body, (index, buf), name="place_slab", grid_spec=spec,
                 out_shape=jax.ShapeDtypeStruct((n_slabs, rows, cols), buf.dtype), semantics=("parallel",))


ELEM_ROWS = 256


def _sum_slabs(r):
    n, rows, cols = r.shape
    br = _divisor_tile(rows, 16, ELEM_ROWS)

    def body(r_ref, o_ref):
        acc = r_ref[0].astype(F32)
        for j in range(1, n):
            acc = acc + r_ref[j].astype(F32)
        o_ref[...] = acc

    return _call(body, (r,), name="sum_slabs", grid=(rows // br,),
                 in_specs=[pl.BlockSpec((n, br, cols), lambda i: (0, i, 0))],
                 out_specs=pl.BlockSpec((br, cols), lambda i: (i, 0)),
                 out_shape=jax.ShapeDtypeStruct((rows, cols), F32), semantics=("parallel",))


def _adamw_math(w, g, m, v):
    c1 = 1.0 - ADAM_B1 ** ADAM_STEP
    c2 = 1.0 - ADAM_B2 ** ADAM_STEP
    nm = ADAM_B1 * m + (1.0 - ADAM_B1) * g
    nv = ADAM_B2 * v + (1.0 - ADAM_B2) * (g * g)
    delta = -ADAM_LR * ((nm / c1) / (jnp.sqrt(nv / c2) + ADAM_EPS) + ADAM_WD * w)
    return delta, nm, nv


def _adamw(w, g, m, v):
    rows, cols = w.shape
    br = _divisor_tile(rows, 8, ELEM_ROWS)

    def body(w_ref, g_ref, m_ref, v_ref, d_ref, nm_ref, nv_ref):
        d_ref[...], nm_ref[...], nv_ref[...] = _adamw_math(w_ref[...], g_ref[...], m_ref[...], v_ref[...])

    blk = pl.BlockSpec((br, cols), lambda i: (i, 0))
    return _call(body, (w, g, m, v), name="adamw", grid=(rows // br,),
                 in_specs=[blk] * 4, out_specs=[blk] * 3,
                 out_shape=[jax.ShapeDtypeStruct((rows, cols), F32)] * 3, semantics=("parallel",))


def _adamw_w_in_t(w_t, m_t, v_t, g_mine, g_theirs):
    rows, depth, d = w_t.shape
    tn = LANES

    def body(w_ref, m_ref, v_ref, ga_ref, gb_ref, g_ref, d_ref, nm_ref, nv_ref):
        g = ga_ref[...] + gb_ref[...]
        g_ref[...] = g
        d_ref[...], nm_ref[...], nv_ref[...] = _adamw_math(w_ref[...], g, m_ref[...], v_ref[...])

    slab = pl.BlockSpec((rows, depth, tn), lambda j: (0, 0, j))
    return _call(body, (w_t, m_t, v_t, g_mine, g_theirs), name="adamw_w_in_t", grid=(d // tn,),
                 in_specs=[slab] * 5, out_specs=[slab] * 4,
                 out_shape=[jax.ShapeDtypeStruct(w_t.shape, F32)] * 4, semantics=("parallel",),
                 vmem_bytes=2 * 9 * _nbytes((rows, max(depth, SUBLANES), tn), F32))


def _adamw_layer(w, m, v, l, g_mine, g_theirs, prev, after):
    _, rows, cols = w.shape
    br = _divisor_tile(rows, 8, ELEM_ROWS)

    def body(w_ref, m_ref, v_ref, ga_ref, gb_ref, *rest):
        g_ref, d_ref, nm_ref, nv_ref = rest[5:]
        g = ga_ref[...] + gb_ref[...]
        g_ref[...] = g
        d_ref[...], nm_ref[...], nv_ref[...] = _adamw_math(w_ref[...], g, m_ref[...], v_ref[...])

    slot = pl.BlockSpec((None, br, cols), lambda i: (l, i, 0))
    blk = pl.BlockSpec((br, cols), lambda i: (i, 0))
    return _call(body, (w, m, v, g_mine, g_theirs) + tuple(prev) + (after,), name="adamw_layer",
                 grid=(rows // br,), in_specs=[slot] * 3 + [blk] * 2 + [ANY] * 5, out_specs=[slot] * 4,
                 out_shape=[jax.ShapeDtypeStruct(w.shape, F32)] * 4,
                 input_output_aliases={5: 0, 6: 1, 7: 2, 8: 3}, semantics=("parallel",))


BIG = ("w_in", "w_out", "w_up", "w_down")
WEIGHTS = ("meta", "attn_norm_g", "w_in", "b_f", "conv_w", "conv_b", "w_gate_a", "b_gate_a", "w_gate_x",
           "b_gate_x", "lru_L", "attn_out_g", "rec_out_g", "w_out", "mlp_norm_g", "w_up", "w_down", "final_g")
SMALL = tuple(k for k in WEIGHTS if k not in BIG)
COL_SHARDED_SMALL = ("meta", "conv_w")


def _packed_rows(shape):
    return -(-math.prod(shape) // (SUBLANES * LANES)) * SUBLANES


def _pack(arrs):
    rows = []
    for a in arrs:
        flat = a.reshape(-1)
        rows.append(jnp.pad(flat, (0, _packed_rows(a.shape) * LANES - flat.shape[0])).reshape(-1, LANES))
    used = sum(r.shape[0] for r in rows)
    rows.append(jnp.zeros((-used % ELEM_ROWS, LANES), F32))
    return jnp.concatenate(rows, axis=0)


def _unpack(buf, shapes):
    out, r0 = [], 0
    for s in shapes:
        nr = _packed_rows(s)
        out.append(buf[r0:r0 + nr].reshape(-1)[:math.prod(s)].reshape(s))
        r0 += nr
    return out


def _halves(a):
    return a.reshape((2, a.shape[0] // 2) + a.shape[1:])


def _cols_from_chips(g):
    return jnp.moveaxis(g, 0, -2).reshape(g.shape[1:-1] + (N_CHIPS * g.shape[-1],))


def kernel(x, meta, attn_norm_g, w_in, b_f, conv_w, conv_b, w_gate_a, b_gate_a, w_gate_x, b_gate_x, lru_L, attn_out_g, rec_out_g, w_out, mlp_norm_g, w_up, w_down, final_g, loss_target, m_meta, m_attn_norm_g, m_w_in, m_b_f, m_conv_w, m_conv_b, m_w_gate_a, m_b_gate_a, m_w_gate_x, m_b_gate_x, m_lru_L, m_attn_out_g, m_rec_out_g, m_w_out, m_mlp_norm_g, m_w_up, m_w_down, m_final_g, v_meta, v_attn_norm_g, v_w_in, v_b_f, v_conv_w, v_conv_b, v_w_gate_a, v_b_gate_a, v_w_gate_x, v_b_gate_x, v_lru_L, v_attn_out_g, v_rec_out_g, v_w_out, v_mlp_norm_g, v_w_up, v_w_down, v_final_g):
    w = dict(meta=meta, attn_norm_g=attn_norm_g, w_in=w_in, b_f=b_f, conv_w=conv_w, conv_b=conv_b,
             w_gate_a=w_gate_a, b_gate_a=b_gate_a, w_gate_x=w_gate_x, b_gate_x=b_gate_x, lru_L=lru_L,
             attn_out_g=attn_out_g, rec_out_g=rec_out_g, w_out=w_out, mlp_norm_g=mlp_norm_g, w_up=w_up,
             w_down=w_down, final_g=final_g)
    m = dict(meta=m_meta, attn_norm_g=m_attn_norm_g, w_in=m_w_in, b_f=m_b_f, conv_w=m_conv_w, conv_b=m_conv_b,
             w_gate_a=m_w_gate_a, b_gate_a=m_b_gate_a, w_gate_x=m_w_gate_x, b_gate_x=m_b_gate_x, lru_L=m_lru_L,
             attn_out_g=m_attn_out_g, rec_out_g=m_rec_out_g, w_out=m_w_out, mlp_norm_g=m_mlp_norm_g,
             w_up=m_w_up, w_down=m_w_down, final_g=m_final_g)
    v = dict(meta=v_meta, attn_norm_g=v_attn_norm_g, w_in=v_w_in, b_f=v_b_f, conv_w=v_conv_w, conv_b=v_conv_b,
             w_gate_a=v_w_gate_a, b_gate_a=v_b_gate_a, w_gate_x=v_w_gate_x, b_gate_x=v_b_gate_x, lru_L=v_lru_L,
             attn_out_g=v_attn_out_g, rec_out_g=v_rec_out_g, w_out=v_w_out, mlp_norm_g=v_mlp_norm_g,
             w_up=v_w_up, w_down=v_w_down, final_g=v_final_g)
    s_len, d = x.shape[1], x.shape[2]
    depth = w_in.shape[0]
    att_w = d // 2
    rec_w = d - att_w
    nh = att_w // HEAD_DIM
    chip = 2 * lax.axis_index("x") + lax.axis_index("y")

    chip1 = chip.reshape(1).astype(jnp.int32)
    w_in_t, m_in_t, v_in_t = [jnp.transpose(a["w_in"], (2, 0, 1)) for a in (w, m, v)]
    w_in_slabs = _cast_w_in_t_to_slabs(w_in_t, chip1)
    pushes, tokens = [], []
    first = w_in_slabs[0]
    for l in range(depth):
        slabs = [w_in_slabs[l]] + [_cast_to_slab(w[k], l, chip1, first) for k in BIG[1:]]
        send_sems, recv_sems, _, lands, token = _push_start(slabs, slabs, "gather_chips_half", f"weights_start_{l}")
        pushes.append((send_sems, recv_sems, lands))
        tokens.append(token[0, 0])
        first = token if l == 0 else first
    passed = [{} for _ in range(depth)]

    g_conv, g_meta = [g.reshape((N_CHIPS, g.shape[1] * g.shape[2]) + g.shape[3:])
                      for g in _all_gather_chips([_halves(w["conv_w"]), _halves(w["meta"] + tokens[0])])]
    p = dict(w)
    p["conv_w"] = _cols_from_chips(g_conv)
    meta_full = jnp.moveaxis(g_meta, 0, 1).reshape(N_META, d)

    def stage(l, after, ids):
        send_sems, recv_sems, lands = pushes[l]
        tag = "_".join(BIG[i] for i in ids)
        sub = [lands[i] for i in ids]
        sub = _push_wait(send_sems, recv_sems, ids, sub, sub, "gather_chips_half", after, f"{tag}_wait_{l}")
        send_sems, recv_sems, _, sub, token = _push_start(sub, sub, "pass_halves", f"{tag}_pass_{l}")
        for j, i in enumerate(ids):
            passed[l][i] = (send_sems, recv_sems, sub[j], j)
        return token[0, 0]

    t_len = N_META + s_len
    pad = -t_len % SEQ_TILE
    h = jnp.concatenate([meta_full, x[0], jnp.zeros((pad, d), F32)], axis=0)
    tgt = jnp.concatenate([jnp.zeros((N_META, d), F32), loss_target[0], jnp.zeros((pad, d), F32)], axis=0)
    z = _rms_fwd(h, _row(p["attn_norm_g"][0] + sum(tokens)))
    stage(0, z, [0])
    saved = []
    for l in range(depth):
        def fetch(k, after, l=l):
            send_sems, recv_sems, land, j = passed[l][BIG.index(k)]
            return _push_wait(send_sems, recv_sems, [j], [land], [land], "pass_halves", after,
                              f"{k}_here_{l}")[0]

        def stage_next(after, which, l=l):
            if which == "own":
                return stage(l, after, [1, 2, 3])
            return stage(l + 1, after, [0]) if l + 1 < depth else 0.0

        g_next = p["attn_norm_g"][l + 1] if l + 1 < depth else p["final_g"]
        h, z, sv = _forward_layer(l, h, z, p, fetch, stage_next, g_next)
        saved.append(sv)
    dh, dh_b, dg_final, loss_part = _loss_bwd(h, _row(p["final_g"]), tgt, s_len)

    small = {k: [None] * depth for k in SMALL if k not in ("meta", "final_g")}
    pushes = [None] * depth
    tok = 0.0
    for l in reversed(range(depth)):
        dh, dh_b, big_mlp, sm_mlp = _backward_mlp(l, dh, dh_b, saved[l], p, tok)
        parts = [big_mlp["w_down"], big_mlp["w_up"]]
        push_mlp = _push_start(parts, [lax.empty(a.shape, a.dtype) for a in parts], "scatter_chips",
                               f"mlp_grads_start_{l}")
        dh, dh_b, big_mix, sm_mix = _backward_mixer(l, dh, dh_b, saved[l], p, push_mlp[4][0, 0])
        parts = [big_mix["w_out"], big_mix["w_in"]]
        push_mix = _push_start(parts, [lax.empty(a.shape, a.dtype) for a in parts], "scatter_chips",
                               f"mixer_grads_start_{l}")
        tok = push_mix[4][0, 0]
        pushes[l] = {("w_down", "w_up"): push_mlp, ("w_out", "w_in"): push_mix}
        for k, val in {**sm_mlp, **sm_mix}.items():
            small[k][l] = val
    grads = {k: jnp.stack(val) for k, val in small.items()}
    grads["final_g"] = dg_final[0]
    grads["meta"] = dh[:N_META]
    dx = dh[N_META:t_len]

    full_shapes = [grads[k].shape for k in SMALL] + [(1,)]
    packed = _pack([grads[k].astype(F32) for k in SMALL] + [loss_part[0, :1] + tok])
    dev1 = (2 * chip + lax.axis_index("c")).reshape(1).astype(jnp.int32)
    slabs = [_place_slab(packed, dev1, N_DEV)]
    small_push = _push_start(slabs, slabs, "gather_devices", "small_grads_start")

    last_token = small_push[4]
    outs = {k: [lax.empty(w[k].shape, F32) for _ in range(4)] for k in BIG[1:]}
    w_in_sums = [None] * depth
    swaps = {}

    def finish(l, wait_after, adam_after):
        send_sems, recv_sems, mine, lands, _ = swaps[l]
        mine, theirs = _push_wait(send_sems, recv_sems, list(range(len(BIG))), mine, lands, "sibling", wait_after,
                                  f"sums_wait_{l}")
        w_in_sums[l] = (mine[0], theirs[0])
        for k, a, b in zip(BIG[1:], mine[1:], theirs[1:]):
            outs[k] = _adamw_layer(w[k], m[k], v[k], l, a, b, outs[k], adam_after)

    for l in reversed(range(depth)):
        sums = {}
        for names, (send_sems, recv_sems, parts, lands, _) in pushes[l].items():
            parts, landed = _push_wait(send_sems, recv_sems, [0, 1], parts, lands, "scatter_chips", last_token,
                                       f"{names[0]}_grads_wait_{l}")
            for k, part, land in zip(names, parts, landed):
                sums[k] = _sum_partials(part, land, chip1)
        mine = [sums[k] for k in BIG]
        swaps[l] = _push_start(mine, [lax.empty(a.shape, a.dtype) for a in mine], "sibling", f"sums_start_{l}")
        if l + 1 < depth:
            finish(l + 1, outs["w_down"][0] if l + 2 < depth else mine[0], swaps[l][4])
    finish(0, outs["w_down"][0] if depth > 1 else swaps[0][4], swaps[0][4])
    outs["w_in"] = [jnp.transpose(r, (1, 2, 0)) for r in _adamw_w_in_t(
        w_in_t, m_in_t, v_in_t, jnp.stack([s[0] for s in w_in_sums], axis=1),
        jnp.stack([s[1] for s in w_in_sums], axis=1))]
    out_g, out_d, out_m, out_v = [{k: outs[k][i] for k in BIG} for i in range(4)]

    landed = _push_wait(small_push[0], small_push[1], [0], small_push[3], small_push[3], "gather_devices",
                        out_g["w_in"], "small_grads_wait")
    total = _sum_slabs(landed[0])
    small_g = dict(zip(SMALL + ("loss",), _unpack(total, full_shapes)))
    for k in COL_SHARDED_SMALL:
        n = w[k].shape[-1]
        small_g[k] = lax.dynamic_slice_in_dim(small_g[k], chip * n, n, axis=small_g[k].ndim - 1)
    local_shapes = [w[k].shape for k in SMALL]
    res = _adamw(_pack([w[k] for k in SMALL]), _pack([small_g[k] for k in SMALL]),
                 _pack([m[k] for k in SMALL]), _pack([v[k] for k in SMALL]))
    out_g.update({k: small_g[k] for k in SMALL})
    for dst, buf in zip((out_d, out_m, out_v), res):
        dst.update(zip(SMALL, _unpack(buf, local_shapes)))

    return (small_g["loss"].reshape(()), dx[None],
            *[out_g[k] for k in WEIGHTS], *[out_d[k] for k in WEIGHTS],
            *[out_m[k] for k in WEIGHTS], *[out_v[k] for k in WEIGHTS])
```

```python
import functools
import math

import jax
import jax.numpy as jnp
from jax import lax
from jax.experimental import pallas as pl
from jax.experimental.pallas import tpu as pltpu

F32 = jnp.float32
BF16 = jnp.bfloat16

N_META = 16
HEAD_DIM = 64
N_REC_BLOCKS = 8
CONV_WIDTH = 4
RG_C = 8.0
NORM_EPS = 1e-6
ADAM_LR = 0.001
ADAM_B1 = 0.9
ADAM_B2 = 0.999
ADAM_EPS = 1e-08
ADAM_WD = 0.01
ADAM_STEP = 10

LANES = 128
SUBLANES = 8
SEQ_TILE = 128
VMEM_CAP = 60 * 2**20
VMEM_SLACK = 6 * 2**20
NEG_BIG = -1e30
N_CHIPS = 4
N_DEV = 8
MESH = pl.DeviceIdType.MESH


def _nbytes(shape, dtype):
    return math.prod(shape) * jnp.dtype(dtype).itemsize


def _call(body, args, *, name, out_shape, grid=(), in_specs=None, out_specs=None, scratch_shapes=(),
          grid_spec=None, semantics=None, vmem_bytes=None, side_effects=None, hbm_results=True, **kw):
    cp = {}
    if semantics is not None:
        cp["dimension_semantics"] = semantics
    if vmem_bytes is not None:
        cp["vmem_limit_bytes"] = int(min(VMEM_CAP, vmem_bytes + VMEM_SLACK))
    if side_effects is not None:
        cp["has_side_effects"] = side_effects
    if grid_spec is not None:
        kw["grid_spec"] = grid_spec
    else:
        kw.update(grid=grid, in_specs=in_specs, out_specs=out_specs, scratch_shapes=scratch_shapes)
    if hbm_results:
        out_shape = jax.tree.map(
            lambda s: pltpu.HBM(s.shape, s.dtype) if isinstance(s, jax.ShapeDtypeStruct) else s, out_shape)
    fn = pl.pallas_call(
        body, name=name, out_shape=out_shape,
        compiler_params=pltpu.CompilerParams(**cp), **kw)
    return fn(*[_in_hbm(a) if jnp.issubdtype(getattr(a, "dtype", jnp.int32), jnp.floating) else a for a in args])


def _divisor_tile(n, unit, target):
    best = None
    for t in range(unit, min(n, target) + 1, unit):
        if n % t == 0:
            best = t
    return n if best is None else best


def _sigmoid(x):
    return 1.0 / (1.0 + jnp.exp(-x))


def _log1p_unit(e):
    series = e * (1.0 - e * (0.5 - e * (1.0 / 3.0)))
    return jnp.where(e < 1e-2, series, jnp.log(1.0 + e))


def _log_sigmoid(x):
    return jnp.minimum(x, 0.0) - _log1p_unit(jnp.exp(-jnp.abs(x)))


def _one_minus_exp(x, exp_x):
    small = -x * (1.0 + x * (1.0 / 2 + x * (1.0 / 6 + x * (1.0 / 24 + x * (1.0 / 120 + x * (1.0 / 720))))))
    return jnp.where(x > -0.25, small, 1.0 - exp_x)


_GELU_K = math.sqrt(2.0 / math.pi)
_GELU_C = 0.044715


def _gelu_and_grad(y):
    th = jnp.tanh(_GELU_K * (y + _GELU_C * y * y * y))
    g = 0.5 * y * (1.0 + th)
    dg = 0.5 * (1.0 + th) + 0.5 * y * (1.0 - th * th) * _GELU_K * (1.0 + 3.0 * _GELU_C * y * y)
    return g, dg


def _rstd(x):
    return lax.rsqrt(jnp.mean(x * x, axis=-1, keepdims=True) + NORM_EPS)


def _rms_bwd(dz, x, g):
    rs = _rstd(x)
    xh = x * rs
    dgp = jnp.sum(dz * xh, axis=0, keepdims=True)
    dxh = dz * g
    dx = rs * (dxh - xh * jnp.mean(dxh * xh, axis=-1, keepdims=True))
    return dx, dgp


def _dot(a, b):
    return jnp.dot(a, b, preferred_element_type=F32)


def _dot_nt(a, b):
    return lax.dot_general(a, b, (((1,), (1,)), ((), ())), preferred_element_type=F32)


def _dot_tn(a, b):
    return lax.dot_general(a, b, (((0,), (0,)), ((), ())), preferred_element_type=F32)


def _full(shape):
    nd = len(shape)
    return pl.BlockSpec(shape, lambda *_: (0,) * nd)


def _rms_fwd(h, g):
    tp, d = h.shape
    tm = _divisor_tile(tp, 16, 544)

    def body(h_ref, g_ref, z_ref):
        x = h_ref[...]
        z_ref[...] = (x * _rstd(x) * g_ref[...]).astype(BF16)

    return _call(body, (h, g), name="rms_fwd", grid=(tp // tm,),
                 in_specs=[pl.BlockSpec((tm, d), lambda i: (i, 0)), _full((1, d))],
                 out_specs=pl.BlockSpec((tm, d), lambda i: (i, 0)),
                 out_shape=jax.ShapeDtypeStruct((tp, d), BF16), semantics=("parallel",))


def _proj(z, w_big_t, att_w):
    tp, d = z.shape
    nb = w_big_t.shape[0]
    tn = _divisor_tile(nb, LANES, 512)
    assert (3 * att_w) % tn == 0
    n_qkv = 3 * att_w // tn
    scale = 1.0 / math.sqrt(HEAD_DIM)

    def body(z_ref, w_ref, qkv_ref, p_ref):
        j = pl.program_id(0)
        acc = _dot_nt(z_ref[...], w_ref[...])

        @pl.when(j < n_qkv)
        def _():
            col = j * tn + lax.broadcasted_iota(jnp.int32, (1, tn), 1)
            qkv_ref[...] = (acc * jnp.where(col < att_w, scale, 1.0)).astype(BF16)

        @pl.when(j >= n_qkv)
        def _():
            p_ref[...] = acc

    vm = 2 * (_nbytes((tp, d), BF16) + _nbytes((d, tn), BF16) + _nbytes((tp, tn), F32) * 2)
    return _call(body, (z, w_big_t), name="proj", grid=(nb // tn,),
                 in_specs=[_full((tp, d)), pl.BlockSpec((tn, d), lambda j: (j, 0))],
                 out_specs=[pl.BlockSpec((tp, tn), lambda j: (0, jnp.minimum(j, n_qkv - 1))),
                            pl.BlockSpec((tp, tn), lambda j: (0, jnp.maximum(j - n_qkv, 0)))],
                 out_shape=[jax.ShapeDtypeStruct((tp, 3 * att_w), BF16),
                            jax.ShapeDtypeStruct((tp, nb - 3 * att_w), F32)],
                 semantics=("arbitrary",), vmem_bytes=vm)


def _tile_cumsum(x, row, reverse=False):
    for s in (1, 2, 4):
        if reverse:
            x = x + jnp.where(row < SUBLANES - s, pltpu.roll(x, SUBLANES - s, 0), 0.0)
        else:
            x = x + jnp.where(row >= s, pltpu.roll(x, s, 0), 0.0)
    return x


def _fgate_fwd(proj, b_f_pad, nh):
    tp, nb = proj.shape
    fblk = nb // LANES - 1

    def body(f_ref, b_ref, c_ref, ct_ref):
        b = b_ref[...]
        row = lax.broadcasted_iota(jnp.int32, (SUBLANES, LANES), 0)

        def step(i, carry):
            r0 = pl.multiple_of(i * SUBLANES, SUBLANES)
            lf = _log_sigmoid(f_ref[pl.ds(r0, SUBLANES), :] + b)
            x = _tile_cumsum(lf, row) + carry
            c_ref[pl.ds(r0, SUBLANES), :] = x
            return x[SUBLANES - 1:SUBLANES, :]

        lax.fori_loop(0, tp // SUBLANES, step, jnp.zeros((1, LANES), F32))
        ct_ref[...] = c_ref[...].T[:nh, :]

    return _call(body, (proj, b_f_pad), name="fgate_fwd", grid=(1,),
                 in_specs=[pl.BlockSpec((tp, LANES), lambda i: (0, fblk)), _full((1, LANES))],
                 out_specs=[_full((tp, LANES)), _full((nh, tp))],
                 out_shape=[jax.ShapeDtypeStruct((tp, LANES), F32), jax.ShapeDtypeStruct((nh, tp), F32)],
                 semantics=("arbitrary",))


def _fgate_bwd(proj, b_f_pad, dc):
    tp, nb = proj.shape
    fblk = nb // LANES - 1

    def body(f_ref, b_ref, dc_ref, df_ref, db_ref, dc_s):
        b = b_ref[...]
        row = lax.broadcasted_iota(jnp.int32, (SUBLANES, LANES), 0)
        nt = tp // SUBLANES

        def step(i, carry):
            suffix, acc = carry
            r0 = pl.multiple_of((nt - 1 - i) * SUBLANES, SUBLANES)
            dlf = _tile_cumsum(dc_ref[pl.ds(r0, SUBLANES), :], row, reverse=True) + suffix
            df = dlf * _sigmoid(-(f_ref[pl.ds(r0, SUBLANES), :] + b))
            dc_s[pl.ds(r0, SUBLANES), :] = df
            return dlf[0:1, :], acc + df

        _, acc = lax.fori_loop(0, nt, step, (jnp.zeros((1, LANES), F32), jnp.zeros((SUBLANES, LANES), F32)))
        df_ref[...] = dc_s[...].astype(BF16)
        db_ref[...] = jnp.broadcast_to(jnp.sum(acc, axis=0, keepdims=True), (SUBLANES, LANES))

    return _call(body, (proj, b_f_pad, dc), name="fgate_bwd", grid=(1,),
                 in_specs=[pl.BlockSpec((tp, LANES), lambda i: (0, fblk)), _full((1, LANES)), _full((tp, LANES))],
                 out_specs=[_full((tp, LANES)), _full((SUBLANES, LANES))],
                 out_shape=[jax.ShapeDtypeStruct((tp, LANES), BF16),
                            jax.ShapeDtypeStruct((SUBLANES, LANES), F32)],
                 scratch_shapes=[pltpu.VMEM((tp, LANES), F32)], semantics=("arbitrary",))


ATT_BQ = 128


ATT_BUCKET = 3
ATT_HEADS = 4


def _for_bucket(i, nq, fn):
    for lo in range(0, nq, ATT_BUCKET):
        hi = min(lo + ATT_BUCKET, nq)
        spans = ([(0, lo * ATT_BQ, False)] if lo else []) + [(lo * ATT_BQ, hi * ATT_BQ, True)]
        pl.when(jnp.logical_and(i >= lo, i < hi))(functools.partial(fn, spans))


def _head_column(c_blk, h):
    lane = lax.broadcasted_iota(jnp.int32, c_blk.shape, 1)
    return jnp.sum(jnp.where(lane == h, c_blk, 0.0), axis=1, keepdims=True)


def _head_columns_into(p, c_ref, ck_s):
    for hh in range(ATT_HEADS):
        ck_s[hh] = jnp.broadcast_to(_head_column(c_ref[...], ATT_HEADS * p + hh), ck_s.shape[1:])


def _pair_diag_cols(x2):
    top = lax.broadcasted_iota(jnp.int32, (LANES, ATT_BQ), 0) < HEAD_DIM
    xt = x2.astype(F32).T.astype(BF16)
    return jnp.concatenate([jnp.where(top, xt, 0), jnp.where(top, 0, xt)], axis=1)


def _pair_diag_rows(x2):
    low = lax.broadcasted_iota(jnp.int32, (ATT_BQ, LANES), 1) < HEAD_DIM
    return jnp.concatenate([jnp.where(low, x2, 0), jnp.where(low, 0, x2)], axis=0)


def _seen_keys(k0, k1, q0):
    keys = k0 + lax.broadcasted_iota(jnp.int32, (k1 - k0, ATT_BQ), 0)
    return keys <= q0 + lax.broadcasted_iota(jnp.int32, (k1 - k0, ATT_BQ), 1)


def _attn_fwd(qkv, c, c_t, nh):
    tp = qkv.shape[0]
    att_w = nh * HEAD_DIM
    ng = nh // ATT_HEADS
    gw = ATT_HEADS * HEAD_DIM
    bq = ATT_BQ
    nq = tp // bq
    pair = 2 * HEAD_DIM
    assert pair == LANES and ATT_HEADS % 2 == 0

    def body(q_ref, k_ref, v_ref, c_ref, ct_ref, o_ref, lse_ref, ck_s, vt_s):
        p = pl.program_id(0)
        i = pl.program_id(1)

        @pl.when(i == 0)
        def _():
            _head_columns_into(p, c_ref, ck_s)
            vt_s[...] = v_ref[...].astype(F32).T.astype(BF16)

        def compute(spans):
            q0 = pl.multiple_of(i * bq, bq)
            o_t, lses = [], []
            for pi in range(ATT_HEADS // 2):
                lo = pair * pi
                heads = (2 * pi, 2 * pi + 1)
                q_cols = _pair_diag_cols(q_ref[:, lo:lo + pair])
                ts = []
                for k0, k1, needs_mask in spans:
                    t2 = _dot(k_ref[k0:k1, lo:lo + pair], q_cols)
                    t_e = [t2[:, e * bq:(e + 1) * bq] - ck_s[hh, k0:k1, :] for e, hh in enumerate(heads)]
                    if needs_mask:
                        seen = _seen_keys(k0, k1, q0)
                        t_e = [jnp.where(seen, t, NEG_BIG) for t in t_e]
                    ts.append(t_e)
                ms = [functools.reduce(jnp.maximum, [jnp.max(t[e], axis=0, keepdims=True) for t in ts])
                      for e in range(2)]
                es = [[jnp.exp(t[e] - ms[e]) for e in range(2)] for t in ts]
                ls = [sum(jnp.sum(e_[e], axis=0, keepdims=True) for e_ in es) for e in range(2)]
                o2 = sum(_dot(vt_s[lo:lo + pair, k0:k1],
                              jnp.concatenate([e_[0].astype(BF16), e_[1].astype(BF16)], axis=1))
                         for e_, (k0, k1, _) in zip(es, spans))
                o_t += [o2[:HEAD_DIM, :bq] / ls[0], o2[HEAD_DIM:, bq:] / ls[1]]
                lses += [ms[e] + ct_ref[pl.ds(ATT_HEADS * p + hh, 1), :] + jnp.log(ls[e])
                         for e, hh in enumerate(heads)]
            o_ref[...] = jnp.concatenate(o_t, axis=0).T
            lse_ref[...] = jnp.concatenate(lses, axis=0)

        _for_bucket(i, nq, compute)

    blk = pl.BlockSpec((bq, gw), lambda p, i: (i, p))
    vm = 6 * _nbytes((tp, gw), BF16) + 2 * ATT_HEADS * _nbytes((tp, LANES), F32) + 2 * _nbytes((tp, LANES), F32) \
        + 8 * ATT_HEADS * _nbytes((bq, tp), F32)
    return _call(body, (qkv, qkv, qkv, c, c_t), name="attn_fwd", grid=(ng, nq),
                 in_specs=[blk,
                           pl.BlockSpec((tp, gw), lambda p, i: (0, ng + p)),
                           pl.BlockSpec((tp, gw), lambda p, i: (0, 2 * ng + p)),
                           _full((tp, LANES)), pl.BlockSpec((nh, bq), lambda p, i: (0, i))],
                 out_specs=[blk, pl.BlockSpec((None, ATT_HEADS, bq), lambda p, i: (p, 0, i))],
                 out_shape=[jax.ShapeDtypeStruct((tp, att_w), F32), jax.ShapeDtypeStruct((ng, ATT_HEADS, tp), F32)],
                 scratch_shapes=[pltpu.VMEM((ATT_HEADS, tp, LANES), F32), pltpu.VMEM((gw, tp), BF16)],
                 semantics=("arbitrary", "arbitrary"), vmem_bytes=vm)


def _attn_bwd(qkv, c, c_t, lse, do, nh):
    tp = qkv.shape[0]
    att_w = nh * HEAD_DIM
    ng = nh // ATT_HEADS
    gw = ATT_HEADS * HEAD_DIM
    bq = ATT_BQ
    nq = tp // bq
    pair = 2 * HEAD_DIM
    assert pair == LANES and ATT_HEADS % 2 == 0
    scale = 1.0 / math.sqrt(HEAD_DIM)

    def body(q_ref, k_ref, v_ref, c_ref, ct_ref, lse_ref, do_ref, dq_ref, dk_ref, dv_ref, dc_ref,
             dk_s, dv_s, dc_s, ck_s, kt_s):
        p = pl.program_id(0)
        i = pl.program_id(1)

        @pl.when(i == 0)
        def _():
            dk_s[...] = jnp.zeros_like(dk_s)
            dv_s[...] = jnp.zeros_like(dv_s)
            dc_s[...] = jnp.zeros_like(dc_s)
            kt_s[...] = k_ref[...].astype(F32).T.astype(BF16)
            _head_columns_into(p, c_ref, ck_s)

        @pl.when(jnp.logical_and(i == 0, p == 0))
        def _():
            dc_ref[...] = jnp.zeros_like(dc_ref)

        def compute(spans):
            q0 = pl.multiple_of(i * bq, bq)
            dq_t = []
            for pi in range(ATT_HEADS // 2):
                lo = pair * pi
                q2 = q_ref[:, lo:lo + pair]
                do2 = do_ref[:, lo:lo + pair].astype(BF16)
                q_cols, do_cols = _pair_diag_cols(q2), _pair_diag_cols(do2)
                q_rows, do_rows = _pair_diag_rows(q2), _pair_diag_rows(do2)
                heads = (2 * pi, 2 * pi + 1)
                col_terms = [ct_ref[pl.ds(ATT_HEADS * p + hh, 1), :] - lse_ref[hh:hh + 1, :] for hh in heads]
                prs, dps = [], []
                for k0, k1, needs_mask in spans:
                    t2 = _dot(k_ref[k0:k1, lo:lo + pair], q_cols)
                    dp2 = _dot(v_ref[k0:k1, lo:lo + pair], do_cols)
                    if needs_mask:
                        seen = _seen_keys(k0, k1, q0)
                    pr_e, dp_e = [], []
                    for e, hh in enumerate(heads):
                        t = t2[:, e * bq:(e + 1) * bq] - ck_s[hh, k0:k1, :]
                        if needs_mask:
                            t = jnp.where(seen, t, NEG_BIG)
                        pr_e.append(jnp.exp(t + col_terms[e]))
                        dp_e.append(dp2[:, e * bq:(e + 1) * bq])
                    prs.append(pr_e)
                    dps.append(dp_e)
                key_sums = [sum(jnp.sum(pr[e] * dp[e], axis=0, keepdims=True) for pr, dp in zip(prs, dps))
                            for e in range(2)]
                dq2 = 0.0
                for (k0, k1, _), pr, dp in zip(spans, prs, dps):
                    ds = [pr[e] * (dp[e] - key_sums[e]) for e in range(2)]
                    for e, hh in enumerate(heads):
                        dc_s[hh, k0:k1, :] += jnp.sum(ds[e], axis=1, keepdims=True)
                    ds2 = jnp.concatenate([ds[0].astype(BF16), ds[1].astype(BF16)], axis=1)
                    pr2 = jnp.concatenate([pr[0].astype(BF16), pr[1].astype(BF16)], axis=1)
                    dk_s[k0:k1, lo:lo + pair] += _dot(ds2, q_rows)
                    dv_s[k0:k1, lo:lo + pair] += _dot(pr2, do_rows)
                    dq2 = dq2 + _dot(kt_s[lo:lo + pair, k0:k1], ds2)
                dq_t.append(jnp.concatenate([dq2[:HEAD_DIM, :bq], dq2[HEAD_DIM:, bq:]], axis=0))
            dq_ref[...] = (jnp.concatenate(dq_t, axis=0) * scale).T.astype(BF16)

        _for_bucket(i, nq, compute)

        @pl.when(i == nq - 1)
        def _():
            dk_ref[...] = dk_s[...].astype(BF16)
            dv_ref[...] = dv_s[...].astype(BF16)
            lane = lax.broadcasted_iota(jnp.int32, (tp, LANES), 1)
            dc = dc_ref[...]
            for hh in range(ATT_HEADS):
                dc = jnp.where(lane == ATT_HEADS * p + hh, -dc_s[hh], dc)
            dc_ref[...] = dc

    blk = pl.BlockSpec((bq, gw), lambda p, i: (i, p))
    col = pl.BlockSpec((tp, gw), lambda p, i: (0, p))
    vm = 7 * _nbytes((tp, gw), BF16) + 2 * _nbytes((tp, gw), F32) + 2 * ATT_HEADS * _nbytes((tp, LANES), F32) \
        + 2 * _nbytes((tp, LANES), F32) + 12 * ATT_HEADS * _nbytes((bq, tp), F32)
    return _call(body, (qkv, qkv, qkv, c, c_t, lse, do), name="attn_bwd", grid=(ng, nq),
                 in_specs=[blk,
                           pl.BlockSpec((tp, gw), lambda p, i: (0, ng + p)),
                           pl.BlockSpec((tp, gw), lambda p, i: (0, 2 * ng + p)),
                           _full((tp, LANES)), pl.BlockSpec((nh, bq), lambda p, i: (0, i)),
                           pl.BlockSpec((None, ATT_HEADS, bq), lambda p, i: (p, 0, i)), blk],
                 out_specs=[blk, col, col, _full((tp, LANES))],
                 out_shape=[jax.ShapeDtypeStruct((tp, att_w), BF16)] * 3 + [jax.ShapeDtypeStruct((tp, LANES), F32)],
                 scratch_shapes=[pltpu.VMEM((tp, gw), F32), pltpu.VMEM((tp, gw), F32),
                                 pltpu.VMEM((ATT_HEADS, tp, 1), F32), pltpu.VMEM((ATT_HEADS, tp, LANES), F32),
                                 pltpu.VMEM((gw, tp), BF16)],
                 semantics=("arbitrary", "arbitrary"), vmem_bytes=vm)


REC_ROWS = 128
HALO = SUBLANES


def _conv_taps(cat):
    taps = []
    for k in range(CONV_WIDTH):
        sh = CONV_WIDTH - 1 - k
        taps.append((pltpu.roll(cat, sh, 0) if sh else cat)[HALO:])
    return taps


def _rec_gates(xc, wa_ref, ba_ref, wx_ref, bx_ref, l_ref):
    xcb = xc.astype(BF16)
    r = _sigmoid(_dot(xcb, wa_ref[...]) + ba_ref[...])
    ig = _sigmoid(_dot(xcb, wx_ref[...]) + bx_ref[...])
    ls = _log_sigmoid(l_ref[...])
    log_a = RG_C * r * ls
    return xcb, r, ig, ls, log_a


def _rec_fwd(proj, xr_blk, yr_blk, rec_w, conv_w, conv_b, wa, ba, wx, bx, lru):
    tp = proj.shape[0]
    w = rec_w
    r_rows = REC_ROWS
    nc = tp // r_rows
    cpb = w // LANES

    def body(xr_ref, yr_ref, cw_ref, cb_ref, wa_ref, ba_ref, wx_ref, bx_ref, l_ref,
             hr_ref, rec_ref, prev_s, carry_s, a_s, u_s):
        i = pl.program_id(0)

        @pl.when(i == 0)
        def _():
            prev_s[...] = jnp.zeros_like(prev_s)
            carry_s[...] = jnp.zeros_like(carry_s)

        x = xr_ref[...]
        taps = _conv_taps(jnp.concatenate([prev_s[...], x], axis=0))
        prev_s[...] = x[r_rows - HALO:]
        xc = cb_ref[...]
        for k in range(CONV_WIDTH):
            xc = xc + cw_ref[k:k + 1, :] * taps[k]
        _, r, ig, ls, log_a = _rec_gates(xc, wa_ref, ba_ref, wx_ref, bx_ref, l_ref)
        a = jnp.exp(log_a)
        a_s[...] = a
        u_s[...] = jnp.sqrt(_one_minus_exp(2.0 * log_a, a * a)) * ig * xc

        def tile(j, h):
            r0 = pl.multiple_of(j * SUBLANES, SUBLANES)
            at = a_s[pl.ds(r0, SUBLANES), :]
            ut = u_s[pl.ds(r0, SUBLANES), :]
            out = []
            for rr in range(SUBLANES):
                h = at[rr:rr + 1] * h + ut[rr:rr + 1]
                out.append(h)
            hr_ref[pl.ds(r0, SUBLANES), :] = jnp.concatenate(out, axis=0)
            return h

        carry_s[0:1, :] = lax.fori_loop(0, r_rows // SUBLANES, tile, carry_s[0:1, :])
        g, _ = _gelu_and_grad(yr_ref[...])
        rec_ref[...] = hr_ref[...] * g

    blk = pl.BlockSpec((r_rows, w), lambda i: (i, 0))
    vm = 16 * _nbytes((r_rows, w), F32) + 4 * _nbytes((w, w), BF16)
    return _call(body, (proj, proj, conv_w, conv_b, wa, ba, wx, bx, lru), name="rec_fwd", grid=(nc,),
                 in_specs=[pl.BlockSpec((r_rows, w), lambda i: (i, xr_blk)),
                           pl.BlockSpec((r_rows, w), lambda i: (i, yr_blk)),
                           _full((CONV_WIDTH, w)), _full((1, w)), _full((w, w)), _full((1, w)),
                           _full((w, w)), _full((1, w)), _full((1, w))],
                 out_specs=[blk, blk],
                 out_shape=[jax.ShapeDtypeStruct((tp, w), F32)] * 2,
                 scratch_shapes=[pltpu.VMEM((HALO, w), F32), pltpu.VMEM((SUBLANES, w), F32),
                                 pltpu.VMEM((r_rows, w), F32), pltpu.VMEM((r_rows, w), F32)],
                 semantics=("arbitrary",), vmem_bytes=vm)


def _rec_bwd(proj, xr_blk, yr_blk, rec_w, hr, drec, conv_w, conv_b, wa, ba, wx, bx, lru):
    tp = proj.shape[0]
    w = rec_w
    r_rows = REC_ROWS
    nc = tp // r_rows
    hpc = r_rows // HALO

    def body(xr_ref, xh_ref, yr_ref, hr_ref, hh_ref, drec_ref, cw_ref, cb_ref, wa_ref, ba_ref, wx_ref, bx_ref,
             l_ref, dxr_ref, dyr_ref, dwa_ref, dwx_ref, small_ref, lam_s, a_s, dhr_s, carry_s, next_s):
        i = pl.program_id(0)
        first = (nc - 1 - i) == 0

        @pl.when(i == 0)
        def _():
            carry_s[...] = jnp.zeros_like(carry_s)
            next_s[...] = jnp.zeros_like(next_s)
            dwa_ref[...] = jnp.zeros_like(dwa_ref)
            dwx_ref[...] = jnp.zeros_like(dwx_ref)
            small_ref[...] = jnp.zeros_like(small_ref)

        x = xr_ref[...]
        xprev = jnp.where(first, 0.0, xh_ref[...])
        taps = _conv_taps(jnp.concatenate([xprev, x], axis=0))
        xc = cb_ref[...]
        for k in range(CONV_WIDTH):
            xc = xc + cw_ref[k:k + 1, :] * taps[k]
        xcb, r, ig, ls, log_a = _rec_gates(xc, wa_ref, ba_ref, wx_ref, bx_ref, l_ref)
        a = jnp.exp(log_a)
        a2 = a * a
        mult = jnp.sqrt(_one_minus_exp(2.0 * log_a, a2))
        g, dg = _gelu_and_grad(yr_ref[...])
        hr_v = hr_ref[...]
        drec_v = drec_ref[...]
        dhr_s[...] = drec_v * g
        dyr_ref[...] = (drec_v * hr_v * dg).astype(BF16)
        a_s[...] = a

        def tile(jj, carry):
            r0 = pl.multiple_of((r_rows // SUBLANES - 1 - jj) * SUBLANES, SUBLANES)
            at = a_s[pl.ds(r0, SUBLANES), :]
            dt = dhr_s[pl.ds(r0, SUBLANES), :]
            out = [None] * SUBLANES
            for rr in range(SUBLANES - 1, -1, -1):
                lam = dt[rr:rr + 1] + carry
                out[rr] = lam
                carry = at[rr:rr + 1] * lam
            lam_s[pl.ds(r0, SUBLANES), :] = jnp.concatenate(out, axis=0)
            return carry

        carry_s[0:1, :] = lax.fori_loop(0, r_rows // SUBLANES, tile, carry_s[0:1, :])
        lam = lam_s[...]
        hprev = jnp.where(first, 0.0, hh_ref[...])
        hr_prev = pltpu.roll(jnp.concatenate([hprev, hr_v], axis=0), 1, 0)[HALO:]
        da = lam * hr_prev
        dxc = lam * mult * ig
        di = lam * mult * xc
        dmult = lam * ig * xc
        dlog_a = da * a - dmult * a2 / mult
        dr = dlog_a * (RG_C * ls)
        dls = jnp.sum(dlog_a * (RG_C * r), axis=0, keepdims=True)
        dga = dr * r * (1.0 - r)
        dgx = di * ig * (1.0 - ig)
        dgab = dga.astype(BF16)
        dgxb = dgx.astype(BF16)
        dxc = dxc + _dot_nt(dgab, wa_ref[...]) + _dot_nt(dgxb, wx_ref[...])
        dwa_ref[...] += _dot_tn(xcb, dgab)
        dwx_ref[...] += _dot_tn(xcb, dgxb)
        cat = jnp.concatenate([dxc, next_s[...]], axis=0)
        next_s[...] = dxc[0:HALO]
        dxr = cw_ref[CONV_WIDTH - 1:CONV_WIDTH, :] * dxc
        for k in range(CONV_WIDTH - 1):
            sh = CONV_WIDTH - 1 - k
            dxr = dxr + cw_ref[k:k + 1, :] * pltpu.roll(cat, r_rows + HALO - sh, 0)[:r_rows]
        dxr_ref[...] = dxr.astype(BF16)
        rows = [jnp.sum(dxc * taps[k], axis=0, keepdims=True) for k in range(CONV_WIDTH)]
        rows += [jnp.sum(dxc, axis=0, keepdims=True), jnp.sum(dga, axis=0, keepdims=True),
                 jnp.sum(dgx, axis=0, keepdims=True), dls * _sigmoid(-l_ref[...])]
        small_ref[...] += jnp.concatenate(rows, axis=0)

    def rev(i):
        return nc - 1 - i

    def halo(i):
        return jnp.maximum(rev(i) * hpc - 1, 0)

    blk = pl.BlockSpec((r_rows, w), lambda i: (rev(i), 0))
    vm = 40 * _nbytes((r_rows, w), F32) + 6 * _nbytes((w, w), F32)
    return _call(body, (proj, proj, proj, hr, hr, drec, conv_w, conv_b, wa, ba, wx, bx, lru),
                 name="rec_bwd", grid=(nc,),
                 in_specs=[pl.BlockSpec((r_rows, w), lambda i: (rev(i), xr_blk)),
                           pl.BlockSpec((HALO, w), lambda i: (halo(i), xr_blk)),
                           pl.BlockSpec((r_rows, w), lambda i: (rev(i), yr_blk)),
                           blk,
                           pl.BlockSpec((HALO, w), lambda i: (halo(i), 0)),
                           blk,
                           _full((CONV_WIDTH, w)), _full((1, w)), _full((w, w)), _full((1, w)),
                           _full((w, w)), _full((1, w)), _full((1, w))],
                 out_specs=[blk, blk, _full((w, w)), _full((w, w)), _full((SUBLANES, w))],
                 out_shape=[jax.ShapeDtypeStruct((tp, w), BF16)] * 2
                 + [jax.ShapeDtypeStruct((w, w), F32)] * 2 + [jax.ShapeDtypeStruct((SUBLANES, w), F32)],
                 scratch_shapes=[pltpu.VMEM((r_rows, w), F32)] * 3
                 + [pltpu.VMEM((SUBLANES, w), F32), pltpu.VMEM((HALO, w), F32)],
                 semantics=("arbitrary",), vmem_bytes=vm)


ROW_TARGET = 544


def _mixer_out(attn, rec, g_a, g_r, w_out, h, g_next):
    tp, d = h.shape
    aw, rw = attn.shape[1], rec.shape[1]
    kc = d // N_CHIPS
    tm = _divisor_tile(tp, 16, ROW_TARGET)

    def body(a_ref, r_ref, ga_ref, gr_ref, w_ref, h_ref, gn_ref, h1_ref, z_ref, mix_ref):
        a = a_ref[...]
        r = r_ref[...]
        mix = jnp.concatenate([a * _rstd(a) * ga_ref[...], r * _rstd(r) * gr_ref[...]], axis=1).astype(BF16)
        mix_ref[...] = mix
        h1 = h_ref[...]
        for j in range(N_CHIPS):
            h1 = h1 + _dot(mix[:, j * kc:(j + 1) * kc], w_ref[j])
        h1_ref[...] = h1
        z_ref[...] = (h1 * _rstd(h1) * gn_ref[...]).astype(BF16)

    row = lambda wd: pl.BlockSpec((tm, wd), lambda i: (i, 0))
    vm = 2 * _nbytes((d, d), BF16) + 12 * _nbytes((tm, d), F32)
    return _call(body, (attn, rec, g_a, g_r, w_out, h, g_next), name="mixer_out", grid=(tp // tm,),
                 in_specs=[row(aw), row(rw), _full((1, aw)), _full((1, rw)), _full(w_out.shape), row(d),
                           _full((1, d))],
                 out_specs=[row(d), row(d), row(d)],
                 out_shape=[jax.ShapeDtypeStruct((tp, d), F32), jax.ShapeDtypeStruct((tp, d), BF16),
                            jax.ShapeDtypeStruct((tp, d), BF16)],
                 semantics=("parallel",), vmem_bytes=vm)


def _mixer_bwd(dh_b, w_out, attn, rec, g_a, g_r):
    tp, d = dh_b.shape
    aw, rw = attn.shape[1], rec.shape[1]
    tm = _divisor_tile(tp, 16, ROW_TARGET)

    def body(dh_ref, w_ref, a_ref, r_ref, ga_ref, gr_ref, da_ref, dr_ref, dg_ref):
        @pl.when(pl.program_id(0) == 0)
        def _():
            dg_ref[...] = jnp.zeros_like(dg_ref)

        dh = dh_ref[...]
        dmix = jnp.concatenate([_dot_nt(dh, w_ref[j]) for j in range(N_CHIPS)], axis=1)
        da, dga = _rms_bwd(dmix[:, :aw], a_ref[...], ga_ref[...])
        dr, dgr = _rms_bwd(dmix[:, aw:], r_ref[...], gr_ref[...])
        da_ref[...] = da
        dr_ref[...] = dr
        dg_ref[...] += jnp.broadcast_to(jnp.concatenate([dga, dgr], axis=1), (SUBLANES, d))

    row = lambda wd: pl.BlockSpec((tm, wd), lambda i: (i, 0))
    vm = 2 * _nbytes((d, d), BF16) + 12 * _nbytes((tm, d), F32)
    return _call(body, (dh_b, w_out, attn, rec, g_a, g_r), name="mixer_bwd", grid=(tp // tm,),
                 in_specs=[row(d), _full(w_out.shape), row(aw), row(rw), _full((1, aw)), _full((1, rw))],
                 out_specs=[row(aw), row(rw), _full((SUBLANES, d))],
                 out_shape=[jax.ShapeDtypeStruct((tp, aw), F32), jax.ShapeDtypeStruct((tp, rw), F32),
                            jax.ShapeDtypeStruct((SUBLANES, d), F32)],
                 semantics=("arbitrary",), vmem_bytes=vm)


def _mlp_up(z, w_up):
    tp, d = z.shape
    fc = w_up.shape[2]
    ff = N_CHIPS * fc
    tn = _divisor_tile(fc, LANES, 512)
    per = fc // tn

    def body(z_ref, w_ref, act_ref, up_ref):
        up = _dot(z_ref[...], w_ref[...])
        r = jnp.maximum(up, 0.0)
        act_ref[...] = (r * r).astype(BF16)
        up_ref[...] = up.astype(BF16)

    col = pl.BlockSpec((tp, tn), lambda j: (0, j))
    vm = 2 * _nbytes((tp, d), BF16) + 2 * _nbytes((d, tn), BF16) + 8 * _nbytes((tp, tn), F32)
    return _call(body, (z, w_up), name="mlp_up", grid=(ff // tn,),
                 in_specs=[_full((tp, d)), pl.BlockSpec((None, d, tn), lambda j: (j // per, 0, j % per))],
                 out_specs=[col, col],
                 out_shape=[jax.ShapeDtypeStruct((tp, ff), BF16)] * 2,
                 semantics=("parallel",), vmem_bytes=vm)


def _mlp_down(act, w_down, h, g_next):
    tp, d = h.shape
    ff = act.shape[1]
    fc = ff // N_CHIPS
    tm = _divisor_tile(tp, 16, ROW_TARGET)

    def body(a_ref, w_ref, h_ref, gn_ref, h2_ref, z_ref):
        h2 = h_ref[...]
        for j in range(N_CHIPS):
            h2 = h2 + _dot(a_ref[:, j * fc:(j + 1) * fc], w_ref[j])
        h2_ref[...] = h2
        z_ref[...] = (h2 * _rstd(h2) * gn_ref[...]).astype(BF16)

    row = lambda wd: pl.BlockSpec((tm, wd), lambda i: (i, 0))
    vm = 2 * _nbytes((ff, d), BF16) + 2 * _nbytes((tm, ff), BF16) + 10 * _nbytes((tm, d), F32)
    return _call(body, (act, w_down, h, g_next), name="mlp_down", grid=(tp // tm,),
                 in_specs=[row(ff), _full(w_down.shape), row(d), _full((1, d))],
                 out_specs=[row(d), row(d)],
                 out_shape=[jax.ShapeDtypeStruct((tp, d), F32), jax.ShapeDtypeStruct((tp, d), BF16)],
                 semantics=("parallel",), vmem_bytes=vm)


def _loss_bwd(h, g, target, n_real):
    tp, d = h.shape
    tm = _divisor_tile(tp, 16, ROW_TARGET)

    def body(h_ref, g_ref, t_ref, dh_ref, dhb_ref, dg_ref, loss_ref):
        i = pl.program_id(0)

        @pl.when(i == 0)
        def _():
            dg_ref[...] = jnp.zeros_like(dg_ref)
            loss_ref[...] = jnp.zeros_like(loss_ref)

        x = h_ref[...]
        gv = g_ref[...]
        rowi = i * tm + lax.broadcasted_iota(jnp.int32, (tm, 1), 0)
        real = jnp.logical_and(rowi >= N_META, rowi < N_META + n_real)
        err = jnp.where(real, x * _rstd(x) * gv - t_ref[...], 0.0)
        loss_ref[...] += 0.5 * jnp.sum(jnp.mean(err * err, axis=-1, keepdims=True))
        dx, dgp = _rms_bwd(err * (1.0 / d), x, gv)
        dh_ref[...] = dx
        dhb_ref[...] = dx.astype(BF16)
        dg_ref[...] += jnp.broadcast_to(dgp, (SUBLANES, d))

    row = pl.BlockSpec((tm, d), lambda i: (i, 0))
    return _call(body, (h, g, target), name="loss_bwd", grid=(tp // tm,),
                 in_specs=[row, _full((1, d)), row],
                 out_specs=[row, row, _full((SUBLANES, d)), _full((SUBLANES, LANES))],
                 out_shape=[jax.ShapeDtypeStruct((tp, d), F32), jax.ShapeDtypeStruct((tp, d), BF16),
                            jax.ShapeDtypeStruct((SUBLANES, d), F32), jax.ShapeDtypeStruct((SUBLANES, LANES), F32)],
                 semantics=("arbitrary",), vmem_bytes=16 * _nbytes((tm, d), F32))


def _mlp_bwd(dh_b, w_down, up, z2):
    tp, d = dh_b.shape
    fc = w_down.shape[1]
    ff = N_CHIPS * fc
    tn = _divisor_tile(fc, LANES, 512)
    per = fc // tn

    def body(dh_ref, z_ref, w_ref, up_ref, dup_ref, gd_ref, gu_ref):
        dh = dh_ref[...]
        r = jnp.maximum(up_ref[...].astype(F32), 0.0)
        dup = (_dot_nt(dh, w_ref[...]) * (2.0 * r)).astype(BF16)
        dup_ref[...] = dup
        gd_ref[...] = _dot_tn((r * r).astype(BF16), dh).astype(BF16)
        gu_ref[...] = _dot_tn(z_ref[...], dup).astype(BF16)

    col = pl.BlockSpec((tp, tn), lambda j: (0, j))
    vm = 4 * _nbytes((tp, d), BF16) + 4 * _nbytes((tn, d), BF16) + 2 * _nbytes((d, tn), BF16) \
        + 10 * _nbytes((tp, tn), F32) + 4 * _nbytes((tn, d), F32)
    return _call(body, (dh_b, z2, w_down, up), name="mlp_bwd", grid=(ff // tn,),
                 in_specs=[_full((tp, d)), _full((tp, d)),
                           pl.BlockSpec((None, tn, d), lambda j: (j // per, j % per, 0)), col],
                 out_specs=[col, pl.BlockSpec((tn, d), lambda j: (j, 0)),
                            pl.BlockSpec((None, d, tn), lambda j: (j // per, 0, j % per))],
                 out_shape=[jax.ShapeDtypeStruct((tp, ff), BF16), jax.ShapeDtypeStruct((ff, d), BF16),
                            jax.ShapeDtypeStruct((N_CHIPS, d, fc), BF16)],
                 semantics=("parallel",), vmem_bytes=vm)


def _grad_w_pieces(pieces, b):
    tp, n = b.shape
    tn = _divisor_tile(n, LANES, 512)
    widths = [pc.shape[1] for pc in pieces]

    def body(*refs):
        p_refs, b_ref, o_refs = refs[:len(pieces)], refs[len(pieces)], refs[len(pieces) + 1:]
        for p_ref, o_ref in zip(p_refs, o_refs):
            o_ref[...] = _dot_tn(p_ref[...], b_ref[...]).astype(BF16)

    vm = 2 * sum(_nbytes((tp, wd), BF16) for wd in widths) + 2 * _nbytes((tp, tn), BF16) \
        + 4 * sum(_nbytes((wd, tn), F32) for wd in widths) + 2 * _nbytes((tp, max(widths)), F32)
    return _call(body, tuple(pieces) + (b,), name="grad_w_pieces", grid=(n // tn,),
                 in_specs=[_full(pc.shape) for pc in pieces] + [pl.BlockSpec((tp, tn), lambda j: (0, j))],
                 out_specs=[pl.BlockSpec((wd, tn), lambda j: (0, j)) for wd in widths],
                 out_shape=[jax.ShapeDtypeStruct((wd, n), BF16) for wd in widths],
                 semantics=("parallel",), vmem_bytes=vm)


def _dx_norm_bwd(pieces, w, w_spec, w_piece, h, g, dres, dot=_dot_nt):
    tp, d = h.shape
    tm = _divisor_tile(tp, 16, ROW_TARGET)
    n = len(pieces)

    def body(*refs):
        dy_refs = refs[:n]
        w_ref, h_ref, g_ref, dres_ref, dh_ref, dhb_ref, dg_ref = refs[n:]

        @pl.when(pl.program_id(0) == 0)
        def _():
            dg_ref[...] = jnp.zeros_like(dg_ref)

        dz = dot(dy_refs[0][...], w_piece(w_ref, 0))
        for i in range(1, n):
            dz = dz + dot(dy_refs[i][...], w_piece(w_ref, i))
        dx, dgp = _rms_bwd(dz, h_ref[...], g_ref[...])
        dh = dres_ref[...] + dx
        dh_ref[...] = dh
        dhb_ref[...] = dh.astype(BF16)
        dg_ref[...] += jnp.broadcast_to(dgp, (SUBLANES, d))

    row = lambda wd: pl.BlockSpec((tm, wd), lambda i: (i, 0))
    kk = sum(wd for _, _, wd in pieces)
    vm = 2 * _nbytes((d, kk), BF16) + 2 * _nbytes((tm, kk), BF16) + 14 * _nbytes((tm, d), F32)
    piece_specs = [pl.BlockSpec((tm, wd), functools.partial(lambda i, cb: (i, cb), cb=cb)) for _, cb, wd in pieces]
    return _call(body, tuple(a for a, _, _ in pieces) + (w, h, g, dres), name="dx_norm_bwd", grid=(tp // tm,),
                 in_specs=piece_specs + [w_spec, row(d), _full((1, d)), row(d)],
                 out_specs=[row(d), row(d), _full((SUBLANES, d))],
                 out_shape=[jax.ShapeDtypeStruct((tp, d), F32), jax.ShapeDtypeStruct((tp, d), BF16),
                            jax.ShapeDtypeStruct((SUBLANES, d), F32)],
                 semantics=("arbitrary",), vmem_bytes=vm)


def _block_diag(wg):
    nb, b, _ = wg.shape
    eye = jnp.eye(nb, dtype=wg.dtype)
    return (eye[:, None, :, None] * wg[:, :, None, :]).reshape(nb * b, nb * b)


def _diag_blocks(dense, nb):
    b = dense.shape[0] // nb
    d4 = dense.reshape(nb, b, nb, b)
    return jnp.stack([d4[i, :, i, :] for i in range(nb)])


def _row(v):
    return v.reshape(1, -1)


def _forward_layer(l, h, z, p, fetch, stage_next, g_next):
    d = h.shape[1]
    att_w = d // 2
    rec_w = d - att_w
    nh = att_w // HEAD_DIM
    wa_d = _block_diag(p["w_gate_a"][l]).astype(BF16)
    wx_d = _block_diag(p["w_gate_x"][l]).astype(BF16)
    b_f_pad = jnp.zeros((1, LANES), F32).at[0, :nh].set(p["b_f"][l])
    w_in_t = fetch("w_in", z)
    big = dict(w_in_big=_pack_w_in_t(w_in_t.reshape(-1, d), att_w, nh))
    qkv, proj = _proj(z, big["w_in_big"], att_w)
    c, c_t = _fgate_fwd(proj, b_f_pad, nh)
    attn, lse_b = _attn_fwd(qkv, c, c_t, nh)
    hr, rec = _rec_fwd(proj, 0, 1, rec_w, p["conv_w"][l], _row(p["conv_b"][l]), wa_d,
                       _row(p["b_gate_a"][l]), wx_d, _row(p["b_gate_x"][l]), _row(p["lru_L"][l]))
    tok = stage_next(attn, "own")
    big["w_out"] = fetch("w_out", rec)
    h1, z2, mix = _mixer_out(attn, rec, _row(p["attn_out_g"][l] + tok), _row(p["rec_out_g"][l]),
                             big["w_out"], h, _row(p["mlp_norm_g"][l]))
    big["w_up"] = fetch("w_up", h1)
    act, up = _mlp_up(z2, big["w_up"])
    tok = stage_next(act, "next")
    big["w_down"] = fetch("w_down", act)
    h2, z_next = _mlp_down(act, big["w_down"], h1, _row(g_next + tok))
    saved = dict(h0=h, z1=z, proj=proj, qkv=qkv, c=c, c_t=c_t, attn=attn, lse_b=lse_b, hr=hr, rec=rec, h1=h1,
                 z2=z2, mix=mix, up=up, wa_d=wa_d, wx_d=wx_d, b_f_pad=b_f_pad, big=big)
    return h2, z_next, saved


def _backward_mlp(l, dh, dh_b, sv, p, tok):
    w_up, w_down = sv["big"]["w_up"], sv["big"]["w_down"]
    fc = w_up.shape[2]
    dup, g_down, g_up = _mlp_bwd(dh_b, w_down, sv["up"], sv["z2"])
    dh, dh_b, dg2 = _dx_norm_bwd([(dup, j, fc) for j in range(N_CHIPS)], w_up, _full(w_up.shape),
                                 lambda w_ref, j: w_ref[j], sv["h1"], _row(p["mlp_norm_g"][l] + tok), dh)
    big = dict(w_down=g_down.reshape((N_CHIPS, -1) + g_down.shape[1:]), w_up=g_up)
    return dh, dh_b, big, dict(mlp_norm_g=dg2[0])


def _backward_mixer(l, dh, dh_b, sv, p, tok):
    d = dh.shape[1]
    att_w = d // 2
    rec_w = d - att_w
    nh = att_w // HEAD_DIM
    small = {}
    g_out, = _grad_w_pieces([sv["mix"]], dh_b)
    dattn, drec, dg_mix = _mixer_bwd(dh_b, sv["big"]["w_out"], sv["attn"], sv["rec"],
                                     _row(p["attn_out_g"][l] + tok), _row(p["rec_out_g"][l]))
    small["attn_out_g"] = dg_mix[0, :att_w]
    small["rec_out_g"] = dg_mix[0, att_w:]
    dxr, dyr, dwa, dwx, sm = _rec_bwd(
        sv["proj"], 0, 1, rec_w, sv["hr"], drec, p["conv_w"][l], _row(p["conv_b"][l]), sv["wa_d"],
        _row(p["b_gate_a"][l]), sv["wx_d"], _row(p["b_gate_x"][l]), _row(p["lru_L"][l]))
    small.update(conv_w=sm[:CONV_WIDTH], conv_b=sm[4], b_gate_a=sm[5], b_gate_x=sm[6], lru_L=sm[7],
                 w_gate_a=_diag_blocks(dwa, N_REC_BLOCKS), w_gate_x=_diag_blocks(dwx, N_REC_BLOCKS))
    dq, dk, dv, dc = _attn_bwd(sv["qkv"], sv["c"], sv["c_t"], sv["lse_b"], dattn, nh)
    df, db_f = _fgate_bwd(sv["proj"], sv["b_f_pad"], dc)
    small["b_f"] = db_f[0, :nh]
    pieces = [dq, dk, dv, dxr, dyr, df]
    offs = [0, att_w, 2 * att_w, 3 * att_w, 3 * att_w + rec_w, 3 * att_w + 2 * rec_w]
    gq, gk, gv, gxr, gyr, gf = _grad_w_pieces(pieces, sv["z1"])
    g_in_t = jnp.concatenate([gq, gk, gv, gf[:nh], gxr, gyr], axis=0)
    w_big = sv["big"]["w_in_big"]
    widths = [pc.shape[1] for pc in pieces]
    dh, dh_b, dg1 = _dx_norm_bwd(
        [(pc, 0, wd) for pc, wd in zip(pieces, widths)], w_big, _full(w_big.shape),
        lambda w_ref, i: w_ref[offs[i]:offs[i] + widths[i], :], sv["h0"], _row(p["attn_norm_g"][l]), dh, dot=_dot)
    small["attn_norm_g"] = dg1[0]
    big = dict(w_in=g_in_t.reshape(N_CHIPS, -1, d), w_out=g_out.reshape((N_CHIPS, -1) + g_out.shape[1:]))
    return dh, dh_b, big, small


def _pack_w_in_t(w_in_t, att_w, nh):
    qkv = w_in_t[:3 * att_w]
    f = w_in_t[3 * att_w:3 * att_w + nh]
    xy = w_in_t[3 * att_w + nh:]
    return jnp.concatenate([qkv, xy, f, jnp.zeros((LANES - nh, w_in_t.shape[1]), w_in_t.dtype)], axis=0)


ANY = pl.BlockSpec(memory_space=pl.ANY)


def _coords():
    return lax.axis_index("x"), lax.axis_index("y"), lax.axis_index("c")


def _other_chips(x, y):
    return [(1 - x, y), (x, 1 - y), (1 - x, 1 - y)]


def _remote(src, dst, send_sems, recv_sems, k, to):
    return pltpu.make_async_remote_copy(src_ref=src, dst_ref=dst, send_sem=send_sems.at[k],
                                        recv_sem=recv_sems.at[k], device_id=to, device_id_type=MESH)


def _all_gather_chips(shards):
    n = len(shards)
    per = 6

    def body(*refs):
        ins, outs = refs[:n], refs[n:2 * n]
        send_sems, recv_sems, local_sems = refs[2 * n:]
        x, y, c = _coords()
        me = 2 * x + y
        sibling = (x, y, 1 - c)
        chips = _other_chips(x, y)
        local = [pltpu.make_async_copy(ins[t], outs[t].at[me], local_sems.at[t]) for t in range(n)]
        for cp in local:
            cp.start()
        sends = []
        for t in range(n):
            for j, (px, py) in enumerate(chips):
                cp = _remote(ins[t].at[c], outs[t].at[me, c], send_sems, recv_sems, per * t + j, (px, py, c))
                cp.start()
                sends.append(cp)
        for t in range(n):
            for j, (px, py) in enumerate(chips):
                landed = outs[t].at[2 * px + py, c]
                _remote(landed, landed, send_sems, recv_sems, per * t + j, (px, py, c)).wait_recv()
                cp = _remote(landed, landed, send_sems, recv_sems, per * t + 3 + j, sibling)
                cp.start()
                sends.append(cp)
        for t in range(n):
            for j, (px, py) in enumerate(chips):
                passed = outs[t].at[2 * px + py, 1 - c]
                _remote(passed, passed, send_sems, recv_sems, per * t + 3 + j, sibling).wait_recv()
        for cp in sends:
            cp.wait_send()
        for cp in local:
            cp.wait()

    return _call(body, tuple(shards), name="all_gather_chips",
                 in_specs=[ANY] * n, out_specs=[ANY] * n,
                 out_shape=[jax.ShapeDtypeStruct((N_CHIPS,) + s.shape, s.dtype) for s in shards],
                 scratch_shapes=[pltpu.SemaphoreType.DMA((per * n,)), pltpu.SemaphoreType.DMA((per * n,)),
                                 pltpu.SemaphoreType.DMA((n,))])


HBM = pl.BlockSpec(memory_space=pltpu.HBM)
SEM = pl.BlockSpec(memory_space=pltpu.SEMAPHORE)
DATAFLOW = pltpu.SideEffectType.DATAFLOW_SIDE_EFFECTING


def _in_hbm(a):
    return pltpu.with_memory_space_constraint(a, pltpu.HBM)


PUSH_ARRIVALS = {"gather_chips_half": N_CHIPS - 1, "pass_halves": N_CHIPS - 1, "scatter_chips": N_CHIPS - 1,
                 "sibling": 1, "gather_devices": N_DEV - 1}


def _column_half(ref3, slab, c):
    hw = ref3.shape[2] // 2
    return ref3.at[slab, :, pl.ds(pl.multiple_of(c * hw, LANES), hw)]


def _push_copies(mode, src, land, send_sems, recv_sems, t):
    x, y, c = _coords()
    chip = 2 * x + y
    if mode == "gather_chips_half":
        return [_remote(_column_half(src, chip, c), _column_half(land, chip, c), send_sems, recv_sems, t, (px, py, c))
                for px, py in _other_chips(x, y)]
    if mode == "pass_halves":
        return [_remote(_column_half(src, 2 * px + py, c), _column_half(land, 2 * px + py, c), send_sems, recv_sems, t,
                        (x, y, 1 - c)) for px, py in _other_chips(x, y)]
    if mode == "scatter_chips":
        return [_remote(src.at[2 * px + py], land.at[chip], send_sems, recv_sems, t, (px, py, c))
                for px, py in _other_chips(x, y)]
    if mode == "sibling":
        return [_remote(src, land, send_sems, recv_sems, t, (x, y, 1 - c))]
    dev = 4 * x + 2 * y + c
    return [_remote(src.at[dev], land.at[dev], send_sems, recv_sems, t, (x ^ (k >> 2), y ^ ((k >> 1) & 1), c ^ (k & 1)))
            for k in range(1, N_DEV)]


def _push_start(srcs, lands, mode, name):
    n = len(srcs)
    same = all(s is ld for s, ld in zip(srcs, lands))
    n_in = n if same else 2 * n

    def body(*refs):
        src_refs = refs[:n]
        land_refs = src_refs if same else refs[n:2 * n]
        send_sems, recv_sems = refs[n_in], refs[n_in + 1]
        token = refs[-1]
        for t in range(n):
            for cp in _push_copies(mode, src_refs[t], land_refs[t], send_sems, recv_sems, t):
                cp.start()
        token[...] = jnp.zeros_like(token)

    operands = tuple(srcs) if same else tuple(srcs) + tuple(lands)
    res = _call(
        body, [_in_hbm(a) for a in operands], name=name,
        out_shape=(pltpu.SemaphoreType.DMA((n,)), pltpu.SemaphoreType.DMA((n,)))
        + tuple(pltpu.HBM(a.shape, a.dtype) for a in operands) + (jax.ShapeDtypeStruct((SUBLANES, LANES), F32),),
        in_specs=[HBM] * n_in, out_specs=(SEM, SEM) + (HBM,) * n_in + (pl.BlockSpec(memory_space=pltpu.VMEM),),
        input_output_aliases={i: 2 + i for i in range(n_in)}, side_effects=DATAFLOW, hbm_results=False)
    send_sems, recv_sems, token = res[0], res[1], res[-1]
    srcs_thru = res[2:2 + n]
    lands_thru = srcs_thru if same else res[2 + n:2 + 2 * n]
    return send_sems, recv_sems, srcs_thru, lands_thru, token


def _push_wait(send_sems, recv_sems, ids, srcs, lands, mode, after, name):
    n = len(lands)
    same = all(s is ld for s, ld in zip(srcs, lands))
    n_in = n if same else 2 * n

    def body(*refs):
        land_refs = refs[:n] if same else refs[n:2 * n]
        send_sems, recv_sems = refs[n_in], refs[n_in + 1]
        x, y, c = _coords()
        for t in range(n):
            if mode == "sibling":
                moved = land_refs[t]
            elif mode in ("gather_chips_half", "pass_halves"):
                moved = land_refs[t].at[pl.ds(0, PUSH_ARRIVALS[mode]), :, pl.ds(0, land_refs[t].shape[2] // 2)]
            else:
                moved = land_refs[t].at[pl.ds(0, PUSH_ARRIVALS[mode])]
            arrivals = _remote(moved, moved, send_sems, recv_sems, ids[t], (x, y, c))
            arrivals.wait_send()
            arrivals.wait_recv()

    operands = tuple(lands) if same else tuple(srcs) + tuple(lands)
    res = _call(
        body, operands + (send_sems, recv_sems, after), name=name,
        out_shape=tuple(pltpu.HBM(a.shape, a.dtype) for a in operands),
        in_specs=[HBM] * n_in + [SEM, SEM, ANY], out_specs=(HBM,) * n_in,
        input_output_aliases={i: i for i in range(n_in)}, side_effects=DATAFLOW)
    return list(res) if same else (list(res[:n]), list(res[n:]))


def _sum_partials(part, landed, chip):
    _, rows, cols = part.shape
    br = _divisor_tile(rows, 16, ELEM_ROWS)

    def body(chip_ref, own_ref, a_ref, b_ref, c_ref, o_ref):
        o_ref[...] = ((own_ref[...].astype(F32) + a_ref[...].astype(F32)) + b_ref[...].astype(F32)) \
            + c_ref[...].astype(F32)

    def other(k):
        return pl.BlockSpec((None, br, cols), lambda i, ch: (jnp.where(ch[0] <= k, k + 1, k), i, 0))

    spec = pltpu.PrefetchScalarGridSpec(
        num_scalar_prefetch=1, grid=(rows // br,),
        in_specs=[pl.BlockSpec((None, br, cols), lambda i, ch: (ch[0], i, 0)), other(0), other(1), other(2)],
        out_specs=pl.BlockSpec((br, cols), lambda i, ch: (i, 0)))
    return _call(body, (chip, part, landed, landed, landed), name="sum_partials", grid_spec=spec,
                 out_shape=jax.ShapeDtypeStruct((rows, cols), F32), semantics=("parallel",))


def _cast_to_slab(w, l, chip, after):
    _, rows, cols = w.shape
    br = _divisor_tile(rows, 16, ELEM_ROWS)

    def body(chip_ref, w_ref, after_ref, o_ref):
        o_ref[...] = w_ref[...].astype(BF16)

    spec = pltpu.PrefetchScalarGridSpec(
        num_scalar_prefetch=1, grid=(rows // br,),
        in_specs=[pl.BlockSpec((None, br, cols), lambda i, ch: (l, i, 0)), ANY],
        out_specs=pl.BlockSpec((None, br, cols), lambda i, ch: (ch[0], i, 0)))
    return _call(body, (chip, w, after), name="cast_to_slab", grid_spec=spec,
                 out_shape=jax.ShapeDtypeStruct((N_CHIPS, rows, cols), BF16), semantics=("parallel",))


def _cast_w_in_t_to_slabs(w_t, chip):
    rows, depth, d = w_t.shape
    tn = _divisor_tile(d, LANES, 256)

    def body(chip_ref, w_ref, *o_refs):
        for l in range(depth):
            o_refs[l][...] = w_ref[:, l, :].astype(BF16)

    spec = pltpu.PrefetchScalarGridSpec(
        num_scalar_prefetch=1, grid=(d // tn,),
        in_specs=[pl.BlockSpec((rows, depth, tn), lambda j, ch: (0, 0, j))],
        out_specs=[pl.BlockSpec((None, rows, tn), lambda j, ch: (ch[0], 0, j))] * depth)
    return _call(body, (chip, w_t), name="cast_w_in_t_to_slabs", grid_spec=spec,
                 out_shape=[jax.ShapeDtypeStruct((N_CHIPS, rows, d), BF16)] * depth, semantics=("parallel",),
                 vmem_bytes=4 * _nbytes((rows, max(depth, SUBLANES), tn), F32))


def _place_slab(buf, index, n_slabs):
    rows, cols = buf.shape
    br = _divisor_tile(rows, SUBLANES, ELEM_ROWS)

    def body(index_ref, b_ref, o_ref):
        o_ref[...] = b_ref[...]

    spec = pltpu.PrefetchScalarGridSpec(
        num_scalar_prefetch=1, grid=(rows // br,),
        in_specs=[pl.BlockSpec((br, cols), lambda i, ix: (i, 0))],
        out_specs=pl.BlockSpec((None, br, cols), lambda i, ix: (ix[0], i, 0)))
    return _call(body, (index, buf), name="place_slab", grid_spec=spec,
                 out_shape=jax.ShapeDtypeStruct((n_slabs, rows, cols), buf.dtype), semantics=("parallel",))


ELEM_ROWS = 256


def _sum_slabs(r):
    n, rows, cols = r.shape
    br = _divisor_tile(rows, 16, ELEM_ROWS)

    def body(r_ref, o_ref):
        acc = r_ref[0].astype(F32)
        for j in range(1, n):
            acc = acc + r_ref[j].astype(F32)
        o_ref[...] = acc

    return _call(body, (r,), name="sum_slabs", grid=(rows // br,),
                 in_specs=[pl.BlockSpec((n, br, cols), lambda i: (0, i, 0))],
                 out_specs=pl.BlockSpec((br, cols), lambda i: (i, 0)),
                 out_shape=jax.ShapeDtypeStruct((rows, cols), F32), semantics=("parallel",))


def _adamw_math(w, g, m, v):
    c1 = 1.0 - ADAM_B1 ** ADAM_STEP
    c2 = 1.0 - ADAM_B2 ** ADAM_STEP
    nm = ADAM_B1 * m + (1.0 - ADAM_B1) * g
    nv = ADAM_B2 * v + (1.0 - ADAM_B2) * (g * g)
    delta = -ADAM_LR * ((nm / c1) / (jnp.sqrt(nv / c2) + ADAM_EPS) + ADAM_WD * w)
    return delta, nm, nv


def _adamw(w, g, m, v):
    rows, cols = w.shape
    br = _divisor_tile(rows, 8, ELEM_ROWS)

    def body(w_ref, g_ref, m_ref, v_ref, d_ref, nm_ref, nv_ref):
        d_ref[...], nm_ref[...], nv_ref[...] = _adamw_math(w_ref[...], g_ref[...], m_ref[...], v_ref[...])

    blk = pl.BlockSpec((br, cols), lambda i: (i, 0))
    return _call(body, (w, g, m, v), name="adamw", grid=(rows // br,),
                 in_specs=[blk] * 4, out_specs=[blk] * 3,
                 out_shape=[jax.ShapeDtypeStruct((rows, cols), F32)] * 3, semantics=("parallel",))


def _adamw_w_in_t(w_t, m_t, v_t, g_mine, g_theirs):
    rows, depth, d = w_t.shape
    tn = LANES

    def body(w_ref, m_ref, v_ref, ga_ref, gb_ref, g_ref, d_ref, nm_ref, nv_ref):
        g = ga_ref[...] + gb_ref[...]
        g_ref[...] = g
        d_ref[...], nm_ref[...], nv_ref[...] = _adamw_math(w_ref[...], g, m_ref[...], v_ref[...])

    slab = pl.BlockSpec((rows, depth, tn), lambda j: (0, 0, j))
    return _call(body, (w_t, m_t, v_t, g_mine, g_theirs), name="adamw_w_in_t", grid=(d // tn,),
                 in_specs=[slab] * 5, out_specs=[slab] * 4,
                 out_shape=[jax.ShapeDtypeStruct(w_t.shape, F32)] * 4, semantics=("parallel",),
                 vmem_bytes=2 * 9 * _nbytes((rows, max(depth, SUBLANES), tn), F32))


def _adamw_layer(w, m, v, l, g_mine, g_theirs, prev, after):
    _, rows, cols = w.shape
    br = _divisor_tile(rows, 8, ELEM_ROWS)

    def body(w_ref, m_ref, v_ref, ga_ref, gb_ref, *rest):
        g_ref, d_ref, nm_ref, nv_ref = rest[5:]
        g = ga_ref[...] + gb_ref[...]
        g_ref[...] = g
        d_ref[...], nm_ref[...], nv_ref[...] = _adamw_math(w_ref[...], g, m_ref[...], v_ref[...])

    slot = pl.BlockSpec((None, br, cols), lambda i: (l, i, 0))
    blk = pl.BlockSpec((br, cols), lambda i: (i, 0))
    return _call(body, (w, m, v, g_mine, g_theirs) + tuple(prev) + (after,), name="adamw_layer",
                 grid=(rows // br,), in_specs=[slot] * 3 + [blk] * 2 + [ANY] * 5, out_specs=[slot] * 4,
                 out_shape=[jax.ShapeDtypeStruct(w.shape, F32)] * 4,
                 input_output_aliases={5: 0, 6: 1, 7: 2, 8: 3}, semantics=("parallel",))


BIG = ("w_in", "w_out", "w_up", "w_down")
WEIGHTS = ("meta", "attn_norm_g", "w_in", "b_f", "conv_w", "conv_b", "w_gate_a", "b_gate_a", "w_gate_x",
           "b_gate_x", "lru_L", "attn_out_g", "rec_out_g", "w_out", "mlp_norm_g", "w_up", "w_down", "final_g")
SMALL = tuple(k for k in WEIGHTS if k not in BIG)
COL_SHARDED_SMALL = ("meta", "conv_w")


def _packed_rows(shape):
    return -(-math.prod(shape) // (SUBLANES * LANES)) * SUBLANES


def _pack(arrs):
    rows = []
    for a in arrs:
        flat = a.reshape(-1)
        rows.append(jnp.pad(flat, (0, _packed_rows(a.shape) * LANES - flat.shape[0])).reshape(-1, LANES))
    used = sum(r.shape[0] for r in rows)
    rows.append(jnp.zeros((-used % ELEM_ROWS, LANES), F32))
    return jnp.concatenate(rows, axis=0)


def _unpack(buf, shapes):
    out, r0 = [], 0
    for s in shapes:
        nr = _packed_rows(s)
        out.append(buf[r0:r0 + nr].reshape(-1)[:math.prod(s)].reshape(s))
        r0 += nr
    return out


def _halves(a):
    return a.reshape((2, a.shape[0] // 2) + a.shape[1:])


def _cols_from_chips(g):
    return jnp.moveaxis(g, 0, -2).reshape(g.shape[1:-1] + (N_CHIPS * g.shape[-1],))


def kernel(x, meta, attn_norm_g, w_in, b_f, conv_w, conv_b, w_gate_a, b_gate_a, w_gate_x, b_gate_x, lru_L, attn_out_g, rec_out_g, w_out, mlp_norm_g, w_up, w_down, final_g, loss_target, m_meta, m_attn_norm_g, m_w_in, m_b_f, m_conv_w, m_conv_b, m_w_gate_a, m_b_gate_a, m_w_gate_x, m_b_gate_x, m_lru_L, m_attn_out_g, m_rec_out_g, m_w_out, m_mlp_norm_g, m_w_up, m_w_down, m_final_g, v_meta, v_attn_norm_g, v_w_in, v_b_f, v_conv_w, v_conv_b, v_w_gate_a, v_b_gate_a, v_w_gate_x, v_b_gate_x, v_lru_L, v_attn_out_g, v_rec_out_g, v_w_out, v_mlp_norm_g, v_w_up, v_w_down, v_final_g):
    w = dict(meta=meta, attn_norm_g=attn_norm_g, w_in=w_in, b_f=b_f, conv_w=conv_w, conv_b=conv_b,
             w_gate_a=w_gate_a, b_gate_a=b_gate_a, w_gate_x=w_gate_x, b_gate_x=b_gate_x, lru_L=lru_L,
             attn_out_g=attn_out_g, rec_out_g=rec_out_g, w_out=w_out, mlp_norm_g=mlp_norm_g, w_up=w_up,
             w_down=w_down, final_g=final_g)
    m = dict(meta=m_meta, attn_norm_g=m_attn_norm_g, w_in=m_w_in, b_f=m_b_f, conv_w=m_conv_w, conv_b=m_conv_b,
             w_gate_a=m_w_gate_a, b_gate_a=m_b_gate_a, w_gate_x=m_w_gate_x, b_gate_x=m_b_gate_x, lru_L=m_lru_L,
             attn_out_g=m_attn_out_g, rec_out_g=m_rec_out_g, w_out=m_w_out, mlp_norm_g=m_mlp_norm_g,
             w_up=m_w_up, w_down=m_w_down, final_g=m_final_g)
    v = dict(meta=v_meta, attn_norm_g=v_attn_norm_g, w_in=v_w_in, b_f=v_b_f, conv_w=v_conv_w, conv_b=v_conv_b,
             w_gate_a=v_w_gate_a, b_gate_a=v_b_gate_a, w_gate_x=v_w_gate_x, b_gate_x=v_b_gate_x, lru_L=v_lru_L,
             attn_out_g=v_attn_out_g, rec_out_g=v_rec_out_g, w_out=v_w_out, mlp_norm_g=v_mlp_norm_g,
             w_up=v_w_up, w_down=v_w_down, final_g=v_final_g)
    s_len, d = x.shape[1], x.shape[2]
    depth = w_in.shape[0]
    att_w = d // 2
    rec_w = d - att_w
    nh = att_w // HEAD_DIM
    chip = 2 * lax.axis_index("x") + lax.axis_index("y")

    g_conv, g_meta = [g.reshape((N_CHIPS, g.shape[1] * g.shape[2]) + g.shape[3:])
                      for g in _all_gather_chips([_halves(w["conv_w"]), _halves(w["meta"])])]
    p = dict(w)
    p["conv_w"] = _cols_from_chips(g_conv)
    meta_full = jnp.moveaxis(g_meta, 0, 1).reshape(N_META, d)

    chip1 = chip.reshape(1).astype(jnp.int32)
    w_in_t, m_in_t, v_in_t = [jnp.transpose(a["w_in"], (2, 0, 1)) for a in (w, m, v)]
    w_in_slabs = _cast_w_in_t_to_slabs(w_in_t, chip1)
    pushes, tokens = [], []
    first = g_meta
    for l in range(depth):
        slabs = [w_in_slabs[l]] + [_cast_to_slab(w[k], l, chip1, first) for k in BIG[1:]]
        send_sems, recv_sems, _, lands, token = _push_start(slabs, slabs, "gather_chips_half", f"weights_start_{l}")
        pushes.append((send_sems, recv_sems, lands))
        tokens.append(token[0, 0])
        first = token if l == 0 else first
    passed = [{} for _ in range(depth)]

    def stage(l, after, ids):
        send_sems, recv_sems, lands = pushes[l]
        tag = "_".join(BIG[i] for i in ids)
        sub = [lands[i] for i in ids]
        sub = _push_wait(send_sems, recv_sems, ids, sub, sub, "gather_chips_half", after, f"{tag}_wait_{l}")
        send_sems, recv_sems, _, sub, token = _push_start(sub, sub, "pass_halves", f"{tag}_pass_{l}")
        for j, i in enumerate(ids):
            passed[l][i] = (send_sems, recv_sems, sub[j], j)
        return token[0, 0]

    t_len = N_META + s_len
    pad = -t_len % SEQ_TILE
    h = jnp.concatenate([meta_full, x[0], jnp.zeros((pad, d), F32)], axis=0)
    tgt = jnp.concatenate([jnp.zeros((N_META, d), F32), loss_target[0], jnp.zeros((pad, d), F32)], axis=0)
    z = _rms_fwd(h, _row(p["attn_norm_g"][0] + sum(tokens)))
    stage(0, z, [0])
    saved = []
    for l in range(depth):
        def fetch(k, after, l=l):
            send_sems, recv_sems, land, j = passed[l][BIG.index(k)]
            return _push_wait(send_sems, recv_sems, [j], [land], [land], "pass_halves", after,
                              f"{k}_here_{l}")[0]

        def stage_next(after, which, l=l):
            if which == "own":
                return stage(l, after, [1, 2, 3])
            return stage(l + 1, after, [0]) if l + 1 < depth else 0.0

        g_next = p["attn_norm_g"][l + 1] if l + 1 < depth else p["final_g"]
        h, z, sv = _forward_layer(l, h, z, p, fetch, stage_next, g_next)
        saved.append(sv)
    dh, dh_b, dg_final, loss_part = _loss_bwd(h, _row(p["final_g"]), tgt, s_len)

    small = {k: [None] * depth for k in SMALL if k not in ("meta", "final_g")}
    pushes = [None] * depth
    tok = 0.0
    for l in reversed(range(depth)):
        dh, dh_b, big_mlp, sm_mlp = _backward_mlp(l, dh, dh_b, saved[l], p, tok)
        parts = [big_mlp["w_down"], big_mlp["w_up"]]
        push_mlp = _push_start(parts, [lax.empty(a.shape, a.dtype) for a in parts], "scatter_chips",
                               f"mlp_grads_start_{l}")
        dh, dh_b, big_mix, sm_mix = _backward_mixer(l, dh, dh_b, saved[l], p, push_mlp[4][0, 0])
        parts = [big_mix["w_out"], big_mix["w_in"]]
        push_mix = _push_start(parts, [lax.empty(a.shape, a.dtype) for a in parts], "scatter_chips",
                               f"mixer_grads_start_{l}")
        tok = push_mix[4][0, 0]
        pushes[l] = {("w_down", "w_up"): push_mlp, ("w_out", "w_in"): push_mix}
        for k, val in {**sm_mlp, **sm_mix}.items():
            small[k][l] = val
    grads = {k: jnp.stack(val) for k, val in small.items()}
    grads["final_g"] = dg_final[0]
    grads["meta"] = dh[:N_META]
    dx = dh[N_META:t_len]

    full_shapes = [grads[k].shape for k in SMALL] + [(1,)]
    packed = _pack([grads[k].astype(F32) for k in SMALL] + [loss_part[0, :1] + tok])
    dev1 = (2 * chip + lax.axis_index("c")).reshape(1).astype(jnp.int32)
    slabs = [_place_slab(packed, dev1, N_DEV)]
    small_push = _push_start(slabs, slabs, "gather_devices", "small_grads_start")

    last_token = small_push[4]
    outs = {k: [lax.empty(w[k].shape, F32) for _ in range(4)] for k in BIG[1:]}
    w_in_sums = [None] * depth
    swaps = {}

    def finish(l, wait_after, adam_after):
        send_sems, recv_sems, mine, lands, _ = swaps[l]
        mine, theirs = _push_wait(send_sems, recv_sems, list(range(len(BIG))), mine, lands, "sibling", wait_after,
                                  f"sums_wait_{l}")
        w_in_sums[l] = (mine[0], theirs[0])
        for k, a, b in zip(BIG[1:], mine[1:], theirs[1:]):
            outs[k] = _adamw_layer(w[k], m[k], v[k], l, a, b, outs[k], adam_after)

    for l in reversed(range(depth)):
        sums = {}
        for names, (send_sems, recv_sems, parts, lands, _) in pushes[l].items():
            parts, landed = _push_wait(send_sems, recv_sems, [0, 1], parts, lands, "scatter_chips", last_token,
                                       f"{names[0]}_grads_wait_{l}")
            for k, part, land in zip(names, parts, landed):
                sums[k] = _sum_partials(part, land, chip1)
        mine = [sums[k] for k in BIG]
        swaps[l] = _push_start(mine, [lax.empty(a.shape, a.dtype) for a in mine], "sibling", f"sums_start_{l}")
        if l + 1 < depth:
            finish(l + 1, outs["w_down"][0] if l + 2 < depth else mine[0], swaps[l][4])
    finish(0, outs["w_down"][0] if depth > 1 else swaps[0][4], swaps[0][4])
    outs["w_in"] = [jnp.transpose(r, (1, 2, 0)) for r in _adamw_w_in_t(
        w_in_t, m_in_t, v_in_t, jnp.stack([s[0] for s in w_in_sums], axis=1),
        jnp.stack([s[1] for s in w_in_sums], axis=1))]
    out_g, out_d, out_m, out_v = [{k: outs[k][i] for k in BIG} for i in range(4)]

    landed = _push_wait(small_push[0], small_push[1], [0], small_push[3], small_push[3], "gather_devices",
                        out_g["w_in"], "small_grads_wait")
    total = _sum_slabs(landed[0])
    small_g = dict(zip(SMALL + ("loss",), _unpack(total, full_shapes)))
    for k in COL_SHARDED_SMALL:
        n = w[k].shape[-1]
        small_g[k] = lax.dynamic_slice_in_dim(small_g[k], chip * n, n, axis=small_g[k].ndim - 1)
    local_shapes = [w[k].shape for k in SMALL]
    res = _adamw(_pack([w[k] for k in SMALL]), _pack([small_g[k] for k in SMALL]),
                 _pack([m[k] for k in SMALL]), _pack([v[k] for k in SMALL]))
    out_g.update({k: small_g[k] for k in SMALL})
    for dst, buf in zip((out_d, out_m, out_v), res):
        dst.update(zip(SMALL, _unpack(buf, local_shapes)))

    return (small_g["loss"].reshape(()), dx[None],
            *[out_g[k] for k in WEIGHTS], *[out_d[k] for k in WEIGHTS],
            *[out_m[k] for k in WEIGHTS], *[out_v[k] for k in WEIGHTS])
```

```python
import functools
import math

import jax
import jax.numpy as jnp
from jax import lax
from jax.experimental import pallas as pl
from jax.experimental.pallas import tpu as pltpu

F32 = jnp.float32
BF16 = jnp.bfloat16

N_META = 16
HEAD_DIM = 64
N_REC_BLOCKS = 8
CONV_WIDTH = 4
RG_C = 8.0
NORM_EPS = 1e-6
ADAM_LR = 0.001
ADAM_B1 = 0.9
ADAM_B2 = 0.999
ADAM_EPS = 1e-08
ADAM_WD = 0.01
ADAM_STEP = 10

LANES = 128
SUBLANES = 8
SEQ_TILE = 128
VMEM_CAP = 60 * 2**20
VMEM_SLACK = 6 * 2**20
NEG_BIG = -1e30
N_CHIPS = 4
N_DEV = 8
MESH = pl.DeviceIdType.MESH


def _nbytes(shape, dtype):
    return math.prod(shape) * jnp.dtype(dtype).itemsize


def _call(body, args, *, name, out_shape, grid=(), in_specs=None, out_specs=None, scratch_shapes=(),
          grid_spec=None, semantics=None, vmem_bytes=None, side_effects=None, hbm_results=True, **kw):
    cp = {}
    if semantics is not None:
        cp["dimension_semantics"] = semantics
    if vmem_bytes is not None:
        cp["vmem_limit_bytes"] = int(min(VMEM_CAP, vmem_bytes + VMEM_SLACK))
    if side_effects is not None:
        cp["has_side_effects"] = side_effects
    if grid_spec is not None:
        kw["grid_spec"] = grid_spec
    else:
        kw.update(grid=grid, in_specs=in_specs, out_specs=out_specs, scratch_shapes=scratch_shapes)
    if hbm_results:
        out_shape = jax.tree.map(
            lambda s: pltpu.HBM(s.shape, s.dtype) if isinstance(s, jax.ShapeDtypeStruct) else s, out_shape)
    fn = pl.pallas_call(
        body, name=name, out_shape=out_shape,
        compiler_params=pltpu.CompilerParams(**cp), **kw)
    return fn(*[_in_hbm(a) if jnp.issubdtype(getattr(a, "dtype", jnp.int32), jnp.floating) else a for a in args])


def _divisor_tile(n, unit, target):
    best = None
    for t in range(unit, min(n, target) + 1, unit):
        if n % t == 0:
            best = t
    return n if best is None else best


def _sigmoid(x):
    return 1.0 / (1.0 + jnp.exp(-x))


def _log1p_unit(e):
    series = e * (1.0 - e * (0.5 - e * (1.0 / 3.0)))
    return jnp.where(e < 1e-2, series, jnp.log(1.0 + e))


def _log_sigmoid(x):
    return jnp.minimum(x, 0.0) - _log1p_unit(jnp.exp(-jnp.abs(x)))


def _one_minus_exp(x, exp_x):
    small = -x * (1.0 + x * (1.0 / 2 + x * (1.0 / 6 + x * (1.0 / 24 + x * (1.0 / 120 + x * (1.0 / 720))))))
    return jnp.where(x > -0.25, small, 1.0 - exp_x)


_GELU_K = math.sqrt(2.0 / math.pi)
_GELU_C = 0.044715


def _gelu_and_grad(y):
    th = jnp.tanh(_GELU_K * (y + _GELU_C * y * y * y))
    g = 0.5 * y * (1.0 + th)
    dg = 0.5 * (1.0 + th) + 0.5 * y * (1.0 - th * th) * _GELU_K * (1.0 + 3.0 * _GELU_C * y * y)
    return g, dg


def _rstd(x):
    return lax.rsqrt(jnp.mean(x * x, axis=-1, keepdims=True) + NORM_EPS)


def _rms_bwd(dz, x, g):
    rs = _rstd(x)
    xh = x * rs
    dgp = jnp.sum(dz * xh, axis=0, keepdims=True)
    dxh = dz * g
    dx = rs * (dxh - xh * jnp.mean(dxh * xh, axis=-1, keepdims=True))
    return dx, dgp


def _dot(a, b):
    return jnp.dot(a, b, preferred_element_type=F32)


def _dot_nt(a, b):
    return lax.dot_general(a, b, (((1,), (1,)), ((), ())), preferred_element_type=F32)


def _dot_tn(a, b):
    return lax.dot_general(a, b, (((0,), (0,)), ((), ())), preferred_element_type=F32)


def _full(shape):
    nd = len(shape)
    return pl.BlockSpec(shape, lambda *_: (0,) * nd)


def _rms_fwd(h, g):
    tp, d = h.shape
    tm = _divisor_tile(tp, 16, 544)

    def body(h_ref, g_ref, z_ref):
        x = h_ref[...]
        z_ref[...] = (x * _rstd(x) * g_ref[...]).astype(BF16)

    return _call(body, (h, g), name="rms_fwd", grid=(tp // tm,),
                 in_specs=[pl.BlockSpec((tm, d), lambda i: (i, 0)), _full((1, d))],
                 out_specs=pl.BlockSpec((tm, d), lambda i: (i, 0)),
                 out_shape=jax.ShapeDtypeStruct((tp, d), BF16), semantics=("parallel",))


def _proj(z, w_big_t, att_w):
    tp, d = z.shape
    nb = w_big_t.shape[0]
    tn = _divisor_tile(nb, LANES, 512)
    assert (3 * att_w) % tn == 0
    n_qkv = 3 * att_w // tn
    scale = 1.0 / math.sqrt(HEAD_DIM)

    def body(z_ref, w_ref, qkv_ref, p_ref):
        j = pl.program_id(0)
        acc = _dot_nt(z_ref[...], w_ref[...])

        @pl.when(j < n_qkv)
        def _():
            col = j * tn + lax.broadcasted_iota(jnp.int32, (1, tn), 1)
            qkv_ref[...] = (acc * jnp.where(col < att_w, scale, 1.0)).astype(BF16)

        @pl.when(j >= n_qkv)
        def _():
            p_ref[...] = acc

    vm = 2 * (_nbytes((tp, d), BF16) + _nbytes((d, tn), BF16) + _nbytes((tp, tn), F32) * 2)
    return _call(body, (z, w_big_t), name="proj", grid=(nb // tn,),
                 in_specs=[_full((tp, d)), pl.BlockSpec((tn, d), lambda j: (j, 0))],
                 out_specs=[pl.BlockSpec((tp, tn), lambda j: (0, jnp.minimum(j, n_qkv - 1))),
                            pl.BlockSpec((tp, tn), lambda j: (0, jnp.maximum(j - n_qkv, 0)))],
                 out_shape=[jax.ShapeDtypeStruct((tp, 3 * att_w), BF16),
                            jax.ShapeDtypeStruct((tp, nb - 3 * att_w), F32)],
                 semantics=("arbitrary",), vmem_bytes=vm)


def _tile_cumsum(x, row, reverse=False):
    for s in (1, 2, 4):
        if reverse:
            x = x + jnp.where(row < SUBLANES - s, pltpu.roll(x, SUBLANES - s, 0), 0.0)
        else:
            x = x + jnp.where(row >= s, pltpu.roll(x, s, 0), 0.0)
    return x


def _fgate_fwd(proj, b_f_pad, nh):
    tp, nb = proj.shape
    fblk = nb // LANES - 1

    def body(f_ref, b_ref, c_ref, ct_ref):
        b = b_ref[...]
        row = lax.broadcasted_iota(jnp.int32, (SUBLANES, LANES), 0)

        def step(i, carry):
            r0 = pl.multiple_of(i * SUBLANES, SUBLANES)
            lf = _log_sigmoid(f_ref[pl.ds(r0, SUBLANES), :] + b)
            x = _tile_cumsum(lf, row) + carry
            c_ref[pl.ds(r0, SUBLANES), :] = x
            return x[SUBLANES - 1:SUBLANES, :]

        lax.fori_loop(0, tp // SUBLANES, step, jnp.zeros((1, LANES), F32))
        ct_ref[...] = c_ref[...].T[:nh, :]

    return _call(body, (proj, b_f_pad), name="fgate_fwd", grid=(1,),
                 in_specs=[pl.BlockSpec((tp, LANES), lambda i: (0, fblk)), _full((1, LANES))],
                 out_specs=[_full((tp, LANES)), _full((nh, tp))],
                 out_shape=[jax.ShapeDtypeStruct((tp, LANES), F32), jax.ShapeDtypeStruct((nh, tp), F32)],
                 semantics=("arbitrary",))


def _fgate_bwd(proj, b_f_pad, dc):
    tp, nb = proj.shape
    fblk = nb // LANES - 1

    def body(f_ref, b_ref, dc_ref, df_ref, db_ref, dc_s):
        b = b_ref[...]
        row = lax.broadcasted_iota(jnp.int32, (SUBLANES, LANES), 0)
        nt = tp // SUBLANES

        def step(i, carry):
            suffix, acc = carry
            r0 = pl.multiple_of((nt - 1 - i) * SUBLANES, SUBLANES)
            dlf = _tile_cumsum(dc_ref[pl.ds(r0, SUBLANES), :], row, reverse=True) + suffix
            df = dlf * _sigmoid(-(f_ref[pl.ds(r0, SUBLANES), :] + b))
            dc_s[pl.ds(r0, SUBLANES), :] = df
            return dlf[0:1, :], acc + df

        _, acc = lax.fori_loop(0, nt, step, (jnp.zeros((1, LANES), F32), jnp.zeros((SUBLANES, LANES), F32)))
        df_ref[...] = dc_s[...].astype(BF16)
        db_ref[...] = jnp.broadcast_to(jnp.sum(acc, axis=0, keepdims=True), (SUBLANES, LANES))

    return _call(body, (proj, b_f_pad, dc), name="fgate_bwd", grid=(1,),
                 in_specs=[pl.BlockSpec((tp, LANES), lambda i: (0, fblk)), _full((1, LANES)), _full((tp, LANES))],
                 out_specs=[_full((tp, LANES)), _full((SUBLANES, LANES))],
                 out_shape=[jax.ShapeDtypeStruct((tp, LANES), BF16),
                            jax.ShapeDtypeStruct((SUBLANES, LANES), F32)],
                 scratch_shapes=[pltpu.VMEM((tp, LANES), F32)], semantics=("arbitrary",))


ATT_BQ = 128


ATT_BUCKET = 3
ATT_HEADS = 4


def _for_bucket(i, nq, fn):
    for lo in range(0, nq, ATT_BUCKET):
        hi = min(lo + ATT_BUCKET, nq)
        spans = ([(0, lo * ATT_BQ, False)] if lo else []) + [(lo * ATT_BQ, hi * ATT_BQ, True)]
        pl.when(jnp.logical_and(i >= lo, i < hi))(functools.partial(fn, spans))


def _head_column(c_blk, h):
    lane = lax.broadcasted_iota(jnp.int32, c_blk.shape, 1)
    return jnp.sum(jnp.where(lane == h, c_blk, 0.0), axis=1, keepdims=True)


def _head_columns_into(p, c_ref, ck_s):
    for hh in range(ATT_HEADS):
        ck_s[hh] = jnp.broadcast_to(_head_column(c_ref[...], ATT_HEADS * p + hh), ck_s.shape[1:])


def _pair_diag_cols(x2):
    top = lax.broadcasted_iota(jnp.int32, (LANES, ATT_BQ), 0) < HEAD_DIM
    xt = x2.astype(F32).T.astype(BF16)
    return jnp.concatenate([jnp.where(top, xt, 0), jnp.where(top, 0, xt)], axis=1)


def _pair_diag_rows(x2):
    low = lax.broadcasted_iota(jnp.int32, (ATT_BQ, LANES), 1) < HEAD_DIM
    return jnp.concatenate([jnp.where(low, x2, 0), jnp.where(low, 0, x2)], axis=0)


def _seen_keys(k0, k1, q0):
    keys = k0 + lax.broadcasted_iota(jnp.int32, (k1 - k0, ATT_BQ), 0)
    return keys <= q0 + lax.broadcasted_iota(jnp.int32, (k1 - k0, ATT_BQ), 1)


def _attn_fwd(qkv, c, c_t, nh):
    tp = qkv.shape[0]
    att_w = nh * HEAD_DIM
    ng = nh // ATT_HEADS
    gw = ATT_HEADS * HEAD_DIM
    bq = ATT_BQ
    nq = tp // bq
    pair = 2 * HEAD_DIM
    assert pair == LANES and ATT_HEADS % 2 == 0

    def body(q_ref, k_ref, v_ref, c_ref, ct_ref, o_ref, lse_ref, ck_s, vt_s):
        p = pl.program_id(0)
        i = pl.program_id(1)

        @pl.when(i == 0)
        def _():
            _head_columns_into(p, c_ref, ck_s)
            vt_s[...] = v_ref[...].astype(F32).T.astype(BF16)

        def compute(spans):
            q0 = pl.multiple_of(i * bq, bq)
            o_t, lses = [], []
            for pi in range(ATT_HEADS // 2):
                lo = pair * pi
                heads = (2 * pi, 2 * pi + 1)
                q_cols = _pair_diag_cols(q_ref[:, lo:lo + pair])
                ts = []
                for k0, k1, needs_mask in spans:
                    t2 = _dot(k_ref[k0:k1, lo:lo + pair], q_cols)
                    t_e = [t2[:, e * bq:(e + 1) * bq] - ck_s[hh, k0:k1, :] for e, hh in enumerate(heads)]
                    if needs_mask:
                        seen = _seen_keys(k0, k1, q0)
                        t_e = [jnp.where(seen, t, NEG_BIG) for t in t_e]
                    ts.append(t_e)
                ms = [functools.reduce(jnp.maximum, [jnp.max(t[e], axis=0, keepdims=True) for t in ts])
                      for e in range(2)]
                es = [[jnp.exp(t[e] - ms[e]) for e in range(2)] for t in ts]
                ls = [sum(jnp.sum(e_[e], axis=0, keepdims=True) for e_ in es) for e in range(2)]
                o2 = sum(_dot(vt_s[lo:lo + pair, k0:k1],
                              jnp.concatenate([e_[0].astype(BF16), e_[1].astype(BF16)], axis=1))
                         for e_, (k0, k1, _) in zip(es, spans))
                o_t += [o2[:HEAD_DIM, :bq] / ls[0], o2[HEAD_DIM:, bq:] / ls[1]]
                lses += [ms[e] + ct_ref[pl.ds(ATT_HEADS * p + hh, 1), :] + jnp.log(ls[e])
                         for e, hh in enumerate(heads)]
            o_ref[...] = jnp.concatenate(o_t, axis=0).T
            lse_ref[...] = jnp.concatenate(lses, axis=0)

        _for_bucket(i, nq, compute)

    blk = pl.BlockSpec((bq, gw), lambda p, i: (i, p))
    vm = 6 * _nbytes((tp, gw), BF16) + 2 * ATT_HEADS * _nbytes((tp, LANES), F32) + 2 * _nbytes((tp, LANES), F32) \
        + 8 * ATT_HEADS * _nbytes((bq, tp), F32)
    return _call(body, (qkv, qkv, qkv, c, c_t), name="attn_fwd", grid=(ng, nq),
                 in_specs=[blk,
                           pl.BlockSpec((tp, gw), lambda p, i: (0, ng + p)),
                           pl.BlockSpec((tp, gw), lambda p, i: (0, 2 * ng + p)),
                           _full((tp, LANES)), pl.BlockSpec((nh, bq), lambda p, i: (0, i))],
                 out_specs=[blk, pl.BlockSpec((None, ATT_HEADS, bq), lambda p, i: (p, 0, i))],
                 out_shape=[jax.ShapeDtypeStruct((tp, att_w), F32), jax.ShapeDtypeStruct((ng, ATT_HEADS, tp), F32)],
                 scratch_shapes=[pltpu.VMEM((ATT_HEADS, tp, LANES), F32), pltpu.VMEM((gw, tp), BF16)],
                 semantics=("arbitrary", "arbitrary"), vmem_bytes=vm)


def _attn_bwd(qkv, c, c_t, lse, do, nh):
    tp = qkv.shape[0]
    att_w = nh * HEAD_DIM
    ng = nh // ATT_HEADS
    gw = ATT_HEADS * HEAD_DIM
    bq = ATT_BQ
    nq = tp // bq
    pair = 2 * HEAD_DIM
    assert pair == LANES and ATT_HEADS % 2 == 0
    scale = 1.0 / math.sqrt(HEAD_DIM)

    def body(q_ref, k_ref, v_ref, c_ref, ct_ref, lse_ref, do_ref, dq_ref, dk_ref, dv_ref, dc_ref,
             dk_s, dv_s, dc_s, ck_s, kt_s):
        p = pl.program_id(0)
        i = pl.program_id(1)

        @pl.when(i == 0)
        def _():
            dk_s[...] = jnp.zeros_like(dk_s)
            dv_s[...] = jnp.zeros_like(dv_s)
            dc_s[...] = jnp.zeros_like(dc_s)
            kt_s[...] = k_ref[...].astype(F32).T.astype(BF16)
            _head_columns_into(p, c_ref, ck_s)

        @pl.when(jnp.logical_and(i == 0, p == 0))
        def _():
            dc_ref[...] = jnp.zeros_like(dc_ref)

        def compute(spans):
            q0 = pl.multiple_of(i * bq, bq)
            dq_t = []
            for pi in range(ATT_HEADS // 2):
                lo = pair * pi
                q2 = q_ref[:, lo:lo + pair]
                do2 = do_ref[:, lo:lo + pair].astype(BF16)
                q_cols, do_cols = _pair_diag_cols(q2), _pair_diag_cols(do2)
                q_rows, do_rows = _pair_diag_rows(q2), _pair_diag_rows(do2)
                heads = (2 * pi, 2 * pi + 1)
                col_terms = [ct_ref[pl.ds(ATT_HEADS * p + hh, 1), :] - lse_ref[hh:hh + 1, :] for hh in heads]
                prs, dps = [], []
                for k0, k1, needs_mask in spans:
                    t2 = _dot(k_ref[k0:k1, lo:lo + pair], q_cols)
                    dp2 = _dot(v_ref[k0:k1, lo:lo + pair], do_cols)
                    if needs_mask:
                        seen = _seen_keys(k0, k1, q0)
                    pr_e, dp_e = [], []
                    for e, hh in enumerate(heads):
                        t = t2[:, e * bq:(e + 1) * bq] - ck_s[hh, k0:k1, :]
                        if needs_mask:
                            t = jnp.where(seen, t, NEG_BIG)
                        pr_e.append(jnp.exp(t + col_terms[e]))
                        dp_e.append(dp2[:, e * bq:(e + 1) * bq])
                    prs.append(pr_e)
                    dps.append(dp_e)
                key_sums = [sum(jnp.sum(pr[e] * dp[e], axis=0, keepdims=True) for pr, dp in zip(prs, dps))
                            for e in range(2)]
                dq2 = 0.0
                for (k0, k1, _), pr, dp in zip(spans, prs, dps):
                    ds = [pr[e] * (dp[e] - key_sums[e]) for e in range(2)]
                    for e, hh in enumerate(heads):
                        dc_s[hh, k0:k1, :] += jnp.sum(ds[e], axis=1, keepdims=True)
                    ds2 = jnp.concatenate([ds[0].astype(BF16), ds[1].astype(BF16)], axis=1)
                    pr2 = jnp.concatenate([pr[0].astype(BF16), pr[1].astype(BF16)], axis=1)
                    dk_s[k0:k1, lo:lo + pair] += _dot(ds2, q_rows)
                    dv_s[k0:k1, lo:lo + pair] += _dot(pr2, do_rows)
                    dq2 = dq2 + _dot(kt_s[lo:lo + pair, k0:k1], ds2)
                dq_t.append(jnp.concatenate([dq2[:HEAD_DIM, :bq], dq2[HEAD_DIM:, bq:]], axis=0))
            dq_ref[...] = (jnp.concatenate(dq_t, axis=0) * scale).T.astype(BF16)

        _for_bucket(i, nq, compute)

        @pl.when(i == nq - 1)
        def _():
            dk_ref[...] = dk_s[...].astype(BF16)
            dv_ref[...] = dv_s[...].astype(BF16)
            lane = lax.broadcasted_iota(jnp.int32, (tp, LANES), 1)
            dc = dc_ref[...]
            for hh in range(ATT_HEADS):
                dc = jnp.where(lane == ATT_HEADS * p + hh, -dc_s[hh], dc)
            dc_ref[...] = dc

    blk = pl.BlockSpec((bq, gw), lambda p, i: (i, p))
    col = pl.BlockSpec((tp, gw), lambda p, i: (0, p))
    vm = 7 * _nbytes((tp, gw), BF16) + 2 * _nbytes((tp, gw), F32) + 2 * ATT_HEADS * _nbytes((tp, LANES), F32) \
        + 2 * _nbytes((tp, LANES), F32) + 12 * ATT_HEADS * _nbytes((bq, tp), F32)
    return _call(body, (qkv, qkv, qkv, c, c_t, lse, do), name="attn_bwd", grid=(ng, nq),
                 in_specs=[blk,
                           pl.BlockSpec((tp, gw), lambda p, i: (0, ng + p)),
                           pl.BlockSpec((tp, gw), lambda p, i: (0, 2 * ng + p)),
                           _full((tp, LANES)), pl.BlockSpec((nh, bq), lambda p, i: (0, i)),
                           pl.BlockSpec((None, ATT_HEADS, bq), lambda p, i: (p, 0, i)), blk],
                 out_specs=[blk, col, col, _full((tp, LANES))],
                 out_shape=[jax.ShapeDtypeStruct((tp, att_w), BF16)] * 3 + [jax.ShapeDtypeStruct((tp, LANES), F32)],
                 scratch_shapes=[pltpu.VMEM((tp, gw), F32), pltpu.VMEM((tp, gw), F32),
                                 pltpu.VMEM((ATT_HEADS, tp, 1), F32), pltpu.VMEM((ATT_HEADS, tp, LANES), F32),
                                 pltpu.VMEM((gw, tp), BF16)],
                 semantics=("arbitrary", "arbitrary"), vmem_bytes=vm)


REC_ROWS = 128
HALO = SUBLANES


def _conv_taps(cat):
    taps = []
    for k in range(CONV_WIDTH):
        sh = CONV_WIDTH - 1 - k
        taps.append((pltpu.roll(cat, sh, 0) if sh else cat)[HALO:])
    return taps


def _rec_gates(xc, wa_ref, ba_ref, wx_ref, bx_ref, l_ref):
    xcb = xc.astype(BF16)
    r = _sigmoid(_dot(xcb, wa_ref[...]) + ba_ref[...])
    ig = _sigmoid(_dot(xcb, wx_ref[...]) + bx_ref[...])
    ls = _log_sigmoid(l_ref[...])
    log_a = RG_C * r * ls
    return xcb, r, ig, ls, log_a


def _rec_fwd(proj, xr_blk, yr_blk, rec_w, conv_w, conv_b, wa, ba, wx, bx, lru):
    tp = proj.shape[0]
    w = rec_w
    r_rows = REC_ROWS
    nc = tp // r_rows
    cpb = w // LANES

    def body(xr_ref, yr_ref, cw_ref, cb_ref, wa_ref, ba_ref, wx_ref, bx_ref, l_ref,
             hr_ref, rec_ref, prev_s, carry_s, a_s, u_s):
        i = pl.program_id(0)

        @pl.when(i == 0)
        def _():
            prev_s[...] = jnp.zeros_like(prev_s)
            carry_s[...] = jnp.zeros_like(carry_s)

        x = xr_ref[...]
        taps = _conv_taps(jnp.concatenate([prev_s[...], x], axis=0))
        prev_s[...] = x[r_rows - HALO:]
        xc = cb_ref[...]
        for k in range(CONV_WIDTH):
            xc = xc + cw_ref[k:k + 1, :] * taps[k]
        _, r, ig, ls, log_a = _rec_gates(xc, wa_ref, ba_ref, wx_ref, bx_ref, l_ref)
        a = jnp.exp(log_a)
        a_s[...] = a
        u_s[...] = jnp.sqrt(_one_minus_exp(2.0 * log_a, a * a)) * ig * xc

        def tile(j, h):
            r0 = pl.multiple_of(j * SUBLANES, SUBLANES)
            at = a_s[pl.ds(r0, SUBLANES), :]
            ut = u_s[pl.ds(r0, SUBLANES), :]
            out = []
            for rr in range(SUBLANES):
                h = at[rr:rr + 1] * h + ut[rr:rr + 1]
                out.append(h)
            hr_ref[pl.ds(r0, SUBLANES), :] = jnp.concatenate(out, axis=0)
            return h

        carry_s[0:1, :] = lax.fori_loop(0, r_rows // SUBLANES, tile, carry_s[0:1, :])
        g, _ = _gelu_and_grad(yr_ref[...])
        rec_ref[...] = hr_ref[...] * g

    blk = pl.BlockSpec((r_rows, w), lambda i: (i, 0))
    vm = 16 * _nbytes((r_rows, w), F32) + 4 * _nbytes((w, w), BF16)
    return _call(body, (proj, proj, conv_w, conv_b, wa, ba, wx, bx, lru), name="rec_fwd", grid=(nc,),
                 in_specs=[pl.BlockSpec((r_rows, w), lambda i: (i, xr_blk)),
                           pl.BlockSpec((r_rows, w), lambda i: (i, yr_blk)),
                           _full((CONV_WIDTH, w)), _full((1, w)), _full((w, w)), _full((1, w)),
                           _full((w, w)), _full((1, w)), _full((1, w))],
                 out_specs=[blk, blk],
                 out_shape=[jax.ShapeDtypeStruct((tp, w), F32)] * 2,
                 scratch_shapes=[pltpu.VMEM((HALO, w), F32), pltpu.VMEM((SUBLANES, w), F32),
                                 pltpu.VMEM((r_rows, w), F32), pltpu.VMEM((r_rows, w), F32)],
                 semantics=("arbitrary",), vmem_bytes=vm)


def _rec_bwd(proj, xr_blk, yr_blk, rec_w, hr, drec, conv_w, conv_b, wa, ba, wx, bx, lru):
    tp = proj.shape[0]
    w = rec_w
    r_rows = REC_ROWS
    nc = tp // r_rows
    hpc = r_rows // HALO

    def body(xr_ref, xh_ref, yr_ref, hr_ref, hh_ref, drec_ref, cw_ref, cb_ref, wa_ref, ba_ref, wx_ref, bx_ref,
             l_ref, dxr_ref, dyr_ref, dwa_ref, dwx_ref, small_ref, lam_s, a_s, dhr_s, carry_s, next_s):
        i = pl.program_id(0)
        first = (nc - 1 - i) == 0

        @pl.when(i == 0)
        def _():
            carry_s[...] = jnp.zeros_like(carry_s)
            next_s[...] = jnp.zeros_like(next_s)
            dwa_ref[...] = jnp.zeros_like(dwa_ref)
            dwx_ref[...] = jnp.zeros_like(dwx_ref)
            small_ref[...] = jnp.zeros_like(small_ref)

        x = xr_ref[...]
        xprev = jnp.where(first, 0.0, xh_ref[...])
        taps = _conv_taps(jnp.concatenate([xprev, x], axis=0))
        xc = cb_ref[...]
        for k in range(CONV_WIDTH):
            xc = xc + cw_ref[k:k + 1, :] * taps[k]
        xcb, r, ig, ls, log_a = _rec_gates(xc, wa_ref, ba_ref, wx_ref, bx_ref, l_ref)
        a = jnp.exp(log_a)
        a2 = a * a
        mult = jnp.sqrt(_one_minus_exp(2.0 * log_a, a2))
        g, dg = _gelu_and_grad(yr_ref[...])
        hr_v = hr_ref[...]
        drec_v = drec_ref[...]
        dhr_s[...] = drec_v * g
        dyr_ref[...] = (drec_v * hr_v * dg).astype(BF16)
        a_s[...] = a

        def tile(jj, carry):
            r0 = pl.multiple_of((r_rows // SUBLANES - 1 - jj) * SUBLANES, SUBLANES)
            at = a_s[pl.ds(r0, SUBLANES), :]
            dt = dhr_s[pl.ds(r0, SUBLANES), :]
            out = [None] * SUBLANES
            for rr in range(SUBLANES - 1, -1, -1):
                lam = dt[rr:rr + 1] + carry
                out[rr] = lam
                carry = at[rr:rr + 1] * lam
            lam_s[pl.ds(r0, SUBLANES), :] = jnp.concatenate(out, axis=0)
            return carry

        carry_s[0:1, :] = lax.fori_loop(0, r_rows // SUBLANES, tile, carry_s[0:1, :])
        lam = lam_s[...]
        hprev = jnp.where(first, 0.0, hh_ref[...])
        hr_prev = pltpu.roll(jnp.concatenate([hprev, hr_v], axis=0), 1, 0)[HALO:]
        da = lam * hr_prev
        dxc = lam * mult * ig
        di = lam * mult * xc
        dmult = lam * ig * xc
        dlog_a = da * a - dmult * a2 / mult
        dr = dlog_a * (RG_C * ls)
        dls = jnp.sum(dlog_a * (RG_C * r), axis=0, keepdims=True)
        dga = dr * r * (1.0 - r)
        dgx = di * ig * (1.0 - ig)
        dgab = dga.astype(BF16)
        dgxb = dgx.astype(BF16)
        dxc = dxc + _dot_nt(dgab, wa_ref[...]) + _dot_nt(dgxb, wx_ref[...])
        dwa_ref[...] += _dot_tn(xcb, dgab)
        dwx_ref[...] += _dot_tn(xcb, dgxb)
        cat = jnp.concatenate([dxc, next_s[...]], axis=0)
        next_s[...] = dxc[0:HALO]
        dxr = cw_ref[CONV_WIDTH - 1:CONV_WIDTH, :] * dxc
        for k in range(CONV_WIDTH - 1):
            sh = CONV_WIDTH - 1 - k
            dxr = dxr + cw_ref[k:k + 1, :] * pltpu.roll(cat, r_rows + HALO - sh, 0)[:r_rows]
        dxr_ref[...] = dxr.astype(BF16)
        rows = [jnp.sum(dxc * taps[k], axis=0, keepdims=True) for k in range(CONV_WIDTH)]
        rows += [jnp.sum(dxc, axis=0, keepdims=True), jnp.sum(dga, axis=0, keepdims=True),
                 jnp.sum(dgx, axis=0, keepdims=True), dls * _sigmoid(-l_ref[...])]
        small_ref[...] += jnp.concatenate(rows, axis=0)

    def rev(i):
        return nc - 1 - i

    def halo(i):
        return jnp.maximum(rev(i) * hpc - 1, 0)

    blk = pl.BlockSpec((r_rows, w), lambda i: (rev(i), 0))
    vm = 40 * _nbytes((r_rows, w), F32) + 6 * _nbytes((w, w), F32)
    return _call(body, (proj, proj, proj, hr, hr, drec, conv_w, conv_b, wa, ba, wx, bx, lru),
                 name="rec_bwd", grid=(nc,),
                 in_specs=[pl.BlockSpec((r_rows, w), lambda i: (rev(i), xr_blk)),
                           pl.BlockSpec((HALO, w), lambda i: (halo(i), xr_blk)),
                           pl.BlockSpec((r_rows, w), lambda i: (rev(i), yr_blk)),
                           blk,
                           pl.BlockSpec((HALO, w), lambda i: (halo(i), 0)),
                           blk,
                           _full((CONV_WIDTH, w)), _full((1, w)), _full((w, w)), _full((1, w)),
                           _full((w, w)), _full((1, w)), _full((1, w))],
                 out_specs=[blk, blk, _full((w, w)), _full((w, w)), _full((SUBLANES, w))],
                 out_shape=[jax.ShapeDtypeStruct((tp, w), BF16)] * 2
                 + [jax.ShapeDtypeStruct((w, w), F32)] * 2 + [jax.ShapeDtypeStruct((SUBLANES, w), F32)],
                 scratch_shapes=[pltpu.VMEM((r_rows, w), F32)] * 3
                 + [pltpu.VMEM((SUBLANES, w), F32), pltpu.VMEM((HALO, w), F32)],
                 semantics=("arbitrary",), vmem_bytes=vm)


ROW_TARGET = 544


def _mixer_out(attn, rec, g_a, g_r, w_out, h, g_next):
    tp, d = h.shape
    aw, rw = attn.shape[1], rec.shape[1]
    kc = d // N_CHIPS
    tm = _divisor_tile(tp, 16, ROW_TARGET)

    def body(a_ref, r_ref, ga_ref, gr_ref, w_ref, h_ref, gn_ref, h1_ref, z_ref, mix_ref):
        a = a_ref[...]
        r = r_ref[...]
        mix = jnp.concatenate([a * _rstd(a) * ga_ref[...], r * _rstd(r) * gr_ref[...]], axis=1).astype(BF16)
        mix_ref[...] = mix
        h1 = h_ref[...]
        for j in range(N_CHIPS):
            h1 = h1 + _dot(mix[:, j * kc:(j + 1) * kc], w_ref[j])
        h1_ref[...] = h1
        z_ref[...] = (h1 * _rstd(h1) * gn_ref[...]).astype(BF16)

    row = lambda wd: pl.BlockSpec((tm, wd), lambda i: (i, 0))
    vm = 2 * _nbytes((d, d), BF16) + 12 * _nbytes((tm, d), F32)
    return _call(body, (attn, rec, g_a, g_r, w_out, h, g_next), name="mixer_out", grid=(tp // tm,),
                 in_specs=[row(aw), row(rw), _full((1, aw)), _full((1, rw)), _full(w_out.shape), row(d),
                           _full((1, d))],
                 out_specs=[row(d), row(d), row(d)],
                 out_shape=[jax.ShapeDtypeStruct((tp, d), F32), jax.ShapeDtypeStruct((tp, d), BF16),
                            jax.ShapeDtypeStruct((tp, d), BF16)],
                 semantics=("parallel",), vmem_bytes=vm)


def _mixer_bwd(dh_b, w_out, attn, rec, g_a, g_r):
    tp, d = dh_b.shape
    aw, rw = attn.shape[1], rec.shape[1]
    tm = _divisor_tile(tp, 16, ROW_TARGET)

    def body(dh_ref, w_ref, a_ref, r_ref, ga_ref, gr_ref, da_ref, dr_ref, dg_ref):
        @pl.when(pl.program_id(0) == 0)
        def _():
            dg_ref[...] = jnp.zeros_like(dg_ref)

        dh = dh_ref[...]
        dmix = jnp.concatenate([_dot_nt(dh, w_ref[j]) for j in range(N_CHIPS)], axis=1)
        da, dga = _rms_bwd(dmix[:, :aw], a_ref[...], ga_ref[...])
        dr, dgr = _rms_bwd(dmix[:, aw:], r_ref[...], gr_ref[...])
        da_ref[...] = da
        dr_ref[...] = dr
        dg_ref[...] += jnp.broadcast_to(jnp.concatenate([dga, dgr], axis=1), (SUBLANES, d))

    row = lambda wd: pl.BlockSpec((tm, wd), lambda i: (i, 0))
    vm = 2 * _nbytes((d, d), BF16) + 12 * _nbytes((tm, d), F32)
    return _call(body, (dh_b, w_out, attn, rec, g_a, g_r), name="mixer_bwd", grid=(tp // tm,),
                 in_specs=[row(d), _full(w_out.shape), row(aw), row(rw), _full((1, aw)), _full((1, rw))],
                 out_specs=[row(aw), row(rw), _full((SUBLANES, d))],
                 out_shape=[jax.ShapeDtypeStruct((tp, aw), F32), jax.ShapeDtypeStruct((tp, rw), F32),
                            jax.ShapeDtypeStruct((SUBLANES, d), F32)],
                 semantics=("arbitrary",), vmem_bytes=vm)


def _mlp_up(z, w_up):
    tp, d = z.shape
    fc = w_up.shape[2]
    ff = N_CHIPS * fc
    tn = _divisor_tile(fc, LANES, 512)
    per = fc // tn

    def body(z_ref, w_ref, act_ref, up_ref):
        up = _dot(z_ref[...], w_ref[...])
        r = jnp.maximum(up, 0.0)
        act_ref[...] = (r * r).astype(BF16)
        up_ref[...] = up.astype(BF16)

    col = pl.BlockSpec((tp, tn), lambda j: (0, j))
    vm = 2 * _nbytes((tp, d), BF16) + 2 * _nbytes((d, tn), BF16) + 8 * _nbytes((tp, tn), F32)
    return _call(body, (z, w_up), name="mlp_up", grid=(ff // tn,),
                 in_specs=[_full((tp, d)), pl.BlockSpec((None, d, tn), lambda j: (j // per, 0, j % per))],
                 out_specs=[col, col],
                 out_shape=[jax.ShapeDtypeStruct((tp, ff), BF16)] * 2,
                 semantics=("parallel",), vmem_bytes=vm)


def _mlp_down(act, w_down, h, g_next):
    tp, d = h.shape
    ff = act.shape[1]
    fc = ff // N_CHIPS
    tm = _divisor_tile(tp, 16, ROW_TARGET)

    def body(a_ref, w_ref, h_ref, gn_ref, h2_ref, z_ref):
        h2 = h_ref[...]
        for j in range(N_CHIPS):
            h2 = h2 + _dot(a_ref[:, j * fc:(j + 1) * fc], w_ref[j])
        h2_ref[...] = h2
        z_ref[...] = (h2 * _rstd(h2) * gn_ref[...]).astype(BF16)

    row = lambda wd: pl.BlockSpec((tm, wd), lambda i: (i, 0))
    vm = 2 * _nbytes((ff, d), BF16) + 2 * _nbytes((tm, ff), BF16) + 10 * _nbytes((tm, d), F32)
    return _call(body, (act, w_down, h, g_next), name="mlp_down", grid=(tp // tm,),
                 in_specs=[row(ff), _full(w_down.shape), row(d), _full((1, d))],
                 out_specs=[row(d), row(d)],
                 out_shape=[jax.ShapeDtypeStruct((tp, d), F32), jax.ShapeDtypeStruct((tp, d), BF16)],
                 semantics=("parallel",), vmem_bytes=vm)


def _loss_bwd(h, g, target, n_real):
    tp, d = h.shape
    tm = _divisor_tile(tp, 16, ROW_TARGET)

    def body(h_ref, g_ref, t_ref, dh_ref, dhb_ref, dg_ref, loss_ref):
        i = pl.program_id(0)

        @pl.when(i == 0)
        def _():
            dg_ref[...] = jnp.zeros_like(dg_ref)
            loss_ref[...] = jnp.zeros_like(loss_ref)

        x = h_ref[...]
        gv = g_ref[...]
        rowi = i * tm + lax.broadcasted_iota(jnp.int32, (tm, 1), 0)
        real = jnp.logical_and(rowi >= N_META, rowi < N_META + n_real)
        err = jnp.where(real, x * _rstd(x) * gv - t_ref[...], 0.0)
        loss_ref[...] += 0.5 * jnp.sum(jnp.mean(err * err, axis=-1, keepdims=True))
        dx, dgp = _rms_bwd(err * (1.0 / d), x, gv)
        dh_ref[...] = dx
        dhb_ref[...] = dx.astype(BF16)
        dg_ref[...] += jnp.broadcast_to(dgp, (SUBLANES, d))

    row = pl.BlockSpec((tm, d), lambda i: (i, 0))
    return _call(body, (h, g, target), name="loss_bwd", grid=(tp // tm,),
                 in_specs=[row, _full((1, d)), row],
                 out_specs=[row, row, _full((SUBLANES, d)), _full((SUBLANES, LANES))],
                 out_shape=[jax.ShapeDtypeStruct((tp, d), F32), jax.ShapeDtypeStruct((tp, d), BF16),
                            jax.ShapeDtypeStruct((SUBLANES, d), F32), jax.ShapeDtypeStruct((SUBLANES, LANES), F32)],
                 semantics=("arbitrary",), vmem_bytes=16 * _nbytes((tm, d), F32))


def _mlp_bwd(dh_b, w_down, up, z2):
    tp, d = dh_b.shape
    fc = w_down.shape[1]
    ff = N_CHIPS * fc
    tn = _divisor_tile(fc, LANES, 512)
    per = fc // tn

    def body(dh_ref, z_ref, w_ref, up_ref, dup_ref, gd_ref, gu_ref):
        dh = dh_ref[...]
        r = jnp.maximum(up_ref[...].astype(F32), 0.0)
        dup = (_dot_nt(dh, w_ref[...]) * (2.0 * r)).astype(BF16)
        dup_ref[...] = dup
        gd_ref[...] = _dot_tn((r * r).astype(BF16), dh).astype(BF16)
        gu_ref[...] = _dot_tn(z_ref[...], dup).astype(BF16)

    col = pl.BlockSpec((tp, tn), lambda j: (0, j))
    vm = 4 * _nbytes((tp, d), BF16) + 4 * _nbytes((tn, d), BF16) + 2 * _nbytes((d, tn), BF16) \
        + 10 * _nbytes((tp, tn), F32) + 4 * _nbytes((tn, d), F32)
    return _call(body, (dh_b, z2, w_down, up), name="mlp_bwd", grid=(ff // tn,),
                 in_specs=[_full((tp, d)), _full((tp, d)),
                           pl.BlockSpec((None, tn, d), lambda j: (j // per, j % per, 0)), col],
                 out_specs=[col, pl.BlockSpec((tn, d), lambda j: (j, 0)),
                            pl.BlockSpec((None, d, tn), lambda j: (j // per, 0, j % per))],
                 out_shape=[jax.ShapeDtypeStruct((tp, ff), BF16), jax.ShapeDtypeStruct((ff, d), BF16),
                            jax.ShapeDtypeStruct((N_CHIPS, d, fc), BF16)],
                 semantics=("parallel",), vmem_bytes=vm)


def _grad_w_pieces(pieces, b):
    tp, n = b.shape
    tn = _divisor_tile(n, LANES, 512)
    widths = [pc.shape[1] for pc in pieces]

    def body(*refs):
        p_refs, b_ref, o_refs = refs[:len(pieces)], refs[len(pieces)], refs[len(pieces) + 1:]
        for p_ref, o_ref in zip(p_refs, o_refs):
            o_ref[...] = _dot_tn(p_ref[...], b_ref[...]).astype(BF16)

    vm = 2 * sum(_nbytes((tp, wd), BF16) for wd in widths) + 2 * _nbytes((tp, tn), BF16) \
        + 4 * sum(_nbytes((wd, tn), F32) for wd in widths) + 2 * _nbytes((tp, max(widths)), F32)
    return _call(body, tuple(pieces) + (b,), name="grad_w_pieces", grid=(n // tn,),
                 in_specs=[_full(pc.shape) for pc in pieces] + [pl.BlockSpec((tp, tn), lambda j: (0, j))],
                 out_specs=[pl.BlockSpec((wd, tn), lambda j: (0, j)) for wd in widths],
                 out_shape=[jax.ShapeDtypeStruct((wd, n), BF16) for wd in widths],
                 semantics=("parallel",), vmem_bytes=vm)


def _dx_norm_bwd(pieces, w, w_spec, w_piece, h, g, dres, dot=_dot_nt):
    tp, d = h.shape
    tm = _divisor_tile(tp, 16, ROW_TARGET)
    n = len(pieces)

    def body(*refs):
        dy_refs = refs[:n]
        w_ref, h_ref, g_ref, dres_ref, dh_ref, dhb_ref, dg_ref = refs[n:]

        @pl.when(pl.program_id(0) == 0)
        def _():
            dg_ref[...] = jnp.zeros_like(dg_ref)

        dz = dot(dy_refs[0][...], w_piece(w_ref, 0))
        for i in range(1, n):
            dz = dz + dot(dy_refs[i][...], w_piece(w_ref, i))
        dx, dgp = _rms_bwd(dz, h_ref[...], g_ref[...])
        dh = dres_ref[...] + dx
        dh_ref[...] = dh
        dhb_ref[...] = dh.astype(BF16)
        dg_ref[...] += jnp.broadcast_to(dgp, (SUBLANES, d))

    row = lambda wd: pl.BlockSpec((tm, wd), lambda i: (i, 0))
    kk = sum(wd for _, _, wd in pieces)
    vm = 2 * _nbytes((d, kk), BF16) + 2 * _nbytes((tm, kk), BF16) + 14 * _nbytes((tm, d), F32)
    piece_specs = [pl.BlockSpec((tm, wd), functools.partial(lambda i, cb: (i, cb), cb=cb)) for _, cb, wd in pieces]
    return _call(body, tuple(a for a, _, _ in pieces) + (w, h, g, dres), name="dx_norm_bwd", grid=(tp // tm,),
                 in_specs=piece_specs + [w_spec, row(d), _full((1, d)), row(d)],
                 out_specs=[row(d), row(d), _full((SUBLANES, d))],
                 out_shape=[jax.ShapeDtypeStruct((tp, d), F32), jax.ShapeDtypeStruct((tp, d), BF16),
                            jax.ShapeDtypeStruct((SUBLANES, d), F32)],
                 semantics=("arbitrary",), vmem_bytes=vm)


def _block_diag(wg):
    nb, b, _ = wg.shape
    eye = jnp.eye(nb, dtype=wg.dtype)
    return (eye[:, None, :, None] * wg[:, :, None, :]).reshape(nb * b, nb * b)


def _diag_blocks(dense, nb):
    b = dense.shape[0] // nb
    d4 = dense.reshape(nb, b, nb, b)
    return jnp.stack([d4[i, :, i, :] for i in range(nb)])


def _row(v):
    return v.reshape(1, -1)


def _forward_layer(l, h, z, p, fetch, stage_next, g_next):
    d = h.shape[1]
    att_w = d // 2
    rec_w = d - att_w
    nh = att_w // HEAD_DIM
    wa_d = _block_diag(p["w_gate_a"][l]).astype(BF16)
    wx_d = _block_diag(p["w_gate_x"][l]).astype(BF16)
    b_f_pad = jnp.zeros((1, LANES), F32).at[0, :nh].set(p["b_f"][l])
    w_in_t = fetch("w_in", z)
    big = dict(w_in_big=_pack_w_in_t(w_in_t.reshape(-1, d), att_w, nh))
    qkv, proj = _proj(z, big["w_in_big"], att_w)
    c, c_t = _fgate_fwd(proj, b_f_pad, nh)
    attn, lse_b = _attn_fwd(qkv, c, c_t, nh)
    hr, rec = _rec_fwd(proj, 0, 1, rec_w, p["conv_w"][l], _row(p["conv_b"][l]), wa_d,
                       _row(p["b_gate_a"][l]), wx_d, _row(p["b_gate_x"][l]), _row(p["lru_L"][l]))
    tok = stage_next(attn, "own")
    big["w_out"] = fetch("w_out", rec)
    h1, z2, mix = _mixer_out(attn, rec, _row(p["attn_out_g"][l] + tok), _row(p["rec_out_g"][l]),
                             big["w_out"], h, _row(p["mlp_norm_g"][l]))
    big["w_up"] = fetch("w_up", h1)
    act, up = _mlp_up(z2, big["w_up"])
    tok = stage_next(act, "next")
    big["w_down"] = fetch("w_down", act)
    h2, z_next = _mlp_down(act, big["w_down"], h1, _row(g_next + tok))
    saved = dict(h0=h, z1=z, proj=proj, qkv=qkv, c=c, c_t=c_t, attn=attn, lse_b=lse_b, hr=hr, rec=rec, h1=h1,
                 z2=z2, mix=mix, up=up, wa_d=wa_d, wx_d=wx_d, b_f_pad=b_f_pad, big=big)
    return h2, z_next, saved


def _backward_mlp(l, dh, dh_b, sv, p, tok):
    w_up, w_down = sv["big"]["w_up"], sv["big"]["w_down"]
    fc = w_up.shape[2]
    dup, g_down, g_up = _mlp_bwd(dh_b, w_down, sv["up"], sv["z2"])
    dh, dh_b, dg2 = _dx_norm_bwd([(dup, j, fc) for j in range(N_CHIPS)], w_up, _full(w_up.shape),
                                 lambda w_ref, j: w_ref[j], sv["h1"], _row(p["mlp_norm_g"][l] + tok), dh)
    big = dict(w_down=g_down.reshape((N_CHIPS, -1) + g_down.shape[1:]), w_up=g_up)
    return dh, dh_b, big, dict(mlp_norm_g=dg2[0])


def _backward_mixer(l, dh, dh_b, sv, p, tok):
    d = dh.shape[1]
    att_w = d // 2
    rec_w = d - att_w
    nh = att_w // HEAD_DIM
    small = {}
    g_out, = _grad_w_pieces([sv["mix"]], dh_b)
    dattn, drec, dg_mix = _mixer_bwd(dh_b, sv["big"]["w_out"], sv["attn"], sv["rec"],
                                     _row(p["attn_out_g"][l] + tok), _row(p["rec_out_g"][l]))
    small["attn_out_g"] = dg_mix[0, :att_w]
    small["rec_out_g"] = dg_mix[0, att_w:]
    dxr, dyr, dwa, dwx, sm = _rec_bwd(
        sv["proj"], 0, 1, rec_w, sv["hr"], drec, p["conv_w"][l], _row(p["conv_b"][l]), sv["wa_d"],
        _row(p["b_gate_a"][l]), sv["wx_d"], _row(p["b_gate_x"][l]), _row(p["lru_L"][l]))
    small.update(conv_w=sm[:CONV_WIDTH], conv_b=sm[4], b_gate_a=sm[5], b_gate_x=sm[6], lru_L=sm[7],
                 w_gate_a=_diag_blocks(dwa, N_REC_BLOCKS), w_gate_x=_diag_blocks(dwx, N_REC_BLOCKS))
    dq, dk, dv, dc = _attn_bwd(sv["qkv"], sv["c"], sv["c_t"], sv["lse_b"], dattn, nh)
    df, db_f = _fgate_bwd(sv["proj"], sv["b_f_pad"], dc)
    small["b_f"] = db_f[0, :nh]
    pieces = [dq, dk, dv, dxr, dyr, df]
    offs = [0, att_w, 2 * att_w, 3 * att_w, 3 * att_w + rec_w, 3 * att_w + 2 * rec_w]
    gq, gk, gv, gxr, gyr, gf = _grad_w_pieces(pieces, sv["z1"])
    g_in_t = jnp.concatenate([gq, gk, gv, gf[:nh], gxr, gyr], axis=0)
    w_big = sv["big"]["w_in_big"]
    widths = [pc.shape[1] for pc in pieces]
    dh, dh_b, dg1 = _dx_norm_bwd(
        [(pc, 0, wd) for pc, wd in zip(pieces, widths)], w_big, _full(w_big.shape),
        lambda w_ref, i: w_ref[offs[i]:offs[i] + widths[i], :], sv["h0"], _row(p["attn_norm_g"][l]), dh, dot=_dot)
    small["attn_norm_g"] = dg1[0]
    big = dict(w_in=g_in_t.reshape(N_CHIPS, -1, d), w_out=g_out.reshape((N_CHIPS, -1) + g_out.shape[1:]))
    return dh, dh_b, big, small


def _pack_w_in_t(w_in_t, att_w, nh):
    qkv = w_in_t[:3 * att_w]
    f = w_in_t[3 * att_w:3 * att_w + nh]
    xy = w_in_t[3 * att_w + nh:]
    return jnp.concatenate([qkv, xy, f, jnp.zeros((LANES - nh, w_in_t.shape[1]), w_in_t.dtype)], axis=0)


ANY = pl.BlockSpec(memory_space=pl.ANY)


def _coords():
    return lax.axis_index("x"), lax.axis_index("y"), lax.axis_index("c")


def _other_chips(x, y):
    return [(1 - x, y), (x, 1 - y), (1 - x, 1 - y)]


def _remote(src, dst, send_sems, recv_sems, k, to):
    return pltpu.make_async_remote_copy(src_ref=src, dst_ref=dst, send_sem=send_sems.at[k],
                                        recv_sem=recv_sems.at[k], device_id=to, device_id_type=MESH)


def _all_gather_chips(shards):
    n = len(shards)
    per = 6

    def body(*refs):
        ins, outs = refs[:n], refs[n:2 * n]
        send_sems, recv_sems, local_sems = refs[2 * n:]
        x, y, c = _coords()
        me = 2 * x + y
        sibling = (x, y, 1 - c)
        chips = _other_chips(x, y)
        local = [pltpu.make_async_copy(ins[t], outs[t].at[me], local_sems.at[t]) for t in range(n)]
        for cp in local:
            cp.start()
        sends = []
        for t in range(n):
            for j, (px, py) in enumerate(chips):
                cp = _remote(ins[t].at[c], outs[t].at[me, c], send_sems, recv_sems, per * t + j, (px, py, c))
                cp.start()
                sends.append(cp)
        for t in range(n):
            for j, (px, py) in enumerate(chips):
                landed = outs[t].at[2 * px + py, c]
                _remote(landed, landed, send_sems, recv_sems, per * t + j, (px, py, c)).wait_recv()
                cp = _remote(landed, landed, send_sems, recv_sems, per * t + 3 + j, sibling)
                cp.start()
                sends.append(cp)
        for t in range(n):
            for j, (px, py) in enumerate(chips):
                passed = outs[t].at[2 * px + py, 1 - c]
                _remote(passed, passed, send_sems, recv_sems, per * t + 3 + j, sibling).wait_recv()
        for cp in sends:
            cp.wait_send()
        for cp in local:
            cp.wait()

    return _call(body, tuple(shards), name="all_gather_chips",
                 in_specs=[ANY] * n, out_specs=[ANY] * n,
                 out_shape=[jax.ShapeDtypeStruct((N_CHIPS,) + s.shape, s.dtype) for s in shards],
                 scratch_shapes=[pltpu.SemaphoreType.DMA((per * n,)), pltpu.SemaphoreType.DMA((per * n,)),
                                 pltpu.SemaphoreType.DMA((n,))])


HBM = pl.BlockSpec(memory_space=pltpu.HBM)
SEM = pl.BlockSpec(memory_space=pltpu.SEMAPHORE)
DATAFLOW = pltpu.SideEffectType.DATAFLOW_SIDE_EFFECTING


def _in_hbm(a):
    return pltpu.with_memory_space_constraint(a, pltpu.HBM)


PUSH_ARRIVALS = {"gather_chips_half": N_CHIPS - 1, "pass_halves": N_CHIPS - 1, "scatter_chips": N_CHIPS - 1,
                 "sibling": 1, "gather_devices": N_DEV - 1}


def _column_half(ref3, slab, c):
    hw = ref3.shape[2] // 2
    return ref3.at[slab, :, pl.ds(pl.multiple_of(c * hw, LANES), hw)]


def _push_copies(mode, src, land, send_sems, recv_sems, t):
    x, y, c = _coords()
    chip = 2 * x + y
    if mode == "gather_chips_half":
        return [_remote(_column_half(src, chip, c), _column_half(land, chip, c), send_sems, recv_sems, t, (px, py, c))
                for px, py in _other_chips(x, y)]
    if mode == "pass_halves":
        return [_remote(_column_half(src, 2 * px + py, c), _column_half(land, 2 * px + py, c), send_sems, recv_sems, t,
                        (x, y, 1 - c)) for px, py in _other_chips(x, y)]
    if mode == "scatter_chips":
        return [_remote(src.at[2 * px + py], land.at[chip], send_sems, recv_sems, t, (px, py, c))
                for px, py in _other_chips(x, y)]
    if mode == "sibling":
        return [_remote(src, land, send_sems, recv_sems, t, (x, y, 1 - c))]
    dev = 4 * x + 2 * y + c
    return [_remote(src.at[dev], land.at[dev], send_sems, recv_sems, t, (x ^ (k >> 2), y ^ ((k >> 1) & 1), c ^ (k & 1)))
            for k in range(1, N_DEV)]


def _push_start(srcs, lands, mode, name):
    n = len(srcs)
    same = all(s is ld for s, ld in zip(srcs, lands))
    n_in = n if same else 2 * n

    def body(*refs):
        src_refs = refs[:n]
        land_refs = src_refs if same else refs[n:2 * n]
        send_sems, recv_sems = refs[n_in], refs[n_in + 1]
        token = refs[-1]
        for t in range(n):
            for cp in _push_copies(mode, src_refs[t], land_refs[t], send_sems, recv_sems, t):
                cp.start()
        token[...] = jnp.zeros_like(token)

    operands = tuple(srcs) if same else tuple(srcs) + tuple(lands)
    res = _call(
        body, [_in_hbm(a) for a in operands], name=name,
        out_shape=(pltpu.SemaphoreType.DMA((n,)), pltpu.SemaphoreType.DMA((n,)))
        + tuple(pltpu.HBM(a.shape, a.dtype) for a in operands) + (jax.ShapeDtypeStruct((SUBLANES, LANES), F32),),
        in_specs=[HBM] * n_in, out_specs=(SEM, SEM) + (HBM,) * n_in + (pl.BlockSpec(memory_space=pltpu.VMEM),),
        input_output_aliases={i: 2 + i for i in range(n_in)}, side_effects=DATAFLOW, hbm_results=False)
    send_sems, recv_sems, token = res[0], res[1], res[-1]
    srcs_thru = res[2:2 + n]
    lands_thru = srcs_thru if same else res[2 + n:2 + 2 * n]
    return send_sems, recv_sems, srcs_thru, lands_thru, token


def _push_wait(send_sems, recv_sems, ids, srcs, lands, mode, after, name):
    n = len(lands)
    same = all(s is ld for s, ld in zip(srcs, lands))
    n_in = n if same else 2 * n

    def body(*refs):
        land_refs = refs[:n] if same else refs[n:2 * n]
        send_sems, recv_sems = refs[n_in], refs[n_in + 1]
        x, y, c = _coords()
        for t in range(n):
            if mode == "sibling":
                moved = land_refs[t]
            elif mode in ("gather_chips_half", "pass_halves"):
                moved = land_refs[t].at[pl.ds(0, PUSH_ARRIVALS[mode]), :, pl.ds(0, land_refs[t].shape[2] // 2)]
            else:
                moved = land_refs[t].at[pl.ds(0, PUSH_ARRIVALS[mode])]
            arrivals = _remote(moved, moved, send_sems, recv_sems, ids[t], (x, y, c))
            arrivals.wait_send()
            arrivals.wait_recv()

    operands = tuple(lands) if same else tuple(srcs) + tuple(lands)
    res = _call(
        body, operands + (send_sems, recv_sems, after), name=name,
        out_shape=tuple(pltpu.HBM(a.shape, a.dtype) for a in operands),
        in_specs=[HBM] * n_in + [SEM, SEM, ANY], out_specs=(HBM,) * n_in,
        input_output_aliases={i: i for i in range(n_in)}, side_effects=DATAFLOW)
    return list(res) if same else (list(res[:n]), list(res[n:]))


def _sum_partials(part, landed, chip):
    _, rows, cols = part.shape
    br = _divisor_tile(rows, 16, ELEM_ROWS)

    def body(chip_ref, own_ref, a_ref, b_ref, c_ref, o_ref):
        o_ref[...] = ((own_ref[...].astype(F32) + a_ref[...].astype(F32)) + b_ref[...].astype(F32)) \
            + c_ref[...].astype(F32)

    def other(k):
        return pl.BlockSpec((None, br, cols), lambda i, ch: (jnp.where(ch[0] <= k, k + 1, k), i, 0))

    spec = pltpu.PrefetchScalarGridSpec(
        num_scalar_prefetch=1, grid=(rows // br,),
        in_specs=[pl.BlockSpec((None, br, cols), lambda i, ch: (ch[0], i, 0)), other(0), other(1), other(2)],
        out_specs=pl.BlockSpec((br, cols), lambda i, ch: (i, 0)))
    return _call(body, (chip, part, landed, landed, landed), name="sum_partials", grid_spec=spec,
                 out_shape=jax.ShapeDtypeStruct((rows, cols), F32), semantics=("parallel",))


def _cast_to_slab(w, l, chip):
    _, rows, cols = w.shape
    br = _divisor_tile(rows, 16, ELEM_ROWS)

    def body(chip_ref, w_ref, o_ref):
        o_ref[...] = w_ref[...].astype(BF16)

    spec = pltpu.PrefetchScalarGridSpec(
        num_scalar_prefetch=1, grid=(rows // br,),
        in_specs=[pl.BlockSpec((None, br, cols), lambda i, ch: (l, i, 0))],
        out_specs=pl.BlockSpec((None, br, cols), lambda i, ch: (ch[0], i, 0)))
    return _call(body, (chip, w), name="cast_to_slab", grid_spec=spec,
                 out_shape=jax.ShapeDtypeStruct((N_CHIPS, rows, cols), BF16), semantics=("parallel",))


def _cast_w_in_t_to_slabs(w_t, chip):
    rows, depth, d = w_t.shape
    tn = _divisor_tile(d, LANES, 256)

    def body(chip_ref, w_ref, *o_refs):
        for l in range(depth):
            o_refs[l][...] = w_ref[:, l, :].astype(BF16)

    spec = pltpu.PrefetchScalarGridSpec(
        num_scalar_prefetch=1, grid=(d // tn,),
        in_specs=[pl.BlockSpec((rows, depth, tn), lambda j, ch: (0, 0, j))],
        out_specs=[pl.BlockSpec((None, rows, tn), lambda j, ch: (ch[0], 0, j))] * depth)
    return _call(body, (chip, w_t), name="cast_w_in_t_to_slabs", grid_spec=spec,
                 out_shape=[jax.ShapeDtypeStruct((N_CHIPS, rows, d), BF16)] * depth, semantics=("parallel",),
                 vmem_bytes=4 * _nbytes((rows, max(depth, SUBLANES), tn), F32))


def _place_slab(buf, index, n_slabs):
    rows, cols = buf.shape
    br = _divisor_tile(rows, SUBLANES, ELEM_ROWS)

    def body(index_ref, b_ref, o_ref):
        o_ref[...] = b_ref[...]

    spec = pltpu.PrefetchScalarGridSpec(
        num_scalar_prefetch=1, grid=(rows // br,),
        in_specs=[pl.BlockSpec((br, cols), lambda i, ix: (i, 0))],
        out_specs=pl.BlockSpec((None, br, cols), lambda i, ix: (ix[0], i, 0)))
    return _call(body, (index, buf), name="place_slab", grid_spec=spec,
                 out_shape=jax.ShapeDtypeStruct((n_slabs, rows, cols), buf.dtype), semantics=("parallel",))


ELEM_ROWS = 256


def _sum_slabs(r):
    n, rows, cols = r.shape
    br = _divisor_tile(rows, 16, ELEM_ROWS)

    def body(r_ref, o_ref):
        acc = r_ref[0].astype(F32)
        for j in range(1, n):
            acc = acc + r_ref[j].astype(F32)
        o_ref[...] = acc

    return _call(body, (r,), name="sum_slabs", grid=(rows // br,),
                 in_specs=[pl.BlockSpec((n, br, cols), lambda i: (0, i, 0))],
                 out_specs=pl.BlockSpec((br, cols), lambda i: (i, 0)),
                 out_shape=jax.ShapeDtypeStruct((rows, cols), F32), semantics=("parallel",))


def _adamw_math(w, g, m, v):
    c1 = 1.0 - ADAM_B1 ** ADAM_STEP
    c2 = 1.0 - ADAM_B2 ** ADAM_STEP
    nm = ADAM_B1 * m + (1.0 - ADAM_B1) * g
    nv = ADAM_B2 * v + (1.0 - ADAM_B2) * (g * g)
    delta = -ADAM_LR * ((nm / c1) / (jnp.sqrt(nv / c2) + ADAM_EPS) + ADAM_WD * w)
    return delta, nm, nv


def _adamw(w, g, m, v):
    rows, cols = w.shape
    br = _divisor_tile(rows, 8, ELEM_ROWS)

    def body(w_ref, g_ref, m_ref, v_ref, d_ref, nm_ref, nv_ref):
        d_ref[...], nm_ref[...], nv_ref[...] = _adamw_math(w_ref[...], g_ref[...], m_ref[...], v_ref[...])

    blk = pl.BlockSpec((br, cols), lambda i: (i, 0))
    return _call(body, (w, g, m, v), name="adamw", grid=(rows // br,),
                 in_specs=[blk] * 4, out_specs=[blk] * 3,
                 out_shape=[jax.ShapeDtypeStruct((rows, cols), F32)] * 3, semantics=("parallel",))


def _adamw_w_in_t(w_t, m_t, v_t, g_mine, g_theirs):
    rows, depth, d = w_t.shape
    tn = LANES

    def body(w_ref, m_ref, v_ref, *rest):
        ga_refs, gb_refs = rest[:depth], rest[depth:2 * depth]
        g_ref, d_ref, nm_ref, nv_ref = rest[2 * depth:]
        for l in range(depth):
            g_ref[:, l, :] = ga_refs[l][...] + gb_refs[l][...]
        d_ref[...], nm_ref[...], nv_ref[...] = _adamw_math(w_ref[...], g_ref[...], m_ref[...], v_ref[...])

    slab = pl.BlockSpec((rows, depth, tn), lambda j: (0, 0, j))
    gblk = pl.BlockSpec((rows, tn), lambda j: (0, j))
    return _call(body, (w_t, m_t, v_t) + tuple(g_mine) + tuple(g_theirs), name="adamw_w_in_t", grid=(d // tn,),
                 in_specs=[slab] * 3 + [gblk] * (2 * depth), out_specs=[slab] * 4,
                 out_shape=[jax.ShapeDtypeStruct(w_t.shape, F32)] * 4, semantics=("parallel",),
                 vmem_bytes=2 * (7 * _nbytes((rows, max(depth, SUBLANES), tn), F32)
                                 + 2 * depth * _nbytes((rows, tn), F32)))


def _adamw_layer(w, m, v, l, g_mine, g_theirs, prev, after):
    _, rows, cols = w.shape
    br = _divisor_tile(rows, 8, ELEM_ROWS)

    def body(w_ref, m_ref, v_ref, ga_ref, gb_ref, *rest):
        g_ref, d_ref, nm_ref, nv_ref = rest[5:]
        g = ga_ref[...] + gb_ref[...]
        g_ref[...] = g
        d_ref[...], nm_ref[...], nv_ref[...] = _adamw_math(w_ref[...], g, m_ref[...], v_ref[...])

    slot = pl.BlockSpec((None, br, cols), lambda i: (l, i, 0))
    blk = pl.BlockSpec((br, cols), lambda i: (i, 0))
    return _call(body, (w, m, v, g_mine, g_theirs) + tuple(prev) + (after,), name="adamw_layer",
                 grid=(rows // br,), in_specs=[slot] * 3 + [blk] * 2 + [ANY] * 5, out_specs=[slot] * 4,
                 out_shape=[jax.ShapeDtypeStruct(w.shape, F32)] * 4,
                 input_output_aliases={5: 0, 6: 1, 7: 2, 8: 3}, semantics=("parallel",))


BIG = ("w_in", "w_out", "w_up", "w_down")
WEIGHTS = ("meta", "attn_norm_g", "w_in", "b_f", "conv_w", "conv_b", "w_gate_a", "b_gate_a", "w_gate_x",
           "b_gate_x", "lru_L", "attn_out_g", "rec_out_g", "w_out", "mlp_norm_g", "w_up", "w_down", "final_g")
SMALL = tuple(k for k in WEIGHTS if k not in BIG)
COL_SHARDED_SMALL = ("meta", "conv_w")


def _packed_rows(shape):
    return -(-math.prod(shape) // (SUBLANES * LANES)) * SUBLANES


def _pack(arrs):
    rows = []
    for a in arrs:
        flat = a.reshape(-1)
        rows.append(jnp.pad(flat, (0, _packed_rows(a.shape) * LANES - flat.shape[0])).reshape(-1, LANES))
    used = sum(r.shape[0] for r in rows)
    rows.append(jnp.zeros((-used % ELEM_ROWS, LANES), F32))
    return jnp.concatenate(rows, axis=0)


def _unpack(buf, shapes):
    out, r0 = [], 0
    for s in shapes:
        nr = _packed_rows(s)
        out.append(buf[r0:r0 + nr].reshape(-1)[:math.prod(s)].reshape(s))
        r0 += nr
    return out


def _halves(a):
    return a.reshape((2, a.shape[0] // 2) + a.shape[1:])


def _cols_from_chips(g):
    return jnp.moveaxis(g, 0, -2).reshape(g.shape[1:-1] + (N_CHIPS * g.shape[-1],))


def kernel(x, meta, attn_norm_g, w_in, b_f, conv_w, conv_b, w_gate_a, b_gate_a, w_gate_x, b_gate_x, lru_L, attn_out_g, rec_out_g, w_out, mlp_norm_g, w_up, w_down, final_g, loss_target, m_meta, m_attn_norm_g, m_w_in, m_b_f, m_conv_w, m_conv_b, m_w_gate_a, m_b_gate_a, m_w_gate_x, m_b_gate_x, m_lru_L, m_attn_out_g, m_rec_out_g, m_w_out, m_mlp_norm_g, m_w_up, m_w_down, m_final_g, v_meta, v_attn_norm_g, v_w_in, v_b_f, v_conv_w, v_conv_b, v_w_gate_a, v_b_gate_a, v_w_gate_x, v_b_gate_x, v_lru_L, v_attn_out_g, v_rec_out_g, v_w_out, v_mlp_norm_g, v_w_up, v_w_down, v_final_g):
    w = dict(meta=meta, attn_norm_g=attn_norm_g, w_in=w_in, b_f=b_f, conv_w=conv_w, conv_b=conv_b,
             w_gate_a=w_gate_a, b_gate_a=b_gate_a, w_gate_x=w_gate_x, b_gate_x=b_gate_x, lru_L=lru_L,
             attn_out_g=attn_out_g, rec_out_g=rec_out_g, w_out=w_out, mlp_norm_g=mlp_norm_g, w_up=w_up,
             w_down=w_down, final_g=final_g)
    m = dict(meta=m_meta, attn_norm_g=m_attn_norm_g, w_in=m_w_in, b_f=m_b_f, conv_w=m_conv_w, conv_b=m_conv_b,
             w_gate_a=m_w_gate_a, b_gate_a=m_b_gate_a, w_gate_x=m_w_gate_x, b_gate_x=m_b_gate_x, lru_L=m_lru_L,
             attn_out_g=m_attn_out_g, rec_out_g=m_rec_out_g, w_out=m_w_out, mlp_norm_g=m_mlp_norm_g,
             w_up=m_w_up, w_down=m_w_down, final_g=m_final_g)
    v = dict(meta=v_meta, attn_norm_g=v_attn_norm_g, w_in=v_w_in, b_f=v_b_f, conv_w=v_conv_w, conv_b=v_conv_b,
             w_gate_a=v_w_gate_a, b_gate_a=v_b_gate_a, w_gate_x=v_w_gate_x, b_gate_x=v_b_gate_x, lru_L=v_lru_L,
             attn_out_g=v_attn_out_g, rec_out_g=v_rec_out_g, w_out=v_w_out, mlp_norm_g=v_mlp_norm_g,
             w_up=v_w_up, w_down=v_w_down, final_g=v_final_g)
    s_len, d = x.shape[1], x.shape[2]
    depth = w_in.shape[0]
    att_w = d // 2
    rec_w = d - att_w
    nh = att_w // HEAD_DIM
    chip = 2 * lax.axis_index("x") + lax.axis_index("y")

    g_conv, g_meta = [g.reshape((N_CHIPS, g.shape[1] * g.shape[2]) + g.shape[3:])
                      for g in _all_gather_chips([_halves(w["conv_w"]), _halves(w["meta"])])]
    p = dict(w)
    p["conv_w"] = _cols_from_chips(g_conv)
    meta_full = jnp.moveaxis(g_meta, 0, 1).reshape(N_META, d)

    chip1 = chip.reshape(1).astype(jnp.int32)
    w_in_t, m_in_t, v_in_t = [jnp.transpose(a["w_in"], (2, 0, 1)) for a in (w, m, v)]
    w_in_slabs = _cast_w_in_t_to_slabs(w_in_t, chip1)
    pushes, tokens = [], []
    for l in range(depth):
        slabs = [w_in_slabs[l]] + [_cast_to_slab(w[k], l, chip1) for k in BIG[1:]]
        send_sems, recv_sems, _, lands, token = _push_start(slabs, slabs, "gather_chips_half", f"weights_start_{l}")
        pushes.append((send_sems, recv_sems, lands))
        tokens.append(token[0, 0])
    passed = [{} for _ in range(depth)]

    def stage(l, after, ids):
        send_sems, recv_sems, lands = pushes[l]
        tag = "_".join(BIG[i] for i in ids)
        sub = [lands[i] for i in ids]
        sub = _push_wait(send_sems, recv_sems, ids, sub, sub, "gather_chips_half", after, f"{tag}_wait_{l}")
        send_sems, recv_sems, _, sub, token = _push_start(sub, sub, "pass_halves", f"{tag}_pass_{l}")
        for j, i in enumerate(ids):
            passed[l][i] = (send_sems, recv_sems, sub[j], j)
        return token[0, 0]

    t_len = N_META + s_len
    pad = -t_len % SEQ_TILE
    h = jnp.concatenate([meta_full, x[0], jnp.zeros((pad, d), F32)], axis=0)
    tgt = jnp.concatenate([jnp.zeros((N_META, d), F32), loss_target[0], jnp.zeros((pad, d), F32)], axis=0)
    z = _rms_fwd(h, _row(p["attn_norm_g"][0] + sum(tokens)))
    stage(0, z, [0])
    saved = []
    for l in range(depth):
        def fetch(k, after, l=l):
            send_sems, recv_sems, land, j = passed[l][BIG.index(k)]
            return _push_wait(send_sems, recv_sems, [j], [land], [land], "pass_halves", after,
                              f"{k}_here_{l}")[0]

        def stage_next(after, which, l=l):
            if which == "own":
                return stage(l, after, [1, 2, 3])
            return stage(l + 1, after, [0]) if l + 1 < depth else 0.0

        g_next = p["attn_norm_g"][l + 1] if l + 1 < depth else p["final_g"]
        h, z, sv = _forward_layer(l, h, z, p, fetch, stage_next, g_next)
        saved.append(sv)
    dh, dh_b, dg_final, loss_part = _loss_bwd(h, _row(p["final_g"]), tgt, s_len)

    small = {k: [None] * depth for k in SMALL if k not in ("meta", "final_g")}
    pushes = [None] * depth
    tok = 0.0
    for l in reversed(range(depth)):
        dh, dh_b, big_mlp, sm_mlp = _backward_mlp(l, dh, dh_b, saved[l], p, tok)
        parts = [big_mlp["w_down"], big_mlp["w_up"]]
        push_mlp = _push_start(parts, [lax.empty(a.shape, a.dtype) for a in parts], "scatter_chips",
                               f"mlp_grads_start_{l}")
        dh, dh_b, big_mix, sm_mix = _backward_mixer(l, dh, dh_b, saved[l], p, push_mlp[4][0, 0])
        parts = [big_mix["w_out"], big_mix["w_in"]]
        push_mix = _push_start(parts, [lax.empty(a.shape, a.dtype) for a in parts], "scatter_chips",
                               f"mixer_grads_start_{l}")
        tok = push_mix[4][0, 0]
        pushes[l] = {("w_down", "w_up"): push_mlp, ("w_out", "w_in"): push_mix}
        for k, val in {**sm_mlp, **sm_mix}.items():
            small[k][l] = val
    grads = {k: jnp.stack(val) for k, val in small.items()}
    grads["final_g"] = dg_final[0]
    grads["meta"] = dh[:N_META]
    dx = dh[N_META:t_len]

    full_shapes = [grads[k].shape for k in SMALL] + [(1,)]
    packed = _pack([grads[k].astype(F32) for k in SMALL] + [loss_part[0, :1] + tok])
    dev1 = (2 * chip + lax.axis_index("c")).reshape(1).astype(jnp.int32)
    slabs = [_place_slab(packed, dev1, N_DEV)]
    small_push = _push_start(slabs, slabs, "gather_devices", "small_grads_start")

    last_token = small_push[4]
    outs = {k: [lax.empty(w[k].shape, F32) for _ in range(4)] for k in BIG[1:]}
    w_in_sums = [None] * depth
    swaps = {}

    def finish(l, wait_after, adam_after):
        send_sems, recv_sems, mine, lands, _ = swaps[l]
        mine, theirs = _push_wait(send_sems, recv_sems, list(range(len(BIG))), mine, lands, "sibling", wait_after,
                                  f"sums_wait_{l}")
        w_in_sums[l] = (mine[0], theirs[0])
        for k, a, b in zip(BIG[1:], mine[1:], theirs[1:]):
            outs[k] = _adamw_layer(w[k], m[k], v[k], l, a, b, outs[k], adam_after)

    for l in reversed(range(depth)):
        sums = {}
        for names, (send_sems, recv_sems, parts, lands, _) in pushes[l].items():
            parts, landed = _push_wait(send_sems, recv_sems, [0, 1], parts, lands, "scatter_chips", last_token,
                                       f"{names[0]}_grads_wait_{l}")
            for k, part, land in zip(names, parts, landed):
                sums[k] = _sum_partials(part, land, chip1)
        mine = [sums[k] for k in BIG]
        swaps[l] = _push_start(mine, [lax.empty(a.shape, a.dtype) for a in mine], "sibling", f"sums_start_{l}")
        if l + 1 < depth:
            finish(l + 1, outs["w_down"][0] if l + 2 < depth else swaps[l][4], swaps[l][4])
    finish(0, outs["w_down"][0] if depth > 1 else swaps[0][4], swaps[0][4])
    outs["w_in"] = [jnp.transpose(r, (1, 2, 0)) for r in _adamw_w_in_t(
        w_in_t, m_in_t, v_in_t, [s[0] for s in w_in_sums], [s[1] for s in w_in_sums])]
    out_g, out_d, out_m, out_v = [{k: outs[k][i] for k in BIG} for i in range(4)]

    landed = _push_wait(small_push[0], small_push[1], [0], small_push[3], small_push[3], "gather_devices",
                        out_g["w_in"], "small_grads_wait")
    total = _sum_slabs(landed[0])
    small_g = dict(zip(SMALL + ("loss",), _unpack(total, full_shapes)))
    for k in COL_SHARDED_SMALL:
        n = w[k].shape[-1]
        small_g[k] = lax.dynamic_slice_in_dim(small_g[k], chip * n, n, axis=small_g[k].ndim - 1)
    local_shapes = [w[k].shape for k in SMALL]
    res = _adamw(_pack([w[k] for k in SMALL]), _pack([small_g[k] for k in SMALL]),
                 _pack([m[k] for k in SMALL]), _pack([v[k] for k in SMALL]))
    out_g.update({k: small_g[k] for k in SMALL})
    for dst, buf in zip((out_d, out_m, out_v), res):
        dst.update(zip(SMALL, _unpack(buf, local_shapes)))

    return (small_g["loss"].reshape(()), dx[None],
            *[out_g[k] for k in WEIGHTS], *[out_d[k] for k in WEIGHTS],
            *[out_m[k] for k in WEIGHTS], *[out_v[k] for k in WEIGHTS])
```

```python
import functools
import math

import jax
import jax.numpy as jnp
from jax import lax
from jax.experimental import pallas as pl
from jax.experimental.pallas import tpu as pltpu

F32 = jnp.float32
BF16 = jnp.bfloat16

N_META = 16
HEAD_DIM = 64
N_REC_BLOCKS = 8
CONV_WIDTH = 4
RG_C = 8.0
NORM_EPS = 1e-6
ADAM_LR = 0.001
ADAM_B1 = 0.9
ADAM_B2 = 0.999
ADAM_EPS = 1e-08
ADAM_WD = 0.01
ADAM_STEP = 10

LANES = 128
SUBLANES = 8
SEQ_TILE = 128
VMEM_CAP = 60 * 2**20
VMEM_SLACK = 6 * 2**20
NEG_BIG = -1e30
N_CHIPS = 4
N_DEV = 8
MESH = pl.DeviceIdType.MESH


def _nbytes(shape, dtype):
    return math.prod(shape) * jnp.dtype(dtype).itemsize


def _call(body, args, *, name, out_shape, grid=(), in_specs=None, out_specs=None, scratch_shapes=(),
          grid_spec=None, semantics=None, vmem_bytes=None, side_effects=None, hbm_results=True, **kw):
    cp = {}
    if semantics is not None:
        cp["dimension_semantics"] = semantics
    if vmem_bytes is not None:
        cp["vmem_limit_bytes"] = int(min(VMEM_CAP, vmem_bytes + VMEM_SLACK))
    if side_effects is not None:
        cp["has_side_effects"] = side_effects
    if grid_spec is not None:
        kw["grid_spec"] = grid_spec
    else:
        kw.update(grid=grid, in_specs=in_specs, out_specs=out_specs, scratch_shapes=scratch_shapes)
    if hbm_results:
        out_shape = jax.tree.map(
            lambda s: pltpu.HBM(s.shape, s.dtype) if isinstance(s, jax.ShapeDtypeStruct) else s, out_shape)
    fn = pl.pallas_call(
        body, name=name, out_shape=out_shape,
        compiler_params=pltpu.CompilerParams(**cp), **kw)
    return fn(*[_in_hbm(a) if jnp.issubdtype(getattr(a, "dtype", jnp.int32), jnp.floating) else a for a in args])


def _divisor_tile(n, unit, target):
    best = None
    for t in range(unit, min(n, target) + 1, unit):
        if n % t == 0:
            best = t
    return n if best is None else best


def _sigmoid(x):
    return 1.0 / (1.0 + jnp.exp(-x))


def _log1p_unit(e):
    series = e * (1.0 - e * (0.5 - e * (1.0 / 3.0)))
    return jnp.where(e < 1e-2, series, jnp.log(1.0 + e))


def _log_sigmoid(x):
    return jnp.minimum(x, 0.0) - _log1p_unit(jnp.exp(-jnp.abs(x)))


def _one_minus_exp(x, exp_x):
    small = -x * (1.0 + x * (1.0 / 2 + x * (1.0 / 6 + x * (1.0 / 24 + x * (1.0 / 120 + x * (1.0 / 720))))))
    return jnp.where(x > -0.25, small, 1.0 - exp_x)


_GELU_K = math.sqrt(2.0 / math.pi)
_GELU_C = 0.044715


def _gelu_and_grad(y):
    th = jnp.tanh(_GELU_K * (y + _GELU_C * y * y * y))
    g = 0.5 * y * (1.0 + th)
    dg = 0.5 * (1.0 + th) + 0.5 * y * (1.0 - th * th) * _GELU_K * (1.0 + 3.0 * _GELU_C * y * y)
    return g, dg


def _rstd(x):
    return lax.rsqrt(jnp.mean(x * x, axis=-1, keepdims=True) + NORM_EPS)


def _rms_bwd(dz, x, g):
    rs = _rstd(x)
    xh = x * rs
    dgp = jnp.sum(dz * xh, axis=0, keepdims=True)
    dxh = dz * g
    dx = rs * (dxh - xh * jnp.mean(dxh * xh, axis=-1, keepdims=True))
    return dx, dgp


def _dot(a, b):
    return jnp.dot(a, b, preferred_element_type=F32)


def _dot_nt(a, b):
    return lax.dot_general(a, b, (((1,), (1,)), ((), ())), preferred_element_type=F32)


def _dot_tn(a, b):
    return lax.dot_general(a, b, (((0,), (0,)), ((), ())), preferred_element_type=F32)


def _full(shape):
    nd = len(shape)
    return pl.BlockSpec(shape, lambda *_: (0,) * nd)


def _rms_fwd(h, g):
    tp, d = h.shape
    tm = _divisor_tile(tp, 16, 544)

    def body(h_ref, g_ref, z_ref):
        x = h_ref[...]
        z_ref[...] = (x * _rstd(x) * g_ref[...]).astype(BF16)

    return _call(body, (h, g), name="rms_fwd", grid=(tp // tm,),
                 in_specs=[pl.BlockSpec((tm, d), lambda i: (i, 0)), _full((1, d))],
                 out_specs=pl.BlockSpec((tm, d), lambda i: (i, 0)),
                 out_shape=jax.ShapeDtypeStruct((tp, d), BF16), semantics=("parallel",))


def _proj(z, w_big_t, att_w):
    tp, d = z.shape
    nb = w_big_t.shape[0]
    tn = _divisor_tile(nb, LANES, 512)
    assert (3 * att_w) % tn == 0
    n_qkv = 3 * att_w // tn
    scale = 1.0 / math.sqrt(HEAD_DIM)

    def body(z_ref, w_ref, qkv_ref, p_ref):
        j = pl.program_id(0)
        acc = _dot_nt(z_ref[...], w_ref[...])

        @pl.when(j < n_qkv)
        def _():
            col = j * tn + lax.broadcasted_iota(jnp.int32, (1, tn), 1)
            qkv_ref[...] = (acc * jnp.where(col < att_w, scale, 1.0)).astype(BF16)

        @pl.when(j >= n_qkv)
        def _():
            p_ref[...] = acc

    vm = 2 * (_nbytes((tp, d), BF16) + _nbytes((d, tn), BF16) + _nbytes((tp, tn), F32) * 2)
    return _call(body, (z, w_big_t), name="proj", grid=(nb // tn,),
                 in_specs=[_full((tp, d)), pl.BlockSpec((tn, d), lambda j: (j, 0))],
                 out_specs=[pl.BlockSpec((tp, tn), lambda j: (0, jnp.minimum(j, n_qkv - 1))),
                            pl.BlockSpec((tp, tn), lambda j: (0, jnp.maximum(j - n_qkv, 0)))],
                 out_shape=[jax.ShapeDtypeStruct((tp, 3 * att_w), BF16),
                            jax.ShapeDtypeStruct((tp, nb - 3 * att_w), F32)],
                 semantics=("arbitrary",), vmem_bytes=vm)


def _tile_cumsum(x, row, reverse=False):
    for s in (1, 2, 4):
        if reverse:
            x = x + jnp.where(row < SUBLANES - s, pltpu.roll(x, SUBLANES - s, 0), 0.0)
        else:
            x = x + jnp.where(row >= s, pltpu.roll(x, s, 0), 0.0)
    return x


def _fgate_fwd(proj, b_f_pad, nh):
    tp, nb = proj.shape
    fblk = nb // LANES - 1

    def body(f_ref, b_ref, c_ref, ct_ref):
        b = b_ref[...]
        row = lax.broadcasted_iota(jnp.int32, (SUBLANES, LANES), 0)

        def step(i, carry):
            r0 = pl.multiple_of(i * SUBLANES, SUBLANES)
            lf = _log_sigmoid(f_ref[pl.ds(r0, SUBLANES), :] + b)
            x = _tile_cumsum(lf, row) + carry
            c_ref[pl.ds(r0, SUBLANES), :] = x
            return x[SUBLANES - 1:SUBLANES, :]

        lax.fori_loop(0, tp // SUBLANES, step, jnp.zeros((1, LANES), F32))
        ct_ref[...] = c_ref[...].T[:nh, :]

    return _call(body, (proj, b_f_pad), name="fgate_fwd", grid=(1,),
                 in_specs=[pl.BlockSpec((tp, LANES), lambda i: (0, fblk)), _full((1, LANES))],
                 out_specs=[_full((tp, LANES)), _full((nh, tp))],
                 out_shape=[jax.ShapeDtypeStruct((tp, LANES), F32), jax.ShapeDtypeStruct((nh, tp), F32)],
                 semantics=("arbitrary",))


def _fgate_bwd(proj, b_f_pad, dc):
    tp, nb = proj.shape
    fblk = nb // LANES - 1

    def body(f_ref, b_ref, dc_ref, df_ref, db_ref, dc_s):
        b = b_ref[...]
        row = lax.broadcasted_iota(jnp.int32, (SUBLANES, LANES), 0)
        nt = tp // SUBLANES

        def step(i, carry):
            suffix, acc = carry
            r0 = pl.multiple_of((nt - 1 - i) * SUBLANES, SUBLANES)
            dlf = _tile_cumsum(dc_ref[pl.ds(r0, SUBLANES), :], row, reverse=True) + suffix
            df = dlf * _sigmoid(-(f_ref[pl.ds(r0, SUBLANES), :] + b))
            dc_s[pl.ds(r0, SUBLANES), :] = df
            return dlf[0:1, :], acc + df

        _, acc = lax.fori_loop(0, nt, step, (jnp.zeros((1, LANES), F32), jnp.zeros((SUBLANES, LANES), F32)))
        df_ref[...] = dc_s[...].astype(BF16)
        db_ref[...] = jnp.broadcast_to(jnp.sum(acc, axis=0, keepdims=True), (SUBLANES, LANES))

    return _call(body, (proj, b_f_pad, dc), name="fgate_bwd", grid=(1,),
                 in_specs=[pl.BlockSpec((tp, LANES), lambda i: (0, fblk)), _full((1, LANES)), _full((tp, LANES))],
                 out_specs=[_full((tp, LANES)), _full((SUBLANES, LANES))],
                 out_shape=[jax.ShapeDtypeStruct((tp, LANES), BF16),
                            jax.ShapeDtypeStruct((SUBLANES, LANES), F32)],
                 scratch_shapes=[pltpu.VMEM((tp, LANES), F32)], semantics=("arbitrary",))


ATT_BQ = 128


ATT_BUCKET = 3
ATT_HEADS = 4


def _for_bucket(i, nq, fn):
    for lo in range(0, nq, ATT_BUCKET):
        hi = min(lo + ATT_BUCKET, nq)
        spans = ([(0, lo * ATT_BQ, False)] if lo else []) + [(lo * ATT_BQ, hi * ATT_BQ, True)]
        pl.when(jnp.logical_and(i >= lo, i < hi))(functools.partial(fn, spans))


def _head_column(c_blk, h):
    lane = lax.broadcasted_iota(jnp.int32, c_blk.shape, 1)
    return jnp.sum(jnp.where(lane == h, c_blk, 0.0), axis=1, keepdims=True)


def _head_columns_into(p, c_ref, ck_s):
    for hh in range(ATT_HEADS):
        ck_s[hh] = jnp.broadcast_to(_head_column(c_ref[...], ATT_HEADS * p + hh), ck_s.shape[1:])


def _pair_diag_cols(x2):
    top = lax.broadcasted_iota(jnp.int32, (LANES, ATT_BQ), 0) < HEAD_DIM
    xt = x2.astype(F32).T.astype(BF16)
    return jnp.concatenate([jnp.where(top, xt, 0), jnp.where(top, 0, xt)], axis=1)


def _pair_diag_rows(x2):
    low = lax.broadcasted_iota(jnp.int32, (ATT_BQ, LANES), 1) < HEAD_DIM
    return jnp.concatenate([jnp.where(low, x2, 0), jnp.where(low, 0, x2)], axis=0)


def _seen_keys(k0, k1, q0):
    keys = k0 + lax.broadcasted_iota(jnp.int32, (k1 - k0, ATT_BQ), 0)
    return keys <= q0 + lax.broadcasted_iota(jnp.int32, (k1 - k0, ATT_BQ), 1)


def _attn_fwd(qkv, c, c_t, nh):
    tp = qkv.shape[0]
    att_w = nh * HEAD_DIM
    ng = nh // ATT_HEADS
    gw = ATT_HEADS * HEAD_DIM
    bq = ATT_BQ
    nq = tp // bq
    pair = 2 * HEAD_DIM
    assert pair == LANES and ATT_HEADS % 2 == 0

    def body(q_ref, k_ref, v_ref, c_ref, ct_ref, o_ref, lse_ref, ck_s, vt_s):
        p = pl.program_id(0)
        i = pl.program_id(1)

        @pl.when(i == 0)
        def _():
            _head_columns_into(p, c_ref, ck_s)
            vt_s[...] = v_ref[...].astype(F32).T.astype(BF16)

        def compute(spans):
            q0 = pl.multiple_of(i * bq, bq)
            o_t, lses = [], []
            for pi in range(ATT_HEADS // 2):
                lo = pair * pi
                heads = (2 * pi, 2 * pi + 1)
                q_cols = _pair_diag_cols(q_ref[:, lo:lo + pair])
                ts = []
                for k0, k1, needs_mask in spans:
                    t2 = _dot(k_ref[k0:k1, lo:lo + pair], q_cols)
                    t_e = [t2[:, e * bq:(e + 1) * bq] - ck_s[hh, k0:k1, :] for e, hh in enumerate(heads)]
                    if needs_mask:
                        seen = _seen_keys(k0, k1, q0)
                        t_e = [jnp.where(seen, t, NEG_BIG) for t in t_e]
                    ts.append(t_e)
                ms = [functools.reduce(jnp.maximum, [jnp.max(t[e], axis=0, keepdims=True) for t in ts])
                      for e in range(2)]
                es = [[jnp.exp(t[e] - ms[e]) for e in range(2)] for t in ts]
                ls = [sum(jnp.sum(e_[e], axis=0, keepdims=True) for e_ in es) for e in range(2)]
                o2 = sum(_dot(vt_s[lo:lo + pair, k0:k1],
                              jnp.concatenate([e_[0].astype(BF16), e_[1].astype(BF16)], axis=1))
                         for e_, (k0, k1, _) in zip(es, spans))
                o_t += [o2[:HEAD_DIM, :bq] / ls[0], o2[HEAD_DIM:, bq:] / ls[1]]
                lses += [ms[e] + ct_ref[pl.ds(ATT_HEADS * p + hh, 1), :] + jnp.log(ls[e])
                         for e, hh in enumerate(heads)]
            o_ref[...] = jnp.concatenate(o_t, axis=0).T
            lse_ref[...] = jnp.concatenate(lses, axis=0)

        _for_bucket(i, nq, compute)

    blk = pl.BlockSpec((bq, gw), lambda p, i: (i, p))
    vm = 6 * _nbytes((tp, gw), BF16) + 2 * ATT_HEADS * _nbytes((tp, LANES), F32) + 2 * _nbytes((tp, LANES), F32) \
        + 8 * ATT_HEADS * _nbytes((bq, tp), F32)
    return _call(body, (qkv, qkv, qkv, c, c_t), name="attn_fwd", grid=(ng, nq),
                 in_specs=[blk,
                           pl.BlockSpec((tp, gw), lambda p, i: (0, ng + p)),
                           pl.BlockSpec((tp, gw), lambda p, i: (0, 2 * ng + p)),
                           _full((tp, LANES)), pl.BlockSpec((nh, bq), lambda p, i: (0, i))],
                 out_specs=[blk, pl.BlockSpec((None, ATT_HEADS, bq), lambda p, i: (p, 0, i))],
                 out_shape=[jax.ShapeDtypeStruct((tp, att_w), F32), jax.ShapeDtypeStruct((ng, ATT_HEADS, tp), F32)],
                 scratch_shapes=[pltpu.VMEM((ATT_HEADS, tp, LANES), F32), pltpu.VMEM((gw, tp), BF16)],
                 semantics=("arbitrary", "arbitrary"), vmem_bytes=vm)


def _attn_bwd(qkv, c, c_t, lse, do, nh):
    tp = qkv.shape[0]
    att_w = nh * HEAD_DIM
    ng = nh // ATT_HEADS
    gw = ATT_HEADS * HEAD_DIM
    bq = ATT_BQ
    nq = tp // bq
    pair = 2 * HEAD_DIM
    assert pair == LANES and ATT_HEADS % 2 == 0
    scale = 1.0 / math.sqrt(HEAD_DIM)

    def body(q_ref, k_ref, v_ref, c_ref, ct_ref, lse_ref, do_ref, dq_ref, dk_ref, dv_ref, dc_ref,
             dk_s, dv_s, dc_s, ck_s, kt_s):
        p = pl.program_id(0)
        i = pl.program_id(1)

        @pl.when(i == 0)
        def _():
            dk_s[...] = jnp.zeros_like(dk_s)
            dv_s[...] = jnp.zeros_like(dv_s)
            dc_s[...] = jnp.zeros_like(dc_s)
            kt_s[...] = k_ref[...].astype(F32).T.astype(BF16)
            _head_columns_into(p, c_ref, ck_s)

        @pl.when(jnp.logical_and(i == 0, p == 0))
        def _():
            dc_ref[...] = jnp.zeros_like(dc_ref)

        def compute(spans):
            q0 = pl.multiple_of(i * bq, bq)
            dq_t = []
            for pi in range(ATT_HEADS // 2):
                lo = pair * pi
                q2 = q_ref[:, lo:lo + pair]
                do2 = do_ref[:, lo:lo + pair].astype(BF16)
                q_cols, do_cols = _pair_diag_cols(q2), _pair_diag_cols(do2)
                q_rows, do_rows = _pair_diag_rows(q2), _pair_diag_rows(do2)
                heads = (2 * pi, 2 * pi + 1)
                col_terms = [ct_ref[pl.ds(ATT_HEADS * p + hh, 1), :] - lse_ref[hh:hh + 1, :] for hh in heads]
                prs, dps = [], []
                for k0, k1, needs_mask in spans:
                    t2 = _dot(k_ref[k0:k1, lo:lo + pair], q_cols)
                    dp2 = _dot(v_ref[k0:k1, lo:lo + pair], do_cols)
                    if needs_mask:
                        seen = _seen_keys(k0, k1, q0)
                    pr_e, dp_e = [], []
                    for e, hh in enumerate(heads):
                        t = t2[:, e * bq:(e + 1) * bq] - ck_s[hh, k0:k1, :]
                        if needs_mask:
                            t = jnp.where(seen, t, NEG_BIG)
                        pr_e.append(jnp.exp(t + col_terms[e]))
                        dp_e.append(dp2[:, e * bq:(e + 1) * bq])
                    prs.append(pr_e)
                    dps.append(dp_e)
                key_sums = [sum(jnp.sum(pr[e] * dp[e], axis=0, keepdims=True) for pr, dp in zip(prs, dps))
                            for e in range(2)]
                dq2 = 0.0
                for (k0, k1, _), pr, dp in zip(spans, prs, dps):
                    ds = [pr[e] * (dp[e] - key_sums[e]) for e in range(2)]
                    for e, hh in enumerate(heads):
                        dc_s[hh, k0:k1, :] += jnp.sum(ds[e], axis=1, keepdims=True)
                    ds2 = jnp.concatenate([ds[0].astype(BF16), ds[1].astype(BF16)], axis=1)
                    pr2 = jnp.concatenate([pr[0].astype(BF16), pr[1].astype(BF16)], axis=1)
                    dk_s[k0:k1, lo:lo + pair] += _dot(ds2, q_rows)
                    dv_s[k0:k1, lo:lo + pair] += _dot(pr2, do_rows)
                    dq2 = dq2 + _dot(kt_s[lo:lo + pair, k0:k1], ds2)
                dq_t.append(jnp.concatenate([dq2[:HEAD_DIM, :bq], dq2[HEAD_DIM:, bq:]], axis=0))
            dq_ref[...] = (jnp.concatenate(dq_t, axis=0) * scale).T.astype(BF16)

        _for_bucket(i, nq, compute)

        @pl.when(i == nq - 1)
        def _():
            dk_ref[...] = dk_s[...].astype(BF16)
            dv_ref[...] = dv_s[...].astype(BF16)
            lane = lax.broadcasted_iota(jnp.int32, (tp, LANES), 1)
            dc = dc_ref[...]
            for hh in range(ATT_HEADS):
                dc = jnp.where(lane == ATT_HEADS * p + hh, -dc_s[hh], dc)
            dc_ref[...] = dc

    blk = pl.BlockSpec((bq, gw), lambda p, i: (i, p))
    col = pl.BlockSpec((tp, gw), lambda p, i: (0, p))
    vm = 7 * _nbytes((tp, gw), BF16) + 2 * _nbytes((tp, gw), F32) + 2 * ATT_HEADS * _nbytes((tp, LANES), F32) \
        + 2 * _nbytes((tp, LANES), F32) + 12 * ATT_HEADS * _nbytes((bq, tp), F32)
    return _call(body, (qkv, qkv, qkv, c, c_t, lse, do), name="attn_bwd", grid=(ng, nq),
                 in_specs=[blk,
                           pl.BlockSpec((tp, gw), lambda p, i: (0, ng + p)),
                           pl.BlockSpec((tp, gw), lambda p, i: (0, 2 * ng + p)),
                           _full((tp, LANES)), pl.BlockSpec((nh, bq), lambda p, i: (0, i)),
                           pl.BlockSpec((None, ATT_HEADS, bq), lambda p, i: (p, 0, i)), blk],
                 out_specs=[blk, col, col, _full((tp, LANES))],
                 out_shape=[jax.ShapeDtypeStruct((tp, att_w), BF16)] * 3 + [jax.ShapeDtypeStruct((tp, LANES), F32)],
                 scratch_shapes=[pltpu.VMEM((tp, gw), F32), pltpu.VMEM((tp, gw), F32),
                                 pltpu.VMEM((ATT_HEADS, tp, 1), F32), pltpu.VMEM((ATT_HEADS, tp, LANES), F32),
                                 pltpu.VMEM((gw, tp), BF16)],
                 semantics=("arbitrary", "arbitrary"), vmem_bytes=vm)


REC_ROWS = 128
HALO = SUBLANES


def _conv_taps(cat):
    taps = []
    for k in range(CONV_WIDTH):
        sh = CONV_WIDTH - 1 - k
        taps.append((pltpu.roll(cat, sh, 0) if sh else cat)[HALO:])
    return taps


def _rec_gates(xc, wa_ref, ba_ref, wx_ref, bx_ref, l_ref):
    xcb = xc.astype(BF16)
    r = _sigmoid(_dot(xcb, wa_ref[...]) + ba_ref[...])
    ig = _sigmoid(_dot(xcb, wx_ref[...]) + bx_ref[...])
    ls = _log_sigmoid(l_ref[...])
    log_a = RG_C * r * ls
    return xcb, r, ig, ls, log_a


def _rec_fwd(proj, xr_blk, yr_blk, rec_w, conv_w, conv_b, wa, ba, wx, bx, lru):
    tp = proj.shape[0]
    w = rec_w
    r_rows = REC_ROWS
    nc = tp // r_rows
    cpb = w // LANES

    def body(xr_ref, yr_ref, cw_ref, cb_ref, wa_ref, ba_ref, wx_ref, bx_ref, l_ref,
             hr_ref, rec_ref, prev_s, carry_s, a_s, u_s):
        i = pl.program_id(0)

        @pl.when(i == 0)
        def _():
            prev_s[...] = jnp.zeros_like(prev_s)
            carry_s[...] = jnp.zeros_like(carry_s)

        x = xr_ref[...]
        taps = _conv_taps(jnp.concatenate([prev_s[...], x], axis=0))
        prev_s[...] = x[r_rows - HALO:]
        xc = cb_ref[...]
        for k in range(CONV_WIDTH):
            xc = xc + cw_ref[k:k + 1, :] * taps[k]
        _, r, ig, ls, log_a = _rec_gates(xc, wa_ref, ba_ref, wx_ref, bx_ref, l_ref)
        a = jnp.exp(log_a)
        a_s[...] = a
        u_s[...] = jnp.sqrt(_one_minus_exp(2.0 * log_a, a * a)) * ig * xc

        def tile(j, h):
            r0 = pl.multiple_of(j * SUBLANES, SUBLANES)
            at = a_s[pl.ds(r0, SUBLANES), :]
            ut = u_s[pl.ds(r0, SUBLANES), :]
            out = []
            for rr in range(SUBLANES):
                h = at[rr:rr + 1] * h + ut[rr:rr + 1]
                out.append(h)
            hr_ref[pl.ds(r0, SUBLANES), :] = jnp.concatenate(out, axis=0)
            return h

        carry_s[0:1, :] = lax.fori_loop(0, r_rows // SUBLANES, tile, carry_s[0:1, :])
        g, _ = _gelu_and_grad(yr_ref[...])
        rec_ref[...] = hr_ref[...] * g

    blk = pl.BlockSpec((r_rows, w), lambda i: (i, 0))
    vm = 16 * _nbytes((r_rows, w), F32) + 4 * _nbytes((w, w), BF16)
    return _call(body, (proj, proj, conv_w, conv_b, wa, ba, wx, bx, lru), name="rec_fwd", grid=(nc,),
                 in_specs=[pl.BlockSpec((r_rows, w), lambda i: (i, xr_blk)),
                           pl.BlockSpec((r_rows, w), lambda i: (i, yr_blk)),
                           _full((CONV_WIDTH, w)), _full((1, w)), _full((w, w)), _full((1, w)),
                           _full((w, w)), _full((1, w)), _full((1, w))],
                 out_specs=[blk, blk],
                 out_shape=[jax.ShapeDtypeStruct((tp, w), F32)] * 2,
                 scratch_shapes=[pltpu.VMEM((HALO, w), F32), pltpu.VMEM((SUBLANES, w), F32),
                                 pltpu.VMEM((r_rows, w), F32), pltpu.VMEM((r_rows, w), F32)],
                 semantics=("arbitrary",), vmem_bytes=vm)


def _rec_bwd(proj, xr_blk, yr_blk, rec_w, hr, drec, conv_w, conv_b, wa, ba, wx, bx, lru):
    tp = proj.shape[0]
    w = rec_w
    r_rows = REC_ROWS
    nc = tp // r_rows
    hpc = r_rows // HALO

    def body(xr_ref, xh_ref, yr_ref, hr_ref, hh_ref, drec_ref, cw_ref, cb_ref, wa_ref, ba_ref, wx_ref, bx_ref,
             l_ref, dxr_ref, dyr_ref, dwa_ref, dwx_ref, small_ref, lam_s, a_s, dhr_s, carry_s, next_s):
        i = pl.program_id(0)
        first = (nc - 1 - i) == 0

        @pl.when(i == 0)
        def _():
            carry_s[...] = jnp.zeros_like(carry_s)
            next_s[...] = jnp.zeros_like(next_s)
            dwa_ref[...] = jnp.zeros_like(dwa_ref)
            dwx_ref[...] = jnp.zeros_like(dwx_ref)
            small_ref[...] = jnp.zeros_like(small_ref)

        x = xr_ref[...]
        xprev = jnp.where(first, 0.0, xh_ref[...])
        taps = _conv_taps(jnp.concatenate([xprev, x], axis=0))
        xc = cb_ref[...]
        for k in range(CONV_WIDTH):
            xc = xc + cw_ref[k:k + 1, :] * taps[k]
        xcb, r, ig, ls, log_a = _rec_gates(xc, wa_ref, ba_ref, wx_ref, bx_ref, l_ref)
        a = jnp.exp(log_a)
        a2 = a * a
        mult = jnp.sqrt(_one_minus_exp(2.0 * log_a, a2))
        g, dg = _gelu_and_grad(yr_ref[...])
        hr_v = hr_ref[...]
        drec_v = drec_ref[...]
        dhr_s[...] = drec_v * g
        dyr_ref[...] = (drec_v * hr_v * dg).astype(BF16)
        a_s[...] = a

        def tile(jj, carry):
            r0 = pl.multiple_of((r_rows // SUBLANES - 1 - jj) * SUBLANES, SUBLANES)
            at = a_s[pl.ds(r0, SUBLANES), :]
            dt = dhr_s[pl.ds(r0, SUBLANES), :]
            out = [None] * SUBLANES
            for rr in range(SUBLANES - 1, -1, -1):
                lam = dt[rr:rr + 1] + carry
                out[rr] = lam
                carry = at[rr:rr + 1] * lam
            lam_s[pl.ds(r0, SUBLANES), :] = jnp.concatenate(out, axis=0)
            return carry

        carry_s[0:1, :] = lax.fori_loop(0, r_rows // SUBLANES, tile, carry_s[0:1, :])
        lam = lam_s[...]
        hprev = jnp.where(first, 0.0, hh_ref[...])
        hr_prev = pltpu.roll(jnp.concatenate([hprev, hr_v], axis=0), 1, 0)[HALO:]
        da = lam * hr_prev
        dxc = lam * mult * ig
        di = lam * mult * xc
        dmult = lam * ig * xc
        dlog_a = da * a - dmult * a2 / mult
        dr = dlog_a * (RG_C * ls)
        dls = jnp.sum(dlog_a * (RG_C * r), axis=0, keepdims=True)
        dga = dr * r * (1.0 - r)
        dgx = di * ig * (1.0 - ig)
        dgab = dga.astype(BF16)
        dgxb = dgx.astype(BF16)
        dxc = dxc + _dot_nt(dgab, wa_ref[...]) + _dot_nt(dgxb, wx_ref[...])
        dwa_ref[...] += _dot_tn(xcb, dgab)
        dwx_ref[...] += _dot_tn(xcb, dgxb)
        cat = jnp.concatenate([dxc, next_s[...]], axis=0)
        next_s[...] = dxc[0:HALO]
        dxr = cw_ref[CONV_WIDTH - 1:CONV_WIDTH, :] * dxc
        for k in range(CONV_WIDTH - 1):
            sh = CONV_WIDTH - 1 - k
            dxr = dxr + cw_ref[k:k + 1, :] * pltpu.roll(cat, r_rows + HALO - sh, 0)[:r_rows]
        dxr_ref[...] = dxr.astype(BF16)
        rows = [jnp.sum(dxc * taps[k], axis=0, keepdims=True) for k in range(CONV_WIDTH)]
        rows += [jnp.sum(dxc, axis=0, keepdims=True), jnp.sum(dga, axis=0, keepdims=True),
                 jnp.sum(dgx, axis=0, keepdims=True), dls * _sigmoid(-l_ref[...])]
        small_ref[...] += jnp.concatenate(rows, axis=0)

    def rev(i):
        return nc - 1 - i

    def halo(i):
        return jnp.maximum(rev(i) * hpc - 1, 0)

    blk = pl.BlockSpec((r_rows, w), lambda i: (rev(i), 0))
    vm = 40 * _nbytes((r_rows, w), F32) + 6 * _nbytes((w, w), F32)
    return _call(body, (proj, proj, proj, hr, hr, drec, conv_w, conv_b, wa, ba, wx, bx, lru),
                 name="rec_bwd", grid=(nc,),
                 in_specs=[pl.BlockSpec((r_rows, w), lambda i: (rev(i), xr_blk)),
                           pl.BlockSpec((HALO, w), lambda i: (halo(i), xr_blk)),
                           pl.BlockSpec((r_rows, w), lambda i: (rev(i), yr_blk)),
                           blk,
                           pl.BlockSpec((HALO, w), lambda i: (halo(i), 0)),
                           blk,
                           _full((CONV_WIDTH, w)), _full((1, w)), _full((w, w)), _full((1, w)),
                           _full((w, w)), _full((1, w)), _full((1, w))],
                 out_specs=[blk, blk, _full((w, w)), _full((w, w)), _full((SUBLANES, w))],
                 out_shape=[jax.ShapeDtypeStruct((tp, w), BF16)] * 2
                 + [jax.ShapeDtypeStruct((w, w), F32)] * 2 + [jax.ShapeDtypeStruct((SUBLANES, w), F32)],
                 scratch_shapes=[pltpu.VMEM((r_rows, w), F32)] * 3
                 + [pltpu.VMEM((SUBLANES, w), F32), pltpu.VMEM((HALO, w), F32)],
                 semantics=("arbitrary",), vmem_bytes=vm)


ROW_TARGET = 544


def _mixer_out(attn, rec, g_a, g_r, w_out, h, g_next):
    tp, d = h.shape
    aw, rw = attn.shape[1], rec.shape[1]
    kc = d // N_CHIPS
    tm = _divisor_tile(tp, 16, ROW_TARGET)

    def body(a_ref, r_ref, ga_ref, gr_ref, w_ref, h_ref, gn_ref, h1_ref, z_ref, mix_ref):
        a = a_ref[...]
        r = r_ref[...]
        mix = jnp.concatenate([a * _rstd(a) * ga_ref[...], r * _rstd(r) * gr_ref[...]], axis=1).astype(BF16)
        mix_ref[...] = mix
        h1 = h_ref[...]
        for j in range(N_CHIPS):
            h1 = h1 + _dot(mix[:, j * kc:(j + 1) * kc], w_ref[j])
        h1_ref[...] = h1
        z_ref[...] = (h1 * _rstd(h1) * gn_ref[...]).astype(BF16)

    row = lambda wd: pl.BlockSpec((tm, wd), lambda i: (i, 0))
    vm = 2 * _nbytes((d, d), BF16) + 12 * _nbytes((tm, d), F32)
    return _call(body, (attn, rec, g_a, g_r, w_out, h, g_next), name="mixer_out", grid=(tp // tm,),
                 in_specs=[row(aw), row(rw), _full((1, aw)), _full((1, rw)), _full(w_out.shape), row(d),
                           _full((1, d))],
                 out_specs=[row(d), row(d), row(d)],
                 out_shape=[jax.ShapeDtypeStruct((tp, d), F32), jax.ShapeDtypeStruct((tp, d), BF16),
                            jax.ShapeDtypeStruct((tp, d), BF16)],
                 semantics=("parallel",), vmem_bytes=vm)


def _mixer_bwd(dh_b, w_out, attn, rec, g_a, g_r):
    tp, d = dh_b.shape
    aw, rw = attn.shape[1], rec.shape[1]
    tm = _divisor_tile(tp, 16, ROW_TARGET)

    def body(dh_ref, w_ref, a_ref, r_ref, ga_ref, gr_ref, da_ref, dr_ref, dg_ref):
        @pl.when(pl.program_id(0) == 0)
        def _():
            dg_ref[...] = jnp.zeros_like(dg_ref)

        dh = dh_ref[...]
        dmix = jnp.concatenate([_dot_nt(dh, w_ref[j]) for j in range(N_CHIPS)], axis=1)
        da, dga = _rms_bwd(dmix[:, :aw], a_ref[...], ga_ref[...])
        dr, dgr = _rms_bwd(dmix[:, aw:], r_ref[...], gr_ref[...])
        da_ref[...] = da
        dr_ref[...] = dr
        dg_ref[...] += jnp.broadcast_to(jnp.concatenate([dga, dgr], axis=1), (SUBLANES, d))

    row = lambda wd: pl.BlockSpec((tm, wd), lambda i: (i, 0))
    vm = 2 * _nbytes((d, d), BF16) + 12 * _nbytes((tm, d), F32)
    return _call(body, (dh_b, w_out, attn, rec, g_a, g_r), name="mixer_bwd", grid=(tp // tm,),
                 in_specs=[row(d), _full(w_out.shape), row(aw), row(rw), _full((1, aw)), _full((1, rw))],
                 out_specs=[row(aw), row(rw), _full((SUBLANES, d))],
                 out_shape=[jax.ShapeDtypeStruct((tp, aw), F32), jax.ShapeDtypeStruct((tp, rw), F32),
                            jax.ShapeDtypeStruct((SUBLANES, d), F32)],
                 semantics=("arbitrary",), vmem_bytes=vm)


def _mlp_up(z, w_up):
    tp, d = z.shape
    fc = w_up.shape[2]
    ff = N_CHIPS * fc
    tn = _divisor_tile(fc, LANES, 512)
    per = fc // tn

    def body(z_ref, w_ref, act_ref, up_ref):
        up = _dot(z_ref[...], w_ref[...])
        r = jnp.maximum(up, 0.0)
        act_ref[...] = (r * r).astype(BF16)
        up_ref[...] = up.astype(BF16)

    col = pl.BlockSpec((tp, tn), lambda j: (0, j))
    vm = 2 * _nbytes((tp, d), BF16) + 2 * _nbytes((d, tn), BF16) + 8 * _nbytes((tp, tn), F32)
    return _call(body, (z, w_up), name="mlp_up", grid=(ff // tn,),
                 in_specs=[_full((tp, d)), pl.BlockSpec((None, d, tn), lambda j: (j // per, 0, j % per))],
                 out_specs=[col, col],
                 out_shape=[jax.ShapeDtypeStruct((tp, ff), BF16)] * 2,
                 semantics=("parallel",), vmem_bytes=vm)


def _mlp_down(act, w_down, h, g_next):
    tp, d = h.shape
    ff = act.shape[1]
    fc = ff // N_CHIPS
    tm = _divisor_tile(tp, 16, ROW_TARGET)

    def body(a_ref, w_ref, h_ref, gn_ref, h2_ref, z_ref):
        h2 = h_ref[...]
        for j in range(N_CHIPS):
            h2 = h2 + _dot(a_ref[:, j * fc:(j + 1) * fc], w_ref[j])
        h2_ref[...] = h2
        z_ref[...] = (h2 * _rstd(h2) * gn_ref[...]).astype(BF16)

    row = lambda wd: pl.BlockSpec((tm, wd), lambda i: (i, 0))
    vm = 2 * _nbytes((ff, d), BF16) + 2 * _nbytes((tm, ff), BF16) + 10 * _nbytes((tm, d), F32)
    return _call(body, (act, w_down, h, g_next), name="mlp_down", grid=(tp // tm,),
                 in_specs=[row(ff), _full(w_down.shape), row(d), _full((1, d))],
                 out_specs=[row(d), row(d)],
                 out_shape=[jax.ShapeDtypeStruct((tp, d), F32), jax.ShapeDtypeStruct((tp, d), BF16)],
                 semantics=("parallel",), vmem_bytes=vm)


def _loss_bwd(h, g, target, n_real):
    tp, d = h.shape
    tm = _divisor_tile(tp, 16, ROW_TARGET)

    def body(h_ref, g_ref, t_ref, dh_ref, dhb_ref, dg_ref, loss_ref):
        i = pl.program_id(0)

        @pl.when(i == 0)
        def _():
            dg_ref[...] = jnp.zeros_like(dg_ref)
            loss_ref[...] = jnp.zeros_like(loss_ref)

        x = h_ref[...]
        gv = g_ref[...]
        rowi = i * tm + lax.broadcasted_iota(jnp.int32, (tm, 1), 0)
        real = jnp.logical_and(rowi >= N_META, rowi < N_META + n_real)
        err = jnp.where(real, x * _rstd(x) * gv - t_ref[...], 0.0)
        loss_ref[...] += 0.5 * jnp.sum(jnp.mean(err * err, axis=-1, keepdims=True))
        dx, dgp = _rms_bwd(err * (1.0 / d), x, gv)
        dh_ref[...] = dx
        dhb_ref[...] = dx.astype(BF16)
        dg_ref[...] += jnp.broadcast_to(dgp, (SUBLANES, d))

    row = pl.BlockSpec((tm, d), lambda i: (i, 0))
    return _call(body, (h, g, target), name="loss_bwd", grid=(tp // tm,),
                 in_specs=[row, _full((1, d)), row],
                 out_specs=[row, row, _full((SUBLANES, d)), _full((SUBLANES, LANES))],
                 out_shape=[jax.ShapeDtypeStruct((tp, d), F32), jax.ShapeDtypeStruct((tp, d), BF16),
                            jax.ShapeDtypeStruct((SUBLANES, d), F32), jax.ShapeDtypeStruct((SUBLANES, LANES), F32)],
                 semantics=("arbitrary",), vmem_bytes=16 * _nbytes((tm, d), F32))


def _mlp_bwd(dh_b, w_down, up, z2):
    tp, d = dh_b.shape
    fc = w_down.shape[1]
    ff = N_CHIPS * fc
    tn = _divisor_tile(fc, LANES, 512)
    per = fc // tn

    def body(dh_ref, z_ref, w_ref, up_ref, dup_ref, gd_ref, gu_ref):
        dh = dh_ref[...]
        r = jnp.maximum(up_ref[...].astype(F32), 0.0)
        dup = (_dot_nt(dh, w_ref[...]) * (2.0 * r)).astype(BF16)
        dup_ref[...] = dup
        gd_ref[...] = _dot_tn((r * r).astype(BF16), dh).astype(BF16)
        gu_ref[...] = _dot_tn(z_ref[...], dup).astype(BF16)

    col = pl.BlockSpec((tp, tn), lambda j: (0, j))
    vm = 4 * _nbytes((tp, d), BF16) + 4 * _nbytes((tn, d), BF16) + 2 * _nbytes((d, tn), BF16) \
        + 10 * _nbytes((tp, tn), F32) + 4 * _nbytes((tn, d), F32)
    return _call(body, (dh_b, z2, w_down, up), name="mlp_bwd", grid=(ff // tn,),
                 in_specs=[_full((tp, d)), _full((tp, d)),
                           pl.BlockSpec((None, tn, d), lambda j: (j // per, j % per, 0)), col],
                 out_specs=[col, pl.BlockSpec((tn, d), lambda j: (j, 0)),
                            pl.BlockSpec((None, d, tn), lambda j: (j // per, 0, j % per))],
                 out_shape=[jax.ShapeDtypeStruct((tp, ff), BF16), jax.ShapeDtypeStruct((ff, d), BF16),
                            jax.ShapeDtypeStruct((N_CHIPS, d, fc), BF16)],
                 semantics=("parallel",), vmem_bytes=vm)


def _grad_w_pieces(pieces, b):
    tp, n = b.shape
    tn = _divisor_tile(n, LANES, 512)
    widths = [pc.shape[1] for pc in pieces]

    def body(*refs):
        p_refs, b_ref, o_refs = refs[:len(pieces)], refs[len(pieces)], refs[len(pieces) + 1:]
        for p_ref, o_ref in zip(p_refs, o_refs):
            o_ref[...] = _dot_tn(p_ref[...], b_ref[...]).astype(BF16)

    vm = 2 * sum(_nbytes((tp, wd), BF16) for wd in widths) + 2 * _nbytes((tp, tn), BF16) \
        + 4 * sum(_nbytes((wd, tn), F32) for wd in widths) + 2 * _nbytes((tp, max(widths)), F32)
    return _call(body, tuple(pieces) + (b,), name="grad_w_pieces", grid=(n // tn,),
                 in_specs=[_full(pc.shape) for pc in pieces] + [pl.BlockSpec((tp, tn), lambda j: (0, j))],
                 out_specs=[pl.BlockSpec((wd, tn), lambda j: (0, j)) for wd in widths],
                 out_shape=[jax.ShapeDtypeStruct((wd, n), BF16) for wd in widths],
                 semantics=("parallel",), vmem_bytes=vm)


def _dx_norm_bwd(pieces, w, w_spec, w_piece, h, g, dres, dot=_dot_nt):
    tp, d = h.shape
    tm = _divisor_tile(tp, 16, ROW_TARGET)
    n = len(pieces)

    def body(*refs):
        dy_refs = refs[:n]
        w_ref, h_ref, g_ref, dres_ref, dh_ref, dhb_ref, dg_ref = refs[n:]

        @pl.when(pl.program_id(0) == 0)
        def _():
            dg_ref[...] = jnp.zeros_like(dg_ref)

        dz = dot(dy_refs[0][...], w_piece(w_ref, 0))
        for i in range(1, n):
            dz = dz + dot(dy_refs[i][...], w_piece(w_ref, i))
        dx, dgp = _rms_bwd(dz, h_ref[...], g_ref[...])
        dh = dres_ref[...] + dx
        dh_ref[...] = dh
        dhb_ref[...] = dh.astype(BF16)
        dg_ref[...] += jnp.broadcast_to(dgp, (SUBLANES, d))

    row = lambda wd: pl.BlockSpec((tm, wd), lambda i: (i, 0))
    kk = sum(wd for _, _, wd in pieces)
    vm = 2 * _nbytes((d, kk), BF16) + 2 * _nbytes((tm, kk), BF16) + 14 * _nbytes((tm, d), F32)
    piece_specs = [pl.BlockSpec((tm, wd), functools.partial(lambda i, cb: (i, cb), cb=cb)) for _, cb, wd in pieces]
    return _call(body, tuple(a for a, _, _ in pieces) + (w, h, g, dres), name="dx_norm_bwd", grid=(tp // tm,),
                 in_specs=piece_specs + [w_spec, row(d), _full((1, d)), row(d)],
                 out_specs=[row(d), row(d), _full((SUBLANES, d))],
                 out_shape=[jax.ShapeDtypeStruct((tp, d), F32), jax.ShapeDtypeStruct((tp, d), BF16),
                            jax.ShapeDtypeStruct((SUBLANES, d), F32)],
                 semantics=("arbitrary",), vmem_bytes=vm)


def _block_diag(wg):
    nb, b, _ = wg.shape
    eye = jnp.eye(nb, dtype=wg.dtype)
    return (eye[:, None, :, None] * wg[:, :, None, :]).reshape(nb * b, nb * b)


def _diag_blocks(dense, nb):
    b = dense.shape[0] // nb
    d4 = dense.reshape(nb, b, nb, b)
    return jnp.stack([d4[i, :, i, :] for i in range(nb)])


def _row(v):
    return v.reshape(1, -1)


def _forward_layer(l, h, z, p, fetch, stage_next, g_next):
    d = h.shape[1]
    att_w = d // 2
    rec_w = d - att_w
    nh = att_w // HEAD_DIM
    wa_d = _block_diag(p["w_gate_a"][l]).astype(BF16)
    wx_d = _block_diag(p["w_gate_x"][l]).astype(BF16)
    b_f_pad = jnp.zeros((1, LANES), F32).at[0, :nh].set(p["b_f"][l])
    w_in_t = fetch("w_in", z)
    big = dict(w_in_big=_pack_w_in_t(w_in_t.reshape(-1, d), att_w, nh))
    qkv, proj = _proj(z, big["w_in_big"], att_w)
    c, c_t = _fgate_fwd(proj, b_f_pad, nh)
    attn, lse_b = _attn_fwd(qkv, c, c_t, nh)
    hr, rec = _rec_fwd(proj, 0, 1, rec_w, p["conv_w"][l], _row(p["conv_b"][l]), wa_d,
                       _row(p["b_gate_a"][l]), wx_d, _row(p["b_gate_x"][l]), _row(p["lru_L"][l]))
    tok = stage_next(attn, "own")
    big["w_out"] = fetch("w_out", rec)
    h1, z2, mix = _mixer_out(attn, rec, _row(p["attn_out_g"][l] + tok), _row(p["rec_out_g"][l]),
                             big["w_out"], h, _row(p["mlp_norm_g"][l]))
    big["w_up"] = fetch("w_up", h1)
    act, up = _mlp_up(z2, big["w_up"])
    tok = stage_next(act, "next")
    big["w_down"] = fetch("w_down", act)
    h2, z_next = _mlp_down(act, big["w_down"], h1, _row(g_next + tok))
    saved = dict(h0=h, z1=z, proj=proj, qkv=qkv, c=c, c_t=c_t, attn=attn, lse_b=lse_b, hr=hr, rec=rec, h1=h1,
                 z2=z2, mix=mix, up=up, wa_d=wa_d, wx_d=wx_d, b_f_pad=b_f_pad, big=big)
    return h2, z_next, saved


def _backward_mlp(l, dh, dh_b, sv, p, tok):
    w_up, w_down = sv["big"]["w_up"], sv["big"]["w_down"]
    fc = w_up.shape[2]
    dup, g_down, g_up = _mlp_bwd(dh_b, w_down, sv["up"], sv["z2"])
    dh, dh_b, dg2 = _dx_norm_bwd([(dup, j, fc) for j in range(N_CHIPS)], w_up, _full(w_up.shape),
                                 lambda w_ref, j: w_ref[j], sv["h1"], _row(p["mlp_norm_g"][l] + tok), dh)
    big = dict(w_down=g_down.reshape((N_CHIPS, -1) + g_down.shape[1:]), w_up=g_up)
    return dh, dh_b, big, dict(mlp_norm_g=dg2[0])


def _backward_mixer(l, dh, dh_b, sv, p, tok):
    d = dh.shape[1]
    att_w = d // 2
    rec_w = d - att_w
    nh = att_w // HEAD_DIM
    small = {}
    g_out, = _grad_w_pieces([sv["mix"]], dh_b)
    dattn, drec, dg_mix = _mixer_bwd(dh_b, sv["big"]["w_out"], sv["attn"], sv["rec"],
                                     _row(p["attn_out_g"][l] + tok), _row(p["rec_out_g"][l]))
    small["attn_out_g"] = dg_mix[0, :att_w]
    small["rec_out_g"] = dg_mix[0, att_w:]
    dxr, dyr, dwa, dwx, sm = _rec_bwd(
        sv["proj"], 0, 1, rec_w, sv["hr"], drec, p["conv_w"][l], _row(p["conv_b"][l]), sv["wa_d"],
        _row(p["b_gate_a"][l]), sv["wx_d"], _row(p["b_gate_x"][l]), _row(p["lru_L"][l]))
    small.update(conv_w=sm[:CONV_WIDTH], conv_b=sm[4], b_gate_a=sm[5], b_gate_x=sm[6], lru_L=sm[7],
                 w_gate_a=_diag_blocks(dwa, N_REC_BLOCKS), w_gate_x=_diag_blocks(dwx, N_REC_BLOCKS))
    dq, dk, dv, dc = _attn_bwd(sv["qkv"], sv["c"], sv["c_t"], sv["lse_b"], dattn, nh)
    df, db_f = _fgate_bwd(sv["proj"], sv["b_f_pad"], dc)
    small["b_f"] = db_f[0, :nh]
    pieces = [dq, dk, dv, dxr, dyr, df]
    offs = [0, att_w, 2 * att_w, 3 * att_w, 3 * att_w + rec_w, 3 * att_w + 2 * rec_w]
    gq, gk, gv, gxr, gyr, gf = _grad_w_pieces(pieces, sv["z1"])
    g_in_t = jnp.concatenate([gq, gk, gv, gf[:nh], gxr, gyr], axis=0)
    w_big = sv["big"]["w_in_big"]
    widths = [pc.shape[1] for pc in pieces]
    dh, dh_b, dg1 = _dx_norm_bwd(
        [(pc, 0, wd) for pc, wd in zip(pieces, widths)], w_big, _full(w_big.shape),
        lambda w_ref, i: w_ref[offs[i]:offs[i] + widths[i], :], sv["h0"], _row(p["attn_norm_g"][l]), dh, dot=_dot)
    small["attn_norm_g"] = dg1[0]
    big = dict(w_in=g_in_t.reshape(N_CHIPS, -1, d), w_out=g_out.reshape((N_CHIPS, -1) + g_out.shape[1:]))
    return dh, dh_b, big, small


def _pack_w_in_t(w_in_t, att_w, nh):
    qkv = w_in_t[:3 * att_w]
    f = w_in_t[3 * att_w:3 * att_w + nh]
    xy = w_in_t[3 * att_w + nh:]
    return jnp.concatenate([qkv, xy, f, jnp.zeros((LANES - nh, w_in_t.shape[1]), w_in_t.dtype)], axis=0)


ANY = pl.BlockSpec(memory_space=pl.ANY)


def _coords():
    return lax.axis_index("x"), lax.axis_index("y"), lax.axis_index("c")


def _other_chips(x, y):
    return [(1 - x, y), (x, 1 - y), (1 - x, 1 - y)]


def _remote(src, dst, send_sems, recv_sems, k, to):
    return pltpu.make_async_remote_copy(src_ref=src, dst_ref=dst, send_sem=send_sems.at[k],
                                        recv_sem=recv_sems.at[k], device_id=to, device_id_type=MESH)


def _all_gather_chips(shards):
    n = len(shards)
    per = 6

    def body(*refs):
        ins, outs = refs[:n], refs[n:2 * n]
        send_sems, recv_sems, local_sems = refs[2 * n:]
        x, y, c = _coords()
        me = 2 * x + y
        sibling = (x, y, 1 - c)
        chips = _other_chips(x, y)
        local = [pltpu.make_async_copy(ins[t], outs[t].at[me], local_sems.at[t]) for t in range(n)]
        for cp in local:
            cp.start()
        sends = []
        for t in range(n):
            for j, (px, py) in enumerate(chips):
                cp = _remote(ins[t].at[c], outs[t].at[me, c], send_sems, recv_sems, per * t + j, (px, py, c))
                cp.start()
                sends.append(cp)
        for t in range(n):
            for j, (px, py) in enumerate(chips):
                landed = outs[t].at[2 * px + py, c]
                _remote(landed, landed, send_sems, recv_sems, per * t + j, (px, py, c)).wait_recv()
                cp = _remote(landed, landed, send_sems, recv_sems, per * t + 3 + j, sibling)
                cp.start()
                sends.append(cp)
        for t in range(n):
            for j, (px, py) in enumerate(chips):
                passed = outs[t].at[2 * px + py, 1 - c]
                _remote(passed, passed, send_sems, recv_sems, per * t + 3 + j, sibling).wait_recv()
        for cp in sends:
            cp.wait_send()
        for cp in local:
            cp.wait()

    return _call(body, tuple(shards), name="all_gather_chips",
                 in_specs=[ANY] * n, out_specs=[ANY] * n,
                 out_shape=[jax.ShapeDtypeStruct((N_CHIPS,) + s.shape, s.dtype) for s in shards],
                 scratch_shapes=[pltpu.SemaphoreType.DMA((per * n,)), pltpu.SemaphoreType.DMA((per * n,)),
                                 pltpu.SemaphoreType.DMA((n,))])


HBM = pl.BlockSpec(memory_space=pltpu.HBM)
SEM = pl.BlockSpec(memory_space=pltpu.SEMAPHORE)
DATAFLOW = pltpu.SideEffectType.DATAFLOW_SIDE_EFFECTING


def _in_hbm(a):
    return pltpu.with_memory_space_constraint(a, pltpu.HBM)


PUSH_ARRIVALS = {"gather_chips_half": N_CHIPS - 1, "pass_halves": N_CHIPS - 1, "scatter_chips": N_CHIPS - 1,
                 "sibling": 1, "gather_devices": N_DEV - 1}


def _column_half(ref3, slab, c):
    hw = ref3.shape[2] // 2
    return ref3.at[slab, :, pl.ds(pl.multiple_of(c * hw, LANES), hw)]


def _push_copies(mode, src, land, send_sems, recv_sems, t):
    x, y, c = _coords()
    chip = 2 * x + y
    if mode == "gather_chips_half":
        return [_remote(_column_half(src, chip, c), _column_half(land, chip, c), send_sems, recv_sems, t, (px, py, c))
                for px, py in _other_chips(x, y)]
    if mode == "pass_halves":
        return [_remote(_column_half(src, 2 * px + py, c), _column_half(land, 2 * px + py, c), send_sems, recv_sems, t,
                        (x, y, 1 - c)) for px, py in _other_chips(x, y)]
    if mode == "scatter_chips":
        return [_remote(src.at[2 * px + py], land.at[chip], send_sems, recv_sems, t, (px, py, c))
                for px, py in _other_chips(x, y)]
    if mode == "sibling":
        return [_remote(src, land, send_sems, recv_sems, t, (x, y, 1 - c))]
    dev = 4 * x + 2 * y + c
    return [_remote(src.at[dev], land.at[dev], send_sems, recv_sems, t, (x ^ (k >> 2), y ^ ((k >> 1) & 1), c ^ (k & 1)))
            for k in range(1, N_DEV)]


def _push_start(srcs, lands, mode, name):
    n = len(srcs)
    same = all(s is ld for s, ld in zip(srcs, lands))
    n_in = n if same else 2 * n

    def body(*refs):
        src_refs = refs[:n]
        land_refs = src_refs if same else refs[n:2 * n]
        send_sems, recv_sems = refs[n_in], refs[n_in + 1]
        token = refs[-1]
        for t in range(n):
            for cp in _push_copies(mode, src_refs[t], land_refs[t], send_sems, recv_sems, t):
                cp.start()
        token[...] = jnp.zeros_like(token)

    operands = tuple(srcs) if same else tuple(srcs) + tuple(lands)
    res = _call(
        body, [_in_hbm(a) for a in operands], name=name,
        out_shape=(pltpu.SemaphoreType.DMA((n,)), pltpu.SemaphoreType.DMA((n,)))
        + tuple(pltpu.HBM(a.shape, a.dtype) for a in operands) + (jax.ShapeDtypeStruct((SUBLANES, LANES), F32),),
        in_specs=[HBM] * n_in, out_specs=(SEM, SEM) + (HBM,) * n_in + (pl.BlockSpec(memory_space=pltpu.VMEM),),
        input_output_aliases={i: 2 + i for i in range(n_in)}, side_effects=DATAFLOW, hbm_results=False)
    send_sems, recv_sems, token = res[0], res[1], res[-1]
    srcs_thru = res[2:2 + n]
    lands_thru = srcs_thru if same else res[2 + n:2 + 2 * n]
    return send_sems, recv_sems, srcs_thru, lands_thru, token


def _push_wait(send_sems, recv_sems, ids, srcs, lands, mode, after, name):
    n = len(lands)
    same = all(s is ld for s, ld in zip(srcs, lands))
    n_in = n if same else 2 * n

    def body(*refs):
        land_refs = refs[:n] if same else refs[n:2 * n]
        send_sems, recv_sems = refs[n_in], refs[n_in + 1]
        x, y, c = _coords()
        for t in range(n):
            if mode == "sibling":
                moved = land_refs[t]
            elif mode in ("gather_chips_half", "pass_halves"):
                moved = land_refs[t].at[pl.ds(0, PUSH_ARRIVALS[mode]), :, pl.ds(0, land_refs[t].shape[2] // 2)]
            else:
                moved = land_refs[t].at[pl.ds(0, PUSH_ARRIVALS[mode])]
            arrivals = _remote(moved, moved, send_sems, recv_sems, ids[t], (x, y, c))
            arrivals.wait_send()
            arrivals.wait_recv()

    operands = tuple(lands) if same else tuple(srcs) + tuple(lands)
    res = _call(
        body, operands + (send_sems, recv_sems, after), name=name,
        out_shape=tuple(pltpu.HBM(a.shape, a.dtype) for a in operands),
        in_specs=[HBM] * n_in + [SEM, SEM, ANY], out_specs=(HBM,) * n_in,
        input_output_aliases={i: i for i in range(n_in)}, side_effects=DATAFLOW)
    return list(res) if same else (list(res[:n]), list(res[n:]))


def _sum_partials(part, landed, chip):
    _, rows, cols = part.shape
    br = _divisor_tile(rows, 16, ELEM_ROWS)

    def body(chip_ref, own_ref, a_ref, b_ref, c_ref, o_ref):
        o_ref[...] = ((own_ref[...].astype(F32) + a_ref[...].astype(F32)) + b_ref[...].astype(F32)) \
            + c_ref[...].astype(F32)

    def other(k):
        return pl.BlockSpec((None, br, cols), lambda i, ch: (jnp.where(ch[0] <= k, k + 1, k), i, 0))

    spec = pltpu.PrefetchScalarGridSpec(
        num_scalar_prefetch=1, grid=(rows // br,),
        in_specs=[pl.BlockSpec((None, br, cols), lambda i, ch: (ch[0], i, 0)), other(0), other(1), other(2)],
        out_specs=pl.BlockSpec((br, cols), lambda i, ch: (i, 0)))
    return _call(body, (chip, part, landed, landed, landed), name="sum_partials", grid_spec=spec,
                 out_shape=jax.ShapeDtypeStruct((rows, cols), F32), semantics=("parallel",))


def _cast_to_slab(w, l, chip):
    _, rows, cols = w.shape
    br = _divisor_tile(rows, 16, ELEM_ROWS)

    def body(chip_ref, w_ref, o_ref):
        o_ref[...] = w_ref[...].astype(BF16)

    spec = pltpu.PrefetchScalarGridSpec(
        num_scalar_prefetch=1, grid=(rows // br,),
        in_specs=[pl.BlockSpec((None, br, cols), lambda i, ch: (l, i, 0))],
        out_specs=pl.BlockSpec((None, br, cols), lambda i, ch: (ch[0], i, 0)))
    return _call(body, (chip, w), name="cast_to_slab", grid_spec=spec,
                 out_shape=jax.ShapeDtypeStruct((N_CHIPS, rows, cols), BF16), semantics=("parallel",))


def _cast_w_in_t_to_slabs(w_t, chip):
    rows, depth, d = w_t.shape
    tn = _divisor_tile(d, LANES, 256)

    def body(chip_ref, w_ref, *o_refs):
        for l in range(depth):
            o_refs[l][...] = w_ref[:, l, :].astype(BF16)

    spec = pltpu.PrefetchScalarGridSpec(
        num_scalar_prefetch=1, grid=(d // tn,),
        in_specs=[pl.BlockSpec((rows, depth, tn), lambda j, ch: (0, 0, j))],
        out_specs=[pl.BlockSpec((None, rows, tn), lambda j, ch: (ch[0], 0, j))] * depth)
    return _call(body, (chip, w_t), name="cast_w_in_t_to_slabs", grid_spec=spec,
                 out_shape=[jax.ShapeDtypeStruct((N_CHIPS, rows, d), BF16)] * depth, semantics=("parallel",),
                 vmem_bytes=4 * _nbytes((rows, max(depth, SUBLANES), tn), F32))


def _place_slab(buf, index, n_slabs):
    rows, cols = buf.shape
    br = _divisor_tile(rows, SUBLANES, ELEM_ROWS)

    def body(index_ref, b_ref, o_ref):
        o_ref[...] = b_ref[...]

    spec = pltpu.PrefetchScalarGridSpec(
        num_scalar_prefetch=1, grid=(rows // br,),
        in_specs=[pl.BlockSpec((br, cols), lambda i, ix: (i, 0))],
        out_specs=pl.BlockSpec((None, br, cols), lambda i, ix: (ix[0], i, 0)))
    return _call(body, (index, buf), name="place_slab", grid_spec=spec,
                 out_shape=jax.ShapeDtypeStruct((n_slabs, rows, cols), buf.dtype), semantics=("parallel",))


ELEM_ROWS = 256


def _sum_slabs(r):
    n, rows, cols = r.shape
    br = _divisor_tile(rows, 16, ELEM_ROWS)

    def body(r_ref, o_ref):
        acc = r_ref[0].astype(F32)
        for j in range(1, n):
            acc = acc + r_ref[j].astype(F32)
        o_ref[...] = acc

    return _call(body, (r,), name="sum_slabs", grid=(rows // br,),
                 in_specs=[pl.BlockSpec((n, br, cols), lambda i: (0, i, 0))],
                 out_specs=pl.BlockSpec((br, cols), lambda i: (i, 0)),
                 out_shape=jax.ShapeDtypeStruct((rows, cols), F32), semantics=("parallel",))


def _adamw_math(w, g, m, v):
    c1 = 1.0 - ADAM_B1 ** ADAM_STEP
    c2 = 1.0 - ADAM_B2 ** ADAM_STEP
    nm = ADAM_B1 * m + (1.0 - ADAM_B1) * g
    nv = ADAM_B2 * v + (1.0 - ADAM_B2) * (g * g)
    delta = -ADAM_LR * ((nm / c1) / (jnp.sqrt(nv / c2) + ADAM_EPS) + ADAM_WD * w)
    return delta, nm, nv


def _adamw(w, g, m, v):
    rows, cols = w.shape
    br = _divisor_tile(rows, 8, ELEM_ROWS)

    def body(w_ref, g_ref, m_ref, v_ref, d_ref, nm_ref, nv_ref):
        d_ref[...], nm_ref[...], nv_ref[...] = _adamw_math(w_ref[...], g_ref[...], m_ref[...], v_ref[...])

    blk = pl.BlockSpec((br, cols), lambda i: (i, 0))
    return _call(body, (w, g, m, v), name="adamw", grid=(rows // br,),
                 in_specs=[blk] * 4, out_specs=[blk] * 3,
                 out_shape=[jax.ShapeDtypeStruct((rows, cols), F32)] * 3, semantics=("parallel",))


def _adamw_w_in_t(w_t, m_t, v_t, g_mine, g_theirs):
    rows, depth, d = w_t.shape
    tn = LANES

    def body(w_ref, m_ref, v_ref, *rest):
        ga_refs, gb_refs = rest[:depth], rest[depth:2 * depth]
        g_ref, d_ref, nm_ref, nv_ref = rest[2 * depth:]
        for l in range(depth):
            g_ref[:, l, :] = ga_refs[l][...] + gb_refs[l][...]
        d_ref[...], nm_ref[...], nv_ref[...] = _adamw_math(w_ref[...], g_ref[...], m_ref[...], v_ref[...])

    slab = pl.BlockSpec((rows, depth, tn), lambda j: (0, 0, j))
    gblk = pl.BlockSpec((rows, tn), lambda j: (0, j))
    return _call(body, (w_t, m_t, v_t) + tuple(g_mine) + tuple(g_theirs), name="adamw_w_in_t", grid=(d // tn,),
                 in_specs=[slab] * 3 + [gblk] * (2 * depth), out_specs=[slab] * 4,
                 out_shape=[jax.ShapeDtypeStruct(w_t.shape, F32)] * 4, semantics=("parallel",),
                 vmem_bytes=2 * (7 * _nbytes((rows, max(depth, SUBLANES), tn), F32)
                                 + 2 * depth * _nbytes((rows, tn), F32)))


def _adamw_layer(w, m, v, l, g_mine, g_theirs, prev, after):
    _, rows, cols = w.shape
    br = _divisor_tile(rows, 8, ELEM_ROWS)

    def body(w_ref, m_ref, v_ref, ga_ref, gb_ref, *rest):
        g_ref, d_ref, nm_ref, nv_ref = rest[5:]
        g = ga_ref[...] + gb_ref[...]
        g_ref[...] = g
        d_ref[...], nm_ref[...], nv_ref[...] = _adamw_math(w_ref[...], g, m_ref[...], v_ref[...])

    slot = pl.BlockSpec((None, br, cols), lambda i: (l, i, 0))
    blk = pl.BlockSpec((br, cols), lambda i: (i, 0))
    return _call(body, (w, m, v, g_mine, g_theirs) + tuple(prev) + (after,), name="adamw_layer",
                 grid=(rows // br,), in_specs=[slot] * 3 + [blk] * 2 + [ANY] * 5, out_specs=[slot] * 4,
                 out_shape=[jax.ShapeDtypeStruct(w.shape, F32)] * 4,
                 input_output_aliases={5: 0, 6: 1, 7: 2, 8: 3}, semantics=("parallel",))


BIG = ("w_in", "w_out", "w_up", "w_down")
WEIGHTS = ("meta", "attn_norm_g", "w_in", "b_f", "conv_w", "conv_b", "w_gate_a", "b_gate_a", "w_gate_x",
           "b_gate_x", "lru_L", "attn_out_g", "rec_out_g", "w_out", "mlp_norm_g", "w_up", "w_down", "final_g")
SMALL = tuple(k for k in WEIGHTS if k not in BIG)
COL_SHARDED_SMALL = ("meta", "conv_w")


def _packed_rows(shape):
    return -(-math.prod(shape) // (SUBLANES * LANES)) * SUBLANES


def _pack(arrs):
    rows = []
    for a in arrs:
        flat = a.reshape(-1)
        rows.append(jnp.pad(flat, (0, _packed_rows(a.shape) * LANES - flat.shape[0])).reshape(-1, LANES))
    used = sum(r.shape[0] for r in rows)
    rows.append(jnp.zeros((-used % ELEM_ROWS, LANES), F32))
    return jnp.concatenate(rows, axis=0)


def _unpack(buf, shapes):
    out, r0 = [], 0
    for s in shapes:
        nr = _packed_rows(s)
        out.append(buf[r0:r0 + nr].reshape(-1)[:math.prod(s)].reshape(s))
        r0 += nr
    return out


def _halves(a):
    return a.reshape((2, a.shape[0] // 2) + a.shape[1:])


def _cols_from_chips(g):
    return jnp.moveaxis(g, 0, -2).reshape(g.shape[1:-1] + (N_CHIPS * g.shape[-1],))


def kernel(x, meta, attn_norm_g, w_in, b_f, conv_w, conv_b, w_gate_a, b_gate_a, w_gate_x, b_gate_x, lru_L, attn_out_g, rec_out_g, w_out, mlp_norm_g, w_up, w_down, final_g, loss_target, m_meta, m_attn_norm_g, m_w_in, m_b_f, m_conv_w, m_conv_b, m_w_gate_a, m_b_gate_a, m_w_gate_x, m_b_gate_x, m_lru_L, m_attn_out_g, m_rec_out_g, m_w_out, m_mlp_norm_g, m_w_up, m_w_down, m_final_g, v_meta, v_attn_norm_g, v_w_in, v_b_f, v_conv_w, v_conv_b, v_w_gate_a, v_b_gate_a, v_w_gate_x, v_b_gate_x, v_lru_L, v_attn_out_g, v_rec_out_g, v_w_out, v_mlp_norm_g, v_w_up, v_w_down, v_final_g):
    w = dict(meta=meta, attn_norm_g=attn_norm_g, w_in=w_in, b_f=b_f, conv_w=conv_w, conv_b=conv_b,
             w_gate_a=w_gate_a, b_gate_a=b_gate_a, w_gate_x=w_gate_x, b_gate_x=b_gate_x, lru_L=lru_L,
             attn_out_g=attn_out_g, rec_out_g=rec_out_g, w_out=w_out, mlp_norm_g=mlp_norm_g, w_up=w_up,
             w_down=w_down, final_g=final_g)
    m = dict(meta=m_meta, attn_norm_g=m_attn_norm_g, w_in=m_w_in, b_f=m_b_f, conv_w=m_conv_w, conv_b=m_conv_b,
             w_gate_a=m_w_gate_a, b_gate_a=m_b_gate_a, w_gate_x=m_w_gate_x, b_gate_x=m_b_gate_x, lru_L=m_lru_L,
             attn_out_g=m_attn_out_g, rec_out_g=m_rec_out_g, w_out=m_w_out, mlp_norm_g=m_mlp_norm_g,
             w_up=m_w_up, w_down=m_w_down, final_g=m_final_g)
    v = dict(meta=v_meta, attn_norm_g=v_attn_norm_g, w_in=v_w_in, b_f=v_b_f, conv_w=v_conv_w, conv_b=v_conv_b,
             w_gate_a=v_w_gate_a, b_gate_a=v_b_gate_a, w_gate_x=v_w_gate_x, b_gate_x=v_b_gate_x, lru_L=v_lru_L,
             attn_out_g=v_attn_out_g, rec_out_g=v_rec_out_g, w_out=v_w_out, mlp_norm_g=v_mlp_norm_g,
             w_up=v_w_up, w_down=v_w_down, final_g=v_final_g)
    s_len, d = x.shape[1], x.shape[2]
    depth = w_in.shape[0]
    att_w = d // 2
    rec_w = d - att_w
    nh = att_w // HEAD_DIM
    chip = 2 * lax.axis_index("x") + lax.axis_index("y")

    g_conv, g_meta = [g.reshape((N_CHIPS, g.shape[1] * g.shape[2]) + g.shape[3:])
                      for g in _all_gather_chips([_halves(w["conv_w"]), _halves(w["meta"])])]
    p = dict(w)
    p["conv_w"] = _cols_from_chips(g_conv)
    meta_full = jnp.moveaxis(g_meta, 0, 1).reshape(N_META, d)

    chip1 = chip.reshape(1).astype(jnp.int32)
    w_in_t, m_in_t, v_in_t = [jnp.transpose(a["w_in"], (2, 0, 1)) for a in (w, m, v)]
    w_in_slabs = _cast_w_in_t_to_slabs(w_in_t, chip1)
    pushes, tokens = [], []
    for l in range(depth):
        slabs = [w_in_slabs[l]] + [_cast_to_slab(w[k], l, chip1) for k in BIG[1:]]
        send_sems, recv_sems, _, lands, token = _push_start(slabs, slabs, "gather_chips_half", f"weights_start_{l}")
        pushes.append((send_sems, recv_sems, lands))
        tokens.append(token[0, 0])
    passed = [{} for _ in range(depth)]

    def stage(l, after, ids):
        send_sems, recv_sems, lands = pushes[l]
        tag = "_".join(BIG[i] for i in ids)
        sub = [lands[i] for i in ids]
        sub = _push_wait(send_sems, recv_sems, ids, sub, sub, "gather_chips_half", after, f"{tag}_wait_{l}")
        send_sems, recv_sems, _, sub, token = _push_start(sub, sub, "pass_halves", f"{tag}_pass_{l}")
        for j, i in enumerate(ids):
            passed[l][i] = (send_sems, recv_sems, sub[j], j)
        return token[0, 0]

    t_len = N_META + s_len
    pad = -t_len % SEQ_TILE
    h = jnp.concatenate([meta_full, x[0], jnp.zeros((pad, d), F32)], axis=0)
    tgt = jnp.concatenate([jnp.zeros((N_META, d), F32), loss_target[0], jnp.zeros((pad, d), F32)], axis=0)
    z = _rms_fwd(h, _row(p["attn_norm_g"][0] + sum(tokens)))
    stage(0, z, [0])
    saved = []
    for l in range(depth):
        def fetch(k, after, l=l):
            send_sems, recv_sems, land, j = passed[l][BIG.index(k)]
            return _push_wait(send_sems, recv_sems, [j], [land], [land], "pass_halves", after,
                              f"{k}_here_{l}")[0]

        def stage_next(after, which, l=l):
            if which == "own":
                return stage(l, after, [1, 2, 3])
            return stage(l + 1, after, [0]) if l + 1 < depth else 0.0

        g_next = p["attn_norm_g"][l + 1] if l + 1 < depth else p["final_g"]
        h, z, sv = _forward_layer(l, h, z, p, fetch, stage_next, g_next)
        saved.append(sv)
    dh, dh_b, dg_final, loss_part = _loss_bwd(h, _row(p["final_g"]), tgt, s_len)

    small = {k: [None] * depth for k in SMALL if k not in ("meta", "final_g")}
    pushes = [None] * depth
    tok = 0.0
    for l in reversed(range(depth)):
        dh, dh_b, big_mlp, sm_mlp = _backward_mlp(l, dh, dh_b, saved[l], p, tok)
        parts = [big_mlp["w_down"], big_mlp["w_up"]]
        push_mlp = _push_start(parts, [lax.empty(a.shape, a.dtype) for a in parts], "scatter_chips",
                               f"mlp_grads_start_{l}")
        dh, dh_b, big_mix, sm_mix = _backward_mixer(l, dh, dh_b, saved[l], p, push_mlp[4][0, 0])
        parts = [big_mix["w_out"], big_mix["w_in"]]
        push_mix = _push_start(parts, [lax.empty(a.shape, a.dtype) for a in parts], "scatter_chips",
                               f"mixer_grads_start_{l}")
        tok = push_mix[4][0, 0]
        pushes[l] = {("w_down", "w_up"): push_mlp, ("w_out", "w_in"): push_mix}
        for k, val in {**sm_mlp, **sm_mix}.items():
            small[k][l] = val
    grads = {k: jnp.stack(val) for k, val in small.items()}
    grads["final_g"] = dg_final[0]
    grads["meta"] = dh[:N_META]
    dx = dh[N_META:t_len]

    full_shapes = [grads[k].shape for k in SMALL] + [(1,)]
    packed = _pack([grads[k].astype(F32) for k in SMALL] + [loss_part[0, :1] + tok])
    dev1 = (2 * chip + lax.axis_index("c")).reshape(1).astype(jnp.int32)
    slabs = [_place_slab(packed, dev1, N_DEV)]
    small_push = _push_start(slabs, slabs, "gather_devices", "small_grads_start")

    last_token = small_push[4]
    outs = {k: [lax.empty(w[k].shape, F32) for _ in range(4)] for k in BIG[1:]}
    w_in_sums = [None] * depth
    swaps = {}

    def finish(l, wait_after, adam_after):
        send_sems, recv_sems, mine, lands, _ = swaps[l]
        mine, theirs = _push_wait(send_sems, recv_sems, list(range(len(BIG))), mine, lands, "sibling", wait_after,
                                  f"sums_wait_{l}")
        w_in_sums[l] = (mine[0], theirs[0])
        for k, a, b in zip(BIG[1:], mine[1:], theirs[1:]):
            outs[k] = _adamw_layer(w[k], m[k], v[k], l, a, b, outs[k], adam_after)

    for l in reversed(range(depth)):
        sums = {}
        for names, (send_sems, recv_sems, parts, lands, _) in pushes[l].items():
            parts, landed = _push_wait(send_sems, recv_sems, [0, 1], parts, lands, "scatter_chips", last_token,
                                       f"{names[0]}_grads_wait_{l}")
            for k, part, land in zip(names, parts, landed):
                sums[k] = _sum_partials(part, land, chip1)
        mine = [sums[k] for k in BIG]
        swaps[l] = _push_start(mine, [lax.empty(a.shape, a.dtype) for a in mine], "sibling", f"sums_start_{l}")
        if l + 1 < depth:
            finish(l + 1, outs["w_down"][0] if l + 2 < depth else mine[0], swaps[l][4])
    finish(0, outs["w_down"][0] if depth > 1 else swaps[0][4], swaps[0][4])
    outs["w_in"] = [jnp.transpose(r, (1, 2, 0)) for r in _adamw_w_in_t(
        w_in_t, m_in_t, v_in_t, [s[0] for s in w_in_sums], [s[1] for s in w_in_sums])]
    out_g, out_d, out_m, out_v = [{k: outs[k][i] for k in BIG} for i in range(4)]

    landed = _push_wait(small_push[0], small_push[1], [0], small_push[3], small_push[3], "gather_devices",
                        out_g["w_in"], "small_grads_wait")
    total = _sum_slabs(landed[0])
    small_g = dict(zip(SMALL + ("loss",), _unpack(total, full_shapes)))
    for k in COL_SHARDED_SMALL:
        n = w[k].shape[-1]
        small_g[k] = lax.dynamic_slice_in_dim(small_g[k], chip * n, n, axis=small_g[k].ndim - 1)
    local_shapes = [w[k].shape for k in SMALL]
    res = _adamw(_pack([w[k] for k in SMALL]), _pack([small_g[k] for k in SMALL]),
                 _pack([m[k] for k in SMALL]), _pack([v[k] for k in SMALL]))
    out_g.update({k: small_g[k] for k in SMALL})
    for dst, buf in zip((out_d, out_m, out_v), res):
        dst.update(zip(SMALL, _unpack(buf, local_shapes)))

    return (small_g["loss"].reshape(()), dx[None],
            *[out_g[k] for k in WEIGHTS], *[out_d[k] for k in WEIGHTS],
            *[out_m[k] for k in WEIGHTS], *[out_v[k] for k in WEIGHTS])
```

```python
import functools
import math

import jax
import jax.numpy as jnp
from jax import lax
from jax.experimental import pallas as pl
from jax.experimental.pallas import tpu as pltpu

F32 = jnp.float32
BF16 = jnp.bfloat16

N_META = 16
HEAD_DIM = 64
N_REC_BLOCKS = 8
CONV_WIDTH = 4
RG_C = 8.0
NORM_EPS = 1e-6
ADAM_LR = 0.001
ADAM_B1 = 0.9
ADAM_B2 = 0.999
ADAM_EPS = 1e-08
ADAM_WD = 0.01
ADAM_STEP = 10

LANES = 128
SUBLANES = 8
SEQ_TILE = 128
VMEM_CAP = 60 * 2**20
VMEM_SLACK = 6 * 2**20
NEG_BIG = -1e30
N_CHIPS = 4
N_DEV = 8
MESH = pl.DeviceIdType.MESH


def _nbytes(shape, dtype):
    return math.prod(shape) * jnp.dtype(dtype).itemsize


def _call(body, args, *, name, out_shape, grid=(), in_specs=None, out_specs=None, scratch_shapes=(),
          grid_spec=None, semantics=None, vmem_bytes=None, side_effects=None, hbm_results=True, **kw):
    cp = {}
    if semantics is not None:
        cp["dimension_semantics"] = semantics
    if vmem_bytes is not None:
        cp["vmem_limit_bytes"] = int(min(VMEM_CAP, vmem_bytes + VMEM_SLACK))
    if side_effects is not None:
        cp["has_side_effects"] = side_effects
    if grid_spec is not None:
        kw["grid_spec"] = grid_spec
    else:
        kw.update(grid=grid, in_specs=in_specs, out_specs=out_specs, scratch_shapes=scratch_shapes)
    if hbm_results:
        out_shape = jax.tree.map(
            lambda s: pltpu.HBM(s.shape, s.dtype) if isinstance(s, jax.ShapeDtypeStruct) else s, out_shape)
    fn = pl.pallas_call(
        body, name=name, out_shape=out_shape,
        compiler_params=pltpu.CompilerParams(**cp), **kw)
    return fn(*[_in_hbm(a) if jnp.issubdtype(getattr(a, "dtype", jnp.int32), jnp.floating) else a for a in args])


def _divisor_tile(n, unit, target):
    best = None
    for t in range(unit, min(n, target) + 1, unit):
        if n % t == 0:
            best = t
    return n if best is None else best


def _sigmoid(x):
    return 1.0 / (1.0 + jnp.exp(-x))


def _log1p_unit(e):
    series = e * (1.0 - e * (0.5 - e * (1.0 / 3.0)))
    return jnp.where(e < 1e-2, series, jnp.log(1.0 + e))


def _log_sigmoid(x):
    return jnp.minimum(x, 0.0) - _log1p_unit(jnp.exp(-jnp.abs(x)))


def _one_minus_exp(x, exp_x):
    small = -x * (1.0 + x * (1.0 / 2 + x * (1.0 / 6 + x * (1.0 / 24 + x * (1.0 / 120 + x * (1.0 / 720))))))
    return jnp.where(x > -0.25, small, 1.0 - exp_x)


_GELU_K = math.sqrt(2.0 / math.pi)
_GELU_C = 0.044715


def _gelu_and_grad(y):
    th = jnp.tanh(_GELU_K * (y + _GELU_C * y * y * y))
    g = 0.5 * y * (1.0 + th)
    dg = 0.5 * (1.0 + th) + 0.5 * y * (1.0 - th * th) * _GELU_K * (1.0 + 3.0 * _GELU_C * y * y)
    return g, dg


def _rstd(x):
    return lax.rsqrt(jnp.mean(x * x, axis=-1, keepdims=True) + NORM_EPS)


def _rms_bwd(dz, x, g):
    rs = _rstd(x)
    xh = x * rs
    dgp = jnp.sum(dz * xh, axis=0, keepdims=True)
    dxh = dz * g
    dx = rs * (dxh - xh * jnp.mean(dxh * xh, axis=-1, keepdims=True))
    return dx, dgp


def _dot(a, b):
    return jnp.dot(a, b, preferred_element_type=F32)


def _dot_nt(a, b):
    return lax.dot_general(a, b, (((1,), (1,)), ((), ())), preferred_element_type=F32)


def _dot_tn(a, b):
    return lax.dot_general(a, b, (((0,), (0,)), ((), ())), preferred_element_type=F32)


def _full(shape):
    nd = len(shape)
    return pl.BlockSpec(shape, lambda *_: (0,) * nd)


def _rms_fwd(h, g):
    tp, d = h.shape
    tm = _divisor_tile(tp, 16, 544)

    def body(h_ref, g_ref, z_ref):
        x = h_ref[...]
        z_ref[...] = (x * _rstd(x) * g_ref[...]).astype(BF16)

    return _call(body, (h, g), name="rms_fwd", grid=(tp // tm,),
                 in_specs=[pl.BlockSpec((tm, d), lambda i: (i, 0)), _full((1, d))],
                 out_specs=pl.BlockSpec((tm, d), lambda i: (i, 0)),
                 out_shape=jax.ShapeDtypeStruct((tp, d), BF16), semantics=("parallel",))


def _proj(z, w_big_t, att_w):
    tp, d = z.shape
    nb = w_big_t.shape[0]
    tn = _divisor_tile(nb, LANES, 512)
    assert (3 * att_w) % tn == 0
    n_qkv = 3 * att_w // tn
    scale = 1.0 / math.sqrt(HEAD_DIM)

    def body(z_ref, w_ref, qkv_ref, p_ref):
        j = pl.program_id(0)
        acc = _dot_nt(z_ref[...], w_ref[...])

        @pl.when(j < n_qkv)
        def _():
            col = j * tn + lax.broadcasted_iota(jnp.int32, (1, tn), 1)
            qkv_ref[...] = (acc * jnp.where(col < att_w, scale, 1.0)).astype(BF16)

        @pl.when(j >= n_qkv)
        def _():
            p_ref[...] = acc

    vm = 2 * (_nbytes((tp, d), BF16) + _nbytes((d, tn), BF16) + _nbytes((tp, tn), F32) * 2)
    return _call(body, (z, w_big_t), name="proj", grid=(nb // tn,),
                 in_specs=[_full((tp, d)), pl.BlockSpec((tn, d), lambda j: (j, 0))],
                 out_specs=[pl.BlockSpec((tp, tn), lambda j: (0, jnp.minimum(j, n_qkv - 1))),
                            pl.BlockSpec((tp, tn), lambda j: (0, jnp.maximum(j - n_qkv, 0)))],
                 out_shape=[jax.ShapeDtypeStruct((tp, 3 * att_w), BF16),
                            jax.ShapeDtypeStruct((tp, nb - 3 * att_w), F32)],
                 semantics=("arbitrary",), vmem_bytes=vm)


def _tile_cumsum(x, row, reverse=False):
    for s in (1, 2, 4):
        if reverse:
            x = x + jnp.where(row < SUBLANES - s, pltpu.roll(x, SUBLANES - s, 0), 0.0)
        else:
            x = x + jnp.where(row >= s, pltpu.roll(x, s, 0), 0.0)
    return x


def _fgate_fwd(proj, b_f_pad, nh):
    tp, nb = proj.shape
    fblk = nb // LANES - 1

    def body(f_ref, b_ref, c_ref, ct_ref):
        b = b_ref[...]
        row = lax.broadcasted_iota(jnp.int32, (SUBLANES, LANES), 0)

        def step(i, carry):
            r0 = pl.multiple_of(i * SUBLANES, SUBLANES)
            lf = _log_sigmoid(f_ref[pl.ds(r0, SUBLANES), :] + b)
            x = _tile_cumsum(lf, row) + carry
            c_ref[pl.ds(r0, SUBLANES), :] = x
            return x[SUBLANES - 1:SUBLANES, :]

        lax.fori_loop(0, tp // SUBLANES, step, jnp.zeros((1, LANES), F32))
        ct_ref[...] = c_ref[...].T[:nh, :]

    return _call(body, (proj, b_f_pad), name="fgate_fwd", grid=(1,),
                 in_specs=[pl.BlockSpec((tp, LANES), lambda i: (0, fblk)), _full((1, LANES))],
                 out_specs=[_full((tp, LANES)), _full((nh, tp))],
                 out_shape=[jax.ShapeDtypeStruct((tp, LANES), F32), jax.ShapeDtypeStruct((nh, tp), F32)],
                 semantics=("arbitrary",))


def _fgate_bwd(proj, b_f_pad, dc):
    tp, nb = proj.shape
    fblk = nb // LANES - 1

    def body(f_ref, b_ref, dc_ref, df_ref, db_ref, dc_s):
        b = b_ref[...]
        row = lax.broadcasted_iota(jnp.int32, (SUBLANES, LANES), 0)
        nt = tp // SUBLANES

        def step(i, carry):
            suffix, acc = carry
            r0 = pl.multiple_of((nt - 1 - i) * SUBLANES, SUBLANES)
            dlf = _tile_cumsum(dc_ref[pl.ds(r0, SUBLANES), :], row, reverse=True) + suffix
            df = dlf * _sigmoid(-(f_ref[pl.ds(r0, SUBLANES), :] + b))
            dc_s[pl.ds(r0, SUBLANES), :] = df
            return dlf[0:1, :], acc + df

        _, acc = lax.fori_loop(0, nt, step, (jnp.zeros((1, LANES), F32), jnp.zeros((SUBLANES, LANES), F32)))
        df_ref[...] = dc_s[...].astype(BF16)
        db_ref[...] = jnp.broadcast_to(jnp.sum(acc, axis=0, keepdims=True), (SUBLANES, LANES))

    return _call(body, (proj, b_f_pad, dc), name="fgate_bwd", grid=(1,),
                 in_specs=[pl.BlockSpec((tp, LANES), lambda i: (0, fblk)), _full((1, LANES)), _full((tp, LANES))],
                 out_specs=[_full((tp, LANES)), _full((SUBLANES, LANES))],
                 out_shape=[jax.ShapeDtypeStruct((tp, LANES), BF16),
                            jax.ShapeDtypeStruct((SUBLANES, LANES), F32)],
                 scratch_shapes=[pltpu.VMEM((tp, LANES), F32)], semantics=("arbitrary",))


ATT_BQ = 128


ATT_BUCKET = 3
ATT_HEADS = 4


def _for_bucket(i, nq, fn):
    for lo in range(0, nq, ATT_BUCKET):
        hi = min(lo + ATT_BUCKET, nq)
        spans = ([(0, lo * ATT_BQ, False)] if lo else []) + [(lo * ATT_BQ, hi * ATT_BQ, True)]
        pl.when(jnp.logical_and(i >= lo, i < hi))(functools.partial(fn, spans))


def _head_column(c_blk, h):
    lane = lax.broadcasted_iota(jnp.int32, c_blk.shape, 1)
    return jnp.sum(jnp.where(lane == h, c_blk, 0.0), axis=1, keepdims=True)


def _head_columns_into(p, c_ref, ck_s):
    for hh in range(ATT_HEADS):
        ck_s[hh] = jnp.broadcast_to(_head_column(c_ref[...], ATT_HEADS * p + hh), ck_s.shape[1:])


def _pair_diag_cols(x2):
    top = lax.broadcasted_iota(jnp.int32, (LANES, ATT_BQ), 0) < HEAD_DIM
    xt = x2.astype(F32).T.astype(BF16)
    return jnp.concatenate([jnp.where(top, xt, 0), jnp.where(top, 0, xt)], axis=1)


def _pair_diag_rows(x2):
    low = lax.broadcasted_iota(jnp.int32, (ATT_BQ, LANES), 1) < HEAD_DIM
    return jnp.concatenate([jnp.where(low, x2, 0), jnp.where(low, 0, x2)], axis=0)


def _seen_keys(k0, k1, q0):
    keys = k0 + lax.broadcasted_iota(jnp.int32, (k1 - k0, ATT_BQ), 0)
    return keys <= q0 + lax.broadcasted_iota(jnp.int32, (k1 - k0, ATT_BQ), 1)


def _attn_fwd(qkv, c, c_t, nh):
    tp = qkv.shape[0]
    att_w = nh * HEAD_DIM
    ng = nh // ATT_HEADS
    gw = ATT_HEADS * HEAD_DIM
    bq = ATT_BQ
    nq = tp // bq
    pair = 2 * HEAD_DIM
    assert pair == LANES and ATT_HEADS % 2 == 0

    def body(q_ref, k_ref, v_ref, c_ref, ct_ref, o_ref, lse_ref, ck_s, vt_s):
        p = pl.program_id(0)
        i = pl.program_id(1)

        @pl.when(i == 0)
        def _():
            _head_columns_into(p, c_ref, ck_s)
            vt_s[...] = v_ref[...].astype(F32).T.astype(BF16)

        def compute(spans):
            q0 = pl.multiple_of(i * bq, bq)
            o_t, lses = [], []
            for pi in range(ATT_HEADS // 2):
                lo = pair * pi
                heads = (2 * pi, 2 * pi + 1)
                q_cols = _pair_diag_cols(q_ref[:, lo:lo + pair])
                ts = []
                for k0, k1, needs_mask in spans:
                    t2 = _dot(k_ref[k0:k1, lo:lo + pair], q_cols)
                    t_e = [t2[:, e * bq:(e + 1) * bq] - ck_s[hh, k0:k1, :] for e, hh in enumerate(heads)]
                    if needs_mask:
                        seen = _seen_keys(k0, k1, q0)
                        t_e = [jnp.where(seen, t, NEG_BIG) for t in t_e]
                    ts.append(t_e)
                ms = [functools.reduce(jnp.maximum, [jnp.max(t[e], axis=0, keepdims=True) for t in ts])
                      for e in range(2)]
                es = [[jnp.exp(t[e] - ms[e]) for e in range(2)] for t in ts]
                ls = [sum(jnp.sum(e_[e], axis=0, keepdims=True) for e_ in es) for e in range(2)]
                o2 = sum(_dot(vt_s[lo:lo + pair, k0:k1],
                              jnp.concatenate([e_[0].astype(BF16), e_[1].astype(BF16)], axis=1))
                         for e_, (k0, k1, _) in zip(es, spans))
                o_t += [o2[:HEAD_DIM, :bq] / ls[0], o2[HEAD_DIM:, bq:] / ls[1]]
                lses += [ms[e] + ct_ref[pl.ds(ATT_HEADS * p + hh, 1), :] + jnp.log(ls[e])
                         for e, hh in enumerate(heads)]
            o_ref[...] = jnp.concatenate(o_t, axis=0).T
            lse_ref[...] = jnp.concatenate(lses, axis=0)

        _for_bucket(i, nq, compute)

    blk = pl.BlockSpec((bq, gw), lambda p, i: (i, p))
    vm = 6 * _nbytes((tp, gw), BF16) + 2 * ATT_HEADS * _nbytes((tp, LANES), F32) + 2 * _nbytes((tp, LANES), F32) \
        + 8 * ATT_HEADS * _nbytes((bq, tp), F32)
    return _call(body, (qkv, qkv, qkv, c, c_t), name="attn_fwd", grid=(ng, nq),
                 in_specs=[blk,
                           pl.BlockSpec((tp, gw), lambda p, i: (0, ng + p)),
                           pl.BlockSpec((tp, gw), lambda p, i: (0, 2 * ng + p)),
                           _full((tp, LANES)), pl.BlockSpec((nh, bq), lambda p, i: (0, i))],
                 out_specs=[blk, pl.BlockSpec((None, ATT_HEADS, bq), lambda p, i: (p, 0, i))],
                 out_shape=[jax.ShapeDtypeStruct((tp, att_w), F32), jax.ShapeDtypeStruct((ng, ATT_HEADS, tp), F32)],
                 scratch_shapes=[pltpu.VMEM((ATT_HEADS, tp, LANES), F32), pltpu.VMEM((gw, tp), BF16)],
                 semantics=("arbitrary", "arbitrary"), vmem_bytes=vm)


def _attn_bwd(qkv, c, c_t, lse, do, nh):
    tp = qkv.shape[0]
    att_w = nh * HEAD_DIM
    ng = nh // ATT_HEADS
    gw = ATT_HEADS * HEAD_DIM
    bq = ATT_BQ
    nq = tp // bq
    pair = 2 * HEAD_DIM
    assert pair == LANES and ATT_HEADS % 2 == 0
    scale = 1.0 / math.sqrt(HEAD_DIM)

    def body(q_ref, k_ref, v_ref, c_ref, ct_ref, lse_ref, do_ref, dq_ref, dk_ref, dv_ref, dc_ref,
             dk_s, dv_s, dc_s, ck_s, kt_s):
        p = pl.program_id(0)
        i = pl.program_id(1)

        @pl.when(i == 0)
        def _():
            dk_s[...] = jnp.zeros_like(dk_s)
            dv_s[...] = jnp.zeros_like(dv_s)
            dc_s[...] = jnp.zeros_like(dc_s)
            kt_s[...] = k_ref[...].astype(F32).T.astype(BF16)
            _head_columns_into(p, c_ref, ck_s)

        @pl.when(jnp.logical_and(i == 0, p == 0))
        def _():
            dc_ref[...] = jnp.zeros_like(dc_ref)

        def compute(spans):
            q0 = pl.multiple_of(i * bq, bq)
            dq_t = []
            for pi in range(ATT_HEADS // 2):
                lo = pair * pi
                q2 = q_ref[:, lo:lo + pair]
                do2 = do_ref[:, lo:lo + pair].astype(BF16)
                q_cols, do_cols = _pair_diag_cols(q2), _pair_diag_cols(do2)
                q_rows, do_rows = _pair_diag_rows(q2), _pair_diag_rows(do2)
                heads = (2 * pi, 2 * pi + 1)
                col_terms = [ct_ref[pl.ds(ATT_HEADS * p + hh, 1), :] - lse_ref[hh:hh + 1, :] for hh in heads]
                prs, dps = [], []
                for k0, k1, needs_mask in spans:
                    t2 = _dot(k_ref[k0:k1, lo:lo + pair], q_cols)
                    dp2 = _dot(v_ref[k0:k1, lo:lo + pair], do_cols)
                    if needs_mask:
                        seen = _seen_keys(k0, k1, q0)
                    pr_e, dp_e = [], []
                    for e, hh in enumerate(heads):
                        t = t2[:, e * bq:(e + 1) * bq] - ck_s[hh, k0:k1, :]
                        if needs_mask:
                            t = jnp.where(seen, t, NEG_BIG)
                        pr_e.append(jnp.exp(t + col_terms[e]))
                        dp_e.append(dp2[:, e * bq:(e + 1) * bq])
                    prs.append(pr_e)
                    dps.append(dp_e)
                key_sums = [sum(jnp.sum(pr[e] * dp[e], axis=0, keepdims=True) for pr, dp in zip(prs, dps))
                            for e in range(2)]
                dq2 = 0.0
                for (k0, k1, _), pr, dp in zip(spans, prs, dps):
                    ds = [pr[e] * (dp[e] - key_sums[e]) for e in range(2)]
                    for e, hh in enumerate(heads):
                        dc_s[hh, k0:k1, :] += jnp.sum(ds[e], axis=1, keepdims=True)
                    ds2 = jnp.concatenate([ds[0].astype(BF16), ds[1].astype(BF16)], axis=1)
                    pr2 = jnp.concatenate([pr[0].astype(BF16), pr[1].astype(BF16)], axis=1)
                    dk_s[k0:k1, lo:lo + pair] += _dot(ds2, q_rows)
                    dv_s[k0:k1, lo:lo + pair] += _dot(pr2, do_rows)
                    dq2 = dq2 + _dot(kt_s[lo:lo + pair, k0:k1], ds2)
                dq_t.append(jnp.concatenate([dq2[:HEAD_DIM, :bq], dq2[HEAD_DIM:, bq:]], axis=0))
            dq_ref[...] = (jnp.concatenate(dq_t, axis=0) * scale).T.astype(BF16)

        _for_bucket(i, nq, compute)

        @pl.when(i == nq - 1)
        def _():
            dk_ref[...] = dk_s[...].astype(BF16)
            dv_ref[...] = dv_s[...].astype(BF16)
            lane = lax.broadcasted_iota(jnp.int32, (tp, LANES), 1)
            dc = dc_ref[...]
            for hh in range(ATT_HEADS):
                dc = jnp.where(lane == ATT_HEADS * p + hh, -dc_s[hh], dc)
            dc_ref[...] = dc

    blk = pl.BlockSpec((bq, gw), lambda p, i: (i, p))
    col = pl.BlockSpec((tp, gw), lambda p, i: (0, p))
    vm = 7 * _nbytes((tp, gw), BF16) + 2 * _nbytes((tp, gw), F32) + 2 * ATT_HEADS * _nbytes((tp, LANES), F32) \
        + 2 * _nbytes((tp, LANES), F32) + 12 * ATT_HEADS * _nbytes((bq, tp), F32)
    return _call(body, (qkv, qkv, qkv, c, c_t, lse, do), name="attn_bwd", grid=(ng, nq),
                 in_specs=[blk,
                           pl.BlockSpec((tp, gw), lambda p, i: (0, ng + p)),
                           pl.BlockSpec((tp, gw), lambda p, i: (0, 2 * ng + p)),
                           _full((tp, LANES)), pl.BlockSpec((nh, bq), lambda p, i: (0, i)),
                           pl.BlockSpec((None, ATT_HEADS, bq), lambda p, i: (p, 0, i)), blk],
                 out_specs=[blk, col, col, _full((tp, LANES))],
                 out_shape=[jax.ShapeDtypeStruct((tp, att_w), BF16)] * 3 + [jax.ShapeDtypeStruct((tp, LANES), F32)],
                 scratch_shapes=[pltpu.VMEM((tp, gw), F32), pltpu.VMEM((tp, gw), F32),
                                 pltpu.VMEM((ATT_HEADS, tp, 1), F32), pltpu.VMEM((ATT_HEADS, tp, LANES), F32),
                                 pltpu.VMEM((gw, tp), BF16)],
                 semantics=("arbitrary", "arbitrary"), vmem_bytes=vm)


REC_ROWS = 128
HALO = SUBLANES


def _conv_taps(cat):
    taps = []
    for k in range(CONV_WIDTH):
        sh = CONV_WIDTH - 1 - k
        taps.append((pltpu.roll(cat, sh, 0) if sh else cat)[HALO:])
    return taps


def _rec_gates(xc, wa_ref, ba_ref, wx_ref, bx_ref, l_ref):
    xcb = xc.astype(BF16)
    r = _sigmoid(_dot(xcb, wa_ref[...]) + ba_ref[...])
    ig = _sigmoid(_dot(xcb, wx_ref[...]) + bx_ref[...])
    ls = _log_sigmoid(l_ref[...])
    log_a = RG_C * r * ls
    return xcb, r, ig, ls, log_a


def _rec_fwd(proj, xr_blk, yr_blk, rec_w, conv_w, conv_b, wa, ba, wx, bx, lru):
    tp = proj.shape[0]
    w = rec_w
    r_rows = REC_ROWS
    nc = tp // r_rows
    cpb = w // LANES

    def body(xr_ref, yr_ref, cw_ref, cb_ref, wa_ref, ba_ref, wx_ref, bx_ref, l_ref,
             hr_ref, rec_ref, prev_s, carry_s, a_s, u_s):
        i = pl.program_id(0)

        @pl.when(i == 0)
        def _():
            prev_s[...] = jnp.zeros_like(prev_s)
            carry_s[...] = jnp.zeros_like(carry_s)

        x = xr_ref[...]
        taps = _conv_taps(jnp.concatenate([prev_s[...], x], axis=0))
        prev_s[...] = x[r_rows - HALO:]
        xc = cb_ref[...]
        for k in range(CONV_WIDTH):
            xc = xc + cw_ref[k:k + 1, :] * taps[k]
        _, r, ig, ls, log_a = _rec_gates(xc, wa_ref, ba_ref, wx_ref, bx_ref, l_ref)
        a = jnp.exp(log_a)
        a_s[...] = a
        u_s[...] = jnp.sqrt(_one_minus_exp(2.0 * log_a, a * a)) * ig * xc

        def tile(j, h):
            r0 = pl.multiple_of(j * SUBLANES, SUBLANES)
            at = a_s[pl.ds(r0, SUBLANES), :]
            ut = u_s[pl.ds(r0, SUBLANES), :]
            out = []
            for rr in range(SUBLANES):
                h = at[rr:rr + 1] * h + ut[rr:rr + 1]
                out.append(h)
            hr_ref[pl.ds(r0, SUBLANES), :] = jnp.concatenate(out, axis=0)
            return h

        carry_s[0:1, :] = lax.fori_loop(0, r_rows // SUBLANES, tile, carry_s[0:1, :])
        g, _ = _gelu_and_grad(yr_ref[...])
        rec_ref[...] = hr_ref[...] * g

    blk = pl.BlockSpec((r_rows, w), lambda i: (i, 0))
    vm = 16 * _nbytes((r_rows, w), F32) + 4 * _nbytes((w, w), BF16)
    return _call(body, (proj, proj, conv_w, conv_b, wa, ba, wx, bx, lru), name="rec_fwd", grid=(nc,),
                 in_specs=[pl.BlockSpec((r_rows, w), lambda i: (i, xr_blk)),
                           pl.BlockSpec((r_rows, w), lambda i: (i, yr_blk)),
                           _full((CONV_WIDTH, w)), _full((1, w)), _full((w, w)), _full((1, w)),
                           _full((w, w)), _full((1, w)), _full((1, w))],
                 out_specs=[blk, blk],
                 out_shape=[jax.ShapeDtypeStruct((tp, w), F32)] * 2,
                 scratch_shapes=[pltpu.VMEM((HALO, w), F32), pltpu.VMEM((SUBLANES, w), F32),
                                 pltpu.VMEM((r_rows, w), F32), pltpu.VMEM((r_rows, w), F32)],
                 semantics=("arbitrary",), vmem_bytes=vm)


def _rec_bwd(proj, xr_blk, yr_blk, rec_w, hr, drec, conv_w, conv_b, wa, ba, wx, bx, lru):
    tp = proj.shape[0]
    w = rec_w
    r_rows = REC_ROWS
    nc = tp // r_rows
    hpc = r_rows // HALO

    def body(xr_ref, xh_ref, yr_ref, hr_ref, hh_ref, drec_ref, cw_ref, cb_ref, wa_ref, ba_ref, wx_ref, bx_ref,
             l_ref, dxr_ref, dyr_ref, dwa_ref, dwx_ref, small_ref, lam_s, a_s, dhr_s, carry_s, next_s):
        i = pl.program_id(0)
        first = (nc - 1 - i) == 0

        @pl.when(i == 0)
        def _():
            carry_s[...] = jnp.zeros_like(carry_s)
            next_s[...] = jnp.zeros_like(next_s)
            dwa_ref[...] = jnp.zeros_like(dwa_ref)
            dwx_ref[...] = jnp.zeros_like(dwx_ref)
            small_ref[...] = jnp.zeros_like(small_ref)

        x = xr_ref[...]
        xprev = jnp.where(first, 0.0, xh_ref[...])
        taps = _conv_taps(jnp.concatenate([xprev, x], axis=0))
        xc = cb_ref[...]
        for k in range(CONV_WIDTH):
            xc = xc + cw_ref[k:k + 1, :] * taps[k]
        xcb, r, ig, ls, log_a = _rec_gates(xc, wa_ref, ba_ref, wx_ref, bx_ref, l_ref)
        a = jnp.exp(log_a)
        a2 = a * a
        mult = jnp.sqrt(_one_minus_exp(2.0 * log_a, a2))
        g, dg = _gelu_and_grad(yr_ref[...])
        hr_v = hr_ref[...]
        drec_v = drec_ref[...]
        dhr_s[...] = drec_v * g
        dyr_ref[...] = (drec_v * hr_v * dg).astype(BF16)
        a_s[...] = a

        def tile(jj, carry):
            r0 = pl.multiple_of((r_rows // SUBLANES - 1 - jj) * SUBLANES, SUBLANES)
            at = a_s[pl.ds(r0, SUBLANES), :]
            dt = dhr_s[pl.ds(r0, SUBLANES), :]
            out = [None] * SUBLANES
            for rr in range(SUBLANES - 1, -1, -1):
                lam = dt[rr:rr + 1] + carry
                out[rr] = lam
                carry = at[rr:rr + 1] * lam
            lam_s[pl.ds(r0, SUBLANES), :] = jnp.concatenate(out, axis=0)
            return carry

        carry_s[0:1, :] = lax.fori_loop(0, r_rows // SUBLANES, tile, carry_s[0:1, :])
        lam = lam_s[...]
        hprev = jnp.where(first, 0.0, hh_ref[...])
        hr_prev = pltpu.roll(jnp.concatenate([hprev, hr_v], axis=0), 1, 0)[HALO:]
        da = lam * hr_prev
        dxc = lam * mult * ig
        di = lam * mult * xc
        dmult = lam * ig * xc
        dlog_a = da * a - dmult * a2 / mult
        dr = dlog_a * (RG_C * ls)
        dls = jnp.sum(dlog_a * (RG_C * r), axis=0, keepdims=True)
        dga = dr * r * (1.0 - r)
        dgx = di * ig * (1.0 - ig)
        dgab = dga.astype(BF16)
        dgxb = dgx.astype(BF16)
        dxc = dxc + _dot_nt(dgab, wa_ref[...]) + _dot_nt(dgxb, wx_ref[...])
        dwa_ref[...] += _dot_tn(xcb, dgab)
        dwx_ref[...] += _dot_tn(xcb, dgxb)
        cat = jnp.concatenate([dxc, next_s[...]], axis=0)
        next_s[...] = dxc[0:HALO]
        dxr = cw_ref[CONV_WIDTH - 1:CONV_WIDTH, :] * dxc
        for k in range(CONV_WIDTH - 1):
            sh = CONV_WIDTH - 1 - k
            dxr = dxr + cw_ref[k:k + 1, :] * pltpu.roll(cat, r_rows + HALO - sh, 0)[:r_rows]
        dxr_ref[...] = dxr.astype(BF16)
        rows = [jnp.sum(dxc * taps[k], axis=0, keepdims=True) for k in range(CONV_WIDTH)]
        rows += [jnp.sum(dxc, axis=0, keepdims=True), jnp.sum(dga, axis=0, keepdims=True),
                 jnp.sum(dgx, axis=0, keepdims=True), dls * _sigmoid(-l_ref[...])]
        small_ref[...] += jnp.concatenate(rows, axis=0)

    def rev(i):
        return nc - 1 - i

    def halo(i):
        return jnp.maximum(rev(i) * hpc - 1, 0)

    blk = pl.BlockSpec((r_rows, w), lambda i: (rev(i), 0))
    vm = 40 * _nbytes((r_rows, w), F32) + 6 * _nbytes((w, w), F32)
    return _call(body, (proj, proj, proj, hr, hr, drec, conv_w, conv_b, wa, ba, wx, bx, lru),
                 name="rec_bwd", grid=(nc,),
                 in_specs=[pl.BlockSpec((r_rows, w), lambda i: (rev(i), xr_blk)),
                           pl.BlockSpec((HALO, w), lambda i: (halo(i), xr_blk)),
                           pl.BlockSpec((r_rows, w), lambda i: (rev(i), yr_blk)),
                           blk,
                           pl.BlockSpec((HALO, w), lambda i: (halo(i), 0)),
                           blk,
                           _full((CONV_WIDTH, w)), _full((1, w)), _full((w, w)), _full((1, w)),
                           _full((w, w)), _full((1, w)), _full((1, w))],
                 out_specs=[blk, blk, _full((w, w)), _full((w, w)), _full((SUBLANES, w))],
                 out_shape=[jax.ShapeDtypeStruct((tp, w), BF16)] * 2
                 + [jax.ShapeDtypeStruct((w, w), F32)] * 2 + [jax.ShapeDtypeStruct((SUBLANES, w), F32)],
                 scratch_shapes=[pltpu.VMEM((r_rows, w), F32)] * 3
                 + [pltpu.VMEM((SUBLANES, w), F32), pltpu.VMEM((HALO, w), F32)],
                 semantics=("arbitrary",), vmem_bytes=vm)


ROW_TARGET = 544


def _mixer_out(attn, rec, g_a, g_r, w_out, h, g_next):
    tp, d = h.shape
    aw, rw = attn.shape[1], rec.shape[1]
    kc = d // N_CHIPS
    tm = _divisor_tile(tp, 16, ROW_TARGET)

    def body(a_ref, r_ref, ga_ref, gr_ref, w_ref, h_ref, gn_ref, h1_ref, z_ref, mix_ref):
        a = a_ref[...]
        r = r_ref[...]
        mix = jnp.concatenate([a * _rstd(a) * ga_ref[...], r * _rstd(r) * gr_ref[...]], axis=1).astype(BF16)
        mix_ref[...] = mix
        h1 = h_ref[...]
        for j in range(N_CHIPS):
            h1 = h1 + _dot(mix[:, j * kc:(j + 1) * kc], w_ref[j])
        h1_ref[...] = h1
        z_ref[...] = (h1 * _rstd(h1) * gn_ref[...]).astype(BF16)

    row = lambda wd: pl.BlockSpec((tm, wd), lambda i: (i, 0))
    vm = 2 * _nbytes((d, d), BF16) + 12 * _nbytes((tm, d), F32)
    return _call(body, (attn, rec, g_a, g_r, w_out, h, g_next), name="mixer_out", grid=(tp // tm,),
                 in_specs=[row(aw), row(rw), _full((1, aw)), _full((1, rw)), _full(w_out.shape), row(d),
                           _full((1, d))],
                 out_specs=[row(d), row(d), row(d)],
                 out_shape=[jax.ShapeDtypeStruct((tp, d), F32), jax.ShapeDtypeStruct((tp, d), BF16),
                            jax.ShapeDtypeStruct((tp, d), BF16)],
                 semantics=("parallel",), vmem_bytes=vm)


def _mixer_bwd(dh_b, w_out, attn, rec, g_a, g_r):
    tp, d = dh_b.shape
    aw, rw = attn.shape[1], rec.shape[1]
    tm = _divisor_tile(tp, 16, ROW_TARGET)

    def body(dh_ref, w_ref, a_ref, r_ref, ga_ref, gr_ref, da_ref, dr_ref, dg_ref):
        @pl.when(pl.program_id(0) == 0)
        def _():
            dg_ref[...] = jnp.zeros_like(dg_ref)

        dh = dh_ref[...]
        dmix = jnp.concatenate([_dot_nt(dh, w_ref[j]) for j in range(N_CHIPS)], axis=1)
        da, dga = _rms_bwd(dmix[:, :aw], a_ref[...], ga_ref[...])
        dr, dgr = _rms_bwd(dmix[:, aw:], r_ref[...], gr_ref[...])
        da_ref[...] = da
        dr_ref[...] = dr
        dg_ref[...] += jnp.broadcast_to(jnp.concatenate([dga, dgr], axis=1), (SUBLANES, d))

    row = lambda wd: pl.BlockSpec((tm, wd), lambda i: (i, 0))
    vm = 2 * _nbytes((d, d), BF16) + 12 * _nbytes((tm, d), F32)
    return _call(body, (dh_b, w_out, attn, rec, g_a, g_r), name="mixer_bwd", grid=(tp // tm,),
                 in_specs=[row(d), _full(w_out.shape), row(aw), row(rw), _full((1, aw)), _full((1, rw))],
                 out_specs=[row(aw), row(rw), _full((SUBLANES, d))],
                 out_shape=[jax.ShapeDtypeStruct((tp, aw), F32), jax.ShapeDtypeStruct((tp, rw), F32),
                            jax.ShapeDtypeStruct((SUBLANES, d), F32)],
                 semantics=("arbitrary",), vmem_bytes=vm)


def _mlp_up(z, w_up):
    tp, d = z.shape
    fc = w_up.shape[2]
    ff = N_CHIPS * fc
    tn = _divisor_tile(fc, LANES, 512)
    per = fc // tn

    def body(z_ref, w_ref, act_ref, up_ref):
        up = _dot(z_ref[...], w_ref[...])
        r = jnp.maximum(up, 0.0)
        act_ref[...] = (r * r).astype(BF16)
        up_ref[...] = up.astype(BF16)

    col = pl.BlockSpec((tp, tn), lambda j: (0, j))
    vm = 2 * _nbytes((tp, d), BF16) + 2 * _nbytes((d, tn), BF16) + 8 * _nbytes((tp, tn), F32)
    return _call(body, (z, w_up), name="mlp_up", grid=(ff // tn,),
                 in_specs=[_full((tp, d)), pl.BlockSpec((None, d, tn), lambda j: (j // per, 0, j % per))],
                 out_specs=[col, col],
                 out_shape=[jax.ShapeDtypeStruct((tp, ff), BF16)] * 2,
                 semantics=("parallel",), vmem_bytes=vm)


def _mlp_down(act, w_down, h, g_next):
    tp, d = h.shape
    ff = act.shape[1]
    fc = ff // N_CHIPS
    tm = _divisor_tile(tp, 16, ROW_TARGET)

    def body(a_ref, w_ref, h_ref, gn_ref, h2_ref, z_ref):
        h2 = h_ref[...]
        for j in range(N_CHIPS):
            h2 = h2 + _dot(a_ref[:, j * fc:(j + 1) * fc], w_ref[j])
        h2_ref[...] = h2
        z_ref[...] = (h2 * _rstd(h2) * gn_ref[...]).astype(BF16)

    row = lambda wd: pl.BlockSpec((tm, wd), lambda i: (i, 0))
    vm = 2 * _nbytes((ff, d), BF16) + 2 * _nbytes((tm, ff), BF16) + 10 * _nbytes((tm, d), F32)
    return _call(body, (act, w_down, h, g_next), name="mlp_down", grid=(tp // tm,),
                 in_specs=[row(ff), _full(w_down.shape), row(d), _full((1, d))],
                 out_specs=[row(d), row(d)],
                 out_shape=[jax.ShapeDtypeStruct((tp, d), F32), jax.ShapeDtypeStruct((tp, d), BF16)],
                 semantics=("parallel",), vmem_bytes=vm)


def _loss_bwd(h, g, target, n_real):
    tp, d = h.shape
    tm = _divisor_tile(tp, 16, ROW_TARGET)

    def body(h_ref, g_ref, t_ref, dh_ref, dhb_ref, dg_ref, loss_ref):
        i = pl.program_id(0)

        @pl.when(i == 0)
        def _():
            dg_ref[...] = jnp.zeros_like(dg_ref)
            loss_ref[...] = jnp.zeros_like(loss_ref)

        x = h_ref[...]
        gv = g_ref[...]
        rowi = i * tm + lax.broadcasted_iota(jnp.int32, (tm, 1), 0)
        real = jnp.logical_and(rowi >= N_META, rowi < N_META + n_real)
        err = jnp.where(real, x * _rstd(x) * gv - t_ref[...], 0.0)
        loss_ref[...] += 0.5 * jnp.sum(jnp.mean(err * err, axis=-1, keepdims=True))
        dx, dgp = _rms_bwd(err * (1.0 / d), x, gv)
        dh_ref[...] = dx
        dhb_ref[...] = dx.astype(BF16)
        dg_ref[...] += jnp.broadcast_to(dgp, (SUBLANES, d))

    row = pl.BlockSpec((tm, d), lambda i: (i, 0))
    return _call(body, (h, g, target), name="loss_bwd", grid=(tp // tm,),
                 in_specs=[row, _full((1, d)), row],
                 out_specs=[row, row, _full((SUBLANES, d)), _full((SUBLANES, LANES))],
                 out_shape=[jax.ShapeDtypeStruct((tp, d), F32), jax.ShapeDtypeStruct((tp, d), BF16),
                            jax.ShapeDtypeStruct((SUBLANES, d), F32), jax.ShapeDtypeStruct((SUBLANES, LANES), F32)],
                 semantics=("arbitrary",), vmem_bytes=16 * _nbytes((tm, d), F32))


def _mlp_bwd(dh_b, w_down, up, z2):
    tp, d = dh_b.shape
    fc = w_down.shape[1]
    ff = N_CHIPS * fc
    tn = _divisor_tile(fc, LANES, 512)
    per = fc // tn

    def body(dh_ref, z_ref, w_ref, up_ref, dup_ref, gd_ref, gu_ref):
        dh = dh_ref[...]
        r = jnp.maximum(up_ref[...].astype(F32), 0.0)
        dup = (_dot_nt(dh, w_ref[...]) * (2.0 * r)).astype(BF16)
        dup_ref[...] = dup
        gd_ref[...] = _dot_tn((r * r).astype(BF16), dh).astype(BF16)
        gu_ref[...] = _dot_tn(z_ref[...], dup).astype(BF16)

    col = pl.BlockSpec((tp, tn), lambda j: (0, j))
    vm = 4 * _nbytes((tp, d), BF16) + 4 * _nbytes((tn, d), BF16) + 2 * _nbytes((d, tn), BF16) \
        + 10 * _nbytes((tp, tn), F32) + 4 * _nbytes((tn, d), F32)
    return _call(body, (dh_b, z2, w_down, up), name="mlp_bwd", grid=(ff // tn,),
                 in_specs=[_full((tp, d)), _full((tp, d)),
                           pl.BlockSpec((None, tn, d), lambda j: (j // per, j % per, 0)), col],
                 out_specs=[col, pl.BlockSpec((tn, d), lambda j: (j, 0)),
                            pl.BlockSpec((None, d, tn), lambda j: (j // per, 0, j % per))],
                 out_shape=[jax.ShapeDtypeStruct((tp, ff), BF16), jax.ShapeDtypeStruct((ff, d), BF16),
                            jax.ShapeDtypeStruct((N_CHIPS, d, fc), BF16)],
                 semantics=("parallel",), vmem_bytes=vm)


def _grad_w_pieces(pieces, b):
    tp, n = b.shape
    tn = _divisor_tile(n, LANES, 512)
    widths = [pc.shape[1] for pc in pieces]

    def body(*refs):
        p_refs, b_ref, o_refs = refs[:len(pieces)], refs[len(pieces)], refs[len(pieces) + 1:]
        for p_ref, o_ref in zip(p_refs, o_refs):
            o_ref[...] = _dot_tn(p_ref[...], b_ref[...]).astype(BF16)

    vm = 2 * sum(_nbytes((tp, wd), BF16) for wd in widths) + 2 * _nbytes((tp, tn), BF16) \
        + 4 * sum(_nbytes((wd, tn), F32) for wd in widths) + 2 * _nbytes((tp, max(widths)), F32)
    return _call(body, tuple(pieces) + (b,), name="grad_w_pieces", grid=(n // tn,),
                 in_specs=[_full(pc.shape) for pc in pieces] + [pl.BlockSpec((tp, tn), lambda j: (0, j))],
                 out_specs=[pl.BlockSpec((wd, tn), lambda j: (0, j)) for wd in widths],
                 out_shape=[jax.ShapeDtypeStruct((wd, n), BF16) for wd in widths],
                 semantics=("parallel",), vmem_bytes=vm)


def _dx_norm_bwd(pieces, w, w_spec, w_piece, h, g, dres, dot=_dot_nt):
    tp, d = h.shape
    tm = _divisor_tile(tp, 16, ROW_TARGET)
    n = len(pieces)

    def body(*refs):
        dy_refs = refs[:n]
        w_ref, h_ref, g_ref, dres_ref, dh_ref, dhb_ref, dg_ref = refs[n:]

        @pl.when(pl.program_id(0) == 0)
        def _():
            dg_ref[...] = jnp.zeros_like(dg_ref)

        dz = dot(dy_refs[0][...], w_piece(w_ref, 0))
        for i in range(1, n):
            dz = dz + dot(dy_refs[i][...], w_piece(w_ref, i))
        dx, dgp = _rms_bwd(dz, h_ref[...], g_ref[...])
        dh = dres_ref[...] + dx
        dh_ref[...] = dh
        dhb_ref[...] = dh.astype(BF16)
        dg_ref[...] += jnp.broadcast_to(dgp, (SUBLANES, d))

    row = lambda wd: pl.BlockSpec((tm, wd), lambda i: (i, 0))
    kk = sum(wd for _, _, wd in pieces)
    vm = 2 * _nbytes((d, kk), BF16) + 2 * _nbytes((tm, kk), BF16) + 14 * _nbytes((tm, d), F32)
    piece_specs = [pl.BlockSpec((tm, wd), functools.partial(lambda i, cb: (i, cb), cb=cb)) for _, cb, wd in pieces]
    return _call(body, tuple(a for a, _, _ in pieces) + (w, h, g, dres), name="dx_norm_bwd", grid=(tp // tm,),
                 in_specs=piece_specs + [w_spec, row(d), _full((1, d)), row(d)],
                 out_specs=[row(d), row(d), _full((SUBLANES, d))],
                 out_shape=[jax.ShapeDtypeStruct((tp, d), F32), jax.ShapeDtypeStruct((tp, d), BF16),
                            jax.ShapeDtypeStruct((SUBLANES, d), F32)],
                 semantics=("arbitrary",), vmem_bytes=vm)


def _block_diag(wg):
    nb, b, _ = wg.shape
    eye = jnp.eye(nb, dtype=wg.dtype)
    return (eye[:, None, :, None] * wg[:, :, None, :]).reshape(nb * b, nb * b)


def _diag_blocks(dense, nb):
    b = dense.shape[0] // nb
    d4 = dense.reshape(nb, b, nb, b)
    return jnp.stack([d4[i, :, i, :] for i in range(nb)])


def _row(v):
    return v.reshape(1, -1)


def _forward_layer(l, h, z, p, fetch, stage_next, g_next):
    d = h.shape[1]
    att_w = d // 2
    rec_w = d - att_w
    nh = att_w // HEAD_DIM
    wa_d = _block_diag(p["w_gate_a"][l]).astype(BF16)
    wx_d = _block_diag(p["w_gate_x"][l]).astype(BF16)
    b_f_pad = jnp.zeros((1, LANES), F32).at[0, :nh].set(p["b_f"][l])
    w_in_t = fetch("w_in", z)
    big = dict(w_in_big=_pack_w_in_t(w_in_t.reshape(-1, d), att_w, nh))
    qkv, proj = _proj(z, big["w_in_big"], att_w)
    c, c_t = _fgate_fwd(proj, b_f_pad, nh)
    attn, lse_b = _attn_fwd(qkv, c, c_t, nh)
    hr, rec = _rec_fwd(proj, 0, 1, rec_w, p["conv_w"][l], _row(p["conv_b"][l]), wa_d,
                       _row(p["b_gate_a"][l]), wx_d, _row(p["b_gate_x"][l]), _row(p["lru_L"][l]))
    tok = stage_next(attn, "own")
    big["w_out"] = fetch("w_out", rec)
    h1, z2, mix = _mixer_out(attn, rec, _row(p["attn_out_g"][l] + tok), _row(p["rec_out_g"][l]),
                             big["w_out"], h, _row(p["mlp_norm_g"][l]))
    big["w_up"] = fetch("w_up", h1)
    act, up = _mlp_up(z2, big["w_up"])
    tok = stage_next(act, "next")
    big["w_down"] = fetch("w_down", act)
    h2, z_next = _mlp_down(act, big["w_down"], h1, _row(g_next + tok))
    saved = dict(h0=h, z1=z, proj=proj, qkv=qkv, c=c, c_t=c_t, attn=attn, lse_b=lse_b, hr=hr, rec=rec, h1=h1,
                 z2=z2, mix=mix, up=up, wa_d=wa_d, wx_d=wx_d, b_f_pad=b_f_pad, big=big)
    return h2, z_next, saved


def _backward_mlp(l, dh, dh_b, sv, p, tok):
    w_up, w_down = sv["big"]["w_up"], sv["big"]["w_down"]
    fc = w_up.shape[2]
    dup, g_down, g_up = _mlp_bwd(dh_b, w_down, sv["up"], sv["z2"])
    dh, dh_b, dg2 = _dx_norm_bwd([(dup, j, fc) for j in range(N_CHIPS)], w_up, _full(w_up.shape),
                                 lambda w_ref, j: w_ref[j], sv["h1"], _row(p["mlp_norm_g"][l] + tok), dh)
    big = dict(w_down=g_down.reshape((N_CHIPS, -1) + g_down.shape[1:]), w_up=g_up)
    return dh, dh_b, big, dict(mlp_norm_g=dg2[0])


def _backward_mixer(l, dh, dh_b, sv, p, tok):
    d = dh.shape[1]
    att_w = d // 2
    rec_w = d - att_w
    nh = att_w // HEAD_DIM
    small = {}
    dattn, drec, dg_mix = _mixer_bwd(dh_b, sv["big"]["w_out"], sv["attn"], sv["rec"],
                                     _row(p["attn_out_g"][l] + tok), _row(p["rec_out_g"][l]))
    small["attn_out_g"] = dg_mix[0, :att_w]
    small["rec_out_g"] = dg_mix[0, att_w:]
    dxr, dyr, dwa, dwx, sm = _rec_bwd(
        sv["proj"], 0, 1, rec_w, sv["hr"], drec, p["conv_w"][l], _row(p["conv_b"][l]), sv["wa_d"],
        _row(p["b_gate_a"][l]), sv["wx_d"], _row(p["b_gate_x"][l]), _row(p["lru_L"][l]))
    small.update(conv_w=sm[:CONV_WIDTH], conv_b=sm[4], b_gate_a=sm[5], b_gate_x=sm[6], lru_L=sm[7],
                 w_gate_a=_diag_blocks(dwa, N_REC_BLOCKS), w_gate_x=_diag_blocks(dwx, N_REC_BLOCKS))
    dq, dk, dv, dc = _attn_bwd(sv["qkv"], sv["c"], sv["c_t"], sv["lse_b"], dattn, nh)
    df, db_f = _fgate_bwd(sv["proj"], sv["b_f_pad"], dc)
    small["b_f"] = db_f[0, :nh]
    pieces = [dq, dk, dv, dxr, dyr, df]
    offs = [0, att_w, 2 * att_w, 3 * att_w, 3 * att_w + rec_w, 3 * att_w + 2 * rec_w]
    gq, gk, gv, gxr, gyr, gf = _grad_w_pieces(pieces, sv["z1"])
    g_in_t = jnp.concatenate([gq, gk, gv, gf[:nh], gxr, gyr], axis=0)
    w_big = sv["big"]["w_in_big"]
    widths = [pc.shape[1] for pc in pieces]
    dh, dh_b, dg1 = _dx_norm_bwd(
        [(pc, 0, wd) for pc, wd in zip(pieces, widths)], w_big, _full(w_big.shape),
        lambda w_ref, i: w_ref[offs[i]:offs[i] + widths[i], :], sv["h0"], _row(p["attn_norm_g"][l]), dh, dot=_dot)
    small["attn_norm_g"] = dg1[0]
    big = dict(w_in=g_in_t.reshape(N_CHIPS, -1, d))
    return dh, dh_b, big, small


def _pack_w_in_t(w_in_t, att_w, nh):
    qkv = w_in_t[:3 * att_w]
    f = w_in_t[3 * att_w:3 * att_w + nh]
    xy = w_in_t[3 * att_w + nh:]
    return jnp.concatenate([qkv, xy, f, jnp.zeros((LANES - nh, w_in_t.shape[1]), w_in_t.dtype)], axis=0)


ANY = pl.BlockSpec(memory_space=pl.ANY)


def _coords():
    return lax.axis_index("x"), lax.axis_index("y"), lax.axis_index("c")


def _other_chips(x, y):
    return [(1 - x, y), (x, 1 - y), (1 - x, 1 - y)]


def _remote(src, dst, send_sems, recv_sems, k, to):
    return pltpu.make_async_remote_copy(src_ref=src, dst_ref=dst, send_sem=send_sems.at[k],
                                        recv_sem=recv_sems.at[k], device_id=to, device_id_type=MESH)


def _all_gather_chips(shards):
    n = len(shards)
    per = 6

    def body(*refs):
        ins, outs = refs[:n], refs[n:2 * n]
        send_sems, recv_sems, local_sems = refs[2 * n:]
        x, y, c = _coords()
        me = 2 * x + y
        sibling = (x, y, 1 - c)
        chips = _other_chips(x, y)
        local = [pltpu.make_async_copy(ins[t], outs[t].at[me], local_sems.at[t]) for t in range(n)]
        for cp in local:
            cp.start()
        sends = []
        for t in range(n):
            for j, (px, py) in enumerate(chips):
                cp = _remote(ins[t].at[c], outs[t].at[me, c], send_sems, recv_sems, per * t + j, (px, py, c))
                cp.start()
                sends.append(cp)
        for t in range(n):
            for j, (px, py) in enumerate(chips):
                landed = outs[t].at[2 * px + py, c]
                _remote(landed, landed, send_sems, recv_sems, per * t + j, (px, py, c)).wait_recv()
                cp = _remote(landed, landed, send_sems, recv_sems, per * t + 3 + j, sibling)
                cp.start()
                sends.append(cp)
        for t in range(n):
            for j, (px, py) in enumerate(chips):
                passed = outs[t].at[2 * px + py, 1 - c]
                _remote(passed, passed, send_sems, recv_sems, per * t + 3 + j, sibling).wait_recv()
        for cp in sends:
            cp.wait_send()
        for cp in local:
            cp.wait()

    return _call(body, tuple(shards), name="all_gather_chips",
                 in_specs=[ANY] * n, out_specs=[ANY] * n,
                 out_shape=[jax.ShapeDtypeStruct((N_CHIPS,) + s.shape, s.dtype) for s in shards],
                 scratch_shapes=[pltpu.SemaphoreType.DMA((per * n,)), pltpu.SemaphoreType.DMA((per * n,)),
                                 pltpu.SemaphoreType.DMA((n,))])


HBM = pl.BlockSpec(memory_space=pltpu.HBM)
SEM = pl.BlockSpec(memory_space=pltpu.SEMAPHORE)
DATAFLOW = pltpu.SideEffectType.DATAFLOW_SIDE_EFFECTING


def _in_hbm(a):
    return pltpu.with_memory_space_constraint(a, pltpu.HBM)


PUSH_ARRIVALS = {"gather_chips_half": N_CHIPS - 1, "pass_halves": N_CHIPS - 1, "scatter_chips": N_CHIPS - 1,
                 "sibling": 1, "gather_devices": N_DEV - 1}


def _column_half(ref3, slab, c):
    hw = ref3.shape[2] // 2
    return ref3.at[slab, :, pl.ds(pl.multiple_of(c * hw, LANES), hw)]


def _push_copies(mode, src, land, send_sems, recv_sems, t):
    x, y, c = _coords()
    chip = 2 * x + y
    if mode == "gather_chips_half":
        return [_remote(_column_half(src, chip, c), _column_half(land, chip, c), send_sems, recv_sems, t, (px, py, c))
                for px, py in _other_chips(x, y)]
    if mode == "pass_halves":
        return [_remote(_column_half(src, 2 * px + py, c), _column_half(land, 2 * px + py, c), send_sems, recv_sems, t,
                        (x, y, 1 - c)) for px, py in _other_chips(x, y)]
    if mode == "scatter_chips":
        return [_remote(src.at[2 * px + py], land.at[chip], send_sems, recv_sems, t, (px, py, c))
                for px, py in _other_chips(x, y)]
    if mode == "sibling":
        return [_remote(src, land, send_sems, recv_sems, t, (x, y, 1 - c))]
    dev = 4 * x + 2 * y + c
    return [_remote(src.at[dev], land.at[dev], send_sems, recv_sems, t, (x ^ (k >> 2), y ^ ((k >> 1) & 1), c ^ (k & 1)))
            for k in range(1, N_DEV)]


def _push_start(srcs, lands, mode, name):
    n = len(srcs)
    same = all(s is ld for s, ld in zip(srcs, lands))
    n_in = n if same else 2 * n

    def body(*refs):
        src_refs = refs[:n]
        land_refs = src_refs if same else refs[n:2 * n]
        send_sems, recv_sems = refs[n_in], refs[n_in + 1]
        token = refs[-1]
        for t in range(n):
            for cp in _push_copies(mode, src_refs[t], land_refs[t], send_sems, recv_sems, t):
                cp.start()
        token[...] = jnp.zeros_like(token)

    operands = tuple(srcs) if same else tuple(srcs) + tuple(lands)
    res = _call(
        body, [_in_hbm(a) for a in operands], name=name,
        out_shape=(pltpu.SemaphoreType.DMA((n,)), pltpu.SemaphoreType.DMA((n,)))
        + tuple(pltpu.HBM(a.shape, a.dtype) for a in operands) + (jax.ShapeDtypeStruct((SUBLANES, LANES), F32),),
        in_specs=[HBM] * n_in, out_specs=(SEM, SEM) + (HBM,) * n_in + (pl.BlockSpec(memory_space=pltpu.VMEM),),
        input_output_aliases={i: 2 + i for i in range(n_in)}, side_effects=DATAFLOW, hbm_results=False)
    send_sems, recv_sems, token = res[0], res[1], res[-1]
    srcs_thru = res[2:2 + n]
    lands_thru = srcs_thru if same else res[2 + n:2 + 2 * n]
    return send_sems, recv_sems, srcs_thru, lands_thru, token


def _push_wait(send_sems, recv_sems, ids, srcs, lands, mode, after, name):
    n = len(lands)
    same = all(s is ld for s, ld in zip(srcs, lands))
    n_in = n if same else 2 * n

    def body(*refs):
        land_refs = refs[:n] if same else refs[n:2 * n]
        send_sems, recv_sems = refs[n_in], refs[n_in + 1]
        x, y, c = _coords()
        for t in range(n):
            if mode == "sibling":
                moved = land_refs[t]
            elif mode in ("gather_chips_half", "pass_halves"):
                moved = land_refs[t].at[pl.ds(0, PUSH_ARRIVALS[mode]), :, pl.ds(0, land_refs[t].shape[2] // 2)]
            else:
                moved = land_refs[t].at[pl.ds(0, PUSH_ARRIVALS[mode])]
            arrivals = _remote(moved, moved, send_sems, recv_sems, ids[t], (x, y, c))
            arrivals.wait_send()
            arrivals.wait_recv()

    operands = tuple(lands) if same else tuple(srcs) + tuple(lands)
    res = _call(
        body, operands + (send_sems, recv_sems, after), name=name,
        out_shape=tuple(pltpu.HBM(a.shape, a.dtype) for a in operands),
        in_specs=[HBM] * n_in + [SEM, SEM, ANY], out_specs=(HBM,) * n_in,
        input_output_aliases={i: i for i in range(n_in)}, side_effects=DATAFLOW)
    return list(res) if same else (list(res[:n]), list(res[n:]))


def _sum_partials(part, landed, chip):
    _, rows, cols = part.shape
    br = _divisor_tile(rows, 16, ELEM_ROWS)

    def body(chip_ref, own_ref, a_ref, b_ref, c_ref, o_ref):
        o_ref[...] = ((own_ref[...].astype(F32) + a_ref[...].astype(F32)) + b_ref[...].astype(F32)) \
            + c_ref[...].astype(F32)

    def other(k):
        return pl.BlockSpec((None, br, cols), lambda i, ch: (jnp.where(ch[0] <= k, k + 1, k), i, 0))

    spec = pltpu.PrefetchScalarGridSpec(
        num_scalar_prefetch=1, grid=(rows // br,),
        in_specs=[pl.BlockSpec((None, br, cols), lambda i, ch: (ch[0], i, 0)), other(0), other(1), other(2)],
        out_specs=pl.BlockSpec((br, cols), lambda i, ch: (i, 0)))
    return _call(body, (chip, part, landed, landed, landed), name="sum_partials", grid_spec=spec,
                 out_shape=jax.ShapeDtypeStruct((rows, cols), F32), semantics=("parallel",))


def _cast_to_slab(w, l, chip):
    _, rows, cols = w.shape
    br = _divisor_tile(rows, 16, ELEM_ROWS)

    def body(chip_ref, w_ref, o_ref):
        o_ref[...] = w_ref[...].astype(BF16)

    spec = pltpu.PrefetchScalarGridSpec(
        num_scalar_prefetch=1, grid=(rows // br,),
        in_specs=[pl.BlockSpec((None, br, cols), lambda i, ch: (l, i, 0))],
        out_specs=pl.BlockSpec((None, br, cols), lambda i, ch: (ch[0], i, 0)))
    return _call(body, (chip, w), name="cast_to_slab", grid_spec=spec,
                 out_shape=jax.ShapeDtypeStruct((N_CHIPS, rows, cols), BF16), semantics=("parallel",))


def _cast_w_in_t_to_slabs(w_t, chip):
    rows, depth, d = w_t.shape
    tn = _divisor_tile(d, LANES, 256)

    def body(chip_ref, w_ref, *o_refs):
        for l in range(depth):
            o_refs[l][...] = w_ref[:, l, :].astype(BF16)

    spec = pltpu.PrefetchScalarGridSpec(
        num_scalar_prefetch=1, grid=(d // tn,),
        in_specs=[pl.BlockSpec((rows, depth, tn), lambda j, ch: (0, 0, j))],
        out_specs=[pl.BlockSpec((None, rows, tn), lambda j, ch: (ch[0], 0, j))] * depth)
    return _call(body, (chip, w_t), name="cast_w_in_t_to_slabs", grid_spec=spec,
                 out_shape=[jax.ShapeDtypeStruct((N_CHIPS, rows, d), BF16)] * depth, semantics=("parallel",),
                 vmem_bytes=4 * _nbytes((rows, max(depth, SUBLANES), tn), F32))


def _place_slab(buf, index, n_slabs):
    rows, cols = buf.shape
    br = _divisor_tile(rows, SUBLANES, ELEM_ROWS)

    def body(index_ref, b_ref, o_ref):
        o_ref[...] = b_ref[...]

    spec = pltpu.PrefetchScalarGridSpec(
        num_scalar_prefetch=1, grid=(rows // br,),
        in_specs=[pl.BlockSpec((br, cols), lambda i, ix: (i, 0))],
        out_specs=pl.BlockSpec((None, br, cols), lambda i, ix: (ix[0], i, 0)))
    return _call(body, (index, buf), name="place_slab", grid_spec=spec,
                 out_shape=jax.ShapeDtypeStruct((n_slabs, rows, cols), buf.dtype), semantics=("parallel",))


ELEM_ROWS = 256


def _sum_slabs(r):
    n, rows, cols = r.shape
    br = _divisor_tile(rows, 16, ELEM_ROWS)

    def body(r_ref, o_ref):
        acc = r_ref[0].astype(F32)
        for j in range(1, n):
            acc = acc + r_ref[j].astype(F32)
        o_ref[...] = acc

    return _call(body, (r,), name="sum_slabs", grid=(rows // br,),
                 in_specs=[pl.BlockSpec((n, br, cols), lambda i: (0, i, 0))],
                 out_specs=pl.BlockSpec((br, cols), lambda i: (i, 0)),
                 out_shape=jax.ShapeDtypeStruct((rows, cols), F32), semantics=("parallel",))


def _adamw_math(w, g, m, v):
    c1 = 1.0 - ADAM_B1 ** ADAM_STEP
    c2 = 1.0 - ADAM_B2 ** ADAM_STEP
    nm = ADAM_B1 * m + (1.0 - ADAM_B1) * g
    nv = ADAM_B2 * v + (1.0 - ADAM_B2) * (g * g)
    delta = -ADAM_LR * ((nm / c1) / (jnp.sqrt(nv / c2) + ADAM_EPS) + ADAM_WD * w)
    return delta, nm, nv


def _adamw(w, g, m, v):
    rows, cols = w.shape
    br = _divisor_tile(rows, 8, ELEM_ROWS)

    def body(w_ref, g_ref, m_ref, v_ref, d_ref, nm_ref, nv_ref):
        d_ref[...], nm_ref[...], nv_ref[...] = _adamw_math(w_ref[...], g_ref[...], m_ref[...], v_ref[...])

    blk = pl.BlockSpec((br, cols), lambda i: (i, 0))
    return _call(body, (w, g, m, v), name="adamw", grid=(rows // br,),
                 in_specs=[blk] * 4, out_specs=[blk] * 3,
                 out_shape=[jax.ShapeDtypeStruct((rows, cols), F32)] * 3, semantics=("parallel",))


def _adamw_w_in_t(w_t, m_t, v_t, g_mine, g_theirs):
    rows, depth, d = w_t.shape
    tn = LANES

    def body(w_ref, m_ref, v_ref, *rest):
        ga_refs, gb_refs = rest[:depth], rest[depth:2 * depth]
        g_ref, d_ref, nm_ref, nv_ref = rest[2 * depth:]
        for l in range(depth):
            g_ref[:, l, :] = ga_refs[l][...] + gb_refs[l][...]
        d_ref[...], nm_ref[...], nv_ref[...] = _adamw_math(w_ref[...], g_ref[...], m_ref[...], v_ref[...])

    slab = pl.BlockSpec((rows, depth, tn), lambda j: (0, 0, j))
    gblk = pl.BlockSpec((rows, tn), lambda j: (0, j))
    return _call(body, (w_t, m_t, v_t) + tuple(g_mine) + tuple(g_theirs), name="adamw_w_in_t", grid=(d // tn,),
                 in_specs=[slab] * 3 + [gblk] * (2 * depth), out_specs=[slab] * 4,
                 out_shape=[jax.ShapeDtypeStruct(w_t.shape, F32)] * 4, semantics=("parallel",),
                 vmem_bytes=2 * (7 * _nbytes((rows, max(depth, SUBLANES), tn), F32)
                                 + 2 * depth * _nbytes((rows, tn), F32)))


def _adamw_layer(w, m, v, l, g_mine, g_theirs, prev, after):
    _, rows, cols = w.shape
    br = _divisor_tile(rows, 8, ELEM_ROWS)

    def body(w_ref, m_ref, v_ref, ga_ref, gb_ref, *rest):
        g_ref, d_ref, nm_ref, nv_ref = rest[5:]
        g = ga_ref[...] + gb_ref[...]
        g_ref[...] = g
        d_ref[...], nm_ref[...], nv_ref[...] = _adamw_math(w_ref[...], g, m_ref[...], v_ref[...])

    slot = pl.BlockSpec((None, br, cols), lambda i: (l, i, 0))
    blk = pl.BlockSpec((br, cols), lambda i: (i, 0))
    return _call(body, (w, m, v, g_mine, g_theirs) + tuple(prev) + (after,), name="adamw_layer",
                 grid=(rows // br,), in_specs=[slot] * 3 + [blk] * 2 + [ANY] * 5, out_specs=[slot] * 4,
                 out_shape=[jax.ShapeDtypeStruct(w.shape, F32)] * 4,
                 input_output_aliases={5: 0, 6: 1, 7: 2, 8: 3}, semantics=("parallel",))


BIG = ("w_in", "w_out", "w_up", "w_down")
WEIGHTS = ("meta", "attn_norm_g", "w_in", "b_f", "conv_w", "conv_b", "w_gate_a", "b_gate_a", "w_gate_x",
           "b_gate_x", "lru_L", "attn_out_g", "rec_out_g", "w_out", "mlp_norm_g", "w_up", "w_down", "final_g")
SMALL = tuple(k for k in WEIGHTS if k not in BIG)
COL_SHARDED_SMALL = ("meta", "conv_w")


def _packed_rows(shape):
    return -(-math.prod(shape) // (SUBLANES * LANES)) * SUBLANES


def _pack(arrs):
    rows = []
    for a in arrs:
        flat = a.reshape(-1)
        rows.append(jnp.pad(flat, (0, _packed_rows(a.shape) * LANES - flat.shape[0])).reshape(-1, LANES))
    used = sum(r.shape[0] for r in rows)
    rows.append(jnp.zeros((-used % ELEM_ROWS, LANES), F32))
    return jnp.concatenate(rows, axis=0)


def _unpack(buf, shapes):
    out, r0 = [], 0
    for s in shapes:
        nr = _packed_rows(s)
        out.append(buf[r0:r0 + nr].reshape(-1)[:math.prod(s)].reshape(s))
        r0 += nr
    return out


def _halves(a):
    return a.reshape((2, a.shape[0] // 2) + a.shape[1:])


def _cols_from_chips(g):
    return jnp.moveaxis(g, 0, -2).reshape(g.shape[1:-1] + (N_CHIPS * g.shape[-1],))


def kernel(x, meta, attn_norm_g, w_in, b_f, conv_w, conv_b, w_gate_a, b_gate_a, w_gate_x, b_gate_x, lru_L, attn_out_g, rec_out_g, w_out, mlp_norm_g, w_up, w_down, final_g, loss_target, m_meta, m_attn_norm_g, m_w_in, m_b_f, m_conv_w, m_conv_b, m_w_gate_a, m_b_gate_a, m_w_gate_x, m_b_gate_x, m_lru_L, m_attn_out_g, m_rec_out_g, m_w_out, m_mlp_norm_g, m_w_up, m_w_down, m_final_g, v_meta, v_attn_norm_g, v_w_in, v_b_f, v_conv_w, v_conv_b, v_w_gate_a, v_b_gate_a, v_w_gate_x, v_b_gate_x, v_lru_L, v_attn_out_g, v_rec_out_g, v_w_out, v_mlp_norm_g, v_w_up, v_w_down, v_final_g):
    w = dict(meta=meta, attn_norm_g=attn_norm_g, w_in=w_in, b_f=b_f, conv_w=conv_w, conv_b=conv_b,
             w_gate_a=w_gate_a, b_gate_a=b_gate_a, w_gate_x=w_gate_x, b_gate_x=b_gate_x, lru_L=lru_L,
             attn_out_g=attn_out_g, rec_out_g=rec_out_g, w_out=w_out, mlp_norm_g=mlp_norm_g, w_up=w_up,
             w_down=w_down, final_g=final_g)
    m = dict(meta=m_meta, attn_norm_g=m_attn_norm_g, w_in=m_w_in, b_f=m_b_f, conv_w=m_conv_w, conv_b=m_conv_b,
             w_gate_a=m_w_gate_a, b_gate_a=m_b_gate_a, w_gate_x=m_w_gate_x, b_gate_x=m_b_gate_x, lru_L=m_lru_L,
             attn_out_g=m_attn_out_g, rec_out_g=m_rec_out_g, w_out=m_w_out, mlp_norm_g=m_mlp_norm_g,
             w_up=m_w_up, w_down=m_w_down, final_g=m_final_g)
    v = dict(meta=v_meta, attn_norm_g=v_attn_norm_g, w_in=v_w_in, b_f=v_b_f, conv_w=v_conv_w, conv_b=v_conv_b,
             w_gate_a=v_w_gate_a, b_gate_a=v_b_gate_a, w_gate_x=v_w_gate_x, b_gate_x=v_b_gate_x, lru_L=v_lru_L,
             attn_out_g=v_attn_out_g, rec_out_g=v_rec_out_g, w_out=v_w_out, mlp_norm_g=v_mlp_norm_g,
             w_up=v_w_up, w_down=v_w_down, final_g=v_final_g)
    s_len, d = x.shape[1], x.shape[2]
    depth = w_in.shape[0]
    att_w = d // 2
    rec_w = d - att_w
    nh = att_w // HEAD_DIM
    chip = 2 * lax.axis_index("x") + lax.axis_index("y")

    g_conv, g_meta = [g.reshape((N_CHIPS, g.shape[1] * g.shape[2]) + g.shape[3:])
                      for g in _all_gather_chips([_halves(w["conv_w"]), _halves(w["meta"])])]
    p = dict(w)
    p["conv_w"] = _cols_from_chips(g_conv)
    meta_full = jnp.moveaxis(g_meta, 0, 1).reshape(N_META, d)

    chip1 = chip.reshape(1).astype(jnp.int32)
    w_in_t, m_in_t, v_in_t = [jnp.transpose(a["w_in"], (2, 0, 1)) for a in (w, m, v)]
    w_in_slabs = _cast_w_in_t_to_slabs(w_in_t, chip1)
    pushes, tokens = [], []
    for l in range(depth):
        slabs = [w_in_slabs[l]] + [_cast_to_slab(w[k], l, chip1) for k in BIG[1:]]
        send_sems, recv_sems, _, lands, token = _push_start(slabs, slabs, "gather_chips_half", f"weights_start_{l}")
        pushes.append((send_sems, recv_sems, lands))
        tokens.append(token[0, 0])
    passed = [{} for _ in range(depth)]

    def stage(l, after, ids):
        send_sems, recv_sems, lands = pushes[l]
        tag = "_".join(BIG[i] for i in ids)
        sub = [lands[i] for i in ids]
        sub = _push_wait(send_sems, recv_sems, ids, sub, sub, "gather_chips_half", after, f"{tag}_wait_{l}")
        send_sems, recv_sems, _, sub, token = _push_start(sub, sub, "pass_halves", f"{tag}_pass_{l}")
        for j, i in enumerate(ids):
            passed[l][i] = (send_sems, recv_sems, sub[j], j)
        return token[0, 0]

    t_len = N_META + s_len
    pad = -t_len % SEQ_TILE
    h = jnp.concatenate([meta_full, x[0], jnp.zeros((pad, d), F32)], axis=0)
    tgt = jnp.concatenate([jnp.zeros((N_META, d), F32), loss_target[0], jnp.zeros((pad, d), F32)], axis=0)
    z = _rms_fwd(h, _row(p["attn_norm_g"][0] + sum(tokens)))
    stage(0, z, [0])
    saved = []
    for l in range(depth):
        def fetch(k, after, l=l):
            send_sems, recv_sems, land, j = passed[l][BIG.index(k)]
            return _push_wait(send_sems, recv_sems, [j], [land], [land], "pass_halves", after,
                              f"{k}_here_{l}")[0]

        def stage_next(after, which, l=l):
            if which == "own":
                return stage(l, after, [1, 2, 3])
            return stage(l + 1, after, [0]) if l + 1 < depth else 0.0

        g_next = p["attn_norm_g"][l + 1] if l + 1 < depth else p["final_g"]
        h, z, sv = _forward_layer(l, h, z, p, fetch, stage_next, g_next)
        saved.append(sv)
    dh, dh_b, dg_final, loss_part = _loss_bwd(h, _row(p["final_g"]), tgt, s_len)

    small = {k: [None] * depth for k in SMALL if k not in ("meta", "final_g")}
    pushes = [None] * depth
    tok = 0.0
    for l in reversed(range(depth)):
        dh, dh_b, big_mlp, sm_mlp = _backward_mlp(l, dh, dh_b, saved[l], p, tok)
        g_out, = _grad_w_pieces([saved[l]["mix"]], dh_b)
        parts = [big_mlp["w_down"], big_mlp["w_up"], g_out.reshape((N_CHIPS, -1) + g_out.shape[1:])]
        push_mlp = _push_start(parts, [lax.empty(a.shape, a.dtype) for a in parts], "scatter_chips",
                               f"mlp_grads_start_{l}")
        dh, dh_b, big_mix, sm_mix = _backward_mixer(l, dh, dh_b, saved[l], p, push_mlp[4][0, 0])
        parts = [big_mix["w_in"]]
        push_mix = _push_start(parts, [lax.empty(a.shape, a.dtype) for a in parts], "scatter_chips",
                               f"mixer_grads_start_{l}")
        tok = push_mix[4][0, 0]
        pushes[l] = {("w_down", "w_up", "w_out"): push_mlp, ("w_in",): push_mix}
        for k, val in {**sm_mlp, **sm_mix}.items():
            small[k][l] = val
    grads = {k: jnp.stack(val) for k, val in small.items()}
    grads["final_g"] = dg_final[0]
    grads["meta"] = dh[:N_META]
    dx = dh[N_META:t_len]

    full_shapes = [grads[k].shape for k in SMALL] + [(1,)]
    packed = _pack([grads[k].astype(F32) for k in SMALL] + [loss_part[0, :1] + tok])
    dev1 = (2 * chip + lax.axis_index("c")).reshape(1).astype(jnp.int32)
    slabs = [_place_slab(packed, dev1, N_DEV)]
    small_push = _push_start(slabs, slabs, "gather_devices", "small_grads_start")

    last_token = small_push[4]
    outs = {k: [lax.empty(w[k].shape, F32) for _ in range(4)] for k in BIG[1:]}
    w_in_sums = [None] * depth
    swaps = {}

    def finish(l, wait_after, adam_after):
        send_sems, recv_sems, mine, lands, _ = swaps[l]
        mine, theirs = _push_wait(send_sems, recv_sems, list(range(len(BIG))), mine, lands, "sibling", wait_after,
                                  f"sums_wait_{l}")
        w_in_sums[l] = (mine[0], theirs[0])
        for k, a, b in zip(BIG[1:], mine[1:], theirs[1:]):
            outs[k] = _adamw_layer(w[k], m[k], v[k], l, a, b, outs[k], adam_after)

    for l in reversed(range(depth)):
        sums = {}
        for names, (send_sems, recv_sems, parts, lands, _) in pushes[l].items():
            parts, landed = _push_wait(send_sems, recv_sems, list(range(len(names))), parts, lands, "scatter_chips",
                                       last_token,
                                       f"{names[0]}_grads_wait_{l}")
            for k, part, land in zip(names, parts, landed):
                sums[k] = _sum_partials(part, land, chip1)
        mine = [sums[k] for k in BIG]
        swaps[l] = _push_start(mine, [lax.empty(a.shape, a.dtype) for a in mine], "sibling", f"sums_start_{l}")
        if l + 1 < depth:
            finish(l + 1, outs["w_down"][0] if l + 2 < depth else mine[0], swaps[l][4])
    finish(0, outs["w_down"][0] if depth > 1 else swaps[0][4], swaps[0][4])
    outs["w_in"] = [jnp.transpose(r, (1, 2, 0)) for r in _adamw_w_in_t(
        w_in_t, m_in_t, v_in_t, [s[0] for s in w_in_sums], [s[1] for s in w_in_sums])]
    out_g, out_d, out_m, out_v = [{k: outs[k][i] for k in BIG} for i in range(4)]

    landed = _push_wait(small_push[0], small_push[1], [0], small_push[3], small_push[3], "gather_devices",
                        out_g["w_in"], "small_grads_wait")
    total = _sum_slabs(landed[0])
    small_g = dict(zip(SMALL + ("loss",), _unpack(total, full_shapes)))
    for k in COL_SHARDED_SMALL:
        n = w[k].shape[-1]
        small_g[k] = lax.dynamic_slice_in_dim(small_g[k], chip * n, n, axis=small_g[k].ndim - 1)
    local_shapes = [w[k].shape for k in SMALL]
    res = _adamw(_pack([w[k] for k in SMALL]), _pack([small_g[k] for k in SMALL]),
                 _pack([m[k] for k in SMALL]), _pack([v[k] for k in SMALL]))
    out_g.update({k: small_g[k] for k in SMALL})
    for dst, buf in zip((out_d, out_m, out_v), res):
        dst.update(zip(SMALL, _unpack(buf, local_shapes)))

    return (small_g["loss"].reshape(()), dx[None],
            *[out_g[k] for k in WEIGHTS], *[out_d[k] for k in WEIGHTS],
            *[out_m[k] for k in WEIGHTS], *[out_v[k] for k in WEIGHTS])
```

```python
import functools
import math

import jax
import jax.numpy as jnp
from jax import lax
from jax.experimental import pallas as pl
from jax.experimental.pallas import tpu as pltpu

F32 = jnp.float32
BF16 = jnp.bfloat16

N_META = 16
HEAD_DIM = 64
N_REC_BLOCKS = 8
CONV_WIDTH = 4
RG_C = 8.0
NORM_EPS = 1e-6
ADAM_LR = 0.001
ADAM_B1 = 0.9
ADAM_B2 = 0.999
ADAM_EPS = 1e-08
ADAM_WD = 0.01
ADAM_STEP = 10

LANES = 128
SUBLANES = 8
SEQ_TILE = 128
VMEM_CAP = 60 * 2**20
VMEM_SLACK = 6 * 2**20
NEG_BIG = -1e30
N_CHIPS = 4
N_DEV = 8
MESH = pl.DeviceIdType.MESH


def _nbytes(shape, dtype):
    return math.prod(shape) * jnp.dtype(dtype).itemsize


def _call(body, args, *, name, out_shape, grid=(), in_specs=None, out_specs=None, scratch_shapes=(),
          grid_spec=None, semantics=None, vmem_bytes=None, side_effects=None, hbm_results=True, **kw):
    cp = {}
    if semantics is not None:
        cp["dimension_semantics"] = semantics
    if vmem_bytes is not None:
        cp["vmem_limit_bytes"] = int(min(VMEM_CAP, vmem_bytes + VMEM_SLACK))
    if side_effects is not None:
        cp["has_side_effects"] = side_effects
    if grid_spec is not None:
        kw["grid_spec"] = grid_spec
    else:
        kw.update(grid=grid, in_specs=in_specs, out_specs=out_specs, scratch_shapes=scratch_shapes)
    if hbm_results:
        out_shape = jax.tree.map(
            lambda s: pltpu.HBM(s.shape, s.dtype) if isinstance(s, jax.ShapeDtypeStruct) else s, out_shape)
    fn = pl.pallas_call(
        body, name=name, out_shape=out_shape,
        compiler_params=pltpu.CompilerParams(**cp), **kw)
    return fn(*[_in_hbm(a) if jnp.issubdtype(getattr(a, "dtype", jnp.int32), jnp.floating) else a for a in args])


def _divisor_tile(n, unit, target):
    best = None
    for t in range(unit, min(n, target) + 1, unit):
        if n % t == 0:
            best = t
    return n if best is None else best


def _sigmoid(x):
    return 1.0 / (1.0 + jnp.exp(-x))


def _log1p_unit(e):
    series = e * (1.0 - e * (0.5 - e * (1.0 / 3.0)))
    return jnp.where(e < 1e-2, series, jnp.log(1.0 + e))


def _log_sigmoid(x):
    return jnp.minimum(x, 0.0) - _log1p_unit(jnp.exp(-jnp.abs(x)))


def _one_minus_exp(x, exp_x):
    small = -x * (1.0 + x * (1.0 / 2 + x * (1.0 / 6 + x * (1.0 / 24 + x * (1.0 / 120 + x * (1.0 / 720))))))
    return jnp.where(x > -0.25, small, 1.0 - exp_x)


_GELU_K = math.sqrt(2.0 / math.pi)
_GELU_C = 0.044715


def _gelu_and_grad(y):
    th = jnp.tanh(_GELU_K * (y + _GELU_C * y * y * y))
    g = 0.5 * y * (1.0 + th)
    dg = 0.5 * (1.0 + th) + 0.5 * y * (1.0 - th * th) * _GELU_K * (1.0 + 3.0 * _GELU_C * y * y)
    return g, dg


def _rstd(x):
    return lax.rsqrt(jnp.mean(x * x, axis=-1, keepdims=True) + NORM_EPS)


def _rms_bwd(dz, x, g):
    rs = _rstd(x)
    xh = x * rs
    dgp = jnp.sum(dz * xh, axis=0, keepdims=True)
    dxh = dz * g
    dx = rs * (dxh - xh * jnp.mean(dxh * xh, axis=-1, keepdims=True))
    return dx, dgp


def _dot(a, b):
    return jnp.dot(a, b, preferred_element_type=F32)


def _dot_nt(a, b):
    return lax.dot_general(a, b, (((1,), (1,)), ((), ())), preferred_element_type=F32)


def _dot_tn(a, b):
    return lax.dot_general(a, b, (((0,), (0,)), ((), ())), preferred_element_type=F32)


def _full(shape):
    nd = len(shape)
    return pl.BlockSpec(shape, lambda *_: (0,) * nd)


def _rms_fwd(h, g):
    tp, d = h.shape
    tm = _divisor_tile(tp, 16, 544)

    def body(h_ref, g_ref, z_ref):
        x = h_ref[...]
        z_ref[...] = (x * _rstd(x) * g_ref[...]).astype(BF16)

    return _call(body, (h, g), name="rms_fwd", grid=(tp // tm,),
                 in_specs=[pl.BlockSpec((tm, d), lambda i: (i, 0)), _full((1, d))],
                 out_specs=pl.BlockSpec((tm, d), lambda i: (i, 0)),
                 out_shape=jax.ShapeDtypeStruct((tp, d), BF16), semantics=("parallel",))


def _proj(z, w_big_t, att_w):
    tp, d = z.shape
    nb = w_big_t.shape[0]
    tn = _divisor_tile(nb, LANES, 512)
    assert (3 * att_w) % tn == 0
    n_qkv = 3 * att_w // tn
    scale = 1.0 / math.sqrt(HEAD_DIM)

    def body(z_ref, w_ref, qkv_ref, p_ref):
        j = pl.program_id(0)
        acc = _dot_nt(z_ref[...], w_ref[...])

        @pl.when(j < n_qkv)
        def _():
            col = j * tn + lax.broadcasted_iota(jnp.int32, (1, tn), 1)
            qkv_ref[...] = (acc * jnp.where(col < att_w, scale, 1.0)).astype(BF16)

        @pl.when(j >= n_qkv)
        def _():
            p_ref[...] = acc

    vm = 2 * (_nbytes((tp, d), BF16) + _nbytes((d, tn), BF16) + _nbytes((tp, tn), F32) * 2)
    return _call(body, (z, w_big_t), name="proj", grid=(nb // tn,),
                 in_specs=[_full((tp, d)), pl.BlockSpec((tn, d), lambda j: (j, 0))],
                 out_specs=[pl.BlockSpec((tp, tn), lambda j: (0, jnp.minimum(j, n_qkv - 1))),
                            pl.BlockSpec((tp, tn), lambda j: (0, jnp.maximum(j - n_qkv, 0)))],
                 out_shape=[jax.ShapeDtypeStruct((tp, 3 * att_w), BF16),
                            jax.ShapeDtypeStruct((tp, nb - 3 * att_w), F32)],
                 semantics=("arbitrary",), vmem_bytes=vm)


def _tile_cumsum(x, row, reverse=False):
    for s in (1, 2, 4):
        if reverse:
            x = x + jnp.where(row < SUBLANES - s, pltpu.roll(x, SUBLANES - s, 0), 0.0)
        else:
            x = x + jnp.where(row >= s, pltpu.roll(x, s, 0), 0.0)
    return x


def _fgate_fwd(proj, b_f_pad, nh):
    tp, nb = proj.shape
    fblk = nb // LANES - 1

    def body(f_ref, b_ref, c_ref, ct_ref):
        b = b_ref[...]
        row = lax.broadcasted_iota(jnp.int32, (SUBLANES, LANES), 0)

        def step(i, carry):
            r0 = pl.multiple_of(i * SUBLANES, SUBLANES)
            lf = _log_sigmoid(f_ref[pl.ds(r0, SUBLANES), :] + b)
            x = _tile_cumsum(lf, row) + carry
            c_ref[pl.ds(r0, SUBLANES), :] = x
            return x[SUBLANES - 1:SUBLANES, :]

        lax.fori_loop(0, tp // SUBLANES, step, jnp.zeros((1, LANES), F32))
        ct_ref[...] = c_ref[...].T[:nh, :]

    return _call(body, (proj, b_f_pad), name="fgate_fwd", grid=(1,),
                 in_specs=[pl.BlockSpec((tp, LANES), lambda i: (0, fblk)), _full((1, LANES))],
                 out_specs=[_full((tp, LANES)), _full((nh, tp))],
                 out_shape=[jax.ShapeDtypeStruct((tp, LANES), F32), jax.ShapeDtypeStruct((nh, tp), F32)],
                 semantics=("arbitrary",))


def _fgate_bwd(proj, b_f_pad, dc):
    tp, nb = proj.shape
    fblk = nb // LANES - 1

    def body(f_ref, b_ref, dc_ref, df_ref, db_ref, dc_s):
        b = b_ref[...]
        row = lax.broadcasted_iota(jnp.int32, (SUBLANES, LANES), 0)
        nt = tp // SUBLANES

        def step(i, carry):
            suffix, acc = carry
            r0 = pl.multiple_of((nt - 1 - i) * SUBLANES, SUBLANES)
            dlf = _tile_cumsum(dc_ref[pl.ds(r0, SUBLANES), :], row, reverse=True) + suffix
            df = dlf * _sigmoid(-(f_ref[pl.ds(r0, SUBLANES), :] + b))
            dc_s[pl.ds(r0, SUBLANES), :] = df
            return dlf[0:1, :], acc + df

        _, acc = lax.fori_loop(0, nt, step, (jnp.zeros((1, LANES), F32), jnp.zeros((SUBLANES, LANES), F32)))
        df_ref[...] = dc_s[...].astype(BF16)
        db_ref[...] = jnp.broadcast_to(jnp.sum(acc, axis=0, keepdims=True), (SUBLANES, LANES))

    return _call(body, (proj, b_f_pad, dc), name="fgate_bwd", grid=(1,),
                 in_specs=[pl.BlockSpec((tp, LANES), lambda i: (0, fblk)), _full((1, LANES)), _full((tp, LANES))],
                 out_specs=[_full((tp, LANES)), _full((SUBLANES, LANES))],
                 out_shape=[jax.ShapeDtypeStruct((tp, LANES), BF16),
                            jax.ShapeDtypeStruct((SUBLANES, LANES), F32)],
                 scratch_shapes=[pltpu.VMEM((tp, LANES), F32)], semantics=("arbitrary",))


ATT_BQ = 128


ATT_BUCKET = 3
ATT_HEADS = 4


def _for_bucket(i, nq, fn):
    for lo in range(0, nq, ATT_BUCKET):
        hi = min(lo + ATT_BUCKET, nq)
        spans = ([(0, lo * ATT_BQ, False)] if lo else []) + [(lo * ATT_BQ, hi * ATT_BQ, True)]
        pl.when(jnp.logical_and(i >= lo, i < hi))(functools.partial(fn, spans))


def _head_column(c_blk, h):
    lane = lax.broadcasted_iota(jnp.int32, c_blk.shape, 1)
    return jnp.sum(jnp.where(lane == h, c_blk, 0.0), axis=1, keepdims=True)


def _head_columns_into(p, c_ref, ck_s):
    for hh in range(ATT_HEADS):
        ck_s[hh] = jnp.broadcast_to(_head_column(c_ref[...], ATT_HEADS * p + hh), ck_s.shape[1:])


def _pair_diag_cols(x2):
    top = lax.broadcasted_iota(jnp.int32, (LANES, ATT_BQ), 0) < HEAD_DIM
    xt = x2.astype(F32).T.astype(BF16)
    return jnp.concatenate([jnp.where(top, xt, 0), jnp.where(top, 0, xt)], axis=1)


def _pair_diag_rows(x2):
    low = lax.broadcasted_iota(jnp.int32, (ATT_BQ, LANES), 1) < HEAD_DIM
    return jnp.concatenate([jnp.where(low, x2, 0), jnp.where(low, 0, x2)], axis=0)


def _seen_keys(k0, k1, q0):
    keys = k0 + lax.broadcasted_iota(jnp.int32, (k1 - k0, ATT_BQ), 0)
    return keys <= q0 + lax.broadcasted_iota(jnp.int32, (k1 - k0, ATT_BQ), 1)


def _attn_fwd(qkv, c, c_t, nh):
    tp = qkv.shape[0]
    att_w = nh * HEAD_DIM
    ng = nh // ATT_HEADS
    gw = ATT_HEADS * HEAD_DIM
    bq = ATT_BQ
    nq = tp // bq
    pair = 2 * HEAD_DIM
    assert pair == LANES and ATT_HEADS % 2 == 0

    def body(q_ref, k_ref, v_ref, c_ref, ct_ref, o_ref, lse_ref, ck_s, vt_s):
        p = pl.program_id(0)
        i = pl.program_id(1)

        @pl.when(i == 0)
        def _():
            _head_columns_into(p, c_ref, ck_s)
            vt_s[...] = v_ref[...].astype(F32).T.astype(BF16)

        def compute(spans):
            q0 = pl.multiple_of(i * bq, bq)
            o_t, lses = [], []
            for pi in range(ATT_HEADS // 2):
                lo = pair * pi
                heads = (2 * pi, 2 * pi + 1)
                q_cols = _pair_diag_cols(q_ref[:, lo:lo + pair])
                ts = []
                for k0, k1, needs_mask in spans:
                    t2 = _dot(k_ref[k0:k1, lo:lo + pair], q_cols)
                    t_e = [t2[:, e * bq:(e + 1) * bq] - ck_s[hh, k0:k1, :] for e, hh in enumerate(heads)]
                    if needs_mask:
                        seen = _seen_keys(k0, k1, q0)
                        t_e = [jnp.where(seen, t, NEG_BIG) for t in t_e]
                    ts.append(t_e)
                ms = [functools.reduce(jnp.maximum, [jnp.max(t[e], axis=0, keepdims=True) for t in ts])
                      for e in range(2)]
                es = [[jnp.exp(t[e] - ms[e]) for e in range(2)] for t in ts]
                ls = [sum(jnp.sum(e_[e], axis=0, keepdims=True) for e_ in es) for e in range(2)]
                o2 = sum(_dot(vt_s[lo:lo + pair, k0:k1],
                              jnp.concatenate([e_[0].astype(BF16), e_[1].astype(BF16)], axis=1))
                         for e_, (k0, k1, _) in zip(es, spans))
                o_t += [o2[:HEAD_DIM, :bq] / ls[0], o2[HEAD_DIM:, bq:] / ls[1]]
                lses += [ms[e] + ct_ref[pl.ds(ATT_HEADS * p + hh, 1), :] + jnp.log(ls[e])
                         for e, hh in enumerate(heads)]
            o_ref[...] = jnp.concatenate(o_t, axis=0).T
            lse_ref[...] = jnp.concatenate(lses, axis=0)

        _for_bucket(i, nq, compute)

    blk = pl.BlockSpec((bq, gw), lambda p, i: (i, p))
    vm = 6 * _nbytes((tp, gw), BF16) + 2 * ATT_HEADS * _nbytes((tp, LANES), F32) + 2 * _nbytes((tp, LANES), F32) \
        + 8 * ATT_HEADS * _nbytes((bq, tp), F32)
    return _call(body, (qkv, qkv, qkv, c, c_t), name="attn_fwd", grid=(ng, nq),
                 in_specs=[blk,
                           pl.BlockSpec((tp, gw), lambda p, i: (0, ng + p)),
                           pl.BlockSpec((tp, gw), lambda p, i: (0, 2 * ng + p)),
                           _full((tp, LANES)), pl.BlockSpec((nh, bq), lambda p, i: (0, i))],
                 out_specs=[blk, pl.BlockSpec((None, ATT_HEADS, bq), lambda p, i: (p, 0, i))],
                 out_shape=[jax.ShapeDtypeStruct((tp, att_w), F32), jax.ShapeDtypeStruct((ng, ATT_HEADS, tp), F32)],
                 scratch_shapes=[pltpu.VMEM((ATT_HEADS, tp, LANES), F32), pltpu.VMEM((gw, tp), BF16)],
                 semantics=("arbitrary", "arbitrary"), vmem_bytes=vm)


def _attn_bwd(qkv, c, c_t, lse, do, nh):
    tp = qkv.shape[0]
    att_w = nh * HEAD_DIM
    ng = nh // ATT_HEADS
    gw = ATT_HEADS * HEAD_DIM
    bq = ATT_BQ
    nq = tp // bq
    pair = 2 * HEAD_DIM
    assert pair == LANES and ATT_HEADS % 2 == 0
    scale = 1.0 / math.sqrt(HEAD_DIM)

    def body(q_ref, k_ref, v_ref, c_ref, ct_ref, lse_ref, do_ref, dq_ref, dk_ref, dv_ref, dc_ref,
             dk_s, dv_s, dc_s, ck_s, kt_s):
        p = pl.program_id(0)
        i = pl.program_id(1)

        @pl.when(i == 0)
        def _():
            dk_s[...] = jnp.zeros_like(dk_s)
            dv_s[...] = jnp.zeros_like(dv_s)
            dc_s[...] = jnp.zeros_like(dc_s)
            kt_s[...] = k_ref[...].astype(F32).T.astype(BF16)
            _head_columns_into(p, c_ref, ck_s)

        @pl.when(jnp.logical_and(i == 0, p == 0))
        def _():
            dc_ref[...] = jnp.zeros_like(dc_ref)

        def compute(spans):
            q0 = pl.multiple_of(i * bq, bq)
            dq_t = []
            for pi in range(ATT_HEADS // 2):
                lo = pair * pi
                q2 = q_ref[:, lo:lo + pair]
                do2 = do_ref[:, lo:lo + pair].astype(BF16)
                q_cols, do_cols = _pair_diag_cols(q2), _pair_diag_cols(do2)
                q_rows, do_rows = _pair_diag_rows(q2), _pair_diag_rows(do2)
                heads = (2 * pi, 2 * pi + 1)
                col_terms = [ct_ref[pl.ds(ATT_HEADS * p + hh, 1), :] - lse_ref[hh:hh + 1, :] for hh in heads]
                prs, dps = [], []
                for k0, k1, needs_mask in spans:
                    t2 = _dot(k_ref[k0:k1, lo:lo + pair], q_cols)
                    dp2 = _dot(v_ref[k0:k1, lo:lo + pair], do_cols)
                    if needs_mask:
                        seen = _seen_keys(k0, k1, q0)
                    pr_e, dp_e = [], []
                    for e, hh in enumerate(heads):
                        t = t2[:, e * bq:(e + 1) * bq] - ck_s[hh, k0:k1, :]
                        if needs_mask:
                            t = jnp.where(seen, t, NEG_BIG)
                        pr_e.append(jnp.exp(t + col_terms[e]))
                        dp_e.append(dp2[:, e * bq:(e + 1) * bq])
                    prs.append(pr_e)
                    dps.append(dp_e)
                key_sums = [sum(jnp.sum(pr[e] * dp[e], axis=0, keepdims=True) for pr, dp in zip(prs, dps))
                            for e in range(2)]
                dq2 = 0.0
                for (k0, k1, _), pr, dp in zip(spans, prs, dps):
                    ds = [pr[e] * (dp[e] - key_sums[e]) for e in range(2)]
                    for e, hh in enumerate(heads):
                        dc_s[hh, k0:k1, :] += jnp.sum(ds[e], axis=1, keepdims=True)
                    ds2 = jnp.concatenate([ds[0].astype(BF16), ds[1].astype(BF16)], axis=1)
                    pr2 = jnp.concatenate([pr[0].astype(BF16), pr[1].astype(BF16)], axis=1)
                    dk_s[k0:k1, lo:lo + pair] += _dot(ds2, q_rows)
                    dv_s[k0:k1, lo:lo + pair] += _dot(pr2, do_rows)
                    dq2 = dq2 + _dot(kt_s[lo:lo + pair, k0:k1], ds2)
                dq_t.append(jnp.concatenate([dq2[:HEAD_DIM, :bq], dq2[HEAD_DIM:, bq:]], axis=0))
            dq_ref[...] = (jnp.concatenate(dq_t, axis=0) * scale).T.astype(BF16)

        _for_bucket(i, nq, compute)

        @pl.when(i == nq - 1)
        def _():
            dk_ref[...] = dk_s[...].astype(BF16)
            dv_ref[...] = dv_s[...].astype(BF16)
            lane = lax.broadcasted_iota(jnp.int32, (tp, LANES), 1)
            dc = dc_ref[...]
            for hh in range(ATT_HEADS):
                dc = jnp.where(lane == ATT_HEADS * p + hh, -dc_s[hh], dc)
            dc_ref[...] = dc

    blk = pl.BlockSpec((bq, gw), lambda p, i: (i, p))
    col = pl.BlockSpec((tp, gw), lambda p, i: (0, p))
    vm = 7 * _nbytes((tp, gw), BF16) + 2 * _nbytes((tp, gw), F32) + 2 * ATT_HEADS * _nbytes((tp, LANES), F32) \
        + 2 * _nbytes((tp, LANES), F32) + 12 * ATT_HEADS * _nbytes((bq, tp), F32)
    return _call(body, (qkv, qkv, qkv, c, c_t, lse, do), name="attn_bwd", grid=(ng, nq),
                 in_specs=[blk,
                           pl.BlockSpec((tp, gw), lambda p, i: (0, ng + p)),
                           pl.BlockSpec((tp, gw), lambda p, i: (0, 2 * ng + p)),
                           _full((tp, LANES)), pl.BlockSpec((nh, bq), lambda p, i: (0, i)),
                           pl.BlockSpec((None, ATT_HEADS, bq), lambda p, i: (p, 0, i)), blk],
                 out_specs=[blk, col, col, _full((tp, LANES))],
                 out_shape=[jax.ShapeDtypeStruct((tp, att_w), BF16)] * 3 + [jax.ShapeDtypeStruct((tp, LANES), F32)],
                 scratch_shapes=[pltpu.VMEM((tp, gw), F32), pltpu.VMEM((tp, gw), F32),
                                 pltpu.VMEM((ATT_HEADS, tp, 1), F32), pltpu.VMEM((ATT_HEADS, tp, LANES), F32),
                                 pltpu.VMEM((gw, tp), BF16)],
                 semantics=("arbitrary", "arbitrary"), vmem_bytes=vm)


REC_ROWS = 128
HALO = SUBLANES


def _conv_taps(cat):
    taps = []
    for k in range(CONV_WIDTH):
        sh = CONV_WIDTH - 1 - k
        taps.append((pltpu.roll(cat, sh, 0) if sh else cat)[HALO:])
    return taps


def _rec_gates(xc, wa_ref, ba_ref, wx_ref, bx_ref, l_ref):
    xcb = xc.astype(BF16)
    r = _sigmoid(_dot(xcb, wa_ref[...]) + ba_ref[...])
    ig = _sigmoid(_dot(xcb, wx_ref[...]) + bx_ref[...])
    ls = _log_sigmoid(l_ref[...])
    log_a = RG_C * r * ls
    return xcb, r, ig, ls, log_a


def _rec_fwd(proj, xr_blk, yr_blk, rec_w, conv_w, conv_b, wa, ba, wx, bx, lru):
    tp = proj.shape[0]
    w = rec_w
    r_rows = REC_ROWS
    nc = tp // r_rows
    cpb = w // LANES

    def body(xr_ref, yr_ref, cw_ref, cb_ref, wa_ref, ba_ref, wx_ref, bx_ref, l_ref,
             hr_ref, rec_ref, prev_s, carry_s, a_s, u_s):
        i = pl.program_id(0)

        @pl.when(i == 0)
        def _():
            prev_s[...] = jnp.zeros_like(prev_s)
            carry_s[...] = jnp.zeros_like(carry_s)

        x = xr_ref[...]
        taps = _conv_taps(jnp.concatenate([prev_s[...], x], axis=0))
        prev_s[...] = x[r_rows - HALO:]
        xc = cb_ref[...]
        for k in range(CONV_WIDTH):
            xc = xc + cw_ref[k:k + 1, :] * taps[k]
        _, r, ig, ls, log_a = _rec_gates(xc, wa_ref, ba_ref, wx_ref, bx_ref, l_ref)
        a = jnp.exp(log_a)
        a_s[...] = a
        u_s[...] = jnp.sqrt(_one_minus_exp(2.0 * log_a, a * a)) * ig * xc

        def tile(j, h):
            r0 = pl.multiple_of(j * SUBLANES, SUBLANES)
            at = a_s[pl.ds(r0, SUBLANES), :]
            ut = u_s[pl.ds(r0, SUBLANES), :]
            out = []
            for rr in range(SUBLANES):
                h = at[rr:rr + 1] * h + ut[rr:rr + 1]
                out.append(h)
            hr_ref[pl.ds(r0, SUBLANES), :] = jnp.concatenate(out, axis=0)
            return h

        carry_s[0:1, :] = lax.fori_loop(0, r_rows // SUBLANES, tile, carry_s[0:1, :])
        g, _ = _gelu_and_grad(yr_ref[...])
        rec_ref[...] = hr_ref[...] * g

    blk = pl.BlockSpec((r_rows, w), lambda i: (i, 0))
    vm = 16 * _nbytes((r_rows, w), F32) + 4 * _nbytes((w, w), BF16)
    return _call(body, (proj, proj, conv_w, conv_b, wa, ba, wx, bx, lru), name="rec_fwd", grid=(nc,),
                 in_specs=[pl.BlockSpec((r_rows, w), lambda i: (i, xr_blk)),
                           pl.BlockSpec((r_rows, w), lambda i: (i, yr_blk)),
                           _full((CONV_WIDTH, w)), _full((1, w)), _full((w, w)), _full((1, w)),
                           _full((w, w)), _full((1, w)), _full((1, w))],
                 out_specs=[blk, blk],
                 out_shape=[jax.ShapeDtypeStruct((tp, w), F32)] * 2,
                 scratch_shapes=[pltpu.VMEM((HALO, w), F32), pltpu.VMEM((SUBLANES, w), F32),
                                 pltpu.VMEM((r_rows, w), F32), pltpu.VMEM((r_rows, w), F32)],
                 semantics=("arbitrary",), vmem_bytes=vm)


def _rec_bwd(proj, xr_blk, yr_blk, rec_w, hr, drec, conv_w, conv_b, wa, ba, wx, bx, lru):
    tp = proj.shape[0]
    w = rec_w
    r_rows = REC_ROWS
    nc = tp // r_rows
    hpc = r_rows // HALO

    def body(xr_ref, xh_ref, yr_ref, hr_ref, hh_ref, drec_ref, cw_ref, cb_ref, wa_ref, ba_ref, wx_ref, bx_ref,
             l_ref, dxr_ref, dyr_ref, dwa_ref, dwx_ref, small_ref, lam_s, a_s, dhr_s, carry_s, next_s):
        i = pl.program_id(0)
        first = (nc - 1 - i) == 0

        @pl.when(i == 0)
        def _():
            carry_s[...] = jnp.zeros_like(carry_s)
            next_s[...] = jnp.zeros_like(next_s)
            dwa_ref[...] = jnp.zeros_like(dwa_ref)
            dwx_ref[...] = jnp.zeros_like(dwx_ref)
            small_ref[...] = jnp.zeros_like(small_ref)

        x = xr_ref[...]
        xprev = jnp.where(first, 0.0, xh_ref[...])
        taps = _conv_taps(jnp.concatenate([xprev, x], axis=0))
        xc = cb_ref[...]
        for k in range(CONV_WIDTH):
            xc = xc + cw_ref[k:k + 1, :] * taps[k]
        xcb, r, ig, ls, log_a = _rec_gates(xc, wa_ref, ba_ref, wx_ref, bx_ref, l_ref)
        a = jnp.exp(log_a)
        a2 = a * a
        mult = jnp.sqrt(_one_minus_exp(2.0 * log_a, a2))
        g, dg = _gelu_and_grad(yr_ref[...])
        hr_v = hr_ref[...]
        drec_v = drec_ref[...]
        dhr_s[...] = drec_v * g
        dyr_ref[...] = (drec_v * hr_v * dg).astype(BF16)
        a_s[...] = a

        def tile(jj, carry):
            r0 = pl.multiple_of((r_rows // SUBLANES - 1 - jj) * SUBLANES, SUBLANES)
            at = a_s[pl.ds(r0, SUBLANES), :]
            dt = dhr_s[pl.ds(r0, SUBLANES), :]
            out = [None] * SUBLANES
            for rr in range(SUBLANES - 1, -1, -1):
                lam = dt[rr:rr + 1] + carry
                out[rr] = lam
                carry = at[rr:rr + 1] * lam
            lam_s[pl.ds(r0, SUBLANES), :] = jnp.concatenate(out, axis=0)
            return carry

        carry_s[0:1, :] = lax.fori_loop(0, r_rows // SUBLANES, tile, carry_s[0:1, :])
        lam = lam_s[...]
        hprev = jnp.where(first, 0.0, hh_ref[...])
        hr_prev = pltpu.roll(jnp.concatenate([hprev, hr_v], axis=0), 1, 0)[HALO:]
        da = lam * hr_prev
        dxc = lam * mult * ig
        di = lam * mult * xc
        dmult = lam * ig * xc
        dlog_a = da * a - dmult * a2 / mult
        dr = dlog_a * (RG_C * ls)
        dls = jnp.sum(dlog_a * (RG_C * r), axis=0, keepdims=True)
        dga = dr * r * (1.0 - r)
        dgx = di * ig * (1.0 - ig)
        dgab = dga.astype(BF16)
        dgxb = dgx.astype(BF16)
        dxc = dxc + _dot_nt(dgab, wa_ref[...]) + _dot_nt(dgxb, wx_ref[...])
        dwa_ref[...] += _dot_tn(xcb, dgab)
        dwx_ref[...] += _dot_tn(xcb, dgxb)
        cat = jnp.concatenate([dxc, next_s[...]], axis=0)
        next_s[...] = dxc[0:HALO]
        dxr = cw_ref[CONV_WIDTH - 1:CONV_WIDTH, :] * dxc
        for k in range(CONV_WIDTH - 1):
            sh = CONV_WIDTH - 1 - k
            dxr = dxr + cw_ref[k:k + 1, :] * pltpu.roll(cat, r_rows + HALO - sh, 0)[:r_rows]
        dxr_ref[...] = dxr.astype(BF16)
        rows = [jnp.sum(dxc * taps[k], axis=0, keepdims=True) for k in range(CONV_WIDTH)]
        rows += [jnp.sum(dxc, axis=0, keepdims=True), jnp.sum(dga, axis=0, keepdims=True),
                 jnp.sum(dgx, axis=0, keepdims=True), dls * _sigmoid(-l_ref[...])]
        small_ref[...] += jnp.concatenate(rows, axis=0)

    def rev(i):
        return nc - 1 - i

    def halo(i):
        return jnp.maximum(rev(i) * hpc - 1, 0)

    blk = pl.BlockSpec((r_rows, w), lambda i: (rev(i), 0))
    vm = 40 * _nbytes((r_rows, w), F32) + 6 * _nbytes((w, w), F32)
    return _call(body, (proj, proj, proj, hr, hr, drec, conv_w, conv_b, wa, ba, wx, bx, lru),
                 name="rec_bwd", grid=(nc,),
                 in_specs=[pl.BlockSpec((r_rows, w), lambda i: (rev(i), xr_blk)),
                           pl.BlockSpec((HALO, w), lambda i: (halo(i), xr_blk)),
                           pl.BlockSpec((r_rows, w), lambda i: (rev(i), yr_blk)),
                           blk,
                           pl.BlockSpec((HALO, w), lambda i: (halo(i), 0)),
                           blk,
                           _full((CONV_WIDTH, w)), _full((1, w)), _full((w, w)), _full((1, w)),
                           _full((w, w)), _full((1, w)), _full((1, w))],
                 out_specs=[blk, blk, _full((w, w)), _full((w, w)), _full((SUBLANES, w))],
                 out_shape=[jax.ShapeDtypeStruct((tp, w), BF16)] * 2
                 + [jax.ShapeDtypeStruct((w, w), F32)] * 2 + [jax.ShapeDtypeStruct((SUBLANES, w), F32)],
                 scratch_shapes=[pltpu.VMEM((r_rows, w), F32)] * 3
                 + [pltpu.VMEM((SUBLANES, w), F32), pltpu.VMEM((HALO, w), F32)],
                 semantics=("arbitrary",), vmem_bytes=vm)


ROW_TARGET = 544


def _mixer_out(attn, rec, g_a, g_r, w_out, h, g_next):
    tp, d = h.shape
    aw, rw = attn.shape[1], rec.shape[1]
    kc = d // N_CHIPS
    tm = _divisor_tile(tp, 16, ROW_TARGET)

    def body(a_ref, r_ref, ga_ref, gr_ref, w_ref, h_ref, gn_ref, h1_ref, z_ref, mix_ref):
        a = a_ref[...]
        r = r_ref[...]
        mix = jnp.concatenate([a * _rstd(a) * ga_ref[...], r * _rstd(r) * gr_ref[...]], axis=1).astype(BF16)
        mix_ref[...] = mix
        h1 = h_ref[...]
        for j in range(N_CHIPS):
            h1 = h1 + _dot(mix[:, j * kc:(j + 1) * kc], w_ref[j])
        h1_ref[...] = h1
        z_ref[...] = (h1 * _rstd(h1) * gn_ref[...]).astype(BF16)

    row = lambda wd: pl.BlockSpec((tm, wd), lambda i: (i, 0))
    vm = 2 * _nbytes((d, d), BF16) + 12 * _nbytes((tm, d), F32)
    return _call(body, (attn, rec, g_a, g_r, w_out, h, g_next), name="mixer_out", grid=(tp // tm,),
                 in_specs=[row(aw), row(rw), _full((1, aw)), _full((1, rw)), _full(w_out.shape), row(d),
                           _full((1, d))],
                 out_specs=[row(d), row(d), row(d)],
                 out_shape=[jax.ShapeDtypeStruct((tp, d), F32), jax.ShapeDtypeStruct((tp, d), BF16),
                            jax.ShapeDtypeStruct((tp, d), BF16)],
                 semantics=("parallel",), vmem_bytes=vm)


def _mixer_bwd(dh_b, w_out, attn, rec, g_a, g_r):
    tp, d = dh_b.shape
    aw, rw = attn.shape[1], rec.shape[1]
    tm = _divisor_tile(tp, 16, ROW_TARGET)

    def body(dh_ref, w_ref, a_ref, r_ref, ga_ref, gr_ref, da_ref, dr_ref, dg_ref):
        @pl.when(pl.program_id(0) == 0)
        def _():
            dg_ref[...] = jnp.zeros_like(dg_ref)

        dh = dh_ref[...]
        dmix = jnp.concatenate([_dot_nt(dh, w_ref[j]) for j in range(N_CHIPS)], axis=1)
        da, dga = _rms_bwd(dmix[:, :aw], a_ref[...], ga_ref[...])
        dr, dgr = _rms_bwd(dmix[:, aw:], r_ref[...], gr_ref[...])
        da_ref[...] = da
        dr_ref[...] = dr
        dg_ref[...] += jnp.broadcast_to(jnp.concatenate([dga, dgr], axis=1), (SUBLANES, d))

    row = lambda wd: pl.BlockSpec((tm, wd), lambda i: (i, 0))
    vm = 2 * _nbytes((d, d), BF16) + 12 * _nbytes((tm, d), F32)
    return _call(body, (dh_b, w_out, attn, rec, g_a, g_r), name="mixer_bwd", grid=(tp // tm,),
                 in_specs=[row(d), _full(w_out.shape), row(aw), row(rw), _full((1, aw)), _full((1, rw))],
                 out_specs=[row(aw), row(rw), _full((SUBLANES, d))],
                 out_shape=[jax.ShapeDtypeStruct((tp, aw), F32), jax.ShapeDtypeStruct((tp, rw), F32),
                            jax.ShapeDtypeStruct((SUBLANES, d), F32)],
                 semantics=("arbitrary",), vmem_bytes=vm)


def _mlp_up(z, w_up):
    tp, d = z.shape
    fc = w_up.shape[2]
    ff = N_CHIPS * fc
    tn = _divisor_tile(fc, LANES, 512)
    per = fc // tn

    def body(z_ref, w_ref, act_ref, up_ref):
        up = _dot(z_ref[...], w_ref[...])
        r = jnp.maximum(up, 0.0)
        act_ref[...] = (r * r).astype(BF16)
        up_ref[...] = up.astype(BF16)

    col = pl.BlockSpec((tp, tn), lambda j: (0, j))
    vm = 2 * _nbytes((tp, d), BF16) + 2 * _nbytes((d, tn), BF16) + 8 * _nbytes((tp, tn), F32)
    return _call(body, (z, w_up), name="mlp_up", grid=(ff // tn,),
                 in_specs=[_full((tp, d)), pl.BlockSpec((None, d, tn), lambda j: (j // per, 0, j % per))],
                 out_specs=[col, col],
                 out_shape=[jax.ShapeDtypeStruct((tp, ff), BF16)] * 2,
                 semantics=("parallel",), vmem_bytes=vm)


def _mlp_down(act, w_down, h, g_next):
    tp, d = h.shape
    ff = act.shape[1]
    fc = ff // N_CHIPS
    tm = _divisor_tile(tp, 16, ROW_TARGET)

    def body(a_ref, w_ref, h_ref, gn_ref, h2_ref, z_ref):
        h2 = h_ref[...]
        for j in range(N_CHIPS):
            h2 = h2 + _dot(a_ref[:, j * fc:(j + 1) * fc], w_ref[j])
        h2_ref[...] = h2
        z_ref[...] = (h2 * _rstd(h2) * gn_ref[...]).astype(BF16)

    row = lambda wd: pl.BlockSpec((tm, wd), lambda i: (i, 0))
    vm = 2 * _nbytes((ff, d), BF16) + 2 * _nbytes((tm, ff), BF16) + 10 * _nbytes((tm, d), F32)
    return _call(body, (act, w_down, h, g_next), name="mlp_down", grid=(tp // tm,),
                 in_specs=[row(ff), _full(w_down.shape), row(d), _full((1, d))],
                 out_specs=[row(d), row(d)],
                 out_shape=[jax.ShapeDtypeStruct((tp, d), F32), jax.ShapeDtypeStruct((tp, d), BF16)],
                 semantics=("parallel",), vmem_bytes=vm)


def _loss_bwd(h, g, target, n_real):
    tp, d = h.shape
    tm = _divisor_tile(tp, 16, ROW_TARGET)

    def body(h_ref, g_ref, t_ref, dh_ref, dhb_ref, dg_ref, loss_ref):
        i = pl.program_id(0)

        @pl.when(i == 0)
        def _():
            dg_ref[...] = jnp.zeros_like(dg_ref)
            loss_ref[...] = jnp.zeros_like(loss_ref)

        x = h_ref[...]
        gv = g_ref[...]
        rowi = i * tm + lax.broadcasted_iota(jnp.int32, (tm, 1), 0)
        real = jnp.logical_and(rowi >= N_META, rowi < N_META + n_real)
        err = jnp.where(real, x * _rstd(x) * gv - t_ref[...], 0.0)
        loss_ref[...] += 0.5 * jnp.sum(jnp.mean(err * err, axis=-1, keepdims=True))
        dx, dgp = _rms_bwd(err * (1.0 / d), x, gv)
        dh_ref[...] = dx
        dhb_ref[...] = dx.astype(BF16)
        dg_ref[...] += jnp.broadcast_to(dgp, (SUBLANES, d))

    row = pl.BlockSpec((tm, d), lambda i: (i, 0))
    return _call(body, (h, g, target), name="loss_bwd", grid=(tp // tm,),
                 in_specs=[row, _full((1, d)), row],
                 out_specs=[row, row, _full((SUBLANES, d)), _full((SUBLANES, LANES))],
                 out_shape=[jax.ShapeDtypeStruct((tp, d), F32), jax.ShapeDtypeStruct((tp, d), BF16),
                            jax.ShapeDtypeStruct((SUBLANES, d), F32), jax.ShapeDtypeStruct((SUBLANES, LANES), F32)],
                 semantics=("arbitrary",), vmem_bytes=16 * _nbytes((tm, d), F32))


def _mlp_bwd(dh_b, w_down, up, z2):
    tp, d = dh_b.shape
    fc = w_down.shape[1]
    ff = N_CHIPS * fc
    tn = _divisor_tile(fc, LANES, 512)
    per = fc // tn

    def body(dh_ref, z_ref, w_ref, up_ref, dup_ref, gd_ref, gu_ref):
        dh = dh_ref[...]
        r = jnp.maximum(up_ref[...].astype(F32), 0.0)
        dup = (_dot_nt(dh, w_ref[...]) * (2.0 * r)).astype(BF16)
        dup_ref[...] = dup
        gd_ref[...] = _dot_tn((r * r).astype(BF16), dh).astype(BF16)
        gu_ref[...] = _dot_tn(z_ref[...], dup).astype(BF16)

    col = pl.BlockSpec((tp, tn), lambda j: (0, j))
    vm = 4 * _nbytes((tp, d), BF16) + 4 * _nbytes((tn, d), BF16) + 2 * _nbytes((d, tn), BF16) \
        + 10 * _nbytes((tp, tn), F32) + 4 * _nbytes((tn, d), F32)
    return _call(body, (dh_b, z2, w_down, up), name="mlp_bwd", grid=(ff // tn,),
                 in_specs=[_full((tp, d)), _full((tp, d)),
                           pl.BlockSpec((None, tn, d), lambda j: (j // per, j % per, 0)), col],
                 out_specs=[col, pl.BlockSpec((tn, d), lambda j: (j, 0)),
                            pl.BlockSpec((None, d, tn), lambda j: (j // per, 0, j % per))],
                 out_shape=[jax.ShapeDtypeStruct((tp, ff), BF16), jax.ShapeDtypeStruct((ff, d), BF16),
                            jax.ShapeDtypeStruct((N_CHIPS, d, fc), BF16)],
                 semantics=("parallel",), vmem_bytes=vm)


def _grad_w_pieces(pieces, b):
    tp, n = b.shape
    tn = _divisor_tile(n, LANES, 512)
    widths = [pc.shape[1] for pc in pieces]

    def body(*refs):
        p_refs, b_ref, o_refs = refs[:len(pieces)], refs[len(pieces)], refs[len(pieces) + 1:]
        for p_ref, o_ref in zip(p_refs, o_refs):
            o_ref[...] = _dot_tn(p_ref[...], b_ref[...]).astype(BF16)

    vm = 2 * sum(_nbytes((tp, wd), BF16) for wd in widths) + 2 * _nbytes((tp, tn), BF16) \
        + 4 * sum(_nbytes((wd, tn), F32) for wd in widths) + 2 * _nbytes((tp, max(widths)), F32)
    return _call(body, tuple(pieces) + (b,), name="grad_w_pieces", grid=(n // tn,),
                 in_specs=[_full(pc.shape) for pc in pieces] + [pl.BlockSpec((tp, tn), lambda j: (0, j))],
                 out_specs=[pl.BlockSpec((wd, tn), lambda j: (0, j)) for wd in widths],
                 out_shape=[jax.ShapeDtypeStruct((wd, n), BF16) for wd in widths],
                 semantics=("parallel",), vmem_bytes=vm)


def _dx_norm_bwd(pieces, w, w_spec, w_piece, h, g, dres, dot=_dot_nt):
    tp, d = h.shape
    tm = _divisor_tile(tp, 16, ROW_TARGET)
    n = len(pieces)

    def body(*refs):
        dy_refs = refs[:n]
        w_ref, h_ref, g_ref, dres_ref, dh_ref, dhb_ref, dg_ref = refs[n:]

        @pl.when(pl.program_id(0) == 0)
        def _():
            dg_ref[...] = jnp.zeros_like(dg_ref)

        dz = dot(dy_refs[0][...], w_piece(w_ref, 0))
        for i in range(1, n):
            dz = dz + dot(dy_refs[i][...], w_piece(w_ref, i))
        dx, dgp = _rms_bwd(dz, h_ref[...], g_ref[...])
        dh = dres_ref[...] + dx
        dh_ref[...] = dh
        dhb_ref[...] = dh.astype(BF16)
        dg_ref[...] += jnp.broadcast_to(dgp, (SUBLANES, d))

    row = lambda wd: pl.BlockSpec((tm, wd), lambda i: (i, 0))
    kk = sum(wd for _, _, wd in pieces)
    vm = 2 * _nbytes((d, kk), BF16) + 2 * _nbytes((tm, kk), BF16) + 14 * _nbytes((tm, d), F32)
    piece_specs = [pl.BlockSpec((tm, wd), functools.partial(lambda i, cb: (i, cb), cb=cb)) for _, cb, wd in pieces]
    return _call(body, tuple(a for a, _, _ in pieces) + (w, h, g, dres), name="dx_norm_bwd", grid=(tp // tm,),
                 in_specs=piece_specs + [w_spec, row(d), _full((1, d)), row(d)],
                 out_specs=[row(d), row(d), _full((SUBLANES, d))],
                 out_shape=[jax.ShapeDtypeStruct((tp, d), F32), jax.ShapeDtypeStruct((tp, d), BF16),
                            jax.ShapeDtypeStruct((SUBLANES, d), F32)],
                 semantics=("arbitrary",), vmem_bytes=vm)


def _block_diag(wg):
    nb, b, _ = wg.shape
    eye = jnp.eye(nb, dtype=wg.dtype)
    return (eye[:, None, :, None] * wg[:, :, None, :]).reshape(nb * b, nb * b)


def _diag_blocks(dense, nb):
    b = dense.shape[0] // nb
    d4 = dense.reshape(nb, b, nb, b)
    return jnp.stack([d4[i, :, i, :] for i in range(nb)])


def _row(v):
    return v.reshape(1, -1)


def _forward_layer(l, h, z, p, fetch, stage_next, g_next):
    d = h.shape[1]
    att_w = d // 2
    rec_w = d - att_w
    nh = att_w // HEAD_DIM
    wa_d = _block_diag(p["w_gate_a"][l]).astype(BF16)
    wx_d = _block_diag(p["w_gate_x"][l]).astype(BF16)
    b_f_pad = jnp.zeros((1, LANES), F32).at[0, :nh].set(p["b_f"][l])
    w_in_t = fetch("w_in", z)
    big = dict(w_in_big=_pack_w_in_t(w_in_t.reshape(-1, d), att_w, nh))
    qkv, proj = _proj(z, big["w_in_big"], att_w)
    c, c_t = _fgate_fwd(proj, b_f_pad, nh)
    attn, lse_b = _attn_fwd(qkv, c, c_t, nh)
    hr, rec = _rec_fwd(proj, 0, 1, rec_w, p["conv_w"][l], _row(p["conv_b"][l]), wa_d,
                       _row(p["b_gate_a"][l]), wx_d, _row(p["b_gate_x"][l]), _row(p["lru_L"][l]))
    tok = stage_next(attn, "own")
    big["w_out"] = fetch("w_out", rec)
    h1, z2, mix = _mixer_out(attn, rec, _row(p["attn_out_g"][l] + tok), _row(p["rec_out_g"][l]),
                             big["w_out"], h, _row(p["mlp_norm_g"][l]))
    big["w_up"] = fetch("w_up", h1)
    act, up = _mlp_up(z2, big["w_up"])
    tok = stage_next(act, "next")
    big["w_down"] = fetch("w_down", act)
    h2, z_next = _mlp_down(act, big["w_down"], h1, _row(g_next + tok))
    saved = dict(h0=h, z1=z, proj=proj, qkv=qkv, c=c, c_t=c_t, attn=attn, lse_b=lse_b, hr=hr, rec=rec, h1=h1,
                 z2=z2, mix=mix, up=up, wa_d=wa_d, wx_d=wx_d, b_f_pad=b_f_pad, big=big)
    return h2, z_next, saved


def _backward_mlp(l, dh, dh_b, sv, p, tok):
    w_up, w_down = sv["big"]["w_up"], sv["big"]["w_down"]
    fc = w_up.shape[2]
    dup, g_down, g_up = _mlp_bwd(dh_b, w_down, sv["up"], sv["z2"])
    dh, dh_b, dg2 = _dx_norm_bwd([(dup, j, fc) for j in range(N_CHIPS)], w_up, _full(w_up.shape),
                                 lambda w_ref, j: w_ref[j], sv["h1"], _row(p["mlp_norm_g"][l] + tok), dh)
    big = dict(w_down=g_down.reshape((N_CHIPS, -1) + g_down.shape[1:]), w_up=g_up)
    return dh, dh_b, big, dict(mlp_norm_g=dg2[0])


def _backward_mixer(l, dh, dh_b, sv, p, tok):
    d = dh.shape[1]
    att_w = d // 2
    rec_w = d - att_w
    nh = att_w // HEAD_DIM
    small = {}
    dattn, drec, dg_mix = _mixer_bwd(dh_b, sv["big"]["w_out"], sv["attn"], sv["rec"],
                                     _row(p["attn_out_g"][l] + tok), _row(p["rec_out_g"][l]))
    small["attn_out_g"] = dg_mix[0, :att_w]
    small["rec_out_g"] = dg_mix[0, att_w:]
    dxr, dyr, dwa, dwx, sm = _rec_bwd(
        sv["proj"], 0, 1, rec_w, sv["hr"], drec, p["conv_w"][l], _row(p["conv_b"][l]), sv["wa_d"],
        _row(p["b_gate_a"][l]), sv["wx_d"], _row(p["b_gate_x"][l]), _row(p["lru_L"][l]))
    small.update(conv_w=sm[:CONV_WIDTH], conv_b=sm[4], b_gate_a=sm[5], b_gate_x=sm[6], lru_L=sm[7],
                 w_gate_a=_diag_blocks(dwa, N_REC_BLOCKS), w_gate_x=_diag_blocks(dwx, N_REC_BLOCKS))
    dq, dk, dv, dc = _attn_bwd(sv["qkv"], sv["c"], sv["c_t"], sv["lse_b"], dattn, nh)
    df, db_f = _fgate_bwd(sv["proj"], sv["b_f_pad"], dc)
    small["b_f"] = db_f[0, :nh]
    pieces = [dq, dk, dv, dxr, dyr, df]
    offs = [0, att_w, 2 * att_w, 3 * att_w, 3 * att_w + rec_w, 3 * att_w + 2 * rec_w]
    gq, gk, gv, gxr, gyr, gf = _grad_w_pieces(pieces, sv["z1"])
    g_in_t = jnp.concatenate([gq, gk, gv, gf[:nh], gxr, gyr], axis=0)
    w_big = sv["big"]["w_in_big"]
    widths = [pc.shape[1] for pc in pieces]
    dh, dh_b, dg1 = _dx_norm_bwd(
        [(pc, 0, wd) for pc, wd in zip(pieces, widths)], w_big, _full(w_big.shape),
        lambda w_ref, i: w_ref[offs[i]:offs[i] + widths[i], :], sv["h0"], _row(p["attn_norm_g"][l]), dh, dot=_dot)
    small["attn_norm_g"] = dg1[0]
    big = dict(w_in=g_in_t.reshape(N_CHIPS, -1, d))
    return dh, dh_b, big, small


def _pack_w_in_t(w_in_t, att_w, nh):
    qkv = w_in_t[:3 * att_w]
    f = w_in_t[3 * att_w:3 * att_w + nh]
    xy = w_in_t[3 * att_w + nh:]
    return jnp.concatenate([qkv, xy, f, jnp.zeros((LANES - nh, w_in_t.shape[1]), w_in_t.dtype)], axis=0)


ANY = pl.BlockSpec(memory_space=pl.ANY)


def _coords():
    return lax.axis_index("x"), lax.axis_index("y"), lax.axis_index("c")


def _other_chips(x, y):
    return [(1 - x, y), (x, 1 - y), (1 - x, 1 - y)]


def _remote(src, dst, send_sems, recv_sems, k, to):
    return pltpu.make_async_remote_copy(src_ref=src, dst_ref=dst, send_sem=send_sems.at[k],
                                        recv_sem=recv_sems.at[k], device_id=to, device_id_type=MESH)


def _all_gather_chips(shards):
    n = len(shards)
    per = 6

    def body(*refs):
        ins, outs = refs[:n], refs[n:2 * n]
        send_sems, recv_sems, local_sems = refs[2 * n:]
        x, y, c = _coords()
        me = 2 * x + y
        sibling = (x, y, 1 - c)
        chips = _other_chips(x, y)
        local = [pltpu.make_async_copy(ins[t], outs[t].at[me], local_sems.at[t]) for t in range(n)]
        for cp in local:
            cp.start()
        sends = []
        for t in range(n):
            for j, (px, py) in enumerate(chips):
                cp = _remote(ins[t].at[c], outs[t].at[me, c], send_sems, recv_sems, per * t + j, (px, py, c))
                cp.start()
                sends.append(cp)
        for t in range(n):
            for j, (px, py) in enumerate(chips):
                landed = outs[t].at[2 * px + py, c]
                _remote(landed, landed, send_sems, recv_sems, per * t + j, (px, py, c)).wait_recv()
                cp = _remote(landed, landed, send_sems, recv_sems, per * t + 3 + j, sibling)
                cp.start()
                sends.append(cp)
        for t in range(n):
            for j, (px, py) in enumerate(chips):
                passed = outs[t].at[2 * px + py, 1 - c]
                _remote(passed, passed, send_sems, recv_sems, per * t + 3 + j, sibling).wait_recv()
        for cp in sends:
            cp.wait_send()
        for cp in local:
            cp.wait()

    return _call(body, tuple(shards), name="all_gather_chips",
                 in_specs=[ANY] * n, out_specs=[ANY] * n,
                 out_shape=[jax.ShapeDtypeStruct((N_CHIPS,) + s.shape, s.dtype) for s in shards],
                 scratch_shapes=[pltpu.SemaphoreType.DMA((per * n,)), pltpu.SemaphoreType.DMA((per * n,)),
                                 pltpu.SemaphoreType.DMA((n,))])


HBM = pl.BlockSpec(memory_space=pltpu.HBM)
SEM = pl.BlockSpec(memory_space=pltpu.SEMAPHORE)
DATAFLOW = pltpu.SideEffectType.DATAFLOW_SIDE_EFFECTING


def _in_hbm(a):
    return pltpu.with_memory_space_constraint(a, pltpu.HBM)


PUSH_ARRIVALS = {"gather_chips_half": N_CHIPS - 1, "pass_halves": N_CHIPS - 1, "scatter_chips": N_CHIPS - 1,
                 "sibling": 1, "gather_devices": N_DEV - 1}


def _column_half(ref3, slab, c):
    hw = ref3.shape[2] // 2
    return ref3.at[slab, :, pl.ds(pl.multiple_of(c * hw, LANES), hw)]


def _push_copies(mode, src, land, send_sems, recv_sems, t):
    x, y, c = _coords()
    chip = 2 * x + y
    if mode == "gather_chips_half":
        return [_remote(_column_half(src, chip, c), _column_half(land, chip, c), send_sems, recv_sems, t, (px, py, c))
                for px, py in _other_chips(x, y)]
    if mode == "pass_halves":
        return [_remote(_column_half(src, 2 * px + py, c), _column_half(land, 2 * px + py, c), send_sems, recv_sems, t,
                        (x, y, 1 - c)) for px, py in _other_chips(x, y)]
    if mode == "scatter_chips":
        return [_remote(src.at[2 * px + py], land.at[chip], send_sems, recv_sems, t, (px, py, c))
                for px, py in _other_chips(x, y)]
    if mode == "sibling":
        return [_remote(src, land, send_sems, recv_sems, t, (x, y, 1 - c))]
    dev = 4 * x + 2 * y + c
    return [_remote(src.at[dev], land.at[dev], send_sems, recv_sems, t, (x ^ (k >> 2), y ^ ((k >> 1) & 1), c ^ (k & 1)))
            for k in range(1, N_DEV)]


def _push_start(srcs, lands, mode, name):
    n = len(srcs)
    same = all(s is ld for s, ld in zip(srcs, lands))
    n_in = n if same else 2 * n

    def body(*refs):
        src_refs = refs[:n]
        land_refs = src_refs if same else refs[n:2 * n]
        send_sems, recv_sems = refs[n_in], refs[n_in + 1]
        token = refs[-1]
        for t in range(n):
            for cp in _push_copies(mode, src_refs[t], land_refs[t], send_sems, recv_sems, t):
                cp.start()
        token[...] = jnp.zeros_like(token)

    operands = tuple(srcs) if same else tuple(srcs) + tuple(lands)
    res = _call(
        body, [_in_hbm(a) for a in operands], name=name,
        out_shape=(pltpu.SemaphoreType.DMA((n,)), pltpu.SemaphoreType.DMA((n,)))
        + tuple(pltpu.HBM(a.shape, a.dtype) for a in operands) + (jax.ShapeDtypeStruct((SUBLANES, LANES), F32),),
        in_specs=[HBM] * n_in, out_specs=(SEM, SEM) + (HBM,) * n_in + (pl.BlockSpec(memory_space=pltpu.VMEM),),
        input_output_aliases={i: 2 + i for i in range(n_in)}, side_effects=DATAFLOW, hbm_results=False)
    send_sems, recv_sems, token = res[0], res[1], res[-1]
    srcs_thru = res[2:2 + n]
    lands_thru = srcs_thru if same else res[2 + n:2 + 2 * n]
    return send_sems, recv_sems, srcs_thru, lands_thru, token


def _push_wait(send_sems, recv_sems, ids, srcs, lands, mode, after, name):
    n = len(lands)
    same = all(s is ld for s, ld in zip(srcs, lands))
    n_in = n if same else 2 * n

    def body(*refs):
        land_refs = refs[:n] if same else refs[n:2 * n]
        send_sems, recv_sems = refs[n_in], refs[n_in + 1]
        x, y, c = _coords()
        for t in range(n):
            if mode == "sibling":
                moved = land_refs[t]
            elif mode in ("gather_chips_half", "pass_halves"):
                moved = land_refs[t].at[pl.ds(0, PUSH_ARRIVALS[mode]), :, pl.ds(0, land_refs[t].shape[2] // 2)]
            else:
                moved = land_refs[t].at[pl.ds(0, PUSH_ARRIVALS[mode])]
            arrivals = _remote(moved, moved, send_sems, recv_sems, ids[t], (x, y, c))
            arrivals.wait_send()
            arrivals.wait_recv()

    operands = tuple(lands) if same else tuple(srcs) + tuple(lands)
    res = _call(
        body, operands + (send_sems, recv_sems, after), name=name,
        out_shape=tuple(pltpu.HBM(a.shape, a.dtype) for a in operands),
        in_specs=[HBM] * n_in + [SEM, SEM, ANY], out_specs=(HBM,) * n_in,
        input_output_aliases={i: i for i in range(n_in)}, side_effects=DATAFLOW)
    return list(res) if same else (list(res[:n]), list(res[n:]))


def _sum_partials(part, landed, chip):
    _, rows, cols = part.shape
    br = _divisor_tile(rows, 16, ELEM_ROWS)

    def body(chip_ref, own_ref, a_ref, b_ref, c_ref, o_ref):
        o_ref[...] = ((own_ref[...].astype(F32) + a_ref[...].astype(F32)) + b_ref[...].astype(F32)) \
            + c_ref[...].astype(F32)

    def other(k):
        return pl.BlockSpec((None, br, cols), lambda i, ch: (jnp.where(ch[0] <= k, k + 1, k), i, 0))

    spec = pltpu.PrefetchScalarGridSpec(
        num_scalar_prefetch=1, grid=(rows // br,),
        in_specs=[pl.BlockSpec((None, br, cols), lambda i, ch: (ch[0], i, 0)), other(0), other(1), other(2)],
        out_specs=pl.BlockSpec((br, cols), lambda i, ch: (i, 0)))
    return _call(body, (chip, part, landed, landed, landed), name="sum_partials", grid_spec=spec,
                 out_shape=jax.ShapeDtypeStruct((rows, cols), F32), semantics=("parallel",))


def _cast_to_slab(w, l, chip):
    _, rows, cols = w.shape
    br = _divisor_tile(rows, 16, ELEM_ROWS)

    def body(chip_ref, w_ref, o_ref):
        o_ref[...] = w_ref[...].astype(BF16)

    spec = pltpu.PrefetchScalarGridSpec(
        num_scalar_prefetch=1, grid=(rows // br,),
        in_specs=[pl.BlockSpec((None, br, cols), lambda i, ch: (l, i, 0))],
        out_specs=pl.BlockSpec((None, br, cols), lambda i, ch: (ch[0], i, 0)))
    return _call(body, (chip, w), name="cast_to_slab", grid_spec=spec,
                 out_shape=jax.ShapeDtypeStruct((N_CHIPS, rows, cols), BF16), semantics=("parallel",))


def _cast_w_in_t_to_slabs(w_t, chip):
    rows, depth, d = w_t.shape
    tn = _divisor_tile(d, LANES, 256)

    def body(chip_ref, w_ref, *o_refs):
        for l in range(depth):
            o_refs[l][...] = w_ref[:, l, :].astype(BF16)

    spec = pltpu.PrefetchScalarGridSpec(
        num_scalar_prefetch=1, grid=(d // tn,),
        in_specs=[pl.BlockSpec((rows, depth, tn), lambda j, ch: (0, 0, j))],
        out_specs=[pl.BlockSpec((None, rows, tn), lambda j, ch: (ch[0], 0, j))] * depth)
    return _call(body, (chip, w_t), name="cast_w_in_t_to_slabs", grid_spec=spec,
                 out_shape=[jax.ShapeDtypeStruct((N_CHIPS, rows, d), BF16)] * depth, semantics=("parallel",),
                 vmem_bytes=4 * _nbytes((rows, max(depth, SUBLANES), tn), F32))


def _place_slab(buf, index, n_slabs):
    rows, cols = buf.shape
    br = _divisor_tile(rows, SUBLANES, ELEM_ROWS)

    def body(index_ref, b_ref, o_ref):
        o_ref[...] = b_ref[...]

    spec = pltpu.PrefetchScalarGridSpec(
        num_scalar_prefetch=1, grid=(rows // br,),
        in_specs=[pl.BlockSpec((br, cols), lambda i, ix: (i, 0))],
        out_specs=pl.BlockSpec((None, br, cols), lambda i, ix: (ix[0], i, 0)))
    return _call(body, (index, buf), name="place_slab", grid_spec=spec,
                 out_shape=jax.ShapeDtypeStruct((n_slabs, rows, cols), buf.dtype), semantics=("parallel",))


ELEM_ROWS = 256


def _sum_slabs(r):
    n, rows, cols = r.shape
    br = _divisor_tile(rows, 16, ELEM_ROWS)

    def body(r_ref, o_ref):
        acc = r_ref[0].astype(F32)
        for j in range(1, n):
            acc = acc + r_ref[j].astype(F32)
        o_ref[...] = acc

    return _call(body, (r,), name="sum_slabs", grid=(rows // br,),
                 in_specs=[pl.BlockSpec((n, br, cols), lambda i: (0, i, 0))],
                 out_specs=pl.BlockSpec((br, cols), lambda i: (i, 0)),
                 out_shape=jax.ShapeDtypeStruct((rows, cols), F32), semantics=("parallel",))


def _adamw_math(w, g, m, v):
    c1 = 1.0 - ADAM_B1 ** ADAM_STEP
    c2 = 1.0 - ADAM_B2 ** ADAM_STEP
    nm = ADAM_B1 * m + (1.0 - ADAM_B1) * g
    nv = ADAM_B2 * v + (1.0 - ADAM_B2) * (g * g)
    delta = -ADAM_LR * ((nm / c1) / (jnp.sqrt(nv / c2) + ADAM_EPS) + ADAM_WD * w)
    return delta, nm, nv


def _adamw(w, g, m, v):
    rows, cols = w.shape
    br = _divisor_tile(rows, 8, ELEM_ROWS)

    def body(w_ref, g_ref, m_ref, v_ref, d_ref, nm_ref, nv_ref):
        d_ref[...], nm_ref[...], nv_ref[...] = _adamw_math(w_ref[...], g_ref[...], m_ref[...], v_ref[...])

    blk = pl.BlockSpec((br, cols), lambda i: (i, 0))
    return _call(body, (w, g, m, v), name="adamw", grid=(rows // br,),
                 in_specs=[blk] * 4, out_specs=[blk] * 3,
                 out_shape=[jax.ShapeDtypeStruct((rows, cols), F32)] * 3, semantics=("parallel",))


def _adamw_w_in_t(w_t, m_t, v_t, g_mine, g_theirs):
    rows, depth, d = w_t.shape
    tn = LANES

    def body(w_ref, m_ref, v_ref, *rest):
        ga_refs, gb_refs = rest[:depth], rest[depth:2 * depth]
        g_ref, d_ref, nm_ref, nv_ref = rest[2 * depth:]
        for l in range(depth):
            g_ref[:, l, :] = ga_refs[l][...] + gb_refs[l][...]
        d_ref[...], nm_ref[...], nv_ref[...] = _adamw_math(w_ref[...], g_ref[...], m_ref[...], v_ref[...])

    slab = pl.BlockSpec((rows, depth, tn), lambda j: (0, 0, j))
    gblk = pl.BlockSpec((rows, tn), lambda j: (0, j))
    return _call(body, (w_t, m_t, v_t) + tuple(g_mine) + tuple(g_theirs), name="adamw_w_in_t", grid=(d // tn,),
                 in_specs=[slab] * 3 + [gblk] * (2 * depth), out_specs=[slab] * 4,
                 out_shape=[jax.ShapeDtypeStruct(w_t.shape, F32)] * 4, semantics=("parallel",),
                 vmem_bytes=2 * (7 * _nbytes((rows, max(depth, SUBLANES), tn), F32)
                                 + 2 * depth * _nbytes((rows, tn), F32)))


def _adamw_layer(w, m, v, l, g_mine, g_theirs, prev, after):
    _, rows, cols = w.shape
    br = _divisor_tile(rows, 8, ELEM_ROWS)

    def body(w_ref, m_ref, v_ref, ga_ref, gb_ref, *rest):
        g_ref, d_ref, nm_ref, nv_ref = rest[5:]
        g = ga_ref[...] + gb_ref[...]
        g_ref[...] = g
        d_ref[...], nm_ref[...], nv_ref[...] = _adamw_math(w_ref[...], g, m_ref[...], v_ref[...])

    slot = pl.BlockSpec((None, br, cols), lambda i: (l, i, 0))
    blk = pl.BlockSpec((br, cols), lambda i: (i, 0))
    return _call(body, (w, m, v, g_mine, g_theirs) + tuple(prev) + (after,), name="adamw_layer",
                 grid=(rows // br,), in_specs=[slot] * 3 + [blk] * 2 + [ANY] * 5, out_specs=[slot] * 4,
                 out_shape=[jax.ShapeDtypeStruct(w.shape, F32)] * 4,
                 input_output_aliases={5: 0, 6: 1, 7: 2, 8: 3}, semantics=("parallel",))


BIG = ("w_in", "w_out", "w_up", "w_down")
WEIGHTS = ("meta", "attn_norm_g", "w_in", "b_f", "conv_w", "conv_b", "w_gate_a", "b_gate_a", "w_gate_x",
           "b_gate_x", "lru_L", "attn_out_g", "rec_out_g", "w_out", "mlp_norm_g", "w_up", "w_down", "final_g")
SMALL = tuple(k for k in WEIGHTS if k not in BIG)
COL_SHARDED_SMALL = ("meta", "conv_w")


def _packed_rows(shape):
    return -(-math.prod(shape) // (SUBLANES * LANES)) * SUBLANES


def _pack(arrs):
    rows = []
    for a in arrs:
        flat = a.reshape(-1)
        rows.append(jnp.pad(flat, (0, _packed_rows(a.shape) * LANES - flat.shape[0])).reshape(-1, LANES))
    used = sum(r.shape[0] for r in rows)
    rows.append(jnp.zeros((-used % ELEM_ROWS, LANES), F32))
    return jnp.concatenate(rows, axis=0)


def _unpack(buf, shapes):
    out, r0 = [], 0
    for s in shapes:
        nr = _packed_rows(s)
        out.append(buf[r0:r0 + nr].reshape(-1)[:math.prod(s)].reshape(s))
        r0 += nr
    return out


def _halves(a):
    return a.reshape((2, a.shape[0] // 2) + a.shape[1:])


def _cols_from_chips(g):
    return jnp.moveaxis(g, 0, -2).reshape(g.shape[1:-1] + (N_CHIPS * g.shape[-1],))


def kernel(x, meta, attn_norm_g, w_in, b_f, conv_w, conv_b, w_gate_a, b_gate_a, w_gate_x, b_gate_x, lru_L, attn_out_g, rec_out_g, w_out, mlp_norm_g, w_up, w_down, final_g, loss_target, m_meta, m_attn_norm_g, m_w_in, m_b_f, m_conv_w, m_conv_b, m_w_gate_a, m_b_gate_a, m_w_gate_x, m_b_gate_x, m_lru_L, m_attn_out_g, m_rec_out_g, m_w_out, m_mlp_norm_g, m_w_up, m_w_down, m_final_g, v_meta, v_attn_norm_g, v_w_in, v_b_f, v_conv_w, v_conv_b, v_w_gate_a, v_b_gate_a, v_w_gate_x, v_b_gate_x, v_lru_L, v_attn_out_g, v_rec_out_g, v_w_out, v_mlp_norm_g, v_w_up, v_w_down, v_final_g):
    w = dict(meta=meta, attn_norm_g=attn_norm_g, w_in=w_in, b_f=b_f, conv_w=conv_w, conv_b=conv_b,
             w_gate_a=w_gate_a, b_gate_a=b_gate_a, w_gate_x=w_gate_x, b_gate_x=b_gate_x, lru_L=lru_L,
             attn_out_g=attn_out_g, rec_out_g=rec_out_g, w_out=w_out, mlp_norm_g=mlp_norm_g, w_up=w_up,
             w_down=w_down, final_g=final_g)
    m = dict(meta=m_meta, attn_norm_g=m_attn_norm_g, w_in=m_w_in, b_f=m_b_f, conv_w=m_conv_w, conv_b=m_conv_b,
             w_gate_a=m_w_gate_a, b_gate_a=m_b_gate_a, w_gate_x=m_w_gate_x, b_gate_x=m_b_gate_x, lru_L=m_lru_L,
             attn_out_g=m_attn_out_g, rec_out_g=m_rec_out_g, w_out=m_w_out, mlp_norm_g=m_mlp_norm_g,
             w_up=m_w_up, w_down=m_w_down, final_g=m_final_g)
    v = dict(meta=v_meta, attn_norm_g=v_attn_norm_g, w_in=v_w_in, b_f=v_b_f, conv_w=v_conv_w, conv_b=v_conv_b,
             w_gate_a=v_w_gate_a, b_gate_a=v_b_gate_a, w_gate_x=v_w_gate_x, b_gate_x=v_b_gate_x, lru_L=v_lru_L,
             attn_out_g=v_attn_out_g, rec_out_g=v_rec_out_g, w_out=v_w_out, mlp_norm_g=v_mlp_norm_g,
             w_up=v_w_up, w_down=v_w_down, final_g=v_final_g)
    s_len, d = x.shape[1], x.shape[2]
    depth = w_in.shape[0]
    att_w = d // 2
    rec_w = d - att_w
    nh = att_w // HEAD_DIM
    chip = 2 * lax.axis_index("x") + lax.axis_index("y")

    g_conv, g_meta = [g.reshape((N_CHIPS, g.shape[1] * g.shape[2]) + g.shape[3:])
                      for g in _all_gather_chips([_halves(w["conv_w"]), _halves(w["meta"])])]
    p = dict(w)
    p["conv_w"] = _cols_from_chips(g_conv)
    meta_full = jnp.moveaxis(g_meta, 0, 1).reshape(N_META, d)

    chip1 = chip.reshape(1).astype(jnp.int32)
    w_in_t, m_in_t, v_in_t = [jnp.transpose(a["w_in"], (2, 0, 1)) for a in (w, m, v)]
    w_in_slabs = _cast_w_in_t_to_slabs(w_in_t, chip1)
    pushes, tokens = [], []
    for l in range(depth):
        slabs = [w_in_slabs[l]] + [_cast_to_slab(w[k], l, chip1) for k in BIG[1:]]
        send_sems, recv_sems, _, lands, token = _push_start(slabs, slabs, "gather_chips_half", f"weights_start_{l}")
        pushes.append((send_sems, recv_sems, lands))
        tokens.append(token[0, 0])
    passed = [{} for _ in range(depth)]

    def stage(l, after, ids):
        send_sems, recv_sems, lands = pushes[l]
        tag = "_".join(BIG[i] for i in ids)
        sub = [lands[i] for i in ids]
        sub = _push_wait(send_sems, recv_sems, ids, sub, sub, "gather_chips_half", after, f"{tag}_wait_{l}")
        send_sems, recv_sems, _, sub, token = _push_start(sub, sub, "pass_halves", f"{tag}_pass_{l}")
        for j, i in enumerate(ids):
            passed[l][i] = (send_sems, recv_sems, sub[j], j)
        return token[0, 0]

    t_len = N_META + s_len
    pad = -t_len % SEQ_TILE
    h = jnp.concatenate([meta_full, x[0], jnp.zeros((pad, d), F32)], axis=0)
    tgt = jnp.concatenate([jnp.zeros((N_META, d), F32), loss_target[0], jnp.zeros((pad, d), F32)], axis=0)
    z = _rms_fwd(h, _row(p["attn_norm_g"][0] + sum(tokens)))
    stage(0, z, [0])
    saved = []
    for l in range(depth):
        def fetch(k, after, l=l):
            send_sems, recv_sems, land, j = passed[l][BIG.index(k)]
            return _push_wait(send_sems, recv_sems, [j], [land], [land], "pass_halves", after,
                              f"{k}_here_{l}")[0]

        def stage_next(after, which, l=l):
            if which == "own":
                return stage(l, after, [1, 2, 3])
            return stage(l + 1, after, [0]) if l + 1 < depth else 0.0

        g_next = p["attn_norm_g"][l + 1] if l + 1 < depth else p["final_g"]
        h, z, sv = _forward_layer(l, h, z, p, fetch, stage_next, g_next)
        saved.append(sv)
    dh, dh_b, dg_final, loss_part = _loss_bwd(h, _row(p["final_g"]), tgt, s_len)

    small = {k: [None] * depth for k in SMALL if k not in ("meta", "final_g")}
    pushes = [None] * depth
    tok = 0.0
    for l in reversed(range(depth)):
        dh, dh_b, big_mlp, sm_mlp = _backward_mlp(l, dh, dh_b, saved[l], p, tok)
        g_out, = _grad_w_pieces([saved[l]["mix"]], dh_b)
        parts = [big_mlp["w_down"], big_mlp["w_up"], g_out.reshape((N_CHIPS, -1) + g_out.shape[1:])]
        push_mlp = _push_start(parts, [lax.empty(a.shape, a.dtype) for a in parts], "scatter_chips",
                               f"mlp_grads_start_{l}")
        dh, dh_b, big_mix, sm_mix = _backward_mixer(l, dh, dh_b, saved[l], p, push_mlp[4][0, 0])
        parts = [big_mix["w_in"]]
        push_mix = _push_start(parts, [lax.empty(a.shape, a.dtype) for a in parts], "scatter_chips",
                               f"mixer_grads_start_{l}")
        tok = push_mix[4][0, 0]
        pushes[l] = {("w_down", "w_up", "w_out"): push_mlp, ("w_in",): push_mix}
        for k, val in {**sm_mlp, **sm_mix}.items():
            small[k][l] = val
    grads = {k: jnp.stack(val) for k, val in small.items()}
    grads["final_g"] = dg_final[0]
    grads["meta"] = dh[:N_META]
    dx = dh[N_META:t_len]

    full_shapes = [grads[k].shape for k in SMALL] + [(1,)]
    packed = _pack([grads[k].astype(F32) for k in SMALL] + [loss_part[0, :1] + tok])
    dev1 = (2 * chip + lax.axis_index("c")).reshape(1).astype(jnp.int32)
    slabs = [_place_slab(packed, dev1, N_DEV)]
    small_push = _push_start(slabs, slabs, "gather_devices", "small_grads_start")

    last_token = small_push[4]
    outs = {k: [lax.empty(w[k].shape, F32) for _ in range(4)] for k in BIG[1:]}
    w_in_sums = [None] * depth
    swaps = {}

    def finish(l, wait_after, adam_after):
        send_sems, recv_sems, mine, lands, _ = swaps[l]
        mine, theirs = _push_wait(send_sems, recv_sems, list(range(len(BIG))), mine, lands, "sibling", wait_after,
                                  f"sums_wait_{l}")
        w_in_sums[l] = (mine[0], theirs[0])
        for k, a, b in zip(BIG[1:], mine[1:], theirs[1:]):
            outs[k] = _adamw_layer(w[k], m[k], v[k], l, a, b, outs[k], adam_after)

    wait_after = last_token
    for l in reversed(range(depth)):
        sums = {}
        for names, (send_sems, recv_sems, parts, lands, _) in pushes[l].items():
            parts, landed = _push_wait(send_sems, recv_sems, list(range(len(names))), parts, lands, "scatter_chips",
                                       wait_after, f"{names[0]}_grads_wait_{l}")
            for k, part, land in zip(names, parts, landed):
                sums[k] = wait_after = _sum_partials(part, land, chip1)
        mine = [sums[k] for k in BIG]
        swaps[l] = _push_start(mine, [lax.empty(a.shape, a.dtype) for a in mine], "sibling", f"sums_start_{l}")
        if l + 1 < depth:
            finish(l + 1, outs["w_down"][0] if l + 2 < depth else mine[0], swaps[l][4])
    finish(0, outs["w_down"][0] if depth > 1 else swaps[0][4], swaps[0][4])
    outs["w_in"] = [jnp.transpose(r, (1, 2, 0)) for r in _adamw_w_in_t(
        w_in_t, m_in_t, v_in_t, [s[0] for s in w_in_sums], [s[1] for s in w_in_sums])]
    out_g, out_d, out_m, out_v = [{k: outs[k][i] for k in BIG} for i in range(4)]

    landed = _push_wait(small_push[0], small_push[1], [0], small_push[3], small_push[3], "gather_devices",
                        out_g["w_in"], "small_grads_wait")
    total = _sum_slabs(landed[0])
    small_g = dict(zip(SMALL + ("loss",), _unpack(total, full_shapes)))
    for k in COL_SHARDED_SMALL:
        n = w[k].shape[-1]
        small_g[k] = lax.dynamic_slice_in_dim(small_g[k], chip * n, n, axis=small_g[k].ndim - 1)
    local_shapes = [w[k].shape for k in SMALL]
    res = _adamw(_pack([w[k] for k in SMALL]), _pack([small_g[k] for k in SMALL]),
                 _pack([m[k] for k in SMALL]), _pack([v[k] for k in SMALL]))
    out_g.update({k: small_g[k] for k in SMALL})
    for dst, buf in zip((out_d, out_m, out_v), res):
        dst.update(zip(SMALL, _unpack(buf, local_shapes)))

    return (small_g["loss"].reshape(()), dx[None],
            *[out_g[k] for k in WEIGHTS], *[out_d[k] for k in WEIGHTS],
            *[out_m[k] for k in WEIGHTS], *[out_v[k] for k in WEIGHTS])
```

```python
import functools
import math

import jax
import jax.numpy as jnp
from jax import lax
from jax.experimental import pallas as pl
from jax.experimental.pallas import tpu as pltpu

F32 = jnp.float32
BF16 = jnp.bfloat16

N_META = 16
HEAD_DIM = 64
N_REC_BLOCKS = 8
CONV_WIDTH = 4
RG_C = 8.0
NORM_EPS = 1e-6
ADAM_LR = 0.001
ADAM_B1 = 0.9
ADAM_B2 = 0.999
ADAM_EPS = 1e-08
ADAM_WD = 0.01
ADAM_STEP = 10

LANES = 128
SUBLANES = 8
SEQ_TILE = 128
VMEM_CAP = 60 * 2**20
VMEM_SLACK = 6 * 2**20
NEG_BIG = -1e30
N_CHIPS = 4
N_DEV = 8
MESH = pl.DeviceIdType.MESH


def _nbytes(shape, dtype):
    return math.prod(shape) * jnp.dtype(dtype).itemsize


def _call(body, args, *, name, out_shape, grid=(), in_specs=None, out_specs=None, scratch_shapes=(),
          grid_spec=None, semantics=None, vmem_bytes=None, side_effects=None, hbm_results=True, **kw):
    cp = {}
    if semantics is not None:
        cp["dimension_semantics"] = semantics
    if vmem_bytes is not None:
        cp["vmem_limit_bytes"] = int(min(VMEM_CAP, vmem_bytes + VMEM_SLACK))
    if side_effects is not None:
        cp["has_side_effects"] = side_effects
    if grid_spec is not None:
        kw["grid_spec"] = grid_spec
    else:
        kw.update(grid=grid, in_specs=in_specs, out_specs=out_specs, scratch_shapes=scratch_shapes)
    if hbm_results:
        out_shape = jax.tree.map(
            lambda s: pltpu.HBM(s.shape, s.dtype) if isinstance(s, jax.ShapeDtypeStruct) else s, out_shape)
    fn = pl.pallas_call(
        body, name=name, out_shape=out_shape,
        compiler_params=pltpu.CompilerParams(**cp), **kw)
    return fn(*[_in_hbm(a) if jnp.issubdtype(getattr(a, "dtype", jnp.int32), jnp.floating) else a for a in args])


def _divisor_tile(n, unit, target):
    best = None
    for t in range(unit, min(n, target) + 1, unit):
        if n % t == 0:
            best = t
    return n if best is None else best


def _sigmoid(x):
    return 1.0 / (1.0 + jnp.exp(-x))


def _log1p_unit(e):
    series = e * (1.0 - e * (0.5 - e * (1.0 / 3.0)))
    return jnp.where(e < 1e-2, series, jnp.log(1.0 + e))


def _log_sigmoid(x):
    return jnp.minimum(x, 0.0) - _log1p_unit(jnp.exp(-jnp.abs(x)))


def _one_minus_exp(x, exp_x):
    small = -x * (1.0 + x * (1.0 / 2 + x * (1.0 / 6 + x * (1.0 / 24 + x * (1.0 / 120 + x * (1.0 / 720))))))
    return jnp.where(x > -0.25, small, 1.0 - exp_x)


_GELU_K = math.sqrt(2.0 / math.pi)
_GELU_C = 0.044715


def _gelu_and_grad(y):
    th = jnp.tanh(_GELU_K * (y + _GELU_C * y * y * y))
    g = 0.5 * y * (1.0 + th)
    dg = 0.5 * (1.0 + th) + 0.5 * y * (1.0 - th * th) * _GELU_K * (1.0 + 3.0 * _GELU_C * y * y)
    return g, dg


def _rstd(x):
    return lax.rsqrt(jnp.mean(x * x, axis=-1, keepdims=True) + NORM_EPS)


def _rms_bwd(dz, x, g):
    rs = _rstd(x)
    xh = x * rs
    dgp = jnp.sum(dz * xh, axis=0, keepdims=True)
    dxh = dz * g
    dx = rs * (dxh - xh * jnp.mean(dxh * xh, axis=-1, keepdims=True))
    return dx, dgp


def _dot(a, b):
    return jnp.dot(a, b, preferred_element_type=F32)


def _dot_nt(a, b):
    return lax.dot_general(a, b, (((1,), (1,)), ((), ())), preferred_element_type=F32)


def _dot_tn(a, b):
    return lax.dot_general(a, b, (((0,), (0,)), ((), ())), preferred_element_type=F32)


def _full(shape):
    nd = len(shape)
    return pl.BlockSpec(shape, lambda *_: (0,) * nd)


def _rms_fwd(h, g):
    tp, d = h.shape
    tm = _divisor_tile(tp, 16, 544)

    def body(h_ref, g_ref, z_ref):
        x = h_ref[...]
        z_ref[...] = (x * _rstd(x) * g_ref[...]).astype(BF16)

    return _call(body, (h, g), name="rms_fwd", grid=(tp // tm,),
                 in_specs=[pl.BlockSpec((tm, d), lambda i: (i, 0)), _full((1, d))],
                 out_specs=pl.BlockSpec((tm, d), lambda i: (i, 0)),
                 out_shape=jax.ShapeDtypeStruct((tp, d), BF16), semantics=("parallel",))


def _proj(z, w_big_t, att_w):
    tp, d = z.shape
    nb = w_big_t.shape[0]
    tn = _divisor_tile(nb, LANES, 512)
    assert (3 * att_w) % tn == 0
    n_qkv = 3 * att_w // tn
    scale = 1.0 / math.sqrt(HEAD_DIM)

    def body(z_ref, w_ref, qkv_ref, p_ref):
        j = pl.program_id(0)
        acc = _dot_nt(z_ref[...], w_ref[...])

        @pl.when(j < n_qkv)
        def _():
            col = j * tn + lax.broadcasted_iota(jnp.int32, (1, tn), 1)
            qkv_ref[...] = (acc * jnp.where(col < att_w, scale, 1.0)).astype(BF16)

        @pl.when(j >= n_qkv)
        def _():
            p_ref[...] = acc

    vm = 2 * (_nbytes((tp, d), BF16) + _nbytes((d, tn), BF16) + _nbytes((tp, tn), F32) * 2)
    return _call(body, (z, w_big_t), name="proj", grid=(nb // tn,),
                 in_specs=[_full((tp, d)), pl.BlockSpec((tn, d), lambda j: (j, 0))],
                 out_specs=[pl.BlockSpec((tp, tn), lambda j: (0, jnp.minimum(j, n_qkv - 1))),
                            pl.BlockSpec((tp, tn), lambda j: (0, jnp.maximum(j - n_qkv, 0)))],
                 out_shape=[jax.ShapeDtypeStruct((tp, 3 * att_w), BF16),
                            jax.ShapeDtypeStruct((tp, nb - 3 * att_w), F32)],
                 semantics=("arbitrary",), vmem_bytes=vm)


def _tile_cumsum(x, row, reverse=False):
    for s in (1, 2, 4):
        if reverse:
            x = x + jnp.where(row < SUBLANES - s, pltpu.roll(x, SUBLANES - s, 0), 0.0)
        else:
            x = x + jnp.where(row >= s, pltpu.roll(x, s, 0), 0.0)
    return x


def _fgate_fwd(proj, b_f_pad, nh):
    tp, nb = proj.shape
    fblk = nb // LANES - 1

    def body(f_ref, b_ref, c_ref, ct_ref):
        b = b_ref[...]
        row = lax.broadcasted_iota(jnp.int32, (SUBLANES, LANES), 0)

        def step(i, carry):
            r0 = pl.multiple_of(i * SUBLANES, SUBLANES)
            lf = _log_sigmoid(f_ref[pl.ds(r0, SUBLANES), :] + b)
            x = _tile_cumsum(lf, row) + carry
            c_ref[pl.ds(r0, SUBLANES), :] = x
            return x[SUBLANES - 1:SUBLANES, :]

        lax.fori_loop(0, tp // SUBLANES, step, jnp.zeros((1, LANES), F32))
        ct_ref[...] = c_ref[...].T[:nh, :]

    return _call(body, (proj, b_f_pad), name="fgate_fwd", grid=(1,),
                 in_specs=[pl.BlockSpec((tp, LANES), lambda i: (0, fblk)), _full((1, LANES))],
                 out_specs=[_full((tp, LANES)), _full((nh, tp))],
                 out_shape=[jax.ShapeDtypeStruct((tp, LANES), F32), jax.ShapeDtypeStruct((nh, tp), F32)],
                 semantics=("arbitrary",))


def _fgate_bwd(proj, b_f_pad, dc):
    tp, nb = proj.shape
    fblk = nb // LANES - 1

    def body(f_ref, b_ref, dc_ref, df_ref, db_ref, dc_s):
        b = b_ref[...]
        row = lax.broadcasted_iota(jnp.int32, (SUBLANES, LANES), 0)
        nt = tp // SUBLANES

        def step(i, carry):
            suffix, acc = carry
            r0 = pl.multiple_of((nt - 1 - i) * SUBLANES, SUBLANES)
            dlf = _tile_cumsum(dc_ref[pl.ds(r0, SUBLANES), :], row, reverse=True) + suffix
            df = dlf * _sigmoid(-(f_ref[pl.ds(r0, SUBLANES), :] + b))
            dc_s[pl.ds(r0, SUBLANES), :] = df
            return dlf[0:1, :], acc + df

        _, acc = lax.fori_loop(0, nt, step, (jnp.zeros((1, LANES), F32), jnp.zeros((SUBLANES, LANES), F32)))
        df_ref[...] = dc_s[...].astype(BF16)
        db_ref[...] = jnp.broadcast_to(jnp.sum(acc, axis=0, keepdims=True), (SUBLANES, LANES))

    return _call(body, (proj, b_f_pad, dc), name="fgate_bwd", grid=(1,),
                 in_specs=[pl.BlockSpec((tp, LANES), lambda i: (0, fblk)), _full((1, LANES)), _full((tp, LANES))],
                 out_specs=[_full((tp, LANES)), _full((SUBLANES, LANES))],
                 out_shape=[jax.ShapeDtypeStruct((tp, LANES), BF16),
                            jax.ShapeDtypeStruct((SUBLANES, LANES), F32)],
                 scratch_shapes=[pltpu.VMEM((tp, LANES), F32)], semantics=("arbitrary",))


ATT_BQ = 128


ATT_BUCKET = 3
ATT_HEADS = 4


def _for_bucket(i, nq, fn):
    for lo in range(0, nq, ATT_BUCKET):
        hi = min(lo + ATT_BUCKET, nq)
        spans = ([(0, lo * ATT_BQ, False)] if lo else []) + [(lo * ATT_BQ, hi * ATT_BQ, True)]
        pl.when(jnp.logical_and(i >= lo, i < hi))(functools.partial(fn, spans))


def _head_column(c_blk, h):
    lane = lax.broadcasted_iota(jnp.int32, c_blk.shape, 1)
    return jnp.sum(jnp.where(lane == h, c_blk, 0.0), axis=1, keepdims=True)


def _head_columns_into(p, c_ref, ck_s):
    for hh in range(ATT_HEADS):
        ck_s[hh] = jnp.broadcast_to(_head_column(c_ref[...], ATT_HEADS * p + hh), ck_s.shape[1:])


def _pair_diag_cols(x2):
    top = lax.broadcasted_iota(jnp.int32, (LANES, ATT_BQ), 0) < HEAD_DIM
    xt = x2.astype(F32).T.astype(BF16)
    return jnp.concatenate([jnp.where(top, xt, 0), jnp.where(top, 0, xt)], axis=1)


def _pair_diag_rows(x2):
    low = lax.broadcasted_iota(jnp.int32, (ATT_BQ, LANES), 1) < HEAD_DIM
    return jnp.concatenate([jnp.where(low, x2, 0), jnp.where(low, 0, x2)], axis=0)


def _seen_keys(k0, k1, q0):
    keys = k0 + lax.broadcasted_iota(jnp.int32, (k1 - k0, ATT_BQ), 0)
    return keys <= q0 + lax.broadcasted_iota(jnp.int32, (k1 - k0, ATT_BQ), 1)


def _attn_fwd(qkv, c, c_t, nh):
    tp = qkv.shape[0]
    att_w = nh * HEAD_DIM
    ng = nh // ATT_HEADS
    gw = ATT_HEADS * HEAD_DIM
    bq = ATT_BQ
    nq = tp // bq
    pair = 2 * HEAD_DIM
    assert pair == LANES and ATT_HEADS % 2 == 0

    def body(q_ref, k_ref, v_ref, c_ref, ct_ref, o_ref, lse_ref, ck_s, vt_s):
        p = pl.program_id(0)
        i = pl.program_id(1)

        @pl.when(i == 0)
        def _():
            _head_columns_into(p, c_ref, ck_s)
            vt_s[...] = v_ref[...].astype(F32).T.astype(BF16)

        def compute(spans):
            q0 = pl.multiple_of(i * bq, bq)
            o_t, lses = [], []
            for pi in range(ATT_HEADS // 2):
                lo = pair * pi
                heads = (2 * pi, 2 * pi + 1)
                q_cols = _pair_diag_cols(q_ref[:, lo:lo + pair])
                ts = []
                for k0, k1, needs_mask in spans:
                    t2 = _dot(k_ref[k0:k1, lo:lo + pair], q_cols)
                    t_e = [t2[:, e * bq:(e + 1) * bq] - ck_s[hh, k0:k1, :] for e, hh in enumerate(heads)]
                    if needs_mask:
                        seen = _seen_keys(k0, k1, q0)
                        t_e = [jnp.where(seen, t, NEG_BIG) for t in t_e]
                    ts.append(t_e)
                ms = [functools.reduce(jnp.maximum, [jnp.max(t[e], axis=0, keepdims=True) for t in ts])
                      for e in range(2)]
                es = [[jnp.exp(t[e] - ms[e]) for e in range(2)] for t in ts]
                ls = [sum(jnp.sum(e_[e], axis=0, keepdims=True) for e_ in es) for e in range(2)]
                o2 = sum(_dot(vt_s[lo:lo + pair, k0:k1],
                              jnp.concatenate([e_[0].astype(BF16), e_[1].astype(BF16)], axis=1))
                         for e_, (k0, k1, _) in zip(es, spans))
                o_t += [o2[:HEAD_DIM, :bq] / ls[0], o2[HEAD_DIM:, bq:] / ls[1]]
                lses += [ms[e] + ct_ref[pl.ds(ATT_HEADS * p + hh, 1), :] + jnp.log(ls[e])
                         for e, hh in enumerate(heads)]
            o_ref[...] = jnp.concatenate(o_t, axis=0).T
            lse_ref[...] = jnp.concatenate(lses, axis=0)

        _for_bucket(i, nq, compute)

    blk = pl.BlockSpec((bq, gw), lambda p, i: (i, p))
    vm = 6 * _nbytes((tp, gw), BF16) + 2 * ATT_HEADS * _nbytes((tp, LANES), F32) + 2 * _nbytes((tp, LANES), F32) \
        + 8 * ATT_HEADS * _nbytes((bq, tp), F32)
    return _call(body, (qkv, qkv, qkv, c, c_t), name="attn_fwd", grid=(ng, nq),
                 in_specs=[blk,
                           pl.BlockSpec((tp, gw), lambda p, i: (0, ng + p)),
                           pl.BlockSpec((tp, gw), lambda p, i: (0, 2 * ng + p)),
                           _full((tp, LANES)), pl.BlockSpec((nh, bq), lambda p, i: (0, i))],
                 out_specs=[blk, pl.BlockSpec((None, ATT_HEADS, bq), lambda p, i: (p, 0, i))],
                 out_shape=[jax.ShapeDtypeStruct((tp, att_w), F32), jax.ShapeDtypeStruct((ng, ATT_HEADS, tp), F32)],
                 scratch_shapes=[pltpu.VMEM((ATT_HEADS, tp, LANES), F32), pltpu.VMEM((gw, tp), BF16)],
                 semantics=("arbitrary", "arbitrary"), vmem_bytes=vm)


def _attn_bwd(qkv, c, c_t, lse, do, nh):
    tp = qkv.shape[0]
    att_w = nh * HEAD_DIM
    ng = nh // ATT_HEADS
    gw = ATT_HEADS * HEAD_DIM
    bq = ATT_BQ
    nq = tp // bq
    pair = 2 * HEAD_DIM
    assert pair == LANES and ATT_HEADS % 2 == 0
    scale = 1.0 / math.sqrt(HEAD_DIM)

    def body(q_ref, k_ref, v_ref, c_ref, ct_ref, lse_ref, do_ref, dq_ref, dk_ref, dv_ref, dc_ref,
             dk_s, dv_s, dc_s, ck_s, kt_s):
        p = pl.program_id(0)
        i = pl.program_id(1)

        @pl.when(i == 0)
        def _():
            dk_s[...] = jnp.zeros_like(dk_s)
            dv_s[...] = jnp.zeros_like(dv_s)
            dc_s[...] = jnp.zeros_like(dc_s)
            kt_s[...] = k_ref[...].astype(F32).T.astype(BF16)
            _head_columns_into(p, c_ref, ck_s)

        @pl.when(jnp.logical_and(i == 0, p == 0))
        def _():
            dc_ref[...] = jnp.zeros_like(dc_ref)

        def compute(spans):
            q0 = pl.multiple_of(i * bq, bq)
            dq_t = []
            for pi in range(ATT_HEADS // 2):
                lo = pair * pi
                q2 = q_ref[:, lo:lo + pair]
                do2 = do_ref[:, lo:lo + pair].astype(BF16)
                q_cols, do_cols = _pair_diag_cols(q2), _pair_diag_cols(do2)
                q_rows, do_rows = _pair_diag_rows(q2), _pair_diag_rows(do2)
                heads = (2 * pi, 2 * pi + 1)
                col_terms = [ct_ref[pl.ds(ATT_HEADS * p + hh, 1), :] - lse_ref[hh:hh + 1, :] for hh in heads]
                prs, dps = [], []
                for k0, k1, needs_mask in spans:
                    t2 = _dot(k_ref[k0:k1, lo:lo + pair], q_cols)
                    dp2 = _dot(v_ref[k0:k1, lo:lo + pair], do_cols)
                    if needs_mask:
                        seen = _seen_keys(k0, k1, q0)
                    pr_e, dp_e = [], []
                    for e, hh in enumerate(heads):
                        t = t2[:, e * bq:(e + 1) * bq] - ck_s[hh, k0:k1, :]
                        if needs_mask:
                            t = jnp.where(seen, t, NEG_BIG)
                        pr_e.append(jnp.exp(t + col_terms[e]))
                        dp_e.append(dp2[:, e * bq:(e + 1) * bq])
                    prs.append(pr_e)
                    dps.append(dp_e)
                key_sums = [sum(jnp.sum(pr[e] * dp[e], axis=0, keepdims=True) for pr, dp in zip(prs, dps))
                            for e in range(2)]
                dq2 = 0.0
                for (k0, k1, _), pr, dp in zip(spans, prs, dps):
                    ds = [pr[e] * (dp[e] - key_sums[e]) for e in range(2)]
                    for e, hh in enumerate(heads):
                        dc_s[hh, k0:k1, :] += jnp.sum(ds[e], axis=1, keepdims=True)
                    ds2 = jnp.concatenate([ds[0].astype(BF16), ds[1].astype(BF16)], axis=1)
                    pr2 = jnp.concatenate([pr[0].astype(BF16), pr[1].astype(BF16)], axis=1)
                    dk_s[k0:k1, lo:lo + pair] += _dot(ds2, q_rows)
                    dv_s[k0:k1, lo:lo + pair] += _dot(pr2, do_rows)
                    dq2 = dq2 + _dot(kt_s[lo:lo + pair, k0:k1], ds2)
                dq_t.append(jnp.concatenate([dq2[:HEAD_DIM, :bq], dq2[HEAD_DIM:, bq:]], axis=0))
            dq_ref[...] = (jnp.concatenate(dq_t, axis=0) * scale).T.astype(BF16)

        _for_bucket(i, nq, compute)

        @pl.when(i == nq - 1)
        def _():
            dk_ref[...] = dk_s[...].astype(BF16)
            dv_ref[...] = dv_s[...].astype(BF16)
            lane = lax.broadcasted_iota(jnp.int32, (tp, LANES), 1)
            dc = dc_ref[...]
            for hh in range(ATT_HEADS):
                dc = jnp.where(lane == ATT_HEADS * p + hh, -dc_s[hh], dc)
            dc_ref[...] = dc

    blk = pl.BlockSpec((bq, gw), lambda p, i: (i, p))
    col = pl.BlockSpec((tp, gw), lambda p, i: (0, p))
    vm = 7 * _nbytes((tp, gw), BF16) + 2 * _nbytes((tp, gw), F32) + 2 * ATT_HEADS * _nbytes((tp, LANES), F32) \
        + 2 * _nbytes((tp, LANES), F32) + 12 * ATT_HEADS * _nbytes((bq, tp), F32)
    return _call(body, (qkv, qkv, qkv, c, c_t, lse, do), name="attn_bwd", grid=(ng, nq),
                 in_specs=[blk,
                           pl.BlockSpec((tp, gw), lambda p, i: (0, ng + p)),
                           pl.BlockSpec((tp, gw), lambda p, i: (0, 2 * ng + p)),
                           _full((tp, LANES)), pl.BlockSpec((nh, bq), lambda p, i: (0, i)),
                           pl.BlockSpec((None, ATT_HEADS, bq), lambda p, i: (p, 0, i)), blk],
                 out_specs=[blk, col, col, _full((tp, LANES))],
                 out_shape=[jax.ShapeDtypeStruct((tp, att_w), BF16)] * 3 + [jax.ShapeDtypeStruct((tp, LANES), F32)],
                 scratch_shapes=[pltpu.VMEM((tp, gw), F32), pltpu.VMEM((tp, gw), F32),
                                 pltpu.VMEM((ATT_HEADS, tp, 1), F32), pltpu.VMEM((ATT_HEADS, tp, LANES), F32),
                                 pltpu.VMEM((gw, tp), BF16)],
                 semantics=("arbitrary", "arbitrary"), vmem_bytes=vm)


REC_ROWS = 128
HALO = SUBLANES


def _conv_taps(cat):
    taps = []
    for k in range(CONV_WIDTH):
        sh = CONV_WIDTH - 1 - k
        taps.append((pltpu.roll(cat, sh, 0) if sh else cat)[HALO:])
    return taps


def _rec_gates(xc, wa_ref, ba_ref, wx_ref, bx_ref, l_ref):
    xcb = xc.astype(BF16)
    r = _sigmoid(_dot(xcb, wa_ref[...]) + ba_ref[...])
    ig = _sigmoid(_dot(xcb, wx_ref[...]) + bx_ref[...])
    ls = _log_sigmoid(l_ref[...])
    log_a = RG_C * r * ls
    return xcb, r, ig, ls, log_a


def _rec_fwd(proj, xr_blk, yr_blk, rec_w, conv_w, conv_b, wa, ba, wx, bx, lru):
    tp = proj.shape[0]
    w = rec_w
    r_rows = REC_ROWS
    nc = tp // r_rows
    cpb = w // LANES

    def body(xr_ref, yr_ref, cw_ref, cb_ref, wa_ref, ba_ref, wx_ref, bx_ref, l_ref,
             hr_ref, rec_ref, prev_s, carry_s, a_s, u_s):
        i = pl.program_id(0)

        @pl.when(i == 0)
        def _():
            prev_s[...] = jnp.zeros_like(prev_s)
            carry_s[...] = jnp.zeros_like(carry_s)

        x = xr_ref[...]
        taps = _conv_taps(jnp.concatenate([prev_s[...], x], axis=0))
        prev_s[...] = x[r_rows - HALO:]
        xc = cb_ref[...]
        for k in range(CONV_WIDTH):
            xc = xc + cw_ref[k:k + 1, :] * taps[k]
        _, r, ig, ls, log_a = _rec_gates(xc, wa_ref, ba_ref, wx_ref, bx_ref, l_ref)
        a = jnp.exp(log_a)
        a_s[...] = a
        u_s[...] = jnp.sqrt(_one_minus_exp(2.0 * log_a, a * a)) * ig * xc

        def tile(j, h):
            r0 = pl.multiple_of(j * SUBLANES, SUBLANES)
            at = a_s[pl.ds(r0, SUBLANES), :]
            ut = u_s[pl.ds(r0, SUBLANES), :]
            out = []
            for rr in range(SUBLANES):
                h = at[rr:rr + 1] * h + ut[rr:rr + 1]
                out.append(h)
            hr_ref[pl.ds(r0, SUBLANES), :] = jnp.concatenate(out, axis=0)
            return h

        carry_s[0:1, :] = lax.fori_loop(0, r_rows // SUBLANES, tile, carry_s[0:1, :])
        g, _ = _gelu_and_grad(yr_ref[...])
        rec_ref[...] = hr_ref[...] * g

    blk = pl.BlockSpec((r_rows, w), lambda i: (i, 0))
    vm = 16 * _nbytes((r_rows, w), F32) + 4 * _nbytes((w, w), BF16)
    return _call(body, (proj, proj, conv_w, conv_b, wa, ba, wx, bx, lru), name="rec_fwd", grid=(nc,),
                 in_specs=[pl.BlockSpec((r_rows, w), lambda i: (i, xr_blk)),
                           pl.BlockSpec((r_rows, w), lambda i: (i, yr_blk)),
                           _full((CONV_WIDTH, w)), _full((1, w)), _full((w, w)), _full((1, w)),
                           _full((w, w)), _full((1, w)), _full((1, w))],
                 out_specs=[blk, blk],
                 out_shape=[jax.ShapeDtypeStruct((tp, w), F32)] * 2,
                 scratch_shapes=[pltpu.VMEM((HALO, w), F32), pltpu.VMEM((SUBLANES, w), F32),
                                 pltpu.VMEM((r_rows, w), F32), pltpu.VMEM((r_rows, w), F32)],
                 semantics=("arbitrary",), vmem_bytes=vm)


def _rec_bwd(proj, xr_blk, yr_blk, rec_w, hr, drec, conv_w, conv_b, wa, ba, wx, bx, lru):
    tp = proj.shape[0]
    w = rec_w
    r_rows = REC_ROWS
    nc = tp // r_rows
    hpc = r_rows // HALO

    def body(xr_ref, xh_ref, yr_ref, hr_ref, hh_ref, drec_ref, cw_ref, cb_ref, wa_ref, ba_ref, wx_ref, bx_ref,
             l_ref, dxr_ref, dyr_ref, dwa_ref, dwx_ref, small_ref, lam_s, a_s, dhr_s, carry_s, next_s):
        i = pl.program_id(0)
        first = (nc - 1 - i) == 0

        @pl.when(i == 0)
        def _():
            carry_s[...] = jnp.zeros_like(carry_s)
            next_s[...] = jnp.zeros_like(next_s)
            dwa_ref[...] = jnp.zeros_like(dwa_ref)
            dwx_ref[...] = jnp.zeros_like(dwx_ref)
            small_ref[...] = jnp.zeros_like(small_ref)

        x = xr_ref[...]
        xprev = jnp.where(first, 0.0, xh_ref[...])
        taps = _conv_taps(jnp.concatenate([xprev, x], axis=0))
        xc = cb_ref[...]
        for k in range(CONV_WIDTH):
            xc = xc + cw_ref[k:k + 1, :] * taps[k]
        xcb, r, ig, ls, log_a = _rec_gates(xc, wa_ref, ba_ref, wx_ref, bx_ref, l_ref)
        a = jnp.exp(log_a)
        a2 = a * a
        mult = jnp.sqrt(_one_minus_exp(2.0 * log_a, a2))
        g, dg = _gelu_and_grad(yr_ref[...])
        hr_v = hr_ref[...]
        drec_v = drec_ref[...]
        dhr_s[...] = drec_v * g
        dyr_ref[...] = (drec_v * hr_v * dg).astype(BF16)
        a_s[...] = a

        def tile(jj, carry):
            r0 = pl.multiple_of((r_rows // SUBLANES - 1 - jj) * SUBLANES, SUBLANES)
            at = a_s[pl.ds(r0, SUBLANES), :]
            dt = dhr_s[pl.ds(r0, SUBLANES), :]
            out = [None] * SUBLANES
            for rr in range(SUBLANES - 1, -1, -1):
                lam = dt[rr:rr + 1] + carry
                out[rr] = lam
                carry = at[rr:rr + 1] * lam
            lam_s[pl.ds(r0, SUBLANES), :] = jnp.concatenate(out, axis=0)
            return carry

        carry_s[0:1, :] = lax.fori_loop(0, r_rows // SUBLANES, tile, carry_s[0:1, :])
        lam = lam_s[...]
        hprev = jnp.where(first, 0.0, hh_ref[...])
        hr_prev = pltpu.roll(jnp.concatenate([hprev, hr_v], axis=0), 1, 0)[HALO:]
        da = lam * hr_prev
        dxc = lam * mult * ig
        di = lam * mult * xc
        dmult = lam * ig * xc
        dlog_a = da * a - dmult * a2 / mult
        dr = dlog_a * (RG_C * ls)
        dls = jnp.sum(dlog_a * (RG_C * r), axis=0, keepdims=True)
        dga = dr * r * (1.0 - r)
        dgx = di * ig * (1.0 - ig)
        dgab = dga.astype(BF16)
        dgxb = dgx.astype(BF16)
        dxc = dxc + _dot_nt(dgab, wa_ref[...]) + _dot_nt(dgxb, wx_ref[...])
        dwa_ref[...] += _dot_tn(xcb, dgab)
        dwx_ref[...] += _dot_tn(xcb, dgxb)
        cat = jnp.concatenate([dxc, next_s[...]], axis=0)
        next_s[...] = dxc[0:HALO]
        dxr = cw_ref[CONV_WIDTH - 1:CONV_WIDTH, :] * dxc
        for k in range(CONV_WIDTH - 1):
            sh = CONV_WIDTH - 1 - k
            dxr = dxr + cw_ref[k:k + 1, :] * pltpu.roll(cat, r_rows + HALO - sh, 0)[:r_rows]
        dxr_ref[...] = dxr.astype(BF16)
        rows = [jnp.sum(dxc * taps[k], axis=0, keepdims=True) for k in range(CONV_WIDTH)]
        rows += [jnp.sum(dxc, axis=0, keepdims=True), jnp.sum(dga, axis=0, keepdims=True),
                 jnp.sum(dgx, axis=0, keepdims=True), dls * _sigmoid(-l_ref[...])]
        small_ref[...] += jnp.concatenate(rows, axis=0)

    def rev(i):
        return nc - 1 - i

    def halo(i):
        return jnp.maximum(rev(i) * hpc - 1, 0)

    blk = pl.BlockSpec((r_rows, w), lambda i: (rev(i), 0))
    vm = 40 * _nbytes((r_rows, w), F32) + 6 * _nbytes((w, w), F32)
    return _call(body, (proj, proj, proj, hr, hr, drec, conv_w, conv_b, wa, ba, wx, bx, lru),
                 name="rec_bwd", grid=(nc,),
                 in_specs=[pl.BlockSpec((r_rows, w), lambda i: (rev(i), xr_blk)),
                           pl.BlockSpec((HALO, w), lambda i: (halo(i), xr_blk)),
                           pl.BlockSpec((r_rows, w), lambda i: (rev(i), yr_blk)),
                           blk,
                           pl.BlockSpec((HALO, w), lambda i: (halo(i), 0)),
                           blk,
                           _full((CONV_WIDTH, w)), _full((1, w)), _full((w, w)), _full((1, w)),
                           _full((w, w)), _full((1, w)), _full((1, w))],
                 out_specs=[blk, blk, _full((w, w)), _full((w, w)), _full((SUBLANES, w))],
                 out_shape=[jax.ShapeDtypeStruct((tp, w), BF16)] * 2
                 + [jax.ShapeDtypeStruct((w, w), F32)] * 2 + [jax.ShapeDtypeStruct((SUBLANES, w), F32)],
                 scratch_shapes=[pltpu.VMEM((r_rows, w), F32)] * 3
                 + [pltpu.VMEM((SUBLANES, w), F32), pltpu.VMEM((HALO, w), F32)],
                 semantics=("arbitrary",), vmem_bytes=vm)


ROW_TARGET = 544


def _mixer_out(attn, rec, g_a, g_r, w_out, h, g_next):
    tp, d = h.shape
    aw, rw = attn.shape[1], rec.shape[1]
    kc = d // N_CHIPS
    tm = _divisor_tile(tp, 16, ROW_TARGET)

    def body(a_ref, r_ref, ga_ref, gr_ref, w_ref, h_ref, gn_ref, h1_ref, z_ref, mix_ref):
        a = a_ref[...]
        r = r_ref[...]
        mix = jnp.concatenate([a * _rstd(a) * ga_ref[...], r * _rstd(r) * gr_ref[...]], axis=1).astype(BF16)
        mix_ref[...] = mix
        h1 = h_ref[...]
        for j in range(N_CHIPS):
            h1 = h1 + _dot(mix[:, j * kc:(j + 1) * kc], w_ref[j])
        h1_ref[...] = h1
        z_ref[...] = (h1 * _rstd(h1) * gn_ref[...]).astype(BF16)

    row = lambda wd: pl.BlockSpec((tm, wd), lambda i: (i, 0))
    vm = 2 * _nbytes((d, d), BF16) + 12 * _nbytes((tm, d), F32)
    return _call(body, (attn, rec, g_a, g_r, w_out, h, g_next), name="mixer_out", grid=(tp // tm,),
                 in_specs=[row(aw), row(rw), _full((1, aw)), _full((1, rw)), _full(w_out.shape), row(d),
                           _full((1, d))],
                 out_specs=[row(d), row(d), row(d)],
                 out_shape=[jax.ShapeDtypeStruct((tp, d), F32), jax.ShapeDtypeStruct((tp, d), BF16),
                            jax.ShapeDtypeStruct((tp, d), BF16)],
                 semantics=("parallel",), vmem_bytes=vm)


def _mixer_bwd(dh_b, w_out, attn, rec, g_a, g_r):
    tp, d = dh_b.shape
    aw, rw = attn.shape[1], rec.shape[1]
    tm = _divisor_tile(tp, 16, ROW_TARGET)

    def body(dh_ref, w_ref, a_ref, r_ref, ga_ref, gr_ref, da_ref, dr_ref, dg_ref):
        @pl.when(pl.program_id(0) == 0)
        def _():
            dg_ref[...] = jnp.zeros_like(dg_ref)

        dh = dh_ref[...]
        dmix = jnp.concatenate([_dot_nt(dh, w_ref[j]) for j in range(N_CHIPS)], axis=1)
        da, dga = _rms_bwd(dmix[:, :aw], a_ref[...], ga_ref[...])
        dr, dgr = _rms_bwd(dmix[:, aw:], r_ref[...], gr_ref[...])
        da_ref[...] = da
        dr_ref[...] = dr
        dg_ref[...] += jnp.broadcast_to(jnp.concatenate([dga, dgr], axis=1), (SUBLANES, d))

    row = lambda wd: pl.BlockSpec((tm, wd), lambda i: (i, 0))
    vm = 2 * _nbytes((d, d), BF16) + 12 * _nbytes((tm, d), F32)
    return _call(body, (dh_b, w_out, attn, rec, g_a, g_r), name="mixer_bwd", grid=(tp // tm,),
                 in_specs=[row(d), _full(w_out.shape), row(aw), row(rw), _full((1, aw)), _full((1, rw))],
                 out_specs=[row(aw), row(rw), _full((SUBLANES, d))],
                 out_shape=[jax.ShapeDtypeStruct((tp, aw), F32), jax.ShapeDtypeStruct((tp, rw), F32),
                            jax.ShapeDtypeStruct((SUBLANES, d), F32)],
                 semantics=("arbitrary",), vmem_bytes=vm)


def _mlp_up(z, w_up):
    tp, d = z.shape
    fc = w_up.shape[2]
    ff = N_CHIPS * fc
    tn = _divisor_tile(fc, LANES, 512)
    per = fc // tn

    def body(z_ref, w_ref, act_ref, up_ref):
        up = _dot(z_ref[...], w_ref[...])
        r = jnp.maximum(up, 0.0)
        act_ref[...] = (r * r).astype(BF16)
        up_ref[...] = up.astype(BF16)

    col = pl.BlockSpec((tp, tn), lambda j: (0, j))
    vm = 2 * _nbytes((tp, d), BF16) + 2 * _nbytes((d, tn), BF16) + 8 * _nbytes((tp, tn), F32)
    return _call(body, (z, w_up), name="mlp_up", grid=(ff // tn,),
                 in_specs=[_full((tp, d)), pl.BlockSpec((None, d, tn), lambda j: (j // per, 0, j % per))],
                 out_specs=[col, col],
                 out_shape=[jax.ShapeDtypeStruct((tp, ff), BF16)] * 2,
                 semantics=("parallel",), vmem_bytes=vm)


def _mlp_down(act, w_down, h, g_next):
    tp, d = h.shape
    ff = act.shape[1]
    fc = ff // N_CHIPS
    tm = _divisor_tile(tp, 16, ROW_TARGET)

    def body(a_ref, w_ref, h_ref, gn_ref, h2_ref, z_ref):
        h2 = h_ref[...]
        for j in range(N_CHIPS):
            h2 = h2 + _dot(a_ref[:, j * fc:(j + 1) * fc], w_ref[j])
        h2_ref[...] = h2
        z_ref[...] = (h2 * _rstd(h2) * gn_ref[...]).astype(BF16)

    row = lambda wd: pl.BlockSpec((tm, wd), lambda i: (i, 0))
    vm = 2 * _nbytes((ff, d), BF16) + 2 * _nbytes((tm, ff), BF16) + 10 * _nbytes((tm, d), F32)
    return _call(body, (act, w_down, h, g_next), name="mlp_down", grid=(tp // tm,),
                 in_specs=[row(ff), _full(w_down.shape), row(d), _full((1, d))],
                 out_specs=[row(d), row(d)],
                 out_shape=[jax.ShapeDtypeStruct((tp, d), F32), jax.ShapeDtypeStruct((tp, d), BF16)],
                 semantics=("parallel",), vmem_bytes=vm)


def _loss_bwd(h, g, target, n_real):
    tp, d = h.shape
    tm = _divisor_tile(tp, 16, ROW_TARGET)

    def body(h_ref, g_ref, t_ref, dh_ref, dhb_ref, dg_ref, loss_ref):
        i = pl.program_id(0)

        @pl.when(i == 0)
        def _():
            dg_ref[...] = jnp.zeros_like(dg_ref)
            loss_ref[...] = jnp.zeros_like(loss_ref)

        x = h_ref[...]
        gv = g_ref[...]
        rowi = i * tm + lax.broadcasted_iota(jnp.int32, (tm, 1), 0)
        real = jnp.logical_and(rowi >= N_META, rowi < N_META + n_real)
        err = jnp.where(real, x * _rstd(x) * gv - t_ref[...], 0.0)
        loss_ref[...] += 0.5 * jnp.sum(jnp.mean(err * err, axis=-1, keepdims=True))
        dx, dgp = _rms_bwd(err * (1.0 / d), x, gv)
        dh_ref[...] = dx
        dhb_ref[...] = dx.astype(BF16)
        dg_ref[...] += jnp.broadcast_to(dgp, (SUBLANES, d))

    row = pl.BlockSpec((tm, d), lambda i: (i, 0))
    return _call(body, (h, g, target), name="loss_bwd", grid=(tp // tm,),
                 in_specs=[row, _full((1, d)), row],
                 out_specs=[row, row, _full((SUBLANES, d)), _full((SUBLANES, LANES))],
                 out_shape=[jax.ShapeDtypeStruct((tp, d), F32), jax.ShapeDtypeStruct((tp, d), BF16),
                            jax.ShapeDtypeStruct((SUBLANES, d), F32), jax.ShapeDtypeStruct((SUBLANES, LANES), F32)],
                 semantics=("arbitrary",), vmem_bytes=16 * _nbytes((tm, d), F32))


def _mlp_bwd(dh_b, w_down, up, z2):
    tp, d = dh_b.shape
    fc = w_down.shape[1]
    ff = N_CHIPS * fc
    tn = _divisor_tile(fc, LANES, 512)
    per = fc // tn

    def body(dh_ref, z_ref, w_ref, up_ref, dup_ref, gd_ref, gu_ref):
        dh = dh_ref[...]
        r = jnp.maximum(up_ref[...].astype(F32), 0.0)
        dup = (_dot_nt(dh, w_ref[...]) * (2.0 * r)).astype(BF16)
        dup_ref[...] = dup
        gd_ref[...] = _dot_tn((r * r).astype(BF16), dh).astype(BF16)
        gu_ref[...] = _dot_tn(z_ref[...], dup).astype(BF16)

    col = pl.BlockSpec((tp, tn), lambda j: (0, j))
    vm = 4 * _nbytes((tp, d), BF16) + 4 * _nbytes((tn, d), BF16) + 2 * _nbytes((d, tn), BF16) \
        + 10 * _nbytes((tp, tn), F32) + 4 * _nbytes((tn, d), F32)
    return _call(body, (dh_b, z2, w_down, up), name="mlp_bwd", grid=(ff // tn,),
                 in_specs=[_full((tp, d)), _full((tp, d)),
                           pl.BlockSpec((None, tn, d), lambda j: (j // per, j % per, 0)), col],
                 out_specs=[col, pl.BlockSpec((tn, d), lambda j: (j, 0)),
                            pl.BlockSpec((None, d, tn), lambda j: (j // per, 0, j % per))],
                 out_shape=[jax.ShapeDtypeStruct((tp, ff), BF16), jax.ShapeDtypeStruct((ff, d), BF16),
                            jax.ShapeDtypeStruct((N_CHIPS, d, fc), BF16)],
                 semantics=("parallel",), vmem_bytes=vm)


def _grad_w_pieces(pieces, b):
    tp, n = b.shape
    tn = _divisor_tile(n, LANES, 512)
    widths = [pc.shape[1] for pc in pieces]

    def body(*refs):
        p_refs, b_ref, o_refs = refs[:len(pieces)], refs[len(pieces)], refs[len(pieces) + 1:]
        for p_ref, o_ref in zip(p_refs, o_refs):
            o_ref[...] = _dot_tn(p_ref[...], b_ref[...]).astype(BF16)

    vm = 2 * sum(_nbytes((tp, wd), BF16) for wd in widths) + 2 * _nbytes((tp, tn), BF16) \
        + 4 * sum(_nbytes((wd, tn), F32) for wd in widths) + 2 * _nbytes((tp, max(widths)), F32)
    return _call(body, tuple(pieces) + (b,), name="grad_w_pieces", grid=(n // tn,),
                 in_specs=[_full(pc.shape) for pc in pieces] + [pl.BlockSpec((tp, tn), lambda j: (0, j))],
                 out_specs=[pl.BlockSpec((wd, tn), lambda j: (0, j)) for wd in widths],
                 out_shape=[jax.ShapeDtypeStruct((wd, n), BF16) for wd in widths],
                 semantics=("parallel",), vmem_bytes=vm)


def _dx_norm_bwd(pieces, w, w_spec, w_piece, h, g, dres, dot=_dot_nt):
    tp, d = h.shape
    tm = _divisor_tile(tp, 16, ROW_TARGET)
    n = len(pieces)

    def body(*refs):
        dy_refs = refs[:n]
        w_ref, h_ref, g_ref, dres_ref, dh_ref, dhb_ref, dg_ref = refs[n:]

        @pl.when(pl.program_id(0) == 0)
        def _():
            dg_ref[...] = jnp.zeros_like(dg_ref)

        dz = dot(dy_refs[0][...], w_piece(w_ref, 0))
        for i in range(1, n):
            dz = dz + dot(dy_refs[i][...], w_piece(w_ref, i))
        dx, dgp = _rms_bwd(dz, h_ref[...], g_ref[...])
        dh = dres_ref[...] + dx
        dh_ref[...] = dh
        dhb_ref[...] = dh.astype(BF16)
        dg_ref[...] += jnp.broadcast_to(dgp, (SUBLANES, d))

    row = lambda wd: pl.BlockSpec((tm, wd), lambda i: (i, 0))
    kk = sum(wd for _, _, wd in pieces)
    vm = 2 * _nbytes((d, kk), BF16) + 2 * _nbytes((tm, kk), BF16) + 14 * _nbytes((tm, d), F32)
    piece_specs = [pl.BlockSpec((tm, wd), functools.partial(lambda i, cb: (i, cb), cb=cb)) for _, cb, wd in pieces]
    return _call(body, tuple(a for a, _, _ in pieces) + (w, h, g, dres), name="dx_norm_bwd", grid=(tp // tm,),
                 in_specs=piece_specs + [w_spec, row(d), _full((1, d)), row(d)],
                 out_specs=[row(d), row(d), _full((SUBLANES, d))],
                 out_shape=[jax.ShapeDtypeStruct((tp, d), F32), jax.ShapeDtypeStruct((tp, d), BF16),
                            jax.ShapeDtypeStruct((SUBLANES, d), F32)],
                 semantics=("arbitrary",), vmem_bytes=vm)


def _block_diag(wg):
    nb, b, _ = wg.shape
    eye = jnp.eye(nb, dtype=wg.dtype)
    return (eye[:, None, :, None] * wg[:, :, None, :]).reshape(nb * b, nb * b)


def _diag_blocks(dense, nb):
    b = dense.shape[0] // nb
    d4 = dense.reshape(nb, b, nb, b)
    return jnp.stack([d4[i, :, i, :] for i in range(nb)])


def _row(v):
    return v.reshape(1, -1)


def _forward_layer(l, h, z, p, fetch, stage_next, g_next):
    d = h.shape[1]
    att_w = d // 2
    rec_w = d - att_w
    nh = att_w // HEAD_DIM
    wa_d = _block_diag(p["w_gate_a"][l]).astype(BF16)
    wx_d = _block_diag(p["w_gate_x"][l]).astype(BF16)
    b_f_pad = jnp.zeros((1, LANES), F32).at[0, :nh].set(p["b_f"][l])
    w_in_t = fetch("w_in", z)
    big = dict(w_in_big=_pack_w_in_t(w_in_t.reshape(-1, d), att_w, nh))
    qkv, proj = _proj(z, big["w_in_big"], att_w)
    c, c_t = _fgate_fwd(proj, b_f_pad, nh)
    attn, lse_b = _attn_fwd(qkv, c, c_t, nh)
    hr, rec = _rec_fwd(proj, 0, 1, rec_w, p["conv_w"][l], _row(p["conv_b"][l]), wa_d,
                       _row(p["b_gate_a"][l]), wx_d, _row(p["b_gate_x"][l]), _row(p["lru_L"][l]))
    tok = stage_next(attn, "own")
    big["w_out"] = fetch("w_out", rec)
    h1, z2, mix = _mixer_out(attn, rec, _row(p["attn_out_g"][l] + tok), _row(p["rec_out_g"][l]),
                             big["w_out"], h, _row(p["mlp_norm_g"][l]))
    big["w_up"] = fetch("w_up", h1)
    act, up = _mlp_up(z2, big["w_up"])
    tok = stage_next(act, "next")
    big["w_down"] = fetch("w_down", act)
    h2, z_next = _mlp_down(act, big["w_down"], h1, _row(g_next + tok))
    saved = dict(h0=h, z1=z, proj=proj, qkv=qkv, c=c, c_t=c_t, attn=attn, lse_b=lse_b, hr=hr, rec=rec, h1=h1,
                 z2=z2, mix=mix, up=up, wa_d=wa_d, wx_d=wx_d, b_f_pad=b_f_pad, big=big)
    return h2, z_next, saved


def _backward_mlp(l, dh, dh_b, sv, p, tok):
    w_up, w_down = sv["big"]["w_up"], sv["big"]["w_down"]
    fc = w_up.shape[2]
    dup, g_down, g_up = _mlp_bwd(dh_b, w_down, sv["up"], sv["z2"])
    dh, dh_b, dg2 = _dx_norm_bwd([(dup, j, fc) for j in range(N_CHIPS)], w_up, _full(w_up.shape),
                                 lambda w_ref, j: w_ref[j], sv["h1"], _row(p["mlp_norm_g"][l] + tok), dh)
    big = dict(w_down=g_down.reshape((N_CHIPS, -1) + g_down.shape[1:]), w_up=g_up)
    return dh, dh_b, big, dict(mlp_norm_g=dg2[0])


def _backward_mixer(l, dh, dh_b, sv, p, tok):
    d = dh.shape[1]
    att_w = d // 2
    rec_w = d - att_w
    nh = att_w // HEAD_DIM
    small = {}
    dattn, drec, dg_mix = _mixer_bwd(dh_b, sv["big"]["w_out"], sv["attn"], sv["rec"],
                                     _row(p["attn_out_g"][l] + tok), _row(p["rec_out_g"][l]))
    small["attn_out_g"] = dg_mix[0, :att_w]
    small["rec_out_g"] = dg_mix[0, att_w:]
    dxr, dyr, dwa, dwx, sm = _rec_bwd(
        sv["proj"], 0, 1, rec_w, sv["hr"], drec, p["conv_w"][l], _row(p["conv_b"][l]), sv["wa_d"],
        _row(p["b_gate_a"][l]), sv["wx_d"], _row(p["b_gate_x"][l]), _row(p["lru_L"][l]))
    small.update(conv_w=sm[:CONV_WIDTH], conv_b=sm[4], b_gate_a=sm[5], b_gate_x=sm[6], lru_L=sm[7],
                 w_gate_a=_diag_blocks(dwa, N_REC_BLOCKS), w_gate_x=_diag_blocks(dwx, N_REC_BLOCKS))
    dq, dk, dv, dc = _attn_bwd(sv["qkv"], sv["c"], sv["c_t"], sv["lse_b"], dattn, nh)
    df, db_f = _fgate_bwd(sv["proj"], sv["b_f_pad"], dc)
    small["b_f"] = db_f[0, :nh]
    pieces = [dq, dk, dv, dxr, dyr, df]
    offs = [0, att_w, 2 * att_w, 3 * att_w, 3 * att_w + rec_w, 3 * att_w + 2 * rec_w]
    gq, gk, gv, gxr, gyr, gf = _grad_w_pieces(pieces, sv["z1"])
    g_in_t = jnp.concatenate([gq, gk, gv, gf[:nh], gxr, gyr], axis=0)
    w_big = sv["big"]["w_in_big"]
    widths = [pc.shape[1] for pc in pieces]
    dh, dh_b, dg1 = _dx_norm_bwd(
        [(pc, 0, wd) for pc, wd in zip(pieces, widths)], w_big, _full(w_big.shape),
        lambda w_ref, i: w_ref[offs[i]:offs[i] + widths[i], :], sv["h0"], _row(p["attn_norm_g"][l]), dh, dot=_dot)
    small["attn_norm_g"] = dg1[0]
    big = dict(w_in=g_in_t.reshape(N_CHIPS, -1, d))
    return dh, dh_b, big, small


def _pack_w_in_t(w_in_t, att_w, nh):
    qkv = w_in_t[:3 * att_w]
    f = w_in_t[3 * att_w:3 * att_w + nh]
    xy = w_in_t[3 * att_w + nh:]
    return jnp.concatenate([qkv, xy, f, jnp.zeros((LANES - nh, w_in_t.shape[1]), w_in_t.dtype)], axis=0)


ANY = pl.BlockSpec(memory_space=pl.ANY)


def _coords():
    return lax.axis_index("x"), lax.axis_index("y"), lax.axis_index("c")


def _other_chips(x, y):
    return [(1 - x, y), (x, 1 - y), (1 - x, 1 - y)]


def _remote(src, dst, send_sems, recv_sems, k, to):
    return pltpu.make_async_remote_copy(src_ref=src, dst_ref=dst, send_sem=send_sems.at[k],
                                        recv_sem=recv_sems.at[k], device_id=to, device_id_type=MESH)


def _all_gather_chips(shards):
    n = len(shards)
    per = 6

    def body(*refs):
        ins, outs = refs[:n], refs[n:2 * n]
        send_sems, recv_sems, local_sems = refs[2 * n:]
        x, y, c = _coords()
        me = 2 * x + y
        sibling = (x, y, 1 - c)
        chips = _other_chips(x, y)
        local = [pltpu.make_async_copy(ins[t], outs[t].at[me], local_sems.at[t]) for t in range(n)]
        for cp in local:
            cp.start()
        sends = []
        for t in range(n):
            for j, (px, py) in enumerate(chips):
                cp = _remote(ins[t].at[c], outs[t].at[me, c], send_sems, recv_sems, per * t + j, (px, py, c))
                cp.start()
                sends.append(cp)
        for t in range(n):
            for j, (px, py) in enumerate(chips):
                landed = outs[t].at[2 * px + py, c]
                _remote(landed, landed, send_sems, recv_sems, per * t + j, (px, py, c)).wait_recv()
                cp = _remote(landed, landed, send_sems, recv_sems, per * t + 3 + j, sibling)
                cp.start()
                sends.append(cp)
        for t in range(n):
            for j, (px, py) in enumerate(chips):
                passed = outs[t].at[2 * px + py, 1 - c]
                _remote(passed, passed, send_sems, recv_sems, per * t + 3 + j, sibling).wait_recv()
        for cp in sends:
            cp.wait_send()
        for cp in local:
            cp.wait()

    return _call(body, tuple(shards), name="all_gather_chips",
                 in_specs=[ANY] * n, out_specs=[ANY] * n,
                 out_shape=[jax.ShapeDtypeStruct((N_CHIPS,) + s.shape, s.dtype) for s in shards],
                 scratch_shapes=[pltpu.SemaphoreType.DMA((per * n,)), pltpu.SemaphoreType.DMA((per * n,)),
                                 pltpu.SemaphoreType.DMA((n,))])


HBM = pl.BlockSpec(memory_space=pltpu.HBM)
SEM = pl.BlockSpec(memory_space=pltpu.SEMAPHORE)
DATAFLOW = pltpu.SideEffectType.DATAFLOW_SIDE_EFFECTING


def _in_hbm(a):
    return pltpu.with_memory_space_constraint(a, pltpu.HBM)


PUSH_ARRIVALS = {"gather_chips_half": N_CHIPS - 1, "pass_halves": N_CHIPS - 1, "scatter_chips": N_CHIPS - 1,
                 "sibling": 1, "gather_devices": N_DEV - 1}


def _column_half(ref3, slab, c):
    hw = ref3.shape[2] // 2
    return ref3.at[slab, :, pl.ds(pl.multiple_of(c * hw, LANES), hw)]


def _push_copies(mode, src, land, send_sems, recv_sems, t):
    x, y, c = _coords()
    chip = 2 * x + y
    if mode == "gather_chips_half":
        return [_remote(_column_half(src, chip, c), _column_half(land, chip, c), send_sems, recv_sems, t, (px, py, c))
                for px, py in _other_chips(x, y)]
    if mode == "pass_halves":
        return [_remote(_column_half(src, 2 * px + py, c), _column_half(land, 2 * px + py, c), send_sems, recv_sems, t,
                        (x, y, 1 - c)) for px, py in _other_chips(x, y)]
    if mode == "scatter_chips":
        return [_remote(src.at[2 * px + py], land.at[chip], send_sems, recv_sems, t, (px, py, c))
                for px, py in _other_chips(x, y)]
    if mode == "sibling":
        return [_remote(src, land, send_sems, recv_sems, t, (x, y, 1 - c))]
    dev = 4 * x + 2 * y + c
    return [_remote(src.at[dev], land.at[dev], send_sems, recv_sems, t, (x ^ (k >> 2), y ^ ((k >> 1) & 1), c ^ (k & 1)))
            for k in range(1, N_DEV)]


def _push_start(srcs, lands, mode, name):
    n = len(srcs)
    same = all(s is ld for s, ld in zip(srcs, lands))
    n_in = n if same else 2 * n

    def body(*refs):
        src_refs = refs[:n]
        land_refs = src_refs if same else refs[n:2 * n]
        send_sems, recv_sems = refs[n_in], refs[n_in + 1]
        token = refs[-1]
        for t in range(n):
            for cp in _push_copies(mode, src_refs[t], land_refs[t], send_sems, recv_sems, t):
                cp.start()
        token[...] = jnp.zeros_like(token)

    operands = tuple(srcs) if same else tuple(srcs) + tuple(lands)
    res = _call(
        body, [_in_hbm(a) for a in operands], name=name,
        out_shape=(pltpu.SemaphoreType.DMA((n,)), pltpu.SemaphoreType.DMA((n,)))
        + tuple(pltpu.HBM(a.shape, a.dtype) for a in operands) + (jax.ShapeDtypeStruct((SUBLANES, LANES), F32),),
        in_specs=[HBM] * n_in, out_specs=(SEM, SEM) + (HBM,) * n_in + (pl.BlockSpec(memory_space=pltpu.VMEM),),
        input_output_aliases={i: 2 + i for i in range(n_in)}, side_effects=DATAFLOW, hbm_results=False)
    send_sems, recv_sems, token = res[0], res[1], res[-1]
    srcs_thru = res[2:2 + n]
    lands_thru = srcs_thru if same else res[2 + n:2 + 2 * n]
    return send_sems, recv_sems, srcs_thru, lands_thru, token


def _push_wait(send_sems, recv_sems, ids, srcs, lands, mode, after, name):
    n = len(lands)
    same = all(s is ld for s, ld in zip(srcs, lands))
    n_in = n if same else 2 * n

    def body(*refs):
        land_refs = refs[:n] if same else refs[n:2 * n]
        send_sems, recv_sems = refs[n_in], refs[n_in + 1]
        x, y, c = _coords()
        for t in range(n):
            if mode == "sibling":
                moved = land_refs[t]
            elif mode in ("gather_chips_half", "pass_halves"):
                moved = land_refs[t].at[pl.ds(0, PUSH_ARRIVALS[mode]), :, pl.ds(0, land_refs[t].shape[2] // 2)]
            else:
                moved = land_refs[t].at[pl.ds(0, PUSH_ARRIVALS[mode])]
            arrivals = _remote(moved, moved, send_sems, recv_sems, ids[t], (x, y, c))
            arrivals.wait_send()
            arrivals.wait_recv()

    operands = tuple(lands) if same else tuple(srcs) + tuple(lands)
    res = _call(
        body, operands + (send_sems, recv_sems, after), name=name,
        out_shape=tuple(pltpu.HBM(a.shape, a.dtype) for a in operands),
        in_specs=[HBM] * n_in + [SEM, SEM, ANY], out_specs=(HBM,) * n_in,
        input_output_aliases={i: i for i in range(n_in)}, side_effects=DATAFLOW)
    return list(res) if same else (list(res[:n]), list(res[n:]))


def _sum_partials(part, landed, chip):
    _, rows, cols = part.shape
    br = _divisor_tile(rows, 16, ELEM_ROWS)

    def body(chip_ref, own_ref, a_ref, b_ref, c_ref, o_ref):
        o_ref[...] = ((own_ref[...].astype(F32) + a_ref[...].astype(F32)) + b_ref[...].astype(F32)) \
            + c_ref[...].astype(F32)

    def other(k):
        return pl.BlockSpec((None, br, cols), lambda i, ch: (jnp.where(ch[0] <= k, k + 1, k), i, 0))

    spec = pltpu.PrefetchScalarGridSpec(
        num_scalar_prefetch=1, grid=(rows // br,),
        in_specs=[pl.BlockSpec((None, br, cols), lambda i, ch: (ch[0], i, 0)), other(0), other(1), other(2)],
        out_specs=pl.BlockSpec((br, cols), lambda i, ch: (i, 0)))
    return _call(body, (chip, part, landed, landed, landed), name="sum_partials", grid_spec=spec,
                 out_shape=jax.ShapeDtypeStruct((rows, cols), F32), semantics=("parallel",))


def _cast_to_slab(w, l, chip):
    _, rows, cols = w.shape
    br = _divisor_tile(rows, 16, ELEM_ROWS)

    def body(chip_ref, w_ref, o_ref):
        o_ref[...] = w_ref[...].astype(BF16)

    spec = pltpu.PrefetchScalarGridSpec(
        num_scalar_prefetch=1, grid=(rows // br,),
        in_specs=[pl.BlockSpec((None, br, cols), lambda i, ch: (l, i, 0))],
        out_specs=pl.BlockSpec((None, br, cols), lambda i, ch: (ch[0], i, 0)))
    return _call(body, (chip, w), name="cast_to_slab", grid_spec=spec,
                 out_shape=jax.ShapeDtypeStruct((N_CHIPS, rows, cols), BF16), semantics=("parallel",))


def _cast_w_in_t_to_slabs(w_t, chip):
    rows, depth, d = w_t.shape
    tn = _divisor_tile(d, LANES, 256)

    def body(chip_ref, w_ref, *o_refs):
        for l in range(depth):
            o_refs[l][...] = w_ref[:, l, :].astype(BF16)

    spec = pltpu.PrefetchScalarGridSpec(
        num_scalar_prefetch=1, grid=(d // tn,),
        in_specs=[pl.BlockSpec((rows, depth, tn), lambda j, ch: (0, 0, j))],
        out_specs=[pl.BlockSpec((None, rows, tn), lambda j, ch: (ch[0], 0, j))] * depth)
    return _call(body, (chip, w_t), name="cast_w_in_t_to_slabs", grid_spec=spec,
                 out_shape=[jax.ShapeDtypeStruct((N_CHIPS, rows, d), BF16)] * depth, semantics=("parallel",),
                 vmem_bytes=4 * _nbytes((rows, max(depth, SUBLANES), tn), F32))


def _place_slab(buf, index, n_slabs):
    rows, cols = buf.shape
    br = _divisor_tile(rows, SUBLANES, ELEM_ROWS)

    def body(index_ref, b_ref, o_ref):
        o_ref[...] = b_ref[...]

    spec = pltpu.PrefetchScalarGridSpec(
        num_scalar_prefetch=1, grid=(rows // br,),
        in_specs=[pl.BlockSpec((br, cols), lambda i, ix: (i, 0))],
        out_specs=pl.BlockSpec((None, br, cols), lambda i, ix: (ix[0], i, 0)))
    return _call(body, (index, buf), name="place_slab", grid_spec=spec,
                 out_shape=jax.ShapeDtypeStruct((n_slabs, rows, cols), buf.dtype), semantics=("parallel",))


ELEM_ROWS = 256


def _sum_slabs(r):
    n, rows, cols = r.shape
    br = _divisor_tile(rows, 16, ELEM_ROWS)

    def body(r_ref, o_ref):
        acc = r_ref[0].astype(F32)
        for j in range(1, n):
            acc = acc + r_ref[j].astype(F32)
        o_ref[...] = acc

    return _call(body, (r,), name="sum_slabs", grid=(rows // br,),
                 in_specs=[pl.BlockSpec((n, br, cols), lambda i: (0, i, 0))],
                 out_specs=pl.BlockSpec((br, cols), lambda i: (i, 0)),
                 out_shape=jax.ShapeDtypeStruct((rows, cols), F32), semantics=("parallel",))


def _adamw_math(w, g, m, v):
    c1 = 1.0 - ADAM_B1 ** ADAM_STEP
    c2 = 1.0 - ADAM_B2 ** ADAM_STEP
    nm = ADAM_B1 * m + (1.0 - ADAM_B1) * g
    nv = ADAM_B2 * v + (1.0 - ADAM_B2) * (g * g)
    delta = -ADAM_LR * ((nm / c1) / (jnp.sqrt(nv / c2) + ADAM_EPS) + ADAM_WD * w)
    return delta, nm, nv


def _adamw(w, g, m, v):
    rows, cols = w.shape
    br = _divisor_tile(rows, 8, ELEM_ROWS)

    def body(w_ref, g_ref, m_ref, v_ref, d_ref, nm_ref, nv_ref):
        d_ref[...], nm_ref[...], nv_ref[...] = _adamw_math(w_ref[...], g_ref[...], m_ref[...], v_ref[...])

    blk = pl.BlockSpec((br, cols), lambda i: (i, 0))
    return _call(body, (w, g, m, v), name="adamw", grid=(rows // br,),
                 in_specs=[blk] * 4, out_specs=[blk] * 3,
                 out_shape=[jax.ShapeDtypeStruct((rows, cols), F32)] * 3, semantics=("parallel",))


def _adamw_w_in_t(w_t, m_t, v_t, g_mine, g_theirs):
    rows, depth, d = w_t.shape
    tn = LANES

    def body(w_ref, m_ref, v_ref, *rest):
        ga_refs, gb_refs = rest[:depth], rest[depth:2 * depth]
        g_ref, d_ref, nm_ref, nv_ref = rest[2 * depth:]
        for l in range(depth):
            g_ref[:, l, :] = ga_refs[l][...] + gb_refs[l][...]
        d_ref[...], nm_ref[...], nv_ref[...] = _adamw_math(w_ref[...], g_ref[...], m_ref[...], v_ref[...])

    slab = pl.BlockSpec((rows, depth, tn), lambda j: (0, 0, j))
    gblk = pl.BlockSpec((rows, tn), lambda j: (0, j))
    return _call(body, (w_t, m_t, v_t) + tuple(g_mine) + tuple(g_theirs), name="adamw_w_in_t", grid=(d // tn,),
                 in_specs=[slab] * 3 + [gblk] * (2 * depth), out_specs=[slab] * 4,
                 out_shape=[jax.ShapeDtypeStruct(w_t.shape, F32)] * 4, semantics=("parallel",),
                 vmem_bytes=2 * (7 * _nbytes((rows, max(depth, SUBLANES), tn), F32)
                                 + 2 * depth * _nbytes((rows, tn), F32)))


def _adamw_layer(w, m, v, l, g_mine, g_theirs, prev, after):
    _, rows, cols = w.shape
    br = _divisor_tile(rows, 8, ELEM_ROWS)

    def body(w_ref, m_ref, v_ref, ga_ref, gb_ref, *rest):
        g_ref, d_ref, nm_ref, nv_ref = rest[5:]
        g = ga_ref[...] + gb_ref[...]
        g_ref[...] = g
        d_ref[...], nm_ref[...], nv_ref[...] = _adamw_math(w_ref[...], g, m_ref[...], v_ref[...])

    slot = pl.BlockSpec((None, br, cols), lambda i: (l, i, 0))
    blk = pl.BlockSpec((br, cols), lambda i: (i, 0))
    return _call(body, (w, m, v, g_mine, g_theirs) + tuple(prev) + (after,), name="adamw_layer",
                 grid=(rows // br,), in_specs=[slot] * 3 + [blk] * 2 + [ANY] * 5, out_specs=[slot] * 4,
                 out_shape=[jax.ShapeDtypeStruct(w.shape, F32)] * 4,
                 input_output_aliases={5: 0, 6: 1, 7: 2, 8: 3}, semantics=("parallel",))


BIG = ("w_in", "w_out", "w_up", "w_down")
WEIGHTS = ("meta", "attn_norm_g", "w_in", "b_f", "conv_w", "conv_b", "w_gate_a", "b_gate_a", "w_gate_x",
           "b_gate_x", "lru_L", "attn_out_g", "rec_out_g", "w_out", "mlp_norm_g", "w_up", "w_down", "final_g")
SMALL = tuple(k for k in WEIGHTS if k not in BIG)
COL_SHARDED_SMALL = ("meta", "conv_w")


def _packed_rows(shape):
    return -(-math.prod(shape) // (SUBLANES * LANES)) * SUBLANES


def _pack(arrs):
    rows = []
    for a in arrs:
        flat = a.reshape(-1)
        rows.append(jnp.pad(flat, (0, _packed_rows(a.shape) * LANES - flat.shape[0])).reshape(-1, LANES))
    used = sum(r.shape[0] for r in rows)
    rows.append(jnp.zeros((-used % ELEM_ROWS, LANES), F32))
    return jnp.concatenate(rows, axis=0)


def _unpack(buf, shapes):
    out, r0 = [], 0
    for s in shapes:
        nr = _packed_rows(s)
        out.append(buf[r0:r0 + nr].reshape(-1)[:math.prod(s)].reshape(s))
        r0 += nr
    return out


def _halves(a):
    return a.reshape((2, a.shape[0] // 2) + a.shape[1:])


def _cols_from_chips(g):
    return jnp.moveaxis(g, 0, -2).reshape(g.shape[1:-1] + (N_CHIPS * g.shape[-1],))


def kernel(x, meta, attn_norm_g, w_in, b_f, conv_w, conv_b, w_gate_a, b_gate_a, w_gate_x, b_gate_x, lru_L, attn_out_g, rec_out_g, w_out, mlp_norm_g, w_up, w_down, final_g, loss_target, m_meta, m_attn_norm_g, m_w_in, m_b_f, m_conv_w, m_conv_b, m_w_gate_a, m_b_gate_a, m_w_gate_x, m_b_gate_x, m_lru_L, m_attn_out_g, m_rec_out_g, m_w_out, m_mlp_norm_g, m_w_up, m_w_down, m_final_g, v_meta, v_attn_norm_g, v_w_in, v_b_f, v_conv_w, v_conv_b, v_w_gate_a, v_b_gate_a, v_w_gate_x, v_b_gate_x, v_lru_L, v_attn_out_g, v_rec_out_g, v_w_out, v_mlp_norm_g, v_w_up, v_w_down, v_final_g):
    w = dict(meta=meta, attn_norm_g=attn_norm_g, w_in=w_in, b_f=b_f, conv_w=conv_w, conv_b=conv_b,
             w_gate_a=w_gate_a, b_gate_a=b_gate_a, w_gate_x=w_gate_x, b_gate_x=b_gate_x, lru_L=lru_L,
             attn_out_g=attn_out_g, rec_out_g=rec_out_g, w_out=w_out, mlp_norm_g=mlp_norm_g, w_up=w_up,
             w_down=w_down, final_g=final_g)
    m = dict(meta=m_meta, attn_norm_g=m_attn_norm_g, w_in=m_w_in, b_f=m_b_f, conv_w=m_conv_w, conv_b=m_conv_b,
             w_gate_a=m_w_gate_a, b_gate_a=m_b_gate_a, w_gate_x=m_w_gate_x, b_gate_x=m_b_gate_x, lru_L=m_lru_L,
             attn_out_g=m_attn_out_g, rec_out_g=m_rec_out_g, w_out=m_w_out, mlp_norm_g=m_mlp_norm_g,
             w_up=m_w_up, w_down=m_w_down, final_g=m_final_g)
    v = dict(meta=v_meta, attn_norm_g=v_attn_norm_g, w_in=v_w_in, b_f=v_b_f, conv_w=v_conv_w, conv_b=v_conv_b,
             w_gate_a=v_w_gate_a, b_gate_a=v_b_gate_a, w_gate_x=v_w_gate_x, b_gate_x=v_b_gate_x, lru_L=v_lru_L,
             attn_out_g=v_attn_out_g, rec_out_g=v_rec_out_g, w_out=v_w_out, mlp_norm_g=v_mlp_norm_g,
             w_up=v_w_up, w_down=v_w_down, final_g=v_final_g)
    s_len, d = x.shape[1], x.shape[2]
    depth = w_in.shape[0]
    att_w = d // 2
    rec_w = d - att_w
    nh = att_w // HEAD_DIM
    chip = 2 * lax.axis_index("x") + lax.axis_index("y")

    g_conv, g_meta = [g.reshape((N_CHIPS, g.shape[1] * g.shape[2]) + g.shape[3:])
                      for g in _all_gather_chips([_halves(w["conv_w"]), _halves(w["meta"])])]
    p = dict(w)
    p["conv_w"] = _cols_from_chips(g_conv)
    meta_full = jnp.moveaxis(g_meta, 0, 1).reshape(N_META, d)

    chip1 = chip.reshape(1).astype(jnp.int32)
    w_in_t, m_in_t, v_in_t = [jnp.transpose(a["w_in"], (2, 0, 1)) for a in (w, m, v)]
    w_in_slabs = _cast_w_in_t_to_slabs(w_in_t, chip1)
    pushes, tokens = [], []
    for l in range(depth):
        slabs = [w_in_slabs[l]] + [_cast_to_slab(w[k], l, chip1) for k in BIG[1:]]
        send_sems, recv_sems, _, lands, token = _push_start(slabs, slabs, "gather_chips_half", f"weights_start_{l}")
        pushes.append((send_sems, recv_sems, lands))
        tokens.append(token[0, 0])
    passed = [{} for _ in range(depth)]

    def stage(l, after, ids):
        send_sems, recv_sems, lands = pushes[l]
        tag = "_".join(BIG[i] for i in ids)
        sub = [lands[i] for i in ids]
        sub = _push_wait(send_sems, recv_sems, ids, sub, sub, "gather_chips_half", after, f"{tag}_wait_{l}")
        send_sems, recv_sems, _, sub, token = _push_start(sub, sub, "pass_halves", f"{tag}_pass_{l}")
        for j, i in enumerate(ids):
            passed[l][i] = (send_sems, recv_sems, sub[j], j)
        return token[0, 0]

    t_len = N_META + s_len
    pad = -t_len % SEQ_TILE
    h = jnp.concatenate([meta_full, x[0], jnp.zeros((pad, d), F32)], axis=0)
    tgt = jnp.concatenate([jnp.zeros((N_META, d), F32), loss_target[0], jnp.zeros((pad, d), F32)], axis=0)
    z = _rms_fwd(h, _row(p["attn_norm_g"][0] + sum(tokens)))
    stage(0, z, [0])
    saved = []
    for l in range(depth):
        def fetch(k, after, l=l):
            send_sems, recv_sems, land, j = passed[l][BIG.index(k)]
            return _push_wait(send_sems, recv_sems, [j], [land], [land], "pass_halves", after,
                              f"{k}_here_{l}")[0]

        def stage_next(after, which, l=l):
            if which == "own":
                return stage(l, after, [1, 2, 3])
            return stage(l + 1, after, [0]) if l + 1 < depth else 0.0

        g_next = p["attn_norm_g"][l + 1] if l + 1 < depth else p["final_g"]
        h, z, sv = _forward_layer(l, h, z, p, fetch, stage_next, g_next)
        saved.append(sv)
    dh, dh_b, dg_final, loss_part = _loss_bwd(h, _row(p["final_g"]), tgt, s_len)

    small = {k: [None] * depth for k in SMALL if k not in ("meta", "final_g")}
    pushes = [None] * depth
    tok = 0.0
    for l in reversed(range(depth)):
        dh, dh_b, big_mlp, sm_mlp = _backward_mlp(l, dh, dh_b, saved[l], p, tok)
        g_out, = _grad_w_pieces([saved[l]["mix"]], dh_b)
        parts = [big_mlp["w_down"], big_mlp["w_up"], g_out.reshape((N_CHIPS, -1) + g_out.shape[1:])]
        push_mlp = _push_start(parts, [lax.empty(a.shape, a.dtype) for a in parts], "scatter_chips",
                               f"mlp_grads_start_{l}")
        dh, dh_b, big_mix, sm_mix = _backward_mixer(l, dh, dh_b, saved[l], p, push_mlp[4][0, 0])
        parts = [big_mix["w_in"]]
        push_mix = _push_start(parts, [lax.empty(a.shape, a.dtype) for a in parts], "scatter_chips",
                               f"mixer_grads_start_{l}")
        tok = push_mix[4][0, 0]
        pushes[l] = {("w_down", "w_up", "w_out"): push_mlp, ("w_in",): push_mix}
        for k, val in {**sm_mlp, **sm_mix}.items():
            small[k][l] = val
    grads = {k: jnp.stack(val) for k, val in small.items()}
    grads["final_g"] = dg_final[0]
    grads["meta"] = dh[:N_META]
    dx = dh[N_META:t_len]

    full_shapes = [grads[k].shape for k in SMALL] + [(1,)]
    packed = _pack([grads[k].astype(F32) for k in SMALL] + [loss_part[0, :1] + tok])
    dev1 = (2 * chip + lax.axis_index("c")).reshape(1).astype(jnp.int32)
    slabs = [_place_slab(packed, dev1, N_DEV)]
    small_push = _push_start(slabs, slabs, "gather_devices", "small_grads_start")

    last_token = small_push[4]
    outs = {k: [lax.empty(w[k].shape, F32) for _ in range(4)] for k in BIG[1:]}
    w_in_sums = [None] * depth
    swaps = {}

    def finish(l, wait_after, adam_after):
        send_sems, recv_sems, mine, lands, _ = swaps[l]
        mine, theirs = _push_wait(send_sems, recv_sems, list(range(len(BIG))), mine, lands, "sibling", wait_after,
                                  f"sums_wait_{l}")
        w_in_sums[l] = (mine[0], theirs[0])
        for k, a, b in zip(BIG[1:], mine[1:], theirs[1:]):
            outs[k] = _adamw_layer(w[k], m[k], v[k], l, a, b, outs[k], adam_after)

    wait_after = last_token
    for l in reversed(range(depth)):
        sums = {}
        for names, (send_sems, recv_sems, parts, lands, _) in pushes[l].items():
            parts, landed = _push_wait(send_sems, recv_sems, list(range(len(names))), parts, lands, "scatter_chips",
                                       wait_after, f"{names[0]}_grads_wait_{l}")
            for k, part, land in zip(names, parts, landed):
                sums[k] = wait_after = _sum_partials(part, land, chip1)
        mine = [sums[k] for k in BIG]
        swaps[l] = _push_start(mine, [lax.empty(a.shape, a.dtype) for a in mine], "sibling", f"sums_start_{l}")
        if l + 2 < depth:
            finish(l + 2, outs["w_down"][0] if l + 3 < depth else mine[0], swaps[l][4])
    for l in reversed(range(min(2, depth))):
        finish(l, outs["w_down"][0] if depth > 2 else swaps[0][4], swaps[0][4])
    outs["w_in"] = [jnp.transpose(r, (1, 2, 0)) for r in _adamw_w_in_t(
        w_in_t, m_in_t, v_in_t, [s[0] for s in w_in_sums], [s[1] for s in w_in_sums])]
    out_g, out_d, out_m, out_v = [{k: outs[k][i] for k in BIG} for i in range(4)]

    landed = _push_wait(small_push[0], small_push[1], [0], small_push[3], small_push[3], "gather_devices",
                        out_g["w_in"], "small_grads_wait")
    total = _sum_slabs(landed[0])
    small_g = dict(zip(SMALL + ("loss",), _unpack(total, full_shapes)))
    for k in COL_SHARDED_SMALL:
        n = w[k].shape[-1]
        small_g[k] = lax.dynamic_slice_in_dim(small_g[k], chip * n, n, axis=small_g[k].ndim - 1)
    local_shapes = [w[k].shape for k in SMALL]
    res = _adamw(_pack([w[k] for k in SMALL]), _pack([small_g[k] for k in SMALL]),
                 _pack([m[k] for k in SMALL]), _pack([v[k] for k in SMALL]))
    out_g.update({k: small_g[k] for k in SMALL})
    for dst, buf in zip((out_d, out_m, out_v), res):
        dst.update(zip(SMALL, _unpack(buf, local_shapes)))

    return (small_g["loss"].reshape(()), dx[None],
            *[out_g[k] for k in WEIGHTS], *[out_d[k] for k in WEIGHTS],
            *[out_m[k] for k in WEIGHTS], *[out_v[k] for k in WEIGHTS])
```

```python
import functools
import math

import jax
import jax.numpy as jnp
from jax import lax
from jax.experimental import pallas as pl
from jax.experimental.pallas import tpu as pltpu

F32 = jnp.float32
BF16 = jnp.bfloat16

N_META = 16
HEAD_DIM = 64
N_REC_BLOCKS = 8
CONV_WIDTH = 4
RG_C = 8.0
NORM_EPS = 1e-6
ADAM_LR = 0.001
ADAM_B1 = 0.9
ADAM_B2 = 0.999
ADAM_EPS = 1e-08
ADAM_WD = 0.01
ADAM_STEP = 10

LANES = 128
SUBLANES = 8
SEQ_TILE = 128
VMEM_CAP = 60 * 2**20
VMEM_SLACK = 6 * 2**20
NEG_BIG = -1e30
N_CHIPS = 4
N_DEV = 8
MESH = pl.DeviceIdType.MESH


def _nbytes(shape, dtype):
    return math.prod(shape) * jnp.dtype(dtype).itemsize


def _call(body, args, *, name, out_shape, grid=(), in_specs=None, out_specs=None, scratch_shapes=(),
          grid_spec=None, semantics=None, vmem_bytes=None, side_effects=None, hbm_results=True, **kw):
    cp = {}
    if semantics is not None:
        cp["dimension_semantics"] = semantics
    if vmem_bytes is not None:
        cp["vmem_limit_bytes"] = int(min(VMEM_CAP, vmem_bytes + VMEM_SLACK))
    if side_effects is not None:
        cp["has_side_effects"] = side_effects
    if grid_spec is not None:
        kw["grid_spec"] = grid_spec
    else:
        kw.update(grid=grid, in_specs=in_specs, out_specs=out_specs, scratch_shapes=scratch_shapes)
    if hbm_results:
        out_shape = jax.tree.map(
            lambda s: pltpu.HBM(s.shape, s.dtype) if isinstance(s, jax.ShapeDtypeStruct) else s, out_shape)
    fn = pl.pallas_call(
        body, name=name, out_shape=out_shape,
        compiler_params=pltpu.CompilerParams(**cp), **kw)
    return fn(*[_in_hbm(a) if jnp.issubdtype(getattr(a, "dtype", jnp.int32), jnp.floating) else a for a in args])


def _divisor_tile(n, unit, target):
    best = None
    for t in range(unit, min(n, target) + 1, unit):
        if n % t == 0:
            best = t
    return n if best is None else best


def _sigmoid(x):
    return 1.0 / (1.0 + jnp.exp(-x))


def _log1p_unit(e):
    series = e * (1.0 - e * (0.5 - e * (1.0 / 3.0)))
    return jnp.where(e < 1e-2, series, jnp.log(1.0 + e))


def _log_sigmoid(x):
    return jnp.minimum(x, 0.0) - _log1p_unit(jnp.exp(-jnp.abs(x)))


def _one_minus_exp(x, exp_x):
    small = -x * (1.0 + x * (1.0 / 2 + x * (1.0 / 6 + x * (1.0 / 24 + x * (1.0 / 120 + x * (1.0 / 720))))))
    return jnp.where(x > -0.25, small, 1.0 - exp_x)


_GELU_K = math.sqrt(2.0 / math.pi)
_GELU_C = 0.044715


def _gelu_and_grad(y):
    th = jnp.tanh(_GELU_K * (y + _GELU_C * y * y * y))
    g = 0.5 * y * (1.0 + th)
    dg = 0.5 * (1.0 + th) + 0.5 * y * (1.0 - th * th) * _GELU_K * (1.0 + 3.0 * _GELU_C * y * y)
    return g, dg


def _rstd(x):
    return lax.rsqrt(jnp.mean(x * x, axis=-1, keepdims=True) + NORM_EPS)


def _rms_bwd(dz, x, g):
    rs = _rstd(x)
    xh = x * rs
    dgp = jnp.sum(dz * xh, axis=0, keepdims=True)
    dxh = dz * g
    dx = rs * (dxh - xh * jnp.mean(dxh * xh, axis=-1, keepdims=True))
    return dx, dgp


def _dot(a, b):
    return jnp.dot(a, b, preferred_element_type=F32)


def _dot_nt(a, b):
    return lax.dot_general(a, b, (((1,), (1,)), ((), ())), preferred_element_type=F32)


def _dot_tn(a, b):
    return lax.dot_general(a, b, (((0,), (0,)), ((), ())), preferred_element_type=F32)


def _full(shape):
    nd = len(shape)
    return pl.BlockSpec(shape, lambda *_: (0,) * nd)


def _rms_fwd(h, g):
    tp, d = h.shape
    tm = _divisor_tile(tp, 16, 544)

    def body(h_ref, g_ref, z_ref):
        x = h_ref[...]
        z_ref[...] = (x * _rstd(x) * g_ref[...]).astype(BF16)

    return _call(body, (h, g), name="rms_fwd", grid=(tp // tm,),
                 in_specs=[pl.BlockSpec((tm, d), lambda i: (i, 0)), _full((1, d))],
                 out_specs=pl.BlockSpec((tm, d), lambda i: (i, 0)),
                 out_shape=jax.ShapeDtypeStruct((tp, d), BF16), semantics=("parallel",))


def _proj(z, w_big_t, att_w):
    tp, d = z.shape
    nb = w_big_t.shape[0]
    tn = _divisor_tile(nb, LANES, 512)
    assert (3 * att_w) % tn == 0
    n_qkv = 3 * att_w // tn
    scale = 1.0 / math.sqrt(HEAD_DIM)

    def body(z_ref, w_ref, qkv_ref, p_ref):
        j = pl.program_id(0)
        acc = _dot_nt(z_ref[...], w_ref[...])

        @pl.when(j < n_qkv)
        def _():
            col = j * tn + lax.broadcasted_iota(jnp.int32, (1, tn), 1)
            qkv_ref[...] = (acc * jnp.where(col < att_w, scale, 1.0)).astype(BF16)

        @pl.when(j >= n_qkv)
        def _():
            p_ref[...] = acc

    vm = 2 * (_nbytes((tp, d), BF16) + _nbytes((d, tn), BF16) + _nbytes((tp, tn), F32) * 2)
    return _call(body, (z, w_big_t), name="proj", grid=(nb // tn,),
                 in_specs=[_full((tp, d)), pl.BlockSpec((tn, d), lambda j: (j, 0))],
                 out_specs=[pl.BlockSpec((tp, tn), lambda j: (0, jnp.minimum(j, n_qkv - 1))),
                            pl.BlockSpec((tp, tn), lambda j: (0, jnp.maximum(j - n_qkv, 0)))],
                 out_shape=[jax.ShapeDtypeStruct((tp, 3 * att_w), BF16),
                            jax.ShapeDtypeStruct((tp, nb - 3 * att_w), F32)],
                 semantics=("arbitrary",), vmem_bytes=vm)


def _tile_cumsum(x, row, reverse=False):
    for s in (1, 2, 4):
        if reverse:
            x = x + jnp.where(row < SUBLANES - s, pltpu.roll(x, SUBLANES - s, 0), 0.0)
        else:
            x = x + jnp.where(row >= s, pltpu.roll(x, s, 0), 0.0)
    return x


def _fgate_fwd(proj, b_f_pad, nh):
    tp, nb = proj.shape
    fblk = nb // LANES - 1

    def body(f_ref, b_ref, c_ref, ct_ref):
        b = b_ref[...]
        row = lax.broadcasted_iota(jnp.int32, (SUBLANES, LANES), 0)

        def step(i, carry):
            r0 = pl.multiple_of(i * SUBLANES, SUBLANES)
            lf = _log_sigmoid(f_ref[pl.ds(r0, SUBLANES), :] + b)
            x = _tile_cumsum(lf, row) + carry
            c_ref[pl.ds(r0, SUBLANES), :] = x
            return x[SUBLANES - 1:SUBLANES, :]

        lax.fori_loop(0, tp // SUBLANES, step, jnp.zeros((1, LANES), F32))
        ct_ref[...] = c_ref[...].T[:nh, :]

    return _call(body, (proj, b_f_pad), name="fgate_fwd", grid=(1,),
                 in_specs=[pl.BlockSpec((tp, LANES), lambda i: (0, fblk)), _full((1, LANES))],
                 out_specs=[_full((tp, LANES)), _full((nh, tp))],
                 out_shape=[jax.ShapeDtypeStruct((tp, LANES), F32), jax.ShapeDtypeStruct((nh, tp), F32)],
                 semantics=("arbitrary",))


def _fgate_bwd(proj, b_f_pad, dc):
    tp, nb = proj.shape
    fblk = nb // LANES - 1

    def body(f_ref, b_ref, dc_ref, df_ref, db_ref, dc_s):
        b = b_ref[...]
        row = lax.broadcasted_iota(jnp.int32, (SUBLANES, LANES), 0)
        nt = tp // SUBLANES

        def step(i, carry):
            suffix, acc = carry
            r0 = pl.multiple_of((nt - 1 - i) * SUBLANES, SUBLANES)
            dlf = _tile_cumsum(dc_ref[pl.ds(r0, SUBLANES), :], row, reverse=True) + suffix
            df = dlf * _sigmoid(-(f_ref[pl.ds(r0, SUBLANES), :] + b))
            dc_s[pl.ds(r0, SUBLANES), :] = df
            return dlf[0:1, :], acc + df

        _, acc = lax.fori_loop(0, nt, step, (jnp.zeros((1, LANES), F32), jnp.zeros((SUBLANES, LANES), F32)))
        df_ref[...] = dc_s[...].astype(BF16)
        db_ref[...] = jnp.broadcast_to(jnp.sum(acc, axis=0, keepdims=True), (SUBLANES, LANES))

    return _call(body, (proj, b_f_pad, dc), name="fgate_bwd", grid=(1,),
                 in_specs=[pl.BlockSpec((tp, LANES), lambda i: (0, fblk)), _full((1, LANES)), _full((tp, LANES))],
                 out_specs=[_full((tp, LANES)), _full((SUBLANES, LANES))],
                 out_shape=[jax.ShapeDtypeStruct((tp, LANES), BF16),
                            jax.ShapeDtypeStruct((SUBLANES, LANES), F32)],
                 scratch_shapes=[pltpu.VMEM((tp, LANES), F32)], semantics=("arbitrary",))


ATT_BQ = 128


ATT_BUCKET = 2
ATT_HEADS = 4


def _for_bucket(i, nq, fn):
    for lo in range(0, nq, ATT_BUCKET):
        hi = min(lo + ATT_BUCKET, nq)
        spans = ([(0, lo * ATT_BQ, False)] if lo else []) + [(lo * ATT_BQ, hi * ATT_BQ, True)]
        pl.when(jnp.logical_and(i >= lo, i < hi))(functools.partial(fn, spans))


def _head_column(c_blk, h):
    lane = lax.broadcasted_iota(jnp.int32, c_blk.shape, 1)
    return jnp.sum(jnp.where(lane == h, c_blk, 0.0), axis=1, keepdims=True)


def _head_columns_into(p, c_ref, ck_s):
    for hh in range(ATT_HEADS):
        ck_s[hh] = jnp.broadcast_to(_head_column(c_ref[...], ATT_HEADS * p + hh), ck_s.shape[1:])


def _pair_diag_cols(x2):
    top = lax.broadcasted_iota(jnp.int32, (LANES, ATT_BQ), 0) < HEAD_DIM
    xt = x2.astype(F32).T.astype(BF16)
    return jnp.concatenate([jnp.where(top, xt, 0), jnp.where(top, 0, xt)], axis=1)


def _pair_diag_rows(x2):
    low = lax.broadcasted_iota(jnp.int32, (ATT_BQ, LANES), 1) < HEAD_DIM
    return jnp.concatenate([jnp.where(low, x2, 0), jnp.where(low, 0, x2)], axis=0)


def _seen_keys(k0, k1, q0):
    keys = k0 + lax.broadcasted_iota(jnp.int32, (k1 - k0, ATT_BQ), 0)
    return keys <= q0 + lax.broadcasted_iota(jnp.int32, (k1 - k0, ATT_BQ), 1)


def _attn_fwd(qkv, c, c_t, nh):
    tp = qkv.shape[0]
    att_w = nh * HEAD_DIM
    ng = nh // ATT_HEADS
    gw = ATT_HEADS * HEAD_DIM
    bq = ATT_BQ
    nq = tp // bq
    pair = 2 * HEAD_DIM
    assert pair == LANES and ATT_HEADS % 2 == 0

    def body(q_ref, k_ref, v_ref, c_ref, ct_ref, o_ref, lse_ref, ck_s, vt_s):
        p = pl.program_id(0)
        i = pl.program_id(1)

        @pl.when(i == 0)
        def _():
            _head_columns_into(p, c_ref, ck_s)
            vt_s[...] = v_ref[...].astype(F32).T.astype(BF16)

        def compute(spans):
            q0 = pl.multiple_of(i * bq, bq)
            o_t, lses = [], []
            for pi in range(ATT_HEADS // 2):
                lo = pair * pi
                heads = (2 * pi, 2 * pi + 1)
                q_cols = _pair_diag_cols(q_ref[:, lo:lo + pair])
                ts = []
                for k0, k1, needs_mask in spans:
                    t2 = _dot(k_ref[k0:k1, lo:lo + pair], q_cols)
                    t_e = [t2[:, e * bq:(e + 1) * bq] - ck_s[hh, k0:k1, :] for e, hh in enumerate(heads)]
                    if needs_mask:
                        seen = _seen_keys(k0, k1, q0)
                        t_e = [jnp.where(seen, t, NEG_BIG) for t in t_e]
                    ts.append(t_e)
                ms = [functools.reduce(jnp.maximum, [jnp.max(t[e], axis=0, keepdims=True) for t in ts])
                      for e in range(2)]
                es = [[jnp.exp(t[e] - ms[e]) for e in range(2)] for t in ts]
                ls = [sum(jnp.sum(e_[e], axis=0, keepdims=True) for e_ in es) for e in range(2)]
                o2 = sum(_dot(vt_s[lo:lo + pair, k0:k1],
                              jnp.concatenate([e_[0].astype(BF16), e_[1].astype(BF16)], axis=1))
                         for e_, (k0, k1, _) in zip(es, spans))
                o_t += [o2[:HEAD_DIM, :bq] / ls[0], o2[HEAD_DIM:, bq:] / ls[1]]
                lses += [ms[e] + ct_ref[pl.ds(ATT_HEADS * p + hh, 1), :] + jnp.log(ls[e])
                         for e, hh in enumerate(heads)]
            o_ref[...] = jnp.concatenate(o_t, axis=0).T
            lse_ref[...] = jnp.concatenate(lses, axis=0)

        _for_bucket(i, nq, compute)

    blk = pl.BlockSpec((bq, gw), lambda p, i: (i, p))
    vm = 6 * _nbytes((tp, gw), BF16) + 2 * ATT_HEADS * _nbytes((tp, LANES), F32) + 2 * _nbytes((tp, LANES), F32) \
        + 8 * ATT_HEADS * _nbytes((bq, tp), F32)
    return _call(body, (qkv, qkv, qkv, c, c_t), name="attn_fwd", grid=(ng, nq),
                 in_specs=[blk,
                           pl.BlockSpec((tp, gw), lambda p, i: (0, ng + p)),
                           pl.BlockSpec((tp, gw), lambda p, i: (0, 2 * ng + p)),
                           _full((tp, LANES)), pl.BlockSpec((nh, bq), lambda p, i: (0, i))],
                 out_specs=[blk, pl.BlockSpec((None, ATT_HEADS, bq), lambda p, i: (p, 0, i))],
                 out_shape=[jax.ShapeDtypeStruct((tp, att_w), F32), jax.ShapeDtypeStruct((ng, ATT_HEADS, tp), F32)],
                 scratch_shapes=[pltpu.VMEM((ATT_HEADS, tp, LANES), F32), pltpu.VMEM((gw, tp), BF16)],
                 semantics=("arbitrary", "arbitrary"), vmem_bytes=vm)


def _attn_bwd(qkv, c, c_t, lse, do, nh):
    tp = qkv.shape[0]
    att_w = nh * HEAD_DIM
    ng = nh // ATT_HEADS
    gw = ATT_HEADS * HEAD_DIM
    bq = ATT_BQ
    nq = tp // bq
    pair = 2 * HEAD_DIM
    assert pair == LANES and ATT_HEADS % 2 == 0
    scale = 1.0 / math.sqrt(HEAD_DIM)

    def body(q_ref, k_ref, v_ref, c_ref, ct_ref, lse_ref, do_ref, dq_ref, dk_ref, dv_ref, dc_ref,
             dk_s, dv_s, dc_s, ck_s, kt_s):
        p = pl.program_id(0)
        i = pl.program_id(1)

        @pl.when(i == 0)
        def _():
            dk_s[...] = jnp.zeros_like(dk_s)
            dv_s[...] = jnp.zeros_like(dv_s)
            dc_s[...] = jnp.zeros_like(dc_s)
            kt_s[...] = k_ref[...].astype(F32).T.astype(BF16)
            _head_columns_into(p, c_ref, ck_s)

        @pl.when(jnp.logical_and(i == 0, p == 0))
        def _():
            dc_ref[...] = jnp.zeros_like(dc_ref)

        def compute(spans):
            q0 = pl.multiple_of(i * bq, bq)
            dq_t = []
            for pi in range(ATT_HEADS // 2):
                lo = pair * pi
                q2 = q_ref[:, lo:lo + pair]
                do2 = do_ref[:, lo:lo + pair].astype(BF16)
                q_cols, do_cols = _pair_diag_cols(q2), _pair_diag_cols(do2)
                q_rows, do_rows = _pair_diag_rows(q2), _pair_diag_rows(do2)
                heads = (2 * pi, 2 * pi + 1)
                col_terms = [ct_ref[pl.ds(ATT_HEADS * p + hh, 1), :] - lse_ref[hh:hh + 1, :] for hh in heads]
                prs, dps = [], []
                for k0, k1, needs_mask in spans:
                    t2 = _dot(k_ref[k0:k1, lo:lo + pair], q_cols)
                    dp2 = _dot(v_ref[k0:k1, lo:lo + pair], do_cols)
                    if needs_mask:
                        seen = _seen_keys(k0, k1, q0)
                    pr_e, dp_e = [], []
                    for e, hh in enumerate(heads):
                        t = t2[:, e * bq:(e + 1) * bq] - ck_s[hh, k0:k1, :]
                        if needs_mask:
                            t = jnp.where(seen, t, NEG_BIG)
                        pr_e.append(jnp.exp(t + col_terms[e]))
                        dp_e.append(dp2[:, e * bq:(e + 1) * bq])
                    prs.append(pr_e)
                    dps.append(dp_e)
                key_sums = [sum(jnp.sum(pr[e] * dp[e], axis=0, keepdims=True) for pr, dp in zip(prs, dps))
                            for e in range(2)]
                dq2 = 0.0
                for (k0, k1, _), pr, dp in zip(spans, prs, dps):
                    ds = [pr[e] * (dp[e] - key_sums[e]) for e in range(2)]
                    for e, hh in enumerate(heads):
                        dc_s[hh, k0:k1, :] += jnp.sum(ds[e], axis=1, keepdims=True)
                    ds2 = jnp.concatenate([ds[0].astype(BF16), ds[1].astype(BF16)], axis=1)
                    pr2 = jnp.concatenate([pr[0].astype(BF16), pr[1].astype(BF16)], axis=1)
                    dk_s[k0:k1, lo:lo + pair] += _dot(ds2, q_rows)
                    dv_s[k0:k1, lo:lo + pair] += _dot(pr2, do_rows)
                    dq2 = dq2 + _dot(kt_s[lo:lo + pair, k0:k1], ds2)
                dq_t.append(jnp.concatenate([dq2[:HEAD_DIM, :bq], dq2[HEAD_DIM:, bq:]], axis=0))
            dq_ref[...] = (jnp.concatenate(dq_t, axis=0) * scale).T.astype(BF16)

        _for_bucket(i, nq, compute)

        @pl.when(i == nq - 1)
        def _():
            dk_ref[...] = dk_s[...].astype(BF16)
            dv_ref[...] = dv_s[...].astype(BF16)
            lane = lax.broadcasted_iota(jnp.int32, (tp, LANES), 1)
            dc = dc_ref[...]
            for hh in range(ATT_HEADS):
                dc = jnp.where(lane == ATT_HEADS * p + hh, -dc_s[hh], dc)
            dc_ref[...] = dc

    blk = pl.BlockSpec((bq, gw), lambda p, i: (i, p))
    col = pl.BlockSpec((tp, gw), lambda p, i: (0, p))
    vm = 7 * _nbytes((tp, gw), BF16) + 2 * _nbytes((tp, gw), F32) + 2 * ATT_HEADS * _nbytes((tp, LANES), F32) \
        + 2 * _nbytes((tp, LANES), F32) + 12 * ATT_HEADS * _nbytes((bq, tp), F32)
    return _call(body, (qkv, qkv, qkv, c, c_t, lse, do), name="attn_bwd", grid=(ng, nq),
                 in_specs=[blk,
                           pl.BlockSpec((tp, gw), lambda p, i: (0, ng + p)),
                           pl.BlockSpec((tp, gw), lambda p, i: (0, 2 * ng + p)),
                           _full((tp, LANES)), pl.BlockSpec((nh, bq), lambda p, i: (0, i)),
                           pl.BlockSpec((None, ATT_HEADS, bq), lambda p, i: (p, 0, i)), blk],
                 out_specs=[blk, col, col, _full((tp, LANES))],
                 out_shape=[jax.ShapeDtypeStruct((tp, att_w), BF16)] * 3 + [jax.ShapeDtypeStruct((tp, LANES), F32)],
                 scratch_shapes=[pltpu.VMEM((tp, gw), F32), pltpu.VMEM((tp, gw), F32),
                                 pltpu.VMEM((ATT_HEADS, tp, 1), F32), pltpu.VMEM((ATT_HEADS, tp, LANES), F32),
                                 pltpu.VMEM((gw, tp), BF16)],
                 semantics=("arbitrary", "arbitrary"), vmem_bytes=vm)


REC_ROWS = 128
HALO = SUBLANES


def _conv_taps(cat):
    taps = []
    for k in range(CONV_WIDTH):
        sh = CONV_WIDTH - 1 - k
        taps.append((pltpu.roll(cat, sh, 0) if sh else cat)[HALO:])
    return taps


def _rec_gates(xc, wa_ref, ba_ref, wx_ref, bx_ref, l_ref):
    xcb = xc.astype(BF16)
    r = _sigmoid(_dot(xcb, wa_ref[...]) + ba_ref[...])
    ig = _sigmoid(_dot(xcb, wx_ref[...]) + bx_ref[...])
    ls = _log_sigmoid(l_ref[...])
    log_a = RG_C * r * ls
    return xcb, r, ig, ls, log_a


def _rec_fwd(proj, xr_blk, yr_blk, rec_w, conv_w, conv_b, wa, ba, wx, bx, lru):
    tp = proj.shape[0]
    w = rec_w
    r_rows = REC_ROWS
    nc = tp // r_rows

    def body(xr_ref, yr_ref, cw_ref, cb_ref, wa_ref, ba_ref, wx_ref, bx_ref, l_ref,
             hr_ref, rec_ref, prev_s, carry_s, a_s, u_s):
        i = pl.program_id(0)

        @pl.when(i == 0)
        def _():
            prev_s[...] = jnp.zeros_like(prev_s)
            carry_s[...] = jnp.zeros_like(carry_s)

        x = xr_ref[...]
        taps = _conv_taps(jnp.concatenate([prev_s[...], x], axis=0))
        prev_s[...] = x[r_rows - HALO:]
        xc = cb_ref[...]
        for k in range(CONV_WIDTH):
            xc = xc + cw_ref[k:k + 1, :] * taps[k]
        _, r, ig, ls, log_a = _rec_gates(xc, wa_ref, ba_ref, wx_ref, bx_ref, l_ref)
        a = jnp.exp(log_a)
        a_s[...] = a
        u_s[...] = jnp.sqrt(_one_minus_exp(2.0 * log_a, a * a)) * ig * xc

        def tile(j, h):
            r0 = pl.multiple_of(j * SUBLANES, SUBLANES)
            at = a_s[pl.ds(r0, SUBLANES), :]
            ut = u_s[pl.ds(r0, SUBLANES), :]
            out = []
            for rr in range(SUBLANES):
                h = at[rr:rr + 1] * h + ut[rr:rr + 1]
                out.append(h)
            hr_ref[pl.ds(r0, SUBLANES), :] = jnp.concatenate(out, axis=0)
            return h

        carry_s[0:1, :] = lax.fori_loop(0, r_rows // SUBLANES, tile, carry_s[0:1, :])
        g, _ = _gelu_and_grad(yr_ref[...])
        rec_ref[...] = hr_ref[...] * g

    blk = pl.BlockSpec((r_rows, w), lambda i: (i, 0))
    vm = 16 * _nbytes((r_rows, w), F32) + 4 * _nbytes((w, w), BF16)
    return _call(body, (proj, proj, conv_w, conv_b, wa, ba, wx, bx, lru), name="rec_fwd", grid=(nc,),
                 in_specs=[pl.BlockSpec((r_rows, w), lambda i: (i, xr_blk)),
                           pl.BlockSpec((r_rows, w), lambda i: (i, yr_blk)),
                           _full((CONV_WIDTH, w)), _full((1, w)), _full((w, w)), _full((1, w)),
                           _full((w, w)), _full((1, w)), _full((1, w))],
                 out_specs=[blk, blk],
                 out_shape=[jax.ShapeDtypeStruct((tp, w), F32)] * 2,
                 scratch_shapes=[pltpu.VMEM((HALO, w), F32), pltpu.VMEM((SUBLANES, w), F32),
                                 pltpu.VMEM((r_rows, w), F32), pltpu.VMEM((r_rows, w), F32)],
                 semantics=("arbitrary",), vmem_bytes=vm)


def _rec_bwd(proj, xr_blk, yr_blk, rec_w, hr, drec, conv_w, conv_b, wa, ba, wx, bx, lru):
    tp = proj.shape[0]
    w = rec_w
    r_rows = REC_ROWS
    nc = tp // r_rows
    hpc = r_rows // HALO

    def body(xr_ref, xh_ref, yr_ref, hr_ref, hh_ref, drec_ref, cw_ref, cb_ref, wa_ref, ba_ref, wx_ref, bx_ref,
             l_ref, dxr_ref, dyr_ref, dwa_ref, dwx_ref, small_ref, lam_s, a_s, dhr_s, carry_s, next_s):
        i = pl.program_id(0)
        first = (nc - 1 - i) == 0

        @pl.when(i == 0)
        def _():
            carry_s[...] = jnp.zeros_like(carry_s)
            next_s[...] = jnp.zeros_like(next_s)
            dwa_ref[...] = jnp.zeros_like(dwa_ref)
            dwx_ref[...] = jnp.zeros_like(dwx_ref)
            small_ref[...] = jnp.zeros_like(small_ref)

        x = xr_ref[...]
        xprev = jnp.where(first, 0.0, xh_ref[...])
        taps = _conv_taps(jnp.concatenate([xprev, x], axis=0))
        xc = cb_ref[...]
        for k in range(CONV_WIDTH):
            xc = xc + cw_ref[k:k + 1, :] * taps[k]
        xcb, r, ig, ls, log_a = _rec_gates(xc, wa_ref, ba_ref, wx_ref, bx_ref, l_ref)
        a = jnp.exp(log_a)
        a2 = a * a
        mult = jnp.sqrt(_one_minus_exp(2.0 * log_a, a2))
        g, dg = _gelu_and_grad(yr_ref[...])
        hr_v = hr_ref[...]
        drec_v = drec_ref[...]
        dhr_s[...] = drec_v * g
        dyr_ref[...] = (drec_v * hr_v * dg).astype(BF16)
        a_s[...] = a

        def tile(jj, carry):
            r0 = pl.multiple_of((r_rows // SUBLANES - 1 - jj) * SUBLANES, SUBLANES)
            at = a_s[pl.ds(r0, SUBLANES), :]
            dt = dhr_s[pl.ds(r0, SUBLANES), :]
            out = [None] * SUBLANES
            for rr in range(SUBLANES - 1, -1, -1):
                lam = dt[rr:rr + 1] + carry
                out[rr] = lam
                carry = at[rr:rr + 1] * lam
            lam_s[pl.ds(r0, SUBLANES), :] = jnp.concatenate(out, axis=0)
            return carry

        carry_s[0:1, :] = lax.fori_loop(0, r_rows // SUBLANES, tile, carry_s[0:1, :])
        lam = lam_s[...]
        hprev = jnp.where(first, 0.0, hh_ref[...])
        hr_prev = pltpu.roll(jnp.concatenate([hprev, hr_v], axis=0), 1, 0)[HALO:]
        da = lam * hr_prev
        dxc = lam * mult * ig
        di = lam * mult * xc
        dmult = lam * ig * xc
        dlog_a = da * a - dmult * a2 / mult
        dr = dlog_a * (RG_C * ls)
        dls = jnp.sum(dlog_a * (RG_C * r), axis=0, keepdims=True)
        dga = dr * r * (1.0 - r)
        dgx = di * ig * (1.0 - ig)
        dgab = dga.astype(BF16)
        dgxb = dgx.astype(BF16)
        dxc = dxc + _dot_nt(dgab, wa_ref[...]) + _dot_nt(dgxb, wx_ref[...])
        dwa_ref[...] += _dot_tn(xcb, dgab)
        dwx_ref[...] += _dot_tn(xcb, dgxb)
        cat = jnp.concatenate([dxc, next_s[...]], axis=0)
        next_s[...] = dxc[0:HALO]
        dxr = cw_ref[CONV_WIDTH - 1:CONV_WIDTH, :] * dxc
        for k in range(CONV_WIDTH - 1):
            sh = CONV_WIDTH - 1 - k
            dxr = dxr + cw_ref[k:k + 1, :] * pltpu.roll(cat, r_rows + HALO - sh, 0)[:r_rows]
        dxr_ref[...] = dxr.astype(BF16)
        rows = [jnp.sum(dxc * taps[k], axis=0, keepdims=True) for k in range(CONV_WIDTH)]
        rows += [jnp.sum(dxc, axis=0, keepdims=True), jnp.sum(dga, axis=0, keepdims=True),
                 jnp.sum(dgx, axis=0, keepdims=True), dls * _sigmoid(-l_ref[...])]
        small_ref[...] += jnp.concatenate(rows, axis=0)

    def rev(i):
        return nc - 1 - i

    def halo(i):
        return jnp.maximum(rev(i) * hpc - 1, 0)

    blk = pl.BlockSpec((r_rows, w), lambda i: (rev(i), 0))
    vm = 40 * _nbytes((r_rows, w), F32) + 6 * _nbytes((w, w), F32)
    return _call(body, (proj, proj, proj, hr, hr, drec, conv_w, conv_b, wa, ba, wx, bx, lru),
                 name="rec_bwd", grid=(nc,),
                 in_specs=[pl.BlockSpec((r_rows, w), lambda i: (rev(i), xr_blk)),
                           pl.BlockSpec((HALO, w), lambda i: (halo(i), xr_blk)),
                           pl.BlockSpec((r_rows, w), lambda i: (rev(i), yr_blk)),
                           blk,
                           pl.BlockSpec((HALO, w), lambda i: (halo(i), 0)),
                           blk,
                           _full((CONV_WIDTH, w)), _full((1, w)), _full((w, w)), _full((1, w)),
                           _full((w, w)), _full((1, w)), _full((1, w))],
                 out_specs=[blk, blk, _full((w, w)), _full((w, w)), _full((SUBLANES, w))],
                 out_shape=[jax.ShapeDtypeStruct((tp, w), BF16)] * 2
                 + [jax.ShapeDtypeStruct((w, w), F32)] * 2 + [jax.ShapeDtypeStruct((SUBLANES, w), F32)],
                 scratch_shapes=[pltpu.VMEM((r_rows, w), F32)] * 3
                 + [pltpu.VMEM((SUBLANES, w), F32), pltpu.VMEM((HALO, w), F32)],
                 semantics=("arbitrary",), vmem_bytes=vm)


ROW_TARGET = 544


def _mixer_out(attn, rec, g_a, g_r, w_out, h, g_next):
    tp, d = h.shape
    aw, rw = attn.shape[1], rec.shape[1]
    kc = d // N_CHIPS
    tm = _divisor_tile(tp, 16, ROW_TARGET)

    def body(a_ref, r_ref, ga_ref, gr_ref, w_ref, h_ref, gn_ref, h1_ref, z_ref, mix_ref):
        a = a_ref[...]
        r = r_ref[...]
        mix = jnp.concatenate([a * _rstd(a) * ga_ref[...], r * _rstd(r) * gr_ref[...]], axis=1).astype(BF16)
        mix_ref[...] = mix
        h1 = h_ref[...]
        for j in range(N_CHIPS):
            h1 = h1 + _dot(mix[:, j * kc:(j + 1) * kc], w_ref[j])
        h1_ref[...] = h1
        z_ref[...] = (h1 * _rstd(h1) * gn_ref[...]).astype(BF16)

    row = lambda wd: pl.BlockSpec((tm, wd), lambda i: (i, 0))
    vm = 2 * _nbytes((d, d), BF16) + 12 * _nbytes((tm, d), F32)
    return _call(body, (attn, rec, g_a, g_r, w_out, h, g_next), name="mixer_out", grid=(tp // tm,),
                 in_specs=[row(aw), row(rw), _full((1, aw)), _full((1, rw)), _full(w_out.shape), row(d),
                           _full((1, d))],
                 out_specs=[row(d), row(d), row(d)],
                 out_shape=[jax.ShapeDtypeStruct((tp, d), F32), jax.ShapeDtypeStruct((tp, d), BF16),
                            jax.ShapeDtypeStruct((tp, d), BF16)],
                 semantics=("parallel",), vmem_bytes=vm)


def _mixer_bwd(dh_b, w_out, attn, rec, g_a, g_r):
    tp, d = dh_b.shape
    aw, rw = attn.shape[1], rec.shape[1]
    tm = _divisor_tile(tp, 16, ROW_TARGET)

    def body(dh_ref, w_ref, a_ref, r_ref, ga_ref, gr_ref, da_ref, dr_ref, dg_ref):
        @pl.when(pl.program_id(0) == 0)
        def _():
            dg_ref[...] = jnp.zeros_like(dg_ref)

        dh = dh_ref[...]
        dmix = jnp.concatenate([_dot_nt(dh, w_ref[j]) for j in range(N_CHIPS)], axis=1)
        da, dga = _rms_bwd(dmix[:, :aw], a_ref[...], ga_ref[...])
        dr, dgr = _rms_bwd(dmix[:, aw:], r_ref[...], gr_ref[...])
        da_ref[...] = da
        dr_ref[...] = dr
        dg_ref[...] += jnp.broadcast_to(jnp.concatenate([dga, dgr], axis=1), (SUBLANES, d))

    row = lambda wd: pl.BlockSpec((tm, wd), lambda i: (i, 0))
    vm = 2 * _nbytes((d, d), BF16) + 12 * _nbytes((tm, d), F32)
    return _call(body, (dh_b, w_out, attn, rec, g_a, g_r), name="mixer_bwd", grid=(tp // tm,),
                 in_specs=[row(d), _full(w_out.shape), row(aw), row(rw), _full((1, aw)), _full((1, rw))],
                 out_specs=[row(aw), row(rw), _full((SUBLANES, d))],
                 out_shape=[jax.ShapeDtypeStruct((tp, aw), F32), jax.ShapeDtypeStruct((tp, rw), F32),
                            jax.ShapeDtypeStruct((SUBLANES, d), F32)],
                 semantics=("arbitrary",), vmem_bytes=vm)


def _mlp_up(z, w_up):
    tp, d = z.shape
    fc = w_up.shape[2]
    ff = N_CHIPS * fc
    tn = _divisor_tile(fc, LANES, 512)
    per = fc // tn

    def body(z_ref, w_ref, act_ref, up_ref):
        up = _dot(z_ref[...], w_ref[...])
        r = jnp.maximum(up, 0.0)
        act_ref[...] = (r * r).astype(BF16)
        up_ref[...] = up.astype(BF16)

    col = pl.BlockSpec((tp, tn), lambda j: (0, j))
    vm = 2 * _nbytes((tp, d), BF16) + 2 * _nbytes((d, tn), BF16) + 8 * _nbytes((tp, tn), F32)
    return _call(body, (z, w_up), name="mlp_up", grid=(ff // tn,),
                 in_specs=[_full((tp, d)), pl.BlockSpec((None, d, tn), lambda j: (j // per, 0, j % per))],
                 out_specs=[col, col],
                 out_shape=[jax.ShapeDtypeStruct((tp, ff), BF16)] * 2,
                 semantics=("parallel",), vmem_bytes=vm)


def _mlp_down(act, w_down, h, g_next):
    tp, d = h.shape
    ff = act.shape[1]
    fc = ff // N_CHIPS
    tm = _divisor_tile(tp, 16, ROW_TARGET)

    def body(a_ref, w_ref, h_ref, gn_ref, h2_ref, z_ref):
        h2 = h_ref[...]
        for j in range(N_CHIPS):
            h2 = h2 + _dot(a_ref[:, j * fc:(j + 1) * fc], w_ref[j])
        h2_ref[...] = h2
        z_ref[...] = (h2 * _rstd(h2) * gn_ref[...]).astype(BF16)

    row = lambda wd: pl.BlockSpec((tm, wd), lambda i: (i, 0))
    vm = 2 * _nbytes((ff, d), BF16) + 2 * _nbytes((tm, ff), BF16) + 10 * _nbytes((tm, d), F32)
    return _call(body, (act, w_down, h, g_next), name="mlp_down", grid=(tp // tm,),
                 in_specs=[row(ff), _full(w_down.shape), row(d), _full((1, d))],
                 out_specs=[row(d), row(d)],
                 out_shape=[jax.ShapeDtypeStruct((tp, d), F32), jax.ShapeDtypeStruct((tp, d), BF16)],
                 semantics=("parallel",), vmem_bytes=vm)


def _loss_bwd(h, g, target, n_real):
    tp, d = h.shape
    tm = _divisor_tile(tp, 16, ROW_TARGET)

    def body(h_ref, g_ref, t_ref, dh_ref, dhb_ref, dg_ref, loss_ref):
        i = pl.program_id(0)

        @pl.when(i == 0)
        def _():
            dg_ref[...] = jnp.zeros_like(dg_ref)
            loss_ref[...] = jnp.zeros_like(loss_ref)

        x = h_ref[...]
        gv = g_ref[...]
        rowi = i * tm + lax.broadcasted_iota(jnp.int32, (tm, 1), 0)
        real = jnp.logical_and(rowi >= N_META, rowi < N_META + n_real)
        err = jnp.where(real, x * _rstd(x) * gv - t_ref[...], 0.0)
        loss_ref[...] += 0.5 * jnp.sum(jnp.mean(err * err, axis=-1, keepdims=True))
        dx, dgp = _rms_bwd(err * (1.0 / d), x, gv)
        dh_ref[...] = dx
        dhb_ref[...] = dx.astype(BF16)
        dg_ref[...] += jnp.broadcast_to(dgp, (SUBLANES, d))

    row = pl.BlockSpec((tm, d), lambda i: (i, 0))
    return _call(body, (h, g, target), name="loss_bwd", grid=(tp // tm,),
                 in_specs=[row, _full((1, d)), row],
                 out_specs=[row, row, _full((SUBLANES, d)), _full((SUBLANES, LANES))],
                 out_shape=[jax.ShapeDtypeStruct((tp, d), F32), jax.ShapeDtypeStruct((tp, d), BF16),
                            jax.ShapeDtypeStruct((SUBLANES, d), F32), jax.ShapeDtypeStruct((SUBLANES, LANES), F32)],
                 semantics=("arbitrary",), vmem_bytes=16 * _nbytes((tm, d), F32))


def _mlp_bwd(dh_b, w_down, up, z2):
    tp, d = dh_b.shape
    fc = w_down.shape[1]
    ff = N_CHIPS * fc
    tn = _divisor_tile(fc, LANES, 512)
    per = fc // tn

    def body(dh_ref, z_ref, w_ref, up_ref, dup_ref, gd_ref, gu_ref):
        dh = dh_ref[...]
        r = jnp.maximum(up_ref[...].astype(F32), 0.0)
        dup = (_dot_nt(dh, w_ref[...]) * (2.0 * r)).astype(BF16)
        dup_ref[...] = dup
        gd_ref[...] = _dot_tn((r * r).astype(BF16), dh).astype(BF16)
        gu_ref[...] = _dot_tn(z_ref[...], dup).astype(BF16)

    col = pl.BlockSpec((tp, tn), lambda j: (0, j))
    vm = 4 * _nbytes((tp, d), BF16) + 4 * _nbytes((tn, d), BF16) + 2 * _nbytes((d, tn), BF16) \
        + 10 * _nbytes((tp, tn), F32) + 4 * _nbytes((tn, d), F32)
    return _call(body, (dh_b, z2, w_down, up), name="mlp_bwd", grid=(ff // tn,),
                 in_specs=[_full((tp, d)), _full((tp, d)),
                           pl.BlockSpec((None, tn, d), lambda j: (j // per, j % per, 0)), col],
                 out_specs=[col, pl.BlockSpec((tn, d), lambda j: (j, 0)),
                            pl.BlockSpec((None, d, tn), lambda j: (j // per, 0, j % per))],
                 out_shape=[jax.ShapeDtypeStruct((tp, ff), BF16), jax.ShapeDtypeStruct((ff, d), BF16),
                            jax.ShapeDtypeStruct((N_CHIPS, d, fc), BF16)],
                 semantics=("parallel",), vmem_bytes=vm)


def _grad_w_pieces(pieces, b):
    tp, n = b.shape
    tn = _divisor_tile(n, LANES, 512)
    widths = [pc.shape[1] for pc in pieces]

    def body(*refs):
        p_refs, b_ref, o_refs = refs[:len(pieces)], refs[len(pieces)], refs[len(pieces) + 1:]
        for p_ref, o_ref in zip(p_refs, o_refs):
            o_ref[...] = _dot_tn(p_ref[...], b_ref[...]).astype(BF16)

    vm = 2 * sum(_nbytes((tp, wd), BF16) for wd in widths) + 2 * _nbytes((tp, tn), BF16) \
        + 4 * sum(_nbytes((wd, tn), F32) for wd in widths) + 2 * _nbytes((tp, max(widths)), F32)
    return _call(body, tuple(pieces) + (b,), name="grad_w_pieces", grid=(n // tn,),
                 in_specs=[_full(pc.shape) for pc in pieces] + [pl.BlockSpec((tp, tn), lambda j: (0, j))],
                 out_specs=[pl.BlockSpec((wd, tn), lambda j: (0, j)) for wd in widths],
                 out_shape=[jax.ShapeDtypeStruct((wd, n), BF16) for wd in widths],
                 semantics=("parallel",), vmem_bytes=vm)


def _dx_norm_bwd(pieces, w, w_spec, w_piece, h, g, dres, dot=_dot_nt):
    tp, d = h.shape
    tm = _divisor_tile(tp, 16, ROW_TARGET)
    n = len(pieces)

    def body(*refs):
        dy_refs = refs[:n]
        w_ref, h_ref, g_ref, dres_ref, dh_ref, dhb_ref, dg_ref = refs[n:]

        @pl.when(pl.program_id(0) == 0)
        def _():
            dg_ref[...] = jnp.zeros_like(dg_ref)

        dz = dot(dy_refs[0][...], w_piece(w_ref, 0))
        for i in range(1, n):
            dz = dz + dot(dy_refs[i][...], w_piece(w_ref, i))
        dx, dgp = _rms_bwd(dz, h_ref[...], g_ref[...])
        dh = dres_ref[...] + dx
        dh_ref[...] = dh
        dhb_ref[...] = dh.astype(BF16)
        dg_ref[...] += jnp.broadcast_to(dgp, (SUBLANES, d))

    row = lambda wd: pl.BlockSpec((tm, wd), lambda i: (i, 0))
    kk = sum(wd for _, _, wd in pieces)
    vm = 2 * _nbytes((d, kk), BF16) + 2 * _nbytes((tm, kk), BF16) + 14 * _nbytes((tm, d), F32)
    piece_specs = [pl.BlockSpec((tm, wd), functools.partial(lambda i, cb: (i, cb), cb=cb)) for _, cb, wd in pieces]
    return _call(body, tuple(a for a, _, _ in pieces) + (w, h, g, dres), name="dx_norm_bwd", grid=(tp // tm,),
                 in_specs=piece_specs + [w_spec, row(d), _full((1, d)), row(d)],
                 out_specs=[row(d), row(d), _full((SUBLANES, d))],
                 out_shape=[jax.ShapeDtypeStruct((tp, d), F32), jax.ShapeDtypeStruct((tp, d), BF16),
                            jax.ShapeDtypeStruct((SUBLANES, d), F32)],
                 semantics=("arbitrary",), vmem_bytes=vm)


def _block_diag(wg):
    nb, b, _ = wg.shape
    eye = jnp.eye(nb, dtype=wg.dtype)
    return (eye[:, None, :, None] * wg[:, :, None, :]).reshape(nb * b, nb * b)


def _diag_blocks(dense, nb):
    b = dense.shape[0] // nb
    d4 = dense.reshape(nb, b, nb, b)
    return jnp.stack([d4[i, :, i, :] for i in range(nb)])


def _row(v):
    return v.reshape(1, -1)


def _forward_layer(l, h, z, p, fetch, stage_next, g_next):
    d = h.shape[1]
    att_w = d // 2
    rec_w = d - att_w
    nh = att_w // HEAD_DIM
    wa_d = _block_diag(p["w_gate_a"][l]).astype(BF16)
    wx_d = _block_diag(p["w_gate_x"][l]).astype(BF16)
    b_f_pad = jnp.zeros((1, LANES), F32).at[0, :nh].set(p["b_f"][l])
    w_in_t = fetch("w_in", z)
    big = dict(w_in_big=_pack_w_in_t(w_in_t.reshape(-1, d), att_w, nh))
    qkv, proj = _proj(z, big["w_in_big"], att_w)
    c, c_t = _fgate_fwd(proj, b_f_pad, nh)
    attn, lse_b = _attn_fwd(qkv, c, c_t, nh)
    hr, rec = _rec_fwd(proj, 0, 1, rec_w, p["conv_w"][l], _row(p["conv_b"][l]), wa_d,
                       _row(p["b_gate_a"][l]), wx_d, _row(p["b_gate_x"][l]), _row(p["lru_L"][l]))
    tok = stage_next(attn, "own")
    big["w_out"] = fetch("w_out", rec)
    h1, z2, mix = _mixer_out(attn, rec, _row(p["attn_out_g"][l] + tok), _row(p["rec_out_g"][l]),
                             big["w_out"], h, _row(p["mlp_norm_g"][l]))
    big["w_up"] = fetch("w_up", h1)
    act, up = _mlp_up(z2, big["w_up"])
    tok = stage_next(act, "next")
    big["w_down"] = fetch("w_down", act)
    h2, z_next = _mlp_down(act, big["w_down"], h1, _row(g_next + tok))
    saved = dict(h0=h, z1=z, proj=proj, qkv=qkv, c=c, c_t=c_t, attn=attn, lse_b=lse_b, hr=hr, rec=rec, h1=h1,
                 z2=z2, mix=mix, up=up, wa_d=wa_d, wx_d=wx_d, b_f_pad=b_f_pad, big=big)
    return h2, z_next, saved


def _backward_mlp(l, dh, dh_b, sv, p, tok):
    w_up, w_down = sv["big"]["w_up"], sv["big"]["w_down"]
    fc = w_up.shape[2]
    dup, g_down, g_up = _mlp_bwd(dh_b, w_down, sv["up"], sv["z2"])
    dh, dh_b, dg2 = _dx_norm_bwd([(dup, j, fc) for j in range(N_CHIPS)], w_up, _full(w_up.shape),
                                 lambda w_ref, j: w_ref[j], sv["h1"], _row(p["mlp_norm_g"][l] + tok), dh)
    big = dict(w_down=g_down.reshape((N_CHIPS, -1) + g_down.shape[1:]), w_up=g_up)
    return dh, dh_b, big, dict(mlp_norm_g=dg2[0])


def _backward_mixer(l, dh, dh_b, sv, p, tok):
    d = dh.shape[1]
    att_w = d // 2
    rec_w = d - att_w
    nh = att_w // HEAD_DIM
    small = {}
    dattn, drec, dg_mix = _mixer_bwd(dh_b, sv["big"]["w_out"], sv["attn"], sv["rec"],
                                     _row(p["attn_out_g"][l] + tok), _row(p["rec_out_g"][l]))
    small["attn_out_g"] = dg_mix[0, :att_w]
    small["rec_out_g"] = dg_mix[0, att_w:]
    dxr, dyr, dwa, dwx, sm = _rec_bwd(
        sv["proj"], 0, 1, rec_w, sv["hr"], drec, p["conv_w"][l], _row(p["conv_b"][l]), sv["wa_d"],
        _row(p["b_gate_a"][l]), sv["wx_d"], _row(p["b_gate_x"][l]), _row(p["lru_L"][l]))
    small.update(conv_w=sm[:CONV_WIDTH], conv_b=sm[4], b_gate_a=sm[5], b_gate_x=sm[6], lru_L=sm[7],
                 w_gate_a=_diag_blocks(dwa, N_REC_BLOCKS), w_gate_x=_diag_blocks(dwx, N_REC_BLOCKS))
    dq, dk, dv, dc = _attn_bwd(sv["qkv"], sv["c"], sv["c_t"], sv["lse_b"], dattn, nh)
    df, db_f = _fgate_bwd(sv["proj"], sv["b_f_pad"], dc)
    small["b_f"] = db_f[0, :nh]
    pieces = [dq, dk, dv, dxr, dyr, df]
    offs = [0, att_w, 2 * att_w, 3 * att_w, 3 * att_w + rec_w, 3 * att_w + 2 * rec_w]
    gq, gk, gv, gxr, gyr, gf = _grad_w_pieces(pieces, sv["z1"])
    g_in_t = jnp.concatenate([gq, gk, gv, gf[:nh], gxr, gyr], axis=0)
    w_big = sv["big"]["w_in_big"]
    widths = [pc.shape[1] for pc in pieces]
    dh, dh_b, dg1 = _dx_norm_bwd(
        [(pc, 0, wd) for pc, wd in zip(pieces, widths)], w_big, _full(w_big.shape),
        lambda w_ref, i: w_ref[offs[i]:offs[i] + widths[i], :], sv["h0"], _row(p["attn_norm_g"][l]), dh, dot=_dot)
    small["attn_norm_g"] = dg1[0]
    big = dict(w_in=g_in_t.reshape(N_CHIPS, -1, d))
    return dh, dh_b, big, small


def _pack_w_in_t(w_in_t, att_w, nh):
    qkv = w_in_t[:3 * att_w]
    f = w_in_t[3 * att_w:3 * att_w + nh]
    xy = w_in_t[3 * att_w + nh:]
    return jnp.concatenate([qkv, xy, f, jnp.zeros((LANES - nh, w_in_t.shape[1]), w_in_t.dtype)], axis=0)


ANY = pl.BlockSpec(memory_space=pl.ANY)


def _coords():
    return lax.axis_index("x"), lax.axis_index("y"), lax.axis_index("c")


def _other_chips(x, y):
    return [(1 - x, y), (x, 1 - y), (1 - x, 1 - y)]


def _remote(src, dst, send_sems, recv_sems, k, to):
    return pltpu.make_async_remote_copy(src_ref=src, dst_ref=dst, send_sem=send_sems.at[k],
                                        recv_sem=recv_sems.at[k], device_id=to, device_id_type=MESH)


def _all_gather_chips(shards):
    n = len(shards)
    per = 6

    def body(*refs):
        ins, outs = refs[:n], refs[n:2 * n]
        send_sems, recv_sems, local_sems = refs[2 * n:]
        x, y, c = _coords()
        me = 2 * x + y
        sibling = (x, y, 1 - c)
        chips = _other_chips(x, y)
        local = [pltpu.make_async_copy(ins[t], outs[t].at[me], local_sems.at[t]) for t in range(n)]
        for cp in local:
            cp.start()
        sends = []
        for t in range(n):
            for j, (px, py) in enumerate(chips):
                cp = _remote(ins[t].at[c], outs[t].at[me, c], send_sems, recv_sems, per * t + j, (px, py, c))
                cp.start()
                sends.append(cp)
        for t in range(n):
            for j, (px, py) in enumerate(chips):
                landed = outs[t].at[2 * px + py, c]
                _remote(landed, landed, send_sems, recv_sems, per * t + j, (px, py, c)).wait_recv()
                cp = _remote(landed, landed, send_sems, recv_sems, per * t + 3 + j, sibling)
                cp.start()
                sends.append(cp)
        for t in range(n):
            for j, (px, py) in enumerate(chips):
                passed = outs[t].at[2 * px + py, 1 - c]
                _remote(passed, passed, send_sems, recv_sems, per * t + 3 + j, sibling).wait_recv()
        for cp in sends:
            cp.wait_send()
        for cp in local:
            cp.wait()

    return _call(body, tuple(shards), name="all_gather_chips",
                 in_specs=[ANY] * n, out_specs=[ANY] * n,
                 out_shape=[jax.ShapeDtypeStruct((N_CHIPS,) + s.shape, s.dtype) for s in shards],
                 scratch_shapes=[pltpu.SemaphoreType.DMA((per * n,)), pltpu.SemaphoreType.DMA((per * n,)),
                                 pltpu.SemaphoreType.DMA((n,))])


HBM = pl.BlockSpec(memory_space=pltpu.HBM)
SEM = pl.BlockSpec(memory_space=pltpu.SEMAPHORE)
DATAFLOW = pltpu.SideEffectType.DATAFLOW_SIDE_EFFECTING


def _in_hbm(a):
    return pltpu.with_memory_space_constraint(a, pltpu.HBM)


PUSH_ARRIVALS = {"gather_chips_half": N_CHIPS - 1, "pass_halves": N_CHIPS - 1, "scatter_chips": N_CHIPS - 1,
                 "sibling": 1, "gather_devices": N_DEV - 1}


def _column_half(ref3, slab, c):
    hw = ref3.shape[2] // 2
    return ref3.at[slab, :, pl.ds(pl.multiple_of(c * hw, LANES), hw)]


def _push_copies(mode, src, land, send_sems, recv_sems, t):
    x, y, c = _coords()
    chip = 2 * x + y
    if mode == "gather_chips_half":
        return [_remote(_column_half(src, chip, c), _column_half(land, chip, c), send_sems, recv_sems, t, (px, py, c))
                for px, py in _other_chips(x, y)]
    if mode == "pass_halves":
        return [_remote(_column_half(src, 2 * px + py, c), _column_half(land, 2 * px + py, c), send_sems, recv_sems, t,
                        (x, y, 1 - c)) for px, py in _other_chips(x, y)]
    if mode == "scatter_chips":
        return [_remote(src.at[2 * px + py], land.at[chip], send_sems, recv_sems, t, (px, py, c))
                for px, py in _other_chips(x, y)]
    if mode == "sibling":
        return [_remote(src, land, send_sems, recv_sems, t, (x, y, 1 - c))]
    dev = 4 * x + 2 * y + c
    return [_remote(src.at[dev], land.at[dev], send_sems, recv_sems, t, (x ^ (k >> 2), y ^ ((k >> 1) & 1), c ^ (k & 1)))
            for k in range(1, N_DEV)]


def _push_start(srcs, lands, mode, name):
    n = len(srcs)
    same = all(s is ld for s, ld in zip(srcs, lands))
    n_in = n if same else 2 * n

    def body(*refs):
        src_refs = refs[:n]
        land_refs = src_refs if same else refs[n:2 * n]
        send_sems, recv_sems = refs[n_in], refs[n_in + 1]
        token = refs[-1]
        for t in range(n):
            for cp in _push_copies(mode, src_refs[t], land_refs[t], send_sems, recv_sems, t):
                cp.start()
        token[...] = jnp.zeros_like(token)

    operands = tuple(srcs) if same else tuple(srcs) + tuple(lands)
    res = _call(
        body, [_in_hbm(a) for a in operands], name=name,
        out_shape=(pltpu.SemaphoreType.DMA((n,)), pltpu.SemaphoreType.DMA((n,)))
        + tuple(pltpu.HBM(a.shape, a.dtype) for a in operands) + (jax.ShapeDtypeStruct((SUBLANES, LANES), F32),),
        in_specs=[HBM] * n_in, out_specs=(SEM, SEM) + (HBM,) * n_in + (pl.BlockSpec(memory_space=pltpu.VMEM),),
        input_output_aliases={i: 2 + i for i in range(n_in)}, side_effects=DATAFLOW, hbm_results=False)
    send_sems, recv_sems, token = res[0], res[1], res[-1]
    srcs_thru = res[2:2 + n]
    lands_thru = srcs_thru if same else res[2 + n:2 + 2 * n]
    return send_sems, recv_sems, srcs_thru, lands_thru, token


def _push_wait(send_sems, recv_sems, ids, srcs, lands, mode, after, name):
    n = len(lands)
    same = all(s is ld for s, ld in zip(srcs, lands))
    n_in = n if same else 2 * n

    def body(*refs):
        land_refs = refs[:n] if same else refs[n:2 * n]
        send_sems, recv_sems = refs[n_in], refs[n_in + 1]
        x, y, c = _coords()
        for t in range(n):
            if mode == "sibling":
                moved = land_refs[t]
            elif mode in ("gather_chips_half", "pass_halves"):
                moved = land_refs[t].at[pl.ds(0, PUSH_ARRIVALS[mode]), :, pl.ds(0, land_refs[t].shape[2] // 2)]
            else:
                moved = land_refs[t].at[pl.ds(0, PUSH_ARRIVALS[mode])]
            arrivals = _remote(moved, moved, send_sems, recv_sems, ids[t], (x, y, c))
            arrivals.wait_send()
            arrivals.wait_recv()

    operands = tuple(lands) if same else tuple(srcs) + tuple(lands)
    res = _call(
        body, operands + (send_sems, recv_sems, after), name=name,
        out_shape=tuple(pltpu.HBM(a.shape, a.dtype) for a in operands),
        in_specs=[HBM] * n_in + [SEM, SEM, ANY], out_specs=(HBM,) * n_in,
        input_output_aliases={i: i for i in range(n_in)}, side_effects=DATAFLOW)
    return list(res) if same else (list(res[:n]), list(res[n:]))


def _sum_partials(part, landed, chip):
    _, rows, cols = part.shape
    br = _divisor_tile(rows, 16, ELEM_ROWS)

    def body(chip_ref, own_ref, a_ref, b_ref, c_ref, o_ref):
        o_ref[...] = ((own_ref[...].astype(F32) + a_ref[...].astype(F32)) + b_ref[...].astype(F32)) \
            + c_ref[...].astype(F32)

    def other(k):
        return pl.BlockSpec((None, br, cols), lambda i, ch: (jnp.where(ch[0] <= k, k + 1, k), i, 0))

    spec = pltpu.PrefetchScalarGridSpec(
        num_scalar_prefetch=1, grid=(rows // br,),
        in_specs=[pl.BlockSpec((None, br, cols), lambda i, ch: (ch[0], i, 0)), other(0), other(1), other(2)],
        out_specs=pl.BlockSpec((br, cols), lambda i, ch: (i, 0)))
    return _call(body, (chip, part, landed, landed, landed), name="sum_partials", grid_spec=spec,
                 out_shape=jax.ShapeDtypeStruct((rows, cols), F32), semantics=("parallel",))


def _cast_to_slab(w, l, chip):
    _, rows, cols = w.shape
    br = _divisor_tile(rows, 16, ELEM_ROWS)

    def body(chip_ref, w_ref, o_ref):
        o_ref[...] = w_ref[...].astype(BF16)

    spec = pltpu.PrefetchScalarGridSpec(
        num_scalar_prefetch=1, grid=(rows // br,),
        in_specs=[pl.BlockSpec((None, br, cols), lambda i, ch: (l, i, 0))],
        out_specs=pl.BlockSpec((None, br, cols), lambda i, ch: (ch[0], i, 0)))
    return _call(body, (chip, w), name="cast_to_slab", grid_spec=spec,
                 out_shape=jax.ShapeDtypeStruct((N_CHIPS, rows, cols), BF16), semantics=("parallel",))


def _cast_w_in_t_to_slabs(w_t, chip):
    rows, depth, d = w_t.shape
    tn = _divisor_tile(d, LANES, 256)

    def body(chip_ref, w_ref, *o_refs):
        for l in range(depth):
            o_refs[l][...] = w_ref[:, l, :].astype(BF16)

    spec = pltpu.PrefetchScalarGridSpec(
        num_scalar_prefetch=1, grid=(d // tn,),
        in_specs=[pl.BlockSpec((rows, depth, tn), lambda j, ch: (0, 0, j))],
        out_specs=[pl.BlockSpec((None, rows, tn), lambda j, ch: (ch[0], 0, j))] * depth)
    return _call(body, (chip, w_t), name="cast_w_in_t_to_slabs", grid_spec=spec,
                 out_shape=[jax.ShapeDtypeStruct((N_CHIPS, rows, d), BF16)] * depth, semantics=("parallel",),
                 vmem_bytes=4 * _nbytes((rows, max(depth, SUBLANES), tn), F32))


def _place_slab(buf, index, n_slabs):
    rows, cols = buf.shape
    br = _divisor_tile(rows, SUBLANES, ELEM_ROWS)

    def body(index_ref, b_ref, o_ref):
        o_ref[...] = b_ref[...]

    spec = pltpu.PrefetchScalarGridSpec(
        num_scalar_prefetch=1, grid=(rows // br,),
        in_specs=[pl.BlockSpec((br, cols), lambda i, ix: (i, 0))],
        out_specs=pl.BlockSpec((None, br, cols), lambda i, ix: (ix[0], i, 0)))
    return _call(body, (index, buf), name="place_slab", grid_spec=spec,
                 out_shape=jax.ShapeDtypeStruct((n_slabs, rows, cols), buf.dtype), semantics=("parallel",))


ELEM_ROWS = 256


def _sum_slabs(r):
    n, rows, cols = r.shape
    br = _divisor_tile(rows, 16, ELEM_ROWS)

    def body(r_ref, o_ref):
        acc = r_ref[0].astype(F32)
        for j in range(1, n):
            acc = acc + r_ref[j].astype(F32)
        o_ref[...] = acc

    return _call(body, (r,), name="sum_slabs", grid=(rows // br,),
                 in_specs=[pl.BlockSpec((n, br, cols), lambda i: (0, i, 0))],
                 out_specs=pl.BlockSpec((br, cols), lambda i: (i, 0)),
                 out_shape=jax.ShapeDtypeStruct((rows, cols), F32), semantics=("parallel",))


def _adamw_math(w, g, m, v):
    c1 = 1.0 - ADAM_B1 ** ADAM_STEP
    c2 = 1.0 - ADAM_B2 ** ADAM_STEP
    nm = ADAM_B1 * m + (1.0 - ADAM_B1) * g
    nv = ADAM_B2 * v + (1.0 - ADAM_B2) * (g * g)
    delta = -ADAM_LR * ((nm / c1) / (jnp.sqrt(nv / c2) + ADAM_EPS) + ADAM_WD * w)
    return delta, nm, nv


def _adamw(w, g, m, v):
    rows, cols = w.shape
    br = _divisor_tile(rows, 8, ELEM_ROWS)

    def body(w_ref, g_ref, m_ref, v_ref, d_ref, nm_ref, nv_ref):
        d_ref[...], nm_ref[...], nv_ref[...] = _adamw_math(w_ref[...], g_ref[...], m_ref[...], v_ref[...])

    blk = pl.BlockSpec((br, cols), lambda i: (i, 0))
    return _call(body, (w, g, m, v), name="adamw", grid=(rows // br,),
                 in_specs=[blk] * 4, out_specs=[blk] * 3,
                 out_shape=[jax.ShapeDtypeStruct((rows, cols), F32)] * 3, semantics=("parallel",))


def _adamw_w_in_t(w_t, m_t, v_t, g_mine, g_theirs):
    rows, depth, d = w_t.shape
    tn = LANES

    def body(w_ref, m_ref, v_ref, *rest):
        ga_refs, gb_refs = rest[:depth], rest[depth:2 * depth]
        g_ref, d_ref, nm_ref, nv_ref = rest[2 * depth:]
        for l in range(depth):
            g_ref[:, l, :] = ga_refs[l][...] + gb_refs[l][...]
        d_ref[...], nm_ref[...], nv_ref[...] = _adamw_math(w_ref[...], g_ref[...], m_ref[...], v_ref[...])

    slab = pl.BlockSpec((rows, depth, tn), lambda j: (0, 0, j))
    gblk = pl.BlockSpec((rows, tn), lambda j: (0, j))
    return _call(body, (w_t, m_t, v_t) + tuple(g_mine) + tuple(g_theirs), name="adamw_w_in_t", grid=(d // tn,),
                 in_specs=[slab] * 3 + [gblk] * (2 * depth), out_specs=[slab] * 4,
                 out_shape=[jax.ShapeDtypeStruct(w_t.shape, F32)] * 4, semantics=("parallel",),
                 vmem_bytes=2 * (7 * _nbytes((rows, max(depth, SUBLANES), tn), F32)
                                 + 2 * depth * _nbytes((rows, tn), F32)))


def _adamw_layer(w, m, v, l, g_mine, g_theirs, prev, after):
    _, rows, cols = w.shape
    br = _divisor_tile(rows, 8, ELEM_ROWS)

    def body(w_ref, m_ref, v_ref, ga_ref, gb_ref, *rest):
        g_ref, d_ref, nm_ref, nv_ref = rest[5:]
        g = ga_ref[...] + gb_ref[...]
        g_ref[...] = g
        d_ref[...], nm_ref[...], nv_ref[...] = _adamw_math(w_ref[...], g, m_ref[...], v_ref[...])

    slot = pl.BlockSpec((None, br, cols), lambda i: (l, i, 0))
    blk = pl.BlockSpec((br, cols), lambda i: (i, 0))
    return _call(body, (w, m, v, g_mine, g_theirs) + tuple(prev) + (after,), name="adamw_layer",
                 grid=(rows // br,), in_specs=[slot] * 3 + [blk] * 2 + [ANY] * 5, out_specs=[slot] * 4,
                 out_shape=[jax.ShapeDtypeStruct(w.shape, F32)] * 4,
                 input_output_aliases={5: 0, 6: 1, 7: 2, 8: 3}, semantics=("parallel",))


BIG = ("w_in", "w_out", "w_up", "w_down")
WEIGHTS = ("meta", "attn_norm_g", "w_in", "b_f", "conv_w", "conv_b", "w_gate_a", "b_gate_a", "w_gate_x",
           "b_gate_x", "lru_L", "attn_out_g", "rec_out_g", "w_out", "mlp_norm_g", "w_up", "w_down", "final_g")
SMALL = tuple(k for k in WEIGHTS if k not in BIG)
COL_SHARDED_SMALL = ("meta", "conv_w")


def _packed_rows(shape):
    return -(-math.prod(shape) // (SUBLANES * LANES)) * SUBLANES


def _pack(arrs):
    rows = []
    for a in arrs:
        flat = a.reshape(-1)
        rows.append(jnp.pad(flat, (0, _packed_rows(a.shape) * LANES - flat.shape[0])).reshape(-1, LANES))
    used = sum(r.shape[0] for r in rows)
    rows.append(jnp.zeros((-used % ELEM_ROWS, LANES), F32))
    return jnp.concatenate(rows, axis=0)


def _unpack(buf, shapes):
    out, r0 = [], 0
    for s in shapes:
        nr = _packed_rows(s)
        out.append(buf[r0:r0 + nr].reshape(-1)[:math.prod(s)].reshape(s))
        r0 += nr
    return out


def _halves(a):
    return a.reshape((2, a.shape[0] // 2) + a.shape[1:])


def _cols_from_chips(g):
    return jnp.moveaxis(g, 0, -2).reshape(g.shape[1:-1] + (N_CHIPS * g.shape[-1],))


def kernel(x, meta, attn_norm_g, w_in, b_f, conv_w, conv_b, w_gate_a, b_gate_a, w_gate_x, b_gate_x, lru_L, attn_out_g, rec_out_g, w_out, mlp_norm_g, w_up, w_down, final_g, loss_target, m_meta, m_attn_norm_g, m_w_in, m_b_f, m_conv_w, m_conv_b, m_w_gate_a, m_b_gate_a, m_w_gate_x, m_b_gate_x, m_lru_L, m_attn_out_g, m_rec_out_g, m_w_out, m_mlp_norm_g, m_w_up, m_w_down, m_final_g, v_meta, v_attn_norm_g, v_w_in, v_b_f, v_conv_w, v_conv_b, v_w_gate_a, v_b_gate_a, v_w_gate_x, v_b_gate_x, v_lru_L, v_attn_out_g, v_rec_out_g, v_w_out, v_mlp_norm_g, v_w_up, v_w_down, v_final_g):
    w = dict(meta=meta, attn_norm_g=attn_norm_g, w_in=w_in, b_f=b_f, conv_w=conv_w, conv_b=conv_b,
             w_gate_a=w_gate_a, b_gate_a=b_gate_a, w_gate_x=w_gate_x, b_gate_x=b_gate_x, lru_L=lru_L,
             attn_out_g=attn_out_g, rec_out_g=rec_out_g, w_out=w_out, mlp_norm_g=mlp_norm_g, w_up=w_up,
             w_down=w_down, final_g=final_g)
    m = dict(meta=m_meta, attn_norm_g=m_attn_norm_g, w_in=m_w_in, b_f=m_b_f, conv_w=m_conv_w, conv_b=m_conv_b,
             w_gate_a=m_w_gate_a, b_gate_a=m_b_gate_a, w_gate_x=m_w_gate_x, b_gate_x=m_b_gate_x, lru_L=m_lru_L,
             attn_out_g=m_attn_out_g, rec_out_g=m_rec_out_g, w_out=m_w_out, mlp_norm_g=m_mlp_norm_g,
             w_up=m_w_up, w_down=m_w_down, final_g=m_final_g)
    v = dict(meta=v_meta, attn_norm_g=v_attn_norm_g, w_in=v_w_in, b_f=v_b_f, conv_w=v_conv_w, conv_b=v_conv_b,
             w_gate_a=v_w_gate_a, b_gate_a=v_b_gate_a, w_gate_x=v_w_gate_x, b_gate_x=v_b_gate_x, lru_L=v_lru_L,
             attn_out_g=v_attn_out_g, rec_out_g=v_rec_out_g, w_out=v_w_out, mlp_norm_g=v_mlp_norm_g,
             w_up=v_w_up, w_down=v_w_down, final_g=v_final_g)
    s_len, d = x.shape[1], x.shape[2]
    depth = w_in.shape[0]
    chip = 2 * lax.axis_index("x") + lax.axis_index("y")

    g_conv, g_meta = [g.reshape((N_CHIPS, g.shape[1] * g.shape[2]) + g.shape[3:])
                      for g in _all_gather_chips([_halves(w["conv_w"]), _halves(w["meta"])])]
    p = dict(w)
    p["conv_w"] = _cols_from_chips(g_conv)
    meta_full = jnp.moveaxis(g_meta, 0, 1).reshape(N_META, d)

    chip1 = chip.reshape(1).astype(jnp.int32)
    w_in_t, m_in_t, v_in_t = [jnp.transpose(a["w_in"], (2, 0, 1)) for a in (w, m, v)]
    w_in_slabs = _cast_w_in_t_to_slabs(w_in_t, chip1)
    pushes, tokens = [], []
    for l in range(depth):
        slabs = [w_in_slabs[l]] + [_cast_to_slab(w[k], l, chip1) for k in BIG[1:]]
        send_sems, recv_sems, _, lands, token = _push_start(slabs, slabs, "gather_chips_half", f"weights_start_{l}")
        pushes.append((send_sems, recv_sems, lands))
        tokens.append(token[0, 0])
    passed = [{} for _ in range(depth)]

    def stage(l, after, ids):
        send_sems, recv_sems, lands = pushes[l]
        tag = "_".join(BIG[i] for i in ids)
        sub = [lands[i] for i in ids]
        sub = _push_wait(send_sems, recv_sems, ids, sub, sub, "gather_chips_half", after, f"{tag}_wait_{l}")
        send_sems, recv_sems, _, sub, token = _push_start(sub, sub, "pass_halves", f"{tag}_pass_{l}")
        for j, i in enumerate(ids):
            passed[l][i] = (send_sems, recv_sems, sub[j], j)
        return token[0, 0]

    t_len = N_META + s_len
    pad = -t_len % SEQ_TILE
    h = jnp.concatenate([meta_full, x[0], jnp.zeros((pad, d), F32)], axis=0)
    tgt = jnp.concatenate([jnp.zeros((N_META, d), F32), loss_target[0], jnp.zeros((pad, d), F32)], axis=0)
    z = _rms_fwd(h, _row(p["attn_norm_g"][0] + sum(tokens)))
    stage(0, z, [0])
    saved = []
    for l in range(depth):
        def fetch(k, after, l=l):
            send_sems, recv_sems, land, j = passed[l][BIG.index(k)]
            return _push_wait(send_sems, recv_sems, [j], [land], [land], "pass_halves", after,
                              f"{k}_here_{l}")[0]

        def stage_next(after, which, l=l):
            if which == "own":
                return stage(l, after, [1, 2, 3])
            return stage(l + 1, after, [0]) if l + 1 < depth else 0.0

        g_next = p["attn_norm_g"][l + 1] if l + 1 < depth else p["final_g"]
        h, z, sv = _forward_layer(l, h, z, p, fetch, stage_next, g_next)
        saved.append(sv)
    dh, dh_b, dg_final, loss_part = _loss_bwd(h, _row(p["final_g"]), tgt, s_len)

    small = {k: [None] * depth for k in SMALL if k not in ("meta", "final_g")}
    pushes = [None] * depth
    tok = 0.0
    for l in reversed(range(depth)):
        dh, dh_b, big_mlp, sm_mlp = _backward_mlp(l, dh, dh_b, saved[l], p, tok)
        g_out, = _grad_w_pieces([saved[l]["mix"]], dh_b)
        parts = [big_mlp["w_down"], big_mlp["w_up"], g_out.reshape((N_CHIPS, -1) + g_out.shape[1:])]
        push_mlp = _push_start(parts, [lax.empty(a.shape, a.dtype) for a in parts], "scatter_chips",
                               f"mlp_grads_start_{l}")
        dh, dh_b, big_mix, sm_mix = _backward_mixer(l, dh, dh_b, saved[l], p, push_mlp[4][0, 0])
        parts = [big_mix["w_in"]]
        push_mix = _push_start(parts, [lax.empty(a.shape, a.dtype) for a in parts], "scatter_chips",
                               f"mixer_grads_start_{l}")
        tok = push_mix[4][0, 0]
        pushes[l] = {("w_down", "w_up", "w_out"): push_mlp, ("w_in",): push_mix}
        for k, val in {**sm_mlp, **sm_mix}.items():
            small[k][l] = val
    grads = {k: jnp.stack(val) for k, val in small.items()}
    grads["final_g"] = dg_final[0]
    grads["meta"] = dh[:N_META]
    dx = dh[N_META:t_len]

    full_shapes = [grads[k].shape for k in SMALL] + [(1,)]
    packed = _pack([grads[k].astype(F32) for k in SMALL] + [loss_part[0, :1] + tok])
    dev1 = (2 * chip + lax.axis_index("c")).reshape(1).astype(jnp.int32)
    slabs = [_place_slab(packed, dev1, N_DEV)]
    small_push = _push_start(slabs, slabs, "gather_devices", "small_grads_start")

    last_token = small_push[4]
    outs = {k: [lax.empty(w[k].shape, F32) for _ in range(4)] for k in BIG[1:]}
    w_in_sums = [None] * depth
    swaps = {}

    def finish(l, wait_after, adam_after):
        send_sems, recv_sems, mine, lands, _ = swaps[l]
        mine, theirs = _push_wait(send_sems, recv_sems, list(range(len(BIG))), mine, lands, "sibling", wait_after,
                                  f"sums_wait_{l}")
        w_in_sums[l] = (mine[0], theirs[0])
        for k, a, b in zip(BIG[1:], mine[1:], theirs[1:]):
            outs[k] = _adamw_layer(w[k], m[k], v[k], l, a, b, outs[k], adam_after)

    wait_after = last_token
    for l in reversed(range(depth)):
        sums = {}
        for names, (send_sems, recv_sems, parts, lands, _) in pushes[l].items():
            parts, landed = _push_wait(send_sems, recv_sems, list(range(len(names))), parts, lands, "scatter_chips",
                                       wait_after, f"{names[0]}_grads_wait_{l}")
            for k, part, land in zip(names, parts, landed):
                sums[k] = wait_after = _sum_partials(part, land, chip1)
        mine = [sums[k] for k in BIG]
        swaps[l] = _push_start(mine, [lax.empty(a.shape, a.dtype) for a in mine], "sibling", f"sums_start_{l}")
        if l + 2 < depth:
            finish(l + 2, outs["w_down"][0] if l + 3 < depth else mine[0], swaps[l][4])
    for l in reversed(range(min(2, depth))):
        finish(l, outs["w_down"][0] if depth > 2 else swaps[0][4], swaps[0][4])
    outs["w_in"] = [jnp.transpose(r, (1, 2, 0)) for r in _adamw_w_in_t(
        w_in_t, m_in_t, v_in_t, [s[0] for s in w_in_sums], [s[1] for s in w_in_sums])]
    out_g, out_d, out_m, out_v = [{k: outs[k][i] for k in BIG} for i in range(4)]

    landed = _push_wait(small_push[0], small_push[1], [0], small_push[3], small_push[3], "gather_devices",
                        out_g["w_in"], "small_grads_wait")
    total = _sum_slabs(landed[0])
    small_g = dict(zip(SMALL + ("loss",), _unpack(total, full_shapes)))
    for k in COL_SHARDED_SMALL:
        n = w[k].shape[-1]
        small_g[k] = lax.dynamic_slice_in_dim(small_g[k], chip * n, n, axis=small_g[k].ndim - 1)
    local_shapes = [w[k].shape for k in SMALL]
    res = _adamw(_pack([w[k] for k in SMALL]), _pack([small_g[k] for k in SMALL]),
                 _pack([m[k] for k in SMALL]), _pack([v[k] for k in SMALL]))
    out_g.update({k: small_g[k] for k in SMALL})
    for dst, buf in zip((out_d, out_m, out_v), res):
        dst.update(zip(SMALL, _unpack(buf, local_shapes)))

    return (small_g["loss"].reshape(()), dx[None],
            *[out_g[k] for k in WEIGHTS], *[out_d[k] for k in WEIGHTS],
            *[out_m[k] for k in WEIGHTS], *[out_v[k] for k in WEIGHTS])
```

```python
import functools
import math

import jax
import jax.numpy as jnp
from jax import lax
from jax.experimental import pallas as pl
from jax.experimental.pallas import tpu as pltpu

F32 = jnp.float32
BF16 = jnp.bfloat16

N_META = 16
HEAD_DIM = 64
N_REC_BLOCKS = 8
CONV_WIDTH = 4
RG_C = 8.0
NORM_EPS = 1e-6
ADAM_LR = 0.001
ADAM_B1 = 0.9
ADAM_B2 = 0.999
ADAM_EPS = 1e-08
ADAM_WD = 0.01
ADAM_STEP = 10

LANES = 128
SUBLANES = 8
SEQ_TILE = 128
VMEM_CAP = 60 * 2**20
VMEM_SLACK = 6 * 2**20
NEG_BIG = -1e30
N_CHIPS = 4
N_DEV = 8
MESH = pl.DeviceIdType.MESH


def _nbytes(shape, dtype):
    return math.prod(shape) * jnp.dtype(dtype).itemsize


def _call(body, args, *, name, out_shape, grid=(), in_specs=None, out_specs=None, scratch_shapes=(),
          grid_spec=None, semantics=None, vmem_bytes=None, side_effects=None, hbm_results=True, **kw):
    cp = {}
    if semantics is not None:
        cp["dimension_semantics"] = semantics
    if vmem_bytes is not None:
        cp["vmem_limit_bytes"] = int(min(VMEM_CAP, vmem_bytes + VMEM_SLACK))
    if side_effects is not None:
        cp["has_side_effects"] = side_effects
    if grid_spec is not None:
        kw["grid_spec"] = grid_spec
    else:
        kw.update(grid=grid, in_specs=in_specs, out_specs=out_specs, scratch_shapes=scratch_shapes)
    if hbm_results:
        out_shape = jax.tree.map(
            lambda s: pltpu.HBM(s.shape, s.dtype) if isinstance(s, jax.ShapeDtypeStruct) else s, out_shape)
    fn = pl.pallas_call(
        body, name=name, out_shape=out_shape,
        compiler_params=pltpu.CompilerParams(**cp), **kw)
    return fn(*[_in_hbm(a) if jnp.issubdtype(getattr(a, "dtype", jnp.int32), jnp.floating) else a for a in args])


def _divisor_tile(n, unit, target):
    best = None
    for t in range(unit, min(n, target) + 1, unit):
        if n % t == 0:
            best = t
    return n if best is None else best


def _sigmoid(x):
    return 1.0 / (1.0 + jnp.exp(-x))


def _log1p_unit(e):
    series = e * (1.0 - e * (0.5 - e * (1.0 / 3.0)))
    return jnp.where(e < 1e-2, series, jnp.log(1.0 + e))


def _log_sigmoid(x):
    return jnp.minimum(x, 0.0) - _log1p_unit(jnp.exp(-jnp.abs(x)))


def _one_minus_exp(x, exp_x):
    small = -x * (1.0 + x * (1.0 / 2 + x * (1.0 / 6 + x * (1.0 / 24 + x * (1.0 / 120 + x * (1.0 / 720))))))
    return jnp.where(x > -0.25, small, 1.0 - exp_x)


_GELU_K = math.sqrt(2.0 / math.pi)
_GELU_C = 0.044715


def _gelu_and_grad(y):
    th = jnp.tanh(_GELU_K * (y + _GELU_C * y * y * y))
    g = 0.5 * y * (1.0 + th)
    dg = 0.5 * (1.0 + th) + 0.5 * y * (1.0 - th * th) * _GELU_K * (1.0 + 3.0 * _GELU_C * y * y)
    return g, dg


def _rstd(x):
    return lax.rsqrt(jnp.mean(x * x, axis=-1, keepdims=True) + NORM_EPS)


def _rms_bwd(dz, x, g):
    rs = _rstd(x)
    xh = x * rs
    dgp = jnp.sum(dz * xh, axis=0, keepdims=True)
    dxh = dz * g
    dx = rs * (dxh - xh * jnp.mean(dxh * xh, axis=-1, keepdims=True))
    return dx, dgp


def _dot(a, b):
    return jnp.dot(a, b, preferred_element_type=F32)


def _dot_nt(a, b):
    return lax.dot_general(a, b, (((1,), (1,)), ((), ())), preferred_element_type=F32)


def _dot_tn(a, b):
    return lax.dot_general(a, b, (((0,), (0,)), ((), ())), preferred_element_type=F32)


def _full(shape):
    nd = len(shape)
    return pl.BlockSpec(shape, lambda *_: (0,) * nd)


def _rms_fwd(h, g):
    tp, d = h.shape
    tm = _divisor_tile(tp, 16, 544)

    def body(h_ref, g_ref, z_ref):
        x = h_ref[...]
        z_ref[...] = (x * _rstd(x) * g_ref[...]).astype(BF16)

    return _call(body, (h, g), name="rms_fwd", grid=(tp // tm,),
                 in_specs=[pl.BlockSpec((tm, d), lambda i: (i, 0)), _full((1, d))],
                 out_specs=pl.BlockSpec((tm, d), lambda i: (i, 0)),
                 out_shape=jax.ShapeDtypeStruct((tp, d), BF16), semantics=("parallel",))


def _proj(z, w_big_t, att_w):
    tp, d = z.shape
    nb = w_big_t.shape[0]
    tn = _divisor_tile(nb, LANES, 512)
    assert (3 * att_w) % tn == 0
    n_qkv = 3 * att_w // tn
    scale = 1.0 / math.sqrt(HEAD_DIM)

    def body(z_ref, w_ref, qkv_ref, p_ref):
        j = pl.program_id(0)
        acc = _dot_nt(z_ref[...], w_ref[...])

        @pl.when(j < n_qkv)
        def _():
            col = j * tn + lax.broadcasted_iota(jnp.int32, (1, tn), 1)
            qkv_ref[...] = (acc * jnp.where(col < att_w, scale, 1.0)).astype(BF16)

        @pl.when(j >= n_qkv)
        def _():
            p_ref[...] = acc

    vm = 2 * (_nbytes((tp, d), BF16) + _nbytes((d, tn), BF16) + _nbytes((tp, tn), F32) * 2)
    return _call(body, (z, w_big_t), name="proj", grid=(nb // tn,),
                 in_specs=[_full((tp, d)), pl.BlockSpec((tn, d), lambda j: (j, 0))],
                 out_specs=[pl.BlockSpec((tp, tn), lambda j: (0, jnp.minimum(j, n_qkv - 1))),
                            pl.BlockSpec((tp, tn), lambda j: (0, jnp.maximum(j - n_qkv, 0)))],
                 out_shape=[jax.ShapeDtypeStruct((tp, 3 * att_w), BF16),
                            jax.ShapeDtypeStruct((tp, nb - 3 * att_w), F32)],
                 semantics=("arbitrary",), vmem_bytes=vm)


def _tile_cumsum(x, row, reverse=False):
    for s in (1, 2, 4):
        if reverse:
            x = x + jnp.where(row < SUBLANES - s, pltpu.roll(x, SUBLANES - s, 0), 0.0)
        else:
            x = x + jnp.where(row >= s, pltpu.roll(x, s, 0), 0.0)
    return x


def _fgate_fwd(proj, b_f_pad, nh):
    tp, nb = proj.shape
    fblk = nb // LANES - 1

    def body(f_ref, b_ref, c_ref, ct_ref):
        b = b_ref[...]
        row = lax.broadcasted_iota(jnp.int32, (SUBLANES, LANES), 0)

        def step(i, carry):
            r0 = pl.multiple_of(i * SUBLANES, SUBLANES)
            lf = _log_sigmoid(f_ref[pl.ds(r0, SUBLANES), :] + b)
            x = _tile_cumsum(lf, row) + carry
            c_ref[pl.ds(r0, SUBLANES), :] = x
            return x[SUBLANES - 1:SUBLANES, :]

        lax.fori_loop(0, tp // SUBLANES, step, jnp.zeros((1, LANES), F32))
        ct_ref[...] = c_ref[...].T[:nh, :]

    return _call(body, (proj, b_f_pad), name="fgate_fwd", grid=(1,),
                 in_specs=[pl.BlockSpec((tp, LANES), lambda i: (0, fblk)), _full((1, LANES))],
                 out_specs=[_full((tp, LANES)), _full((nh, tp))],
                 out_shape=[jax.ShapeDtypeStruct((tp, LANES), F32), jax.ShapeDtypeStruct((nh, tp), F32)],
                 semantics=("arbitrary",))


def _fgate_bwd(proj, b_f_pad, dc):
    tp, nb = proj.shape
    fblk = nb // LANES - 1

    def body(f_ref, b_ref, dc_ref, df_ref, db_ref, dc_s):
        b = b_ref[...]
        row = lax.broadcasted_iota(jnp.int32, (SUBLANES, LANES), 0)
        nt = tp // SUBLANES

        def step(i, carry):
            suffix, acc = carry
            r0 = pl.multiple_of((nt - 1 - i) * SUBLANES, SUBLANES)
            dlf = _tile_cumsum(dc_ref[pl.ds(r0, SUBLANES), :], row, reverse=True) + suffix
            df = dlf * _sigmoid(-(f_ref[pl.ds(r0, SUBLANES), :] + b))
            dc_s[pl.ds(r0, SUBLANES), :] = df
            return dlf[0:1, :], acc + df

        _, acc = lax.fori_loop(0, nt, step, (jnp.zeros((1, LANES), F32), jnp.zeros((SUBLANES, LANES), F32)))
        df_ref[...] = dc_s[...].astype(BF16)
        db_ref[...] = jnp.broadcast_to(jnp.sum(acc, axis=0, keepdims=True), (SUBLANES, LANES))

    return _call(body, (proj, b_f_pad, dc), name="fgate_bwd", grid=(1,),
                 in_specs=[pl.BlockSpec((tp, LANES), lambda i: (0, fblk)), _full((1, LANES)), _full((tp, LANES))],
                 out_specs=[_full((tp, LANES)), _full((SUBLANES, LANES))],
                 out_shape=[jax.ShapeDtypeStruct((tp, LANES), BF16),
                            jax.ShapeDtypeStruct((SUBLANES, LANES), F32)],
                 scratch_shapes=[pltpu.VMEM((tp, LANES), F32)], semantics=("arbitrary",))


ATT_BQ = 128


ATT_BUCKET = 1
ATT_HEADS = 4


def _for_bucket(i, nq, fn):
    for lo in range(0, nq, ATT_BUCKET):
        hi = min(lo + ATT_BUCKET, nq)
        spans = ([(0, lo * ATT_BQ, False)] if lo else []) + [(lo * ATT_BQ, hi * ATT_BQ, True)]
        pl.when(jnp.logical_and(i >= lo, i < hi))(functools.partial(fn, spans))


def _head_column(c_blk, h):
    lane = lax.broadcasted_iota(jnp.int32, c_blk.shape, 1)
    return jnp.sum(jnp.where(lane == h, c_blk, 0.0), axis=1, keepdims=True)


def _head_columns_into(p, c_ref, ck_s):
    for hh in range(ATT_HEADS):
        ck_s[hh] = jnp.broadcast_to(_head_column(c_ref[...], ATT_HEADS * p + hh), ck_s.shape[1:])


def _pair_diag_cols(x2):
    top = lax.broadcasted_iota(jnp.int32, (LANES, ATT_BQ), 0) < HEAD_DIM
    xt = x2.astype(F32).T.astype(BF16)
    return jnp.concatenate([jnp.where(top, xt, 0), jnp.where(top, 0, xt)], axis=1)


def _pair_diag_rows(x2):
    low = lax.broadcasted_iota(jnp.int32, (ATT_BQ, LANES), 1) < HEAD_DIM
    return jnp.concatenate([jnp.where(low, x2, 0), jnp.where(low, 0, x2)], axis=0)


def _seen_keys(k0, k1, q0):
    keys = k0 + lax.broadcasted_iota(jnp.int32, (k1 - k0, ATT_BQ), 0)
    return keys <= q0 + lax.broadcasted_iota(jnp.int32, (k1 - k0, ATT_BQ), 1)


def _attn_fwd(qkv, c, c_t, nh):
    tp = qkv.shape[0]
    att_w = nh * HEAD_DIM
    ng = nh // ATT_HEADS
    gw = ATT_HEADS * HEAD_DIM
    bq = ATT_BQ
    nq = tp // bq
    pair = 2 * HEAD_DIM
    assert pair == LANES and ATT_HEADS % 2 == 0

    def body(q_ref, k_ref, v_ref, c_ref, ct_ref, o_ref, lse_ref, ck_s, vt_s):
        p = pl.program_id(0)
        i = pl.program_id(1)

        @pl.when(i == 0)
        def _():
            _head_columns_into(p, c_ref, ck_s)
            vt_s[...] = v_ref[...].astype(F32).T.astype(BF16)

        def compute(spans):
            q0 = pl.multiple_of(i * bq, bq)
            o_t, lses = [], []
            for pi in range(ATT_HEADS // 2):
                lo = pair * pi
                heads = (2 * pi, 2 * pi + 1)
                q_cols = _pair_diag_cols(q_ref[:, lo:lo + pair])
                ts = []
                for k0, k1, needs_mask in spans:
                    t2 = _dot(k_ref[k0:k1, lo:lo + pair], q_cols)
                    t_e = [t2[:, e * bq:(e + 1) * bq] - ck_s[hh, k0:k1, :] for e, hh in enumerate(heads)]
                    if needs_mask:
                        seen = _seen_keys(k0, k1, q0)
                        t_e = [jnp.where(seen, t, NEG_BIG) for t in t_e]
                    ts.append(t_e)
                ms = [functools.reduce(jnp.maximum, [jnp.max(t[e], axis=0, keepdims=True) for t in ts])
                      for e in range(2)]
                es = [[jnp.exp(t[e] - ms[e]) for e in range(2)] for t in ts]
                ls = [sum(jnp.sum(e_[e], axis=0, keepdims=True) for e_ in es) for e in range(2)]
                o2 = sum(_dot(vt_s[lo:lo + pair, k0:k1],
                              jnp.concatenate([e_[0].astype(BF16), e_[1].astype(BF16)], axis=1))
                         for e_, (k0, k1, _) in zip(es, spans))
                o_t += [o2[:HEAD_DIM, :bq] / ls[0], o2[HEAD_DIM:, bq:] / ls[1]]
                lses += [ms[e] + ct_ref[pl.ds(ATT_HEADS * p + hh, 1), :] + jnp.log(ls[e])
                         for e, hh in enumerate(heads)]
            o_ref[...] = jnp.concatenate(o_t, axis=0).T
            lse_ref[...] = jnp.concatenate(lses, axis=0)

        _for_bucket(i, nq, compute)

    blk = pl.BlockSpec((bq, gw), lambda p, i: (i, p))
    vm = 6 * _nbytes((tp, gw), BF16) + 2 * ATT_HEADS * _nbytes((tp, LANES), F32) + 2 * _nbytes((tp, LANES), F32) \
        + 8 * ATT_HEADS * _nbytes((bq, tp), F32)
    return _call(body, (qkv, qkv, qkv, c, c_t), name="attn_fwd", grid=(ng, nq),
                 in_specs=[blk,
                           pl.BlockSpec((tp, gw), lambda p, i: (0, ng + p)),
                           pl.BlockSpec((tp, gw), lambda p, i: (0, 2 * ng + p)),
                           _full((tp, LANES)), pl.BlockSpec((nh, bq), lambda p, i: (0, i))],
                 out_specs=[blk, pl.BlockSpec((None, ATT_HEADS, bq), lambda p, i: (p, 0, i))],
                 out_shape=[jax.ShapeDtypeStruct((tp, att_w), F32), jax.ShapeDtypeStruct((ng, ATT_HEADS, tp), F32)],
                 scratch_shapes=[pltpu.VMEM((ATT_HEADS, tp, LANES), F32), pltpu.VMEM((gw, tp), BF16)],
                 semantics=("arbitrary", "arbitrary"), vmem_bytes=vm)


def _attn_bwd(qkv, c, c_t, lse, do, nh):
    tp = qkv.shape[0]
    att_w = nh * HEAD_DIM
    ng = nh // ATT_HEADS
    gw = ATT_HEADS * HEAD_DIM
    bq = ATT_BQ
    nq = tp // bq
    pair = 2 * HEAD_DIM
    assert pair == LANES and ATT_HEADS % 2 == 0
    scale = 1.0 / math.sqrt(HEAD_DIM)

    def body(q_ref, k_ref, v_ref, c_ref, ct_ref, lse_ref, do_ref, dq_ref, dk_ref, dv_ref, dc_ref,
             dk_s, dv_s, dc_s, ck_s, kt_s):
        p = pl.program_id(0)
        i = pl.program_id(1)

        @pl.when(i == 0)
        def _():
            dk_s[...] = jnp.zeros_like(dk_s)
            dv_s[...] = jnp.zeros_like(dv_s)
            dc_s[...] = jnp.zeros_like(dc_s)
            kt_s[...] = k_ref[...].astype(F32).T.astype(BF16)
            _head_columns_into(p, c_ref, ck_s)

        @pl.when(jnp.logical_and(i == 0, p == 0))
        def _():
            dc_ref[...] = jnp.zeros_like(dc_ref)

        def compute(spans):
            q0 = pl.multiple_of(i * bq, bq)
            dq_t = []
            for pi in range(ATT_HEADS // 2):
                lo = pair * pi
                q2 = q_ref[:, lo:lo + pair]
                do2 = do_ref[:, lo:lo + pair].astype(BF16)
                q_cols, do_cols = _pair_diag_cols(q2), _pair_diag_cols(do2)
                q_rows, do_rows = _pair_diag_rows(q2), _pair_diag_rows(do2)
                heads = (2 * pi, 2 * pi + 1)
                col_terms = [ct_ref[pl.ds(ATT_HEADS * p + hh, 1), :] - lse_ref[hh:hh + 1, :] for hh in heads]
                prs, dps = [], []
                for k0, k1, needs_mask in spans:
                    t2 = _dot(k_ref[k0:k1, lo:lo + pair], q_cols)
                    dp2 = _dot(v_ref[k0:k1, lo:lo + pair], do_cols)
                    if needs_mask:
                        seen = _seen_keys(k0, k1, q0)
                    pr_e, dp_e = [], []
                    for e, hh in enumerate(heads):
                        t = t2[:, e * bq:(e + 1) * bq] - ck_s[hh, k0:k1, :]
                        if needs_mask:
                            t = jnp.where(seen, t, NEG_BIG)
                        pr_e.append(jnp.exp(t + col_terms[e]))
                        dp_e.append(dp2[:, e * bq:(e + 1) * bq])
                    prs.append(pr_e)
                    dps.append(dp_e)
                key_sums = [sum(jnp.sum(pr[e] * dp[e], axis=0, keepdims=True) for pr, dp in zip(prs, dps))
                            for e in range(2)]
                dq2 = 0.0
                for (k0, k1, _), pr, dp in zip(spans, prs, dps):
                    ds = [pr[e] * (dp[e] - key_sums[e]) for e in range(2)]
                    for e, hh in enumerate(heads):
                        dc_s[hh, k0:k1, :] += jnp.sum(ds[e], axis=1, keepdims=True)
                    ds2 = jnp.concatenate([ds[0].astype(BF16), ds[1].astype(BF16)], axis=1)
                    pr2 = jnp.concatenate([pr[0].astype(BF16), pr[1].astype(BF16)], axis=1)
                    dk_s[k0:k1, lo:lo + pair] += _dot(ds2, q_rows)
                    dv_s[k0:k1, lo:lo + pair] += _dot(pr2, do_rows)
                    dq2 = dq2 + _dot(kt_s[lo:lo + pair, k0:k1], ds2)
                dq_t.append(jnp.concatenate([dq2[:HEAD_DIM, :bq], dq2[HEAD_DIM:, bq:]], axis=0))
            dq_ref[...] = (jnp.concatenate(dq_t, axis=0) * scale).T.astype(BF16)

        _for_bucket(i, nq, compute)

        @pl.when(i == nq - 1)
        def _():
            dk_ref[...] = dk_s[...].astype(BF16)
            dv_ref[...] = dv_s[...].astype(BF16)
            lane = lax.broadcasted_iota(jnp.int32, (tp, LANES), 1)
            dc = dc_ref[...]
            for hh in range(ATT_HEADS):
                dc = jnp.where(lane == ATT_HEADS * p + hh, -dc_s[hh], dc)
            dc_ref[...] = dc

    blk = pl.BlockSpec((bq, gw), lambda p, i: (i, p))
    col = pl.BlockSpec((tp, gw), lambda p, i: (0, p))
    vm = 7 * _nbytes((tp, gw), BF16) + 2 * _nbytes((tp, gw), F32) + 2 * ATT_HEADS * _nbytes((tp, LANES), F32) \
        + 2 * _nbytes((tp, LANES), F32) + 12 * ATT_HEADS * _nbytes((bq, tp), F32)
    return _call(body, (qkv, qkv, qkv, c, c_t, lse, do), name="attn_bwd", grid=(ng, nq),
                 in_specs=[blk,
                           pl.BlockSpec((tp, gw), lambda p, i: (0, ng + p)),
                           pl.BlockSpec((tp, gw), lambda p, i: (0, 2 * ng + p)),
                           _full((tp, LANES)), pl.BlockSpec((nh, bq), lambda p, i: (0, i)),
                           pl.BlockSpec((None, ATT_HEADS, bq), lambda p, i: (p, 0, i)), blk],
                 out_specs=[blk, col, col, _full((tp, LANES))],
                 out_shape=[jax.ShapeDtypeStruct((tp, att_w), BF16)] * 3 + [jax.ShapeDtypeStruct((tp, LANES), F32)],
                 scratch_shapes=[pltpu.VMEM((tp, gw), F32), pltpu.VMEM((tp, gw), F32),
                                 pltpu.VMEM((ATT_HEADS, tp, 1), F32), pltpu.VMEM((ATT_HEADS, tp, LANES), F32),
                                 pltpu.VMEM((gw, tp), BF16)],
                 semantics=("arbitrary", "arbitrary"), vmem_bytes=vm)


REC_ROWS = 128
HALO = SUBLANES


def _conv_taps(cat):
    taps = []
    for k in range(CONV_WIDTH):
        sh = CONV_WIDTH - 1 - k
        taps.append((pltpu.roll(cat, sh, 0) if sh else cat)[HALO:])
    return taps


def _rec_gates(xc, wa_ref, ba_ref, wx_ref, bx_ref, l_ref):
    xcb = xc.astype(BF16)
    r = _sigmoid(_dot(xcb, wa_ref[...]) + ba_ref[...])
    ig = _sigmoid(_dot(xcb, wx_ref[...]) + bx_ref[...])
    ls = _log_sigmoid(l_ref[...])
    log_a = RG_C * r * ls
    return xcb, r, ig, ls, log_a


def _rec_fwd(proj, xr_blk, yr_blk, rec_w, conv_w, conv_b, wa, ba, wx, bx, lru):
    tp = proj.shape[0]
    w = rec_w
    r_rows = REC_ROWS
    nc = tp // r_rows

    def body(xr_ref, yr_ref, cw_ref, cb_ref, wa_ref, ba_ref, wx_ref, bx_ref, l_ref,
             hr_ref, rec_ref, prev_s, carry_s, a_s, u_s):
        i = pl.program_id(0)

        @pl.when(i == 0)
        def _():
            prev_s[...] = jnp.zeros_like(prev_s)
            carry_s[...] = jnp.zeros_like(carry_s)

        x = xr_ref[...]
        taps = _conv_taps(jnp.concatenate([prev_s[...], x], axis=0))
        prev_s[...] = x[r_rows - HALO:]
        xc = cb_ref[...]
        for k in range(CONV_WIDTH):
            xc = xc + cw_ref[k:k + 1, :] * taps[k]
        _, r, ig, ls, log_a = _rec_gates(xc, wa_ref, ba_ref, wx_ref, bx_ref, l_ref)
        a = jnp.exp(log_a)
        a_s[...] = a
        u_s[...] = jnp.sqrt(_one_minus_exp(2.0 * log_a, a * a)) * ig * xc

        def tile(j, h):
            r0 = pl.multiple_of(j * SUBLANES, SUBLANES)
            at = a_s[pl.ds(r0, SUBLANES), :]
            ut = u_s[pl.ds(r0, SUBLANES), :]
            out = []
            for rr in range(SUBLANES):
                h = at[rr:rr + 1] * h + ut[rr:rr + 1]
                out.append(h)
            hr_ref[pl.ds(r0, SUBLANES), :] = jnp.concatenate(out, axis=0)
            return h

        carry_s[0:1, :] = lax.fori_loop(0, r_rows // SUBLANES, tile, carry_s[0:1, :])
        g, _ = _gelu_and_grad(yr_ref[...])
        rec_ref[...] = hr_ref[...] * g

    blk = pl.BlockSpec((r_rows, w), lambda i: (i, 0))
    vm = 16 * _nbytes((r_rows, w), F32) + 4 * _nbytes((w, w), BF16)
    return _call(body, (proj, proj, conv_w, conv_b, wa, ba, wx, bx, lru), name="rec_fwd", grid=(nc,),
                 in_specs=[pl.BlockSpec((r_rows, w), lambda i: (i, xr_blk)),
                           pl.BlockSpec((r_rows, w), lambda i: (i, yr_blk)),
                           _full((CONV_WIDTH, w)), _full((1, w)), _full((w, w)), _full((1, w)),
                           _full((w, w)), _full((1, w)), _full((1, w))],
                 out_specs=[blk, blk],
                 out_shape=[jax.ShapeDtypeStruct((tp, w), F32)] * 2,
                 scratch_shapes=[pltpu.VMEM((HALO, w), F32), pltpu.VMEM((SUBLANES, w), F32),
                                 pltpu.VMEM((r_rows, w), F32), pltpu.VMEM((r_rows, w), F32)],
                 semantics=("arbitrary",), vmem_bytes=vm)


def _rec_bwd(proj, xr_blk, yr_blk, rec_w, hr, drec, conv_w, conv_b, wa, ba, wx, bx, lru):
    tp = proj.shape[0]
    w = rec_w
    r_rows = REC_ROWS
    nc = tp // r_rows
    hpc = r_rows // HALO

    def body(xr_ref, xh_ref, yr_ref, hr_ref, hh_ref, drec_ref, cw_ref, cb_ref, wa_ref, ba_ref, wx_ref, bx_ref,
             l_ref, dxr_ref, dyr_ref, dwa_ref, dwx_ref, small_ref, lam_s, a_s, dhr_s, carry_s, next_s):
        i = pl.program_id(0)
        first = (nc - 1 - i) == 0

        @pl.when(i == 0)
        def _():
            carry_s[...] = jnp.zeros_like(carry_s)
            next_s[...] = jnp.zeros_like(next_s)
            dwa_ref[...] = jnp.zeros_like(dwa_ref)
            dwx_ref[...] = jnp.zeros_like(dwx_ref)
            small_ref[...] = jnp.zeros_like(small_ref)

        x = xr_ref[...]
        xprev = jnp.where(first, 0.0, xh_ref[...])
        taps = _conv_taps(jnp.concatenate([xprev, x], axis=0))
        xc = cb_ref[...]
        for k in range(CONV_WIDTH):
            xc = xc + cw_ref[k:k + 1, :] * taps[k]
        xcb, r, ig, ls, log_a = _rec_gates(xc, wa_ref, ba_ref, wx_ref, bx_ref, l_ref)
        a = jnp.exp(log_a)
        a2 = a * a
        mult = jnp.sqrt(_one_minus_exp(2.0 * log_a, a2))
        g, dg = _gelu_and_grad(yr_ref[...])
        hr_v = hr_ref[...]
        drec_v = drec_ref[...]
        dhr_s[...] = drec_v * g
        dyr_ref[...] = (drec_v * hr_v * dg).astype(BF16)
        a_s[...] = a

        def tile(jj, carry):
            r0 = pl.multiple_of((r_rows // SUBLANES - 1 - jj) * SUBLANES, SUBLANES)
            at = a_s[pl.ds(r0, SUBLANES), :]
            dt = dhr_s[pl.ds(r0, SUBLANES), :]
            out = [None] * SUBLANES
            for rr in range(SUBLANES - 1, -1, -1):
                lam = dt[rr:rr + 1] + carry
                out[rr] = lam
                carry = at[rr:rr + 1] * lam
            lam_s[pl.ds(r0, SUBLANES), :] = jnp.concatenate(out, axis=0)
            return carry

        carry_s[0:1, :] = lax.fori_loop(0, r_rows // SUBLANES, tile, carry_s[0:1, :])
        lam = lam_s[...]
        hprev = jnp.where(first, 0.0, hh_ref[...])
        hr_prev = pltpu.roll(jnp.concatenate([hprev, hr_v], axis=0), 1, 0)[HALO:]
        da = lam * hr_prev
        dxc = lam * mult * ig
        di = lam * mult * xc
        dmult = lam * ig * xc
        dlog_a = da * a - dmult * a2 / mult
        dr = dlog_a * (RG_C * ls)
        dls = jnp.sum(dlog_a * (RG_C * r), axis=0, keepdims=True)
        dga = dr * r * (1.0 - r)
        dgx = di * ig * (1.0 - ig)
        dgab = dga.astype(BF16)
        dgxb = dgx.astype(BF16)
        dxc = dxc + _dot_nt(dgab, wa_ref[...]) + _dot_nt(dgxb, wx_ref[...])
        dwa_ref[...] += _dot_tn(xcb, dgab)
        dwx_ref[...] += _dot_tn(xcb, dgxb)
        cat = jnp.concatenate([dxc, next_s[...]], axis=0)
        next_s[...] = dxc[0:HALO]
        dxr = cw_ref[CONV_WIDTH - 1:CONV_WIDTH, :] * dxc
        for k in range(CONV_WIDTH - 1):
            sh = CONV_WIDTH - 1 - k
            dxr = dxr + cw_ref[k:k + 1, :] * pltpu.roll(cat, r_rows + HALO - sh, 0)[:r_rows]
        dxr_ref[...] = dxr.astype(BF16)
        rows = [jnp.sum(dxc * taps[k], axis=0, keepdims=True) for k in range(CONV_WIDTH)]
        rows += [jnp.sum(dxc, axis=0, keepdims=True), jnp.sum(dga, axis=0, keepdims=True),
                 jnp.sum(dgx, axis=0, keepdims=True), dls * _sigmoid(-l_ref[...])]
        small_ref[...] += jnp.concatenate(rows, axis=0)

    def rev(i):
        return nc - 1 - i

    def halo(i):
        return jnp.maximum(rev(i) * hpc - 1, 0)

    blk = pl.BlockSpec((r_rows, w), lambda i: (rev(i), 0))
    vm = 40 * _nbytes((r_rows, w), F32) + 6 * _nbytes((w, w), F32)
    return _call(body, (proj, proj, proj, hr, hr, drec, conv_w, conv_b, wa, ba, wx, bx, lru),
                 name="rec_bwd", grid=(nc,),
                 in_specs=[pl.BlockSpec((r_rows, w), lambda i: (rev(i), xr_blk)),
                           pl.BlockSpec((HALO, w), lambda i: (halo(i), xr_blk)),
                           pl.BlockSpec((r_rows, w), lambda i: (rev(i), yr_blk)),
                           blk,
                           pl.BlockSpec((HALO, w), lambda i: (halo(i), 0)),
                           blk,
                           _full((CONV_WIDTH, w)), _full((1, w)), _full((w, w)), _full((1, w)),
                           _full((w, w)), _full((1, w)), _full((1, w))],
                 out_specs=[blk, blk, _full((w, w)), _full((w, w)), _full((SUBLANES, w))],
                 out_shape=[jax.ShapeDtypeStruct((tp, w), BF16)] * 2
                 + [jax.ShapeDtypeStruct((w, w), F32)] * 2 + [jax.ShapeDtypeStruct((SUBLANES, w), F32)],
                 scratch_shapes=[pltpu.VMEM((r_rows, w), F32)] * 3
                 + [pltpu.VMEM((SUBLANES, w), F32), pltpu.VMEM((HALO, w), F32)],
                 semantics=("arbitrary",), vmem_bytes=vm)


ROW_TARGET = 544


def _mixer_out(attn, rec, g_a, g_r, w_out, h, g_next):
    tp, d = h.shape
    aw, rw = attn.shape[1], rec.shape[1]
    kc = d // N_CHIPS
    tm = _divisor_tile(tp, 16, ROW_TARGET)

    def body(a_ref, r_ref, ga_ref, gr_ref, w_ref, h_ref, gn_ref, h1_ref, z_ref, mix_ref):
        a = a_ref[...]
        r = r_ref[...]
        mix = jnp.concatenate([a * _rstd(a) * ga_ref[...], r * _rstd(r) * gr_ref[...]], axis=1).astype(BF16)
        mix_ref[...] = mix
        h1 = h_ref[...]
        for j in range(N_CHIPS):
            h1 = h1 + _dot(mix[:, j * kc:(j + 1) * kc], w_ref[j])
        h1_ref[...] = h1
        z_ref[...] = (h1 * _rstd(h1) * gn_ref[...]).astype(BF16)

    row = lambda wd: pl.BlockSpec((tm, wd), lambda i: (i, 0))
    vm = 2 * _nbytes((d, d), BF16) + 12 * _nbytes((tm, d), F32)
    return _call(body, (attn, rec, g_a, g_r, w_out, h, g_next), name="mixer_out", grid=(tp // tm,),
                 in_specs=[row(aw), row(rw), _full((1, aw)), _full((1, rw)), _full(w_out.shape), row(d),
                           _full((1, d))],
                 out_specs=[row(d), row(d), row(d)],
                 out_shape=[jax.ShapeDtypeStruct((tp, d), F32), jax.ShapeDtypeStruct((tp, d), BF16),
                            jax.ShapeDtypeStruct((tp, d), BF16)],
                 semantics=("parallel",), vmem_bytes=vm)


def _mixer_bwd(dh_b, w_out, attn, rec, g_a, g_r):
    tp, d = dh_b.shape
    aw, rw = attn.shape[1], rec.shape[1]
    tm = _divisor_tile(tp, 16, ROW_TARGET)

    def body(dh_ref, w_ref, a_ref, r_ref, ga_ref, gr_ref, da_ref, dr_ref, dg_ref):
        @pl.when(pl.program_id(0) == 0)
        def _():
            dg_ref[...] = jnp.zeros_like(dg_ref)

        dh = dh_ref[...]
        dmix = jnp.concatenate([_dot_nt(dh, w_ref[j]) for j in range(N_CHIPS)], axis=1)
        da, dga = _rms_bwd(dmix[:, :aw], a_ref[...], ga_ref[...])
        dr, dgr = _rms_bwd(dmix[:, aw:], r_ref[...], gr_ref[...])
        da_ref[...] = da
        dr_ref[...] = dr
        dg_ref[...] += jnp.broadcast_to(jnp.concatenate([dga, dgr], axis=1), (SUBLANES, d))

    row = lambda wd: pl.BlockSpec((tm, wd), lambda i: (i, 0))
    vm = 2 * _nbytes((d, d), BF16) + 12 * _nbytes((tm, d), F32)
    return _call(body, (dh_b, w_out, attn, rec, g_a, g_r), name="mixer_bwd", grid=(tp // tm,),
                 in_specs=[row(d), _full(w_out.shape), row(aw), row(rw), _full((1, aw)), _full((1, rw))],
                 out_specs=[row(aw), row(rw), _full((SUBLANES, d))],
                 out_shape=[jax.ShapeDtypeStruct((tp, aw), F32), jax.ShapeDtypeStruct((tp, rw), F32),
                            jax.ShapeDtypeStruct((SUBLANES, d), F32)],
                 semantics=("arbitrary",), vmem_bytes=vm)


def _mlp_up(z, w_up):
    tp, d = z.shape
    fc = w_up.shape[2]
    ff = N_CHIPS * fc
    tn = _divisor_tile(fc, LANES, 512)
    per = fc // tn

    def body(z_ref, w_ref, act_ref, up_ref):
        up = _dot(z_ref[...], w_ref[...])
        r = jnp.maximum(up, 0.0)
        act_ref[...] = (r * r).astype(BF16)
        up_ref[...] = up.astype(BF16)

    col = pl.BlockSpec((tp, tn), lambda j: (0, j))
    vm = 2 * _nbytes((tp, d), BF16) + 2 * _nbytes((d, tn), BF16) + 8 * _nbytes((tp, tn), F32)
    return _call(body, (z, w_up), name="mlp_up", grid=(ff // tn,),
                 in_specs=[_full((tp, d)), pl.BlockSpec((None, d, tn), lambda j: (j // per, 0, j % per))],
                 out_specs=[col, col],
                 out_shape=[jax.ShapeDtypeStruct((tp, ff), BF16)] * 2,
                 semantics=("parallel",), vmem_bytes=vm)


def _mlp_down(act, w_down, h, g_next):
    tp, d = h.shape
    ff = act.shape[1]
    fc = ff // N_CHIPS
    tm = _divisor_tile(tp, 16, ROW_TARGET)

    def body(a_ref, w_ref, h_ref, gn_ref, h2_ref, z_ref):
        h2 = h_ref[...]
        for j in range(N_CHIPS):
            h2 = h2 + _dot(a_ref[:, j * fc:(j + 1) * fc], w_ref[j])
        h2_ref[...] = h2
        z_ref[...] = (h2 * _rstd(h2) * gn_ref[...]).astype(BF16)

    row = lambda wd: pl.BlockSpec((tm, wd), lambda i: (i, 0))
    vm = 2 * _nbytes((ff, d), BF16) + 2 * _nbytes((tm, ff), BF16) + 10 * _nbytes((tm, d), F32)
    return _call(body, (act, w_down, h, g_next), name="mlp_down", grid=(tp // tm,),
                 in_specs=[row(ff), _full(w_down.shape), row(d), _full((1, d))],
                 out_specs=[row(d), row(d)],
                 out_shape=[jax.ShapeDtypeStruct((tp, d), F32), jax.ShapeDtypeStruct((tp, d), BF16)],
                 semantics=("parallel",), vmem_bytes=vm)


def _loss_bwd(h, g, target, n_real):
    tp, d = h.shape
    tm = _divisor_tile(tp, 16, ROW_TARGET)

    def body(h_ref, g_ref, t_ref, dh_ref, dhb_ref, dg_ref, loss_ref):
        i = pl.program_id(0)

        @pl.when(i == 0)
        def _():
            dg_ref[...] = jnp.zeros_like(dg_ref)
            loss_ref[...] = jnp.zeros_like(loss_ref)

        x = h_ref[...]
        gv = g_ref[...]
        rowi = i * tm + lax.broadcasted_iota(jnp.int32, (tm, 1), 0)
        real = jnp.logical_and(rowi >= N_META, rowi < N_META + n_real)
        err = jnp.where(real, x * _rstd(x) * gv - t_ref[...], 0.0)
        loss_ref[...] += 0.5 * jnp.sum(jnp.mean(err * err, axis=-1, keepdims=True))
        dx, dgp = _rms_bwd(err * (1.0 / d), x, gv)
        dh_ref[...] = dx
        dhb_ref[...] = dx.astype(BF16)
        dg_ref[...] += jnp.broadcast_to(dgp, (SUBLANES, d))

    row = pl.BlockSpec((tm, d), lambda i: (i, 0))
    return _call(body, (h, g, target), name="loss_bwd", grid=(tp // tm,),
                 in_specs=[row, _full((1, d)), row],
                 out_specs=[row, row, _full((SUBLANES, d)), _full((SUBLANES, LANES))],
                 out_shape=[jax.ShapeDtypeStruct((tp, d), F32), jax.ShapeDtypeStruct((tp, d), BF16),
                            jax.ShapeDtypeStruct((SUBLANES, d), F32), jax.ShapeDtypeStruct((SUBLANES, LANES), F32)],
                 semantics=("arbitrary",), vmem_bytes=16 * _nbytes((tm, d), F32))


def _mlp_bwd(dh_b, w_down, up, z2):
    tp, d = dh_b.shape
    fc = w_down.shape[1]
    ff = N_CHIPS * fc
    tn = _divisor_tile(fc, LANES, 512)
    per = fc // tn

    def body(dh_ref, z_ref, w_ref, up_ref, dup_ref, gd_ref, gu_ref):
        dh = dh_ref[...]
        r = jnp.maximum(up_ref[...].astype(F32), 0.0)
        dup = (_dot_nt(dh, w_ref[...]) * (2.0 * r)).astype(BF16)
        dup_ref[...] = dup
        gd_ref[...] = _dot_tn((r * r).astype(BF16), dh).astype(BF16)
        gu_ref[...] = _dot_tn(z_ref[...], dup).astype(BF16)

    col = pl.BlockSpec((tp, tn), lambda j: (0, j))
    vm = 4 * _nbytes((tp, d), BF16) + 4 * _nbytes((tn, d), BF16) + 2 * _nbytes((d, tn), BF16) \
        + 10 * _nbytes((tp, tn), F32) + 4 * _nbytes((tn, d), F32)
    return _call(body, (dh_b, z2, w_down, up), name="mlp_bwd", grid=(ff // tn,),
                 in_specs=[_full((tp, d)), _full((tp, d)),
                           pl.BlockSpec((None, tn, d), lambda j: (j // per, j % per, 0)), col],
                 out_specs=[col, pl.BlockSpec((tn, d), lambda j: (j, 0)),
                            pl.BlockSpec((None, d, tn), lambda j: (j // per, 0, j % per))],
                 out_shape=[jax.ShapeDtypeStruct((tp, ff), BF16), jax.ShapeDtypeStruct((ff, d), BF16),
                            jax.ShapeDtypeStruct((N_CHIPS, d, fc), BF16)],
                 semantics=("parallel",), vmem_bytes=vm)


def _grad_w_pieces(pieces, b):
    tp, n = b.shape
    tn = _divisor_tile(n, LANES, 512)
    widths = [pc.shape[1] for pc in pieces]

    def body(*refs):
        p_refs, b_ref, o_refs = refs[:len(pieces)], refs[len(pieces)], refs[len(pieces) + 1:]
        for p_ref, o_ref in zip(p_refs, o_refs):
            o_ref[...] = _dot_tn(p_ref[...], b_ref[...]).astype(BF16)

    vm = 2 * sum(_nbytes((tp, wd), BF16) for wd in widths) + 2 * _nbytes((tp, tn), BF16) \
        + 4 * sum(_nbytes((wd, tn), F32) for wd in widths) + 2 * _nbytes((tp, max(widths)), F32)
    return _call(body, tuple(pieces) + (b,), name="grad_w_pieces", grid=(n // tn,),
                 in_specs=[_full(pc.shape) for pc in pieces] + [pl.BlockSpec((tp, tn), lambda j: (0, j))],
                 out_specs=[pl.BlockSpec((wd, tn), lambda j: (0, j)) for wd in widths],
                 out_shape=[jax.ShapeDtypeStruct((wd, n), BF16) for wd in widths],
                 semantics=("parallel",), vmem_bytes=vm)


def _dx_norm_bwd(pieces, w, w_spec, w_piece, h, g, dres, dot=_dot_nt):
    tp, d = h.shape
    tm = _divisor_tile(tp, 16, ROW_TARGET)
    n = len(pieces)

    def body(*refs):
        dy_refs = refs[:n]
        w_ref, h_ref, g_ref, dres_ref, dh_ref, dhb_ref, dg_ref = refs[n:]

        @pl.when(pl.program_id(0) == 0)
        def _():
            dg_ref[...] = jnp.zeros_like(dg_ref)

        dz = dot(dy_refs[0][...], w_piece(w_ref, 0))
        for i in range(1, n):
            dz = dz + dot(dy_refs[i][...], w_piece(w_ref, i))
        dx, dgp = _rms_bwd(dz, h_ref[...], g_ref[...])
        dh = dres_ref[...] + dx
        dh_ref[...] = dh
        dhb_ref[...] = dh.astype(BF16)
        dg_ref[...] += jnp.broadcast_to(dgp, (SUBLANES, d))

    row = lambda wd: pl.BlockSpec((tm, wd), lambda i: (i, 0))
    kk = sum(wd for _, _, wd in pieces)
    vm = 2 * _nbytes((d, kk), BF16) + 2 * _nbytes((tm, kk), BF16) + 14 * _nbytes((tm, d), F32)
    piece_specs = [pl.BlockSpec((tm, wd), functools.partial(lambda i, cb: (i, cb), cb=cb)) for _, cb, wd in pieces]
    return _call(body, tuple(a for a, _, _ in pieces) + (w, h, g, dres), name="dx_norm_bwd", grid=(tp // tm,),
                 in_specs=piece_specs + [w_spec, row(d), _full((1, d)), row(d)],
                 out_specs=[row(d), row(d), _full((SUBLANES, d))],
                 out_shape=[jax.ShapeDtypeStruct((tp, d), F32), jax.ShapeDtypeStruct((tp, d), BF16),
                            jax.ShapeDtypeStruct((SUBLANES, d), F32)],
                 semantics=("arbitrary",), vmem_bytes=vm)


def _block_diag(wg):
    nb, b, _ = wg.shape
    eye = jnp.eye(nb, dtype=wg.dtype)
    return (eye[:, None, :, None] * wg[:, :, None, :]).reshape(nb * b, nb * b)


def _diag_blocks(dense, nb):
    b = dense.shape[0] // nb
    d4 = dense.reshape(nb, b, nb, b)
    return jnp.stack([d4[i, :, i, :] for i in range(nb)])


def _row(v):
    return v.reshape(1, -1)


def _forward_layer(l, h, z, p, fetch, stage_next, g_next):
    d = h.shape[1]
    att_w = d // 2
    rec_w = d - att_w
    nh = att_w // HEAD_DIM
    wa_d = _block_diag(p["w_gate_a"][l]).astype(BF16)
    wx_d = _block_diag(p["w_gate_x"][l]).astype(BF16)
    b_f_pad = jnp.zeros((1, LANES), F32).at[0, :nh].set(p["b_f"][l])
    w_in_t = fetch("w_in", z)
    big = dict(w_in_big=_pack_w_in_t(w_in_t.reshape(-1, d), att_w, nh))
    qkv, proj = _proj(z, big["w_in_big"], att_w)
    c, c_t = _fgate_fwd(proj, b_f_pad, nh)
    attn, lse_b = _attn_fwd(qkv, c, c_t, nh)
    hr, rec = _rec_fwd(proj, 0, 1, rec_w, p["conv_w"][l], _row(p["conv_b"][l]), wa_d,
                       _row(p["b_gate_a"][l]), wx_d, _row(p["b_gate_x"][l]), _row(p["lru_L"][l]))
    tok = stage_next(attn, "own")
    big["w_out"] = fetch("w_out", rec)
    h1, z2, mix = _mixer_out(attn, rec, _row(p["attn_out_g"][l] + tok), _row(p["rec_out_g"][l]),
                             big["w_out"], h, _row(p["mlp_norm_g"][l]))
    big["w_up"] = fetch("w_up", h1)
    act, up = _mlp_up(z2, big["w_up"])
    tok = stage_next(act, "next")
    big["w_down"] = fetch("w_down", act)
    h2, z_next = _mlp_down(act, big["w_down"], h1, _row(g_next + tok))
    saved = dict(h0=h, z1=z, proj=proj, qkv=qkv, c=c, c_t=c_t, attn=attn, lse_b=lse_b, hr=hr, rec=rec, h1=h1,
                 z2=z2, mix=mix, up=up, wa_d=wa_d, wx_d=wx_d, b_f_pad=b_f_pad, big=big)
    return h2, z_next, saved


def _backward_mlp(l, dh, dh_b, sv, p, tok):
    w_up, w_down = sv["big"]["w_up"], sv["big"]["w_down"]
    fc = w_up.shape[2]
    dup, g_down, g_up = _mlp_bwd(dh_b, w_down, sv["up"], sv["z2"])
    dh, dh_b, dg2 = _dx_norm_bwd([(dup, j, fc) for j in range(N_CHIPS)], w_up, _full(w_up.shape),
                                 lambda w_ref, j: w_ref[j], sv["h1"], _row(p["mlp_norm_g"][l] + tok), dh)
    big = dict(w_down=g_down.reshape((N_CHIPS, -1) + g_down.shape[1:]), w_up=g_up)
    return dh, dh_b, big, dict(mlp_norm_g=dg2[0])


def _backward_mixer(l, dh, dh_b, sv, p, tok):
    d = dh.shape[1]
    att_w = d // 2
    rec_w = d - att_w
    nh = att_w // HEAD_DIM
    small = {}
    dattn, drec, dg_mix = _mixer_bwd(dh_b, sv["big"]["w_out"], sv["attn"], sv["rec"],
                                     _row(p["attn_out_g"][l] + tok), _row(p["rec_out_g"][l]))
    small["attn_out_g"] = dg_mix[0, :att_w]
    small["rec_out_g"] = dg_mix[0, att_w:]
    dxr, dyr, dwa, dwx, sm = _rec_bwd(
        sv["proj"], 0, 1, rec_w, sv["hr"], drec, p["conv_w"][l], _row(p["conv_b"][l]), sv["wa_d"],
        _row(p["b_gate_a"][l]), sv["wx_d"], _row(p["b_gate_x"][l]), _row(p["lru_L"][l]))
    small.update(conv_w=sm[:CONV_WIDTH], conv_b=sm[4], b_gate_a=sm[5], b_gate_x=sm[6], lru_L=sm[7],
                 w_gate_a=_diag_blocks(dwa, N_REC_BLOCKS), w_gate_x=_diag_blocks(dwx, N_REC_BLOCKS))
    dq, dk, dv, dc = _attn_bwd(sv["qkv"], sv["c"], sv["c_t"], sv["lse_b"], dattn, nh)
    df, db_f = _fgate_bwd(sv["proj"], sv["b_f_pad"], dc)
    small["b_f"] = db_f[0, :nh]
    pieces = [dq, dk, dv, dxr, dyr, df]
    offs = [0, att_w, 2 * att_w, 3 * att_w, 3 * att_w + rec_w, 3 * att_w + 2 * rec_w]
    gq, gk, gv, gxr, gyr, gf = _grad_w_pieces(pieces, sv["z1"])
    g_in_t = jnp.concatenate([gq, gk, gv, gf[:nh], gxr, gyr], axis=0)
    w_big = sv["big"]["w_in_big"]
    widths = [pc.shape[1] for pc in pieces]
    dh, dh_b, dg1 = _dx_norm_bwd(
        [(pc, 0, wd) for pc, wd in zip(pieces, widths)], w_big, _full(w_big.shape),
        lambda w_ref, i: w_ref[offs[i]:offs[i] + widths[i], :], sv["h0"], _row(p["attn_norm_g"][l]), dh, dot=_dot)
    small["attn_norm_g"] = dg1[0]
    big = dict(w_in=g_in_t.reshape(N_CHIPS, -1, d))
    return dh, dh_b, big, small


def _pack_w_in_t(w_in_t, att_w, nh):
    qkv = w_in_t[:3 * att_w]
    f = w_in_t[3 * att_w:3 * att_w + nh]
    xy = w_in_t[3 * att_w + nh:]
    return jnp.concatenate([qkv, xy, f, jnp.zeros((LANES - nh, w_in_t.shape[1]), w_in_t.dtype)], axis=0)


ANY = pl.BlockSpec(memory_space=pl.ANY)


def _coords():
    return lax.axis_index("x"), lax.axis_index("y"), lax.axis_index("c")


def _other_chips(x, y):
    return [(1 - x, y), (x, 1 - y), (1 - x, 1 - y)]


def _remote(src, dst, send_sems, recv_sems, k, to):
    return pltpu.make_async_remote_copy(src_ref=src, dst_ref=dst, send_sem=send_sems.at[k],
                                        recv_sem=recv_sems.at[k], device_id=to, device_id_type=MESH)


def _all_gather_chips(shards):
    n = len(shards)
    per = 6

    def body(*refs):
        ins, outs = refs[:n], refs[n:2 * n]
        send_sems, recv_sems, local_sems = refs[2 * n:]
        x, y, c = _coords()
        me = 2 * x + y
        sibling = (x, y, 1 - c)
        chips = _other_chips(x, y)
        local = [pltpu.make_async_copy(ins[t], outs[t].at[me], local_sems.at[t]) for t in range(n)]
        for cp in local:
            cp.start()
        sends = []
        for t in range(n):
            for j, (px, py) in enumerate(chips):
                cp = _remote(ins[t].at[c], outs[t].at[me, c], send_sems, recv_sems, per * t + j, (px, py, c))
                cp.start()
                sends.append(cp)
        for t in range(n):
            for j, (px, py) in enumerate(chips):
                landed = outs[t].at[2 * px + py, c]
                _remote(landed, landed, send_sems, recv_sems, per * t + j, (px, py, c)).wait_recv()
                cp = _remote(landed, landed, send_sems, recv_sems, per * t + 3 + j, sibling)
                cp.start()
                sends.append(cp)
        for t in range(n):
            for j, (px, py) in enumerate(chips):
                passed = outs[t].at[2 * px + py, 1 - c]
                _remote(passed, passed, send_sems, recv_sems, per * t + 3 + j, sibling).wait_recv()
        for cp in sends:
            cp.wait_send()
        for cp in local:
            cp.wait()

    return _call(body, tuple(shards), name="all_gather_chips",
                 in_specs=[ANY] * n, out_specs=[ANY] * n,
                 out_shape=[jax.ShapeDtypeStruct((N_CHIPS,) + s.shape, s.dtype) for s in shards],
                 scratch_shapes=[pltpu.SemaphoreType.DMA((per * n,)), pltpu.SemaphoreType.DMA((per * n,)),
                                 pltpu.SemaphoreType.DMA((n,))])


HBM = pl.BlockSpec(memory_space=pltpu.HBM)
SEM = pl.BlockSpec(memory_space=pltpu.SEMAPHORE)
DATAFLOW = pltpu.SideEffectType.DATAFLOW_SIDE_EFFECTING


def _in_hbm(a):
    return pltpu.with_memory_space_constraint(a, pltpu.HBM)


PUSH_ARRIVALS = {"gather_chips_half": N_CHIPS - 1, "pass_halves": N_CHIPS - 1, "scatter_chips": N_CHIPS - 1,
                 "sibling": 1, "gather_devices": N_DEV - 1}


def _column_half(ref3, slab, c):
    hw = ref3.shape[2] // 2
    return ref3.at[slab, :, pl.ds(pl.multiple_of(c * hw, LANES), hw)]


def _push_copies(mode, src, land, send_sems, recv_sems, t):
    x, y, c = _coords()
    chip = 2 * x + y
    if mode == "gather_chips_half":
        return [_remote(_column_half(src, chip, c), _column_half(land, chip, c), send_sems, recv_sems, t, (px, py, c))
                for px, py in _other_chips(x, y)]
    if mode == "pass_halves":
        return [_remote(_column_half(src, 2 * px + py, c), _column_half(land, 2 * px + py, c), send_sems, recv_sems, t,
                        (x, y, 1 - c)) for px, py in _other_chips(x, y)]
    if mode == "scatter_chips":
        return [_remote(src.at[2 * px + py], land.at[chip], send_sems, recv_sems, t, (px, py, c))
                for px, py in _other_chips(x, y)]
    if mode == "sibling":
        return [_remote(src, land, send_sems, recv_sems, t, (x, y, 1 - c))]
    dev = 4 * x + 2 * y + c
    return [_remote(src.at[dev], land.at[dev], send_sems, recv_sems, t, (x ^ (k >> 2), y ^ ((k >> 1) & 1), c ^ (k & 1)))
            for k in range(1, N_DEV)]


def _push_start(srcs, lands, mode, name):
    n = len(srcs)
    same = all(s is ld for s, ld in zip(srcs, lands))
    n_in = n if same else 2 * n

    def body(*refs):
        src_refs = refs[:n]
        land_refs = src_refs if same else refs[n:2 * n]
        send_sems, recv_sems = refs[n_in], refs[n_in + 1]
        token = refs[-1]
        for t in range(n):
            for cp in _push_copies(mode, src_refs[t], land_refs[t], send_sems, recv_sems, t):
                cp.start()
        token[...] = jnp.zeros_like(token)

    operands = tuple(srcs) if same else tuple(srcs) + tuple(lands)
    res = _call(
        body, [_in_hbm(a) for a in operands], name=name,
        out_shape=(pltpu.SemaphoreType.DMA((n,)), pltpu.SemaphoreType.DMA((n,)))
        + tuple(pltpu.HBM(a.shape, a.dtype) for a in operands) + (jax.ShapeDtypeStruct((SUBLANES, LANES), F32),),
        in_specs=[HBM] * n_in, out_specs=(SEM, SEM) + (HBM,) * n_in + (pl.BlockSpec(memory_space=pltpu.VMEM),),
        input_output_aliases={i: 2 + i for i in range(n_in)}, side_effects=DATAFLOW, hbm_results=False)
    send_sems, recv_sems, token = res[0], res[1], res[-1]
    srcs_thru = res[2:2 + n]
    lands_thru = srcs_thru if same else res[2 + n:2 + 2 * n]
    return send_sems, recv_sems, srcs_thru, lands_thru, token


def _push_wait(send_sems, recv_sems, ids, srcs, lands, mode, after, name):
    n = len(lands)
    same = all(s is ld for s, ld in zip(srcs, lands))
    n_in = n if same else 2 * n

    def body(*refs):
        land_refs = refs[:n] if same else refs[n:2 * n]
        send_sems, recv_sems = refs[n_in], refs[n_in + 1]
        x, y, c = _coords()
        for t in range(n):
            if mode == "sibling":
                moved = land_refs[t]
            elif mode in ("gather_chips_half", "pass_halves"):
                moved = land_refs[t].at[pl.ds(0, PUSH_ARRIVALS[mode]), :, pl.ds(0, land_refs[t].shape[2] // 2)]
            else:
                moved = land_refs[t].at[pl.ds(0, PUSH_ARRIVALS[mode])]
            arrivals = _remote(moved, moved, send_sems, recv_sems, ids[t], (x, y, c))
            arrivals.wait_send()
            arrivals.wait_recv()

    operands = tuple(lands) if same else tuple(srcs) + tuple(lands)
    res = _call(
        body, operands + (send_sems, recv_sems, after), name=name,
        out_shape=tuple(pltpu.HBM(a.shape, a.dtype) for a in operands),
        in_specs=[HBM] * n_in + [SEM, SEM, ANY], out_specs=(HBM,) * n_in,
        input_output_aliases={i: i for i in range(n_in)}, side_effects=DATAFLOW)
    return list(res) if same else (list(res[:n]), list(res[n:]))


def _sum_partials(part, landed, chip):
    _, rows, cols = part.shape
    br = _divisor_tile(rows, 16, ELEM_ROWS)

    def body(chip_ref, own_ref, a_ref, b_ref, c_ref, o_ref):
        o_ref[...] = ((own_ref[...].astype(F32) + a_ref[...].astype(F32)) + b_ref[...].astype(F32)) \
            + c_ref[...].astype(F32)

    def other(k):
        return pl.BlockSpec((None, br, cols), lambda i, ch: (jnp.where(ch[0] <= k, k + 1, k), i, 0))

    spec = pltpu.PrefetchScalarGridSpec(
        num_scalar_prefetch=1, grid=(rows // br,),
        in_specs=[pl.BlockSpec((None, br, cols), lambda i, ch: (ch[0], i, 0)), other(0), other(1), other(2)],
        out_specs=pl.BlockSpec((br, cols), lambda i, ch: (i, 0)))
    return _call(body, (chip, part, landed, landed, landed), name="sum_partials", grid_spec=spec,
                 out_shape=jax.ShapeDtypeStruct((rows, cols), F32), semantics=("parallel",))


def _cast_to_slab(w, l, chip):
    _, rows, cols = w.shape
    br = _divisor_tile(rows, 16, ELEM_ROWS)

    def body(chip_ref, w_ref, o_ref):
        o_ref[...] = w_ref[...].astype(BF16)

    spec = pltpu.PrefetchScalarGridSpec(
        num_scalar_prefetch=1, grid=(rows // br,),
        in_specs=[pl.BlockSpec((None, br, cols), lambda i, ch: (l, i, 0))],
        out_specs=pl.BlockSpec((None, br, cols), lambda i, ch: (ch[0], i, 0)))
    return _call(body, (chip, w), name="cast_to_slab", grid_spec=spec,
                 out_shape=jax.ShapeDtypeStruct((N_CHIPS, rows, cols), BF16), semantics=("parallel",))


def _cast_w_in_t_to_slabs(w_t, chip):
    rows, depth, d = w_t.shape
    tn = _divisor_tile(d, LANES, 256)

    def body(chip_ref, w_ref, *o_refs):
        for l in range(depth):
            o_refs[l][...] = w_ref[:, l, :].astype(BF16)

    spec = pltpu.PrefetchScalarGridSpec(
        num_scalar_prefetch=1, grid=(d // tn,),
        in_specs=[pl.BlockSpec((rows, depth, tn), lambda j, ch: (0, 0, j))],
        out_specs=[pl.BlockSpec((None, rows, tn), lambda j, ch: (ch[0], 0, j))] * depth)
    return _call(body, (chip, w_t), name="cast_w_in_t_to_slabs", grid_spec=spec,
                 out_shape=[jax.ShapeDtypeStruct((N_CHIPS, rows, d), BF16)] * depth, semantics=("parallel",),
                 vmem_bytes=4 * _nbytes((rows, max(depth, SUBLANES), tn), F32))


def _place_slab(buf, index, n_slabs):
    rows, cols = buf.shape
    br = _divisor_tile(rows, SUBLANES, ELEM_ROWS)

    def body(index_ref, b_ref, o_ref):
        o_ref[...] = b_ref[...]

    spec = pltpu.PrefetchScalarGridSpec(
        num_scalar_prefetch=1, grid=(rows // br,),
        in_specs=[pl.BlockSpec((br, cols), lambda i, ix: (i, 0))],
        out_specs=pl.BlockSpec((None, br, cols), lambda i, ix: (ix[0], i, 0)))
    return _call(body, (index, buf), name="place_slab", grid_spec=spec,
                 out_shape=jax.ShapeDtypeStruct((n_slabs, rows, cols), buf.dtype), semantics=("parallel",))


ELEM_ROWS = 256


def _sum_slabs(r):
    n, rows, cols = r.shape
    br = _divisor_tile(rows, 16, ELEM_ROWS)

    def body(r_ref, o_ref):
        acc = r_ref[0].astype(F32)
        for j in range(1, n):
            acc = acc + r_ref[j].astype(F32)
        o_ref[...] = acc

    return _call(body, (r,), name="sum_slabs", grid=(rows // br,),
                 in_specs=[pl.BlockSpec((n, br, cols), lambda i: (0, i, 0))],
                 out_specs=pl.BlockSpec((br, cols), lambda i: (i, 0)),
                 out_shape=jax.ShapeDtypeStruct((rows, cols), F32), semantics=("parallel",))


def _adamw_math(w, g, m, v):
    c1 = 1.0 - ADAM_B1 ** ADAM_STEP
    c2 = 1.0 - ADAM_B2 ** ADAM_STEP
    nm = ADAM_B1 * m + (1.0 - ADAM_B1) * g
    nv = ADAM_B2 * v + (1.0 - ADAM_B2) * (g * g)
    delta = -ADAM_LR * ((nm / c1) / (jnp.sqrt(nv / c2) + ADAM_EPS) + ADAM_WD * w)
    return delta, nm, nv


def _adamw(w, g, m, v):
    rows, cols = w.shape
    br = _divisor_tile(rows, 8, ELEM_ROWS)

    def body(w_ref, g_ref, m_ref, v_ref, d_ref, nm_ref, nv_ref):
        d_ref[...], nm_ref[...], nv_ref[...] = _adamw_math(w_ref[...], g_ref[...], m_ref[...], v_ref[...])

    blk = pl.BlockSpec((br, cols), lambda i: (i, 0))
    return _call(body, (w, g, m, v), name="adamw", grid=(rows // br,),
                 in_specs=[blk] * 4, out_specs=[blk] * 3,
                 out_shape=[jax.ShapeDtypeStruct((rows, cols), F32)] * 3, semantics=("parallel",))


def _adamw_w_in_t(w_t, m_t, v_t, g_mine, g_theirs):
    rows, depth, d = w_t.shape
    tn = LANES

    def body(w_ref, m_ref, v_ref, *rest):
        ga_refs, gb_refs = rest[:depth], rest[depth:2 * depth]
        g_ref, d_ref, nm_ref, nv_ref = rest[2 * depth:]
        for l in range(depth):
            g_ref[:, l, :] = ga_refs[l][...] + gb_refs[l][...]
        d_ref[...], nm_ref[...], nv_ref[...] = _adamw_math(w_ref[...], g_ref[...], m_ref[...], v_ref[...])

    slab = pl.BlockSpec((rows, depth, tn), lambda j: (0, 0, j))
    gblk = pl.BlockSpec((rows, tn), lambda j: (0, j))
    return _call(body, (w_t, m_t, v_t) + tuple(g_mine) + tuple(g_theirs), name="adamw_w_in_t", grid=(d // tn,),
                 in_specs=[slab] * 3 + [gblk] * (2 * depth), out_specs=[slab] * 4,
                 out_shape=[jax.ShapeDtypeStruct(w_t.shape, F32)] * 4, semantics=("parallel",),
                 vmem_bytes=2 * (7 * _nbytes((rows, max(depth, SUBLANES), tn), F32)
                                 + 2 * depth * _nbytes((rows, tn), F32)))


def _adamw_layer(w, m, v, l, g_mine, g_theirs, prev, after):
    _, rows, cols = w.shape
    br = _divisor_tile(rows, 8, ELEM_ROWS)

    def body(w_ref, m_ref, v_ref, ga_ref, gb_ref, *rest):
        g_ref, d_ref, nm_ref, nv_ref = rest[5:]
        g = ga_ref[...] + gb_ref[...]
        g_ref[...] = g
        d_ref[...], nm_ref[...], nv_ref[...] = _adamw_math(w_ref[...], g, m_ref[...], v_ref[...])

    slot = pl.BlockSpec((None, br, cols), lambda i: (l, i, 0))
    blk = pl.BlockSpec((br, cols), lambda i: (i, 0))
    return _call(body, (w, m, v, g_mine, g_theirs) + tuple(prev) + (after,), name="adamw_layer",
                 grid=(rows // br,), in_specs=[slot] * 3 + [blk] * 2 + [ANY] * 5, out_specs=[slot] * 4,
                 out_shape=[jax.ShapeDtypeStruct(w.shape, F32)] * 4,
                 input_output_aliases={5: 0, 6: 1, 7: 2, 8: 3}, semantics=("parallel",))


BIG = ("w_in", "w_out", "w_up", "w_down")
WEIGHTS = ("meta", "attn_norm_g", "w_in", "b_f", "conv_w", "conv_b", "w_gate_a", "b_gate_a", "w_gate_x",
           "b_gate_x", "lru_L", "attn_out_g", "rec_out_g", "w_out", "mlp_norm_g", "w_up", "w_down", "final_g")
SMALL = tuple(k for k in WEIGHTS if k not in BIG)
COL_SHARDED_SMALL = ("meta", "conv_w")


def _packed_rows(shape):
    return -(-math.prod(shape) // (SUBLANES * LANES)) * SUBLANES


def _pack(arrs):
    rows = []
    for a in arrs:
        flat = a.reshape(-1)
        rows.append(jnp.pad(flat, (0, _packed_rows(a.shape) * LANES - flat.shape[0])).reshape(-1, LANES))
    used = sum(r.shape[0] for r in rows)
    rows.append(jnp.zeros((-used % ELEM_ROWS, LANES), F32))
    return jnp.concatenate(rows, axis=0)


def _unpack(buf, shapes):
    out, r0 = [], 0
    for s in shapes:
        nr = _packed_rows(s)
        out.append(buf[r0:r0 + nr].reshape(-1)[:math.prod(s)].reshape(s))
        r0 += nr
    return out


def _halves(a):
    return a.reshape((2, a.shape[0] // 2) + a.shape[1:])


def _cols_from_chips(g):
    return jnp.moveaxis(g, 0, -2).reshape(g.shape[1:-1] + (N_CHIPS * g.shape[-1],))


def kernel(x, meta, attn_norm_g, w_in, b_f, conv_w, conv_b, w_gate_a, b_gate_a, w_gate_x, b_gate_x, lru_L, attn_out_g, rec_out_g, w_out, mlp_norm_g, w_up, w_down, final_g, loss_target, m_meta, m_attn_norm_g, m_w_in, m_b_f, m_conv_w, m_conv_b, m_w_gate_a, m_b_gate_a, m_w_gate_x, m_b_gate_x, m_lru_L, m_attn_out_g, m_rec_out_g, m_w_out, m_mlp_norm_g, m_w_up, m_w_down, m_final_g, v_meta, v_attn_norm_g, v_w_in, v_b_f, v_conv_w, v_conv_b, v_w_gate_a, v_b_gate_a, v_w_gate_x, v_b_gate_x, v_lru_L, v_attn_out_g, v_rec_out_g, v_w_out, v_mlp_norm_g, v_w_up, v_w_down, v_final_g):
    w = dict(meta=meta, attn_norm_g=attn_norm_g, w_in=w_in, b_f=b_f, conv_w=conv_w, conv_b=conv_b,
             w_gate_a=w_gate_a, b_gate_a=b_gate_a, w_gate_x=w_gate_x, b_gate_x=b_gate_x, lru_L=lru_L,
             attn_out_g=attn_out_g, rec_out_g=rec_out_g, w_out=w_out, mlp_norm_g=mlp_norm_g, w_up=w_up,
             w_down=w_down, final_g=final_g)
    m = dict(meta=m_meta, attn_norm_g=m_attn_norm_g, w_in=m_w_in, b_f=m_b_f, conv_w=m_conv_w, conv_b=m_conv_b,
             w_gate_a=m_w_gate_a, b_gate_a=m_b_gate_a, w_gate_x=m_w_gate_x, b_gate_x=m_b_gate_x, lru_L=m_lru_L,
             attn_out_g=m_attn_out_g, rec_out_g=m_rec_out_g, w_out=m_w_out, mlp_norm_g=m_mlp_norm_g,
             w_up=m_w_up, w_down=m_w_down, final_g=m_final_g)
    v = dict(meta=v_meta, attn_norm_g=v_attn_norm_g, w_in=v_w_in, b_f=v_b_f, conv_w=v_conv_w, conv_b=v_conv_b,
             w_gate_a=v_w_gate_a, b_gate_a=v_b_gate_a, w_gate_x=v_w_gate_x, b_gate_x=v_b_gate_x, lru_L=v_lru_L,
             attn_out_g=v_attn_out_g, rec_out_g=v_rec_out_g, w_out=v_w_out, mlp_norm_g=v_mlp_norm_g,
             w_up=v_w_up, w_down=v_w_down, final_g=v_final_g)
    s_len, d = x.shape[1], x.shape[2]
    depth = w_in.shape[0]
    chip = 2 * lax.axis_index("x") + lax.axis_index("y")

    g_conv, g_meta = [g.reshape((N_CHIPS, g.shape[1] * g.shape[2]) + g.shape[3:])
                      for g in _all_gather_chips([_halves(w["conv_w"]), _halves(w["meta"])])]
    p = dict(w)
    p["conv_w"] = _cols_from_chips(g_conv)
    meta_full = jnp.moveaxis(g_meta, 0, 1).reshape(N_META, d)

    chip1 = chip.reshape(1).astype(jnp.int32)
    w_in_t, m_in_t, v_in_t = [jnp.transpose(a["w_in"], (2, 0, 1)) for a in (w, m, v)]
    w_in_slabs = _cast_w_in_t_to_slabs(w_in_t, chip1)
    pushes, tokens = [], []
    for l in range(depth):
        slabs = [w_in_slabs[l]] + [_cast_to_slab(w[k], l, chip1) for k in BIG[1:]]
        send_sems, recv_sems, _, lands, token = _push_start(slabs, slabs, "gather_chips_half", f"weights_start_{l}")
        pushes.append((send_sems, recv_sems, lands))
        tokens.append(token[0, 0])
    passed = [{} for _ in range(depth)]

    def stage(l, after, ids):
        send_sems, recv_sems, lands = pushes[l]
        tag = "_".join(BIG[i] for i in ids)
        sub = [lands[i] for i in ids]
        sub = _push_wait(send_sems, recv_sems, ids, sub, sub, "gather_chips_half", after, f"{tag}_wait_{l}")
        send_sems, recv_sems, _, sub, token = _push_start(sub, sub, "pass_halves", f"{tag}_pass_{l}")
        for j, i in enumerate(ids):
            passed[l][i] = (send_sems, recv_sems, sub[j], j)
        return token[0, 0]

    t_len = N_META + s_len
    pad = -t_len % SEQ_TILE
    h = jnp.concatenate([meta_full, x[0], jnp.zeros((pad, d), F32)], axis=0)
    tgt = jnp.concatenate([jnp.zeros((N_META, d), F32), loss_target[0], jnp.zeros((pad, d), F32)], axis=0)
    z = _rms_fwd(h, _row(p["attn_norm_g"][0] + sum(tokens)))
    stage(0, z, [0])
    saved = []
    for l in range(depth):
        def fetch(k, after, l=l):
            send_sems, recv_sems, land, j = passed[l][BIG.index(k)]
            return _push_wait(send_sems, recv_sems, [j], [land], [land], "pass_halves", after,
                              f"{k}_here_{l}")[0]

        def stage_next(after, which, l=l):
            if which == "own":
                return stage(l, after, [1, 2, 3])
            return stage(l + 1, after, [0]) if l + 1 < depth else 0.0

        g_next = p["attn_norm_g"][l + 1] if l + 1 < depth else p["final_g"]
        h, z, sv = _forward_layer(l, h, z, p, fetch, stage_next, g_next)
        saved.append(sv)
    dh, dh_b, dg_final, loss_part = _loss_bwd(h, _row(p["final_g"]), tgt, s_len)

    small = {k: [None] * depth for k in SMALL if k not in ("meta", "final_g")}
    pushes = [None] * depth
    tok = 0.0
    for l in reversed(range(depth)):
        dh, dh_b, big_mlp, sm_mlp = _backward_mlp(l, dh, dh_b, saved[l], p, tok)
        g_out, = _grad_w_pieces([saved[l]["mix"]], dh_b)
        parts = [big_mlp["w_down"], big_mlp["w_up"], g_out.reshape((N_CHIPS, -1) + g_out.shape[1:])]
        push_mlp = _push_start(parts, [lax.empty(a.shape, a.dtype) for a in parts], "scatter_chips",
                               f"mlp_grads_start_{l}")
        dh, dh_b, big_mix, sm_mix = _backward_mixer(l, dh, dh_b, saved[l], p, push_mlp[4][0, 0])
        parts = [big_mix["w_in"]]
        push_mix = _push_start(parts, [lax.empty(a.shape, a.dtype) for a in parts], "scatter_chips",
                               f"mixer_grads_start_{l}")
        tok = push_mix[4][0, 0]
        pushes[l] = {("w_down", "w_up", "w_out"): push_mlp, ("w_in",): push_mix}
        for k, val in {**sm_mlp, **sm_mix}.items():
            small[k][l] = val
    grads = {k: jnp.stack(val) for k, val in small.items()}
    grads["final_g"] = dg_final[0]
    grads["meta"] = dh[:N_META]
    dx = dh[N_META:t_len]

    full_shapes = [grads[k].shape for k in SMALL] + [(1,)]
    packed = _pack([grads[k].astype(F32) for k in SMALL] + [loss_part[0, :1] + tok])
    dev1 = (2 * chip + lax.axis_index("c")).reshape(1).astype(jnp.int32)
    slabs = [_place_slab(packed, dev1, N_DEV)]
    small_push = _push_start(slabs, slabs, "gather_devices", "small_grads_start")

    last_token = small_push[4]
    outs = {k: [lax.empty(w[k].shape, F32) for _ in range(4)] for k in BIG[1:]}
    w_in_sums = [None] * depth
    swaps = {}

    def finish(l, wait_after, adam_after):
        send_sems, recv_sems, mine, lands, _ = swaps[l]
        mine, theirs = _push_wait(send_sems, recv_sems, list(range(len(BIG))), mine, lands, "sibling", wait_after,
                                  f"sums_wait_{l}")
        w_in_sums[l] = (mine[0], theirs[0])
        for k, a, b in zip(BIG[1:], mine[1:], theirs[1:]):
            outs[k] = _adamw_layer(w[k], m[k], v[k], l, a, b, outs[k], adam_after)

    wait_after = last_token
    for l in reversed(range(depth)):
        sums = {}
        for names, (send_sems, recv_sems, parts, lands, _) in pushes[l].items():
            parts, landed = _push_wait(send_sems, recv_sems, list(range(len(names))), parts, lands, "scatter_chips",
                                       wait_after, f"{names[0]}_grads_wait_{l}")
            for k, part, land in zip(names, parts, landed):
                sums[k] = wait_after = _sum_partials(part, land, chip1)
        mine = [sums[k] for k in BIG]
        swaps[l] = _push_start(mine, [lax.empty(a.shape, a.dtype) for a in mine], "sibling", f"sums_start_{l}")
        if l + 2 < depth:
            finish(l + 2, outs["w_down"][0] if l + 3 < depth else mine[0], swaps[l][4])
    for l in reversed(range(min(2, depth))):
        finish(l, outs["w_down"][0] if depth > 2 else swaps[0][4], swaps[0][4])
    outs["w_in"] = [jnp.transpose(r, (1, 2, 0)) for r in _adamw_w_in_t(
        w_in_t, m_in_t, v_in_t, [s[0] for s in w_in_sums], [s[1] for s in w_in_sums])]
    out_g, out_d, out_m, out_v = [{k: outs[k][i] for k in BIG} for i in range(4)]

    landed = _push_wait(small_push[0], small_push[1], [0], small_push[3], small_push[3], "gather_devices",
                        out_g["w_in"], "small_grads_wait")
    total = _sum_slabs(landed[0])
    small_g = dict(zip(SMALL + ("loss",), _unpack(total, full_shapes)))
    for k in COL_SHARDED_SMALL:
        n = w[k].shape[-1]
        small_g[k] = lax.dynamic_slice_in_dim(small_g[k], chip * n, n, axis=small_g[k].ndim - 1)
    local_shapes = [w[k].shape for k in SMALL]
    res = _adamw(_pack([w[k] for k in SMALL]), _pack([small_g[k] for k in SMALL]),
                 _pack([m[k] for k in SMALL]), _pack([v[k] for k in SMALL]))
    out_g.update({k: small_g[k] for k in SMALL})
    for dst, buf in zip((out_d, out_m, out_v), res):
        dst.update(zip(SMALL, _unpack(buf, local_shapes)))

    return (small_g["loss"].reshape(()), dx[None],
            *[out_g[k] for k in WEIGHTS], *[out_d[k] for k in WEIGHTS],
            *[out_m[k] for k in WEIGHTS], *[out_v[k] for k in WEIGHTS])
```

```python
import functools
import math

import jax
import jax.numpy as jnp
from jax import lax
from jax.experimental import pallas as pl
from jax.experimental.pallas import tpu as pltpu

F32 = jnp.float32
BF16 = jnp.bfloat16

N_META = 16
HEAD_DIM = 64
N_REC_BLOCKS = 8
CONV_WIDTH = 4
RG_C = 8.0
NORM_EPS = 1e-6
ADAM_LR = 0.001
ADAM_B1 = 0.9
ADAM_B2 = 0.999
ADAM_EPS = 1e-08
ADAM_WD = 0.01
ADAM_STEP = 10

LANES = 128
SUBLANES = 8
SEQ_TILE = 128
VMEM_CAP = 60 * 2**20
VMEM_SLACK = 6 * 2**20
NEG_BIG = -1e30
N_CHIPS = 4
N_DEV = 8
MESH = pl.DeviceIdType.MESH


def _nbytes(shape, dtype):
    return math.prod(shape) * jnp.dtype(dtype).itemsize


def _call(body, args, *, name, out_shape, grid=(), in_specs=None, out_specs=None, scratch_shapes=(),
          grid_spec=None, semantics=None, vmem_bytes=None, side_effects=None, hbm_results=True, **kw):
    cp = {}
    if semantics is not None:
        cp["dimension_semantics"] = semantics
    if vmem_bytes is not None:
        cp["vmem_limit_bytes"] = int(min(VMEM_CAP, vmem_bytes + VMEM_SLACK))
    if side_effects is not None:
        cp["has_side_effects"] = side_effects
    if grid_spec is not None:
        kw["grid_spec"] = grid_spec
    else:
        kw.update(grid=grid, in_specs=in_specs, out_specs=out_specs, scratch_shapes=scratch_shapes)
    if hbm_results:
        out_shape = jax.tree.map(
            lambda s: pltpu.HBM(s.shape, s.dtype) if isinstance(s, jax.ShapeDtypeStruct) else s, out_shape)
    fn = pl.pallas_call(
        body, name=name, out_shape=out_shape,
        compiler_params=pltpu.CompilerParams(**cp), **kw)
    return fn(*[_in_hbm(a) if jnp.issubdtype(getattr(a, "dtype", jnp.int32), jnp.floating) else a for a in args])


def _divisor_tile(n, unit, target):
    best = None
    for t in range(unit, min(n, target) + 1, unit):
        if n % t == 0:
            best = t
    return n if best is None else best


def _sigmoid(x):
    return 1.0 / (1.0 + jnp.exp(-x))


def _log1p_unit(e):
    series = e * (1.0 - e * (0.5 - e * (1.0 / 3.0)))
    return jnp.where(e < 1e-2, series, jnp.log(1.0 + e))


def _log_sigmoid(x):
    return jnp.minimum(x, 0.0) - _log1p_unit(jnp.exp(-jnp.abs(x)))


def _one_minus_exp(x, exp_x):
    small = -x * (1.0 + x * (1.0 / 2 + x * (1.0 / 6 + x * (1.0 / 24 + x * (1.0 / 120 + x * (1.0 / 720))))))
    return jnp.where(x > -0.25, small, 1.0 - exp_x)


_GELU_K = math.sqrt(2.0 / math.pi)
_GELU_C = 0.044715


def _gelu_and_grad(y):
    th = jnp.tanh(_GELU_K * (y + _GELU_C * y * y * y))
    g = 0.5 * y * (1.0 + th)
    dg = 0.5 * (1.0 + th) + 0.5 * y * (1.0 - th * th) * _GELU_K * (1.0 + 3.0 * _GELU_C * y * y)
    return g, dg


def _rstd(x):
    return lax.rsqrt(jnp.mean(x * x, axis=-1, keepdims=True) + NORM_EPS)


def _rms_bwd(dz, x, g):
    rs = _rstd(x)
    xh = x * rs
    dgp = jnp.sum(dz * xh, axis=0, keepdims=True)
    dxh = dz * g
    dx = rs * (dxh - xh * jnp.mean(dxh * xh, axis=-1, keepdims=True))
    return dx, dgp


def _dot(a, b):
    return jnp.dot(a, b, preferred_element_type=F32)


def _dot_nt(a, b):
    return lax.dot_general(a, b, (((1,), (1,)), ((), ())), preferred_element_type=F32)


def _dot_tn(a, b):
    return lax.dot_general(a, b, (((0,), (0,)), ((), ())), preferred_element_type=F32)


def _full(shape):
    nd = len(shape)
    return pl.BlockSpec(shape, lambda *_: (0,) * nd)


def _rms_fwd(h, g):
    tp, d = h.shape
    tm = _divisor_tile(tp, 16, 544)

    def body(h_ref, g_ref, z_ref):
        x = h_ref[...]
        z_ref[...] = (x * _rstd(x) * g_ref[...]).astype(BF16)

    return _call(body, (h, g), name="rms_fwd", grid=(tp // tm,),
                 in_specs=[pl.BlockSpec((tm, d), lambda i: (i, 0)), _full((1, d))],
                 out_specs=pl.BlockSpec((tm, d), lambda i: (i, 0)),
                 out_shape=jax.ShapeDtypeStruct((tp, d), BF16), semantics=("parallel",))


def _proj(z, w_big_t, att_w):
    tp, d = z.shape
    nb = w_big_t.shape[0]
    tn = _divisor_tile(nb, LANES, 512)
    assert (3 * att_w) % tn == 0
    n_qkv = 3 * att_w // tn
    scale = 1.0 / math.sqrt(HEAD_DIM)

    def body(z_ref, w_ref, qkv_ref, p_ref):
        j = pl.program_id(0)
        acc = _dot_nt(z_ref[...], w_ref[...])

        @pl.when(j < n_qkv)
        def _():
            col = j * tn + lax.broadcasted_iota(jnp.int32, (1, tn), 1)
            qkv_ref[...] = (acc * jnp.where(col < att_w, scale, 1.0)).astype(BF16)

        @pl.when(j >= n_qkv)
        def _():
            p_ref[...] = acc

    vm = 2 * (_nbytes((tp, d), BF16) + _nbytes((d, tn), BF16) + _nbytes((tp, tn), F32) * 2)
    return _call(body, (z, w_big_t), name="proj", grid=(nb // tn,),
                 in_specs=[_full((tp, d)), pl.BlockSpec((tn, d), lambda j: (j, 0))],
                 out_specs=[pl.BlockSpec((tp, tn), lambda j: (0, jnp.minimum(j, n_qkv - 1))),
                            pl.BlockSpec((tp, tn), lambda j: (0, jnp.maximum(j - n_qkv, 0)))],
                 out_shape=[jax.ShapeDtypeStruct((tp, 3 * att_w), BF16),
                            jax.ShapeDtypeStruct((tp, nb - 3 * att_w), F32)],
                 semantics=("arbitrary",), vmem_bytes=vm)


def _tile_cumsum(x, row, reverse=False):
    for s in (1, 2, 4):
        if reverse:
            x = x + jnp.where(row < SUBLANES - s, pltpu.roll(x, SUBLANES - s, 0), 0.0)
        else:
            x = x + jnp.where(row >= s, pltpu.roll(x, s, 0), 0.0)
    return x


def _fgate_fwd(proj, b_f_pad, nh):
    tp, nb = proj.shape
    fblk = nb // LANES - 1

    def body(f_ref, b_ref, c_ref, ct_ref):
        b = b_ref[...]
        row = lax.broadcasted_iota(jnp.int32, (SUBLANES, LANES), 0)

        def step(i, carry):
            r0 = pl.multiple_of(i * SUBLANES, SUBLANES)
            lf = _log_sigmoid(f_ref[pl.ds(r0, SUBLANES), :] + b)
            x = _tile_cumsum(lf, row) + carry
            c_ref[pl.ds(r0, SUBLANES), :] = x
            return x[SUBLANES - 1:SUBLANES, :]

        lax.fori_loop(0, tp // SUBLANES, step, jnp.zeros((1, LANES), F32))
        ct_ref[...] = c_ref[...].T[:nh, :]

    return _call(body, (proj, b_f_pad), name="fgate_fwd", grid=(1,),
                 in_specs=[pl.BlockSpec((tp, LANES), lambda i: (0, fblk)), _full((1, LANES))],
                 out_specs=[_full((tp, LANES)), _full((nh, tp))],
                 out_shape=[jax.ShapeDtypeStruct((tp, LANES), F32), jax.ShapeDtypeStruct((nh, tp), F32)],
                 semantics=("arbitrary",))


def _fgate_bwd(proj, b_f_pad, dc):
    tp, nb = proj.shape
    fblk = nb // LANES - 1

    def body(f_ref, b_ref, dc_ref, df_ref, db_ref, dc_s):
        b = b_ref[...]
        row = lax.broadcasted_iota(jnp.int32, (SUBLANES, LANES), 0)
        nt = tp // SUBLANES

        def step(i, carry):
            suffix, acc = carry
            r0 = pl.multiple_of((nt - 1 - i) * SUBLANES, SUBLANES)
            dlf = _tile_cumsum(dc_ref[pl.ds(r0, SUBLANES), :], row, reverse=True) + suffix
            df = dlf * _sigmoid(-(f_ref[pl.ds(r0, SUBLANES), :] + b))
            dc_s[pl.ds(r0, SUBLANES), :] = df
            return dlf[0:1, :], acc + df

        _, acc = lax.fori_loop(0, nt, step, (jnp.zeros((1, LANES), F32), jnp.zeros((SUBLANES, LANES), F32)))
        df_ref[...] = dc_s[...].astype(BF16)
        db_ref[...] = jnp.broadcast_to(jnp.sum(acc, axis=0, keepdims=True), (SUBLANES, LANES))

    return _call(body, (proj, b_f_pad, dc), name="fgate_bwd", grid=(1,),
                 in_specs=[pl.BlockSpec((tp, LANES), lambda i: (0, fblk)), _full((1, LANES)), _full((tp, LANES))],
                 out_specs=[_full((tp, LANES)), _full((SUBLANES, LANES))],
                 out_shape=[jax.ShapeDtypeStruct((tp, LANES), BF16),
                            jax.ShapeDtypeStruct((SUBLANES, LANES), F32)],
                 scratch_shapes=[pltpu.VMEM((tp, LANES), F32)], semantics=("arbitrary",))


ATT_BQ = 128


ATT_BUCKET = 2
ATT_HEADS = 4


def _for_bucket(i, nq, fn):
    for lo in range(0, nq, ATT_BUCKET):
        hi = min(lo + ATT_BUCKET, nq)
        spans = ([(0, lo * ATT_BQ, False)] if lo else []) + [(lo * ATT_BQ, hi * ATT_BQ, True)]
        pl.when(jnp.logical_and(i >= lo, i < hi))(functools.partial(fn, spans))


def _head_column(c_blk, h):
    lane = lax.broadcasted_iota(jnp.int32, c_blk.shape, 1)
    return jnp.sum(jnp.where(lane == h, c_blk, 0.0), axis=1, keepdims=True)


def _head_columns_into(p, c_ref, ck_s):
    for hh in range(ATT_HEADS):
        ck_s[hh] = jnp.broadcast_to(_head_column(c_ref[...], ATT_HEADS * p + hh), ck_s.shape[1:])


def _pair_diag_cols(x2):
    top = lax.broadcasted_iota(jnp.int32, (LANES, ATT_BQ), 0) < HEAD_DIM
    xt = x2.astype(F32).T.astype(BF16)
    return jnp.concatenate([jnp.where(top, xt, 0), jnp.where(top, 0, xt)], axis=1)


def _pair_diag_rows(x2):
    low = lax.broadcasted_iota(jnp.int32, (ATT_BQ, LANES), 1) < HEAD_DIM
    return jnp.concatenate([jnp.where(low, x2, 0), jnp.where(low, 0, x2)], axis=0)


def _seen_keys(k0, k1, q0):
    keys = k0 + lax.broadcasted_iota(jnp.int32, (k1 - k0, ATT_BQ), 0)
    return keys <= q0 + lax.broadcasted_iota(jnp.int32, (k1 - k0, ATT_BQ), 1)


def _attn_fwd(qkv, c, c_t, nh):
    tp = qkv.shape[0]
    att_w = nh * HEAD_DIM
    ng = nh // ATT_HEADS
    gw = ATT_HEADS * HEAD_DIM
    bq = ATT_BQ
    nq = tp // bq
    pair = 2 * HEAD_DIM
    assert pair == LANES and ATT_HEADS % 2 == 0

    def body(q_ref, k_ref, v_ref, c_ref, ct_ref, o_ref, lse_ref, ck_s, vt_s):
        p = pl.program_id(0)
        i = pl.program_id(1)

        @pl.when(i == 0)
        def _():
            _head_columns_into(p, c_ref, ck_s)
            vt_s[...] = v_ref[...].astype(F32).T.astype(BF16)

        def compute(spans):
            q0 = pl.multiple_of(i * bq, bq)
            o_t, lses = [], []
            for pi in range(ATT_HEADS // 2):
                lo = pair * pi
                heads = (2 * pi, 2 * pi + 1)
                q_cols = _pair_diag_cols(q_ref[:, lo:lo + pair])
                ts = []
                for k0, k1, needs_mask in spans:
                    t2 = _dot(k_ref[k0:k1, lo:lo + pair], q_cols)
                    t_e = [t2[:, e * bq:(e + 1) * bq] - ck_s[hh, k0:k1, :] for e, hh in enumerate(heads)]
                    if needs_mask:
                        seen = _seen_keys(k0, k1, q0)
                        t_e = [jnp.where(seen, t, NEG_BIG) for t in t_e]
                    ts.append(t_e)
                ms = [functools.reduce(jnp.maximum, [jnp.max(t[e], axis=0, keepdims=True) for t in ts])
                      for e in range(2)]
                es = [[jnp.exp(t[e] - ms[e]) for e in range(2)] for t in ts]
                ls = [sum(jnp.sum(e_[e], axis=0, keepdims=True) for e_ in es) for e in range(2)]
                o2 = sum(_dot(vt_s[lo:lo + pair, k0:k1],
                              jnp.concatenate([e_[0].astype(BF16), e_[1].astype(BF16)], axis=1))
                         for e_, (k0, k1, _) in zip(es, spans))
                o_t += [o2[:HEAD_DIM, :bq] / ls[0], o2[HEAD_DIM:, bq:] / ls[1]]
                lses += [ms[e] + ct_ref[pl.ds(ATT_HEADS * p + hh, 1), :] + jnp.log(ls[e])
                         for e, hh in enumerate(heads)]
            o_ref[...] = jnp.concatenate(o_t, axis=0).T
            lse_ref[...] = jnp.concatenate(lses, axis=0)

        _for_bucket(i, nq, compute)

    blk = pl.BlockSpec((bq, gw), lambda p, i: (i, p))
    vm = 6 * _nbytes((tp, gw), BF16) + 2 * ATT_HEADS * _nbytes((tp, LANES), F32) + 2 * _nbytes((tp, LANES), F32) \
        + 8 * ATT_HEADS * _nbytes((bq, tp), F32)
    return _call(body, (qkv, qkv, qkv, c, c_t), name="attn_fwd", grid=(ng, nq),
                 in_specs=[blk,
                           pl.BlockSpec((tp, gw), lambda p, i: (0, ng + p)),
                           pl.BlockSpec((tp, gw), lambda p, i: (0, 2 * ng + p)),
                           _full((tp, LANES)), pl.BlockSpec((nh, bq), lambda p, i: (0, i))],
                 out_specs=[blk, pl.BlockSpec((None, ATT_HEADS, bq), lambda p, i: (p, 0, i))],
                 out_shape=[jax.ShapeDtypeStruct((tp, att_w), F32), jax.ShapeDtypeStruct((ng, ATT_HEADS, tp), F32)],
                 scratch_shapes=[pltpu.VMEM((ATT_HEADS, tp, LANES), F32), pltpu.VMEM((gw, tp), BF16)],
                 semantics=("arbitrary", "arbitrary"), vmem_bytes=vm)


def _attn_bwd(qkv, c, c_t, lse, do, nh):
    tp = qkv.shape[0]
    att_w = nh * HEAD_DIM
    ng = nh // ATT_HEADS
    gw = ATT_HEADS * HEAD_DIM
    bq = ATT_BQ
    nq = tp // bq
    pair = 2 * HEAD_DIM
    assert pair == LANES and ATT_HEADS % 2 == 0
    scale = 1.0 / math.sqrt(HEAD_DIM)

    def body(q_ref, k_ref, v_ref, c_ref, ct_ref, lse_ref, do_ref, dq_ref, dk_ref, dv_ref, dc_ref,
             dk_s, dv_s, dc_s, ck_s, kt_s):
        p = pl.program_id(0)
        i = pl.program_id(1)

        @pl.when(i == 0)
        def _():
            dk_s[...] = jnp.zeros_like(dk_s)
            dv_s[...] = jnp.zeros_like(dv_s)
            dc_s[...] = jnp.zeros_like(dc_s)
            kt_s[...] = k_ref[...].astype(F32).T.astype(BF16)
            _head_columns_into(p, c_ref, ck_s)

        @pl.when(jnp.logical_and(i == 0, p == 0))
        def _():
            dc_ref[...] = jnp.zeros_like(dc_ref)

        def compute(spans):
            q0 = pl.multiple_of(i * bq, bq)
            dq_t = []
            for pi in range(ATT_HEADS // 2):
                lo = pair * pi
                q2 = q_ref[:, lo:lo + pair]
                do2 = do_ref[:, lo:lo + pair].astype(BF16)
                q_cols, do_cols = _pair_diag_cols(q2), _pair_diag_cols(do2)
                q_rows, do_rows = _pair_diag_rows(q2), _pair_diag_rows(do2)
                heads = (2 * pi, 2 * pi + 1)
                col_terms = [ct_ref[pl.ds(ATT_HEADS * p + hh, 1), :] - lse_ref[hh:hh + 1, :] for hh in heads]
                prs, dps = [], []
                for k0, k1, needs_mask in spans:
                    t2 = _dot(k_ref[k0:k1, lo:lo + pair], q_cols)
                    dp2 = _dot(v_ref[k0:k1, lo:lo + pair], do_cols)
                    if needs_mask:
                        seen = _seen_keys(k0, k1, q0)
                    pr_e, dp_e = [], []
                    for e, hh in enumerate(heads):
                        t = t2[:, e * bq:(e + 1) * bq] - ck_s[hh, k0:k1, :]
                        if needs_mask:
                            t = jnp.where(seen, t, NEG_BIG)
                        pr_e.append(jnp.exp(t + col_terms[e]))
                        dp_e.append(dp2[:, e * bq:(e + 1) * bq])
                    prs.append(pr_e)
                    dps.append(dp_e)
                key_sums = [sum(jnp.sum(pr[e] * dp[e], axis=0, keepdims=True) for pr, dp in zip(prs, dps))
                            for e in range(2)]
                dq2 = 0.0
                for (k0, k1, _), pr, dp in zip(spans, prs, dps):
                    ds = [pr[e] * (dp[e] - key_sums[e]) for e in range(2)]
                    for e, hh in enumerate(heads):
                        dc_s[hh, k0:k1, :] += jnp.sum(ds[e], axis=1, keepdims=True)
                    ds2 = jnp.concatenate([ds[0].astype(BF16), ds[1].astype(BF16)], axis=1)
                    pr2 = jnp.concatenate([pr[0].astype(BF16), pr[1].astype(BF16)], axis=1)
                    dk_s[k0:k1, lo:lo + pair] += _dot(ds2, q_rows)
                    dv_s[k0:k1, lo:lo + pair] += _dot(pr2, do_rows)
                    dq2 = dq2 + _dot(kt_s[lo:lo + pair, k0:k1], ds2)
                dq_t.append(jnp.concatenate([dq2[:HEAD_DIM, :bq], dq2[HEAD_DIM:, bq:]], axis=0))
            dq_ref[...] = (jnp.concatenate(dq_t, axis=0) * scale).T.astype(BF16)

        _for_bucket(i, nq, compute)

        @pl.when(i == nq - 1)
        def _():
            dk_ref[...] = dk_s[...].astype(BF16)
            dv_ref[...] = dv_s[...].astype(BF16)
            lane = lax.broadcasted_iota(jnp.int32, (tp, LANES), 1)
            dc = dc_ref[...]
            for hh in range(ATT_HEADS):
                dc = jnp.where(lane == ATT_HEADS * p + hh, -dc_s[hh], dc)
            dc_ref[...] = dc

    blk = pl.BlockSpec((bq, gw), lambda p, i: (i, p))
    col = pl.BlockSpec((tp, gw), lambda p, i: (0, p))
    vm = 7 * _nbytes((tp, gw), BF16) + 2 * _nbytes((tp, gw), F32) + 2 * ATT_HEADS * _nbytes((tp, LANES), F32) \
        + 2 * _nbytes((tp, LANES), F32) + 12 * ATT_HEADS * _nbytes((bq, tp), F32)
    return _call(body, (qkv, qkv, qkv, c, c_t, lse, do), name="attn_bwd", grid=(ng, nq),
                 in_specs=[blk,
                           pl.BlockSpec((tp, gw), lambda p, i: (0, ng + p)),
                           pl.BlockSpec((tp, gw), lambda p, i: (0, 2 * ng + p)),
                           _full((tp, LANES)), pl.BlockSpec((nh, bq), lambda p, i: (0, i)),
                           pl.BlockSpec((None, ATT_HEADS, bq), lambda p, i: (p, 0, i)), blk],
                 out_specs=[blk, col, col, _full((tp, LANES))],
                 out_shape=[jax.ShapeDtypeStruct((tp, att_w), BF16)] * 3 + [jax.ShapeDtypeStruct((tp, LANES), F32)],
                 scratch_shapes=[pltpu.VMEM((tp, gw), F32), pltpu.VMEM((tp, gw), F32),
                                 pltpu.VMEM((ATT_HEADS, tp, 1), F32), pltpu.VMEM((ATT_HEADS, tp, LANES), F32),
                                 pltpu.VMEM((gw, tp), BF16)],
                 semantics=("arbitrary", "arbitrary"), vmem_bytes=vm)


REC_ROWS = 128
HALO = SUBLANES


def _conv_taps(cat):
    taps = []
    for k in range(CONV_WIDTH):
        sh = CONV_WIDTH - 1 - k
        taps.append((pltpu.roll(cat, sh, 0) if sh else cat)[HALO:])
    return taps


def _rec_gates(xc, wa_ref, ba_ref, wx_ref, bx_ref, l_ref):
    xcb = xc.astype(BF16)
    r = _sigmoid(_dot(xcb, wa_ref[...]) + ba_ref[...])
    ig = _sigmoid(_dot(xcb, wx_ref[...]) + bx_ref[...])
    ls = _log_sigmoid(l_ref[...])
    log_a = RG_C * r * ls
    return xcb, r, ig, ls, log_a


def _rec_fwd(proj, xr_blk, yr_blk, rec_w, conv_w, conv_b, wa, ba, wx, bx, lru):
    tp = proj.shape[0]
    w = rec_w
    r_rows = REC_ROWS
    nc = tp // r_rows

    def body(xr_ref, yr_ref, cw_ref, cb_ref, wa_ref, ba_ref, wx_ref, bx_ref, l_ref,
             hr_ref, rec_ref, prev_s, carry_s, a_s, u_s):
        i = pl.program_id(0)

        @pl.when(i == 0)
        def _():
            prev_s[...] = jnp.zeros_like(prev_s)
            carry_s[...] = jnp.zeros_like(carry_s)

        x = xr_ref[...]
        taps = _conv_taps(jnp.concatenate([prev_s[...], x], axis=0))
        prev_s[...] = x[r_rows - HALO:]
        xc = cb_ref[...]
        for k in range(CONV_WIDTH):
            xc = xc + cw_ref[k:k + 1, :] * taps[k]
        _, r, ig, ls, log_a = _rec_gates(xc, wa_ref, ba_ref, wx_ref, bx_ref, l_ref)
        a = jnp.exp(log_a)
        a_s[...] = a
        u_s[...] = jnp.sqrt(_one_minus_exp(2.0 * log_a, a * a)) * ig * xc

        def tile(j, h):
            r0 = pl.multiple_of(j * SUBLANES, SUBLANES)
            at = a_s[pl.ds(r0, SUBLANES), :]
            ut = u_s[pl.ds(r0, SUBLANES), :]
            out = []
            for rr in range(SUBLANES):
                h = at[rr:rr + 1] * h + ut[rr:rr + 1]
                out.append(h)
            hr_ref[pl.ds(r0, SUBLANES), :] = jnp.concatenate(out, axis=0)
            return h

        carry_s[0:1, :] = lax.fori_loop(0, r_rows // SUBLANES, tile, carry_s[0:1, :])
        g, _ = _gelu_and_grad(yr_ref[...])
        rec_ref[...] = hr_ref[...] * g

    blk = pl.BlockSpec((r_rows, w), lambda i: (i, 0))
    vm = 16 * _nbytes((r_rows, w), F32) + 4 * _nbytes((w, w), BF16)
    return _call(body, (proj, proj, conv_w, conv_b, wa, ba, wx, bx, lru), name="rec_fwd", grid=(nc,),
                 in_specs=[pl.BlockSpec((r_rows, w), lambda i: (i, xr_blk)),
                           pl.BlockSpec((r_rows, w), lambda i: (i, yr_blk)),
                           _full((CONV_WIDTH, w)), _full((1, w)), _full((w, w)), _full((1, w)),
                           _full((w, w)), _full((1, w)), _full((1, w))],
                 out_specs=[blk, blk],
                 out_shape=[jax.ShapeDtypeStruct((tp, w), F32)] * 2,
                 scratch_shapes=[pltpu.VMEM((HALO, w), F32), pltpu.VMEM((SUBLANES, w), F32),
                                 pltpu.VMEM((r_rows, w), F32), pltpu.VMEM((r_rows, w), F32)],
                 semantics=("arbitrary",), vmem_bytes=vm)


def _rec_bwd(proj, xr_blk, yr_blk, rec_w, hr, drec, conv_w, conv_b, wa, ba, wx, bx, lru):
    tp = proj.shape[0]
    w = rec_w
    r_rows = REC_ROWS
    nc = tp // r_rows
    hpc = r_rows // HALO

    def body(xr_ref, xh_ref, yr_ref, hr_ref, hh_ref, drec_ref, cw_ref, cb_ref, wa_ref, ba_ref, wx_ref, bx_ref,
             l_ref, dxr_ref, dyr_ref, dwa_ref, dwx_ref, small_ref, lam_s, a_s, dhr_s, carry_s, next_s):
        i = pl.program_id(0)
        first = (nc - 1 - i) == 0

        @pl.when(i == 0)
        def _():
            carry_s[...] = jnp.zeros_like(carry_s)
            next_s[...] = jnp.zeros_like(next_s)
            dwa_ref[...] = jnp.zeros_like(dwa_ref)
            dwx_ref[...] = jnp.zeros_like(dwx_ref)
            small_ref[...] = jnp.zeros_like(small_ref)

        x = xr_ref[...]
        xprev = jnp.where(first, 0.0, xh_ref[...])
        taps = _conv_taps(jnp.concatenate([xprev, x], axis=0))
        xc = cb_ref[...]
        for k in range(CONV_WIDTH):
            xc = xc + cw_ref[k:k + 1, :] * taps[k]
        xcb, r, ig, ls, log_a = _rec_gates(xc, wa_ref, ba_ref, wx_ref, bx_ref, l_ref)
        a = jnp.exp(log_a)
        a2 = a * a
        mult = jnp.sqrt(_one_minus_exp(2.0 * log_a, a2))
        g, dg = _gelu_and_grad(yr_ref[...])
        hr_v = hr_ref[...]
        drec_v = drec_ref[...]
        dhr_s[...] = drec_v * g
        dyr_ref[...] = (drec_v * hr_v * dg).astype(BF16)
        a_s[...] = a

        def tile(jj, carry):
            r0 = pl.multiple_of((r_rows // SUBLANES - 1 - jj) * SUBLANES, SUBLANES)
            at = a_s[pl.ds(r0, SUBLANES), :]
            dt = dhr_s[pl.ds(r0, SUBLANES), :]
            out = [None] * SUBLANES
            for rr in range(SUBLANES - 1, -1, -1):
                lam = dt[rr:rr + 1] + carry
                out[rr] = lam
                carry = at[rr:rr + 1] * lam
            lam_s[pl.ds(r0, SUBLANES), :] = jnp.concatenate(out, axis=0)
            return carry

        carry_s[0:1, :] = lax.fori_loop(0, r_rows // SUBLANES, tile, carry_s[0:1, :])
        lam = lam_s[...]
        hprev = jnp.where(first, 0.0, hh_ref[...])
        hr_prev = pltpu.roll(jnp.concatenate([hprev, hr_v], axis=0), 1, 0)[HALO:]
        da = lam * hr_prev
        dxc = lam * mult * ig
        di = lam * mult * xc
        dmult = lam * ig * xc
        dlog_a = da * a - dmult * a2 / mult
        dr = dlog_a * (RG_C * ls)
        dls = jnp.sum(dlog_a * (RG_C * r), axis=0, keepdims=True)
        dga = dr * r * (1.0 - r)
        dgx = di * ig * (1.0 - ig)
        dgab = dga.astype(BF16)
        dgxb = dgx.astype(BF16)
        dxc = dxc + _dot_nt(dgab, wa_ref[...]) + _dot_nt(dgxb, wx_ref[...])
        dwa_ref[...] += _dot_tn(xcb, dgab)
        dwx_ref[...] += _dot_tn(xcb, dgxb)
        cat = jnp.concatenate([dxc, next_s[...]], axis=0)
        next_s[...] = dxc[0:HALO]
        dxr = cw_ref[CONV_WIDTH - 1:CONV_WIDTH, :] * dxc
        for k in range(CONV_WIDTH - 1):
            sh = CONV_WIDTH - 1 - k
            dxr = dxr + cw_ref[k:k + 1, :] * pltpu.roll(cat, r_rows + HALO - sh, 0)[:r_rows]
        dxr_ref[...] = dxr.astype(BF16)
        rows = [jnp.sum(dxc * taps[k], axis=0, keepdims=True) for k in range(CONV_WIDTH)]
        rows += [jnp.sum(dxc, axis=0, keepdims=True), jnp.sum(dga, axis=0, keepdims=True),
                 jnp.sum(dgx, axis=0, keepdims=True), dls * _sigmoid(-l_ref[...])]
        small_ref[...] += jnp.concatenate(rows, axis=0)

    def rev(i):
        return nc - 1 - i

    def halo(i):
        return jnp.maximum(rev(i) * hpc - 1, 0)

    blk = pl.BlockSpec((r_rows, w), lambda i: (rev(i), 0))
    vm = 40 * _nbytes((r_rows, w), F32) + 6 * _nbytes((w, w), F32)
    return _call(body, (proj, proj, proj, hr, hr, drec, conv_w, conv_b, wa, ba, wx, bx, lru),
                 name="rec_bwd", grid=(nc,),
                 in_specs=[pl.BlockSpec((r_rows, w), lambda i: (rev(i), xr_blk)),
                           pl.BlockSpec((HALO, w), lambda i: (halo(i), xr_blk)),
                           pl.BlockSpec((r_rows, w), lambda i: (rev(i), yr_blk)),
                           blk,
                           pl.BlockSpec((HALO, w), lambda i: (halo(i), 0)),
                           blk,
                           _full((CONV_WIDTH, w)), _full((1, w)), _full((w, w)), _full((1, w)),
                           _full((w, w)), _full((1, w)), _full((1, w))],
                 out_specs=[blk, blk, _full((w, w)), _full((w, w)), _full((SUBLANES, w))],
                 out_shape=[jax.ShapeDtypeStruct((tp, w), BF16)] * 2
                 + [jax.ShapeDtypeStruct((w, w), F32)] * 2 + [jax.ShapeDtypeStruct((SUBLANES, w), F32)],
                 scratch_shapes=[pltpu.VMEM((r_rows, w), F32)] * 3
                 + [pltpu.VMEM((SUBLANES, w), F32), pltpu.VMEM((HALO, w), F32)],
                 semantics=("arbitrary",), vmem_bytes=vm)


ROW_TARGET = 544


def _mixer_out(attn, rec, g_a, g_r, w_out, h, g_next):
    tp, d = h.shape
    aw, rw = attn.shape[1], rec.shape[1]
    kc = d // N_CHIPS
    tm = _divisor_tile(tp, 16, ROW_TARGET)

    def body(a_ref, r_ref, ga_ref, gr_ref, w_ref, h_ref, gn_ref, h1_ref, z_ref, mix_ref):
        a = a_ref[...]
        r = r_ref[...]
        mix = jnp.concatenate([a * _rstd(a) * ga_ref[...], r * _rstd(r) * gr_ref[...]], axis=1).astype(BF16)
        mix_ref[...] = mix
        h1 = h_ref[...]
        for j in range(N_CHIPS):
            h1 = h1 + _dot(mix[:, j * kc:(j + 1) * kc], w_ref[j])
        h1_ref[...] = h1
        z_ref[...] = (h1 * _rstd(h1) * gn_ref[...]).astype(BF16)

    row = lambda wd: pl.BlockSpec((tm, wd), lambda i: (i, 0))
    vm = 2 * _nbytes((d, d), BF16) + 12 * _nbytes((tm, d), F32)
    return _call(body, (attn, rec, g_a, g_r, w_out, h, g_next), name="mixer_out", grid=(tp // tm,),
                 in_specs=[row(aw), row(rw), _full((1, aw)), _full((1, rw)), _full(w_out.shape), row(d),
                           _full((1, d))],
                 out_specs=[row(d), row(d), row(d)],
                 out_shape=[jax.ShapeDtypeStruct((tp, d), F32), jax.ShapeDtypeStruct((tp, d), BF16),
                            jax.ShapeDtypeStruct((tp, d), BF16)],
                 semantics=("parallel",), vmem_bytes=vm)


def _mixer_bwd(dh_b, w_out, attn, rec, g_a, g_r):
    tp, d = dh_b.shape
    aw, rw = attn.shape[1], rec.shape[1]
    tm = _divisor_tile(tp, 16, ROW_TARGET)

    def body(dh_ref, w_ref, a_ref, r_ref, ga_ref, gr_ref, da_ref, dr_ref, dg_ref):
        @pl.when(pl.program_id(0) == 0)
        def _():
            dg_ref[...] = jnp.zeros_like(dg_ref)

        dh = dh_ref[...]
        dmix = jnp.concatenate([_dot_nt(dh, w_ref[j]) for j in range(N_CHIPS)], axis=1)
        da, dga = _rms_bwd(dmix[:, :aw], a_ref[...], ga_ref[...])
        dr, dgr = _rms_bwd(dmix[:, aw:], r_ref[...], gr_ref[...])
        da_ref[...] = da
        dr_ref[...] = dr
        dg_ref[...] += jnp.broadcast_to(jnp.concatenate([dga, dgr], axis=1), (SUBLANES, d))

    row = lambda wd: pl.BlockSpec((tm, wd), lambda i: (i, 0))
    vm = 2 * _nbytes((d, d), BF16) + 12 * _nbytes((tm, d), F32)
    return _call(body, (dh_b, w_out, attn, rec, g_a, g_r), name="mixer_bwd", grid=(tp // tm,),
                 in_specs=[row(d), _full(w_out.shape), row(aw), row(rw), _full((1, aw)), _full((1, rw))],
                 out_specs=[row(aw), row(rw), _full((SUBLANES, d))],
                 out_shape=[jax.ShapeDtypeStruct((tp, aw), F32), jax.ShapeDtypeStruct((tp, rw), F32),
                            jax.ShapeDtypeStruct((SUBLANES, d), F32)],
                 semantics=("arbitrary",), vmem_bytes=vm)


def _mlp_fwd(z, w_up, w_down, h, g_next):
    tp, d = h.shape
    fc = w_up.shape[2]
    ff = N_CHIPS * fc
    tm = _divisor_tile(tp, 16, ROW_TARGET)

    def body(z_ref, wu_ref, wd_ref, h_ref, gn_ref, up_ref, h2_ref, zn_ref):
        h2 = h_ref[...]
        for j in range(N_CHIPS):
            up = _dot(z_ref[...], wu_ref[j])
            r = jnp.maximum(up, 0.0)
            up_ref[:, j * fc:(j + 1) * fc] = up.astype(BF16)
            h2 = h2 + _dot((r * r).astype(BF16), wd_ref[j])
        h2_ref[...] = h2
        zn_ref[...] = (h2 * _rstd(h2) * gn_ref[...]).astype(BF16)

    row = lambda wd: pl.BlockSpec((tm, wd), lambda i: (i, 0))
    resident = lambda shape: pl.BlockSpec(shape, lambda i: (0,) * len(shape), pipeline_mode=pl.Buffered(1))
    vm = (2 * _nbytes((ff, d), BF16) + 2 * _nbytes((tm, ff), BF16) + 12 * _nbytes((tm, d), F32)
          + 4 * _nbytes((tm, fc), F32))
    return _call(body, (z, w_up, w_down, h, g_next), name="mlp_fwd", grid=(tp // tm,),
                 in_specs=[row(d), resident(w_up.shape), resident(w_down.shape), row(d), _full((1, d))],
                 out_specs=[row(ff), row(d), row(d)],
                 out_shape=[jax.ShapeDtypeStruct((tp, ff), BF16), jax.ShapeDtypeStruct((tp, d), F32),
                            jax.ShapeDtypeStruct((tp, d), BF16)],
                 semantics=("parallel",), vmem_bytes=vm)


def _loss_bwd(h, g, target, n_real):
    tp, d = h.shape
    tm = _divisor_tile(tp, 16, ROW_TARGET)

    def body(h_ref, g_ref, t_ref, dh_ref, dhb_ref, dg_ref, loss_ref):
        i = pl.program_id(0)

        @pl.when(i == 0)
        def _():
            dg_ref[...] = jnp.zeros_like(dg_ref)
            loss_ref[...] = jnp.zeros_like(loss_ref)

        x = h_ref[...]
        gv = g_ref[...]
        rowi = i * tm + lax.broadcasted_iota(jnp.int32, (tm, 1), 0)
        real = jnp.logical_and(rowi >= N_META, rowi < N_META + n_real)
        err = jnp.where(real, x * _rstd(x) * gv - t_ref[...], 0.0)
        loss_ref[...] += 0.5 * jnp.sum(jnp.mean(err * err, axis=-1, keepdims=True))
        dx, dgp = _rms_bwd(err * (1.0 / d), x, gv)
        dh_ref[...] = dx
        dhb_ref[...] = dx.astype(BF16)
        dg_ref[...] += jnp.broadcast_to(dgp, (SUBLANES, d))

    row = pl.BlockSpec((tm, d), lambda i: (i, 0))
    return _call(body, (h, g, target), name="loss_bwd", grid=(tp // tm,),
                 in_specs=[row, _full((1, d)), row],
                 out_specs=[row, row, _full((SUBLANES, d)), _full((SUBLANES, LANES))],
                 out_shape=[jax.ShapeDtypeStruct((tp, d), F32), jax.ShapeDtypeStruct((tp, d), BF16),
                            jax.ShapeDtypeStruct((SUBLANES, d), F32), jax.ShapeDtypeStruct((SUBLANES, LANES), F32)],
                 semantics=("arbitrary",), vmem_bytes=16 * _nbytes((tm, d), F32))


def _mlp_bwd(dh_b, w_down, up, z2):
    tp, d = dh_b.shape
    fc = w_down.shape[1]
    ff = N_CHIPS * fc
    tn = _divisor_tile(fc, LANES, 512)
    per = fc // tn

    def body(dh_ref, z_ref, w_ref, up_ref, dup_ref, gd_ref, gu_ref):
        dh = dh_ref[...]
        r = jnp.maximum(up_ref[...].astype(F32), 0.0)
        dup = (_dot_nt(dh, w_ref[...]) * (2.0 * r)).astype(BF16)
        dup_ref[...] = dup
        gd_ref[...] = _dot_tn((r * r).astype(BF16), dh).astype(BF16)
        gu_ref[...] = _dot_tn(z_ref[...], dup).astype(BF16)

    col = pl.BlockSpec((tp, tn), lambda j: (0, j))
    vm = 4 * _nbytes((tp, d), BF16) + 4 * _nbytes((tn, d), BF16) + 2 * _nbytes((d, tn), BF16) \
        + 10 * _nbytes((tp, tn), F32) + 4 * _nbytes((tn, d), F32)
    return _call(body, (dh_b, z2, w_down, up), name="mlp_bwd", grid=(ff // tn,),
                 in_specs=[_full((tp, d)), _full((tp, d)),
                           pl.BlockSpec((None, tn, d), lambda j: (j // per, j % per, 0)), col],
                 out_specs=[col, pl.BlockSpec((tn, d), lambda j: (j, 0)),
                            pl.BlockSpec((None, d, tn), lambda j: (j // per, 0, j % per))],
                 out_shape=[jax.ShapeDtypeStruct((tp, ff), BF16), jax.ShapeDtypeStruct((ff, d), BF16),
                            jax.ShapeDtypeStruct((N_CHIPS, d, fc), BF16)],
                 semantics=("parallel",), vmem_bytes=vm)


def _grad_w_pieces(pieces, b):
    tp, n = b.shape
    tn = _divisor_tile(n, LANES, 512)
    widths = [pc.shape[1] for pc in pieces]

    def body(*refs):
        p_refs, b_ref, o_refs = refs[:len(pieces)], refs[len(pieces)], refs[len(pieces) + 1:]
        for p_ref, o_ref in zip(p_refs, o_refs):
            o_ref[...] = _dot_tn(p_ref[...], b_ref[...]).astype(BF16)

    vm = 2 * sum(_nbytes((tp, wd), BF16) for wd in widths) + 2 * _nbytes((tp, tn), BF16) \
        + 4 * sum(_nbytes((wd, tn), F32) for wd in widths) + 2 * _nbytes((tp, max(widths)), F32)
    return _call(body, tuple(pieces) + (b,), name="grad_w_pieces", grid=(n // tn,),
                 in_specs=[_full(pc.shape) for pc in pieces] + [pl.BlockSpec((tp, tn), lambda j: (0, j))],
                 out_specs=[pl.BlockSpec((wd, tn), lambda j: (0, j)) for wd in widths],
                 out_shape=[jax.ShapeDtypeStruct((wd, n), BF16) for wd in widths],
                 semantics=("parallel",), vmem_bytes=vm)


def _dx_norm_bwd(pieces, w, w_spec, w_piece, h, g, dres, dot=_dot_nt):
    tp, d = h.shape
    tm = _divisor_tile(tp, 16, ROW_TARGET)
    n = len(pieces)

    def body(*refs):
        dy_refs = refs[:n]
        w_ref, h_ref, g_ref, dres_ref, dh_ref, dhb_ref, dg_ref = refs[n:]

        @pl.when(pl.program_id(0) == 0)
        def _():
            dg_ref[...] = jnp.zeros_like(dg_ref)

        dz = dot(dy_refs[0][...], w_piece(w_ref, 0))
        for i in range(1, n):
            dz = dz + dot(dy_refs[i][...], w_piece(w_ref, i))
        dx, dgp = _rms_bwd(dz, h_ref[...], g_ref[...])
        dh = dres_ref[...] + dx
        dh_ref[...] = dh
        dhb_ref[...] = dh.astype(BF16)
        dg_ref[...] += jnp.broadcast_to(dgp, (SUBLANES, d))

    row = lambda wd: pl.BlockSpec((tm, wd), lambda i: (i, 0))
    kk = sum(wd for _, _, wd in pieces)
    vm = 2 * _nbytes((d, kk), BF16) + 2 * _nbytes((tm, kk), BF16) + 14 * _nbytes((tm, d), F32)
    piece_specs = [pl.BlockSpec((tm, wd), functools.partial(lambda i, cb: (i, cb), cb=cb)) for _, cb, wd in pieces]
    return _call(body, tuple(a for a, _, _ in pieces) + (w, h, g, dres), name="dx_norm_bwd", grid=(tp // tm,),
                 in_specs=piece_specs + [w_spec, row(d), _full((1, d)), row(d)],
                 out_specs=[row(d), row(d), _full((SUBLANES, d))],
                 out_shape=[jax.ShapeDtypeStruct((tp, d), F32), jax.ShapeDtypeStruct((tp, d), BF16),
                            jax.ShapeDtypeStruct((SUBLANES, d), F32)],
                 semantics=("arbitrary",), vmem_bytes=vm)


def _block_diag(wg):
    nb, b, _ = wg.shape
    eye = jnp.eye(nb, dtype=wg.dtype)
    return (eye[:, None, :, None] * wg[:, :, None, :]).reshape(nb * b, nb * b)


def _diag_blocks(dense, nb):
    b = dense.shape[0] // nb
    d4 = dense.reshape(nb, b, nb, b)
    return jnp.stack([d4[i, :, i, :] for i in range(nb)])


def _row(v):
    return v.reshape(1, -1)


def _forward_layer(l, h, z, p, fetch, stage_next, g_next):
    d = h.shape[1]
    att_w = d // 2
    rec_w = d - att_w
    nh = att_w // HEAD_DIM
    wa_d = _block_diag(p["w_gate_a"][l]).astype(BF16)
    wx_d = _block_diag(p["w_gate_x"][l]).astype(BF16)
    b_f_pad = jnp.zeros((1, LANES), F32).at[0, :nh].set(p["b_f"][l])
    w_in_t = fetch("w_in", z)
    big = dict(w_in_big=_pack_w_in_t(w_in_t.reshape(-1, d), att_w, nh))
    qkv, proj = _proj(z, big["w_in_big"], att_w)
    c, c_t = _fgate_fwd(proj, b_f_pad, nh)
    attn, lse_b = _attn_fwd(qkv, c, c_t, nh)
    hr, rec = _rec_fwd(proj, 0, 1, rec_w, p["conv_w"][l], _row(p["conv_b"][l]), wa_d,
                       _row(p["b_gate_a"][l]), wx_d, _row(p["b_gate_x"][l]), _row(p["lru_L"][l]))
    tok = stage_next(attn, "own")
    big["w_out"] = fetch("w_out", rec)
    h1, z2, mix = _mixer_out(attn, rec, _row(p["attn_out_g"][l] + tok), _row(p["rec_out_g"][l]),
                             big["w_out"], h, _row(p["mlp_norm_g"][l]))
    tok = stage_next(h1, "next")
    big["w_up"] = fetch("w_up", h1)
    big["w_down"] = fetch("w_down", h1)
    up, h2, z_next = _mlp_fwd(z2, big["w_up"], big["w_down"], h1, _row(g_next + tok))
    saved = dict(h0=h, z1=z, proj=proj, qkv=qkv, c=c, c_t=c_t, attn=attn, lse_b=lse_b, hr=hr, rec=rec, h1=h1,
                 z2=z2, mix=mix, up=up, wa_d=wa_d, wx_d=wx_d, b_f_pad=b_f_pad, big=big)
    return h2, z_next, saved


def _backward_mlp(l, dh, dh_b, sv, p, tok):
    w_up, w_down = sv["big"]["w_up"], sv["big"]["w_down"]
    fc = w_up.shape[2]
    dup, g_down, g_up = _mlp_bwd(dh_b, w_down, sv["up"], sv["z2"])
    dh, dh_b, dg2 = _dx_norm_bwd([(dup, j, fc) for j in range(N_CHIPS)], w_up, _full(w_up.shape),
                                 lambda w_ref, j: w_ref[j], sv["h1"], _row(p["mlp_norm_g"][l] + tok), dh)
    big = dict(w_down=g_down.reshape((N_CHIPS, -1) + g_down.shape[1:]), w_up=g_up)
    return dh, dh_b, big, dict(mlp_norm_g=dg2[0])


def _backward_mixer(l, dh, dh_b, sv, p, tok):
    d = dh.shape[1]
    att_w = d // 2
    rec_w = d - att_w
    nh = att_w // HEAD_DIM
    small = {}
    dattn, drec, dg_mix = _mixer_bwd(dh_b, sv["big"]["w_out"], sv["attn"], sv["rec"],
                                     _row(p["attn_out_g"][l] + tok), _row(p["rec_out_g"][l]))
    small["attn_out_g"] = dg_mix[0, :att_w]
    small["rec_out_g"] = dg_mix[0, att_w:]
    dxr, dyr, dwa, dwx, sm = _rec_bwd(
        sv["proj"], 0, 1, rec_w, sv["hr"], drec, p["conv_w"][l], _row(p["conv_b"][l]), sv["wa_d"],
        _row(p["b_gate_a"][l]), sv["wx_d"], _row(p["b_gate_x"][l]), _row(p["lru_L"][l]))
    small.update(conv_w=sm[:CONV_WIDTH], conv_b=sm[4], b_gate_a=sm[5], b_gate_x=sm[6], lru_L=sm[7],
                 w_gate_a=_diag_blocks(dwa, N_REC_BLOCKS), w_gate_x=_diag_blocks(dwx, N_REC_BLOCKS))
    dq, dk, dv, dc = _attn_bwd(sv["qkv"], sv["c"], sv["c_t"], sv["lse_b"], dattn, nh)
    df, db_f = _fgate_bwd(sv["proj"], sv["b_f_pad"], dc)
    small["b_f"] = db_f[0, :nh]
    pieces = [dq, dk, dv, dxr, dyr, df]
    offs = [0, att_w, 2 * att_w, 3 * att_w, 3 * att_w + rec_w, 3 * att_w + 2 * rec_w]
    gq, gk, gv, gxr, gyr, gf = _grad_w_pieces(pieces, sv["z1"])
    g_in_t = jnp.concatenate([gq, gk, gv, gf[:nh], gxr, gyr], axis=0)
    w_big = sv["big"]["w_in_big"]
    widths = [pc.shape[1] for pc in pieces]
    dh, dh_b, dg1 = _dx_norm_bwd(
        [(pc, 0, wd) for pc, wd in zip(pieces, widths)], w_big, _full(w_big.shape),
        lambda w_ref, i: w_ref[offs[i]:offs[i] + widths[i], :], sv["h0"], _row(p["attn_norm_g"][l]), dh, dot=_dot)
    small["attn_norm_g"] = dg1[0]
    big = dict(w_in=g_in_t.reshape(N_CHIPS, -1, d))
    return dh, dh_b, big, small


def _pack_w_in_t(w_in_t, att_w, nh):
    qkv = w_in_t[:3 * att_w]
    f = w_in_t[3 * att_w:3 * att_w + nh]
    xy = w_in_t[3 * att_w + nh:]
    return jnp.concatenate([qkv, xy, f, jnp.zeros((LANES - nh, w_in_t.shape[1]), w_in_t.dtype)], axis=0)


ANY = pl.BlockSpec(memory_space=pl.ANY)


def _coords():
    return lax.axis_index("x"), lax.axis_index("y"), lax.axis_index("c")


def _other_chips(x, y):
    return [(1 - x, y), (x, 1 - y), (1 - x, 1 - y)]


def _remote(src, dst, send_sems, recv_sems, k, to):
    return pltpu.make_async_remote_copy(src_ref=src, dst_ref=dst, send_sem=send_sems.at[k],
                                        recv_sem=recv_sems.at[k], device_id=to, device_id_type=MESH)


def _all_gather_chips(shards):
    n = len(shards)
    per = 6

    def body(*refs):
        ins, outs = refs[:n], refs[n:2 * n]
        send_sems, recv_sems, local_sems = refs[2 * n:]
        x, y, c = _coords()
        me = 2 * x + y
        sibling = (x, y, 1 - c)
        chips = _other_chips(x, y)
        local = [pltpu.make_async_copy(ins[t], outs[t].at[me], local_sems.at[t]) for t in range(n)]
        for cp in local:
            cp.start()
        sends = []
        for t in range(n):
            for j, (px, py) in enumerate(chips):
                cp = _remote(ins[t].at[c], outs[t].at[me, c], send_sems, recv_sems, per * t + j, (px, py, c))
                cp.start()
                sends.append(cp)
        for t in range(n):
            for j, (px, py) in enumerate(chips):
                landed = outs[t].at[2 * px + py, c]
                _remote(landed, landed, send_sems, recv_sems, per * t + j, (px, py, c)).wait_recv()
                cp = _remote(landed, landed, send_sems, recv_sems, per * t + 3 + j, sibling)
                cp.start()
                sends.append(cp)
        for t in range(n):
            for j, (px, py) in enumerate(chips):
                passed = outs[t].at[2 * px + py, 1 - c]
                _remote(passed, passed, send_sems, recv_sems, per * t + 3 + j, sibling).wait_recv()
        for cp in sends:
            cp.wait_send()
        for cp in local:
            cp.wait()

    return _call(body, tuple(shards), name="all_gather_chips",
                 in_specs=[ANY] * n, out_specs=[ANY] * n,
                 out_shape=[jax.ShapeDtypeStruct((N_CHIPS,) + s.shape, s.dtype) for s in shards],
                 scratch_shapes=[pltpu.SemaphoreType.DMA((per * n,)), pltpu.SemaphoreType.DMA((per * n,)),
                                 pltpu.SemaphoreType.DMA((n,))])


HBM = pl.BlockSpec(memory_space=pltpu.HBM)
SEM = pl.BlockSpec(memory_space=pltpu.SEMAPHORE)
DATAFLOW = pltpu.SideEffectType.DATAFLOW_SIDE_EFFECTING


def _in_hbm(a):
    return pltpu.with_memory_space_constraint(a, pltpu.HBM)


PUSH_ARRIVALS = {"gather_chips_half": N_CHIPS - 1, "pass_halves": N_CHIPS - 1, "scatter_chips": N_CHIPS - 1,
                 "sibling": 1, "gather_devices": N_DEV - 1}


def _column_half(ref3, slab, c):
    hw = ref3.shape[2] // 2
    return ref3.at[slab, :, pl.ds(pl.multiple_of(c * hw, LANES), hw)]


def _push_copies(mode, src, land, send_sems, recv_sems, t):
    x, y, c = _coords()
    chip = 2 * x + y
    if mode == "gather_chips_half":
        return [_remote(_column_half(src, chip, c), _column_half(land, chip, c), send_sems, recv_sems, t, (px, py, c))
                for px, py in _other_chips(x, y)]
    if mode == "pass_halves":
        return [_remote(_column_half(src, 2 * px + py, c), _column_half(land, 2 * px + py, c), send_sems, recv_sems, t,
                        (x, y, 1 - c)) for px, py in _other_chips(x, y)]
    if mode == "scatter_chips":
        return [_remote(src.at[2 * px + py], land.at[chip], send_sems, recv_sems, t, (px, py, c))
                for px, py in _other_chips(x, y)]
    if mode == "sibling":
        return [_remote(src, land, send_sems, recv_sems, t, (x, y, 1 - c))]
    dev = 4 * x + 2 * y + c
    return [_remote(src.at[dev], land.at[dev], send_sems, recv_sems, t, (x ^ (k >> 2), y ^ ((k >> 1) & 1), c ^ (k & 1)))
            for k in range(1, N_DEV)]


def _push_start(srcs, lands, mode, name):
    n = len(srcs)
    same = all(s is ld for s, ld in zip(srcs, lands))
    n_in = n if same else 2 * n

    def body(*refs):
        src_refs = refs[:n]
        land_refs = src_refs if same else refs[n:2 * n]
        send_sems, recv_sems = refs[n_in], refs[n_in + 1]
        token = refs[-1]
        for t in range(n):
            for cp in _push_copies(mode, src_refs[t], land_refs[t], send_sems, recv_sems, t):
                cp.start()
        token[...] = jnp.zeros_like(token)

    operands = tuple(srcs) if same else tuple(srcs) + tuple(lands)
    res = _call(
        body, [_in_hbm(a) for a in operands], name=name,
        out_shape=(pltpu.SemaphoreType.DMA((n,)), pltpu.SemaphoreType.DMA((n,)))
        + tuple(pltpu.HBM(a.shape, a.dtype) for a in operands) + (jax.ShapeDtypeStruct((SUBLANES, LANES), F32),),
        in_specs=[HBM] * n_in, out_specs=(SEM, SEM) + (HBM,) * n_in + (pl.BlockSpec(memory_space=pltpu.VMEM),),
        input_output_aliases={i: 2 + i for i in range(n_in)}, side_effects=DATAFLOW, hbm_results=False)
    send_sems, recv_sems, token = res[0], res[1], res[-1]
    srcs_thru = res[2:2 + n]
    lands_thru = srcs_thru if same else res[2 + n:2 + 2 * n]
    return send_sems, recv_sems, srcs_thru, lands_thru, token


def _push_wait(send_sems, recv_sems, ids, srcs, lands, mode, after, name):
    n = len(lands)
    same = all(s is ld for s, ld in zip(srcs, lands))
    n_in = n if same else 2 * n

    def body(*refs):
        land_refs = refs[:n] if same else refs[n:2 * n]
        send_sems, recv_sems = refs[n_in], refs[n_in + 1]
        x, y, c = _coords()
        for t in range(n):
            if mode == "sibling":
                moved = land_refs[t]
            elif mode in ("gather_chips_half", "pass_halves"):
                moved = land_refs[t].at[pl.ds(0, PUSH_ARRIVALS[mode]), :, pl.ds(0, land_refs[t].shape[2] // 2)]
            else:
                moved = land_refs[t].at[pl.ds(0, PUSH_ARRIVALS[mode])]
            arrivals = _remote(moved, moved, send_sems, recv_sems, ids[t], (x, y, c))
            arrivals.wait_send()
            arrivals.wait_recv()

    operands = tuple(lands) if same else tuple(srcs) + tuple(lands)
    res = _call(
        body, operands + (send_sems, recv_sems, after), name=name,
        out_shape=tuple(pltpu.HBM(a.shape, a.dtype) for a in operands),
        in_specs=[HBM] * n_in + [SEM, SEM, ANY], out_specs=(HBM,) * n_in,
        input_output_aliases={i: i for i in range(n_in)}, side_effects=DATAFLOW)
    return list(res) if same else (list(res[:n]), list(res[n:]))


def _sum_partials(part, landed, chip):
    _, rows, cols = part.shape
    br = _divisor_tile(rows, 16, ELEM_ROWS)

    def body(chip_ref, own_ref, a_ref, b_ref, c_ref, o_ref):
        o_ref[...] = ((own_ref[...].astype(F32) + a_ref[...].astype(F32)) + b_ref[...].astype(F32)) \
            + c_ref[...].astype(F32)

    def other(k):
        return pl.BlockSpec((None, br, cols), lambda i, ch: (jnp.where(ch[0] <= k, k + 1, k), i, 0))

    spec = pltpu.PrefetchScalarGridSpec(
        num_scalar_prefetch=1, grid=(rows // br,),
        in_specs=[pl.BlockSpec((None, br, cols), lambda i, ch: (ch[0], i, 0)), other(0), other(1), other(2)],
        out_specs=pl.BlockSpec((br, cols), lambda i, ch: (i, 0)))
    return _call(body, (chip, part, landed, landed, landed), name="sum_partials", grid_spec=spec,
                 out_shape=jax.ShapeDtypeStruct((rows, cols), F32), semantics=("parallel",))


def _cast_to_slab(w, l, chip):
    _, rows, cols = w.shape
    br = _divisor_tile(rows, 16, ELEM_ROWS)

    def body(chip_ref, w_ref, o_ref):
        o_ref[...] = w_ref[...].astype(BF16)

    spec = pltpu.PrefetchScalarGridSpec(
        num_scalar_prefetch=1, grid=(rows // br,),
        in_specs=[pl.BlockSpec((None, br, cols), lambda i, ch: (l, i, 0))],
        out_specs=pl.BlockSpec((None, br, cols), lambda i, ch: (ch[0], i, 0)))
    return _call(body, (chip, w), name="cast_to_slab", grid_spec=spec,
                 out_shape=jax.ShapeDtypeStruct((N_CHIPS, rows, cols), BF16), semantics=("parallel",))


def _cast_w_in_t_to_slabs(w_t, chip):
    rows, depth, d = w_t.shape
    tn = _divisor_tile(d, LANES, 256)

    def body(chip_ref, w_ref, *o_refs):
        for l in range(depth):
            o_refs[l][...] = w_ref[:, l, :].astype(BF16)

    spec = pltpu.PrefetchScalarGridSpec(
        num_scalar_prefetch=1, grid=(d // tn,),
        in_specs=[pl.BlockSpec((rows, depth, tn), lambda j, ch: (0, 0, j))],
        out_specs=[pl.BlockSpec((None, rows, tn), lambda j, ch: (ch[0], 0, j))] * depth)
    return _call(body, (chip, w_t), name="cast_w_in_t_to_slabs", grid_spec=spec,
                 out_shape=[jax.ShapeDtypeStruct((N_CHIPS, rows, d), BF16)] * depth, semantics=("parallel",),
                 vmem_bytes=4 * _nbytes((rows, max(depth, SUBLANES), tn), F32))


def _place_slab(buf, index, n_slabs):
    rows, cols = buf.shape
    br = _divisor_tile(rows, SUBLANES, ELEM_ROWS)

    def body(index_ref, b_ref, o_ref):
        o_ref[...] = b_ref[...]

    spec = pltpu.PrefetchScalarGridSpec(
        num_scalar_prefetch=1, grid=(rows // br,),
        in_specs=[pl.BlockSpec((br, cols), lambda i, ix: (i, 0))],
        out_specs=pl.BlockSpec((None, br, cols), lambda i, ix: (ix[0], i, 0)))
    return _call(body, (index, buf), name="place_slab", grid_spec=spec,
                 out_shape=jax.ShapeDtypeStruct((n_slabs, rows, cols), buf.dtype), semantics=("parallel",))


ELEM_ROWS = 256


def _sum_slabs(r):
    n, rows, cols = r.shape
    br = _divisor_tile(rows, 16, ELEM_ROWS)

    def body(r_ref, o_ref):
        acc = r_ref[0].astype(F32)
        for j in range(1, n):
            acc = acc + r_ref[j].astype(F32)
        o_ref[...] = acc

    return _call(body, (r,), name="sum_slabs", grid=(rows // br,),
                 in_specs=[pl.BlockSpec((n, br, cols), lambda i: (0, i, 0))],
                 out_specs=pl.BlockSpec((br, cols), lambda i: (i, 0)),
                 out_shape=jax.ShapeDtypeStruct((rows, cols), F32), semantics=("parallel",))


def _adamw_math(w, g, m, v):
    c1 = 1.0 - ADAM_B1 ** ADAM_STEP
    c2 = 1.0 - ADAM_B2 ** ADAM_STEP
    nm = ADAM_B1 * m + (1.0 - ADAM_B1) * g
    nv = ADAM_B2 * v + (1.0 - ADAM_B2) * (g * g)
    delta = -ADAM_LR * ((nm / c1) / (jnp.sqrt(nv / c2) + ADAM_EPS) + ADAM_WD * w)
    return delta, nm, nv


def _adamw(w, g, m, v):
    rows, cols = w.shape
    br = _divisor_tile(rows, 8, ELEM_ROWS)

    def body(w_ref, g_ref, m_ref, v_ref, d_ref, nm_ref, nv_ref):
        d_ref[...], nm_ref[...], nv_ref[...] = _adamw_math(w_ref[...], g_ref[...], m_ref[...], v_ref[...])

    blk = pl.BlockSpec((br, cols), lambda i: (i, 0))
    return _call(body, (w, g, m, v), name="adamw", grid=(rows // br,),
                 in_specs=[blk] * 4, out_specs=[blk] * 3,
                 out_shape=[jax.ShapeDtypeStruct((rows, cols), F32)] * 3, semantics=("parallel",))


def _adamw_w_in_t(w_t, m_t, v_t, g_mine, g_theirs):
    rows, depth, d = w_t.shape
    tn = LANES

    def body(w_ref, m_ref, v_ref, *rest):
        ga_refs, gb_refs = rest[:depth], rest[depth:2 * depth]
        g_ref, d_ref, nm_ref, nv_ref = rest[2 * depth:]
        for l in range(depth):
            g_ref[:, l, :] = ga_refs[l][...] + gb_refs[l][...]
        d_ref[...], nm_ref[...], nv_ref[...] = _adamw_math(w_ref[...], g_ref[...], m_ref[...], v_ref[...])

    slab = pl.BlockSpec((rows, depth, tn), lambda j: (0, 0, j))
    gblk = pl.BlockSpec((rows, tn), lambda j: (0, j))
    return _call(body, (w_t, m_t, v_t) + tuple(g_mine) + tuple(g_theirs), name="adamw_w_in_t", grid=(d // tn,),
                 in_specs=[slab] * 3 + [gblk] * (2 * depth), out_specs=[slab] * 4,
                 out_shape=[jax.ShapeDtypeStruct(w_t.shape, F32)] * 4, semantics=("parallel",),
                 vmem_bytes=2 * (7 * _nbytes((rows, max(depth, SUBLANES), tn), F32)
                                 + 2 * depth * _nbytes((rows, tn), F32)))


def _adamw_layer(w, m, v, l, g_mine, g_theirs, prev, after):
    _, rows, cols = w.shape
    br = _divisor_tile(rows, 8, ELEM_ROWS)

    def body(w_ref, m_ref, v_ref, ga_ref, gb_ref, *rest):
        g_ref, d_ref, nm_ref, nv_ref = rest[5:]
        g = ga_ref[...] + gb_ref[...]
        g_ref[...] = g
        d_ref[...], nm_ref[...], nv_ref[...] = _adamw_math(w_ref[...], g, m_ref[...], v_ref[...])

    slot = pl.BlockSpec((None, br, cols), lambda i: (l, i, 0))
    blk = pl.BlockSpec((br, cols), lambda i: (i, 0))
    return _call(body, (w, m, v, g_mine, g_theirs) + tuple(prev) + (after,), name="adamw_layer",
                 grid=(rows // br,), in_specs=[slot] * 3 + [blk] * 2 + [ANY] * 5, out_specs=[slot] * 4,
                 out_shape=[jax.ShapeDtypeStruct(w.shape, F32)] * 4,
                 input_output_aliases={5: 0, 6: 1, 7: 2, 8: 3}, semantics=("parallel",))


BIG = ("w_in", "w_out", "w_up", "w_down")
WEIGHTS = ("meta", "attn_norm_g", "w_in", "b_f", "conv_w", "conv_b", "w_gate_a", "b_gate_a", "w_gate_x",
           "b_gate_x", "lru_L", "attn_out_g", "rec_out_g", "w_out", "mlp_norm_g", "w_up", "w_down", "final_g")
SMALL = tuple(k for k in WEIGHTS if k not in BIG)
COL_SHARDED_SMALL = ("meta", "conv_w")


def _packed_rows(shape):
    return -(-math.prod(shape) // (SUBLANES * LANES)) * SUBLANES


def _pack(arrs):
    rows = []
    for a in arrs:
        flat = a.reshape(-1)
        rows.append(jnp.pad(flat, (0, _packed_rows(a.shape) * LANES - flat.shape[0])).reshape(-1, LANES))
    used = sum(r.shape[0] for r in rows)
    rows.append(jnp.zeros((-used % ELEM_ROWS, LANES), F32))
    return jnp.concatenate(rows, axis=0)


def _unpack(buf, shapes):
    out, r0 = [], 0
    for s in shapes:
        nr = _packed_rows(s)
        out.append(buf[r0:r0 + nr].reshape(-1)[:math.prod(s)].reshape(s))
        r0 += nr
    return out


def _halves(a):
    return a.reshape((2, a.shape[0] // 2) + a.shape[1:])


def _cols_from_chips(g):
    return jnp.moveaxis(g, 0, -2).reshape(g.shape[1:-1] + (N_CHIPS * g.shape[-1],))


def kernel(x, meta, attn_norm_g, w_in, b_f, conv_w, conv_b, w_gate_a, b_gate_a, w_gate_x, b_gate_x, lru_L, attn_out_g, rec_out_g, w_out, mlp_norm_g, w_up, w_down, final_g, loss_target, m_meta, m_attn_norm_g, m_w_in, m_b_f, m_conv_w, m_conv_b, m_w_gate_a, m_b_gate_a, m_w_gate_x, m_b_gate_x, m_lru_L, m_attn_out_g, m_rec_out_g, m_w_out, m_mlp_norm_g, m_w_up, m_w_down, m_final_g, v_meta, v_attn_norm_g, v_w_in, v_b_f, v_conv_w, v_conv_b, v_w_gate_a, v_b_gate_a, v_w_gate_x, v_b_gate_x, v_lru_L, v_attn_out_g, v_rec_out_g, v_w_out, v_mlp_norm_g, v_w_up, v_w_down, v_final_g):
    w = dict(meta=meta, attn_norm_g=attn_norm_g, w_in=w_in, b_f=b_f, conv_w=conv_w, conv_b=conv_b,
             w_gate_a=w_gate_a, b_gate_a=b_gate_a, w_gate_x=w_gate_x, b_gate_x=b_gate_x, lru_L=lru_L,
             attn_out_g=attn_out_g, rec_out_g=rec_out_g, w_out=w_out, mlp_norm_g=mlp_norm_g, w_up=w_up,
             w_down=w_down, final_g=final_g)
    m = dict(meta=m_meta, attn_norm_g=m_attn_norm_g, w_in=m_w_in, b_f=m_b_f, conv_w=m_conv_w, conv_b=m_conv_b,
             w_gate_a=m_w_gate_a, b_gate_a=m_b_gate_a, w_gate_x=m_w_gate_x, b_gate_x=m_b_gate_x, lru_L=m_lru_L,
             attn_out_g=m_attn_out_g, rec_out_g=m_rec_out_g, w_out=m_w_out, mlp_norm_g=m_mlp_norm_g,
             w_up=m_w_up, w_down=m_w_down, final_g=m_final_g)
    v = dict(meta=v_meta, attn_norm_g=v_attn_norm_g, w_in=v_w_in, b_f=v_b_f, conv_w=v_conv_w, conv_b=v_conv_b,
             w_gate_a=v_w_gate_a, b_gate_a=v_b_gate_a, w_gate_x=v_w_gate_x, b_gate_x=v_b_gate_x, lru_L=v_lru_L,
             attn_out_g=v_attn_out_g, rec_out_g=v_rec_out_g, w_out=v_w_out, mlp_norm_g=v_mlp_norm_g,
             w_up=v_w_up, w_down=v_w_down, final_g=v_final_g)
    s_len, d = x.shape[1], x.shape[2]
    depth = w_in.shape[0]
    chip = 2 * lax.axis_index("x") + lax.axis_index("y")

    g_conv, g_meta = [g.reshape((N_CHIPS, g.shape[1] * g.shape[2]) + g.shape[3:])
                      for g in _all_gather_chips([_halves(w["conv_w"]), _halves(w["meta"])])]
    p = dict(w)
    p["conv_w"] = _cols_from_chips(g_conv)
    meta_full = jnp.moveaxis(g_meta, 0, 1).reshape(N_META, d)

    chip1 = chip.reshape(1).astype(jnp.int32)
    w_in_t, m_in_t, v_in_t = [jnp.transpose(a["w_in"], (2, 0, 1)) for a in (w, m, v)]
    w_in_slabs = _cast_w_in_t_to_slabs(w_in_t, chip1)
    pushes, tokens = [], []
    for l in range(depth):
        slabs = [w_in_slabs[l]] + [_cast_to_slab(w[k], l, chip1) for k in BIG[1:]]
        send_sems, recv_sems, _, lands, token = _push_start(slabs, slabs, "gather_chips_half", f"weights_start_{l}")
        pushes.append((send_sems, recv_sems, lands))
        tokens.append(token[0, 0])
    passed = [{} for _ in range(depth)]

    def stage(l, after, ids):
        send_sems, recv_sems, lands = pushes[l]
        tag = "_".join(BIG[i] for i in ids)
        sub = [lands[i] for i in ids]
        sub = _push_wait(send_sems, recv_sems, ids, sub, sub, "gather_chips_half", after, f"{tag}_wait_{l}")
        send_sems, recv_sems, _, sub, token = _push_start(sub, sub, "pass_halves", f"{tag}_pass_{l}")
        for j, i in enumerate(ids):
            passed[l][i] = (send_sems, recv_sems, sub[j], j)
        return token[0, 0]

    t_len = N_META + s_len
    pad = -t_len % SEQ_TILE
    h = jnp.concatenate([meta_full, x[0], jnp.zeros((pad, d), F32)], axis=0)
    tgt = jnp.concatenate([jnp.zeros((N_META, d), F32), loss_target[0], jnp.zeros((pad, d), F32)], axis=0)
    z = _rms_fwd(h, _row(p["attn_norm_g"][0] + sum(tokens)))
    stage(0, z, [0])
    saved = []
    for l in range(depth):
        def fetch(k, after, l=l):
            send_sems, recv_sems, land, j = passed[l][BIG.index(k)]
            return _push_wait(send_sems, recv_sems, [j], [land], [land], "pass_halves", after,
                              f"{k}_here_{l}")[0]

        def stage_next(after, which, l=l):
            if which == "own":
                return stage(l, after, [1, 2, 3])
            return stage(l + 1, after, [0]) if l + 1 < depth else 0.0

        g_next = p["attn_norm_g"][l + 1] if l + 1 < depth else p["final_g"]
        h, z, sv = _forward_layer(l, h, z, p, fetch, stage_next, g_next)
        saved.append(sv)
    dh, dh_b, dg_final, loss_part = _loss_bwd(h, _row(p["final_g"]), tgt, s_len)

    small = {k: [None] * depth for k in SMALL if k not in ("meta", "final_g")}
    pushes = [None] * depth
    tok = 0.0
    for l in reversed(range(depth)):
        dh, dh_b, big_mlp, sm_mlp = _backward_mlp(l, dh, dh_b, saved[l], p, tok)
        g_out, = _grad_w_pieces([saved[l]["mix"]], dh_b)
        parts = [big_mlp["w_down"], big_mlp["w_up"], g_out.reshape((N_CHIPS, -1) + g_out.shape[1:])]
        push_mlp = _push_start(parts, [lax.empty(a.shape, a.dtype) for a in parts], "scatter_chips",
                               f"mlp_grads_start_{l}")
        dh, dh_b, big_mix, sm_mix = _backward_mixer(l, dh, dh_b, saved[l], p, push_mlp[4][0, 0])
        parts = [big_mix["w_in"]]
        push_mix = _push_start(parts, [lax.empty(a.shape, a.dtype) for a in parts], "scatter_chips",
                               f"mixer_grads_start_{l}")
        tok = push_mix[4][0, 0]
        pushes[l] = {("w_down", "w_up", "w_out"): push_mlp, ("w_in",): push_mix}
        for k, val in {**sm_mlp, **sm_mix}.items():
            small[k][l] = val
    grads = {k: jnp.stack(val) for k, val in small.items()}
    grads["final_g"] = dg_final[0]
    grads["meta"] = dh[:N_META]
    dx = dh[N_META:t_len]

    full_shapes = [grads[k].shape for k in SMALL] + [(1,)]
    packed = _pack([grads[k].astype(F32) for k in SMALL] + [loss_part[0, :1] + tok])
    dev1 = (2 * chip + lax.axis_index("c")).reshape(1).astype(jnp.int32)
    slabs = [_place_slab(packed, dev1, N_DEV)]
    small_push = _push_start(slabs, slabs, "gather_devices", "small_grads_start")

    last_token = small_push[4]
    outs = {k: [lax.empty(w[k].shape, F32) for _ in range(4)] for k in BIG[1:]}
    w_in_sums = [None] * depth
    swaps = {}

    def finish(l, wait_after, adam_after):
        send_sems, recv_sems, mine, lands, _ = swaps[l]
        mine, theirs = _push_wait(send_sems, recv_sems, list(range(len(BIG))), mine, lands, "sibling", wait_after,
                                  f"sums_wait_{l}")
        w_in_sums[l] = (mine[0], theirs[0])
        for k, a, b in zip(BIG[1:], mine[1:], theirs[1:]):
            outs[k] = _adamw_layer(w[k], m[k], v[k], l, a, b, outs[k], adam_after)

    wait_after = last_token
    for l in reversed(range(depth)):
        sums = {}
        for names, (send_sems, recv_sems, parts, lands, _) in pushes[l].items():
            parts, landed = _push_wait(send_sems, recv_sems, list(range(len(names))), parts, lands, "scatter_chips",
                                       wait_after, f"{names[0]}_grads_wait_{l}")
            for k, part, land in zip(names, parts, landed):
                sums[k] = wait_after = _sum_partials(part, land, chip1)
        mine = [sums[k] for k in BIG]
        swaps[l] = _push_start(mine, [lax.empty(a.shape, a.dtype) for a in mine], "sibling", f"sums_start_{l}")
        if l + 2 < depth:
            finish(l + 2, outs["w_down"][0] if l + 3 < depth else mine[0], swaps[l][4])
    for l in reversed(range(min(2, depth))):
        finish(l, outs["w_down"][0] if depth > 2 else swaps[0][4], swaps[0][4])
    outs["w_in"] = [jnp.transpose(r, (1, 2, 0)) for r in _adamw_w_in_t(
        w_in_t, m_in_t, v_in_t, [s[0] for s in w_in_sums], [s[1] for s in w_in_sums])]
    out_g, out_d, out_m, out_v = [{k: outs[k][i] for k in BIG} for i in range(4)]

    landed = _push_wait(small_push[0], small_push[1], [0], small_push[3], small_push[3], "gather_devices",
                        out_g["w_in"], "small_grads_wait")
    total = _sum_slabs(landed[0])
    small_g = dict(zip(SMALL + ("loss",), _unpack(total, full_shapes)))
    for k in COL_SHARDED_SMALL:
        n = w[k].shape[-1]
        small_g[k] = lax.dynamic_slice_in_dim(small_g[k], chip * n, n, axis=small_g[k].ndim - 1)
    local_shapes = [w[k].shape for k in SMALL]
    res = _adamw(_pack([w[k] for k in SMALL]), _pack([small_g[k] for k in SMALL]),
                 _pack([m[k] for k in SMALL]), _pack([v[k] for k in SMALL]))
    out_g.update({k: small_g[k] for k in SMALL})
    for dst, buf in zip((out_d, out_m, out_v), res):
        dst.update(zip(SMALL, _unpack(buf, local_shapes)))

    return (small_g["loss"].reshape(()), dx[None],
            *[out_g[k] for k in WEIGHTS], *[out_d[k] for k in WEIGHTS],
            *[out_m[k] for k in WEIGHTS], *[out_v[k] for k in WEIGHTS])
```

```python
import functools
import math

import jax
import jax.numpy as jnp
from jax import lax
from jax.experimental import pallas as pl
from jax.experimental.pallas import tpu as pltpu

F32 = jnp.float32
BF16 = jnp.bfloat16

N_META = 16
HEAD_DIM = 64
N_REC_BLOCKS = 8
CONV_WIDTH = 4
RG_C = 8.0
NORM_EPS = 1e-6
ADAM_LR = 0.001
ADAM_B1 = 0.9
ADAM_B2 = 0.999
ADAM_EPS = 1e-08
ADAM_WD = 0.01
ADAM_STEP = 10

LANES = 128
SUBLANES = 8
SEQ_TILE = 128
VMEM_CAP = 60 * 2**20
VMEM_SLACK = 6 * 2**20
NEG_BIG = -1e30
N_CHIPS = 4
N_DEV = 8
MESH = pl.DeviceIdType.MESH


def _nbytes(shape, dtype):
    return math.prod(shape) * jnp.dtype(dtype).itemsize


def _call(body, args, *, name, out_shape, grid=(), in_specs=None, out_specs=None, scratch_shapes=(),
          grid_spec=None, semantics=None, vmem_bytes=None, side_effects=None, hbm_results=True, **kw):
    cp = {}
    if semantics is not None:
        cp["dimension_semantics"] = semantics
    if vmem_bytes is not None:
        cp["vmem_limit_bytes"] = int(min(VMEM_CAP, vmem_bytes + VMEM_SLACK))
    if side_effects is not None:
        cp["has_side_effects"] = side_effects
    if grid_spec is not None:
        kw["grid_spec"] = grid_spec
    else:
        kw.update(grid=grid, in_specs=in_specs, out_specs=out_specs, scratch_shapes=scratch_shapes)
    if hbm_results:
        out_shape = jax.tree.map(
            lambda s: pltpu.HBM(s.shape, s.dtype) if isinstance(s, jax.ShapeDtypeStruct) else s, out_shape)
    fn = pl.pallas_call(
        body, name=name, out_shape=out_shape,
        compiler_params=pltpu.CompilerParams(**cp), **kw)
    return fn(*[_in_hbm(a) if jnp.issubdtype(getattr(a, "dtype", jnp.int32), jnp.floating) else a for a in args])


def _divisor_tile(n, unit, target):
    best = None
    for t in range(unit, min(n, target) + 1, unit):
        if n % t == 0:
            best = t
    return n if best is None else best


def _sigmoid(x):
    return 1.0 / (1.0 + jnp.exp(-x))


def _log1p_unit(e):
    series = e * (1.0 - e * (0.5 - e * (1.0 / 3.0)))
    return jnp.where(e < 1e-2, series, jnp.log(1.0 + e))


def _log_sigmoid(x):
    return jnp.minimum(x, 0.0) - _log1p_unit(jnp.exp(-jnp.abs(x)))


def _one_minus_exp(x, exp_x):
    small = -x * (1.0 + x * (1.0 / 2 + x * (1.0 / 6 + x * (1.0 / 24 + x * (1.0 / 120 + x * (1.0 / 720))))))
    return jnp.where(x > -0.25, small, 1.0 - exp_x)


_GELU_K = math.sqrt(2.0 / math.pi)
_GELU_C = 0.044715


def _gelu_and_grad(y):
    th = jnp.tanh(_GELU_K * (y + _GELU_C * y * y * y))
    g = 0.5 * y * (1.0 + th)
    dg = 0.5 * (1.0 + th) + 0.5 * y * (1.0 - th * th) * _GELU_K * (1.0 + 3.0 * _GELU_C * y * y)
    return g, dg


def _rstd(x):
    return lax.rsqrt(jnp.mean(x * x, axis=-1, keepdims=True) + NORM_EPS)


def _rms_bwd(dz, x, g):
    rs = _rstd(x)
    xh = x * rs
    dgp = jnp.sum(dz * xh, axis=0, keepdims=True)
    dxh = dz * g
    dx = rs * (dxh - xh * jnp.mean(dxh * xh, axis=-1, keepdims=True))
    return dx, dgp


def _dot(a, b):
    return jnp.dot(a, b, preferred_element_type=F32)


def _dot_nt(a, b):
    return lax.dot_general(a, b, (((1,), (1,)), ((), ())), preferred_element_type=F32)


def _dot_tn(a, b):
    return lax.dot_general(a, b, (((0,), (0,)), ((), ())), preferred_element_type=F32)


def _full(shape):
    nd = len(shape)
    return pl.BlockSpec(shape, lambda *_: (0,) * nd)


def _rms_fwd(h, g):
    tp, d = h.shape
    tm = _divisor_tile(tp, 16, 544)

    def body(h_ref, g_ref, z_ref):
        x = h_ref[...]
        z_ref[...] = (x * _rstd(x) * g_ref[...]).astype(BF16)

    return _call(body, (h, g), name="rms_fwd", grid=(tp // tm,),
                 in_specs=[pl.BlockSpec((tm, d), lambda i: (i, 0)), _full((1, d))],
                 out_specs=pl.BlockSpec((tm, d), lambda i: (i, 0)),
                 out_shape=jax.ShapeDtypeStruct((tp, d), BF16), semantics=("parallel",))


def _proj(z, w_big_t, att_w):
    tp, d = z.shape
    nb = w_big_t.shape[0]
    tn = _divisor_tile(nb, LANES, 512)
    assert (3 * att_w) % tn == 0
    n_qkv = 3 * att_w // tn
    scale = 1.0 / math.sqrt(HEAD_DIM)

    def body(z_ref, w_ref, qkv_ref, p_ref):
        j = pl.program_id(0)
        acc = _dot_nt(z_ref[...], w_ref[...])

        @pl.when(j < n_qkv)
        def _():
            col = j * tn + lax.broadcasted_iota(jnp.int32, (1, tn), 1)
            qkv_ref[...] = (acc * jnp.where(col < att_w, scale, 1.0)).astype(BF16)

        @pl.when(j >= n_qkv)
        def _():
            p_ref[...] = acc

    vm = 2 * (_nbytes((tp, d), BF16) + _nbytes((d, tn), BF16) + _nbytes((tp, tn), F32) * 2)
    return _call(body, (z, w_big_t), name="proj", grid=(nb // tn,),
                 in_specs=[_full((tp, d)), pl.BlockSpec((tn, d), lambda j: (j, 0))],
                 out_specs=[pl.BlockSpec((tp, tn), lambda j: (0, jnp.minimum(j, n_qkv - 1))),
                            pl.BlockSpec((tp, tn), lambda j: (0, jnp.maximum(j - n_qkv, 0)))],
                 out_shape=[jax.ShapeDtypeStruct((tp, 3 * att_w), BF16),
                            jax.ShapeDtypeStruct((tp, nb - 3 * att_w), F32)],
                 semantics=("arbitrary",), vmem_bytes=vm)


def _tile_cumsum(x, row, reverse=False):
    for s in (1, 2, 4):
        if reverse:
            x = x + jnp.where(row < SUBLANES - s, pltpu.roll(x, SUBLANES - s, 0), 0.0)
        else:
            x = x + jnp.where(row >= s, pltpu.roll(x, s, 0), 0.0)
    return x


def _fgate_fwd(proj, b_f_pad, nh):
    tp, nb = proj.shape
    fblk = nb // LANES - 1

    def body(f_ref, b_ref, c_ref, ct_ref):
        b = b_ref[...]
        row = lax.broadcasted_iota(jnp.int32, (SUBLANES, LANES), 0)

        def step(i, carry):
            r0 = pl.multiple_of(i * SUBLANES, SUBLANES)
            lf = _log_sigmoid(f_ref[pl.ds(r0, SUBLANES), :] + b)
            x = _tile_cumsum(lf, row) + carry
            c_ref[pl.ds(r0, SUBLANES), :] = x
            return x[SUBLANES - 1:SUBLANES, :]

        lax.fori_loop(0, tp // SUBLANES, step, jnp.zeros((1, LANES), F32))
        ct_ref[...] = c_ref[...].T[:nh, :]

    return _call(body, (proj, b_f_pad), name="fgate_fwd", grid=(1,),
                 in_specs=[pl.BlockSpec((tp, LANES), lambda i: (0, fblk)), _full((1, LANES))],
                 out_specs=[_full((tp, LANES)), _full((nh, tp))],
                 out_shape=[jax.ShapeDtypeStruct((tp, LANES), F32), jax.ShapeDtypeStruct((nh, tp), F32)],
                 semantics=("arbitrary",))


def _fgate_bwd(proj, b_f_pad, dc):
    tp, nb = proj.shape
    fblk = nb // LANES - 1

    def body(f_ref, b_ref, dc_ref, df_ref, db_ref, dc_s):
        b = b_ref[...]
        row = lax.broadcasted_iota(jnp.int32, (SUBLANES, LANES), 0)
        nt = tp // SUBLANES

        def step(i, carry):
            suffix, acc = carry
            r0 = pl.multiple_of((nt - 1 - i) * SUBLANES, SUBLANES)
            dlf = _tile_cumsum(dc_ref[pl.ds(r0, SUBLANES), :], row, reverse=True) + suffix
            df = dlf * _sigmoid(-(f_ref[pl.ds(r0, SUBLANES), :] + b))
            dc_s[pl.ds(r0, SUBLANES), :] = df
            return dlf[0:1, :], acc + df

        _, acc = lax.fori_loop(0, nt, step, (jnp.zeros((1, LANES), F32), jnp.zeros((SUBLANES, LANES), F32)))
        df_ref[...] = dc_s[...].astype(BF16)
        db_ref[...] = jnp.broadcast_to(jnp.sum(acc, axis=0, keepdims=True), (SUBLANES, LANES))

    return _call(body, (proj, b_f_pad, dc), name="fgate_bwd", grid=(1,),
                 in_specs=[pl.BlockSpec((tp, LANES), lambda i: (0, fblk)), _full((1, LANES)), _full((tp, LANES))],
                 out_specs=[_full((tp, LANES)), _full((SUBLANES, LANES))],
                 out_shape=[jax.ShapeDtypeStruct((tp, LANES), BF16),
                            jax.ShapeDtypeStruct((SUBLANES, LANES), F32)],
                 scratch_shapes=[pltpu.VMEM((tp, LANES), F32)], semantics=("arbitrary",))


ATT_BQ = 128


ATT_BUCKET = 2
ATT_HEADS = 4


def _for_bucket(i, nq, fn):
    for lo in range(0, nq, ATT_BUCKET):
        hi = min(lo + ATT_BUCKET, nq)
        spans = ([(0, lo * ATT_BQ, False)] if lo else []) + [(lo * ATT_BQ, hi * ATT_BQ, True)]
        pl.when(jnp.logical_and(i >= lo, i < hi))(functools.partial(fn, spans))


def _head_column(c_blk, h):
    lane = lax.broadcasted_iota(jnp.int32, c_blk.shape, 1)
    return jnp.sum(jnp.where(lane == h, c_blk, 0.0), axis=1, keepdims=True)


def _head_columns_into(p, c_ref, ck_s):
    for hh in range(ATT_HEADS):
        ck_s[hh] = jnp.broadcast_to(_head_column(c_ref[...], ATT_HEADS * p + hh), ck_s.shape[1:])


def _pair_diag_cols(x2):
    top = lax.broadcasted_iota(jnp.int32, (LANES, ATT_BQ), 0) < HEAD_DIM
    xt = x2.astype(F32).T.astype(BF16)
    return jnp.concatenate([jnp.where(top, xt, 0), jnp.where(top, 0, xt)], axis=1)


def _pair_diag_rows(x2):
    low = lax.broadcasted_iota(jnp.int32, (ATT_BQ, LANES), 1) < HEAD_DIM
    return jnp.concatenate([jnp.where(low, x2, 0), jnp.where(low, 0, x2)], axis=0)


def _seen_keys(k0, k1, q0):
    keys = k0 + lax.broadcasted_iota(jnp.int32, (k1 - k0, ATT_BQ), 0)
    return keys <= q0 + lax.broadcasted_iota(jnp.int32, (k1 - k0, ATT_BQ), 1)


def _attn_fwd(qkv, c, c_t, nh):
    tp = qkv.shape[0]
    att_w = nh * HEAD_DIM
    ng = nh // ATT_HEADS
    gw = ATT_HEADS * HEAD_DIM
    bq = ATT_BQ
    nq = tp // bq
    pair = 2 * HEAD_DIM
    assert pair == LANES and ATT_HEADS % 2 == 0

    def body(q_ref, k_ref, v_ref, c_ref, ct_ref, o_ref, lse_ref, ck_s, vt_s):
        p = pl.program_id(0)
        i = pl.program_id(1)

        @pl.when(i == 0)
        def _():
            _head_columns_into(p, c_ref, ck_s)
            vt_s[...] = v_ref[...].astype(F32).T.astype(BF16)

        def compute(spans):
            q0 = pl.multiple_of(i * bq, bq)
            o_t, lses = [], []
            for pi in range(ATT_HEADS // 2):
                lo = pair * pi
                heads = (2 * pi, 2 * pi + 1)
                q_cols = _pair_diag_cols(q_ref[:, lo:lo + pair])
                ts = []
                for k0, k1, needs_mask in spans:
                    t2 = _dot(k_ref[k0:k1, lo:lo + pair], q_cols)
                    t_e = [t2[:, e * bq:(e + 1) * bq] - ck_s[hh, k0:k1, :] for e, hh in enumerate(heads)]
                    if needs_mask:
                        seen = _seen_keys(k0, k1, q0)
                        t_e = [jnp.where(seen, t, NEG_BIG) for t in t_e]
                    ts.append(t_e)
                ms = [functools.reduce(jnp.maximum, [jnp.max(t[e], axis=0, keepdims=True) for t in ts])
                      for e in range(2)]
                es = [[jnp.exp(t[e] - ms[e]) for e in range(2)] for t in ts]
                ls = [sum(jnp.sum(e_[e], axis=0, keepdims=True) for e_ in es) for e in range(2)]
                o2 = sum(_dot(vt_s[lo:lo + pair, k0:k1],
                              jnp.concatenate([e_[0].astype(BF16), e_[1].astype(BF16)], axis=1))
                         for e_, (k0, k1, _) in zip(es, spans))
                o_t += [o2[:HEAD_DIM, :bq] / ls[0], o2[HEAD_DIM:, bq:] / ls[1]]
                lses += [ms[e] + ct_ref[pl.ds(ATT_HEADS * p + hh, 1), :] + jnp.log(ls[e])
                         for e, hh in enumerate(heads)]
            o_ref[...] = jnp.concatenate(o_t, axis=0).T
            lse_ref[...] = jnp.concatenate(lses, axis=0)

        _for_bucket(i, nq, compute)

    blk = pl.BlockSpec((bq, gw), lambda p, i: (i, p))
    vm = 6 * _nbytes((tp, gw), BF16) + 2 * ATT_HEADS * _nbytes((tp, LANES), F32) + 2 * _nbytes((tp, LANES), F32) \
        + 8 * ATT_HEADS * _nbytes((bq, tp), F32)
    return _call(body, (qkv, qkv, qkv, c, c_t), name="attn_fwd", grid=(ng, nq),
                 in_specs=[blk,
                           pl.BlockSpec((tp, gw), lambda p, i: (0, ng + p)),
                           pl.BlockSpec((tp, gw), lambda p, i: (0, 2 * ng + p)),
                           _full((tp, LANES)), pl.BlockSpec((nh, bq), lambda p, i: (0, i))],
                 out_specs=[blk, pl.BlockSpec((None, ATT_HEADS, bq), lambda p, i: (p, 0, i))],
                 out_shape=[jax.ShapeDtypeStruct((tp, att_w), F32), jax.ShapeDtypeStruct((ng, ATT_HEADS, tp), F32)],
                 scratch_shapes=[pltpu.VMEM((ATT_HEADS, tp, LANES), F32), pltpu.VMEM((gw, tp), BF16)],
                 semantics=("arbitrary", "arbitrary"), vmem_bytes=vm)


def _attn_bwd(qkv, c, c_t, lse, do, nh):
    tp = qkv.shape[0]
    att_w = nh * HEAD_DIM
    ng = nh // ATT_HEADS
    gw = ATT_HEADS * HEAD_DIM
    bq = ATT_BQ
    nq = tp // bq
    pair = 2 * HEAD_DIM
    assert pair == LANES and ATT_HEADS % 2 == 0
    scale = 1.0 / math.sqrt(HEAD_DIM)

    def body(q_ref, k_ref, v_ref, c_ref, ct_ref, lse_ref, do_ref, dq_ref, dk_ref, dv_ref, dc_ref,
             dk_s, dv_s, dc_s, ck_s, kt_s):
        p = pl.program_id(0)
        i = pl.program_id(1)

        @pl.when(i == 0)
        def _():
            dk_s[...] = jnp.zeros_like(dk_s)
            dv_s[...] = jnp.zeros_like(dv_s)
            dc_s[...] = jnp.zeros_like(dc_s)
            kt_s[...] = k_ref[...].astype(F32).T.astype(BF16)
            _head_columns_into(p, c_ref, ck_s)

        @pl.when(jnp.logical_and(i == 0, p == 0))
        def _():
            dc_ref[...] = jnp.zeros_like(dc_ref)

        def compute(spans):
            q0 = pl.multiple_of(i * bq, bq)
            dq_t = []
            for pi in range(ATT_HEADS // 2):
                lo = pair * pi
                q2 = q_ref[:, lo:lo + pair]
                do2 = do_ref[:, lo:lo + pair].astype(BF16)
                q_cols, do_cols = _pair_diag_cols(q2), _pair_diag_cols(do2)
                q_rows, do_rows = _pair_diag_rows(q2), _pair_diag_rows(do2)
                heads = (2 * pi, 2 * pi + 1)
                col_terms = [ct_ref[pl.ds(ATT_HEADS * p + hh, 1), :] - lse_ref[hh:hh + 1, :] for hh in heads]
                prs, dps = [], []
                for k0, k1, needs_mask in spans:
                    t2 = _dot(k_ref[k0:k1, lo:lo + pair], q_cols)
                    dp2 = _dot(v_ref[k0:k1, lo:lo + pair], do_cols)
                    if needs_mask:
                        seen = _seen_keys(k0, k1, q0)
                    pr_e, dp_e = [], []
                    for e, hh in enumerate(heads):
                        t = t2[:, e * bq:(e + 1) * bq] - ck_s[hh, k0:k1, :]
                        if needs_mask:
                            t = jnp.where(seen, t, NEG_BIG)
                        pr_e.append(jnp.exp(t + col_terms[e]))
                        dp_e.append(dp2[:, e * bq:(e + 1) * bq])
                    prs.append(pr_e)
                    dps.append(dp_e)
                key_sums = [sum(jnp.sum(pr[e] * dp[e], axis=0, keepdims=True) for pr, dp in zip(prs, dps))
                            for e in range(2)]
                dq2 = 0.0
                for (k0, k1, _), pr, dp in zip(spans, prs, dps):
                    ds = [pr[e] * (dp[e] - key_sums[e]) for e in range(2)]
                    for e, hh in enumerate(heads):
                        dc_s[hh, k0:k1, :] += jnp.sum(ds[e], axis=1, keepdims=True)
                    ds2 = jnp.concatenate([ds[0].astype(BF16), ds[1].astype(BF16)], axis=1)
                    pr2 = jnp.concatenate([pr[0].astype(BF16), pr[1].astype(BF16)], axis=1)
                    dk_s[k0:k1, lo:lo + pair] += _dot(ds2, q_rows)
                    dv_s[k0:k1, lo:lo + pair] += _dot(pr2, do_rows)
                    dq2 = dq2 + _dot(kt_s[lo:lo + pair, k0:k1], ds2)
                dq_t.append(jnp.concatenate([dq2[:HEAD_DIM, :bq], dq2[HEAD_DIM:, bq:]], axis=0))
            dq_ref[...] = (jnp.concatenate(dq_t, axis=0) * scale).T.astype(BF16)

        _for_bucket(i, nq, compute)

        @pl.when(i == nq - 1)
        def _():
            dk_ref[...] = dk_s[...].astype(BF16)
            dv_ref[...] = dv_s[...].astype(BF16)
            lane = lax.broadcasted_iota(jnp.int32, (tp, LANES), 1)
            dc = dc_ref[...]
            for hh in range(ATT_HEADS):
                dc = jnp.where(lane == ATT_HEADS * p + hh, -dc_s[hh], dc)
            dc_ref[...] = dc

    blk = pl.BlockSpec((bq, gw), lambda p, i: (i, p))
    col = pl.BlockSpec((tp, gw), lambda p, i: (0, p))
    vm = 7 * _nbytes((tp, gw), BF16) + 2 * _nbytes((tp, gw), F32) + 2 * ATT_HEADS * _nbytes((tp, LANES), F32) \
        + 2 * _nbytes((tp, LANES), F32) + 12 * ATT_HEADS * _nbytes((bq, tp), F32)
    return _call(body, (qkv, qkv, qkv, c, c_t, lse, do), name="attn_bwd", grid=(ng, nq),
                 in_specs=[blk,
                           pl.BlockSpec((tp, gw), lambda p, i: (0, ng + p)),
                           pl.BlockSpec((tp, gw), lambda p, i: (0, 2 * ng + p)),
                           _full((tp, LANES)), pl.BlockSpec((nh, bq), lambda p, i: (0, i)),
                           pl.BlockSpec((None, ATT_HEADS, bq), lambda p, i: (p, 0, i)), blk],
                 out_specs=[blk, col, col, _full((tp, LANES))],
                 out_shape=[jax.ShapeDtypeStruct((tp, att_w), BF16)] * 3 + [jax.ShapeDtypeStruct((tp, LANES), F32)],
                 scratch_shapes=[pltpu.VMEM((tp, gw), F32), pltpu.VMEM((tp, gw), F32),
                                 pltpu.VMEM((ATT_HEADS, tp, 1), F32), pltpu.VMEM((ATT_HEADS, tp, LANES), F32),
                                 pltpu.VMEM((gw, tp), BF16)],
                 semantics=("arbitrary", "arbitrary"), vmem_bytes=vm)


REC_ROWS = 128
HALO = SUBLANES


def _conv_taps(cat):
    taps = []
    for k in range(CONV_WIDTH):
        sh = CONV_WIDTH - 1 - k
        taps.append((pltpu.roll(cat, sh, 0) if sh else cat)[HALO:])
    return taps


def _rec_gates(xc, wa_ref, ba_ref, wx_ref, bx_ref, l_ref):
    xcb = xc.astype(BF16)
    r = _sigmoid(_dot(xcb, wa_ref[...]) + ba_ref[...])
    ig = _sigmoid(_dot(xcb, wx_ref[...]) + bx_ref[...])
    ls = _log_sigmoid(l_ref[...])
    log_a = RG_C * r * ls
    return xcb, r, ig, ls, log_a


def _rec_fwd(proj, xr_blk, yr_blk, rec_w, conv_w, conv_b, wa, ba, wx, bx, lru):
    tp = proj.shape[0]
    w = rec_w
    r_rows = REC_ROWS
    nc = tp // r_rows

    def body(xr_ref, yr_ref, cw_ref, cb_ref, wa_ref, ba_ref, wx_ref, bx_ref, l_ref,
             hr_ref, rec_ref, prev_s, carry_s, a_s, u_s):
        i = pl.program_id(0)

        @pl.when(i == 0)
        def _():
            prev_s[...] = jnp.zeros_like(prev_s)
            carry_s[...] = jnp.zeros_like(carry_s)

        x = xr_ref[...]
        taps = _conv_taps(jnp.concatenate([prev_s[...], x], axis=0))
        prev_s[...] = x[r_rows - HALO:]
        xc = cb_ref[...]
        for k in range(CONV_WIDTH):
            xc = xc + cw_ref[k:k + 1, :] * taps[k]
        _, r, ig, ls, log_a = _rec_gates(xc, wa_ref, ba_ref, wx_ref, bx_ref, l_ref)
        a = jnp.exp(log_a)
        a_s[...] = a
        u_s[...] = jnp.sqrt(_one_minus_exp(2.0 * log_a, a * a)) * ig * xc

        def tile(j, h):
            r0 = pl.multiple_of(j * SUBLANES, SUBLANES)
            at = a_s[pl.ds(r0, SUBLANES), :]
            ut = u_s[pl.ds(r0, SUBLANES), :]
            out = []
            for rr in range(SUBLANES):
                h = at[rr:rr + 1] * h + ut[rr:rr + 1]
                out.append(h)
            hr_ref[pl.ds(r0, SUBLANES), :] = jnp.concatenate(out, axis=0)
            return h

        carry_s[0:1, :] = lax.fori_loop(0, r_rows // SUBLANES, tile, carry_s[0:1, :])
        g, _ = _gelu_and_grad(yr_ref[...])
        rec_ref[...] = hr_ref[...] * g

    blk = pl.BlockSpec((r_rows, w), lambda i: (i, 0))
    vm = 16 * _nbytes((r_rows, w), F32) + 4 * _nbytes((w, w), BF16)
    return _call(body, (proj, proj, conv_w, conv_b, wa, ba, wx, bx, lru), name="rec_fwd", grid=(nc,),
                 in_specs=[pl.BlockSpec((r_rows, w), lambda i: (i, xr_blk)),
                           pl.BlockSpec((r_rows, w), lambda i: (i, yr_blk)),
                           _full((CONV_WIDTH, w)), _full((1, w)), _full((w, w)), _full((1, w)),
                           _full((w, w)), _full((1, w)), _full((1, w))],
                 out_specs=[blk, blk],
                 out_shape=[jax.ShapeDtypeStruct((tp, w), F32)] * 2,
                 scratch_shapes=[pltpu.VMEM((HALO, w), F32), pltpu.VMEM((SUBLANES, w), F32),
                                 pltpu.VMEM((r_rows, w), F32), pltpu.VMEM((r_rows, w), F32)],
                 semantics=("arbitrary",), vmem_bytes=vm)


def _rec_bwd(proj, xr_blk, yr_blk, rec_w, hr, drec, conv_w, conv_b, wa, ba, wx, bx, lru):
    tp = proj.shape[0]
    w = rec_w
    r_rows = REC_ROWS
    nc = tp // r_rows
    hpc = r_rows // HALO

    def body(xr_ref, xh_ref, yr_ref, hr_ref, hh_ref, drec_ref, cw_ref, cb_ref, wa_ref, ba_ref, wx_ref, bx_ref,
             l_ref, dxr_ref, dyr_ref, dwa_ref, dwx_ref, small_ref, lam_s, a_s, dhr_s, carry_s, next_s):
        i = pl.program_id(0)
        first = (nc - 1 - i) == 0

        @pl.when(i == 0)
        def _():
            carry_s[...] = jnp.zeros_like(carry_s)
            next_s[...] = jnp.zeros_like(next_s)
            dwa_ref[...] = jnp.zeros_like(dwa_ref)
            dwx_ref[...] = jnp.zeros_like(dwx_ref)
            small_ref[...] = jnp.zeros_like(small_ref)

        x = xr_ref[...]
        xprev = jnp.where(first, 0.0, xh_ref[...])
        taps = _conv_taps(jnp.concatenate([xprev, x], axis=0))
        xc = cb_ref[...]
        for k in range(CONV_WIDTH):
            xc = xc + cw_ref[k:k + 1, :] * taps[k]
        xcb, r, ig, ls, log_a = _rec_gates(xc, wa_ref, ba_ref, wx_ref, bx_ref, l_ref)
        a = jnp.exp(log_a)
        a2 = a * a
        mult = jnp.sqrt(_one_minus_exp(2.0 * log_a, a2))
        g, dg = _gelu_and_grad(yr_ref[...])
        hr_v = hr_ref[...]
        drec_v = drec_ref[...]
        dhr_s[...] = drec_v * g
        dyr_ref[...] = (drec_v * hr_v * dg).astype(BF16)
        a_s[...] = a

        def tile(jj, carry):
            r0 = pl.multiple_of((r_rows // SUBLANES - 1 - jj) * SUBLANES, SUBLANES)
            at = a_s[pl.ds(r0, SUBLANES), :]
            dt = dhr_s[pl.ds(r0, SUBLANES), :]
            out = [None] * SUBLANES
            for rr in range(SUBLANES - 1, -1, -1):
                lam = dt[rr:rr + 1] + carry
                out[rr] = lam
                carry = at[rr:rr + 1] * lam
            lam_s[pl.ds(r0, SUBLANES), :] = jnp.concatenate(out, axis=0)
            return carry

        carry_s[0:1, :] = lax.fori_loop(0, r_rows // SUBLANES, tile, carry_s[0:1, :])
        lam = lam_s[...]
        hprev = jnp.where(first, 0.0, hh_ref[...])
        hr_prev = pltpu.roll(jnp.concatenate([hprev, hr_v], axis=0), 1, 0)[HALO:]
        da = lam * hr_prev
        dxc = lam * mult * ig
        di = lam * mult * xc
        dmult = lam * ig * xc
        dlog_a = da * a - dmult * a2 / mult
        dr = dlog_a * (RG_C * ls)
        dls = jnp.sum(dlog_a * (RG_C * r), axis=0, keepdims=True)
        dga = dr * r * (1.0 - r)
        dgx = di * ig * (1.0 - ig)
        dgab = dga.astype(BF16)
        dgxb = dgx.astype(BF16)
        dxc = dxc + _dot_nt(dgab, wa_ref[...]) + _dot_nt(dgxb, wx_ref[...])
        dwa_ref[...] += _dot_tn(xcb, dgab)
        dwx_ref[...] += _dot_tn(xcb, dgxb)
        cat = jnp.concatenate([dxc, next_s[...]], axis=0)
        next_s[...] = dxc[0:HALO]
        dxr = cw_ref[CONV_WIDTH - 1:CONV_WIDTH, :] * dxc
        for k in range(CONV_WIDTH - 1):
            sh = CONV_WIDTH - 1 - k
            dxr = dxr + cw_ref[k:k + 1, :] * pltpu.roll(cat, r_rows + HALO - sh, 0)[:r_rows]
        dxr_ref[...] = dxr.astype(BF16)
        rows = [jnp.sum(dxc * taps[k], axis=0, keepdims=True) for k in range(CONV_WIDTH)]
        rows += [jnp.sum(dxc, axis=0, keepdims=True), jnp.sum(dga, axis=0, keepdims=True),
                 jnp.sum(dgx, axis=0, keepdims=True), dls * _sigmoid(-l_ref[...])]
        small_ref[...] += jnp.concatenate(rows, axis=0)

    def rev(i):
        return nc - 1 - i

    def halo(i):
        return jnp.maximum(rev(i) * hpc - 1, 0)

    blk = pl.BlockSpec((r_rows, w), lambda i: (rev(i), 0))
    vm = 40 * _nbytes((r_rows, w), F32) + 6 * _nbytes((w, w), F32)
    return _call(body, (proj, proj, proj, hr, hr, drec, conv_w, conv_b, wa, ba, wx, bx, lru),
                 name="rec_bwd", grid=(nc,),
                 in_specs=[pl.BlockSpec((r_rows, w), lambda i: (rev(i), xr_blk)),
                           pl.BlockSpec((HALO, w), lambda i: (halo(i), xr_blk)),
                           pl.BlockSpec((r_rows, w), lambda i: (rev(i), yr_blk)),
                           blk,
                           pl.BlockSpec((HALO, w), lambda i: (halo(i), 0)),
                           blk,
                           _full((CONV_WIDTH, w)), _full((1, w)), _full((w, w)), _full((1, w)),
                           _full((w, w)), _full((1, w)), _full((1, w))],
                 out_specs=[blk, blk, _full((w, w)), _full((w, w)), _full((SUBLANES, w))],
                 out_shape=[jax.ShapeDtypeStruct((tp, w), BF16)] * 2
                 + [jax.ShapeDtypeStruct((w, w), F32)] * 2 + [jax.ShapeDtypeStruct((SUBLANES, w), F32)],
                 scratch_shapes=[pltpu.VMEM((r_rows, w), F32)] * 3
                 + [pltpu.VMEM((SUBLANES, w), F32), pltpu.VMEM((HALO, w), F32)],
                 semantics=("arbitrary",), vmem_bytes=vm)


ROW_TARGET = 544


def _mixer_out(attn, rec, g_a, g_r, w_out, h, g_next):
    tp, d = h.shape
    aw, rw = attn.shape[1], rec.shape[1]
    kc = d // N_CHIPS
    tm = _divisor_tile(tp, 16, ROW_TARGET)

    def body(a_ref, r_ref, ga_ref, gr_ref, w_ref, h_ref, gn_ref, h1_ref, z_ref, mix_ref):
        a = a_ref[...]
        r = r_ref[...]
        mix = jnp.concatenate([a * _rstd(a) * ga_ref[...], r * _rstd(r) * gr_ref[...]], axis=1).astype(BF16)
        mix_ref[...] = mix
        h1 = h_ref[...]
        for j in range(N_CHIPS):
            h1 = h1 + _dot(mix[:, j * kc:(j + 1) * kc], w_ref[j])
        h1_ref[...] = h1
        z_ref[...] = (h1 * _rstd(h1) * gn_ref[...]).astype(BF16)

    row = lambda wd: pl.BlockSpec((tm, wd), lambda i: (i, 0))
    vm = 2 * _nbytes((d, d), BF16) + 12 * _nbytes((tm, d), F32)
    return _call(body, (attn, rec, g_a, g_r, w_out, h, g_next), name="mixer_out", grid=(tp // tm,),
                 in_specs=[row(aw), row(rw), _full((1, aw)), _full((1, rw)), _full(w_out.shape), row(d),
                           _full((1, d))],
                 out_specs=[row(d), row(d), row(d)],
                 out_shape=[jax.ShapeDtypeStruct((tp, d), F32), jax.ShapeDtypeStruct((tp, d), BF16),
                            jax.ShapeDtypeStruct((tp, d), BF16)],
                 semantics=("parallel",), vmem_bytes=vm)


def _mixer_bwd(dh_b, w_out, attn, rec, g_a, g_r, mix):
    tp, d = dh_b.shape
    aw, rw = attn.shape[1], rec.shape[1]
    tm = _divisor_tile(tp, 16, ROW_TARGET)
    steps = tp // tm

    def body(dh_ref, w_ref, a_ref, r_ref, ga_ref, gr_ref, mix_ref, da_ref, dr_ref, dg_ref, gw_ref, gw_acc):
        @pl.when(pl.program_id(0) == 0)
        def _():
            dg_ref[...] = jnp.zeros_like(dg_ref)
            gw_acc[...] = jnp.zeros_like(gw_acc)

        dh = dh_ref[...]
        gw_acc[...] += _dot_tn(mix_ref[...], dh)

        @pl.when(pl.program_id(0) == steps - 1)
        def _():
            gw_ref[...] = gw_acc[...].astype(BF16)

        dmix = jnp.concatenate([_dot_nt(dh, w_ref[j]) for j in range(N_CHIPS)], axis=1)
        da, dga = _rms_bwd(dmix[:, :aw], a_ref[...], ga_ref[...])
        dr, dgr = _rms_bwd(dmix[:, aw:], r_ref[...], gr_ref[...])
        da_ref[...] = da
        dr_ref[...] = dr
        dg_ref[...] += jnp.broadcast_to(jnp.concatenate([dga, dgr], axis=1), (SUBLANES, d))

    row = lambda wd: pl.BlockSpec((tm, wd), lambda i: (i, 0))
    vm = 4 * _nbytes((d, d), BF16) + 3 * _nbytes((d, d), F32) + 14 * _nbytes((tm, d), F32)
    return _call(body, (dh_b, w_out, attn, rec, g_a, g_r, mix), name="mixer_bwd", grid=(steps,),
                 in_specs=[row(d), _full(w_out.shape), row(aw), row(rw), _full((1, aw)), _full((1, rw)), row(d)],
                 out_specs=[row(aw), row(rw), _full((SUBLANES, d)), _full((d, d))],
                 out_shape=[jax.ShapeDtypeStruct((tp, aw), F32), jax.ShapeDtypeStruct((tp, rw), F32),
                            jax.ShapeDtypeStruct((SUBLANES, d), F32), jax.ShapeDtypeStruct((d, d), BF16)],
                 scratch_shapes=[pltpu.VMEM((d, d), F32)],
                 semantics=("arbitrary",), vmem_bytes=vm)


def _mlp_fwd(z, w_up, w_down, h, g_next):
    tp, d = h.shape
    fc = w_up.shape[2]
    ff = N_CHIPS * fc
    tm = _divisor_tile(tp, 16, ROW_TARGET)

    def body(z_ref, wu_ref, wd_ref, h_ref, gn_ref, up_ref, h2_ref, zn_ref):
        h2 = h_ref[...]
        for j in range(N_CHIPS):
            up = _dot(z_ref[...], wu_ref[j])
            r = jnp.maximum(up, 0.0)
            up_ref[:, j * fc:(j + 1) * fc] = up.astype(BF16)
            h2 = h2 + _dot((r * r).astype(BF16), wd_ref[j])
        h2_ref[...] = h2
        zn_ref[...] = (h2 * _rstd(h2) * gn_ref[...]).astype(BF16)

    row = lambda wd: pl.BlockSpec((tm, wd), lambda i: (i, 0))
    resident = lambda shape: pl.BlockSpec(shape, lambda i: (0,) * len(shape), pipeline_mode=pl.Buffered(1))
    vm = (2 * _nbytes((ff, d), BF16) + 2 * _nbytes((tm, ff), BF16) + 12 * _nbytes((tm, d), F32)
          + 4 * _nbytes((tm, fc), F32))
    return _call(body, (z, w_up, w_down, h, g_next), name="mlp_fwd", grid=(tp // tm,),
                 in_specs=[row(d), resident(w_up.shape), resident(w_down.shape), row(d), _full((1, d))],
                 out_specs=[row(ff), row(d), row(d)],
                 out_shape=[jax.ShapeDtypeStruct((tp, ff), BF16), jax.ShapeDtypeStruct((tp, d), F32),
                            jax.ShapeDtypeStruct((tp, d), BF16)],
                 semantics=("parallel",), vmem_bytes=vm)


def _loss_bwd(h, g, target, n_real):
    tp, d = h.shape
    tm = _divisor_tile(tp, 16, ROW_TARGET)

    def body(h_ref, g_ref, t_ref, dh_ref, dhb_ref, dg_ref, loss_ref):
        i = pl.program_id(0)

        @pl.when(i == 0)
        def _():
            dg_ref[...] = jnp.zeros_like(dg_ref)
            loss_ref[...] = jnp.zeros_like(loss_ref)

        x = h_ref[...]
        gv = g_ref[...]
        rowi = i * tm + lax.broadcasted_iota(jnp.int32, (tm, 1), 0)
        real = jnp.logical_and(rowi >= N_META, rowi < N_META + n_real)
        err = jnp.where(real, x * _rstd(x) * gv - t_ref[...], 0.0)
        loss_ref[...] += 0.5 * jnp.sum(jnp.mean(err * err, axis=-1, keepdims=True))
        dx, dgp = _rms_bwd(err * (1.0 / d), x, gv)
        dh_ref[...] = dx
        dhb_ref[...] = dx.astype(BF16)
        dg_ref[...] += jnp.broadcast_to(dgp, (SUBLANES, d))

    row = pl.BlockSpec((tm, d), lambda i: (i, 0))
    return _call(body, (h, g, target), name="loss_bwd", grid=(tp // tm,),
                 in_specs=[row, _full((1, d)), row],
                 out_specs=[row, row, _full((SUBLANES, d)), _full((SUBLANES, LANES))],
                 out_shape=[jax.ShapeDtypeStruct((tp, d), F32), jax.ShapeDtypeStruct((tp, d), BF16),
                            jax.ShapeDtypeStruct((SUBLANES, d), F32), jax.ShapeDtypeStruct((SUBLANES, LANES), F32)],
                 semantics=("arbitrary",), vmem_bytes=16 * _nbytes((tm, d), F32))


def _mlp_bwd(dh_b, w_down, up, z2):
    tp, d = dh_b.shape
    fc = w_down.shape[1]
    ff = N_CHIPS * fc
    tn = _divisor_tile(fc, LANES, 512)
    per = fc // tn

    def body(dh_ref, z_ref, w_ref, up_ref, dup_ref, gd_ref, gu_ref):
        dh = dh_ref[...]
        r = jnp.maximum(up_ref[...].astype(F32), 0.0)
        dup = (_dot_nt(dh, w_ref[...]) * (2.0 * r)).astype(BF16)
        dup_ref[...] = dup
        gd_ref[...] = _dot_tn((r * r).astype(BF16), dh).astype(BF16)
        gu_ref[...] = _dot_tn(z_ref[...], dup).astype(BF16)

    col = pl.BlockSpec((tp, tn), lambda j: (0, j))
    vm = 4 * _nbytes((tp, d), BF16) + 4 * _nbytes((tn, d), BF16) + 2 * _nbytes((d, tn), BF16) \
        + 10 * _nbytes((tp, tn), F32) + 4 * _nbytes((tn, d), F32)
    return _call(body, (dh_b, z2, w_down, up), name="mlp_bwd", grid=(ff // tn,),
                 in_specs=[_full((tp, d)), _full((tp, d)),
                           pl.BlockSpec((None, tn, d), lambda j: (j // per, j % per, 0)), col],
                 out_specs=[col, pl.BlockSpec((tn, d), lambda j: (j, 0)),
                            pl.BlockSpec((None, d, tn), lambda j: (j // per, 0, j % per))],
                 out_shape=[jax.ShapeDtypeStruct((tp, ff), BF16), jax.ShapeDtypeStruct((ff, d), BF16),
                            jax.ShapeDtypeStruct((N_CHIPS, d, fc), BF16)],
                 semantics=("parallel",), vmem_bytes=vm)


def _grad_w_pieces(pieces, b):
    tp, n = b.shape
    tn = _divisor_tile(n, LANES, 512)
    widths = [pc.shape[1] for pc in pieces]

    def body(*refs):
        p_refs, b_ref, o_refs = refs[:len(pieces)], refs[len(pieces)], refs[len(pieces) + 1:]
        for p_ref, o_ref in zip(p_refs, o_refs):
            o_ref[...] = _dot_tn(p_ref[...], b_ref[...]).astype(BF16)

    vm = 2 * sum(_nbytes((tp, wd), BF16) for wd in widths) + 2 * _nbytes((tp, tn), BF16) \
        + 4 * sum(_nbytes((wd, tn), F32) for wd in widths) + 2 * _nbytes((tp, max(widths)), F32)
    return _call(body, tuple(pieces) + (b,), name="grad_w_pieces", grid=(n // tn,),
                 in_specs=[_full(pc.shape) for pc in pieces] + [pl.BlockSpec((tp, tn), lambda j: (0, j))],
                 out_specs=[pl.BlockSpec((wd, tn), lambda j: (0, j)) for wd in widths],
                 out_shape=[jax.ShapeDtypeStruct((wd, n), BF16) for wd in widths],
                 semantics=("parallel",), vmem_bytes=vm)


def _dx_norm_bwd(pieces, w, w_spec, w_piece, h, g, dres, dot=_dot_nt):
    tp, d = h.shape
    tm = _divisor_tile(tp, 16, ROW_TARGET)
    n = len(pieces)

    def body(*refs):
        dy_refs = refs[:n]
        w_ref, h_ref, g_ref, dres_ref, dh_ref, dhb_ref, dg_ref = refs[n:]

        @pl.when(pl.program_id(0) == 0)
        def _():
            dg_ref[...] = jnp.zeros_like(dg_ref)

        dz = dot(dy_refs[0][...], w_piece(w_ref, 0))
        for i in range(1, n):
            dz = dz + dot(dy_refs[i][...], w_piece(w_ref, i))
        dx, dgp = _rms_bwd(dz, h_ref[...], g_ref[...])
        dh = dres_ref[...] + dx
        dh_ref[...] = dh
        dhb_ref[...] = dh.astype(BF16)
        dg_ref[...] += jnp.broadcast_to(dgp, (SUBLANES, d))

    row = lambda wd: pl.BlockSpec((tm, wd), lambda i: (i, 0))
    kk = sum(wd for _, _, wd in pieces)
    vm = 2 * _nbytes((d, kk), BF16) + 2 * _nbytes((tm, kk), BF16) + 14 * _nbytes((tm, d), F32)
    piece_specs = [pl.BlockSpec((tm, wd), functools.partial(lambda i, cb: (i, cb), cb=cb)) for _, cb, wd in pieces]
    return _call(body, tuple(a for a, _, _ in pieces) + (w, h, g, dres), name="dx_norm_bwd", grid=(tp // tm,),
                 in_specs=piece_specs + [w_spec, row(d), _full((1, d)), row(d)],
                 out_specs=[row(d), row(d), _full((SUBLANES, d))],
                 out_shape=[jax.ShapeDtypeStruct((tp, d), F32), jax.ShapeDtypeStruct((tp, d), BF16),
                            jax.ShapeDtypeStruct((SUBLANES, d), F32)],
                 semantics=("arbitrary",), vmem_bytes=vm)


def _block_diag(wg):
    nb, b, _ = wg.shape
    eye = jnp.eye(nb, dtype=wg.dtype)
    return (eye[:, None, :, None] * wg[:, :, None, :]).reshape(nb * b, nb * b)


def _diag_blocks(dense, nb):
    b = dense.shape[0] // nb
    d4 = dense.reshape(nb, b, nb, b)
    return jnp.stack([d4[i, :, i, :] for i in range(nb)])


def _row(v):
    return v.reshape(1, -1)


def _forward_layer(l, h, z, p, fetch, stage_next, g_next):
    d = h.shape[1]
    att_w = d // 2
    rec_w = d - att_w
    nh = att_w // HEAD_DIM
    wa_d = _block_diag(p["w_gate_a"][l]).astype(BF16)
    wx_d = _block_diag(p["w_gate_x"][l]).astype(BF16)
    b_f_pad = jnp.zeros((1, LANES), F32).at[0, :nh].set(p["b_f"][l])
    w_in_t = fetch("w_in", z)
    big = dict(w_in_big=_pack_w_in_t(w_in_t.reshape(-1, d), att_w, nh))
    qkv, proj = _proj(z, big["w_in_big"], att_w)
    c, c_t = _fgate_fwd(proj, b_f_pad, nh)
    attn, lse_b = _attn_fwd(qkv, c, c_t, nh)
    hr, rec = _rec_fwd(proj, 0, 1, rec_w, p["conv_w"][l], _row(p["conv_b"][l]), wa_d,
                       _row(p["b_gate_a"][l]), wx_d, _row(p["b_gate_x"][l]), _row(p["lru_L"][l]))
    tok = stage_next(attn, "own")
    big["w_out"] = fetch("w_out", rec)
    h1, z2, mix = _mixer_out(attn, rec, _row(p["attn_out_g"][l] + tok), _row(p["rec_out_g"][l]),
                             big["w_out"], h, _row(p["mlp_norm_g"][l]))
    tok = stage_next(h1, "next")
    big["w_up"] = fetch("w_up", h1)
    big["w_down"] = fetch("w_down", h1)
    up, h2, z_next = _mlp_fwd(z2, big["w_up"], big["w_down"], h1, _row(g_next + tok))
    saved = dict(h0=h, z1=z, proj=proj, qkv=qkv, c=c, c_t=c_t, attn=attn, lse_b=lse_b, hr=hr, rec=rec, h1=h1,
                 z2=z2, mix=mix, up=up, wa_d=wa_d, wx_d=wx_d, b_f_pad=b_f_pad, big=big)
    return h2, z_next, saved


def _backward_mlp(l, dh, dh_b, sv, p, tok):
    w_up, w_down = sv["big"]["w_up"], sv["big"]["w_down"]
    fc = w_up.shape[2]
    dup, g_down, g_up = _mlp_bwd(dh_b, w_down, sv["up"], sv["z2"])
    dh, dh_b, dg2 = _dx_norm_bwd([(dup, j, fc) for j in range(N_CHIPS)], w_up, _full(w_up.shape),
                                 lambda w_ref, j: w_ref[j], sv["h1"], _row(p["mlp_norm_g"][l] + tok), dh)
    big = dict(w_down=g_down.reshape((N_CHIPS, -1) + g_down.shape[1:]), w_up=g_up)
    return dh, dh_b, big, dict(mlp_norm_g=dg2[0])


def _backward_mixer(l, dh, dh_b, sv, p, send_w_out):
    d = dh.shape[1]
    att_w = d // 2
    rec_w = d - att_w
    nh = att_w // HEAD_DIM
    small = {}
    dattn, drec, dg_mix, g_out = _mixer_bwd(dh_b, sv["big"]["w_out"], sv["attn"], sv["rec"],
                                            _row(p["attn_out_g"][l]), _row(p["rec_out_g"][l]), sv["mix"])
    tok = send_w_out(g_out)
    small["attn_out_g"] = dg_mix[0, :att_w]
    small["rec_out_g"] = dg_mix[0, att_w:]
    dxr, dyr, dwa, dwx, sm = _rec_bwd(
        sv["proj"], 0, 1, rec_w, sv["hr"], drec, p["conv_w"][l], _row(p["conv_b"][l] + tok), sv["wa_d"],
        _row(p["b_gate_a"][l]), sv["wx_d"], _row(p["b_gate_x"][l]), _row(p["lru_L"][l]))
    small.update(conv_w=sm[:CONV_WIDTH], conv_b=sm[4], b_gate_a=sm[5], b_gate_x=sm[6], lru_L=sm[7],
                 w_gate_a=_diag_blocks(dwa, N_REC_BLOCKS), w_gate_x=_diag_blocks(dwx, N_REC_BLOCKS))
    dq, dk, dv, dc = _attn_bwd(sv["qkv"], sv["c"], sv["c_t"], sv["lse_b"], dattn, nh)
    df, db_f = _fgate_bwd(sv["proj"], sv["b_f_pad"], dc)
    small["b_f"] = db_f[0, :nh]
    pieces = [dq, dk, dv, dxr, dyr, df]
    offs = [0, att_w, 2 * att_w, 3 * att_w, 3 * att_w + rec_w, 3 * att_w + 2 * rec_w]
    gq, gk, gv, gxr, gyr, gf = _grad_w_pieces(pieces, sv["z1"])
    g_in_t = jnp.concatenate([gq, gk, gv, gf[:nh], gxr, gyr], axis=0)
    w_big = sv["big"]["w_in_big"]
    widths = [pc.shape[1] for pc in pieces]
    dh, dh_b, dg1 = _dx_norm_bwd(
        [(pc, 0, wd) for pc, wd in zip(pieces, widths)], w_big, _full(w_big.shape),
        lambda w_ref, i: w_ref[offs[i]:offs[i] + widths[i], :], sv["h0"], _row(p["attn_norm_g"][l]), dh, dot=_dot)
    small["attn_norm_g"] = dg1[0]
    big = dict(w_in=g_in_t.reshape(N_CHIPS, -1, d))
    return dh, dh_b, big, small


def _pack_w_in_t(w_in_t, att_w, nh):
    qkv = w_in_t[:3 * att_w]
    f = w_in_t[3 * att_w:3 * att_w + nh]
    xy = w_in_t[3 * att_w + nh:]
    return jnp.concatenate([qkv, xy, f, jnp.zeros((LANES - nh, w_in_t.shape[1]), w_in_t.dtype)], axis=0)


ANY = pl.BlockSpec(memory_space=pl.ANY)


def _coords():
    return lax.axis_index("x"), lax.axis_index("y"), lax.axis_index("c")


def _other_chips(x, y):
    return [(1 - x, y), (x, 1 - y), (1 - x, 1 - y)]


def _remote(src, dst, send_sems, recv_sems, k, to):
    return pltpu.make_async_remote_copy(src_ref=src, dst_ref=dst, send_sem=send_sems.at[k],
                                        recv_sem=recv_sems.at[k], device_id=to, device_id_type=MESH)


def _all_gather_chips(shards):
    n = len(shards)
    per = 6

    def body(*refs):
        ins, outs = refs[:n], refs[n:2 * n]
        send_sems, recv_sems, local_sems = refs[2 * n:]
        x, y, c = _coords()
        me = 2 * x + y
        sibling = (x, y, 1 - c)
        chips = _other_chips(x, y)
        local = [pltpu.make_async_copy(ins[t], outs[t].at[me], local_sems.at[t]) for t in range(n)]
        for cp in local:
            cp.start()
        sends = []
        for t in range(n):
            for j, (px, py) in enumerate(chips):
                cp = _remote(ins[t].at[c], outs[t].at[me, c], send_sems, recv_sems, per * t + j, (px, py, c))
                cp.start()
                sends.append(cp)
        for t in range(n):
            for j, (px, py) in enumerate(chips):
                landed = outs[t].at[2 * px + py, c]
                _remote(landed, landed, send_sems, recv_sems, per * t + j, (px, py, c)).wait_recv()
                cp = _remote(landed, landed, send_sems, recv_sems, per * t + 3 + j, sibling)
                cp.start()
                sends.append(cp)
        for t in range(n):
            for j, (px, py) in enumerate(chips):
                passed = outs[t].at[2 * px + py, 1 - c]
                _remote(passed, passed, send_sems, recv_sems, per * t + 3 + j, sibling).wait_recv()
        for cp in sends:
            cp.wait_send()
        for cp in local:
            cp.wait()

    return _call(body, tuple(shards), name="all_gather_chips",
                 in_specs=[ANY] * n, out_specs=[ANY] * n,
                 out_shape=[jax.ShapeDtypeStruct((N_CHIPS,) + s.shape, s.dtype) for s in shards],
                 scratch_shapes=[pltpu.SemaphoreType.DMA((per * n,)), pltpu.SemaphoreType.DMA((per * n,)),
                                 pltpu.SemaphoreType.DMA((n,))])


HBM = pl.BlockSpec(memory_space=pltpu.HBM)
SEM = pl.BlockSpec(memory_space=pltpu.SEMAPHORE)
DATAFLOW = pltpu.SideEffectType.DATAFLOW_SIDE_EFFECTING


def _in_hbm(a):
    return pltpu.with_memory_space_constraint(a, pltpu.HBM)


PUSH_ARRIVALS = {"gather_chips_half": N_CHIPS - 1, "pass_halves": N_CHIPS - 1, "scatter_chips": N_CHIPS - 1,
                 "sibling": 1, "gather_devices": N_DEV - 1}


def _column_half(ref3, slab, c):
    hw = ref3.shape[2] // 2
    return ref3.at[slab, :, pl.ds(pl.multiple_of(c * hw, LANES), hw)]


def _push_copies(mode, src, land, send_sems, recv_sems, t):
    x, y, c = _coords()
    chip = 2 * x + y
    if mode == "gather_chips_half":
        return [_remote(_column_half(src, chip, c), _column_half(land, chip, c), send_sems, recv_sems, t, (px, py, c))
                for px, py in _other_chips(x, y)]
    if mode == "pass_halves":
        return [_remote(_column_half(src, 2 * px + py, c), _column_half(land, 2 * px + py, c), send_sems, recv_sems, t,
                        (x, y, 1 - c)) for px, py in _other_chips(x, y)]
    if mode == "scatter_chips":
        return [_remote(src.at[2 * px + py], land.at[chip], send_sems, recv_sems, t, (px, py, c))
                for px, py in _other_chips(x, y)]
    if mode == "sibling":
        return [_remote(src, land, send_sems, recv_sems, t, (x, y, 1 - c))]
    dev = 4 * x + 2 * y + c
    return [_remote(src.at[dev], land.at[dev], send_sems, recv_sems, t, (x ^ (k >> 2), y ^ ((k >> 1) & 1), c ^ (k & 1)))
            for k in range(1, N_DEV)]


def _push_start(srcs, lands, mode, name):
    n = len(srcs)
    same = all(s is ld for s, ld in zip(srcs, lands))
    n_in = n if same else 2 * n

    def body(*refs):
        src_refs = refs[:n]
        land_refs = src_refs if same else refs[n:2 * n]
        send_sems, recv_sems = refs[n_in], refs[n_in + 1]
        token = refs[-1]
        for t in range(n):
            for cp in _push_copies(mode, src_refs[t], land_refs[t], send_sems, recv_sems, t):
                cp.start()
        token[...] = jnp.zeros_like(token)

    operands = tuple(srcs) if same else tuple(srcs) + tuple(lands)
    res = _call(
        body, [_in_hbm(a) for a in operands], name=name,
        out_shape=(pltpu.SemaphoreType.DMA((n,)), pltpu.SemaphoreType.DMA((n,)))
        + tuple(pltpu.HBM(a.shape, a.dtype) for a in operands) + (jax.ShapeDtypeStruct((SUBLANES, LANES), F32),),
        in_specs=[HBM] * n_in, out_specs=(SEM, SEM) + (HBM,) * n_in + (pl.BlockSpec(memory_space=pltpu.VMEM),),
        input_output_aliases={i: 2 + i for i in range(n_in)}, side_effects=DATAFLOW, hbm_results=False)
    send_sems, recv_sems, token = res[0], res[1], res[-1]
    srcs_thru = res[2:2 + n]
    lands_thru = srcs_thru if same else res[2 + n:2 + 2 * n]
    return send_sems, recv_sems, srcs_thru, lands_thru, token


def _push_wait(send_sems, recv_sems, ids, srcs, lands, mode, after, name):
    n = len(lands)
    same = all(s is ld for s, ld in zip(srcs, lands))
    n_in = n if same else 2 * n

    def body(*refs):
        land_refs = refs[:n] if same else refs[n:2 * n]
        send_sems, recv_sems = refs[n_in], refs[n_in + 1]
        x, y, c = _coords()
        for t in range(n):
            if mode == "sibling":
                moved = land_refs[t]
            elif mode in ("gather_chips_half", "pass_halves"):
                moved = land_refs[t].at[pl.ds(0, PUSH_ARRIVALS[mode]), :, pl.ds(0, land_refs[t].shape[2] // 2)]
            else:
                moved = land_refs[t].at[pl.ds(0, PUSH_ARRIVALS[mode])]
            arrivals = _remote(moved, moved, send_sems, recv_sems, ids[t], (x, y, c))
            arrivals.wait_send()
            arrivals.wait_recv()

    operands = tuple(lands) if same else tuple(srcs) + tuple(lands)
    res = _call(
        body, operands + (send_sems, recv_sems, after), name=name,
        out_shape=tuple(pltpu.HBM(a.shape, a.dtype) for a in operands),
        in_specs=[HBM] * n_in + [SEM, SEM, ANY], out_specs=(HBM,) * n_in,
        input_output_aliases={i: i for i in range(n_in)}, side_effects=DATAFLOW)
    return list(res) if same else (list(res[:n]), list(res[n:]))


def _sum_partials(part, landed, chip):
    _, rows, cols = part.shape
    br = _divisor_tile(rows, 16, ELEM_ROWS)

    def body(chip_ref, own_ref, a_ref, b_ref, c_ref, o_ref):
        o_ref[...] = ((own_ref[...].astype(F32) + a_ref[...].astype(F32)) + b_ref[...].astype(F32)) \
            + c_ref[...].astype(F32)

    def other(k):
        return pl.BlockSpec((None, br, cols), lambda i, ch: (jnp.where(ch[0] <= k, k + 1, k), i, 0))

    spec = pltpu.PrefetchScalarGridSpec(
        num_scalar_prefetch=1, grid=(rows // br,),
        in_specs=[pl.BlockSpec((None, br, cols), lambda i, ch: (ch[0], i, 0)), other(0), other(1), other(2)],
        out_specs=pl.BlockSpec((br, cols), lambda i, ch: (i, 0)))
    return _call(body, (chip, part, landed, landed, landed), name="sum_partials", grid_spec=spec,
                 out_shape=jax.ShapeDtypeStruct((rows, cols), F32), semantics=("parallel",))


def _cast_to_slab(w, l, chip):
    _, rows, cols = w.shape
    br = _divisor_tile(rows, 16, ELEM_ROWS)

    def body(chip_ref, w_ref, o_ref):
        o_ref[...] = w_ref[...].astype(BF16)

    spec = pltpu.PrefetchScalarGridSpec(
        num_scalar_prefetch=1, grid=(rows // br,),
        in_specs=[pl.BlockSpec((None, br, cols), lambda i, ch: (l, i, 0))],
        out_specs=pl.BlockSpec((None, br, cols), lambda i, ch: (ch[0], i, 0)))
    return _call(body, (chip, w), name="cast_to_slab", grid_spec=spec,
                 out_shape=jax.ShapeDtypeStruct((N_CHIPS, rows, cols), BF16), semantics=("parallel",))


def _cast_w_in_t_to_slabs(w_t, chip):
    rows, depth, d = w_t.shape
    tn = _divisor_tile(d, LANES, 256)

    def body(chip_ref, w_ref, *o_refs):
        for l in range(depth):
            o_refs[l][...] = w_ref[:, l, :].astype(BF16)

    spec = pltpu.PrefetchScalarGridSpec(
        num_scalar_prefetch=1, grid=(d // tn,),
        in_specs=[pl.BlockSpec((rows, depth, tn), lambda j, ch: (0, 0, j))],
        out_specs=[pl.BlockSpec((None, rows, tn), lambda j, ch: (ch[0], 0, j))] * depth)
    return _call(body, (chip, w_t), name="cast_w_in_t_to_slabs", grid_spec=spec,
                 out_shape=[jax.ShapeDtypeStruct((N_CHIPS, rows, d), BF16)] * depth, semantics=("parallel",),
                 vmem_bytes=4 * _nbytes((rows, max(depth, SUBLANES), tn), F32))


def _place_slab(buf, index, n_slabs):
    rows, cols = buf.shape
    br = _divisor_tile(rows, SUBLANES, ELEM_ROWS)

    def body(index_ref, b_ref, o_ref):
        o_ref[...] = b_ref[...]

    spec = pltpu.PrefetchScalarGridSpec(
        num_scalar_prefetch=1, grid=(rows // br,),
        in_specs=[pl.BlockSpec((br, cols), lambda i, ix: (i, 0))],
        out_specs=pl.BlockSpec((None, br, cols), lambda i, ix: (ix[0], i, 0)))
    return _call(body, (index, buf), name="place_slab", grid_spec=spec,
                 out_shape=jax.ShapeDtypeStruct((n_slabs, rows, cols), buf.dtype), semantics=("parallel",))


ELEM_ROWS = 256


def _sum_slabs(r):
    n, rows, cols = r.shape
    br = _divisor_tile(rows, 16, ELEM_ROWS)

    def body(r_ref, o_ref):
        acc = r_ref[0].astype(F32)
        for j in range(1, n):
            acc = acc + r_ref[j].astype(F32)
        o_ref[...] = acc

    return _call(body, (r,), name="sum_slabs", grid=(rows // br,),
                 in_specs=[pl.BlockSpec((n, br, cols), lambda i: (0, i, 0))],
                 out_specs=pl.BlockSpec((br, cols), lambda i: (i, 0)),
                 out_shape=jax.ShapeDtypeStruct((rows, cols), F32), semantics=("parallel",))


def _adamw_math(w, g, m, v):
    c1 = 1.0 - ADAM_B1 ** ADAM_STEP
    c2 = 1.0 - ADAM_B2 ** ADAM_STEP
    nm = ADAM_B1 * m + (1.0 - ADAM_B1) * g
    nv = ADAM_B2 * v + (1.0 - ADAM_B2) * (g * g)
    delta = -ADAM_LR * ((nm / c1) / (jnp.sqrt(nv / c2) + ADAM_EPS) + ADAM_WD * w)
    return delta, nm, nv


def _adamw(w, g, m, v):
    rows, cols = w.shape
    br = _divisor_tile(rows, 8, ELEM_ROWS)

    def body(w_ref, g_ref, m_ref, v_ref, d_ref, nm_ref, nv_ref):
        d_ref[...], nm_ref[...], nv_ref[...] = _adamw_math(w_ref[...], g_ref[...], m_ref[...], v_ref[...])

    blk = pl.BlockSpec((br, cols), lambda i: (i, 0))
    return _call(body, (w, g, m, v), name="adamw", grid=(rows // br,),
                 in_specs=[blk] * 4, out_specs=[blk] * 3,
                 out_shape=[jax.ShapeDtypeStruct((rows, cols), F32)] * 3, semantics=("parallel",))


def _adamw_w_in_t(w_t, m_t, v_t, g_mine, g_theirs):
    rows, depth, d = w_t.shape
    tn = LANES

    def body(w_ref, m_ref, v_ref, *rest):
        ga_refs, gb_refs = rest[:depth], rest[depth:2 * depth]
        g_ref, d_ref, nm_ref, nv_ref = rest[2 * depth:]
        for l in range(depth):
            g_ref[:, l, :] = ga_refs[l][...] + gb_refs[l][...]
        d_ref[...], nm_ref[...], nv_ref[...] = _adamw_math(w_ref[...], g_ref[...], m_ref[...], v_ref[...])

    slab = pl.BlockSpec((rows, depth, tn), lambda j: (0, 0, j))
    gblk = pl.BlockSpec((rows, tn), lambda j: (0, j))
    return _call(body, (w_t, m_t, v_t) + tuple(g_mine) + tuple(g_theirs), name="adamw_w_in_t", grid=(d // tn,),
                 in_specs=[slab] * 3 + [gblk] * (2 * depth), out_specs=[slab] * 4,
                 out_shape=[jax.ShapeDtypeStruct(w_t.shape, F32)] * 4, semantics=("parallel",),
                 vmem_bytes=2 * (7 * _nbytes((rows, max(depth, SUBLANES), tn), F32)
                                 + 2 * depth * _nbytes((rows, tn), F32)))


def _adamw_layer(w, m, v, l, g_mine, g_theirs, prev, after):
    _, rows, cols = w.shape
    br = _divisor_tile(rows, 8, ELEM_ROWS)

    def body(w_ref, m_ref, v_ref, ga_ref, gb_ref, *rest):
        g_ref, d_ref, nm_ref, nv_ref = rest[5:]
        g = ga_ref[...] + gb_ref[...]
        g_ref[...] = g
        d_ref[...], nm_ref[...], nv_ref[...] = _adamw_math(w_ref[...], g, m_ref[...], v_ref[...])

    slot = pl.BlockSpec((None, br, cols), lambda i: (l, i, 0))
    blk = pl.BlockSpec((br, cols), lambda i: (i, 0))
    return _call(body, (w, m, v, g_mine, g_theirs) + tuple(prev) + (after,), name="adamw_layer",
                 grid=(rows // br,), in_specs=[slot] * 3 + [blk] * 2 + [ANY] * 5, out_specs=[slot] * 4,
                 out_shape=[jax.ShapeDtypeStruct(w.shape, F32)] * 4,
                 input_output_aliases={5: 0, 6: 1, 7: 2, 8: 3}, semantics=("parallel",))


BIG = ("w_in", "w_out", "w_up", "w_down")
WEIGHTS = ("meta", "attn_norm_g", "w_in", "b_f", "conv_w", "conv_b", "w_gate_a", "b_gate_a", "w_gate_x",
           "b_gate_x", "lru_L", "attn_out_g", "rec_out_g", "w_out", "mlp_norm_g", "w_up", "w_down", "final_g")
SMALL = tuple(k for k in WEIGHTS if k not in BIG)
COL_SHARDED_SMALL = ("meta", "conv_w")


def _packed_rows(shape):
    return -(-math.prod(shape) // (SUBLANES * LANES)) * SUBLANES


def _pack(arrs):
    rows = []
    for a in arrs:
        flat = a.reshape(-1)
        rows.append(jnp.pad(flat, (0, _packed_rows(a.shape) * LANES - flat.shape[0])).reshape(-1, LANES))
    used = sum(r.shape[0] for r in rows)
    rows.append(jnp.zeros((-used % ELEM_ROWS, LANES), F32))
    return jnp.concatenate(rows, axis=0)


def _unpack(buf, shapes):
    out, r0 = [], 0
    for s in shapes:
        nr = _packed_rows(s)
        out.append(buf[r0:r0 + nr].reshape(-1)[:math.prod(s)].reshape(s))
        r0 += nr
    return out


def _halves(a):
    return a.reshape((2, a.shape[0] // 2) + a.shape[1:])


def _cols_from_chips(g):
    return jnp.moveaxis(g, 0, -2).reshape(g.shape[1:-1] + (N_CHIPS * g.shape[-1],))


def kernel(x, meta, attn_norm_g, w_in, b_f, conv_w, conv_b, w_gate_a, b_gate_a, w_gate_x, b_gate_x, lru_L, attn_out_g, rec_out_g, w_out, mlp_norm_g, w_up, w_down, final_g, loss_target, m_meta, m_attn_norm_g, m_w_in, m_b_f, m_conv_w, m_conv_b, m_w_gate_a, m_b_gate_a, m_w_gate_x, m_b_gate_x, m_lru_L, m_attn_out_g, m_rec_out_g, m_w_out, m_mlp_norm_g, m_w_up, m_w_down, m_final_g, v_meta, v_attn_norm_g, v_w_in, v_b_f, v_conv_w, v_conv_b, v_w_gate_a, v_b_gate_a, v_w_gate_x, v_b_gate_x, v_lru_L, v_attn_out_g, v_rec_out_g, v_w_out, v_mlp_norm_g, v_w_up, v_w_down, v_final_g):
    w = dict(meta=meta, attn_norm_g=attn_norm_g, w_in=w_in, b_f=b_f, conv_w=conv_w, conv_b=conv_b,
             w_gate_a=w_gate_a, b_gate_a=b_gate_a, w_gate_x=w_gate_x, b_gate_x=b_gate_x, lru_L=lru_L,
             attn_out_g=attn_out_g, rec_out_g=rec_out_g, w_out=w_out, mlp_norm_g=mlp_norm_g, w_up=w_up,
             w_down=w_down, final_g=final_g)
    m = dict(meta=m_meta, attn_norm_g=m_attn_norm_g, w_in=m_w_in, b_f=m_b_f, conv_w=m_conv_w, conv_b=m_conv_b,
             w_gate_a=m_w_gate_a, b_gate_a=m_b_gate_a, w_gate_x=m_w_gate_x, b_gate_x=m_b_gate_x, lru_L=m_lru_L,
             attn_out_g=m_attn_out_g, rec_out_g=m_rec_out_g, w_out=m_w_out, mlp_norm_g=m_mlp_norm_g,
             w_up=m_w_up, w_down=m_w_down, final_g=m_final_g)
    v = dict(meta=v_meta, attn_norm_g=v_attn_norm_g, w_in=v_w_in, b_f=v_b_f, conv_w=v_conv_w, conv_b=v_conv_b,
             w_gate_a=v_w_gate_a, b_gate_a=v_b_gate_a, w_gate_x=v_w_gate_x, b_gate_x=v_b_gate_x, lru_L=v_lru_L,
             attn_out_g=v_attn_out_g, rec_out_g=v_rec_out_g, w_out=v_w_out, mlp_norm_g=v_mlp_norm_g,
             w_up=v_w_up, w_down=v_w_down, final_g=v_final_g)
    s_len, d = x.shape[1], x.shape[2]
    depth = w_in.shape[0]
    chip = 2 * lax.axis_index("x") + lax.axis_index("y")

    g_conv, g_meta = [g.reshape((N_CHIPS, g.shape[1] * g.shape[2]) + g.shape[3:])
                      for g in _all_gather_chips([_halves(w["conv_w"]), _halves(w["meta"])])]
    p = dict(w)
    p["conv_w"] = _cols_from_chips(g_conv)
    meta_full = jnp.moveaxis(g_meta, 0, 1).reshape(N_META, d)

    chip1 = chip.reshape(1).astype(jnp.int32)
    w_in_t, m_in_t, v_in_t = [jnp.transpose(a["w_in"], (2, 0, 1)) for a in (w, m, v)]
    w_in_slabs = _cast_w_in_t_to_slabs(w_in_t, chip1)
    pushes, tokens = [], []
    for l in range(depth):
        slabs = [w_in_slabs[l]] + [_cast_to_slab(w[k], l, chip1) for k in BIG[1:]]
        send_sems, recv_sems, _, lands, token = _push_start(slabs, slabs, "gather_chips_half", f"weights_start_{l}")
        pushes.append((send_sems, recv_sems, lands))
        tokens.append(token[0, 0])
    passed = [{} for _ in range(depth)]

    def stage(l, after, ids):
        send_sems, recv_sems, lands = pushes[l]
        tag = "_".join(BIG[i] for i in ids)
        sub = [lands[i] for i in ids]
        sub = _push_wait(send_sems, recv_sems, ids, sub, sub, "gather_chips_half", after, f"{tag}_wait_{l}")
        send_sems, recv_sems, _, sub, token = _push_start(sub, sub, "pass_halves", f"{tag}_pass_{l}")
        for j, i in enumerate(ids):
            passed[l][i] = (send_sems, recv_sems, sub[j], j)
        return token[0, 0]

    t_len = N_META + s_len
    pad = -t_len % SEQ_TILE
    h = jnp.concatenate([meta_full, x[0], jnp.zeros((pad, d), F32)], axis=0)
    tgt = jnp.concatenate([jnp.zeros((N_META, d), F32), loss_target[0], jnp.zeros((pad, d), F32)], axis=0)
    z = _rms_fwd(h, _row(p["attn_norm_g"][0] + sum(tokens)))
    stage(0, z, [0])
    saved = []
    for l in range(depth):
        def fetch(k, after, l=l):
            send_sems, recv_sems, land, j = passed[l][BIG.index(k)]
            return _push_wait(send_sems, recv_sems, [j], [land], [land], "pass_halves", after,
                              f"{k}_here_{l}")[0]

        def stage_next(after, which, l=l):
            if which == "own":
                return stage(l, after, [1, 2, 3])
            return stage(l + 1, after, [0]) if l + 1 < depth else 0.0

        g_next = p["attn_norm_g"][l + 1] if l + 1 < depth else p["final_g"]
        h, z, sv = _forward_layer(l, h, z, p, fetch, stage_next, g_next)
        saved.append(sv)
    dh, dh_b, dg_final, loss_part = _loss_bwd(h, _row(p["final_g"]), tgt, s_len)

    small = {k: [None] * depth for k in SMALL if k not in ("meta", "final_g")}
    pushes = [None] * depth
    tok = 0.0
    for l in reversed(range(depth)):
        dh, dh_b, big_mlp, sm_mlp = _backward_mlp(l, dh, dh_b, saved[l], p, tok)
        sent = []

        def send_w_out(g_out):
            parts = [big_mlp["w_down"], big_mlp["w_up"], g_out.reshape((N_CHIPS, -1) + g_out.shape[1:])]
            sent.append(_push_start(parts, [lax.empty(a.shape, a.dtype) for a in parts], "scatter_chips",
                                    f"mlp_grads_start_{l}"))
            return sent[0][4][0, 0]

        dh, dh_b, big_mix, sm_mix = _backward_mixer(l, dh, dh_b, saved[l], p, send_w_out)
        push_mlp = sent[0]
        parts = [big_mix["w_in"]]
        push_mix = _push_start(parts, [lax.empty(a.shape, a.dtype) for a in parts], "scatter_chips",
                               f"mixer_grads_start_{l}")
        tok = push_mix[4][0, 0]
        pushes[l] = {("w_down", "w_up", "w_out"): push_mlp, ("w_in",): push_mix}
        for k, val in {**sm_mlp, **sm_mix}.items():
            small[k][l] = val
    grads = {k: jnp.stack(val) for k, val in small.items()}
    grads["final_g"] = dg_final[0]
    grads["meta"] = dh[:N_META]
    dx = dh[N_META:t_len]

    full_shapes = [grads[k].shape for k in SMALL] + [(1,)]
    packed = _pack([grads[k].astype(F32) for k in SMALL] + [loss_part[0, :1] + tok])
    dev1 = (2 * chip + lax.axis_index("c")).reshape(1).astype(jnp.int32)
    slabs = [_place_slab(packed, dev1, N_DEV)]
    small_push = _push_start(slabs, slabs, "gather_devices", "small_grads_start")

    last_token = small_push[4]
    outs = {k: [lax.empty(w[k].shape, F32) for _ in range(4)] for k in BIG[1:]}
    w_in_sums = [None] * depth
    swaps = {}

    def finish(l, wait_after, adam_after):
        send_sems, recv_sems, mine, lands, _ = swaps[l]
        mine, theirs = _push_wait(send_sems, recv_sems, list(range(len(BIG))), mine, lands, "sibling", wait_after,
                                  f"sums_wait_{l}")
        w_in_sums[l] = (mine[0], theirs[0])
        for k, a, b in zip(BIG[1:], mine[1:], theirs[1:]):
            outs[k] = _adamw_layer(w[k], m[k], v[k], l, a, b, outs[k], adam_after)

    wait_after = last_token
    for l in reversed(range(depth)):
        sums = {}
        for names, (send_sems, recv_sems, parts, lands, _) in pushes[l].items():
            parts, landed = _push_wait(send_sems, recv_sems, list(range(len(names))), parts, lands, "scatter_chips",
                                       wait_after, f"{names[0]}_grads_wait_{l}")
            for k, part, land in zip(names, parts, landed):
                sums[k] = wait_after = _sum_partials(part, land, chip1)
        mine = [sums[k] for k in BIG]
        swaps[l] = _push_start(mine, [lax.empty(a.shape, a.dtype) for a in mine], "sibling", f"sums_start_{l}")
        if l + 2 < depth:
            finish(l + 2, outs["w_down"][0] if l + 3 < depth else mine[0], swaps[l][4])
    for l in reversed(range(min(2, depth))):
        finish(l, outs["w_down"][0] if depth > 2 else swaps[0][4], swaps[0][4])
    outs["w_in"] = [jnp.transpose(r, (1, 2, 0)) for r in _adamw_w_in_t(
        w_in_t, m_in_t, v_in_t, [s[0] for s in w_in_sums], [s[1] for s in w_in_sums])]
    out_g, out_d, out_m, out_v = [{k: outs[k][i] for k in BIG} for i in range(4)]

    landed = _push_wait(small_push[0], small_push[1], [0], small_push[3], small_push[3], "gather_devices",
                        out_g["w_in"], "small_grads_wait")
    total = _sum_slabs(landed[0])
    small_g = dict(zip(SMALL + ("loss",), _unpack(total, full_shapes)))
    for k in COL_SHARDED_SMALL:
        n = w[k].shape[-1]
        small_g[k] = lax.dynamic_slice_in_dim(small_g[k], chip * n, n, axis=small_g[k].ndim - 1)
    local_shapes = [w[k].shape for k in SMALL]
    res = _adamw(_pack([w[k] for k in SMALL]), _pack([small_g[k] for k in SMALL]),
                 _pack([m[k] for k in SMALL]), _pack([v[k] for k in SMALL]))
    out_g.update({k: small_g[k] for k in SMALL})
    for dst, buf in zip((out_d, out_m, out_v), res):
        dst.update(zip(SMALL, _unpack(buf, local_shapes)))

    return (small_g["loss"].reshape(()), dx[None],
            *[out_g[k] for k in WEIGHTS], *[out_d[k] for k in WEIGHTS],
            *[out_m[k] for k in WEIGHTS], *[out_v[k] for k in WEIGHTS])
```

```python
import functools
import math

import jax
import jax.numpy as jnp
from jax import lax
from jax.experimental import pallas as pl
from jax.experimental.pallas import tpu as pltpu

F32 = jnp.float32
BF16 = jnp.bfloat16

N_META = 16
HEAD_DIM = 64
N_REC_BLOCKS = 8
CONV_WIDTH = 4
RG_C = 8.0
NORM_EPS = 1e-6
ADAM_LR = 0.001
ADAM_B1 = 0.9
ADAM_B2 = 0.999
ADAM_EPS = 1e-08
ADAM_WD = 0.01
ADAM_STEP = 10

LANES = 128
SUBLANES = 8
SEQ_TILE = 128
VMEM_CAP = 60 * 2**20
VMEM_SLACK = 6 * 2**20
NEG_BIG = -1e30
N_CHIPS = 4
N_DEV = 8
MESH = pl.DeviceIdType.MESH


def _nbytes(shape, dtype):
    return math.prod(shape) * jnp.dtype(dtype).itemsize


def _call(body, args, *, name, out_shape, grid=(), in_specs=None, out_specs=None, scratch_shapes=(),
          grid_spec=None, semantics=None, vmem_bytes=None, side_effects=None, hbm_results=True, **kw):
    cp = {}
    if semantics is not None:
        cp["dimension_semantics"] = semantics
    if vmem_bytes is not None:
        cp["vmem_limit_bytes"] = int(min(VMEM_CAP, vmem_bytes + VMEM_SLACK))
    if side_effects is not None:
        cp["has_side_effects"] = side_effects
    if grid_spec is not None:
        kw["grid_spec"] = grid_spec
    else:
        kw.update(grid=grid, in_specs=in_specs, out_specs=out_specs, scratch_shapes=scratch_shapes)
    if hbm_results:
        out_shape = jax.tree.map(
            lambda s: pltpu.HBM(s.shape, s.dtype) if isinstance(s, jax.ShapeDtypeStruct) else s, out_shape)
    fn = pl.pallas_call(
        body, name=name, out_shape=out_shape,
        compiler_params=pltpu.CompilerParams(**cp), **kw)
    return fn(*[_in_hbm(a) if jnp.issubdtype(getattr(a, "dtype", jnp.int32), jnp.floating) else a for a in args])


def _divisor_tile(n, unit, target):
    best = None
    for t in range(unit, min(n, target) + 1, unit):
        if n % t == 0:
            best = t
    return n if best is None else best


def _sigmoid(x):
    return 1.0 / (1.0 + jnp.exp(-x))


def _log1p_unit(e):
    series = e * (1.0 - e * (0.5 - e * (1.0 / 3.0)))
    return jnp.where(e < 1e-2, series, jnp.log(1.0 + e))


def _log_sigmoid(x):
    return jnp.minimum(x, 0.0) - _log1p_unit(jnp.exp(-jnp.abs(x)))


def _one_minus_exp(x, exp_x):
    small = -x * (1.0 + x * (1.0 / 2 + x * (1.0 / 6 + x * (1.0 / 24 + x * (1.0 / 120 + x * (1.0 / 720))))))
    return jnp.where(x > -0.25, small, 1.0 - exp_x)


_GELU_K = math.sqrt(2.0 / math.pi)
_GELU_C = 0.044715


def _gelu_and_grad(y):
    th = jnp.tanh(_GELU_K * (y + _GELU_C * y * y * y))
    g = 0.5 * y * (1.0 + th)
    dg = 0.5 * (1.0 + th) + 0.5 * y * (1.0 - th * th) * _GELU_K * (1.0 + 3.0 * _GELU_C * y * y)
    return g, dg


def _rstd(x):
    return lax.rsqrt(jnp.mean(x * x, axis=-1, keepdims=True) + NORM_EPS)


def _rms_bwd(dz, x, g):
    rs = _rstd(x)
    xh = x * rs
    dgp = jnp.sum(dz * xh, axis=0, keepdims=True)
    dxh = dz * g
    dx = rs * (dxh - xh * jnp.mean(dxh * xh, axis=-1, keepdims=True))
    return dx, dgp


def _dot(a, b):
    return jnp.dot(a, b, preferred_element_type=F32)


def _dot_nt(a, b):
    return lax.dot_general(a, b, (((1,), (1,)), ((), ())), preferred_element_type=F32)


def _dot_tn(a, b):
    return lax.dot_general(a, b, (((0,), (0,)), ((), ())), preferred_element_type=F32)


def _full(shape):
    nd = len(shape)
    return pl.BlockSpec(shape, lambda *_: (0,) * nd)


def _rms_fwd(h, g):
    tp, d = h.shape
    tm = _divisor_tile(tp, 16, 544)

    def body(h_ref, g_ref, z_ref):
        x = h_ref[...]
        z_ref[...] = (x * _rstd(x) * g_ref[...]).astype(BF16)

    return _call(body, (h, g), name="rms_fwd", grid=(tp // tm,),
                 in_specs=[pl.BlockSpec((tm, d), lambda i: (i, 0)), _full((1, d))],
                 out_specs=pl.BlockSpec((tm, d), lambda i: (i, 0)),
                 out_shape=jax.ShapeDtypeStruct((tp, d), BF16), semantics=("parallel",))


def _proj(z, w_big_t, att_w):
    tp, d = z.shape
    nb = w_big_t.shape[0]
    tn = _divisor_tile(nb, LANES, 512)
    assert (3 * att_w) % tn == 0
    n_qkv = 3 * att_w // tn
    scale = 1.0 / math.sqrt(HEAD_DIM)

    def body(z_ref, w_ref, qkv_ref, p_ref):
        j = pl.program_id(0)
        acc = _dot_nt(z_ref[...], w_ref[...])

        @pl.when(j < n_qkv)
        def _():
            col = j * tn + lax.broadcasted_iota(jnp.int32, (1, tn), 1)
            qkv_ref[...] = (acc * jnp.where(col < att_w, scale, 1.0)).astype(BF16)

        @pl.when(j >= n_qkv)
        def _():
            p_ref[...] = acc

    vm = 2 * (_nbytes((tp, d), BF16) + _nbytes((d, tn), BF16) + _nbytes((tp, tn), F32) * 2)
    return _call(body, (z, w_big_t), name="proj", grid=(nb // tn,),
                 in_specs=[_full((tp, d)), pl.BlockSpec((tn, d), lambda j: (j, 0))],
                 out_specs=[pl.BlockSpec((tp, tn), lambda j: (0, jnp.minimum(j, n_qkv - 1))),
                            pl.BlockSpec((tp, tn), lambda j: (0, jnp.maximum(j - n_qkv, 0)))],
                 out_shape=[jax.ShapeDtypeStruct((tp, 3 * att_w), BF16),
                            jax.ShapeDtypeStruct((tp, nb - 3 * att_w), F32)],
                 semantics=("arbitrary",), vmem_bytes=vm)


def _tile_cumsum(x, row, reverse=False):
    for s in (1, 2, 4):
        if reverse:
            x = x + jnp.where(row < SUBLANES - s, pltpu.roll(x, SUBLANES - s, 0), 0.0)
        else:
            x = x + jnp.where(row >= s, pltpu.roll(x, s, 0), 0.0)
    return x


def _fgate_fwd(proj, b_f_pad, nh):
    tp, nb = proj.shape
    fblk = nb // LANES - 1

    def body(f_ref, b_ref, c_ref, ct_ref):
        b = b_ref[...]
        row = lax.broadcasted_iota(jnp.int32, (SUBLANES, LANES), 0)

        def step(i, carry):
            r0 = pl.multiple_of(i * SUBLANES, SUBLANES)
            lf = _log_sigmoid(f_ref[pl.ds(r0, SUBLANES), :] + b)
            x = _tile_cumsum(lf, row) + carry
            c_ref[pl.ds(r0, SUBLANES), :] = x
            return x[SUBLANES - 1:SUBLANES, :]

        lax.fori_loop(0, tp // SUBLANES, step, jnp.zeros((1, LANES), F32))
        ct_ref[...] = c_ref[...].T[:nh, :]

    return _call(body, (proj, b_f_pad), name="fgate_fwd", grid=(1,),
                 in_specs=[pl.BlockSpec((tp, LANES), lambda i: (0, fblk)), _full((1, LANES))],
                 out_specs=[_full((tp, LANES)), _full((nh, tp))],
                 out_shape=[jax.ShapeDtypeStruct((tp, LANES), F32), jax.ShapeDtypeStruct((nh, tp), F32)],
                 semantics=("arbitrary",))


def _fgate_bwd(proj, b_f_pad, dc):
    tp, nb = proj.shape
    fblk = nb // LANES - 1

    def body(f_ref, b_ref, dc_ref, df_ref, db_ref, dc_s):
        b = b_ref[...]
        row = lax.broadcasted_iota(jnp.int32, (SUBLANES, LANES), 0)
        nt = tp // SUBLANES

        def step(i, carry):
            suffix, acc = carry
            r0 = pl.multiple_of((nt - 1 - i) * SUBLANES, SUBLANES)
            dlf = _tile_cumsum(dc_ref[pl.ds(r0, SUBLANES), :], row, reverse=True) + suffix
            df = dlf * _sigmoid(-(f_ref[pl.ds(r0, SUBLANES), :] + b))
            dc_s[pl.ds(r0, SUBLANES), :] = df
            return dlf[0:1, :], acc + df

        _, acc = lax.fori_loop(0, nt, step, (jnp.zeros((1, LANES), F32), jnp.zeros((SUBLANES, LANES), F32)))
        df_ref[...] = dc_s[...].astype(BF16)
        db_ref[...] = jnp.broadcast_to(jnp.sum(acc, axis=0, keepdims=True), (SUBLANES, LANES))

    return _call(body, (proj, b_f_pad, dc), name="fgate_bwd", grid=(1,),
                 in_specs=[pl.BlockSpec((tp, LANES), lambda i: (0, fblk)), _full((1, LANES)), _full((tp, LANES))],
                 out_specs=[_full((tp, LANES)), _full((SUBLANES, LANES))],
                 out_shape=[jax.ShapeDtypeStruct((tp, LANES), BF16),
                            jax.ShapeDtypeStruct((SUBLANES, LANES), F32)],
                 scratch_shapes=[pltpu.VMEM((tp, LANES), F32)], semantics=("arbitrary",))


ATT_BQ = 128


ATT_BUCKET = 2
ATT_HEADS = 4


def _for_bucket(i, nq, fn):
    for lo in range(0, nq, ATT_BUCKET):
        hi = min(lo + ATT_BUCKET, nq)
        spans = ([(0, lo * ATT_BQ, False)] if lo else []) + [(lo * ATT_BQ, hi * ATT_BQ, True)]
        pl.when(jnp.logical_and(i >= lo, i < hi))(functools.partial(fn, spans))


def _head_column(c_blk, h):
    lane = lax.broadcasted_iota(jnp.int32, c_blk.shape, 1)
    return jnp.sum(jnp.where(lane == h, c_blk, 0.0), axis=1, keepdims=True)


def _head_columns_into(p, c_ref, ck_s):
    for hh in range(ATT_HEADS):
        ck_s[hh] = jnp.broadcast_to(_head_column(c_ref[...], ATT_HEADS * p + hh), ck_s.shape[1:])


def _pair_diag_cols(x2):
    top = lax.broadcasted_iota(jnp.int32, (LANES, ATT_BQ), 0) < HEAD_DIM
    xt = x2.astype(F32).T.astype(BF16)
    return jnp.concatenate([jnp.where(top, xt, 0), jnp.where(top, 0, xt)], axis=1)


def _pair_diag_rows(x2):
    low = lax.broadcasted_iota(jnp.int32, (ATT_BQ, LANES), 1) < HEAD_DIM
    return jnp.concatenate([jnp.where(low, x2, 0), jnp.where(low, 0, x2)], axis=0)


def _seen_keys(k0, k1, q0):
    keys = k0 + lax.broadcasted_iota(jnp.int32, (k1 - k0, ATT_BQ), 0)
    return keys <= q0 + lax.broadcasted_iota(jnp.int32, (k1 - k0, ATT_BQ), 1)


def _attn_fwd(qkv, c, c_t, nh):
    tp = qkv.shape[0]
    att_w = nh * HEAD_DIM
    ng = nh // ATT_HEADS
    gw = ATT_HEADS * HEAD_DIM
    bq = ATT_BQ
    nq = tp // bq
    pair = 2 * HEAD_DIM
    assert pair == LANES and ATT_HEADS % 2 == 0

    def body(q_ref, k_ref, v_ref, c_ref, ct_ref, o_ref, lse_ref, ck_s, vt_s):
        p = pl.program_id(0)
        i = pl.program_id(1)

        @pl.when(i == 0)
        def _():
            _head_columns_into(p, c_ref, ck_s)
            vt_s[...] = v_ref[...].astype(F32).T.astype(BF16)

        def compute(spans):
            q0 = pl.multiple_of(i * bq, bq)
            o_t, lses = [], []
            for pi in range(ATT_HEADS // 2):
                lo = pair * pi
                heads = (2 * pi, 2 * pi + 1)
                q_cols = _pair_diag_cols(q_ref[:, lo:lo + pair])
                ts = []
                for k0, k1, needs_mask in spans:
                    t2 = _dot(k_ref[k0:k1, lo:lo + pair], q_cols)
                    t_e = [t2[:, e * bq:(e + 1) * bq] - ck_s[hh, k0:k1, :] for e, hh in enumerate(heads)]
                    if needs_mask:
                        seen = _seen_keys(k0, k1, q0)
                        t_e = [jnp.where(seen, t, NEG_BIG) for t in t_e]
                    ts.append(t_e)
                ms = [functools.reduce(jnp.maximum, [jnp.max(t[e], axis=0, keepdims=True) for t in ts])
                      for e in range(2)]
                es = [[jnp.exp(t[e] - ms[e]) for e in range(2)] for t in ts]
                ls = [sum(jnp.sum(e_[e], axis=0, keepdims=True) for e_ in es) for e in range(2)]
                o2 = sum(_dot(vt_s[lo:lo + pair, k0:k1],
                              jnp.concatenate([e_[0].astype(BF16), e_[1].astype(BF16)], axis=1))
                         for e_, (k0, k1, _) in zip(es, spans))
                o_t += [o2[:HEAD_DIM, :bq] / ls[0], o2[HEAD_DIM:, bq:] / ls[1]]
                lses += [ms[e] + ct_ref[pl.ds(ATT_HEADS * p + hh, 1), :] + jnp.log(ls[e])
                         for e, hh in enumerate(heads)]
            o_ref[...] = jnp.concatenate(o_t, axis=0).T
            lse_ref[...] = jnp.concatenate(lses, axis=0)

        _for_bucket(i, nq, compute)

    blk = pl.BlockSpec((bq, gw), lambda p, i: (i, p))
    vm = 6 * _nbytes((tp, gw), BF16) + 2 * ATT_HEADS * _nbytes((tp, LANES), F32) + 2 * _nbytes((tp, LANES), F32) \
        + 8 * ATT_HEADS * _nbytes((bq, tp), F32)
    return _call(body, (qkv, qkv, qkv, c, c_t), name="attn_fwd", grid=(ng, nq),
                 in_specs=[blk,
                           pl.BlockSpec((tp, gw), lambda p, i: (0, ng + p)),
                           pl.BlockSpec((tp, gw), lambda p, i: (0, 2 * ng + p)),
                           _full((tp, LANES)), pl.BlockSpec((nh, bq), lambda p, i: (0, i))],
                 out_specs=[blk, pl.BlockSpec((None, ATT_HEADS, bq), lambda p, i: (p, 0, i))],
                 out_shape=[jax.ShapeDtypeStruct((tp, att_w), F32), jax.ShapeDtypeStruct((ng, ATT_HEADS, tp), F32)],
                 scratch_shapes=[pltpu.VMEM((ATT_HEADS, tp, LANES), F32), pltpu.VMEM((gw, tp), BF16)],
                 semantics=("arbitrary", "arbitrary"), vmem_bytes=vm)


def _attn_bwd(qkv, c, c_t, lse, do, nh):
    tp = qkv.shape[0]
    att_w = nh * HEAD_DIM
    ng = nh // ATT_HEADS
    gw = ATT_HEADS * HEAD_DIM
    bq = ATT_BQ
    nq = tp // bq
    pair = 2 * HEAD_DIM
    assert pair == LANES and ATT_HEADS % 2 == 0
    scale = 1.0 / math.sqrt(HEAD_DIM)

    def body(q_ref, k_ref, v_ref, c_ref, ct_ref, lse_ref, do_ref, dq_ref, dk_ref, dv_ref, dc_ref,
             dk_s, dv_s, dc_s, ck_s, kt_s):
        p = pl.program_id(0)
        i = pl.program_id(1)

        @pl.when(i == 0)
        def _():
            dk_s[...] = jnp.zeros_like(dk_s)
            dv_s[...] = jnp.zeros_like(dv_s)
            dc_s[...] = jnp.zeros_like(dc_s)
            kt_s[...] = k_ref[...].astype(F32).T.astype(BF16)
            _head_columns_into(p, c_ref, ck_s)

        @pl.when(jnp.logical_and(i == 0, p == 0))
        def _():
            dc_ref[...] = jnp.zeros_like(dc_ref)

        def compute(spans):
            q0 = pl.multiple_of(i * bq, bq)
            dq_t = []
            for pi in range(ATT_HEADS // 2):
                lo = pair * pi
                q2 = q_ref[:, lo:lo + pair]
                do2 = do_ref[:, lo:lo + pair].astype(BF16)
                q_cols, do_cols = _pair_diag_cols(q2), _pair_diag_cols(do2)
                q_rows, do_rows = _pair_diag_rows(q2), _pair_diag_rows(do2)
                heads = (2 * pi, 2 * pi + 1)
                col_terms = [ct_ref[pl.ds(ATT_HEADS * p + hh, 1), :] - lse_ref[hh:hh + 1, :] for hh in heads]
                prs, dps = [], []
                for k0, k1, needs_mask in spans:
                    t2 = _dot(k_ref[k0:k1, lo:lo + pair], q_cols)
                    dp2 = _dot(v_ref[k0:k1, lo:lo + pair], do_cols)
                    if needs_mask:
                        seen = _seen_keys(k0, k1, q0)
                    pr_e, dp_e = [], []
                    for e, hh in enumerate(heads):
                        t = t2[:, e * bq:(e + 1) * bq] - ck_s[hh, k0:k1, :]
                        if needs_mask:
                            t = jnp.where(seen, t, NEG_BIG)
                        pr_e.append(jnp.exp(t + col_terms[e]))
                        dp_e.append(dp2[:, e * bq:(e + 1) * bq])
                    prs.append(pr_e)
                    dps.append(dp_e)
                key_sums = [sum(jnp.sum(pr[e] * dp[e], axis=0, keepdims=True) for pr, dp in zip(prs, dps))
                            for e in range(2)]
                dq2 = 0.0
                for (k0, k1, _), pr, dp in zip(spans, prs, dps):
                    ds = [pr[e] * (dp[e] - key_sums[e]) for e in range(2)]
                    for e, hh in enumerate(heads):
                        dc_s[hh, k0:k1, :] += jnp.sum(ds[e], axis=1, keepdims=True)
                    ds2 = jnp.concatenate([ds[0].astype(BF16), ds[1].astype(BF16)], axis=1)
                    pr2 = jnp.concatenate([pr[0].astype(BF16), pr[1].astype(BF16)], axis=1)
                    dk_s[k0:k1, lo:lo + pair] += _dot(ds2, q_rows)
                    dv_s[k0:k1, lo:lo + pair] += _dot(pr2, do_rows)
                    dq2 = dq2 + _dot(kt_s[lo:lo + pair, k0:k1], ds2)
                dq_t.append(jnp.concatenate([dq2[:HEAD_DIM, :bq], dq2[HEAD_DIM:, bq:]], axis=0))
            dq_ref[...] = (jnp.concatenate(dq_t, axis=0) * scale).T.astype(BF16)

        _for_bucket(i, nq, compute)

        @pl.when(i == nq - 1)
        def _():
            dk_ref[...] = dk_s[...].astype(BF16)
            dv_ref[...] = dv_s[...].astype(BF16)
            lane = lax.broadcasted_iota(jnp.int32, (tp, LANES), 1)
            dc = dc_ref[...]
            for hh in range(ATT_HEADS):
                dc = jnp.where(lane == ATT_HEADS * p + hh, -dc_s[hh], dc)
            dc_ref[...] = dc

    blk = pl.BlockSpec((bq, gw), lambda p, i: (i, p))
    col = pl.BlockSpec((tp, gw), lambda p, i: (0, p))
    vm = 7 * _nbytes((tp, gw), BF16) + 2 * _nbytes((tp, gw), F32) + 2 * ATT_HEADS * _nbytes((tp, LANES), F32) \
        + 2 * _nbytes((tp, LANES), F32) + 12 * ATT_HEADS * _nbytes((bq, tp), F32)
    return _call(body, (qkv, qkv, qkv, c, c_t, lse, do), name="attn_bwd", grid=(ng, nq),
                 in_specs=[blk,
                           pl.BlockSpec((tp, gw), lambda p, i: (0, ng + p)),
                           pl.BlockSpec((tp, gw), lambda p, i: (0, 2 * ng + p)),
                           _full((tp, LANES)), pl.BlockSpec((nh, bq), lambda p, i: (0, i)),
                           pl.BlockSpec((None, ATT_HEADS, bq), lambda p, i: (p, 0, i)), blk],
                 out_specs=[blk, col, col, _full((tp, LANES))],
                 out_shape=[jax.ShapeDtypeStruct((tp, att_w), BF16)] * 3 + [jax.ShapeDtypeStruct((tp, LANES), F32)],
                 scratch_shapes=[pltpu.VMEM((tp, gw), F32), pltpu.VMEM((tp, gw), F32),
                                 pltpu.VMEM((ATT_HEADS, tp, 1), F32), pltpu.VMEM((ATT_HEADS, tp, LANES), F32),
                                 pltpu.VMEM((gw, tp), BF16)],
                 semantics=("arbitrary", "arbitrary"), vmem_bytes=vm)


REC_ROWS = 128
HALO = SUBLANES


def _conv_taps(cat):
    taps = []
    for k in range(CONV_WIDTH):
        sh = CONV_WIDTH - 1 - k
        taps.append((pltpu.roll(cat, sh, 0) if sh else cat)[HALO:])
    return taps


def _rec_gates(xc, wa_ref, ba_ref, wx_ref, bx_ref, l_ref):
    xcb = xc.astype(BF16)
    r = _sigmoid(_dot(xcb, wa_ref[...]) + ba_ref[...])
    ig = _sigmoid(_dot(xcb, wx_ref[...]) + bx_ref[...])
    ls = _log_sigmoid(l_ref[...])
    log_a = RG_C * r * ls
    return xcb, r, ig, ls, log_a


def _rec_fwd(proj, xr_blk, yr_blk, rec_w, conv_w, conv_b, wa, ba, wx, bx, lru):
    tp = proj.shape[0]
    w = rec_w
    r_rows = REC_ROWS
    nc = tp // r_rows

    def body(xr_ref, yr_ref, cw_ref, cb_ref, wa_ref, ba_ref, wx_ref, bx_ref, l_ref,
             hr_ref, rec_ref, prev_s, carry_s, a_s, u_s):
        i = pl.program_id(0)

        @pl.when(i == 0)
        def _():
            prev_s[...] = jnp.zeros_like(prev_s)
            carry_s[...] = jnp.zeros_like(carry_s)

        x = xr_ref[...]
        taps = _conv_taps(jnp.concatenate([prev_s[...], x], axis=0))
        prev_s[...] = x[r_rows - HALO:]
        xc = cb_ref[...]
        for k in range(CONV_WIDTH):
            xc = xc + cw_ref[k:k + 1, :] * taps[k]
        _, r, ig, ls, log_a = _rec_gates(xc, wa_ref, ba_ref, wx_ref, bx_ref, l_ref)
        a = jnp.exp(log_a)
        a_s[...] = a
        u_s[...] = jnp.sqrt(_one_minus_exp(2.0 * log_a, a * a)) * ig * xc

        def tile(j, h):
            r0 = pl.multiple_of(j * SUBLANES, SUBLANES)
            at = a_s[pl.ds(r0, SUBLANES), :]
            ut = u_s[pl.ds(r0, SUBLANES), :]
            out = []
            for rr in range(SUBLANES):
                h = at[rr:rr + 1] * h + ut[rr:rr + 1]
                out.append(h)
            hr_ref[pl.ds(r0, SUBLANES), :] = jnp.concatenate(out, axis=0)
            return h

        carry_s[0:1, :] = lax.fori_loop(0, r_rows // SUBLANES, tile, carry_s[0:1, :])
        g, _ = _gelu_and_grad(yr_ref[...])
        rec_ref[...] = hr_ref[...] * g

    blk = pl.BlockSpec((r_rows, w), lambda i: (i, 0))
    vm = 16 * _nbytes((r_rows, w), F32) + 4 * _nbytes((w, w), BF16)
    return _call(body, (proj, proj, conv_w, conv_b, wa, ba, wx, bx, lru), name="rec_fwd", grid=(nc,),
                 in_specs=[pl.BlockSpec((r_rows, w), lambda i: (i, xr_blk)),
                           pl.BlockSpec((r_rows, w), lambda i: (i, yr_blk)),
                           _full((CONV_WIDTH, w)), _full((1, w)), _full((w, w)), _full((1, w)),
                           _full((w, w)), _full((1, w)), _full((1, w))],
                 out_specs=[blk, blk],
                 out_shape=[jax.ShapeDtypeStruct((tp, w), F32)] * 2,
                 scratch_shapes=[pltpu.VMEM((HALO, w), F32), pltpu.VMEM((SUBLANES, w), F32),
                                 pltpu.VMEM((r_rows, w), F32), pltpu.VMEM((r_rows, w), F32)],
                 semantics=("arbitrary",), vmem_bytes=vm)


def _rec_bwd(proj, xr_blk, yr_blk, rec_w, hr, drec, conv_w, conv_b, wa, ba, wx, bx, lru):
    tp = proj.shape[0]
    w = rec_w
    r_rows = REC_ROWS
    nc = tp // r_rows
    hpc = r_rows // HALO

    def body(xr_ref, xh_ref, yr_ref, hr_ref, hh_ref, drec_ref, cw_ref, cb_ref, wa_ref, ba_ref, wx_ref, bx_ref,
             l_ref, dxr_ref, dyr_ref, dwa_ref, dwx_ref, small_ref, lam_s, a_s, dhr_s, carry_s, next_s):
        i = pl.program_id(0)
        first = (nc - 1 - i) == 0

        @pl.when(i == 0)
        def _():
            carry_s[...] = jnp.zeros_like(carry_s)
            next_s[...] = jnp.zeros_like(next_s)
            dwa_ref[...] = jnp.zeros_like(dwa_ref)
            dwx_ref[...] = jnp.zeros_like(dwx_ref)
            small_ref[...] = jnp.zeros_like(small_ref)

        x = xr_ref[...]
        xprev = jnp.where(first, 0.0, xh_ref[...])
        taps = _conv_taps(jnp.concatenate([xprev, x], axis=0))
        xc = cb_ref[...]
        for k in range(CONV_WIDTH):
            xc = xc + cw_ref[k:k + 1, :] * taps[k]
        xcb, r, ig, ls, log_a = _rec_gates(xc, wa_ref, ba_ref, wx_ref, bx_ref, l_ref)
        a = jnp.exp(log_a)
        a2 = a * a
        mult = jnp.sqrt(_one_minus_exp(2.0 * log_a, a2))
        g, dg = _gelu_and_grad(yr_ref[...])
        hr_v = hr_ref[...]
        drec_v = drec_ref[...]
        dhr_s[...] = drec_v * g
        dyr_ref[...] = (drec_v * hr_v * dg).astype(BF16)
        a_s[...] = a

        def tile(jj, carry):
            r0 = pl.multiple_of((r_rows // SUBLANES - 1 - jj) * SUBLANES, SUBLANES)
            at = a_s[pl.ds(r0, SUBLANES), :]
            dt = dhr_s[pl.ds(r0, SUBLANES), :]
            out = [None] * SUBLANES
            for rr in range(SUBLANES - 1, -1, -1):
                lam = dt[rr:rr + 1] + carry
                out[rr] = lam
                carry = at[rr:rr + 1] * lam
            lam_s[pl.ds(r0, SUBLANES), :] = jnp.concatenate(out, axis=0)
            return carry

        carry_s[0:1, :] = lax.fori_loop(0, r_rows // SUBLANES, tile, carry_s[0:1, :])
        lam = lam_s[...]
        hprev = jnp.where(first, 0.0, hh_ref[...])
        hr_prev = pltpu.roll(jnp.concatenate([hprev, hr_v], axis=0), 1, 0)[HALO:]
        da = lam * hr_prev
        dxc = lam * mult * ig
        di = lam * mult * xc
        dmult = lam * ig * xc
        dlog_a = da * a - dmult * a2 / mult
        dr = dlog_a * (RG_C * ls)
        dls = jnp.sum(dlog_a * (RG_C * r), axis=0, keepdims=True)
        dga = dr * r * (1.0 - r)
        dgx = di * ig * (1.0 - ig)
        dgab = dga.astype(BF16)
        dgxb = dgx.astype(BF16)
        dxc = dxc + _dot_nt(dgab, wa_ref[...]) + _dot_nt(dgxb, wx_ref[...])
        dwa_ref[...] += _dot_tn(xcb, dgab)
        dwx_ref[...] += _dot_tn(xcb, dgxb)
        cat = jnp.concatenate([dxc, next_s[...]], axis=0)
        next_s[...] = dxc[0:HALO]
        dxr = cw_ref[CONV_WIDTH - 1:CONV_WIDTH, :] * dxc
        for k in range(CONV_WIDTH - 1):
            sh = CONV_WIDTH - 1 - k
            dxr = dxr + cw_ref[k:k + 1, :] * pltpu.roll(cat, r_rows + HALO - sh, 0)[:r_rows]
        dxr_ref[...] = dxr.astype(BF16)
        rows = [jnp.sum(dxc * taps[k], axis=0, keepdims=True) for k in range(CONV_WIDTH)]
        rows += [jnp.sum(dxc, axis=0, keepdims=True), jnp.sum(dga, axis=0, keepdims=True),
                 jnp.sum(dgx, axis=0, keepdims=True), dls * _sigmoid(-l_ref[...])]
        small_ref[...] += jnp.concatenate(rows, axis=0)

    def rev(i):
        return nc - 1 - i

    def halo(i):
        return jnp.maximum(rev(i) * hpc - 1, 0)

    blk = pl.BlockSpec((r_rows, w), lambda i: (rev(i), 0))
    vm = 40 * _nbytes((r_rows, w), F32) + 6 * _nbytes((w, w), F32)
    return _call(body, (proj, proj, proj, hr, hr, drec, conv_w, conv_b, wa, ba, wx, bx, lru),
                 name="rec_bwd", grid=(nc,),
                 in_specs=[pl.BlockSpec((r_rows, w), lambda i: (rev(i), xr_blk)),
                           pl.BlockSpec((HALO, w), lambda i: (halo(i), xr_blk)),
                           pl.BlockSpec((r_rows, w), lambda i: (rev(i), yr_blk)),
                           blk,
                           pl.BlockSpec((HALO, w), lambda i: (halo(i), 0)),
                           blk,
                           _full((CONV_WIDTH, w)), _full((1, w)), _full((w, w)), _full((1, w)),
                           _full((w, w)), _full((1, w)), _full((1, w))],
                 out_specs=[blk, blk, _full((w, w)), _full((w, w)), _full((SUBLANES, w))],
                 out_shape=[jax.ShapeDtypeStruct((tp, w), BF16)] * 2
                 + [jax.ShapeDtypeStruct((w, w), F32)] * 2 + [jax.ShapeDtypeStruct((SUBLANES, w), F32)],
                 scratch_shapes=[pltpu.VMEM((r_rows, w), F32)] * 3
                 + [pltpu.VMEM((SUBLANES, w), F32), pltpu.VMEM((HALO, w), F32)],
                 semantics=("arbitrary",), vmem_bytes=vm)


ROW_TARGET = 544


def _mixer_out(attn, rec, g_a, g_r, w_out, h, g_next):
    tp, d = h.shape
    aw, rw = attn.shape[1], rec.shape[1]
    kc = d // N_CHIPS
    tm = _divisor_tile(tp, 16, ROW_TARGET)

    def body(a_ref, r_ref, ga_ref, gr_ref, w_ref, h_ref, gn_ref, h1_ref, z_ref, mix_ref):
        a = a_ref[...]
        r = r_ref[...]
        mix = jnp.concatenate([a * _rstd(a) * ga_ref[...], r * _rstd(r) * gr_ref[...]], axis=1).astype(BF16)
        mix_ref[...] = mix
        h1 = h_ref[...]
        for j in range(N_CHIPS):
            h1 = h1 + _dot(mix[:, j * kc:(j + 1) * kc], w_ref[j])
        h1_ref[...] = h1
        z_ref[...] = (h1 * _rstd(h1) * gn_ref[...]).astype(BF16)

    row = lambda wd: pl.BlockSpec((tm, wd), lambda i: (i, 0))
    vm = 2 * _nbytes((d, d), BF16) + 12 * _nbytes((tm, d), F32)
    return _call(body, (attn, rec, g_a, g_r, w_out, h, g_next), name="mixer_out", grid=(tp // tm,),
                 in_specs=[row(aw), row(rw), _full((1, aw)), _full((1, rw)), _full(w_out.shape), row(d),
                           _full((1, d))],
                 out_specs=[row(d), row(d), row(d)],
                 out_shape=[jax.ShapeDtypeStruct((tp, d), F32), jax.ShapeDtypeStruct((tp, d), BF16),
                            jax.ShapeDtypeStruct((tp, d), BF16)],
                 semantics=("parallel",), vmem_bytes=vm)


def _mixer_bwd(dh_b, w_out, attn, rec, g_a, g_r):
    tp, d = dh_b.shape
    aw, rw = attn.shape[1], rec.shape[1]
    tm = _divisor_tile(tp, 16, ROW_TARGET)

    def body(dh_ref, w_ref, a_ref, r_ref, ga_ref, gr_ref, da_ref, dr_ref, dg_ref):
        @pl.when(pl.program_id(0) == 0)
        def _():
            dg_ref[...] = jnp.zeros_like(dg_ref)

        dh = dh_ref[...]
        dmix = jnp.concatenate([_dot_nt(dh, w_ref[j]) for j in range(N_CHIPS)], axis=1)
        da, dga = _rms_bwd(dmix[:, :aw], a_ref[...], ga_ref[...])
        dr, dgr = _rms_bwd(dmix[:, aw:], r_ref[...], gr_ref[...])
        da_ref[...] = da
        dr_ref[...] = dr
        dg_ref[...] += jnp.broadcast_to(jnp.concatenate([dga, dgr], axis=1), (SUBLANES, d))

    row = lambda wd: pl.BlockSpec((tm, wd), lambda i: (i, 0))
    vm = 2 * _nbytes((d, d), BF16) + 12 * _nbytes((tm, d), F32)
    return _call(body, (dh_b, w_out, attn, rec, g_a, g_r), name="mixer_bwd", grid=(tp // tm,),
                 in_specs=[row(d), _full(w_out.shape), row(aw), row(rw), _full((1, aw)), _full((1, rw))],
                 out_specs=[row(aw), row(rw), _full((SUBLANES, d))],
                 out_shape=[jax.ShapeDtypeStruct((tp, aw), F32), jax.ShapeDtypeStruct((tp, rw), F32),
                            jax.ShapeDtypeStruct((SUBLANES, d), F32)],
                 semantics=("arbitrary",), vmem_bytes=vm)


def _mlp_fwd(z, w_up, w_down, h, g_next):
    tp, d = h.shape
    fc = w_up.shape[2]
    ff = N_CHIPS * fc
    tm = _divisor_tile(tp, 16, ROW_TARGET)

    def body(z_ref, wu_ref, wd_ref, h_ref, gn_ref, up_ref, h2_ref, zn_ref):
        h2 = h_ref[...]
        for j in range(N_CHIPS):
            up = _dot(z_ref[...], wu_ref[j])
            r = jnp.maximum(up, 0.0)
            up_ref[:, j * fc:(j + 1) * fc] = up.astype(BF16)
            h2 = h2 + _dot((r * r).astype(BF16), wd_ref[j])
        h2_ref[...] = h2
        zn_ref[...] = (h2 * _rstd(h2) * gn_ref[...]).astype(BF16)

    row = lambda wd: pl.BlockSpec((tm, wd), lambda i: (i, 0))
    resident = lambda shape: pl.BlockSpec(shape, lambda i: (0,) * len(shape), pipeline_mode=pl.Buffered(1))
    vm = (2 * _nbytes((ff, d), BF16) + 2 * _nbytes((tm, ff), BF16) + 12 * _nbytes((tm, d), F32)
          + 4 * _nbytes((tm, fc), F32))
    return _call(body, (z, w_up, w_down, h, g_next), name="mlp_fwd", grid=(tp // tm,),
                 in_specs=[row(d), resident(w_up.shape), resident(w_down.shape), row(d), _full((1, d))],
                 out_specs=[row(ff), row(d), row(d)],
                 out_shape=[jax.ShapeDtypeStruct((tp, ff), BF16), jax.ShapeDtypeStruct((tp, d), F32),
                            jax.ShapeDtypeStruct((tp, d), BF16)],
                 semantics=("parallel",), vmem_bytes=vm)


def _loss_bwd(h, g, target, n_real):
    tp, d = h.shape
    tm = _divisor_tile(tp, 16, ROW_TARGET)

    def body(h_ref, g_ref, t_ref, dh_ref, dhb_ref, dg_ref, loss_ref):
        i = pl.program_id(0)

        @pl.when(i == 0)
        def _():
            dg_ref[...] = jnp.zeros_like(dg_ref)
            loss_ref[...] = jnp.zeros_like(loss_ref)

        x = h_ref[...]
        gv = g_ref[...]
        rowi = i * tm + lax.broadcasted_iota(jnp.int32, (tm, 1), 0)
        real = jnp.logical_and(rowi >= N_META, rowi < N_META + n_real)
        err = jnp.where(real, x * _rstd(x) * gv - t_ref[...], 0.0)
        loss_ref[...] += 0.5 * jnp.sum(jnp.mean(err * err, axis=-1, keepdims=True))
        dx, dgp = _rms_bwd(err * (1.0 / d), x, gv)
        dh_ref[...] = dx
        dhb_ref[...] = dx.astype(BF16)
        dg_ref[...] += jnp.broadcast_to(dgp, (SUBLANES, d))

    row = pl.BlockSpec((tm, d), lambda i: (i, 0))
    return _call(body, (h, g, target), name="loss_bwd", grid=(tp // tm,),
                 in_specs=[row, _full((1, d)), row],
                 out_specs=[row, row, _full((SUBLANES, d)), _full((SUBLANES, LANES))],
                 out_shape=[jax.ShapeDtypeStruct((tp, d), F32), jax.ShapeDtypeStruct((tp, d), BF16),
                            jax.ShapeDtypeStruct((SUBLANES, d), F32), jax.ShapeDtypeStruct((SUBLANES, LANES), F32)],
                 semantics=("arbitrary",), vmem_bytes=16 * _nbytes((tm, d), F32))


def _mlp_bwd(dh_b, w_down, w_up, up, z2):
    tp, d = dh_b.shape
    fc = w_down.shape[1]
    ff = N_CHIPS * fc
    tn = _divisor_tile(fc, LANES, 256)
    per = fc // tn

    def body(dh_ref, z_ref, wd_ref, wu_ref, up_ref, dz_ref, gd_ref, gu_ref):
        @pl.when(pl.program_id(0) == 0)
        def _():
            dz_ref[...] = jnp.zeros_like(dz_ref)

        dh = dh_ref[...]
        r = jnp.maximum(up_ref[...].astype(F32), 0.0)
        dup = (_dot_nt(dh, wd_ref[...]) * (2.0 * r)).astype(BF16)
        gd_ref[...] = _dot_tn((r * r).astype(BF16), dh).astype(BF16)
        gu_ref[...] = _dot_tn(z_ref[...], dup).astype(BF16)
        dz_ref[...] += _dot_nt(dup, wu_ref[...])

    col = pl.BlockSpec((tp, tn), lambda j: (0, j))
    resident = pl.BlockSpec((tp, d), lambda j: (0, 0), pipeline_mode=pl.Buffered(1))
    w_up_cols = pl.BlockSpec((None, d, tn), lambda j: (j // per, 0, j % per))
    vm = 2 * _nbytes((tp, d), BF16) + 3 * _nbytes((tp, d), F32) + 4 * _nbytes((tn, d), BF16) \
        + 4 * _nbytes((d, tn), BF16) + 10 * _nbytes((tp, tn), F32) + 4 * _nbytes((tn, d), F32)
    return _call(body, (dh_b, z2, w_down, w_up, up), name="mlp_bwd", grid=(ff // tn,),
                 in_specs=[resident, resident, pl.BlockSpec((None, tn, d), lambda j: (j // per, j % per, 0)),
                           w_up_cols, col],
                 out_specs=[_full((tp, d)), pl.BlockSpec((tn, d), lambda j: (j, 0)), w_up_cols],
                 out_shape=[jax.ShapeDtypeStruct((tp, d), F32), jax.ShapeDtypeStruct((ff, d), BF16),
                            jax.ShapeDtypeStruct((N_CHIPS, d, fc), BF16)],
                 semantics=("arbitrary",), vmem_bytes=vm)


def _norm_bwd(dz, h, g, dres):
    tp, d = h.shape
    tm = _divisor_tile(tp, 16, ROW_TARGET)

    def body(dz_ref, h_ref, g_ref, dres_ref, dh_ref, dhb_ref, dg_ref):
        @pl.when(pl.program_id(0) == 0)
        def _():
            dg_ref[...] = jnp.zeros_like(dg_ref)

        dx, dgp = _rms_bwd(dz_ref[...], h_ref[...], g_ref[...])
        dh = dres_ref[...] + dx
        dh_ref[...] = dh
        dhb_ref[...] = dh.astype(BF16)
        dg_ref[...] += jnp.broadcast_to(dgp, (SUBLANES, d))

    row = pl.BlockSpec((tm, d), lambda i: (i, 0))
    return _call(body, (dz, h, g, dres), name="norm_bwd", grid=(tp // tm,),
                 in_specs=[row, row, _full((1, d)), row],
                 out_specs=[row, row, _full((SUBLANES, d))],
                 out_shape=[jax.ShapeDtypeStruct((tp, d), F32), jax.ShapeDtypeStruct((tp, d), BF16),
                            jax.ShapeDtypeStruct((SUBLANES, d), F32)],
                 semantics=("arbitrary",), vmem_bytes=16 * _nbytes((tm, d), F32))


def _grad_w_pieces(pieces, b):
    tp, n = b.shape
    tn = _divisor_tile(n, LANES, 512)
    widths = [pc.shape[1] for pc in pieces]

    def body(*refs):
        p_refs, b_ref, o_refs = refs[:len(pieces)], refs[len(pieces)], refs[len(pieces) + 1:]
        for p_ref, o_ref in zip(p_refs, o_refs):
            o_ref[...] = _dot_tn(p_ref[...], b_ref[...]).astype(BF16)

    vm = 2 * sum(_nbytes((tp, wd), BF16) for wd in widths) + 2 * _nbytes((tp, tn), BF16) \
        + 4 * sum(_nbytes((wd, tn), F32) for wd in widths) + 2 * _nbytes((tp, max(widths)), F32)
    return _call(body, tuple(pieces) + (b,), name="grad_w_pieces", grid=(n // tn,),
                 in_specs=[_full(pc.shape) for pc in pieces] + [pl.BlockSpec((tp, tn), lambda j: (0, j))],
                 out_specs=[pl.BlockSpec((wd, tn), lambda j: (0, j)) for wd in widths],
                 out_shape=[jax.ShapeDtypeStruct((wd, n), BF16) for wd in widths],
                 semantics=("parallel",), vmem_bytes=vm)


def _dx_norm_bwd(pieces, w, w_spec, w_piece, h, g, dres, dot=_dot_nt):
    tp, d = h.shape
    tm = _divisor_tile(tp, 16, ROW_TARGET)
    n = len(pieces)

    def body(*refs):
        dy_refs = refs[:n]
        w_ref, h_ref, g_ref, dres_ref, dh_ref, dhb_ref, dg_ref = refs[n:]

        @pl.when(pl.program_id(0) == 0)
        def _():
            dg_ref[...] = jnp.zeros_like(dg_ref)

        dz = dot(dy_refs[0][...], w_piece(w_ref, 0))
        for i in range(1, n):
            dz = dz + dot(dy_refs[i][...], w_piece(w_ref, i))
        dx, dgp = _rms_bwd(dz, h_ref[...], g_ref[...])
        dh = dres_ref[...] + dx
        dh_ref[...] = dh
        dhb_ref[...] = dh.astype(BF16)
        dg_ref[...] += jnp.broadcast_to(dgp, (SUBLANES, d))

    row = lambda wd: pl.BlockSpec((tm, wd), lambda i: (i, 0))
    kk = sum(wd for _, _, wd in pieces)
    vm = 2 * _nbytes((d, kk), BF16) + 2 * _nbytes((tm, kk), BF16) + 14 * _nbytes((tm, d), F32)
    piece_specs = [pl.BlockSpec((tm, wd), functools.partial(lambda i, cb: (i, cb), cb=cb)) for _, cb, wd in pieces]
    return _call(body, tuple(a for a, _, _ in pieces) + (w, h, g, dres), name="dx_norm_bwd", grid=(tp // tm,),
                 in_specs=piece_specs + [w_spec, row(d), _full((1, d)), row(d)],
                 out_specs=[row(d), row(d), _full((SUBLANES, d))],
                 out_shape=[jax.ShapeDtypeStruct((tp, d), F32), jax.ShapeDtypeStruct((tp, d), BF16),
                            jax.ShapeDtypeStruct((SUBLANES, d), F32)],
                 semantics=("arbitrary",), vmem_bytes=vm)


def _block_diag(wg):
    nb, b, _ = wg.shape
    eye = jnp.eye(nb, dtype=wg.dtype)
    return (eye[:, None, :, None] * wg[:, :, None, :]).reshape(nb * b, nb * b)


def _diag_blocks(dense, nb):
    b = dense.shape[0] // nb
    d4 = dense.reshape(nb, b, nb, b)
    return jnp.stack([d4[i, :, i, :] for i in range(nb)])


def _row(v):
    return v.reshape(1, -1)


def _forward_layer(l, h, z, p, fetch, stage_next, g_next):
    d = h.shape[1]
    att_w = d // 2
    rec_w = d - att_w
    nh = att_w // HEAD_DIM
    wa_d = _block_diag(p["w_gate_a"][l]).astype(BF16)
    wx_d = _block_diag(p["w_gate_x"][l]).astype(BF16)
    b_f_pad = jnp.zeros((1, LANES), F32).at[0, :nh].set(p["b_f"][l])
    w_in_t = fetch("w_in", z)
    big = dict(w_in_big=_pack_w_in_t(w_in_t.reshape(-1, d), att_w, nh))
    qkv, proj = _proj(z, big["w_in_big"], att_w)
    c, c_t = _fgate_fwd(proj, b_f_pad, nh)
    attn, lse_b = _attn_fwd(qkv, c, c_t, nh)
    hr, rec = _rec_fwd(proj, 0, 1, rec_w, p["conv_w"][l], _row(p["conv_b"][l]), wa_d,
                       _row(p["b_gate_a"][l]), wx_d, _row(p["b_gate_x"][l]), _row(p["lru_L"][l]))
    tok = stage_next(attn, "own")
    big["w_out"] = fetch("w_out", rec)
    h1, z2, mix = _mixer_out(attn, rec, _row(p["attn_out_g"][l] + tok), _row(p["rec_out_g"][l]),
                             big["w_out"], h, _row(p["mlp_norm_g"][l]))
    tok = stage_next(h1, "next")
    big["w_up"] = fetch("w_up", h1)
    big["w_down"] = fetch("w_down", h1)
    up, h2, z_next = _mlp_fwd(z2, big["w_up"], big["w_down"], h1, _row(g_next + tok))
    saved = dict(h0=h, z1=z, proj=proj, qkv=qkv, c=c, c_t=c_t, attn=attn, lse_b=lse_b, hr=hr, rec=rec, h1=h1,
                 z2=z2, mix=mix, up=up, wa_d=wa_d, wx_d=wx_d, b_f_pad=b_f_pad, big=big)
    return h2, z_next, saved


def _backward_mlp(l, dh, dh_b, sv, p, tok):
    w_up, w_down = sv["big"]["w_up"], sv["big"]["w_down"]
    dz2, g_down, g_up = _mlp_bwd(dh_b, w_down, w_up, sv["up"], sv["z2"])
    dh, dh_b, dg2 = _norm_bwd(dz2, sv["h1"], _row(p["mlp_norm_g"][l] + tok), dh)
    big = dict(w_down=g_down.reshape((N_CHIPS, -1) + g_down.shape[1:]), w_up=g_up)
    return dh, dh_b, big, dict(mlp_norm_g=dg2[0])


def _backward_mixer(l, dh, dh_b, sv, p, tok):
    d = dh.shape[1]
    att_w = d // 2
    rec_w = d - att_w
    nh = att_w // HEAD_DIM
    small = {}
    dattn, drec, dg_mix = _mixer_bwd(dh_b, sv["big"]["w_out"], sv["attn"], sv["rec"],
                                     _row(p["attn_out_g"][l] + tok), _row(p["rec_out_g"][l]))
    small["attn_out_g"] = dg_mix[0, :att_w]
    small["rec_out_g"] = dg_mix[0, att_w:]
    dxr, dyr, dwa, dwx, sm = _rec_bwd(
        sv["proj"], 0, 1, rec_w, sv["hr"], drec, p["conv_w"][l], _row(p["conv_b"][l]), sv["wa_d"],
        _row(p["b_gate_a"][l]), sv["wx_d"], _row(p["b_gate_x"][l]), _row(p["lru_L"][l]))
    small.update(conv_w=sm[:CONV_WIDTH], conv_b=sm[4], b_gate_a=sm[5], b_gate_x=sm[6], lru_L=sm[7],
                 w_gate_a=_diag_blocks(dwa, N_REC_BLOCKS), w_gate_x=_diag_blocks(dwx, N_REC_BLOCKS))
    dq, dk, dv, dc = _attn_bwd(sv["qkv"], sv["c"], sv["c_t"], sv["lse_b"], dattn, nh)
    df, db_f = _fgate_bwd(sv["proj"], sv["b_f_pad"], dc)
    small["b_f"] = db_f[0, :nh]
    pieces = [dq, dk, dv, dxr, dyr, df]
    offs = [0, att_w, 2 * att_w, 3 * att_w, 3 * att_w + rec_w, 3 * att_w + 2 * rec_w]
    gq, gk, gv, gxr, gyr, gf = _grad_w_pieces(pieces, sv["z1"])
    g_in_t = jnp.concatenate([gq, gk, gv, gf[:nh], gxr, gyr], axis=0)
    w_big = sv["big"]["w_in_big"]
    widths = [pc.shape[1] for pc in pieces]
    dh, dh_b, dg1 = _dx_norm_bwd(
        [(pc, 0, wd) for pc, wd in zip(pieces, widths)], w_big, _full(w_big.shape),
        lambda w_ref, i: w_ref[offs[i]:offs[i] + widths[i], :], sv["h0"], _row(p["attn_norm_g"][l]), dh, dot=_dot)
    small["attn_norm_g"] = dg1[0]
    big = dict(w_in=g_in_t.reshape(N_CHIPS, -1, d))
    return dh, dh_b, big, small


def _pack_w_in_t(w_in_t, att_w, nh):
    qkv = w_in_t[:3 * att_w]
    f = w_in_t[3 * att_w:3 * att_w + nh]
    xy = w_in_t[3 * att_w + nh:]
    return jnp.concatenate([qkv, xy, f, jnp.zeros((LANES - nh, w_in_t.shape[1]), w_in_t.dtype)], axis=0)


ANY = pl.BlockSpec(memory_space=pl.ANY)


def _coords():
    return lax.axis_index("x"), lax.axis_index("y"), lax.axis_index("c")


def _other_chips(x, y):
    return [(1 - x, y), (x, 1 - y), (1 - x, 1 - y)]


def _remote(src, dst, send_sems, recv_sems, k, to):
    return pltpu.make_async_remote_copy(src_ref=src, dst_ref=dst, send_sem=send_sems.at[k],
                                        recv_sem=recv_sems.at[k], device_id=to, device_id_type=MESH)


def _all_gather_chips(shards):
    n = len(shards)
    per = 6

    def body(*refs):
        ins, outs = refs[:n], refs[n:2 * n]
        send_sems, recv_sems, local_sems = refs[2 * n:]
        x, y, c = _coords()
        me = 2 * x + y
        sibling = (x, y, 1 - c)
        chips = _other_chips(x, y)
        local = [pltpu.make_async_copy(ins[t], outs[t].at[me], local_sems.at[t]) for t in range(n)]
        for cp in local:
            cp.start()
        sends = []
        for t in range(n):
            for j, (px, py) in enumerate(chips):
                cp = _remote(ins[t].at[c], outs[t].at[me, c], send_sems, recv_sems, per * t + j, (px, py, c))
                cp.start()
                sends.append(cp)
        for t in range(n):
            for j, (px, py) in enumerate(chips):
                landed = outs[t].at[2 * px + py, c]
                _remote(landed, landed, send_sems, recv_sems, per * t + j, (px, py, c)).wait_recv()
                cp = _remote(landed, landed, send_sems, recv_sems, per * t + 3 + j, sibling)
                cp.start()
                sends.append(cp)
        for t in range(n):
            for j, (px, py) in enumerate(chips):
                passed = outs[t].at[2 * px + py, 1 - c]
                _remote(passed, passed, send_sems, recv_sems, per * t + 3 + j, sibling).wait_recv()
        for cp in sends:
            cp.wait_send()
        for cp in local:
            cp.wait()

    return _call(body, tuple(shards), name="all_gather_chips",
                 in_specs=[ANY] * n, out_specs=[ANY] * n,
                 out_shape=[jax.ShapeDtypeStruct((N_CHIPS,) + s.shape, s.dtype) for s in shards],
                 scratch_shapes=[pltpu.SemaphoreType.DMA((per * n,)), pltpu.SemaphoreType.DMA((per * n,)),
                                 pltpu.SemaphoreType.DMA((n,))])


HBM = pl.BlockSpec(memory_space=pltpu.HBM)
SEM = pl.BlockSpec(memory_space=pltpu.SEMAPHORE)
DATAFLOW = pltpu.SideEffectType.DATAFLOW_SIDE_EFFECTING


def _in_hbm(a):
    return pltpu.with_memory_space_constraint(a, pltpu.HBM)


PUSH_ARRIVALS = {"gather_chips_half": N_CHIPS - 1, "pass_halves": N_CHIPS - 1, "scatter_chips": N_CHIPS - 1,
                 "sibling": 1, "gather_devices": N_DEV - 1}


def _column_half(ref3, slab, c):
    hw = ref3.shape[2] // 2
    return ref3.at[slab, :, pl.ds(pl.multiple_of(c * hw, LANES), hw)]


def _push_copies(mode, src, land, send_sems, recv_sems, t):
    x, y, c = _coords()
    chip = 2 * x + y
    if mode == "gather_chips_half":
        return [_remote(_column_half(src, chip, c), _column_half(land, chip, c), send_sems, recv_sems, t, (px, py, c))
                for px, py in _other_chips(x, y)]
    if mode == "pass_halves":
        return [_remote(_column_half(src, 2 * px + py, c), _column_half(land, 2 * px + py, c), send_sems, recv_sems, t,
                        (x, y, 1 - c)) for px, py in _other_chips(x, y)]
    if mode == "scatter_chips":
        return [_remote(src.at[2 * px + py], land.at[chip], send_sems, recv_sems, t, (px, py, c))
                for px, py in _other_chips(x, y)]
    if mode == "sibling":
        return [_remote(src, land, send_sems, recv_sems, t, (x, y, 1 - c))]
    dev = 4 * x + 2 * y + c
    return [_remote(src.at[dev], land.at[dev], send_sems, recv_sems, t, (x ^ (k >> 2), y ^ ((k >> 1) & 1), c ^ (k & 1)))
            for k in range(1, N_DEV)]


def _push_start(srcs, lands, mode, name):
    n = len(srcs)
    same = all(s is ld for s, ld in zip(srcs, lands))
    n_in = n if same else 2 * n

    def body(*refs):
        src_refs = refs[:n]
        land_refs = src_refs if same else refs[n:2 * n]
        send_sems, recv_sems = refs[n_in], refs[n_in + 1]
        token = refs[-1]
        for t in range(n):
            for cp in _push_copies(mode, src_refs[t], land_refs[t], send_sems, recv_sems, t):
                cp.start()
        token[...] = jnp.zeros_like(token)

    operands = tuple(srcs) if same else tuple(srcs) + tuple(lands)
    res = _call(
        body, [_in_hbm(a) for a in operands], name=name,
        out_shape=(pltpu.SemaphoreType.DMA((n,)), pltpu.SemaphoreType.DMA((n,)))
        + tuple(pltpu.HBM(a.shape, a.dtype) for a in operands) + (jax.ShapeDtypeStruct((SUBLANES, LANES), F32),),
        in_specs=[HBM] * n_in, out_specs=(SEM, SEM) + (HBM,) * n_in + (pl.BlockSpec(memory_space=pltpu.VMEM),),
        input_output_aliases={i: 2 + i for i in range(n_in)}, side_effects=DATAFLOW, hbm_results=False)
    send_sems, recv_sems, token = res[0], res[1], res[-1]
    srcs_thru = res[2:2 + n]
    lands_thru = srcs_thru if same else res[2 + n:2 + 2 * n]
    return send_sems, recv_sems, srcs_thru, lands_thru, token


def _push_wait(send_sems, recv_sems, ids, srcs, lands, mode, after, name):
    n = len(lands)
    same = all(s is ld for s, ld in zip(srcs, lands))
    n_in = n if same else 2 * n

    def body(*refs):
        land_refs = refs[:n] if same else refs[n:2 * n]
        send_sems, recv_sems = refs[n_in], refs[n_in + 1]
        x, y, c = _coords()
        for t in range(n):
            if mode == "sibling":
                moved = land_refs[t]
            elif mode in ("gather_chips_half", "pass_halves"):
                moved = land_refs[t].at[pl.ds(0, PUSH_ARRIVALS[mode]), :, pl.ds(0, land_refs[t].shape[2] // 2)]
            else:
                moved = land_refs[t].at[pl.ds(0, PUSH_ARRIVALS[mode])]
            arrivals = _remote(moved, moved, send_sems, recv_sems, ids[t], (x, y, c))
            arrivals.wait_send()
            arrivals.wait_recv()

    operands = tuple(lands) if same else tuple(srcs) + tuple(lands)
    res = _call(
        body, operands + (send_sems, recv_sems, after), name=name,
        out_shape=tuple(pltpu.HBM(a.shape, a.dtype) for a in operands),
        in_specs=[HBM] * n_in + [SEM, SEM, ANY], out_specs=(HBM,) * n_in,
        input_output_aliases={i: i for i in range(n_in)}, side_effects=DATAFLOW)
    return list(res) if same else (list(res[:n]), list(res[n:]))


def _sum_partials(part, landed, chip):
    _, rows, cols = part.shape
    br = _divisor_tile(rows, 16, ELEM_ROWS)

    def body(chip_ref, own_ref, a_ref, b_ref, c_ref, o_ref):
        o_ref[...] = ((own_ref[...].astype(F32) + a_ref[...].astype(F32)) + b_ref[...].astype(F32)) \
            + c_ref[...].astype(F32)

    def other(k):
        return pl.BlockSpec((None, br, cols), lambda i, ch: (jnp.where(ch[0] <= k, k + 1, k), i, 0))

    spec = pltpu.PrefetchScalarGridSpec(
        num_scalar_prefetch=1, grid=(rows // br,),
        in_specs=[pl.BlockSpec((None, br, cols), lambda i, ch: (ch[0], i, 0)), other(0), other(1), other(2)],
        out_specs=pl.BlockSpec((br, cols), lambda i, ch: (i, 0)))
    return _call(body, (chip, part, landed, landed, landed), name="sum_partials", grid_spec=spec,
                 out_shape=jax.ShapeDtypeStruct((rows, cols), F32), semantics=("parallel",))


def _cast_to_slab(w, l, chip):
    _, rows, cols = w.shape
    br = _divisor_tile(rows, 16, ELEM_ROWS)

    def body(chip_ref, w_ref, o_ref):
        o_ref[...] = w_ref[...].astype(BF16)

    spec = pltpu.PrefetchScalarGridSpec(
        num_scalar_prefetch=1, grid=(rows // br,),
        in_specs=[pl.BlockSpec((None, br, cols), lambda i, ch: (l, i, 0))],
        out_specs=pl.BlockSpec((None, br, cols), lambda i, ch: (ch[0], i, 0)))
    return _call(body, (chip, w), name="cast_to_slab", grid_spec=spec,
                 out_shape=jax.ShapeDtypeStruct((N_CHIPS, rows, cols), BF16), semantics=("parallel",))


def _cast_w_in_t_to_slabs(w_t, chip):
    rows, depth, d = w_t.shape
    tn = _divisor_tile(d, LANES, 256)

    def body(chip_ref, w_ref, *o_refs):
        for l in range(depth):
            o_refs[l][...] = w_ref[:, l, :].astype(BF16)

    spec = pltpu.PrefetchScalarGridSpec(
        num_scalar_prefetch=1, grid=(d // tn,),
        in_specs=[pl.BlockSpec((rows, depth, tn), lambda j, ch: (0, 0, j))],
        out_specs=[pl.BlockSpec((None, rows, tn), lambda j, ch: (ch[0], 0, j))] * depth)
    return _call(body, (chip, w_t), name="cast_w_in_t_to_slabs", grid_spec=spec,
                 out_shape=[jax.ShapeDtypeStruct((N_CHIPS, rows, d), BF16)] * depth, semantics=("parallel",),
                 vmem_bytes=4 * _nbytes((rows, max(depth, SUBLANES), tn), F32))


def _place_slab(buf, index, n_slabs):
    rows, cols = buf.shape
    br = _divisor_tile(rows, SUBLANES, ELEM_ROWS)

    def body(index_ref, b_ref, o_ref):
        o_ref[...] = b_ref[...]

    spec = pltpu.PrefetchScalarGridSpec(
        num_scalar_prefetch=1, grid=(rows // br,),
        in_specs=[pl.BlockSpec((br, cols), lambda i, ix: (i, 0))],
        out_specs=pl.BlockSpec((None, br, cols), lambda i, ix: (ix[0], i, 0)))
    return _call(body, (index, buf), name="place_slab", grid_spec=spec,
                 out_shape=jax.ShapeDtypeStruct((n_slabs, rows, cols), buf.dtype), semantics=("parallel",))


ELEM_ROWS = 256


def _sum_slabs(r):
    n, rows, cols = r.shape
    br = _divisor_tile(rows, 16, ELEM_ROWS)

    def body(r_ref, o_ref):
        acc = r_ref[0].astype(F32)
        for j in range(1, n):
            acc = acc + r_ref[j].astype(F32)
        o_ref[...] = acc

    return _call(body, (r,), name="sum_slabs", grid=(rows // br,),
                 in_specs=[pl.BlockSpec((n, br, cols), lambda i: (0, i, 0))],
                 out_specs=pl.BlockSpec((br, cols), lambda i: (i, 0)),
                 out_shape=jax.ShapeDtypeStruct((rows, cols), F32), semantics=("parallel",))


def _adamw_math(w, g, m, v):
    c1 = 1.0 - ADAM_B1 ** ADAM_STEP
    c2 = 1.0 - ADAM_B2 ** ADAM_STEP
    nm = ADAM_B1 * m + (1.0 - ADAM_B1) * g
    nv = ADAM_B2 * v + (1.0 - ADAM_B2) * (g * g)
    delta = -ADAM_LR * ((nm / c1) / (jnp.sqrt(nv / c2) + ADAM_EPS) + ADAM_WD * w)
    return delta, nm, nv


def _adamw(w, g, m, v):
    rows, cols = w.shape
    br = _divisor_tile(rows, 8, ELEM_ROWS)

    def body(w_ref, g_ref, m_ref, v_ref, d_ref, nm_ref, nv_ref):
        d_ref[...], nm_ref[...], nv_ref[...] = _adamw_math(w_ref[...], g_ref[...], m_ref[...], v_ref[...])

    blk = pl.BlockSpec((br, cols), lambda i: (i, 0))
    return _call(body, (w, g, m, v), name="adamw", grid=(rows // br,),
                 in_specs=[blk] * 4, out_specs=[blk] * 3,
                 out_shape=[jax.ShapeDtypeStruct((rows, cols), F32)] * 3, semantics=("parallel",))


def _adamw_w_in_t(w_t, m_t, v_t, g_mine, g_theirs):
    rows, depth, d = w_t.shape
    tn = LANES

    def body(w_ref, m_ref, v_ref, *rest):
        ga_refs, gb_refs = rest[:depth], rest[depth:2 * depth]
        g_ref, d_ref, nm_ref, nv_ref = rest[2 * depth:]
        for l in range(depth):
            g_ref[:, l, :] = ga_refs[l][...] + gb_refs[l][...]
        d_ref[...], nm_ref[...], nv_ref[...] = _adamw_math(w_ref[...], g_ref[...], m_ref[...], v_ref[...])

    slab = pl.BlockSpec((rows, depth, tn), lambda j: (0, 0, j))
    gblk = pl.BlockSpec((rows, tn), lambda j: (0, j))
    return _call(body, (w_t, m_t, v_t) + tuple(g_mine) + tuple(g_theirs), name="adamw_w_in_t", grid=(d // tn,),
                 in_specs=[slab] * 3 + [gblk] * (2 * depth), out_specs=[slab] * 4,
                 out_shape=[jax.ShapeDtypeStruct(w_t.shape, F32)] * 4, semantics=("parallel",),
                 vmem_bytes=2 * (7 * _nbytes((rows, max(depth, SUBLANES), tn), F32)
                                 + 2 * depth * _nbytes((rows, tn), F32)))


def _adamw_layer(w, m, v, l, g_mine, g_theirs, prev, after):
    _, rows, cols = w.shape
    br = _divisor_tile(rows, 8, ELEM_ROWS)

    def body(w_ref, m_ref, v_ref, ga_ref, gb_ref, *rest):
        g_ref, d_ref, nm_ref, nv_ref = rest[5:]
        g = ga_ref[...] + gb_ref[...]
        g_ref[...] = g
        d_ref[...], nm_ref[...], nv_ref[...] = _adamw_math(w_ref[...], g, m_ref[...], v_ref[...])

    slot = pl.BlockSpec((None, br, cols), lambda i: (l, i, 0))
    blk = pl.BlockSpec((br, cols), lambda i: (i, 0))
    return _call(body, (w, m, v, g_mine, g_theirs) + tuple(prev) + (after,), name="adamw_layer",
                 grid=(rows // br,), in_specs=[slot] * 3 + [blk] * 2 + [ANY] * 5, out_specs=[slot] * 4,
                 out_shape=[jax.ShapeDtypeStruct(w.shape, F32)] * 4,
                 input_output_aliases={5: 0, 6: 1, 7: 2, 8: 3}, semantics=("parallel",))


BIG = ("w_in", "w_out", "w_up", "w_down")
WEIGHTS = ("meta", "attn_norm_g", "w_in", "b_f", "conv_w", "conv_b", "w_gate_a", "b_gate_a", "w_gate_x",
           "b_gate_x", "lru_L", "attn_out_g", "rec_out_g", "w_out", "mlp_norm_g", "w_up", "w_down", "final_g")
SMALL = tuple(k for k in WEIGHTS if k not in BIG)
COL_SHARDED_SMALL = ("meta", "conv_w")


def _packed_rows(shape):
    return -(-math.prod(shape) // (SUBLANES * LANES)) * SUBLANES


def _pack(arrs):
    rows = []
    for a in arrs:
        flat = a.reshape(-1)
        rows.append(jnp.pad(flat, (0, _packed_rows(a.shape) * LANES - flat.shape[0])).reshape(-1, LANES))
    used = sum(r.shape[0] for r in rows)
    rows.append(jnp.zeros((-used % ELEM_ROWS, LANES), F32))
    return jnp.concatenate(rows, axis=0)


def _unpack(buf, shapes):
    out, r0 = [], 0
    for s in shapes:
        nr = _packed_rows(s)
        out.append(buf[r0:r0 + nr].reshape(-1)[:math.prod(s)].reshape(s))
        r0 += nr
    return out


def _halves(a):
    return a.reshape((2, a.shape[0] // 2) + a.shape[1:])


def _cols_from_chips(g):
    return jnp.moveaxis(g, 0, -2).reshape(g.shape[1:-1] + (N_CHIPS * g.shape[-1],))


def kernel(x, meta, attn_norm_g, w_in, b_f, conv_w, conv_b, w_gate_a, b_gate_a, w_gate_x, b_gate_x, lru_L, attn_out_g, rec_out_g, w_out, mlp_norm_g, w_up, w_down, final_g, loss_target, m_meta, m_attn_norm_g, m_w_in, m_b_f, m_conv_w, m_conv_b, m_w_gate_a, m_b_gate_a, m_w_gate_x, m_b_gate_x, m_lru_L, m_attn_out_g, m_rec_out_g, m_w_out, m_mlp_norm_g, m_w_up, m_w_down, m_final_g, v_meta, v_attn_norm_g, v_w_in, v_b_f, v_conv_w, v_conv_b, v_w_gate_a, v_b_gate_a, v_w_gate_x, v_b_gate_x, v_lru_L, v_attn_out_g, v_rec_out_g, v_w_out, v_mlp_norm_g, v_w_up, v_w_down, v_final_g):
    w = dict(meta=meta, attn_norm_g=attn_norm_g, w_in=w_in, b_f=b_f, conv_w=conv_w, conv_b=conv_b,
             w_gate_a=w_gate_a, b_gate_a=b_gate_a, w_gate_x=w_gate_x, b_gate_x=b_gate_x, lru_L=lru_L,
             attn_out_g=attn_out_g, rec_out_g=rec_out_g, w_out=w_out, mlp_norm_g=mlp_norm_g, w_up=w_up,
             w_down=w_down, final_g=final_g)
    m = dict(meta=m_meta, attn_norm_g=m_attn_norm_g, w_in=m_w_in, b_f=m_b_f, conv_w=m_conv_w, conv_b=m_conv_b,
             w_gate_a=m_w_gate_a, b_gate_a=m_b_gate_a, w_gate_x=m_w_gate_x, b_gate_x=m_b_gate_x, lru_L=m_lru_L,
             attn_out_g=m_attn_out_g, rec_out_g=m_rec_out_g, w_out=m_w_out, mlp_norm_g=m_mlp_norm_g,
             w_up=m_w_up, w_down=m_w_down, final_g=m_final_g)
    v = dict(meta=v_meta, attn_norm_g=v_attn_norm_g, w_in=v_w_in, b_f=v_b_f, conv_w=v_conv_w, conv_b=v_conv_b,
             w_gate_a=v_w_gate_a, b_gate_a=v_b_gate_a, w_gate_x=v_w_gate_x, b_gate_x=v_b_gate_x, lru_L=v_lru_L,
             attn_out_g=v_attn_out_g, rec_out_g=v_rec_out_g, w_out=v_w_out, mlp_norm_g=v_mlp_norm_g,
             w_up=v_w_up, w_down=v_w_down, final_g=v_final_g)
    s_len, d = x.shape[1], x.shape[2]
    depth = w_in.shape[0]
    chip = 2 * lax.axis_index("x") + lax.axis_index("y")

    g_conv, g_meta = [g.reshape((N_CHIPS, g.shape[1] * g.shape[2]) + g.shape[3:])
                      for g in _all_gather_chips([_halves(w["conv_w"]), _halves(w["meta"])])]
    p = dict(w)
    p["conv_w"] = _cols_from_chips(g_conv)
    meta_full = jnp.moveaxis(g_meta, 0, 1).reshape(N_META, d)

    chip1 = chip.reshape(1).astype(jnp.int32)
    w_in_t, m_in_t, v_in_t = [jnp.transpose(a["w_in"], (2, 0, 1)) for a in (w, m, v)]
    w_in_slabs = _cast_w_in_t_to_slabs(w_in_t, chip1)
    pushes, tokens = [], []
    for l in range(depth):
        slabs = [w_in_slabs[l]] + [_cast_to_slab(w[k], l, chip1) for k in BIG[1:]]
        send_sems, recv_sems, _, lands, token = _push_start(slabs, slabs, "gather_chips_half", f"weights_start_{l}")
        pushes.append((send_sems, recv_sems, lands))
        tokens.append(token[0, 0])
    passed = [{} for _ in range(depth)]

    def stage(l, after, ids):
        send_sems, recv_sems, lands = pushes[l]
        tag = "_".join(BIG[i] for i in ids)
        sub = [lands[i] for i in ids]
        sub = _push_wait(send_sems, recv_sems, ids, sub, sub, "gather_chips_half", after, f"{tag}_wait_{l}")
        send_sems, recv_sems, _, sub, token = _push_start(sub, sub, "pass_halves", f"{tag}_pass_{l}")
        for j, i in enumerate(ids):
            passed[l][i] = (send_sems, recv_sems, sub[j], j)
        return token[0, 0]

    t_len = N_META + s_len
    pad = -t_len % SEQ_TILE
    h = jnp.concatenate([meta_full, x[0], jnp.zeros((pad, d), F32)], axis=0)
    tgt = jnp.concatenate([jnp.zeros((N_META, d), F32), loss_target[0], jnp.zeros((pad, d), F32)], axis=0)
    z = _rms_fwd(h, _row(p["attn_norm_g"][0] + sum(tokens)))
    stage(0, z, [0])
    saved = []
    for l in range(depth):
        def fetch(k, after, l=l):
            send_sems, recv_sems, land, j = passed[l][BIG.index(k)]
            return _push_wait(send_sems, recv_sems, [j], [land], [land], "pass_halves", after,
                              f"{k}_here_{l}")[0]

        def stage_next(after, which, l=l):
            if which == "own":
                return stage(l, after, [1, 2, 3])
            return stage(l + 1, after, [0]) if l + 1 < depth else 0.0

        g_next = p["attn_norm_g"][l + 1] if l + 1 < depth else p["final_g"]
        h, z, sv = _forward_layer(l, h, z, p, fetch, stage_next, g_next)
        saved.append(sv)
    dh, dh_b, dg_final, loss_part = _loss_bwd(h, _row(p["final_g"]), tgt, s_len)

    small = {k: [None] * depth for k in SMALL if k not in ("meta", "final_g")}
    pushes = [None] * depth
    tok = 0.0
    for l in reversed(range(depth)):
        dh, dh_b, big_mlp, sm_mlp = _backward_mlp(l, dh, dh_b, saved[l], p, tok)
        g_out, = _grad_w_pieces([saved[l]["mix"]], dh_b)
        parts = [big_mlp["w_down"], big_mlp["w_up"], g_out.reshape((N_CHIPS, -1) + g_out.shape[1:])]
        push_mlp = _push_start(parts, [lax.empty(a.shape, a.dtype) for a in parts], "scatter_chips",
                               f"mlp_grads_start_{l}")
        dh, dh_b, big_mix, sm_mix = _backward_mixer(l, dh, dh_b, saved[l], p, push_mlp[4][0, 0])
        parts = [big_mix["w_in"]]
        push_mix = _push_start(parts, [lax.empty(a.shape, a.dtype) for a in parts], "scatter_chips",
                               f"mixer_grads_start_{l}")
        tok = push_mix[4][0, 0]
        pushes[l] = {("w_down", "w_up", "w_out"): push_mlp, ("w_in",): push_mix}
        for k, val in {**sm_mlp, **sm_mix}.items():
            small[k][l] = val
    grads = {k: jnp.stack(val) for k, val in small.items()}
    grads["final_g"] = dg_final[0]
    grads["meta"] = dh[:N_META]
    dx = dh[N_META:t_len]

    full_shapes = [grads[k].shape for k in SMALL] + [(1,)]
    packed = _pack([grads[k].astype(F32) for k in SMALL] + [loss_part[0, :1] + tok])
    dev1 = (2 * chip + lax.axis_index("c")).reshape(1).astype(jnp.int32)
    slabs = [_place_slab(packed, dev1, N_DEV)]
    small_push = _push_start(slabs, slabs, "gather_devices", "small_grads_start")

    last_token = small_push[4]
    outs = {k: [lax.empty(w[k].shape, F32) for _ in range(4)] for k in BIG[1:]}
    w_in_sums = [None] * depth
    swaps = {}

    def finish(l, wait_after, adam_after):
        send_sems, recv_sems, mine, lands, _ = swaps[l]
        mine, theirs = _push_wait(send_sems, recv_sems, list(range(len(BIG))), mine, lands, "sibling", wait_after,
                                  f"sums_wait_{l}")
        w_in_sums[l] = (mine[0], theirs[0])
        for k, a, b in zip(BIG[1:], mine[1:], theirs[1:]):
            outs[k] = _adamw_layer(w[k], m[k], v[k], l, a, b, outs[k], adam_after)

    wait_after = last_token
    for l in reversed(range(depth)):
        sums = {}
        for names, (send_sems, recv_sems, parts, lands, _) in pushes[l].items():
            parts, landed = _push_wait(send_sems, recv_sems, list(range(len(names))), parts, lands, "scatter_chips",
                                       wait_after, f"{names[0]}_grads_wait_{l}")
            for k, part, land in zip(names, parts, landed):
                sums[k] = wait_after = _sum_partials(part, land, chip1)
        mine = [sums[k] for k in BIG]
        swaps[l] = _push_start(mine, [lax.empty(a.shape, a.dtype) for a in mine], "sibling", f"sums_start_{l}")
        if l + 2 < depth:
            finish(l + 2, outs["w_down"][0] if l + 3 < depth else mine[0], swaps[l][4])
    for l in reversed(range(min(2, depth))):
        finish(l, outs["w_down"][0] if depth > 2 else swaps[0][4], swaps[0][4])
    outs["w_in"] = [jnp.transpose(r, (1, 2, 0)) for r in _adamw_w_in_t(
        w_in_t, m_in_t, v_in_t, [s[0] for s in w_in_sums], [s[1] for s in w_in_sums])]
    out_g, out_d, out_m, out_v = [{k: outs[k][i] for k in BIG} for i in range(4)]

    landed = _push_wait(small_push[0], small_push[1], [0], small_push[3], small_push[3], "gather_devices",
                        out_g["w_in"], "small_grads_wait")
    total = _sum_slabs(landed[0])
    small_g = dict(zip(SMALL + ("loss",), _unpack(total, full_shapes)))
    for k in COL_SHARDED_SMALL:
        n = w[k].shape[-1]
        small_g[k] = lax.dynamic_slice_in_dim(small_g[k], chip * n, n, axis=small_g[k].ndim - 1)
    local_shapes = [w[k].shape for k in SMALL]
    res = _adamw(_pack([w[k] for k in SMALL]), _pack([small_g[k] for k in SMALL]),
                 _pack([m[k] for k in SMALL]), _pack([v[k] for k in SMALL]))
    out_g.update({k: small_g[k] for k in SMALL})
    for dst, buf in zip((out_d, out_m, out_v), res):
        dst.update(zip(SMALL, _unpack(buf, local_shapes)))

    return (small_g["loss"].reshape(()), dx[None],
            *[out_g[k] for k in WEIGHTS], *[out_d[k] for k in WEIGHTS],
            *[out_m[k] for k in WEIGHTS], *[out_v[k] for k in WEIGHTS])
```
